```python
import jax, jax.numpy as jnp
from jax import lax
import numpy as np

D_MODEL = 1024
BATCH = 8
SEQ = 8192
DEPTH = 4

CHUNK = 64
N_LEFT_CHUNKS = 8
BAND = (N_LEFT_CHUNKS + 1) * CHUNK
HEAD_DIM = 64
N_HEADS_A = 8
N_HEADS_B = 8
A_W = N_HEADS_A * HEAD_DIM
B_W = N_HEADS_B * HEAD_DIM
MIX_W = A_W + B_W
REL_CLIP = 256
Q_BLOCK = 128
LRU_WIDTH = D_MODEL
LRU_BLOCKS = 4
LRU_BLOCK_W = LRU_WIDTH // LRU_BLOCKS
CONV_WIDTH = 4
LRU_C = 8.0
D_FF = -(-(8 * D_MODEL) // (3 * 256)) * 256
RMS_EPS = 1e-6
N_ATTN_LAYERS = (DEPTH + 1) // 2
N_REC_LAYERS = DEPTH // 2

kernel_name = "hybrid_chunked_sb_rglru_trunk"


def rmsnorm(x, g):
    xf = x.astype(jnp.float32)
    y = xf * lax.rsqrt(jnp.mean(xf * xf, axis=-1, keepdims=True) + RMS_EPS)
    return (y * g.astype(jnp.float32)).astype(x.dtype)


def chunked_relpos_attention(q, k, v, rel_bias):
    b, s, h, dh = q.shape
    nc = s // CHUNK
    qc = q.reshape(b, nc, CHUNK, h, dh)

    def gather_band(t):
        tc = t.reshape(b, nc, CHUNK, h, dh)
        tp = jnp.pad(tc, ((0, 0), (N_LEFT_CHUNKS, 0), (0, 0), (0, 0), (0, 0)))
        return jnp.concatenate([tp[:, j:j + nc] for j in range(N_LEFT_CHUNKS + 1)], axis=2)

    kb, vb = gather_band(k), gather_band(v)
    scores = jnp.einsum('bcqhd,bckhd->bhcqk', qc, kb).astype(jnp.float32) * (dh ** -0.5)
    qpos = N_LEFT_CHUNKS * CHUNK + jnp.arange(CHUNK)
    kpos = jnp.arange(BAND)
    rel = jnp.clip(qpos[:, None] - kpos[None, :], -REL_CLIP, REL_CLIP) + REL_CLIP
    bias = rel_bias[:, rel].astype(jnp.float32)
    key_chunk = jnp.arange(nc)[:, None] - N_LEFT_CHUNKS + (kpos // CHUNK)[None, :]
    valid = key_chunk >= 0
    scores = scores + bias[None, :, None, :, :]
    scores = jnp.where(valid[None, None, :, None, :], scores, -jnp.inf)
    p = jax.nn.softmax(scores, axis=-1).astype(v.dtype)
    o = jnp.einsum('bhcqk,bckhd->bcqhd', p, vb)
    return o.reshape(b, s, h * dh)


def stick_breaking_attention(q, k, v):
    b, s, h, dh = q.shape
    nb = s // Q_BLOCK
    scale = dh ** -0.5
    kpos = jnp.arange(s)
    q_blocks = q.reshape(b, nb, Q_BLOCK, h, dh).transpose(1, 0, 2, 3, 4)

    def one_block(args):
        q_blk, blk = args
        z = jnp.einsum('bqhd,bkhd->bhqk', q_blk, k).astype(jnp.float32) * scale
        qpos = blk * Q_BLOCK + jnp.arange(Q_BLOCK)
        causal = kpos[None, :] < qpos[:, None]
        log_beta = jax.nn.log_sigmoid(z)
        log_1m_beta = jnp.where(causal, jax.nn.log_sigmoid(-z), 0.0)
        after = lax.cumsum(log_1m_beta, axis=3, reverse=True) - log_1m_beta
        w = jnp.where(causal, jnp.exp(log_beta + after), 0.0).astype(v.dtype)
        return jnp.einsum('bhqk,bkhd->bqhd', w, v)

    o = lax.map(one_block, (q_blocks, jnp.arange(nb)))
    return o.transpose(1, 0, 2, 3, 4).reshape(b, s, h * dh)


def attention_mixer(h, w_in, rel_bias, w_out):
    b, s, _ = h.shape
    proj = h @ w_in
    part_a, part_b = proj[..., :3 * A_W], proj[..., 3 * A_W:]
    qa, ka, va = [t.reshape(b, s, N_HEADS_A, HEAD_DIM) for t in jnp.split(part_a, 3, axis=-1)]
    qs, ks, vs = [t.reshape(b, s, N_HEADS_B, HEAD_DIM) for t in jnp.split(part_b, 3, axis=-1)]
    out_a = chunked_relpos_attention(qa, ka, va, rel_bias)
    out_b = stick_breaking_attention(qs, ks, vs)
    return jnp.concatenate([out_a, out_b], axis=-1) @ w_out


def recurrent_mixer(h, w_in, conv_w, conv_b, w_a, b_a, w_i, b_i, lam, w_out):
    b, s, _ = h.shape
    proj = h @ w_in
    gate, xr = jnp.split(proj, 2, axis=-1)
    gate = jax.nn.gelu(gate, approximate=True)
    xc = lax.conv_general_dilated(
        xr, conv_w, window_strides=(1,), padding=[(CONV_WIDTH - 1, 0)],
        dimension_numbers=('NWC', 'WIO', 'NWC'), feature_group_count=LRU_WIDTH) + conv_b
    xg = xc.reshape(b, s, LRU_BLOCKS, LRU_BLOCK_W)
    r = jax.nn.sigmoid(jnp.einsum('bsni,nij->bsnj', xg, w_a) + b_a).reshape(b, s, LRU_WIDTH)
    i = jax.nn.sigmoid(jnp.einsum('bsni,nij->bsnj', xg, w_i) + b_i).reshape(b, s, LRU_WIDTH)
    log_a = -LRU_C * r.astype(jnp.float32) * jax.nn.softplus(-lam.astype(jnp.float32))
    a = jnp.exp(log_a)
    mult = jnp.sqrt(-jnp.expm1(2.0 * log_a))
    u = mult * (i * xc).astype(jnp.float32)

    def combine(left, right):
        a1, b1 = left
        a2, b2 = right
        return a1 * a2, a2 * b1 + b2

    _, hs = lax.associative_scan(combine, (a, u), axis=1)
    return (hs.astype(h.dtype) * gate) @ w_out


def swiglu(h, w_gate, w_up, w_down):
    return (jax.nn.silu(h @ w_gate) * (h @ w_up)) @ w_down


def _fwd_setup_inputs(seed: int = 0) -> dict:
    key = jax.random.key(seed)
    ks = jax.random.split(key, 24)
    f32 = jnp.float32

    def nrm(k, shape, fan_in):
        return jax.random.normal(k, shape, f32) * (fan_in ** -0.5)

    def gain(k, shape):
        return 1.0 + 0.05 * jax.random.normal(k, shape, f32)

    u = jax.random.uniform(ks[12], (N_REC_LAYERS, LRU_WIDTH), f32, 0.9, 0.999)
    base = u ** (1.0 / LRU_C)
    lam = jnp.log(base) - jnp.log1p(-base)
    return {
        'x': jax.random.normal(ks[0], (BATCH, SEQ, D_MODEL), f32),
        'attn_w_in': nrm(ks[1], (N_ATTN_LAYERS, D_MODEL, 3 * MIX_W), D_MODEL),
        'attn_rel_bias': 0.2 * jax.random.normal(ks[2], (N_ATTN_LAYERS, N_HEADS_A, 2 * REL_CLIP + 1), f32),
        'attn_w_out': nrm(ks[3], (N_ATTN_LAYERS, MIX_W, D_MODEL), MIX_W),
        'rg_w_in': nrm(ks[4], (N_REC_LAYERS, D_MODEL, 2 * LRU_WIDTH), D_MODEL),
        'rg_conv_w': nrm(ks[5], (N_REC_LAYERS, CONV_WIDTH, 1, LRU_WIDTH), CONV_WIDTH),
        'rg_conv_b': 0.01 * jax.random.normal(ks[6], (N_REC_LAYERS, LRU_WIDTH), f32),
        'rg_w_a': nrm(ks[7], (N_REC_LAYERS, LRU_BLOCKS, LRU_BLOCK_W, LRU_BLOCK_W), LRU_BLOCK_W),
        'rg_b_a': 0.01 * jax.random.normal(ks[8], (N_REC_LAYERS, LRU_BLOCKS, LRU_BLOCK_W), f32),
        'rg_w_i': nrm(ks[9], (N_REC_LAYERS, LRU_BLOCKS, LRU_BLOCK_W, LRU_BLOCK_W), LRU_BLOCK_W),
        'rg_b_i': 0.01 * jax.random.normal(ks[10], (N_REC_LAYERS, LRU_BLOCKS, LRU_BLOCK_W), f32),
        'rg_lambda': lam,
        'rg_w_out': nrm(ks[11], (N_REC_LAYERS, LRU_WIDTH, D_MODEL), LRU_WIDTH),
        'norm_mix_pre': gain(ks[13], (DEPTH, D_MODEL)),
        'norm_mix_post': gain(ks[14], (DEPTH, D_MODEL)),
        'norm_ffn_pre': gain(ks[15], (DEPTH, D_MODEL)),
        'norm_ffn_post': gain(ks[16], (DEPTH, D_MODEL)),
        'ffn_w_gate': nrm(ks[17], (DEPTH, D_MODEL, D_FF), D_MODEL),
        'ffn_w_up': nrm(ks[18], (DEPTH, D_MODEL, D_FF), D_MODEL),
        'ffn_w_down': nrm(ks[19], (DEPTH, D_FF, D_MODEL), D_FF),
    }


def _fwd_reference(x, attn_w_in, attn_rel_bias, attn_w_out, rg_w_in, rg_conv_w, rg_conv_b,
              rg_w_a, rg_b_a, rg_w_i, rg_b_i, rg_lambda, rg_w_out,
              norm_mix_pre, norm_mix_post, norm_ffn_pre, norm_ffn_post,
              ffn_w_gate, ffn_w_up, ffn_w_down):
    for layer in range(DEPTH):
        j = layer // 2
        h = rmsnorm(x, norm_mix_pre[layer])
        if layer % 2 == 0:
            m = attention_mixer(h, attn_w_in[j], attn_rel_bias[j], attn_w_out[j])
        else:
            m = recurrent_mixer(h, rg_w_in[j], rg_conv_w[j], rg_conv_b[j], rg_w_a[j], rg_b_a[j],
                                rg_w_i[j], rg_b_i[j], rg_lambda[j], rg_w_out[j])
        x = x + rmsnorm(m, norm_mix_post[layer])
        h = rmsnorm(x, norm_ffn_pre[layer])
        f = swiglu(h, ffn_w_gate[layer], ffn_w_up[layer], ffn_w_down[layer])
        x = x + rmsnorm(f, norm_ffn_post[layer])
    return x


import jax as _jax
import jax.numpy as _jnp

TWIN_FORMAT = 'train_step'
FWD_PARAMS = ['x', 'attn_w_in', 'attn_rel_bias', 'attn_w_out', 'rg_w_in', 'rg_conv_w', 'rg_conv_b', 'rg_w_a', 'rg_b_a', 'rg_w_i', 'rg_b_i', 'rg_lambda', 'rg_w_out', 'norm_mix_pre', 'norm_mix_post', 'norm_ffn_pre', 'norm_ffn_post', 'ffn_w_gate', 'ffn_w_up', 'ffn_w_down']
TWIN_WEIGHTS = ['attn_w_in', 'attn_rel_bias', 'attn_w_out', 'rg_w_in', 'rg_conv_w', 'rg_conv_b', 'rg_w_a', 'rg_b_a', 'rg_w_i', 'rg_b_i', 'rg_lambda', 'rg_w_out', 'norm_mix_pre', 'norm_mix_post', 'norm_ffn_pre', 'norm_ffn_post', 'ffn_w_gate', 'ffn_w_up', 'ffn_w_down']
TWIN_DIFF_INPUT = 'x'
TWIN_INPUTS = ['x', 'attn_w_in', 'attn_rel_bias', 'attn_w_out', 'rg_w_in', 'rg_conv_w', 'rg_conv_b', 'rg_w_a', 'rg_b_a', 'rg_w_i', 'rg_b_i', 'rg_lambda', 'rg_w_out', 'norm_mix_pre', 'norm_mix_post', 'norm_ffn_pre', 'norm_ffn_post', 'ffn_w_gate', 'ffn_w_up', 'ffn_w_down', 'loss_target', 'm_attn_w_in', 'm_attn_rel_bias', 'm_attn_w_out', 'm_rg_w_in', 'm_rg_conv_w', 'm_rg_conv_b', 'm_rg_w_a', 'm_rg_b_a', 'm_rg_w_i', 'm_rg_b_i', 'm_rg_lambda', 'm_rg_w_out', 'm_norm_mix_pre', 'm_norm_mix_post', 'm_norm_ffn_pre', 'm_norm_ffn_post', 'm_ffn_w_gate', 'm_ffn_w_up', 'm_ffn_w_down', 'v_attn_w_in', 'v_attn_rel_bias', 'v_attn_w_out', 'v_rg_w_in', 'v_rg_conv_w', 'v_rg_conv_b', 'v_rg_w_a', 'v_rg_b_a', 'v_rg_w_i', 'v_rg_b_i', 'v_rg_lambda', 'v_rg_w_out', 'v_norm_mix_pre', 'v_norm_mix_post', 'v_norm_ffn_pre', 'v_norm_ffn_post', 'v_ffn_w_gate', 'v_ffn_w_up', 'v_ffn_w_down']
TWIN_OUTPUTS = ['loss', 'grad_x', 'grad_attn_w_in', 'grad_attn_rel_bias', 'grad_attn_w_out', 'grad_rg_w_in', 'grad_rg_conv_w', 'grad_rg_conv_b', 'grad_rg_w_a', 'grad_rg_b_a', 'grad_rg_w_i', 'grad_rg_b_i', 'grad_rg_lambda', 'grad_rg_w_out', 'grad_norm_mix_pre', 'grad_norm_mix_post', 'grad_norm_ffn_pre', 'grad_norm_ffn_post', 'grad_ffn_w_gate', 'grad_ffn_w_up', 'grad_ffn_w_down', 'delta_attn_w_in', 'delta_attn_rel_bias', 'delta_attn_w_out', 'delta_rg_w_in', 'delta_rg_conv_w', 'delta_rg_conv_b', 'delta_rg_w_a', 'delta_rg_b_a', 'delta_rg_w_i', 'delta_rg_b_i', 'delta_rg_lambda', 'delta_rg_w_out', 'delta_norm_mix_pre', 'delta_norm_mix_post', 'delta_norm_ffn_pre', 'delta_norm_ffn_post', 'delta_ffn_w_gate', 'delta_ffn_w_up', 'delta_ffn_w_down', 'new_m_attn_w_in', 'new_m_attn_rel_bias', 'new_m_attn_w_out', 'new_m_rg_w_in', 'new_m_rg_conv_w', 'new_m_rg_conv_b', 'new_m_rg_w_a', 'new_m_rg_b_a', 'new_m_rg_w_i', 'new_m_rg_b_i', 'new_m_rg_lambda', 'new_m_rg_w_out', 'new_m_norm_mix_pre', 'new_m_norm_mix_post', 'new_m_norm_ffn_pre', 'new_m_norm_ffn_post', 'new_m_ffn_w_gate', 'new_m_ffn_w_up', 'new_m_ffn_w_down', 'new_v_attn_w_in', 'new_v_attn_rel_bias', 'new_v_attn_w_out', 'new_v_rg_w_in', 'new_v_rg_conv_w', 'new_v_rg_conv_b', 'new_v_rg_w_a', 'new_v_rg_b_a', 'new_v_rg_w_i', 'new_v_rg_b_i', 'new_v_rg_lambda', 'new_v_rg_w_out', 'new_v_norm_mix_pre', 'new_v_norm_mix_post', 'new_v_norm_ffn_pre', 'new_v_norm_ffn_post', 'new_v_ffn_w_gate', 'new_v_ffn_w_up', 'new_v_ffn_w_down']
TWIN_LEAF_KINDS = {'loss': 'loss', 'grad_x': 'grad_x', 'grad_attn_w_in': 'grad_w', 'grad_attn_rel_bias': 'grad_w', 'grad_attn_w_out': 'grad_w', 'grad_rg_w_in': 'grad_w', 'grad_rg_conv_w': 'grad_w', 'grad_rg_conv_b': 'grad_w', 'grad_rg_w_a': 'grad_w', 'grad_rg_b_a': 'grad_w', 'grad_rg_w_i': 'grad_w', 'grad_rg_b_i': 'grad_w', 'grad_rg_lambda': 'grad_w', 'grad_rg_w_out': 'grad_w', 'grad_norm_mix_pre': 'grad_w', 'grad_norm_mix_post': 'grad_w', 'grad_norm_ffn_pre': 'grad_w', 'grad_norm_ffn_post': 'grad_w', 'grad_ffn_w_gate': 'grad_w', 'grad_ffn_w_up': 'grad_w', 'grad_ffn_w_down': 'grad_w', 'delta_attn_w_in': 'delta_w', 'delta_attn_rel_bias': 'delta_w', 'delta_attn_w_out': 'delta_w', 'delta_rg_w_in': 'delta_w', 'delta_rg_conv_w': 'delta_w', 'delta_rg_conv_b': 'delta_w', 'delta_rg_w_a': 'delta_w', 'delta_rg_b_a': 'delta_w', 'delta_rg_w_i': 'delta_w', 'delta_rg_b_i': 'delta_w', 'delta_rg_lambda': 'delta_w', 'delta_rg_w_out': 'delta_w', 'delta_norm_mix_pre': 'delta_w', 'delta_norm_mix_post': 'delta_w', 'delta_norm_ffn_pre': 'delta_w', 'delta_norm_ffn_post': 'delta_w', 'delta_ffn_w_gate': 'delta_w', 'delta_ffn_w_up': 'delta_w', 'delta_ffn_w_down': 'delta_w', 'new_m_attn_w_in': 'new_m', 'new_m_attn_rel_bias': 'new_m', 'new_m_attn_w_out': 'new_m', 'new_m_rg_w_in': 'new_m', 'new_m_rg_conv_w': 'new_m', 'new_m_rg_conv_b': 'new_m', 'new_m_rg_w_a': 'new_m', 'new_m_rg_b_a': 'new_m', 'new_m_rg_w_i': 'new_m', 'new_m_rg_b_i': 'new_m', 'new_m_rg_lambda': 'new_m', 'new_m_rg_w_out': 'new_m', 'new_m_norm_mix_pre': 'new_m', 'new_m_norm_mix_post': 'new_m', 'new_m_norm_ffn_pre': 'new_m', 'new_m_norm_ffn_post': 'new_m', 'new_m_ffn_w_gate': 'new_m', 'new_m_ffn_w_up': 'new_m', 'new_m_ffn_w_down': 'new_m', 'new_v_attn_w_in': 'new_v', 'new_v_attn_rel_bias': 'new_v', 'new_v_attn_w_out': 'new_v', 'new_v_rg_w_in': 'new_v', 'new_v_rg_conv_w': 'new_v', 'new_v_rg_conv_b': 'new_v', 'new_v_rg_w_a': 'new_v', 'new_v_rg_b_a': 'new_v', 'new_v_rg_w_i': 'new_v', 'new_v_rg_b_i': 'new_v', 'new_v_rg_lambda': 'new_v', 'new_v_rg_w_out': 'new_v', 'new_v_norm_mix_pre': 'new_v', 'new_v_norm_mix_post': 'new_v', 'new_v_norm_ffn_pre': 'new_v', 'new_v_norm_ffn_post': 'new_v', 'new_v_ffn_w_gate': 'new_v', 'new_v_ffn_w_up': 'new_v', 'new_v_ffn_w_down': 'new_v'}


def _forward(args):
    return _fwd_reference(*[args[k] for k in FWD_PARAMS])


def _output_shape():
    def fwd():
        inp = _fwd_setup_inputs(0)
        return _fwd_reference(*[inp[k] for k in FWD_PARAMS])
    out = _jax.eval_shape(fwd)
    return out.shape, out.dtype

N_MICROBATCH = 1
ADAM_LR = 0.001
ADAM_B1 = 0.9
ADAM_B2 = 0.999
ADAM_EPS = 1e-08
ADAM_WD = 0.01
ADAM_STEP = 10
PER_EXAMPLE_BATCH_AXIS = {'x': 0, 'loss_target': 0}
SHARED_INPUTS = []
_WEIGHT_DTYPES = {'attn_w_in': _jnp.float32, 'attn_rel_bias': _jnp.float32, 'attn_w_out': _jnp.float32, 'rg_w_in': _jnp.float32, 'rg_conv_w': _jnp.float32, 'rg_conv_b': _jnp.float32, 'rg_w_a': _jnp.float32, 'rg_b_a': _jnp.float32, 'rg_w_i': _jnp.float32, 'rg_b_i': _jnp.float32, 'rg_lambda': _jnp.float32, 'rg_w_out': _jnp.float32, 'norm_mix_pre': _jnp.float32, 'norm_mix_post': _jnp.float32, 'norm_ffn_pre': _jnp.float32, 'norm_ffn_post': _jnp.float32, 'ffn_w_gate': _jnp.float32, 'ffn_w_up': _jnp.float32, 'ffn_w_down': _jnp.float32}
MOMENT_SCALE = {'attn_w_in': 1.929075e+00, 'attn_rel_bias': 2.418117e-01, 'attn_w_out': 3.343906e+00, 'rg_w_in': 2.166916e+00, 'rg_conv_w': 3.270768e+00, 'rg_conv_b': 6.668431e+01, 'rg_w_a': 1.593057e+00, 'rg_b_a': 8.156683e-01, 'rg_w_i': 3.016647e+00, 'rg_b_i': 1.207886e+00, 'rg_lambda': 1.502391e+00, 'rg_w_out': 3.610379e+00, 'norm_mix_pre': 3.219365e+00, 'norm_mix_post': 6.304021e+01, 'norm_ffn_pre': 2.690103e+00, 'norm_ffn_post': 6.384104e+01, 'ffn_w_gate': 9.639275e-01, 'ffn_w_up': 1.203199e+00, 'ffn_w_down': 2.015963e+00}


def _to_microbatches(a, axis):
    t = _jnp.moveaxis(a, axis, 0)
    t = t.reshape((N_MICROBATCH, t.shape[0] // N_MICROBATCH) + t.shape[1:])
    return _jnp.moveaxis(t, 1, axis + 1)


def setup_inputs(seed: int = 0) -> dict:
    inp = _fwd_setup_inputs(seed)
    key = _jax.random.fold_in(_jax.random.key(seed), 7919)
    shape, _ = _output_shape()
    out = dict(inp)
    out["loss_target"] = _jax.random.normal(_jax.random.fold_in(key, 0), shape, _jnp.float32)
    for i, name in enumerate(TWIN_WEIGHTS):
        w = inp[name].astype(_jnp.float32)
        if MOMENT_SCALE is None:
            s = _jnp.sqrt(_jnp.mean(_jnp.square(w)) + 1e-30)
        else:
            s = MOMENT_SCALE[name]
        km, kv = _jax.random.split(_jax.random.fold_in(key, i + 1))
        out[name] = w
        out["m_" + name] = s * _jax.random.normal(km, w.shape, _jnp.float32)
        out["v_" + name] = (s * s) * _jax.random.uniform(kv, w.shape, _jnp.float32, 0.5, 1.5)
    if N_MICROBATCH > 1:
        for name, axis in PER_EXAMPLE_BATCH_AXIS.items():
            out[name] = _to_microbatches(out[name], axis)
    return {'x': out['x'], 'attn_w_in': out['attn_w_in'], 'attn_rel_bias': out['attn_rel_bias'], 'attn_w_out': out['attn_w_out'], 'rg_w_in': out['rg_w_in'], 'rg_conv_w': out['rg_conv_w'], 'rg_conv_b': out['rg_conv_b'], 'rg_w_a': out['rg_w_a'], 'rg_b_a': out['rg_b_a'], 'rg_w_i': out['rg_w_i'], 'rg_b_i': out['rg_b_i'], 'rg_lambda': out['rg_lambda'], 'rg_w_out': out['rg_w_out'], 'norm_mix_pre': out['norm_mix_pre'], 'norm_mix_post': out['norm_mix_post'], 'norm_ffn_pre': out['norm_ffn_pre'], 'norm_ffn_post': out['norm_ffn_post'], 'ffn_w_gate': out['ffn_w_gate'], 'ffn_w_up': out['ffn_w_up'], 'ffn_w_down': out['ffn_w_down'], 'loss_target': out['loss_target'], 'm_attn_w_in': out['m_attn_w_in'], 'm_attn_rel_bias': out['m_attn_rel_bias'], 'm_attn_w_out': out['m_attn_w_out'], 'm_rg_w_in': out['m_rg_w_in'], 'm_rg_conv_w': out['m_rg_conv_w'], 'm_rg_conv_b': out['m_rg_conv_b'], 'm_rg_w_a': out['m_rg_w_a'], 'm_rg_b_a': out['m_rg_b_a'], 'm_rg_w_i': out['m_rg_w_i'], 'm_rg_b_i': out['m_rg_b_i'], 'm_rg_lambda': out['m_rg_lambda'], 'm_rg_w_out': out['m_rg_w_out'], 'm_norm_mix_pre': out['m_norm_mix_pre'], 'm_norm_mix_post': out['m_norm_mix_post'], 'm_norm_ffn_pre': out['m_norm_ffn_pre'], 'm_norm_ffn_post': out['m_norm_ffn_post'], 'm_ffn_w_gate': out['m_ffn_w_gate'], 'm_ffn_w_up': out['m_ffn_w_up'], 'm_ffn_w_down': out['m_ffn_w_down'], 'v_attn_w_in': out['v_attn_w_in'], 'v_attn_rel_bias': out['v_attn_rel_bias'], 'v_attn_w_out': out['v_attn_w_out'], 'v_rg_w_in': out['v_rg_w_in'], 'v_rg_conv_w': out['v_rg_conv_w'], 'v_rg_conv_b': out['v_rg_conv_b'], 'v_rg_w_a': out['v_rg_w_a'], 'v_rg_b_a': out['v_rg_b_a'], 'v_rg_w_i': out['v_rg_w_i'], 'v_rg_b_i': out['v_rg_b_i'], 'v_rg_lambda': out['v_rg_lambda'], 'v_rg_w_out': out['v_rg_w_out'], 'v_norm_mix_pre': out['v_norm_mix_pre'], 'v_norm_mix_post': out['v_norm_mix_post'], 'v_norm_ffn_pre': out['v_norm_ffn_pre'], 'v_norm_ffn_post': out['v_norm_ffn_post'], 'v_ffn_w_gate': out['v_ffn_w_gate'], 'v_ffn_w_up': out['v_ffn_w_up'], 'v_ffn_w_down': out['v_ffn_w_down']}


def _loss(weights, diff, rest, loss_target):
    with _jax.named_scope("forward"):
        args = {**rest, TWIN_DIFF_INPUT: diff, **{k: w.astype(_WEIGHT_DTYPES[k]) for k, w in weights.items()}}
        y = _forward(args)
    with _jax.named_scope("loss_head"):
        err = _jnp.square(y.astype(_jnp.float32) - loss_target)
        return 0.5 * _jnp.sum(_jnp.mean(err, axis=-1)) if err.ndim else 0.5 * err


def _adamw(w, g, m, v):
    m = ADAM_B1 * m + (1.0 - ADAM_B1) * g
    v = ADAM_B2 * v + (1.0 - ADAM_B2) * _jnp.square(g)
    m_hat = m / (1.0 - ADAM_B1 ** ADAM_STEP)
    v_hat = v / (1.0 - ADAM_B2 ** ADAM_STEP)
    delta = -ADAM_LR * (m_hat / (_jnp.sqrt(v_hat) + ADAM_EPS) + ADAM_WD * w)
    return delta, m, v


def reference(x, attn_w_in, attn_rel_bias, attn_w_out, rg_w_in, rg_conv_w, rg_conv_b, rg_w_a, rg_b_a, rg_w_i, rg_b_i, rg_lambda, rg_w_out, norm_mix_pre, norm_mix_post, norm_ffn_pre, norm_ffn_post, ffn_w_gate, ffn_w_up, ffn_w_down, loss_target, m_attn_w_in, m_attn_rel_bias, m_attn_w_out, m_rg_w_in, m_rg_conv_w, m_rg_conv_b, m_rg_w_a, m_rg_b_a, m_rg_w_i, m_rg_b_i, m_rg_lambda, m_rg_w_out, m_norm_mix_pre, m_norm_mix_post, m_norm_ffn_pre, m_norm_ffn_post, m_ffn_w_gate, m_ffn_w_up, m_ffn_w_down, v_attn_w_in, v_attn_rel_bias, v_attn_w_out, v_rg_w_in, v_rg_conv_w, v_rg_conv_b, v_rg_w_a, v_rg_b_a, v_rg_w_i, v_rg_b_i, v_rg_lambda, v_rg_w_out, v_norm_mix_pre, v_norm_mix_post, v_norm_ffn_pre, v_norm_ffn_post, v_ffn_w_gate, v_ffn_w_up, v_ffn_w_down):
    given = dict(x=x, attn_w_in=attn_w_in, attn_rel_bias=attn_rel_bias, attn_w_out=attn_w_out, rg_w_in=rg_w_in, rg_conv_w=rg_conv_w, rg_conv_b=rg_conv_b, rg_w_a=rg_w_a, rg_b_a=rg_b_a, rg_w_i=rg_w_i, rg_b_i=rg_b_i, rg_lambda=rg_lambda, rg_w_out=rg_w_out, norm_mix_pre=norm_mix_pre, norm_mix_post=norm_mix_post, norm_ffn_pre=norm_ffn_pre, norm_ffn_post=norm_ffn_post, ffn_w_gate=ffn_w_gate, ffn_w_up=ffn_w_up, ffn_w_down=ffn_w_down, loss_target=loss_target, m_attn_w_in=m_attn_w_in, m_attn_rel_bias=m_attn_rel_bias, m_attn_w_out=m_attn_w_out, m_rg_w_in=m_rg_w_in, m_rg_conv_w=m_rg_conv_w, m_rg_conv_b=m_rg_conv_b, m_rg_w_a=m_rg_w_a, m_rg_b_a=m_rg_b_a, m_rg_w_i=m_rg_w_i, m_rg_b_i=m_rg_b_i, m_rg_lambda=m_rg_lambda, m_rg_w_out=m_rg_w_out, m_norm_mix_pre=m_norm_mix_pre, m_norm_mix_post=m_norm_mix_post, m_norm_ffn_pre=m_norm_ffn_pre, m_norm_ffn_post=m_norm_ffn_post, m_ffn_w_gate=m_ffn_w_gate, m_ffn_w_up=m_ffn_w_up, m_ffn_w_down=m_ffn_w_down, v_attn_w_in=v_attn_w_in, v_attn_rel_bias=v_attn_rel_bias, v_attn_w_out=v_attn_w_out, v_rg_w_in=v_rg_w_in, v_rg_conv_w=v_rg_conv_w, v_rg_conv_b=v_rg_conv_b, v_rg_w_a=v_rg_w_a, v_rg_b_a=v_rg_b_a, v_rg_w_i=v_rg_w_i, v_rg_b_i=v_rg_b_i, v_rg_lambda=v_rg_lambda, v_rg_w_out=v_rg_w_out, v_norm_mix_pre=v_norm_mix_pre, v_norm_mix_post=v_norm_mix_post, v_norm_ffn_pre=v_norm_ffn_pre, v_norm_ffn_post=v_norm_ffn_post, v_ffn_w_gate=v_ffn_w_gate, v_ffn_w_up=v_ffn_w_up, v_ffn_w_down=v_ffn_w_down)
    weights = {n: given[n] for n in TWIN_WEIGHTS}
    shared = {n: given[n] for n in SHARED_INPUTS}
    per_example = {n: given[n] for n in ['x']}
    grad_fn = _jax.value_and_grad(_loss, argnums=(0, 1))

    def one_microbatch(ex, loss_target):
        ex = dict(ex)
        diff = ex.pop(TWIN_DIFF_INPUT)
        return grad_fn(weights, diff, {**shared, **ex}, loss_target)

    if N_MICROBATCH == 1:
        loss, (grad_w, grad_x) = one_microbatch(per_example, given["loss_target"])
    else:
        def body(carry, xs):
            loss_sum, grad_sum = carry
            l_k, (gw_k, gx_k) = one_microbatch(xs[0], xs[1])
            with _jax.named_scope("update"):
                return (loss_sum + l_k, _jax.tree.map(_jnp.add, grad_sum, gw_k)), gx_k

        init = (_jnp.zeros((), _jnp.float32), _jax.tree.map(_jnp.zeros_like, weights))
        (loss, grad_w), grad_x = _jax.lax.scan(body, init, (per_example, given["loss_target"]))
    with _jax.named_scope("update"):
        delta_w, new_m, new_v = {}, {}, {}
        for n in TWIN_WEIGHTS:
            delta_w[n], new_m[n], new_v[n] = _adamw(weights[n], grad_w[n], given["m_" + n], given["v_" + n])
    return (loss, grad_x, *[grad_w[n] for n in TWIN_WEIGHTS], *[delta_w[n] for n in TWIN_WEIGHTS],
            *[new_m[n] for n in TWIN_WEIGHTS], *[new_v[n] for n in TWIN_WEIGHTS])
```

```python
import functools

import numpy as np
import jax
import jax.numpy as jnp
from jax import lax
from jax.experimental import pallas as pl
from jax.experimental.pallas import tpu as pltpu

f32 = jnp.float32
bf16 = jnp.bfloat16
SDS = jax.ShapeDtypeStruct
MESH = pl.DeviceIdType.MESH

D_MODEL = 1024
N_CHIPS = 4
DEPTH = 4
HEAD_DIM = 64
CHUNK = 64
N_LEFT = 8
REL_CLIP = 256
A_W = 512
LRU_BLOCKS = 4
LRU_BW = 256
LRU_C = 8.0
D_FF = 2816
RMS_EPS = 1e-6
LANES = 128
SUBLANES = 8
VMEM_LIMIT = 56 * 1024 * 1024

QB_A = 2 * CHUNK
KW_A = QB_A + N_LEFT * CHUNK
PAD_A = N_LEFT * CHUNK
EXT_A = 768
SB_BLK = 256

ADAM_LR, ADAM_B1, ADAM_B2, ADAM_EPS, ADAM_WD, ADAM_STEP = 0.001, 0.9, 0.999, 1e-08, 0.01, 10


def _cparams(sem):
    return pltpu.CompilerParams(dimension_semantics=sem, vmem_limit_bytes=VMEM_LIMIT)


def _gemm(name, operands, in_specs, o_spec, out_shape, grid, dims, acc_shape):
    nred = grid[2]
    npair = len(operands) // 2

    def body(*refs):
        o_ref = refs[2 * npair]
        p = None
        for t in range(npair):
            d = lax.dot_general(refs[2 * t][...], refs[2 * t + 1][...], (dims, ((), ())),
                                preferred_element_type=f32)
            p = d if p is None else p + d
        if nred == 1:
            o_ref[...] = p.astype(o_ref.dtype)
        else:
            acc = refs[2 * npair + 1]
            r = pl.program_id(2)

            @pl.when(r == 0)
            def _():
                acc[...] = p

            @pl.when(r > 0)
            def _():
                acc[...] += p

            @pl.when(r == nred - 1)
            def _():
                o_ref[...] = acc[...].astype(o_ref.dtype)

    scratch = [] if nred == 1 else [pltpu.VMEM(acc_shape, f32)]
    return pl.pallas_call(
        body, grid=grid, in_specs=in_specs, out_specs=o_spec, out_shape=out_shape,
        scratch_shapes=scratch, name=name,
        compiler_params=_cparams(("parallel", "parallel", "arbitrary")))(*operands)


NN = ((1,), (0,))
NT = ((1,), (1,))
TN = ((0,), (0,))


def _tile(t, want=512):
    return min(want, t)


def _mm_cols(name, a, w, l, out_dtype):
    t, k = a.shape
    _, s, _, ns = w.shape
    tm = _tile(t)
    return _gemm(
        name, [a, w],
        [pl.BlockSpec((tm, k), lambda i, j, r: (i, 0)),
         pl.BlockSpec((None, None, k, ns), lambda i, j, r: (l, j, 0, 0))],
        pl.BlockSpec((tm, ns), lambda i, j, r: (i, j)),
        SDS((t, s * ns), out_dtype), (t // tm, s, 1), NN, None)


def _mm_cols_t(name, dy, w, l, out_dtype):
    t = dy.shape[0]
    _, s, k, ns = w.shape
    tm = _tile(t)
    return _gemm(
        name, [dy, w],
        [pl.BlockSpec((tm, ns), lambda i, j, r: (i, r)),
         pl.BlockSpec((None, None, k, ns), lambda i, j, r: (l, r, 0, 0))],
        pl.BlockSpec((tm, k), lambda i, j, r: (i, 0)),
        SDS((t, k), out_dtype), (t // tm, 1, s), NT, (tm, k))


def _mm_wgrad_cols(name, a, dy, s):
    t, k = a.shape
    ns = dy.shape[1] // s
    tt = _tile(t)
    return _gemm(
        name, [a, dy],
        [pl.BlockSpec((tt, k), lambda i, j, r: (r, 0)),
         pl.BlockSpec((tt, ns), lambda i, j, r: (r, i))],
        pl.BlockSpec((None, k, ns), lambda i, j, r: (i, 0, 0)),
        SDS((s, k, ns), f32), (s, 1, t // tt), TN, (k, ns))


def _mm_rows(name, parts, w, l, out_dtype):
    t = parts[0].shape[0]
    n = w.shape[3]
    tm = _tile(t)
    ops, specs = [], []
    for p_i, a in enumerate(parts):
        kp = a.shape[1]
        ops += [a, w]
        specs += [pl.BlockSpec((tm, kp), lambda i, j, r: (i, 0)),
                  pl.BlockSpec((None, None, kp, n), lambda i, j, r, p_i=p_i: (l, 0, p_i, 0))]
    return _gemm(name, ops, specs, pl.BlockSpec((tm, n), lambda i, j, r: (i, 0)),
                 SDS((t, n), out_dtype), (t // tm, 1, 1), NN, None)


def _mm_rows_t(name, dy, w, l, out_dtype):
    t, n = dy.shape
    k = w.shape[2]
    tm = _tile(t)
    return _gemm(
        name, [dy, w],
        [pl.BlockSpec((tm, n), lambda i, j, r: (i, 0)),
         pl.BlockSpec((None, None, k, n), lambda i, j, r: (l, 0, 0, 0))],
        pl.BlockSpec((tm, k), lambda i, j, r: (i, 0)),
        SDS((t, k), out_dtype), (t // tm, 1, 1), NT, None)


def _mm_wgrad(name, a, dy):
    t, k = a.shape
    n = dy.shape[1]
    tt = _tile(t)
    return _gemm(
        name, [a, dy],
        [pl.BlockSpec((tt, k), lambda i, j, r: (r, 0)),
         pl.BlockSpec((tt, n), lambda i, j, r: (r, 0))],
        pl.BlockSpec((k, n), lambda i, j, r: (0, 0)),
        SDS((k, n), f32), (1, 1, t // tt), TN, (k, n))


def _ffn_up(h, wg, wu, l):
    t, k = h.shape
    s, fs = wg.shape[1], wg.shape[3]
    tm = _tile(t)

    def body(h_ref, wg_ref, wu_ref, g_ref, u_ref, hid_ref):
        hv = h_ref[...]
        g = jnp.dot(hv, wg_ref[...], preferred_element_type=f32)
        u = jnp.dot(hv, wu_ref[...], preferred_element_type=f32)
        g_ref[...] = g.astype(bf16)
        u_ref[...] = u.astype(bf16)
        hid_ref[...] = (g * jax.nn.sigmoid(g) * u).astype(bf16)

    wspec = pl.BlockSpec((None, None, k, fs), lambda j, i: (l, j, 0, 0))
    ospec = pl.BlockSpec((None, tm, fs), lambda j, i: (j, i, 0))
    return pl.pallas_call(
        body, grid=(s, t // tm), name="ffn_up",
        in_specs=[pl.BlockSpec((tm, k), lambda j, i: (i, 0)), wspec, wspec],
        out_specs=[ospec, ospec, ospec], out_shape=[SDS((s, t, fs), bf16)] * 3,
        compiler_params=_cparams(("parallel", "parallel")))(h, wg, wu)


def _ffn_down(hid, wd, l):
    s, t, fs = hid.shape
    n = wd.shape[3]
    tm = _tile(t)
    return _gemm(
        "ffn_down", [hid, wd],
        [pl.BlockSpec((None, tm, fs), lambda i, j, r: (r, i, 0)),
         pl.BlockSpec((None, None, fs, n), lambda i, j, r: (l, r, 0, 0))],
        pl.BlockSpec((tm, n), lambda i, j, r: (i, 0)),
        SDS((t, n), f32), (t // tm, 1, s), NN, (tm, n))


def _ffn_down_bwd(df, wd, l, g, u):
    t, n = df.shape
    s, fs = wd.shape[1], wd.shape[2]
    tm = _tile(t)

    def body(df_ref, wd_ref, g_ref, u_ref, dg_ref, du_ref):
        dh = lax.dot_general(df_ref[...], wd_ref[...], (NT, ((), ())), preferred_element_type=f32)
        gv = g_ref[...].astype(f32)
        uv = u_ref[...].astype(f32)
        sg = jax.nn.sigmoid(gv)
        du_ref[...] = (dh * gv * sg).astype(bf16)
        dg_ref[...] = (dh * uv * (sg * (1.0 + gv * (1.0 - sg)))).astype(bf16)

    bspec = pl.BlockSpec((None, tm, fs), lambda j, i: (j, i, 0))
    return pl.pallas_call(
        body, grid=(s, t // tm), name="ffn_down_bwd",
        in_specs=[pl.BlockSpec((tm, n), lambda j, i: (i, 0)),
                  pl.BlockSpec((None, None, fs, n), lambda j, i: (l, j, 0, 0)), bspec, bspec],
        out_specs=[bspec, bspec], out_shape=[SDS((s, t, fs), bf16)] * 2,
        compiler_params=_cparams(("parallel", "parallel")))(df, wd, g, u)


def _ffn_up_bwd(dg, du, wg, wu, l):
    s, t, fs = dg.shape
    k = wg.shape[2]
    tm = _tile(t)
    aspec = pl.BlockSpec((None, tm, fs), lambda i, j, r: (r, i, 0))
    wspec = pl.BlockSpec((None, None, k, fs), lambda i, j, r: (l, r, 0, 0))
    return _gemm("ffn_up_bwd", [dg, wg, du, wu], [aspec, wspec, aspec, wspec],
                 pl.BlockSpec((tm, k), lambda i, j, r: (i, 0)),
                 SDS((t, k), f32), (t // tm, 1, s), NT, (tm, k))


def _ffn_wgrad_up(name, h, dy):
    t, k = h.shape
    s, _, fs = dy.shape
    tt = _tile(t)
    return _gemm(
        name, [h, dy],
        [pl.BlockSpec((tt, k), lambda i, j, r: (r, 0)),
         pl.BlockSpec((None, tt, fs), lambda i, j, r: (i, r, 0))],
        pl.BlockSpec((None, k, fs), lambda i, j, r: (i, 0, 0)),
        SDS((s, k, fs), f32), (s, 1, t // tt), TN, (k, fs))


def _ffn_wgrad_down(hid, df):
    s, t, fs = hid.shape
    n = df.shape[1]
    tt = _tile(t)
    return _gemm(
        "ffn_wgrad_down", [hid, df],
        [pl.BlockSpec((None, tt, fs), lambda i, j, r: (i, r, 0)),
         pl.BlockSpec((tt, n), lambda i, j, r: (r, 0))],
        pl.BlockSpec((None, fs, n), lambda i, j, r: (i, 0, 0)),
        SDS((s, fs, n), f32), (s, 1, t // tt), TN, (fs, n))


def _rows(name, fn, rows, consts, row_outs, acc_outs=(), tr=256):
    rows = [r if isinstance(r, tuple) else (r, r.shape[1], 0) for r in rows]
    t = rows[0][0].shape[0]
    tr = min(tr, t)
    nin = len(rows) + len(consts)
    no, na = len(row_outs), len(acc_outs)

    def body(*refs):
        vals = fn(*[r[...] for r in refs[:nin]])
        if not isinstance(vals, (tuple, list)):
            vals = (vals,)
        for k in range(no):
            refs[nin + k][...] = vals[k].astype(refs[nin + k].dtype)
        first = pl.program_id(0) == 0
        for k in range(na):
            ref, val = refs[nin + no + k], vals[no + k]

            @pl.when(first)
            def _(ref=ref, val=val):
                ref[...] = val

            @pl.when(jnp.logical_not(first))
            def _(ref=ref, val=val):
                ref[...] += val

    in_specs = [pl.BlockSpec((tr, w), lambda i, cb=cb: (i, cb)) for (_, w, cb) in rows]
    in_specs += [pl.BlockSpec(c.shape, lambda i, nd=c.ndim: (0,) * nd) for c in consts]
    out_specs = [pl.BlockSpec((tr, w), lambda i: (i, 0)) for (w, _) in row_outs]
    out_specs += [pl.BlockSpec(s, lambda i, nd=len(s): (0,) * nd) for (s, _) in acc_outs]
    out_shape = [SDS((t, w), dt) for (w, dt) in row_outs] + [SDS(s, dt) for (s, dt) in acc_outs]
    res = pl.pallas_call(
        body, grid=(t // tr,), in_specs=in_specs, out_specs=out_specs, out_shape=out_shape,
        name=name, compiler_params=_cparams(("arbitrary",)))(*[r[0] for r in rows], *consts)
    return res


def _rstd(x):
    return lax.rsqrt(jnp.mean(x * x, axis=-1, keepdims=True) + RMS_EPS)


def _norm_fwd(x, g):
    return x * _rstd(x) * g


def _norm_bwd(u, dy, g):
    r = _rstd(u)
    n = u * r
    dn = dy * g
    du = r * (dn - n * jnp.mean(dn * n, axis=-1, keepdims=True))
    return du, jnp.sum(dy * n, axis=0, keepdims=True)


def _gelu(x):
    c = 0.7978845608028654
    return 0.5 * x * (1.0 + jnp.tanh(c * (x + 0.044715 * x * x * x)))


def _gelu_grad(x):
    c = 0.7978845608028654
    th = jnp.tanh(c * (x + 0.044715 * x * x * x))
    return 0.5 * (1.0 + th) + 0.5 * x * (1.0 - th * th) * c * (1.0 + 3.0 * 0.044715 * x * x)


def _chunk_valid(start):
    qi = lax.broadcasted_iota(jnp.int32, (QB_A, KW_A), 0)
    kj = lax.broadcasted_iota(jnp.int32, (QB_A, KW_A), 1)
    qc = qi // CHUNK
    kc = kj // CHUNK
    return (kc >= qc) & (kc <= qc + N_LEFT) & (kj + start >= PAD_A)


def _chunk_probs(q, k, bias, valid):
    s = lax.dot_general(q, k, (NT, ((), ())), preferred_element_type=f32) * (HEAD_DIM ** -0.5) + bias
    s = jnp.where(valid, s, -1e30)
    p = jnp.exp(s - jnp.max(s, axis=-1, keepdims=True))
    return p / jnp.sum(p, axis=-1, keepdims=True)


def _chunk_attn_fwd(proj, kpad, vpad, bias):
    t = proj.shape[0]
    tp = kpad.shape[0]

    def body(q_ref, k_ref, v_ref, b_ref, o_ref):
        start = pl.multiple_of(pl.program_id(1) * QB_A, QB_A)
        valid = _chunk_valid(start)
        for h in range(2):
            cols = pl.ds(h * HEAD_DIM, HEAD_DIM)
            k = k_ref[pl.ds(start, KW_A), cols]
            v = v_ref[pl.ds(start, KW_A), cols]
            p = _chunk_probs(q_ref[:, cols], k, b_ref[h], valid)
            o_ref[:, cols] = jnp.dot(p.astype(bf16), v, preferred_element_type=f32).astype(bf16)

    kv_spec = pl.BlockSpec((tp, LANES), lambda hp, qb: (0, hp))
    return pl.pallas_call(
        body, grid=(A_W // LANES, t // QB_A), name="chunk_attn_fwd",
        in_specs=[pl.BlockSpec((QB_A, LANES), lambda hp, qb: (qb, hp)), kv_spec, kv_spec,
                  pl.BlockSpec((2, QB_A, KW_A), lambda hp, qb: (hp, 0, 0))],
        out_specs=pl.BlockSpec((QB_A, LANES), lambda hp, qb: (qb, hp)),
        out_shape=SDS((t, A_W), bf16),
        compiler_params=_cparams(("parallel", "arbitrary")))(proj, kpad, vpad, bias)


def _chunk_attn_bwd(proj, kpad, vpad, bias, dout):
    t = proj.shape[0]
    tp = kpad.shape[0]

    def body(q_ref, k_ref, v_ref, b_ref, do_ref, dq_ref, dk_ref, dv_ref, db_ref):
        qb = pl.program_id(1)
        start = pl.multiple_of(qb * QB_A, QB_A)
        valid = _chunk_valid(start)

        @pl.when(qb == 0)
        def _():
            dk_ref[...] = jnp.zeros_like(dk_ref)
            dv_ref[...] = jnp.zeros_like(dv_ref)
            db_ref[...] = jnp.zeros_like(db_ref)

        for h in range(2):
            cols = pl.ds(h * HEAD_DIM, HEAD_DIM)
            win = pl.ds(start, KW_A)
            q = q_ref[:, cols]
            k = k_ref[win, cols]
            v = v_ref[win, cols]
            do = do_ref[:, cols]
            p = _chunk_probs(q, k, b_ref[h], valid)
            dp = lax.dot_general(do, v, (NT, ((), ())), preferred_element_type=f32)
            ds = p * (dp - jnp.sum(dp * p, axis=-1, keepdims=True))
            db_ref[h] += ds
            dsb = (ds * (HEAD_DIM ** -0.5)).astype(bf16)
            dq_ref[:, cols] = jnp.dot(dsb, k, preferred_element_type=f32).astype(bf16)
            dk_ref[win, cols] += lax.dot_general(dsb, q, (TN, ((), ())), preferred_element_type=f32)
            dv_ref[win, cols] += lax.dot_general(p.astype(bf16), do, (TN, ((), ())), preferred_element_type=f32)

    kv_spec = pl.BlockSpec((tp, LANES), lambda hp, qb: (0, hp))
    q_spec = pl.BlockSpec((QB_A, LANES), lambda hp, qb: (qb, hp))
    b_spec = pl.BlockSpec((2, QB_A, KW_A), lambda hp, qb: (hp, 0, 0))
    return pl.pallas_call(
        body, grid=(A_W // LANES, t // QB_A), name="chunk_attn_bwd",
        in_specs=[q_spec, kv_spec, kv_spec, b_spec, q_spec],
        out_specs=[q_spec, kv_spec, kv_spec, b_spec],
        out_shape=[SDS((t, A_W), bf16), SDS((tp, A_W), f32), SDS((tp, A_W), f32),
                   SDS((2 * A_W // LANES, QB_A, KW_A), f32)],
        compiler_params=_cparams(("parallel", "arbitrary")))(proj, kpad, vpad, bias, dout)


def _bias_index():
    u = np.arange(EXT_A)
    ext_idx = np.clip(PAD_A + QB_A - 1 - u, -REL_CLIP, REL_CLIP) + REL_CLIP
    qi = np.arange(QB_A)[:, None]
    kj = np.arange(KW_A)[None, :]
    win_idx = QB_A - 1 - qi + kj
    return ext_idx, win_idx


def _bias_window(table):
    ext_idx, win_idx = _bias_index()
    ext = jnp.take(table, jnp.asarray(ext_idx), axis=1)
    return jnp.take(ext, jnp.asarray(win_idx), axis=1)


def _bias_window_grad(dbias):
    ext_idx, _ = _bias_index()
    nh = dbias.shape[0]
    d = jnp.flip(dbias, axis=1)
    d = jnp.pad(d, ((0, 0), (0, 0), (0, EXT_A + 1 - KW_A)))
    m = d.reshape(nh, QB_A * (EXT_A + 1))[:, :QB_A * EXT_A].reshape(nh, QB_A, EXT_A)
    dext = jnp.sum(m, axis=1)
    onehot = jnp.asarray((ext_idx[:, None] == np.arange(2 * REL_CLIP + 1)[None, :]).astype(np.float32))
    return jnp.dot(dext, onehot, precision=lax.Precision.HIGHEST)


def _tri_suffix(x, tri):
    hi = x.astype(bf16)
    lo = (x - hi.astype(f32)).astype(bf16)
    return jnp.dot(hi, tri, preferred_element_type=f32) + jnp.dot(lo, tri, preferred_element_type=f32)


def _sb_block(q, k, run, tri, causal):
    z = lax.dot_general(q, k, (NT, ((), ())), preferred_element_type=f32) * (HEAD_DIM ** -0.5)
    e = jnp.exp(-jnp.abs(z))
    l1p = jnp.log(1.0 + e)
    lb = jnp.minimum(z, 0.0) - l1p
    lmb = lb - z
    if causal is not None:
        lmb = jnp.where(causal, lmb, 0.0)
    cs = _tri_suffix(lmb, tri)
    w = jnp.exp(lb + (run + cs - lmb))
    if causal is not None:
        w = jnp.where(causal, w, 0.0)
    return z, e, w, run + cs[:, 0:1]


def _sb_tri():
    r = lax.broadcasted_iota(jnp.int32, (SB_BLK, SB_BLK), 0)
    c = lax.broadcasted_iota(jnp.int32, (SB_BLK, SB_BLK), 1)
    return (r >= c).astype(bf16), c < r


def _sb_fwd(proj):
    t = proj.shape[0]
    cb = A_W // LANES

    def body(q_ref, k_ref, v_ref, o_ref, of_ref):
        qb = pl.program_id(1)
        tri, diag = _sb_tri()
        for h in range(2):
            cols = pl.ds(h * HEAD_DIM, HEAD_DIM)
            q = q_ref[:, cols]

            def pair(kb, run, acc, causal):
                rows = pl.ds(pl.multiple_of(kb * SB_BLK, SB_BLK), SB_BLK)
                _, _, w, run = _sb_block(q, k_ref[rows, cols], run, tri, causal)
                return run, acc + jnp.dot(w.astype(bf16), v_ref[rows, cols], preferred_element_type=f32)

            run, acc = pair(qb, jnp.zeros((SB_BLK, 1), f32), jnp.zeros((SB_BLK, HEAD_DIM), f32), diag)

            def step(i, carry):
                return pair(qb - 1 - i, carry[0], carry[1], None)

            run, acc = lax.fori_loop(0, qb, step, (run, acc))
            o_ref[:, cols] = acc.astype(bf16)
            of_ref[:, cols] = acc

    ospec = pl.BlockSpec((SB_BLK, LANES), lambda hp, qb: (qb, hp))
    return pl.pallas_call(
        body, grid=(cb, t // SB_BLK), name="sb_attn_fwd",
        in_specs=[pl.BlockSpec((SB_BLK, LANES), lambda hp, qb: (qb, 3 * cb + hp)),
                  pl.BlockSpec((t, LANES), lambda hp, qb: (0, 4 * cb + hp)),
                  pl.BlockSpec((t, LANES), lambda hp, qb: (0, 5 * cb + hp))],
        out_specs=[ospec, ospec], out_shape=[SDS((t, A_W), bf16), SDS((t, A_W), f32)],
        compiler_params=_cparams(("parallel", "arbitrary")))(proj, proj, proj)


def _sb_bwd(proj, out_b, dout):
    t = proj.shape[0]
    cb = A_W // LANES

    def body(q_ref, k_ref, v_ref, o_ref, do_ref, dq_ref, dk_ref, dv_ref):
        qb = pl.program_id(1)
        tri, diag = _sb_tri()

        @pl.when(qb == 0)
        def _():
            dk_ref[...] = jnp.zeros_like(dk_ref)
            dv_ref[...] = jnp.zeros_like(dv_ref)

        for h in range(2):
            cols = pl.ds(h * HEAD_DIM, HEAD_DIM)
            q = q_ref[:, cols]
            do = do_ref[:, cols]
            dsum = jnp.sum(do.astype(f32) * o_ref[:, cols], axis=-1, keepdims=True)

            def pair(kb, run, gsum, dq, causal):
                rows = pl.ds(pl.multiple_of(kb * SB_BLK, SB_BLK), SB_BLK)
                k = k_ref[rows, cols]
                v = v_ref[rows, cols]
                z, e, w, run = _sb_block(q, k, run, tri, causal)
                inv = 1.0 / (1.0 + e)
                beta = jnp.where(z >= 0.0, inv, e * inv)
                wb = w.astype(bf16)
                g = lax.dot_general(do, v, (NT, ((), ())), preferred_element_type=f32) * wb.astype(f32)
                sg = _tri_suffix(g, tri)
                dz = g * (1.0 - beta) - (dsum - gsum - sg) * beta
                if causal is not None:
                    dz = jnp.where(causal, dz, 0.0)
                dzb = (dz * (HEAD_DIM ** -0.5)).astype(bf16)
                dk_ref[rows, cols] += lax.dot_general(dzb, q, (TN, ((), ())), preferred_element_type=f32)
                dv_ref[rows, cols] += lax.dot_general(wb, do, (TN, ((), ())), preferred_element_type=f32)
                return run, gsum + sg[:, 0:1], dq + jnp.dot(dzb, k, preferred_element_type=f32)

            zero = jnp.zeros((SB_BLK, 1), f32)
            carry = pair(qb, zero, zero, jnp.zeros((SB_BLK, HEAD_DIM), f32), diag)

            def step(i, c):
                return pair(qb - 1 - i, c[0], c[1], c[2], None)

            carry = lax.fori_loop(0, qb, step, carry)
            dq_ref[:, cols] = carry[2].astype(bf16)

    kv_in = lambda seg: pl.BlockSpec((t, LANES), lambda hp, qb: (0, seg * cb + hp))
    q_spec = pl.BlockSpec((SB_BLK, LANES), lambda hp, qb: (qb, hp))
    kv_out = pl.BlockSpec((t, LANES), lambda hp, qb: (0, hp))
    return pl.pallas_call(
        body, grid=(cb, t // SB_BLK), name="sb_attn_bwd",
        in_specs=[pl.BlockSpec((SB_BLK, LANES), lambda hp, qb: (qb, 3 * cb + hp)), kv_in(4), kv_in(5),
                  q_spec, pl.BlockSpec((SB_BLK, LANES), lambda hp, qb: (qb, cb + hp))],
        out_specs=[q_spec, kv_out, kv_out],
        out_shape=[SDS((t, A_W), bf16), SDS((t, A_W), f32), SDS((t, A_W), f32)],
        compiler_params=_cparams(("parallel", "arbitrary")))(proj, proj, proj, out_b, dout)


def _halo_specs(tr, w, col, nblk):
    per = tr // SUBLANES
    cur = pl.BlockSpec((tr, w), lambda i: (i, col))
    prev = pl.BlockSpec((SUBLANES, w), lambda i: (jnp.maximum(i * per - 1, 0), col))
    nxt = pl.BlockSpec((SUBLANES, w), lambda i: (jnp.minimum((i + 1) * per, nblk * per - 1), col))
    return cur, prev, nxt


def _taps_before(cur, prev8, first):
    prev8 = jnp.where(first, 0.0, prev8)
    ext = jnp.concatenate([prev8, cur], axis=0)
    return [pltpu.roll(ext, s, 0)[SUBLANES:] for s in (3, 2, 1)]


def _taps_after(cur, next8, last):
    n = cur.shape[0]
    next8 = jnp.where(last, 0.0, next8)
    ext = jnp.concatenate([cur, next8], axis=0)
    return [pltpu.roll(ext, n + SUBLANES - s, 0)[:n] for s in (1, 2, 3)]


def _block_diag(x, w_ref, dims):
    outs = [lax.dot_general(x[:, n * LRU_BW:(n + 1) * LRU_BW], w_ref[n], (dims, ((), ())),
                            preferred_element_type=f32) for n in range(LRU_BLOCKS)]
    return jnp.concatenate(outs, axis=1)


def _lru_gates(xc, wa_ref, wi_ref, ba, bi, lam):
    xb = xc.astype(bf16)
    r = jax.nn.sigmoid(_block_diag(xb, wa_ref, NN) + ba)
    ig = jax.nn.sigmoid(_block_diag(xb, wi_ref, NN) + bi)
    sp = jnp.maximum(-lam, 0.0) + jnp.log(1.0 + jnp.exp(-jnp.abs(lam)))
    log_a = -LRU_C * r * sp
    a = jnp.exp(log_a)
    x2 = 2.0 * log_a
    one_minus = jnp.where(x2 > -1e-2, -x2 * (1.0 + x2 * (0.5 + x2 * (1.0 / 6.0))), 1.0 - a * a)
    mult = jnp.sqrt(one_minus)
    return xb, r, ig, sp, a, mult


def _rg_gates_fwd(proj, conv_w, conv_b, wa, wi, ba, bi, lam, tr=256):
    t = proj.shape[0]
    w = D_MODEL
    tr = min(tr, t)
    nblk = t // tr
    cur, prev, _ = _halo_specs(tr, w, 1, nblk)

    def body(x_ref, xp_ref, cw_ref, cb_ref, wa_ref, wi_ref, ba_ref, bi_ref, lam_ref, xc_ref, a_ref, u_ref):
        x = x_ref[...]
        taps = _taps_before(x, xp_ref[...], pl.program_id(0) == 0) + [x]
        xc = cb_ref[...]
        for k in range(4):
            xc = xc + cw_ref[k:k + 1, :] * taps[k]
        _, _, ig, _, a, mult = _lru_gates(xc, wa_ref, wi_ref, ba_ref[...], bi_ref[...], lam_ref[...])
        xc_ref[...] = xc
        a_ref[...] = a
        u_ref[...] = mult * (ig * xc)

    full = lambda a_: pl.BlockSpec(a_.shape, lambda i, nd=a_.ndim: (0,) * nd)
    ospec = pl.BlockSpec((tr, w), lambda i: (i, 0))
    return pl.pallas_call(
        body, grid=(nblk,), name="rg_gates_fwd",
        in_specs=[cur, prev] + [full(a_) for a_ in (conv_w, conv_b, wa, wi, ba, bi, lam)],
        out_specs=[ospec] * 3, out_shape=[SDS((t, w), f32)] * 3,
        compiler_params=_cparams(("parallel",)))(proj, proj, conv_w, conv_b, wa, wi, ba, bi, lam)


def _lru_scan(name, a, b, reverse, tt=512):
    t, w = a.shape
    tt = min(tt, t)
    nt = t // tt
    ng = tt // SUBLANES

    def body(a_ref, b_ref, h_ref, carry_ref):
        @pl.when(pl.program_id(0) == 0)
        def _():
            carry_ref[...] = jnp.zeros_like(carry_ref)

        row = lax.broadcasted_iota(jnp.int32, (SUBLANES, w), 0)

        def group(gi, carry):
            g = (ng - 1 - gi) if reverse else gi
            rows = pl.ds(pl.multiple_of(g * SUBLANES, SUBLANES), SUBLANES)
            av = a_ref[rows, :]
            bv = b_ref[rows, :]
            for s in (1, 2, 4):
                sh = (SUBLANES - s) if reverse else s
                ok = (row < SUBLANES - s) if reverse else (row >= s)
                a_s = pltpu.roll(av, sh, 0)
                b_s = pltpu.roll(bv, sh, 0)
                bv = jnp.where(ok, av * b_s + bv, bv)
                av = jnp.where(ok, av * a_s, av)
            h = av * carry + bv
            h_ref[rows, :] = h
            edge = h[0:1, :] if reverse else h[SUBLANES - 1:SUBLANES, :]
            return jnp.broadcast_to(edge, (SUBLANES, w))

        carry_ref[...] = lax.fori_loop(0, ng, group, carry_ref[...])

    tmap = (lambda i: (nt - 1 - i, 0)) if reverse else (lambda i: (i, 0))
    spec = pl.BlockSpec((tt, w), tmap)
    return pl.pallas_call(
        body, grid=(nt,), name=name, in_specs=[spec, spec], out_specs=spec,
        out_shape=SDS((t, w), f32), scratch_shapes=[pltpu.VMEM((SUBLANES, w), f32)],
        compiler_params=_cparams(("arbitrary",)))(a, b)


def _rg_gates_bwd(dhs, c, hs, xc, wa, wi, ba, bi, lam, tr=256):
    t, w = xc.shape
    tr = min(tr, t)
    nblk = t // tr
    cur, prev, nxt = _halo_specs(tr, w, 0, nblk)

    def body(dhs_ref, c_ref, cn_ref, hs_ref, hp_ref, xc_ref, wa_ref, wi_ref, ba_ref, bi_ref, lam_ref,
             dxc_ref, dwa_ref, dwi_ref, dba_ref, dbi_ref, dlam_ref):
        i = pl.program_id(0)
        c_next = _taps_after(c_ref[...], cn_ref[...], i == nblk - 1)[0]
        h_prev = _taps_before(hs_ref[...], hp_ref[...], i == 0)[2]
        xc = xc_ref[...]
        lam = lam_ref[...]
        xb, r, ig, sp, a, mult = _lru_gates(xc, wa_ref, wi_ref, ba_ref[...], bi_ref[...], lam)
        dh = dhs_ref[...] + c_next
        dlog_a = dh * h_prev * a - (dh * ig * xc) * (a * a / mult)
        dpre_a = (dlog_a * (-LRU_C * sp) * r * (1.0 - r)).astype(bf16)
        dpre_i = (dh * mult * xc * ig * (1.0 - ig)).astype(bf16)
        dxc_ref[...] = (dh * mult * ig + _block_diag(dpre_a, wa_ref, NT) + _block_diag(dpre_i, wi_ref, NT))
        dsig = 1.0 / (1.0 + jnp.exp(lam))
        sums = [jnp.sum(dpre_a.astype(f32), axis=0, keepdims=True),
                jnp.sum(dpre_i.astype(f32), axis=0, keepdims=True),
                jnp.sum(dlog_a * (-LRU_C * r), axis=0, keepdims=True) * (-dsig)]

        @pl.when(i == 0)
        def _():
            dwa_ref[...] = jnp.zeros_like(dwa_ref)
            dwi_ref[...] = jnp.zeros_like(dwi_ref)
            dba_ref[...] = jnp.zeros_like(dba_ref)
            dbi_ref[...] = jnp.zeros_like(dbi_ref)
            dlam_ref[...] = jnp.zeros_like(dlam_ref)

        for n in range(LRU_BLOCKS):
            sl = slice(n * LRU_BW, (n + 1) * LRU_BW)
            dwa_ref[n] += lax.dot_general(xb[:, sl], dpre_a[:, sl], (TN, ((), ())), preferred_element_type=f32)
            dwi_ref[n] += lax.dot_general(xb[:, sl], dpre_i[:, sl], (TN, ((), ())), preferred_element_type=f32)
        dba_ref[...] += sums[0]
        dbi_ref[...] += sums[1]
        dlam_ref[...] += sums[2]

    full = lambda a_: pl.BlockSpec(a_.shape, lambda i, nd=a_.ndim: (0,) * nd)
    vec = pl.BlockSpec((1, w), lambda i: (0, 0))
    mat = pl.BlockSpec((LRU_BLOCKS, LRU_BW, LRU_BW), lambda i: (0, 0, 0))
    return pl.pallas_call(
        body, grid=(nblk,), name="rg_gates_bwd",
        in_specs=[cur, cur, nxt, cur, prev, cur] + [full(a_) for a_ in (wa, wi, ba, bi, lam)],
        out_specs=[cur, mat, mat, vec, vec, vec],
        out_shape=[SDS((t, w), f32), SDS((LRU_BLOCKS, LRU_BW, LRU_BW), f32), SDS((LRU_BLOCKS, LRU_BW, LRU_BW), f32),
                   SDS((1, w), f32), SDS((1, w), f32), SDS((1, w), f32)],
        compiler_params=_cparams(("arbitrary",)))(dhs, c, c, hs, hs, xc, wa, wi, ba, bi, lam)


def _rg_conv_bwd(dxc, proj, conv_w, tr=256):
    t, w = dxc.shape
    tr = min(tr, t)
    nblk = t // tr
    cur, _, nxt = _halo_specs(tr, w, 0, nblk)
    xcur, xprev, _ = _halo_specs(tr, w, 1, nblk)

    def body(d_ref, dn_ref, x_ref, xp_ref, cw_ref, dx_ref, dcw_ref, dcb_ref):
        i = pl.program_id(0)
        d = d_ref[...]
        x = x_ref[...]
        after = _taps_after(d, dn_ref[...], i == nblk - 1)
        before = _taps_before(x, xp_ref[...], i == 0) + [x]
        dx = cw_ref[3:4, :] * d
        for s in (1, 2, 3):
            dx = dx + cw_ref[3 - s:4 - s, :] * after[s - 1]
        dx_ref[...] = dx.astype(bf16)
        dcw = jnp.concatenate([jnp.sum(d * before[k], axis=0, keepdims=True) for k in range(4)], axis=0)
        dcb = jnp.sum(d, axis=0, keepdims=True)

        @pl.when(i == 0)
        def _():
            dcw_ref[...] = dcw
            dcb_ref[...] = dcb

        @pl.when(i > 0)
        def _():
            dcw_ref[...] += dcw
            dcb_ref[...] += dcb

    return pl.pallas_call(
        body, grid=(nblk,), name="rg_conv_bwd",
        in_specs=[cur, nxt, xcur, xprev, pl.BlockSpec((4, w), lambda i: (0, 0))],
        out_specs=[cur, pl.BlockSpec((4, w), lambda i: (0, 0)), pl.BlockSpec((1, w), lambda i: (0, 0))],
        out_shape=[SDS((t, w), bf16), SDS((4, w), f32), SDS((1, w), f32)],
        compiler_params=_cparams(("arbitrary",)))(dxc, dxc, proj, proj, conv_w)


def _attn_fwd(h, wts, j):
    proj = _mm_cols("attn_in", h, wts["attn_w_in"], j, bf16)
    kpad = jnp.pad(proj[:, A_W:2 * A_W], ((PAD_A, 0), (0, 0)))
    vpad = jnp.pad(proj[:, 2 * A_W:3 * A_W], ((PAD_A, 0), (0, 0)))
    bias = _bias_window(wts["attn_rel_bias"][j])
    out_a = _chunk_attn_fwd(proj, kpad, vpad, bias)
    out_b, out_b32 = _sb_fwd(proj)
    m = _mm_rows("attn_out", [out_a, out_b], wts["attn_w_out"], j, f32)
    return m, (proj, kpad, vpad, bias, out_a, out_b, out_b32)


def _attn_bwd(dm, h, saved, wts, j, grads):
    proj, kpad, vpad, bias, out_a, out_b, out_b32 = saved
    dout = _mm_rows_t("attn_out_t", dm, wts["attn_w_out"], j, bf16)
    grads["attn_w_out"][j] = jnp.concatenate(
        [_mm_wgrad("attn_out_wgrad_a", out_a, dm), _mm_wgrad("attn_out_wgrad_b", out_b, dm)], axis=0)
    dqa, dka, dva, dbias = _chunk_attn_bwd(proj, kpad, vpad, bias, dout)
    dqs, dks, dvs = _sb_bwd(proj, out_b32, dout)
    grads["attn_rel_bias"][j] = _bias_window_grad(dbias)
    dproj = jnp.concatenate([dqa, dka[PAD_A:].astype(bf16), dva[PAD_A:].astype(bf16),
                             dqs, dks.astype(bf16), dvs.astype(bf16)], axis=1)
    grads["attn_w_in"][j] = _mm_wgrad_cols("attn_in_wgrad", h, dproj, N_CHIPS)
    return _mm_cols_t("attn_in_t", dproj, wts["attn_w_in"], j, f32)


def _rg_fwd(h, wts, j):
    proj = _mm_cols("rg_in", h, wts["rg_w_in"], j, f32)
    small = [wts[k][j] for k in ("rg_conv_w", "rg_conv_b", "rg_w_a", "rg_w_i", "rg_b_a", "rg_b_i", "rg_lambda")]
    xc, a, u = _rg_gates_fwd(proj, *small)
    hs = _lru_scan("lru_scan_fwd", a, u, False)
    yp = _rows("rg_gate_out", lambda hv, gv: hv * _gelu(gv), [hs, (proj, D_MODEL, 0)], [], [(D_MODEL, bf16)])[0]
    m = _mm_rows("rg_out", [yp], wts["rg_w_out"], j, f32)
    return m, (proj, xc, a, hs, yp)


def _rg_bwd(dm, h, saved, wts, j, grads):
    proj, xc, a, hs, yp = saved
    dyp = _mm_rows_t("rg_out_t", dm, wts["rg_w_out"], j, f32)
    grads["rg_w_out"][j] = _mm_wgrad("rg_out_wgrad", yp, dm)

    def gate_bwd(dy, hv, gv, av):
        dhs = dy * _gelu(gv)
        return dhs, av * dhs, dy * hv * _gelu_grad(gv)

    dhs, ab, dgate = _rows("rg_gate_out_bwd", gate_bwd, [dyp, hs, (proj, D_MODEL, 0), a], [],
                           [(D_MODEL, f32), (D_MODEL, f32), (D_MODEL, bf16)])
    c = _lru_scan("lru_scan_bwd", a, ab, True)
    wa, wi, ba, bi, lam = [wts[k][j] for k in ("rg_w_a", "rg_w_i", "rg_b_a", "rg_b_i", "rg_lambda")]
    dxc, dwa, dwi, dba, dbi, dlam = _rg_gates_bwd(dhs, c, hs, xc, wa, wi, ba, bi, lam)
    dxr, dcw, dcb = _rg_conv_bwd(dxc, proj, wts["rg_conv_w"][j])
    for k, v in (("rg_w_a", dwa), ("rg_w_i", dwi), ("rg_b_a", dba), ("rg_b_i", dbi), ("rg_lambda", dlam),
                 ("rg_conv_w", dcw), ("rg_conv_b", dcb)):
        grads[k][j] = v
    dproj = jnp.concatenate([dgate, dxr], axis=1)
    grads["rg_w_in"][j] = _mm_wgrad_cols("rg_in_wgrad", h, dproj, N_CHIPS)
    return _mm_cols_t("rg_in_t", dproj, wts["rg_w_in"], j, f32)


def _local_step(x, target, wts):
    t = x.shape[0]
    d = D_MODEL
    gains = {k: wts[k] for k in ("norm_mix_pre", "norm_mix_post", "norm_ffn_pre", "norm_ffn_post")}
    gain = lambda k, l: gains[k][l:l + 1]

    saved = []
    h = _rows("norm_in", _norm_fwd, [x], [gain("norm_mix_pre", 0)], [(d, bf16)])[0]
    loss_cols = None
    for l in range(DEPTH):
        j = l // 2
        m, mix_saved = (_attn_fwd if l % 2 == 0 else _rg_fwd)(h, wts, j)

        def resid_next(xv, mv, g_post, g_next):
            x1 = xv + _norm_fwd(mv, g_post)
            return x1, _norm_fwd(x1, g_next)

        x1, h2 = _rows("resid_mix", resid_next, [x, m], [gain("norm_mix_post", l), gain("norm_ffn_pre", l)],
                       [(d, f32), (d, bf16)])
        g, u, hid = _ffn_up(h2, wts["ffn_w_gate"], wts["ffn_w_up"], l)
        f = _ffn_down(hid, wts["ffn_w_down"], l)
        saved.append((x, h, m, mix_saved, x1, h2, g, u, hid, f))
        if l + 1 < DEPTH:
            x, h = _rows("resid_ffn", resid_next, [x1, f], [gain("norm_ffn_post", l), gain("norm_mix_pre", l + 1)],
                         [(d, f32), (d, bf16)])
        else:
            def resid_loss(xv, fv, tv, g_post):
                err = xv + _norm_fwd(fv, g_post) - tv
                return err * (1.0 / d), jnp.sum(err * err, axis=0, keepdims=True)

            dx, loss_cols = _rows("resid_loss", resid_loss, [x1, f, target], [gain("norm_ffn_post", l)],
                                  [(d, f32)], [((1, d), f32)])
    loss = 0.5 * jnp.sum(loss_cols) / d

    names = ("attn_w_in", "attn_rel_bias", "attn_w_out", "rg_w_in", "rg_conv_w", "rg_conv_b", "rg_w_a", "rg_b_a",
             "rg_w_i", "rg_b_i", "rg_lambda", "rg_w_out", "norm_mix_pre", "norm_mix_post", "norm_ffn_pre",
             "norm_ffn_post", "ffn_w_gate", "ffn_w_up", "ffn_w_down")
    grads = {k: {} for k in names}

    def norm_bwd_cast(uv, dyv, gv):
        du, dg = _norm_bwd(uv, dyv, gv)
        return du, dg

    def norm_bwd_resid(uv, dhv, dxv, gv):
        du, dg = _norm_bwd(uv, dhv, gv)
        return dxv + du, dg

    for l in reversed(range(DEPTH)):
        j = l // 2
        x_in, h, m, mix_saved, x1, h2, g, u, hid, f = saved[l]
        df, grads["norm_ffn_post"][l] = _rows("norm_ffn_post_bwd", norm_bwd_cast, [f, dx], [gain("norm_ffn_post", l)],
                                              [(d, bf16)], [((1, d), f32)])
        dg, du = _ffn_down_bwd(df, wts["ffn_w_down"], l, g, u)
        grads["ffn_w_down"][l] = _ffn_wgrad_down(hid, df)
        dh2 = _ffn_up_bwd(dg, du, wts["ffn_w_gate"], wts["ffn_w_up"], l)
        grads["ffn_w_gate"][l] = _ffn_wgrad_up("ffn_wgrad_gate", h2, dg)
        grads["ffn_w_up"][l] = _ffn_wgrad_up("ffn_wgrad_up", h2, du)
        dx1, grads["norm_ffn_pre"][l] = _rows("norm_ffn_pre_bwd", norm_bwd_resid, [x1, dh2, dx],
                                              [gain("norm_ffn_pre", l)], [(d, f32)], [((1, d), f32)])
        dm, grads["norm_mix_post"][l] = _rows("norm_mix_post_bwd", norm_bwd_cast, [m, dx1], [gain("norm_mix_post", l)],
                                              [(d, bf16)], [((1, d), f32)])
        dh = (_attn_bwd if l % 2 == 0 else _rg_bwd)(dm, h, mix_saved, wts, j, grads)
        dx, grads["norm_mix_pre"][l] = _rows("norm_mix_pre_bwd", norm_bwd_resid, [x_in, dh, dx1],
                                             [gain("norm_mix_pre", l)], [(d, f32)], [((1, d), f32)])
    return loss, dx, grads


ANY = pl.BlockSpec(memory_space=pl.ANY)
PACK_COLS = 1024
HALF_ROWS = 6160
HALF_TILE = 560


def _mesh_pos():
    x, y, c = lax.axis_index("x"), lax.axis_index("y"), lax.axis_index("c")
    return x, y, c, [(1 - x, y), (x, 1 - y), (1 - x, 1 - y)]


def _all_gather(shards):
    n = len(shards)

    def body(*refs):
        ins, outs = refs[:n], refs[n:2 * n]
        send, recv, loc = refs[2 * n:]
        x, y, c, chips = _mesh_pos()
        q = 2 * x + y
        copies = []
        for t in range(n):
            dst = outs[t].at[:, q]
            copies.append(pltpu.make_async_copy(ins[t], dst, loc.at[t]))
            for j, (px, py) in enumerate(chips):
                copies.append(pltpu.make_async_remote_copy(
                    src_ref=ins[t], dst_ref=dst, send_sem=send.at[3 * t + j], recv_sem=recv.at[3 * t + j],
                    device_id=(px, py, c), device_id_type=MESH))
        for cp in copies:
            cp.start()
        for cp in copies:
            cp.wait()

    return pl.pallas_call(
        body, name="weight_all_gather", in_specs=[ANY] * n, out_specs=[ANY] * n,
        out_shape=[SDS((s.shape[0], N_CHIPS) + s.shape[1:], s.dtype) for s in shards],
        scratch_shapes=[pltpu.SemaphoreType.DMA((3 * n,)), pltpu.SemaphoreType.DMA((3 * n,)),
                        pltpu.SemaphoreType.DMA((n,))])(*shards)


def _pair_exchange(gp):
    def body(g_ref, r_ref, send, recv):
        x, y, c, _ = _mesh_pos()
        src = g_ref.at[:, pl.ds(pl.multiple_of((1 - c) * HALF_ROWS, SUBLANES), HALF_ROWS)]
        cp = pltpu.make_async_remote_copy(src_ref=src, dst_ref=r_ref, send_sem=send, recv_sem=recv,
                                          device_id=(x, y, 1 - c), device_id_type=MESH)
        cp.start()
        cp.wait()

    return pl.pallas_call(
        body, name="grad_pair_exchange", in_specs=[ANY], out_specs=ANY,
        out_shape=SDS((N_CHIPS, HALF_ROWS, PACK_COLS), f32),
        scratch_shapes=[pltpu.SemaphoreType.DMA, pltpu.SemaphoreType.DMA])(gp)


def _pair_sum(gp, got, c):
    nt = HALF_ROWS // HALF_TILE

    def body(c_ref, a_ref, b_ref, o_ref):
        o_ref[...] = (a_ref[...] + b_ref[...]).astype(bf16)

    blk = (None, HALF_TILE, PACK_COLS)
    return pl.pallas_call(
        body, name="grad_pair_sum", out_shape=SDS((N_CHIPS, HALF_ROWS, PACK_COLS), bf16),
        grid_spec=pltpu.PrefetchScalarGridSpec(
            num_scalar_prefetch=1, grid=(N_CHIPS, nt),
            in_specs=[pl.BlockSpec(blk, lambda q, i, c_ref: (q, c_ref[0] * nt + i, 0)),
                      pl.BlockSpec(blk, lambda q, i, c_ref: (q, i, 0))],
            out_specs=pl.BlockSpec(blk, lambda q, i, c_ref: (q, i, 0))),
        compiler_params=_cparams(("parallel", "parallel")))(c, gp, got)


def _chip_exchange(h):
    def body(h_ref, s_ref, send, recv, loc):
        x, y, c, chips = _mesh_pos()
        q = 2 * x + y
        copies = [pltpu.make_async_copy(h_ref.at[q], s_ref.at[q], loc)]
        for j, (px, py) in enumerate(chips):
            copies.append(pltpu.make_async_remote_copy(
                src_ref=h_ref.at[2 * px + py], dst_ref=s_ref.at[q], send_sem=send.at[j], recv_sem=recv.at[j],
                device_id=(px, py, c), device_id_type=MESH))
        for cp in copies:
            cp.start()
        for cp in copies:
            cp.wait()

    return pl.pallas_call(
        body, name="grad_chip_exchange", in_specs=[ANY], out_specs=ANY, out_shape=SDS(h.shape, h.dtype),
        scratch_shapes=[pltpu.SemaphoreType.DMA((3,)), pltpu.SemaphoreType.DMA((3,)), pltpu.SemaphoreType.DMA])(h)


def _chip_sum(s):
    def body(s_ref, o_ref):
        acc = s_ref[0].astype(f32) + s_ref[1].astype(f32)
        o_ref[...] = (acc + s_ref[2].astype(f32)) + s_ref[3].astype(f32)

    return pl.pallas_call(
        body, name="grad_chip_sum", grid=(HALF_ROWS // HALF_TILE,),
        in_specs=[pl.BlockSpec((N_CHIPS, HALF_TILE, PACK_COLS), lambda i: (0, i, 0))],
        out_specs=pl.BlockSpec((HALF_TILE, PACK_COLS), lambda i: (i, 0)),
        out_shape=SDS((HALF_ROWS, PACK_COLS), f32), compiler_params=_cparams(("parallel",)))(s)


def _pair_gather(half):
    def body(h_ref, o_ref, send, recv, loc):
        x, y, c, _ = _mesh_pos()
        mine = pltpu.make_async_copy(h_ref, o_ref.at[c], loc)
        cp = pltpu.make_async_remote_copy(src_ref=h_ref, dst_ref=o_ref.at[c], send_sem=send, recv_sem=recv,
                                          device_id=(x, y, 1 - c), device_id_type=MESH)
        mine.start()
        cp.start()
        mine.wait()
        cp.wait()

    return pl.pallas_call(
        body, name="grad_pair_gather", in_specs=[ANY], out_specs=ANY,
        out_shape=SDS((2,) + half.shape, f32),
        scratch_shapes=[pltpu.SemaphoreType.DMA, pltpu.SemaphoreType.DMA, pltpu.SemaphoreType.DMA])(half)


COL_SHARDED = ("attn_w_in", "rg_w_in", "ffn_w_gate", "ffn_w_up")
ROW_SHARDED = ("attn_w_out", "rg_w_out")
GATES = ("rg_w_a", "rg_w_i")
VECTORS = ("rg_conv_w", "rg_conv_b", "rg_b_a", "rg_b_i", "rg_lambda")
REPLICATED = ("norm_mix_pre", "norm_mix_post", "norm_ffn_pre", "norm_ffn_post", "attn_rel_bias")
PACK_ORDER = COL_SHARDED + ROW_SHARDED + GATES + ("ffn_w_down",) + VECTORS + REPLICATED
WEIGHTS = ("attn_w_in", "attn_rel_bias", "attn_w_out", "rg_w_in", "rg_conv_w", "rg_conv_b", "rg_w_a", "rg_b_a",
           "rg_w_i", "rg_b_i", "rg_lambda", "rg_w_out", "norm_mix_pre", "norm_mix_post", "norm_ffn_pre",
           "norm_ffn_post", "ffn_w_gate", "ffn_w_up", "ffn_w_down")
SMALL = VECTORS + REPLICATED


def _gather_weights(w):
    big = list(COL_SHARDED + ROW_SHARDED + GATES + ("ffn_w_down",))
    shards = []
    for k in big:
        a = w[k].astype(bf16)
        shards.append(a.reshape((-1,) + a.shape[-2:]))
    vec = jnp.concatenate([w[k].reshape(-1) for k in VECTORS])
    shards.append(vec.reshape(1, -1, LANES))
    got = dict(zip(big + ["vec"], _all_gather(shards)))
    out = {k: w[k] for k in REPLICATED}
    for k in COL_SHARDED + ("ffn_w_down",):
        out[k] = got[k]
    for k in ROW_SHARDED:
        l, s, ks, n = got[k].shape
        out[k] = got[k].reshape(l, 1, s * ks, n)
    for k in GATES:
        out[k] = got[k].reshape(2, LRU_BLOCKS, LRU_BW, LRU_BW)
    vec = got["vec"].reshape(N_CHIPS, -1)
    off = 0
    for k in VECTORS:
        shp = w[k].shape
        n = int(np.prod(shp))
        piece = vec[:, off:off + n].reshape((N_CHIPS,) + shp)
        off += n
        if k == "rg_conv_w":
            out[k] = piece.reshape(N_CHIPS, 2, 4, 256).transpose(1, 2, 0, 3).reshape(2, 4, D_MODEL)
        elif k in ("rg_b_a", "rg_b_i"):
            out[k] = piece.transpose(1, 2, 0, 3).reshape(2, 1, D_MODEL)
        else:
            out[k] = piece.transpose(1, 0, 2).reshape(2, 1, D_MODEL)
    return out


def _grad_blocks(name, g):
    st = jnp.stack([g[i] for i in sorted(g)])
    if name in COL_SHARDED or name == "ffn_w_down":
        st = st.transpose(1, 0, 2, 3)
    elif name in ROW_SHARDED:
        l, k, n = st.shape
        st = st.reshape(l, N_CHIPS, k // N_CHIPS, n).transpose(1, 0, 2, 3)
    elif name in GATES:
        st = st.reshape(2, LRU_BLOCKS, N_CHIPS, LRU_BW // N_CHIPS, LRU_BW).transpose(2, 0, 1, 3, 4)
    elif name == "rg_conv_w":
        st = st.reshape(2, 4, N_CHIPS, -1).transpose(2, 0, 1, 3)
    elif name in ("rg_b_a", "rg_b_i"):
        st = st.reshape(2, LRU_BLOCKS, N_CHIPS, -1).transpose(2, 0, 1, 3)
    elif name in VECTORS:
        st = st.reshape(2, N_CHIPS, -1).transpose(1, 0, 2)
    else:
        st = jnp.broadcast_to(st.reshape(1, -1), (N_CHIPS, st.size))
    return st.reshape(N_CHIPS, -1)


def _reduce_gradients(grads, shard_shapes):
    blocks = [_grad_blocks(k, grads[k]) for k in PACK_ORDER]
    used = sum(b.shape[1] for b in blocks)
    total = 2 * HALF_ROWS * PACK_COLS
    gp = jnp.concatenate(blocks + [jnp.zeros((N_CHIPS, total - used), f32)], axis=1)
    gp = gp.reshape(N_CHIPS, 2 * HALF_ROWS, PACK_COLS)
    c = lax.axis_index("c").astype(jnp.int32).reshape(1)
    part = _pair_sum(gp, _pair_exchange(gp), c)
    half = _chip_sum(_chip_exchange(part))
    flat = _pair_gather(half).reshape(-1)
    out, off = {}, 0
    for k in PACK_ORDER:
        n = int(np.prod(shard_shapes[k]))
        out[k] = flat[off:off + n].reshape(shard_shapes[k])
        off += n
    return out


def _adamw_fn(w, g, m, v):
    m = ADAM_B1 * m + (1.0 - ADAM_B1) * g
    v = ADAM_B2 * v + (1.0 - ADAM_B2) * (g * g)
    m_hat = m / (1.0 - ADAM_B1 ** ADAM_STEP)
    v_hat = v / (1.0 - ADAM_B2 ** ADAM_STEP)
    return -ADAM_LR * (m_hat / (jnp.sqrt(v_hat) + ADAM_EPS) + ADAM_WD * w), m, v


def _adamw(name, w, g, m, v):
    shp = w.shape
    if w.size >= 1 << 16:
        width = shp[-1]
        ops = [a.reshape(-1, width) for a in (w, g, m, v)]
        res = _rows(name, _adamw_fn, ops, [], [(width, f32)] * 3)
        return [r.reshape(shp) for r in res]
    n = w.size
    rows = -(-n // (SUBLANES * LANES)) * SUBLANES
    ops = [jnp.pad(a.reshape(-1), (0, rows * LANES - n)).reshape(rows, LANES) for a in (w, g, m, v)]
    res = _rows(name, _adamw_fn, ops, [], [(LANES, f32)] * 3, tr=rows)
    return [r.reshape(-1)[:n].reshape(shp) for r in res]


def kernel(x, attn_w_in, attn_rel_bias, attn_w_out, rg_w_in, rg_conv_w, rg_conv_b, rg_w_a, rg_b_a, rg_w_i, rg_b_i, rg_lambda, rg_w_out, norm_mix_pre, norm_mix_post, norm_ffn_pre, norm_ffn_post, ffn_w_gate, ffn_w_up, ffn_w_down, loss_target, m_attn_w_in, m_attn_rel_bias, m_attn_w_out, m_rg_w_in, m_rg_conv_w, m_rg_conv_b, m_rg_w_a, m_rg_b_a, m_rg_w_i, m_rg_b_i, m_rg_lambda, m_rg_w_out, m_norm_mix_pre, m_norm_mix_post, m_norm_ffn_pre, m_norm_ffn_post, m_ffn_w_gate, m_ffn_w_up, m_ffn_w_down, v_attn_w_in, v_attn_rel_bias, v_attn_w_out, v_rg_w_in, v_rg_conv_w, v_rg_conv_b, v_rg_w_a, v_rg_b_a, v_rg_w_i, v_rg_b_i, v_rg_lambda, v_rg_w_out, v_norm_mix_pre, v_norm_mix_post, v_norm_ffn_pre, v_norm_ffn_post, v_ffn_w_gate, v_ffn_w_up, v_ffn_w_down):
    w = dict(zip(WEIGHTS, (attn_w_in, attn_rel_bias, attn_w_out, rg_w_in, rg_conv_w, rg_conv_b, rg_w_a, rg_b_a, rg_w_i,
                           rg_b_i, rg_lambda, rg_w_out, norm_mix_pre, norm_mix_post, norm_ffn_pre, norm_ffn_post,
                           ffn_w_gate, ffn_w_up, ffn_w_down)))
    m = dict(zip(WEIGHTS, (m_attn_w_in, m_attn_rel_bias, m_attn_w_out, m_rg_w_in, m_rg_conv_w, m_rg_conv_b, m_rg_w_a,
                           m_rg_b_a, m_rg_w_i, m_rg_b_i, m_rg_lambda, m_rg_w_out, m_norm_mix_pre, m_norm_mix_post,
                           m_norm_ffn_pre, m_norm_ffn_post, m_ffn_w_gate, m_ffn_w_up, m_ffn_w_down)))
    v = dict(zip(WEIGHTS, (v_attn_w_in, v_attn_rel_bias, v_attn_w_out, v_rg_w_in, v_rg_conv_w, v_rg_conv_b, v_rg_w_a,
                           v_rg_b_a, v_rg_w_i, v_rg_b_i, v_rg_lambda, v_rg_w_out, v_norm_mix_pre, v_norm_mix_post,
                           v_norm_ffn_pre, v_norm_ffn_post, v_ffn_w_gate, v_ffn_w_up, v_ffn_w_down)))
    wts = _gather_weights(w)
    loss, dx, grads = _local_step(x[0], loss_target[0], wts)
    loss = lax.psum(loss, ("x", "y", "c"))
    g = _reduce_gradients(grads, {k: w[k].shape for k in WEIGHTS})

    big = [k for k in WEIGHTS if k not in SMALL]
    upd = {k: _adamw("adamw_" + k, w[k], g[k], m[k], v[k]) for k in big}
    cat = lambda d: jnp.concatenate([d[k].reshape(-1) for k in SMALL])
    small = _adamw("adamw_small", cat(w), cat(g), cat(m), cat(v))
    off = 0
    for k in SMALL:
        n = w[k].size
        upd[k] = [r[off:off + n].reshape(w[k].shape) for r in small]
        off += n
    return (loss, dx[None], *[g[k] for k in WEIGHTS], *[upd[k][0] for k in WEIGHTS],
            *[upd[k][1] for k in WEIGHTS], *[upd[k][2] for k in WEIGHTS])
```

```python
import functools

import numpy as np
import jax
import jax.numpy as jnp
from jax import lax
from jax.experimental import pallas as pl
from jax.experimental.pallas import tpu as pltpu

f32 = jnp.float32
bf16 = jnp.bfloat16
SDS = jax.ShapeDtypeStruct
MESH = pl.DeviceIdType.MESH

D_MODEL = 1024
N_CHIPS = 4
DEPTH = 4
HEAD_DIM = 64
CHUNK = 64
N_LEFT = 8
REL_CLIP = 256
A_W = 512
LRU_BLOCKS = 4
LRU_BW = 256
LRU_C = 8.0
D_FF = 2816
RMS_EPS = 1e-6
LANES = 128
SUBLANES = 8
VMEM_LIMIT = 56 * 1024 * 1024

QB_A = 2 * CHUNK
KW_A = QB_A + N_LEFT * CHUNK
PAD_A = N_LEFT * CHUNK
EXT_A = 768
SB_BLK = 256
SB_DEAD = -110.0

ADAM_LR, ADAM_B1, ADAM_B2, ADAM_EPS, ADAM_WD, ADAM_STEP = 0.001, 0.9, 0.999, 1e-08, 0.01, 10


def _cparams(sem):
    return pltpu.CompilerParams(dimension_semantics=sem, vmem_limit_bytes=VMEM_LIMIT)


def _gemm(name, operands, in_specs, o_spec, out_shape, grid, dims, acc_shape):
    nred = grid[2]
    npair = len(operands) // 2

    def body(*refs):
        o_ref = refs[2 * npair]
        p = None
        for t in range(npair):
            d = lax.dot_general(refs[2 * t][...], refs[2 * t + 1][...], (dims, ((), ())),
                                preferred_element_type=f32)
            p = d if p is None else p + d
        if nred == 1:
            o_ref[...] = p.astype(o_ref.dtype)
        else:
            acc = refs[2 * npair + 1]
            r = pl.program_id(2)

            @pl.when(r == 0)
            def _():
                acc[...] = p

            @pl.when(r > 0)
            def _():
                acc[...] += p

            @pl.when(r == nred - 1)
            def _():
                o_ref[...] = acc[...].astype(o_ref.dtype)

    scratch = [] if nred == 1 else [pltpu.VMEM(acc_shape, f32)]
    return pl.pallas_call(
        body, grid=grid, in_specs=in_specs, out_specs=o_spec, out_shape=out_shape,
        scratch_shapes=scratch, name=name,
        compiler_params=_cparams(("parallel", "parallel", "arbitrary")))(*operands)


NN = ((1,), (0,))
NT = ((1,), (1,))
TN = ((0,), (0,))


def _tile(t, want=512):
    return min(want, t)


def _mm_cols(name, a, w, l, out_dtype):
    t, k = a.shape
    _, s, _, ns = w.shape
    tm = _tile(t)
    return _gemm(
        name, [a, w],
        [pl.BlockSpec((tm, k), lambda i, j, r: (i, 0)),
         pl.BlockSpec((None, None, k, ns), lambda i, j, r: (l, j, 0, 0))],
        pl.BlockSpec((tm, ns), lambda i, j, r: (i, j)),
        SDS((t, s * ns), out_dtype), (t // tm, s, 1), NN, None)


def _mm_cols_t(name, dy, w, l, out_dtype):
    t = dy.shape[0]
    _, s, k, ns = w.shape
    tm = _tile(t)
    return _gemm(
        name, [dy, w],
        [pl.BlockSpec((tm, ns), lambda i, j, r: (i, r)),
         pl.BlockSpec((None, None, k, ns), lambda i, j, r: (l, r, 0, 0))],
        pl.BlockSpec((tm, k), lambda i, j, r: (i, 0)),
        SDS((t, k), out_dtype), (t // tm, 1, s), NT, (tm, k))


def _mm_wgrad_cols(name, a, dy, s):
    t, k = a.shape
    ns = dy.shape[1] // s
    tt = _tile(t)
    return _gemm(
        name, [a, dy],
        [pl.BlockSpec((tt, k), lambda i, j, r: (r, 0)),
         pl.BlockSpec((tt, ns), lambda i, j, r: (r, i))],
        pl.BlockSpec((None, k, ns), lambda i, j, r: (i, 0, 0)),
        SDS((s, k, ns), f32), (s, 1, t // tt), TN, (k, ns))


def _mm_rows(name, parts, w, l, out_dtype):
    t = parts[0].shape[0]
    n = w.shape[3]
    tm = _tile(t)
    ops, specs = [], []
    for p_i, a in enumerate(parts):
        kp = a.shape[1]
        ops += [a, w]
        specs += [pl.BlockSpec((tm, kp), lambda i, j, r: (i, 0)),
                  pl.BlockSpec((None, None, kp, n), lambda i, j, r, p_i=p_i: (l, 0, p_i, 0))]
    return _gemm(name, ops, specs, pl.BlockSpec((tm, n), lambda i, j, r: (i, 0)),
                 SDS((t, n), out_dtype), (t // tm, 1, 1), NN, None)


def _mm_rows_t(name, dy, w, l, out_dtype):
    t, n = dy.shape
    k = w.shape[2]
    tm = _tile(t)
    return _gemm(
        name, [dy, w],
        [pl.BlockSpec((tm, n), lambda i, j, r: (i, 0)),
         pl.BlockSpec((None, None, k, n), lambda i, j, r: (l, 0, 0, 0))],
        pl.BlockSpec((tm, k), lambda i, j, r: (i, 0)),
        SDS((t, k), out_dtype), (t // tm, 1, 1), NT, None)


def _mm_wgrad(name, a, dy):
    t, k = a.shape
    n = dy.shape[1]
    tt = _tile(t)
    return _gemm(
        name, [a, dy],
        [pl.BlockSpec((tt, k), lambda i, j, r: (r, 0)),
         pl.BlockSpec((tt, n), lambda i, j, r: (r, 0))],
        pl.BlockSpec((k, n), lambda i, j, r: (0, 0)),
        SDS((k, n), f32), (1, 1, t // tt), TN, (k, n))


def _ffn_up(h, wg, wu, l):
    t, k = h.shape
    s, fs = wg.shape[1], wg.shape[3]
    tm = _tile(t)

    def body(h_ref, wg_ref, wu_ref, g_ref, u_ref, hid_ref):
        hv = h_ref[...]
        g = jnp.dot(hv, wg_ref[...], preferred_element_type=f32)
        u = jnp.dot(hv, wu_ref[...], preferred_element_type=f32)
        g_ref[...] = g.astype(bf16)
        u_ref[...] = u.astype(bf16)
        hid_ref[...] = (g * jax.nn.sigmoid(g) * u).astype(bf16)

    wspec = pl.BlockSpec((None, None, k, fs), lambda j, i: (l, j, 0, 0))
    ospec = pl.BlockSpec((None, tm, fs), lambda j, i: (j, i, 0))
    return pl.pallas_call(
        body, grid=(s, t // tm), name="ffn_up",
        in_specs=[pl.BlockSpec((tm, k), lambda j, i: (i, 0)), wspec, wspec],
        out_specs=[ospec, ospec, ospec], out_shape=[SDS((s, t, fs), bf16)] * 3,
        compiler_params=_cparams(("parallel", "parallel")))(h, wg, wu)


def _ffn_down(hid, wd, l):
    s, t, fs = hid.shape
    n = wd.shape[3]
    tm = _tile(t)
    return _gemm(
        "ffn_down", [hid, wd],
        [pl.BlockSpec((None, tm, fs), lambda i, j, r: (r, i, 0)),
         pl.BlockSpec((None, None, fs, n), lambda i, j, r: (l, r, 0, 0))],
        pl.BlockSpec((tm, n), lambda i, j, r: (i, 0)),
        SDS((t, n), f32), (t // tm, 1, s), NN, (tm, n))


def _ffn_down_bwd(df, wd, l, g, u):
    t, n = df.shape
    s, fs = wd.shape[1], wd.shape[2]
    tm = _tile(t)

    def body(df_ref, wd_ref, g_ref, u_ref, dg_ref, du_ref):
        dh = lax.dot_general(df_ref[...], wd_ref[...], (NT, ((), ())), preferred_element_type=f32)
        gv = g_ref[...].astype(f32)
        uv = u_ref[...].astype(f32)
        sg = jax.nn.sigmoid(gv)
        du_ref[...] = (dh * gv * sg).astype(bf16)
        dg_ref[...] = (dh * uv * (sg * (1.0 + gv * (1.0 - sg)))).astype(bf16)

    bspec = pl.BlockSpec((None, tm, fs), lambda j, i: (j, i, 0))
    return pl.pallas_call(
        body, grid=(s, t // tm), name="ffn_down_bwd",
        in_specs=[pl.BlockSpec((tm, n), lambda j, i: (i, 0)),
                  pl.BlockSpec((None, None, fs, n), lambda j, i: (l, j, 0, 0)), bspec, bspec],
        out_specs=[bspec, bspec], out_shape=[SDS((s, t, fs), bf16)] * 2,
        compiler_params=_cparams(("parallel", "parallel")))(df, wd, g, u)


def _ffn_up_bwd(dg, du, wg, wu, l):
    s, t, fs = dg.shape
    k = wg.shape[2]
    tm = _tile(t)
    aspec = pl.BlockSpec((None, tm, fs), lambda i, j, r: (r, i, 0))
    wspec = pl.BlockSpec((None, None, k, fs), lambda i, j, r: (l, r, 0, 0))
    return _gemm("ffn_up_bwd", [dg, wg, du, wu], [aspec, wspec, aspec, wspec],
                 pl.BlockSpec((tm, k), lambda i, j, r: (i, 0)),
                 SDS((t, k), f32), (t // tm, 1, s), NT, (tm, k))


def _ffn_wgrad_up(name, h, dy):
    t, k = h.shape
    s, _, fs = dy.shape
    tt = _tile(t)
    return _gemm(
        name, [h, dy],
        [pl.BlockSpec((tt, k), lambda i, j, r: (r, 0)),
         pl.BlockSpec((None, tt, fs), lambda i, j, r: (i, r, 0))],
        pl.BlockSpec((None, k, fs), lambda i, j, r: (i, 0, 0)),
        SDS((s, k, fs), f32), (s, 1, t // tt), TN, (k, fs))


def _ffn_wgrad_down(hid, df):
    s, t, fs = hid.shape
    n = df.shape[1]
    tt = _tile(t)
    return _gemm(
        "ffn_wgrad_down", [hid, df],
        [pl.BlockSpec((None, tt, fs), lambda i, j, r: (i, r, 0)),
         pl.BlockSpec((tt, n), lambda i, j, r: (r, 0))],
        pl.BlockSpec((None, fs, n), lambda i, j, r: (i, 0, 0)),
        SDS((s, fs, n), f32), (s, 1, t // tt), TN, (fs, n))


def _rows(name, fn, rows, consts, row_outs, acc_outs=(), tr=256):
    rows = [r if isinstance(r, tuple) else (r, r.shape[1], 0) for r in rows]
    t = rows[0][0].shape[0]
    tr = min(tr, t)
    nin = len(rows) + len(consts)
    no, na = len(row_outs), len(acc_outs)

    def body(*refs):
        vals = fn(*[r[...] for r in refs[:nin]])
        if not isinstance(vals, (tuple, list)):
            vals = (vals,)
        for k in range(no):
            refs[nin + k][...] = vals[k].astype(refs[nin + k].dtype)
        first = pl.program_id(0) == 0
        for k in range(na):
            ref, val = refs[nin + no + k], vals[no + k]

            @pl.when(first)
            def _(ref=ref, val=val):
                ref[...] = val

            @pl.when(jnp.logical_not(first))
            def _(ref=ref, val=val):
                ref[...] += val

    in_specs = [pl.BlockSpec((tr, w), lambda i, cb=cb: (i, cb)) for (_, w, cb) in rows]
    in_specs += [pl.BlockSpec(c.shape, lambda i, nd=c.ndim: (0,) * nd) for c in consts]
    out_specs = [pl.BlockSpec((tr, w), lambda i: (i, 0)) for (w, _) in row_outs]
    out_specs += [pl.BlockSpec(s, lambda i, nd=len(s): (0,) * nd) for (s, _) in acc_outs]
    out_shape = [SDS((t, w), dt) for (w, dt) in row_outs] + [SDS(s, dt) for (s, dt) in acc_outs]
    res = pl.pallas_call(
        body, grid=(t // tr,), in_specs=in_specs, out_specs=out_specs, out_shape=out_shape,
        name=name, compiler_params=_cparams(("arbitrary",)))(*[r[0] for r in rows], *consts)
    return res


def _rstd(x):
    return lax.rsqrt(jnp.mean(x * x, axis=-1, keepdims=True) + RMS_EPS)


def _norm_fwd(x, g):
    return x * _rstd(x) * g


def _norm_bwd(u, dy, g):
    r = _rstd(u)
    n = u * r
    dn = dy * g
    du = r * (dn - n * jnp.mean(dn * n, axis=-1, keepdims=True))
    return du, jnp.sum(dy * n, axis=0, keepdims=True)


def _gelu(x):
    c = 0.7978845608028654
    return 0.5 * x * (1.0 + jnp.tanh(c * (x + 0.044715 * x * x * x)))


def _gelu_grad(x):
    c = 0.7978845608028654
    th = jnp.tanh(c * (x + 0.044715 * x * x * x))
    return 0.5 * (1.0 + th) + 0.5 * x * (1.0 - th * th) * c * (1.0 + 3.0 * 0.044715 * x * x)


def _chunk_valid(start):
    qi = lax.broadcasted_iota(jnp.int32, (QB_A, KW_A), 0)
    kj = lax.broadcasted_iota(jnp.int32, (QB_A, KW_A), 1)
    qc = qi // CHUNK
    kc = kj // CHUNK
    return (kc >= qc) & (kc <= qc + N_LEFT) & (kj + start >= PAD_A)


def _chunk_probs(q, k, bias, valid):
    s = lax.dot_general(q, k, (NT, ((), ())), preferred_element_type=f32) * (HEAD_DIM ** -0.5) + bias
    s = jnp.where(valid, s, -1e30)
    p = jnp.exp(s - jnp.max(s, axis=-1, keepdims=True))
    return p / jnp.sum(p, axis=-1, keepdims=True)


def _chunk_attn_fwd(proj, kpad, vpad, bias):
    t = proj.shape[0]
    tp = kpad.shape[0]

    def body(q_ref, k_ref, v_ref, b_ref, o_ref):
        start = pl.multiple_of(pl.program_id(1) * QB_A, QB_A)
        valid = _chunk_valid(start)
        for h in range(2):
            cols = pl.ds(h * HEAD_DIM, HEAD_DIM)
            k = k_ref[pl.ds(start, KW_A), cols]
            v = v_ref[pl.ds(start, KW_A), cols]
            p = _chunk_probs(q_ref[:, cols], k, b_ref[h], valid)
            o_ref[:, cols] = jnp.dot(p.astype(bf16), v, preferred_element_type=f32).astype(bf16)

    kv_spec = pl.BlockSpec((tp, LANES), lambda hp, qb: (0, hp))
    return pl.pallas_call(
        body, grid=(A_W // LANES, t // QB_A), name="chunk_attn_fwd",
        in_specs=[pl.BlockSpec((QB_A, LANES), lambda hp, qb: (qb, hp)), kv_spec, kv_spec,
                  pl.BlockSpec((2, QB_A, KW_A), lambda hp, qb: (hp, 0, 0))],
        out_specs=pl.BlockSpec((QB_A, LANES), lambda hp, qb: (qb, hp)),
        out_shape=SDS((t, A_W), bf16),
        compiler_params=_cparams(("parallel", "arbitrary")))(proj, kpad, vpad, bias)


def _chunk_attn_bwd(proj, kpad, vpad, bias, dout):
    t = proj.shape[0]
    tp = kpad.shape[0]

    def body(q_ref, k_ref, v_ref, b_ref, do_ref, dq_ref, dk_ref, dv_ref, db_ref):
        qb = pl.program_id(1)
        start = pl.multiple_of(qb * QB_A, QB_A)
        valid = _chunk_valid(start)

        @pl.when(qb == 0)
        def _():
            dk_ref[...] = jnp.zeros_like(dk_ref)
            dv_ref[...] = jnp.zeros_like(dv_ref)
            db_ref[...] = jnp.zeros_like(db_ref)

        for h in range(2):
            cols = pl.ds(h * HEAD_DIM, HEAD_DIM)
            win = pl.ds(start, KW_A)
            q = q_ref[:, cols]
            k = k_ref[win, cols]
            v = v_ref[win, cols]
            do = do_ref[:, cols]
            p = _chunk_probs(q, k, b_ref[h], valid)
            dp = lax.dot_general(do, v, (NT, ((), ())), preferred_element_type=f32)
            ds = p * (dp - jnp.sum(dp * p, axis=-1, keepdims=True))
            db_ref[h] += ds
            dsb = (ds * (HEAD_DIM ** -0.5)).astype(bf16)
            dq_ref[:, cols] = jnp.dot(dsb, k, preferred_element_type=f32).astype(bf16)
            dk_ref[win, cols] += lax.dot_general(dsb, q, (TN, ((), ())), preferred_element_type=f32)
            dv_ref[win, cols] += lax.dot_general(p.astype(bf16), do, (TN, ((), ())), preferred_element_type=f32)

    kv_spec = pl.BlockSpec((tp, LANES), lambda hp, qb: (0, hp))
    q_spec = pl.BlockSpec((QB_A, LANES), lambda hp, qb: (qb, hp))
    b_spec = pl.BlockSpec((2, QB_A, KW_A), lambda hp, qb: (hp, 0, 0))
    return pl.pallas_call(
        body, grid=(A_W // LANES, t // QB_A), name="chunk_attn_bwd",
        in_specs=[q_spec, kv_spec, kv_spec, b_spec, q_spec],
        out_specs=[q_spec, kv_spec, kv_spec, b_spec],
        out_shape=[SDS((t, A_W), bf16), SDS((tp, A_W), f32), SDS((tp, A_W), f32),
                   SDS((2 * A_W // LANES, QB_A, KW_A), f32)],
        compiler_params=_cparams(("parallel", "arbitrary")))(proj, kpad, vpad, bias, dout)


def _bias_ext(table):
    flat = PAD_A + QB_A - 1 - REL_CLIP
    top = jnp.broadcast_to(table[:, 2 * REL_CLIP:], (table.shape[0], flat))
    lo = 2 * REL_CLIP - (EXT_A - 1 - flat)
    return jnp.concatenate([top, jnp.flip(table[:, lo:], axis=1)], axis=1)


def _bias_window(table):
    nh = table.shape[0]
    e = jnp.broadcast_to(_bias_ext(table)[:, None, :], (nh, QB_A, EXT_A)).reshape(nh, QB_A * EXT_A)
    e = jnp.pad(e, ((0, 0), (0, QB_A)))
    m = e.reshape(nh, QB_A, EXT_A + 1)
    return jnp.flip(m, axis=1)[:, :, :KW_A]


def _bias_window_grad(dbias):
    nh = dbias.shape[0]
    d = jnp.flip(dbias, axis=1)
    d = jnp.pad(d, ((0, 0), (0, 0), (0, EXT_A + 1 - KW_A)))
    m = d.reshape(nh, QB_A * (EXT_A + 1))[:, :QB_A * EXT_A].reshape(nh, QB_A, EXT_A)
    dext = jnp.sum(m, axis=1)
    flat = PAD_A + QB_A - 1 - REL_CLIP
    lo = 2 * REL_CLIP - (EXT_A - 1 - flat)
    tail = jnp.flip(dext[:, flat:], axis=1)
    tail = tail.at[:, -1].add(jnp.sum(dext[:, :flat], axis=1))
    return jnp.pad(tail, ((0, 0), (lo, 0)))


def _tri_suffix(x, tri):
    hi = x.astype(bf16)
    lo = (x - hi.astype(f32)).astype(bf16)
    return jnp.dot(hi, tri, preferred_element_type=f32) + jnp.dot(lo, tri, preferred_element_type=f32)


def _sb_block(q, k, run, tri, causal):
    z = lax.dot_general(q, k, (NT, ((), ())), preferred_element_type=f32) * (HEAD_DIM ** -0.5)
    e = jnp.exp(-jnp.abs(z))
    l1p = jnp.log(1.0 + e)
    lb = jnp.minimum(z, 0.0) - l1p
    lmb = lb - z
    if causal is not None:
        lmb = jnp.where(causal, lmb, 0.0)
    cs = _tri_suffix(lmb, tri)
    w = jnp.exp(lb + (run + cs - lmb))
    if causal is not None:
        w = jnp.where(causal, w, 0.0)
    return z, e, w, run + cs[:, 0:1]


def _sb_tri():
    r = lax.broadcasted_iota(jnp.int32, (SB_BLK, SB_BLK), 0)
    c = lax.broadcasted_iota(jnp.int32, (SB_BLK, SB_BLK), 1)
    return (r >= c).astype(bf16), c < r


def _sb_live(runs):
    m = runs[0]
    for r in runs[1:]:
        m = jnp.maximum(m, r)
    return jnp.max(m) > SB_DEAD


def _sb_fwd(proj):
    t = proj.shape[0]
    cb = A_W // LANES
    heads = [pl.ds(h * HEAD_DIM, HEAD_DIM) for h in range(LANES // HEAD_DIM)]

    def body(q_ref, k_ref, v_ref, o_ref, of_ref):
        qb = pl.program_id(1)
        tri, diag = _sb_tri()
        qs = [q_ref[:, cols] for cols in heads]

        def pair(kb, carry, causal):
            rows = pl.ds(pl.multiple_of(kb * SB_BLK, SB_BLK), SB_BLK)
            out = []
            for cols, q, (run, acc) in zip(heads, qs, carry):
                _, _, w, run = _sb_block(q, k_ref[rows, cols], run, tri, causal)
                out.append((run, acc + jnp.dot(w.astype(bf16), v_ref[rows, cols], preferred_element_type=f32)))
            return tuple(out)

        init = tuple((jnp.zeros((SB_BLK, 1), f32), jnp.zeros((SB_BLK, HEAD_DIM), f32)) for _ in heads)
        carry = pair(qb, init, diag)

        def cond(st):
            return (st[0] < qb) & _sb_live([c[0] for c in st[1]])

        def step(st):
            return st[0] + 1, pair(qb - 1 - st[0], st[1], None)

        _, carry = lax.while_loop(cond, step, (jnp.int32(0), carry))
        for cols, (_, acc) in zip(heads, carry):
            o_ref[:, cols] = acc.astype(bf16)
            of_ref[:, cols] = acc

    ospec = pl.BlockSpec((SB_BLK, LANES), lambda hp, qb: (qb, hp))
    return pl.pallas_call(
        body, grid=(cb, t // SB_BLK), name="sb_attn_fwd",
        in_specs=[pl.BlockSpec((SB_BLK, LANES), lambda hp, qb: (qb, 3 * cb + hp)),
                  pl.BlockSpec((t, LANES), lambda hp, qb: (0, 4 * cb + hp)),
                  pl.BlockSpec((t, LANES), lambda hp, qb: (0, 5 * cb + hp))],
        out_specs=[ospec, ospec], out_shape=[SDS((t, A_W), bf16), SDS((t, A_W), f32)],
        compiler_params=_cparams(("parallel", "arbitrary")))(proj, proj, proj)


def _sb_bwd(proj, out_b, dout):
    t = proj.shape[0]
    cb = A_W // LANES
    heads = [pl.ds(h * HEAD_DIM, HEAD_DIM) for h in range(LANES // HEAD_DIM)]

    def body(q_ref, k_ref, v_ref, o_ref, do_ref, dq_ref, dk_ref, dv_ref):
        qb = pl.program_id(1)
        tri, diag = _sb_tri()

        @pl.when(qb == 0)
        def _():
            dk_ref[...] = jnp.zeros_like(dk_ref)
            dv_ref[...] = jnp.zeros_like(dv_ref)

        qs = [q_ref[:, cols] for cols in heads]
        dos = [do_ref[:, cols] for cols in heads]
        dsums = [jnp.sum(do.astype(f32) * o_ref[:, cols], axis=-1, keepdims=True) for cols, do in zip(heads, dos)]

        def pair(kb, carry, causal):
            rows = pl.ds(pl.multiple_of(kb * SB_BLK, SB_BLK), SB_BLK)
            out = []
            for cols, q, do, dsum, (run, gsum, dq) in zip(heads, qs, dos, dsums, carry):
                k = k_ref[rows, cols]
                v = v_ref[rows, cols]
                z, e, w, run = _sb_block(q, k, run, tri, causal)
                inv = 1.0 / (1.0 + e)
                beta = jnp.where(z >= 0.0, inv, e * inv)
                wb = w.astype(bf16)
                g = lax.dot_general(do, v, (NT, ((), ())), preferred_element_type=f32) * wb.astype(f32)
                sg = _tri_suffix(g, tri)
                dz = g * (1.0 - beta) - (dsum - gsum - sg) * beta
                if causal is not None:
                    dz = jnp.where(causal, dz, 0.0)
                dzb = (dz * (HEAD_DIM ** -0.5)).astype(bf16)
                dk_ref[rows, cols] += lax.dot_general(dzb, q, (TN, ((), ())), preferred_element_type=f32)
                dv_ref[rows, cols] += lax.dot_general(wb, do, (TN, ((), ())), preferred_element_type=f32)
                out.append((run, gsum + sg[:, 0:1], dq + jnp.dot(dzb, k, preferred_element_type=f32)))
            return tuple(out)

        zero = jnp.zeros((SB_BLK, 1), f32)
        carry = pair(qb, tuple((zero, zero, jnp.zeros((SB_BLK, HEAD_DIM), f32)) for _ in heads), diag)

        def cond(st):
            return (st[0] < qb) & _sb_live([c[0] for c in st[1]])

        def step(st):
            return st[0] + 1, pair(qb - 1 - st[0], st[1], None)

        _, carry = lax.while_loop(cond, step, (jnp.int32(0), carry))
        for cols, c in zip(heads, carry):
            dq_ref[:, cols] = c[2].astype(bf16)

    kv_in = lambda seg: pl.BlockSpec((t, LANES), lambda hp, qb: (0, seg * cb + hp))
    q_spec = pl.BlockSpec((SB_BLK, LANES), lambda hp, qb: (qb, hp))
    kv_out = pl.BlockSpec((t, LANES), lambda hp, qb: (0, hp))
    return pl.pallas_call(
        body, grid=(cb, t // SB_BLK), name="sb_attn_bwd",
        in_specs=[pl.BlockSpec((SB_BLK, LANES), lambda hp, qb: (qb, 3 * cb + hp)), kv_in(4), kv_in(5),
                  q_spec, pl.BlockSpec((SB_BLK, LANES), lambda hp, qb: (qb, cb + hp))],
        out_specs=[q_spec, kv_out, kv_out],
        out_shape=[SDS((t, A_W), bf16), SDS((t, A_W), f32), SDS((t, A_W), f32)],
        compiler_params=_cparams(("parallel", "arbitrary")))(proj, proj, proj, out_b, dout)


def _halo_specs(tr, w, col, nblk):
    per = tr // SUBLANES
    cur = pl.BlockSpec((tr, w), lambda i: (i, col))
    prev = pl.BlockSpec((SUBLANES, w), lambda i: (jnp.maximum(i * per - 1, 0), col))
    nxt = pl.BlockSpec((SUBLANES, w), lambda i: (jnp.minimum((i + 1) * per, nblk * per - 1), col))
    return cur, prev, nxt


def _taps_before(cur, prev8, first):
    prev8 = jnp.where(first, 0.0, prev8)
    ext = jnp.concatenate([prev8, cur], axis=0)
    return [pltpu.roll(ext, s, 0)[SUBLANES:] for s in (3, 2, 1)]


def _taps_after(cur, next8, last):
    n = cur.shape[0]
    next8 = jnp.where(last, 0.0, next8)
    ext = jnp.concatenate([cur, next8], axis=0)
    return [pltpu.roll(ext, n + SUBLANES - s, 0)[:n] for s in (1, 2, 3)]


def _block_diag(x, w_ref, dims):
    outs = [lax.dot_general(x[:, n * LRU_BW:(n + 1) * LRU_BW], w_ref[n], (dims, ((), ())),
                            preferred_element_type=f32) for n in range(LRU_BLOCKS)]
    return jnp.concatenate(outs, axis=1)


def _lru_gates(xc, wa_ref, wi_ref, ba, bi, lam):
    xb = xc.astype(bf16)
    r = jax.nn.sigmoid(_block_diag(xb, wa_ref, NN) + ba)
    ig = jax.nn.sigmoid(_block_diag(xb, wi_ref, NN) + bi)
    sp = jnp.maximum(-lam, 0.0) + jnp.log(1.0 + jnp.exp(-jnp.abs(lam)))
    log_a = -LRU_C * r * sp
    a = jnp.exp(log_a)
    x2 = 2.0 * log_a
    one_minus = jnp.where(x2 > -1e-2, -x2 * (1.0 + x2 * (0.5 + x2 * (1.0 / 6.0))), 1.0 - a * a)
    mult = jnp.sqrt(one_minus)
    return xb, r, ig, sp, a, mult


def _rg_gates_fwd(proj, conv_w, conv_b, wa, wi, ba, bi, lam, tr=256):
    t = proj.shape[0]
    w = D_MODEL
    tr = min(tr, t)
    nblk = t // tr
    cur, prev, _ = _halo_specs(tr, w, 1, nblk)

    def body(x_ref, xp_ref, cw_ref, cb_ref, wa_ref, wi_ref, ba_ref, bi_ref, lam_ref, xc_ref, a_ref, u_ref):
        x = x_ref[...]
        taps = _taps_before(x, xp_ref[...], pl.program_id(0) == 0) + [x]
        xc = cb_ref[...]
        for k in range(4):
            xc = xc + cw_ref[k:k + 1, :] * taps[k]
        _, _, ig, _, a, mult = _lru_gates(xc, wa_ref, wi_ref, ba_ref[...], bi_ref[...], lam_ref[...])
        xc_ref[...] = xc
        a_ref[...] = a
        u_ref[...] = mult * (ig * xc)

    full = lambda a_: pl.BlockSpec(a_.shape, lambda i, nd=a_.ndim: (0,) * nd)
    ospec = pl.BlockSpec((tr, w), lambda i: (i, 0))
    return pl.pallas_call(
        body, grid=(nblk,), name="rg_gates_fwd",
        in_specs=[cur, prev] + [full(a_) for a_ in (conv_w, conv_b, wa, wi, ba, bi, lam)],
        out_specs=[ospec] * 3, out_shape=[SDS((t, w), f32)] * 3,
        compiler_params=_cparams(("parallel",)))(proj, proj, conv_w, conv_b, wa, wi, ba, bi, lam)


def _lru_scan(name, a, b, reverse, tt=512):
    t, w = a.shape
    tt = min(tt, t)
    nt = t // tt
    ng = tt // SUBLANES

    def body(a_ref, b_ref, h_ref, carry_ref):
        @pl.when(pl.program_id(0) == 0)
        def _():
            carry_ref[...] = jnp.zeros_like(carry_ref)

        row = lax.broadcasted_iota(jnp.int32, (SUBLANES, w), 0)

        def group(gi, carry):
            g = (ng - 1 - gi) if reverse else gi
            rows = pl.ds(pl.multiple_of(g * SUBLANES, SUBLANES), SUBLANES)
            av = a_ref[rows, :]
            bv = b_ref[rows, :]
            for s in (1, 2, 4):
                sh = (SUBLANES - s) if reverse else s
                ok = (row < SUBLANES - s) if reverse else (row >= s)
                a_s = pltpu.roll(av, sh, 0)
                b_s = pltpu.roll(bv, sh, 0)
                bv = jnp.where(ok, av * b_s + bv, bv)
                av = jnp.where(ok, av * a_s, av)
            h = av * carry + bv
            h_ref[rows, :] = h
            edge = h[0:1, :] if reverse else h[SUBLANES - 1:SUBLANES, :]
            return jnp.broadcast_to(edge, (SUBLANES, w))

        carry_ref[...] = lax.fori_loop(0, ng, group, carry_ref[...])

    tmap = (lambda i: (nt - 1 - i, 0)) if reverse else (lambda i: (i, 0))
    spec = pl.BlockSpec((tt, w), tmap)
    return pl.pallas_call(
        body, grid=(nt,), name=name, in_specs=[spec, spec], out_specs=spec,
        out_shape=SDS((t, w), f32), scratch_shapes=[pltpu.VMEM((SUBLANES, w), f32)],
        compiler_params=_cparams(("arbitrary",)))(a, b)


def _rg_gates_bwd(dhs, c, hs, xc, wa, wi, ba, bi, lam, tr=256):
    t, w = xc.shape
    tr = min(tr, t)
    nblk = t // tr
    cur, prev, nxt = _halo_specs(tr, w, 0, nblk)

    def body(dhs_ref, c_ref, cn_ref, hs_ref, hp_ref, xc_ref, wa_ref, wi_ref, ba_ref, bi_ref, lam_ref,
             dxc_ref, dwa_ref, dwi_ref, dba_ref, dbi_ref, dlam_ref):
        i = pl.program_id(0)
        c_next = _taps_after(c_ref[...], cn_ref[...], i == nblk - 1)[0]
        h_prev = _taps_before(hs_ref[...], hp_ref[...], i == 0)[2]
        xc = xc_ref[...]
        lam = lam_ref[...]
        xb, r, ig, sp, a, mult = _lru_gates(xc, wa_ref, wi_ref, ba_ref[...], bi_ref[...], lam)
        dh = dhs_ref[...] + c_next
        dlog_a = dh * h_prev * a - (dh * ig * xc) * (a * a / mult)
        dpre_a = (dlog_a * (-LRU_C * sp) * r * (1.0 - r)).astype(bf16)
        dpre_i = (dh * mult * xc * ig * (1.0 - ig)).astype(bf16)
        dxc_ref[...] = (dh * mult * ig + _block_diag(dpre_a, wa_ref, NT) + _block_diag(dpre_i, wi_ref, NT))
        dsig = 1.0 / (1.0 + jnp.exp(lam))
        sums = [jnp.sum(dpre_a.astype(f32), axis=0, keepdims=True),
                jnp.sum(dpre_i.astype(f32), axis=0, keepdims=True),
                jnp.sum(dlog_a * (-LRU_C * r), axis=0, keepdims=True) * (-dsig)]

        @pl.when(i == 0)
        def _():
            dwa_ref[...] = jnp.zeros_like(dwa_ref)
            dwi_ref[...] = jnp.zeros_like(dwi_ref)
            dba_ref[...] = jnp.zeros_like(dba_ref)
            dbi_ref[...] = jnp.zeros_like(dbi_ref)
            dlam_ref[...] = jnp.zeros_like(dlam_ref)

        for n in range(LRU_BLOCKS):
            sl = slice(n * LRU_BW, (n + 1) * LRU_BW)
            dwa_ref[n] += lax.dot_general(xb[:, sl], dpre_a[:, sl], (TN, ((), ())), preferred_element_type=f32)
            dwi_ref[n] += lax.dot_general(xb[:, sl], dpre_i[:, sl], (TN, ((), ())), preferred_element_type=f32)
        dba_ref[...] += sums[0]
        dbi_ref[...] += sums[1]
        dlam_ref[...] += sums[2]

    full = lambda a_: pl.BlockSpec(a_.shape, lambda i, nd=a_.ndim: (0,) * nd)
    vec = pl.BlockSpec((1, w), lambda i: (0, 0))
    mat = pl.BlockSpec((LRU_BLOCKS, LRU_BW, LRU_BW), lambda i: (0, 0, 0))
    return pl.pallas_call(
        body, grid=(nblk,), name="rg_gates_bwd",
        in_specs=[cur, cur, nxt, cur, prev, cur] + [full(a_) for a_ in (wa, wi, ba, bi, lam)],
        out_specs=[cur, mat, mat, vec, vec, vec],
        out_shape=[SDS((t, w), f32), SDS((LRU_BLOCKS, LRU_BW, LRU_BW), f32), SDS((LRU_BLOCKS, LRU_BW, LRU_BW), f32),
                   SDS((1, w), f32), SDS((1, w), f32), SDS((1, w), f32)],
        compiler_params=_cparams(("arbitrary",)))(dhs, c, c, hs, hs, xc, wa, wi, ba, bi, lam)


def _rg_conv_bwd(dxc, proj, conv_w, tr=256):
    t, w = dxc.shape
    tr = min(tr, t)
    nblk = t // tr
    cur, _, nxt = _halo_specs(tr, w, 0, nblk)
    xcur, xprev, _ = _halo_specs(tr, w, 1, nblk)

    def body(d_ref, dn_ref, x_ref, xp_ref, cw_ref, dx_ref, dcw_ref, dcb_ref):
        i = pl.program_id(0)
        d = d_ref[...]
        x = x_ref[...]
        after = _taps_after(d, dn_ref[...], i == nblk - 1)
        before = _taps_before(x, xp_ref[...], i == 0) + [x]
        dx = cw_ref[3:4, :] * d
        for s in (1, 2, 3):
            dx = dx + cw_ref[3 - s:4 - s, :] * after[s - 1]
        dx_ref[...] = dx.astype(bf16)
        dcw = jnp.concatenate([jnp.sum(d * before[k], axis=0, keepdims=True) for k in range(4)], axis=0)
        dcb = jnp.sum(d, axis=0, keepdims=True)

        @pl.when(i == 0)
        def _():
            dcw_ref[...] = dcw
            dcb_ref[...] = dcb

        @pl.when(i > 0)
        def _():
            dcw_ref[...] += dcw
            dcb_ref[...] += dcb

    return pl.pallas_call(
        body, grid=(nblk,), name="rg_conv_bwd",
        in_specs=[cur, nxt, xcur, xprev, pl.BlockSpec((4, w), lambda i: (0, 0))],
        out_specs=[cur, pl.BlockSpec((4, w), lambda i: (0, 0)), pl.BlockSpec((1, w), lambda i: (0, 0))],
        out_shape=[SDS((t, w), bf16), SDS((4, w), f32), SDS((1, w), f32)],
        compiler_params=_cparams(("arbitrary",)))(dxc, dxc, proj, proj, conv_w)


def _attn_fwd(h, wts, j):
    proj = _mm_cols("attn_in", h, wts["attn_w_in"], j, bf16)
    kpad = jnp.pad(proj[:, A_W:2 * A_W], ((PAD_A, 0), (0, 0)))
    vpad = jnp.pad(proj[:, 2 * A_W:3 * A_W], ((PAD_A, 0), (0, 0)))
    bias = _bias_window(wts["attn_rel_bias"][j])
    out_a = _chunk_attn_fwd(proj, kpad, vpad, bias)
    out_b, out_b32 = _sb_fwd(proj)
    m = _mm_rows("attn_out", [out_a, out_b], wts["attn_w_out"], j, f32)
    return m, (proj, kpad, vpad, bias, out_a, out_b, out_b32)


def _attn_bwd(dm, h, saved, wts, j, grads):
    proj, kpad, vpad, bias, out_a, out_b, out_b32 = saved
    dout = _mm_rows_t("attn_out_t", dm, wts["attn_w_out"], j, bf16)
    grads["attn_w_out"][j] = jnp.concatenate(
        [_mm_wgrad("attn_out_wgrad_a", out_a, dm), _mm_wgrad("attn_out_wgrad_b", out_b, dm)], axis=0)
    dqa, dka, dva, dbias = _chunk_attn_bwd(proj, kpad, vpad, bias, dout)
    dqs, dks, dvs = _sb_bwd(proj, out_b32, dout)
    grads["attn_rel_bias"][j] = _bias_window_grad(dbias)
    dproj = jnp.concatenate([dqa, dka[PAD_A:].astype(bf16), dva[PAD_A:].astype(bf16),
                             dqs, dks.astype(bf16), dvs.astype(bf16)], axis=1)
    grads["attn_w_in"][j] = _mm_wgrad_cols("attn_in_wgrad", h, dproj, N_CHIPS)
    return _mm_cols_t("attn_in_t", dproj, wts["attn_w_in"], j, f32)


def _rg_fwd(h, wts, j):
    proj = _mm_cols("rg_in", h, wts["rg_w_in"], j, f32)
    small = [wts[k][j] for k in ("rg_conv_w", "rg_conv_b", "rg_w_a", "rg_w_i", "rg_b_a", "rg_b_i", "rg_lambda")]
    xc, a, u = _rg_gates_fwd(proj, *small)
    hs = _lru_scan("lru_scan_fwd", a, u, False)
    yp = _rows("rg_gate_out", lambda hv, gv: hv * _gelu(gv), [hs, (proj, D_MODEL, 0)], [], [(D_MODEL, bf16)])[0]
    m = _mm_rows("rg_out", [yp], wts["rg_w_out"], j, f32)
    return m, (proj, xc, a, hs, yp)


def _rg_bwd(dm, h, saved, wts, j, grads):
    proj, xc, a, hs, yp = saved
    dyp = _mm_rows_t("rg_out_t", dm, wts["rg_w_out"], j, f32)
    grads["rg_w_out"][j] = _mm_wgrad("rg_out_wgrad", yp, dm)

    def gate_bwd(dy, hv, gv, av):
        dhs = dy * _gelu(gv)
        return dhs, av * dhs, dy * hv * _gelu_grad(gv)

    dhs, ab, dgate = _rows("rg_gate_out_bwd", gate_bwd, [dyp, hs, (proj, D_MODEL, 0), a], [],
                           [(D_MODEL, f32), (D_MODEL, f32), (D_MODEL, bf16)])
    c = _lru_scan("lru_scan_bwd", a, ab, True)
    wa, wi, ba, bi, lam = [wts[k][j] for k in ("rg_w_a", "rg_w_i", "rg_b_a", "rg_b_i", "rg_lambda")]
    dxc, dwa, dwi, dba, dbi, dlam = _rg_gates_bwd(dhs, c, hs, xc, wa, wi, ba, bi, lam)
    dxr, dcw, dcb = _rg_conv_bwd(dxc, proj, wts["rg_conv_w"][j])
    for k, v in (("rg_w_a", dwa), ("rg_w_i", dwi), ("rg_b_a", dba), ("rg_b_i", dbi), ("rg_lambda", dlam),
                 ("rg_conv_w", dcw), ("rg_conv_b", dcb)):
        grads[k][j] = v
    dproj = jnp.concatenate([dgate, dxr], axis=1)
    grads["rg_w_in"][j] = _mm_wgrad_cols("rg_in_wgrad", h, dproj, N_CHIPS)
    return _mm_cols_t("rg_in_t", dproj, wts["rg_w_in"], j, f32)


def _local_step(x, target, wts):
    t = x.shape[0]
    d = D_MODEL
    gains = {k: wts[k] for k in ("norm_mix_pre", "norm_mix_post", "norm_ffn_pre", "norm_ffn_post")}
    gain = lambda k, l: gains[k][l:l + 1]

    saved = []
    h = _rows("norm_in", _norm_fwd, [x], [gain("norm_mix_pre", 0)], [(d, bf16)])[0]
    loss_cols = None
    for l in range(DEPTH):
        j = l // 2
        m, mix_saved = (_attn_fwd if l % 2 == 0 else _rg_fwd)(h, wts, j)

        def resid_next(xv, mv, g_post, g_next):
            x1 = xv + _norm_fwd(mv, g_post)
            return x1, _norm_fwd(x1, g_next)

        x1, h2 = _rows("resid_mix", resid_next, [x, m], [gain("norm_mix_post", l), gain("norm_ffn_pre", l)],
                       [(d, f32), (d, bf16)])
        g, u, hid = _ffn_up(h2, wts["ffn_w_gate"], wts["ffn_w_up"], l)
        f = _ffn_down(hid, wts["ffn_w_down"], l)
        saved.append((x, h, m, mix_saved, x1, h2, g, u, hid, f))
        if l + 1 < DEPTH:
            x, h = _rows("resid_ffn", resid_next, [x1, f], [gain("norm_ffn_post", l), gain("norm_mix_pre", l + 1)],
                         [(d, f32), (d, bf16)])
        else:
            def resid_loss(xv, fv, tv, g_post):
                err = xv + _norm_fwd(fv, g_post) - tv
                return err * (1.0 / d), jnp.sum(err * err, axis=0, keepdims=True)

            dx, loss_cols = _rows("resid_loss", resid_loss, [x1, f, target], [gain("norm_ffn_post", l)],
                                  [(d, f32)], [((1, d), f32)])
    loss = 0.5 * jnp.sum(loss_cols) / d

    names = ("attn_w_in", "attn_rel_bias", "attn_w_out", "rg_w_in", "rg_conv_w", "rg_conv_b", "rg_w_a", "rg_b_a",
             "rg_w_i", "rg_b_i", "rg_lambda", "rg_w_out", "norm_mix_pre", "norm_mix_post", "norm_ffn_pre",
             "norm_ffn_post", "ffn_w_gate", "ffn_w_up", "ffn_w_down")
    grads = {k: {} for k in names}

    def norm_bwd_cast(uv, dyv, gv):
        du, dg = _norm_bwd(uv, dyv, gv)
        return du, dg

    def norm_bwd_resid(uv, dhv, dxv, gv):
        du, dg = _norm_bwd(uv, dhv, gv)
        return dxv + du, dg

    for l in reversed(range(DEPTH)):
        j = l // 2
        x_in, h, m, mix_saved, x1, h2, g, u, hid, f = saved[l]
        df, grads["norm_ffn_post"][l] = _rows("norm_ffn_post_bwd", norm_bwd_cast, [f, dx], [gain("norm_ffn_post", l)],
                                              [(d, bf16)], [((1, d), f32)])
        dg, du = _ffn_down_bwd(df, wts["ffn_w_down"], l, g, u)
        grads["ffn_w_down"][l] = _ffn_wgrad_down(hid, df)
        dh2 = _ffn_up_bwd(dg, du, wts["ffn_w_gate"], wts["ffn_w_up"], l)
        grads["ffn_w_gate"][l] = _ffn_wgrad_up("ffn_wgrad_gate", h2, dg)
        grads["ffn_w_up"][l] = _ffn_wgrad_up("ffn_wgrad_up", h2, du)
        dx1, grads["norm_ffn_pre"][l] = _rows("norm_ffn_pre_bwd", norm_bwd_resid, [x1, dh2, dx],
                                              [gain("norm_ffn_pre", l)], [(d, f32)], [((1, d), f32)])
        dm, grads["norm_mix_post"][l] = _rows("norm_mix_post_bwd", norm_bwd_cast, [m, dx1], [gain("norm_mix_post", l)],
                                              [(d, bf16)], [((1, d), f32)])
        dh = (_attn_bwd if l % 2 == 0 else _rg_bwd)(dm, h, mix_saved, wts, j, grads)
        dx, grads["norm_mix_pre"][l] = _rows("norm_mix_pre_bwd", norm_bwd_resid, [x_in, dh, dx1],
                                             [gain("norm_mix_pre", l)], [(d, f32)], [((1, d), f32)])
    return loss, dx, grads


ANY = pl.BlockSpec(memory_space=pl.ANY)
PACK_COLS = 1024
HALF_ROWS = 6160
HALF_TILE = 560


def _mesh_pos():
    x, y, c = lax.axis_index("x"), lax.axis_index("y"), lax.axis_index("c")
    return x, y, c, [(1 - x, y), (x, 1 - y), (1 - x, 1 - y)]


def _all_gather(shards):
    n = len(shards)

    def body(*refs):
        ins, outs = refs[:n], refs[n:2 * n]
        send, recv, loc = refs[2 * n:]
        x, y, c, chips = _mesh_pos()
        q = 2 * x + y
        copies = []
        for t in range(n):
            dst = outs[t].at[:, q]
            copies.append(pltpu.make_async_copy(ins[t], dst, loc.at[t]))
            for j, (px, py) in enumerate(chips):
                copies.append(pltpu.make_async_remote_copy(
                    src_ref=ins[t], dst_ref=dst, send_sem=send.at[3 * t + j], recv_sem=recv.at[3 * t + j],
                    device_id=(px, py, c), device_id_type=MESH))
        for cp in copies:
            cp.start()
        for cp in copies:
            cp.wait()

    return pl.pallas_call(
        body, name="weight_all_gather", in_specs=[ANY] * n, out_specs=[ANY] * n,
        out_shape=[SDS((s.shape[0], N_CHIPS) + s.shape[1:], s.dtype) for s in shards],
        scratch_shapes=[pltpu.SemaphoreType.DMA((3 * n,)), pltpu.SemaphoreType.DMA((3 * n,)),
                        pltpu.SemaphoreType.DMA((n,))])(*shards)


def _pair_exchange(gp):
    def body(g_ref, r_ref, send, recv):
        x, y, c, _ = _mesh_pos()
        src = g_ref.at[:, pl.ds(pl.multiple_of((1 - c) * HALF_ROWS, SUBLANES), HALF_ROWS)]
        cp = pltpu.make_async_remote_copy(src_ref=src, dst_ref=r_ref, send_sem=send, recv_sem=recv,
                                          device_id=(x, y, 1 - c), device_id_type=MESH)
        cp.start()
        cp.wait()

    return pl.pallas_call(
        body, name="grad_pair_exchange", in_specs=[ANY], out_specs=ANY,
        out_shape=SDS((N_CHIPS, HALF_ROWS, PACK_COLS), f32),
        scratch_shapes=[pltpu.SemaphoreType.DMA, pltpu.SemaphoreType.DMA])(gp)


def _pair_sum(gp, got, c):
    nt = HALF_ROWS // HALF_TILE

    def body(c_ref, a_ref, b_ref, o_ref):
        o_ref[...] = (a_ref[...] + b_ref[...]).astype(bf16)

    blk = (None, HALF_TILE, PACK_COLS)
    return pl.pallas_call(
        body, name="grad_pair_sum", out_shape=SDS((N_CHIPS, HALF_ROWS, PACK_COLS), bf16),
        grid_spec=pltpu.PrefetchScalarGridSpec(
            num_scalar_prefetch=1, grid=(N_CHIPS, nt),
            in_specs=[pl.BlockSpec(blk, lambda q, i, c_ref: (q, c_ref[0] * nt + i, 0)),
                      pl.BlockSpec(blk, lambda q, i, c_ref: (q, i, 0))],
            out_specs=pl.BlockSpec(blk, lambda q, i, c_ref: (q, i, 0))),
        compiler_params=_cparams(("parallel", "parallel")))(c, gp, got)


def _chip_exchange(h):
    def body(h_ref, s_ref, send, recv, loc):
        x, y, c, chips = _mesh_pos()
        q = 2 * x + y
        copies = [pltpu.make_async_copy(h_ref.at[q], s_ref.at[q], loc)]
        for j, (px, py) in enumerate(chips):
            copies.append(pltpu.make_async_remote_copy(
                src_ref=h_ref.at[2 * px + py], dst_ref=s_ref.at[q], send_sem=send.at[j], recv_sem=recv.at[j],
                device_id=(px, py, c), device_id_type=MESH))
        for cp in copies:
            cp.start()
        for cp in copies:
            cp.wait()

    return pl.pallas_call(
        body, name="grad_chip_exchange", in_specs=[ANY], out_specs=ANY, out_shape=SDS(h.shape, h.dtype),
        scratch_shapes=[pltpu.SemaphoreType.DMA((3,)), pltpu.SemaphoreType.DMA((3,)), pltpu.SemaphoreType.DMA])(h)


def _chip_sum(s):
    def body(s_ref, o_ref):
        acc = s_ref[0].astype(f32) + s_ref[1].astype(f32)
        o_ref[...] = (acc + s_ref[2].astype(f32)) + s_ref[3].astype(f32)

    return pl.pallas_call(
        body, name="grad_chip_sum", grid=(HALF_ROWS // HALF_TILE,),
        in_specs=[pl.BlockSpec((N_CHIPS, HALF_TILE, PACK_COLS), lambda i: (0, i, 0))],
        out_specs=pl.BlockSpec((HALF_TILE, PACK_COLS), lambda i: (i, 0)),
        out_shape=SDS((HALF_ROWS, PACK_COLS), f32), compiler_params=_cparams(("parallel",)))(s)


def _pair_gather(half):
    def body(h_ref, o_ref, send, recv, loc):
        x, y, c, _ = _mesh_pos()
        mine = pltpu.make_async_copy(h_ref, o_ref.at[c], loc)
        cp = pltpu.make_async_remote_copy(src_ref=h_ref, dst_ref=o_ref.at[c], send_sem=send, recv_sem=recv,
                                          device_id=(x, y, 1 - c), device_id_type=MESH)
        mine.start()
        cp.start()
        mine.wait()
        cp.wait()

    return pl.pallas_call(
        body, name="grad_pair_gather", in_specs=[ANY], out_specs=ANY,
        out_shape=SDS((2,) + half.shape, f32),
        scratch_shapes=[pltpu.SemaphoreType.DMA, pltpu.SemaphoreType.DMA, pltpu.SemaphoreType.DMA])(half)


COL_SHARDED = ("attn_w_in", "rg_w_in", "ffn_w_gate", "ffn_w_up")
ROW_SHARDED = ("attn_w_out", "rg_w_out")
GATES = ("rg_w_a", "rg_w_i")
VECTORS = ("rg_conv_w", "rg_conv_b", "rg_b_a", "rg_b_i", "rg_lambda")
REPLICATED = ("norm_mix_pre", "norm_mix_post", "norm_ffn_pre", "norm_ffn_post", "attn_rel_bias")
PACK_ORDER = COL_SHARDED + ROW_SHARDED + GATES + ("ffn_w_down",) + VECTORS + REPLICATED
WEIGHTS = ("attn_w_in", "attn_rel_bias", "attn_w_out", "rg_w_in", "rg_conv_w", "rg_conv_b", "rg_w_a", "rg_b_a",
           "rg_w_i", "rg_b_i", "rg_lambda", "rg_w_out", "norm_mix_pre", "norm_mix_post", "norm_ffn_pre",
           "norm_ffn_post", "ffn_w_gate", "ffn_w_up", "ffn_w_down")
SMALL = VECTORS + REPLICATED


def _gather_weights(w):
    big = list(COL_SHARDED + ROW_SHARDED + GATES + ("ffn_w_down",))
    shards = []
    for k in big:
        a = w[k].astype(bf16)
        shards.append(a.reshape((-1,) + a.shape[-2:]))
    vec = jnp.concatenate([w[k].reshape(-1) for k in VECTORS])
    shards.append(vec.reshape(1, -1, LANES))
    got = dict(zip(big + ["vec"], _all_gather(shards)))
    out = {k: w[k] for k in REPLICATED}
    for k in COL_SHARDED + ("ffn_w_down",):
        out[k] = got[k]
    for k in ROW_SHARDED:
        l, s, ks, n = got[k].shape
        out[k] = got[k].reshape(l, 1, s * ks, n)
    for k in GATES:
        out[k] = got[k].reshape(2, LRU_BLOCKS, LRU_BW, LRU_BW)
    vec = got["vec"].reshape(N_CHIPS, -1)
    off = 0
    for k in VECTORS:
        shp = w[k].shape
        n = int(np.prod(shp))
        piece = vec[:, off:off + n].reshape((N_CHIPS,) + shp)
        off += n
        if k == "rg_conv_w":
            out[k] = piece.reshape(N_CHIPS, 2, 4, 256).transpose(1, 2, 0, 3).reshape(2, 4, D_MODEL)
        elif k in ("rg_b_a", "rg_b_i"):
            out[k] = piece.transpose(1, 2, 0, 3).reshape(2, 1, D_MODEL)
        else:
            out[k] = piece.transpose(1, 0, 2).reshape(2, 1, D_MODEL)
    return out


def _grad_blocks(name, g):
    st = jnp.stack([g[i] for i in sorted(g)])
    if name in COL_SHARDED or name == "ffn_w_down":
        st = st.transpose(1, 0, 2, 3)
    elif name in ROW_SHARDED:
        l, k, n = st.shape
        st = st.reshape(l, N_CHIPS, k // N_CHIPS, n).transpose(1, 0, 2, 3)
    elif name in GATES:
        st = st.reshape(2, LRU_BLOCKS, N_CHIPS, LRU_BW // N_CHIPS, LRU_BW).transpose(2, 0, 1, 3, 4)
    elif name == "rg_conv_w":
        st = st.reshape(2, 4, N_CHIPS, -1).transpose(2, 0, 1, 3)
    elif name in ("rg_b_a", "rg_b_i"):
        st = st.reshape(2, LRU_BLOCKS, N_CHIPS, -1).transpose(2, 0, 1, 3)
    elif name in VECTORS:
        st = st.reshape(2, N_CHIPS, -1).transpose(1, 0, 2)
    else:
        st = jnp.broadcast_to(st.reshape(1, -1), (N_CHIPS, st.size))
    return st.reshape(N_CHIPS, -1)


def _reduce_gradients(grads, shard_shapes):
    blocks = [_grad_blocks(k, grads[k]) for k in PACK_ORDER]
    used = sum(b.shape[1] for b in blocks)
    total = 2 * HALF_ROWS * PACK_COLS
    gp = jnp.concatenate(blocks + [jnp.zeros((N_CHIPS, total - used), f32)], axis=1)
    gp = gp.reshape(N_CHIPS, 2 * HALF_ROWS, PACK_COLS)
    c = lax.axis_index("c").astype(jnp.int32).reshape(1)
    part = _pair_sum(gp, _pair_exchange(gp), c)
    half = _chip_sum(_chip_exchange(part))
    flat = _pair_gather(half).reshape(-1)
    out, off = {}, 0
    for k in PACK_ORDER:
        n = int(np.prod(shard_shapes[k]))
        out[k] = flat[off:off + n].reshape(shard_shapes[k])
        off += n
    return out


def _adamw_fn(w, g, m, v):
    m = ADAM_B1 * m + (1.0 - ADAM_B1) * g
    v = ADAM_B2 * v + (1.0 - ADAM_B2) * (g * g)
    m_hat = m / (1.0 - ADAM_B1 ** ADAM_STEP)
    v_hat = v / (1.0 - ADAM_B2 ** ADAM_STEP)
    return -ADAM_LR * (m_hat / (jnp.sqrt(v_hat) + ADAM_EPS) + ADAM_WD * w), m, v


def _adamw(name, w, g, m, v):
    shp = w.shape
    if w.size >= 1 << 16:
        width = shp[-1]
        ops = [a.reshape(-1, width) for a in (w, g, m, v)]
        res = _rows(name, _adamw_fn, ops, [], [(width, f32)] * 3)
        return [r.reshape(shp) for r in res]
    n = w.size
    rows = -(-n // (SUBLANES * LANES)) * SUBLANES
    ops = [jnp.pad(a.reshape(-1), (0, rows * LANES - n)).reshape(rows, LANES) for a in (w, g, m, v)]
    res = _rows(name, _adamw_fn, ops, [], [(LANES, f32)] * 3, tr=rows)
    return [r.reshape(-1)[:n].reshape(shp) for r in res]


def kernel(x, attn_w_in, attn_rel_bias, attn_w_out, rg_w_in, rg_conv_w, rg_conv_b, rg_w_a, rg_b_a, rg_w_i, rg_b_i, rg_lambda, rg_w_out, norm_mix_pre, norm_mix_post, norm_ffn_pre, norm_ffn_post, ffn_w_gate, ffn_w_up, ffn_w_down, loss_target, m_attn_w_in, m_attn_rel_bias, m_attn_w_out, m_rg_w_in, m_rg_conv_w, m_rg_conv_b, m_rg_w_a, m_rg_b_a, m_rg_w_i, m_rg_b_i, m_rg_lambda, m_rg_w_out, m_norm_mix_pre, m_norm_mix_post, m_norm_ffn_pre, m_norm_ffn_post, m_ffn_w_gate, m_ffn_w_up, m_ffn_w_down, v_attn_w_in, v_attn_rel_bias, v_attn_w_out, v_rg_w_in, v_rg_conv_w, v_rg_conv_b, v_rg_w_a, v_rg_b_a, v_rg_w_i, v_rg_b_i, v_rg_lambda, v_rg_w_out, v_norm_mix_pre, v_norm_mix_post, v_norm_ffn_pre, v_norm_ffn_post, v_ffn_w_gate, v_ffn_w_up, v_ffn_w_down):
    w = dict(zip(WEIGHTS, (attn_w_in, attn_rel_bias, attn_w_out, rg_w_in, rg_conv_w, rg_conv_b, rg_w_a, rg_b_a, rg_w_i,
                           rg_b_i, rg_lambda, rg_w_out, norm_mix_pre, norm_mix_post, norm_ffn_pre, norm_ffn_post,
                           ffn_w_gate, ffn_w_up, ffn_w_down)))
    m = dict(zip(WEIGHTS, (m_attn_w_in, m_attn_rel_bias, m_attn_w_out, m_rg_w_in, m_rg_conv_w, m_rg_conv_b, m_rg_w_a,
                           m_rg_b_a, m_rg_w_i, m_rg_b_i, m_rg_lambda, m_rg_w_out, m_norm_mix_pre, m_norm_mix_post,
                           m_norm_ffn_pre, m_norm_ffn_post, m_ffn_w_gate, m_ffn_w_up, m_ffn_w_down)))
    v = dict(zip(WEIGHTS, (v_attn_w_in, v_attn_rel_bias, v_attn_w_out, v_rg_w_in, v_rg_conv_w, v_rg_conv_b, v_rg_w_a,
                           v_rg_b_a, v_rg_w_i, v_rg_b_i, v_rg_lambda, v_rg_w_out, v_norm_mix_pre, v_norm_mix_post,
                           v_norm_ffn_pre, v_norm_ffn_post, v_ffn_w_gate, v_ffn_w_up, v_ffn_w_down)))
    wts = _gather_weights(w)
    loss, dx, grads = _local_step(x[0], loss_target[0], wts)
    loss = lax.psum(loss, ("x", "y", "c"))
    g = _reduce_gradients(grads, {k: w[k].shape for k in WEIGHTS})

    big = [k for k in WEIGHTS if k not in SMALL]
    upd = {k: _adamw("adamw_" + k, w[k], g[k], m[k], v[k]) for k in big}
    cat = lambda d: jnp.concatenate([d[k].reshape(-1) for k in SMALL])
    small = _adamw("adamw_small", cat(w), cat(g), cat(m), cat(v))
    off = 0
    for k in SMALL:
        n = w[k].size
        upd[k] = [r[off:off + n].reshape(w[k].shape) for r in small]
        off += n
    return (loss, dx[None], *[g[k] for k in WEIGHTS], *[upd[k][0] for k in WEIGHTS],
            *[upd[k][1] for k in WEIGHTS], *[upd[k][2] for k in WEIGHTS])
```

```python
import functools

import numpy as np
import jax
import jax.numpy as jnp
from jax import lax
from jax.experimental import pallas as pl
from jax.experimental.pallas import tpu as pltpu

f32 = jnp.float32
bf16 = jnp.bfloat16
SDS = jax.ShapeDtypeStruct
MESH = pl.DeviceIdType.MESH

D_MODEL = 1024
N_CHIPS = 4
DEPTH = 4
HEAD_DIM = 64
CHUNK = 64
N_LEFT = 8
REL_CLIP = 256
A_W = 512
LRU_BLOCKS = 4
LRU_BW = 256
LRU_C = 8.0
D_FF = 2816
RMS_EPS = 1e-6
LANES = 128
SUBLANES = 8
VMEM_LIMIT = 56 * 1024 * 1024

QB_A = 2 * CHUNK
KW_A = QB_A + N_LEFT * CHUNK
PAD_A = N_LEFT * CHUNK
EXT_A = 768
SB_BLK = 256
SB_DEAD = -110.0

ADAM_LR, ADAM_B1, ADAM_B2, ADAM_EPS, ADAM_WD, ADAM_STEP = 0.001, 0.9, 0.999, 1e-08, 0.01, 10


def _cparams(sem):
    return pltpu.CompilerParams(dimension_semantics=sem, vmem_limit_bytes=VMEM_LIMIT)


def _gemm(name, operands, in_specs, o_spec, out_shape, grid, dims, acc_shape, into=None):
    nred = grid[2]
    npair = len(operands) // 2
    nin = 2 * npair + (into is not None)

    def body(*refs):
        o_ref = refs[nin]
        p = None
        for t in range(npair):
            d = lax.dot_general(refs[2 * t][...], refs[2 * t + 1][...], (dims, ((), ())),
                                preferred_element_type=f32)
            p = d if p is None else p + d
        if nred == 1:
            o_ref[...] = p.astype(o_ref.dtype)
        else:
            acc = refs[nin + 1]
            r = pl.program_id(2)

            @pl.when(r == 0)
            def _():
                acc[...] = p

            @pl.when(r > 0)
            def _():
                acc[...] += p

            @pl.when(r == nred - 1)
            def _():
                o_ref[...] = acc[...].astype(o_ref.dtype)

    scratch = [] if nred == 1 else [pltpu.VMEM(acc_shape, f32)]
    extra, alias = ([], {}) if into is None else ([into], {2 * npair: 0})
    return pl.pallas_call(
        body, grid=grid, in_specs=list(in_specs) + [pl.BlockSpec(memory_space=pl.ANY)] * len(extra),
        out_specs=o_spec, out_shape=out_shape, scratch_shapes=scratch, name=name, input_output_aliases=alias,
        compiler_params=_cparams(("parallel", "parallel", "arbitrary")))(*operands, *extra)


NN = ((1,), (0,))
NT = ((1,), (1,))
TN = ((0,), (0,))


def _tile(t, want=512):
    return min(want, t)


def _mm_cols(name, a, w, l, out_dtype):
    t, k = a.shape
    _, s, _, ns = w.shape
    tm = _tile(t)
    return _gemm(
        name, [a, w],
        [pl.BlockSpec((tm, k), lambda i, j, r: (i, 0)),
         pl.BlockSpec((None, None, k, ns), lambda i, j, r: (l, j, 0, 0))],
        pl.BlockSpec((tm, ns), lambda i, j, r: (i, j)),
        SDS((t, s * ns), out_dtype), (t // tm, s, 1), NN, None)


def _mm_cols_t(name, dy, w, l, out_dtype):
    t = dy.shape[0]
    _, s, k, ns = w.shape
    tm = _tile(t)
    return _gemm(
        name, [dy, w],
        [pl.BlockSpec((tm, ns), lambda i, j, r: (i, r)),
         pl.BlockSpec((None, None, k, ns), lambda i, j, r: (l, r, 0, 0))],
        pl.BlockSpec((tm, k), lambda i, j, r: (i, 0)),
        SDS((t, k), out_dtype), (t // tm, 1, s), NT, (tm, k))


def _mm_wgrad_cols(name, a, dy, buf, l):
    t, k = a.shape
    _, s, _, ns = buf.shape
    tt = _tile(t)
    return _gemm(
        name, [a, dy],
        [pl.BlockSpec((tt, k), lambda i, j, r: (r, 0)),
         pl.BlockSpec((tt, ns), lambda i, j, r: (r, i))],
        pl.BlockSpec((None, None, k, ns), lambda i, j, r: (l, i, 0, 0)),
        SDS(buf.shape, f32), (s, 1, t // tt), TN, (k, ns), into=buf)


def _mm_rows(name, parts, w, l, out_dtype):
    t = parts[0].shape[0]
    n = w.shape[3]
    tm = _tile(t)
    ops, specs = [], []
    for p_i, a in enumerate(parts):
        kp = a.shape[1]
        ops += [a, w]
        specs += [pl.BlockSpec((tm, kp), lambda i, j, r: (i, 0)),
                  pl.BlockSpec((None, None, kp, n), lambda i, j, r, p_i=p_i: (l, 0, p_i, 0))]
    return _gemm(name, ops, specs, pl.BlockSpec((tm, n), lambda i, j, r: (i, 0)),
                 SDS((t, n), out_dtype), (t // tm, 1, 1), NN, None)


def _mm_rows_t(name, dy, w, l, out_dtype):
    t, n = dy.shape
    k = w.shape[2]
    tm = _tile(t)
    return _gemm(
        name, [dy, w],
        [pl.BlockSpec((tm, n), lambda i, j, r: (i, 0)),
         pl.BlockSpec((None, None, k, n), lambda i, j, r: (l, 0, 0, 0))],
        pl.BlockSpec((tm, k), lambda i, j, r: (i, 0)),
        SDS((t, k), out_dtype), (t // tm, 1, 1), NT, None)


def _mm_wgrad(name, a, dy, buf, l, part=0):
    t, k = a.shape
    n = dy.shape[1]
    tt = _tile(t)
    return _gemm(
        name, [a, dy],
        [pl.BlockSpec((tt, k), lambda i, j, r: (r, 0)),
         pl.BlockSpec((tt, n), lambda i, j, r: (r, 0))],
        pl.BlockSpec((None, k, n), lambda i, j, r: (l, part, 0)),
        SDS(buf.shape, f32), (1, 1, t // tt), TN, (k, n), into=buf)


def _ffn_up(h, wg, wu, l):
    t, k = h.shape
    s, fs = wg.shape[1], wg.shape[3]
    tm = _tile(t)

    def body(h_ref, wg_ref, wu_ref, g_ref, u_ref, hid_ref):
        hv = h_ref[...]
        g = jnp.dot(hv, wg_ref[...], preferred_element_type=f32)
        u = jnp.dot(hv, wu_ref[...], preferred_element_type=f32)
        g_ref[...] = g.astype(bf16)
        u_ref[...] = u.astype(bf16)
        hid_ref[...] = (g * jax.nn.sigmoid(g) * u).astype(bf16)

    wspec = pl.BlockSpec((None, None, k, fs), lambda j, i: (l, j, 0, 0))
    ospec = pl.BlockSpec((None, tm, fs), lambda j, i: (j, i, 0))
    return pl.pallas_call(
        body, grid=(s, t // tm), name="ffn_up",
        in_specs=[pl.BlockSpec((tm, k), lambda j, i: (i, 0)), wspec, wspec],
        out_specs=[ospec, ospec, ospec], out_shape=[SDS((s, t, fs), bf16)] * 3,
        compiler_params=_cparams(("parallel", "parallel")))(h, wg, wu)


def _ffn_down(hid, wd, l):
    s, t, fs = hid.shape
    n = wd.shape[3]
    tm = _tile(t)
    return _gemm(
        "ffn_down", [hid, wd],
        [pl.BlockSpec((None, tm, fs), lambda i, j, r: (r, i, 0)),
         pl.BlockSpec((None, None, fs, n), lambda i, j, r: (l, r, 0, 0))],
        pl.BlockSpec((tm, n), lambda i, j, r: (i, 0)),
        SDS((t, n), f32), (t // tm, 1, s), NN, (tm, n))


def _ffn_down_bwd(df, wd, l, g, u):
    t, n = df.shape
    s, fs = wd.shape[1], wd.shape[2]
    tm = _tile(t)

    def body(df_ref, wd_ref, g_ref, u_ref, dg_ref, du_ref):
        dh = lax.dot_general(df_ref[...], wd_ref[...], (NT, ((), ())), preferred_element_type=f32)
        gv = g_ref[...].astype(f32)
        uv = u_ref[...].astype(f32)
        sg = jax.nn.sigmoid(gv)
        du_ref[...] = (dh * gv * sg).astype(bf16)
        dg_ref[...] = (dh * uv * (sg * (1.0 + gv * (1.0 - sg)))).astype(bf16)

    bspec = pl.BlockSpec((None, tm, fs), lambda j, i: (j, i, 0))
    return pl.pallas_call(
        body, grid=(s, t // tm), name="ffn_down_bwd",
        in_specs=[pl.BlockSpec((tm, n), lambda j, i: (i, 0)),
                  pl.BlockSpec((None, None, fs, n), lambda j, i: (l, j, 0, 0)), bspec, bspec],
        out_specs=[bspec, bspec], out_shape=[SDS((s, t, fs), bf16)] * 2,
        compiler_params=_cparams(("parallel", "parallel")))(df, wd, g, u)


def _ffn_up_bwd(dg, du, wg, wu, l):
    s, t, fs = dg.shape
    k = wg.shape[2]
    tm = _tile(t)
    aspec = pl.BlockSpec((None, tm, fs), lambda i, j, r: (r, i, 0))
    wspec = pl.BlockSpec((None, None, k, fs), lambda i, j, r: (l, r, 0, 0))
    return _gemm("ffn_up_bwd", [dg, wg, du, wu], [aspec, wspec, aspec, wspec],
                 pl.BlockSpec((tm, k), lambda i, j, r: (i, 0)),
                 SDS((t, k), f32), (t // tm, 1, s), NT, (tm, k))


def _ffn_wgrad_up(name, h, dy, buf, l):
    t, k = h.shape
    s, _, fs = dy.shape
    tt = _tile(t)
    return _gemm(
        name, [h, dy],
        [pl.BlockSpec((tt, k), lambda i, j, r: (r, 0)),
         pl.BlockSpec((None, tt, fs), lambda i, j, r: (i, r, 0))],
        pl.BlockSpec((None, None, k, fs), lambda i, j, r: (l, i, 0, 0)),
        SDS(buf.shape, f32), (s, 1, t // tt), TN, (k, fs), into=buf)


def _ffn_wgrad_down(hid, df, buf, l):
    s, t, fs = hid.shape
    n = df.shape[1]
    tt = _tile(t)
    return _gemm(
        "ffn_wgrad_down", [hid, df],
        [pl.BlockSpec((None, tt, fs), lambda i, j, r: (i, r, 0)),
         pl.BlockSpec((tt, n), lambda i, j, r: (r, 0))],
        pl.BlockSpec((None, None, fs, n), lambda i, j, r: (l, i, 0, 0)),
        SDS(buf.shape, f32), (s, 1, t // tt), TN, (fs, n), into=buf)


def _rows(name, fn, rows, consts, row_outs, acc_outs=(), tr=256):
    rows = [r if isinstance(r, tuple) else (r, r.shape[1], 0) for r in rows]
    t = rows[0][0].shape[0]
    tr = min(tr, t)
    nin = len(rows) + len(consts)
    no, na = len(row_outs), len(acc_outs)

    def body(*refs):
        vals = fn(*[r[...] for r in refs[:nin]])
        if not isinstance(vals, (tuple, list)):
            vals = (vals,)
        for k in range(no):
            refs[nin + k][...] = vals[k].astype(refs[nin + k].dtype)
        first = pl.program_id(0) == 0
        for k in range(na):
            ref, val = refs[nin + no + k], vals[no + k]

            @pl.when(first)
            def _(ref=ref, val=val):
                ref[...] = val

            @pl.when(jnp.logical_not(first))
            def _(ref=ref, val=val):
                ref[...] += val

    in_specs = [pl.BlockSpec((tr, w), lambda i, cb=cb: (i, cb)) for (_, w, cb) in rows]
    in_specs += [pl.BlockSpec(c.shape, lambda i, nd=c.ndim: (0,) * nd) for c in consts]
    out_specs = [pl.BlockSpec((tr, w), lambda i: (i, 0)) for (w, _) in row_outs]
    out_specs += [pl.BlockSpec(s, lambda i, nd=len(s): (0,) * nd) for (s, _) in acc_outs]
    out_shape = [SDS((t, w), dt) for (w, dt) in row_outs] + [SDS(s, dt) for (s, dt) in acc_outs]
    res = pl.pallas_call(
        body, grid=(t // tr,), in_specs=in_specs, out_specs=out_specs, out_shape=out_shape,
        name=name, compiler_params=_cparams(("arbitrary",)))(*[r[0] for r in rows], *consts)
    return res


def _rstd(x):
    return lax.rsqrt(jnp.mean(x * x, axis=-1, keepdims=True) + RMS_EPS)


def _norm_fwd(x, g):
    return x * _rstd(x) * g


def _norm_bwd(u, dy, g):
    r = _rstd(u)
    n = u * r
    dn = dy * g
    du = r * (dn - n * jnp.mean(dn * n, axis=-1, keepdims=True))
    return du, jnp.sum(dy * n, axis=0, keepdims=True)


def _gelu(x):
    c = 0.7978845608028654
    return 0.5 * x * (1.0 + jnp.tanh(c * (x + 0.044715 * x * x * x)))


def _gelu_grad(x):
    c = 0.7978845608028654
    th = jnp.tanh(c * (x + 0.044715 * x * x * x))
    return 0.5 * (1.0 + th) + 0.5 * x * (1.0 - th * th) * c * (1.0 + 3.0 * 0.044715 * x * x)


def _chunk_valid(start):
    qi = lax.broadcasted_iota(jnp.int32, (QB_A, KW_A), 0)
    kj = lax.broadcasted_iota(jnp.int32, (QB_A, KW_A), 1)
    qc = qi // CHUNK
    kc = kj // CHUNK
    return (kc >= qc) & (kc <= qc + N_LEFT) & (kj + start >= PAD_A)


def _chunk_probs(q, k, bias, valid):
    s = lax.dot_general(q, k, (NT, ((), ())), preferred_element_type=f32) * (HEAD_DIM ** -0.5) + bias
    s = jnp.where(valid, s, -1e30)
    p = jnp.exp(s - jnp.max(s, axis=-1, keepdims=True))
    return p / jnp.sum(p, axis=-1, keepdims=True)


def _chunk_attn_fwd(proj, kpad, vpad, bias):
    t = proj.shape[0]
    tp = kpad.shape[0]

    def body(q_ref, k_ref, v_ref, b_ref, o_ref):
        start = pl.multiple_of(pl.program_id(1) * QB_A, QB_A)
        valid = _chunk_valid(start)
        for h in range(2):
            cols = pl.ds(h * HEAD_DIM, HEAD_DIM)
            k = k_ref[pl.ds(start, KW_A), cols]
            v = v_ref[pl.ds(start, KW_A), cols]
            p = _chunk_probs(q_ref[:, cols], k, b_ref[h], valid)
            o_ref[:, cols] = jnp.dot(p.astype(bf16), v, preferred_element_type=f32).astype(bf16)

    kv_spec = pl.BlockSpec((tp, LANES), lambda hp, qb: (0, hp))
    return pl.pallas_call(
        body, grid=(A_W // LANES, t // QB_A), name="chunk_attn_fwd",
        in_specs=[pl.BlockSpec((QB_A, LANES), lambda hp, qb: (qb, hp)), kv_spec, kv_spec,
                  pl.BlockSpec((2, QB_A, KW_A), lambda hp, qb: (hp, 0, 0))],
        out_specs=pl.BlockSpec((QB_A, LANES), lambda hp, qb: (qb, hp)),
        out_shape=SDS((t, A_W), bf16),
        compiler_params=_cparams(("parallel", "arbitrary")))(proj, kpad, vpad, bias)


def _chunk_attn_bwd(proj, kpad, vpad, bias, dout):
    t = proj.shape[0]
    tp = kpad.shape[0]

    def body(q_ref, k_ref, v_ref, b_ref, do_ref, dq_ref, dk_ref, dv_ref, db_ref):
        qb = pl.program_id(1)
        start = pl.multiple_of(qb * QB_A, QB_A)
        valid = _chunk_valid(start)

        @pl.when(qb == 0)
        def _():
            dk_ref[...] = jnp.zeros_like(dk_ref)
            dv_ref[...] = jnp.zeros_like(dv_ref)
            db_ref[...] = jnp.zeros_like(db_ref)

        for h in range(2):
            cols = pl.ds(h * HEAD_DIM, HEAD_DIM)
            win = pl.ds(start, KW_A)
            q = q_ref[:, cols]
            k = k_ref[win, cols]
            v = v_ref[win, cols]
            do = do_ref[:, cols]
            p = _chunk_probs(q, k, b_ref[h], valid)
            dp = lax.dot_general(do, v, (NT, ((), ())), preferred_element_type=f32)
            ds = p * (dp - jnp.sum(dp * p, axis=-1, keepdims=True))
            db_ref[h] += ds
            dsb = (ds * (HEAD_DIM ** -0.5)).astype(bf16)
            dq_ref[:, cols] = jnp.dot(dsb, k, preferred_element_type=f32).astype(bf16)
            dk_ref[win, cols] += lax.dot_general(dsb, q, (TN, ((), ())), preferred_element_type=f32)
            dv_ref[win, cols] += lax.dot_general(p.astype(bf16), do, (TN, ((), ())), preferred_element_type=f32)

    kv_spec = pl.BlockSpec((tp, LANES), lambda hp, qb: (0, hp))
    q_spec = pl.BlockSpec((QB_A, LANES), lambda hp, qb: (qb, hp))
    b_spec = pl.BlockSpec((2, QB_A, KW_A), lambda hp, qb: (hp, 0, 0))
    return pl.pallas_call(
        body, grid=(A_W // LANES, t // QB_A), name="chunk_attn_bwd",
        in_specs=[q_spec, kv_spec, kv_spec, b_spec, q_spec],
        out_specs=[q_spec, kv_spec, kv_spec, b_spec],
        out_shape=[SDS((t, A_W), bf16), SDS((tp, A_W), f32), SDS((tp, A_W), f32),
                   SDS((2 * A_W // LANES, QB_A, KW_A), f32)],
        compiler_params=_cparams(("parallel", "arbitrary")))(proj, kpad, vpad, bias, dout)


def _bias_ext(table):
    flat = PAD_A + QB_A - 1 - REL_CLIP
    top = jnp.broadcast_to(table[:, 2 * REL_CLIP:], (table.shape[0], flat))
    lo = 2 * REL_CLIP - (EXT_A - 1 - flat)
    return jnp.concatenate([top, jnp.flip(table[:, lo:], axis=1)], axis=1)


def _bias_window(table):
    nh = table.shape[0]
    e = jnp.broadcast_to(_bias_ext(table)[:, None, :], (nh, QB_A, EXT_A)).reshape(nh, QB_A * EXT_A)
    e = jnp.pad(e, ((0, 0), (0, QB_A)))
    m = e.reshape(nh, QB_A, EXT_A + 1)
    return jnp.flip(m, axis=1)[:, :, :KW_A]


def _bias_window_grad(dbias):
    nh = dbias.shape[0]
    d = jnp.flip(dbias, axis=1)
    d = jnp.pad(d, ((0, 0), (0, 0), (0, EXT_A + 1 - KW_A)))
    m = d.reshape(nh, QB_A * (EXT_A + 1))[:, :QB_A * EXT_A].reshape(nh, QB_A, EXT_A)
    dext = jnp.sum(m, axis=1)
    flat = PAD_A + QB_A - 1 - REL_CLIP
    lo = 2 * REL_CLIP - (EXT_A - 1 - flat)
    tail = jnp.flip(dext[:, flat:], axis=1)
    tail = tail.at[:, -1].add(jnp.sum(dext[:, :flat], axis=1))
    return jnp.pad(tail, ((0, 0), (lo, 0)))


def _tri_suffix(x, tri):
    hi = x.astype(bf16)
    lo = (x - hi.astype(f32)).astype(bf16)
    return jnp.dot(hi, tri, preferred_element_type=f32) + jnp.dot(lo, tri, preferred_element_type=f32)


def _sb_block(q, k, run, tri, causal):
    z = lax.dot_general(q, k, (NT, ((), ())), preferred_element_type=f32) * (HEAD_DIM ** -0.5)
    e = jnp.exp(-jnp.abs(z))
    l1p = jnp.log(1.0 + e)
    lb = jnp.minimum(z, 0.0) - l1p
    lmb = lb - z
    if causal is not None:
        lmb = jnp.where(causal, lmb, 0.0)
    cs = _tri_suffix(lmb, tri)
    w = jnp.exp(lb + (run + cs - lmb))
    if causal is not None:
        w = jnp.where(causal, w, 0.0)
    return z, e, w, run + cs[:, 0:1]


def _sb_tri():
    r = lax.broadcasted_iota(jnp.int32, (SB_BLK, SB_BLK), 0)
    c = lax.broadcasted_iota(jnp.int32, (SB_BLK, SB_BLK), 1)
    return (r >= c).astype(bf16), c < r


def _sb_live(runs):
    m = runs[0]
    for r in runs[1:]:
        m = jnp.maximum(m, r)
    return jnp.max(m) > SB_DEAD


def _sb_fwd(proj):
    t = proj.shape[0]
    cb = A_W // LANES
    heads = [pl.ds(h * HEAD_DIM, HEAD_DIM) for h in range(LANES // HEAD_DIM)]

    def body(q_ref, k_ref, v_ref, o_ref, of_ref):
        qb = pl.program_id(1)
        tri, diag = _sb_tri()
        qs = [q_ref[:, cols] for cols in heads]

        def pair(kb, carry, causal):
            rows = pl.ds(pl.multiple_of(kb * SB_BLK, SB_BLK), SB_BLK)
            out = []
            for cols, q, (run, acc) in zip(heads, qs, carry):
                _, _, w, run = _sb_block(q, k_ref[rows, cols], run, tri, causal)
                out.append((run, acc + jnp.dot(w.astype(bf16), v_ref[rows, cols], preferred_element_type=f32)))
            return tuple(out)

        init = tuple((jnp.zeros((SB_BLK, 1), f32), jnp.zeros((SB_BLK, HEAD_DIM), f32)) for _ in heads)
        carry = pair(qb, init, diag)

        def cond(st):
            return (st[0] < qb) & _sb_live([c[0] for c in st[1]])

        def step(st):
            return st[0] + 1, pair(qb - 1 - st[0], st[1], None)

        _, carry = lax.while_loop(cond, step, (jnp.int32(0), carry))
        for cols, (_, acc) in zip(heads, carry):
            o_ref[:, cols] = acc.astype(bf16)
            of_ref[:, cols] = acc

    ospec = pl.BlockSpec((SB_BLK, LANES), lambda hp, qb: (qb, hp))
    return pl.pallas_call(
        body, grid=(cb, t // SB_BLK), name="sb_attn_fwd",
        in_specs=[pl.BlockSpec((SB_BLK, LANES), lambda hp, qb: (qb, 3 * cb + hp)),
                  pl.BlockSpec((t, LANES), lambda hp, qb: (0, 4 * cb + hp)),
                  pl.BlockSpec((t, LANES), lambda hp, qb: (0, 5 * cb + hp))],
        out_specs=[ospec, ospec], out_shape=[SDS((t, A_W), bf16), SDS((t, A_W), f32)],
        compiler_params=_cparams(("parallel", "arbitrary")))(proj, proj, proj)


def _sb_bwd(proj, out_b, dout):
    t = proj.shape[0]
    cb = A_W // LANES
    heads = [pl.ds(h * HEAD_DIM, HEAD_DIM) for h in range(LANES // HEAD_DIM)]

    def body(q_ref, k_ref, v_ref, o_ref, do_ref, dq_ref, dk_ref, dv_ref):
        qb = pl.program_id(1)
        tri, diag = _sb_tri()

        @pl.when(qb == 0)
        def _():
            dk_ref[...] = jnp.zeros_like(dk_ref)
            dv_ref[...] = jnp.zeros_like(dv_ref)

        qs = [q_ref[:, cols] for cols in heads]
        dos = [do_ref[:, cols] for cols in heads]
        dsums = [jnp.sum(do.astype(f32) * o_ref[:, cols], axis=-1, keepdims=True) for cols, do in zip(heads, dos)]

        def pair(kb, carry, causal):
            rows = pl.ds(pl.multiple_of(kb * SB_BLK, SB_BLK), SB_BLK)
            out = []
            for cols, q, do, dsum, (run, gsum, dq) in zip(heads, qs, dos, dsums, carry):
                k = k_ref[rows, cols]
                v = v_ref[rows, cols]
                z, e, w, run = _sb_block(q, k, run, tri, causal)
                inv = 1.0 / (1.0 + e)
                beta = jnp.where(z >= 0.0, inv, e * inv)
                wb = w.astype(bf16)
                g = lax.dot_general(do, v, (NT, ((), ())), preferred_element_type=f32) * wb.astype(f32)
                sg = _tri_suffix(g, tri)
                dz = g * (1.0 - beta) - (dsum - gsum - sg) * beta
                if causal is not None:
                    dz = jnp.where(causal, dz, 0.0)
                dzb = (dz * (HEAD_DIM ** -0.5)).astype(bf16)
                dk_ref[rows, cols] += lax.dot_general(dzb, q, (TN, ((), ())), preferred_element_type=f32)
                dv_ref[rows, cols] += lax.dot_general(wb, do, (TN, ((), ())), preferred_element_type=f32)
                out.append((run, gsum + sg[:, 0:1], dq + jnp.dot(dzb, k, preferred_element_type=f32)))
            return tuple(out)

        zero = jnp.zeros((SB_BLK, 1), f32)
        carry = pair(qb, tuple((zero, zero, jnp.zeros((SB_BLK, HEAD_DIM), f32)) for _ in heads), diag)

        def cond(st):
            return (st[0] < qb) & _sb_live([c[0] for c in st[1]])

        def step(st):
            return st[0] + 1, pair(qb - 1 - st[0], st[1], None)

        _, carry = lax.while_loop(cond, step, (jnp.int32(0), carry))
        for cols, c in zip(heads, carry):
            dq_ref[:, cols] = c[2].astype(bf16)

    kv_in = lambda seg: pl.BlockSpec((t, LANES), lambda hp, qb: (0, seg * cb + hp))
    q_spec = pl.BlockSpec((SB_BLK, LANES), lambda hp, qb: (qb, hp))
    kv_out = pl.BlockSpec((t, LANES), lambda hp, qb: (0, hp))
    return pl.pallas_call(
        body, grid=(cb, t // SB_BLK), name="sb_attn_bwd",
        in_specs=[pl.BlockSpec((SB_BLK, LANES), lambda hp, qb: (qb, 3 * cb + hp)), kv_in(4), kv_in(5),
                  q_spec, pl.BlockSpec((SB_BLK, LANES), lambda hp, qb: (qb, cb + hp))],
        out_specs=[q_spec, kv_out, kv_out],
        out_shape=[SDS((t, A_W), bf16), SDS((t, A_W), f32), SDS((t, A_W), f32)],
        compiler_params=_cparams(("parallel", "arbitrary")))(proj, proj, proj, out_b, dout)


def _halo_specs(tr, w, col, nblk):
    per = tr // SUBLANES
    cur = pl.BlockSpec((tr, w), lambda i: (i, col))
    prev = pl.BlockSpec((SUBLANES, w), lambda i: (jnp.maximum(i * per - 1, 0), col))
    nxt = pl.BlockSpec((SUBLANES, w), lambda i: (jnp.minimum((i + 1) * per, nblk * per - 1), col))
    return cur, prev, nxt


def _taps_before(cur, prev8, first):
    prev8 = jnp.where(first, 0.0, prev8)
    ext = jnp.concatenate([prev8, cur], axis=0)
    return [pltpu.roll(ext, s, 0)[SUBLANES:] for s in (3, 2, 1)]


def _taps_after(cur, next8, last):
    n = cur.shape[0]
    next8 = jnp.where(last, 0.0, next8)
    ext = jnp.concatenate([cur, next8], axis=0)
    return [pltpu.roll(ext, n + SUBLANES - s, 0)[:n] for s in (1, 2, 3)]


def _block_diag(x, w_ref, dims):
    outs = [lax.dot_general(x[:, n * LRU_BW:(n + 1) * LRU_BW], w_ref[n], (dims, ((), ())),
                            preferred_element_type=f32) for n in range(LRU_BLOCKS)]
    return jnp.concatenate(outs, axis=1)


def _lru_gates(xc, wa_ref, wi_ref, ba, bi, lam):
    xb = xc.astype(bf16)
    r = jax.nn.sigmoid(_block_diag(xb, wa_ref, NN) + ba)
    ig = jax.nn.sigmoid(_block_diag(xb, wi_ref, NN) + bi)
    sp = jnp.maximum(-lam, 0.0) + jnp.log(1.0 + jnp.exp(-jnp.abs(lam)))
    log_a = -LRU_C * r * sp
    a = jnp.exp(log_a)
    x2 = 2.0 * log_a
    one_minus = jnp.where(x2 > -1e-2, -x2 * (1.0 + x2 * (0.5 + x2 * (1.0 / 6.0))), 1.0 - a * a)
    mult = jnp.sqrt(one_minus)
    return xb, r, ig, sp, a, mult


def _rg_gates_fwd(proj, conv_w, conv_b, wa, wi, ba, bi, lam, tr=256):
    t = proj.shape[0]
    w = D_MODEL
    tr = min(tr, t)
    nblk = t // tr
    cur, prev, _ = _halo_specs(tr, w, 1, nblk)

    def body(x_ref, xp_ref, cw_ref, cb_ref, wa_ref, wi_ref, ba_ref, bi_ref, lam_ref, xc_ref, a_ref, u_ref):
        x = x_ref[...]
        taps = _taps_before(x, xp_ref[...], pl.program_id(0) == 0) + [x]
        xc = cb_ref[...]
        for k in range(4):
            xc = xc + cw_ref[k:k + 1, :] * taps[k]
        _, _, ig, _, a, mult = _lru_gates(xc, wa_ref, wi_ref, ba_ref[...], bi_ref[...], lam_ref[...])
        xc_ref[...] = xc
        a_ref[...] = a
        u_ref[...] = mult * (ig * xc)

    full = lambda a_: pl.BlockSpec(a_.shape, lambda i, nd=a_.ndim: (0,) * nd)
    ospec = pl.BlockSpec((tr, w), lambda i: (i, 0))
    return pl.pallas_call(
        body, grid=(nblk,), name="rg_gates_fwd",
        in_specs=[cur, prev] + [full(a_) for a_ in (conv_w, conv_b, wa, wi, ba, bi, lam)],
        out_specs=[ospec] * 3, out_shape=[SDS((t, w), f32)] * 3,
        compiler_params=_cparams(("parallel",)))(proj, proj, conv_w, conv_b, wa, wi, ba, bi, lam)


def _lru_scan(name, a, b, reverse, tt=512):
    t, w = a.shape
    tt = min(tt, t)
    nt = t // tt
    ng = tt // SUBLANES

    def body(a_ref, b_ref, h_ref, carry_ref):
        @pl.when(pl.program_id(0) == 0)
        def _():
            carry_ref[...] = jnp.zeros_like(carry_ref)

        row = lax.broadcasted_iota(jnp.int32, (SUBLANES, w), 0)

        def group(gi, carry):
            g = (ng - 1 - gi) if reverse else gi
            rows = pl.ds(pl.multiple_of(g * SUBLANES, SUBLANES), SUBLANES)
            av = a_ref[rows, :]
            bv = b_ref[rows, :]
            for s in (1, 2, 4):
                sh = (SUBLANES - s) if reverse else s
                ok = (row < SUBLANES - s) if reverse else (row >= s)
                a_s = pltpu.roll(av, sh, 0)
                b_s = pltpu.roll(bv, sh, 0)
                bv = jnp.where(ok, av * b_s + bv, bv)
                av = jnp.where(ok, av * a_s, av)
            h = av * carry + bv
            h_ref[rows, :] = h
            edge = h[0:1, :] if reverse else h[SUBLANES - 1:SUBLANES, :]
            return jnp.broadcast_to(edge, (SUBLANES, w))

        carry_ref[...] = lax.fori_loop(0, ng, group, carry_ref[...])

    tmap = (lambda i: (nt - 1 - i, 0)) if reverse else (lambda i: (i, 0))
    spec = pl.BlockSpec((tt, w), tmap)
    return pl.pallas_call(
        body, grid=(nt,), name=name, in_specs=[spec, spec], out_specs=spec,
        out_shape=SDS((t, w), f32), scratch_shapes=[pltpu.VMEM((SUBLANES, w), f32)],
        compiler_params=_cparams(("arbitrary",)))(a, b)


def _rg_gates_bwd(dhs, c, hs, xc, wa, wi, ba, bi, lam, tr=256):
    t, w = xc.shape
    tr = min(tr, t)
    nblk = t // tr
    cur, prev, nxt = _halo_specs(tr, w, 0, nblk)

    def body(dhs_ref, c_ref, cn_ref, hs_ref, hp_ref, xc_ref, wa_ref, wi_ref, ba_ref, bi_ref, lam_ref,
             dxc_ref, dwa_ref, dwi_ref, dba_ref, dbi_ref, dlam_ref):
        i = pl.program_id(0)
        c_next = _taps_after(c_ref[...], cn_ref[...], i == nblk - 1)[0]
        h_prev = _taps_before(hs_ref[...], hp_ref[...], i == 0)[2]
        xc = xc_ref[...]
        lam = lam_ref[...]
        xb, r, ig, sp, a, mult = _lru_gates(xc, wa_ref, wi_ref, ba_ref[...], bi_ref[...], lam)
        dh = dhs_ref[...] + c_next
        dlog_a = dh * h_prev * a - (dh * ig * xc) * (a * a / mult)
        dpre_a = (dlog_a * (-LRU_C * sp) * r * (1.0 - r)).astype(bf16)
        dpre_i = (dh * mult * xc * ig * (1.0 - ig)).astype(bf16)
        dxc_ref[...] = (dh * mult * ig + _block_diag(dpre_a, wa_ref, NT) + _block_diag(dpre_i, wi_ref, NT))
        dsig = 1.0 / (1.0 + jnp.exp(lam))
        sums = [jnp.sum(dpre_a.astype(f32), axis=0, keepdims=True),
                jnp.sum(dpre_i.astype(f32), axis=0, keepdims=True),
                jnp.sum(dlog_a * (-LRU_C * r), axis=0, keepdims=True) * (-dsig)]

        @pl.when(i == 0)
        def _():
            dwa_ref[...] = jnp.zeros_like(dwa_ref)
            dwi_ref[...] = jnp.zeros_like(dwi_ref)
            dba_ref[...] = jnp.zeros_like(dba_ref)
            dbi_ref[...] = jnp.zeros_like(dbi_ref)
            dlam_ref[...] = jnp.zeros_like(dlam_ref)

        for n in range(LRU_BLOCKS):
            sl = slice(n * LRU_BW, (n + 1) * LRU_BW)
            dwa_ref[n] += lax.dot_general(xb[:, sl], dpre_a[:, sl], (TN, ((), ())), preferred_element_type=f32)
            dwi_ref[n] += lax.dot_general(xb[:, sl], dpre_i[:, sl], (TN, ((), ())), preferred_element_type=f32)
        dba_ref[...] += sums[0]
        dbi_ref[...] += sums[1]
        dlam_ref[...] += sums[2]

    full = lambda a_: pl.BlockSpec(a_.shape, lambda i, nd=a_.ndim: (0,) * nd)
    vec = pl.BlockSpec((1, w), lambda i: (0, 0))
    mat = pl.BlockSpec((LRU_BLOCKS, LRU_BW, LRU_BW), lambda i: (0, 0, 0))
    return pl.pallas_call(
        body, grid=(nblk,), name="rg_gates_bwd",
        in_specs=[cur, cur, nxt, cur, prev, cur] + [full(a_) for a_ in (wa, wi, ba, bi, lam)],
        out_specs=[cur, mat, mat, vec, vec, vec],
        out_shape=[SDS((t, w), f32), SDS((LRU_BLOCKS, LRU_BW, LRU_BW), f32), SDS((LRU_BLOCKS, LRU_BW, LRU_BW), f32),
                   SDS((1, w), f32), SDS((1, w), f32), SDS((1, w), f32)],
        compiler_params=_cparams(("arbitrary",)))(dhs, c, c, hs, hs, xc, wa, wi, ba, bi, lam)


def _rg_conv_bwd(dxc, proj, conv_w, tr=256):
    t, w = dxc.shape
    tr = min(tr, t)
    nblk = t // tr
    cur, _, nxt = _halo_specs(tr, w, 0, nblk)
    xcur, xprev, _ = _halo_specs(tr, w, 1, nblk)

    def body(d_ref, dn_ref, x_ref, xp_ref, cw_ref, dx_ref, dcw_ref, dcb_ref):
        i = pl.program_id(0)
        d = d_ref[...]
        x = x_ref[...]
        after = _taps_after(d, dn_ref[...], i == nblk - 1)
        before = _taps_before(x, xp_ref[...], i == 0) + [x]
        dx = cw_ref[3:4, :] * d
        for s in (1, 2, 3):
            dx = dx + cw_ref[3 - s:4 - s, :] * after[s - 1]
        dx_ref[...] = dx.astype(bf16)
        dcw = jnp.concatenate([jnp.sum(d * before[k], axis=0, keepdims=True) for k in range(4)], axis=0)
        dcb = jnp.sum(d, axis=0, keepdims=True)

        @pl.when(i == 0)
        def _():
            dcw_ref[...] = dcw
            dcb_ref[...] = dcb

        @pl.when(i > 0)
        def _():
            dcw_ref[...] += dcw
            dcb_ref[...] += dcb

    return pl.pallas_call(
        body, grid=(nblk,), name="rg_conv_bwd",
        in_specs=[cur, nxt, xcur, xprev, pl.BlockSpec((4, w), lambda i: (0, 0))],
        out_specs=[cur, pl.BlockSpec((4, w), lambda i: (0, 0)), pl.BlockSpec((1, w), lambda i: (0, 0))],
        out_shape=[SDS((t, w), bf16), SDS((4, w), f32), SDS((1, w), f32)],
        compiler_params=_cparams(("arbitrary",)))(dxc, dxc, proj, proj, conv_w)


def _attn_fwd(h, wts, j):
    proj = _mm_cols("attn_in", h, wts["attn_w_in"], j, bf16)
    kpad = jnp.pad(proj[:, A_W:2 * A_W], ((PAD_A, 0), (0, 0)))
    vpad = jnp.pad(proj[:, 2 * A_W:3 * A_W], ((PAD_A, 0), (0, 0)))
    bias = _bias_window(wts["attn_rel_bias"][j])
    out_a = _chunk_attn_fwd(proj, kpad, vpad, bias)
    out_b, out_b32 = _sb_fwd(proj)
    m = _mm_rows("attn_out", [out_a, out_b], wts["attn_w_out"], j, f32)
    return m, (proj, kpad, vpad, bias, out_a, out_b, out_b32)


def _attn_bwd(dm, h, saved, wts, j, grads):
    proj, kpad, vpad, bias, out_a, out_b, out_b32 = saved
    dout = _mm_rows_t("attn_out_t", dm, wts["attn_w_out"], j, bf16)
    grads["attn_w_out"] = _mm_wgrad("attn_out_wgrad_a", out_a, dm, grads["attn_w_out"], j, 0)
    grads["attn_w_out"] = _mm_wgrad("attn_out_wgrad_b", out_b, dm, grads["attn_w_out"], j, 1)
    dqa, dka, dva, dbias = _chunk_attn_bwd(proj, kpad, vpad, bias, dout)
    dqs, dks, dvs = _sb_bwd(proj, out_b32, dout)
    grads["attn_rel_bias"][j] = _bias_window_grad(dbias)
    dproj = jnp.concatenate([dqa, dka[PAD_A:].astype(bf16), dva[PAD_A:].astype(bf16),
                             dqs, dks.astype(bf16), dvs.astype(bf16)], axis=1)
    grads["attn_w_in"] = _mm_wgrad_cols("attn_in_wgrad", h, dproj, grads["attn_w_in"], j)
    return _mm_cols_t("attn_in_t", dproj, wts["attn_w_in"], j, f32)


def _rg_fwd(h, wts, j):
    proj = _mm_cols("rg_in", h, wts["rg_w_in"], j, f32)
    small = [wts[k][j] for k in ("rg_conv_w", "rg_conv_b", "rg_w_a", "rg_w_i", "rg_b_a", "rg_b_i", "rg_lambda")]
    xc, a, u = _rg_gates_fwd(proj, *small)
    hs = _lru_scan("lru_scan_fwd", a, u, False)
    yp = _rows("rg_gate_out", lambda hv, gv: hv * _gelu(gv), [hs, (proj, D_MODEL, 0)], [], [(D_MODEL, bf16)])[0]
    m = _mm_rows("rg_out", [yp], wts["rg_w_out"], j, f32)
    return m, (proj, xc, a, hs, yp)


def _rg_bwd(dm, h, saved, wts, j, grads):
    proj, xc, a, hs, yp = saved
    dyp = _mm_rows_t("rg_out_t", dm, wts["rg_w_out"], j, f32)
    grads["rg_w_out"] = _mm_wgrad("rg_out_wgrad", yp, dm, grads["rg_w_out"], j)

    def gate_bwd(dy, hv, gv, av):
        dhs = dy * _gelu(gv)
        return dhs, av * dhs, dy * hv * _gelu_grad(gv)

    dhs, ab, dgate = _rows("rg_gate_out_bwd", gate_bwd, [dyp, hs, (proj, D_MODEL, 0), a], [],
                           [(D_MODEL, f32), (D_MODEL, f32), (D_MODEL, bf16)])
    c = _lru_scan("lru_scan_bwd", a, ab, True)
    wa, wi, ba, bi, lam = [wts[k][j] for k in ("rg_w_a", "rg_w_i", "rg_b_a", "rg_b_i", "rg_lambda")]
    dxc, dwa, dwi, dba, dbi, dlam = _rg_gates_bwd(dhs, c, hs, xc, wa, wi, ba, bi, lam)
    dxr, dcw, dcb = _rg_conv_bwd(dxc, proj, wts["rg_conv_w"][j])
    for k, v in (("rg_w_a", dwa), ("rg_w_i", dwi), ("rg_b_a", dba), ("rg_b_i", dbi), ("rg_lambda", dlam),
                 ("rg_conv_w", dcw), ("rg_conv_b", dcb)):
        grads[k][j] = v
    dproj = jnp.concatenate([dgate, dxr], axis=1)
    grads["rg_w_in"] = _mm_wgrad_cols("rg_in_wgrad", h, dproj, grads["rg_w_in"], j)
    return _mm_cols_t("rg_in_t", dproj, wts["rg_w_in"], j, f32)


def _local_step(x, target, wts):
    t = x.shape[0]
    d = D_MODEL
    gains = {k: wts[k] for k in ("norm_mix_pre", "norm_mix_post", "norm_ffn_pre", "norm_ffn_post")}
    gain = lambda k, l: gains[k][l:l + 1]

    saved = []
    h = _rows("norm_in", _norm_fwd, [x], [gain("norm_mix_pre", 0)], [(d, bf16)])[0]
    loss_cols = None
    for l in range(DEPTH):
        j = l // 2
        m, mix_saved = (_attn_fwd if l % 2 == 0 else _rg_fwd)(h, wts, j)

        def resid_next(xv, mv, g_post, g_next):
            x1 = xv + _norm_fwd(mv, g_post)
            return x1, _norm_fwd(x1, g_next)

        x1, h2 = _rows("resid_mix", resid_next, [x, m], [gain("norm_mix_post", l), gain("norm_ffn_pre", l)],
                       [(d, f32), (d, bf16)])
        g, u, hid = _ffn_up(h2, wts["ffn_w_gate"], wts["ffn_w_up"], l)
        f = _ffn_down(hid, wts["ffn_w_down"], l)
        saved.append((x, h, m, mix_saved, x1, h2, g, u, hid, f))
        if l + 1 < DEPTH:
            x, h = _rows("resid_ffn", resid_next, [x1, f], [gain("norm_ffn_post", l), gain("norm_mix_pre", l + 1)],
                         [(d, f32), (d, bf16)])
        else:
            def resid_loss(xv, fv, tv, g_post):
                err = xv + _norm_fwd(fv, g_post) - tv
                return err * (1.0 / d), jnp.sum(err * err, axis=0, keepdims=True)

            dx, loss_cols = _rows("resid_loss", resid_loss, [x1, f, target], [gain("norm_ffn_post", l)],
                                  [(d, f32)], [((1, d), f32)])
    loss = 0.5 * jnp.sum(loss_cols) / d

    grads = {k: {} for k in SMALL_GRADS}
    for k in BIG_GRADS:
        shp = wts[k].shape
        grads[k] = jnp.zeros((shp[0],) + shp[2:] if shp[1] == 1 else shp, f32)

    def norm_bwd_cast(uv, dyv, gv):
        du, dg = _norm_bwd(uv, dyv, gv)
        return du, dg

    def norm_bwd_resid(uv, dhv, dxv, gv):
        du, dg = _norm_bwd(uv, dhv, gv)
        return dxv + du, dg

    for l in reversed(range(DEPTH)):
        j = l // 2
        x_in, h, m, mix_saved, x1, h2, g, u, hid, f = saved[l]
        df, grads["norm_ffn_post"][l] = _rows("norm_ffn_post_bwd", norm_bwd_cast, [f, dx], [gain("norm_ffn_post", l)],
                                              [(d, bf16)], [((1, d), f32)])
        dg, du = _ffn_down_bwd(df, wts["ffn_w_down"], l, g, u)
        grads["ffn_w_down"] = _ffn_wgrad_down(hid, df, grads["ffn_w_down"], l)
        dh2 = _ffn_up_bwd(dg, du, wts["ffn_w_gate"], wts["ffn_w_up"], l)
        grads["ffn_w_gate"] = _ffn_wgrad_up("ffn_wgrad_gate", h2, dg, grads["ffn_w_gate"], l)
        grads["ffn_w_up"] = _ffn_wgrad_up("ffn_wgrad_up", h2, du, grads["ffn_w_up"], l)
        dx1, grads["norm_ffn_pre"][l] = _rows("norm_ffn_pre_bwd", norm_bwd_resid, [x1, dh2, dx],
                                              [gain("norm_ffn_pre", l)], [(d, f32)], [((1, d), f32)])
        dm, grads["norm_mix_post"][l] = _rows("norm_mix_post_bwd", norm_bwd_cast, [m, dx1], [gain("norm_mix_post", l)],
                                              [(d, bf16)], [((1, d), f32)])
        dh = (_attn_bwd if l % 2 == 0 else _rg_bwd)(dm, h, mix_saved, wts, j, grads)
        dx, grads["norm_mix_pre"][l] = _rows("norm_mix_pre_bwd", norm_bwd_resid, [x_in, dh, dx1],
                                             [gain("norm_mix_pre", l)], [(d, f32)], [((1, d), f32)])
    return loss, dx, grads


ANY = pl.BlockSpec(memory_space=pl.ANY)
PACK_COLS = 1024
SMALL_ROWS = 288


def _mesh_pos():
    x, y, c = lax.axis_index("x"), lax.axis_index("y"), lax.axis_index("c")
    return x, y, c, [(1 - x, y), (x, 1 - y), (1 - x, 1 - y)]


def _run_copies(copies):
    for cp in copies:
        cp.start()
    for cp in copies:
        cp.wait()


def _all_gather(shards):
    n = len(shards)

    def body(*refs):
        ins, outs = refs[:n], refs[n:2 * n]
        send, recv, loc = refs[2 * n:]
        x, y, c, chips = _mesh_pos()
        q = 2 * x + y
        copies = []
        for t in range(n):
            dst = outs[t].at[:, q]
            copies.append(pltpu.make_async_copy(ins[t], dst, loc.at[t]))
            for j, (px, py) in enumerate(chips):
                copies.append(pltpu.make_async_remote_copy(
                    src_ref=ins[t], dst_ref=dst, send_sem=send.at[3 * t + j], recv_sem=recv.at[3 * t + j],
                    device_id=(px, py, c), device_id_type=MESH))
        _run_copies(copies)

    return pl.pallas_call(
        body, name="weight_all_gather", in_specs=[ANY] * n, out_specs=[ANY] * n,
        out_shape=[SDS((s.shape[0], N_CHIPS) + s.shape[1:], s.dtype) for s in shards],
        scratch_shapes=[pltpu.SemaphoreType.DMA((3 * n,)), pltpu.SemaphoreType.DMA((3 * n,)),
                        pltpu.SemaphoreType.DMA((n,))])(*shards)


def _pair_exchange(gs):
    n = len(gs)

    def body(*refs):
        ins, outs = refs[:n], refs[n:2 * n]
        send, recv = refs[2 * n:]
        x, y, c, _ = _mesh_pos()
        copies = []
        for t in range(n):
            half = ins[t].shape[2] // 2
            src = ins[t].at[:, :, pl.ds(pl.multiple_of((1 - c) * half, SUBLANES), half)]
            copies.append(pltpu.make_async_remote_copy(
                src_ref=src, dst_ref=outs[t], send_sem=send.at[t], recv_sem=recv.at[t],
                device_id=(x, y, 1 - c), device_id_type=MESH))
        _run_copies(copies)

    return pl.pallas_call(
        body, name="grad_pair_exchange", in_specs=[ANY] * n, out_specs=[ANY] * n,
        out_shape=[SDS(g.shape[:2] + (g.shape[2] // 2, g.shape[3]), f32) for g in gs],
        scratch_shapes=[pltpu.SemaphoreType.DMA((n,)), pltpu.SemaphoreType.DMA((n,))])(*gs)


def _pair_sum(name, g, got, c):
    l, s, r, cols = g.shape

    def body(c_ref, a_ref, b_ref, o_ref):
        o_ref[...] = (a_ref[...] + b_ref[...]).astype(bf16)

    blk = (None, None, r // 2, cols)
    return pl.pallas_call(
        body, name=name, out_shape=SDS(got.shape, bf16),
        grid_spec=pltpu.PrefetchScalarGridSpec(
            num_scalar_prefetch=1, grid=(l, s),
            in_specs=[pl.BlockSpec(blk, lambda i, q, c_ref: (i, q, c_ref[0], 0)),
                      pl.BlockSpec(blk, lambda i, q, c_ref: (i, q, 0, 0))],
            out_specs=pl.BlockSpec(blk, lambda i, q, c_ref: (i, q, 0, 0))),
        compiler_params=_cparams(("parallel", "parallel")))(c, g, got)


def _chip_exchange(hs):
    n = len(hs)

    def body(*refs):
        ins, outs = refs[:n], refs[n:2 * n]
        send, recv, loc = refs[2 * n:]
        x, y, c, chips = _mesh_pos()
        q = 2 * x + y
        copies = []
        for t in range(n):
            copies.append(pltpu.make_async_copy(ins[t].at[:, q], outs[t].at[:, q], loc.at[t]))
            for j, (px, py) in enumerate(chips):
                copies.append(pltpu.make_async_remote_copy(
                    src_ref=ins[t].at[:, 2 * px + py], dst_ref=outs[t].at[:, q], send_sem=send.at[3 * t + j],
                    recv_sem=recv.at[3 * t + j], device_id=(px, py, c), device_id_type=MESH))
        _run_copies(copies)

    return pl.pallas_call(
        body, name="grad_chip_exchange", in_specs=[ANY] * n, out_specs=[ANY] * n,
        out_shape=[SDS(h.shape, h.dtype) for h in hs],
        scratch_shapes=[pltpu.SemaphoreType.DMA((3 * n,)), pltpu.SemaphoreType.DMA((3 * n,)),
                        pltpu.SemaphoreType.DMA((n,))])(*hs)


def _chip_sum(name, s):
    l, _, r, cols = s.shape

    def body(s_ref, o_ref):
        acc = s_ref[0].astype(f32) + s_ref[1].astype(f32)
        o_ref[...] = (acc + s_ref[2].astype(f32)) + s_ref[3].astype(f32)

    return pl.pallas_call(
        body, name=name, grid=(l,),
        in_specs=[pl.BlockSpec((None, N_CHIPS, r, cols), lambda i: (i, 0, 0, 0))],
        out_specs=pl.BlockSpec((None, r, cols), lambda i: (i, 0, 0)),
        out_shape=SDS((l, r, cols), f32), compiler_params=_cparams(("parallel",)))(s)


def _pair_gather(halves):
    n = len(halves)

    def body(*refs):
        ins, outs = refs[:n], refs[n:2 * n]
        send, recv, loc = refs[2 * n:]
        x, y, c, _ = _mesh_pos()
        copies = []
        for t in range(n):
            half = ins[t].shape[1]
            dst = outs[t].at[:, pl.ds(pl.multiple_of(c * half, SUBLANES), half)]
            copies.append(pltpu.make_async_copy(ins[t], dst, loc.at[t]))
            copies.append(pltpu.make_async_remote_copy(
                src_ref=ins[t], dst_ref=dst, send_sem=send.at[t], recv_sem=recv.at[t],
                device_id=(x, y, 1 - c), device_id_type=MESH))
        _run_copies(copies)

    return pl.pallas_call(
        body, name="grad_pair_gather", in_specs=[ANY] * n, out_specs=[ANY] * n,
        out_shape=[SDS((h.shape[0], 2 * h.shape[1], h.shape[2]), f32) for h in halves],
        scratch_shapes=[pltpu.SemaphoreType.DMA((n,)), pltpu.SemaphoreType.DMA((n,)),
                        pltpu.SemaphoreType.DMA((n,))])(*halves)


COL_SHARDED = ("attn_w_in", "rg_w_in", "ffn_w_gate", "ffn_w_up")
ROW_SHARDED = ("attn_w_out", "rg_w_out")
GATES = ("rg_w_a", "rg_w_i")
VECTORS = ("rg_conv_w", "rg_conv_b", "rg_b_a", "rg_b_i", "rg_lambda")
REPLICATED = ("norm_mix_pre", "norm_mix_post", "norm_ffn_pre", "norm_ffn_post", "attn_rel_bias")
BIG_GRADS = COL_SHARDED + ROW_SHARDED + ("ffn_w_down",)
SMALL_GRADS = GATES + VECTORS + REPLICATED
WEIGHTS =("attn_w_in", "attn_rel_bias", "attn_w_out", "rg_w_in", "rg_conv_w", "rg_conv_b", "rg_w_a", "rg_b_a",
           "rg_w_i", "rg_b_i", "rg_lambda", "rg_w_out", "norm_mix_pre", "norm_mix_post", "norm_ffn_pre",
           "norm_ffn_post", "ffn_w_gate", "ffn_w_up", "ffn_w_down")
SMALL = VECTORS + REPLICATED


def _gather_weights(w):
    big = list(COL_SHARDED + ROW_SHARDED + GATES + ("ffn_w_down",))
    shards = []
    for k in big:
        a = w[k].astype(bf16)
        shards.append(a.reshape((-1,) + a.shape[-2:]))
    vec = jnp.concatenate([w[k].reshape(-1) for k in VECTORS])
    shards.append(vec.reshape(1, -1, LANES))
    got = dict(zip(big + ["vec"], _all_gather(shards)))
    out = {k: w[k] for k in REPLICATED}
    for k in COL_SHARDED + ("ffn_w_down",):
        out[k] = got[k]
    for k in ROW_SHARDED:
        l, s, ks, n = got[k].shape
        out[k] = got[k].reshape(l, 1, s * ks, n)
    for k in GATES:
        out[k] = got[k].reshape(2, LRU_BLOCKS, LRU_BW, LRU_BW)
    vec = got["vec"].reshape(N_CHIPS, -1)
    off = 0
    for k in VECTORS:
        shp = w[k].shape
        n = int(np.prod(shp))
        piece = vec[:, off:off + n].reshape((N_CHIPS,) + shp)
        off += n
        if k == "rg_conv_w":
            out[k] = piece.reshape(N_CHIPS, 2, 4, 256).transpose(1, 2, 0, 3).reshape(2, 4, D_MODEL)
        elif k in ("rg_b_a", "rg_b_i"):
            out[k] = piece.transpose(1, 2, 0, 3).reshape(2, 1, D_MODEL)
        else:
            out[k] = piece.transpose(1, 0, 2).reshape(2, 1, D_MODEL)
    return out


def _grad_blocks(name, g):
    st = jnp.stack([g[i] for i in sorted(g)])
    if name in GATES:
        st = st.reshape(2, LRU_BLOCKS, N_CHIPS, LRU_BW // N_CHIPS, LRU_BW).transpose(2, 0, 1, 3, 4)
    elif name == "rg_conv_w":
        st = st.reshape(2, 4, N_CHIPS, -1).transpose(2, 0, 1, 3)
    elif name in ("rg_b_a", "rg_b_i"):
        st = st.reshape(2, LRU_BLOCKS, N_CHIPS, -1).transpose(2, 0, 1, 3)
    elif name in VECTORS:
        st = st.reshape(2, N_CHIPS, -1).transpose(1, 0, 2)
    else:
        st = jnp.broadcast_to(st.reshape(1, -1), (N_CHIPS, st.size))
    return st.reshape(N_CHIPS, -1)


def _reduce_gradients(grads, shard_shapes):
    gs = []
    for k in BIG_GRADS:
        g = grads[k]
        if g.ndim == 3:
            g = g.reshape(g.shape[0], N_CHIPS, g.shape[1] // N_CHIPS, g.shape[2])
        gs.append(g)
    blocks = [_grad_blocks(k, grads[k]) for k in SMALL_GRADS]
    used = sum(b.shape[1] for b in blocks)
    small = jnp.concatenate(blocks + [jnp.zeros((N_CHIPS, SMALL_ROWS * PACK_COLS - used), f32)], axis=1)
    gs.append(small.reshape(1, N_CHIPS, SMALL_ROWS, PACK_COLS))
    names = BIG_GRADS + ("small",)
    c = lax.axis_index("c").astype(jnp.int32).reshape(1)
    parts = [_pair_sum("grad_pair_sum_" + k, g, r, c) for k, g, r in zip(names, gs, _pair_exchange(gs))]
    halves = [_chip_sum("grad_chip_sum_" + k, s) for k, s in zip(names, _chip_exchange(parts))]
    full = _pair_gather(halves)
    out = {k: f.reshape(shard_shapes[k]) for k, f in zip(BIG_GRADS, full)}
    flat, off = full[-1].reshape(-1), 0
    for k in SMALL_GRADS:
        n = int(np.prod(shard_shapes[k]))
        out[k] = flat[off:off + n].reshape(shard_shapes[k])
        off += n
    return out


def _adamw_fn(w, g, m, v):
    m = ADAM_B1 * m + (1.0 - ADAM_B1) * g
    v = ADAM_B2 * v + (1.0 - ADAM_B2) * (g * g)
    m_hat = m / (1.0 - ADAM_B1 ** ADAM_STEP)
    v_hat = v / (1.0 - ADAM_B2 ** ADAM_STEP)
    return -ADAM_LR * (m_hat / (jnp.sqrt(v_hat) + ADAM_EPS) + ADAM_WD * w), m, v


def _adamw(name, w, g, m, v):
    shp = w.shape
    if w.size >= 1 << 16:
        width = shp[-1]
        ops = [a.reshape(-1, width) for a in (w, g, m, v)]
        res = _rows(name, _adamw_fn, ops, [], [(width, f32)] * 3)
        return [r.reshape(shp) for r in res]
    n = w.size
    rows = -(-n // (SUBLANES * LANES)) * SUBLANES
    ops = [jnp.pad(a.reshape(-1), (0, rows * LANES - n)).reshape(rows, LANES) for a in (w, g, m, v)]
    res = _rows(name, _adamw_fn, ops, [], [(LANES, f32)] * 3, tr=rows)
    return [r.reshape(-1)[:n].reshape(shp) for r in res]


def kernel(x, attn_w_in, attn_rel_bias, attn_w_out, rg_w_in, rg_conv_w, rg_conv_b, rg_w_a, rg_b_a, rg_w_i, rg_b_i, rg_lambda, rg_w_out, norm_mix_pre, norm_mix_post, norm_ffn_pre, norm_ffn_post, ffn_w_gate, ffn_w_up, ffn_w_down, loss_target, m_attn_w_in, m_attn_rel_bias, m_attn_w_out, m_rg_w_in, m_rg_conv_w, m_rg_conv_b, m_rg_w_a, m_rg_b_a, m_rg_w_i, m_rg_b_i, m_rg_lambda, m_rg_w_out, m_norm_mix_pre, m_norm_mix_post, m_norm_ffn_pre, m_norm_ffn_post, m_ffn_w_gate, m_ffn_w_up, m_ffn_w_down, v_attn_w_in, v_attn_rel_bias, v_attn_w_out, v_rg_w_in, v_rg_conv_w, v_rg_conv_b, v_rg_w_a, v_rg_b_a, v_rg_w_i, v_rg_b_i, v_rg_lambda, v_rg_w_out, v_norm_mix_pre, v_norm_mix_post, v_norm_ffn_pre, v_norm_ffn_post, v_ffn_w_gate, v_ffn_w_up, v_ffn_w_down):
    w = dict(zip(WEIGHTS, (attn_w_in, attn_rel_bias, attn_w_out, rg_w_in, rg_conv_w, rg_conv_b, rg_w_a, rg_b_a, rg_w_i,
                           rg_b_i, rg_lambda, rg_w_out, norm_mix_pre, norm_mix_post, norm_ffn_pre, norm_ffn_post,
                           ffn_w_gate, ffn_w_up, ffn_w_down)))
    m = dict(zip(WEIGHTS, (m_attn_w_in, m_attn_rel_bias, m_attn_w_out, m_rg_w_in, m_rg_conv_w, m_rg_conv_b, m_rg_w_a,
                           m_rg_b_a, m_rg_w_i, m_rg_b_i, m_rg_lambda, m_rg_w_out, m_norm_mix_pre, m_norm_mix_post,
                           m_norm_ffn_pre, m_norm_ffn_post, m_ffn_w_gate, m_ffn_w_up, m_ffn_w_down)))
    v = dict(zip(WEIGHTS, (v_attn_w_in, v_attn_rel_bias, v_attn_w_out, v_rg_w_in, v_rg_conv_w, v_rg_conv_b, v_rg_w_a,
                           v_rg_b_a, v_rg_w_i, v_rg_b_i, v_rg_lambda, v_rg_w_out, v_norm_mix_pre, v_norm_mix_post,
                           v_norm_ffn_pre, v_norm_ffn_post, v_ffn_w_gate, v_ffn_w_up, v_ffn_w_down)))
    wts = _gather_weights(w)
    loss, dx, grads = _local_step(x[0], loss_target[0], wts)
    loss = lax.psum(loss, ("x", "y", "c"))
    g = _reduce_gradients(grads, {k: w[k].shape for k in WEIGHTS})

    big = [k for k in WEIGHTS if k not in SMALL]
    upd = {k: _adamw("adamw_" + k, w[k], g[k], m[k], v[k]) for k in big}
    cat = lambda d: jnp.concatenate([d[k].reshape(-1) for k in SMALL])
    small = _adamw("adamw_small", cat(w), cat(g), cat(m), cat(v))
    off = 0
    for k in SMALL:
        n = w[k].size
        upd[k] = [r[off:off + n].reshape(w[k].shape) for r in small]
        off += n
    return (loss, dx[None], *[g[k] for k in WEIGHTS], *[upd[k][0] for k in WEIGHTS],
            *[upd[k][1] for k in WEIGHTS], *[upd[k][2] for k in WEIGHTS])
```

```python
import functools

import numpy as np
import jax
import jax.numpy as jnp
from jax import lax
from jax.experimental import pallas as pl
from jax.experimental.pallas import tpu as pltpu

f32 = jnp.float32
bf16 = jnp.bfloat16
SDS = jax.ShapeDtypeStruct
MESH = pl.DeviceIdType.MESH

D_MODEL = 1024
N_CHIPS = 4
DEPTH = 4
HEAD_DIM = 64
CHUNK = 64
N_LEFT = 8
REL_CLIP = 256
A_W = 512
LRU_BLOCKS = 4
LRU_BW = 256
LRU_C = 8.0
D_FF = 2816
RMS_EPS = 1e-6
LANES = 128
SUBLANES = 8
VMEM_LIMIT = 56 * 1024 * 1024

QB_A = 2 * CHUNK
KW_A = QB_A + N_LEFT * CHUNK
PAD_A = N_LEFT * CHUNK
EXT_A = 768
SB_BLK = 256
SB_DEAD = -110.0

ADAM_LR, ADAM_B1, ADAM_B2, ADAM_EPS, ADAM_WD, ADAM_STEP = 0.001, 0.9, 0.999, 1e-08, 0.01, 10


def _cparams(sem):
    return pltpu.CompilerParams(dimension_semantics=sem, vmem_limit_bytes=VMEM_LIMIT)


def _gemm(name, operands, in_specs, o_spec, out_shape, grid, dims, acc_shape, into=None):
    nred = grid[2]
    npair = len(operands) // 2
    nin = 2 * npair + (into is not None)

    def body(*refs):
        o_ref = refs[nin]
        p = None
        for t in range(npair):
            d = lax.dot_general(refs[2 * t][...], refs[2 * t + 1][...], (dims, ((), ())),
                                preferred_element_type=f32)
            p = d if p is None else p + d
        if nred == 1:
            o_ref[...] = p.astype(o_ref.dtype)
        else:
            acc = refs[nin + 1]
            r = pl.program_id(2)

            @pl.when(r == 0)
            def _():
                acc[...] = p

            @pl.when(r > 0)
            def _():
                acc[...] += p

            @pl.when(r == nred - 1)
            def _():
                o_ref[...] = acc[...].astype(o_ref.dtype)

    scratch = [] if nred == 1 else [pltpu.VMEM(acc_shape, f32)]
    extra, alias = ([], {}) if into is None else ([into], {2 * npair: 0})
    return pl.pallas_call(
        body, grid=grid, in_specs=list(in_specs) + [pl.BlockSpec(memory_space=pl.ANY)] * len(extra),
        out_specs=o_spec, out_shape=out_shape, scratch_shapes=scratch, name=name, input_output_aliases=alias,
        compiler_params=_cparams(("parallel", "parallel", "arbitrary")))(*operands, *extra)


NN = ((1,), (0,))
NT = ((1,), (1,))
TN = ((0,), (0,))


def _tile(t, want=512):
    return min(want, t)


def _mm_cols(name, a, w, l, out_dtype):
    t, k = a.shape
    _, s, _, ns = w.shape
    tm = _tile(t)
    return _gemm(
        name, [a, w],
        [pl.BlockSpec((tm, k), lambda i, j, r: (i, 0)),
         pl.BlockSpec((None, None, k, ns), lambda i, j, r: (l, j, 0, 0))],
        pl.BlockSpec((tm, ns), lambda i, j, r: (i, j)),
        SDS((t, s * ns), out_dtype), (t // tm, s, 1), NN, None)


def _mm_cols_t(name, dy, w, l, out_dtype):
    t = dy.shape[0]
    _, s, k, ns = w.shape
    tm = _tile(t)
    return _gemm(
        name, [dy, w],
        [pl.BlockSpec((tm, ns), lambda i, j, r: (i, r)),
         pl.BlockSpec((None, None, k, ns), lambda i, j, r: (l, r, 0, 0))],
        pl.BlockSpec((tm, k), lambda i, j, r: (i, 0)),
        SDS((t, k), out_dtype), (t // tm, 1, s), NT, (tm, k))


def _mm_wgrad_cols(name, a, dy, buf, l):
    t, k = a.shape
    _, s, _, ns = buf.shape
    tt = _tile(t)
    return _gemm(
        name, [a, dy],
        [pl.BlockSpec((tt, k), lambda i, j, r: (r, 0)),
         pl.BlockSpec((tt, ns), lambda i, j, r: (r, i))],
        pl.BlockSpec((None, None, k, ns), lambda i, j, r: (l, i, 0, 0)),
        SDS(buf.shape, f32), (s, 1, t // tt), TN, (k, ns), into=buf)


def _mm_rows(name, parts, w, l, out_dtype):
    t = parts[0].shape[0]
    n = w.shape[3]
    tm = _tile(t)
    ops, specs = [], []
    for p_i, a in enumerate(parts):
        kp = a.shape[1]
        ops += [a, w]
        specs += [pl.BlockSpec((tm, kp), lambda i, j, r: (i, 0)),
                  pl.BlockSpec((None, None, kp, n), lambda i, j, r, p_i=p_i: (l, 0, p_i, 0))]
    return _gemm(name, ops, specs, pl.BlockSpec((tm, n), lambda i, j, r: (i, 0)),
                 SDS((t, n), out_dtype), (t // tm, 1, 1), NN, None)


def _mm_rows_t(name, dy, w, l, out_dtype):
    t, n = dy.shape
    k = w.shape[2]
    tm = _tile(t)
    return _gemm(
        name, [dy, w],
        [pl.BlockSpec((tm, n), lambda i, j, r: (i, 0)),
         pl.BlockSpec((None, None, k, n), lambda i, j, r: (l, 0, 0, 0))],
        pl.BlockSpec((tm, k), lambda i, j, r: (i, 0)),
        SDS((t, k), out_dtype), (t // tm, 1, 1), NT, None)


def _mm_wgrad(name, a, dy, buf, l, part=0):
    t, k = a.shape
    n = dy.shape[1]
    tt = _tile(t)
    return _gemm(
        name, [a, dy],
        [pl.BlockSpec((tt, k), lambda i, j, r: (r, 0)),
         pl.BlockSpec((tt, n), lambda i, j, r: (r, 0))],
        pl.BlockSpec((None, k, n), lambda i, j, r: (l, part, 0)),
        SDS(buf.shape, f32), (1, 1, t // tt), TN, (k, n), into=buf)


def _ffn_up(h, wg, wu, l):
    t, k = h.shape
    s, fs = wg.shape[1], wg.shape[3]
    tm = _tile(t)

    def body(h_ref, wg_ref, wu_ref, g_ref, u_ref, hid_ref):
        hv = h_ref[...]
        g = jnp.dot(hv, wg_ref[...], preferred_element_type=f32)
        u = jnp.dot(hv, wu_ref[...], preferred_element_type=f32)
        g_ref[...] = g.astype(bf16)
        u_ref[...] = u.astype(bf16)
        hid_ref[...] = (g * jax.nn.sigmoid(g) * u).astype(bf16)

    wspec = pl.BlockSpec((None, None, k, fs), lambda j, i: (l, j, 0, 0))
    ospec = pl.BlockSpec((None, tm, fs), lambda j, i: (j, i, 0))
    return pl.pallas_call(
        body, grid=(s, t // tm), name="ffn_up",
        in_specs=[pl.BlockSpec((tm, k), lambda j, i: (i, 0)), wspec, wspec],
        out_specs=[ospec, ospec, ospec], out_shape=[SDS((s, t, fs), bf16)] * 3,
        compiler_params=_cparams(("parallel", "parallel")))(h, wg, wu)


def _ffn_down(hid, wd, l):
    s, t, fs = hid.shape
    n = wd.shape[3]
    tm = _tile(t)
    return _gemm(
        "ffn_down", [hid, wd],
        [pl.BlockSpec((None, tm, fs), lambda i, j, r: (r, i, 0)),
         pl.BlockSpec((None, None, fs, n), lambda i, j, r: (l, r, 0, 0))],
        pl.BlockSpec((tm, n), lambda i, j, r: (i, 0)),
        SDS((t, n), f32), (t // tm, 1, s), NN, (tm, n))


def _ffn_down_bwd(df, wd, l, g, u):
    t, n = df.shape
    s, fs = wd.shape[1], wd.shape[2]
    tm = _tile(t)

    def body(df_ref, wd_ref, g_ref, u_ref, dg_ref, du_ref):
        dh = lax.dot_general(df_ref[...], wd_ref[...], (NT, ((), ())), preferred_element_type=f32)
        gv = g_ref[...].astype(f32)
        uv = u_ref[...].astype(f32)
        sg = jax.nn.sigmoid(gv)
        du_ref[...] = (dh * gv * sg).astype(bf16)
        dg_ref[...] = (dh * uv * (sg * (1.0 + gv * (1.0 - sg)))).astype(bf16)

    bspec = pl.BlockSpec((None, tm, fs), lambda j, i: (j, i, 0))
    return pl.pallas_call(
        body, grid=(s, t // tm), name="ffn_down_bwd",
        in_specs=[pl.BlockSpec((tm, n), lambda j, i: (i, 0)),
                  pl.BlockSpec((None, None, fs, n), lambda j, i: (l, j, 0, 0)), bspec, bspec],
        out_specs=[bspec, bspec], out_shape=[SDS((s, t, fs), bf16)] * 2,
        compiler_params=_cparams(("parallel", "parallel")))(df, wd, g, u)


def _ffn_up_bwd(dg, du, wg, wu, l):
    s, t, fs = dg.shape
    k = wg.shape[2]
    tm = _tile(t)
    aspec = pl.BlockSpec((None, tm, fs), lambda i, j, r: (r, i, 0))
    wspec = pl.BlockSpec((None, None, k, fs), lambda i, j, r: (l, r, 0, 0))
    return _gemm("ffn_up_bwd", [dg, wg, du, wu], [aspec, wspec, aspec, wspec],
                 pl.BlockSpec((tm, k), lambda i, j, r: (i, 0)),
                 SDS((t, k), f32), (t // tm, 1, s), NT, (tm, k))


def _ffn_wgrad_up(name, h, dy, buf, l):
    t, k = h.shape
    s, _, fs = dy.shape
    tt = _tile(t)
    return _gemm(
        name, [h, dy],
        [pl.BlockSpec((tt, k), lambda i, j, r: (r, 0)),
         pl.BlockSpec((None, tt, fs), lambda i, j, r: (i, r, 0))],
        pl.BlockSpec((None, None, k, fs), lambda i, j, r: (l, i, 0, 0)),
        SDS(buf.shape, f32), (s, 1, t // tt), TN, (k, fs), into=buf)


def _ffn_wgrad_down(hid, df, buf, l):
    s, t, fs = hid.shape
    n = df.shape[1]
    tt = _tile(t)
    return _gemm(
        "ffn_wgrad_down", [hid, df],
        [pl.BlockSpec((None, tt, fs), lambda i, j, r: (i, r, 0)),
         pl.BlockSpec((tt, n), lambda i, j, r: (r, 0))],
        pl.BlockSpec((None, None, fs, n), lambda i, j, r: (l, i, 0, 0)),
        SDS(buf.shape, f32), (s, 1, t // tt), TN, (fs, n), into=buf)


def _rows(name, fn, rows, consts, row_outs, acc_outs=(), tr=256):
    rows = [r if isinstance(r, tuple) else (r, r.shape[1], 0) for r in rows]
    t = rows[0][0].shape[0]
    tr = min(tr, t)
    nin = len(rows) + len(consts)
    no, na = len(row_outs), len(acc_outs)

    def body(*refs):
        vals = fn(*[r[...] for r in refs[:nin]])
        if not isinstance(vals, (tuple, list)):
            vals = (vals,)
        for k in range(no):
            refs[nin + k][...] = vals[k].astype(refs[nin + k].dtype)
        first = pl.program_id(0) == 0
        for k in range(na):
            ref, val = refs[nin + no + k], vals[no + k]

            @pl.when(first)
            def _(ref=ref, val=val):
                ref[...] = val

            @pl.when(jnp.logical_not(first))
            def _(ref=ref, val=val):
                ref[...] += val

    in_specs = [pl.BlockSpec((tr, w), lambda i, cb=cb: (i, cb)) for (_, w, cb) in rows]
    in_specs += [pl.BlockSpec(c.shape, lambda i, nd=c.ndim: (0,) * nd) for c in consts]
    out_specs = [pl.BlockSpec((tr, w), lambda i: (i, 0)) for (w, _) in row_outs]
    out_specs += [pl.BlockSpec(s, lambda i, nd=len(s): (0,) * nd) for (s, _) in acc_outs]
    out_shape = [SDS((t, w), dt) for (w, dt) in row_outs] + [SDS(s, dt) for (s, dt) in acc_outs]
    res = pl.pallas_call(
        body, grid=(t // tr,), in_specs=in_specs, out_specs=out_specs, out_shape=out_shape,
        name=name, compiler_params=_cparams(("arbitrary",)))(*[r[0] for r in rows], *consts)
    return res


def _rstd(x):
    return lax.rsqrt(jnp.mean(x * x, axis=-1, keepdims=True) + RMS_EPS)


def _norm_fwd(x, g):
    return x * _rstd(x) * g


def _norm_bwd(u, dy, g):
    r = _rstd(u)
    n = u * r
    dn = dy * g
    du = r * (dn - n * jnp.mean(dn * n, axis=-1, keepdims=True))
    return du, jnp.sum(dy * n, axis=0, keepdims=True)


def _gelu(x):
    c = 0.7978845608028654
    return 0.5 * x * (1.0 + jnp.tanh(c * (x + 0.044715 * x * x * x)))


def _gelu_grad(x):
    c = 0.7978845608028654
    th = jnp.tanh(c * (x + 0.044715 * x * x * x))
    return 0.5 * (1.0 + th) + 0.5 * x * (1.0 - th * th) * c * (1.0 + 3.0 * 0.044715 * x * x)


def _chunk_valid(start):
    qi = lax.broadcasted_iota(jnp.int32, (QB_A, KW_A), 0)
    kj = lax.broadcasted_iota(jnp.int32, (QB_A, KW_A), 1)
    qc = qi // CHUNK
    kc = kj // CHUNK
    return (kc >= qc) & (kc <= qc + N_LEFT) & (kj + start >= PAD_A)


def _chunk_probs(q, k, bias, valid):
    s = lax.dot_general(q, k, (NT, ((), ())), preferred_element_type=f32) * (HEAD_DIM ** -0.5) + bias
    s = jnp.where(valid, s, -1e30)
    p = jnp.exp(s - jnp.max(s, axis=-1, keepdims=True))
    return p / jnp.sum(p, axis=-1, keepdims=True)


def _chunk_attn_fwd(proj, kpad, vpad, bias):
    t = proj.shape[0]
    tp = kpad.shape[0]

    def body(q_ref, k_ref, v_ref, b_ref, o_ref):
        start = pl.multiple_of(pl.program_id(1) * QB_A, QB_A)
        valid = _chunk_valid(start)
        for h in range(2):
            cols = pl.ds(h * HEAD_DIM, HEAD_DIM)
            k = k_ref[pl.ds(start, KW_A), cols]
            v = v_ref[pl.ds(start, KW_A), cols]
            p = _chunk_probs(q_ref[:, cols], k, b_ref[h], valid)
            o_ref[:, cols] = jnp.dot(p.astype(bf16), v, preferred_element_type=f32).astype(bf16)

    kv_spec = pl.BlockSpec((tp, LANES), lambda hp, qb: (0, hp))
    return pl.pallas_call(
        body, grid=(A_W // LANES, t // QB_A), name="chunk_attn_fwd",
        in_specs=[pl.BlockSpec((QB_A, LANES), lambda hp, qb: (qb, hp)), kv_spec, kv_spec,
                  pl.BlockSpec((2, QB_A, KW_A), lambda hp, qb: (hp, 0, 0))],
        out_specs=pl.BlockSpec((QB_A, LANES), lambda hp, qb: (qb, hp)),
        out_shape=SDS((t, A_W), bf16),
        compiler_params=_cparams(("parallel", "arbitrary")))(proj, kpad, vpad, bias)


def _chunk_attn_bwd(proj, kpad, vpad, bias, dout):
    t = proj.shape[0]
    tp = kpad.shape[0]

    def body(q_ref, k_ref, v_ref, b_ref, do_ref, dq_ref, dk_ref, dv_ref, db_ref):
        qb = pl.program_id(1)
        start = pl.multiple_of(qb * QB_A, QB_A)
        valid = _chunk_valid(start)

        @pl.when(qb == 0)
        def _():
            dk_ref[...] = jnp.zeros_like(dk_ref)
            dv_ref[...] = jnp.zeros_like(dv_ref)
            db_ref[...] = jnp.zeros_like(db_ref)

        for h in range(2):
            cols = pl.ds(h * HEAD_DIM, HEAD_DIM)
            win = pl.ds(start, KW_A)
            q = q_ref[:, cols]
            k = k_ref[win, cols]
            v = v_ref[win, cols]
            do = do_ref[:, cols]
            p = _chunk_probs(q, k, b_ref[h], valid)
            dp = lax.dot_general(do, v, (NT, ((), ())), preferred_element_type=f32)
            ds = p * (dp - jnp.sum(dp * p, axis=-1, keepdims=True))
            db_ref[h] += ds
            dsb = (ds * (HEAD_DIM ** -0.5)).astype(bf16)
            dq_ref[:, cols] = jnp.dot(dsb, k, preferred_element_type=f32).astype(bf16)
            dk_ref[win, cols] += lax.dot_general(dsb, q, (TN, ((), ())), preferred_element_type=f32)
            dv_ref[win, cols] += lax.dot_general(p.astype(bf16), do, (TN, ((), ())), preferred_element_type=f32)

    kv_spec = pl.BlockSpec((tp, LANES), lambda hp, qb: (0, hp))
    q_spec = pl.BlockSpec((QB_A, LANES), lambda hp, qb: (qb, hp))
    b_spec = pl.BlockSpec((2, QB_A, KW_A), lambda hp, qb: (hp, 0, 0))
    return pl.pallas_call(
        body, grid=(A_W // LANES, t // QB_A), name="chunk_attn_bwd",
        in_specs=[q_spec, kv_spec, kv_spec, b_spec, q_spec],
        out_specs=[q_spec, kv_spec, kv_spec, b_spec],
        out_shape=[SDS((t, A_W), bf16), SDS((tp, A_W), f32), SDS((tp, A_W), f32),
                   SDS((2 * A_W // LANES, QB_A, KW_A), f32)],
        compiler_params=_cparams(("parallel", "arbitrary")))(proj, kpad, vpad, bias, dout)


def _bias_ext(table):
    flat = PAD_A + QB_A - 1 - REL_CLIP
    top = jnp.broadcast_to(table[:, 2 * REL_CLIP:], (table.shape[0], flat))
    lo = 2 * REL_CLIP - (EXT_A - 1 - flat)
    return jnp.concatenate([top, jnp.flip(table[:, lo:], axis=1)], axis=1)


def _bias_window(table):
    nh = table.shape[0]
    e = jnp.broadcast_to(_bias_ext(table)[:, None, :], (nh, QB_A, EXT_A)).reshape(nh, QB_A * EXT_A)
    e = jnp.pad(e, ((0, 0), (0, QB_A)))
    m = e.reshape(nh, QB_A, EXT_A + 1)
    return jnp.flip(m, axis=1)[:, :, :KW_A]


def _bias_window_grad(dbias):
    nh = dbias.shape[0]
    d = jnp.flip(dbias, axis=1)
    d = jnp.pad(d, ((0, 0), (0, 0), (0, EXT_A + 1 - KW_A)))
    m = d.reshape(nh, QB_A * (EXT_A + 1))[:, :QB_A * EXT_A].reshape(nh, QB_A, EXT_A)
    dext = jnp.sum(m, axis=1)
    flat = PAD_A + QB_A - 1 - REL_CLIP
    lo = 2 * REL_CLIP - (EXT_A - 1 - flat)
    tail = jnp.flip(dext[:, flat:], axis=1)
    tail = tail.at[:, -1].add(jnp.sum(dext[:, :flat], axis=1))
    return jnp.pad(tail, ((0, 0), (lo, 0)))


def _tri_suffix(x, tri):
    hi = x.astype(bf16)
    lo = (x - hi.astype(f32)).astype(bf16)
    return jnp.dot(hi, tri, preferred_element_type=f32) + jnp.dot(lo, tri, preferred_element_type=f32)


def _sb_block(q, k, run, tri, causal):
    z = lax.dot_general(q, k, (NT, ((), ())), preferred_element_type=f32) * (HEAD_DIM ** -0.5)
    e = jnp.exp(-jnp.abs(z))
    l1p = jnp.log(1.0 + e)
    lb = jnp.minimum(z, 0.0) - l1p
    lmb = lb - z
    if causal is not None:
        lmb = jnp.where(causal, lmb, 0.0)
    cs = _tri_suffix(lmb, tri)
    w = jnp.exp(lb + (run + cs - lmb))
    if causal is not None:
        w = jnp.where(causal, w, 0.0)
    return z, e, w, run + cs[:, 0:1]


def _sb_tri():
    r = lax.broadcasted_iota(jnp.int32, (SB_BLK, SB_BLK), 0)
    c = lax.broadcasted_iota(jnp.int32, (SB_BLK, SB_BLK), 1)
    return (r >= c).astype(bf16), c < r


def _sb_live(runs):
    m = runs[0]
    for r in runs[1:]:
        m = jnp.maximum(m, r)
    return jnp.max(m) > SB_DEAD


def _sb_fwd(proj):
    t = proj.shape[0]
    cb = A_W // LANES
    heads = [pl.ds(h * HEAD_DIM, HEAD_DIM) for h in range(LANES // HEAD_DIM)]

    def body(q_ref, k_ref, v_ref, o_ref, of_ref):
        qb = pl.program_id(1)
        tri, diag = _sb_tri()
        qs = [q_ref[:, cols] for cols in heads]

        def pair(kb, carry, causal):
            rows = pl.ds(pl.multiple_of(kb * SB_BLK, SB_BLK), SB_BLK)
            out = []
            for cols, q, (run, acc) in zip(heads, qs, carry):
                _, _, w, run = _sb_block(q, k_ref[rows, cols], run, tri, causal)
                out.append((run, acc + jnp.dot(w.astype(bf16), v_ref[rows, cols], preferred_element_type=f32)))
            return tuple(out)

        init = tuple((jnp.zeros((SB_BLK, 1), f32), jnp.zeros((SB_BLK, HEAD_DIM), f32)) for _ in heads)
        carry = pair(qb, init, diag)

        def cond(st):
            return (st[0] < qb) & _sb_live([c[0] for c in st[1]])

        def step(st):
            return st[0] + 1, pair(qb - 1 - st[0], st[1], None)

        _, carry = lax.while_loop(cond, step, (jnp.int32(0), carry))
        for cols, (_, acc) in zip(heads, carry):
            o_ref[:, cols] = acc.astype(bf16)
            of_ref[:, cols] = acc

    ospec = pl.BlockSpec((SB_BLK, LANES), lambda hp, qb: (qb, hp))
    return pl.pallas_call(
        body, grid=(cb, t // SB_BLK), name="sb_attn_fwd",
        in_specs=[pl.BlockSpec((SB_BLK, LANES), lambda hp, qb: (qb, 3 * cb + hp)),
                  pl.BlockSpec((t, LANES), lambda hp, qb: (0, 4 * cb + hp)),
                  pl.BlockSpec((t, LANES), lambda hp, qb: (0, 5 * cb + hp))],
        out_specs=[ospec, ospec], out_shape=[SDS((t, A_W), bf16), SDS((t, A_W), f32)],
        compiler_params=_cparams(("parallel", "arbitrary")))(proj, proj, proj)


def _sb_bwd(proj, out_b, dout):
    t = proj.shape[0]
    cb = A_W // LANES
    heads = [pl.ds(h * HEAD_DIM, HEAD_DIM) for h in range(LANES // HEAD_DIM)]

    def body(q_ref, k_ref, v_ref, o_ref, do_ref, dq_ref, dk_ref, dv_ref):
        qb = pl.program_id(1)
        tri, diag = _sb_tri()

        @pl.when(qb == 0)
        def _():
            dk_ref[...] = jnp.zeros_like(dk_ref)
            dv_ref[...] = jnp.zeros_like(dv_ref)

        qs = [q_ref[:, cols] for cols in heads]
        dos = [do_ref[:, cols] for cols in heads]
        dsums = [jnp.sum(do.astype(f32) * o_ref[:, cols], axis=-1, keepdims=True) for cols, do in zip(heads, dos)]

        def pair(kb, carry, causal):
            rows = pl.ds(pl.multiple_of(kb * SB_BLK, SB_BLK), SB_BLK)
            out = []
            for cols, q, do, dsum, (run, gsum, dq) in zip(heads, qs, dos, dsums, carry):
                k = k_ref[rows, cols]
                v = v_ref[rows, cols]
                z, e, w, run = _sb_block(q, k, run, tri, causal)
                inv = 1.0 / (1.0 + e)
                beta = jnp.where(z >= 0.0, inv, e * inv)
                wb = w.astype(bf16)
                g = lax.dot_general(do, v, (NT, ((), ())), preferred_element_type=f32) * wb.astype(f32)
                sg = _tri_suffix(g, tri)
                dz = g * (1.0 - beta) - (dsum - gsum - sg) * beta
                if causal is not None:
                    dz = jnp.where(causal, dz, 0.0)
                dzb = (dz * (HEAD_DIM ** -0.5)).astype(bf16)
                dk_ref[rows, cols] += lax.dot_general(dzb, q, (TN, ((), ())), preferred_element_type=f32)
                dv_ref[rows, cols] += lax.dot_general(wb, do, (TN, ((), ())), preferred_element_type=f32)
                out.append((run, gsum + sg[:, 0:1], dq + jnp.dot(dzb, k, preferred_element_type=f32)))
            return tuple(out)

        zero = jnp.zeros((SB_BLK, 1), f32)
        carry = pair(qb, tuple((zero, zero, jnp.zeros((SB_BLK, HEAD_DIM), f32)) for _ in heads), diag)

        def cond(st):
            return (st[0] < qb) & _sb_live([c[0] for c in st[1]])

        def step(st):
            return st[0] + 1, pair(qb - 1 - st[0], st[1], None)

        _, carry = lax.while_loop(cond, step, (jnp.int32(0), carry))
        for cols, c in zip(heads, carry):
            dq_ref[:, cols] = c[2].astype(bf16)

    kv_in = lambda seg: pl.BlockSpec((t, LANES), lambda hp, qb: (0, seg * cb + hp))
    q_spec = pl.BlockSpec((SB_BLK, LANES), lambda hp, qb: (qb, hp))
    kv_out = pl.BlockSpec((t, LANES), lambda hp, qb: (0, hp))
    return pl.pallas_call(
        body, grid=(cb, t // SB_BLK), name="sb_attn_bwd",
        in_specs=[pl.BlockSpec((SB_BLK, LANES), lambda hp, qb: (qb, 3 * cb + hp)), kv_in(4), kv_in(5),
                  q_spec, pl.BlockSpec((SB_BLK, LANES), lambda hp, qb: (qb, cb + hp))],
        out_specs=[q_spec, kv_out, kv_out],
        out_shape=[SDS((t, A_W), bf16), SDS((t, A_W), f32), SDS((t, A_W), f32)],
        compiler_params=_cparams(("parallel", "arbitrary")))(proj, proj, proj, out_b, dout)


def _halo_specs(tr, w, col, nblk):
    per = tr // SUBLANES
    cur = pl.BlockSpec((tr, w), lambda i: (i, col))
    prev = pl.BlockSpec((SUBLANES, w), lambda i: (jnp.maximum(i * per - 1, 0), col))
    nxt = pl.BlockSpec((SUBLANES, w), lambda i: (jnp.minimum((i + 1) * per, nblk * per - 1), col))
    return cur, prev, nxt


def _taps_before(cur, prev8, first):
    prev8 = jnp.where(first, 0.0, prev8)
    ext = jnp.concatenate([prev8, cur], axis=0)
    return [pltpu.roll(ext, s, 0)[SUBLANES:] for s in (3, 2, 1)]


def _taps_after(cur, next8, last):
    n = cur.shape[0]
    next8 = jnp.where(last, 0.0, next8)
    ext = jnp.concatenate([cur, next8], axis=0)
    return [pltpu.roll(ext, n + SUBLANES - s, 0)[:n] for s in (1, 2, 3)]


def _block_diag(x, w_ref, dims):
    outs = [lax.dot_general(x[:, n * LRU_BW:(n + 1) * LRU_BW], w_ref[n], (dims, ((), ())),
                            preferred_element_type=f32) for n in range(LRU_BLOCKS)]
    return jnp.concatenate(outs, axis=1)


def _lru_gates(xc, wa_ref, wi_ref, ba, bi, lam):
    xb = xc.astype(bf16)
    r = jax.nn.sigmoid(_block_diag(xb, wa_ref, NN) + ba)
    ig = jax.nn.sigmoid(_block_diag(xb, wi_ref, NN) + bi)
    sp = jnp.maximum(-lam, 0.0) + jnp.log(1.0 + jnp.exp(-jnp.abs(lam)))
    log_a = -LRU_C * r * sp
    a = jnp.exp(log_a)
    x2 = 2.0 * log_a
    one_minus = jnp.where(x2 > -1e-2, -x2 * (1.0 + x2 * (0.5 + x2 * (1.0 / 6.0))), 1.0 - a * a)
    mult = jnp.sqrt(one_minus)
    return xb, r, ig, sp, a, mult


def _rg_gates_fwd(proj, conv_w, conv_b, wa, wi, ba, bi, lam, tr=256):
    t = proj.shape[0]
    w = D_MODEL
    tr = min(tr, t)
    nblk = t // tr
    cur, prev, _ = _halo_specs(tr, w, 1, nblk)

    def body(x_ref, xp_ref, cw_ref, cb_ref, wa_ref, wi_ref, ba_ref, bi_ref, lam_ref, xc_ref, a_ref, u_ref):
        x = x_ref[...]
        taps = _taps_before(x, xp_ref[...], pl.program_id(0) == 0) + [x]
        xc = cb_ref[...]
        for k in range(4):
            xc = xc + cw_ref[k:k + 1, :] * taps[k]
        _, _, ig, _, a, mult = _lru_gates(xc, wa_ref, wi_ref, ba_ref[...], bi_ref[...], lam_ref[...])
        xc_ref[...] = xc
        a_ref[...] = a
        u_ref[...] = mult * (ig * xc)

    full = lambda a_: pl.BlockSpec(a_.shape, lambda i, nd=a_.ndim: (0,) * nd)
    ospec = pl.BlockSpec((tr, w), lambda i: (i, 0))
    return pl.pallas_call(
        body, grid=(nblk,), name="rg_gates_fwd",
        in_specs=[cur, prev] + [full(a_) for a_ in (conv_w, conv_b, wa, wi, ba, bi, lam)],
        out_specs=[ospec] * 3, out_shape=[SDS((t, w), f32)] * 3,
        compiler_params=_cparams(("parallel",)))(proj, proj, conv_w, conv_b, wa, wi, ba, bi, lam)


def _lru_scan(name, a, b, reverse, tt=512):
    t, w = a.shape
    tt = min(tt, t)
    nt = t // tt
    ng = tt // SUBLANES

    def body(a_ref, b_ref, h_ref, carry_ref):
        @pl.when(pl.program_id(0) == 0)
        def _():
            carry_ref[...] = jnp.zeros_like(carry_ref)

        row = lax.broadcasted_iota(jnp.int32, (SUBLANES, w), 0)

        def group(gi, carry):
            g = (ng - 1 - gi) if reverse else gi
            rows = pl.ds(pl.multiple_of(g * SUBLANES, SUBLANES), SUBLANES)
            av = a_ref[rows, :]
            bv = b_ref[rows, :]
            for s in (1, 2, 4):
                sh = (SUBLANES - s) if reverse else s
                ok = (row < SUBLANES - s) if reverse else (row >= s)
                a_s = pltpu.roll(av, sh, 0)
                b_s = pltpu.roll(bv, sh, 0)
                bv = jnp.where(ok, av * b_s + bv, bv)
                av = jnp.where(ok, av * a_s, av)
            h = av * carry + bv
            h_ref[rows, :] = h
            edge = h[0:1, :] if reverse else h[SUBLANES - 1:SUBLANES, :]
            return jnp.broadcast_to(edge, (SUBLANES, w))

        carry_ref[...] = lax.fori_loop(0, ng, group, carry_ref[...])

    tmap = (lambda i: (nt - 1 - i, 0)) if reverse else (lambda i: (i, 0))
    spec = pl.BlockSpec((tt, w), tmap)
    return pl.pallas_call(
        body, grid=(nt,), name=name, in_specs=[spec, spec], out_specs=spec,
        out_shape=SDS((t, w), f32), scratch_shapes=[pltpu.VMEM((SUBLANES, w), f32)],
        compiler_params=_cparams(("arbitrary",)))(a, b)


def _rg_gates_bwd(dhs, c, hs, xc, wa, wi, ba, bi, lam, tr=256):
    t, w = xc.shape
    tr = min(tr, t)
    nblk = t // tr
    cur, prev, nxt = _halo_specs(tr, w, 0, nblk)

    def body(dhs_ref, c_ref, cn_ref, hs_ref, hp_ref, xc_ref, wa_ref, wi_ref, ba_ref, bi_ref, lam_ref,
             dxc_ref, dwa_ref, dwi_ref, dba_ref, dbi_ref, dlam_ref):
        i = pl.program_id(0)
        c_next = _taps_after(c_ref[...], cn_ref[...], i == nblk - 1)[0]
        h_prev = _taps_before(hs_ref[...], hp_ref[...], i == 0)[2]
        xc = xc_ref[...]
        lam = lam_ref[...]
        xb, r, ig, sp, a, mult = _lru_gates(xc, wa_ref, wi_ref, ba_ref[...], bi_ref[...], lam)
        dh = dhs_ref[...] + c_next
        dlog_a = dh * h_prev * a - (dh * ig * xc) * (a * a / mult)
        dpre_a = (dlog_a * (-LRU_C * sp) * r * (1.0 - r)).astype(bf16)
        dpre_i = (dh * mult * xc * ig * (1.0 - ig)).astype(bf16)
        dxc_ref[...] = (dh * mult * ig + _block_diag(dpre_a, wa_ref, NT) + _block_diag(dpre_i, wi_ref, NT))
        dsig = 1.0 / (1.0 + jnp.exp(lam))
        sums = [jnp.sum(dpre_a.astype(f32), axis=0, keepdims=True),
                jnp.sum(dpre_i.astype(f32), axis=0, keepdims=True),
                jnp.sum(dlog_a * (-LRU_C * r), axis=0, keepdims=True) * (-dsig)]

        @pl.when(i == 0)
        def _():
            dwa_ref[...] = jnp.zeros_like(dwa_ref)
            dwi_ref[...] = jnp.zeros_like(dwi_ref)
            dba_ref[...] = jnp.zeros_like(dba_ref)
            dbi_ref[...] = jnp.zeros_like(dbi_ref)
            dlam_ref[...] = jnp.zeros_like(dlam_ref)

        for n in range(LRU_BLOCKS):
            sl = slice(n * LRU_BW, (n + 1) * LRU_BW)
            dwa_ref[n] += lax.dot_general(xb[:, sl], dpre_a[:, sl], (TN, ((), ())), preferred_element_type=f32)
            dwi_ref[n] += lax.dot_general(xb[:, sl], dpre_i[:, sl], (TN, ((), ())), preferred_element_type=f32)
        dba_ref[...] += sums[0]
        dbi_ref[...] += sums[1]
        dlam_ref[...] += sums[2]

    full = lambda a_: pl.BlockSpec(a_.shape, lambda i, nd=a_.ndim: (0,) * nd)
    vec = pl.BlockSpec((1, w), lambda i: (0, 0))
    mat = pl.BlockSpec((LRU_BLOCKS, LRU_BW, LRU_BW), lambda i: (0, 0, 0))
    return pl.pallas_call(
        body, grid=(nblk,), name="rg_gates_bwd",
        in_specs=[cur, cur, nxt, cur, prev, cur] + [full(a_) for a_ in (wa, wi, ba, bi, lam)],
        out_specs=[cur, mat, mat, vec, vec, vec],
        out_shape=[SDS((t, w), f32), SDS((LRU_BLOCKS, LRU_BW, LRU_BW), f32), SDS((LRU_BLOCKS, LRU_BW, LRU_BW), f32),
                   SDS((1, w), f32), SDS((1, w), f32), SDS((1, w), f32)],
        compiler_params=_cparams(("arbitrary",)))(dhs, c, c, hs, hs, xc, wa, wi, ba, bi, lam)


def _rg_conv_bwd(dxc, proj, conv_w, tr=256):
    t, w = dxc.shape
    tr = min(tr, t)
    nblk = t // tr
    cur, _, nxt = _halo_specs(tr, w, 0, nblk)
    xcur, xprev, _ = _halo_specs(tr, w, 1, nblk)

    def body(d_ref, dn_ref, x_ref, xp_ref, cw_ref, dx_ref, dcw_ref, dcb_ref):
        i = pl.program_id(0)
        d = d_ref[...]
        x = x_ref[...]
        after = _taps_after(d, dn_ref[...], i == nblk - 1)
        before = _taps_before(x, xp_ref[...], i == 0) + [x]
        dx = cw_ref[3:4, :] * d
        for s in (1, 2, 3):
            dx = dx + cw_ref[3 - s:4 - s, :] * after[s - 1]
        dx_ref[...] = dx.astype(bf16)
        dcw = jnp.concatenate([jnp.sum(d * before[k], axis=0, keepdims=True) for k in range(4)], axis=0)
        dcb = jnp.sum(d, axis=0, keepdims=True)

        @pl.when(i == 0)
        def _():
            dcw_ref[...] = dcw
            dcb_ref[...] = dcb

        @pl.when(i > 0)
        def _():
            dcw_ref[...] += dcw
            dcb_ref[...] += dcb

    return pl.pallas_call(
        body, grid=(nblk,), name="rg_conv_bwd",
        in_specs=[cur, nxt, xcur, xprev, pl.BlockSpec((4, w), lambda i: (0, 0))],
        out_specs=[cur, pl.BlockSpec((4, w), lambda i: (0, 0)), pl.BlockSpec((1, w), lambda i: (0, 0))],
        out_shape=[SDS((t, w), bf16), SDS((4, w), f32), SDS((1, w), f32)],
        compiler_params=_cparams(("arbitrary",)))(dxc, dxc, proj, proj, conv_w)


def _attn_fwd(h, wts, j):
    proj = _mm_cols("attn_in", h, wts["attn_w_in"], j, bf16)
    kpad = jnp.pad(proj[:, A_W:2 * A_W], ((PAD_A, 0), (0, 0)))
    vpad = jnp.pad(proj[:, 2 * A_W:3 * A_W], ((PAD_A, 0), (0, 0)))
    bias = _bias_window(wts["attn_rel_bias"][j])
    out_a = _chunk_attn_fwd(proj, kpad, vpad, bias)
    out_b, out_b32 = _sb_fwd(proj)
    m = _mm_rows("attn_out", [out_a, out_b], wts["attn_w_out"], j, f32)
    return m, (proj, kpad, vpad, bias, out_a, out_b, out_b32)


def _attn_bwd(dm, h, saved, wts, j, grads):
    proj, kpad, vpad, bias, out_a, out_b, out_b32 = saved
    dout = _mm_rows_t("attn_out_t", dm, wts["attn_w_out"], j, bf16)
    grads["attn_w_out"] = _mm_wgrad("attn_out_wgrad_a", out_a, dm, grads["attn_w_out"], j, 0)
    grads["attn_w_out"] = _mm_wgrad("attn_out_wgrad_b", out_b, dm, grads["attn_w_out"], j, 1)
    dqa, dka, dva, dbias = _chunk_attn_bwd(proj, kpad, vpad, bias, dout)
    dqs, dks, dvs = _sb_bwd(proj, out_b32, dout)
    grads["attn_rel_bias"][j] = _bias_window_grad(dbias)
    dproj = jnp.concatenate([dqa, dka[PAD_A:].astype(bf16), dva[PAD_A:].astype(bf16),
                             dqs, dks.astype(bf16), dvs.astype(bf16)], axis=1)
    grads["attn_w_in"] = _mm_wgrad_cols("attn_in_wgrad", h, dproj, grads["attn_w_in"], j)
    return _mm_cols_t("attn_in_t", dproj, wts["attn_w_in"], j, f32)


def _rg_fwd(h, wts, j):
    proj = _mm_cols("rg_in", h, wts["rg_w_in"], j, f32)
    small = [wts[k][j] for k in ("rg_conv_w", "rg_conv_b", "rg_w_a", "rg_w_i", "rg_b_a", "rg_b_i", "rg_lambda")]
    xc, a, u = _rg_gates_fwd(proj, *small)
    hs = _lru_scan("lru_scan_fwd", a, u, False)
    yp = _rows("rg_gate_out", lambda hv, gv: hv * _gelu(gv), [hs, (proj, D_MODEL, 0)], [], [(D_MODEL, bf16)])[0]
    m = _mm_rows("rg_out", [yp], wts["rg_w_out"], j, f32)
    return m, (proj, xc, a, hs, yp)


def _rg_bwd(dm, h, saved, wts, j, grads):
    proj, xc, a, hs, yp = saved
    dyp = _mm_rows_t("rg_out_t", dm, wts["rg_w_out"], j, f32)
    grads["rg_w_out"] = _mm_wgrad("rg_out_wgrad", yp, dm, grads["rg_w_out"], j)

    def gate_bwd(dy, hv, gv, av):
        dhs = dy * _gelu(gv)
        return dhs, av * dhs, dy * hv * _gelu_grad(gv)

    dhs, ab, dgate = _rows("rg_gate_out_bwd", gate_bwd, [dyp, hs, (proj, D_MODEL, 0), a], [],
                           [(D_MODEL, f32), (D_MODEL, f32), (D_MODEL, bf16)])
    c = _lru_scan("lru_scan_bwd", a, ab, True)
    wa, wi, ba, bi, lam = [wts[k][j] for k in ("rg_w_a", "rg_w_i", "rg_b_a", "rg_b_i", "rg_lambda")]
    dxc, dwa, dwi, dba, dbi, dlam = _rg_gates_bwd(dhs, c, hs, xc, wa, wi, ba, bi, lam)
    dxr, dcw, dcb = _rg_conv_bwd(dxc, proj, wts["rg_conv_w"][j])
    for k, v in (("rg_w_a", dwa), ("rg_w_i", dwi), ("rg_b_a", dba), ("rg_b_i", dbi), ("rg_lambda", dlam),
                 ("rg_conv_w", dcw), ("rg_conv_b", dcb)):
        grads[k][j] = v
    dproj = jnp.concatenate([dgate, dxr], axis=1)
    grads["rg_w_in"] = _mm_wgrad_cols("rg_in_wgrad", h, dproj, grads["rg_w_in"], j)
    return _mm_cols_t("rg_in_t", dproj, wts["rg_w_in"], j, f32)


def _local_step(x, target, wts):
    t = x.shape[0]
    d = D_MODEL
    gains = {k: wts[k] for k in ("norm_mix_pre", "norm_mix_post", "norm_ffn_pre", "norm_ffn_post")}
    gain = lambda k, l: gains[k][l:l + 1]

    saved = []
    h = _rows("norm_in", _norm_fwd, [x], [gain("norm_mix_pre", 0)], [(d, bf16)])[0]
    loss_cols = None
    for l in range(DEPTH):
        j = l // 2
        m, mix_saved = (_attn_fwd if l % 2 == 0 else _rg_fwd)(h, wts, j)

        def resid_next(xv, mv, g_post, g_next):
            x1 = xv + _norm_fwd(mv, g_post)
            return x1, _norm_fwd(x1, g_next)

        x1, h2 = _rows("resid_mix", resid_next, [x, m], [gain("norm_mix_post", l), gain("norm_ffn_pre", l)],
                       [(d, f32), (d, bf16)])
        g, u, hid = _ffn_up(h2, wts["ffn_w_gate"], wts["ffn_w_up"], l)
        f = _ffn_down(hid, wts["ffn_w_down"], l)
        saved.append((x, h, m, mix_saved, x1, h2, g, u, hid, f))
        if l + 1 < DEPTH:
            x, h = _rows("resid_ffn", resid_next, [x1, f], [gain("norm_ffn_post", l), gain("norm_mix_pre", l + 1)],
                         [(d, f32), (d, bf16)])
        else:
            def resid_loss(xv, fv, tv, g_post):
                err = xv + _norm_fwd(fv, g_post) - tv
                return err * (1.0 / d), jnp.sum(err * err, axis=0, keepdims=True)

            dx, loss_cols = _rows("resid_loss", resid_loss, [x1, f, target], [gain("norm_ffn_post", l)],
                                  [(d, f32)], [((1, d), f32)])
    loss = 0.5 * jnp.sum(loss_cols) / d

    grads = {k: {} for k in SMALL_GRADS}
    for k in BIG_GRADS:
        shp = wts[k].shape
        grads[k] = jnp.zeros((shp[0],) + shp[2:] if shp[1] == 1 else shp, f32)

    def norm_bwd_cast(uv, dyv, gv):
        du, dg = _norm_bwd(uv, dyv, gv)
        return du, dg

    def norm_bwd_resid(uv, dhv, dxv, gv):
        du, dg = _norm_bwd(uv, dhv, gv)
        return dxv + du, dg

    for l in reversed(range(DEPTH)):
        j = l // 2
        x_in, h, m, mix_saved, x1, h2, g, u, hid, f = saved[l]
        df, grads["norm_ffn_post"][l] = _rows("norm_ffn_post_bwd", norm_bwd_cast, [f, dx], [gain("norm_ffn_post", l)],
                                              [(d, bf16)], [((1, d), f32)])
        dg, du = _ffn_down_bwd(df, wts["ffn_w_down"], l, g, u)
        grads["ffn_w_down"] = _ffn_wgrad_down(hid, df, grads["ffn_w_down"], l)
        dh2 = _ffn_up_bwd(dg, du, wts["ffn_w_gate"], wts["ffn_w_up"], l)
        grads["ffn_w_gate"] = _ffn_wgrad_up("ffn_wgrad_gate", h2, dg, grads["ffn_w_gate"], l)
        grads["ffn_w_up"] = _ffn_wgrad_up("ffn_wgrad_up", h2, du, grads["ffn_w_up"], l)
        dx1, grads["norm_ffn_pre"][l] = _rows("norm_ffn_pre_bwd", norm_bwd_resid, [x1, dh2, dx],
                                              [gain("norm_ffn_pre", l)], [(d, f32)], [((1, d), f32)])
        dm, grads["norm_mix_post"][l] = _rows("norm_mix_post_bwd", norm_bwd_cast, [m, dx1], [gain("norm_mix_post", l)],
                                              [(d, bf16)], [((1, d), f32)])
        dh = (_attn_bwd if l % 2 == 0 else _rg_bwd)(dm, h, mix_saved, wts, j, grads)
        dx, grads["norm_mix_pre"][l] = _rows("norm_mix_pre_bwd", norm_bwd_resid, [x_in, dh, dx1],
                                             [gain("norm_mix_pre", l)], [(d, f32)], [((1, d), f32)])
    return loss, dx, grads


ANY = pl.BlockSpec(memory_space=pl.ANY)
PACK_COLS = 1024
SMALL_ROWS = 288


def _mesh_pos():
    x, y, c = lax.axis_index("x"), lax.axis_index("y"), lax.axis_index("c")
    return x, y, c, [(1 - x, y), (x, 1 - y), (1 - x, 1 - y)]


def _run_copies(copies):
    for cp in copies:
        cp.start()
    for cp in copies:
        cp.wait()


def _all_gather(shards):
    n = len(shards)
    per = 7

    def body(*refs):
        ins, outs = refs[:n], refs[n:2 * n]
        send, recv = refs[2 * n:]
        x, y, c, chips = _mesh_pos()
        q = 2 * x + y
        sibling = (x, y, 1 - c)

        def copy(k, src, dst, to):
            return pltpu.make_async_remote_copy(src_ref=src, dst_ref=dst, send_sem=send.at[k], recv_sem=recv.at[k],
                                                device_id=to, device_id_type=MESH)

        own, sent, passed = [], [], []
        for t in range(n):
            half = ins[t].shape[1] // 2
            rows = pl.ds(pl.multiple_of(c * half, half), half)
            own.append(copy(per * t, ins[t], outs[t].at[:, q], sibling))
            for j, (px, py) in enumerate(chips):
                sent.append(copy(per * t + 1 + j, ins[t].at[:, rows], outs[t].at[:, q, rows], (px, py, c)))
        for cp in own + sent:
            cp.start()
        for t in range(n):
            half = ins[t].shape[1] // 2
            rows = pl.ds(pl.multiple_of(c * half, half), half)
            for j, (px, py) in enumerate(chips):
                landed = outs[t].at[:, 2 * px + py, rows]
                copy(per * t + 1 + j, landed, landed, sibling).wait_recv()
                passed.append(copy(per * t + 4 + j, landed, landed, sibling))
                passed[-1].start()
        for cp in sent:
            cp.wait_send()
        for cp in own + passed:
            cp.wait()

    return pl.pallas_call(
        body, name="weight_all_gather", in_specs=[ANY] * n, out_specs=[ANY] * n,
        out_shape=[SDS((s.shape[0], N_CHIPS) + s.shape[1:], s.dtype) for s in shards],
        scratch_shapes=[pltpu.SemaphoreType.DMA((per * n,)), pltpu.SemaphoreType.DMA((per * n,))])(*shards)


def _pair_exchange(gs):
    n = len(gs)

    def body(*refs):
        ins, outs = refs[:n], refs[n:2 * n]
        send, recv = refs[2 * n:]
        x, y, c, _ = _mesh_pos()
        copies = []
        for t in range(n):
            half = ins[t].shape[2] // 2
            src = ins[t].at[:, :, pl.ds(pl.multiple_of((1 - c) * half, SUBLANES), half)]
            copies.append(pltpu.make_async_remote_copy(
                src_ref=src, dst_ref=outs[t], send_sem=send.at[t], recv_sem=recv.at[t],
                device_id=(x, y, 1 - c), device_id_type=MESH))
        _run_copies(copies)

    return pl.pallas_call(
        body, name="grad_pair_exchange", in_specs=[ANY] * n, out_specs=[ANY] * n,
        out_shape=[SDS(g.shape[:2] + (g.shape[2] // 2, g.shape[3]), f32) for g in gs],
        scratch_shapes=[pltpu.SemaphoreType.DMA((n,)), pltpu.SemaphoreType.DMA((n,))])(*gs)


def _pair_sum(name, g, got, c):
    l, s, r, cols = g.shape

    def body(c_ref, a_ref, b_ref, o_ref):
        o_ref[...] = (a_ref[...] + b_ref[...]).astype(bf16)

    blk = (None, None, r // 2, cols)
    return pl.pallas_call(
        body, name=name, out_shape=SDS(got.shape, bf16),
        grid_spec=pltpu.PrefetchScalarGridSpec(
            num_scalar_prefetch=1, grid=(l, s),
            in_specs=[pl.BlockSpec(blk, lambda i, q, c_ref: (i, q, c_ref[0], 0)),
                      pl.BlockSpec(blk, lambda i, q, c_ref: (i, q, 0, 0))],
            out_specs=pl.BlockSpec(blk, lambda i, q, c_ref: (i, q, 0, 0))),
        compiler_params=_cparams(("parallel", "parallel")))(c, g, got)


def _chip_exchange(hs):
    n = len(hs)

    def body(*refs):
        ins, outs = refs[:n], refs[n:2 * n]
        send, recv = refs[2 * n:]
        x, y, c, chips = _mesh_pos()
        q = 2 * x + y
        copies = []
        for t in range(n):
            for j, (px, py) in enumerate(chips):
                copies.append(pltpu.make_async_remote_copy(
                    src_ref=ins[t].at[:, 2 * px + py], dst_ref=outs[t].at[:, q], send_sem=send.at[3 * t + j],
                    recv_sem=recv.at[3 * t + j], device_id=(px, py, c), device_id_type=MESH))
        _run_copies(copies)

    return pl.pallas_call(
        body, name="grad_chip_exchange", in_specs=[ANY] * n, out_specs=[ANY] * n,
        out_shape=[SDS(h.shape, h.dtype) for h in hs],
        scratch_shapes=[pltpu.SemaphoreType.DMA((3 * n,)), pltpu.SemaphoreType.DMA((3 * n,))])(*hs)


def _chip_sum(name, s, h, pos):
    l, _, r, cols = s.shape

    def body(pos_ref, s0, s1, s2, s3, own_ref, o_ref):
        vals = [jnp.where(pos_ref[0] == p, own_ref[...], ref[...]).astype(f32) for p, ref in enumerate((s0, s1, s2, s3))]
        o_ref[...] = ((vals[0] + vals[1]) + vals[2]) + vals[3]

    blk = (None, None, r, cols)
    slot = lambda p: pl.BlockSpec(blk, lambda i, pos_ref: (i, jnp.where(pos_ref[0] == p, (p + 1) % N_CHIPS, p), 0, 0))
    return pl.pallas_call(
        body, name=name, out_shape=SDS((l, 2 * r, cols), f32),
        grid_spec=pltpu.PrefetchScalarGridSpec(
            num_scalar_prefetch=1, grid=(l,),
            in_specs=[slot(p) for p in range(N_CHIPS)] + [pl.BlockSpec(blk, lambda i, pos_ref: (i, pos_ref[0], 0, 0))],
            out_specs=pl.BlockSpec((None, r, cols), lambda i, pos_ref: (i, pos_ref[1], 0))),
        compiler_params=_cparams(("parallel",)))(pos, s, s, s, s, h)


def _pair_gather(fulls):
    n = len(fulls)

    def body(*refs):
        ins, outs = refs[:n], refs[n:2 * n]
        send, recv = refs[2 * n:]
        x, y, c, _ = _mesh_pos()
        copies = []
        for t in range(n):
            half = outs[t].shape[1] // 2
            rows = outs[t].at[:, pl.ds(pl.multiple_of(c * half, SUBLANES), half)]
            copies.append(pltpu.make_async_remote_copy(
                src_ref=rows, dst_ref=rows, send_sem=send.at[t], recv_sem=recv.at[t],
                device_id=(x, y, 1 - c), device_id_type=MESH))
        _run_copies(copies)

    return pl.pallas_call(
        body, name="grad_pair_gather", in_specs=[ANY] * n, out_specs=[ANY] * n,
        out_shape=[SDS(f.shape, f32) for f in fulls], input_output_aliases={t: t for t in range(n)},
        scratch_shapes=[pltpu.SemaphoreType.DMA((n,)), pltpu.SemaphoreType.DMA((n,))])(*fulls)


COL_SHARDED = ("attn_w_in", "rg_w_in", "ffn_w_gate", "ffn_w_up")
ROW_SHARDED = ("attn_w_out", "rg_w_out")
GATES = ("rg_w_a", "rg_w_i")
VECTORS = ("rg_conv_w", "rg_conv_b", "rg_b_a", "rg_b_i", "rg_lambda")
REPLICATED = ("norm_mix_pre", "norm_mix_post", "norm_ffn_pre", "norm_ffn_post", "attn_rel_bias")
BIG_GRADS = COL_SHARDED + ROW_SHARDED + ("ffn_w_down",)
SMALL_GRADS = GATES + VECTORS + REPLICATED
WEIGHTS =("attn_w_in", "attn_rel_bias", "attn_w_out", "rg_w_in", "rg_conv_w", "rg_conv_b", "rg_w_a", "rg_b_a",
           "rg_w_i", "rg_b_i", "rg_lambda", "rg_w_out", "norm_mix_pre", "norm_mix_post", "norm_ffn_pre",
           "norm_ffn_post", "ffn_w_gate", "ffn_w_up", "ffn_w_down")
SMALL = VECTORS + REPLICATED


def _gather_weights(w):
    big = list(COL_SHARDED + ROW_SHARDED + GATES + ("ffn_w_down",))
    shards = []
    for k in big:
        a = w[k].astype(bf16)
        shards.append(a.reshape((-1,) + a.shape[-2:]))
    vec = jnp.concatenate([w[k].reshape(-1) for k in VECTORS])
    shards.append(vec.reshape(1, -1, LANES))
    got = dict(zip(big + ["vec"], _all_gather(shards)))
    out = {k: w[k] for k in REPLICATED}
    for k in COL_SHARDED + ("ffn_w_down",):
        out[k] = got[k]
    for k in ROW_SHARDED:
        l, s, ks, n = got[k].shape
        out[k] = got[k].reshape(l, 1, s * ks, n)
    for k in GATES:
        out[k] = got[k].reshape(2, LRU_BLOCKS, LRU_BW, LRU_BW)
    vec = got["vec"].reshape(N_CHIPS, -1)
    off = 0
    for k in VECTORS:
        shp = w[k].shape
        n = int(np.prod(shp))
        piece = vec[:, off:off + n].reshape((N_CHIPS,) + shp)
        off += n
        if k == "rg_conv_w":
            out[k] = piece.reshape(N_CHIPS, 2, 4, 256).transpose(1, 2, 0, 3).reshape(2, 4, D_MODEL)
        elif k in ("rg_b_a", "rg_b_i"):
            out[k] = piece.transpose(1, 2, 0, 3).reshape(2, 1, D_MODEL)
        else:
            out[k] = piece.transpose(1, 0, 2).reshape(2, 1, D_MODEL)
    return out


def _grad_blocks(name, g):
    st = jnp.stack([g[i] for i in sorted(g)])
    if name in GATES:
        st = st.reshape(2, LRU_BLOCKS, N_CHIPS, LRU_BW // N_CHIPS, LRU_BW).transpose(2, 0, 1, 3, 4)
    elif name == "rg_conv_w":
        st = st.reshape(2, 4, N_CHIPS, -1).transpose(2, 0, 1, 3)
    elif name in ("rg_b_a", "rg_b_i"):
        st = st.reshape(2, LRU_BLOCKS, N_CHIPS, -1).transpose(2, 0, 1, 3)
    elif name in VECTORS:
        st = st.reshape(2, N_CHIPS, -1).transpose(1, 0, 2)
    else:
        st = jnp.broadcast_to(st.reshape(1, -1), (N_CHIPS, st.size))
    return st.reshape(N_CHIPS, -1)


def _reduce_gradients(grads, shard_shapes):
    gs = []
    for k in BIG_GRADS:
        g = grads[k]
        if g.ndim == 3:
            g = g.reshape(g.shape[0], N_CHIPS, g.shape[1] // N_CHIPS, g.shape[2])
        gs.append(g)
    blocks = [_grad_blocks(k, grads[k]) for k in SMALL_GRADS]
    used = sum(b.shape[1] for b in blocks)
    small = jnp.concatenate(blocks + [jnp.zeros((N_CHIPS, SMALL_ROWS * PACK_COLS - used), f32)], axis=1)
    gs.append(small.reshape(1, N_CHIPS, SMALL_ROWS, PACK_COLS))
    names = BIG_GRADS + ("small",)
    c = lax.axis_index("c").astype(jnp.int32).reshape(1)
    pos = jnp.stack([2 * lax.axis_index("x") + lax.axis_index("y"), lax.axis_index("c")]).astype(jnp.int32)
    parts = [_pair_sum("grad_pair_sum_" + k, g, r, c) for k, g, r in zip(names, gs, _pair_exchange(gs))]
    slots = _chip_exchange(parts)
    full = _pair_gather([_chip_sum("grad_chip_sum_" + k, s, h, pos) for k, s, h in zip(names, slots, parts)])
    out = {k: f.reshape(shard_shapes[k]) for k, f in zip(BIG_GRADS, full)}
    flat, off = full[-1].reshape(-1), 0
    for k in SMALL_GRADS:
        n = int(np.prod(shard_shapes[k]))
        out[k] = flat[off:off + n].reshape(shard_shapes[k])
        off += n
    return out


def _adamw_fn(w, g, m, v):
    m = ADAM_B1 * m + (1.0 - ADAM_B1) * g
    v = ADAM_B2 * v + (1.0 - ADAM_B2) * (g * g)
    m_hat = m / (1.0 - ADAM_B1 ** ADAM_STEP)
    v_hat = v / (1.0 - ADAM_B2 ** ADAM_STEP)
    return -ADAM_LR * (m_hat / (jnp.sqrt(v_hat) + ADAM_EPS) + ADAM_WD * w), m, v


def _adamw(name, w, g, m, v):
    shp = w.shape
    if w.size >= 1 << 16:
        width = shp[-1]
        ops = [a.reshape(-1, width) for a in (w, g, m, v)]
        res = _rows(name, _adamw_fn, ops, [], [(width, f32)] * 3)
        return [r.reshape(shp) for r in res]
    n = w.size
    rows = -(-n // (SUBLANES * LANES)) * SUBLANES
    ops = [jnp.pad(a.reshape(-1), (0, rows * LANES - n)).reshape(rows, LANES) for a in (w, g, m, v)]
    res = _rows(name, _adamw_fn, ops, [], [(LANES, f32)] * 3, tr=rows)
    return [r.reshape(-1)[:n].reshape(shp) for r in res]


def kernel(x, attn_w_in, attn_rel_bias, attn_w_out, rg_w_in, rg_conv_w, rg_conv_b, rg_w_a, rg_b_a, rg_w_i, rg_b_i, rg_lambda, rg_w_out, norm_mix_pre, norm_mix_post, norm_ffn_pre, norm_ffn_post, ffn_w_gate, ffn_w_up, ffn_w_down, loss_target, m_attn_w_in, m_attn_rel_bias, m_attn_w_out, m_rg_w_in, m_rg_conv_w, m_rg_conv_b, m_rg_w_a, m_rg_b_a, m_rg_w_i, m_rg_b_i, m_rg_lambda, m_rg_w_out, m_norm_mix_pre, m_norm_mix_post, m_norm_ffn_pre, m_norm_ffn_post, m_ffn_w_gate, m_ffn_w_up, m_ffn_w_down, v_attn_w_in, v_attn_rel_bias, v_attn_w_out, v_rg_w_in, v_rg_conv_w, v_rg_conv_b, v_rg_w_a, v_rg_b_a, v_rg_w_i, v_rg_b_i, v_rg_lambda, v_rg_w_out, v_norm_mix_pre, v_norm_mix_post, v_norm_ffn_pre, v_norm_ffn_post, v_ffn_w_gate, v_ffn_w_up, v_ffn_w_down):
    w = dict(zip(WEIGHTS, (attn_w_in, attn_rel_bias, attn_w_out, rg_w_in, rg_conv_w, rg_conv_b, rg_w_a, rg_b_a, rg_w_i,
                           rg_b_i, rg_lambda, rg_w_out, norm_mix_pre, norm_mix_post, norm_ffn_pre, norm_ffn_post,
                           ffn_w_gate, ffn_w_up, ffn_w_down)))
    m = dict(zip(WEIGHTS, (m_attn_w_in, m_attn_rel_bias, m_attn_w_out, m_rg_w_in, m_rg_conv_w, m_rg_conv_b, m_rg_w_a,
                           m_rg_b_a, m_rg_w_i, m_rg_b_i, m_rg_lambda, m_rg_w_out, m_norm_mix_pre, m_norm_mix_post,
                           m_norm_ffn_pre, m_norm_ffn_post, m_ffn_w_gate, m_ffn_w_up, m_ffn_w_down)))
    v = dict(zip(WEIGHTS, (v_attn_w_in, v_attn_rel_bias, v_attn_w_out, v_rg_w_in, v_rg_conv_w, v_rg_conv_b, v_rg_w_a,
                           v_rg_b_a, v_rg_w_i, v_rg_b_i, v_rg_lambda, v_rg_w_out, v_norm_mix_pre, v_norm_mix_post,
                           v_norm_ffn_pre, v_norm_ffn_post, v_ffn_w_gate, v_ffn_w_up, v_ffn_w_down)))
    wts = _gather_weights(w)
    loss, dx, grads = _local_step(x[0], loss_target[0], wts)
    loss = lax.psum(loss, ("x", "y", "c"))
    g = _reduce_gradients(grads, {k: w[k].shape for k in WEIGHTS})

    big = [k for k in WEIGHTS if k not in SMALL]
    upd = {k: _adamw("adamw_" + k, w[k], g[k], m[k], v[k]) for k in big}
    cat = lambda d: jnp.concatenate([d[k].reshape(-1) for k in SMALL])
    small = _adamw("adamw_small", cat(w), cat(g), cat(m), cat(v))
    off = 0
    for k in SMALL:
        n = w[k].size
        upd[k] = [r[off:off + n].reshape(w[k].shape) for r in small]
        off += n
    return (loss, dx[None], *[g[k] for k in WEIGHTS], *[upd[k][0] for k in WEIGHTS],
            *[upd[k][1] for k in WEIGHTS], *[upd[k][2] for k in WEIGHTS])
```

```python
import functools

import numpy as np
import jax
import jax.numpy as jnp
from jax import lax
from jax.experimental import pallas as pl
from jax.experimental.pallas import tpu as pltpu

f32 = jnp.float32
bf16 = jnp.bfloat16
SDS = jax.ShapeDtypeStruct
MESH = pl.DeviceIdType.MESH

D_MODEL = 1024
N_CHIPS = 4
DEPTH = 4
HEAD_DIM = 64
CHUNK = 64
N_LEFT = 8
REL_CLIP = 256
A_W = 512
LRU_BLOCKS = 4
LRU_BW = 256
LRU_C = 8.0
D_FF = 2816
RMS_EPS = 1e-6
LANES = 128
SUBLANES = 8
VMEM_LIMIT = 56 * 1024 * 1024

QB_A = 2 * CHUNK
QSUB_A = 2
KW_A = QB_A + N_LEFT * CHUNK
PAD_A = N_LEFT * CHUNK
EXT_A = 768
SB_BLK = 256
SB_DEAD = -110.0

ADAM_LR, ADAM_B1, ADAM_B2, ADAM_EPS, ADAM_WD, ADAM_STEP = 0.001, 0.9, 0.999, 1e-08, 0.01, 10


def _cparams(sem):
    return pltpu.CompilerParams(dimension_semantics=sem, vmem_limit_bytes=VMEM_LIMIT)


def _gemm(name, operands, in_specs, o_spec, out_shape, grid, dims, acc_shape, into=None):
    nred = grid[2]
    npair = len(operands) // 2
    nin = 2 * npair + (into is not None)

    def body(*refs):
        o_ref = refs[nin]
        p = None
        for t in range(npair):
            d = lax.dot_general(refs[2 * t][...], refs[2 * t + 1][...], (dims, ((), ())),
                                preferred_element_type=f32)
            p = d if p is None else p + d
        if nred == 1:
            o_ref[...] = p.astype(o_ref.dtype)
        else:
            acc = refs[nin + 1]
            r = pl.program_id(2)

            @pl.when(r == 0)
            def _():
                acc[...] = p

            @pl.when(r > 0)
            def _():
                acc[...] += p

            @pl.when(r == nred - 1)
            def _():
                o_ref[...] = acc[...].astype(o_ref.dtype)

    scratch = [] if nred == 1 else [pltpu.VMEM(acc_shape, f32)]
    extra, alias = ([], {}) if into is None else ([into], {2 * npair: 0})
    return pl.pallas_call(
        body, grid=grid, in_specs=list(in_specs) + [pl.BlockSpec(memory_space=pl.ANY)] * len(extra),
        out_specs=o_spec, out_shape=out_shape, scratch_shapes=scratch, name=name, input_output_aliases=alias,
        compiler_params=_cparams(("parallel", "parallel", "arbitrary")))(*operands, *extra)


NN = ((1,), (0,))
NT = ((1,), (1,))
TN = ((0,), (0,))


def _tile(t, want=1024):
    return min(want, t)


def _mm_cols(name, a, w, l, out_dtype):
    t, k = a.shape
    _, s, _, ns = w.shape
    tm = _tile(t)
    return _gemm(
        name, [a, w],
        [pl.BlockSpec((tm, k), lambda i, j, r: (i, 0)),
         pl.BlockSpec((None, None, k, ns), lambda i, j, r: (l, j, 0, 0))],
        pl.BlockSpec((tm, ns), lambda i, j, r: (i, j)),
        SDS((t, s * ns), out_dtype), (t // tm, s, 1), NN, None)


def _mm_cols_t(name, dy, w, l, out_dtype):
    t = dy.shape[0]
    _, s, k, ns = w.shape
    tm = _tile(t)
    return _gemm(
        name, [dy, w],
        [pl.BlockSpec((tm, ns), lambda i, j, r: (i, r)),
         pl.BlockSpec((None, None, k, ns), lambda i, j, r: (l, r, 0, 0))],
        pl.BlockSpec((tm, k), lambda i, j, r: (i, 0)),
        SDS((t, k), out_dtype), (t // tm, 1, s), NT, (tm, k))


def _mm_wgrad_cols(name, a, dy, buf, l):
    t, k = a.shape
    _, s, _, ns = buf.shape
    tt = _tile(t)
    return _gemm(
        name, [a, dy],
        [pl.BlockSpec((tt, k), lambda i, j, r: (r, 0)),
         pl.BlockSpec((tt, ns), lambda i, j, r: (r, i))],
        pl.BlockSpec((None, None, k, ns), lambda i, j, r: (l, i, 0, 0)),
        SDS(buf.shape, f32), (s, 1, t // tt), TN, (k, ns), into=buf)


def _mm_rows(name, parts, w, l, out_dtype):
    t = parts[0].shape[0]
    n = w.shape[3]
    tm = _tile(t)
    ops, specs = [], []
    for p_i, a in enumerate(parts):
        kp = a.shape[1]
        ops += [a, w]
        specs += [pl.BlockSpec((tm, kp), lambda i, j, r: (i, 0)),
                  pl.BlockSpec((None, None, kp, n), lambda i, j, r, p_i=p_i: (l, 0, p_i, 0))]
    return _gemm(name, ops, specs, pl.BlockSpec((tm, n), lambda i, j, r: (i, 0)),
                 SDS((t, n), out_dtype), (t // tm, 1, 1), NN, None)


def _mm_rows_t(name, dy, w, l, out_dtype):
    t, n = dy.shape
    k = w.shape[2]
    tm = _tile(t)
    return _gemm(
        name, [dy, w],
        [pl.BlockSpec((tm, n), lambda i, j, r: (i, 0)),
         pl.BlockSpec((None, None, k, n), lambda i, j, r: (l, 0, 0, 0))],
        pl.BlockSpec((tm, k), lambda i, j, r: (i, 0)),
        SDS((t, k), out_dtype), (t // tm, 1, 1), NT, None)


def _mm_wgrad(name, a, dy, buf, l, part=0):
    t, k = a.shape
    n = dy.shape[1]
    tt = _tile(t)
    return _gemm(
        name, [a, dy],
        [pl.BlockSpec((tt, k), lambda i, j, r: (r, 0)),
         pl.BlockSpec((tt, n), lambda i, j, r: (r, 0))],
        pl.BlockSpec((None, k, n), lambda i, j, r: (l, part, 0)),
        SDS(buf.shape, f32), (1, 1, t // tt), TN, (k, n), into=buf)


def _ffn_up(h, wg, wu, l):
    t, k = h.shape
    s, fs = wg.shape[1], wg.shape[3]
    tm = _tile(t)

    def body(h_ref, wg_ref, wu_ref, g_ref, u_ref, hid_ref):
        hv = h_ref[...]
        g = jnp.dot(hv, wg_ref[...], preferred_element_type=f32)
        u = jnp.dot(hv, wu_ref[...], preferred_element_type=f32)
        g_ref[...] = g.astype(bf16)
        u_ref[...] = u.astype(bf16)
        hid_ref[...] = (g * jax.nn.sigmoid(g) * u).astype(bf16)

    wspec = pl.BlockSpec((None, None, k, fs), lambda j, i: (l, j, 0, 0))
    ospec = pl.BlockSpec((None, tm, fs), lambda j, i: (j, i, 0))
    return pl.pallas_call(
        body, grid=(s, t // tm), name="ffn_up",
        in_specs=[pl.BlockSpec((tm, k), lambda j, i: (i, 0)), wspec, wspec],
        out_specs=[ospec, ospec, ospec], out_shape=[SDS((s, t, fs), bf16)] * 3,
        compiler_params=_cparams(("parallel", "parallel")))(h, wg, wu)


def _ffn_down(hid, wd, l):
    s, t, fs = hid.shape
    n = wd.shape[3]
    tm = _tile(t, 512)
    ops, specs = [], []
    for r in range(s):
        ops += [hid, wd]
        specs += [pl.BlockSpec((None, tm, fs), lambda i, j, k, r=r: (r, i, 0)),
                  pl.BlockSpec((None, None, fs, n), lambda i, j, k, r=r: (l, r, 0, 0))]
    return _gemm("ffn_down", ops, specs, pl.BlockSpec((tm, n), lambda i, j, k: (i, 0)),
                 SDS((t, n), f32), (t // tm, 1, 1), NN, None)


def _ffn_down_bwd(df, wd, l, g, u):
    t, n = df.shape
    s, fs = wd.shape[1], wd.shape[2]
    tm = _tile(t)

    def body(df_ref, wd_ref, g_ref, u_ref, dg_ref, du_ref):
        dh = lax.dot_general(df_ref[...], wd_ref[...], (NT, ((), ())), preferred_element_type=f32)
        gv = g_ref[...].astype(f32)
        uv = u_ref[...].astype(f32)
        sg = jax.nn.sigmoid(gv)
        du_ref[...] = (dh * gv * sg).astype(bf16)
        dg_ref[...] = (dh * uv * (sg * (1.0 + gv * (1.0 - sg)))).astype(bf16)

    bspec = pl.BlockSpec((None, tm, fs), lambda j, i: (j, i, 0))
    return pl.pallas_call(
        body, grid=(s, t // tm), name="ffn_down_bwd",
        in_specs=[pl.BlockSpec((tm, n), lambda j, i: (i, 0)),
                  pl.BlockSpec((None, None, fs, n), lambda j, i: (l, j, 0, 0)), bspec, bspec],
        out_specs=[bspec, bspec], out_shape=[SDS((s, t, fs), bf16)] * 2,
        compiler_params=_cparams(("parallel", "parallel")))(df, wd, g, u)


def _ffn_up_bwd(dg, du, wg, wu, l):
    s, t, fs = dg.shape
    k = wg.shape[2]
    tm = _tile(t, 512)
    ops, specs = [], []
    for r in range(s):
        aspec = pl.BlockSpec((None, tm, fs), lambda i, j, kk, r=r: (r, i, 0))
        wspec = pl.BlockSpec((None, None, k, fs), lambda i, j, kk, r=r: (l, r, 0, 0))
        ops += [dg, wg, du, wu]
        specs += [aspec, wspec, aspec, wspec]
    return _gemm("ffn_up_bwd", ops, specs, pl.BlockSpec((tm, k), lambda i, j, kk: (i, 0)),
                 SDS((t, k), f32), (t // tm, 1, 1), NT, None)


def _ffn_wgrad_up(name, h, dy, buf, l):
    t, k = h.shape
    s, _, fs = dy.shape
    tt = _tile(t)
    return _gemm(
        name, [h, dy],
        [pl.BlockSpec((tt, k), lambda i, j, r: (r, 0)),
         pl.BlockSpec((None, tt, fs), lambda i, j, r: (i, r, 0))],
        pl.BlockSpec((None, None, k, fs), lambda i, j, r: (l, i, 0, 0)),
        SDS(buf.shape, f32), (s, 1, t // tt), TN, (k, fs), into=buf)


def _ffn_wgrad_down(hid, df, buf, l):
    s, t, fs = hid.shape
    n = df.shape[1]
    tt = _tile(t)
    return _gemm(
        "ffn_wgrad_down", [hid, df],
        [pl.BlockSpec((None, tt, fs), lambda i, j, r: (i, r, 0)),
         pl.BlockSpec((tt, n), lambda i, j, r: (r, 0))],
        pl.BlockSpec((None, None, fs, n), lambda i, j, r: (l, i, 0, 0)),
        SDS(buf.shape, f32), (s, 1, t // tt), TN, (fs, n), into=buf)


def _rows(name, fn, rows, consts, row_outs, acc_outs=(), tr=256):
    rows = [r if isinstance(r, tuple) else (r, r.shape[1], 0) for r in rows]
    t = rows[0][0].shape[0]
    tr = min(tr, t)
    nin = len(rows) + len(consts)
    no, na = len(row_outs), len(acc_outs)

    def body(*refs):
        vals = fn(*[r[...] for r in refs[:nin]])
        if not isinstance(vals, (tuple, list)):
            vals = (vals,)
        for k in range(no):
            refs[nin + k][...] = vals[k].astype(refs[nin + k].dtype)
        first = pl.program_id(0) == 0
        for k in range(na):
            ref, val = refs[nin + no + k], vals[no + k]

            @pl.when(first)
            def _(ref=ref, val=val):
                ref[...] = val

            @pl.when(jnp.logical_not(first))
            def _(ref=ref, val=val):
                ref[...] += val

    in_specs = [pl.BlockSpec((tr, w), lambda i, cb=cb: (i, cb)) for (_, w, cb) in rows]
    in_specs += [pl.BlockSpec(c.shape, lambda i, nd=c.ndim: (0,) * nd) for c in consts]
    out_specs = [pl.BlockSpec((tr, w), lambda i: (i, 0)) for (w, _) in row_outs]
    out_specs += [pl.BlockSpec(s, lambda i, nd=len(s): (0,) * nd) for (s, _) in acc_outs]
    out_shape = [SDS((t, w), dt) for (w, dt) in row_outs] + [SDS(s, dt) for (s, dt) in acc_outs]
    res = pl.pallas_call(
        body, grid=(t // tr,), in_specs=in_specs, out_specs=out_specs, out_shape=out_shape,
        name=name, compiler_params=_cparams(("arbitrary",)))(*[r[0] for r in rows], *consts)
    return res


def _rstd(x):
    return lax.rsqrt(jnp.mean(x * x, axis=-1, keepdims=True) + RMS_EPS)


def _norm_fwd(x, g):
    return x * _rstd(x) * g


def _norm_bwd(u, dy, g):
    r = _rstd(u)
    n = u * r
    dn = dy * g
    du = r * (dn - n * jnp.mean(dn * n, axis=-1, keepdims=True))
    return du, jnp.sum(dy * n, axis=0, keepdims=True)


def _gelu(x):
    c = 0.7978845608028654
    return 0.5 * x * (1.0 + jnp.tanh(c * (x + 0.044715 * x * x * x)))


def _gelu_grad(x):
    c = 0.7978845608028654
    th = jnp.tanh(c * (x + 0.044715 * x * x * x))
    return 0.5 * (1.0 + th) + 0.5 * x * (1.0 - th * th) * c * (1.0 + 3.0 * 0.044715 * x * x)


def _mask_heads(x):
    lane = lax.broadcasted_iota(jnp.int32, x.shape, 1)
    return [jnp.where((lane >= h * HEAD_DIM) & (lane < (h + 1) * HEAD_DIM), x, jnp.zeros_like(x))
            for h in range(LANES // HEAD_DIM)]


def _chunk_valid(start):
    qi = lax.broadcasted_iota(jnp.int32, (QB_A, KW_A), 0)
    kj = lax.broadcasted_iota(jnp.int32, (QB_A, KW_A), 1)
    qc = qi // CHUNK
    kc = kj // CHUNK
    return (kc >= qc) & (kc <= qc + N_LEFT) & (kj + start >= PAD_A)


def _chunk_probs(q, k, bias, valid):
    s = lax.dot_general(q, k, (NT, ((), ())), preferred_element_type=f32) * (HEAD_DIM ** -0.5) + bias
    s = jnp.where(valid, s, -1e30)
    p = jnp.exp(s - jnp.max(s, axis=-1, keepdims=True))
    return p / jnp.sum(p, axis=-1, keepdims=True)


def _chunk_attn_fwd(proj, kpad, vpad, bias):
    t = proj.shape[0]
    tp = kpad.shape[0]
    step = QSUB_A * QB_A

    def body(q_ref, k_ref, v_ref, b_ref, o_ref):
        for sb in range(QSUB_A):
            start = pl.multiple_of((pl.program_id(1) * QSUB_A + sb) * QB_A, QB_A)
            rows = pl.ds(sb * QB_A, QB_A)
            valid = _chunk_valid(start)
            kw = k_ref[pl.ds(start, KW_A), :]
            qm = _mask_heads(q_ref[rows, :])
            vm = _mask_heads(v_ref[pl.ds(start, KW_A), :])
            o = None
            for h in range(len(qm)):
                p = _chunk_probs(qm[h], kw, b_ref[h], valid)
                d = jnp.dot(p.astype(bf16), vm[h], preferred_element_type=f32)
                o = d if o is None else o + d
            o_ref[rows, :] = o.astype(bf16)

    kv_spec = pl.BlockSpec((tp, LANES), lambda hp, qb: (0, hp))
    return pl.pallas_call(
        body, grid=(A_W // LANES, t // step), name="chunk_attn_fwd",
        in_specs=[pl.BlockSpec((step, LANES), lambda hp, qb: (qb, hp)), kv_spec, kv_spec,
                  pl.BlockSpec((2, QB_A, KW_A), lambda hp, qb: (hp, 0, 0))],
        out_specs=pl.BlockSpec((step, LANES), lambda hp, qb: (qb, hp)),
        out_shape=SDS((t, A_W), bf16),
        compiler_params=_cparams(("parallel", "arbitrary")))(proj, kpad, vpad, bias)


def _chunk_attn_bwd(proj, kpad, vpad, bias, dout):
    t = proj.shape[0]
    tp = kpad.shape[0]
    step = QSUB_A * QB_A

    def body(q_ref, k_ref, v_ref, b_ref, do_ref, dq_ref, dk_ref, dv_ref, db_ref):
        qb = pl.program_id(1)

        @pl.when(qb == 0)
        def _():
            dk_ref[...] = jnp.zeros_like(dk_ref)
            dv_ref[...] = jnp.zeros_like(dv_ref)
            db_ref[...] = jnp.zeros_like(db_ref)

        for sb in range(QSUB_A):
            start = pl.multiple_of((qb * QSUB_A + sb) * QB_A, QB_A)
            rows = pl.ds(sb * QB_A, QB_A)
            win = pl.ds(start, KW_A)
            valid = _chunk_valid(start)
            kw = k_ref[win, :]
            vw = v_ref[win, :]
            qm = _mask_heads(q_ref[rows, :])
            dom = _mask_heads(do_ref[rows, :])
            km = _mask_heads(kw)
            dq = dk = dv = None
            for h in range(len(qm)):
                p = _chunk_probs(qm[h], kw, b_ref[h], valid)
                dp = lax.dot_general(dom[h], vw, (NT, ((), ())), preferred_element_type=f32)
                ds = p * (dp - jnp.sum(dp * p, axis=-1, keepdims=True))
                db_ref[h] += ds
                dsb = (ds * (HEAD_DIM ** -0.5)).astype(bf16)
                terms = (jnp.dot(dsb, km[h], preferred_element_type=f32),
                         lax.dot_general(dsb, qm[h], (TN, ((), ())), preferred_element_type=f32),
                         lax.dot_general(p.astype(bf16), dom[h], (TN, ((), ())), preferred_element_type=f32))
                dq, dk, dv = terms if dq is None else (dq + terms[0], dk + terms[1], dv + terms[2])
            dq_ref[rows, :] = dq.astype(bf16)
            dk_ref[win, :] += dk
            dv_ref[win, :] += dv

    kv_spec = pl.BlockSpec((tp, LANES), lambda hp, qb: (0, hp))
    q_spec = pl.BlockSpec((step, LANES), lambda hp, qb: (qb, hp))
    b_spec = pl.BlockSpec((2, QB_A, KW_A), lambda hp, qb: (hp, 0, 0))
    return pl.pallas_call(
        body, grid=(A_W // LANES, t // step), name="chunk_attn_bwd",
        in_specs=[q_spec, kv_spec, kv_spec, b_spec, q_spec],
        out_specs=[q_spec, kv_spec, kv_spec, b_spec],
        out_shape=[SDS((t, A_W), bf16), SDS((tp, A_W), f32), SDS((tp, A_W), f32),
                   SDS((2 * A_W // LANES, QB_A, KW_A), f32)],
        compiler_params=_cparams(("parallel", "arbitrary")))(proj, kpad, vpad, bias, dout)


def _bias_ext(table):
    flat = PAD_A + QB_A - 1 - REL_CLIP
    top = jnp.broadcast_to(table[:, 2 * REL_CLIP:], (table.shape[0], flat))
    lo = 2 * REL_CLIP - (EXT_A - 1 - flat)
    return jnp.concatenate([top, jnp.flip(table[:, lo:], axis=1)], axis=1)


def _bias_window(table):
    nh = table.shape[0]
    e = jnp.broadcast_to(_bias_ext(table)[:, None, :], (nh, QB_A, EXT_A)).reshape(nh, QB_A * EXT_A)
    e = jnp.pad(e, ((0, 0), (0, QB_A)))
    m = e.reshape(nh, QB_A, EXT_A + 1)
    return jnp.flip(m, axis=1)[:, :, :KW_A]


def _bias_window_grad(dbias):
    nh = dbias.shape[0]
    d = jnp.flip(dbias, axis=1)
    d = jnp.pad(d, ((0, 0), (0, 0), (0, EXT_A + 1 - KW_A)))
    m = d.reshape(nh, QB_A * (EXT_A + 1))[:, :QB_A * EXT_A].reshape(nh, QB_A, EXT_A)
    dext = jnp.sum(m, axis=1)
    flat = PAD_A + QB_A - 1 - REL_CLIP
    lo = 2 * REL_CLIP - (EXT_A - 1 - flat)
    tail = jnp.flip(dext[:, flat:], axis=1)
    tail = tail.at[:, -1].add(jnp.sum(dext[:, :flat], axis=1))
    return jnp.pad(tail, ((0, 0), (lo, 0)))


def _tri_suffix(x, tri):
    hi = x.astype(bf16)
    lo = (x - hi.astype(f32)).astype(bf16)
    return jnp.dot(hi, tri, preferred_element_type=f32) + jnp.dot(lo, tri, preferred_element_type=f32)


def _sb_block(q, k, run, tri, causal):
    z = lax.dot_general(q, k, (NT, ((), ())), preferred_element_type=f32) * (HEAD_DIM ** -0.5)
    e = jnp.exp(-jnp.abs(z))
    l1p = jnp.log(1.0 + e)
    lb = jnp.minimum(z, 0.0) - l1p
    lmb = lb - z
    if causal is not None:
        lmb = jnp.where(causal, lmb, 0.0)
    cs = _tri_suffix(lmb, tri)
    w = jnp.exp(lb + (run + cs - lmb))
    if causal is not None:
        w = jnp.where(causal, w, 0.0)
    return z, e, w, run + cs[:, 0:1]


def _sb_tri():
    r = lax.broadcasted_iota(jnp.int32, (SB_BLK, SB_BLK), 0)
    c = lax.broadcasted_iota(jnp.int32, (SB_BLK, SB_BLK), 1)
    return (r >= c).astype(bf16), c < r


def _sb_live(runs):
    m = runs[0]
    for r in runs[1:]:
        m = jnp.maximum(m, r)
    return jnp.max(m) > SB_DEAD


def _sb_fwd(proj):
    t = proj.shape[0]
    cb = A_W // LANES
    nh = LANES // HEAD_DIM

    def body(q_ref, k_ref, v_ref, o_ref, of_ref):
        qb = pl.program_id(1)
        tri, diag = _sb_tri()
        qm = _mask_heads(q_ref[...])

        def pair(kb, carry, causal):
            rows = pl.ds(pl.multiple_of(kb * SB_BLK, SB_BLK), SB_BLK)
            k = k_ref[rows, :]
            vm = _mask_heads(v_ref[rows, :])
            runs, acc = [], carry[nh]
            for h in range(nh):
                _, _, w, run = _sb_block(qm[h], k, carry[h], tri, causal)
                acc = acc + jnp.dot(w.astype(bf16), vm[h], preferred_element_type=f32)
                runs.append(run)
            return (*runs, acc)

        zero = jnp.zeros((SB_BLK, 1), f32)
        carry = pair(qb, (zero,) * nh + (jnp.zeros((SB_BLK, LANES), f32),), diag)

        def cond(st):
            return (st[0] < qb) & _sb_live(st[1][:nh])

        def step(st):
            return st[0] + 1, pair(qb - 1 - st[0], st[1], None)

        _, carry = lax.while_loop(cond, step, (jnp.int32(0), carry))
        o_ref[...] = carry[nh].astype(bf16)
        of_ref[...] = carry[nh]

    ospec = pl.BlockSpec((SB_BLK, LANES), lambda hp, qb: (qb, hp))
    return pl.pallas_call(
        body, grid=(cb, t // SB_BLK), name="sb_attn_fwd",
        in_specs=[pl.BlockSpec((SB_BLK, LANES), lambda hp, qb: (qb, 3 * cb + hp)),
                  pl.BlockSpec((t, LANES), lambda hp, qb: (0, 4 * cb + hp)),
                  pl.BlockSpec((t, LANES), lambda hp, qb: (0, 5 * cb + hp))],
        out_specs=[ospec, ospec], out_shape=[SDS((t, A_W), bf16), SDS((t, A_W), f32)],
        compiler_params=_cparams(("parallel", "arbitrary")))(proj, proj, proj)


def _sb_bwd(proj, out_b, dout):
    t = proj.shape[0]
    cb = A_W // LANES
    nh = LANES // HEAD_DIM

    def body(q_ref, k_ref, v_ref, o_ref, do_ref, dq_ref, dk_ref, dv_ref):
        qb = pl.program_id(1)
        tri, diag = _sb_tri()

        @pl.when(qb == 0)
        def _():
            dk_ref[...] = jnp.zeros_like(dk_ref)
            dv_ref[...] = jnp.zeros_like(dv_ref)

        qm = _mask_heads(q_ref[...])
        do = do_ref[...]
        dom = _mask_heads(do)
        dsums = [jnp.sum(t_, axis=-1, keepdims=True) for t_ in _mask_heads(do.astype(f32) * o_ref[...])]

        def pair(kb, carry, causal):
            rows = pl.ds(pl.multiple_of(kb * SB_BLK, SB_BLK), SB_BLK)
            k = k_ref[rows, :]
            v = v_ref[rows, :]
            km = _mask_heads(k)
            new, dq, dk, dv = [], carry[2 * nh], None, None
            for h in range(nh):
                z, e, w, run = _sb_block(qm[h], k, carry[2 * h], tri, causal)
                inv = 1.0 / (1.0 + e)
                beta = jnp.where(z >= 0.0, inv, e * inv)
                wb = w.astype(bf16)
                g = lax.dot_general(dom[h], v, (NT, ((), ())), preferred_element_type=f32) * wb.astype(f32)
                sg = _tri_suffix(g, tri)
                dz = g * (1.0 - beta) - (dsums[h] - carry[2 * h + 1] - sg) * beta
                if causal is not None:
                    dz = jnp.where(causal, dz, 0.0)
                dzb = (dz * (HEAD_DIM ** -0.5)).astype(bf16)
                dq = dq + jnp.dot(dzb, km[h], preferred_element_type=f32)
                tk = lax.dot_general(dzb, qm[h], (TN, ((), ())), preferred_element_type=f32)
                tv = lax.dot_general(wb, dom[h], (TN, ((), ())), preferred_element_type=f32)
                dk, dv = (tk, tv) if dk is None else (dk + tk, dv + tv)
                new += [run, carry[2 * h + 1] + sg[:, 0:1]]
            dk_ref[rows, :] += dk
            dv_ref[rows, :] += dv
            return (*new, dq)

        zero = jnp.zeros((SB_BLK, 1), f32)
        carry = pair(qb, (zero,) * (2 * nh) + (jnp.zeros((SB_BLK, LANES), f32),), diag)

        def cond(st):
            return (st[0] < qb) & _sb_live(st[1][0:2 * nh:2])

        def step(st):
            return st[0] + 1, pair(qb - 1 - st[0], st[1], None)

        _, carry = lax.while_loop(cond, step, (jnp.int32(0), carry))
        dq_ref[...] = carry[2 * nh].astype(bf16)

    kv_in = lambda seg: pl.BlockSpec((t, LANES), lambda hp, qb: (0, seg * cb + hp))
    q_spec = pl.BlockSpec((SB_BLK, LANES), lambda hp, qb: (qb, hp))
    kv_out = pl.BlockSpec((t, LANES), lambda hp, qb: (0, hp))
    return pl.pallas_call(
        body, grid=(cb, t // SB_BLK), name="sb_attn_bwd",
        in_specs=[pl.BlockSpec((SB_BLK, LANES), lambda hp, qb: (qb, 3 * cb + hp)), kv_in(4), kv_in(5),
                  q_spec, pl.BlockSpec((SB_BLK, LANES), lambda hp, qb: (qb, cb + hp))],
        out_specs=[q_spec, kv_out, kv_out],
        out_shape=[SDS((t, A_W), bf16), SDS((t, A_W), f32), SDS((t, A_W), f32)],
        compiler_params=_cparams(("parallel", "arbitrary")))(proj, proj, proj, out_b, dout)


def _halo_specs(tr, w, col, nblk):
    per = tr // SUBLANES
    cur = pl.BlockSpec((tr, w), lambda i: (i, col))
    prev = pl.BlockSpec((SUBLANES, w), lambda i: (jnp.maximum(i * per - 1, 0), col))
    nxt = pl.BlockSpec((SUBLANES, w), lambda i: (jnp.minimum((i + 1) * per, nblk * per - 1), col))
    return cur, prev, nxt


def _taps_before(cur, prev8, first):
    prev8 = jnp.where(first, 0.0, prev8)
    ext = jnp.concatenate([prev8, cur], axis=0)
    return [pltpu.roll(ext, s, 0)[SUBLANES:] for s in (3, 2, 1)]


def _taps_after(cur, next8, last):
    n = cur.shape[0]
    next8 = jnp.where(last, 0.0, next8)
    ext = jnp.concatenate([cur, next8], axis=0)
    return [pltpu.roll(ext, n + SUBLANES - s, 0)[:n] for s in (1, 2, 3)]


def _block_diag(x, w_ref, dims):
    outs = [lax.dot_general(x[:, n * LRU_BW:(n + 1) * LRU_BW], w_ref[n], (dims, ((), ())),
                            preferred_element_type=f32) for n in range(LRU_BLOCKS)]
    return jnp.concatenate(outs, axis=1)


def _lru_gates(xc, wa_ref, wi_ref, ba, bi, lam):
    xb = xc.astype(bf16)
    r = jax.nn.sigmoid(_block_diag(xb, wa_ref, NN) + ba)
    ig = jax.nn.sigmoid(_block_diag(xb, wi_ref, NN) + bi)
    sp = jnp.maximum(-lam, 0.0) + jnp.log(1.0 + jnp.exp(-jnp.abs(lam)))
    log_a = -LRU_C * r * sp
    a = jnp.exp(log_a)
    x2 = 2.0 * log_a
    one_minus = jnp.where(x2 > -1e-2, -x2 * (1.0 + x2 * (0.5 + x2 * (1.0 / 6.0))), 1.0 - a * a)
    mult = jnp.sqrt(one_minus)
    return xb, r, ig, sp, a, mult


def _rg_gates_fwd(proj, conv_w, conv_b, wa, wi, ba, bi, lam, tr=256):
    t = proj.shape[0]
    w = D_MODEL
    tr = min(tr, t)
    nblk = t // tr
    cur, prev, _ = _halo_specs(tr, w, 1, nblk)

    def body(x_ref, xp_ref, cw_ref, cb_ref, wa_ref, wi_ref, ba_ref, bi_ref, lam_ref, xc_ref, a_ref, u_ref):
        x = x_ref[...]
        taps = _taps_before(x, xp_ref[...], pl.program_id(0) == 0) + [x]
        xc = cb_ref[...]
        for k in range(4):
            xc = xc + cw_ref[k:k + 1, :] * taps[k]
        _, _, ig, _, a, mult = _lru_gates(xc, wa_ref, wi_ref, ba_ref[...], bi_ref[...], lam_ref[...])
        xc_ref[...] = xc
        a_ref[...] = a
        u_ref[...] = mult * (ig * xc)

    full = lambda a_: pl.BlockSpec(a_.shape, lambda i, nd=a_.ndim: (0,) * nd)
    ospec = pl.BlockSpec((tr, w), lambda i: (i, 0))
    return pl.pallas_call(
        body, grid=(nblk,), name="rg_gates_fwd",
        in_specs=[cur, prev] + [full(a_) for a_ in (conv_w, conv_b, wa, wi, ba, bi, lam)],
        out_specs=[ospec] * 3, out_shape=[SDS((t, w), f32)] * 3,
        compiler_params=_cparams(("parallel",)))(proj, proj, conv_w, conv_b, wa, wi, ba, bi, lam)


def _lru_scan(name, a, b, reverse, tt=512):
    t, w = a.shape
    tt = min(tt, t)
    nt = t // tt
    ng = tt // SUBLANES

    def body(a_ref, b_ref, h_ref, carry_ref):
        @pl.when(pl.program_id(0) == 0)
        def _():
            carry_ref[...] = jnp.zeros_like(carry_ref)

        row = lax.broadcasted_iota(jnp.int32, (SUBLANES, w), 0)

        def group(gi, carry):
            g = (ng - 1 - gi) if reverse else gi
            rows = pl.ds(pl.multiple_of(g * SUBLANES, SUBLANES), SUBLANES)
            av = a_ref[rows, :]
            bv = b_ref[rows, :]
            for s in (1, 2, 4):
                sh = (SUBLANES - s) if reverse else s
                ok = (row < SUBLANES - s) if reverse else (row >= s)
                a_s = pltpu.roll(av, sh, 0)
                b_s = pltpu.roll(bv, sh, 0)
                bv = jnp.where(ok, av * b_s + bv, bv)
                av = jnp.where(ok, av * a_s, av)
            h = av * carry + bv
            h_ref[rows, :] = h
            edge = h[0:1, :] if reverse else h[SUBLANES - 1:SUBLANES, :]
            return jnp.broadcast_to(edge, (SUBLANES, w))

        carry_ref[...] = lax.fori_loop(0, ng, group, carry_ref[...])

    tmap = (lambda i: (nt - 1 - i, 0)) if reverse else (lambda i: (i, 0))
    spec = pl.BlockSpec((tt, w), tmap)
    return pl.pallas_call(
        body, grid=(nt,), name=name, in_specs=[spec, spec], out_specs=spec,
        out_shape=SDS((t, w), f32), scratch_shapes=[pltpu.VMEM((SUBLANES, w), f32)],
        compiler_params=_cparams(("arbitrary",)))(a, b)


def _rg_gates_bwd(dhs, c, hs, xc, wa, wi, ba, bi, lam, tr=256):
    t, w = xc.shape
    tr = min(tr, t)
    nblk = t // tr
    cur, prev, nxt = _halo_specs(tr, w, 0, nblk)

    def body(dhs_ref, c_ref, cn_ref, hs_ref, hp_ref, xc_ref, wa_ref, wi_ref, ba_ref, bi_ref, lam_ref,
             dxc_ref, dwa_ref, dwi_ref, dba_ref, dbi_ref, dlam_ref):
        i = pl.program_id(0)
        c_next = _taps_after(c_ref[...], cn_ref[...], i == nblk - 1)[0]
        h_prev = _taps_before(hs_ref[...], hp_ref[...], i == 0)[2]
        xc = xc_ref[...]
        lam = lam_ref[...]
        xb, r, ig, sp, a, mult = _lru_gates(xc, wa_ref, wi_ref, ba_ref[...], bi_ref[...], lam)
        dh = dhs_ref[...] + c_next
        dlog_a = dh * h_prev * a - (dh * ig * xc) * (a * a / mult)
        dpre_a = (dlog_a * (-LRU_C * sp) * r * (1.0 - r)).astype(bf16)
        dpre_i = (dh * mult * xc * ig * (1.0 - ig)).astype(bf16)
        dxc_ref[...] = (dh * mult * ig + _block_diag(dpre_a, wa_ref, NT) + _block_diag(dpre_i, wi_ref, NT))
        dsig = 1.0 / (1.0 + jnp.exp(lam))
        sums = [jnp.sum(dpre_a.astype(f32), axis=0, keepdims=True),
                jnp.sum(dpre_i.astype(f32), axis=0, keepdims=True),
                jnp.sum(dlog_a * (-LRU_C * r), axis=0, keepdims=True) * (-dsig)]

        @pl.when(i == 0)
        def _():
            dwa_ref[...] = jnp.zeros_like(dwa_ref)
            dwi_ref[...] = jnp.zeros_like(dwi_ref)
            dba_ref[...] = jnp.zeros_like(dba_ref)
            dbi_ref[...] = jnp.zeros_like(dbi_ref)
            dlam_ref[...] = jnp.zeros_like(dlam_ref)

        for n in range(LRU_BLOCKS):
            sl = slice(n * LRU_BW, (n + 1) * LRU_BW)
            dwa_ref[n] += lax.dot_general(xb[:, sl], dpre_a[:, sl], (TN, ((), ())), preferred_element_type=f32)
            dwi_ref[n] += lax.dot_general(xb[:, sl], dpre_i[:, sl], (TN, ((), ())), preferred_element_type=f32)
        dba_ref[...] += sums[0]
        dbi_ref[...] += sums[1]
        dlam_ref[...] += sums[2]

    full = lambda a_: pl.BlockSpec(a_.shape, lambda i, nd=a_.ndim: (0,) * nd)
    vec = pl.BlockSpec((1, w), lambda i: (0, 0))
    mat = pl.BlockSpec((LRU_BLOCKS, LRU_BW, LRU_BW), lambda i: (0, 0, 0))
    return pl.pallas_call(
        body, grid=(nblk,), name="rg_gates_bwd",
        in_specs=[cur, cur, nxt, cur, prev, cur] + [full(a_) for a_ in (wa, wi, ba, bi, lam)],
        out_specs=[cur, mat, mat, vec, vec, vec],
        out_shape=[SDS((t, w), f32), SDS((LRU_BLOCKS, LRU_BW, LRU_BW), f32), SDS((LRU_BLOCKS, LRU_BW, LRU_BW), f32),
                   SDS((1, w), f32), SDS((1, w), f32), SDS((1, w), f32)],
        compiler_params=_cparams(("arbitrary",)))(dhs, c, c, hs, hs, xc, wa, wi, ba, bi, lam)


def _rg_conv_bwd(dxc, proj, conv_w, tr=256):
    t, w = dxc.shape
    tr = min(tr, t)
    nblk = t // tr
    cur, _, nxt = _halo_specs(tr, w, 0, nblk)
    xcur, xprev, _ = _halo_specs(tr, w, 1, nblk)

    def body(d_ref, dn_ref, x_ref, xp_ref, cw_ref, dx_ref, dcw_ref, dcb_ref):
        i = pl.program_id(0)
        d = d_ref[...]
        x = x_ref[...]
        after = _taps_after(d, dn_ref[...], i == nblk - 1)
        before = _taps_before(x, xp_ref[...], i == 0) + [x]
        dx = cw_ref[3:4, :] * d
        for s in (1, 2, 3):
            dx = dx + cw_ref[3 - s:4 - s, :] * after[s - 1]
        dx_ref[...] = dx.astype(bf16)
        dcw = jnp.concatenate([jnp.sum(d * before[k], axis=0, keepdims=True) for k in range(4)], axis=0)
        dcb = jnp.sum(d, axis=0, keepdims=True)

        @pl.when(i == 0)
        def _():
            dcw_ref[...] = dcw
            dcb_ref[...] = dcb

        @pl.when(i > 0)
        def _():
            dcw_ref[...] += dcw
            dcb_ref[...] += dcb

    return pl.pallas_call(
        body, grid=(nblk,), name="rg_conv_bwd",
        in_specs=[cur, nxt, xcur, xprev, pl.BlockSpec((4, w), lambda i: (0, 0))],
        out_specs=[cur, pl.BlockSpec((4, w), lambda i: (0, 0)), pl.BlockSpec((1, w), lambda i: (0, 0))],
        out_shape=[SDS((t, w), bf16), SDS((4, w), f32), SDS((1, w), f32)],
        compiler_params=_cparams(("arbitrary",)))(dxc, dxc, proj, proj, conv_w)


def _attn_fwd(h, wts, j):
    proj = _mm_cols("attn_in", h, wts["attn_w_in"], j, bf16)
    kpad = jnp.pad(proj[:, A_W:2 * A_W], ((PAD_A, 0), (0, 0)))
    vpad = jnp.pad(proj[:, 2 * A_W:3 * A_W], ((PAD_A, 0), (0, 0)))
    bias = _bias_window(wts["attn_rel_bias"][j])
    out_a = _chunk_attn_fwd(proj, kpad, vpad, bias)
    out_b, out_b32 = _sb_fwd(proj)
    m = _mm_rows("attn_out", [out_a, out_b], wts["attn_w_out"], j, f32)
    return m, (proj, kpad, vpad, bias, out_a, out_b, out_b32)


def _attn_bwd(dm, h, saved, wts, j, grads):
    proj, kpad, vpad, bias, out_a, out_b, out_b32 = saved
    dout = _mm_rows_t("attn_out_t", dm, wts["attn_w_out"], j, bf16)
    grads["attn_w_out"] = _mm_wgrad("attn_out_wgrad_a", out_a, dm, grads["attn_w_out"], j, 0)
    grads["attn_w_out"] = _mm_wgrad("attn_out_wgrad_b", out_b, dm, grads["attn_w_out"], j, 1)
    dqa, dka, dva, dbias = _chunk_attn_bwd(proj, kpad, vpad, bias, dout)
    dqs, dks, dvs = _sb_bwd(proj, out_b32, dout)
    grads["attn_rel_bias"][j] = _bias_window_grad(dbias)
    dproj = jnp.concatenate([dqa, dka[PAD_A:].astype(bf16), dva[PAD_A:].astype(bf16),
                             dqs, dks.astype(bf16), dvs.astype(bf16)], axis=1)
    grads["attn_w_in"] = _mm_wgrad_cols("attn_in_wgrad", h, dproj, grads["attn_w_in"], j)
    return _mm_cols_t("attn_in_t", dproj, wts["attn_w_in"], j, f32)


def _rg_fwd(h, wts, j):
    proj = _mm_cols("rg_in", h, wts["rg_w_in"], j, f32)
    small = [wts[k][j] for k in ("rg_conv_w", "rg_conv_b", "rg_w_a", "rg_w_i", "rg_b_a", "rg_b_i", "rg_lambda")]
    xc, a, u = _rg_gates_fwd(proj, *small)
    hs = _lru_scan("lru_scan_fwd", a, u, False)
    yp = _rows("rg_gate_out", lambda hv, gv: hv * _gelu(gv), [hs, (proj, D_MODEL, 0)], [], [(D_MODEL, bf16)])[0]
    m = _mm_rows("rg_out", [yp], wts["rg_w_out"], j, f32)
    return m, (proj, xc, a, hs, yp)


def _rg_bwd(dm, h, saved, wts, j, grads):
    proj, xc, a, hs, yp = saved
    dyp = _mm_rows_t("rg_out_t", dm, wts["rg_w_out"], j, f32)
    grads["rg_w_out"] = _mm_wgrad("rg_out_wgrad", yp, dm, grads["rg_w_out"], j)

    def gate_bwd(dy, hv, gv, av):
        dhs = dy * _gelu(gv)
        return dhs, av * dhs, dy * hv * _gelu_grad(gv)

    dhs, ab, dgate = _rows("rg_gate_out_bwd", gate_bwd, [dyp, hs, (proj, D_MODEL, 0), a], [],
                           [(D_MODEL, f32), (D_MODEL, f32), (D_MODEL, bf16)])
    c = _lru_scan("lru_scan_bwd", a, ab, True)
    wa, wi, ba, bi, lam = [wts[k][j] for k in ("rg_w_a", "rg_w_i", "rg_b_a", "rg_b_i", "rg_lambda")]
    dxc, dwa, dwi, dba, dbi, dlam = _rg_gates_bwd(dhs, c, hs, xc, wa, wi, ba, bi, lam)
    dxr, dcw, dcb = _rg_conv_bwd(dxc, proj, wts["rg_conv_w"][j])
    for k, v in (("rg_w_a", dwa), ("rg_w_i", dwi), ("rg_b_a", dba), ("rg_b_i", dbi), ("rg_lambda", dlam),
                 ("rg_conv_w", dcw), ("rg_conv_b", dcb)):
        grads[k][j] = v
    dproj = jnp.concatenate([dgate, dxr], axis=1)
    grads["rg_w_in"] = _mm_wgrad_cols("rg_in_wgrad", h, dproj, grads["rg_w_in"], j)
    return _mm_cols_t("rg_in_t", dproj, wts["rg_w_in"], j, f32)


def _local_step(x, target, wts):
    t = x.shape[0]
    d = D_MODEL
    gains = {k: wts[k] for k in ("norm_mix_pre", "norm_mix_post", "norm_ffn_pre", "norm_ffn_post")}
    gain = lambda k, l: gains[k][l:l + 1]

    saved = []
    h = _rows("norm_in", _norm_fwd, [x], [gain("norm_mix_pre", 0)], [(d, bf16)])[0]
    loss_cols = None
    for l in range(DEPTH):
        j = l // 2
        m, mix_saved = (_attn_fwd if l % 2 == 0 else _rg_fwd)(h, wts, j)

        def resid_next(xv, mv, g_post, g_next):
            x1 = xv + _norm_fwd(mv, g_post)
            return x1, _norm_fwd(x1, g_next)

        x1, h2 = _rows("resid_mix", resid_next, [x, m], [gain("norm_mix_post", l), gain("norm_ffn_pre", l)],
                       [(d, f32), (d, bf16)])
        g, u, hid = _ffn_up(h2, wts["ffn_w_gate"], wts["ffn_w_up"], l)
        f = _ffn_down(hid, wts["ffn_w_down"], l)
        saved.append((x, h, m, mix_saved, x1, h2, g, u, hid, f))
        if l + 1 < DEPTH:
            x, h = _rows("resid_ffn", resid_next, [x1, f], [gain("norm_ffn_post", l), gain("norm_mix_pre", l + 1)],
                         [(d, f32), (d, bf16)])
        else:
            def resid_loss(xv, fv, tv, g_post):
                err = xv + _norm_fwd(fv, g_post) - tv
                return err * (1.0 / d), jnp.sum(err * err, axis=0, keepdims=True)

            dx, loss_cols = _rows("resid_loss", resid_loss, [x1, f, target], [gain("norm_ffn_post", l)],
                                  [(d, f32)], [((1, d), f32)])
    loss = 0.5 * jnp.sum(loss_cols) / d

    grads = {k: {} for k in SMALL_GRADS}
    for k in BIG_GRADS:
        shp = wts[k].shape
        grads[k] = jnp.zeros((shp[0],) + shp[2:] if shp[1] == 1 else shp, f32)

    def norm_bwd_cast(uv, dyv, gv):
        du, dg = _norm_bwd(uv, dyv, gv)
        return du, dg

    def norm_bwd_resid(uv, dhv, dxv, gv):
        du, dg = _norm_bwd(uv, dhv, gv)
        return dxv + du, dg

    for l in reversed(range(DEPTH)):
        j = l // 2
        x_in, h, m, mix_saved, x1, h2, g, u, hid, f = saved[l]
        df, grads["norm_ffn_post"][l] = _rows("norm_ffn_post_bwd", norm_bwd_cast, [f, dx], [gain("norm_ffn_post", l)],
                                              [(d, bf16)], [((1, d), f32)])
        dg, du = _ffn_down_bwd(df, wts["ffn_w_down"], l, g, u)
        grads["ffn_w_down"] = _ffn_wgrad_down(hid, df, grads["ffn_w_down"], l)
        dh2 = _ffn_up_bwd(dg, du, wts["ffn_w_gate"], wts["ffn_w_up"], l)
        grads["ffn_w_gate"] = _ffn_wgrad_up("ffn_wgrad_gate", h2, dg, grads["ffn_w_gate"], l)
        grads["ffn_w_up"] = _ffn_wgrad_up("ffn_wgrad_up", h2, du, grads["ffn_w_up"], l)
        dx1, grads["norm_ffn_pre"][l] = _rows("norm_ffn_pre_bwd", norm_bwd_resid, [x1, dh2, dx],
                                              [gain("norm_ffn_pre", l)], [(d, f32)], [((1, d), f32)])
        dm, grads["norm_mix_post"][l] = _rows("norm_mix_post_bwd", norm_bwd_cast, [m, dx1], [gain("norm_mix_post", l)],
                                              [(d, bf16)], [((1, d), f32)])
        dh = (_attn_bwd if l % 2 == 0 else _rg_bwd)(dm, h, mix_saved, wts, j, grads)
        dx, grads["norm_mix_pre"][l] = _rows("norm_mix_pre_bwd", norm_bwd_resid, [x_in, dh, dx1],
                                             [gain("norm_mix_pre", l)], [(d, f32)], [((1, d), f32)])
    return loss, dx, grads


ANY = pl.BlockSpec(memory_space=pl.ANY)
PACK_COLS = 1024
SMALL_ROWS = 288


def _mesh_pos():
    x, y, c = lax.axis_index("x"), lax.axis_index("y"), lax.axis_index("c")
    return x, y, c, [(1 - x, y), (x, 1 - y), (1 - x, 1 - y)]


def _run_copies(copies):
    for cp in copies:
        cp.start()
    for cp in copies:
        cp.wait()


def _all_gather(shards):
    n = len(shards)
    per = 7

    def body(*refs):
        ins, outs = refs[:n], refs[n:2 * n]
        send, recv = refs[2 * n:]
        x, y, c, chips = _mesh_pos()
        q = 2 * x + y
        sibling = (x, y, 1 - c)

        def copy(k, src, dst, to):
            return pltpu.make_async_remote_copy(src_ref=src, dst_ref=dst, send_sem=send.at[k], recv_sem=recv.at[k],
                                                device_id=to, device_id_type=MESH)

        own, sent, passed = [], [], []
        for t in range(n):
            half = ins[t].shape[1] // 2
            rows = pl.ds(pl.multiple_of(c * half, half), half)
            own.append(copy(per * t, ins[t], outs[t].at[:, q], sibling))
            for j, (px, py) in enumerate(chips):
                sent.append(copy(per * t + 1 + j, ins[t].at[:, rows], outs[t].at[:, q, rows], (px, py, c)))
        for cp in own + sent:
            cp.start()
        for t in range(n):
            half = ins[t].shape[1] // 2
            rows = pl.ds(pl.multiple_of(c * half, half), half)
            for j, (px, py) in enumerate(chips):
                landed = outs[t].at[:, 2 * px + py, rows]
                copy(per * t + 1 + j, landed, landed, sibling).wait_recv()
                passed.append(copy(per * t + 4 + j, landed, landed, sibling))
                passed[-1].start()
        for cp in sent:
            cp.wait_send()
        for cp in own + passed:
            cp.wait()

    return pl.pallas_call(
        body, name="weight_all_gather", in_specs=[ANY] * n, out_specs=[ANY] * n,
        out_shape=[SDS((s.shape[0], N_CHIPS) + s.shape[1:], s.dtype) for s in shards],
        scratch_shapes=[pltpu.SemaphoreType.DMA((per * n,)), pltpu.SemaphoreType.DMA((per * n,))])(*shards)


def _pair_exchange(gs):
    n = len(gs)

    def body(*refs):
        ins, outs = refs[:n], refs[n:2 * n]
        send, recv = refs[2 * n:]
        x, y, c, _ = _mesh_pos()
        copies = []
        for t in range(n):
            half = ins[t].shape[2] // 2
            src = ins[t].at[:, :, pl.ds(pl.multiple_of((1 - c) * half, SUBLANES), half)]
            copies.append(pltpu.make_async_remote_copy(
                src_ref=src, dst_ref=outs[t], send_sem=send.at[t], recv_sem=recv.at[t],
                device_id=(x, y, 1 - c), device_id_type=MESH))
        _run_copies(copies)

    return pl.pallas_call(
        body, name="grad_pair_exchange", in_specs=[ANY] * n, out_specs=[ANY] * n,
        out_shape=[SDS(g.shape[:2] + (g.shape[2] // 2, g.shape[3]), f32) for g in gs],
        scratch_shapes=[pltpu.SemaphoreType.DMA((n,)), pltpu.SemaphoreType.DMA((n,))])(*gs)


def _pair_sum(name, g, got, c):
    l, s, r, cols = g.shape

    def body(c_ref, a_ref, b_ref, o_ref):
        o_ref[...] = (a_ref[...] + b_ref[...]).astype(bf16)

    blk = (None, None, r // 2, cols)
    return pl.pallas_call(
        body, name=name, out_shape=SDS(got.shape, bf16),
        grid_spec=pltpu.PrefetchScalarGridSpec(
            num_scalar_prefetch=1, grid=(l, s),
            in_specs=[pl.BlockSpec(blk, lambda i, q, c_ref: (i, q, c_ref[0], 0)),
                      pl.BlockSpec(blk, lambda i, q, c_ref: (i, q, 0, 0))],
            out_specs=pl.BlockSpec(blk, lambda i, q, c_ref: (i, q, 0, 0))),
        compiler_params=_cparams(("parallel", "parallel")))(c, g, got)


def _chip_exchange(hs):
    n = len(hs)

    def body(*refs):
        ins, outs = refs[:n], refs[n:2 * n]
        send, recv = refs[2 * n:]
        x, y, c, chips = _mesh_pos()
        q = 2 * x + y
        copies = []
        for t in range(n):
            for j, (px, py) in enumerate(chips):
                copies.append(pltpu.make_async_remote_copy(
                    src_ref=ins[t].at[:, 2 * px + py], dst_ref=outs[t].at[:, q], send_sem=send.at[3 * t + j],
                    recv_sem=recv.at[3 * t + j], device_id=(px, py, c), device_id_type=MESH))
        _run_copies(copies)

    return pl.pallas_call(
        body, name="grad_chip_exchange", in_specs=[ANY] * n, out_specs=[ANY] * n,
        out_shape=[SDS(h.shape, h.dtype) for h in hs],
        scratch_shapes=[pltpu.SemaphoreType.DMA((3 * n,)), pltpu.SemaphoreType.DMA((3 * n,))])(*hs)


def _chip_sum(name, s, h, pos):
    l, _, r, cols = s.shape

    def body(pos_ref, s0, s1, s2, s3, own_ref, o_ref):
        vals = [jnp.where(pos_ref[0] == p, own_ref[...], ref[...]).astype(f32) for p, ref in enumerate((s0, s1, s2, s3))]
        o_ref[...] = ((vals[0] + vals[1]) + vals[2]) + vals[3]

    blk = (None, None, r, cols)
    slot = lambda p: pl.BlockSpec(blk, lambda i, pos_ref: (i, jnp.where(pos_ref[0] == p, (p + 1) % N_CHIPS, p), 0, 0))
    return pl.pallas_call(
        body, name=name, out_shape=SDS((l, 2 * r, cols), f32),
        grid_spec=pltpu.PrefetchScalarGridSpec(
            num_scalar_prefetch=1, grid=(l,),
            in_specs=[slot(p) for p in range(N_CHIPS)] + [pl.BlockSpec(blk, lambda i, pos_ref: (i, pos_ref[0], 0, 0))],
            out_specs=pl.BlockSpec((None, r, cols), lambda i, pos_ref: (i, pos_ref[1], 0))),
        compiler_params=_cparams(("parallel",)))(pos, s, s, s, s, h)


def _pair_gather(fulls):
    n = len(fulls)

    def body(*refs):
        ins, outs = refs[:n], refs[n:2 * n]
        send, recv = refs[2 * n:]
        x, y, c, _ = _mesh_pos()
        copies = []
        for t in range(n):
            half = outs[t].shape[1] // 2
            rows = outs[t].at[:, pl.ds(pl.multiple_of(c * half, SUBLANES), half)]
            copies.append(pltpu.make_async_remote_copy(
                src_ref=rows, dst_ref=rows, send_sem=send.at[t], recv_sem=recv.at[t],
                device_id=(x, y, 1 - c), device_id_type=MESH))
        _run_copies(copies)

    return pl.pallas_call(
        body, name="grad_pair_gather", in_specs=[ANY] * n, out_specs=[ANY] * n,
        out_shape=[SDS(f.shape, f32) for f in fulls], input_output_aliases={t: t for t in range(n)},
        scratch_shapes=[pltpu.SemaphoreType.DMA((n,)), pltpu.SemaphoreType.DMA((n,))])(*fulls)


COL_SHARDED = ("attn_w_in", "rg_w_in", "ffn_w_gate", "ffn_w_up")
ROW_SHARDED = ("attn_w_out", "rg_w_out")
GATES = ("rg_w_a", "rg_w_i")
VECTORS = ("rg_conv_w", "rg_conv_b", "rg_b_a", "rg_b_i", "rg_lambda")
REPLICATED = ("norm_mix_pre", "norm_mix_post", "norm_ffn_pre", "norm_ffn_post", "attn_rel_bias")
BIG_GRADS = COL_SHARDED + ROW_SHARDED + ("ffn_w_down",)
SMALL_GRADS = GATES + VECTORS + REPLICATED
WEIGHTS =("attn_w_in", "attn_rel_bias", "attn_w_out", "rg_w_in", "rg_conv_w", "rg_conv_b", "rg_w_a", "rg_b_a",
           "rg_w_i", "rg_b_i", "rg_lambda", "rg_w_out", "norm_mix_pre", "norm_mix_post", "norm_ffn_pre",
           "norm_ffn_post", "ffn_w_gate", "ffn_w_up", "ffn_w_down")
SMALL = VECTORS + REPLICATED


def _gather_weights(w):
    big = list(COL_SHARDED + ROW_SHARDED + GATES + ("ffn_w_down",))
    shards = []
    for k in big:
        a = w[k].astype(bf16)
        shards.append(a.reshape((-1,) + a.shape[-2:]))
    vec = jnp.concatenate([w[k].reshape(-1) for k in VECTORS])
    shards.append(vec.reshape(1, -1, LANES))
    got = dict(zip(big + ["vec"], _all_gather(shards)))
    out = {k: w[k] for k in REPLICATED}
    for k in COL_SHARDED + ("ffn_w_down",):
        out[k] = got[k]
    for k in ROW_SHARDED:
        l, s, ks, n = got[k].shape
        out[k] = got[k].reshape(l, 1, s * ks, n)
    for k in GATES:
        out[k] = got[k].reshape(2, LRU_BLOCKS, LRU_BW, LRU_BW)
    vec = got["vec"].reshape(N_CHIPS, -1)
    off = 0
    for k in VECTORS:
        shp = w[k].shape
        n = int(np.prod(shp))
        piece = vec[:, off:off + n].reshape((N_CHIPS,) + shp)
        off += n
        if k == "rg_conv_w":
            out[k] = piece.reshape(N_CHIPS, 2, 4, 256).transpose(1, 2, 0, 3).reshape(2, 4, D_MODEL)
        elif k in ("rg_b_a", "rg_b_i"):
            out[k] = piece.transpose(1, 2, 0, 3).reshape(2, 1, D_MODEL)
        else:
            out[k] = piece.transpose(1, 0, 2).reshape(2, 1, D_MODEL)
    return out


def _grad_blocks(name, g):
    st = jnp.stack([g[i] for i in sorted(g)])
    if name in GATES:
        st = st.reshape(2, LRU_BLOCKS, N_CHIPS, LRU_BW // N_CHIPS, LRU_BW).transpose(2, 0, 1, 3, 4)
    elif name == "rg_conv_w":
        st = st.reshape(2, 4, N_CHIPS, -1).transpose(2, 0, 1, 3)
    elif name in ("rg_b_a", "rg_b_i"):
        st = st.reshape(2, LRU_BLOCKS, N_CHIPS, -1).transpose(2, 0, 1, 3)
    elif name in VECTORS:
        st = st.reshape(2, N_CHIPS, -1).transpose(1, 0, 2)
    else:
        st = jnp.broadcast_to(st.reshape(1, -1), (N_CHIPS, st.size))
    return st.reshape(N_CHIPS, -1)


def _reduce_gradients(grads, shard_shapes):
    gs = []
    for k in BIG_GRADS:
        g = grads[k]
        if g.ndim == 3:
            g = g.reshape(g.shape[0], N_CHIPS, g.shape[1] // N_CHIPS, g.shape[2])
        gs.append(g)
    blocks = [_grad_blocks(k, grads[k]) for k in SMALL_GRADS]
    used = sum(b.shape[1] for b in blocks)
    small = jnp.concatenate(blocks + [jnp.zeros((N_CHIPS, SMALL_ROWS * PACK_COLS - used), f32)], axis=1)
    gs.append(small.reshape(1, N_CHIPS, SMALL_ROWS, PACK_COLS))
    names = BIG_GRADS + ("small",)
    c = lax.axis_index("c").astype(jnp.int32).reshape(1)
    pos = jnp.stack([2 * lax.axis_index("x") + lax.axis_index("y"), lax.axis_index("c")]).astype(jnp.int32)
    parts = [_pair_sum("grad_pair_sum_" + k, g, r, c) for k, g, r in zip(names, gs, _pair_exchange(gs))]
    slots = _chip_exchange(parts)
    full = _pair_gather([_chip_sum("grad_chip_sum_" + k, s, h, pos) for k, s, h in zip(names, slots, parts)])
    out = {k: f.reshape(shard_shapes[k]) for k, f in zip(BIG_GRADS, full)}
    flat, off = full[-1].reshape(-1), 0
    for k in SMALL_GRADS:
        n = int(np.prod(shard_shapes[k]))
        out[k] = flat[off:off + n].reshape(shard_shapes[k])
        off += n
    return out


def _adamw_fn(w, g, m, v):
    m = ADAM_B1 * m + (1.0 - ADAM_B1) * g
    v = ADAM_B2 * v + (1.0 - ADAM_B2) * (g * g)
    m_hat = m / (1.0 - ADAM_B1 ** ADAM_STEP)
    v_hat = v / (1.0 - ADAM_B2 ** ADAM_STEP)
    return -ADAM_LR * (m_hat / (jnp.sqrt(v_hat) + ADAM_EPS) + ADAM_WD * w), m, v


def _adamw(name, w, g, m, v):
    shp = w.shape
    if w.size >= 1 << 16:
        width = shp[-1]
        ops = [a.reshape(-1, width) for a in (w, g, m, v)]
        res = _rows(name, _adamw_fn, ops, [], [(width, f32)] * 3)
        return [r.reshape(shp) for r in res]
    n = w.size
    rows = -(-n // (SUBLANES * LANES)) * SUBLANES
    ops = [jnp.pad(a.reshape(-1), (0, rows * LANES - n)).reshape(rows, LANES) for a in (w, g, m, v)]
    res = _rows(name, _adamw_fn, ops, [], [(LANES, f32)] * 3, tr=rows)
    return [r.reshape(-1)[:n].reshape(shp) for r in res]


def kernel(x, attn_w_in, attn_rel_bias, attn_w_out, rg_w_in, rg_conv_w, rg_conv_b, rg_w_a, rg_b_a, rg_w_i, rg_b_i, rg_lambda, rg_w_out, norm_mix_pre, norm_mix_post, norm_ffn_pre, norm_ffn_post, ffn_w_gate, ffn_w_up, ffn_w_down, loss_target, m_attn_w_in, m_attn_rel_bias, m_attn_w_out, m_rg_w_in, m_rg_conv_w, m_rg_conv_b, m_rg_w_a, m_rg_b_a, m_rg_w_i, m_rg_b_i, m_rg_lambda, m_rg_w_out, m_norm_mix_pre, m_norm_mix_post, m_norm_ffn_pre, m_norm_ffn_post, m_ffn_w_gate, m_ffn_w_up, m_ffn_w_down, v_attn_w_in, v_attn_rel_bias, v_attn_w_out, v_rg_w_in, v_rg_conv_w, v_rg_conv_b, v_rg_w_a, v_rg_b_a, v_rg_w_i, v_rg_b_i, v_rg_lambda, v_rg_w_out, v_norm_mix_pre, v_norm_mix_post, v_norm_ffn_pre, v_norm_ffn_post, v_ffn_w_gate, v_ffn_w_up, v_ffn_w_down):
    w = dict(zip(WEIGHTS, (attn_w_in, attn_rel_bias, attn_w_out, rg_w_in, rg_conv_w, rg_conv_b, rg_w_a, rg_b_a, rg_w_i,
                           rg_b_i, rg_lambda, rg_w_out, norm_mix_pre, norm_mix_post, norm_ffn_pre, norm_ffn_post,
                           ffn_w_gate, ffn_w_up, ffn_w_down)))
    m = dict(zip(WEIGHTS, (m_attn_w_in, m_attn_rel_bias, m_attn_w_out, m_rg_w_in, m_rg_conv_w, m_rg_conv_b, m_rg_w_a,
                           m_rg_b_a, m_rg_w_i, m_rg_b_i, m_rg_lambda, m_rg_w_out, m_norm_mix_pre, m_norm_mix_post,
                           m_norm_ffn_pre, m_norm_ffn_post, m_ffn_w_gate, m_ffn_w_up, m_ffn_w_down)))
    v = dict(zip(WEIGHTS, (v_attn_w_in, v_attn_rel_bias, v_attn_w_out, v_rg_w_in, v_rg_conv_w, v_rg_conv_b, v_rg_w_a,
                           v_rg_b_a, v_rg_w_i, v_rg_b_i, v_rg_lambda, v_rg_w_out, v_norm_mix_pre, v_norm_mix_post,
                           v_norm_ffn_pre, v_norm_ffn_post, v_ffn_w_gate, v_ffn_w_up, v_ffn_w_down)))
    wts = _gather_weights(w)
    loss, dx, grads = _local_step(x[0], loss_target[0], wts)
    loss = lax.psum(loss, ("x", "y", "c"))
    g = _reduce_gradients(grads, {k: w[k].shape for k in WEIGHTS})

    big = [k for k in WEIGHTS if k not in SMALL]
    upd = {k: _adamw("adamw_" + k, w[k], g[k], m[k], v[k]) for k in big}
    cat = lambda d: jnp.concatenate([d[k].reshape(-1) for k in SMALL])
    small = _adamw("adamw_small", cat(w), cat(g), cat(m), cat(v))
    off = 0
    for k in SMALL:
        n = w[k].size
        upd[k] = [r[off:off + n].reshape(w[k].shape) for r in small]
        off += n
    return (loss, dx[None], *[g[k] for k in WEIGHTS], *[upd[k][0] for k in WEIGHTS],
            *[upd[k][1] for k in WEIGHTS], *[upd[k][2] for k in WEIGHTS])
```

```python
import functools

import numpy as np
import jax
import jax.numpy as jnp
from jax import lax
from jax.experimental import pallas as pl
from jax.experimental.pallas import tpu as pltpu

f32 = jnp.float32
bf16 = jnp.bfloat16
SDS = jax.ShapeDtypeStruct
MESH = pl.DeviceIdType.MESH

D_MODEL = 1024
N_CHIPS = 4
DEPTH = 4
HEAD_DIM = 64
CHUNK = 64
N_LEFT = 8
REL_CLIP = 256
A_W = 512
LRU_BLOCKS = 4
LRU_BW = 256
LRU_C = 8.0
D_FF = 2816
RMS_EPS = 1e-6
LANES = 128
SUBLANES = 8
VMEM_LIMIT = 56 * 1024 * 1024

QB_A = 2 * CHUNK
QSUB_A = 2
KW_A = QB_A + N_LEFT * CHUNK
PAD_A = N_LEFT * CHUNK
EXT_A = 768
SB_BLK = 256
SB_DEAD = -110.0

ADAM_LR, ADAM_B1, ADAM_B2, ADAM_EPS, ADAM_WD, ADAM_STEP = 0.001, 0.9, 0.999, 1e-08, 0.01, 10


def _cparams(sem):
    return pltpu.CompilerParams(dimension_semantics=sem, vmem_limit_bytes=VMEM_LIMIT)


def _gemm(name, operands, in_specs, o_spec, out_shape, grid, dims, acc_shape, into=None):
    nred = grid[2]
    npair = len(operands) // 2
    nin = 2 * npair + (into is not None)

    def body(*refs):
        o_ref = refs[nin]
        p = None
        for t in range(npair):
            d = lax.dot_general(refs[2 * t][...], refs[2 * t + 1][...], (dims, ((), ())),
                                preferred_element_type=f32)
            p = d if p is None else p + d
        if nred == 1:
            o_ref[...] = p.astype(o_ref.dtype)
        else:
            acc = refs[nin + 1]
            r = pl.program_id(2)

            @pl.when(r == 0)
            def _():
                acc[...] = p

            @pl.when(r > 0)
            def _():
                acc[...] += p

            @pl.when(r == nred - 1)
            def _():
                o_ref[...] = acc[...].astype(o_ref.dtype)

    scratch = [] if nred == 1 else [pltpu.VMEM(acc_shape, f32)]
    extra, alias = ([], {}) if into is None else ([into], {2 * npair: 0})
    return pl.pallas_call(
        body, grid=grid, in_specs=list(in_specs) + [pl.BlockSpec(memory_space=pl.ANY)] * len(extra),
        out_specs=o_spec, out_shape=out_shape, scratch_shapes=scratch, name=name, input_output_aliases=alias,
        compiler_params=_cparams(("parallel", "parallel", "arbitrary")))(*operands, *extra)


NN = ((1,), (0,))
NT = ((1,), (1,))
TN = ((0,), (0,))


def _tile(t, want=1024):
    return min(want, t)


def _mm_cols(name, a, w, l, out_dtype):
    t, k = a.shape
    _, s, _, ns = w.shape
    tm = _tile(t)
    return _gemm(
        name, [a, w],
        [pl.BlockSpec((tm, k), lambda i, j, r: (i, 0)),
         pl.BlockSpec((None, None, k, ns), lambda i, j, r: (l, j, 0, 0))],
        pl.BlockSpec((tm, ns), lambda i, j, r: (i, j)),
        SDS((t, s * ns), out_dtype), (t // tm, s, 1), NN, None)


def _mm_cols_t(name, dy, w, l, out_dtype):
    t = dy.shape[0]
    _, s, k, ns = w.shape
    tm = _tile(t)
    return _gemm(
        name, [dy, w],
        [pl.BlockSpec((tm, ns), lambda i, j, r: (i, r)),
         pl.BlockSpec((None, None, k, ns), lambda i, j, r: (l, r, 0, 0))],
        pl.BlockSpec((tm, k), lambda i, j, r: (i, 0)),
        SDS((t, k), out_dtype), (t // tm, 1, s), NT, (tm, k))


def _mm_wgrad_cols(name, a, dy, buf, l):
    t, k = a.shape
    _, s, _, ns = buf.shape
    tt = _tile(t)
    return _gemm(
        name, [a, dy],
        [pl.BlockSpec((tt, k), lambda i, j, r: (r, 0)),
         pl.BlockSpec((tt, ns), lambda i, j, r: (r, i))],
        pl.BlockSpec((None, None, k, ns), lambda i, j, r: (l, i, 0, 0)),
        SDS(buf.shape, f32), (s, 1, t // tt), TN, (k, ns), into=buf)


def _mm_rows(name, parts, w, l, out_dtype):
    t = parts[0].shape[0]
    n = w.shape[3]
    tm = _tile(t)
    ops, specs = [], []
    for p_i, a in enumerate(parts):
        kp = a.shape[1]
        ops += [a, w]
        specs += [pl.BlockSpec((tm, kp), lambda i, j, r: (i, 0)),
                  pl.BlockSpec((None, None, kp, n), lambda i, j, r, p_i=p_i: (l, 0, p_i, 0))]
    return _gemm(name, ops, specs, pl.BlockSpec((tm, n), lambda i, j, r: (i, 0)),
                 SDS((t, n), out_dtype), (t // tm, 1, 1), NN, None)


def _mm_rows_t(name, dy, w, l, out_dtype):
    t, n = dy.shape
    k = w.shape[2]
    tm = _tile(t)
    return _gemm(
        name, [dy, w],
        [pl.BlockSpec((tm, n), lambda i, j, r: (i, 0)),
         pl.BlockSpec((None, None, k, n), lambda i, j, r: (l, 0, 0, 0))],
        pl.BlockSpec((tm, k), lambda i, j, r: (i, 0)),
        SDS((t, k), out_dtype), (t // tm, 1, 1), NT, None)


def _mm_wgrad(name, a, dy, buf, l, part=0):
    t, k = a.shape
    n = dy.shape[1]
    tt = _tile(t)
    return _gemm(
        name, [a, dy],
        [pl.BlockSpec((tt, k), lambda i, j, r: (r, 0)),
         pl.BlockSpec((tt, n), lambda i, j, r: (r, 0))],
        pl.BlockSpec((None, k, n), lambda i, j, r: (l, part, 0)),
        SDS(buf.shape, f32), (1, 1, t // tt), TN, (k, n), into=buf)


def _ffn_up(h, wg, wu, l, gather):
    t, k = h.shape
    s, fs = wg.shape[1], wg.shape[3]
    tm = _tile(t)

    def body(h_ref, wg_ref, wu_ref, g_ref, u_ref, hid_ref):
        hv = h_ref[...]
        g = jnp.dot(hv, wg_ref[...], preferred_element_type=f32)
        u = jnp.dot(hv, wu_ref[...], preferred_element_type=f32)
        g_ref[...] = g.astype(bf16)
        u_ref[...] = u.astype(bf16)
        hid_ref[...] = (g * jax.nn.sigmoid(g) * u).astype(bf16)

    wspec = pl.BlockSpec((None, None, k, fs), lambda j, i: (l, j, 0, 0))
    ospec = pl.BlockSpec((None, tm, fs), lambda j, i: (j, i, 0))
    return _call(
        body, [h, wg, wu], grid=(s, t // tm), name="ffn_up",
        in_specs=[pl.BlockSpec((tm, k), lambda j, i: (i, 0)), wspec, wspec],
        out_specs=[ospec, ospec, ospec], out_shape=[SDS((s, t, fs), bf16)] * 3,
        sem=("parallel", "parallel"), gather=gather)


def _ffn_down(hid, wd, l):
    s, t, fs = hid.shape
    n = wd.shape[3]
    tm = _tile(t, 512)
    ops, specs = [], []
    for r in range(s):
        ops += [hid, wd]
        specs += [pl.BlockSpec((None, tm, fs), lambda i, j, k, r=r: (r, i, 0)),
                  pl.BlockSpec((None, None, fs, n), lambda i, j, k, r=r: (l, r, 0, 0))]
    return _gemm("ffn_down", ops, specs, pl.BlockSpec((tm, n), lambda i, j, k: (i, 0)),
                 SDS((t, n), f32), (t // tm, 1, 1), NN, None)


def _ffn_down_bwd(df, wd, l, g, u):
    t, n = df.shape
    s, fs = wd.shape[1], wd.shape[2]
    tm = _tile(t)

    def body(df_ref, wd_ref, g_ref, u_ref, dg_ref, du_ref):
        dh = lax.dot_general(df_ref[...], wd_ref[...], (NT, ((), ())), preferred_element_type=f32)
        gv = g_ref[...].astype(f32)
        uv = u_ref[...].astype(f32)
        sg = jax.nn.sigmoid(gv)
        du_ref[...] = (dh * gv * sg).astype(bf16)
        dg_ref[...] = (dh * uv * (sg * (1.0 + gv * (1.0 - sg)))).astype(bf16)

    bspec = pl.BlockSpec((None, tm, fs), lambda j, i: (j, i, 0))
    return pl.pallas_call(
        body, grid=(s, t // tm), name="ffn_down_bwd",
        in_specs=[pl.BlockSpec((tm, n), lambda j, i: (i, 0)),
                  pl.BlockSpec((None, None, fs, n), lambda j, i: (l, j, 0, 0)), bspec, bspec],
        out_specs=[bspec, bspec], out_shape=[SDS((s, t, fs), bf16)] * 2,
        compiler_params=_cparams(("parallel", "parallel")))(df, wd, g, u)


def _ffn_up_bwd(dg, du, wg, wu, l):
    s, t, fs = dg.shape
    k = wg.shape[2]
    tm = _tile(t, 512)
    ops, specs = [], []
    for r in range(s):
        aspec = pl.BlockSpec((None, tm, fs), lambda i, j, kk, r=r: (r, i, 0))
        wspec = pl.BlockSpec((None, None, k, fs), lambda i, j, kk, r=r: (l, r, 0, 0))
        ops += [dg, wg, du, wu]
        specs += [aspec, wspec, aspec, wspec]
    return _gemm("ffn_up_bwd", ops, specs, pl.BlockSpec((tm, k), lambda i, j, kk: (i, 0)),
                 SDS((t, k), f32), (t // tm, 1, 1), NT, None)


def _ffn_wgrad_up(name, h, dy, buf, l):
    t, k = h.shape
    s, _, fs = dy.shape
    tt = _tile(t)
    return _gemm(
        name, [h, dy],
        [pl.BlockSpec((tt, k), lambda i, j, r: (r, 0)),
         pl.BlockSpec((None, tt, fs), lambda i, j, r: (i, r, 0))],
        pl.BlockSpec((None, None, k, fs), lambda i, j, r: (l, i, 0, 0)),
        SDS(buf.shape, f32), (s, 1, t // tt), TN, (k, fs), into=buf)


def _ffn_wgrad_down(hid, df, buf, l):
    s, t, fs = hid.shape
    n = df.shape[1]
    tt = _tile(t)
    return _gemm(
        "ffn_wgrad_down", [hid, df],
        [pl.BlockSpec((None, tt, fs), lambda i, j, r: (i, r, 0)),
         pl.BlockSpec((tt, n), lambda i, j, r: (r, 0))],
        pl.BlockSpec((None, None, fs, n), lambda i, j, r: (l, i, 0, 0)),
        SDS(buf.shape, f32), (s, 1, t // tt), TN, (fs, n), into=buf)


def _rows(name, fn, rows, consts, row_outs, acc_outs=(), tr=256):
    rows = [r if isinstance(r, tuple) else (r, r.shape[1], 0) for r in rows]
    t = rows[0][0].shape[0]
    tr = min(tr, t)
    nin = len(rows) + len(consts)
    no, na = len(row_outs), len(acc_outs)

    def body(*refs):
        vals = fn(*[r[...] for r in refs[:nin]])
        if not isinstance(vals, (tuple, list)):
            vals = (vals,)
        for k in range(no):
            refs[nin + k][...] = vals[k].astype(refs[nin + k].dtype)
        first = pl.program_id(0) == 0
        for k in range(na):
            ref, val = refs[nin + no + k], vals[no + k]

            @pl.when(first)
            def _(ref=ref, val=val):
                ref[...] = val

            @pl.when(jnp.logical_not(first))
            def _(ref=ref, val=val):
                ref[...] += val

    in_specs = [pl.BlockSpec((tr, w), lambda i, cb=cb: (i, cb)) for (_, w, cb) in rows]
    in_specs += [pl.BlockSpec(c.shape, lambda i, nd=c.ndim: (0,) * nd) for c in consts]
    out_specs = [pl.BlockSpec((tr, w), lambda i: (i, 0)) for (w, _) in row_outs]
    out_specs += [pl.BlockSpec(s, lambda i, nd=len(s): (0,) * nd) for (s, _) in acc_outs]
    out_shape = [SDS((t, w), dt) for (w, dt) in row_outs] + [SDS(s, dt) for (s, dt) in acc_outs]
    res = pl.pallas_call(
        body, grid=(t // tr,), in_specs=in_specs, out_specs=out_specs, out_shape=out_shape,
        name=name, compiler_params=_cparams(("arbitrary",)))(*[r[0] for r in rows], *consts)
    return res


def _rstd(x):
    return lax.rsqrt(jnp.mean(x * x, axis=-1, keepdims=True) + RMS_EPS)


def _norm_fwd(x, g):
    return x * _rstd(x) * g


def _norm_bwd(u, dy, g):
    r = _rstd(u)
    n = u * r
    dn = dy * g
    du = r * (dn - n * jnp.mean(dn * n, axis=-1, keepdims=True))
    return du, jnp.sum(dy * n, axis=0, keepdims=True)


def _gelu(x):
    c = 0.7978845608028654
    return 0.5 * x * (1.0 + jnp.tanh(c * (x + 0.044715 * x * x * x)))


def _gelu_grad(x):
    c = 0.7978845608028654
    th = jnp.tanh(c * (x + 0.044715 * x * x * x))
    return 0.5 * (1.0 + th) + 0.5 * x * (1.0 - th * th) * c * (1.0 + 3.0 * 0.044715 * x * x)


def _mask_heads(x):
    lane = lax.broadcasted_iota(jnp.int32, x.shape, 1)
    return [jnp.where((lane >= h * HEAD_DIM) & (lane < (h + 1) * HEAD_DIM), x, jnp.zeros_like(x))
            for h in range(LANES // HEAD_DIM)]


def _chunk_valid(start):
    qi = lax.broadcasted_iota(jnp.int32, (QB_A, KW_A), 0)
    kj = lax.broadcasted_iota(jnp.int32, (QB_A, KW_A), 1)
    qc = qi // CHUNK
    kc = kj // CHUNK
    return (kc >= qc) & (kc <= qc + N_LEFT) & (kj + start >= PAD_A)


def _chunk_probs(q, k, bias, valid):
    s = lax.dot_general(q, k, (NT, ((), ())), preferred_element_type=f32) * (HEAD_DIM ** -0.5) + bias
    s = jnp.where(valid, s, -1e30)
    p = jnp.exp(s - jnp.max(s, axis=-1, keepdims=True))
    return p / jnp.sum(p, axis=-1, keepdims=True)


def _chunk_attn_fwd(proj, kpad, vpad, bias, gather):
    t = proj.shape[0]
    tp = kpad.shape[0]
    step = QSUB_A * QB_A

    def body(q_ref, k_ref, v_ref, b_ref, o_ref):
        for sb in range(QSUB_A):
            start = pl.multiple_of((pl.program_id(1) * QSUB_A + sb) * QB_A, QB_A)
            rows = pl.ds(sb * QB_A, QB_A)
            valid = _chunk_valid(start)
            kw = k_ref[pl.ds(start, KW_A), :]
            qm = _mask_heads(q_ref[rows, :])
            vm = _mask_heads(v_ref[pl.ds(start, KW_A), :])
            o = None
            for h in range(len(qm)):
                p = _chunk_probs(qm[h], kw, b_ref[h], valid)
                d = jnp.dot(p.astype(bf16), vm[h], preferred_element_type=f32)
                o = d if o is None else o + d
            o_ref[rows, :] = o.astype(bf16)

    kv_spec = pl.BlockSpec((tp, LANES), lambda hp, qb: (0, hp))
    outs, new = _call(
        body, [proj, kpad, vpad, bias], grid=(A_W // LANES, t // step), name="chunk_attn_fwd",
        in_specs=[pl.BlockSpec((step, LANES), lambda hp, qb: (qb, hp)), kv_spec, kv_spec,
                  pl.BlockSpec((2, QB_A, KW_A), lambda hp, qb: (hp, 0, 0))],
        out_specs=[pl.BlockSpec((step, LANES), lambda hp, qb: (qb, hp))],
        out_shape=[SDS((t, A_W), bf16)], sem=("parallel", "arbitrary"), gather=gather)
    return outs[0], new


def _chunk_attn_bwd(proj, kpad, vpad, bias, dout):
    t = proj.shape[0]
    tp = kpad.shape[0]
    step = QSUB_A * QB_A

    def body(q_ref, k_ref, v_ref, b_ref, do_ref, dq_ref, dk_ref, dv_ref, db_ref):
        qb = pl.program_id(1)

        @pl.when(qb == 0)
        def _():
            dk_ref[...] = jnp.zeros_like(dk_ref)
            dv_ref[...] = jnp.zeros_like(dv_ref)
            db_ref[...] = jnp.zeros_like(db_ref)

        for sb in range(QSUB_A):
            start = pl.multiple_of((qb * QSUB_A + sb) * QB_A, QB_A)
            rows = pl.ds(sb * QB_A, QB_A)
            win = pl.ds(start, KW_A)
            valid = _chunk_valid(start)
            kw = k_ref[win, :]
            vw = v_ref[win, :]
            qm = _mask_heads(q_ref[rows, :])
            dom = _mask_heads(do_ref[rows, :])
            km = _mask_heads(kw)
            dq = dk = dv = None
            for h in range(len(qm)):
                p = _chunk_probs(qm[h], kw, b_ref[h], valid)
                dp = lax.dot_general(dom[h], vw, (NT, ((), ())), preferred_element_type=f32)
                ds = p * (dp - jnp.sum(dp * p, axis=-1, keepdims=True))
                db_ref[h] += ds
                dsb = (ds * (HEAD_DIM ** -0.5)).astype(bf16)
                terms = (jnp.dot(dsb, km[h], preferred_element_type=f32),
                         lax.dot_general(dsb, qm[h], (TN, ((), ())), preferred_element_type=f32),
                         lax.dot_general(p.astype(bf16), dom[h], (TN, ((), ())), preferred_element_type=f32))
                dq, dk, dv = terms if dq is None else (dq + terms[0], dk + terms[1], dv + terms[2])
            dq_ref[rows, :] = dq.astype(bf16)
            dk_ref[win, :] += dk
            dv_ref[win, :] += dv

    kv_spec = pl.BlockSpec((tp, LANES), lambda hp, qb: (0, hp))
    q_spec = pl.BlockSpec((step, LANES), lambda hp, qb: (qb, hp))
    b_spec = pl.BlockSpec((2, QB_A, KW_A), lambda hp, qb: (hp, 0, 0))
    return pl.pallas_call(
        body, grid=(A_W // LANES, t // step), name="chunk_attn_bwd",
        in_specs=[q_spec, kv_spec, kv_spec, b_spec, q_spec],
        out_specs=[q_spec, kv_spec, kv_spec, b_spec],
        out_shape=[SDS((t, A_W), bf16), SDS((tp, A_W), f32), SDS((tp, A_W), f32),
                   SDS((2 * A_W // LANES, QB_A, KW_A), f32)],
        compiler_params=_cparams(("parallel", "arbitrary")))(proj, kpad, vpad, bias, dout)


def _bias_ext(table):
    flat = PAD_A + QB_A - 1 - REL_CLIP
    top = jnp.broadcast_to(table[:, 2 * REL_CLIP:], (table.shape[0], flat))
    lo = 2 * REL_CLIP - (EXT_A - 1 - flat)
    return jnp.concatenate([top, jnp.flip(table[:, lo:], axis=1)], axis=1)


def _bias_window(table):
    nh = table.shape[0]
    e = jnp.broadcast_to(_bias_ext(table)[:, None, :], (nh, QB_A, EXT_A)).reshape(nh, QB_A * EXT_A)
    e = jnp.pad(e, ((0, 0), (0, QB_A)))
    m = e.reshape(nh, QB_A, EXT_A + 1)
    return jnp.flip(m, axis=1)[:, :, :KW_A]


def _bias_window_grad(dbias):
    nh = dbias.shape[0]
    d = jnp.flip(dbias, axis=1)
    d = jnp.pad(d, ((0, 0), (0, 0), (0, EXT_A + 1 - KW_A)))
    m = d.reshape(nh, QB_A * (EXT_A + 1))[:, :QB_A * EXT_A].reshape(nh, QB_A, EXT_A)
    dext = jnp.sum(m, axis=1)
    flat = PAD_A + QB_A - 1 - REL_CLIP
    lo = 2 * REL_CLIP - (EXT_A - 1 - flat)
    tail = jnp.flip(dext[:, flat:], axis=1)
    tail = tail.at[:, -1].add(jnp.sum(dext[:, :flat], axis=1))
    return jnp.pad(tail, ((0, 0), (lo, 0)))


def _tri_suffix(x, tri):
    hi = x.astype(bf16)
    lo = (x - hi.astype(f32)).astype(bf16)
    return jnp.dot(hi, tri, preferred_element_type=f32) + jnp.dot(lo, tri, preferred_element_type=f32)


def _sb_block(q, k, run, tri, causal):
    z = lax.dot_general(q, k, (NT, ((), ())), preferred_element_type=f32) * (HEAD_DIM ** -0.5)
    e = jnp.exp(-jnp.abs(z))
    l1p = jnp.log(1.0 + e)
    lb = jnp.minimum(z, 0.0) - l1p
    lmb = lb - z
    if causal is not None:
        lmb = jnp.where(causal, lmb, 0.0)
    cs = _tri_suffix(lmb, tri)
    w = jnp.exp(lb + (run + cs - lmb))
    if causal is not None:
        w = jnp.where(causal, w, 0.0)
    return z, e, w, run + cs[:, 0:1]


def _sb_tri():
    r = lax.broadcasted_iota(jnp.int32, (SB_BLK, SB_BLK), 0)
    c = lax.broadcasted_iota(jnp.int32, (SB_BLK, SB_BLK), 1)
    return (r >= c).astype(bf16), c < r


def _sb_live(runs):
    m = runs[0]
    for r in runs[1:]:
        m = jnp.maximum(m, r)
    return jnp.max(m) > SB_DEAD


def _sb_fwd(proj, gather):
    t = proj.shape[0]
    cb = A_W // LANES
    nh = LANES // HEAD_DIM

    def body(q_ref, k_ref, v_ref, o_ref, of_ref):
        qb = pl.program_id(1)
        tri, diag = _sb_tri()
        qm = _mask_heads(q_ref[...])

        def pair(kb, carry, causal):
            rows = pl.ds(pl.multiple_of(kb * SB_BLK, SB_BLK), SB_BLK)
            k = k_ref[rows, :]
            vm = _mask_heads(v_ref[rows, :])
            runs, acc = [], carry[nh]
            for h in range(nh):
                _, _, w, run = _sb_block(qm[h], k, carry[h], tri, causal)
                acc = acc + jnp.dot(w.astype(bf16), vm[h], preferred_element_type=f32)
                runs.append(run)
            return (*runs, acc)

        zero = jnp.zeros((SB_BLK, 1), f32)
        carry = pair(qb, (zero,) * nh + (jnp.zeros((SB_BLK, LANES), f32),), diag)

        def cond(st):
            return (st[0] < qb) & _sb_live(st[1][:nh])

        def step(st):
            return st[0] + 1, pair(qb - 1 - st[0], st[1], None)

        _, carry = lax.while_loop(cond, step, (jnp.int32(0), carry))
        o_ref[...] = carry[nh].astype(bf16)
        of_ref[...] = carry[nh]

    ospec = pl.BlockSpec((SB_BLK, LANES), lambda hp, qb: (qb, hp))
    return _call(
        body, [proj, proj, proj], grid=(cb, t // SB_BLK), name="sb_attn_fwd",
        in_specs=[pl.BlockSpec((SB_BLK, LANES), lambda hp, qb: (qb, 3 * cb + hp)),
                  pl.BlockSpec((t, LANES), lambda hp, qb: (0, 4 * cb + hp)),
                  pl.BlockSpec((t, LANES), lambda hp, qb: (0, 5 * cb + hp))],
        out_specs=[ospec, ospec], out_shape=[SDS((t, A_W), bf16), SDS((t, A_W), f32)],
        sem=("parallel", "arbitrary"), gather=gather)


def _sb_bwd(proj, out_b, dout):
    t = proj.shape[0]
    cb = A_W // LANES
    nh = LANES // HEAD_DIM

    def body(q_ref, k_ref, v_ref, o_ref, do_ref, dq_ref, dk_ref, dv_ref):
        qb = pl.program_id(1)
        tri, diag = _sb_tri()

        @pl.when(qb == 0)
        def _():
            dk_ref[...] = jnp.zeros_like(dk_ref)
            dv_ref[...] = jnp.zeros_like(dv_ref)

        qm = _mask_heads(q_ref[...])
        do = do_ref[...]
        dom = _mask_heads(do)
        dsums = [jnp.sum(t_, axis=-1, keepdims=True) for t_ in _mask_heads(do.astype(f32) * o_ref[...])]

        def pair(kb, carry, causal):
            rows = pl.ds(pl.multiple_of(kb * SB_BLK, SB_BLK), SB_BLK)
            k = k_ref[rows, :]
            v = v_ref[rows, :]
            km = _mask_heads(k)
            new, dq, dk, dv = [], carry[2 * nh], None, None
            for h in range(nh):
                z, e, w, run = _sb_block(qm[h], k, carry[2 * h], tri, causal)
                inv = 1.0 / (1.0 + e)
                beta = jnp.where(z >= 0.0, inv, e * inv)
                wb = w.astype(bf16)
                g = lax.dot_general(dom[h], v, (NT, ((), ())), preferred_element_type=f32) * wb.astype(f32)
                sg = _tri_suffix(g, tri)
                dz = g * (1.0 - beta) - (dsums[h] - carry[2 * h + 1] - sg) * beta
                if causal is not None:
                    dz = jnp.where(causal, dz, 0.0)
                dzb = (dz * (HEAD_DIM ** -0.5)).astype(bf16)
                dq = dq + jnp.dot(dzb, km[h], preferred_element_type=f32)
                tk = lax.dot_general(dzb, qm[h], (TN, ((), ())), preferred_element_type=f32)
                tv = lax.dot_general(wb, dom[h], (TN, ((), ())), preferred_element_type=f32)
                dk, dv = (tk, tv) if dk is None else (dk + tk, dv + tv)
                new += [run, carry[2 * h + 1] + sg[:, 0:1]]
            dk_ref[rows, :] += dk
            dv_ref[rows, :] += dv
            return (*new, dq)

        zero = jnp.zeros((SB_BLK, 1), f32)
        carry = pair(qb, (zero,) * (2 * nh) + (jnp.zeros((SB_BLK, LANES), f32),), diag)

        def cond(st):
            return (st[0] < qb) & _sb_live(st[1][0:2 * nh:2])

        def step(st):
            return st[0] + 1, pair(qb - 1 - st[0], st[1], None)

        _, carry = lax.while_loop(cond, step, (jnp.int32(0), carry))
        dq_ref[...] = carry[2 * nh].astype(bf16)

    kv_in = lambda seg: pl.BlockSpec((t, LANES), lambda hp, qb: (0, seg * cb + hp))
    q_spec = pl.BlockSpec((SB_BLK, LANES), lambda hp, qb: (qb, hp))
    kv_out = pl.BlockSpec((t, LANES), lambda hp, qb: (0, hp))
    return pl.pallas_call(
        body, grid=(cb, t // SB_BLK), name="sb_attn_bwd",
        in_specs=[pl.BlockSpec((SB_BLK, LANES), lambda hp, qb: (qb, 3 * cb + hp)), kv_in(4), kv_in(5),
                  q_spec, pl.BlockSpec((SB_BLK, LANES), lambda hp, qb: (qb, cb + hp))],
        out_specs=[q_spec, kv_out, kv_out],
        out_shape=[SDS((t, A_W), bf16), SDS((t, A_W), f32), SDS((t, A_W), f32)],
        compiler_params=_cparams(("parallel", "arbitrary")))(proj, proj, proj, out_b, dout)


def _halo_specs(tr, w, col, nblk):
    per = tr // SUBLANES
    cur = pl.BlockSpec((tr, w), lambda i: (i, col))
    prev = pl.BlockSpec((SUBLANES, w), lambda i: (jnp.maximum(i * per - 1, 0), col))
    nxt = pl.BlockSpec((SUBLANES, w), lambda i: (jnp.minimum((i + 1) * per, nblk * per - 1), col))
    return cur, prev, nxt


def _taps_before(cur, prev8, first):
    prev8 = jnp.where(first, 0.0, prev8)
    ext = jnp.concatenate([prev8, cur], axis=0)
    return [pltpu.roll(ext, s, 0)[SUBLANES:] for s in (3, 2, 1)]


def _taps_after(cur, next8, last):
    n = cur.shape[0]
    next8 = jnp.where(last, 0.0, next8)
    ext = jnp.concatenate([cur, next8], axis=0)
    return [pltpu.roll(ext, n + SUBLANES - s, 0)[:n] for s in (1, 2, 3)]


def _block_diag(x, w_ref, dims):
    outs = [lax.dot_general(x[:, n * LRU_BW:(n + 1) * LRU_BW], w_ref[n], (dims, ((), ())),
                            preferred_element_type=f32) for n in range(LRU_BLOCKS)]
    return jnp.concatenate(outs, axis=1)


def _lru_gates(xc, wa_ref, wi_ref, ba, bi, lam):
    xb = xc.astype(bf16)
    r = jax.nn.sigmoid(_block_diag(xb, wa_ref, NN) + ba)
    ig = jax.nn.sigmoid(_block_diag(xb, wi_ref, NN) + bi)
    sp = jnp.maximum(-lam, 0.0) + jnp.log(1.0 + jnp.exp(-jnp.abs(lam)))
    log_a = -LRU_C * r * sp
    a = jnp.exp(log_a)
    x2 = 2.0 * log_a
    one_minus = jnp.where(x2 > -1e-2, -x2 * (1.0 + x2 * (0.5 + x2 * (1.0 / 6.0))), 1.0 - a * a)
    mult = jnp.sqrt(one_minus)
    return xb, r, ig, sp, a, mult


def _rg_gates_fwd(proj, conv_w, conv_b, wa, wi, ba, bi, lam, tr=256):
    t = proj.shape[0]
    w = D_MODEL
    tr = min(tr, t)
    nblk = t // tr
    cur, prev, _ = _halo_specs(tr, w, 1, nblk)

    def body(x_ref, xp_ref, cw_ref, cb_ref, wa_ref, wi_ref, ba_ref, bi_ref, lam_ref, xc_ref, a_ref, u_ref):
        x = x_ref[...]
        taps = _taps_before(x, xp_ref[...], pl.program_id(0) == 0) + [x]
        xc = cb_ref[...]
        for k in range(4):
            xc = xc + cw_ref[k:k + 1, :] * taps[k]
        _, _, ig, _, a, mult = _lru_gates(xc, wa_ref, wi_ref, ba_ref[...], bi_ref[...], lam_ref[...])
        xc_ref[...] = xc
        a_ref[...] = a
        u_ref[...] = mult * (ig * xc)

    full = lambda a_: pl.BlockSpec(a_.shape, lambda i, nd=a_.ndim: (0,) * nd)
    ospec = pl.BlockSpec((tr, w), lambda i: (i, 0))
    return pl.pallas_call(
        body, grid=(nblk,), name="rg_gates_fwd",
        in_specs=[cur, prev] + [full(a_) for a_ in (conv_w, conv_b, wa, wi, ba, bi, lam)],
        out_specs=[ospec] * 3, out_shape=[SDS((t, w), f32)] * 3,
        compiler_params=_cparams(("parallel",)))(proj, proj, conv_w, conv_b, wa, wi, ba, bi, lam)


def _lru_scan(name, a, b, reverse, tt=512):
    t, w = a.shape
    tt = min(tt, t)
    nt = t // tt
    ng = tt // SUBLANES

    def body(a_ref, b_ref, h_ref, carry_ref):
        @pl.when(pl.program_id(0) == 0)
        def _():
            carry_ref[...] = jnp.zeros_like(carry_ref)

        row = lax.broadcasted_iota(jnp.int32, (SUBLANES, w), 0)

        def group(gi, carry):
            g = (ng - 1 - gi) if reverse else gi
            rows = pl.ds(pl.multiple_of(g * SUBLANES, SUBLANES), SUBLANES)
            av = a_ref[rows, :]
            bv = b_ref[rows, :]
            for s in (1, 2, 4):
                sh = (SUBLANES - s) if reverse else s
                ok = (row < SUBLANES - s) if reverse else (row >= s)
                a_s = pltpu.roll(av, sh, 0)
                b_s = pltpu.roll(bv, sh, 0)
                bv = jnp.where(ok, av * b_s + bv, bv)
                av = jnp.where(ok, av * a_s, av)
            h = av * carry + bv
            h_ref[rows, :] = h
            edge = h[0:1, :] if reverse else h[SUBLANES - 1:SUBLANES, :]
            return jnp.broadcast_to(edge, (SUBLANES, w))

        carry_ref[...] = lax.fori_loop(0, ng, group, carry_ref[...])

    tmap = (lambda i: (nt - 1 - i, 0)) if reverse else (lambda i: (i, 0))
    spec = pl.BlockSpec((tt, w), tmap)
    return pl.pallas_call(
        body, grid=(nt,), name=name, in_specs=[spec, spec], out_specs=spec,
        out_shape=SDS((t, w), f32), scratch_shapes=[pltpu.VMEM((SUBLANES, w), f32)],
        compiler_params=_cparams(("arbitrary",)))(a, b)


def _rg_gates_bwd(dhs, c, hs, xc, wa, wi, ba, bi, lam, tr=256):
    t, w = xc.shape
    tr = min(tr, t)
    nblk = t // tr
    cur, prev, nxt = _halo_specs(tr, w, 0, nblk)

    def body(dhs_ref, c_ref, cn_ref, hs_ref, hp_ref, xc_ref, wa_ref, wi_ref, ba_ref, bi_ref, lam_ref,
             dxc_ref, dwa_ref, dwi_ref, dba_ref, dbi_ref, dlam_ref):
        i = pl.program_id(0)
        c_next = _taps_after(c_ref[...], cn_ref[...], i == nblk - 1)[0]
        h_prev = _taps_before(hs_ref[...], hp_ref[...], i == 0)[2]
        xc = xc_ref[...]
        lam = lam_ref[...]
        xb, r, ig, sp, a, mult = _lru_gates(xc, wa_ref, wi_ref, ba_ref[...], bi_ref[...], lam)
        dh = dhs_ref[...] + c_next
        dlog_a = dh * h_prev * a - (dh * ig * xc) * (a * a / mult)
        dpre_a = (dlog_a * (-LRU_C * sp) * r * (1.0 - r)).astype(bf16)
        dpre_i = (dh * mult * xc * ig * (1.0 - ig)).astype(bf16)
        dxc_ref[...] = (dh * mult * ig + _block_diag(dpre_a, wa_ref, NT) + _block_diag(dpre_i, wi_ref, NT))
        dsig = 1.0 / (1.0 + jnp.exp(lam))
        sums = [jnp.sum(dpre_a.astype(f32), axis=0, keepdims=True),
                jnp.sum(dpre_i.astype(f32), axis=0, keepdims=True),
                jnp.sum(dlog_a * (-LRU_C * r), axis=0, keepdims=True) * (-dsig)]

        @pl.when(i == 0)
        def _():
            dwa_ref[...] = jnp.zeros_like(dwa_ref)
            dwi_ref[...] = jnp.zeros_like(dwi_ref)
            dba_ref[...] = jnp.zeros_like(dba_ref)
            dbi_ref[...] = jnp.zeros_like(dbi_ref)
            dlam_ref[...] = jnp.zeros_like(dlam_ref)

        for n in range(LRU_BLOCKS):
            sl = slice(n * LRU_BW, (n + 1) * LRU_BW)
            dwa_ref[n] += lax.dot_general(xb[:, sl], dpre_a[:, sl], (TN, ((), ())), preferred_element_type=f32)
            dwi_ref[n] += lax.dot_general(xb[:, sl], dpre_i[:, sl], (TN, ((), ())), preferred_element_type=f32)
        dba_ref[...] += sums[0]
        dbi_ref[...] += sums[1]
        dlam_ref[...] += sums[2]

    full = lambda a_: pl.BlockSpec(a_.shape, lambda i, nd=a_.ndim: (0,) * nd)
    vec = pl.BlockSpec((1, w), lambda i: (0, 0))
    mat = pl.BlockSpec((LRU_BLOCKS, LRU_BW, LRU_BW), lambda i: (0, 0, 0))
    return pl.pallas_call(
        body, grid=(nblk,), name="rg_gates_bwd",
        in_specs=[cur, cur, nxt, cur, prev, cur] + [full(a_) for a_ in (wa, wi, ba, bi, lam)],
        out_specs=[cur, mat, mat, vec, vec, vec],
        out_shape=[SDS((t, w), f32), SDS((LRU_BLOCKS, LRU_BW, LRU_BW), f32), SDS((LRU_BLOCKS, LRU_BW, LRU_BW), f32),
                   SDS((1, w), f32), SDS((1, w), f32), SDS((1, w), f32)],
        compiler_params=_cparams(("arbitrary",)))(dhs, c, c, hs, hs, xc, wa, wi, ba, bi, lam)


def _rg_conv_bwd(dxc, proj, conv_w, tr=256):
    t, w = dxc.shape
    tr = min(tr, t)
    nblk = t // tr
    cur, _, nxt = _halo_specs(tr, w, 0, nblk)
    xcur, xprev, _ = _halo_specs(tr, w, 1, nblk)

    def body(d_ref, dn_ref, x_ref, xp_ref, cw_ref, dx_ref, dcw_ref, dcb_ref):
        i = pl.program_id(0)
        d = d_ref[...]
        x = x_ref[...]
        after = _taps_after(d, dn_ref[...], i == nblk - 1)
        before = _taps_before(x, xp_ref[...], i == 0) + [x]
        dx = cw_ref[3:4, :] * d
        for s in (1, 2, 3):
            dx = dx + cw_ref[3 - s:4 - s, :] * after[s - 1]
        dx_ref[...] = dx.astype(bf16)
        dcw = jnp.concatenate([jnp.sum(d * before[k], axis=0, keepdims=True) for k in range(4)], axis=0)
        dcb = jnp.sum(d, axis=0, keepdims=True)

        @pl.when(i == 0)
        def _():
            dcw_ref[...] = dcw
            dcb_ref[...] = dcb

        @pl.when(i > 0)
        def _():
            dcw_ref[...] += dcw
            dcb_ref[...] += dcb

    return pl.pallas_call(
        body, grid=(nblk,), name="rg_conv_bwd",
        in_specs=[cur, nxt, xcur, xprev, pl.BlockSpec((4, w), lambda i: (0, 0))],
        out_specs=[cur, pl.BlockSpec((4, w), lambda i: (0, 0)), pl.BlockSpec((1, w), lambda i: (0, 0))],
        out_shape=[SDS((t, w), bf16), SDS((4, w), f32), SDS((1, w), f32)],
        compiler_params=_cparams(("arbitrary",)))(dxc, dxc, proj, proj, conv_w)


def _attn_fwd(h, wts, j, plan):
    proj = _mm_cols("attn_in", h, wts["attn_w_in"], j, bf16)
    kpad = jnp.pad(proj[:, A_W:2 * A_W], ((PAD_A, 0), (0, 0)))
    vpad = jnp.pad(proj[:, 2 * A_W:3 * A_W], ((PAD_A, 0), (0, 0)))
    bias = _bias_window(wts["attn_rel_bias"][j])
    plan = plan if j == 0 else None
    out_a = _carried(plan, "chunk_attn_fwd", wts, _chunk_attn_fwd, proj, kpad, vpad, bias)
    out_b, out_b32 = _carried(plan, "sb_attn_fwd", wts, _sb_fwd, proj)
    m = _mm_rows("attn_out", [out_a, out_b], wts["attn_w_out"], j, f32)
    return m, (proj, kpad, vpad, bias, out_a, out_b, out_b32)


def _attn_bwd(dm, h, saved, wts, j, grads):
    proj, kpad, vpad, bias, out_a, out_b, out_b32 = saved
    dout = _mm_rows_t("attn_out_t", dm, wts["attn_w_out"], j, bf16)
    grads["attn_w_out"] = _mm_wgrad("attn_out_wgrad_a", out_a, dm, grads["attn_w_out"], j, 0)
    grads["attn_w_out"] = _mm_wgrad("attn_out_wgrad_b", out_b, dm, grads["attn_w_out"], j, 1)
    dqa, dka, dva, dbias = _chunk_attn_bwd(proj, kpad, vpad, bias, dout)
    dqs, dks, dvs = _sb_bwd(proj, out_b32, dout)
    grads["attn_rel_bias"][j] = _bias_window_grad(dbias)
    dproj = jnp.concatenate([dqa, dka[PAD_A:].astype(bf16), dva[PAD_A:].astype(bf16),
                             dqs, dks.astype(bf16), dvs.astype(bf16)], axis=1)
    grads["attn_w_in"] = _mm_wgrad_cols("attn_in_wgrad", h, dproj, grads["attn_w_in"], j)
    return _mm_cols_t("attn_in_t", dproj, wts["attn_w_in"], j, f32)


def _rg_fwd(h, wts, j, plan):
    proj =_mm_cols("rg_in", h, wts["rg_w_in"], j, f32)
    small = [wts[k][j] for k in ("rg_conv_w", "rg_conv_b", "rg_w_a", "rg_w_i", "rg_b_a", "rg_b_i", "rg_lambda")]
    xc, a, u = _rg_gates_fwd(proj, *small)
    hs = _lru_scan("lru_scan_fwd", a, u, False)
    yp = _rows("rg_gate_out", lambda hv, gv: hv * _gelu(gv), [hs, (proj, D_MODEL, 0)], [], [(D_MODEL, bf16)])[0]
    m = _mm_rows("rg_out", [yp], wts["rg_w_out"], j, f32)
    return m, (proj, xc, a, hs, yp)


def _rg_bwd(dm, h, saved, wts, j, grads):
    proj, xc, a, hs, yp = saved
    dyp = _mm_rows_t("rg_out_t", dm, wts["rg_w_out"], j, f32)
    grads["rg_w_out"] = _mm_wgrad("rg_out_wgrad", yp, dm, grads["rg_w_out"], j)

    def gate_bwd(dy, hv, gv, av):
        dhs = dy * _gelu(gv)
        return dhs, av * dhs, dy * hv * _gelu_grad(gv)

    dhs, ab, dgate = _rows("rg_gate_out_bwd", gate_bwd, [dyp, hs, (proj, D_MODEL, 0), a], [],
                           [(D_MODEL, f32), (D_MODEL, f32), (D_MODEL, bf16)])
    c = _lru_scan("lru_scan_bwd", a, ab, True)
    wa, wi, ba, bi, lam = [wts[k][j] for k in ("rg_w_a", "rg_w_i", "rg_b_a", "rg_b_i", "rg_lambda")]
    dxc, dwa, dwi, dba, dbi, dlam = _rg_gates_bwd(dhs, c, hs, xc, wa, wi, ba, bi, lam)
    dxr, dcw, dcb = _rg_conv_bwd(dxc, proj, wts["rg_conv_w"][j])
    for k, v in (("rg_w_a", dwa), ("rg_w_i", dwi), ("rg_b_a", dba), ("rg_b_i", dbi), ("rg_lambda", dlam),
                 ("rg_conv_w", dcw), ("rg_conv_b", dcb)):
        grads[k][j] = v
    dproj = jnp.concatenate([dgate, dxr], axis=1)
    grads["rg_w_in"] = _mm_wgrad_cols("rg_in_wgrad", h, dproj, grads["rg_w_in"], j)
    return _mm_cols_t("rg_in_t", dproj, wts["rg_w_in"], j, f32)


def _local_step(x, target, wts, plan=None):
    t = x.shape[0]
    d = D_MODEL
    gains = {k: wts[k] for k in ("norm_mix_pre", "norm_mix_post", "norm_ffn_pre", "norm_ffn_post")}
    gain = lambda k, l: gains[k][l:l + 1]

    saved = []
    h = _rows("norm_in", _norm_fwd, [x], [gain("norm_mix_pre", 0)], [(d, bf16)])[0]
    loss_cols = None
    for l in range(DEPTH):
        j = l // 2
        m, mix_saved = (_attn_fwd if l % 2 == 0 else _rg_fwd)(h, wts, j, plan)

        def resid_next(xv, mv, g_post, g_next):
            x1 = xv + _norm_fwd(mv, g_post)
            return x1, _norm_fwd(x1, g_next)

        x1, h2 = _rows("resid_mix", resid_next, [x, m], [gain("norm_mix_post", l), gain("norm_ffn_pre", l)],
                       [(d, f32), (d, bf16)])
        g, u, hid = _carried(plan if l == 0 else None, "ffn_up", wts, _ffn_up, h2, wts["ffn_w_gate"],
                             wts["ffn_w_up"], l)
        f = _ffn_down(hid, wts["ffn_w_down"], l)
        saved.append((x, h, m, mix_saved, x1, h2, g, u, hid, f))
        if l + 1 < DEPTH:
            x, h = _rows("resid_ffn", resid_next, [x1, f], [gain("norm_ffn_post", l), gain("norm_mix_pre", l + 1)],
                         [(d, f32), (d, bf16)])
        else:
            def resid_loss(xv, fv, tv, g_post):
                err = xv + _norm_fwd(fv, g_post) - tv
                return err * (1.0 / d), jnp.sum(err * err, axis=0, keepdims=True)

            dx, loss_cols = _rows("resid_loss", resid_loss, [x1, f, target], [gain("norm_ffn_post", l)],
                                  [(d, f32)], [((1, d), f32)])
    loss = 0.5 * jnp.sum(loss_cols) / d

    grads = {k: {} for k in SMALL_GRADS}
    for k in BIG_GRADS:
        shp = wts[k].shape
        grads[k] = jnp.zeros((shp[0],) + shp[2:] if shp[1] == 1 else shp, f32)

    def norm_bwd_cast(uv, dyv, gv):
        du, dg = _norm_bwd(uv, dyv, gv)
        return du, dg

    def norm_bwd_resid(uv, dhv, dxv, gv):
        du, dg = _norm_bwd(uv, dhv, gv)
        return dxv + du, dg

    for l in reversed(range(DEPTH)):
        j = l // 2
        x_in, h, m, mix_saved, x1, h2, g, u, hid, f = saved[l]
        df, grads["norm_ffn_post"][l] = _rows("norm_ffn_post_bwd", norm_bwd_cast, [f, dx], [gain("norm_ffn_post", l)],
                                              [(d, bf16)], [((1, d), f32)])
        dg, du = _ffn_down_bwd(df, wts["ffn_w_down"], l, g, u)
        grads["ffn_w_down"] = _ffn_wgrad_down(hid, df, grads["ffn_w_down"], l)
        dh2 = _ffn_up_bwd(dg, du, wts["ffn_w_gate"], wts["ffn_w_up"], l)
        grads["ffn_w_gate"] = _ffn_wgrad_up("ffn_wgrad_gate", h2, dg, grads["ffn_w_gate"], l)
        grads["ffn_w_up"] = _ffn_wgrad_up("ffn_wgrad_up", h2, du, grads["ffn_w_up"], l)
        dx1, grads["norm_ffn_pre"][l] = _rows("norm_ffn_pre_bwd", norm_bwd_resid, [x1, dh2, dx],
                                              [gain("norm_ffn_pre", l)], [(d, f32)], [((1, d), f32)])
        dm, grads["norm_mix_post"][l] = _rows("norm_mix_post_bwd", norm_bwd_cast, [m, dx1], [gain("norm_mix_post", l)],
                                              [(d, bf16)], [((1, d), f32)])
        dh = (_attn_bwd if l % 2 == 0 else _rg_bwd)(dm, h, mix_saved, wts, j, grads)
        dx, grads["norm_mix_pre"][l] = _rows("norm_mix_pre_bwd", norm_bwd_resid, [x_in, dh, dx1],
                                             [gain("norm_mix_pre", l)], [(d, f32)], [((1, d), f32)])
    return loss, dx, grads


ANY = pl.BlockSpec(memory_space=pl.ANY)
PACK_COLS = 1024
SMALL_ROWS = 288


def _mesh_pos():
    x, y, c = lax.axis_index("x"), lax.axis_index("y"), lax.axis_index("c")
    return x, y, c, [(1 - x, y), (x, 1 - y), (1 - x, 1 - y)]


def _run_copies(copies):
    for cp in copies:
        cp.start()
    for cp in copies:
        cp.wait()


GATHER_SEMS = 7


def _gather_copies(items, ins, outs, send, recv):
    x, y, c, chips = _mesh_pos()
    q = 2 * x + y
    sibling = (x, y, 1 - c)

    def copy(k, src, dst, to):
        return pltpu.make_async_remote_copy(src_ref=src, dst_ref=dst, send_sem=send.at[k], recv_sem=recv.at[k],
                                            device_id=to, device_id_type=MESH)

    own, sent, passed = [], [], []
    for i, (t, l0, nl) in enumerate(items):
        lay = pl.ds(l0, nl)
        half = ins[t].shape[1] // 2
        rows = pl.ds(pl.multiple_of(c * half, half), half)
        own.append(copy(GATHER_SEMS * i, ins[t].at[lay], outs[t].at[lay, q], sibling))
        for j, (px, py) in enumerate(chips):
            sent.append(copy(GATHER_SEMS * i + 1 + j, ins[t].at[lay, rows], outs[t].at[lay, q, rows], (px, py, c)))
            landed = outs[t].at[lay, 2 * px + py, rows]
            passed.append(copy(GATHER_SEMS * i + 4 + j, landed, landed, sibling))
    return own, sent, passed


def _gather_start(items, ins, outs, send, recv):
    own, sent, _ = _gather_copies(items, ins, outs, send, recv)
    for cp in own + sent:
        cp.start()


def _gather_finish(items, ins, outs, send, recv):
    own, sent, passed = _gather_copies(items, ins, outs, send, recv)
    for arrived, forward in zip(sent, passed):
        arrived.wait_recv()
        forward.start()
    for cp in sent:
        cp.wait_send()
    for cp in own + passed:
        cp.wait()


def _gather_call(items, shards):
    n = len(shards)
    nsem = GATHER_SEMS * len(items)

    def body(*refs):
        ins, outs = refs[:n], refs[n:2 * n]
        _gather_start(items, ins, outs, *refs[2 * n:])
        _gather_finish(items, ins, outs, *refs[2 * n:])

    return pl.pallas_call(
        body, name="weight_all_gather", in_specs=[ANY] * n, out_specs=[ANY] * n,
        out_shape=[SDS((s.shape[0], N_CHIPS) + s.shape[1:], s.dtype) for s in shards],
        scratch_shapes=[pltpu.SemaphoreType.DMA((nsem,)), pltpu.SemaphoreType.DMA((nsem,))])(*shards)


def _call(body, operands, *, name, grid, in_specs, out_specs, out_shape, sem, scratch=(), gather=None):
    if gather is None:
        return pl.pallas_call(body, grid=grid, in_specs=in_specs, out_specs=out_specs, out_shape=out_shape,
                              scratch_shapes=list(scratch), name=name, compiler_params=_cparams(sem))(*operands), None
    items, shards, gathered = gather
    n_in, n_out, n_scr, ng = len(operands), len(out_shape), len(scratch), len(shards)
    nsem = GATHER_SEMS * len(items)

    def full(*refs):
        ins, sh = refs[:n_in], refs[n_in:n_in + ng]
        outs = refs[n_in + 2 * ng:n_in + 2 * ng + n_out]
        io = refs[n_in + 2 * ng + n_out:n_in + 3 * ng + n_out]
        scr = refs[n_in + 3 * ng + n_out:]
        ids = [pl.program_id(a) for a in range(len(grid))]
        first = functools.reduce(jnp.logical_and, [i == 0 for i in ids])
        last = functools.reduce(jnp.logical_and, [i == g - 1 for i, g in zip(ids, grid)])

        @pl.when(first)
        def _():
            _gather_start(items, sh, io, scr[n_scr], scr[n_scr + 1])

        body(*ins, *outs, *scr[:n_scr])

        @pl.when(last)
        def _():
            _gather_finish(items, sh, io, scr[n_scr], scr[n_scr + 1])

    res = pl.pallas_call(
        full, grid=grid, in_specs=list(in_specs) + [ANY] * (2 * ng), out_specs=list(out_specs) + [ANY] * ng,
        out_shape=list(out_shape) + [SDS(g.shape, g.dtype) for g in gathered],
        scratch_shapes=list(scratch) + [pltpu.SemaphoreType.DMA((nsem,)), pltpu.SemaphoreType.DMA((nsem,))],
        input_output_aliases={n_in + ng + t: n_out + t for t in range(ng)}, name=name,
        compiler_params=_cparams(("arbitrary",) * len(grid)))(*operands, *shards, *gathered)
    return res[:n_out], res[n_out:]


def _pair_exchange(gs):
    n = len(gs)

    def body(*refs):
        ins, outs = refs[:n], refs[n:2 * n]
        send, recv = refs[2 * n:]
        x, y, c, _ = _mesh_pos()
        copies = []
        for t in range(n):
            half = ins[t].shape[2] // 2
            src = ins[t].at[:, :, pl.ds(pl.multiple_of((1 - c) * half, SUBLANES), half)]
            copies.append(pltpu.make_async_remote_copy(
                src_ref=src, dst_ref=outs[t], send_sem=send.at[t], recv_sem=recv.at[t],
                device_id=(x, y, 1 - c), device_id_type=MESH))
        _run_copies(copies)

    return pl.pallas_call(
        body, name="grad_pair_exchange", in_specs=[ANY] * n, out_specs=[ANY] * n,
        out_shape=[SDS(g.shape[:2] + (g.shape[2] // 2, g.shape[3]), f32) for g in gs],
        scratch_shapes=[pltpu.SemaphoreType.DMA((n,)), pltpu.SemaphoreType.DMA((n,))])(*gs)


def _pair_sum(name, g, got, c):
    l, s, r, cols = g.shape

    def body(c_ref, a_ref, b_ref, o_ref):
        o_ref[...] = (a_ref[...] + b_ref[...]).astype(bf16)

    blk = (None, None, r // 2, cols)
    return pl.pallas_call(
        body, name=name, out_shape=SDS(got.shape, bf16),
        grid_spec=pltpu.PrefetchScalarGridSpec(
            num_scalar_prefetch=1, grid=(l, s),
            in_specs=[pl.BlockSpec(blk, lambda i, q, c_ref: (i, q, c_ref[0], 0)),
                      pl.BlockSpec(blk, lambda i, q, c_ref: (i, q, 0, 0))],
            out_specs=pl.BlockSpec(blk, lambda i, q, c_ref: (i, q, 0, 0))),
        compiler_params=_cparams(("parallel", "parallel")))(c, g, got)


def _chip_exchange(hs):
    n = len(hs)

    def body(*refs):
        ins, outs = refs[:n], refs[n:2 * n]
        send, recv = refs[2 * n:]
        x, y, c, chips = _mesh_pos()
        q = 2 * x + y
        copies = []
        for t in range(n):
            for j, (px, py) in enumerate(chips):
                copies.append(pltpu.make_async_remote_copy(
                    src_ref=ins[t].at[:, 2 * px + py], dst_ref=outs[t].at[:, q], send_sem=send.at[3 * t + j],
                    recv_sem=recv.at[3 * t + j], device_id=(px, py, c), device_id_type=MESH))
        _run_copies(copies)

    return pl.pallas_call(
        body, name="grad_chip_exchange", in_specs=[ANY] * n, out_specs=[ANY] * n,
        out_shape=[SDS(h.shape, h.dtype) for h in hs],
        scratch_shapes=[pltpu.SemaphoreType.DMA((3 * n,)), pltpu.SemaphoreType.DMA((3 * n,))])(*hs)


def _chip_sum(name, s, h, pos):
    l, _, r, cols = s.shape

    def body(pos_ref, s0, s1, s2, s3, own_ref, o_ref):
        vals = [jnp.where(pos_ref[0] == p, own_ref[...], ref[...]).astype(f32) for p, ref in enumerate((s0, s1, s2, s3))]
        o_ref[...] = ((vals[0] + vals[1]) + vals[2]) + vals[3]

    blk = (None, None, r, cols)
    slot = lambda p: pl.BlockSpec(blk, lambda i, pos_ref: (i, jnp.where(pos_ref[0] == p, (p + 1) % N_CHIPS, p), 0, 0))
    return pl.pallas_call(
        body, name=name, out_shape=SDS((l, 2 * r, cols), f32),
        grid_spec=pltpu.PrefetchScalarGridSpec(
            num_scalar_prefetch=1, grid=(l,),
            in_specs=[slot(p) for p in range(N_CHIPS)] + [pl.BlockSpec(blk, lambda i, pos_ref: (i, pos_ref[0], 0, 0))],
            out_specs=pl.BlockSpec((None, r, cols), lambda i, pos_ref: (i, pos_ref[1], 0))),
        compiler_params=_cparams(("parallel",)))(pos, s, s, s, s, h)


def _pair_gather(fulls):
    n = len(fulls)

    def body(*refs):
        ins, outs = refs[:n], refs[n:2 * n]
        send, recv = refs[2 * n:]
        x, y, c, _ = _mesh_pos()
        copies = []
        for t in range(n):
            half = outs[t].shape[1] // 2
            rows = outs[t].at[:, pl.ds(pl.multiple_of(c * half, SUBLANES), half)]
            copies.append(pltpu.make_async_remote_copy(
                src_ref=rows, dst_ref=rows, send_sem=send.at[t], recv_sem=recv.at[t],
                device_id=(x, y, 1 - c), device_id_type=MESH))
        _run_copies(copies)

    return pl.pallas_call(
        body, name="grad_pair_gather", in_specs=[ANY] * n, out_specs=[ANY] * n,
        out_shape=[SDS(f.shape, f32) for f in fulls], input_output_aliases={t: t for t in range(n)},
        scratch_shapes=[pltpu.SemaphoreType.DMA((n,)), pltpu.SemaphoreType.DMA((n,))])(*fulls)


COL_SHARDED = ("attn_w_in", "rg_w_in", "ffn_w_gate", "ffn_w_up")
ROW_SHARDED = ("attn_w_out", "rg_w_out")
GATES = ("rg_w_a", "rg_w_i")
VECTORS = ("rg_conv_w", "rg_conv_b", "rg_b_a", "rg_b_i", "rg_lambda")
REPLICATED = ("norm_mix_pre", "norm_mix_post", "norm_ffn_pre", "norm_ffn_post", "attn_rel_bias")
BIG_GRADS = COL_SHARDED + ROW_SHARDED + ("ffn_w_down",)
SMALL_GRADS = GATES + VECTORS + REPLICATED
WEIGHTS =("attn_w_in", "attn_rel_bias", "attn_w_out", "rg_w_in", "rg_conv_w", "rg_conv_b", "rg_w_a", "rg_b_a",
           "rg_w_i", "rg_b_i", "rg_lambda", "rg_w_out", "norm_mix_pre", "norm_mix_post", "norm_ffn_pre",
           "norm_ffn_post", "ffn_w_gate", "ffn_w_up", "ffn_w_down")
SMALL = VECTORS + REPLICATED


GATHER_PARTS = {
    "first": (("attn_w_in", 0, 1), ("attn_w_out", 0, 1), ("rg_w_a", 0, 8), ("rg_w_i", 0, 8), ("vec", 0, 1)),
    "chunk_attn_fwd": (("ffn_w_gate", 0, 1), ("ffn_w_up", 0, 1), ("ffn_w_down", 0, 1), ("rg_w_in", 0, 1),
                       ("rg_w_out", 0, 1)),
    "sb_attn_fwd": (("ffn_w_gate", 1, 3), ("ffn_w_up", 1, 3), ("ffn_w_down", 1, 3), ("attn_w_in", 1, 1),
                    ("attn_w_out", 1, 1)),
    "ffn_up": (("rg_w_in", 1, 1), ("rg_w_out", 1, 1)),
}


class _WeightGather:
    def __init__(self, w):
        self.w = w
        self.names = list(COL_SHARDED + ROW_SHARDED + GATES + ("ffn_w_down", "vec"))
        self.shards = {}
        for k in self.names[:-1]:
            a = w[k].astype(bf16)
            self.shards[k] = a.reshape((-1,) + a.shape[-2:])
        self.shards["vec"] = jnp.concatenate([w[k].reshape(-1) for k in VECTORS]).reshape(1, -1, LANES)
        got = _gather_call(self._items("first", self.names), [self.shards[k] for k in self.names])
        self.raw = dict(zip(self.names, got))

    @staticmethod
    def _items(part, names):
        return [(names.index(k), l0, nl) for k, l0, nl in GATHER_PARTS[part]]

    def part(self, part):
        names = list(dict.fromkeys(k for k, _, _ in GATHER_PARTS[part]))
        return (self._items(part, names), [self.shards[k] for k in names], [self.raw[k] for k in names]), names

    def views(self):
        got, w = self.raw, self.w
        out = {k: w[k] for k in REPLICATED}
        for k in COL_SHARDED + ("ffn_w_down",):
            out[k] = got[k]
        for k in ROW_SHARDED:
            l, s, ks, n = got[k].shape
            out[k] = got[k].reshape(l, 1, s * ks, n)
        for k in GATES:
            out[k] = got[k].reshape(2, LRU_BLOCKS, LRU_BW, LRU_BW)
        vec = got["vec"].reshape(N_CHIPS, -1)
        off = 0
        for k in VECTORS:
            shp = w[k].shape
            n = int(np.prod(shp))
            piece = vec[:, off:off + n].reshape((N_CHIPS,) + shp)
            off += n
            if k == "rg_conv_w":
                out[k] = piece.reshape(N_CHIPS, 2, 4, 256).transpose(1, 2, 0, 3).reshape(2, 4, D_MODEL)
            elif k in ("rg_b_a", "rg_b_i"):
                out[k] = piece.transpose(1, 2, 0, 3).reshape(2, 1, D_MODEL)
            else:
                out[k] = piece.transpose(1, 0, 2).reshape(2, 1, D_MODEL)
        return out


def _carried(plan, part, wts, fn, *args):
    if plan is None:
        return fn(*args, None)[0]
    gather, names = plan.part(part)
    out, new = fn(*args, gather)
    plan.raw.update(zip(names, new))
    wts.update(plan.views())
    return out


def _grad_blocks(name, g):
    st = jnp.stack([g[i] for i in sorted(g)])
    if name in GATES:
        st = st.reshape(2, LRU_BLOCKS, N_CHIPS, LRU_BW // N_CHIPS, LRU_BW).transpose(2, 0, 1, 3, 4)
    elif name == "rg_conv_w":
        st = st.reshape(2, 4, N_CHIPS, -1).transpose(2, 0, 1, 3)
    elif name in ("rg_b_a", "rg_b_i"):
        st = st.reshape(2, LRU_BLOCKS, N_CHIPS, -1).transpose(2, 0, 1, 3)
    elif name in VECTORS:
        st = st.reshape(2, N_CHIPS, -1).transpose(1, 0, 2)
    else:
        st = jnp.broadcast_to(st.reshape(1, -1), (N_CHIPS, st.size))
    return st.reshape(N_CHIPS, -1)


def _reduce_gradients(grads, shard_shapes):
    gs = []
    for k in BIG_GRADS:
        g = grads[k]
        if g.ndim == 3:
            g = g.reshape(g.shape[0], N_CHIPS, g.shape[1] // N_CHIPS, g.shape[2])
        gs.append(g)
    blocks = [_grad_blocks(k, grads[k]) for k in SMALL_GRADS]
    used = sum(b.shape[1] for b in blocks)
    small = jnp.concatenate(blocks + [jnp.zeros((N_CHIPS, SMALL_ROWS * PACK_COLS - used), f32)], axis=1)
    gs.append(small.reshape(1, N_CHIPS, SMALL_ROWS, PACK_COLS))
    names = BIG_GRADS + ("small",)
    c = lax.axis_index("c").astype(jnp.int32).reshape(1)
    pos = jnp.stack([2 * lax.axis_index("x") + lax.axis_index("y"), lax.axis_index("c")]).astype(jnp.int32)
    parts = [_pair_sum("grad_pair_sum_" + k, g, r, c) for k, g, r in zip(names, gs, _pair_exchange(gs))]
    slots = _chip_exchange(parts)
    full = _pair_gather([_chip_sum("grad_chip_sum_" + k, s, h, pos) for k, s, h in zip(names, slots, parts)])
    out = {k: f.reshape(shard_shapes[k]) for k, f in zip(BIG_GRADS, full)}
    flat, off = full[-1].reshape(-1), 0
    for k in SMALL_GRADS:
        n = int(np.prod(shard_shapes[k]))
        out[k] = flat[off:off + n].reshape(shard_shapes[k])
        off += n
    return out


def _adamw_fn(w, g, m, v):
    m = ADAM_B1 * m + (1.0 - ADAM_B1) * g
    v = ADAM_B2 * v + (1.0 - ADAM_B2) * (g * g)
    m_hat = m / (1.0 - ADAM_B1 ** ADAM_STEP)
    v_hat = v / (1.0 - ADAM_B2 ** ADAM_STEP)
    return -ADAM_LR * (m_hat / (jnp.sqrt(v_hat) + ADAM_EPS) + ADAM_WD * w), m, v


def _adamw(name, w, g, m, v):
    shp = w.shape
    if w.size >= 1 << 16:
        width = shp[-1]
        ops = [a.reshape(-1, width) for a in (w, g, m, v)]
        res = _rows(name, _adamw_fn, ops, [], [(width, f32)] * 3)
        return [r.reshape(shp) for r in res]
    n = w.size
    rows = -(-n // (SUBLANES * LANES)) * SUBLANES
    ops = [jnp.pad(a.reshape(-1), (0, rows * LANES - n)).reshape(rows, LANES) for a in (w, g, m, v)]
    res = _rows(name, _adamw_fn, ops, [], [(LANES, f32)] * 3, tr=rows)
    return [r.reshape(-1)[:n].reshape(shp) for r in res]


def kernel(x, attn_w_in, attn_rel_bias, attn_w_out, rg_w_in, rg_conv_w, rg_conv_b, rg_w_a, rg_b_a, rg_w_i, rg_b_i, rg_lambda, rg_w_out, norm_mix_pre, norm_mix_post, norm_ffn_pre, norm_ffn_post, ffn_w_gate, ffn_w_up, ffn_w_down, loss_target, m_attn_w_in, m_attn_rel_bias, m_attn_w_out, m_rg_w_in, m_rg_conv_w, m_rg_conv_b, m_rg_w_a, m_rg_b_a, m_rg_w_i, m_rg_b_i, m_rg_lambda, m_rg_w_out, m_norm_mix_pre, m_norm_mix_post, m_norm_ffn_pre, m_norm_ffn_post, m_ffn_w_gate, m_ffn_w_up, m_ffn_w_down, v_attn_w_in, v_attn_rel_bias, v_attn_w_out, v_rg_w_in, v_rg_conv_w, v_rg_conv_b, v_rg_w_a, v_rg_b_a, v_rg_w_i, v_rg_b_i, v_rg_lambda, v_rg_w_out, v_norm_mix_pre, v_norm_mix_post, v_norm_ffn_pre, v_norm_ffn_post, v_ffn_w_gate, v_ffn_w_up, v_ffn_w_down):
    w = dict(zip(WEIGHTS, (attn_w_in, attn_rel_bias, attn_w_out, rg_w_in, rg_conv_w, rg_conv_b, rg_w_a, rg_b_a, rg_w_i,
                           rg_b_i, rg_lambda, rg_w_out, norm_mix_pre, norm_mix_post, norm_ffn_pre, norm_ffn_post,
                           ffn_w_gate, ffn_w_up, ffn_w_down)))
    m = dict(zip(WEIGHTS, (m_attn_w_in, m_attn_rel_bias, m_attn_w_out, m_rg_w_in, m_rg_conv_w, m_rg_conv_b, m_rg_w_a,
                           m_rg_b_a, m_rg_w_i, m_rg_b_i, m_rg_lambda, m_rg_w_out, m_norm_mix_pre, m_norm_mix_post,
                           m_norm_ffn_pre, m_norm_ffn_post, m_ffn_w_gate, m_ffn_w_up, m_ffn_w_down)))
    v = dict(zip(WEIGHTS, (v_attn_w_in, v_attn_rel_bias, v_attn_w_out, v_rg_w_in, v_rg_conv_w, v_rg_conv_b, v_rg_w_a,
                           v_rg_b_a, v_rg_w_i, v_rg_b_i, v_rg_lambda, v_rg_w_out, v_norm_mix_pre, v_norm_mix_post,
                           v_norm_ffn_pre, v_norm_ffn_post, v_ffn_w_gate, v_ffn_w_up, v_ffn_w_down)))
    plan = _WeightGather(w)
    loss, dx, grads = _local_step(x[0], loss_target[0], plan.views(), plan)
    loss = lax.psum(loss, ("x", "y", "c"))
    g = _reduce_gradients(grads, {k: w[k].shape for k in WEIGHTS})

    big = [k for k in WEIGHTS if k not in SMALL]
    upd = {k: _adamw("adamw_" + k, w[k], g[k], m[k], v[k]) for k in big}
    cat = lambda d: jnp.concatenate([d[k].reshape(-1) for k in SMALL])
    small = _adamw("adamw_small", cat(w), cat(g), cat(m), cat(v))
    off = 0
    for k in SMALL:
        n = w[k].size
        upd[k] = [r[off:off + n].reshape(w[k].shape) for r in small]
        off += n
    return (loss, dx[None], *[g[k] for k in WEIGHTS], *[upd[k][0] for k in WEIGHTS],
            *[upd[k][1] for k in WEIGHTS], *[upd[k][2] for k in WEIGHTS])
```

```python
import functools

import numpy as np
import jax
import jax.numpy as jnp
from jax import lax
from jax.experimental import pallas as pl
from jax.experimental.pallas import tpu as pltpu

f32 = jnp.float32
bf16 = jnp.bfloat16
SDS = jax.ShapeDtypeStruct
MESH = pl.DeviceIdType.MESH

D_MODEL = 1024
N_CHIPS = 4
DEPTH = 4
HEAD_DIM = 64
CHUNK = 64
N_LEFT = 8
REL_CLIP = 256
A_W = 512
LRU_BLOCKS = 4
LRU_BW = 256
LRU_C = 8.0
D_FF = 2816
RMS_EPS = 1e-6
LANES = 128
SUBLANES = 8
VMEM_LIMIT = 56 * 1024 * 1024

QB_A = 2 * CHUNK
QSUB_A = 2
KW_A = QB_A + N_LEFT * CHUNK
PAD_A = N_LEFT * CHUNK
EXT_A = 768
SB_BLK = 256
SB_DEAD = -110.0

ADAM_LR, ADAM_B1, ADAM_B2, ADAM_EPS, ADAM_WD, ADAM_STEP = 0.001, 0.9, 0.999, 1e-08, 0.01, 10


def _cparams(sem):
    return pltpu.CompilerParams(dimension_semantics=sem, vmem_limit_bytes=VMEM_LIMIT)


def _gemm(name, operands, in_specs, o_spec, out_shape, grid, dims, acc_shape, into=None):
    nred = grid[2]
    npair = len(operands) // 2
    nin = 2 * npair + (into is not None)

    def body(*refs):
        o_ref = refs[nin]
        p = None
        for t in range(npair):
            d = lax.dot_general(refs[2 * t][...], refs[2 * t + 1][...], (dims, ((), ())),
                                preferred_element_type=f32)
            p = d if p is None else p + d
        if nred == 1:
            o_ref[...] = p.astype(o_ref.dtype)
        else:
            acc = refs[nin + 1]
            r = pl.program_id(2)

            @pl.when(r == 0)
            def _():
                acc[...] = p

            @pl.when(r > 0)
            def _():
                acc[...] += p

            @pl.when(r == nred - 1)
            def _():
                o_ref[...] = acc[...].astype(o_ref.dtype)

    scratch = [] if nred == 1 else [pltpu.VMEM(acc_shape, f32)]
    extra, alias = ([], {}) if into is None else ([into], {2 * npair: 0})
    return pl.pallas_call(
        body, grid=grid, in_specs=list(in_specs) + [pl.BlockSpec(memory_space=pl.ANY)] * len(extra),
        out_specs=o_spec, out_shape=out_shape, scratch_shapes=scratch, name=name, input_output_aliases=alias,
        compiler_params=_cparams(("parallel", "parallel", "arbitrary")))(*operands, *extra)


class _Fresh:
    def __init__(self, shape):
        self.shape = tuple(shape)


def _into(buf):
    return None if isinstance(buf, _Fresh) else buf


NN = ((1,), (0,))
NT = ((1,), (1,))
TN = ((0,), (0,))


def _tile(t, want=1024):
    return min(want, t)


def _mm_cols(name, a, w, l, out_dtype):
    t, k = a.shape
    _, s, _, ns = w.shape
    tm = _tile(t)
    return _gemm(
        name, [a, w],
        [pl.BlockSpec((tm, k), lambda i, j, r: (i, 0)),
         pl.BlockSpec((None, None, k, ns), lambda i, j, r: (l, j, 0, 0))],
        pl.BlockSpec((tm, ns), lambda i, j, r: (i, j)),
        SDS((t, s * ns), out_dtype), (t // tm, s, 1), NN, None)


def _mm_cols_t(name, dy, w, l, out_dtype):
    t = dy.shape[0]
    _, s, k, ns = w.shape
    tm = _tile(t)
    return _gemm(
        name, [dy, w],
        [pl.BlockSpec((tm, ns), lambda i, j, r: (i, r)),
         pl.BlockSpec((None, None, k, ns), lambda i, j, r: (l, r, 0, 0))],
        pl.BlockSpec((tm, k), lambda i, j, r: (i, 0)),
        SDS((t, k), out_dtype), (t // tm, 1, s), NT, (tm, k))


def _mm_wgrad_cols(name, a, dy, buf, l):
    t, k = a.shape
    _, s, _, ns = buf.shape
    tt = _tile(t)
    return _gemm(
        name, [a, dy],
        [pl.BlockSpec((tt, k), lambda i, j, r: (r, 0)),
         pl.BlockSpec((tt, ns), lambda i, j, r: (r, i))],
        pl.BlockSpec((None, None, k, ns), lambda i, j, r: (l, i, 0, 0)),
        SDS(buf.shape, f32), (s, 1, t // tt), TN, (k, ns), into=_into(buf))


def _mm_rows(name, parts, w, l, out_dtype):
    t = parts[0].shape[0]
    n = w.shape[3]
    tm = _tile(t)
    ops, specs = [], []
    for p_i, a in enumerate(parts):
        kp = a.shape[1]
        ops += [a, w]
        specs += [pl.BlockSpec((tm, kp), lambda i, j, r: (i, 0)),
                  pl.BlockSpec((None, None, kp, n), lambda i, j, r, p_i=p_i: (l, 0, p_i, 0))]
    return _gemm(name, ops, specs, pl.BlockSpec((tm, n), lambda i, j, r: (i, 0)),
                 SDS((t, n), out_dtype), (t // tm, 1, 1), NN, None)


def _mm_rows_t(name, dy, w, l, out_dtype):
    t, n = dy.shape
    k = w.shape[2]
    tm = _tile(t)
    return _gemm(
        name, [dy, w],
        [pl.BlockSpec((tm, n), lambda i, j, r: (i, 0)),
         pl.BlockSpec((None, None, k, n), lambda i, j, r: (l, 0, 0, 0))],
        pl.BlockSpec((tm, k), lambda i, j, r: (i, 0)),
        SDS((t, k), out_dtype), (t // tm, 1, 1), NT, None)


def _mm_wgrad(name, a, dy, buf, l, part=0):
    t, k = a.shape
    n = dy.shape[1]
    tt = _tile(t)
    return _gemm(
        name, [a, dy],
        [pl.BlockSpec((tt, k), lambda i, j, r: (r, 0)),
         pl.BlockSpec((tt, n), lambda i, j, r: (r, 0))],
        pl.BlockSpec((None, k, n), lambda i, j, r: (l, part, 0)),
        SDS(buf.shape, f32), (1, 1, t // tt), TN, (k, n), into=_into(buf))


def _ffn_up(h, wg, wu, l, gather):
    t, k = h.shape
    s, fs = wg.shape[1], wg.shape[3]
    tm = _tile(t)

    def body(h_ref, wg_ref, wu_ref, g_ref, u_ref, hid_ref):
        hv = h_ref[...]
        g = jnp.dot(hv, wg_ref[...], preferred_element_type=f32)
        u = jnp.dot(hv, wu_ref[...], preferred_element_type=f32)
        g_ref[...] = g.astype(bf16)
        u_ref[...] = u.astype(bf16)
        hid_ref[...] = (g * jax.nn.sigmoid(g) * u).astype(bf16)

    wspec = pl.BlockSpec((None, None, k, fs), lambda j, i: (l, j, 0, 0))
    ospec = pl.BlockSpec((None, tm, fs), lambda j, i: (j, i, 0))
    return _call(
        body, [h, wg, wu], grid=(s, t // tm), name="ffn_up",
        in_specs=[pl.BlockSpec((tm, k), lambda j, i: (i, 0)), wspec, wspec],
        out_specs=[ospec, ospec, ospec], out_shape=[SDS((s, t, fs), bf16)] * 3,
        sem=("parallel", "parallel"), gather=gather)


def _ffn_down(hid, wd, l):
    s, t, fs = hid.shape
    n = wd.shape[3]
    tm = _tile(t, 512)
    ops, specs = [], []
    for r in range(s):
        ops += [hid, wd]
        specs += [pl.BlockSpec((None, tm, fs), lambda i, j, k, r=r: (r, i, 0)),
                  pl.BlockSpec((None, None, fs, n), lambda i, j, k, r=r: (l, r, 0, 0))]
    return _gemm("ffn_down", ops, specs, pl.BlockSpec((tm, n), lambda i, j, k: (i, 0)),
                 SDS((t, n), f32), (t // tm, 1, 1), NN, None)


def _ffn_down_bwd(df, wd, l, g, u):
    t, n = df.shape
    s, fs = wd.shape[1], wd.shape[2]
    tm = _tile(t)

    def body(df_ref, wd_ref, g_ref, u_ref, dg_ref, du_ref):
        dh = lax.dot_general(df_ref[...], wd_ref[...], (NT, ((), ())), preferred_element_type=f32)
        gv = g_ref[...].astype(f32)
        uv = u_ref[...].astype(f32)
        sg = jax.nn.sigmoid(gv)
        du_ref[...] = (dh * gv * sg).astype(bf16)
        dg_ref[...] = (dh * uv * (sg * (1.0 + gv * (1.0 - sg)))).astype(bf16)

    bspec = pl.BlockSpec((None, tm, fs), lambda j, i: (j, i, 0))
    return pl.pallas_call(
        body, grid=(s, t // tm), name="ffn_down_bwd",
        in_specs=[pl.BlockSpec((tm, n), lambda j, i: (i, 0)),
                  pl.BlockSpec((None, None, fs, n), lambda j, i: (l, j, 0, 0)), bspec, bspec],
        out_specs=[bspec, bspec], out_shape=[SDS((s, t, fs), bf16)] * 2,
        compiler_params=_cparams(("parallel", "parallel")))(df, wd, g, u)


def _ffn_up_bwd(dg, du, wg, wu, l):
    s, t, fs = dg.shape
    k = wg.shape[2]
    tm = _tile(t, 512)
    ops, specs = [], []
    for r in range(s):
        aspec = pl.BlockSpec((None, tm, fs), lambda i, j, kk, r=r: (r, i, 0))
        wspec = pl.BlockSpec((None, None, k, fs), lambda i, j, kk, r=r: (l, r, 0, 0))
        ops += [dg, wg, du, wu]
        specs += [aspec, wspec, aspec, wspec]
    return _gemm("ffn_up_bwd", ops, specs, pl.BlockSpec((tm, k), lambda i, j, kk: (i, 0)),
                 SDS((t, k), f32), (t // tm, 1, 1), NT, None)


def _ffn_wgrad_up(h, dg, du, buf_g, buf_u, l):
    t, k = h.shape
    s, _, fs = dg.shape
    tt = _tile(t)
    nred = t // tt

    fresh = isinstance(buf_g, _Fresh)

    def body(*refs):
        h_ref, dg_ref, du_ref = refs[:3]
        og_ref, ou_ref, acc_g, acc_u = refs[-4:]
        r = pl.program_id(1)
        ht = h_ref[...].T
        pg = jnp.dot(ht, dg_ref[...], preferred_element_type=f32)
        pu = jnp.dot(ht, du_ref[...], preferred_element_type=f32)

        @pl.when(r == 0)
        def _():
            acc_g[...] = pg
            acc_u[...] = pu

        @pl.when(r > 0)
        def _():
            acc_g[...] += pg
            acc_u[...] += pu

        @pl.when(r == nred - 1)
        def _():
            og_ref[...] = acc_g[...]
            ou_ref[...] = acc_u[...]

    dspec = pl.BlockSpec((None, tt, fs), lambda i, r: (i, r, 0))
    ospec = pl.BlockSpec((None, None, k, fs), lambda i, r: (l, i, 0, 0))
    extra, alias = ([], {}) if fresh else ([buf_g, buf_u], {3: 0, 4: 1})
    return pl.pallas_call(
        body, grid=(s, nred), name="ffn_wgrad_up",
        in_specs=[pl.BlockSpec((tt, k), lambda i, r: (r, 0)), dspec, dspec] + [ANY] * len(extra),
        out_specs=[ospec, ospec], out_shape=[SDS(buf_g.shape, f32), SDS(buf_u.shape, f32)],
        scratch_shapes=[pltpu.VMEM((k, fs), f32)] * 2, input_output_aliases=alias,
        compiler_params=_cparams(("parallel", "arbitrary")))(h, dg, du, *extra)


def _ffn_wgrad_down(hid, df, buf, l):
    s, t, fs = hid.shape
    n = df.shape[1]
    tt = _tile(t)
    return _gemm(
        "ffn_wgrad_down", [hid, df],
        [pl.BlockSpec((None, tt, fs), lambda i, j, r: (i, r, 0)),
         pl.BlockSpec((tt, n), lambda i, j, r: (r, 0))],
        pl.BlockSpec((None, None, fs, n), lambda i, j, r: (l, i, 0, 0)),
        SDS(buf.shape, f32), (s, 1, t // tt), TN, (fs, n), into=_into(buf))


def _rows(name, fn, rows, consts, row_outs, acc_outs=(), tr=512):
    rows = [r if isinstance(r, tuple) else (r, r.shape[1], 0) for r in rows]
    t = rows[0][0].shape[0]
    tr = max(d for d in range(SUBLANES, min(tr, t) + 1, SUBLANES) if t % d == 0)
    nin = len(rows) + len(consts)
    no, na = len(row_outs), len(acc_outs)

    def body(*refs):
        vals = fn(*[r[...] for r in refs[:nin]])
        if not isinstance(vals, (tuple, list)):
            vals = (vals,)
        for k in range(no):
            refs[nin + k][...] = vals[k].astype(refs[nin + k].dtype)
        first = pl.program_id(0) == 0
        for k in range(na):
            ref, val = refs[nin + no + k], vals[no + k]

            @pl.when(first)
            def _(ref=ref, val=val):
                ref[...] = val

            @pl.when(jnp.logical_not(first))
            def _(ref=ref, val=val):
                ref[...] += val

    in_specs = [pl.BlockSpec((tr, w), lambda i, cb=cb: (i, cb)) for (_, w, cb) in rows]
    in_specs += [pl.BlockSpec(c.shape, lambda i, nd=c.ndim: (0,) * nd) for c in consts]
    out_specs = [pl.BlockSpec((tr, w), lambda i: (i, 0)) for (w, _) in row_outs]
    out_specs += [pl.BlockSpec(s, lambda i, nd=len(s): (0,) * nd) for (s, _) in acc_outs]
    out_shape = [SDS((t, w), dt) for (w, dt) in row_outs] + [SDS(s, dt) for (s, dt) in acc_outs]
    res = pl.pallas_call(
        body, grid=(t // tr,), in_specs=in_specs, out_specs=out_specs, out_shape=out_shape,
        name=name, compiler_params=_cparams(("arbitrary",)))(*[r[0] for r in rows], *consts)
    return res


def _rstd(x):
    return lax.rsqrt(jnp.mean(x * x, axis=-1, keepdims=True) + RMS_EPS)


def _norm_fwd(x, g):
    return x * _rstd(x) * g


def _norm_bwd(u, dy, g):
    r = _rstd(u)
    n = u * r
    dn = dy * g
    du = r * (dn - n * jnp.mean(dn * n, axis=-1, keepdims=True))
    return du, jnp.sum(dy * n, axis=0, keepdims=True)


def _gelu(x):
    c = 0.7978845608028654
    return 0.5 * x * (1.0 + jnp.tanh(c * (x + 0.044715 * x * x * x)))


def _gelu_grad(x):
    c = 0.7978845608028654
    th = jnp.tanh(c * (x + 0.044715 * x * x * x))
    return 0.5 * (1.0 + th) + 0.5 * x * (1.0 - th * th) * c * (1.0 + 3.0 * 0.044715 * x * x)


def _mask_heads(x):
    lane = lax.broadcasted_iota(jnp.int32, x.shape, 1)
    return [jnp.where((lane >= h * HEAD_DIM) & (lane < (h + 1) * HEAD_DIM), x, jnp.zeros_like(x))
            for h in range(LANES // HEAD_DIM)]


def _chunk_valid(start):
    qi = lax.broadcasted_iota(jnp.int32, (QB_A, KW_A), 0)
    kj = lax.broadcasted_iota(jnp.int32, (QB_A, KW_A), 1)
    qc = qi // CHUNK
    kc = kj // CHUNK
    return (kc >= qc) & (kc <= qc + N_LEFT) & (kj + start >= PAD_A)


def _chunk_probs(q, k, bias, valid):
    s = lax.dot_general(q, k, (NT, ((), ())), preferred_element_type=f32) * (HEAD_DIM ** -0.5) + bias
    s = jnp.where(valid, s, -1e30)
    p = jnp.exp(s - jnp.max(s, axis=-1, keepdims=True))
    return p / jnp.sum(p, axis=-1, keepdims=True)


def _chunk_attn_fwd(proj, kpad, vpad, bias, gather):
    t = proj.shape[0]
    tp = kpad.shape[0]
    step = QSUB_A * QB_A

    def body(q_ref, k_ref, v_ref, b_ref, o_ref):
        for sb in range(QSUB_A):
            start = pl.multiple_of((pl.program_id(1) * QSUB_A + sb) * QB_A, QB_A)
            rows = pl.ds(sb * QB_A, QB_A)
            valid = _chunk_valid(start)
            kw = k_ref[pl.ds(start, KW_A), :]
            qm = _mask_heads(q_ref[rows, :])
            vm = _mask_heads(v_ref[pl.ds(start, KW_A), :])
            o = None
            for h in range(len(qm)):
                p = _chunk_probs(qm[h], kw, b_ref[h], valid)
                d = jnp.dot(p.astype(bf16), vm[h], preferred_element_type=f32)
                o = d if o is None else o + d
            o_ref[rows, :] = o.astype(bf16)

    kv_spec = pl.BlockSpec((tp, LANES), lambda hp, qb: (0, hp))
    outs, new = _call(
        body, [proj, kpad, vpad, bias], grid=(A_W // LANES, t // step), name="chunk_attn_fwd",
        in_specs=[pl.BlockSpec((step, LANES), lambda hp, qb: (qb, hp)), kv_spec, kv_spec,
                  pl.BlockSpec((2, QB_A, KW_A), lambda hp, qb: (hp, 0, 0))],
        out_specs=[pl.BlockSpec((step, LANES), lambda hp, qb: (qb, hp))],
        out_shape=[SDS((t, A_W), bf16)], sem=("parallel", "arbitrary"), gather=gather)
    return outs[0], new


def _chunk_attn_bwd(proj, kpad, vpad, bias, dout):
    t = proj.shape[0]
    tp = kpad.shape[0]
    step = QSUB_A * QB_A

    def body(q_ref, k_ref, v_ref, b_ref, do_ref, dq_ref, dk_ref, dv_ref, db_ref):
        qb = pl.program_id(1)

        @pl.when(qb == 0)
        def _():
            dk_ref[...] = jnp.zeros_like(dk_ref)
            dv_ref[...] = jnp.zeros_like(dv_ref)
            db_ref[...] = jnp.zeros_like(db_ref)

        for sb in range(QSUB_A):
            start = pl.multiple_of((qb * QSUB_A + sb) * QB_A, QB_A)
            rows = pl.ds(sb * QB_A, QB_A)
            win = pl.ds(start, KW_A)
            valid = _chunk_valid(start)
            kw = k_ref[win, :]
            vw = v_ref[win, :]
            qm = _mask_heads(q_ref[rows, :])
            dom = _mask_heads(do_ref[rows, :])
            km = _mask_heads(kw)
            dq = dk = dv = None
            for h in range(len(qm)):
                p = _chunk_probs(qm[h], kw, b_ref[h], valid)
                dp = lax.dot_general(dom[h], vw, (NT, ((), ())), preferred_element_type=f32)
                ds = p * (dp - jnp.sum(dp * p, axis=-1, keepdims=True))
                db_ref[h] += ds
                dsb = (ds * (HEAD_DIM ** -0.5)).astype(bf16)
                terms = (jnp.dot(dsb, km[h], preferred_element_type=f32),
                         lax.dot_general(dsb, qm[h], (TN, ((), ())), preferred_element_type=f32),
                         lax.dot_general(p.astype(bf16), dom[h], (TN, ((), ())), preferred_element_type=f32))
                dq, dk, dv = terms if dq is None else (dq + terms[0], dk + terms[1], dv + terms[2])
            dq_ref[rows, :] = dq.astype(bf16)
            dk_ref[win, :] += dk
            dv_ref[win, :] += dv

    kv_spec = pl.BlockSpec((tp, LANES), lambda hp, qb: (0, hp))
    q_spec = pl.BlockSpec((step, LANES), lambda hp, qb: (qb, hp))
    b_spec = pl.BlockSpec((2, QB_A, KW_A), lambda hp, qb: (hp, 0, 0))
    return pl.pallas_call(
        body, grid=(A_W // LANES, t // step), name="chunk_attn_bwd",
        in_specs=[q_spec, kv_spec, kv_spec, b_spec, q_spec],
        out_specs=[q_spec, kv_spec, kv_spec, b_spec],
        out_shape=[SDS((t, A_W), bf16), SDS((tp, A_W), f32), SDS((tp, A_W), f32),
                   SDS((2 * A_W // LANES, QB_A, KW_A), f32)],
        compiler_params=_cparams(("parallel", "arbitrary")))(proj, kpad, vpad, bias, dout)


def _bias_ext(table):
    flat = PAD_A + QB_A - 1 - REL_CLIP
    top = jnp.broadcast_to(table[:, 2 * REL_CLIP:], (table.shape[0], flat))
    lo = 2 * REL_CLIP - (EXT_A - 1 - flat)
    return jnp.concatenate([top, jnp.flip(table[:, lo:], axis=1)], axis=1)


def _bias_window(table):
    nh = table.shape[0]
    e = jnp.broadcast_to(_bias_ext(table)[:, None, :], (nh, QB_A, EXT_A)).reshape(nh, QB_A * EXT_A)
    m = e[:, :QB_A * (EXT_A - 1)].reshape(nh, QB_A, EXT_A - 1)
    return m[:, :, QB_A - 1:]


def _bias_window_grad(dbias):
    nh = dbias.shape[0]
    m = jnp.pad(dbias, ((0, 0), (0, 0), (QB_A - 1, 0))).reshape(nh, QB_A * (EXT_A - 1))
    dext = jnp.sum(jnp.pad(m, ((0, 0), (0, QB_A))).reshape(nh, QB_A, EXT_A), axis=1)
    flat = PAD_A + QB_A - 1 - REL_CLIP
    lo = 2 * REL_CLIP - (EXT_A - 1 - flat)
    tail = jnp.flip(dext[:, flat:], axis=1)
    tail = tail.at[:, -1].add(jnp.sum(dext[:, :flat], axis=1))
    return jnp.pad(tail, ((0, 0), (lo, 0)))


def _tri_suffix(x, tri):
    hi = x.astype(bf16)
    lo = (x - hi.astype(f32)).astype(bf16)
    return jnp.dot(hi, tri, preferred_element_type=f32) + jnp.dot(lo, tri, preferred_element_type=f32)


def _sb_block(q, k, run, tri, causal):
    z = lax.dot_general(q, k, (NT, ((), ())), preferred_element_type=f32) * (HEAD_DIM ** -0.5)
    e = jnp.exp(-jnp.abs(z))
    l1p = jnp.log(1.0 + e)
    lb = jnp.minimum(z, 0.0) - l1p
    lmb = lb - z
    if causal is not None:
        lmb = jnp.where(causal, lmb, 0.0)
    cs = _tri_suffix(lmb, tri)
    w = jnp.exp(lb + (run + cs - lmb))
    if causal is not None:
        w = jnp.where(causal, w, 0.0)
    return z, e, w, run + cs[:, 0:1]


def _sb_tri():
    r = lax.broadcasted_iota(jnp.int32, (SB_BLK, SB_BLK), 0)
    c = lax.broadcasted_iota(jnp.int32, (SB_BLK, SB_BLK), 1)
    return (r >= c).astype(bf16), c < r


def _sb_live(runs):
    m = runs[0]
    for r in runs[1:]:
        m = jnp.maximum(m, r)
    return jnp.max(m) > SB_DEAD


def _sb_fwd(proj, gather):
    t = proj.shape[0]
    cb = A_W // LANES
    nh = LANES // HEAD_DIM

    def body(q_ref, k_ref, v_ref, o_ref, of_ref):
        qb = pl.program_id(1)
        tri, diag = _sb_tri()
        qm = _mask_heads(q_ref[...])

        def pair(kb, carry, causal):
            rows = pl.ds(pl.multiple_of(kb * SB_BLK, SB_BLK), SB_BLK)
            k = k_ref[rows, :]
            vm = _mask_heads(v_ref[rows, :])
            runs, acc = [], carry[nh]
            for h in range(nh):
                _, _, w, run = _sb_block(qm[h], k, carry[h], tri, causal)
                acc = acc + jnp.dot(w.astype(bf16), vm[h], preferred_element_type=f32)
                runs.append(run)
            return (*runs, acc)

        zero = jnp.zeros((SB_BLK, 1), f32)
        carry = pair(qb, (zero,) * nh + (jnp.zeros((SB_BLK, LANES), f32),), diag)

        def cond(st):
            return (st[0] < qb) & _sb_live(st[1][:nh])

        def step(st):
            return st[0] + 1, pair(qb - 1 - st[0], st[1], None)

        _, carry = lax.while_loop(cond, step, (jnp.int32(0), carry))
        o_ref[...] = carry[nh].astype(bf16)
        of_ref[...] = carry[nh]

    ospec = pl.BlockSpec((SB_BLK, LANES), lambda hp, qb: (qb, hp))
    return _call(
        body, [proj, proj, proj], grid=(cb, t // SB_BLK), name="sb_attn_fwd",
        in_specs=[pl.BlockSpec((SB_BLK, LANES), lambda hp, qb: (qb, 3 * cb + hp)),
                  pl.BlockSpec((t, LANES), lambda hp, qb: (0, 4 * cb + hp)),
                  pl.BlockSpec((t, LANES), lambda hp, qb: (0, 5 * cb + hp))],
        out_specs=[ospec, ospec], out_shape=[SDS((t, A_W), bf16), SDS((t, A_W), f32)],
        sem=("parallel", "arbitrary"), gather=gather)


def _sb_bwd(proj, out_b, dout):
    t = proj.shape[0]
    cb = A_W // LANES
    nh = LANES // HEAD_DIM

    def body(q_ref, k_ref, v_ref, o_ref, do_ref, dq_ref, dk_ref, dv_ref):
        qb = pl.program_id(1)
        tri, diag = _sb_tri()

        @pl.when(qb == 0)
        def _():
            dk_ref[...] = jnp.zeros_like(dk_ref)
            dv_ref[...] = jnp.zeros_like(dv_ref)

        qm = _mask_heads(q_ref[...])
        do = do_ref[...]
        dom = _mask_heads(do)
        dsums = [jnp.sum(t_, axis=-1, keepdims=True) for t_ in _mask_heads(do.astype(f32) * o_ref[...])]

        def pair(kb, carry, causal):
            rows = pl.ds(pl.multiple_of(kb * SB_BLK, SB_BLK), SB_BLK)
            k = k_ref[rows, :]
            v = v_ref[rows, :]
            km = _mask_heads(k)
            new, dq, dk, dv = [], carry[2 * nh], None, None
            for h in range(nh):
                z, e, w, run = _sb_block(qm[h], k, carry[2 * h], tri, causal)
                inv = 1.0 / (1.0 + e)
                beta = jnp.where(z >= 0.0, inv, e * inv)
                wb = w.astype(bf16)
                g = lax.dot_general(dom[h], v, (NT, ((), ())), preferred_element_type=f32) * wb.astype(f32)
                sg = _tri_suffix(g, tri)
                dz = g * (1.0 - beta) - (dsums[h] - carry[2 * h + 1] - sg) * beta
                if causal is not None:
                    dz = jnp.where(causal, dz, 0.0)
                dzb = (dz * (HEAD_DIM ** -0.5)).astype(bf16)
                dq = dq + jnp.dot(dzb, km[h], preferred_element_type=f32)
                tk = lax.dot_general(dzb, qm[h], (TN, ((), ())), preferred_element_type=f32)
                tv = lax.dot_general(wb, dom[h], (TN, ((), ())), preferred_element_type=f32)
                dk, dv = (tk, tv) if dk is None else (dk + tk, dv + tv)
                new += [run, carry[2 * h + 1] + sg[:, 0:1]]
            dk_ref[rows, :] += dk
            dv_ref[rows, :] += dv
            return (*new, dq)

        zero = jnp.zeros((SB_BLK, 1), f32)
        carry = pair(qb, (zero,) * (2 * nh) + (jnp.zeros((SB_BLK, LANES), f32),), diag)

        def cond(st):
            return (st[0] < qb) & _sb_live(st[1][0:2 * nh:2])

        def step(st):
            return st[0] + 1, pair(qb - 1 - st[0], st[1], None)

        _, carry = lax.while_loop(cond, step, (jnp.int32(0), carry))
        dq_ref[...] = carry[2 * nh].astype(bf16)

    kv_in = lambda seg: pl.BlockSpec((t, LANES), lambda hp, qb: (0, seg * cb + hp))
    q_spec = pl.BlockSpec((SB_BLK, LANES), lambda hp, qb: (qb, hp))
    kv_out = pl.BlockSpec((t, LANES), lambda hp, qb: (0, hp))
    return pl.pallas_call(
        body, grid=(cb, t // SB_BLK), name="sb_attn_bwd",
        in_specs=[pl.BlockSpec((SB_BLK, LANES), lambda hp, qb: (qb, 3 * cb + hp)), kv_in(4), kv_in(5),
                  q_spec, pl.BlockSpec((SB_BLK, LANES), lambda hp, qb: (qb, cb + hp))],
        out_specs=[q_spec, kv_out, kv_out],
        out_shape=[SDS((t, A_W), bf16), SDS((t, A_W), f32), SDS((t, A_W), f32)],
        compiler_params=_cparams(("parallel", "arbitrary")))(proj, proj, proj, out_b, dout)


def _halo_specs(tr, w, col, nblk):
    per = tr // SUBLANES
    cur = pl.BlockSpec((tr, w), lambda i: (i, col))
    prev = pl.BlockSpec((SUBLANES, w), lambda i: (jnp.maximum(i * per - 1, 0), col))
    nxt = pl.BlockSpec((SUBLANES, w), lambda i: (jnp.minimum((i + 1) * per, nblk * per - 1), col))
    return cur, prev, nxt


def _taps_before(cur, prev8, first):
    prev8 = jnp.where(first, 0.0, prev8)
    ext = jnp.concatenate([prev8, cur], axis=0)
    return [pltpu.roll(ext, s, 0)[SUBLANES:] for s in (3, 2, 1)]


def _taps_after(cur, next8, last):
    n = cur.shape[0]
    next8 = jnp.where(last, 0.0, next8)
    ext = jnp.concatenate([cur, next8], axis=0)
    return [pltpu.roll(ext, n + SUBLANES - s, 0)[:n] for s in (1, 2, 3)]


def _block_diag(x, w_ref, dims):
    outs = [lax.dot_general(x[:, n * LRU_BW:(n + 1) * LRU_BW], w_ref[n], (dims, ((), ())),
                            preferred_element_type=f32) for n in range(LRU_BLOCKS)]
    return jnp.concatenate(outs, axis=1)


def _lru_gates(xc, wa_ref, wi_ref, ba, bi, lam):
    xb = xc.astype(bf16)
    r = jax.nn.sigmoid(_block_diag(xb, wa_ref, NN) + ba)
    ig = jax.nn.sigmoid(_block_diag(xb, wi_ref, NN) + bi)
    sp = jnp.maximum(-lam, 0.0) + jnp.log(1.0 + jnp.exp(-jnp.abs(lam)))
    log_a = -LRU_C * r * sp
    a = jnp.exp(log_a)
    x2 = 2.0 * log_a
    one_minus = jnp.where(x2 > -1e-2, -x2 * (1.0 + x2 * (0.5 + x2 * (1.0 / 6.0))), 1.0 - a * a)
    mult = jnp.sqrt(one_minus)
    return xb, r, ig, sp, a, mult


def _rg_gates_fwd(proj, conv_w, conv_b, wa, wi, ba, bi, lam, tr=512):
    t = proj.shape[0]
    w = D_MODEL
    tr = min(tr, t)
    nblk = t // tr
    cur, prev, _ = _halo_specs(tr, w, 1, nblk)

    def body(x_ref, xp_ref, cw_ref, cb_ref, wa_ref, wi_ref, ba_ref, bi_ref, lam_ref, xc_ref, a_ref, u_ref):
        x = x_ref[...]
        taps = _taps_before(x, xp_ref[...], pl.program_id(0) == 0) + [x]
        xc = cb_ref[...]
        for k in range(4):
            xc = xc + cw_ref[k:k + 1, :] * taps[k]
        _, _, ig, _, a, mult = _lru_gates(xc, wa_ref, wi_ref, ba_ref[...], bi_ref[...], lam_ref[...])
        xc_ref[...] = xc
        a_ref[...] = a
        u_ref[...] = mult * (ig * xc)

    full = lambda a_: pl.BlockSpec(a_.shape, lambda i, nd=a_.ndim: (0,) * nd)
    ospec = pl.BlockSpec((tr, w), lambda i: (i, 0))
    return pl.pallas_call(
        body, grid=(nblk,), name="rg_gates_fwd",
        in_specs=[cur, prev] + [full(a_) for a_ in (conv_w, conv_b, wa, wi, ba, bi, lam)],
        out_specs=[ospec] * 3, out_shape=[SDS((t, w), f32)] * 3,
        compiler_params=_cparams(("parallel",)))(proj, proj, conv_w, conv_b, wa, wi, ba, bi, lam)


def _lru_scan(name, a, b, reverse, tt=512):
    t, w = a.shape
    tt = min(tt, t)
    nt = t // tt
    ng = tt // SUBLANES

    def body(a_ref, b_ref, h_ref, carry_ref):
        @pl.when(pl.program_id(0) == 0)
        def _():
            carry_ref[...] = jnp.zeros_like(carry_ref)

        row = lax.broadcasted_iota(jnp.int32, (SUBLANES, w), 0)

        def group(gi, carry):
            g = (ng - 1 - gi) if reverse else gi
            rows = pl.ds(pl.multiple_of(g * SUBLANES, SUBLANES), SUBLANES)
            av = a_ref[rows, :]
            bv = b_ref[rows, :]
            for s in (1, 2, 4):
                sh = (SUBLANES - s) if reverse else s
                ok = (row < SUBLANES - s) if reverse else (row >= s)
                a_s = pltpu.roll(av, sh, 0)
                b_s = pltpu.roll(bv, sh, 0)
                bv = jnp.where(ok, av * b_s + bv, bv)
                av = jnp.where(ok, av * a_s, av)
            h = av * carry + bv
            h_ref[rows, :] = h
            edge = h[0:1, :] if reverse else h[SUBLANES - 1:SUBLANES, :]
            return jnp.broadcast_to(edge, (SUBLANES, w))

        carry_ref[...] = lax.fori_loop(0, ng, group, carry_ref[...])

    tmap = (lambda i: (nt - 1 - i, 0)) if reverse else (lambda i: (i, 0))
    spec = pl.BlockSpec((tt, w), tmap)
    return pl.pallas_call(
        body, grid=(nt,), name=name, in_specs=[spec, spec], out_specs=spec,
        out_shape=SDS((t, w), f32), scratch_shapes=[pltpu.VMEM((SUBLANES, w), f32)],
        compiler_params=_cparams(("arbitrary",)))(a, b)


def _rg_gates_bwd(dhs, c, hs, xc, wa, wi, ba, bi, lam, tr=256):
    t, w = xc.shape
    tr = min(tr, t)
    nblk = t // tr
    cur, prev, nxt = _halo_specs(tr, w, 0, nblk)

    def body(dhs_ref, c_ref, cn_ref, hs_ref, hp_ref, xc_ref, wa_ref, wi_ref, ba_ref, bi_ref, lam_ref,
             dxc_ref, dwa_ref, dwi_ref, dba_ref, dbi_ref, dlam_ref):
        i = pl.program_id(0)
        c_next = _taps_after(c_ref[...], cn_ref[...], i == nblk - 1)[0]
        h_prev = _taps_before(hs_ref[...], hp_ref[...], i == 0)[2]
        xc = xc_ref[...]
        lam = lam_ref[...]
        xb, r, ig, sp, a, mult = _lru_gates(xc, wa_ref, wi_ref, ba_ref[...], bi_ref[...], lam)
        dh = dhs_ref[...] + c_next
        dlog_a = dh * h_prev * a - (dh * ig * xc) * (a * a / mult)
        dpre_a = (dlog_a * (-LRU_C * sp) * r * (1.0 - r)).astype(bf16)
        dpre_i = (dh * mult * xc * ig * (1.0 - ig)).astype(bf16)
        dxc_ref[...] = (dh * mult * ig + _block_diag(dpre_a, wa_ref, NT) + _block_diag(dpre_i, wi_ref, NT))
        dsig = 1.0 / (1.0 + jnp.exp(lam))
        sums = [jnp.sum(dpre_a.astype(f32), axis=0, keepdims=True),
                jnp.sum(dpre_i.astype(f32), axis=0, keepdims=True),
                jnp.sum(dlog_a * (-LRU_C * r), axis=0, keepdims=True) * (-dsig)]

        @pl.when(i == 0)
        def _():
            dwa_ref[...] = jnp.zeros_like(dwa_ref)
            dwi_ref[...] = jnp.zeros_like(dwi_ref)
            dba_ref[...] = jnp.zeros_like(dba_ref)
            dbi_ref[...] = jnp.zeros_like(dbi_ref)
            dlam_ref[...] = jnp.zeros_like(dlam_ref)

        for n in range(LRU_BLOCKS):
            sl = slice(n * LRU_BW, (n + 1) * LRU_BW)
            dwa_ref[n] += lax.dot_general(xb[:, sl], dpre_a[:, sl], (TN, ((), ())), preferred_element_type=f32)
            dwi_ref[n] += lax.dot_general(xb[:, sl], dpre_i[:, sl], (TN, ((), ())), preferred_element_type=f32)
        dba_ref[...] += sums[0]
        dbi_ref[...] += sums[1]
        dlam_ref[...] += sums[2]

    full = lambda a_: pl.BlockSpec(a_.shape, lambda i, nd=a_.ndim: (0,) * nd)
    vec = pl.BlockSpec((1, w), lambda i: (0, 0))
    mat = pl.BlockSpec((LRU_BLOCKS, LRU_BW, LRU_BW), lambda i: (0, 0, 0))
    return pl.pallas_call(
        body, grid=(nblk,), name="rg_gates_bwd",
        in_specs=[cur, cur, nxt, cur, prev, cur] + [full(a_) for a_ in (wa, wi, ba, bi, lam)],
        out_specs=[cur, mat, mat, vec, vec, vec],
        out_shape=[SDS((t, w), f32), SDS((LRU_BLOCKS, LRU_BW, LRU_BW), f32), SDS((LRU_BLOCKS, LRU_BW, LRU_BW), f32),
                   SDS((1, w), f32), SDS((1, w), f32), SDS((1, w), f32)],
        compiler_params=_cparams(("arbitrary",)))(dhs, c, c, hs, hs, xc, wa, wi, ba, bi, lam)


def _rg_conv_bwd(dxc, proj, conv_w, tr=512):
    t, w = dxc.shape
    tr = min(tr, t)
    nblk = t // tr
    cur, _, nxt = _halo_specs(tr, w, 0, nblk)
    xcur, xprev, _ = _halo_specs(tr, w, 1, nblk)

    def body(d_ref, dn_ref, x_ref, xp_ref, cw_ref, dx_ref, dcw_ref, dcb_ref):
        i = pl.program_id(0)
        d = d_ref[...]
        x = x_ref[...]
        after = _taps_after(d, dn_ref[...], i == nblk - 1)
        before = _taps_before(x, xp_ref[...], i == 0) + [x]
        dx = cw_ref[3:4, :] * d
        for s in (1, 2, 3):
            dx = dx + cw_ref[3 - s:4 - s, :] * after[s - 1]
        dx_ref[...] = dx.astype(bf16)
        dcw = jnp.concatenate([jnp.sum(d * before[k], axis=0, keepdims=True) for k in range(4)], axis=0)
        dcb = jnp.sum(d, axis=0, keepdims=True)

        @pl.when(i == 0)
        def _():
            dcw_ref[...] = dcw
            dcb_ref[...] = dcb

        @pl.when(i > 0)
        def _():
            dcw_ref[...] += dcw
            dcb_ref[...] += dcb

    return pl.pallas_call(
        body, grid=(nblk,), name="rg_conv_bwd",
        in_specs=[cur, nxt, xcur, xprev, pl.BlockSpec((4, w), lambda i: (0, 0))],
        out_specs=[cur, pl.BlockSpec((4, w), lambda i: (0, 0)), pl.BlockSpec((1, w), lambda i: (0, 0))],
        out_shape=[SDS((t, w), bf16), SDS((4, w), f32), SDS((1, w), f32)],
        compiler_params=_cparams(("arbitrary",)))(dxc, dxc, proj, proj, conv_w)


def _attn_fwd(h, wts, j, plan):
    proj = _mm_cols("attn_in", h, wts["attn_w_in"], j, bf16)
    kpad = jnp.pad(proj[:, A_W:2 * A_W], ((PAD_A, 0), (0, 0)))
    vpad = jnp.pad(proj[:, 2 * A_W:3 * A_W], ((PAD_A, 0), (0, 0)))
    bias = _bias_window(wts["attn_rel_bias"][j])
    plan = plan if j == 0 else None
    out_a = _carried(plan, "chunk_attn_fwd", wts, _chunk_attn_fwd, proj, kpad, vpad, bias)
    out_b, out_b32 = _carried(plan, "sb_attn_fwd", wts, _sb_fwd, proj)
    m = _mm_rows("attn_out", [out_a, out_b], wts["attn_w_out"], j, f32)
    return m, (proj, kpad, vpad, bias, out_a, out_b, out_b32)


def _attn_bwd(dm, h, saved, wts, j, grads):
    proj, kpad, vpad, bias, out_a, out_b, out_b32 = saved
    dout = _mm_rows_t("attn_out_t", dm, wts["attn_w_out"], j, bf16)
    grads["attn_w_out"] = _mm_wgrad("attn_out_wgrad_a", out_a, dm, grads["attn_w_out"], j, 0)
    grads["attn_w_out"] = _mm_wgrad("attn_out_wgrad_b", out_b, dm, grads["attn_w_out"], j, 1)
    dqa, dka, dva, dbias = _chunk_attn_bwd(proj, kpad, vpad, bias, dout)
    dqs, dks, dvs = _sb_bwd(proj, out_b32, dout)
    grads["attn_rel_bias"][j] = _bias_window_grad(dbias)
    dproj = jnp.concatenate([dqa, dka[PAD_A:].astype(bf16), dva[PAD_A:].astype(bf16),
                             dqs, dks.astype(bf16), dvs.astype(bf16)], axis=1)
    grads["attn_w_in"] = _mm_wgrad_cols("attn_in_wgrad", h, dproj, grads["attn_w_in"], j)
    return _mm_cols_t("attn_in_t", dproj, wts["attn_w_in"], j, f32)


def _rg_fwd(h, wts, j, plan):
    proj =_mm_cols("rg_in", h, wts["rg_w_in"], j, f32)
    small = [wts[k][j] for k in ("rg_conv_w", "rg_conv_b", "rg_w_a", "rg_w_i", "rg_b_a", "rg_b_i", "rg_lambda")]
    xc, a, u = _rg_gates_fwd(proj, *small)
    hs = _lru_scan("lru_scan_fwd", a, u, False)
    yp = _rows("rg_gate_out", lambda hv, gv: hv * _gelu(gv), [hs, (proj, D_MODEL, 0)], [], [(D_MODEL, bf16)])[0]
    m = _mm_rows("rg_out", [yp], wts["rg_w_out"], j, f32)
    return m, (proj, xc, a, hs, yp)


def _rg_bwd(dm, h, saved, wts, j, grads):
    proj, xc, a, hs, yp = saved
    dyp = _mm_rows_t("rg_out_t", dm, wts["rg_w_out"], j, f32)
    grads["rg_w_out"] = _mm_wgrad("rg_out_wgrad", yp, dm, grads["rg_w_out"], j)

    def gate_bwd(dy, hv, gv, av):
        dhs = dy * _gelu(gv)
        return dhs, av * dhs, dy * hv * _gelu_grad(gv)

    dhs, ab, dgate = _rows("rg_gate_out_bwd", gate_bwd, [dyp, hs, (proj, D_MODEL, 0), a], [],
                           [(D_MODEL, f32), (D_MODEL, f32), (D_MODEL, bf16)])
    c = _lru_scan("lru_scan_bwd", a, ab, True)
    wa, wi, ba, bi, lam = [wts[k][j] for k in ("rg_w_a", "rg_w_i", "rg_b_a", "rg_b_i", "rg_lambda")]
    dxc, dwa, dwi, dba, dbi, dlam = _rg_gates_bwd(dhs, c, hs, xc, wa, wi, ba, bi, lam)
    dxr, dcw, dcb = _rg_conv_bwd(dxc, proj, wts["rg_conv_w"][j])
    for k, v in (("rg_w_a", dwa), ("rg_w_i", dwi), ("rg_b_a", dba), ("rg_b_i", dbi), ("rg_lambda", dlam),
                 ("rg_conv_w", dcw), ("rg_conv_b", dcb)):
        grads[k][j] = v
    dproj = jnp.concatenate([dgate, dxr], axis=1)
    grads["rg_w_in"] = _mm_wgrad_cols("rg_in_wgrad", h, dproj, grads["rg_w_in"], j)
    return _mm_cols_t("rg_in_t", dproj, wts["rg_w_in"], j, f32)


def _local_step(x, target, wts, plan=None):
    t = x.shape[0]
    d = D_MODEL
    gains = {k: wts[k] for k in ("norm_mix_pre", "norm_mix_post", "norm_ffn_pre", "norm_ffn_post")}
    gain = lambda k, l: gains[k][l:l + 1]

    saved = []
    h = _rows("norm_in", _norm_fwd, [x], [gain("norm_mix_pre", 0)], [(d, bf16)])[0]
    loss_cols = None
    for l in range(DEPTH):
        j = l // 2
        m, mix_saved = (_attn_fwd if l % 2 == 0 else _rg_fwd)(h, wts, j, plan)

        def resid_next(xv, mv, g_post, g_next):
            x1 = xv + _norm_fwd(mv, g_post)
            return x1, _norm_fwd(x1, g_next)

        x1, h2 = _rows("resid_mix", resid_next, [x, m], [gain("norm_mix_post", l), gain("norm_ffn_pre", l)],
                       [(d, f32), (d, bf16)])
        g, u, hid = _carried(plan if l == 0 else None, "ffn_up", wts, _ffn_up, h2, wts["ffn_w_gate"],
                             wts["ffn_w_up"], l)
        f = _ffn_down(hid, wts["ffn_w_down"], l)
        saved.append((x, h, m, mix_saved, x1, h2, g, u, hid, f))
        if l + 1 < DEPTH:
            x, h = _rows("resid_ffn", resid_next, [x1, f], [gain("norm_ffn_post", l), gain("norm_mix_pre", l + 1)],
                         [(d, f32), (d, bf16)])
        else:
            def resid_loss(xv, fv, tv, g_post):
                err = xv + _norm_fwd(fv, g_post) - tv
                return err * (1.0 / d), jnp.sum(err * err, axis=0, keepdims=True)

            dx, loss_cols = _rows("resid_loss", resid_loss, [x1, f, target], [gain("norm_ffn_post", l)],
                                  [(d, f32)], [((1, d), f32)])
    loss = 0.5 * jnp.sum(loss_cols) / d

    grads = {k: {} for k in SMALL_GRADS}
    for k in BIG_GRADS:
        shp = wts[k].shape
        grads[k] = _Fresh((shp[0],) + shp[2:] if shp[1] == 1 else shp)

    def norm_bwd_cast(uv, dyv, gv):
        du, dg = _norm_bwd(uv, dyv, gv)
        return du, dg

    def norm_bwd_resid(uv, dhv, dxv, gv):
        du, dg = _norm_bwd(uv, dhv, gv)
        return dxv + du, dg

    for l in reversed(range(DEPTH)):
        j = l // 2
        x_in, h, m, mix_saved, x1, h2, g, u, hid, f = saved[l]
        df, grads["norm_ffn_post"][l] = _rows("norm_ffn_post_bwd", norm_bwd_cast, [f, dx], [gain("norm_ffn_post", l)],
                                              [(d, bf16)], [((1, d), f32)])
        dg, du = _ffn_down_bwd(df, wts["ffn_w_down"], l, g, u)
        grads["ffn_w_down"] = _ffn_wgrad_down(hid, df, grads["ffn_w_down"], l)
        dh2 = _ffn_up_bwd(dg, du, wts["ffn_w_gate"], wts["ffn_w_up"], l)
        grads["ffn_w_gate"], grads["ffn_w_up"] = _ffn_wgrad_up(h2, dg, du, grads["ffn_w_gate"], grads["ffn_w_up"], l)
        dx1, grads["norm_ffn_pre"][l] = _rows("norm_ffn_pre_bwd", norm_bwd_resid, [x1, dh2, dx],
                                              [gain("norm_ffn_pre", l)], [(d, f32)], [((1, d), f32)])
        dm, grads["norm_mix_post"][l] = _rows("norm_mix_post_bwd", norm_bwd_cast, [m, dx1], [gain("norm_mix_post", l)],
                                              [(d, bf16)], [((1, d), f32)])
        dh = (_attn_bwd if l % 2 == 0 else _rg_bwd)(dm, h, mix_saved, wts, j, grads)
        dx, grads["norm_mix_pre"][l] = _rows("norm_mix_pre_bwd", norm_bwd_resid, [x_in, dh, dx1],
                                             [gain("norm_mix_pre", l)], [(d, f32)], [((1, d), f32)])
    return loss, dx, grads


ANY = pl.BlockSpec(memory_space=pl.ANY)
PACK_COLS = 1024
SMALL_ROWS = 288


def _mesh_pos():
    x, y, c = lax.axis_index("x"), lax.axis_index("y"), lax.axis_index("c")
    return x, y, c, [(1 - x, y), (x, 1 - y), (1 - x, 1 - y)]


def _run_copies(copies):
    for cp in copies:
        cp.start()
    for cp in copies:
        cp.wait()


GATHER_SEMS = 7


def _gather_copies(items, ins, outs, send, recv):
    x, y, c, chips = _mesh_pos()
    q = 2 * x + y
    sibling = (x, y, 1 - c)

    def copy(k, src, dst, to):
        return pltpu.make_async_remote_copy(src_ref=src, dst_ref=dst, send_sem=send.at[k], recv_sem=recv.at[k],
                                            device_id=to, device_id_type=MESH)

    own, sent, passed = [], [], []
    for i, (t, l0, nl) in enumerate(items):
        lay = pl.ds(l0, nl)
        half = ins[t].shape[1] // 2
        rows = pl.ds(pl.multiple_of(c * half, half), half)
        own.append(copy(GATHER_SEMS * i, ins[t].at[lay], outs[t].at[lay, q], sibling))
        for j, (px, py) in enumerate(chips):
            sent.append(copy(GATHER_SEMS * i + 1 + j, ins[t].at[lay, rows], outs[t].at[lay, q, rows], (px, py, c)))
            landed = outs[t].at[lay, 2 * px + py, rows]
            passed.append(copy(GATHER_SEMS * i + 4 + j, landed, landed, sibling))
    return own, sent, passed


def _gather_start(items, ins, outs, send, recv):
    own, sent, _ = _gather_copies(items, ins, outs, send, recv)
    for cp in own + sent:
        cp.start()


def _gather_finish(items, ins, outs, send, recv):
    own, sent, passed = _gather_copies(items, ins, outs, send, recv)
    for arrived, forward in zip(sent, passed):
        arrived.wait_recv()
        forward.start()
    for cp in sent:
        cp.wait_send()
    for cp in own + passed:
        cp.wait()


def _gather_call(items, shards):
    n = len(shards)
    nsem = GATHER_SEMS * len(items)

    def body(*refs):
        ins, outs = refs[:n], refs[n:2 * n]
        _gather_start(items, ins, outs, *refs[2 * n:])
        _gather_finish(items, ins, outs, *refs[2 * n:])

    return pl.pallas_call(
        body, name="weight_all_gather", in_specs=[ANY] * n, out_specs=[ANY] * n,
        out_shape=[SDS((s.shape[0], N_CHIPS) + s.shape[1:], s.dtype) for s in shards],
        scratch_shapes=[pltpu.SemaphoreType.DMA((nsem,)), pltpu.SemaphoreType.DMA((nsem,))])(*shards)


def _call(body, operands, *, name, grid, in_specs, out_specs, out_shape, sem, scratch=(), gather=None):
    if gather is None:
        return pl.pallas_call(body, grid=grid, in_specs=in_specs, out_specs=out_specs, out_shape=out_shape,
                              scratch_shapes=list(scratch), name=name, compiler_params=_cparams(sem))(*operands), None
    items, shards, gathered = gather
    n_in, n_out, n_scr, ng = len(operands), len(out_shape), len(scratch), len(shards)
    nsem = GATHER_SEMS * len(items)

    def full(*refs):
        ins, sh = refs[:n_in], refs[n_in:n_in + ng]
        outs = refs[n_in + 2 * ng:n_in + 2 * ng + n_out]
        io = refs[n_in + 2 * ng + n_out:n_in + 3 * ng + n_out]
        scr = refs[n_in + 3 * ng + n_out:]
        ids = [pl.program_id(a) for a in range(len(grid))]
        first = functools.reduce(jnp.logical_and, [i == 0 for i in ids])
        last = functools.reduce(jnp.logical_and, [i == g - 1 for i, g in zip(ids, grid)])

        @pl.when(first)
        def _():
            _gather_start(items, sh, io, scr[n_scr], scr[n_scr + 1])

        body(*ins, *outs, *scr[:n_scr])

        @pl.when(last)
        def _():
            _gather_finish(items, sh, io, scr[n_scr], scr[n_scr + 1])

    res = pl.pallas_call(
        full, grid=grid, in_specs=list(in_specs) + [ANY] * (2 * ng), out_specs=list(out_specs) + [ANY] * ng,
        out_shape=list(out_shape) + [SDS(g.shape, g.dtype) for g in gathered],
        scratch_shapes=list(scratch) + [pltpu.SemaphoreType.DMA((nsem,)), pltpu.SemaphoreType.DMA((nsem,))],
        input_output_aliases={n_in + ng + t: n_out + t for t in range(ng)}, name=name,
        compiler_params=_cparams(("arbitrary",) * len(grid)))(*operands, *shards, *gathered)
    return res[:n_out], res[n_out:]


def _pair_exchange(gs):
    n = len(gs)

    def body(*refs):
        ins, outs = refs[:n], refs[n:2 * n]
        send, recv = refs[2 * n:]
        x, y, c, _ = _mesh_pos()
        copies = []
        for t in range(n):
            half = ins[t].shape[2] // 2
            src = ins[t].at[:, :, pl.ds(pl.multiple_of((1 - c) * half, SUBLANES), half)]
            copies.append(pltpu.make_async_remote_copy(
                src_ref=src, dst_ref=outs[t], send_sem=send.at[t], recv_sem=recv.at[t],
                device_id=(x, y, 1 - c), device_id_type=MESH))
        _run_copies(copies)

    return pl.pallas_call(
        body, name="grad_pair_exchange", in_specs=[ANY] * n, out_specs=[ANY] * n,
        out_shape=[SDS(g.shape[:2] + (g.shape[2] // 2, g.shape[3]), f32) for g in gs],
        scratch_shapes=[pltpu.SemaphoreType.DMA((n,)), pltpu.SemaphoreType.DMA((n,))])(*gs)


def _pair_sum(name, g, got, c):
    l, s, r, cols = g.shape

    def body(c_ref, a_ref, b_ref, o_ref):
        o_ref[...] = (a_ref[...] + b_ref[...]).astype(bf16)

    blk = (None, None, r // 2, cols)
    return pl.pallas_call(
        body, name=name, out_shape=SDS(got.shape, bf16),
        grid_spec=pltpu.PrefetchScalarGridSpec(
            num_scalar_prefetch=1, grid=(l, s),
            in_specs=[pl.BlockSpec(blk, lambda i, q, c_ref: (i, q, c_ref[0], 0)),
                      pl.BlockSpec(blk, lambda i, q, c_ref: (i, q, 0, 0))],
            out_specs=pl.BlockSpec(blk, lambda i, q, c_ref: (i, q, 0, 0))),
        compiler_params=_cparams(("parallel", "parallel")))(c, g, got)


def _chip_exchange(hs):
    n = len(hs)

    def body(*refs):
        ins, outs = refs[:n], refs[n:2 * n]
        send, recv = refs[2 * n:]
        x, y, c, chips = _mesh_pos()
        q = 2 * x + y
        copies = []
        for t in range(n):
            for j, (px, py) in enumerate(chips):
                copies.append(pltpu.make_async_remote_copy(
                    src_ref=ins[t].at[:, 2 * px + py], dst_ref=outs[t].at[:, q], send_sem=send.at[3 * t + j],
                    recv_sem=recv.at[3 * t + j], device_id=(px, py, c), device_id_type=MESH))
        _run_copies(copies)

    return pl.pallas_call(
        body, name="grad_chip_exchange", in_specs=[ANY] * n, out_specs=[ANY] * n,
        out_shape=[SDS(h.shape, h.dtype) for h in hs],
        scratch_shapes=[pltpu.SemaphoreType.DMA((3 * n,)), pltpu.SemaphoreType.DMA((3 * n,))])(*hs)


def _chip_sum(name, s, h, pos):
    l, _, r, cols = s.shape

    def body(pos_ref, s0, s1, s2, s3, own_ref, o_ref):
        vals = [jnp.where(pos_ref[0] == p, own_ref[...], ref[...]).astype(f32) for p, ref in enumerate((s0, s1, s2, s3))]
        o_ref[...] = ((vals[0] + vals[1]) + vals[2]) + vals[3]

    blk = (None, None, r, cols)
    slot = lambda p: pl.BlockSpec(blk, lambda i, pos_ref: (i, jnp.where(pos_ref[0] == p, (p + 1) % N_CHIPS, p), 0, 0))
    return pl.pallas_call(
        body, name=name, out_shape=SDS((l, 2 * r, cols), f32),
        grid_spec=pltpu.PrefetchScalarGridSpec(
            num_scalar_prefetch=1, grid=(l,),
            in_specs=[slot(p) for p in range(N_CHIPS)] + [pl.BlockSpec(blk, lambda i, pos_ref: (i, pos_ref[0], 0, 0))],
            out_specs=pl.BlockSpec((None, r, cols), lambda i, pos_ref: (i, pos_ref[1], 0))),
        compiler_params=_cparams(("parallel",)))(pos, s, s, s, s, h)


def _pair_gather(fulls):
    n = len(fulls)

    def body(*refs):
        ins, outs = refs[:n], refs[n:2 * n]
        send, recv = refs[2 * n:]
        x, y, c, _ = _mesh_pos()
        copies = []
        for t in range(n):
            half = outs[t].shape[1] // 2
            rows = outs[t].at[:, pl.ds(pl.multiple_of(c * half, SUBLANES), half)]
            copies.append(pltpu.make_async_remote_copy(
                src_ref=rows, dst_ref=rows, send_sem=send.at[t], recv_sem=recv.at[t],
                device_id=(x, y, 1 - c), device_id_type=MESH))
        _run_copies(copies)

    return pl.pallas_call(
        body, name="grad_pair_gather", in_specs=[ANY] * n, out_specs=[ANY] * n,
        out_shape=[SDS(f.shape, f32) for f in fulls], input_output_aliases={t: t for t in range(n)},
        scratch_shapes=[pltpu.SemaphoreType.DMA((n,)), pltpu.SemaphoreType.DMA((n,))])(*fulls)


COL_SHARDED = ("attn_w_in", "rg_w_in", "ffn_w_gate", "ffn_w_up")
ROW_SHARDED = ("attn_w_out", "rg_w_out")
GATES = ("rg_w_a", "rg_w_i")
VECTORS = ("rg_conv_w", "rg_conv_b", "rg_b_a", "rg_b_i", "rg_lambda")
REPLICATED = ("norm_mix_pre", "norm_mix_post", "norm_ffn_pre", "norm_ffn_post", "attn_rel_bias")
BIG_GRADS = COL_SHARDED + ROW_SHARDED + ("ffn_w_down",)
SMALL_GRADS = GATES + VECTORS + REPLICATED
WEIGHTS =("attn_w_in", "attn_rel_bias", "attn_w_out", "rg_w_in", "rg_conv_w", "rg_conv_b", "rg_w_a", "rg_b_a",
           "rg_w_i", "rg_b_i", "rg_lambda", "rg_w_out", "norm_mix_pre", "norm_mix_post", "norm_ffn_pre",
           "norm_ffn_post", "ffn_w_gate", "ffn_w_up", "ffn_w_down")
SMALL = VECTORS + REPLICATED


GATHER_PARTS = {
    "first": (("attn_w_in", 0, 1), ("attn_w_out", 0, 1), ("rg_w_a", 0, 8), ("rg_w_i", 0, 8), ("vec", 0, 1)),
    "chunk_attn_fwd": (("ffn_w_gate", 0, 1), ("ffn_w_up", 0, 1), ("ffn_w_down", 0, 1), ("rg_w_in", 0, 1),
                       ("rg_w_out", 0, 1)),
    "sb_attn_fwd": (("ffn_w_gate", 1, 3), ("ffn_w_up", 1, 3), ("ffn_w_down", 1, 3), ("attn_w_in", 1, 1),
                    ("attn_w_out", 1, 1)),
    "ffn_up": (("rg_w_in", 1, 1), ("rg_w_out", 1, 1)),
}


class _WeightGather:
    def __init__(self, w):
        self.w = w
        self.names = list(COL_SHARDED + ROW_SHARDED + GATES + ("ffn_w_down", "vec"))
        self.shards = {}
        for k in self.names[:-1]:
            a = w[k].astype(bf16)
            self.shards[k] = a.reshape((-1,) + a.shape[-2:])
        self.shards["vec"] = jnp.concatenate([w[k].reshape(-1) for k in VECTORS]).reshape(1, -1, LANES)
        got = _gather_call(self._items("first", self.names), [self.shards[k] for k in self.names])
        self.raw = dict(zip(self.names, got))

    @staticmethod
    def _items(part, names):
        return [(names.index(k), l0, nl) for k, l0, nl in GATHER_PARTS[part]]

    def part(self, part):
        names = list(dict.fromkeys(k for k, _, _ in GATHER_PARTS[part]))
        return (self._items(part, names), [self.shards[k] for k in names], [self.raw[k] for k in names]), names

    def views(self):
        got, w = self.raw, self.w
        out = {k: w[k] for k in REPLICATED}
        for k in COL_SHARDED + ("ffn_w_down",):
            out[k] = got[k]
        for k in ROW_SHARDED:
            l, s, ks, n = got[k].shape
            out[k] = got[k].reshape(l, 1, s * ks, n)
        for k in GATES:
            out[k] = got[k].reshape(2, LRU_BLOCKS, LRU_BW, LRU_BW)
        vec = got["vec"].reshape(N_CHIPS, -1)
        off = 0
        for k in VECTORS:
            shp = w[k].shape
            n = int(np.prod(shp))
            piece = vec[:, off:off + n].reshape((N_CHIPS,) + shp)
            off += n
            if k == "rg_conv_w":
                out[k] = piece.reshape(N_CHIPS, 2, 4, 256).transpose(1, 2, 0, 3).reshape(2, 4, D_MODEL)
            elif k in ("rg_b_a", "rg_b_i"):
                out[k] = piece.transpose(1, 2, 0, 3).reshape(2, 1, D_MODEL)
            else:
                out[k] = piece.transpose(1, 0, 2).reshape(2, 1, D_MODEL)
        return out


def _carried(plan, part, wts, fn, *args):
    if plan is None:
        return fn(*args, None)[0]
    gather, names = plan.part(part)
    out, new = fn(*args, gather)
    plan.raw.update(zip(names, new))
    wts.update(plan.views())
    return out


def _grad_blocks(name, g):
    st = jnp.stack([g[i] for i in sorted(g)])
    if name in GATES:
        st = st.reshape(2, LRU_BLOCKS, N_CHIPS, LRU_BW // N_CHIPS, LRU_BW).transpose(2, 0, 1, 3, 4)
    elif name == "rg_conv_w":
        st = st.reshape(2, 4, N_CHIPS, -1).transpose(2, 0, 1, 3)
    elif name in ("rg_b_a", "rg_b_i"):
        st = st.reshape(2, LRU_BLOCKS, N_CHIPS, -1).transpose(2, 0, 1, 3)
    elif name in VECTORS:
        st = st.reshape(2, N_CHIPS, -1).transpose(1, 0, 2)
    else:
        st = jnp.broadcast_to(st.reshape(1, -1), (N_CHIPS, st.size))
    return st.reshape(N_CHIPS, -1)


def _reduce_gradients(grads, shard_shapes):
    gs = []
    for k in BIG_GRADS:
        g = grads[k]
        if g.ndim == 3:
            g = g.reshape(g.shape[0], N_CHIPS, g.shape[1] // N_CHIPS, g.shape[2])
        gs.append(g)
    blocks = [_grad_blocks(k, grads[k]) for k in SMALL_GRADS]
    used = sum(b.shape[1] for b in blocks)
    small = jnp.concatenate(blocks + [jnp.zeros((N_CHIPS, SMALL_ROWS * PACK_COLS - used), f32)], axis=1)
    gs.append(small.reshape(1, N_CHIPS, SMALL_ROWS, PACK_COLS))
    names = BIG_GRADS + ("small",)
    c = lax.axis_index("c").astype(jnp.int32).reshape(1)
    pos = jnp.stack([2 * lax.axis_index("x") + lax.axis_index("y"), lax.axis_index("c")]).astype(jnp.int32)
    parts = [_pair_sum("grad_pair_sum_" + k, g, r, c) for k, g, r in zip(names, gs, _pair_exchange(gs))]
    slots = _chip_exchange(parts)
    full = _pair_gather([_chip_sum("grad_chip_sum_" + k, s, h, pos) for k, s, h in zip(names, slots, parts)])
    out = {k: f.reshape(shard_shapes[k]) for k, f in zip(BIG_GRADS, full)}
    flat, off = full[-1].reshape(-1), 0
    for k in SMALL_GRADS:
        n = int(np.prod(shard_shapes[k]))
        out[k] = flat[off:off + n].reshape(shard_shapes[k])
        off += n
    return out


def _adamw_fn(w, g, m, v):
    m = ADAM_B1 * m + (1.0 - ADAM_B1) * g
    v = ADAM_B2 * v + (1.0 - ADAM_B2) * (g * g)
    m_hat = m / (1.0 - ADAM_B1 ** ADAM_STEP)
    v_hat = v / (1.0 - ADAM_B2 ** ADAM_STEP)
    return -ADAM_LR * (m_hat / (jnp.sqrt(v_hat) + ADAM_EPS) + ADAM_WD * w), m, v


def _adamw(name, w, g, m, v):
    shp = w.shape
    if w.size >= 1 << 16:
        width = shp[-1]
        ops = [a.reshape(-1, width) for a in (w, g, m, v)]
        res = _rows(name, _adamw_fn, ops, [], [(width, f32)] * 3)
        return [r.reshape(shp) for r in res]
    n = w.size
    rows = -(-n // (SUBLANES * LANES)) * SUBLANES
    ops = [jnp.pad(a.reshape(-1), (0, rows * LANES - n)).reshape(rows, LANES) for a in (w, g, m, v)]
    res = _rows(name, _adamw_fn, ops, [], [(LANES, f32)] * 3, tr=rows)
    return [r.reshape(-1)[:n].reshape(shp) for r in res]


def kernel(x, attn_w_in, attn_rel_bias, attn_w_out, rg_w_in, rg_conv_w, rg_conv_b, rg_w_a, rg_b_a, rg_w_i, rg_b_i, rg_lambda, rg_w_out, norm_mix_pre, norm_mix_post, norm_ffn_pre, norm_ffn_post, ffn_w_gate, ffn_w_up, ffn_w_down, loss_target, m_attn_w_in, m_attn_rel_bias, m_attn_w_out, m_rg_w_in, m_rg_conv_w, m_rg_conv_b, m_rg_w_a, m_rg_b_a, m_rg_w_i, m_rg_b_i, m_rg_lambda, m_rg_w_out, m_norm_mix_pre, m_norm_mix_post, m_norm_ffn_pre, m_norm_ffn_post, m_ffn_w_gate, m_ffn_w_up, m_ffn_w_down, v_attn_w_in, v_attn_rel_bias, v_attn_w_out, v_rg_w_in, v_rg_conv_w, v_rg_conv_b, v_rg_w_a, v_rg_b_a, v_rg_w_i, v_rg_b_i, v_rg_lambda, v_rg_w_out, v_norm_mix_pre, v_norm_mix_post, v_norm_ffn_pre, v_norm_ffn_post, v_ffn_w_gate, v_ffn_w_up, v_ffn_w_down):
    w = dict(zip(WEIGHTS, (attn_w_in, attn_rel_bias, attn_w_out, rg_w_in, rg_conv_w, rg_conv_b, rg_w_a, rg_b_a, rg_w_i,
                           rg_b_i, rg_lambda, rg_w_out, norm_mix_pre, norm_mix_post, norm_ffn_pre, norm_ffn_post,
                           ffn_w_gate, ffn_w_up, ffn_w_down)))
    m = dict(zip(WEIGHTS, (m_attn_w_in, m_attn_rel_bias, m_attn_w_out, m_rg_w_in, m_rg_conv_w, m_rg_conv_b, m_rg_w_a,
                           m_rg_b_a, m_rg_w_i, m_rg_b_i, m_rg_lambda, m_rg_w_out, m_norm_mix_pre, m_norm_mix_post,
                           m_norm_ffn_pre, m_norm_ffn_post, m_ffn_w_gate, m_ffn_w_up, m_ffn_w_down)))
    v = dict(zip(WEIGHTS, (v_attn_w_in, v_attn_rel_bias, v_attn_w_out, v_rg_w_in, v_rg_conv_w, v_rg_conv_b, v_rg_w_a,
                           v_rg_b_a, v_rg_w_i, v_rg_b_i, v_rg_lambda, v_rg_w_out, v_norm_mix_pre, v_norm_mix_post,
                           v_norm_ffn_pre, v_norm_ffn_post, v_ffn_w_gate, v_ffn_w_up, v_ffn_w_down)))
    plan = _WeightGather(w)
    loss, dx, grads = _local_step(x[0], loss_target[0], plan.views(), plan)
    loss = lax.psum(loss, ("x", "y", "c"))
    g = _reduce_gradients(grads, {k: w[k].shape for k in WEIGHTS})

    big = [k for k in WEIGHTS if k not in SMALL]
    upd = {k: _adamw("adamw_" + k, w[k], g[k], m[k], v[k]) for k in big}
    cat = lambda d: jnp.concatenate([d[k].reshape(-1) for k in SMALL])
    small = _adamw("adamw_small", cat(w), cat(g), cat(m), cat(v))
    off = 0
    for k in SMALL:
        n = w[k].size
        upd[k] = [r[off:off + n].reshape(w[k].shape) for r in small]
        off += n
    return (loss, dx[None], *[g[k] for k in WEIGHTS], *[upd[k][0] for k in WEIGHTS],
            *[upd[k][1] for k in WEIGHTS], *[upd[k][2] for k in WEIGHTS])
```

```python
import functools

import numpy as np
import jax
import jax.numpy as jnp
from jax import lax
from jax.experimental import pallas as pl
from jax.experimental.pallas import tpu as pltpu

f32 = jnp.float32
bf16 = jnp.bfloat16
SDS = jax.ShapeDtypeStruct
MESH = pl.DeviceIdType.MESH

D_MODEL = 1024
N_CHIPS = 4
DEPTH = 4
HEAD_DIM = 64
CHUNK = 64
N_LEFT = 8
REL_CLIP = 256
A_W = 512
LRU_BLOCKS = 4
LRU_BW = 256
LRU_C = 8.0
D_FF = 2816
RMS_EPS = 1e-6
LANES = 128
SUBLANES = 8
VMEM_LIMIT = 56 * 1024 * 1024

QB_A = 2 * CHUNK
QSUB_A = 2
KW_A = QB_A + N_LEFT * CHUNK
PAD_A = N_LEFT * CHUNK
EXT_A = 768
SB_BLK = 256
SB_DEAD = -110.0

ADAM_LR, ADAM_B1, ADAM_B2, ADAM_EPS, ADAM_WD, ADAM_STEP = 0.001, 0.9, 0.999, 1e-08, 0.01, 10


def _cparams(sem):
    return pltpu.CompilerParams(dimension_semantics=sem, vmem_limit_bytes=VMEM_LIMIT)


def _gemm(name, operands, in_specs, o_spec, out_shape, grid, dims, acc_shape, into=None):
    nred = grid[2]
    npair = len(operands) // 2
    nin = 2 * npair + (into is not None)

    def body(*refs):
        o_ref = refs[nin]
        p = None
        for t in range(npair):
            d = lax.dot_general(refs[2 * t][...], refs[2 * t + 1][...], (dims, ((), ())),
                                preferred_element_type=f32)
            p = d if p is None else p + d
        if nred == 1:
            o_ref[...] = p.astype(o_ref.dtype)
        else:
            acc = refs[nin + 1]
            r = pl.program_id(2)

            @pl.when(r == 0)
            def _():
                acc[...] = p

            @pl.when(r > 0)
            def _():
                acc[...] += p

            @pl.when(r == nred - 1)
            def _():
                o_ref[...] = acc[...].astype(o_ref.dtype)

    scratch = [] if nred == 1 else [pltpu.VMEM(acc_shape, f32)]
    extra, alias = ([], {}) if into is None else ([into], {2 * npair: 0})
    return pl.pallas_call(
        body, grid=grid, in_specs=list(in_specs) + [pl.BlockSpec(memory_space=pl.ANY)] * len(extra),
        out_specs=o_spec, out_shape=out_shape, scratch_shapes=scratch, name=name, input_output_aliases=alias,
        compiler_params=_cparams(("parallel", "parallel", "arbitrary")))(*operands, *extra)


class _Fresh:
    def __init__(self, shape):
        self.shape = tuple(shape)


def _into(buf):
    return None if isinstance(buf, _Fresh) else buf


NN = ((1,), (0,))
NT = ((1,), (1,))
TN = ((0,), (0,))


def _tile(t, want=1024):
    return min(want, t)


def _mm_cols(name, a, w, l, out_dtype):
    t, k = a.shape
    _, s, _, ns = w.shape
    tm = _tile(t)
    return _gemm(
        name, [a, w],
        [pl.BlockSpec((tm, k), lambda i, j, r: (i, 0)),
         pl.BlockSpec((None, None, k, ns), lambda i, j, r: (l, j, 0, 0))],
        pl.BlockSpec((tm, ns), lambda i, j, r: (i, j)),
        SDS((t, s * ns), out_dtype), (t // tm, s, 1), NN, None)


def _mm_cols_t(name, dy, w, l, out_dtype):
    t = dy.shape[0]
    _, s, k, ns = w.shape
    tm = _tile(t)
    return _gemm(
        name, [dy, w],
        [pl.BlockSpec((tm, ns), lambda i, j, r: (i, r)),
         pl.BlockSpec((None, None, k, ns), lambda i, j, r: (l, r, 0, 0))],
        pl.BlockSpec((tm, k), lambda i, j, r: (i, 0)),
        SDS((t, k), out_dtype), (t // tm, 1, s), NT, (tm, k))


def _mm_wgrad_cols(name, a, dy, buf, l):
    t, k = a.shape
    _, s, _, ns = buf.shape
    tt = _tile(t)
    return _gemm(
        name, [a, dy],
        [pl.BlockSpec((tt, k), lambda i, j, r: (r, 0)),
         pl.BlockSpec((tt, ns), lambda i, j, r: (r, i))],
        pl.BlockSpec((None, None, k, ns), lambda i, j, r: (l, i, 0, 0)),
        SDS(buf.shape, f32), (s, 1, t // tt), TN, (k, ns), into=_into(buf))


def _mm_rows(name, parts, w, l, out_dtype):
    t = parts[0].shape[0]
    n = w.shape[3]
    tm = _tile(t)
    ops, specs = [], []
    for p_i, a in enumerate(parts):
        kp = a.shape[1]
        ops += [a, w]
        specs += [pl.BlockSpec((tm, kp), lambda i, j, r: (i, 0)),
                  pl.BlockSpec((None, None, kp, n), lambda i, j, r, p_i=p_i: (l, 0, p_i, 0))]
    return _gemm(name, ops, specs, pl.BlockSpec((tm, n), lambda i, j, r: (i, 0)),
                 SDS((t, n), out_dtype), (t // tm, 1, 1), NN, None)


def _mm_rows_t(name, dy, w, l, out_dtype):
    t, n = dy.shape
    k = w.shape[2]
    tm = _tile(t)
    return _gemm(
        name, [dy, w],
        [pl.BlockSpec((tm, n), lambda i, j, r: (i, 0)),
         pl.BlockSpec((None, None, k, n), lambda i, j, r: (l, 0, 0, 0))],
        pl.BlockSpec((tm, k), lambda i, j, r: (i, 0)),
        SDS((t, k), out_dtype), (t // tm, 1, 1), NT, None)


def _mm_wgrad(name, a, dy, buf, l, part=0):
    t, k = a.shape
    n = dy.shape[1]
    tt = _tile(t)
    return _gemm(
        name, [a, dy],
        [pl.BlockSpec((tt, k), lambda i, j, r: (r, 0)),
         pl.BlockSpec((tt, n), lambda i, j, r: (r, 0))],
        pl.BlockSpec((None, k, n), lambda i, j, r: (l, part, 0)),
        SDS(buf.shape, f32), (1, 1, t // tt), TN, (k, n), into=_into(buf))


def _ffn_up(h, wg, wu, l, gather):
    t, k = h.shape
    s, fs = wg.shape[1], wg.shape[2]
    tm = _tile(t)

    def body(h_ref, wg_ref, wu_ref, g_ref, u_ref, hid_ref):
        hv = h_ref[...]
        g = lax.dot_general(hv, wg_ref[...], (NT, ((), ())), preferred_element_type=f32)
        u = lax.dot_general(hv, wu_ref[...], (NT, ((), ())), preferred_element_type=f32)
        g_ref[...] = g.astype(bf16)
        u_ref[...] = u.astype(bf16)
        hid_ref[...] = (g * jax.nn.sigmoid(g) * u).astype(bf16)

    wspec = pl.BlockSpec((None, None, fs, k), lambda j, i: (l, j, 0, 0))
    ospec = pl.BlockSpec((None, tm, fs), lambda j, i: (j, i, 0))
    return _call(
        body, [h, wg, wu], grid=(s, t // tm), name="ffn_up",
        in_specs=[pl.BlockSpec((tm, k), lambda j, i: (i, 0)), wspec, wspec],
        out_specs=[ospec, ospec, ospec], out_shape=[SDS((s, t, fs), bf16)] * 3,
        sem=("parallel", "parallel"), gather=gather)


def _ffn_down(hid, wd, l):
    s, t, fs = hid.shape
    n = wd.shape[3]
    tm = _tile(t, 512)
    ops, specs = [], []
    for r in range(s):
        ops += [hid, wd]
        specs += [pl.BlockSpec((None, tm, fs), lambda i, j, k, r=r: (r, i, 0)),
                  pl.BlockSpec((None, None, fs, n), lambda i, j, k, r=r: (l, r, 0, 0))]
    return _gemm("ffn_down", ops, specs, pl.BlockSpec((tm, n), lambda i, j, k: (i, 0)),
                 SDS((t, n), f32), (t // tm, 1, 1), NN, None)


def _ffn_down_bwd(df, wd, l, g, u):
    t, n = df.shape
    s, fs = wd.shape[1], wd.shape[2]
    tm = _tile(t)

    def body(df_ref, wd_ref, g_ref, u_ref, dg_ref, du_ref):
        dh = lax.dot_general(df_ref[...], wd_ref[...], (NT, ((), ())), preferred_element_type=f32)
        gv = g_ref[...].astype(f32)
        uv = u_ref[...].astype(f32)
        sg = jax.nn.sigmoid(gv)
        du_ref[...] = (dh * gv * sg).astype(bf16)
        dg_ref[...] = (dh * uv * (sg * (1.0 + gv * (1.0 - sg)))).astype(bf16)

    bspec = pl.BlockSpec((None, tm, fs), lambda j, i: (j, i, 0))
    return pl.pallas_call(
        body, grid=(s, t // tm), name="ffn_down_bwd",
        in_specs=[pl.BlockSpec((tm, n), lambda j, i: (i, 0)),
                  pl.BlockSpec((None, None, fs, n), lambda j, i: (l, j, 0, 0)), bspec, bspec],
        out_specs=[bspec, bspec], out_shape=[SDS((s, t, fs), bf16)] * 2,
        compiler_params=_cparams(("parallel", "parallel")))(df, wd, g, u)


def _ffn_up_bwd(dg, du, wg, wu, l):
    s, t, fs = dg.shape
    k = wg.shape[3]
    tm = _tile(t, 512)
    ops, specs = [], []
    for r in range(s):
        aspec = pl.BlockSpec((None, tm, fs), lambda i, j, kk, r=r: (r, i, 0))
        wspec = pl.BlockSpec((None, None, fs, k), lambda i, j, kk, r=r: (l, r, 0, 0))
        ops += [dg, wg, du, wu]
        specs += [aspec, wspec, aspec, wspec]
    return _gemm("ffn_up_bwd", ops, specs, pl.BlockSpec((tm, k), lambda i, j, kk: (i, 0)),
                 SDS((t, k), f32), (t // tm, 1, 1), NN, None)


def _ffn_wgrad_up(h, dg, du, buf_g, buf_u, l):
    t, k = h.shape
    s, _, fs = dg.shape
    tt = _tile(t)
    nred = t // tt

    fresh = isinstance(buf_g, _Fresh)

    def body(*refs):
        h_ref, dg_ref, du_ref = refs[:3]
        og_ref, ou_ref, acc_g, acc_u = refs[-4:]
        r = pl.program_id(1)
        hv = h_ref[...]
        pg = lax.dot_general(dg_ref[...], hv, (TN, ((), ())), preferred_element_type=f32)
        pu = lax.dot_general(du_ref[...], hv, (TN, ((), ())), preferred_element_type=f32)

        @pl.when(r == 0)
        def _():
            acc_g[...] = pg
            acc_u[...] = pu

        @pl.when(r > 0)
        def _():
            acc_g[...] += pg
            acc_u[...] += pu

        @pl.when(r == nred - 1)
        def _():
            og_ref[...] = acc_g[...]
            ou_ref[...] = acc_u[...]

    dspec = pl.BlockSpec((None, tt, fs), lambda i, r: (i, r, 0))
    ospec = pl.BlockSpec((None, None, fs, k), lambda i, r: (l, i, 0, 0))
    extra, alias = ([], {}) if fresh else ([buf_g, buf_u], {3: 0, 4: 1})
    return pl.pallas_call(
        body, grid=(s, nred), name="ffn_wgrad_up",
        in_specs=[pl.BlockSpec((tt, k), lambda i, r: (r, 0)), dspec, dspec] + [ANY] * len(extra),
        out_specs=[ospec, ospec], out_shape=[SDS(buf_g.shape, f32), SDS(buf_u.shape, f32)],
        scratch_shapes=[pltpu.VMEM((fs, k), f32)] * 2, input_output_aliases=alias,
        compiler_params=_cparams(("parallel", "arbitrary")))(h, dg, du, *extra)


def _ffn_wgrad_down(hid, df, buf, l):
    s, t, fs = hid.shape
    n = df.shape[1]
    tt = _tile(t)
    return _gemm(
        "ffn_wgrad_down", [hid, df],
        [pl.BlockSpec((None, tt, fs), lambda i, j, r: (i, r, 0)),
         pl.BlockSpec((tt, n), lambda i, j, r: (r, 0))],
        pl.BlockSpec((None, None, fs, n), lambda i, j, r: (l, i, 0, 0)),
        SDS(buf.shape, f32), (s, 1, t // tt), TN, (fs, n), into=_into(buf))


def _rows(name, fn, rows, consts, row_outs, acc_outs=(), tr=512):
    rows = [r if isinstance(r, tuple) else (r, r.shape[1], 0) for r in rows]
    t = rows[0][0].shape[0]
    tr = max(d for d in range(SUBLANES, min(tr, t) + 1, SUBLANES) if t % d == 0)
    nin = len(rows) + len(consts)
    no, na = len(row_outs), len(acc_outs)

    def body(*refs):
        vals = fn(*[r[...] for r in refs[:nin]])
        if not isinstance(vals, (tuple, list)):
            vals = (vals,)
        for k in range(no):
            refs[nin + k][...] = vals[k].astype(refs[nin + k].dtype)
        first = pl.program_id(0) == 0
        for k in range(na):
            ref, val = refs[nin + no + k], vals[no + k]

            @pl.when(first)
            def _(ref=ref, val=val):
                ref[...] = val

            @pl.when(jnp.logical_not(first))
            def _(ref=ref, val=val):
                ref[...] += val

    in_specs = [pl.BlockSpec((tr, w), lambda i, cb=cb: (i, cb)) for (_, w, cb) in rows]
    in_specs += [pl.BlockSpec(c.shape, lambda i, nd=c.ndim: (0,) * nd) for c in consts]
    out_specs = [pl.BlockSpec((tr, w), lambda i: (i, 0)) for (w, _) in row_outs]
    out_specs += [pl.BlockSpec(s, lambda i, nd=len(s): (0,) * nd) for (s, _) in acc_outs]
    out_shape = [SDS((t, w), dt) for (w, dt) in row_outs] + [SDS(s, dt) for (s, dt) in acc_outs]
    res = pl.pallas_call(
        body, grid=(t // tr,), in_specs=in_specs, out_specs=out_specs, out_shape=out_shape,
        name=name, compiler_params=_cparams(("arbitrary",)))(*[r[0] for r in rows], *consts)
    return res


def _rstd(x):
    return lax.rsqrt(jnp.mean(x * x, axis=-1, keepdims=True) + RMS_EPS)


def _norm_fwd(x, g):
    return x * _rstd(x) * g


def _norm_bwd(u, dy, g):
    r = _rstd(u)
    n = u * r
    dn = dy * g
    du = r * (dn - n * jnp.mean(dn * n, axis=-1, keepdims=True))
    return du, jnp.sum(dy * n, axis=0, keepdims=True)


def _gelu(x):
    c = 0.7978845608028654
    return 0.5 * x * (1.0 + jnp.tanh(c * (x + 0.044715 * x * x * x)))


def _gelu_grad(x):
    c = 0.7978845608028654
    th = jnp.tanh(c * (x + 0.044715 * x * x * x))
    return 0.5 * (1.0 + th) + 0.5 * x * (1.0 - th * th) * c * (1.0 + 3.0 * 0.044715 * x * x)


def _mask_heads(x):
    lane = lax.broadcasted_iota(jnp.int32, x.shape, 1)
    return [jnp.where((lane >= h * HEAD_DIM) & (lane < (h + 1) * HEAD_DIM), x, jnp.zeros_like(x))
            for h in range(LANES // HEAD_DIM)]


def _chunk_valid(start):
    qi = lax.broadcasted_iota(jnp.int32, (QB_A, KW_A), 0)
    kj = lax.broadcasted_iota(jnp.int32, (QB_A, KW_A), 1)
    qc = qi // CHUNK
    kc = kj // CHUNK
    return (kc >= qc) & (kc <= qc + N_LEFT) & (kj + start >= PAD_A)


def _chunk_probs(q, k, bias, valid):
    s = lax.dot_general(q, k, (NT, ((), ())), preferred_element_type=f32) * (HEAD_DIM ** -0.5) + bias
    s = jnp.where(valid, s, -1e30)
    p = jnp.exp(s - jnp.max(s, axis=-1, keepdims=True))
    return p / jnp.sum(p, axis=-1, keepdims=True)


def _chunk_attn_fwd(proj, kpad, vpad, bias, gather):
    t = proj.shape[0]
    tp = kpad.shape[0]
    step = QSUB_A * QB_A

    def body(q_ref, k_ref, v_ref, b_ref, o_ref):
        for sb in range(QSUB_A):
            start = pl.multiple_of((pl.program_id(1) * QSUB_A + sb) * QB_A, QB_A)
            rows = pl.ds(sb * QB_A, QB_A)
            valid = _chunk_valid(start)
            kw = k_ref[pl.ds(start, KW_A), :]
            qm = _mask_heads(q_ref[rows, :])
            vm = _mask_heads(v_ref[pl.ds(start, KW_A), :])
            o = None
            for h in range(len(qm)):
                p = _chunk_probs(qm[h], kw, b_ref[h], valid)
                d = jnp.dot(p.astype(bf16), vm[h], preferred_element_type=f32)
                o = d if o is None else o + d
            o_ref[rows, :] = o.astype(bf16)

    kv_spec = pl.BlockSpec((tp, LANES), lambda hp, qb: (0, hp))
    outs, new = _call(
        body, [proj, kpad, vpad, bias], grid=(A_W // LANES, t // step), name="chunk_attn_fwd",
        in_specs=[pl.BlockSpec((step, LANES), lambda hp, qb: (qb, hp)), kv_spec, kv_spec,
                  pl.BlockSpec((2, QB_A, KW_A), lambda hp, qb: (hp, 0, 0))],
        out_specs=[pl.BlockSpec((step, LANES), lambda hp, qb: (qb, hp))],
        out_shape=[SDS((t, A_W), bf16)], sem=("parallel", "arbitrary"), gather=gather)
    return outs[0], new


def _chunk_attn_bwd(proj, kpad, vpad, bias, dout):
    t = proj.shape[0]
    tp = kpad.shape[0]
    step = QSUB_A * QB_A

    def body(q_ref, k_ref, v_ref, b_ref, do_ref, dq_ref, dk_ref, dv_ref, db_ref):
        qb = pl.program_id(1)

        @pl.when(qb == 0)
        def _():
            dk_ref[...] = jnp.zeros_like(dk_ref)
            dv_ref[...] = jnp.zeros_like(dv_ref)
            db_ref[...] = jnp.zeros_like(db_ref)

        for sb in range(QSUB_A):
            start = pl.multiple_of((qb * QSUB_A + sb) * QB_A, QB_A)
            rows = pl.ds(sb * QB_A, QB_A)
            win = pl.ds(start, KW_A)
            valid = _chunk_valid(start)
            kw = k_ref[win, :]
            vw = v_ref[win, :]
            qm = _mask_heads(q_ref[rows, :])
            dom = _mask_heads(do_ref[rows, :])
            km = _mask_heads(kw)
            dq = dk = dv = None
            for h in range(len(qm)):
                p = _chunk_probs(qm[h], kw, b_ref[h], valid)
                dp = lax.dot_general(dom[h], vw, (NT, ((), ())), preferred_element_type=f32)
                ds = p * (dp - jnp.sum(dp * p, axis=-1, keepdims=True))
                db_ref[h] += ds
                dsb = (ds * (HEAD_DIM ** -0.5)).astype(bf16)
                terms = (jnp.dot(dsb, km[h], preferred_element_type=f32),
                         lax.dot_general(dsb, qm[h], (TN, ((), ())), preferred_element_type=f32),
                         lax.dot_general(p.astype(bf16), dom[h], (TN, ((), ())), preferred_element_type=f32))
                dq, dk, dv = terms if dq is None else (dq + terms[0], dk + terms[1], dv + terms[2])
            dq_ref[rows, :] = dq.astype(bf16)
            dk_ref[win, :] += dk
            dv_ref[win, :] += dv

    kv_spec = pl.BlockSpec((tp, LANES), lambda hp, qb: (0, hp))
    q_spec = pl.BlockSpec((step, LANES), lambda hp, qb: (qb, hp))
    b_spec = pl.BlockSpec((2, QB_A, KW_A), lambda hp, qb: (hp, 0, 0))
    return pl.pallas_call(
        body, grid=(A_W // LANES, t // step), name="chunk_attn_bwd",
        in_specs=[q_spec, kv_spec, kv_spec, b_spec, q_spec],
        out_specs=[q_spec, kv_spec, kv_spec, b_spec],
        out_shape=[SDS((t, A_W), bf16), SDS((tp, A_W), f32), SDS((tp, A_W), f32),
                   SDS((2 * A_W // LANES, QB_A, KW_A), f32)],
        compiler_params=_cparams(("parallel", "arbitrary")))(proj, kpad, vpad, bias, dout)


def _bias_ext(table):
    flat = PAD_A + QB_A - 1 - REL_CLIP
    top = jnp.broadcast_to(table[:, 2 * REL_CLIP:], (table.shape[0], flat))
    lo = 2 * REL_CLIP - (EXT_A - 1 - flat)
    return jnp.concatenate([top, jnp.flip(table[:, lo:], axis=1)], axis=1)


def _bias_window(table):
    nh = table.shape[0]
    e = jnp.broadcast_to(_bias_ext(table)[:, None, :], (nh, QB_A, EXT_A)).reshape(nh, QB_A * EXT_A)
    m = e[:, :QB_A * (EXT_A - 1)].reshape(nh, QB_A, EXT_A - 1)
    return m[:, :, QB_A - 1:]


def _bias_window_grad(dbias):
    nh = dbias.shape[0]
    m = jnp.pad(dbias, ((0, 0), (0, 0), (QB_A - 1, 0))).reshape(nh, QB_A * (EXT_A - 1))
    dext = jnp.sum(jnp.pad(m, ((0, 0), (0, QB_A))).reshape(nh, QB_A, EXT_A), axis=1)
    flat = PAD_A + QB_A - 1 - REL_CLIP
    lo = 2 * REL_CLIP - (EXT_A - 1 - flat)
    tail = jnp.flip(dext[:, flat:], axis=1)
    tail = tail.at[:, -1].add(jnp.sum(dext[:, :flat], axis=1))
    return jnp.pad(tail, ((0, 0), (lo, 0)))


def _tri_suffix(x, tri):
    hi = x.astype(bf16)
    lo = (x - hi.astype(f32)).astype(bf16)
    return jnp.dot(hi, tri, preferred_element_type=f32) + jnp.dot(lo, tri, preferred_element_type=f32)


def _sb_block(q, k, run, tri, causal):
    z = lax.dot_general(q, k, (NT, ((), ())), preferred_element_type=f32) * (HEAD_DIM ** -0.5)
    e = jnp.exp(-jnp.abs(z))
    l1p = jnp.log(1.0 + e)
    lb = jnp.minimum(z, 0.0) - l1p
    lmb = lb - z
    if causal is not None:
        lmb = jnp.where(causal, lmb, 0.0)
    cs = _tri_suffix(lmb, tri)
    w = jnp.exp(lb + (run + cs - lmb))
    if causal is not None:
        w = jnp.where(causal, w, 0.0)
    return z, e, w, run + cs[:, 0:1]


def _sb_tri():
    r = lax.broadcasted_iota(jnp.int32, (SB_BLK, SB_BLK), 0)
    c = lax.broadcasted_iota(jnp.int32, (SB_BLK, SB_BLK), 1)
    return (r >= c).astype(bf16), c < r


def _sb_live(runs):
    m = runs[0]
    for r in runs[1:]:
        m = jnp.maximum(m, r)
    return jnp.max(m) > SB_DEAD


def _sb_fwd(proj, gather):
    t = proj.shape[0]
    cb = A_W // LANES
    nh = LANES // HEAD_DIM

    def body(q_ref, k_ref, v_ref, o_ref, of_ref):
        qb = pl.program_id(1)
        tri, diag = _sb_tri()
        qm = _mask_heads(q_ref[...])

        def pair(kb, carry, causal):
            rows = pl.ds(pl.multiple_of(kb * SB_BLK, SB_BLK), SB_BLK)
            k = k_ref[rows, :]
            vm = _mask_heads(v_ref[rows, :])
            runs, acc = [], carry[nh]
            for h in range(nh):
                _, _, w, run = _sb_block(qm[h], k, carry[h], tri, causal)
                acc = acc + jnp.dot(w.astype(bf16), vm[h], preferred_element_type=f32)
                runs.append(run)
            return (*runs, acc)

        zero = jnp.zeros((SB_BLK, 1), f32)
        carry = pair(qb, (zero,) * nh + (jnp.zeros((SB_BLK, LANES), f32),), diag)

        def cond(st):
            return (st[0] < qb) & _sb_live(st[1][:nh])

        def step(st):
            return st[0] + 1, pair(qb - 1 - st[0], st[1], None)

        _, carry = lax.while_loop(cond, step, (jnp.int32(0), carry))
        o_ref[...] = carry[nh].astype(bf16)
        of_ref[...] = carry[nh]

    ospec = pl.BlockSpec((SB_BLK, LANES), lambda hp, qb: (qb, hp))
    return _call(
        body, [proj, proj, proj], grid=(cb, t // SB_BLK), name="sb_attn_fwd",
        in_specs=[pl.BlockSpec((SB_BLK, LANES), lambda hp, qb: (qb, 3 * cb + hp)),
                  pl.BlockSpec((t, LANES), lambda hp, qb: (0, 4 * cb + hp)),
                  pl.BlockSpec((t, LANES), lambda hp, qb: (0, 5 * cb + hp))],
        out_specs=[ospec, ospec], out_shape=[SDS((t, A_W), bf16), SDS((t, A_W), f32)],
        sem=("parallel", "arbitrary"), gather=gather)


def _sb_bwd(proj, out_b, dout):
    t = proj.shape[0]
    cb = A_W // LANES
    nh = LANES // HEAD_DIM

    def body(q_ref, k_ref, v_ref, o_ref, do_ref, dq_ref, dk_ref, dv_ref):
        qb = pl.program_id(1)
        tri, diag = _sb_tri()

        @pl.when(qb == 0)
        def _():
            dk_ref[...] = jnp.zeros_like(dk_ref)
            dv_ref[...] = jnp.zeros_like(dv_ref)

        qm = _mask_heads(q_ref[...])
        do = do_ref[...]
        dom = _mask_heads(do)
        dsums = [jnp.sum(t_, axis=-1, keepdims=True) for t_ in _mask_heads(do.astype(f32) * o_ref[...])]

        def pair(kb, carry, causal):
            rows = pl.ds(pl.multiple_of(kb * SB_BLK, SB_BLK), SB_BLK)
            k = k_ref[rows, :]
            v = v_ref[rows, :]
            km = _mask_heads(k)
            new, dq, dk, dv = [], carry[2 * nh], None, None
            for h in range(nh):
                z, e, w, run = _sb_block(qm[h], k, carry[2 * h], tri, causal)
                inv = 1.0 / (1.0 + e)
                beta = jnp.where(z >= 0.0, inv, e * inv)
                wb = w.astype(bf16)
                g = lax.dot_general(dom[h], v, (NT, ((), ())), preferred_element_type=f32) * wb.astype(f32)
                sg = _tri_suffix(g, tri)
                dz = g * (1.0 - beta) - (dsums[h] - carry[2 * h + 1] - sg) * beta
                if causal is not None:
                    dz = jnp.where(causal, dz, 0.0)
                dzb = (dz * (HEAD_DIM ** -0.5)).astype(bf16)
                dq = dq + jnp.dot(dzb, km[h], preferred_element_type=f32)
                tk = lax.dot_general(dzb, qm[h], (TN, ((), ())), preferred_element_type=f32)
                tv = lax.dot_general(wb, dom[h], (TN, ((), ())), preferred_element_type=f32)
                dk, dv = (tk, tv) if dk is None else (dk + tk, dv + tv)
                new += [run, carry[2 * h + 1] + sg[:, 0:1]]
            dk_ref[rows, :] += dk
            dv_ref[rows, :] += dv
            return (*new, dq)

        zero = jnp.zeros((SB_BLK, 1), f32)
        carry = pair(qb, (zero,) * (2 * nh) + (jnp.zeros((SB_BLK, LANES), f32),), diag)

        def cond(st):
            return (st[0] < qb) & _sb_live(st[1][0:2 * nh:2])

        def step(st):
            return st[0] + 1, pair(qb - 1 - st[0], st[1], None)

        _, carry = lax.while_loop(cond, step, (jnp.int32(0), carry))
        dq_ref[...] = carry[2 * nh].astype(bf16)

    kv_in = lambda seg: pl.BlockSpec((t, LANES), lambda hp, qb: (0, seg * cb + hp))
    q_spec = pl.BlockSpec((SB_BLK, LANES), lambda hp, qb: (qb, hp))
    kv_out = pl.BlockSpec((t, LANES), lambda hp, qb: (0, hp))
    return pl.pallas_call(
        body, grid=(cb, t // SB_BLK), name="sb_attn_bwd",
        in_specs=[pl.BlockSpec((SB_BLK, LANES), lambda hp, qb: (qb, 3 * cb + hp)), kv_in(4), kv_in(5),
                  q_spec, pl.BlockSpec((SB_BLK, LANES), lambda hp, qb: (qb, cb + hp))],
        out_specs=[q_spec, kv_out, kv_out],
        out_shape=[SDS((t, A_W), bf16), SDS((t, A_W), f32), SDS((t, A_W), f32)],
        compiler_params=_cparams(("parallel", "arbitrary")))(proj, proj, proj, out_b, dout)


def _halo_specs(tr, w, col, nblk):
    per = tr // SUBLANES
    cur = pl.BlockSpec((tr, w), lambda i: (i, col))
    prev = pl.BlockSpec((SUBLANES, w), lambda i: (jnp.maximum(i * per - 1, 0), col))
    nxt = pl.BlockSpec((SUBLANES, w), lambda i: (jnp.minimum((i + 1) * per, nblk * per - 1), col))
    return cur, prev, nxt


def _taps_before(cur, prev8, first):
    prev8 = jnp.where(first, 0.0, prev8)
    ext = jnp.concatenate([prev8, cur], axis=0)
    return [pltpu.roll(ext, s, 0)[SUBLANES:] for s in (3, 2, 1)]


def _taps_after(cur, next8, last):
    n = cur.shape[0]
    next8 = jnp.where(last, 0.0, next8)
    ext = jnp.concatenate([cur, next8], axis=0)
    return [pltpu.roll(ext, n + SUBLANES - s, 0)[:n] for s in (1, 2, 3)]


def _block_diag(x, w_ref, dims):
    outs = [lax.dot_general(x[:, n * LRU_BW:(n + 1) * LRU_BW], w_ref[n], (dims, ((), ())),
                            preferred_element_type=f32) for n in range(LRU_BLOCKS)]
    return jnp.concatenate(outs, axis=1)


def _lru_gates(xc, wa_ref, wi_ref, ba, bi, lam):
    xb = xc.astype(bf16)
    r = jax.nn.sigmoid(_block_diag(xb, wa_ref, NN) + ba)
    ig = jax.nn.sigmoid(_block_diag(xb, wi_ref, NN) + bi)
    sp = jnp.maximum(-lam, 0.0) + jnp.log(1.0 + jnp.exp(-jnp.abs(lam)))
    log_a = -LRU_C * r * sp
    a = jnp.exp(log_a)
    x2 = 2.0 * log_a
    one_minus = jnp.where(x2 > -1e-2, -x2 * (1.0 + x2 * (0.5 + x2 * (1.0 / 6.0))), 1.0 - a * a)
    mult = jnp.sqrt(one_minus)
    return xb, r, ig, sp, a, mult


def _rg_gates_fwd(proj, conv_w, conv_b, wa, wi, ba, bi, lam, tr=512):
    t = proj.shape[0]
    w = D_MODEL
    tr = min(tr, t)
    nblk = t // tr
    cur, prev, _ = _halo_specs(tr, w, 1, nblk)

    def body(x_ref, xp_ref, cw_ref, cb_ref, wa_ref, wi_ref, ba_ref, bi_ref, lam_ref, xc_ref, a_ref, u_ref):
        x = x_ref[...]
        taps = _taps_before(x, xp_ref[...], pl.program_id(0) == 0) + [x]
        xc = cb_ref[...]
        for k in range(4):
            xc = xc + cw_ref[k:k + 1, :] * taps[k]
        _, _, ig, _, a, mult = _lru_gates(xc, wa_ref, wi_ref, ba_ref[...], bi_ref[...], lam_ref[...])
        xc_ref[...] = xc
        a_ref[...] = a
        u_ref[...] = mult * (ig * xc)

    full = lambda a_: pl.BlockSpec(a_.shape, lambda i, nd=a_.ndim: (0,) * nd)
    ospec = pl.BlockSpec((tr, w), lambda i: (i, 0))
    return pl.pallas_call(
        body, grid=(nblk,), name="rg_gates_fwd",
        in_specs=[cur, prev] + [full(a_) for a_ in (conv_w, conv_b, wa, wi, ba, bi, lam)],
        out_specs=[ospec] * 3, out_shape=[SDS((t, w), f32)] * 3,
        compiler_params=_cparams(("parallel",)))(proj, proj, conv_w, conv_b, wa, wi, ba, bi, lam)


def _lru_scan(name, a, b, reverse, tt=512):
    t, w = a.shape
    tt = min(tt, t)
    nt = t // tt
    ng = tt // SUBLANES

    def body(a_ref, b_ref, h_ref, carry_ref):
        @pl.when(pl.program_id(0) == 0)
        def _():
            carry_ref[...] = jnp.zeros_like(carry_ref)

        row = lax.broadcasted_iota(jnp.int32, (SUBLANES, w), 0)

        def group(gi, carry):
            g = (ng - 1 - gi) if reverse else gi
            rows = pl.ds(pl.multiple_of(g * SUBLANES, SUBLANES), SUBLANES)
            av = a_ref[rows, :]
            bv = b_ref[rows, :]
            for s in (1, 2, 4):
                sh = (SUBLANES - s) if reverse else s
                ok = (row < SUBLANES - s) if reverse else (row >= s)
                a_s = pltpu.roll(av, sh, 0)
                b_s = pltpu.roll(bv, sh, 0)
                bv = jnp.where(ok, av * b_s + bv, bv)
                av = jnp.where(ok, av * a_s, av)
            h = av * carry + bv
            h_ref[rows, :] = h
            edge = h[0:1, :] if reverse else h[SUBLANES - 1:SUBLANES, :]
            return jnp.broadcast_to(edge, (SUBLANES, w))

        carry_ref[...] = lax.fori_loop(0, ng, group, carry_ref[...])

    tmap = (lambda i: (nt - 1 - i, 0)) if reverse else (lambda i: (i, 0))
    spec = pl.BlockSpec((tt, w), tmap)
    return pl.pallas_call(
        body, grid=(nt,), name=name, in_specs=[spec, spec], out_specs=spec,
        out_shape=SDS((t, w), f32), scratch_shapes=[pltpu.VMEM((SUBLANES, w), f32)],
        compiler_params=_cparams(("arbitrary",)))(a, b)


def _rg_gates_bwd(dhs, c, hs, xc, wa, wi, ba, bi, lam, tr=256):
    t, w = xc.shape
    tr = min(tr, t)
    nblk = t // tr
    cur, prev, nxt = _halo_specs(tr, w, 0, nblk)

    def body(dhs_ref, c_ref, cn_ref, hs_ref, hp_ref, xc_ref, wa_ref, wi_ref, ba_ref, bi_ref, lam_ref,
             dxc_ref, dwa_ref, dwi_ref, dba_ref, dbi_ref, dlam_ref):
        i = pl.program_id(0)
        c_next = _taps_after(c_ref[...], cn_ref[...], i == nblk - 1)[0]
        h_prev = _taps_before(hs_ref[...], hp_ref[...], i == 0)[2]
        xc = xc_ref[...]
        lam = lam_ref[...]
        xb, r, ig, sp, a, mult = _lru_gates(xc, wa_ref, wi_ref, ba_ref[...], bi_ref[...], lam)
        dh = dhs_ref[...] + c_next
        dlog_a = dh * h_prev * a - (dh * ig * xc) * (a * a / mult)
        dpre_a = (dlog_a * (-LRU_C * sp) * r * (1.0 - r)).astype(bf16)
        dpre_i = (dh * mult * xc * ig * (1.0 - ig)).astype(bf16)
        dxc_ref[...] = (dh * mult * ig + _block_diag(dpre_a, wa_ref, NT) + _block_diag(dpre_i, wi_ref, NT))
        dsig = 1.0 / (1.0 + jnp.exp(lam))
        sums = [jnp.sum(dpre_a.astype(f32), axis=0, keepdims=True),
                jnp.sum(dpre_i.astype(f32), axis=0, keepdims=True),
                jnp.sum(dlog_a * (-LRU_C * r), axis=0, keepdims=True) * (-dsig)]

        @pl.when(i == 0)
        def _():
            dwa_ref[...] = jnp.zeros_like(dwa_ref)
            dwi_ref[...] = jnp.zeros_like(dwi_ref)
            dba_ref[...] = jnp.zeros_like(dba_ref)
            dbi_ref[...] = jnp.zeros_like(dbi_ref)
            dlam_ref[...] = jnp.zeros_like(dlam_ref)

        for n in range(LRU_BLOCKS):
            sl = slice(n * LRU_BW, (n + 1) * LRU_BW)
            dwa_ref[n] += lax.dot_general(xb[:, sl], dpre_a[:, sl], (TN, ((), ())), preferred_element_type=f32)
            dwi_ref[n] += lax.dot_general(xb[:, sl], dpre_i[:, sl], (TN, ((), ())), preferred_element_type=f32)
        dba_ref[...] += sums[0]
        dbi_ref[...] += sums[1]
        dlam_ref[...] += sums[2]

    full = lambda a_: pl.BlockSpec(a_.shape, lambda i, nd=a_.ndim: (0,) * nd)
    vec = pl.BlockSpec((1, w), lambda i: (0, 0))
    mat = pl.BlockSpec((LRU_BLOCKS, LRU_BW, LRU_BW), lambda i: (0, 0, 0))
    return pl.pallas_call(
        body, grid=(nblk,), name="rg_gates_bwd",
        in_specs=[cur, cur, nxt, cur, prev, cur] + [full(a_) for a_ in (wa, wi, ba, bi, lam)],
        out_specs=[cur, mat, mat, vec, vec, vec],
        out_shape=[SDS((t, w), f32), SDS((LRU_BLOCKS, LRU_BW, LRU_BW), f32), SDS((LRU_BLOCKS, LRU_BW, LRU_BW), f32),
                   SDS((1, w), f32), SDS((1, w), f32), SDS((1, w), f32)],
        compiler_params=_cparams(("arbitrary",)))(dhs, c, c, hs, hs, xc, wa, wi, ba, bi, lam)


def _rg_conv_bwd(dxc, proj, conv_w, tr=512):
    t, w = dxc.shape
    tr = min(tr, t)
    nblk = t // tr
    cur, _, nxt = _halo_specs(tr, w, 0, nblk)
    xcur, xprev, _ = _halo_specs(tr, w, 1, nblk)

    def body(d_ref, dn_ref, x_ref, xp_ref, cw_ref, dx_ref, dcw_ref, dcb_ref):
        i = pl.program_id(0)
        d = d_ref[...]
        x = x_ref[...]
        after = _taps_after(d, dn_ref[...], i == nblk - 1)
        before = _taps_before(x, xp_ref[...], i == 0) + [x]
        dx = cw_ref[3:4, :] * d
        for s in (1, 2, 3):
            dx = dx + cw_ref[3 - s:4 - s, :] * after[s - 1]
        dx_ref[...] = dx.astype(bf16)
        dcw = jnp.concatenate([jnp.sum(d * before[k], axis=0, keepdims=True) for k in range(4)], axis=0)
        dcb = jnp.sum(d, axis=0, keepdims=True)

        @pl.when(i == 0)
        def _():
            dcw_ref[...] = dcw
            dcb_ref[...] = dcb

        @pl.when(i > 0)
        def _():
            dcw_ref[...] += dcw
            dcb_ref[...] += dcb

    return pl.pallas_call(
        body, grid=(nblk,), name="rg_conv_bwd",
        in_specs=[cur, nxt, xcur, xprev, pl.BlockSpec((4, w), lambda i: (0, 0))],
        out_specs=[cur, pl.BlockSpec((4, w), lambda i: (0, 0)), pl.BlockSpec((1, w), lambda i: (0, 0))],
        out_shape=[SDS((t, w), bf16), SDS((4, w), f32), SDS((1, w), f32)],
        compiler_params=_cparams(("arbitrary",)))(dxc, dxc, proj, proj, conv_w)


def _attn_fwd(h, wts, j, plan):
    proj = _mm_cols("attn_in", h, wts["attn_w_in"], j, bf16)
    kpad = jnp.pad(proj[:, A_W:2 * A_W], ((PAD_A, 0), (0, 0)))
    vpad = jnp.pad(proj[:, 2 * A_W:3 * A_W], ((PAD_A, 0), (0, 0)))
    bias = _bias_window(wts["attn_rel_bias"][j])
    plan = plan if j == 0 else None
    out_a = _carried(plan, "chunk_attn_fwd", wts, _chunk_attn_fwd, proj, kpad, vpad, bias)
    out_b, out_b32 = _carried(plan, "sb_attn_fwd", wts, _sb_fwd, proj)
    m = _mm_rows("attn_out", [out_a, out_b], wts["attn_w_out"], j, f32)
    return m, (proj, kpad, vpad, bias, out_a, out_b, out_b32)


def _attn_bwd(dm, h, saved, wts, j, grads):
    proj, kpad, vpad, bias, out_a, out_b, out_b32 = saved
    dout = _mm_rows_t("attn_out_t", dm, wts["attn_w_out"], j, bf16)
    grads["attn_w_out"] = _mm_wgrad("attn_out_wgrad_a", out_a, dm, grads["attn_w_out"], j, 0)
    grads["attn_w_out"] = _mm_wgrad("attn_out_wgrad_b", out_b, dm, grads["attn_w_out"], j, 1)
    dqa, dka, dva, dbias = _chunk_attn_bwd(proj, kpad, vpad, bias, dout)
    dqs, dks, dvs = _sb_bwd(proj, out_b32, dout)
    grads["attn_rel_bias"][j] = _bias_window_grad(dbias)
    dproj = jnp.concatenate([dqa, dka[PAD_A:].astype(bf16), dva[PAD_A:].astype(bf16),
                             dqs, dks.astype(bf16), dvs.astype(bf16)], axis=1)
    grads["attn_w_in"] = _mm_wgrad_cols("attn_in_wgrad", h, dproj, grads["attn_w_in"], j)
    return _mm_cols_t("attn_in_t", dproj, wts["attn_w_in"], j, f32)


def _rg_fwd(h, wts, j, plan):
    proj =_mm_cols("rg_in", h, wts["rg_w_in"], j, f32)
    small = [wts[k][j] for k in ("rg_conv_w", "rg_conv_b", "rg_w_a", "rg_w_i", "rg_b_a", "rg_b_i", "rg_lambda")]
    xc, a, u = _rg_gates_fwd(proj, *small)
    hs = _lru_scan("lru_scan_fwd", a, u, False)
    yp = _rows("rg_gate_out", lambda hv, gv: hv * _gelu(gv), [hs, (proj, D_MODEL, 0)], [], [(D_MODEL, bf16)])[0]
    m = _mm_rows("rg_out", [yp], wts["rg_w_out"], j, f32)
    return m, (proj, xc, a, hs, yp)


def _rg_bwd(dm, h, saved, wts, j, grads):
    proj, xc, a, hs, yp = saved
    dyp = _mm_rows_t("rg_out_t", dm, wts["rg_w_out"], j, f32)
    grads["rg_w_out"] = _mm_wgrad("rg_out_wgrad", yp, dm, grads["rg_w_out"], j)

    def gate_bwd(dy, hv, gv, av):
        dhs = dy * _gelu(gv)
        return dhs, av * dhs, dy * hv * _gelu_grad(gv)

    dhs, ab, dgate = _rows("rg_gate_out_bwd", gate_bwd, [dyp, hs, (proj, D_MODEL, 0), a], [],
                           [(D_MODEL, f32), (D_MODEL, f32), (D_MODEL, bf16)])
    c = _lru_scan("lru_scan_bwd", a, ab, True)
    wa, wi, ba, bi, lam = [wts[k][j] for k in ("rg_w_a", "rg_w_i", "rg_b_a", "rg_b_i", "rg_lambda")]
    dxc, dwa, dwi, dba, dbi, dlam = _rg_gates_bwd(dhs, c, hs, xc, wa, wi, ba, bi, lam)
    dxr, dcw, dcb = _rg_conv_bwd(dxc, proj, wts["rg_conv_w"][j])
    for k, v in (("rg_w_a", dwa), ("rg_w_i", dwi), ("rg_b_a", dba), ("rg_b_i", dbi), ("rg_lambda", dlam),
                 ("rg_conv_w", dcw), ("rg_conv_b", dcb)):
        grads[k][j] = v
    dproj = jnp.concatenate([dgate, dxr], axis=1)
    grads["rg_w_in"] = _mm_wgrad_cols("rg_in_wgrad", h, dproj, grads["rg_w_in"], j)
    return _mm_cols_t("rg_in_t", dproj, wts["rg_w_in"], j, f32)


def _local_step(x, target, wts, plan=None):
    t = x.shape[0]
    d = D_MODEL
    gains = {k: wts[k] for k in ("norm_mix_pre", "norm_mix_post", "norm_ffn_pre", "norm_ffn_post")}
    gain = lambda k, l: gains[k][l:l + 1]

    saved = []
    h = _rows("norm_in", _norm_fwd, [x], [gain("norm_mix_pre", 0)], [(d, bf16)])[0]
    loss_cols = None
    for l in range(DEPTH):
        j = l // 2
        m, mix_saved = (_attn_fwd if l % 2 == 0 else _rg_fwd)(h, wts, j, plan)

        def resid_next(xv, mv, g_post, g_next):
            x1 = xv + _norm_fwd(mv, g_post)
            return x1, _norm_fwd(x1, g_next)

        x1, h2 = _rows("resid_mix", resid_next, [x, m], [gain("norm_mix_post", l), gain("norm_ffn_pre", l)],
                       [(d, f32), (d, bf16)])
        g, u, hid = _carried(plan if l == 0 else None, "ffn_up", wts, _ffn_up, h2, wts["ffn_w_gate"],
                             wts["ffn_w_up"], l)
        f = _ffn_down(hid, wts["ffn_w_down"], l)
        saved.append((x, h, m, mix_saved, x1, h2, g, u, hid, f))
        if l + 1 < DEPTH:
            x, h = _rows("resid_ffn", resid_next, [x1, f], [gain("norm_ffn_post", l), gain("norm_mix_pre", l + 1)],
                         [(d, f32), (d, bf16)])
        else:
            def resid_loss(xv, fv, tv, g_post):
                err = xv + _norm_fwd(fv, g_post) - tv
                return err * (1.0 / d), jnp.sum(err * err, axis=0, keepdims=True)

            dx, loss_cols = _rows("resid_loss", resid_loss, [x1, f, target], [gain("norm_ffn_post", l)],
                                  [(d, f32)], [((1, d), f32)])
    loss = 0.5 * jnp.sum(loss_cols) / d

    grads = {k: {} for k in SMALL_GRADS}
    for k in BIG_GRADS:
        shp = wts[k].shape
        grads[k] = _Fresh((shp[0],) + shp[2:] if shp[1] == 1 else shp)

    def norm_bwd_cast(uv, dyv, gv):
        du, dg = _norm_bwd(uv, dyv, gv)
        return du, dg

    def norm_bwd_resid(uv, dhv, dxv, gv):
        du, dg = _norm_bwd(uv, dhv, gv)
        return dxv + du, dg

    for l in reversed(range(DEPTH)):
        j = l // 2
        x_in, h, m, mix_saved, x1, h2, g, u, hid, f = saved[l]
        df, grads["norm_ffn_post"][l] = _rows("norm_ffn_post_bwd", norm_bwd_cast, [f, dx], [gain("norm_ffn_post", l)],
                                              [(d, bf16)], [((1, d), f32)])
        dg, du = _ffn_down_bwd(df, wts["ffn_w_down"], l, g, u)
        grads["ffn_w_down"] = _ffn_wgrad_down(hid, df, grads["ffn_w_down"], l)
        dh2 = _ffn_up_bwd(dg, du, wts["ffn_w_gate"], wts["ffn_w_up"], l)
        grads["ffn_w_gate"], grads["ffn_w_up"] = _ffn_wgrad_up(h2, dg, du, grads["ffn_w_gate"], grads["ffn_w_up"], l)
        dx1, grads["norm_ffn_pre"][l] = _rows("norm_ffn_pre_bwd", norm_bwd_resid, [x1, dh2, dx],
                                              [gain("norm_ffn_pre", l)], [(d, f32)], [((1, d), f32)])
        dm, grads["norm_mix_post"][l] = _rows("norm_mix_post_bwd", norm_bwd_cast, [m, dx1], [gain("norm_mix_post", l)],
                                              [(d, bf16)], [((1, d), f32)])
        dh = (_attn_bwd if l % 2 == 0 else _rg_bwd)(dm, h, mix_saved, wts, j, grads)
        dx, grads["norm_mix_pre"][l] = _rows("norm_mix_pre_bwd", norm_bwd_resid, [x_in, dh, dx1],
                                             [gain("norm_mix_pre", l)], [(d, f32)], [((1, d), f32)])
    return loss, dx, grads


ANY = pl.BlockSpec(memory_space=pl.ANY)
PACK_COLS = 1024
SMALL_ROWS = 288


def _mesh_pos():
    x, y, c = lax.axis_index("x"), lax.axis_index("y"), lax.axis_index("c")
    return x, y, c, [(1 - x, y), (x, 1 - y), (1 - x, 1 - y)]


def _run_copies(copies):
    for cp in copies:
        cp.start()
    for cp in copies:
        cp.wait()


GATHER_SEMS = 7


def _gather_copies(items, ins, outs, send, recv):
    x, y, c, chips = _mesh_pos()
    q = 2 * x + y
    sibling = (x, y, 1 - c)

    def copy(k, src, dst, to):
        return pltpu.make_async_remote_copy(src_ref=src, dst_ref=dst, send_sem=send.at[k], recv_sem=recv.at[k],
                                            device_id=to, device_id_type=MESH)

    own, sent, passed = [], [], []
    for i, (t, l0, nl) in enumerate(items):
        lay = pl.ds(l0, nl)
        half = ins[t].shape[1] // 2
        rows = pl.ds(pl.multiple_of(c * half, half), half)
        own.append(copy(GATHER_SEMS * i, ins[t].at[lay], outs[t].at[lay, q], sibling))
        for j, (px, py) in enumerate(chips):
            sent.append(copy(GATHER_SEMS * i + 1 + j, ins[t].at[lay, rows], outs[t].at[lay, q, rows], (px, py, c)))
            landed = outs[t].at[lay, 2 * px + py, rows]
            passed.append(copy(GATHER_SEMS * i + 4 + j, landed, landed, sibling))
    return own, sent, passed


def _gather_start(items, ins, outs, send, recv):
    own, sent, _ = _gather_copies(items, ins, outs, send, recv)
    for cp in own + sent:
        cp.start()


def _gather_finish(items, ins, outs, send, recv):
    own, sent, passed = _gather_copies(items, ins, outs, send, recv)
    for arrived, forward in zip(sent, passed):
        arrived.wait_recv()
        forward.start()
    for cp in sent:
        cp.wait_send()
    for cp in own + passed:
        cp.wait()


def _gather_call(items, shards):
    n = len(shards)
    nsem = GATHER_SEMS * len(items)

    def body(*refs):
        ins, outs = refs[:n], refs[n:2 * n]
        _gather_start(items, ins, outs, *refs[2 * n:])
        _gather_finish(items, ins, outs, *refs[2 * n:])

    return pl.pallas_call(
        body, name="weight_all_gather", in_specs=[ANY] * n, out_specs=[ANY] * n,
        out_shape=[SDS((s.shape[0], N_CHIPS) + s.shape[1:], s.dtype) for s in shards],
        scratch_shapes=[pltpu.SemaphoreType.DMA((nsem,)), pltpu.SemaphoreType.DMA((nsem,))])(*shards)


def _call(body, operands, *, name, grid, in_specs, out_specs, out_shape, sem, scratch=(), gather=None):
    if gather is None:
        return pl.pallas_call(body, grid=grid, in_specs=in_specs, out_specs=out_specs, out_shape=out_shape,
                              scratch_shapes=list(scratch), name=name, compiler_params=_cparams(sem))(*operands), None
    items, shards, gathered = gather
    n_in, n_out, n_scr, ng = len(operands), len(out_shape), len(scratch), len(shards)
    nsem = GATHER_SEMS * len(items)

    def full(*refs):
        ins, sh = refs[:n_in], refs[n_in:n_in + ng]
        outs = refs[n_in + 2 * ng:n_in + 2 * ng + n_out]
        io = refs[n_in + 2 * ng + n_out:n_in + 3 * ng + n_out]
        scr = refs[n_in + 3 * ng + n_out:]
        ids = [pl.program_id(a) for a in range(len(grid))]
        first = functools.reduce(jnp.logical_and, [i == 0 for i in ids])
        last = functools.reduce(jnp.logical_and, [i == g - 1 for i, g in zip(ids, grid)])

        @pl.when(first)
        def _():
            _gather_start(items, sh, io, scr[n_scr], scr[n_scr + 1])

        body(*ins, *outs, *scr[:n_scr])

        @pl.when(last)
        def _():
            _gather_finish(items, sh, io, scr[n_scr], scr[n_scr + 1])

    res = pl.pallas_call(
        full, grid=grid, in_specs=list(in_specs) + [ANY] * (2 * ng), out_specs=list(out_specs) + [ANY] * ng,
        out_shape=list(out_shape) + [SDS(g.shape, g.dtype) for g in gathered],
        scratch_shapes=list(scratch) + [pltpu.SemaphoreType.DMA((nsem,)), pltpu.SemaphoreType.DMA((nsem,))],
        input_output_aliases={n_in + ng + t: n_out + t for t in range(ng)}, name=name,
        compiler_params=_cparams(("arbitrary",) * len(grid)))(*operands, *shards, *gathered)
    return res[:n_out], res[n_out:]


def _pair_exchange(gs):
    n = len(gs)

    def body(*refs):
        ins, outs = refs[:n], refs[n:2 * n]
        send, recv = refs[2 * n:]
        x, y, c, _ = _mesh_pos()
        copies = []
        for t in range(n):
            half = ins[t].shape[2] // 2
            src = ins[t].at[:, :, pl.ds(pl.multiple_of((1 - c) * half, SUBLANES), half)]
            copies.append(pltpu.make_async_remote_copy(
                src_ref=src, dst_ref=outs[t], send_sem=send.at[t], recv_sem=recv.at[t],
                device_id=(x, y, 1 - c), device_id_type=MESH))
        _run_copies(copies)

    return pl.pallas_call(
        body, name="grad_pair_exchange", in_specs=[ANY] * n, out_specs=[ANY] * n,
        out_shape=[SDS(g.shape[:2] + (g.shape[2] // 2, g.shape[3]), f32) for g in gs],
        scratch_shapes=[pltpu.SemaphoreType.DMA((n,)), pltpu.SemaphoreType.DMA((n,))])(*gs)


def _pair_sum(name, g, got, c):
    l, s, r, cols = g.shape

    def body(c_ref, a_ref, b_ref, o_ref):
        o_ref[...] = (a_ref[...] + b_ref[...]).astype(bf16)

    blk = (None, None, r // 2, cols)
    return pl.pallas_call(
        body, name=name, out_shape=SDS(got.shape, bf16),
        grid_spec=pltpu.PrefetchScalarGridSpec(
            num_scalar_prefetch=1, grid=(l, s),
            in_specs=[pl.BlockSpec(blk, lambda i, q, c_ref: (i, q, c_ref[0], 0)),
                      pl.BlockSpec(blk, lambda i, q, c_ref: (i, q, 0, 0))],
            out_specs=pl.BlockSpec(blk, lambda i, q, c_ref: (i, q, 0, 0))),
        compiler_params=_cparams(("parallel", "parallel")))(c, g, got)


def _chip_exchange(hs):
    n = len(hs)

    def body(*refs):
        ins, outs = refs[:n], refs[n:2 * n]
        send, recv = refs[2 * n:]
        x, y, c, chips = _mesh_pos()
        q = 2 * x + y
        copies = []
        for t in range(n):
            for j, (px, py) in enumerate(chips):
                copies.append(pltpu.make_async_remote_copy(
                    src_ref=ins[t].at[:, 2 * px + py], dst_ref=outs[t].at[:, q], send_sem=send.at[3 * t + j],
                    recv_sem=recv.at[3 * t + j], device_id=(px, py, c), device_id_type=MESH))
        _run_copies(copies)

    return pl.pallas_call(
        body, name="grad_chip_exchange", in_specs=[ANY] * n, out_specs=[ANY] * n,
        out_shape=[SDS(h.shape, h.dtype) for h in hs],
        scratch_shapes=[pltpu.SemaphoreType.DMA((3 * n,)), pltpu.SemaphoreType.DMA((3 * n,))])(*hs)


def _chip_sum(name, s, h, pos):
    l, _, r, cols = s.shape

    def body(pos_ref, s0, s1, s2, s3, own_ref, o_ref):
        vals = [jnp.where(pos_ref[0] == p, own_ref[...], ref[...]).astype(f32) for p, ref in enumerate((s0, s1, s2, s3))]
        o_ref[...] = ((vals[0] + vals[1]) + vals[2]) + vals[3]

    blk = (None, None, r, cols)
    slot = lambda p: pl.BlockSpec(blk, lambda i, pos_ref: (i, jnp.where(pos_ref[0] == p, (p + 1) % N_CHIPS, p), 0, 0))
    return pl.pallas_call(
        body, name=name, out_shape=SDS((l, 2 * r, cols), f32),
        grid_spec=pltpu.PrefetchScalarGridSpec(
            num_scalar_prefetch=1, grid=(l,),
            in_specs=[slot(p) for p in range(N_CHIPS)] + [pl.BlockSpec(blk, lambda i, pos_ref: (i, pos_ref[0], 0, 0))],
            out_specs=pl.BlockSpec((None, r, cols), lambda i, pos_ref: (i, pos_ref[1], 0))),
        compiler_params=_cparams(("parallel",)))(pos, s, s, s, s, h)


def _pair_gather(fulls):
    n = len(fulls)

    def body(*refs):
        ins, outs = refs[:n], refs[n:2 * n]
        send, recv = refs[2 * n:]
        x, y, c, _ = _mesh_pos()
        copies = []
        for t in range(n):
            half = outs[t].shape[1] // 2
            rows = outs[t].at[:, pl.ds(pl.multiple_of(c * half, SUBLANES), half)]
            copies.append(pltpu.make_async_remote_copy(
                src_ref=rows, dst_ref=rows, send_sem=send.at[t], recv_sem=recv.at[t],
                device_id=(x, y, 1 - c), device_id_type=MESH))
        _run_copies(copies)

    return pl.pallas_call(
        body, name="grad_pair_gather", in_specs=[ANY] * n, out_specs=[ANY] * n,
        out_shape=[SDS(f.shape, f32) for f in fulls], input_output_aliases={t: t for t in range(n)},
        scratch_shapes=[pltpu.SemaphoreType.DMA((n,)), pltpu.SemaphoreType.DMA((n,))])(*fulls)


COL_SHARDED = ("attn_w_in", "rg_w_in", "ffn_w_gate", "ffn_w_up")
ROW_SHARDED = ("attn_w_out", "rg_w_out")
GATES = ("rg_w_a", "rg_w_i")
VECTORS = ("rg_conv_w", "rg_conv_b", "rg_b_a", "rg_b_i", "rg_lambda")
REPLICATED = ("norm_mix_pre", "norm_mix_post", "norm_ffn_pre", "norm_ffn_post", "attn_rel_bias")
BIG_GRADS = COL_SHARDED + ROW_SHARDED + ("ffn_w_down",)
SMALL_GRADS = GATES + VECTORS + REPLICATED
WEIGHTS =("attn_w_in", "attn_rel_bias", "attn_w_out", "rg_w_in", "rg_conv_w", "rg_conv_b", "rg_w_a", "rg_b_a",
           "rg_w_i", "rg_b_i", "rg_lambda", "rg_w_out", "norm_mix_pre", "norm_mix_post", "norm_ffn_pre",
           "norm_ffn_post", "ffn_w_gate", "ffn_w_up", "ffn_w_down")
SMALL = VECTORS + REPLICATED


GATHER_PARTS = {
    "first": (("attn_w_in", 0, 1), ("attn_w_out", 0, 1), ("rg_w_a", 0, 8), ("rg_w_i", 0, 8), ("vec", 0, 1)),
    "chunk_attn_fwd": (("ffn_w_gate", 0, 1), ("ffn_w_up", 0, 1), ("ffn_w_down", 0, 1), ("rg_w_in", 0, 1),
                       ("rg_w_out", 0, 1)),
    "sb_attn_fwd": (("ffn_w_gate", 1, 3), ("ffn_w_up", 1, 3), ("ffn_w_down", 1, 3), ("attn_w_in", 1, 1),
                    ("attn_w_out", 1, 1)),
    "ffn_up": (("rg_w_in", 1, 1), ("rg_w_out", 1, 1)),
}


TRANSPOSED = ("ffn_w_gate", "ffn_w_up")


def _natural(name, a):
    return jnp.swapaxes(a, 1, 2) if name in TRANSPOSED else a


class _WeightGather:
    def __init__(self, w):
        self.w = w
        self.names = list(COL_SHARDED + ROW_SHARDED + GATES + ("ffn_w_down", "vec"))
        self.shards = {}
        for k in self.names[:-1]:
            a = _natural(k, w[k]).astype(bf16)
            self.shards[k] = a.reshape((-1,) + a.shape[-2:])
        self.shards["vec"] = jnp.concatenate([w[k].reshape(-1) for k in VECTORS]).reshape(1, -1, LANES)
        got = _gather_call(self._items("first", self.names), [self.shards[k] for k in self.names])
        self.raw = dict(zip(self.names, got))

    @staticmethod
    def _items(part, names):
        return [(names.index(k), l0, nl) for k, l0, nl in GATHER_PARTS[part]]

    def part(self, part):
        names = list(dict.fromkeys(k for k, _, _ in GATHER_PARTS[part]))
        return (self._items(part, names), [self.shards[k] for k in names], [self.raw[k] for k in names]), names

    def views(self):
        got, w = self.raw, self.w
        out = {k: w[k] for k in REPLICATED}
        for k in COL_SHARDED + ("ffn_w_down",):
            out[k] = got[k]
        for k in ROW_SHARDED:
            l, s, ks, n = got[k].shape
            out[k] = got[k].reshape(l, 1, s * ks, n)
        for k in GATES:
            out[k] = got[k].reshape(2, LRU_BLOCKS, LRU_BW, LRU_BW)
        vec = got["vec"].reshape(N_CHIPS, -1)
        off = 0
        for k in VECTORS:
            shp = w[k].shape
            n = int(np.prod(shp))
            piece = vec[:, off:off + n].reshape((N_CHIPS,) + shp)
            off += n
            if k == "rg_conv_w":
                out[k] = piece.reshape(N_CHIPS, 2, 4, 256).transpose(1, 2, 0, 3).reshape(2, 4, D_MODEL)
            elif k in ("rg_b_a", "rg_b_i"):
                out[k] = piece.transpose(1, 2, 0, 3).reshape(2, 1, D_MODEL)
            else:
                out[k] = piece.transpose(1, 0, 2).reshape(2, 1, D_MODEL)
        return out


def _carried(plan, part, wts, fn, *args):
    if plan is None:
        return fn(*args, None)[0]
    gather, names = plan.part(part)
    out, new = fn(*args, gather)
    plan.raw.update(zip(names, new))
    wts.update(plan.views())
    return out


def _grad_blocks(name, g):
    st = jnp.stack([g[i] for i in sorted(g)])
    if name in GATES:
        st = st.reshape(2, LRU_BLOCKS, N_CHIPS, LRU_BW // N_CHIPS, LRU_BW).transpose(2, 0, 1, 3, 4)
    elif name == "rg_conv_w":
        st = st.reshape(2, 4, N_CHIPS, -1).transpose(2, 0, 1, 3)
    elif name in ("rg_b_a", "rg_b_i"):
        st = st.reshape(2, LRU_BLOCKS, N_CHIPS, -1).transpose(2, 0, 1, 3)
    elif name in VECTORS:
        st = st.reshape(2, N_CHIPS, -1).transpose(1, 0, 2)
    else:
        st = jnp.broadcast_to(st.reshape(1, -1), (N_CHIPS, st.size))
    return st.reshape(N_CHIPS, -1)


def _reduce_gradients(grads, shard_shapes):
    gs = []
    for k in BIG_GRADS:
        g = grads[k]
        if g.ndim == 3:
            g = g.reshape(g.shape[0], N_CHIPS, g.shape[1] // N_CHIPS, g.shape[2])
        gs.append(g)
    blocks = [_grad_blocks(k, grads[k]) for k in SMALL_GRADS]
    used = sum(b.shape[1] for b in blocks)
    small = jnp.concatenate(blocks + [jnp.zeros((N_CHIPS, SMALL_ROWS * PACK_COLS - used), f32)], axis=1)
    gs.append(small.reshape(1, N_CHIPS, SMALL_ROWS, PACK_COLS))
    names = BIG_GRADS + ("small",)
    c = lax.axis_index("c").astype(jnp.int32).reshape(1)
    pos = jnp.stack([2 * lax.axis_index("x") + lax.axis_index("y"), lax.axis_index("c")]).astype(jnp.int32)
    parts = [_pair_sum("grad_pair_sum_" + k, g, r, c) for k, g, r in zip(names, gs, _pair_exchange(gs))]
    slots = _chip_exchange(parts)
    full = _pair_gather([_chip_sum("grad_chip_sum_" + k, s, h, pos) for k, s, h in zip(names, slots, parts)])
    out = {k: f.reshape(shard_shapes[k]) for k, f in zip(BIG_GRADS, full)}
    flat, off = full[-1].reshape(-1), 0
    for k in SMALL_GRADS:
        n = int(np.prod(shard_shapes[k]))
        out[k] = flat[off:off + n].reshape(shard_shapes[k])
        off += n
    return out


def _adamw_fn(w, g, m, v):
    m = ADAM_B1 * m + (1.0 - ADAM_B1) * g
    v = ADAM_B2 * v + (1.0 - ADAM_B2) * (g * g)
    m_hat = m / (1.0 - ADAM_B1 ** ADAM_STEP)
    v_hat = v / (1.0 - ADAM_B2 ** ADAM_STEP)
    return -ADAM_LR * (m_hat / (jnp.sqrt(v_hat) + ADAM_EPS) + ADAM_WD * w), m, v


def _adamw(name, w, g, m, v):
    shp = w.shape
    if w.size >= 1 << 16:
        width = shp[-1]
        ops = [a.reshape(-1, width) for a in (w, g, m, v)]
        res = _rows(name, _adamw_fn, ops, [], [(width, f32)] * 3)
        return [r.reshape(shp) for r in res]
    n = w.size
    rows = -(-n // (SUBLANES * LANES)) * SUBLANES
    ops = [jnp.pad(a.reshape(-1), (0, rows * LANES - n)).reshape(rows, LANES) for a in (w, g, m, v)]
    res = _rows(name, _adamw_fn, ops, [], [(LANES, f32)] * 3, tr=rows)
    return [r.reshape(-1)[:n].reshape(shp) for r in res]


def kernel(x, attn_w_in, attn_rel_bias, attn_w_out, rg_w_in, rg_conv_w, rg_conv_b, rg_w_a, rg_b_a, rg_w_i, rg_b_i, rg_lambda, rg_w_out, norm_mix_pre, norm_mix_post, norm_ffn_pre, norm_ffn_post, ffn_w_gate, ffn_w_up, ffn_w_down, loss_target, m_attn_w_in, m_attn_rel_bias, m_attn_w_out, m_rg_w_in, m_rg_conv_w, m_rg_conv_b, m_rg_w_a, m_rg_b_a, m_rg_w_i, m_rg_b_i, m_rg_lambda, m_rg_w_out, m_norm_mix_pre, m_norm_mix_post, m_norm_ffn_pre, m_norm_ffn_post, m_ffn_w_gate, m_ffn_w_up, m_ffn_w_down, v_attn_w_in, v_attn_rel_bias, v_attn_w_out, v_rg_w_in, v_rg_conv_w, v_rg_conv_b, v_rg_w_a, v_rg_b_a, v_rg_w_i, v_rg_b_i, v_rg_lambda, v_rg_w_out, v_norm_mix_pre, v_norm_mix_post, v_norm_ffn_pre, v_norm_ffn_post, v_ffn_w_gate, v_ffn_w_up, v_ffn_w_down):
    w = dict(zip(WEIGHTS, (attn_w_in, attn_rel_bias, attn_w_out, rg_w_in, rg_conv_w, rg_conv_b, rg_w_a, rg_b_a, rg_w_i,
                           rg_b_i, rg_lambda, rg_w_out, norm_mix_pre, norm_mix_post, norm_ffn_pre, norm_ffn_post,
                           ffn_w_gate, ffn_w_up, ffn_w_down)))
    m = dict(zip(WEIGHTS, (m_attn_w_in, m_attn_rel_bias, m_attn_w_out, m_rg_w_in, m_rg_conv_w, m_rg_conv_b, m_rg_w_a,
                           m_rg_b_a, m_rg_w_i, m_rg_b_i, m_rg_lambda, m_rg_w_out, m_norm_mix_pre, m_norm_mix_post,
                           m_norm_ffn_pre, m_norm_ffn_post, m_ffn_w_gate, m_ffn_w_up, m_ffn_w_down)))
    v = dict(zip(WEIGHTS, (v_attn_w_in, v_attn_rel_bias, v_attn_w_out, v_rg_w_in, v_rg_conv_w, v_rg_conv_b, v_rg_w_a,
                           v_rg_b_a, v_rg_w_i, v_rg_b_i, v_rg_lambda, v_rg_w_out, v_norm_mix_pre, v_norm_mix_post,
                           v_norm_ffn_pre, v_norm_ffn_post, v_ffn_w_gate, v_ffn_w_up, v_ffn_w_down)))
    plan = _WeightGather(w)
    loss, dx, grads = _local_step(x[0], loss_target[0], plan.views(), plan)
    loss = lax.psum(loss, ("x", "y", "c"))
    g = _reduce_gradients(grads, {k: _natural(k, w[k]).shape for k in WEIGHTS})

    big = [k for k in WEIGHTS if k not in SMALL]
    upd = {}
    for k in big:
        res = _adamw("adamw_" + k, _natural(k, w[k]), g[k], _natural(k, m[k]), _natural(k, v[k]))
        upd[k] = [_natural(k, r) for r in res]
        g[k] = _natural(k, g[k])
    cat = lambda d: jnp.concatenate([d[k].reshape(-1) for k in SMALL])
    small = _adamw("adamw_small", cat(w), cat(g), cat(m), cat(v))
    off = 0
    for k in SMALL:
        n = w[k].size
        upd[k] = [r[off:off + n].reshape(w[k].shape) for r in small]
        off += n
    return (loss, dx[None], *[g[k] for k in WEIGHTS], *[upd[k][0] for k in WEIGHTS],
            *[upd[k][1] for k in WEIGHTS], *[upd[k][2] for k in WEIGHTS])
```

```python
import functools

import numpy as np
import jax
import jax.numpy as jnp
from jax import lax
from jax.experimental import pallas as pl
from jax.experimental.pallas import tpu as pltpu

f32 = jnp.float32
bf16 = jnp.bfloat16
SDS = jax.ShapeDtypeStruct
MESH = pl.DeviceIdType.MESH

D_MODEL = 1024
N_CHIPS = 4
DEPTH = 4
HEAD_DIM = 64
CHUNK = 64
N_LEFT = 8
REL_CLIP = 256
A_W = 512
LRU_BLOCKS = 4
LRU_BW = 256
LRU_C = 8.0
D_FF = 2816
RMS_EPS = 1e-6
LANES = 128
SUBLANES = 8
VMEM_LIMIT = 56 * 1024 * 1024

QB_A = 2 * CHUNK
QSUB_A = 4
KW_A = QB_A + N_LEFT * CHUNK
PAD_A = N_LEFT * CHUNK
EXT_A = 768
SB_BLK = 256
QSUB_B = 2
SB_DEAD = -110.0

ADAM_LR, ADAM_B1, ADAM_B2, ADAM_EPS, ADAM_WD, ADAM_STEP = 0.001, 0.9, 0.999, 1e-08, 0.01, 10


def _cparams(sem):
    return pltpu.CompilerParams(dimension_semantics=sem, vmem_limit_bytes=VMEM_LIMIT)


def _gemm(name, operands, in_specs, o_spec, out_shape, grid, dims, acc_shape, into=None):
    nred = grid[2]
    npair = len(operands) // 2
    nin = 2 * npair + (into is not None)

    def body(*refs):
        o_ref = refs[nin]
        p = None
        for t in range(npair):
            d = lax.dot_general(refs[2 * t][...], refs[2 * t + 1][...], (dims, ((), ())),
                                preferred_element_type=f32)
            p = d if p is None else p + d
        if nred == 1:
            o_ref[...] = p.astype(o_ref.dtype)
        else:
            acc = refs[nin + 1]
            r = pl.program_id(2)

            @pl.when(r == 0)
            def _():
                acc[...] = p

            @pl.when(r > 0)
            def _():
                acc[...] += p

            @pl.when(r == nred - 1)
            def _():
                o_ref[...] = acc[...].astype(o_ref.dtype)

    scratch = [] if nred == 1 else [pltpu.VMEM(acc_shape, f32)]
    extra, alias = ([], {}) if into is None else ([into], {2 * npair: 0})
    return pl.pallas_call(
        body, grid=grid, in_specs=list(in_specs) + [pl.BlockSpec(memory_space=pl.ANY)] * len(extra),
        out_specs=o_spec, out_shape=out_shape, scratch_shapes=scratch, name=name, input_output_aliases=alias,
        compiler_params=_cparams(("parallel", "parallel", "arbitrary")))(*operands, *extra)


class _Fresh:
    def __init__(self, shape):
        self.shape = tuple(shape)


def _into(buf):
    return None if isinstance(buf, _Fresh) else buf


NN = ((1,), (0,))
NT = ((1,), (1,))
TN = ((0,), (0,))


def _tile(t, want=1024):
    return min(want, t)


def _mm_cols(name, a, w, l, out_dtype):
    t, k = a.shape
    _, s, _, ns = w.shape
    tm = _tile(t)
    return _gemm(
        name, [a, w],
        [pl.BlockSpec((tm, k), lambda i, j, r: (i, 0)),
         pl.BlockSpec((None, None, k, ns), lambda i, j, r: (l, j, 0, 0))],
        pl.BlockSpec((tm, ns), lambda i, j, r: (i, j)),
        SDS((t, s * ns), out_dtype), (t // tm, s, 1), NN, None)


def _mm_cols_t(name, dy, w, l, out_dtype):
    t = dy.shape[0]
    _, s, k, ns = w.shape
    tm = _tile(t)
    return _gemm(
        name, [dy, w],
        [pl.BlockSpec((tm, ns), lambda i, j, r: (i, r)),
         pl.BlockSpec((None, None, k, ns), lambda i, j, r: (l, r, 0, 0))],
        pl.BlockSpec((tm, k), lambda i, j, r: (i, 0)),
        SDS((t, k), out_dtype), (t // tm, 1, s), NT, (tm, k))


def _mm_wgrad_cols(name, a, dy, buf, l):
    t, k = a.shape
    _, s, _, ns = buf.shape
    tt = _tile(t)
    return _gemm(
        name, [a, dy],
        [pl.BlockSpec((tt, k), lambda i, j, r: (r, 0)),
         pl.BlockSpec((tt, ns), lambda i, j, r: (r, i))],
        pl.BlockSpec((None, None, k, ns), lambda i, j, r: (l, i, 0, 0)),
        SDS(buf.shape, f32), (s, 1, t // tt), TN, (k, ns), into=_into(buf))


def _mm_rows(name, parts, w, l, out_dtype):
    t = parts[0].shape[0]
    n = w.shape[3]
    tm = _tile(t)
    ops, specs = [], []
    for p_i, a in enumerate(parts):
        kp = a.shape[1]
        ops += [a, w]
        specs += [pl.BlockSpec((tm, kp), lambda i, j, r: (i, 0)),
                  pl.BlockSpec((None, None, kp, n), lambda i, j, r, p_i=p_i: (l, 0, p_i, 0))]
    return _gemm(name, ops, specs, pl.BlockSpec((tm, n), lambda i, j, r: (i, 0)),
                 SDS((t, n), out_dtype), (t // tm, 1, 1), NN, None)


def _mm_rows_t(name, dy, w, l, out_dtype):
    t, n = dy.shape
    k = w.shape[2]
    tm = _tile(t)
    return _gemm(
        name, [dy, w],
        [pl.BlockSpec((tm, n), lambda i, j, r: (i, 0)),
         pl.BlockSpec((None, None, k, n), lambda i, j, r: (l, 0, 0, 0))],
        pl.BlockSpec((tm, k), lambda i, j, r: (i, 0)),
        SDS((t, k), out_dtype), (t // tm, 1, 1), NT, None)


def _mm_wgrad(name, a, dy, buf, l, part=0):
    t, k = a.shape
    n = dy.shape[1]
    tt = _tile(t)
    return _gemm(
        name, [a, dy],
        [pl.BlockSpec((tt, k), lambda i, j, r: (r, 0)),
         pl.BlockSpec((tt, n), lambda i, j, r: (r, 0))],
        pl.BlockSpec((None, k, n), lambda i, j, r: (l, part, 0)),
        SDS(buf.shape, f32), (1, 1, t // tt), TN, (k, n), into=_into(buf))


def _ffn_up(h, wg, wu, l, gather):
    t, k = h.shape
    s, fs = wg.shape[1], wg.shape[2]
    tm = _tile(t)

    def body(h_ref, wg_ref, wu_ref, g_ref, u_ref, hid_ref):
        hv = h_ref[...]
        g = lax.dot_general(hv, wg_ref[...], (NT, ((), ())), preferred_element_type=f32)
        u = lax.dot_general(hv, wu_ref[...], (NT, ((), ())), preferred_element_type=f32)
        g_ref[...] = g.astype(bf16)
        u_ref[...] = u.astype(bf16)
        hid_ref[...] = (g * jax.nn.sigmoid(g) * u).astype(bf16)

    wspec = pl.BlockSpec((None, None, fs, k), lambda j, i: (l, j, 0, 0))
    ospec = pl.BlockSpec((None, tm, fs), lambda j, i: (j, i, 0))
    return _call(
        body, [h, wg, wu], grid=(s, t // tm), name="ffn_up",
        in_specs=[pl.BlockSpec((tm, k), lambda j, i: (i, 0)), wspec, wspec],
        out_specs=[ospec, ospec, ospec], out_shape=[SDS((s, t, fs), bf16)] * 3,
        sem=("parallel", "parallel"), gather=gather)


def _ffn_down(hid, wd, l):
    s, t, fs = hid.shape
    n = wd.shape[3]
    tm = _tile(t, 512)
    ops, specs = [], []
    for r in range(s):
        ops += [hid, wd]
        specs += [pl.BlockSpec((None, tm, fs), lambda i, j, k, r=r: (r, i, 0)),
                  pl.BlockSpec((None, None, fs, n), lambda i, j, k, r=r: (l, r, 0, 0))]
    return _gemm("ffn_down", ops, specs, pl.BlockSpec((tm, n), lambda i, j, k: (i, 0)),
                 SDS((t, n), f32), (t // tm, 1, 1), NN, None)


def _ffn_down_bwd(df, wd, l, g, u):
    t, n = df.shape
    s, fs = wd.shape[1], wd.shape[2]
    tm = _tile(t)

    def body(df_ref, wd_ref, g_ref, u_ref, dg_ref, du_ref):
        dh = lax.dot_general(df_ref[...], wd_ref[...], (NT, ((), ())), preferred_element_type=f32)
        gv = g_ref[...].astype(f32)
        uv = u_ref[...].astype(f32)
        sg = jax.nn.sigmoid(gv)
        du_ref[...] = (dh * gv * sg).astype(bf16)
        dg_ref[...] = (dh * uv * (sg * (1.0 + gv * (1.0 - sg)))).astype(bf16)

    bspec = pl.BlockSpec((None, tm, fs), lambda j, i: (j, i, 0))
    return pl.pallas_call(
        body, grid=(s, t // tm), name="ffn_down_bwd",
        in_specs=[pl.BlockSpec((tm, n), lambda j, i: (i, 0)),
                  pl.BlockSpec((None, None, fs, n), lambda j, i: (l, j, 0, 0)), bspec, bspec],
        out_specs=[bspec, bspec], out_shape=[SDS((s, t, fs), bf16)] * 2,
        compiler_params=_cparams(("parallel", "parallel")))(df, wd, g, u)


def _ffn_up_bwd(dg, du, wg, wu, l):
    s, t, fs = dg.shape
    k = wg.shape[3]
    tm = _tile(t, 512)
    ops, specs = [], []
    for r in range(s):
        aspec = pl.BlockSpec((None, tm, fs), lambda i, j, kk, r=r: (r, i, 0))
        wspec = pl.BlockSpec((None, None, fs, k), lambda i, j, kk, r=r: (l, r, 0, 0))
        ops += [dg, wg, du, wu]
        specs += [aspec, wspec, aspec, wspec]
    return _gemm("ffn_up_bwd", ops, specs, pl.BlockSpec((tm, k), lambda i, j, kk: (i, 0)),
                 SDS((t, k), f32), (t // tm, 1, 1), NN, None)


def _ffn_wgrad_up(h, dg, du, buf_g, buf_u, l):
    t, k = h.shape
    s, _, fs = dg.shape
    tt = _tile(t)
    nred = t // tt

    fresh = isinstance(buf_g, _Fresh)

    def body(*refs):
        h_ref, dg_ref, du_ref = refs[:3]
        og_ref, ou_ref, acc_g, acc_u = refs[-4:]
        r = pl.program_id(1)
        hv = h_ref[...]
        pg = lax.dot_general(dg_ref[...], hv, (TN, ((), ())), preferred_element_type=f32)
        pu = lax.dot_general(du_ref[...], hv, (TN, ((), ())), preferred_element_type=f32)

        @pl.when(r == 0)
        def _():
            acc_g[...] = pg
            acc_u[...] = pu

        @pl.when(r > 0)
        def _():
            acc_g[...] += pg
            acc_u[...] += pu

        @pl.when(r == nred - 1)
        def _():
            og_ref[...] = acc_g[...]
            ou_ref[...] = acc_u[...]

    dspec = pl.BlockSpec((None, tt, fs), lambda i, r: (i, r, 0))
    ospec = pl.BlockSpec((None, None, fs, k), lambda i, r: (l, i, 0, 0))
    extra, alias = ([], {}) if fresh else ([buf_g, buf_u], {3: 0, 4: 1})
    return pl.pallas_call(
        body, grid=(s, nred), name="ffn_wgrad_up",
        in_specs=[pl.BlockSpec((tt, k), lambda i, r: (r, 0)), dspec, dspec] + [ANY] * len(extra),
        out_specs=[ospec, ospec], out_shape=[SDS(buf_g.shape, f32), SDS(buf_u.shape, f32)],
        scratch_shapes=[pltpu.VMEM((fs, k), f32)] * 2, input_output_aliases=alias,
        compiler_params=_cparams(("parallel", "arbitrary")))(h, dg, du, *extra)


def _ffn_wgrad_down(hid, df, buf, l):
    s, t, fs = hid.shape
    n = df.shape[1]
    tt = _tile(t)
    return _gemm(
        "ffn_wgrad_down", [hid, df],
        [pl.BlockSpec((None, tt, fs), lambda i, j, r: (i, r, 0)),
         pl.BlockSpec((tt, n), lambda i, j, r: (r, 0))],
        pl.BlockSpec((None, None, fs, n), lambda i, j, r: (l, i, 0, 0)),
        SDS(buf.shape, f32), (s, 1, t // tt), TN, (fs, n), into=_into(buf))


def _rows(name, fn, rows, consts, row_outs, acc_outs=(), tr=512):
    rows = [r if isinstance(r, tuple) else (r, r.shape[1], 0) for r in rows]
    t = rows[0][0].shape[0]
    tr = max(d for d in range(SUBLANES, min(tr, t) + 1, SUBLANES) if t % d == 0)
    nin = len(rows) + len(consts)
    no, na = len(row_outs), len(acc_outs)

    def body(*refs):
        vals = fn(*[r[...] for r in refs[:nin]])
        if not isinstance(vals, (tuple, list)):
            vals = (vals,)
        for k in range(no):
            refs[nin + k][...] = vals[k].astype(refs[nin + k].dtype)
        first = pl.program_id(0) == 0
        for k in range(na):
            ref, val = refs[nin + no + k], vals[no + k]

            @pl.when(first)
            def _(ref=ref, val=val):
                ref[...] = val

            @pl.when(jnp.logical_not(first))
            def _(ref=ref, val=val):
                ref[...] += val

    in_specs = [pl.BlockSpec((tr, w), lambda i, cb=cb: (i, cb)) for (_, w, cb) in rows]
    in_specs += [pl.BlockSpec(c.shape, lambda i, nd=c.ndim: (0,) * nd) for c in consts]
    out_specs = [pl.BlockSpec((tr, w), lambda i: (i, 0)) for (w, _) in row_outs]
    out_specs += [pl.BlockSpec(s, lambda i, nd=len(s): (0,) * nd) for (s, _) in acc_outs]
    out_shape = [SDS((t, w), dt) for (w, dt) in row_outs] + [SDS(s, dt) for (s, dt) in acc_outs]
    res = pl.pallas_call(
        body, grid=(t // tr,), in_specs=in_specs, out_specs=out_specs, out_shape=out_shape,
        name=name, compiler_params=_cparams(("arbitrary",)))(*[r[0] for r in rows], *consts)
    return res


def _rstd(x):
    return lax.rsqrt(jnp.mean(x * x, axis=-1, keepdims=True) + RMS_EPS)


def _norm_fwd(x, g):
    return x * _rstd(x) * g


def _norm_bwd(u, dy, g):
    r = _rstd(u)
    n = u * r
    dn = dy * g
    du = r * (dn - n * jnp.mean(dn * n, axis=-1, keepdims=True))
    return du, jnp.sum(dy * n, axis=0, keepdims=True)


def _gelu(x):
    c = 0.7978845608028654
    return 0.5 * x * (1.0 + jnp.tanh(c * (x + 0.044715 * x * x * x)))


def _gelu_grad(x):
    c = 0.7978845608028654
    th = jnp.tanh(c * (x + 0.044715 * x * x * x))
    return 0.5 * (1.0 + th) + 0.5 * x * (1.0 - th * th) * c * (1.0 + 3.0 * 0.044715 * x * x)


def _mask_heads(x):
    lane = lax.broadcasted_iota(jnp.int32, x.shape, 1)
    return [jnp.where((lane >= h * HEAD_DIM) & (lane < (h + 1) * HEAD_DIM), x, jnp.zeros_like(x))
            for h in range(LANES // HEAD_DIM)]


def _chunk_valid(start):
    qi = lax.broadcasted_iota(jnp.int32, (QB_A, KW_A), 0)
    kj = lax.broadcasted_iota(jnp.int32, (QB_A, KW_A), 1)
    qc = qi // CHUNK
    kc = kj // CHUNK
    return (kc >= qc) & (kc <= qc + N_LEFT) & (kj + start >= PAD_A)


def _chunk_probs(q, k, bias, valid):
    s = lax.dot_general(q, k, (NT, ((), ())), preferred_element_type=f32) * (HEAD_DIM ** -0.5) + bias
    s = jnp.where(valid, s, -1e30)
    p = jnp.exp(s - jnp.max(s, axis=-1, keepdims=True))
    return p / jnp.sum(p, axis=-1, keepdims=True)


def _chunk_attn_fwd(proj, kpad, vpad, bias, gather):
    t = proj.shape[0]
    tp = kpad.shape[0]
    step = QSUB_A * QB_A

    def body(q_ref, k_ref, v_ref, b_ref, o_ref):
        for sb in range(QSUB_A):
            start = pl.multiple_of((pl.program_id(1) * QSUB_A + sb) * QB_A, QB_A)
            rows = pl.ds(sb * QB_A, QB_A)
            valid = _chunk_valid(start)
            kw = k_ref[pl.ds(start, KW_A), :]
            qm = _mask_heads(q_ref[rows, :])
            vm = _mask_heads(v_ref[pl.ds(start, KW_A), :])
            o = None
            for h in range(len(qm)):
                p = _chunk_probs(qm[h], kw, b_ref[h], valid)
                d = jnp.dot(p.astype(bf16), vm[h], preferred_element_type=f32)
                o = d if o is None else o + d
            o_ref[rows, :] = o.astype(bf16)

    kv_spec = pl.BlockSpec((tp, LANES), lambda hp, qb: (0, hp))
    outs, new = _call(
        body, [proj, kpad, vpad, bias], grid=(A_W // LANES, t // step), name="chunk_attn_fwd",
        in_specs=[pl.BlockSpec((step, LANES), lambda hp, qb: (qb, hp)), kv_spec, kv_spec,
                  pl.BlockSpec((2, QB_A, KW_A), lambda hp, qb: (hp, 0, 0))],
        out_specs=[pl.BlockSpec((step, LANES), lambda hp, qb: (qb, hp))],
        out_shape=[SDS((t, A_W), bf16)], sem=("parallel", "arbitrary"), gather=gather)
    return outs[0], new


def _chunk_attn_bwd(proj, kpad, vpad, bias, dout):
    t = proj.shape[0]
    tp = kpad.shape[0]
    step = QSUB_A * QB_A

    def body(q_ref, k_ref, v_ref, b_ref, do_ref, dq_ref, dk_ref, dv_ref, db_ref):
        qb = pl.program_id(1)

        @pl.when(qb == 0)
        def _():
            dk_ref[...] = jnp.zeros_like(dk_ref)
            dv_ref[...] = jnp.zeros_like(dv_ref)
            db_ref[...] = jnp.zeros_like(db_ref)

        for sb in range(QSUB_A):
            start = pl.multiple_of((qb * QSUB_A + sb) * QB_A, QB_A)
            rows = pl.ds(sb * QB_A, QB_A)
            win = pl.ds(start, KW_A)
            valid = _chunk_valid(start)
            kw = k_ref[win, :]
            vw = v_ref[win, :]
            qm = _mask_heads(q_ref[rows, :])
            dom = _mask_heads(do_ref[rows, :])
            km = _mask_heads(kw)
            dq = dk = dv = None
            for h in range(len(qm)):
                p = _chunk_probs(qm[h], kw, b_ref[h], valid)
                dp = lax.dot_general(dom[h], vw, (NT, ((), ())), preferred_element_type=f32)
                ds = p * (dp - jnp.sum(dp * p, axis=-1, keepdims=True))
                db_ref[h] += ds
                dsb = (ds * (HEAD_DIM ** -0.5)).astype(bf16)
                terms = (jnp.dot(dsb, km[h], preferred_element_type=f32),
                         lax.dot_general(dsb, qm[h], (TN, ((), ())), preferred_element_type=f32),
                         lax.dot_general(p.astype(bf16), dom[h], (TN, ((), ())), preferred_element_type=f32))
                dq, dk, dv = terms if dq is None else (dq + terms[0], dk + terms[1], dv + terms[2])
            dq_ref[rows, :] = dq.astype(bf16)
            dk_ref[win, :] += dk
            dv_ref[win, :] += dv

    kv_spec = pl.BlockSpec((tp, LANES), lambda hp, qb: (0, hp))
    q_spec = pl.BlockSpec((step, LANES), lambda hp, qb: (qb, hp))
    b_spec = pl.BlockSpec((2, QB_A, KW_A), lambda hp, qb: (hp, 0, 0))
    return pl.pallas_call(
        body, grid=(A_W // LANES, t // step), name="chunk_attn_bwd",
        in_specs=[q_spec, kv_spec, kv_spec, b_spec, q_spec],
        out_specs=[q_spec, kv_spec, kv_spec, b_spec],
        out_shape=[SDS((t, A_W), bf16), SDS((tp, A_W), f32), SDS((tp, A_W), f32),
                   SDS((2 * A_W // LANES, QB_A, KW_A), f32)],
        compiler_params=_cparams(("parallel", "arbitrary")))(proj, kpad, vpad, bias, dout)


def _bias_ext(table):
    flat = PAD_A + QB_A - 1 - REL_CLIP
    top = jnp.broadcast_to(table[:, 2 * REL_CLIP:], (table.shape[0], flat))
    lo = 2 * REL_CLIP - (EXT_A - 1 - flat)
    return jnp.concatenate([top, jnp.flip(table[:, lo:], axis=1)], axis=1)


def _bias_window(table):
    nh = table.shape[0]
    e = jnp.broadcast_to(_bias_ext(table)[:, None, :], (nh, QB_A, EXT_A)).reshape(nh, QB_A * EXT_A)
    m = e[:, :QB_A * (EXT_A - 1)].reshape(nh, QB_A, EXT_A - 1)
    return m[:, :, QB_A - 1:]


def _bias_window_grad(dbias):
    nh = dbias.shape[0]
    m = jnp.pad(dbias, ((0, 0), (0, 0), (QB_A - 1, 0))).reshape(nh, QB_A * (EXT_A - 1))
    dext = jnp.sum(jnp.pad(m, ((0, 0), (0, QB_A))).reshape(nh, QB_A, EXT_A), axis=1)
    flat = PAD_A + QB_A - 1 - REL_CLIP
    lo = 2 * REL_CLIP - (EXT_A - 1 - flat)
    tail = jnp.flip(dext[:, flat:], axis=1)
    tail = tail.at[:, -1].add(jnp.sum(dext[:, :flat], axis=1))
    return jnp.pad(tail, ((0, 0), (lo, 0)))


def _tri_suffix(x, tri):
    hi = x.astype(bf16)
    lo = (x - hi.astype(f32)).astype(bf16)
    return jnp.dot(hi, tri, preferred_element_type=f32) + jnp.dot(lo, tri, preferred_element_type=f32)


def _sb_block(q, k, run, tri, causal):
    z = lax.dot_general(q, k, (NT, ((), ())), preferred_element_type=f32) * (HEAD_DIM ** -0.5)
    e = jnp.exp(-jnp.abs(z))
    l1p = jnp.log(1.0 + e)
    lb = jnp.minimum(z, 0.0) - l1p
    lmb = lb - z
    if causal is not None:
        lmb = jnp.where(causal, lmb, 0.0)
    cs = _tri_suffix(lmb, tri)
    w = jnp.exp(lb + (run + cs - lmb))
    if causal is not None:
        w = jnp.where(causal, w, 0.0)
    return z, e, w, run + cs[:, 0:1]


def _sb_tri():
    r = lax.broadcasted_iota(jnp.int32, (SB_BLK, SB_BLK), 0)
    c = lax.broadcasted_iota(jnp.int32, (SB_BLK, SB_BLK), 1)
    return (r >= c).astype(bf16), c < r


def _sb_live(runs):
    m = runs[0]
    for r in runs[1:]:
        m = jnp.maximum(m, r)
    return jnp.max(m) > SB_DEAD


def _sb_fwd(proj, gather):
    t = proj.shape[0]
    cb = A_W // LANES
    nh = LANES // HEAD_DIM

    step_rows = QSUB_B * SB_BLK

    def body(q_ref, k_ref, v_ref, o_ref, of_ref):
        tri, diag = _sb_tri()
        for sb in range(QSUB_B):
            _sb_fwd_block(pl.program_id(1) * QSUB_B + sb, pl.ds(sb * SB_BLK, SB_BLK), tri, diag,
                          q_ref, k_ref, v_ref, o_ref, of_ref)

    def _sb_fwd_block(qb, qrows, tri, diag, q_ref, k_ref, v_ref, o_ref, of_ref):
        qm = _mask_heads(q_ref[qrows, :])

        def pair(kb, carry, causal):
            rows = pl.ds(pl.multiple_of(kb * SB_BLK, SB_BLK), SB_BLK)
            k = k_ref[rows, :]
            vm = _mask_heads(v_ref[rows, :])
            runs, acc = [], carry[nh]
            for h in range(nh):
                _, _, w, run = _sb_block(qm[h], k, carry[h], tri, causal)
                acc = acc + jnp.dot(w.astype(bf16), vm[h], preferred_element_type=f32)
                runs.append(run)
            return (*runs, acc)

        zero = jnp.zeros((SB_BLK, 1), f32)
        carry = pair(qb, (zero,) * nh + (jnp.zeros((SB_BLK, LANES), f32),), diag)

        def cond(st):
            return (st[0] < qb) & _sb_live(st[1][:nh])

        def step(st):
            return st[0] + 1, pair(qb - 1 - st[0], st[1], None)

        _, carry = lax.while_loop(cond, step, (jnp.int32(0), carry))
        o_ref[qrows, :] = carry[nh].astype(bf16)
        of_ref[qrows, :] = carry[nh]

    ospec = pl.BlockSpec((step_rows, LANES), lambda hp, qb: (qb, hp))
    return _call(
        body, [proj, proj, proj], grid=(cb, t // step_rows), name="sb_attn_fwd",
        in_specs=[pl.BlockSpec((step_rows, LANES), lambda hp, qb: (qb, 3 * cb + hp)),
                  pl.BlockSpec((t, LANES), lambda hp, qb: (0, 4 * cb + hp)),
                  pl.BlockSpec((t, LANES), lambda hp, qb: (0, 5 * cb + hp))],
        out_specs=[ospec, ospec], out_shape=[SDS((t, A_W), bf16), SDS((t, A_W), f32)],
        sem=("parallel", "arbitrary"), gather=gather)


def _sb_bwd(proj, out_b, dout):
    t = proj.shape[0]
    cb = A_W // LANES
    nh = LANES // HEAD_DIM

    step_rows = QSUB_B * SB_BLK

    def body(q_ref, k_ref, v_ref, o_ref, do_ref, dq_ref, dk_ref, dv_ref):
        tri, diag = _sb_tri()

        @pl.when(pl.program_id(1) == 0)
        def _():
            dk_ref[...] = jnp.zeros_like(dk_ref)
            dv_ref[...] = jnp.zeros_like(dv_ref)

        for sb in range(QSUB_B):
            _sb_bwd_block(pl.program_id(1) * QSUB_B + sb, pl.ds(sb * SB_BLK, SB_BLK), tri, diag,
                          q_ref, k_ref, v_ref, o_ref, do_ref, dq_ref, dk_ref, dv_ref)

    def _sb_bwd_block(qb, qrows, tri, diag, q_ref, k_ref, v_ref, o_ref, do_ref, dq_ref, dk_ref, dv_ref):
        qm = _mask_heads(q_ref[qrows, :])
        do = do_ref[qrows, :]
        dom = _mask_heads(do)
        dsums = [jnp.sum(t_, axis=-1, keepdims=True) for t_ in _mask_heads(do.astype(f32) * o_ref[qrows, :])]

        def pair(kb, carry, causal):
            rows = pl.ds(pl.multiple_of(kb * SB_BLK, SB_BLK), SB_BLK)
            k = k_ref[rows, :]
            v = v_ref[rows, :]
            km = _mask_heads(k)
            new, dq, dk, dv = [], carry[2 * nh], None, None
            for h in range(nh):
                z, e, w, run = _sb_block(qm[h], k, carry[2 * h], tri, causal)
                inv = 1.0 / (1.0 + e)
                beta = jnp.where(z >= 0.0, inv, e * inv)
                wb = w.astype(bf16)
                g = lax.dot_general(dom[h], v, (NT, ((), ())), preferred_element_type=f32) * wb.astype(f32)
                sg = _tri_suffix(g, tri)
                dz = g * (1.0 - beta) - (dsums[h] - carry[2 * h + 1] - sg) * beta
                if causal is not None:
                    dz = jnp.where(causal, dz, 0.0)
                dzb = (dz * (HEAD_DIM ** -0.5)).astype(bf16)
                dq = dq + jnp.dot(dzb, km[h], preferred_element_type=f32)
                tk = lax.dot_general(dzb, qm[h], (TN, ((), ())), preferred_element_type=f32)
                tv = lax.dot_general(wb, dom[h], (TN, ((), ())), preferred_element_type=f32)
                dk, dv = (tk, tv) if dk is None else (dk + tk, dv + tv)
                new += [run, carry[2 * h + 1] + sg[:, 0:1]]
            dk_ref[rows, :] += dk
            dv_ref[rows, :] += dv
            return (*new, dq)

        zero = jnp.zeros((SB_BLK, 1), f32)
        carry = pair(qb, (zero,) * (2 * nh) + (jnp.zeros((SB_BLK, LANES), f32),), diag)

        def cond(st):
            return (st[0] < qb) & _sb_live(st[1][0:2 * nh:2])

        def step(st):
            return st[0] + 1, pair(qb - 1 - st[0], st[1], None)

        _, carry = lax.while_loop(cond, step, (jnp.int32(0), carry))
        dq_ref[qrows, :] = carry[2 * nh].astype(bf16)

    kv_in = lambda seg: pl.BlockSpec((t, LANES), lambda hp, qb: (0, seg * cb + hp))
    q_spec = pl.BlockSpec((step_rows, LANES), lambda hp, qb: (qb, hp))
    kv_out = pl.BlockSpec((t, LANES), lambda hp, qb: (0, hp))
    return pl.pallas_call(
        body, grid=(cb, t // step_rows), name="sb_attn_bwd",
        in_specs=[pl.BlockSpec((step_rows, LANES), lambda hp, qb: (qb, 3 * cb + hp)), kv_in(4), kv_in(5),
                  q_spec, pl.BlockSpec((step_rows, LANES), lambda hp, qb: (qb, cb + hp))],
        out_specs=[q_spec, kv_out, kv_out],
        out_shape=[SDS((t, A_W), bf16), SDS((t, A_W), f32), SDS((t, A_W), f32)],
        compiler_params=_cparams(("parallel", "arbitrary")))(proj, proj, proj, out_b, dout)


def _halo_specs(tr, w, col, nblk):
    per = tr // SUBLANES
    cur = pl.BlockSpec((tr, w), lambda i: (i, col))
    prev = pl.BlockSpec((SUBLANES, w), lambda i: (jnp.maximum(i * per - 1, 0), col))
    nxt = pl.BlockSpec((SUBLANES, w), lambda i: (jnp.minimum((i + 1) * per, nblk * per - 1), col))
    return cur, prev, nxt


def _taps_before(cur, prev8, first):
    prev8 = jnp.where(first, 0.0, prev8)
    ext = jnp.concatenate([prev8, cur], axis=0)
    return [pltpu.roll(ext, s, 0)[SUBLANES:] for s in (3, 2, 1)]


def _taps_after(cur, next8, last):
    n = cur.shape[0]
    next8 = jnp.where(last, 0.0, next8)
    ext = jnp.concatenate([cur, next8], axis=0)
    return [pltpu.roll(ext, n + SUBLANES - s, 0)[:n] for s in (1, 2, 3)]


def _block_diag(x, w_ref, dims):
    outs = [lax.dot_general(x[:, n * LRU_BW:(n + 1) * LRU_BW], w_ref[n], (dims, ((), ())),
                            preferred_element_type=f32) for n in range(LRU_BLOCKS)]
    return jnp.concatenate(outs, axis=1)


def _lru_gates(xc, wa_ref, wi_ref, ba, bi, lam):
    xb = xc.astype(bf16)
    r = jax.nn.sigmoid(_block_diag(xb, wa_ref, NN) + ba)
    ig = jax.nn.sigmoid(_block_diag(xb, wi_ref, NN) + bi)
    sp = jnp.maximum(-lam, 0.0) + jnp.log(1.0 + jnp.exp(-jnp.abs(lam)))
    log_a = -LRU_C * r * sp
    a = jnp.exp(log_a)
    x2 = 2.0 * log_a
    one_minus = jnp.where(x2 > -1e-2, -x2 * (1.0 + x2 * (0.5 + x2 * (1.0 / 6.0))), 1.0 - a * a)
    mult = jnp.sqrt(one_minus)
    return xb, r, ig, sp, a, mult


def _rg_gates_fwd(proj, conv_w, conv_b, wa, wi, ba, bi, lam, tr=512):
    t = proj.shape[0]
    w = D_MODEL
    tr = min(tr, t)
    nblk = t // tr
    cur, prev, _ = _halo_specs(tr, w, 1, nblk)

    def body(x_ref, xp_ref, cw_ref, cb_ref, wa_ref, wi_ref, ba_ref, bi_ref, lam_ref, xc_ref, a_ref, u_ref):
        x = x_ref[...]
        taps = _taps_before(x, xp_ref[...], pl.program_id(0) == 0) + [x]
        xc = cb_ref[...]
        for k in range(4):
            xc = xc + cw_ref[k:k + 1, :] * taps[k]
        _, _, ig, _, a, mult = _lru_gates(xc, wa_ref, wi_ref, ba_ref[...], bi_ref[...], lam_ref[...])
        xc_ref[...] = xc
        a_ref[...] = a
        u_ref[...] = mult * (ig * xc)

    full = lambda a_: pl.BlockSpec(a_.shape, lambda i, nd=a_.ndim: (0,) * nd)
    ospec = pl.BlockSpec((tr, w), lambda i: (i, 0))
    return pl.pallas_call(
        body, grid=(nblk,), name="rg_gates_fwd",
        in_specs=[cur, prev] + [full(a_) for a_ in (conv_w, conv_b, wa, wi, ba, bi, lam)],
        out_specs=[ospec] * 3, out_shape=[SDS((t, w), f32)] * 3,
        compiler_params=_cparams(("parallel",)))(proj, proj, conv_w, conv_b, wa, wi, ba, bi, lam)


def _lru_scan(name, a, b, reverse, tt=512):
    t, w = a.shape
    tt = min(tt, t)
    nt = t // tt
    ng = tt // SUBLANES

    def body(a_ref, b_ref, h_ref, carry_ref):
        @pl.when(pl.program_id(0) == 0)
        def _():
            carry_ref[...] = jnp.zeros_like(carry_ref)

        row = lax.broadcasted_iota(jnp.int32, (SUBLANES, w), 0)

        def group(gi, carry):
            g = (ng - 1 - gi) if reverse else gi
            rows = pl.ds(pl.multiple_of(g * SUBLANES, SUBLANES), SUBLANES)
            av = a_ref[rows, :]
            bv = b_ref[rows, :]
            for s in (1, 2, 4):
                sh = (SUBLANES - s) if reverse else s
                ok = (row < SUBLANES - s) if reverse else (row >= s)
                a_s = pltpu.roll(av, sh, 0)
                b_s = pltpu.roll(bv, sh, 0)
                bv = jnp.where(ok, av * b_s + bv, bv)
                av = jnp.where(ok, av * a_s, av)
            h = av * carry + bv
            h_ref[rows, :] = h
            edge = h[0:1, :] if reverse else h[SUBLANES - 1:SUBLANES, :]
            return jnp.broadcast_to(edge, (SUBLANES, w))

        carry_ref[...] = lax.fori_loop(0, ng, group, carry_ref[...], unroll=4)

    tmap = (lambda i: (nt - 1 - i, 0)) if reverse else (lambda i: (i, 0))
    spec = pl.BlockSpec((tt, w), tmap)
    return pl.pallas_call(
        body, grid=(nt,), name=name, in_specs=[spec, spec], out_specs=spec,
        out_shape=SDS((t, w), f32), scratch_shapes=[pltpu.VMEM((SUBLANES, w), f32)],
        compiler_params=_cparams(("arbitrary",)))(a, b)


def _rg_gates_bwd(dhs, c, hs, xc, wa, wi, ba, bi, lam, tr=256):
    t, w = xc.shape
    tr = min(tr, t)
    nblk = t // tr
    cur, prev, nxt = _halo_specs(tr, w, 0, nblk)

    def body(dhs_ref, c_ref, cn_ref, hs_ref, hp_ref, xc_ref, wa_ref, wi_ref, ba_ref, bi_ref, lam_ref,
             dxc_ref, dwa_ref, dwi_ref, dba_ref, dbi_ref, dlam_ref):
        i = pl.program_id(0)
        c_next = _taps_after(c_ref[...], cn_ref[...], i == nblk - 1)[0]
        h_prev = _taps_before(hs_ref[...], hp_ref[...], i == 0)[2]
        xc = xc_ref[...]
        lam = lam_ref[...]
        xb, r, ig, sp, a, mult = _lru_gates(xc, wa_ref, wi_ref, ba_ref[...], bi_ref[...], lam)
        dh = dhs_ref[...] + c_next
        dlog_a = dh * h_prev * a - (dh * ig * xc) * (a * a / mult)
        dpre_a = (dlog_a * (-LRU_C * sp) * r * (1.0 - r)).astype(bf16)
        dpre_i = (dh * mult * xc * ig * (1.0 - ig)).astype(bf16)
        dxc_ref[...] = (dh * mult * ig + _block_diag(dpre_a, wa_ref, NT) + _block_diag(dpre_i, wi_ref, NT))
        dsig = 1.0 / (1.0 + jnp.exp(lam))
        sums = [jnp.sum(dpre_a.astype(f32), axis=0, keepdims=True),
                jnp.sum(dpre_i.astype(f32), axis=0, keepdims=True),
                jnp.sum(dlog_a * (-LRU_C * r), axis=0, keepdims=True) * (-dsig)]

        @pl.when(i == 0)
        def _():
            dwa_ref[...] = jnp.zeros_like(dwa_ref)
            dwi_ref[...] = jnp.zeros_like(dwi_ref)
            dba_ref[...] = jnp.zeros_like(dba_ref)
            dbi_ref[...] = jnp.zeros_like(dbi_ref)
            dlam_ref[...] = jnp.zeros_like(dlam_ref)

        for n in range(LRU_BLOCKS):
            sl = slice(n * LRU_BW, (n + 1) * LRU_BW)
            dwa_ref[n] += lax.dot_general(xb[:, sl], dpre_a[:, sl], (TN, ((), ())), preferred_element_type=f32)
            dwi_ref[n] += lax.dot_general(xb[:, sl], dpre_i[:, sl], (TN, ((), ())), preferred_element_type=f32)
        dba_ref[...] += sums[0]
        dbi_ref[...] += sums[1]
        dlam_ref[...] += sums[2]

    full = lambda a_: pl.BlockSpec(a_.shape, lambda i, nd=a_.ndim: (0,) * nd)
    vec = pl.BlockSpec((1, w), lambda i: (0, 0))
    mat = pl.BlockSpec((LRU_BLOCKS, LRU_BW, LRU_BW), lambda i: (0, 0, 0))
    return pl.pallas_call(
        body, grid=(nblk,), name="rg_gates_bwd",
        in_specs=[cur, cur, nxt, cur, prev, cur] + [full(a_) for a_ in (wa, wi, ba, bi, lam)],
        out_specs=[cur, mat, mat, vec, vec, vec],
        out_shape=[SDS((t, w), f32), SDS((LRU_BLOCKS, LRU_BW, LRU_BW), f32), SDS((LRU_BLOCKS, LRU_BW, LRU_BW), f32),
                   SDS((1, w), f32), SDS((1, w), f32), SDS((1, w), f32)],
        compiler_params=_cparams(("arbitrary",)))(dhs, c, c, hs, hs, xc, wa, wi, ba, bi, lam)


def _rg_conv_bwd(dxc, proj, conv_w, tr=512):
    t, w = dxc.shape
    tr = min(tr, t)
    nblk = t // tr
    cur, _, nxt = _halo_specs(tr, w, 0, nblk)
    xcur, xprev, _ = _halo_specs(tr, w, 1, nblk)

    def body(d_ref, dn_ref, x_ref, xp_ref, cw_ref, dx_ref, dcw_ref, dcb_ref):
        i = pl.program_id(0)
        d = d_ref[...]
        x = x_ref[...]
        after = _taps_after(d, dn_ref[...], i == nblk - 1)
        before = _taps_before(x, xp_ref[...], i == 0) + [x]
        dx = cw_ref[3:4, :] * d
        for s in (1, 2, 3):
            dx = dx + cw_ref[3 - s:4 - s, :] * after[s - 1]
        dx_ref[...] = dx.astype(bf16)
        dcw = jnp.concatenate([jnp.sum(d * before[k], axis=0, keepdims=True) for k in range(4)], axis=0)
        dcb = jnp.sum(d, axis=0, keepdims=True)

        @pl.when(i == 0)
        def _():
            dcw_ref[...] = dcw
            dcb_ref[...] = dcb

        @pl.when(i > 0)
        def _():
            dcw_ref[...] += dcw
            dcb_ref[...] += dcb

    return pl.pallas_call(
        body, grid=(nblk,), name="rg_conv_bwd",
        in_specs=[cur, nxt, xcur, xprev, pl.BlockSpec((4, w), lambda i: (0, 0))],
        out_specs=[cur, pl.BlockSpec((4, w), lambda i: (0, 0)), pl.BlockSpec((1, w), lambda i: (0, 0))],
        out_shape=[SDS((t, w), bf16), SDS((4, w), f32), SDS((1, w), f32)],
        compiler_params=_cparams(("arbitrary",)))(dxc, dxc, proj, proj, conv_w)


def _attn_fwd(h, wts, j, plan):
    proj = _mm_cols("attn_in", h, wts["attn_w_in"], j, bf16)
    kpad = jnp.pad(proj[:, A_W:2 * A_W], ((PAD_A, 0), (0, 0)))
    vpad = jnp.pad(proj[:, 2 * A_W:3 * A_W], ((PAD_A, 0), (0, 0)))
    bias = _bias_window(wts["attn_rel_bias"][j])
    plan = plan if j == 0 else None
    out_a = _carried(plan, "chunk_attn_fwd", wts, _chunk_attn_fwd, proj, kpad, vpad, bias)
    out_b, out_b32 = _carried(plan, "sb_attn_fwd", wts, _sb_fwd, proj)
    m = _mm_rows("attn_out", [out_a, out_b], wts["attn_w_out"], j, f32)
    return m, (proj, kpad, vpad, bias, out_a, out_b, out_b32)


def _attn_bwd(dm, h, saved, wts, j, grads):
    proj, kpad, vpad, bias, out_a, out_b, out_b32 = saved
    dout = _mm_rows_t("attn_out_t", dm, wts["attn_w_out"], j, bf16)
    grads["attn_w_out"] = _mm_wgrad("attn_out_wgrad_a", out_a, dm, grads["attn_w_out"], j, 0)
    grads["attn_w_out"] = _mm_wgrad("attn_out_wgrad_b", out_b, dm, grads["attn_w_out"], j, 1)
    dqa, dka, dva, dbias = _chunk_attn_bwd(proj, kpad, vpad, bias, dout)
    dqs, dks, dvs = _sb_bwd(proj, out_b32, dout)
    grads["attn_rel_bias"][j] = _bias_window_grad(dbias)
    dproj = jnp.concatenate([dqa, dka[PAD_A:].astype(bf16), dva[PAD_A:].astype(bf16),
                             dqs, dks.astype(bf16), dvs.astype(bf16)], axis=1)
    grads["attn_w_in"] = _mm_wgrad_cols("attn_in_wgrad", h, dproj, grads["attn_w_in"], j)
    return _mm_cols_t("attn_in_t", dproj, wts["attn_w_in"], j, f32)


def _rg_fwd(h, wts, j, plan):
    proj =_mm_cols("rg_in", h, wts["rg_w_in"], j, f32)
    small = [wts[k][j] for k in ("rg_conv_w", "rg_conv_b", "rg_w_a", "rg_w_i", "rg_b_a", "rg_b_i", "rg_lambda")]
    xc, a, u = _rg_gates_fwd(proj, *small)
    hs = _lru_scan("lru_scan_fwd", a, u, False)
    yp = _rows("rg_gate_out", lambda hv, gv: hv * _gelu(gv), [hs, (proj, D_MODEL, 0)], [], [(D_MODEL, bf16)])[0]
    m = _mm_rows("rg_out", [yp], wts["rg_w_out"], j, f32)
    return m, (proj, xc, a, hs, yp)


def _rg_bwd(dm, h, saved, wts, j, grads):
    proj, xc, a, hs, yp = saved
    dyp = _mm_rows_t("rg_out_t", dm, wts["rg_w_out"], j, f32)
    grads["rg_w_out"] = _mm_wgrad("rg_out_wgrad", yp, dm, grads["rg_w_out"], j)

    def gate_bwd(dy, hv, gv, av):
        dhs = dy * _gelu(gv)
        return dhs, av * dhs, dy * hv * _gelu_grad(gv)

    dhs, ab, dgate = _rows("rg_gate_out_bwd", gate_bwd, [dyp, hs, (proj, D_MODEL, 0), a], [],
                           [(D_MODEL, f32), (D_MODEL, f32), (D_MODEL, bf16)])
    c = _lru_scan("lru_scan_bwd", a, ab, True)
    wa, wi, ba, bi, lam = [wts[k][j] for k in ("rg_w_a", "rg_w_i", "rg_b_a", "rg_b_i", "rg_lambda")]
    dxc, dwa, dwi, dba, dbi, dlam = _rg_gates_bwd(dhs, c, hs, xc, wa, wi, ba, bi, lam)
    dxr, dcw, dcb = _rg_conv_bwd(dxc, proj, wts["rg_conv_w"][j])
    for k, v in (("rg_w_a", dwa), ("rg_w_i", dwi), ("rg_b_a", dba), ("rg_b_i", dbi), ("rg_lambda", dlam),
                 ("rg_conv_w", dcw), ("rg_conv_b", dcb)):
        grads[k][j] = v
    dproj = jnp.concatenate([dgate, dxr], axis=1)
    grads["rg_w_in"] = _mm_wgrad_cols("rg_in_wgrad", h, dproj, grads["rg_w_in"], j)
    return _mm_cols_t("rg_in_t", dproj, wts["rg_w_in"], j, f32)


def _local_step(x, target, wts, plan=None):
    t = x.shape[0]
    d = D_MODEL
    gains = {k: wts[k] for k in ("norm_mix_pre", "norm_mix_post", "norm_ffn_pre", "norm_ffn_post")}
    gain = lambda k, l: gains[k][l:l + 1]

    saved = []
    h = _rows("norm_in", _norm_fwd, [x], [gain("norm_mix_pre", 0)], [(d, bf16)])[0]
    loss_cols = None
    for l in range(DEPTH):
        j = l // 2
        m, mix_saved = (_attn_fwd if l % 2 == 0 else _rg_fwd)(h, wts, j, plan)

        def resid_next(xv, mv, g_post, g_next):
            x1 = xv + _norm_fwd(mv, g_post)
            return x1, _norm_fwd(x1, g_next)

        x1, h2 = _rows("resid_mix", resid_next, [x, m], [gain("norm_mix_post", l), gain("norm_ffn_pre", l)],
                       [(d, f32), (d, bf16)])
        g, u, hid = _carried(plan if l == 0 else None, "ffn_up", wts, _ffn_up, h2, wts["ffn_w_gate"],
                             wts["ffn_w_up"], l)
        f = _ffn_down(hid, wts["ffn_w_down"], l)
        saved.append((x, h, m, mix_saved, x1, h2, g, u, hid, f))
        if l + 1 < DEPTH:
            x, h = _rows("resid_ffn", resid_next, [x1, f], [gain("norm_ffn_post", l), gain("norm_mix_pre", l + 1)],
                         [(d, f32), (d, bf16)])
        else:
            def resid_loss(xv, fv, tv, g_post):
                err = xv + _norm_fwd(fv, g_post) - tv
                return err * (1.0 / d), jnp.sum(err * err, axis=0, keepdims=True)

            dx, loss_cols = _rows("resid_loss", resid_loss, [x1, f, target], [gain("norm_ffn_post", l)],
                                  [(d, f32)], [((1, d), f32)])
    loss = 0.5 * jnp.sum(loss_cols) / d

    grads = {k: {} for k in SMALL_GRADS}
    for k in BIG_GRADS:
        shp = wts[k].shape
        grads[k] = _Fresh((shp[0],) + shp[2:] if shp[1] == 1 else shp)

    def norm_bwd_cast(uv, dyv, gv):
        du, dg = _norm_bwd(uv, dyv, gv)
        return du, dg

    def norm_bwd_resid(uv, dhv, dxv, gv):
        du, dg = _norm_bwd(uv, dhv, gv)
        return dxv + du, dg

    for l in reversed(range(DEPTH)):
        j = l // 2
        x_in, h, m, mix_saved, x1, h2, g, u, hid, f = saved[l]
        df, grads["norm_ffn_post"][l] = _rows("norm_ffn_post_bwd", norm_bwd_cast, [f, dx], [gain("norm_ffn_post", l)],
                                              [(d, bf16)], [((1, d), f32)])
        dg, du = _ffn_down_bwd(df, wts["ffn_w_down"], l, g, u)
        grads["ffn_w_down"] = _ffn_wgrad_down(hid, df, grads["ffn_w_down"], l)
        dh2 = _ffn_up_bwd(dg, du, wts["ffn_w_gate"], wts["ffn_w_up"], l)
        grads["ffn_w_gate"], grads["ffn_w_up"] = _ffn_wgrad_up(h2, dg, du, grads["ffn_w_gate"], grads["ffn_w_up"], l)
        dx1, grads["norm_ffn_pre"][l] = _rows("norm_ffn_pre_bwd", norm_bwd_resid, [x1, dh2, dx],
                                              [gain("norm_ffn_pre", l)], [(d, f32)], [((1, d), f32)])
        dm, grads["norm_mix_post"][l] = _rows("norm_mix_post_bwd", norm_bwd_cast, [m, dx1], [gain("norm_mix_post", l)],
                                              [(d, bf16)], [((1, d), f32)])
        dh = (_attn_bwd if l % 2 == 0 else _rg_bwd)(dm, h, mix_saved, wts, j, grads)
        dx, grads["norm_mix_pre"][l] = _rows("norm_mix_pre_bwd", norm_bwd_resid, [x_in, dh, dx1],
                                             [gain("norm_mix_pre", l)], [(d, f32)], [((1, d), f32)])
    return loss, dx, grads


ANY = pl.BlockSpec(memory_space=pl.ANY)
PACK_COLS = 1024
SMALL_ROWS = 288


def _mesh_pos():
    x, y, c = lax.axis_index("x"), lax.axis_index("y"), lax.axis_index("c")
    return x, y, c, [(1 - x, y), (x, 1 - y), (1 - x, 1 - y)]


def _run_copies(copies):
    for cp in copies:
        cp.start()
    for cp in copies:
        cp.wait()


GATHER_SEMS = 7


def _gather_copies(items, ins, outs, send, recv):
    x, y, c, chips = _mesh_pos()
    q = 2 * x + y
    sibling = (x, y, 1 - c)

    def copy(k, src, dst, to):
        return pltpu.make_async_remote_copy(src_ref=src, dst_ref=dst, send_sem=send.at[k], recv_sem=recv.at[k],
                                            device_id=to, device_id_type=MESH)

    own, sent, passed = [], [], []
    for i, (t, l0, nl) in enumerate(items):
        lay = pl.ds(l0, nl)
        half = ins[t].shape[1] // 2
        rows = pl.ds(pl.multiple_of(c * half, half), half)
        own.append(copy(GATHER_SEMS * i, ins[t].at[lay], outs[t].at[lay, q], sibling))
        for j, (px, py) in enumerate(chips):
            sent.append(copy(GATHER_SEMS * i + 1 + j, ins[t].at[lay, rows], outs[t].at[lay, q, rows], (px, py, c)))
            landed = outs[t].at[lay, 2 * px + py, rows]
            passed.append(copy(GATHER_SEMS * i + 4 + j, landed, landed, sibling))
    return own, sent, passed


def _gather_start(items, ins, outs, send, recv):
    own, sent, _ = _gather_copies(items, ins, outs, send, recv)
    for cp in own + sent:
        cp.start()


def _gather_finish(items, ins, outs, send, recv):
    own, sent, passed = _gather_copies(items, ins, outs, send, recv)
    for arrived, forward in zip(sent, passed):
        arrived.wait_recv()
        forward.start()
    for cp in sent:
        cp.wait_send()
    for cp in own + passed:
        cp.wait()


def _gather_call(items, shards):
    n = len(shards)
    nsem = GATHER_SEMS * len(items)

    def body(*refs):
        ins, outs = refs[:n], refs[n:2 * n]
        _gather_start(items, ins, outs, *refs[2 * n:])
        _gather_finish(items, ins, outs, *refs[2 * n:])

    return pl.pallas_call(
        body, name="weight_all_gather", in_specs=[ANY] * n, out_specs=[ANY] * n,
        out_shape=[SDS((s.shape[0], N_CHIPS) + s.shape[1:], s.dtype) for s in shards],
        scratch_shapes=[pltpu.SemaphoreType.DMA((nsem,)), pltpu.SemaphoreType.DMA((nsem,))])(*shards)


def _call(body, operands, *, name, grid, in_specs, out_specs, out_shape, sem, scratch=(), gather=None):
    if gather is None:
        return pl.pallas_call(body, grid=grid, in_specs=in_specs, out_specs=out_specs, out_shape=out_shape,
                              scratch_shapes=list(scratch), name=name, compiler_params=_cparams(sem))(*operands), None
    items, shards, gathered = gather
    n_in, n_out, n_scr, ng = len(operands), len(out_shape), len(scratch), len(shards)
    nsem = GATHER_SEMS * len(items)

    def full(*refs):
        ins, sh = refs[:n_in], refs[n_in:n_in + ng]
        outs = refs[n_in + 2 * ng:n_in + 2 * ng + n_out]
        io = refs[n_in + 2 * ng + n_out:n_in + 3 * ng + n_out]
        scr = refs[n_in + 3 * ng + n_out:]
        ids = [pl.program_id(a) for a in range(len(grid))]
        first = functools.reduce(jnp.logical_and, [i == 0 for i in ids])
        last = functools.reduce(jnp.logical_and, [i == g - 1 for i, g in zip(ids, grid)])

        @pl.when(first)
        def _():
            _gather_start(items, sh, io, scr[n_scr], scr[n_scr + 1])

        body(*ins, *outs, *scr[:n_scr])

        @pl.when(last)
        def _():
            _gather_finish(items, sh, io, scr[n_scr], scr[n_scr + 1])

    res = pl.pallas_call(
        full, grid=grid, in_specs=list(in_specs) + [ANY] * (2 * ng), out_specs=list(out_specs) + [ANY] * ng,
        out_shape=list(out_shape) + [SDS(g.shape, g.dtype) for g in gathered],
        scratch_shapes=list(scratch) + [pltpu.SemaphoreType.DMA((nsem,)), pltpu.SemaphoreType.DMA((nsem,))],
        input_output_aliases={n_in + ng + t: n_out + t for t in range(ng)}, name=name,
        compiler_params=_cparams(("arbitrary",) * len(grid)))(*operands, *shards, *gathered)
    return res[:n_out], res[n_out:]


def _pair_exchange(gs):
    n = len(gs)

    def body(*refs):
        ins, outs = refs[:n], refs[n:2 * n]
        send, recv = refs[2 * n:]
        x, y, c, _ = _mesh_pos()
        copies = []
        for t in range(n):
            half = ins[t].shape[2] // 2
            src = ins[t].at[:, :, pl.ds(pl.multiple_of((1 - c) * half, SUBLANES), half)]
            copies.append(pltpu.make_async_remote_copy(
                src_ref=src, dst_ref=outs[t], send_sem=send.at[t], recv_sem=recv.at[t],
                device_id=(x, y, 1 - c), device_id_type=MESH))
        _run_copies(copies)

    return pl.pallas_call(
        body, name="grad_pair_exchange", in_specs=[ANY] * n, out_specs=[ANY] * n,
        out_shape=[SDS(g.shape[:2] + (g.shape[2] // 2, g.shape[3]), f32) for g in gs],
        scratch_shapes=[pltpu.SemaphoreType.DMA((n,)), pltpu.SemaphoreType.DMA((n,))])(*gs)


def _pair_sum(name, g, got, c):
    l, s, r, cols = g.shape

    def body(c_ref, a_ref, b_ref, o_ref):
        o_ref[...] = (a_ref[...] + b_ref[...]).astype(bf16)

    blk = (None, None, r // 2, cols)
    return pl.pallas_call(
        body, name=name, out_shape=SDS(got.shape, bf16),
        grid_spec=pltpu.PrefetchScalarGridSpec(
            num_scalar_prefetch=1, grid=(l, s),
            in_specs=[pl.BlockSpec(blk, lambda i, q, c_ref: (i, q, c_ref[0], 0)),
                      pl.BlockSpec(blk, lambda i, q, c_ref: (i, q, 0, 0))],
            out_specs=pl.BlockSpec(blk, lambda i, q, c_ref: (i, q, 0, 0))),
        compiler_params=_cparams(("parallel", "parallel")))(c, g, got)


def _chip_exchange(hs):
    n = len(hs)

    def body(*refs):
        ins, outs = refs[:n], refs[n:2 * n]
        send, recv = refs[2 * n:]
        x, y, c, chips = _mesh_pos()
        q = 2 * x + y
        copies = []
        for t in range(n):
            for j, (px, py) in enumerate(chips):
                copies.append(pltpu.make_async_remote_copy(
                    src_ref=ins[t].at[:, 2 * px + py], dst_ref=outs[t].at[:, q], send_sem=send.at[3 * t + j],
                    recv_sem=recv.at[3 * t + j], device_id=(px, py, c), device_id_type=MESH))
        _run_copies(copies)

    return pl.pallas_call(
        body, name="grad_chip_exchange", in_specs=[ANY] * n, out_specs=[ANY] * n,
        out_shape=[SDS(h.shape, h.dtype) for h in hs],
        scratch_shapes=[pltpu.SemaphoreType.DMA((3 * n,)), pltpu.SemaphoreType.DMA((3 * n,))])(*hs)


def _chip_sum(name, s, h, pos):
    l, _, r, cols = s.shape

    def body(pos_ref, s0, s1, s2, s3, own_ref, o_ref):
        vals = [jnp.where(pos_ref[0] == p, own_ref[...], ref[...]).astype(f32) for p, ref in enumerate((s0, s1, s2, s3))]
        o_ref[...] = ((vals[0] + vals[1]) + vals[2]) + vals[3]

    blk = (None, None, r, cols)
    slot = lambda p: pl.BlockSpec(blk, lambda i, pos_ref: (i, jnp.where(pos_ref[0] == p, (p + 1) % N_CHIPS, p), 0, 0))
    return pl.pallas_call(
        body, name=name, out_shape=SDS((l, 2 * r, cols), f32),
        grid_spec=pltpu.PrefetchScalarGridSpec(
            num_scalar_prefetch=1, grid=(l,),
            in_specs=[slot(p) for p in range(N_CHIPS)] + [pl.BlockSpec(blk, lambda i, pos_ref: (i, pos_ref[0], 0, 0))],
            out_specs=pl.BlockSpec((None, r, cols), lambda i, pos_ref: (i, pos_ref[1], 0))),
        compiler_params=_cparams(("parallel",)))(pos, s, s, s, s, h)


def _pair_gather(fulls):
    n = len(fulls)

    def body(*refs):
        ins, outs = refs[:n], refs[n:2 * n]
        send, recv = refs[2 * n:]
        x, y, c, _ = _mesh_pos()
        copies = []
        for t in range(n):
            half = outs[t].shape[1] // 2
            rows = outs[t].at[:, pl.ds(pl.multiple_of(c * half, SUBLANES), half)]
            copies.append(pltpu.make_async_remote_copy(
                src_ref=rows, dst_ref=rows, send_sem=send.at[t], recv_sem=recv.at[t],
                device_id=(x, y, 1 - c), device_id_type=MESH))
        _run_copies(copies)

    return pl.pallas_call(
        body, name="grad_pair_gather", in_specs=[ANY] * n, out_specs=[ANY] * n,
        out_shape=[SDS(f.shape, f32) for f in fulls], input_output_aliases={t: t for t in range(n)},
        scratch_shapes=[pltpu.SemaphoreType.DMA((n,)), pltpu.SemaphoreType.DMA((n,))])(*fulls)


COL_SHARDED = ("attn_w_in", "rg_w_in", "ffn_w_gate", "ffn_w_up")
ROW_SHARDED = ("attn_w_out", "rg_w_out")
GATES = ("rg_w_a", "rg_w_i")
VECTORS = ("rg_conv_w", "rg_conv_b", "rg_b_a", "rg_b_i", "rg_lambda")
REPLICATED = ("norm_mix_pre", "norm_mix_post", "norm_ffn_pre", "norm_ffn_post", "attn_rel_bias")
BIG_GRADS = COL_SHARDED + ROW_SHARDED + ("ffn_w_down",)
SMALL_GRADS = GATES + VECTORS + REPLICATED
WEIGHTS =("attn_w_in", "attn_rel_bias", "attn_w_out", "rg_w_in", "rg_conv_w", "rg_conv_b", "rg_w_a", "rg_b_a",
           "rg_w_i", "rg_b_i", "rg_lambda", "rg_w_out", "norm_mix_pre", "norm_mix_post", "norm_ffn_pre",
           "norm_ffn_post", "ffn_w_gate", "ffn_w_up", "ffn_w_down")
SMALL = VECTORS + REPLICATED


GATHER_PARTS = {
    "first": (("attn_w_in", 0, 1), ("attn_w_out", 0, 1), ("rg_w_a", 0, 8), ("rg_w_i", 0, 8), ("vec", 0, 1)),
    "chunk_attn_fwd": (("ffn_w_gate", 0, 1), ("ffn_w_up", 0, 1), ("ffn_w_down", 0, 1), ("rg_w_in", 0, 1),
                       ("rg_w_out", 0, 1)),
    "sb_attn_fwd": (("ffn_w_gate", 1, 3), ("ffn_w_up", 1, 3), ("ffn_w_down", 1, 3), ("attn_w_in", 1, 1),
                    ("attn_w_out", 1, 1)),
    "ffn_up": (("rg_w_in", 1, 1), ("rg_w_out", 1, 1)),
}


TRANSPOSED = ("ffn_w_gate", "ffn_w_up")


def _natural(name, a):
    return jnp.swapaxes(a, 1, 2) if name in TRANSPOSED else a


class _WeightGather:
    def __init__(self, w):
        self.w = w
        self.names = list(COL_SHARDED + ROW_SHARDED + GATES + ("ffn_w_down", "vec"))
        self.shards = {}
        for k in self.names[:-1]:
            a = _natural(k, w[k]).astype(bf16)
            self.shards[k] = a.reshape((-1,) + a.shape[-2:])
        self.shards["vec"] = jnp.concatenate([w[k].reshape(-1) for k in VECTORS]).reshape(1, -1, LANES)
        got = _gather_call(self._items("first", self.names), [self.shards[k] for k in self.names])
        self.raw = dict(zip(self.names, got))

    @staticmethod
    def _items(part, names):
        return [(names.index(k), l0, nl) for k, l0, nl in GATHER_PARTS[part]]

    def part(self, part):
        names = list(dict.fromkeys(k for k, _, _ in GATHER_PARTS[part]))
        return (self._items(part, names), [self.shards[k] for k in names], [self.raw[k] for k in names]), names

    def views(self):
        got, w = self.raw, self.w
        out = {k: w[k] for k in REPLICATED}
        for k in COL_SHARDED + ("ffn_w_down",):
            out[k] = got[k]
        for k in ROW_SHARDED:
            l, s, ks, n = got[k].shape
            out[k] = got[k].reshape(l, 1, s * ks, n)
        for k in GATES:
            out[k] = got[k].reshape(2, LRU_BLOCKS, LRU_BW, LRU_BW)
        vec = got["vec"].reshape(N_CHIPS, -1)
        off = 0
        for k in VECTORS:
            shp = w[k].shape
            n = int(np.prod(shp))
            piece = vec[:, off:off + n].reshape((N_CHIPS,) + shp)
            off += n
            if k == "rg_conv_w":
                out[k] = piece.reshape(N_CHIPS, 2, 4, 256).transpose(1, 2, 0, 3).reshape(2, 4, D_MODEL)
            elif k in ("rg_b_a", "rg_b_i"):
                out[k] = piece.transpose(1, 2, 0, 3).reshape(2, 1, D_MODEL)
            else:
                out[k] = piece.transpose(1, 0, 2).reshape(2, 1, D_MODEL)
        return out


def _carried(plan, part, wts, fn, *args):
    if plan is None:
        return fn(*args, None)[0]
    gather, names = plan.part(part)
    out, new = fn(*args, gather)
    plan.raw.update(zip(names, new))
    wts.update(plan.views())
    return out


def _grad_blocks(name, g):
    st = jnp.stack([g[i] for i in sorted(g)])
    if name in GATES:
        st = st.reshape(2, LRU_BLOCKS, N_CHIPS, LRU_BW // N_CHIPS, LRU_BW).transpose(2, 0, 1, 3, 4)
    elif name == "rg_conv_w":
        st = st.reshape(2, 4, N_CHIPS, -1).transpose(2, 0, 1, 3)
    elif name in ("rg_b_a", "rg_b_i"):
        st = st.reshape(2, LRU_BLOCKS, N_CHIPS, -1).transpose(2, 0, 1, 3)
    elif name in VECTORS:
        st = st.reshape(2, N_CHIPS, -1).transpose(1, 0, 2)
    else:
        st = jnp.broadcast_to(st.reshape(1, -1), (N_CHIPS, st.size))
    return st.reshape(N_CHIPS, -1)


def _reduce_gradients(grads, shard_shapes):
    gs = []
    for k in BIG_GRADS:
        g = grads[k]
        if g.ndim == 3:
            g = g.reshape(g.shape[0], N_CHIPS, g.shape[1] // N_CHIPS, g.shape[2])
        gs.append(g)
    blocks = [_grad_blocks(k, grads[k]) for k in SMALL_GRADS]
    used = sum(b.shape[1] for b in blocks)
    small = jnp.concatenate(blocks + [jnp.zeros((N_CHIPS, SMALL_ROWS * PACK_COLS - used), f32)], axis=1)
    gs.append(small.reshape(1, N_CHIPS, SMALL_ROWS, PACK_COLS))
    names = BIG_GRADS + ("small",)
    c = lax.axis_index("c").astype(jnp.int32).reshape(1)
    pos = jnp.stack([2 * lax.axis_index("x") + lax.axis_index("y"), lax.axis_index("c")]).astype(jnp.int32)
    parts = [_pair_sum("grad_pair_sum_" + k, g, r, c) for k, g, r in zip(names, gs, _pair_exchange(gs))]
    slots = _chip_exchange(parts)
    full = _pair_gather([_chip_sum("grad_chip_sum_" + k, s, h, pos) for k, s, h in zip(names, slots, parts)])
    out = {k: f.reshape(shard_shapes[k]) for k, f in zip(BIG_GRADS, full)}
    flat, off = full[-1].reshape(-1), 0
    for k in SMALL_GRADS:
        n = int(np.prod(shard_shapes[k]))
        out[k] = flat[off:off + n].reshape(shard_shapes[k])
        off += n
    return out


def _adamw_fn(w, g, m, v):
    m = ADAM_B1 * m + (1.0 - ADAM_B1) * g
    v = ADAM_B2 * v + (1.0 - ADAM_B2) * (g * g)
    m_hat = m / (1.0 - ADAM_B1 ** ADAM_STEP)
    v_hat = v / (1.0 - ADAM_B2 ** ADAM_STEP)
    return -ADAM_LR * (m_hat / (jnp.sqrt(v_hat) + ADAM_EPS) + ADAM_WD * w), m, v


def _adamw(name, w, g, m, v):
    shp = w.shape
    if w.size >= 1 << 16:
        width = shp[-1]
        ops = [a.reshape(-1, width) for a in (w, g, m, v)]
        res = _rows(name, _adamw_fn, ops, [], [(width, f32)] * 3)
        return [r.reshape(shp) for r in res]
    n = w.size
    rows = -(-n // (SUBLANES * LANES)) * SUBLANES
    ops = [jnp.pad(a.reshape(-1), (0, rows * LANES - n)).reshape(rows, LANES) for a in (w, g, m, v)]
    res = _rows(name, _adamw_fn, ops, [], [(LANES, f32)] * 3, tr=rows)
    return [r.reshape(-1)[:n].reshape(shp) for r in res]


def kernel(x, attn_w_in, attn_rel_bias, attn_w_out, rg_w_in, rg_conv_w, rg_conv_b, rg_w_a, rg_b_a, rg_w_i, rg_b_i, rg_lambda, rg_w_out, norm_mix_pre, norm_mix_post, norm_ffn_pre, norm_ffn_post, ffn_w_gate, ffn_w_up, ffn_w_down, loss_target, m_attn_w_in, m_attn_rel_bias, m_attn_w_out, m_rg_w_in, m_rg_conv_w, m_rg_conv_b, m_rg_w_a, m_rg_b_a, m_rg_w_i, m_rg_b_i, m_rg_lambda, m_rg_w_out, m_norm_mix_pre, m_norm_mix_post, m_norm_ffn_pre, m_norm_ffn_post, m_ffn_w_gate, m_ffn_w_up, m_ffn_w_down, v_attn_w_in, v_attn_rel_bias, v_attn_w_out, v_rg_w_in, v_rg_conv_w, v_rg_conv_b, v_rg_w_a, v_rg_b_a, v_rg_w_i, v_rg_b_i, v_rg_lambda, v_rg_w_out, v_norm_mix_pre, v_norm_mix_post, v_norm_ffn_pre, v_norm_ffn_post, v_ffn_w_gate, v_ffn_w_up, v_ffn_w_down):
    w = dict(zip(WEIGHTS, (attn_w_in, attn_rel_bias, attn_w_out, rg_w_in, rg_conv_w, rg_conv_b, rg_w_a, rg_b_a, rg_w_i,
                           rg_b_i, rg_lambda, rg_w_out, norm_mix_pre, norm_mix_post, norm_ffn_pre, norm_ffn_post,
                           ffn_w_gate, ffn_w_up, ffn_w_down)))
    m = dict(zip(WEIGHTS, (m_attn_w_in, m_attn_rel_bias, m_attn_w_out, m_rg_w_in, m_rg_conv_w, m_rg_conv_b, m_rg_w_a,
                           m_rg_b_a, m_rg_w_i, m_rg_b_i, m_rg_lambda, m_rg_w_out, m_norm_mix_pre, m_norm_mix_post,
                           m_norm_ffn_pre, m_norm_ffn_post, m_ffn_w_gate, m_ffn_w_up, m_ffn_w_down)))
    v = dict(zip(WEIGHTS, (v_attn_w_in, v_attn_rel_bias, v_attn_w_out, v_rg_w_in, v_rg_conv_w, v_rg_conv_b, v_rg_w_a,
                           v_rg_b_a, v_rg_w_i, v_rg_b_i, v_rg_lambda, v_rg_w_out, v_norm_mix_pre, v_norm_mix_post,
                           v_norm_ffn_pre, v_norm_ffn_post, v_ffn_w_gate, v_ffn_w_up, v_ffn_w_down)))
    plan = _WeightGather(w)
    loss, dx, grads = _local_step(x[0], loss_target[0], plan.views(), plan)
    loss = lax.psum(loss, ("x", "y", "c"))
    g = _reduce_gradients(grads, {k: _natural(k, w[k]).shape for k in WEIGHTS})

    big = [k for k in WEIGHTS if k not in SMALL]
    upd = {}
    for k in big:
        res = _adamw("adamw_" + k, _natural(k, w[k]), g[k], _natural(k, m[k]), _natural(k, v[k]))
        upd[k] = [_natural(k, r) for r in res]
        g[k] = _natural(k, g[k])
    cat = lambda d: jnp.concatenate([d[k].reshape(-1) for k in SMALL])
    small = _adamw("adamw_small", cat(w), cat(g), cat(m), cat(v))
    off = 0
    for k in SMALL:
        n = w[k].size
        upd[k] = [r[off:off + n].reshape(w[k].shape) for r in small]
        off += n
    return (loss, dx[None], *[g[k] for k in WEIGHTS], *[upd[k][0] for k in WEIGHTS],
            *[upd[k][1] for k in WEIGHTS], *[upd[k][2] for k in WEIGHTS])
```

```python
import functools

import numpy as np
import jax
import jax.numpy as jnp
from jax import lax
from jax.experimental import pallas as pl
from jax.experimental.pallas import tpu as pltpu

f32 = jnp.float32
bf16 = jnp.bfloat16
SDS = jax.ShapeDtypeStruct
MESH = pl.DeviceIdType.MESH

D_MODEL = 1024
N_CHIPS = 4
DEPTH = 4
HEAD_DIM = 64
CHUNK = 64
N_LEFT = 8
REL_CLIP = 256
A_W = 512
LRU_BLOCKS = 4
LRU_BW = 256
LRU_C = 8.0
D_FF = 2816
RMS_EPS = 1e-6
LANES = 128
SUBLANES = 8
VMEM_LIMIT = 56 * 1024 * 1024

QB_A = 2 * CHUNK
QSUB_A = 4
KW_A = QB_A + N_LEFT * CHUNK
PAD_A = N_LEFT * CHUNK
EXT_A = 768
SB_BLK = 256
QSUB_B = 2
SB_DEAD = -110.0

ADAM_LR, ADAM_B1, ADAM_B2, ADAM_EPS, ADAM_WD, ADAM_STEP = 0.001, 0.9, 0.999, 1e-08, 0.01, 10


def _cparams(sem):
    return pltpu.CompilerParams(dimension_semantics=sem, vmem_limit_bytes=VMEM_LIMIT)


def _gemm(name, operands, in_specs, o_spec, out_shape, grid, dims, acc_shape, into=None):
    nred = grid[2]
    npair = len(operands) // 2
    nin = 2 * npair + (into is not None)

    def body(*refs):
        o_ref = refs[nin]
        p = None
        for t in range(npair):
            d = lax.dot_general(refs[2 * t][...], refs[2 * t + 1][...], (dims, ((), ())),
                                preferred_element_type=f32)
            p = d if p is None else p + d
        if nred == 1:
            o_ref[...] = p.astype(o_ref.dtype)
        else:
            acc = refs[nin + 1]
            r = pl.program_id(2)

            @pl.when(r == 0)
            def _():
                acc[...] = p

            @pl.when(r > 0)
            def _():
                acc[...] += p

            @pl.when(r == nred - 1)
            def _():
                o_ref[...] = acc[...].astype(o_ref.dtype)

    scratch = [] if nred == 1 else [pltpu.VMEM(acc_shape, f32)]
    extra, alias = ([], {}) if into is None else ([into], {2 * npair: 0})
    return pl.pallas_call(
        body, grid=grid, in_specs=list(in_specs) + [pl.BlockSpec(memory_space=pl.ANY)] * len(extra),
        out_specs=o_spec, out_shape=out_shape, scratch_shapes=scratch, name=name, input_output_aliases=alias,
        compiler_params=_cparams(("parallel", "parallel", "arbitrary")))(*operands, *extra)


class _Fresh:
    def __init__(self, shape):
        self.shape = tuple(shape)


def _into(buf):
    return None if isinstance(buf, _Fresh) else buf


NN = ((1,), (0,))
NT = ((1,), (1,))
TN = ((0,), (0,))


def _tile(t, want=1024):
    return min(want, t)


def _mm_cols(name, a, w, l, out_dtype):
    t, k = a.shape
    _, s, _, ns = w.shape
    tm = _tile(t)
    return _gemm(
        name, [a, w],
        [pl.BlockSpec((tm, k), lambda i, j, r: (i, 0)),
         pl.BlockSpec((None, None, k, ns), lambda i, j, r: (l, j, 0, 0))],
        pl.BlockSpec((tm, ns), lambda i, j, r: (i, j)),
        SDS((t, s * ns), out_dtype), (t // tm, s, 1), NN, None)


def _mm_cols_t(name, dy, w, l, out_dtype):
    t = dy.shape[0]
    _, s, k, ns = w.shape
    tm = _tile(t)
    return _gemm(
        name, [dy, w],
        [pl.BlockSpec((tm, ns), lambda i, j, r: (i, r)),
         pl.BlockSpec((None, None, k, ns), lambda i, j, r: (l, r, 0, 0))],
        pl.BlockSpec((tm, k), lambda i, j, r: (i, 0)),
        SDS((t, k), out_dtype), (t // tm, 1, s), NT, (tm, k))


def _mm_wgrad_cols(name, a, dy, buf, l):
    t, k = a.shape
    _, s, _, ns = buf.shape
    tt = _tile(t)
    return _gemm(
        name, [a, dy],
        [pl.BlockSpec((tt, k), lambda i, j, r: (r, 0)),
         pl.BlockSpec((tt, ns), lambda i, j, r: (r, i))],
        pl.BlockSpec((None, None, k, ns), lambda i, j, r: (l, i, 0, 0)),
        SDS(buf.shape, f32), (s, 1, t // tt), TN, (k, ns), into=_into(buf))


def _mm_rows(name, parts, w, l, out_dtype):
    t = parts[0].shape[0]
    n = w.shape[3]
    tm = _tile(t)
    ops, specs = [], []
    for p_i, a in enumerate(parts):
        kp = a.shape[1]
        ops += [a, w]
        specs += [pl.BlockSpec((tm, kp), lambda i, j, r: (i, 0)),
                  pl.BlockSpec((None, None, kp, n), lambda i, j, r, p_i=p_i: (l, 0, p_i, 0))]
    return _gemm(name, ops, specs, pl.BlockSpec((tm, n), lambda i, j, r: (i, 0)),
                 SDS((t, n), out_dtype), (t // tm, 1, 1), NN, None)


def _mm_rows_t(name, dy, w, l, out_dtype):
    t, n = dy.shape
    k = w.shape[2]
    tm = _tile(t)
    return _gemm(
        name, [dy, w],
        [pl.BlockSpec((tm, n), lambda i, j, r: (i, 0)),
         pl.BlockSpec((None, None, k, n), lambda i, j, r: (l, 0, 0, 0))],
        pl.BlockSpec((tm, k), lambda i, j, r: (i, 0)),
        SDS((t, k), out_dtype), (t // tm, 1, 1), NT, None)


def _mm_wgrad(name, a, dy, buf, l, part=0):
    t, k = a.shape
    n = dy.shape[1]
    tt = _tile(t)
    return _gemm(
        name, [a, dy],
        [pl.BlockSpec((tt, k), lambda i, j, r: (r, 0)),
         pl.BlockSpec((tt, n), lambda i, j, r: (r, 0))],
        pl.BlockSpec((None, k, n), lambda i, j, r: (l, part, 0)),
        SDS(buf.shape, f32), (1, 1, t // tt), TN, (k, n), into=_into(buf))


def _ffn_up(h, wg, wu, l, gather):
    t, k = h.shape
    s, fs = wg.shape[1], wg.shape[2]
    tm = _tile(t)

    def body(h_ref, wg_ref, wu_ref, g_ref, u_ref, hid_ref):
        hv = h_ref[...]
        g = lax.dot_general(hv, wg_ref[...], (NT, ((), ())), preferred_element_type=f32)
        u = lax.dot_general(hv, wu_ref[...], (NT, ((), ())), preferred_element_type=f32)
        g_ref[...] = g.astype(bf16)
        u_ref[...] = u.astype(bf16)
        hid_ref[...] = (g * jax.nn.sigmoid(g) * u).astype(bf16)

    wspec = pl.BlockSpec((None, None, fs, k), lambda j, i: (l, j, 0, 0))
    ospec = pl.BlockSpec((None, tm, fs), lambda j, i: (j, i, 0))
    return _call(
        body, [h, wg, wu], grid=(s, t // tm), name="ffn_up",
        in_specs=[pl.BlockSpec((tm, k), lambda j, i: (i, 0)), wspec, wspec],
        out_specs=[ospec, ospec, ospec], out_shape=[SDS((s, t, fs), bf16)] * 3,
        sem=("parallel", "parallel"), gather=gather)


def _ffn_down(hid, wd, l):
    s, t, fs = hid.shape
    n = wd.shape[3]
    tm = _tile(t, 512)
    ops, specs = [], []
    for r in range(s):
        ops += [hid, wd]
        specs += [pl.BlockSpec((None, tm, fs), lambda i, j, k, r=r: (r, i, 0)),
                  pl.BlockSpec((None, None, fs, n), lambda i, j, k, r=r: (l, r, 0, 0))]
    return _gemm("ffn_down", ops, specs, pl.BlockSpec((tm, n), lambda i, j, k: (i, 0)),
                 SDS((t, n), f32), (t // tm, 1, 1), NN, None)


def _ffn_down_bwd(df, wd, l, g, u):
    t, n = df.shape
    s, fs = wd.shape[1], wd.shape[2]
    tm = _tile(t)

    def body(df_ref, wd_ref, g_ref, u_ref, dg_ref, du_ref):
        dh = lax.dot_general(df_ref[...], wd_ref[...], (NT, ((), ())), preferred_element_type=f32)
        gv = g_ref[...].astype(f32)
        uv = u_ref[...].astype(f32)
        sg = jax.nn.sigmoid(gv)
        du_ref[...] = (dh * gv * sg).astype(bf16)
        dg_ref[...] = (dh * uv * (sg * (1.0 + gv * (1.0 - sg)))).astype(bf16)

    bspec = pl.BlockSpec((None, tm, fs), lambda j, i: (j, i, 0))
    return pl.pallas_call(
        body, grid=(s, t // tm), name="ffn_down_bwd",
        in_specs=[pl.BlockSpec((tm, n), lambda j, i: (i, 0)),
                  pl.BlockSpec((None, None, fs, n), lambda j, i: (l, j, 0, 0)), bspec, bspec],
        out_specs=[bspec, bspec], out_shape=[SDS((s, t, fs), bf16)] * 2,
        compiler_params=_cparams(("parallel", "parallel")))(df, wd, g, u)


def _ffn_up_bwd(dg, du, wg, wu, l):
    s, t, fs = dg.shape
    k = wg.shape[3]
    tm = _tile(t, 512)
    ops, specs = [], []
    for r in range(s):
        aspec = pl.BlockSpec((None, tm, fs), lambda i, j, kk, r=r: (r, i, 0))
        wspec = pl.BlockSpec((None, None, fs, k), lambda i, j, kk, r=r: (l, r, 0, 0))
        ops += [dg, wg, du, wu]
        specs += [aspec, wspec, aspec, wspec]
    return _gemm("ffn_up_bwd", ops, specs, pl.BlockSpec((tm, k), lambda i, j, kk: (i, 0)),
                 SDS((t, k), f32), (t // tm, 1, 1), NN, None)


def _ffn_wgrad_up(h, dg, du, buf_g, buf_u, l):
    t, k = h.shape
    s, _, fs = dg.shape
    tt = _tile(t)
    nred = t // tt

    fresh = isinstance(buf_g, _Fresh)

    def body(*refs):
        h_ref, dg_ref, du_ref = refs[:3]
        og_ref, ou_ref, acc_g, acc_u = refs[-4:]
        r = pl.program_id(1)
        hv = h_ref[...]
        pg = lax.dot_general(dg_ref[...], hv, (TN, ((), ())), preferred_element_type=f32)
        pu = lax.dot_general(du_ref[...], hv, (TN, ((), ())), preferred_element_type=f32)

        @pl.when(r == 0)
        def _():
            acc_g[...] = pg
            acc_u[...] = pu

        @pl.when(r > 0)
        def _():
            acc_g[...] += pg
            acc_u[...] += pu

        @pl.when(r == nred - 1)
        def _():
            og_ref[...] = acc_g[...]
            ou_ref[...] = acc_u[...]

    dspec = pl.BlockSpec((None, tt, fs), lambda i, r: (i, r, 0))
    ospec = pl.BlockSpec((None, None, fs, k), lambda i, r: (l, i, 0, 0))
    extra, alias = ([], {}) if fresh else ([buf_g, buf_u], {3: 0, 4: 1})
    return pl.pallas_call(
        body, grid=(s, nred), name="ffn_wgrad_up",
        in_specs=[pl.BlockSpec((tt, k), lambda i, r: (r, 0)), dspec, dspec] + [ANY] * len(extra),
        out_specs=[ospec, ospec], out_shape=[SDS(buf_g.shape, f32), SDS(buf_u.shape, f32)],
        scratch_shapes=[pltpu.VMEM((fs, k), f32)] * 2, input_output_aliases=alias,
        compiler_params=_cparams(("parallel", "arbitrary")))(h, dg, du, *extra)


def _ffn_wgrad_down(hid, df, buf, l):
    s, t, fs = hid.shape
    n = df.shape[1]
    tt = _tile(t)
    return _gemm(
        "ffn_wgrad_down", [hid, df],
        [pl.BlockSpec((None, tt, fs), lambda i, j, r: (i, r, 0)),
         pl.BlockSpec((tt, n), lambda i, j, r: (r, 0))],
        pl.BlockSpec((None, None, fs, n), lambda i, j, r: (l, i, 0, 0)),
        SDS(buf.shape, f32), (s, 1, t // tt), TN, (fs, n), into=_into(buf))


def _rows(name, fn, rows, consts, row_outs, acc_outs=(), tr=512):
    rows = [r if isinstance(r, tuple) else (r, r.shape[1], 0) for r in rows]
    t = rows[0][0].shape[0]
    tr = max(d for d in range(SUBLANES, min(tr, t) + 1, SUBLANES) if t % d == 0)
    nin = len(rows) + len(consts)
    no, na = len(row_outs), len(acc_outs)

    def body(*refs):
        vals = fn(*[r[...] for r in refs[:nin]])
        if not isinstance(vals, (tuple, list)):
            vals = (vals,)
        for k in range(no):
            refs[nin + k][...] = vals[k].astype(refs[nin + k].dtype)
        first = pl.program_id(0) == 0
        for k in range(na):
            ref, val = refs[nin + no + k], vals[no + k]

            @pl.when(first)
            def _(ref=ref, val=val):
                ref[...] = val

            @pl.when(jnp.logical_not(first))
            def _(ref=ref, val=val):
                ref[...] += val

    in_specs = [pl.BlockSpec((tr, w), lambda i, cb=cb: (i, cb)) for (_, w, cb) in rows]
    in_specs += [pl.BlockSpec(c.shape, lambda i, nd=c.ndim: (0,) * nd) for c in consts]
    out_specs = [pl.BlockSpec((tr, w), lambda i: (i, 0)) for (w, _) in row_outs]
    out_specs += [pl.BlockSpec(s, lambda i, nd=len(s): (0,) * nd) for (s, _) in acc_outs]
    out_shape = [SDS((t, w), dt) for (w, dt) in row_outs] + [SDS(s, dt) for (s, dt) in acc_outs]
    res = pl.pallas_call(
        body, grid=(t // tr,), in_specs=in_specs, out_specs=out_specs, out_shape=out_shape,
        name=name, compiler_params=_cparams(("arbitrary",)))(*[r[0] for r in rows], *consts)
    return res


def _rstd(x):
    return lax.rsqrt(jnp.mean(x * x, axis=-1, keepdims=True) + RMS_EPS)


def _norm_fwd(x, g):
    return x * _rstd(x) * g


def _norm_bwd(u, dy, g):
    r = _rstd(u)
    n = u * r
    dn = dy * g
    du = r * (dn - n * jnp.mean(dn * n, axis=-1, keepdims=True))
    return du, jnp.sum(dy * n, axis=0, keepdims=True)


def _gelu(x):
    c = 0.7978845608028654
    return 0.5 * x * (1.0 + jnp.tanh(c * (x + 0.044715 * x * x * x)))


def _gelu_grad(x):
    c = 0.7978845608028654
    th = jnp.tanh(c * (x + 0.044715 * x * x * x))
    return 0.5 * (1.0 + th) + 0.5 * x * (1.0 - th * th) * c * (1.0 + 3.0 * 0.044715 * x * x)


def _mask_heads(x):
    lane = lax.broadcasted_iota(jnp.int32, x.shape, 1)
    return [jnp.where((lane >= h * HEAD_DIM) & (lane < (h + 1) * HEAD_DIM), x, jnp.zeros_like(x))
            for h in range(LANES // HEAD_DIM)]


def _chunk_valid(start):
    qi = lax.broadcasted_iota(jnp.int32, (QB_A, KW_A), 0)
    kj = lax.broadcasted_iota(jnp.int32, (QB_A, KW_A), 1)
    qc = qi // CHUNK
    kc = kj // CHUNK
    return (kc >= qc) & (kc <= qc + N_LEFT) & (kj + start >= PAD_A)


def _chunk_probs(q, k, bias, valid):
    s = lax.dot_general(q, k, (NT, ((), ())), preferred_element_type=f32) * (HEAD_DIM ** -0.5) + bias
    s = jnp.where(valid, s, -1e30)
    p = jnp.exp(s - jnp.max(s, axis=-1, keepdims=True))
    return p / jnp.sum(p, axis=-1, keepdims=True)


def _chunk_attn_fwd(proj, kpad, vpad, bias, gather):
    t = proj.shape[0]
    tp = kpad.shape[0]
    step = QSUB_A * QB_A

    def body(q_ref, k_ref, v_ref, b_ref, o_ref):
        for sb in range(QSUB_A):
            start = pl.multiple_of((pl.program_id(1) * QSUB_A + sb) * QB_A, QB_A)
            rows = pl.ds(sb * QB_A, QB_A)
            valid = _chunk_valid(start)
            kw = k_ref[pl.ds(start, KW_A), :]
            qm = _mask_heads(q_ref[rows, :])
            vm = _mask_heads(v_ref[pl.ds(start, KW_A), :])
            o = None
            for h in range(len(qm)):
                p = _chunk_probs(qm[h], kw, b_ref[h], valid)
                d = jnp.dot(p.astype(bf16), vm[h], preferred_element_type=f32)
                o = d if o is None else o + d
            o_ref[rows, :] = o.astype(bf16)

    kv_spec = pl.BlockSpec((tp, LANES), lambda hp, qb: (0, hp))
    outs, new = _call(
        body, [proj, kpad, vpad, bias], grid=(A_W // LANES, t // step), name="chunk_attn_fwd",
        in_specs=[pl.BlockSpec((step, LANES), lambda hp, qb: (qb, hp)), kv_spec, kv_spec,
                  pl.BlockSpec((2, QB_A, KW_A), lambda hp, qb: (hp, 0, 0))],
        out_specs=[pl.BlockSpec((step, LANES), lambda hp, qb: (qb, hp))],
        out_shape=[SDS((t, A_W), bf16)], sem=("parallel", "arbitrary"), gather=gather)
    return outs[0], new


def _chunk_attn_bwd(proj, kpad, vpad, bias, dout):
    t = proj.shape[0]
    tp = kpad.shape[0]
    step = QSUB_A * QB_A

    def body(q_ref, k_ref, v_ref, b_ref, do_ref, dq_ref, dk_ref, dv_ref, db_ref):
        qb = pl.program_id(1)

        @pl.when(qb == 0)
        def _():
            dk_ref[...] = jnp.zeros_like(dk_ref)
            dv_ref[...] = jnp.zeros_like(dv_ref)
            db_ref[...] = jnp.zeros_like(db_ref)

        for sb in range(QSUB_A):
            start = pl.multiple_of((qb * QSUB_A + sb) * QB_A, QB_A)
            rows = pl.ds(sb * QB_A, QB_A)
            win = pl.ds(start, KW_A)
            valid = _chunk_valid(start)
            kw = k_ref[win, :]
            vw = v_ref[win, :]
            qm = _mask_heads(q_ref[rows, :])
            dom = _mask_heads(do_ref[rows, :])
            km = _mask_heads(kw)
            dq = dk = dv = None
            for h in range(len(qm)):
                p = _chunk_probs(qm[h], kw, b_ref[h], valid)
                dp = lax.dot_general(dom[h], vw, (NT, ((), ())), preferred_element_type=f32)
                ds = p * (dp - jnp.sum(dp * p, axis=-1, keepdims=True))
                db_ref[h] += ds
                dsb = (ds * (HEAD_DIM ** -0.5)).astype(bf16)
                terms = (jnp.dot(dsb, km[h], preferred_element_type=f32),
                         lax.dot_general(dsb, qm[h], (TN, ((), ())), preferred_element_type=f32),
                         lax.dot_general(p.astype(bf16), dom[h], (TN, ((), ())), preferred_element_type=f32))
                dq, dk, dv = terms if dq is None else (dq + terms[0], dk + terms[1], dv + terms[2])
            dq_ref[rows, :] = dq.astype(bf16)
            dk_ref[win, :] += dk
            dv_ref[win, :] += dv

    kv_spec = pl.BlockSpec((tp, LANES), lambda hp, qb: (0, hp))
    q_spec = pl.BlockSpec((step, LANES), lambda hp, qb: (qb, hp))
    b_spec = pl.BlockSpec((2, QB_A, KW_A), lambda hp, qb: (hp, 0, 0))
    return pl.pallas_call(
        body, grid=(A_W // LANES, t // step), name="chunk_attn_bwd",
        in_specs=[q_spec, kv_spec, kv_spec, b_spec, q_spec],
        out_specs=[q_spec, kv_spec, kv_spec, b_spec],
        out_shape=[SDS((t, A_W), bf16), SDS((tp, A_W), f32), SDS((tp, A_W), f32),
                   SDS((2 * A_W // LANES, QB_A, KW_A), f32)],
        compiler_params=_cparams(("parallel", "arbitrary")))(proj, kpad, vpad, bias, dout)


def _bias_ext(table):
    flat = PAD_A + QB_A - 1 - REL_CLIP
    top = jnp.broadcast_to(table[:, 2 * REL_CLIP:], (table.shape[0], flat))
    lo = 2 * REL_CLIP - (EXT_A - 1 - flat)
    return jnp.concatenate([top, jnp.flip(table[:, lo:], axis=1)], axis=1)


def _bias_window(table):
    nh = table.shape[0]
    e = jnp.broadcast_to(_bias_ext(table)[:, None, :], (nh, QB_A, EXT_A)).reshape(nh, QB_A * EXT_A)
    m = e[:, :QB_A * (EXT_A - 1)].reshape(nh, QB_A, EXT_A - 1)
    return m[:, :, QB_A - 1:]


def _bias_window_grad(dbias):
    nh = dbias.shape[0]
    m = jnp.pad(dbias, ((0, 0), (0, 0), (QB_A - 1, 0))).reshape(nh, QB_A * (EXT_A - 1))
    dext = jnp.sum(jnp.pad(m, ((0, 0), (0, QB_A))).reshape(nh, QB_A, EXT_A), axis=1)
    flat = PAD_A + QB_A - 1 - REL_CLIP
    lo = 2 * REL_CLIP - (EXT_A - 1 - flat)
    tail = jnp.flip(dext[:, flat:], axis=1)
    tail = tail.at[:, -1].add(jnp.sum(dext[:, :flat], axis=1))
    return jnp.pad(tail, ((0, 0), (lo, 0)))


def _tri_suffix(x, tri):
    hi = x.astype(bf16)
    lo = (x - hi.astype(f32)).astype(bf16)
    return jnp.dot(hi, tri, preferred_element_type=f32) + jnp.dot(lo, tri, preferred_element_type=f32)


def _sb_block(q, k, run, tri, causal):
    z = lax.dot_general(q, k, (NT, ((), ())), preferred_element_type=f32) * (HEAD_DIM ** -0.5)
    e = jnp.exp(-jnp.abs(z))
    l1p = jnp.log(1.0 + e)
    lb = jnp.minimum(z, 0.0) - l1p
    lmb = lb - z
    if causal is not None:
        lmb = jnp.where(causal, lmb, 0.0)
    cs = _tri_suffix(lmb, tri)
    w = jnp.exp(lb + (run + cs - lmb))
    if causal is not None:
        w = jnp.where(causal, w, 0.0)
    return z, e, w, run + cs[:, 0:1]


def _sb_tri():
    r = lax.broadcasted_iota(jnp.int32, (SB_BLK, SB_BLK), 0)
    c = lax.broadcasted_iota(jnp.int32, (SB_BLK, SB_BLK), 1)
    return (r >= c).astype(bf16), c < r


def _sb_live(runs):
    m = runs[0]
    for r in runs[1:]:
        m = jnp.maximum(m, r)
    return jnp.max(m) > SB_DEAD


def _sb_fwd(proj, gather):
    t = proj.shape[0]
    cb = A_W // LANES
    nh = LANES // HEAD_DIM

    step_rows = QSUB_B * SB_BLK

    def body(q_ref, k_ref, v_ref, o_ref, of_ref):
        tri, diag = _sb_tri()
        for sb in range(QSUB_B):
            _sb_fwd_block(pl.program_id(1) * QSUB_B + sb, pl.ds(sb * SB_BLK, SB_BLK), tri, diag,
                          q_ref, k_ref, v_ref, o_ref, of_ref)

    def _sb_fwd_block(qb, qrows, tri, diag, q_ref, k_ref, v_ref, o_ref, of_ref):
        qm = _mask_heads(q_ref[qrows, :])

        def pair(kb, carry, causal):
            rows = pl.ds(pl.multiple_of(kb * SB_BLK, SB_BLK), SB_BLK)
            k = k_ref[rows, :]
            vm = _mask_heads(v_ref[rows, :])
            runs, acc = [], carry[nh]
            for h in range(nh):
                _, _, w, run = _sb_block(qm[h], k, carry[h], tri, causal)
                acc = acc + jnp.dot(w.astype(bf16), vm[h], preferred_element_type=f32)
                runs.append(run)
            return (*runs, acc)

        zero = jnp.zeros((SB_BLK, 1), f32)
        carry = pair(qb, (zero,) * nh + (jnp.zeros((SB_BLK, LANES), f32),), diag)

        def cond(st):
            return (st[0] < qb) & _sb_live(st[1][:nh])

        def step(st):
            return st[0] + 1, pair(qb - 1 - st[0], st[1], None)

        _, carry = lax.while_loop(cond, step, (jnp.int32(0), carry))
        o_ref[qrows, :] = carry[nh].astype(bf16)
        of_ref[qrows, :] = carry[nh]

    ospec = pl.BlockSpec((step_rows, LANES), lambda hp, qb: (qb, hp))
    return _call(
        body, [proj, proj, proj], grid=(cb, t // step_rows), name="sb_attn_fwd",
        in_specs=[pl.BlockSpec((step_rows, LANES), lambda hp, qb: (qb, 3 * cb + hp)),
                  pl.BlockSpec((t, LANES), lambda hp, qb: (0, 4 * cb + hp)),
                  pl.BlockSpec((t, LANES), lambda hp, qb: (0, 5 * cb + hp))],
        out_specs=[ospec, ospec], out_shape=[SDS((t, A_W), bf16), SDS((t, A_W), f32)],
        sem=("parallel", "arbitrary"), gather=gather)


def _sb_bwd(proj, out_b, dout, gather):
    t = proj.shape[0]
    cb = A_W // LANES
    nh = LANES // HEAD_DIM

    step_rows = QSUB_B * SB_BLK

    def body(q_ref, k_ref, v_ref, o_ref, do_ref, dq_ref, dk_ref, dv_ref):
        tri, diag = _sb_tri()

        @pl.when(pl.program_id(1) == 0)
        def _():
            dk_ref[...] = jnp.zeros_like(dk_ref)
            dv_ref[...] = jnp.zeros_like(dv_ref)

        for sb in range(QSUB_B):
            _sb_bwd_block(pl.program_id(1) * QSUB_B + sb, pl.ds(sb * SB_BLK, SB_BLK), tri, diag,
                          q_ref, k_ref, v_ref, o_ref, do_ref, dq_ref, dk_ref, dv_ref)

    def _sb_bwd_block(qb, qrows, tri, diag, q_ref, k_ref, v_ref, o_ref, do_ref, dq_ref, dk_ref, dv_ref):
        qm = _mask_heads(q_ref[qrows, :])
        do = do_ref[qrows, :]
        dom = _mask_heads(do)
        dsums = [jnp.sum(t_, axis=-1, keepdims=True) for t_ in _mask_heads(do.astype(f32) * o_ref[qrows, :])]

        def pair(kb, carry, causal):
            rows = pl.ds(pl.multiple_of(kb * SB_BLK, SB_BLK), SB_BLK)
            k = k_ref[rows, :]
            v = v_ref[rows, :]
            km = _mask_heads(k)
            new, dq, dk, dv = [], carry[2 * nh], None, None
            for h in range(nh):
                z, e, w, run = _sb_block(qm[h], k, carry[2 * h], tri, causal)
                inv = 1.0 / (1.0 + e)
                beta = jnp.where(z >= 0.0, inv, e * inv)
                wb = w.astype(bf16)
                g = lax.dot_general(dom[h], v, (NT, ((), ())), preferred_element_type=f32) * wb.astype(f32)
                sg = _tri_suffix(g, tri)
                dz = g * (1.0 - beta) - (dsums[h] - carry[2 * h + 1] - sg) * beta
                if causal is not None:
                    dz = jnp.where(causal, dz, 0.0)
                dzb = (dz * (HEAD_DIM ** -0.5)).astype(bf16)
                dq = dq + jnp.dot(dzb, km[h], preferred_element_type=f32)
                tk = lax.dot_general(dzb, qm[h], (TN, ((), ())), preferred_element_type=f32)
                tv = lax.dot_general(wb, dom[h], (TN, ((), ())), preferred_element_type=f32)
                dk, dv = (tk, tv) if dk is None else (dk + tk, dv + tv)
                new += [run, carry[2 * h + 1] + sg[:, 0:1]]
            dk_ref[rows, :] += dk
            dv_ref[rows, :] += dv
            return (*new, dq)

        zero = jnp.zeros((SB_BLK, 1), f32)
        carry = pair(qb, (zero,) * (2 * nh) + (jnp.zeros((SB_BLK, LANES), f32),), diag)

        def cond(st):
            return (st[0] < qb) & _sb_live(st[1][0:2 * nh:2])

        def step(st):
            return st[0] + 1, pair(qb - 1 - st[0], st[1], None)

        _, carry = lax.while_loop(cond, step, (jnp.int32(0), carry))
        dq_ref[qrows, :] = carry[2 * nh].astype(bf16)

    kv_in = lambda seg: pl.BlockSpec((t, LANES), lambda hp, qb: (0, seg * cb + hp))
    q_spec = pl.BlockSpec((step_rows, LANES), lambda hp, qb: (qb, hp))
    kv_out = pl.BlockSpec((t, LANES), lambda hp, qb: (0, hp))
    return _call(
        body, [proj, proj, proj, out_b, dout], grid=(cb, t // step_rows), name="sb_attn_bwd",
        in_specs=[pl.BlockSpec((step_rows, LANES), lambda hp, qb: (qb, 3 * cb + hp)), kv_in(4), kv_in(5),
                  q_spec, pl.BlockSpec((step_rows, LANES), lambda hp, qb: (qb, cb + hp))],
        out_specs=[q_spec, kv_out, kv_out],
        out_shape=[SDS((t, A_W), bf16), SDS((t, A_W), f32), SDS((t, A_W), f32)],
        sem=("parallel", "arbitrary"), gather=gather)


def _halo_specs(tr, w, col, nblk):
    per = tr // SUBLANES
    cur = pl.BlockSpec((tr, w), lambda i: (i, col))
    prev = pl.BlockSpec((SUBLANES, w), lambda i: (jnp.maximum(i * per - 1, 0), col))
    nxt = pl.BlockSpec((SUBLANES, w), lambda i: (jnp.minimum((i + 1) * per, nblk * per - 1), col))
    return cur, prev, nxt


def _taps_before(cur, prev8, first):
    prev8 = jnp.where(first, 0.0, prev8)
    ext = jnp.concatenate([prev8, cur], axis=0)
    return [pltpu.roll(ext, s, 0)[SUBLANES:] for s in (3, 2, 1)]


def _taps_after(cur, next8, last):
    n = cur.shape[0]
    next8 = jnp.where(last, 0.0, next8)
    ext = jnp.concatenate([cur, next8], axis=0)
    return [pltpu.roll(ext, n + SUBLANES - s, 0)[:n] for s in (1, 2, 3)]


def _block_diag(x, w_ref, dims):
    outs = [lax.dot_general(x[:, n * LRU_BW:(n + 1) * LRU_BW], w_ref[n], (dims, ((), ())),
                            preferred_element_type=f32) for n in range(LRU_BLOCKS)]
    return jnp.concatenate(outs, axis=1)


def _lru_gates(xc, wa_ref, wi_ref, ba, bi, lam):
    xb = xc.astype(bf16)
    r = jax.nn.sigmoid(_block_diag(xb, wa_ref, NN) + ba)
    ig = jax.nn.sigmoid(_block_diag(xb, wi_ref, NN) + bi)
    sp = jnp.maximum(-lam, 0.0) + jnp.log(1.0 + jnp.exp(-jnp.abs(lam)))
    log_a = -LRU_C * r * sp
    a = jnp.exp(log_a)
    x2 = 2.0 * log_a
    one_minus = jnp.where(x2 > -1e-2, -x2 * (1.0 + x2 * (0.5 + x2 * (1.0 / 6.0))), 1.0 - a * a)
    mult = jnp.sqrt(one_minus)
    return xb, r, ig, sp, a, mult


def _rg_gates_fwd(proj, conv_w, conv_b, wa, wi, ba, bi, lam, tr=512):
    t = proj.shape[0]
    w = D_MODEL
    tr = min(tr, t)
    nblk = t // tr
    cur, prev, _ = _halo_specs(tr, w, 1, nblk)

    def body(x_ref, xp_ref, cw_ref, cb_ref, wa_ref, wi_ref, ba_ref, bi_ref, lam_ref, xc_ref, a_ref, u_ref):
        x = x_ref[...]
        taps = _taps_before(x, xp_ref[...], pl.program_id(0) == 0) + [x]
        xc = cb_ref[...]
        for k in range(4):
            xc = xc + cw_ref[k:k + 1, :] * taps[k]
        _, _, ig, _, a, mult = _lru_gates(xc, wa_ref, wi_ref, ba_ref[...], bi_ref[...], lam_ref[...])
        xc_ref[...] = xc
        a_ref[...] = a
        u_ref[...] = mult * (ig * xc)

    full = lambda a_: pl.BlockSpec(a_.shape, lambda i, nd=a_.ndim: (0,) * nd)
    ospec = pl.BlockSpec((tr, w), lambda i: (i, 0))
    return pl.pallas_call(
        body, grid=(nblk,), name="rg_gates_fwd",
        in_specs=[cur, prev] + [full(a_) for a_ in (conv_w, conv_b, wa, wi, ba, bi, lam)],
        out_specs=[ospec] * 3, out_shape=[SDS((t, w), f32)] * 3,
        compiler_params=_cparams(("parallel",)))(proj, proj, conv_w, conv_b, wa, wi, ba, bi, lam)


def _lru_scan(name, a, b, reverse, tt=512):
    t, w = a.shape
    tt = min(tt, t)
    nt = t // tt
    ng = tt // SUBLANES

    def body(a_ref, b_ref, h_ref, carry_ref):
        @pl.when(pl.program_id(0) == 0)
        def _():
            carry_ref[...] = jnp.zeros_like(carry_ref)

        row = lax.broadcasted_iota(jnp.int32, (SUBLANES, w), 0)

        def group(gi, carry):
            g = (ng - 1 - gi) if reverse else gi
            rows = pl.ds(pl.multiple_of(g * SUBLANES, SUBLANES), SUBLANES)
            av = a_ref[rows, :]
            bv = b_ref[rows, :]
            for s in (1, 2, 4):
                sh = (SUBLANES - s) if reverse else s
                ok = (row < SUBLANES - s) if reverse else (row >= s)
                a_s = pltpu.roll(av, sh, 0)
                b_s = pltpu.roll(bv, sh, 0)
                bv = jnp.where(ok, av * b_s + bv, bv)
                av = jnp.where(ok, av * a_s, av)
            h = av * carry + bv
            h_ref[rows, :] = h
            edge = h[0:1, :] if reverse else h[SUBLANES - 1:SUBLANES, :]
            return jnp.broadcast_to(edge, (SUBLANES, w))

        carry_ref[...] = lax.fori_loop(0, ng, group, carry_ref[...], unroll=4)

    tmap = (lambda i: (nt - 1 - i, 0)) if reverse else (lambda i: (i, 0))
    spec = pl.BlockSpec((tt, w), tmap)
    return pl.pallas_call(
        body, grid=(nt,), name=name, in_specs=[spec, spec], out_specs=spec,
        out_shape=SDS((t, w), f32), scratch_shapes=[pltpu.VMEM((SUBLANES, w), f32)],
        compiler_params=_cparams(("arbitrary",)))(a, b)


def _rg_gates_bwd(dhs, c, hs, xc, wa, wi, ba, bi, lam, tr=256):
    t, w = xc.shape
    tr = min(tr, t)
    nblk = t // tr
    cur, prev, nxt = _halo_specs(tr, w, 0, nblk)

    def body(dhs_ref, c_ref, cn_ref, hs_ref, hp_ref, xc_ref, wa_ref, wi_ref, ba_ref, bi_ref, lam_ref,
             dxc_ref, dwa_ref, dwi_ref, dba_ref, dbi_ref, dlam_ref):
        i = pl.program_id(0)
        c_next = _taps_after(c_ref[...], cn_ref[...], i == nblk - 1)[0]
        h_prev = _taps_before(hs_ref[...], hp_ref[...], i == 0)[2]
        xc = xc_ref[...]
        lam = lam_ref[...]
        xb, r, ig, sp, a, mult = _lru_gates(xc, wa_ref, wi_ref, ba_ref[...], bi_ref[...], lam)
        dh = dhs_ref[...] + c_next
        dlog_a = dh * h_prev * a - (dh * ig * xc) * (a * a / mult)
        dpre_a = (dlog_a * (-LRU_C * sp) * r * (1.0 - r)).astype(bf16)
        dpre_i = (dh * mult * xc * ig * (1.0 - ig)).astype(bf16)
        dxc_ref[...] = (dh * mult * ig + _block_diag(dpre_a, wa_ref, NT) + _block_diag(dpre_i, wi_ref, NT))
        dsig = 1.0 / (1.0 + jnp.exp(lam))
        sums = [jnp.sum(dpre_a.astype(f32), axis=0, keepdims=True),
                jnp.sum(dpre_i.astype(f32), axis=0, keepdims=True),
                jnp.sum(dlog_a * (-LRU_C * r), axis=0, keepdims=True) * (-dsig)]

        @pl.when(i == 0)
        def _():
            dwa_ref[...] = jnp.zeros_like(dwa_ref)
            dwi_ref[...] = jnp.zeros_like(dwi_ref)
            dba_ref[...] = jnp.zeros_like(dba_ref)
            dbi_ref[...] = jnp.zeros_like(dbi_ref)
            dlam_ref[...] = jnp.zeros_like(dlam_ref)

        for n in range(LRU_BLOCKS):
            sl = slice(n * LRU_BW, (n + 1) * LRU_BW)
            dwa_ref[n] += lax.dot_general(xb[:, sl], dpre_a[:, sl], (TN, ((), ())), preferred_element_type=f32)
            dwi_ref[n] += lax.dot_general(xb[:, sl], dpre_i[:, sl], (TN, ((), ())), preferred_element_type=f32)
        dba_ref[...] += sums[0]
        dbi_ref[...] += sums[1]
        dlam_ref[...] += sums[2]

    full = lambda a_: pl.BlockSpec(a_.shape, lambda i, nd=a_.ndim: (0,) * nd)
    vec = pl.BlockSpec((1, w), lambda i: (0, 0))
    mat = pl.BlockSpec((LRU_BLOCKS, LRU_BW, LRU_BW), lambda i: (0, 0, 0))
    return pl.pallas_call(
        body, grid=(nblk,), name="rg_gates_bwd",
        in_specs=[cur, cur, nxt, cur, prev, cur] + [full(a_) for a_ in (wa, wi, ba, bi, lam)],
        out_specs=[cur, mat, mat, vec, vec, vec],
        out_shape=[SDS((t, w), f32), SDS((LRU_BLOCKS, LRU_BW, LRU_BW), f32), SDS((LRU_BLOCKS, LRU_BW, LRU_BW), f32),
                   SDS((1, w), f32), SDS((1, w), f32), SDS((1, w), f32)],
        compiler_params=_cparams(("arbitrary",)))(dhs, c, c, hs, hs, xc, wa, wi, ba, bi, lam)


def _rg_conv_bwd(dxc, proj, conv_w, tr=512):
    t, w = dxc.shape
    tr = min(tr, t)
    nblk = t // tr
    cur, _, nxt = _halo_specs(tr, w, 0, nblk)
    xcur, xprev, _ = _halo_specs(tr, w, 1, nblk)

    def body(d_ref, dn_ref, x_ref, xp_ref, cw_ref, dx_ref, dcw_ref, dcb_ref):
        i = pl.program_id(0)
        d = d_ref[...]
        x = x_ref[...]
        after = _taps_after(d, dn_ref[...], i == nblk - 1)
        before = _taps_before(x, xp_ref[...], i == 0) + [x]
        dx = cw_ref[3:4, :] * d
        for s in (1, 2, 3):
            dx = dx + cw_ref[3 - s:4 - s, :] * after[s - 1]
        dx_ref[...] = dx.astype(bf16)
        dcw = jnp.concatenate([jnp.sum(d * before[k], axis=0, keepdims=True) for k in range(4)], axis=0)
        dcb = jnp.sum(d, axis=0, keepdims=True)

        @pl.when(i == 0)
        def _():
            dcw_ref[...] = dcw
            dcb_ref[...] = dcb

        @pl.when(i > 0)
        def _():
            dcw_ref[...] += dcw
            dcb_ref[...] += dcb

    return pl.pallas_call(
        body, grid=(nblk,), name="rg_conv_bwd",
        in_specs=[cur, nxt, xcur, xprev, pl.BlockSpec((4, w), lambda i: (0, 0))],
        out_specs=[cur, pl.BlockSpec((4, w), lambda i: (0, 0)), pl.BlockSpec((1, w), lambda i: (0, 0))],
        out_shape=[SDS((t, w), bf16), SDS((4, w), f32), SDS((1, w), f32)],
        compiler_params=_cparams(("arbitrary",)))(dxc, dxc, proj, proj, conv_w)


def _attn_fwd(h, wts, j, plan):
    proj = _mm_cols("attn_in", h, wts["attn_w_in"], j, bf16)
    kpad = jnp.pad(proj[:, A_W:2 * A_W], ((PAD_A, 0), (0, 0)))
    vpad = jnp.pad(proj[:, 2 * A_W:3 * A_W], ((PAD_A, 0), (0, 0)))
    bias = _bias_window(wts["attn_rel_bias"][j])
    plan = plan if j == 0 else None
    out_a = _carried(plan, "chunk_attn_fwd", wts, _chunk_attn_fwd, proj, kpad, vpad, bias)
    out_b, out_b32 = _carried(plan, "sb_attn_fwd", wts, _sb_fwd, proj)
    m = _mm_rows("attn_out", [out_a, out_b], wts["attn_w_out"], j, f32)
    return m, (proj, kpad, vpad, bias, out_a, out_b, out_b32)


def _attn_bwd(dm, h, saved, wts, j, grads, exch):
    proj, kpad, vpad, bias, out_a, out_b, out_b32 = saved
    dout = _mm_rows_t("attn_out_t", dm, wts["attn_w_out"], j, bf16)
    grads["attn_w_out"][j] = _mm_wgrad("attn_out_wgrad_a", out_a, dm, grads["attn_w_out"][j], 0, 0)
    grads["attn_w_out"][j] = _mm_wgrad("attn_out_wgrad_b", out_b, dm, grads["attn_w_out"][j], 0, 1)
    dqa, dka, dva, dbias = _chunk_attn_bwd(proj, kpad, vpad, bias, dout)
    if exch is not None and j == 0:
        (dqs, dks, dvs), slots = _sb_bwd(proj, out_b32, dout, exch.carry())
        exch.carried(slots)
    else:
        dqs, dks, dvs = _sb_bwd(proj, out_b32, dout, None)[0]
    grads["attn_rel_bias"][j] = _bias_window_grad(dbias)
    dproj = jnp.concatenate([dqa, dka[PAD_A:].astype(bf16), dva[PAD_A:].astype(bf16),
                             dqs, dks.astype(bf16), dvs.astype(bf16)], axis=1)
    grads["attn_w_in"][j] = _mm_wgrad_cols("attn_in_wgrad", h, dproj, grads["attn_w_in"][j], 0)
    return _mm_cols_t("attn_in_t", dproj, wts["attn_w_in"], j, f32)


def _rg_fwd(h, wts, j, plan):
    proj =_mm_cols("rg_in", h, wts["rg_w_in"], j, f32)
    small = [wts[k][j] for k in ("rg_conv_w", "rg_conv_b", "rg_w_a", "rg_w_i", "rg_b_a", "rg_b_i", "rg_lambda")]
    xc, a, u = _rg_gates_fwd(proj, *small)
    hs = _lru_scan("lru_scan_fwd", a, u, False)
    yp = _rows("rg_gate_out", lambda hv, gv: hv * _gelu(gv), [hs, (proj, D_MODEL, 0)], [], [(D_MODEL, bf16)])[0]
    m = _mm_rows("rg_out", [yp], wts["rg_w_out"], j, f32)
    return m, (proj, xc, a, hs, yp)


def _rg_bwd(dm, h, saved, wts, j, grads, exch):
    proj, xc, a, hs, yp = saved
    dyp = _mm_rows_t("rg_out_t", dm, wts["rg_w_out"], j, f32)
    grads["rg_w_out"][j] = _mm_wgrad("rg_out_wgrad", yp, dm, grads["rg_w_out"][j], 0)

    def gate_bwd(dy, hv, gv, av):
        dhs = dy * _gelu(gv)
        return dhs, av * dhs, dy * hv * _gelu_grad(gv)

    dhs, ab, dgate = _rows("rg_gate_out_bwd", gate_bwd, [dyp, hs, (proj, D_MODEL, 0), a], [],
                           [(D_MODEL, f32), (D_MODEL, f32), (D_MODEL, bf16)])
    c = _lru_scan("lru_scan_bwd", a, ab, True)
    wa, wi, ba, bi, lam = [wts[k][j] for k in ("rg_w_a", "rg_w_i", "rg_b_a", "rg_b_i", "rg_lambda")]
    dxc, dwa, dwi, dba, dbi, dlam = _rg_gates_bwd(dhs, c, hs, xc, wa, wi, ba, bi, lam)
    dxr, dcw, dcb = _rg_conv_bwd(dxc, proj, wts["rg_conv_w"][j])
    for k, v in (("rg_w_a", dwa), ("rg_w_i", dwi), ("rg_b_a", dba), ("rg_b_i", dbi), ("rg_lambda", dlam),
                 ("rg_conv_w", dcw), ("rg_conv_b", dcb)):
        grads[k][j] = v
    dproj = jnp.concatenate([dgate, dxr], axis=1)
    grads["rg_w_in"][j] = _mm_wgrad_cols("rg_in_wgrad", h, dproj, grads["rg_w_in"][j], 0)
    return _mm_cols_t("rg_in_t", dproj, wts["rg_w_in"], j, f32)


def _local_step(x, target, wts, plan=None, exch=None):
    t = x.shape[0]
    d = D_MODEL
    gains = {k: wts[k] for k in ("norm_mix_pre", "norm_mix_post", "norm_ffn_pre", "norm_ffn_post")}
    gain = lambda k, l: gains[k][l:l + 1]

    saved = []
    h = _rows("norm_in", _norm_fwd, [x], [gain("norm_mix_pre", 0)], [(d, bf16)])[0]
    loss_cols = None
    for l in range(DEPTH):
        j = l // 2
        m, mix_saved = (_attn_fwd if l % 2 == 0 else _rg_fwd)(h, wts, j, plan)

        def resid_next(xv, mv, g_post, g_next):
            x1 = xv + _norm_fwd(mv, g_post)
            return x1, _norm_fwd(x1, g_next)

        x1, h2 = _rows("resid_mix", resid_next, [x, m], [gain("norm_mix_post", l), gain("norm_ffn_pre", l)],
                       [(d, f32), (d, bf16)])
        g, u, hid = _carried(plan if l == 0 else None, "ffn_up", wts, _ffn_up, h2, wts["ffn_w_gate"],
                             wts["ffn_w_up"], l)
        f = _ffn_down(hid, wts["ffn_w_down"], l)
        saved.append((x, h, m, mix_saved, x1, h2, g, u, hid, f))
        if l + 1 < DEPTH:
            x, h = _rows("resid_ffn", resid_next, [x1, f], [gain("norm_ffn_post", l), gain("norm_mix_pre", l + 1)],
                         [(d, f32), (d, bf16)])
        else:
            def resid_loss(xv, fv, tv, g_post):
                err = xv + _norm_fwd(fv, g_post) - tv
                return err * (1.0 / d), jnp.sum(err * err, axis=0, keepdims=True)

            dx, loss_cols = _rows("resid_loss", resid_loss, [x1, f, target], [gain("norm_ffn_post", l)],
                                  [(d, f32)], [((1, d), f32)])
    loss = 0.5 * jnp.sum(loss_cols) / d

    grads = {k: {} for k in SMALL_GRADS}
    for k in BIG_GRADS:
        shp = wts[k].shape
        shp = (shp[0] // 2,) + (shp[2:] if shp[1] == 1 else shp[1:])
        grads[k] = [_Fresh(shp), _Fresh(shp)]

    def norm_bwd_cast(uv, dyv, gv):
        du, dg = _norm_bwd(uv, dyv, gv)
        return du, dg

    def norm_bwd_resid(uv, dhv, dxv, gv):
        du, dg = _norm_bwd(uv, dhv, gv)
        return dxv + du, dg

    for l in reversed(range(DEPTH)):
        j = l // 2
        x_in, h, m, mix_saved, x1, h2, g, u, hid, f = saved[l]
        df, grads["norm_ffn_post"][l] = _rows("norm_ffn_post_bwd", norm_bwd_cast, [f, dx], [gain("norm_ffn_post", l)],
                                              [(d, bf16)], [((1, d), f32)])
        dg, du = _ffn_down_bwd(df, wts["ffn_w_down"], l, g, u)
        gi, ll = l // 2, l % 2
        grads["ffn_w_down"][gi] = _ffn_wgrad_down(hid, df, grads["ffn_w_down"][gi], ll)
        dh2 = _ffn_up_bwd(dg, du, wts["ffn_w_gate"], wts["ffn_w_up"], l)
        grads["ffn_w_gate"][gi], grads["ffn_w_up"][gi] = _ffn_wgrad_up(
            h2, dg, du, grads["ffn_w_gate"][gi], grads["ffn_w_up"][gi], ll)
        dx1, grads["norm_ffn_pre"][l] = _rows("norm_ffn_pre_bwd", norm_bwd_resid, [x1, dh2, dx],
                                              [gain("norm_ffn_pre", l)], [(d, f32)], [((1, d), f32)])
        dm, grads["norm_mix_post"][l] = _rows("norm_mix_post_bwd", norm_bwd_cast, [m, dx1], [gain("norm_mix_post", l)],
                                              [(d, bf16)], [((1, d), f32)])
        dh = (_attn_bwd if l % 2 == 0 else _rg_bwd)(dm, h, mix_saved, wts, j, grads, exch)
        dx, grads["norm_mix_pre"][l] = _rows("norm_mix_pre_bwd", norm_bwd_resid, [x_in, dh, dx1],
                                             [gain("norm_mix_pre", l)], [(d, f32)], [((1, d), f32)])
        if exch is not None and l == DEPTH // 2:
            exch.upper(grads)
    return loss, dx, grads


ANY = pl.BlockSpec(memory_space=pl.ANY)
PACK_COLS = 1024
SMALL_ROWS = 288


def _mesh_pos():
    x, y, c = lax.axis_index("x"), lax.axis_index("y"), lax.axis_index("c")
    return x, y, c, [(1 - x, y), (x, 1 - y), (1 - x, 1 - y)]


def _run_copies(copies):
    for cp in copies:
        cp.start()
    for cp in copies:
        cp.wait()


GATHER_SEMS = 7


def _gather_copies(items, ins, outs, send, recv):
    x, y, c, chips = _mesh_pos()
    q = 2 * x + y
    sibling = (x, y, 1 - c)

    def copy(k, src, dst, to):
        return pltpu.make_async_remote_copy(src_ref=src, dst_ref=dst, send_sem=send.at[k], recv_sem=recv.at[k],
                                            device_id=to, device_id_type=MESH)

    own, sent, passed = [], [], []
    for i, (t, l0, nl) in enumerate(items):
        lay = pl.ds(l0, nl)
        half = ins[t].shape[1] // 2
        rows = pl.ds(pl.multiple_of(c * half, half), half)
        own.append(copy(GATHER_SEMS * i, ins[t].at[lay], outs[t].at[lay, q], sibling))
        for j, (px, py) in enumerate(chips):
            sent.append(copy(GATHER_SEMS * i + 1 + j, ins[t].at[lay, rows], outs[t].at[lay, q, rows], (px, py, c)))
            landed = outs[t].at[lay, 2 * px + py, rows]
            passed.append(copy(GATHER_SEMS * i + 4 + j, landed, landed, sibling))
    return own, sent, passed


def _gather_start(items, ins, outs, send, recv):
    own, sent, _ = _gather_copies(items, ins, outs, send, recv)
    for cp in own + sent:
        cp.start()


def _gather_finish(items, ins, outs, send, recv):
    own, sent, passed = _gather_copies(items, ins, outs, send, recv)
    for arrived, forward in zip(sent, passed):
        arrived.wait_recv()
        forward.start()
    for cp in sent:
        cp.wait_send()
    for cp in own + passed:
        cp.wait()


def _gather_call(items, shards):
    n = len(shards)
    nsem = GATHER_SEMS * len(items)

    def body(*refs):
        ins, outs = refs[:n], refs[n:2 * n]
        _gather_start(items, ins, outs, *refs[2 * n:])
        _gather_finish(items, ins, outs, *refs[2 * n:])

    return pl.pallas_call(
        body, name="weight_all_gather", in_specs=[ANY] * n, out_specs=[ANY] * n,
        out_shape=[SDS((s.shape[0], N_CHIPS) + s.shape[1:], s.dtype) for s in shards],
        scratch_shapes=[pltpu.SemaphoreType.DMA((nsem,)), pltpu.SemaphoreType.DMA((nsem,))])(*shards)


def _call(body, operands, *, name, grid, in_specs, out_specs, out_shape, sem, scratch=(), gather=None):
    if gather is None:
        return pl.pallas_call(body, grid=grid, in_specs=in_specs, out_specs=out_specs, out_shape=out_shape,
                              scratch_shapes=list(scratch), name=name, compiler_params=_cparams(sem))(*operands), None
    start, finish, c_ins, c_io, c_new, nsem = gather
    n_in, n_out, n_scr = len(operands), len(out_shape), len(scratch)
    ni, nio, nco = len(c_ins), len(c_io), len(c_io) + len(c_new)

    def full(*refs):
        ins, sh = refs[:n_in], refs[n_in:n_in + ni]
        outs = refs[n_in + ni + nio:n_in + ni + nio + n_out]
        co = refs[n_in + ni + nio + n_out:n_in + ni + nio + n_out + nco]
        scr = refs[n_in + ni + nio + n_out + nco:]
        ids = [pl.program_id(a) for a in range(len(grid))]
        first = functools.reduce(jnp.logical_and, [i == 0 for i in ids])
        last = functools.reduce(jnp.logical_and, [i == g - 1 for i, g in zip(ids, grid)])

        @pl.when(first)
        def _():
            start(sh, co, scr[n_scr], scr[n_scr + 1])

        body(*ins, *outs, *scr[:n_scr])

        @pl.when(last)
        def _():
            finish(sh, co, scr[n_scr], scr[n_scr + 1])

    res = pl.pallas_call(
        full, grid=grid, in_specs=list(in_specs) + [ANY] * (ni + nio), out_specs=list(out_specs) + [ANY] * nco,
        out_shape=list(out_shape) + [SDS(g.shape, g.dtype) for g in list(c_io) + list(c_new)],
        scratch_shapes=list(scratch) + [pltpu.SemaphoreType.DMA((nsem,)), pltpu.SemaphoreType.DMA((nsem,))],
        input_output_aliases={n_in + ni + t: n_out + t for t in range(nio)}, name=name,
        compiler_params=_cparams(("arbitrary",) * len(grid)))(*operands, *c_ins, *c_io)
    return res[:n_out], res[n_out:]


def _pair_exchange(gs):
    n = len(gs)

    def body(*refs):
        ins, outs = refs[:n], refs[n:2 * n]
        send, recv = refs[2 * n:]
        x, y, c, _ = _mesh_pos()
        copies = []
        for t in range(n):
            half = ins[t].shape[2] // 2
            src = ins[t].at[:, :, pl.ds(pl.multiple_of((1 - c) * half, SUBLANES), half)]
            copies.append(pltpu.make_async_remote_copy(
                src_ref=src, dst_ref=outs[t], send_sem=send.at[t], recv_sem=recv.at[t],
                device_id=(x, y, 1 - c), device_id_type=MESH))
        _run_copies(copies)

    return pl.pallas_call(
        body, name="grad_pair_exchange", in_specs=[ANY] * n, out_specs=[ANY] * n,
        out_shape=[SDS(g.shape[:2] + (g.shape[2] // 2, g.shape[3]), f32) for g in gs],
        scratch_shapes=[pltpu.SemaphoreType.DMA((n,)), pltpu.SemaphoreType.DMA((n,))])(*gs)


def _pair_sum(name, g, got, c):
    l, s, r, cols = g.shape

    def body(c_ref, a_ref, b_ref, o_ref):
        o_ref[...] = (a_ref[...] + b_ref[...]).astype(bf16)

    blk = (None, None, r // 2, cols)
    return pl.pallas_call(
        body, name=name, out_shape=SDS(got.shape, bf16),
        grid_spec=pltpu.PrefetchScalarGridSpec(
            num_scalar_prefetch=1, grid=(l, s),
            in_specs=[pl.BlockSpec(blk, lambda i, q, c_ref: (i, q, c_ref[0], 0)),
                      pl.BlockSpec(blk, lambda i, q, c_ref: (i, q, 0, 0))],
            out_specs=pl.BlockSpec(blk, lambda i, q, c_ref: (i, q, 0, 0))),
        compiler_params=_cparams(("parallel", "parallel")))(c, g, got)


def _chip_exchange(hs):
    n = len(hs)

    def body(*refs):
        _chip_copies(refs[:n], refs[n:2 * n], *refs[2 * n:], start=True)
        _chip_copies(refs[:n], refs[n:2 * n], *refs[2 * n:], start=False)

    return pl.pallas_call(
        body, name="grad_chip_exchange", in_specs=[ANY] * n, out_specs=[ANY] * n,
        out_shape=[SDS(h.shape, h.dtype) for h in hs],
        scratch_shapes=[pltpu.SemaphoreType.DMA((3 * n,)), pltpu.SemaphoreType.DMA((3 * n,))])(*hs)


def _chip_copies(ins, outs, send, recv, start):
    x, y, c, chips = _mesh_pos()
    q = 2 * x + y
    for t in range(len(ins)):
        for j, (px, py) in enumerate(chips):
            cp = pltpu.make_async_remote_copy(
                src_ref=ins[t].at[:, 2 * px + py], dst_ref=outs[t].at[:, q], send_sem=send.at[3 * t + j],
                recv_sem=recv.at[3 * t + j], device_id=(px, py, c), device_id_type=MESH)
            cp.start() if start else cp.wait()


def _chip_carry(hs):
    return (functools.partial(_chip_copies, start=True), functools.partial(_chip_copies, start=False),
            hs, [], [SDS(h.shape, h.dtype) for h in hs], 3 * len(hs))


def _chip_sum(name, s, h, pos, l0, layers, into):
    l, _, r, cols = s.shape

    def body(pos_ref, s0, s1, s2, s3, own_ref, *rest):
        vals = [jnp.where(pos_ref[0] == p, own_ref[...], ref[...]).astype(f32) for p, ref in enumerate((s0, s1, s2, s3))]
        rest[-1][...] = ((vals[0] + vals[1]) + vals[2]) + vals[3]

    blk = (None, None, r, cols)
    slot = lambda p: pl.BlockSpec(blk, lambda i, pos_ref: (i, jnp.where(pos_ref[0] == p, (p + 1) % N_CHIPS, p), 0, 0))
    extra, alias = ([], {}) if into is None else ([into], {6: 0})
    return pl.pallas_call(
        body, name=name, out_shape=SDS((layers, 2 * r, cols), f32), input_output_aliases=alias,
        grid_spec=pltpu.PrefetchScalarGridSpec(
            num_scalar_prefetch=1, grid=(l,),
            in_specs=[slot(p) for p in range(N_CHIPS)] + [pl.BlockSpec(blk, lambda i, pos_ref: (i, pos_ref[0], 0, 0))]
            + [ANY] * len(extra),
            out_specs=pl.BlockSpec((None, r, cols), lambda i, pos_ref: (l0 + i, pos_ref[1], 0))),
        compiler_params=_cparams(("parallel",)))(pos, s, s, s, s, h, *extra)


def _pair_gather(fulls):
    n = len(fulls)

    def body(*refs):
        ins, outs = refs[:n], refs[n:2 * n]
        send, recv = refs[2 * n:]
        x, y, c, _ = _mesh_pos()
        copies = []
        for t in range(n):
            half = outs[t].shape[1] // 2
            rows = outs[t].at[:, pl.ds(pl.multiple_of(c * half, SUBLANES), half)]
            copies.append(pltpu.make_async_remote_copy(
                src_ref=rows, dst_ref=rows, send_sem=send.at[t], recv_sem=recv.at[t],
                device_id=(x, y, 1 - c), device_id_type=MESH))
        _run_copies(copies)

    return pl.pallas_call(
        body, name="grad_pair_gather", in_specs=[ANY] * n, out_specs=[ANY] * n,
        out_shape=[SDS(f.shape, f32) for f in fulls], input_output_aliases={t: t for t in range(n)},
        scratch_shapes=[pltpu.SemaphoreType.DMA((n,)), pltpu.SemaphoreType.DMA((n,))])(*fulls)


COL_SHARDED = ("attn_w_in", "rg_w_in", "ffn_w_gate", "ffn_w_up")
ROW_SHARDED = ("attn_w_out", "rg_w_out")
GATES = ("rg_w_a", "rg_w_i")
VECTORS = ("rg_conv_w", "rg_conv_b", "rg_b_a", "rg_b_i", "rg_lambda")
REPLICATED = ("norm_mix_pre", "norm_mix_post", "norm_ffn_pre", "norm_ffn_post", "attn_rel_bias")
BIG_GRADS = COL_SHARDED + ROW_SHARDED + ("ffn_w_down",)
SMALL_GRADS = GATES + VECTORS + REPLICATED
WEIGHTS =("attn_w_in", "attn_rel_bias", "attn_w_out", "rg_w_in", "rg_conv_w", "rg_conv_b", "rg_w_a", "rg_b_a",
           "rg_w_i", "rg_b_i", "rg_lambda", "rg_w_out", "norm_mix_pre", "norm_mix_post", "norm_ffn_pre",
           "norm_ffn_post", "ffn_w_gate", "ffn_w_up", "ffn_w_down")
SMALL = VECTORS + REPLICATED


GATHER_PARTS = {
    "first": (("attn_w_in", 0, 1), ("attn_w_out", 0, 1), ("rg_w_a", 0, 8), ("rg_w_i", 0, 8), ("vec", 0, 1)),
    "chunk_attn_fwd": (("ffn_w_gate", 0, 1), ("ffn_w_up", 0, 1), ("ffn_w_down", 0, 1), ("rg_w_in", 0, 1),
                       ("rg_w_out", 0, 1)),
    "sb_attn_fwd": (("ffn_w_gate", 1, 3), ("ffn_w_up", 1, 3), ("ffn_w_down", 1, 3), ("attn_w_in", 1, 1),
                    ("attn_w_out", 1, 1)),
    "ffn_up": (("rg_w_in", 1, 1), ("rg_w_out", 1, 1)),
}


TRANSPOSED = ("ffn_w_gate", "ffn_w_up")


def _natural(name, a):
    return jnp.swapaxes(a, 1, 2) if name in TRANSPOSED else a


class _WeightGather:
    def __init__(self, w):
        self.w = w
        self.names = list(COL_SHARDED + ROW_SHARDED + GATES + ("ffn_w_down", "vec"))
        self.shards = {}
        for k in self.names[:-1]:
            a = _natural(k, w[k]).astype(bf16)
            self.shards[k] = a.reshape((-1,) + a.shape[-2:])
        self.shards["vec"] = jnp.concatenate([w[k].reshape(-1) for k in VECTORS]).reshape(1, -1, LANES)
        got = _gather_call(self._items("first", self.names), [self.shards[k] for k in self.names])
        self.raw = dict(zip(self.names, got))

    @staticmethod
    def _items(part, names):
        return [(names.index(k), l0, nl) for k, l0, nl in GATHER_PARTS[part]]

    def part(self, part):
        names = list(dict.fromkeys(k for k, _, _ in GATHER_PARTS[part]))
        items = self._items(part, names)
        return (functools.partial(_gather_start, items), functools.partial(_gather_finish, items),
                [self.shards[k] for k in names], [self.raw[k] for k in names], [], GATHER_SEMS * len(items)), names

    def views(self):
        got, w = self.raw, self.w
        out = {k: w[k] for k in REPLICATED}
        for k in COL_SHARDED + ("ffn_w_down",):
            out[k] = got[k]
        for k in ROW_SHARDED:
            l, s, ks, n = got[k].shape
            out[k] = got[k].reshape(l, 1, s * ks, n)
        for k in GATES:
            out[k] = got[k].reshape(2, LRU_BLOCKS, LRU_BW, LRU_BW)
        vec = got["vec"].reshape(N_CHIPS, -1)
        off = 0
        for k in VECTORS:
            shp = w[k].shape
            n = int(np.prod(shp))
            piece = vec[:, off:off + n].reshape((N_CHIPS,) + shp)
            off += n
            if k == "rg_conv_w":
                out[k] = piece.reshape(N_CHIPS, 2, 4, 256).transpose(1, 2, 0, 3).reshape(2, 4, D_MODEL)
            elif k in ("rg_b_a", "rg_b_i"):
                out[k] = piece.transpose(1, 2, 0, 3).reshape(2, 1, D_MODEL)
            else:
                out[k] = piece.transpose(1, 0, 2).reshape(2, 1, D_MODEL)
        return out


def _carried(plan, part, wts, fn, *args):
    if plan is None:
        return fn(*args, None)[0]
    gather, names = plan.part(part)
    out, new = fn(*args, gather)
    plan.raw.update(zip(names, new))
    wts.update(plan.views())
    return out


def _grad_blocks(name, g):
    st = jnp.stack([g[i] for i in sorted(g)])
    if name in GATES:
        st = st.reshape(2, LRU_BLOCKS, N_CHIPS, LRU_BW // N_CHIPS, LRU_BW).transpose(2, 0, 1, 3, 4)
    elif name == "rg_conv_w":
        st = st.reshape(2, 4, N_CHIPS, -1).transpose(2, 0, 1, 3)
    elif name in ("rg_b_a", "rg_b_i"):
        st = st.reshape(2, LRU_BLOCKS, N_CHIPS, -1).transpose(2, 0, 1, 3)
    elif name in VECTORS:
        st = st.reshape(2, N_CHIPS, -1).transpose(1, 0, 2)
    else:
        st = jnp.broadcast_to(st.reshape(1, -1), (N_CHIPS, st.size))
    return st.reshape(N_CHIPS, -1)


class _GradExchange:
    def __init__(self):
        self.c = lax.axis_index("c").astype(jnp.int32).reshape(1)
        self.pos = jnp.stack([2 * lax.axis_index("x") + lax.axis_index("y"), lax.axis_index("c")]).astype(jnp.int32)
        self.parts_up = self.slots_up = None

    @staticmethod
    def _blocked(g):
        if g.ndim == 3:
            g = g.reshape(g.shape[0], N_CHIPS, g.shape[1] // N_CHIPS, g.shape[2])
        return g

    def _pair(self, tag, names, gs):
        return [_pair_sum("grad_pair_sum_" + tag + k, g, r, self.c) for k, g, r in zip(names, gs, _pair_exchange(gs))]

    def upper(self, grads):
        self.parts_up = self._pair("up_", BIG_GRADS, [self._blocked(grads[k][1]) for k in BIG_GRADS])

    def carry(self):
        return _chip_carry(self.parts_up)

    def carried(self, slots):
        self.slots_up = slots

    def finish(self, grads, shard_shapes):
        blocks = [_grad_blocks(k, grads[k]) for k in SMALL_GRADS]
        used = sum(b.shape[1] for b in blocks)
        small = jnp.concatenate(blocks + [jnp.zeros((N_CHIPS, SMALL_ROWS * PACK_COLS - used), f32)], axis=1)
        names = BIG_GRADS + ("small",)
        gs = [self._blocked(grads[k][0]) for k in BIG_GRADS] + [small.reshape(1, N_CHIPS, SMALL_ROWS, PACK_COLS)]
        parts = self._pair("lo_", names, gs)
        slots = _chip_exchange(parts)
        if self.parts_up is None:
            self.upper(grads)
        if self.slots_up is None:
            self.slots_up = _chip_exchange(self.parts_up)
        fulls = []
        for i, k in enumerate(names):
            into, nl = None, parts[i].shape[0]
            if k != "small":
                into = _chip_sum("grad_chip_sum_up_" + k, self.slots_up[i], self.parts_up[i], self.pos, nl, 2 * nl, None)
            fulls.append(_chip_sum("grad_chip_sum_lo_" + k, slots[i], parts[i], self.pos, 0,
                                   2 * nl if k != "small" else nl, into))
        full = _pair_gather(fulls)
        out = {k: f.reshape(shard_shapes[k]) for k, f in zip(BIG_GRADS, full)}
        flat, off = full[-1].reshape(-1), 0
        for k in SMALL_GRADS:
            n = int(np.prod(shard_shapes[k]))
            out[k] = flat[off:off + n].reshape(shard_shapes[k])
            off += n
        return out


def _adamw_fn(w, g, m, v):
    m = ADAM_B1 * m + (1.0 - ADAM_B1) * g
    v = ADAM_B2 * v + (1.0 - ADAM_B2) * (g * g)
    m_hat = m / (1.0 - ADAM_B1 ** ADAM_STEP)
    v_hat = v / (1.0 - ADAM_B2 ** ADAM_STEP)
    return -ADAM_LR * (m_hat / (jnp.sqrt(v_hat) + ADAM_EPS) + ADAM_WD * w), m, v


def _adamw(name, w, g, m, v):
    shp = w.shape
    if w.size >= 1 << 16:
        width = shp[-1]
        ops = [a.reshape(-1, width) for a in (w, g, m, v)]
        res = _rows(name, _adamw_fn, ops, [], [(width, f32)] * 3)
        return [r.reshape(shp) for r in res]
    n = w.size
    rows = -(-n // (SUBLANES * LANES)) * SUBLANES
    ops = [jnp.pad(a.reshape(-1), (0, rows * LANES - n)).reshape(rows, LANES) for a in (w, g, m, v)]
    res = _rows(name, _adamw_fn, ops, [], [(LANES, f32)] * 3, tr=rows)
    return [r.reshape(-1)[:n].reshape(shp) for r in res]


def kernel(x, attn_w_in, attn_rel_bias, attn_w_out, rg_w_in, rg_conv_w, rg_conv_b, rg_w_a, rg_b_a, rg_w_i, rg_b_i, rg_lambda, rg_w_out, norm_mix_pre, norm_mix_post, norm_ffn_pre, norm_ffn_post, ffn_w_gate, ffn_w_up, ffn_w_down, loss_target, m_attn_w_in, m_attn_rel_bias, m_attn_w_out, m_rg_w_in, m_rg_conv_w, m_rg_conv_b, m_rg_w_a, m_rg_b_a, m_rg_w_i, m_rg_b_i, m_rg_lambda, m_rg_w_out, m_norm_mix_pre, m_norm_mix_post, m_norm_ffn_pre, m_norm_ffn_post, m_ffn_w_gate, m_ffn_w_up, m_ffn_w_down, v_attn_w_in, v_attn_rel_bias, v_attn_w_out, v_rg_w_in, v_rg_conv_w, v_rg_conv_b, v_rg_w_a, v_rg_b_a, v_rg_w_i, v_rg_b_i, v_rg_lambda, v_rg_w_out, v_norm_mix_pre, v_norm_mix_post, v_norm_ffn_pre, v_norm_ffn_post, v_ffn_w_gate, v_ffn_w_up, v_ffn_w_down):
    w = dict(zip(WEIGHTS, (attn_w_in, attn_rel_bias, attn_w_out, rg_w_in, rg_conv_w, rg_conv_b, rg_w_a, rg_b_a, rg_w_i,
                           rg_b_i, rg_lambda, rg_w_out, norm_mix_pre, norm_mix_post, norm_ffn_pre, norm_ffn_post,
                           ffn_w_gate, ffn_w_up, ffn_w_down)))
    m = dict(zip(WEIGHTS, (m_attn_w_in, m_attn_rel_bias, m_attn_w_out, m_rg_w_in, m_rg_conv_w, m_rg_conv_b, m_rg_w_a,
                           m_rg_b_a, m_rg_w_i, m_rg_b_i, m_rg_lambda, m_rg_w_out, m_norm_mix_pre, m_norm_mix_post,
                           m_norm_ffn_pre, m_norm_ffn_post, m_ffn_w_gate, m_ffn_w_up, m_ffn_w_down)))
    v = dict(zip(WEIGHTS, (v_attn_w_in, v_attn_rel_bias, v_attn_w_out, v_rg_w_in, v_rg_conv_w, v_rg_conv_b, v_rg_w_a,
                           v_rg_b_a, v_rg_w_i, v_rg_b_i, v_rg_lambda, v_rg_w_out, v_norm_mix_pre, v_norm_mix_post,
                           v_norm_ffn_pre, v_norm_ffn_post, v_ffn_w_gate, v_ffn_w_up, v_ffn_w_down)))
    plan = _WeightGather(w)
    exch = _GradExchange()
    loss, dx, grads = _local_step(x[0], loss_target[0], plan.views(), plan, exch)
    loss = lax.psum(loss, ("x", "y", "c"))
    g = exch.finish(grads, {k: _natural(k, w[k]).shape for k in WEIGHTS})

    big = [k for k in WEIGHTS if k not in SMALL]
    upd = {}
    for k in big:
        res = _adamw("adamw_" + k, _natural(k, w[k]), g[k], _natural(k, m[k]), _natural(k, v[k]))
        upd[k] = [_natural(k, r) for r in res]
        g[k] = _natural(k, g[k])
    cat = lambda d: jnp.concatenate([d[k].reshape(-1) for k in SMALL])
    small = _adamw("adamw_small", cat(w), cat(g), cat(m), cat(v))
    off = 0
    for k in SMALL:
        n = w[k].size
        upd[k] = [r[off:off + n].reshape(w[k].shape) for r in small]
        off += n
    return (loss, dx[None], *[g[k] for k in WEIGHTS], *[upd[k][0] for k in WEIGHTS],
            *[upd[k][1] for k in WEIGHTS], *[upd[k][2] for k in WEIGHTS])
```

```python
import functools

import numpy as np
import jax
import jax.numpy as jnp
from jax import lax
from jax.experimental import pallas as pl
from jax.experimental.pallas import tpu as pltpu

f32 = jnp.float32
bf16 = jnp.bfloat16
SDS = jax.ShapeDtypeStruct
MESH = pl.DeviceIdType.MESH

D_MODEL = 1024
N_CHIPS = 4
DEPTH = 4
HEAD_DIM = 64
CHUNK = 64
N_LEFT = 8
REL_CLIP = 256
A_W = 512
LRU_BLOCKS = 4
LRU_BW = 256
LRU_C = 8.0
D_FF = 2816
RMS_EPS = 1e-6
LANES = 128
SUBLANES = 8
VMEM_LIMIT = 56 * 1024 * 1024

QB_A = 2 * CHUNK
QSUB_A = 4
KW_A = QB_A + N_LEFT * CHUNK
PAD_A = N_LEFT * CHUNK
EXT_A = 768
SB_BLK = 256
QSUB_B = 2
SB_DEAD = -110.0

ADAM_LR, ADAM_B1, ADAM_B2, ADAM_EPS, ADAM_WD, ADAM_STEP = 0.001, 0.9, 0.999, 1e-08, 0.01, 10


def _cparams(sem):
    return pltpu.CompilerParams(dimension_semantics=sem, vmem_limit_bytes=VMEM_LIMIT)


def _gemm(name, operands, in_specs, o_spec, out_shape, grid, dims, acc_shape, into=None):
    nred = grid[2]
    npair = len(operands) // 2
    nin = 2 * npair + (into is not None)

    def body(*refs):
        o_ref = refs[nin]
        p = None
        for t in range(npair):
            d = lax.dot_general(refs[2 * t][...], refs[2 * t + 1][...], (dims, ((), ())),
                                preferred_element_type=f32)
            p = d if p is None else p + d
        if nred == 1:
            o_ref[...] = p.astype(o_ref.dtype)
        else:
            acc = refs[nin + 1]
            r = pl.program_id(2)

            @pl.when(r == 0)
            def _():
                acc[...] = p

            @pl.when(r > 0)
            def _():
                acc[...] += p

            @pl.when(r == nred - 1)
            def _():
                o_ref[...] = acc[...].astype(o_ref.dtype)

    scratch = [] if nred == 1 else [pltpu.VMEM(acc_shape, f32)]
    extra, alias = ([], {}) if into is None else ([into], {2 * npair: 0})
    return pl.pallas_call(
        body, grid=grid, in_specs=list(in_specs) + [pl.BlockSpec(memory_space=pl.ANY)] * len(extra),
        out_specs=o_spec, out_shape=out_shape, scratch_shapes=scratch, name=name, input_output_aliases=alias,
        compiler_params=_cparams(("parallel", "parallel", "arbitrary")))(*operands, *extra)


LOWER_LAYERS = {"attn_w_in": 1, "attn_w_out": 1, "rg_w_in": 0, "rg_w_out": 0,
                "ffn_w_gate": 1, "ffn_w_up": 1, "ffn_w_down": 1}


def _grad_slot(name, l):
    n = LOWER_LAYERS[name]
    return (0, l) if l < n else (1, l - n)


class _Fresh:
    def __init__(self, shape):
        self.shape = tuple(shape)


def _into(buf):
    return None if isinstance(buf, _Fresh) else buf


NN = ((1,), (0,))
NT = ((1,), (1,))
TN = ((0,), (0,))


def _tile(t, want=1024):
    return min(want, t)


def _mm_cols(name, a, w, l, out_dtype):
    t, k = a.shape
    _, s, _, ns = w.shape
    tm = _tile(t)
    return _gemm(
        name, [a, w],
        [pl.BlockSpec((tm, k), lambda i, j, r: (i, 0)),
         pl.BlockSpec((None, None, k, ns), lambda i, j, r: (l, j, 0, 0))],
        pl.BlockSpec((tm, ns), lambda i, j, r: (i, j)),
        SDS((t, s * ns), out_dtype), (t // tm, s, 1), NN, None)


def _mm_cols_t(name, dy, w, l, out_dtype):
    t = dy.shape[0]
    _, s, k, ns = w.shape
    tm = _tile(t)
    return _gemm(
        name, [dy, w],
        [pl.BlockSpec((tm, ns), lambda i, j, r: (i, r)),
         pl.BlockSpec((None, None, k, ns), lambda i, j, r: (l, r, 0, 0))],
        pl.BlockSpec((tm, k), lambda i, j, r: (i, 0)),
        SDS((t, k), out_dtype), (t // tm, 1, s), NT, (tm, k))


def _mm_wgrad_cols(name, a, dy, buf, l):
    t, k = a.shape
    _, s, _, ns = buf.shape
    tt = _tile(t)
    return _gemm(
        name, [a, dy],
        [pl.BlockSpec((tt, k), lambda i, j, r: (r, 0)),
         pl.BlockSpec((tt, ns), lambda i, j, r: (r, i))],
        pl.BlockSpec((None, None, k, ns), lambda i, j, r: (l, i, 0, 0)),
        SDS(buf.shape, f32), (s, 1, t // tt), TN, (k, ns), into=_into(buf))


def _mm_rows(name, parts, w, l, out_dtype):
    t = parts[0].shape[0]
    n = w.shape[3]
    tm = _tile(t)
    ops, specs = [], []
    for p_i, a in enumerate(parts):
        kp = a.shape[1]
        ops += [a, w]
        specs += [pl.BlockSpec((tm, kp), lambda i, j, r: (i, 0)),
                  pl.BlockSpec((None, None, kp, n), lambda i, j, r, p_i=p_i: (l, 0, p_i, 0))]
    return _gemm(name, ops, specs, pl.BlockSpec((tm, n), lambda i, j, r: (i, 0)),
                 SDS((t, n), out_dtype), (t // tm, 1, 1), NN, None)


def _mm_rows_t(name, dy, w, l, out_dtype):
    t, n = dy.shape
    k = w.shape[2]
    tm = _tile(t)
    return _gemm(
        name, [dy, w],
        [pl.BlockSpec((tm, n), lambda i, j, r: (i, 0)),
         pl.BlockSpec((None, None, k, n), lambda i, j, r: (l, 0, 0, 0))],
        pl.BlockSpec((tm, k), lambda i, j, r: (i, 0)),
        SDS((t, k), out_dtype), (t // tm, 1, 1), NT, None)


def _mm_wgrad(name, a, dy, buf, l, part=0):
    t, k = a.shape
    n = dy.shape[1]
    tt = _tile(t)
    return _gemm(
        name, [a, dy],
        [pl.BlockSpec((tt, k), lambda i, j, r: (r, 0)),
         pl.BlockSpec((tt, n), lambda i, j, r: (r, 0))],
        pl.BlockSpec((None, k, n), lambda i, j, r: (l, part, 0)),
        SDS(buf.shape, f32), (1, 1, t // tt), TN, (k, n), into=_into(buf))


def _ffn_up(h, wg, wu, l, gather):
    t, k = h.shape
    s, fs = wg.shape[1], wg.shape[2]
    tm = _tile(t)

    def body(h_ref, wg_ref, wu_ref, g_ref, u_ref, hid_ref):
        hv = h_ref[...]
        g = lax.dot_general(hv, wg_ref[...], (NT, ((), ())), preferred_element_type=f32)
        u = lax.dot_general(hv, wu_ref[...], (NT, ((), ())), preferred_element_type=f32)
        g_ref[...] = g.astype(bf16)
        u_ref[...] = u.astype(bf16)
        hid_ref[...] = (g * jax.nn.sigmoid(g) * u).astype(bf16)

    wspec = pl.BlockSpec((None, None, fs, k), lambda j, i: (l, j, 0, 0))
    ospec = pl.BlockSpec((None, tm, fs), lambda j, i: (j, i, 0))
    return _call(
        body, [h, wg, wu], grid=(s, t // tm), name="ffn_up",
        in_specs=[pl.BlockSpec((tm, k), lambda j, i: (i, 0)), wspec, wspec],
        out_specs=[ospec, ospec, ospec], out_shape=[SDS((s, t, fs), bf16)] * 3,
        sem=("parallel", "parallel"), gather=gather)


def _ffn_down(hid, wd, l):
    s, t, fs = hid.shape
    n = wd.shape[3]
    tm = _tile(t, 512)
    ops, specs = [], []
    for r in range(s):
        ops += [hid, wd]
        specs += [pl.BlockSpec((None, tm, fs), lambda i, j, k, r=r: (r, i, 0)),
                  pl.BlockSpec((None, None, fs, n), lambda i, j, k, r=r: (l, r, 0, 0))]
    return _gemm("ffn_down", ops, specs, pl.BlockSpec((tm, n), lambda i, j, k: (i, 0)),
                 SDS((t, n), f32), (t // tm, 1, 1), NN, None)


def _ffn_down_bwd(df, wd, l, g, u, gather):
    t, n = df.shape
    s, fs = wd.shape[1], wd.shape[2]
    tm = _tile(t)

    def body(df_ref, wd_ref, g_ref, u_ref, dg_ref, du_ref):
        dh = lax.dot_general(df_ref[...], wd_ref[...], (NT, ((), ())), preferred_element_type=f32)
        gv = g_ref[...].astype(f32)
        uv = u_ref[...].astype(f32)
        sg = jax.nn.sigmoid(gv)
        du_ref[...] = (dh * gv * sg).astype(bf16)
        dg_ref[...] = (dh * uv * (sg * (1.0 + gv * (1.0 - sg)))).astype(bf16)

    bspec = pl.BlockSpec((None, tm, fs), lambda j, i: (j, i, 0))
    return _call(
        body, [df, wd, g, u], grid=(s, t // tm), name="ffn_down_bwd",
        in_specs=[pl.BlockSpec((tm, n), lambda j, i: (i, 0)),
                  pl.BlockSpec((None, None, fs, n), lambda j, i: (l, j, 0, 0)), bspec, bspec],
        out_specs=[bspec, bspec], out_shape=[SDS((s, t, fs), bf16)] * 2,
        sem=("parallel", "parallel"), gather=gather)


def _ffn_up_bwd(dg, du, wg, wu, l):
    s, t, fs = dg.shape
    k = wg.shape[3]
    tm = _tile(t, 512)
    ops, specs = [], []
    for r in range(s):
        aspec = pl.BlockSpec((None, tm, fs), lambda i, j, kk, r=r: (r, i, 0))
        wspec = pl.BlockSpec((None, None, fs, k), lambda i, j, kk, r=r: (l, r, 0, 0))
        ops += [dg, wg, du, wu]
        specs += [aspec, wspec, aspec, wspec]
    return _gemm("ffn_up_bwd", ops, specs, pl.BlockSpec((tm, k), lambda i, j, kk: (i, 0)),
                 SDS((t, k), f32), (t // tm, 1, 1), NN, None)


def _ffn_wgrad_up(h, dg, du, buf_g, buf_u, l):
    t, k = h.shape
    s, _, fs = dg.shape
    tt = _tile(t)
    nred = t // tt

    fresh = isinstance(buf_g, _Fresh)

    def body(*refs):
        h_ref, dg_ref, du_ref = refs[:3]
        og_ref, ou_ref, acc_g, acc_u = refs[-4:]
        r = pl.program_id(1)
        hv = h_ref[...]
        pg = lax.dot_general(dg_ref[...], hv, (TN, ((), ())), preferred_element_type=f32)
        pu = lax.dot_general(du_ref[...], hv, (TN, ((), ())), preferred_element_type=f32)

        @pl.when(r == 0)
        def _():
            acc_g[...] = pg
            acc_u[...] = pu

        @pl.when(r > 0)
        def _():
            acc_g[...] += pg
            acc_u[...] += pu

        @pl.when(r == nred - 1)
        def _():
            og_ref[...] = acc_g[...]
            ou_ref[...] = acc_u[...]

    dspec = pl.BlockSpec((None, tt, fs), lambda i, r: (i, r, 0))
    ospec = pl.BlockSpec((None, None, fs, k), lambda i, r: (l, i, 0, 0))
    extra, alias = ([], {}) if fresh else ([buf_g, buf_u], {3: 0, 4: 1})
    return pl.pallas_call(
        body, grid=(s, nred), name="ffn_wgrad_up",
        in_specs=[pl.BlockSpec((tt, k), lambda i, r: (r, 0)), dspec, dspec] + [ANY] * len(extra),
        out_specs=[ospec, ospec], out_shape=[SDS(buf_g.shape, f32), SDS(buf_u.shape, f32)],
        scratch_shapes=[pltpu.VMEM((fs, k), f32)] * 2, input_output_aliases=alias,
        compiler_params=_cparams(("parallel", "arbitrary")))(h, dg, du, *extra)


def _ffn_wgrad_down(hid, df, buf, l):
    s, t, fs = hid.shape
    n = df.shape[1]
    tt = _tile(t)
    return _gemm(
        "ffn_wgrad_down", [hid, df],
        [pl.BlockSpec((None, tt, fs), lambda i, j, r: (i, r, 0)),
         pl.BlockSpec((tt, n), lambda i, j, r: (r, 0))],
        pl.BlockSpec((None, None, fs, n), lambda i, j, r: (l, i, 0, 0)),
        SDS(buf.shape, f32), (s, 1, t // tt), TN, (fs, n), into=_into(buf))


def _rows(name, fn, rows, consts, row_outs, acc_outs=(), tr=512):
    rows = [r if isinstance(r, tuple) else (r, r.shape[1], 0) for r in rows]
    t = rows[0][0].shape[0]
    tr = max(d for d in range(SUBLANES, min(tr, t) + 1, SUBLANES) if t % d == 0)
    nin = len(rows) + len(consts)
    no, na = len(row_outs), len(acc_outs)

    def body(*refs):
        vals = fn(*[r[...] for r in refs[:nin]])
        if not isinstance(vals, (tuple, list)):
            vals = (vals,)
        for k in range(no):
            refs[nin + k][...] = vals[k].astype(refs[nin + k].dtype)
        first = pl.program_id(0) == 0
        for k in range(na):
            ref, val = refs[nin + no + k], vals[no + k]

            @pl.when(first)
            def _(ref=ref, val=val):
                ref[...] = val

            @pl.when(jnp.logical_not(first))
            def _(ref=ref, val=val):
                ref[...] += val

    in_specs = [pl.BlockSpec((tr, w), lambda i, cb=cb: (i, cb)) for (_, w, cb) in rows]
    in_specs += [pl.BlockSpec(c.shape, lambda i, nd=c.ndim: (0,) * nd) for c in consts]
    out_specs = [pl.BlockSpec((tr, w), lambda i: (i, 0)) for (w, _) in row_outs]
    out_specs += [pl.BlockSpec(s, lambda i, nd=len(s): (0,) * nd) for (s, _) in acc_outs]
    out_shape = [SDS((t, w), dt) for (w, dt) in row_outs] + [SDS(s, dt) for (s, dt) in acc_outs]
    res = pl.pallas_call(
        body, grid=(t // tr,), in_specs=in_specs, out_specs=out_specs, out_shape=out_shape,
        name=name, compiler_params=_cparams(("arbitrary",)))(*[r[0] for r in rows], *consts)
    return res


def _rstd(x):
    return lax.rsqrt(jnp.mean(x * x, axis=-1, keepdims=True) + RMS_EPS)


def _norm_fwd(x, g):
    return x * _rstd(x) * g


def _norm_bwd(u, dy, g):
    r = _rstd(u)
    n = u * r
    dn = dy * g
    du = r * (dn - n * jnp.mean(dn * n, axis=-1, keepdims=True))
    return du, jnp.sum(dy * n, axis=0, keepdims=True)


def _gelu(x):
    c = 0.7978845608028654
    return 0.5 * x * (1.0 + jnp.tanh(c * (x + 0.044715 * x * x * x)))


def _gelu_grad(x):
    c = 0.7978845608028654
    th = jnp.tanh(c * (x + 0.044715 * x * x * x))
    return 0.5 * (1.0 + th) + 0.5 * x * (1.0 - th * th) * c * (1.0 + 3.0 * 0.044715 * x * x)


def _mask_heads(x):
    lane = lax.broadcasted_iota(jnp.int32, x.shape, 1)
    return [jnp.where((lane >= h * HEAD_DIM) & (lane < (h + 1) * HEAD_DIM), x, jnp.zeros_like(x))
            for h in range(LANES // HEAD_DIM)]


def _chunk_valid(start):
    qi = lax.broadcasted_iota(jnp.int32, (QB_A, KW_A), 0)
    kj = lax.broadcasted_iota(jnp.int32, (QB_A, KW_A), 1)
    qc = qi // CHUNK
    kc = kj // CHUNK
    return (kc >= qc) & (kc <= qc + N_LEFT) & (kj + start >= PAD_A)


def _chunk_probs(q, k, bias, valid):
    s = lax.dot_general(q, k, (NT, ((), ())), preferred_element_type=f32) * (HEAD_DIM ** -0.5) + bias
    s = jnp.where(valid, s, -1e30)
    p = jnp.exp(s - jnp.max(s, axis=-1, keepdims=True))
    return p / jnp.sum(p, axis=-1, keepdims=True)


def _chunk_attn_fwd(proj, kpad, vpad, bias, gather):
    t = proj.shape[0]
    tp = kpad.shape[0]
    step = QSUB_A * QB_A

    def body(q_ref, k_ref, v_ref, b_ref, o_ref):
        for sb in range(QSUB_A):
            start = pl.multiple_of((pl.program_id(1) * QSUB_A + sb) * QB_A, QB_A)
            rows = pl.ds(sb * QB_A, QB_A)
            valid = _chunk_valid(start)
            kw = k_ref[pl.ds(start, KW_A), :]
            qm = _mask_heads(q_ref[rows, :])
            vm = _mask_heads(v_ref[pl.ds(start, KW_A), :])
            o = None
            for h in range(len(qm)):
                p = _chunk_probs(qm[h], kw, b_ref[h], valid)
                d = jnp.dot(p.astype(bf16), vm[h], preferred_element_type=f32)
                o = d if o is None else o + d
            o_ref[rows, :] = o.astype(bf16)

    kv_spec = pl.BlockSpec((tp, LANES), lambda hp, qb: (0, hp))
    outs, new = _call(
        body, [proj, kpad, vpad, bias], grid=(A_W // LANES, t // step), name="chunk_attn_fwd",
        in_specs=[pl.BlockSpec((step, LANES), lambda hp, qb: (qb, hp)), kv_spec, kv_spec,
                  pl.BlockSpec((2, QB_A, KW_A), lambda hp, qb: (hp, 0, 0))],
        out_specs=[pl.BlockSpec((step, LANES), lambda hp, qb: (qb, hp))],
        out_shape=[SDS((t, A_W), bf16)], sem=("parallel", "arbitrary"), gather=gather)
    return outs[0], new


def _chunk_attn_bwd(proj, kpad, vpad, bias, dout):
    t = proj.shape[0]
    tp = kpad.shape[0]
    step = QSUB_A * QB_A

    def body(q_ref, k_ref, v_ref, b_ref, do_ref, dq_ref, dk_ref, dv_ref, db_ref):
        qb = pl.program_id(1)

        @pl.when(qb == 0)
        def _():
            dk_ref[...] = jnp.zeros_like(dk_ref)
            dv_ref[...] = jnp.zeros_like(dv_ref)
            db_ref[...] = jnp.zeros_like(db_ref)

        for sb in range(QSUB_A):
            start = pl.multiple_of((qb * QSUB_A + sb) * QB_A, QB_A)
            rows = pl.ds(sb * QB_A, QB_A)
            win = pl.ds(start, KW_A)
            valid = _chunk_valid(start)
            kw = k_ref[win, :]
            vw = v_ref[win, :]
            qm = _mask_heads(q_ref[rows, :])
            dom = _mask_heads(do_ref[rows, :])
            km = _mask_heads(kw)
            dq = dk = dv = None
            for h in range(len(qm)):
                p = _chunk_probs(qm[h], kw, b_ref[h], valid)
                dp = lax.dot_general(dom[h], vw, (NT, ((), ())), preferred_element_type=f32)
                ds = p * (dp - jnp.sum(dp * p, axis=-1, keepdims=True))
                db_ref[h] += ds
                dsb = (ds * (HEAD_DIM ** -0.5)).astype(bf16)
                terms = (jnp.dot(dsb, km[h], preferred_element_type=f32),
                         lax.dot_general(dsb, qm[h], (TN, ((), ())), preferred_element_type=f32),
                         lax.dot_general(p.astype(bf16), dom[h], (TN, ((), ())), preferred_element_type=f32))
                dq, dk, dv = terms if dq is None else (dq + terms[0], dk + terms[1], dv + terms[2])
            dq_ref[rows, :] = dq.astype(bf16)
            dk_ref[win, :] += dk
            dv_ref[win, :] += dv

    kv_spec = pl.BlockSpec((tp, LANES), lambda hp, qb: (0, hp))
    q_spec = pl.BlockSpec((step, LANES), lambda hp, qb: (qb, hp))
    b_spec = pl.BlockSpec((2, QB_A, KW_A), lambda hp, qb: (hp, 0, 0))
    return pl.pallas_call(
        body, grid=(A_W // LANES, t // step), name="chunk_attn_bwd",
        in_specs=[q_spec, kv_spec, kv_spec, b_spec, q_spec],
        out_specs=[q_spec, kv_spec, kv_spec, b_spec],
        out_shape=[SDS((t, A_W), bf16), SDS((tp, A_W), f32), SDS((tp, A_W), f32),
                   SDS((2 * A_W // LANES, QB_A, KW_A), f32)],
        compiler_params=_cparams(("parallel", "arbitrary")))(proj, kpad, vpad, bias, dout)


def _bias_ext(table):
    flat = PAD_A + QB_A - 1 - REL_CLIP
    top = jnp.broadcast_to(table[:, 2 * REL_CLIP:], (table.shape[0], flat))
    lo = 2 * REL_CLIP - (EXT_A - 1 - flat)
    return jnp.concatenate([top, jnp.flip(table[:, lo:], axis=1)], axis=1)


def _bias_window(table):
    nh = table.shape[0]
    e = jnp.broadcast_to(_bias_ext(table)[:, None, :], (nh, QB_A, EXT_A)).reshape(nh, QB_A * EXT_A)
    m = e[:, :QB_A * (EXT_A - 1)].reshape(nh, QB_A, EXT_A - 1)
    return m[:, :, QB_A - 1:]


def _bias_window_grad(dbias):
    nh = dbias.shape[0]
    m = jnp.pad(dbias, ((0, 0), (0, 0), (QB_A - 1, 0))).reshape(nh, QB_A * (EXT_A - 1))
    dext = jnp.sum(jnp.pad(m, ((0, 0), (0, QB_A))).reshape(nh, QB_A, EXT_A), axis=1)
    flat = PAD_A + QB_A - 1 - REL_CLIP
    lo = 2 * REL_CLIP - (EXT_A - 1 - flat)
    tail = jnp.flip(dext[:, flat:], axis=1)
    tail = tail.at[:, -1].add(jnp.sum(dext[:, :flat], axis=1))
    return jnp.pad(tail, ((0, 0), (lo, 0)))


def _tri_suffix(x, tri):
    hi = x.astype(bf16)
    lo = (x - hi.astype(f32)).astype(bf16)
    return jnp.dot(hi, tri, preferred_element_type=f32) + jnp.dot(lo, tri, preferred_element_type=f32)


def _sb_block(q, k, run, tri, causal):
    z = lax.dot_general(q, k, (NT, ((), ())), preferred_element_type=f32) * (HEAD_DIM ** -0.5)
    e = jnp.exp(-jnp.abs(z))
    l1p = jnp.log(1.0 + e)
    lb = jnp.minimum(z, 0.0) - l1p
    lmb = lb - z
    if causal is not None:
        lmb = jnp.where(causal, lmb, 0.0)
    cs = _tri_suffix(lmb, tri)
    w = jnp.exp(lb + (run + cs - lmb))
    if causal is not None:
        w = jnp.where(causal, w, 0.0)
    return z, e, w, run + cs[:, 0:1]


def _sb_tri():
    r = lax.broadcasted_iota(jnp.int32, (SB_BLK, SB_BLK), 0)
    c = lax.broadcasted_iota(jnp.int32, (SB_BLK, SB_BLK), 1)
    return (r >= c).astype(bf16), c < r


def _sb_live(runs):
    m = runs[0]
    for r in runs[1:]:
        m = jnp.maximum(m, r)
    return jnp.max(m) > SB_DEAD


def _sb_fwd(proj, gather):
    t = proj.shape[0]
    cb = A_W // LANES
    nh = LANES // HEAD_DIM

    step_rows = QSUB_B * SB_BLK

    def body(q_ref, k_ref, v_ref, o_ref, of_ref):
        tri, diag = _sb_tri()
        for sb in range(QSUB_B):
            _sb_fwd_block(pl.program_id(1) * QSUB_B + sb, pl.ds(sb * SB_BLK, SB_BLK), tri, diag,
                          q_ref, k_ref, v_ref, o_ref, of_ref)

    def _sb_fwd_block(qb, qrows, tri, diag, q_ref, k_ref, v_ref, o_ref, of_ref):
        qm = _mask_heads(q_ref[qrows, :])

        def pair(kb, carry, causal):
            rows = pl.ds(pl.multiple_of(kb * SB_BLK, SB_BLK), SB_BLK)
            k = k_ref[rows, :]
            vm = _mask_heads(v_ref[rows, :])
            runs, acc = [], carry[nh]
            for h in range(nh):
                _, _, w, run = _sb_block(qm[h], k, carry[h], tri, causal)
                acc = acc + jnp.dot(w.astype(bf16), vm[h], preferred_element_type=f32)
                runs.append(run)
            return (*runs, acc)

        zero = jnp.zeros((SB_BLK, 1), f32)
        carry = pair(qb, (zero,) * nh + (jnp.zeros((SB_BLK, LANES), f32),), diag)

        def cond(st):
            return (st[0] < qb) & _sb_live(st[1][:nh])

        def step(st):
            return st[0] + 1, pair(qb - 1 - st[0], st[1], None)

        _, carry = lax.while_loop(cond, step, (jnp.int32(0), carry))
        o_ref[qrows, :] = carry[nh].astype(bf16)
        of_ref[qrows, :] = carry[nh]

    ospec = pl.BlockSpec((step_rows, LANES), lambda hp, qb: (qb, hp))
    return _call(
        body, [proj, proj, proj], grid=(cb, t // step_rows), name="sb_attn_fwd",
        in_specs=[pl.BlockSpec((step_rows, LANES), lambda hp, qb: (qb, 3 * cb + hp)),
                  pl.BlockSpec((t, LANES), lambda hp, qb: (0, 4 * cb + hp)),
                  pl.BlockSpec((t, LANES), lambda hp, qb: (0, 5 * cb + hp))],
        out_specs=[ospec, ospec], out_shape=[SDS((t, A_W), bf16), SDS((t, A_W), f32)],
        sem=("parallel", "arbitrary"), gather=gather)


def _sb_bwd(proj, out_b, dout, gather):
    t = proj.shape[0]
    cb = A_W // LANES
    nh = LANES // HEAD_DIM

    step_rows = QSUB_B * SB_BLK

    def body(q_ref, k_ref, v_ref, o_ref, do_ref, dq_ref, dk_ref, dv_ref):
        tri, diag = _sb_tri()

        @pl.when(pl.program_id(1) == 0)
        def _():
            dk_ref[...] = jnp.zeros_like(dk_ref)
            dv_ref[...] = jnp.zeros_like(dv_ref)

        for sb in range(QSUB_B):
            _sb_bwd_block(pl.program_id(1) * QSUB_B + sb, pl.ds(sb * SB_BLK, SB_BLK), tri, diag,
                          q_ref, k_ref, v_ref, o_ref, do_ref, dq_ref, dk_ref, dv_ref)

    def _sb_bwd_block(qb, qrows, tri, diag, q_ref, k_ref, v_ref, o_ref, do_ref, dq_ref, dk_ref, dv_ref):
        qm = _mask_heads(q_ref[qrows, :])
        do = do_ref[qrows, :]
        dom = _mask_heads(do)
        dsums = [jnp.sum(t_, axis=-1, keepdims=True) for t_ in _mask_heads(do.astype(f32) * o_ref[qrows, :])]

        def pair(kb, carry, causal):
            rows = pl.ds(pl.multiple_of(kb * SB_BLK, SB_BLK), SB_BLK)
            k = k_ref[rows, :]
            v = v_ref[rows, :]
            km = _mask_heads(k)
            new, dq, dk, dv = [], carry[2 * nh], None, None
            for h in range(nh):
                z, e, w, run = _sb_block(qm[h], k, carry[2 * h], tri, causal)
                inv = 1.0 / (1.0 + e)
                beta = jnp.where(z >= 0.0, inv, e * inv)
                wb = w.astype(bf16)
                g = lax.dot_general(dom[h], v, (NT, ((), ())), preferred_element_type=f32) * wb.astype(f32)
                sg = _tri_suffix(g, tri)
                dz = g * (1.0 - beta) - (dsums[h] - carry[2 * h + 1] - sg) * beta
                if causal is not None:
                    dz = jnp.where(causal, dz, 0.0)
                dzb = (dz * (HEAD_DIM ** -0.5)).astype(bf16)
                dq = dq + jnp.dot(dzb, km[h], preferred_element_type=f32)
                tk = lax.dot_general(dzb, qm[h], (TN, ((), ())), preferred_element_type=f32)
                tv = lax.dot_general(wb, dom[h], (TN, ((), ())), preferred_element_type=f32)
                dk, dv = (tk, tv) if dk is None else (dk + tk, dv + tv)
                new += [run, carry[2 * h + 1] + sg[:, 0:1]]
            dk_ref[rows, :] += dk
            dv_ref[rows, :] += dv
            return (*new, dq)

        zero = jnp.zeros((SB_BLK, 1), f32)
        carry = pair(qb, (zero,) * (2 * nh) + (jnp.zeros((SB_BLK, LANES), f32),), diag)

        def cond(st):
            return (st[0] < qb) & _sb_live(st[1][0:2 * nh:2])

        def step(st):
            return st[0] + 1, pair(qb - 1 - st[0], st[1], None)

        _, carry = lax.while_loop(cond, step, (jnp.int32(0), carry))
        dq_ref[qrows, :] = carry[2 * nh].astype(bf16)

    kv_in = lambda seg: pl.BlockSpec((t, LANES), lambda hp, qb: (0, seg * cb + hp))
    q_spec = pl.BlockSpec((step_rows, LANES), lambda hp, qb: (qb, hp))
    kv_out = pl.BlockSpec((t, LANES), lambda hp, qb: (0, hp))
    return _call(
        body, [proj, proj, proj, out_b, dout], grid=(cb, t // step_rows), name="sb_attn_bwd",
        in_specs=[pl.BlockSpec((step_rows, LANES), lambda hp, qb: (qb, 3 * cb + hp)), kv_in(4), kv_in(5),
                  q_spec, pl.BlockSpec((step_rows, LANES), lambda hp, qb: (qb, cb + hp))],
        out_specs=[q_spec, kv_out, kv_out],
        out_shape=[SDS((t, A_W), bf16), SDS((t, A_W), f32), SDS((t, A_W), f32)],
        sem=("parallel", "arbitrary"), gather=gather)


def _halo_specs(tr, w, col, nblk):
    per = tr // SUBLANES
    cur = pl.BlockSpec((tr, w), lambda i: (i, col))
    prev = pl.BlockSpec((SUBLANES, w), lambda i: (jnp.maximum(i * per - 1, 0), col))
    nxt = pl.BlockSpec((SUBLANES, w), lambda i: (jnp.minimum((i + 1) * per, nblk * per - 1), col))
    return cur, prev, nxt


def _taps_before(cur, prev8, first):
    prev8 = jnp.where(first, 0.0, prev8)
    ext = jnp.concatenate([prev8, cur], axis=0)
    return [pltpu.roll(ext, s, 0)[SUBLANES:] for s in (3, 2, 1)]


def _taps_after(cur, next8, last):
    n = cur.shape[0]
    next8 = jnp.where(last, 0.0, next8)
    ext = jnp.concatenate([cur, next8], axis=0)
    return [pltpu.roll(ext, n + SUBLANES - s, 0)[:n] for s in (1, 2, 3)]


def _block_diag(x, w_ref, dims):
    outs = [lax.dot_general(x[:, n * LRU_BW:(n + 1) * LRU_BW], w_ref[n], (dims, ((), ())),
                            preferred_element_type=f32) for n in range(LRU_BLOCKS)]
    return jnp.concatenate(outs, axis=1)


def _lru_gates(xc, wa_ref, wi_ref, ba, bi, lam):
    xb = xc.astype(bf16)
    r = jax.nn.sigmoid(_block_diag(xb, wa_ref, NN) + ba)
    ig = jax.nn.sigmoid(_block_diag(xb, wi_ref, NN) + bi)
    sp = jnp.maximum(-lam, 0.0) + jnp.log(1.0 + jnp.exp(-jnp.abs(lam)))
    log_a = -LRU_C * r * sp
    a = jnp.exp(log_a)
    x2 = 2.0 * log_a
    one_minus = jnp.where(x2 > -1e-2, -x2 * (1.0 + x2 * (0.5 + x2 * (1.0 / 6.0))), 1.0 - a * a)
    mult = jnp.sqrt(one_minus)
    return xb, r, ig, sp, a, mult


def _rg_gates_fwd(proj, conv_w, conv_b, wa, wi, ba, bi, lam, tr=512):
    t = proj.shape[0]
    w = D_MODEL
    tr = min(tr, t)
    nblk = t // tr
    cur, prev, _ = _halo_specs(tr, w, 1, nblk)

    def body(x_ref, xp_ref, cw_ref, cb_ref, wa_ref, wi_ref, ba_ref, bi_ref, lam_ref, xc_ref, a_ref, u_ref):
        x = x_ref[...]
        taps = _taps_before(x, xp_ref[...], pl.program_id(0) == 0) + [x]
        xc = cb_ref[...]
        for k in range(4):
            xc = xc + cw_ref[k:k + 1, :] * taps[k]
        _, _, ig, _, a, mult = _lru_gates(xc, wa_ref, wi_ref, ba_ref[...], bi_ref[...], lam_ref[...])
        xc_ref[...] = xc
        a_ref[...] = a
        u_ref[...] = mult * (ig * xc)

    full = lambda a_: pl.BlockSpec(a_.shape, lambda i, nd=a_.ndim: (0,) * nd)
    ospec = pl.BlockSpec((tr, w), lambda i: (i, 0))
    return pl.pallas_call(
        body, grid=(nblk,), name="rg_gates_fwd",
        in_specs=[cur, prev] + [full(a_) for a_ in (conv_w, conv_b, wa, wi, ba, bi, lam)],
        out_specs=[ospec] * 3, out_shape=[SDS((t, w), f32)] * 3,
        compiler_params=_cparams(("parallel",)))(proj, proj, conv_w, conv_b, wa, wi, ba, bi, lam)


def _lru_scan(name, a, b, reverse, tt=512):
    t, w = a.shape
    tt = min(tt, t)
    nt = t // tt
    ng = tt // SUBLANES

    def body(a_ref, b_ref, h_ref, carry_ref):
        @pl.when(pl.program_id(0) == 0)
        def _():
            carry_ref[...] = jnp.zeros_like(carry_ref)

        row = lax.broadcasted_iota(jnp.int32, (SUBLANES, w), 0)

        def group(gi, carry):
            g = (ng - 1 - gi) if reverse else gi
            rows = pl.ds(pl.multiple_of(g * SUBLANES, SUBLANES), SUBLANES)
            av = a_ref[rows, :]
            bv = b_ref[rows, :]
            for s in (1, 2, 4):
                sh = (SUBLANES - s) if reverse else s
                ok = (row < SUBLANES - s) if reverse else (row >= s)
                a_s = pltpu.roll(av, sh, 0)
                b_s = pltpu.roll(bv, sh, 0)
                bv = jnp.where(ok, av * b_s + bv, bv)
                av = jnp.where(ok, av * a_s, av)
            h = av * carry + bv
            h_ref[rows, :] = h
            edge = h[0:1, :] if reverse else h[SUBLANES - 1:SUBLANES, :]
            return jnp.broadcast_to(edge, (SUBLANES, w))

        carry_ref[...] = lax.fori_loop(0, ng, group, carry_ref[...], unroll=4)

    tmap = (lambda i: (nt - 1 - i, 0)) if reverse else (lambda i: (i, 0))
    spec = pl.BlockSpec((tt, w), tmap)
    return pl.pallas_call(
        body, grid=(nt,), name=name, in_specs=[spec, spec], out_specs=spec,
        out_shape=SDS((t, w), f32), scratch_shapes=[pltpu.VMEM((SUBLANES, w), f32)],
        compiler_params=_cparams(("arbitrary",)))(a, b)


def _rg_gates_bwd(dhs, c, hs, xc, wa, wi, ba, bi, lam, tr=256):
    t, w = xc.shape
    tr = min(tr, t)
    nblk = t // tr
    cur, prev, nxt = _halo_specs(tr, w, 0, nblk)

    def body(dhs_ref, c_ref, cn_ref, hs_ref, hp_ref, xc_ref, wa_ref, wi_ref, ba_ref, bi_ref, lam_ref,
             dxc_ref, dwa_ref, dwi_ref, dba_ref, dbi_ref, dlam_ref):
        i = pl.program_id(0)
        c_next = _taps_after(c_ref[...], cn_ref[...], i == nblk - 1)[0]
        h_prev = _taps_before(hs_ref[...], hp_ref[...], i == 0)[2]
        xc = xc_ref[...]
        lam = lam_ref[...]
        xb, r, ig, sp, a, mult = _lru_gates(xc, wa_ref, wi_ref, ba_ref[...], bi_ref[...], lam)
        dh = dhs_ref[...] + c_next
        dlog_a = dh * h_prev * a - (dh * ig * xc) * (a * a / mult)
        dpre_a = (dlog_a * (-LRU_C * sp) * r * (1.0 - r)).astype(bf16)
        dpre_i = (dh * mult * xc * ig * (1.0 - ig)).astype(bf16)
        dxc_ref[...] = (dh * mult * ig + _block_diag(dpre_a, wa_ref, NT) + _block_diag(dpre_i, wi_ref, NT))
        dsig = 1.0 / (1.0 + jnp.exp(lam))
        sums = [jnp.sum(dpre_a.astype(f32), axis=0, keepdims=True),
                jnp.sum(dpre_i.astype(f32), axis=0, keepdims=True),
                jnp.sum(dlog_a * (-LRU_C * r), axis=0, keepdims=True) * (-dsig)]

        @pl.when(i == 0)
        def _():
            dwa_ref[...] = jnp.zeros_like(dwa_ref)
            dwi_ref[...] = jnp.zeros_like(dwi_ref)
            dba_ref[...] = jnp.zeros_like(dba_ref)
            dbi_ref[...] = jnp.zeros_like(dbi_ref)
            dlam_ref[...] = jnp.zeros_like(dlam_ref)

        for n in range(LRU_BLOCKS):
            sl = slice(n * LRU_BW, (n + 1) * LRU_BW)
            dwa_ref[n] += lax.dot_general(xb[:, sl], dpre_a[:, sl], (TN, ((), ())), preferred_element_type=f32)
            dwi_ref[n] += lax.dot_general(xb[:, sl], dpre_i[:, sl], (TN, ((), ())), preferred_element_type=f32)
        dba_ref[...] += sums[0]
        dbi_ref[...] += sums[1]
        dlam_ref[...] += sums[2]

    full = lambda a_: pl.BlockSpec(a_.shape, lambda i, nd=a_.ndim: (0,) * nd)
    vec = pl.BlockSpec((1, w), lambda i: (0, 0))
    mat = pl.BlockSpec((LRU_BLOCKS, LRU_BW, LRU_BW), lambda i: (0, 0, 0))
    return pl.pallas_call(
        body, grid=(nblk,), name="rg_gates_bwd",
        in_specs=[cur, cur, nxt, cur, prev, cur] + [full(a_) for a_ in (wa, wi, ba, bi, lam)],
        out_specs=[cur, mat, mat, vec, vec, vec],
        out_shape=[SDS((t, w), f32), SDS((LRU_BLOCKS, LRU_BW, LRU_BW), f32), SDS((LRU_BLOCKS, LRU_BW, LRU_BW), f32),
                   SDS((1, w), f32), SDS((1, w), f32), SDS((1, w), f32)],
        compiler_params=_cparams(("arbitrary",)))(dhs, c, c, hs, hs, xc, wa, wi, ba, bi, lam)


def _rg_conv_bwd(dxc, proj, conv_w, tr=512):
    t, w = dxc.shape
    tr = min(tr, t)
    nblk = t // tr
    cur, _, nxt = _halo_specs(tr, w, 0, nblk)
    xcur, xprev, _ = _halo_specs(tr, w, 1, nblk)

    def body(d_ref, dn_ref, x_ref, xp_ref, cw_ref, dx_ref, dcw_ref, dcb_ref):
        i = pl.program_id(0)
        d = d_ref[...]
        x = x_ref[...]
        after = _taps_after(d, dn_ref[...], i == nblk - 1)
        before = _taps_before(x, xp_ref[...], i == 0) + [x]
        dx = cw_ref[3:4, :] * d
        for s in (1, 2, 3):
            dx = dx + cw_ref[3 - s:4 - s, :] * after[s - 1]
        dx_ref[...] = dx.astype(bf16)
        dcw = jnp.concatenate([jnp.sum(d * before[k], axis=0, keepdims=True) for k in range(4)], axis=0)
        dcb = jnp.sum(d, axis=0, keepdims=True)

        @pl.when(i == 0)
        def _():
            dcw_ref[...] = dcw
            dcb_ref[...] = dcb

        @pl.when(i > 0)
        def _():
            dcw_ref[...] += dcw
            dcb_ref[...] += dcb

    return pl.pallas_call(
        body, grid=(nblk,), name="rg_conv_bwd",
        in_specs=[cur, nxt, xcur, xprev, pl.BlockSpec((4, w), lambda i: (0, 0))],
        out_specs=[cur, pl.BlockSpec((4, w), lambda i: (0, 0)), pl.BlockSpec((1, w), lambda i: (0, 0))],
        out_shape=[SDS((t, w), bf16), SDS((4, w), f32), SDS((1, w), f32)],
        compiler_params=_cparams(("arbitrary",)))(dxc, dxc, proj, proj, conv_w)


def _attn_fwd(h, wts, j, plan):
    proj = _mm_cols("attn_in", h, wts["attn_w_in"], j, bf16)
    kpad = jnp.pad(proj[:, A_W:2 * A_W], ((PAD_A, 0), (0, 0)))
    vpad = jnp.pad(proj[:, 2 * A_W:3 * A_W], ((PAD_A, 0), (0, 0)))
    bias = _bias_window(wts["attn_rel_bias"][j])
    plan = plan if j == 0 else None
    out_a = _carried(plan, "chunk_attn_fwd", wts, _chunk_attn_fwd, proj, kpad, vpad, bias)
    out_b, out_b32 = _carried(plan, "sb_attn_fwd", wts, _sb_fwd, proj)
    m = _mm_rows("attn_out", [out_a, out_b], wts["attn_w_out"], j, f32)
    return m, (proj, kpad, vpad, bias, out_a, out_b, out_b32)


def _attn_bwd(dm, h, saved, wts, j, grads, exch):
    proj, kpad, vpad, bias, out_a, out_b, out_b32 = saved
    dout = _mm_rows_t("attn_out_t", dm, wts["attn_w_out"], j, bf16)
    gi, ll = _grad_slot("attn_w_out", j)
    grads["attn_w_out"][gi] = _mm_wgrad("attn_out_wgrad_a", out_a, dm, grads["attn_w_out"][gi], ll, 0)
    grads["attn_w_out"][gi] = _mm_wgrad("attn_out_wgrad_b", out_b, dm, grads["attn_w_out"][gi], ll, 1)
    dqa, dka, dva, dbias = _chunk_attn_bwd(proj, kpad, vpad, bias, dout)
    if exch is not None and j == 0:
        (dqs, dks, dvs), slots = _sb_bwd(proj, out_b32, dout, exch.carry())
        exch.carried(slots)
    else:
        dqs, dks, dvs = _sb_bwd(proj, out_b32, dout, None)[0]
    grads["attn_rel_bias"][j] = _bias_window_grad(dbias)
    dproj = jnp.concatenate([dqa, dka[PAD_A:].astype(bf16), dva[PAD_A:].astype(bf16),
                             dqs, dks.astype(bf16), dvs.astype(bf16)], axis=1)
    grads["attn_w_in"][gi] = _mm_wgrad_cols("attn_in_wgrad", h, dproj, grads["attn_w_in"][gi], ll)
    return _mm_cols_t("attn_in_t", dproj, wts["attn_w_in"], j, f32)


def _rg_fwd(h, wts, j, plan):
    proj =_mm_cols("rg_in", h, wts["rg_w_in"], j, f32)
    small = [wts[k][j] for k in ("rg_conv_w", "rg_conv_b", "rg_w_a", "rg_w_i", "rg_b_a", "rg_b_i", "rg_lambda")]
    xc, a, u = _rg_gates_fwd(proj, *small)
    hs = _lru_scan("lru_scan_fwd", a, u, False)
    yp = _rows("rg_gate_out", lambda hv, gv: hv * _gelu(gv), [hs, (proj, D_MODEL, 0)], [], [(D_MODEL, bf16)])[0]
    m = _mm_rows("rg_out", [yp], wts["rg_w_out"], j, f32)
    return m, (proj, xc, a, hs, yp)


def _rg_bwd(dm, h, saved, wts, j, grads, exch):
    proj, xc, a, hs, yp = saved
    dyp = _mm_rows_t("rg_out_t", dm, wts["rg_w_out"], j, f32)
    gi, ll = _grad_slot("rg_w_out", j)
    grads["rg_w_out"][gi] = _mm_wgrad("rg_out_wgrad", yp, dm, grads["rg_w_out"][gi], ll)

    def gate_bwd(dy, hv, gv, av):
        dhs = dy * _gelu(gv)
        return dhs, av * dhs, dy * hv * _gelu_grad(gv)

    dhs, ab, dgate = _rows("rg_gate_out_bwd", gate_bwd, [dyp, hs, (proj, D_MODEL, 0), a], [],
                           [(D_MODEL, f32), (D_MODEL, f32), (D_MODEL, bf16)])
    c = _lru_scan("lru_scan_bwd", a, ab, True)
    wa, wi, ba, bi, lam = [wts[k][j] for k in ("rg_w_a", "rg_w_i", "rg_b_a", "rg_b_i", "rg_lambda")]
    dxc, dwa, dwi, dba, dbi, dlam = _rg_gates_bwd(dhs, c, hs, xc, wa, wi, ba, bi, lam)
    dxr, dcw, dcb = _rg_conv_bwd(dxc, proj, wts["rg_conv_w"][j])
    for k, v in (("rg_w_a", dwa), ("rg_w_i", dwi), ("rg_b_a", dba), ("rg_b_i", dbi), ("rg_lambda", dlam),
                 ("rg_conv_w", dcw), ("rg_conv_b", dcb)):
        grads[k][j] = v
    dproj = jnp.concatenate([dgate, dxr], axis=1)
    grads["rg_w_in"][gi] = _mm_wgrad_cols("rg_in_wgrad", h, dproj, grads["rg_w_in"][gi], ll)
    return _mm_cols_t("rg_in_t", dproj, wts["rg_w_in"], j, f32)


def _local_step(x, target, wts, plan=None, exch=None):
    t = x.shape[0]
    d = D_MODEL
    gains = {k: wts[k] for k in ("norm_mix_pre", "norm_mix_post", "norm_ffn_pre", "norm_ffn_post")}
    gain = lambda k, l: gains[k][l:l + 1]

    saved = []
    h = _rows("norm_in", _norm_fwd, [x], [gain("norm_mix_pre", 0)], [(d, bf16)])[0]
    loss_cols = None
    for l in range(DEPTH):
        j = l // 2
        m, mix_saved = (_attn_fwd if l % 2 == 0 else _rg_fwd)(h, wts, j, plan)

        def resid_next(xv, mv, g_post, g_next):
            x1 = xv + _norm_fwd(mv, g_post)
            return x1, _norm_fwd(x1, g_next)

        x1, h2 = _rows("resid_mix", resid_next, [x, m], [gain("norm_mix_post", l), gain("norm_ffn_pre", l)],
                       [(d, f32), (d, bf16)])
        g, u, hid = _carried(plan if l == 0 else None, "ffn_up", wts, _ffn_up, h2, wts["ffn_w_gate"],
                             wts["ffn_w_up"], l)
        f = _ffn_down(hid, wts["ffn_w_down"], l)
        saved.append((x, h, m, mix_saved, x1, h2, g, u, hid, f))
        if l + 1 < DEPTH:
            x, h = _rows("resid_ffn", resid_next, [x1, f], [gain("norm_ffn_post", l), gain("norm_mix_pre", l + 1)],
                         [(d, f32), (d, bf16)])
        else:
            def resid_loss(xv, fv, tv, g_post):
                err = xv + _norm_fwd(fv, g_post) - tv
                return err * (1.0 / d), jnp.sum(err * err, axis=0, keepdims=True)

            dx, loss_cols = _rows("resid_loss", resid_loss, [x1, f, target], [gain("norm_ffn_post", l)],
                                  [(d, f32)], [((1, d), f32)])
    loss = 0.5 * jnp.sum(loss_cols) / d

    grads = {k: {} for k in SMALL_GRADS}
    for k in BIG_GRADS:
        shp = wts[k].shape
        rest = shp[2:] if shp[1] == 1 else shp[1:]
        grads[k] = [_Fresh((LOWER_LAYERS[k],) + rest), _Fresh((shp[0] - LOWER_LAYERS[k],) + rest)]

    def norm_bwd_cast(uv, dyv, gv):
        du, dg = _norm_bwd(uv, dyv, gv)
        return du, dg

    def norm_bwd_resid(uv, dhv, dxv, gv):
        du, dg = _norm_bwd(uv, dhv, gv)
        return dxv + du, dg

    for l in reversed(range(DEPTH)):
        j = l // 2
        x_in, h, m, mix_saved, x1, h2, g, u, hid, f = saved[l]
        df, grads["norm_ffn_post"][l] = _rows("norm_ffn_post_bwd", norm_bwd_cast, [f, dx], [gain("norm_ffn_post", l)],
                                              [(d, bf16)], [((1, d), f32)])
        if exch is not None and l == 0:
            (dg, du), got = _ffn_down_bwd(df, wts["ffn_w_down"], l, g, u, exch.upper_carry(grads))
            exch.upper_got(got)
        else:
            dg, du = _ffn_down_bwd(df, wts["ffn_w_down"], l, g, u, None)[0]
        gi, ll = _grad_slot("ffn_w_down", l)
        grads["ffn_w_down"][gi] = _ffn_wgrad_down(hid, df, grads["ffn_w_down"][gi], ll)
        dh2 = _ffn_up_bwd(dg, du, wts["ffn_w_gate"], wts["ffn_w_up"], l)
        grads["ffn_w_gate"][gi], grads["ffn_w_up"][gi] = _ffn_wgrad_up(
            h2, dg, du, grads["ffn_w_gate"][gi], grads["ffn_w_up"][gi], ll)
        dx1, grads["norm_ffn_pre"][l] = _rows("norm_ffn_pre_bwd", norm_bwd_resid, [x1, dh2, dx],
                                              [gain("norm_ffn_pre", l)], [(d, f32)], [((1, d), f32)])
        dm, grads["norm_mix_post"][l] = _rows("norm_mix_post_bwd", norm_bwd_cast, [m, dx1], [gain("norm_mix_post", l)],
                                              [(d, bf16)], [((1, d), f32)])
        dh = (_attn_bwd if l % 2 == 0 else _rg_bwd)(dm, h, mix_saved, wts, j, grads, exch)
        dx, grads["norm_mix_pre"][l] = _rows("norm_mix_pre_bwd", norm_bwd_resid, [x_in, dh, dx1],
                                             [gain("norm_mix_pre", l)], [(d, f32)], [((1, d), f32)])
    return loss, dx, grads


ANY = pl.BlockSpec(memory_space=pl.ANY)
PACK_COLS = 1024
SMALL_ROWS = 288


def _mesh_pos():
    x, y, c = lax.axis_index("x"), lax.axis_index("y"), lax.axis_index("c")
    return x, y, c, [(1 - x, y), (x, 1 - y), (1 - x, 1 - y)]


def _run_copies(copies):
    for cp in copies:
        cp.start()
    for cp in copies:
        cp.wait()


GATHER_SEMS = 7


def _gather_copies(items, ins, outs, send, recv):
    x, y, c, chips = _mesh_pos()
    q = 2 * x + y
    sibling = (x, y, 1 - c)

    def copy(k, src, dst, to):
        return pltpu.make_async_remote_copy(src_ref=src, dst_ref=dst, send_sem=send.at[k], recv_sem=recv.at[k],
                                            device_id=to, device_id_type=MESH)

    own, sent, passed = [], [], []
    for i, (t, l0, nl) in enumerate(items):
        lay = pl.ds(l0, nl)
        half = ins[t].shape[1] // 2
        rows = pl.ds(pl.multiple_of(c * half, half), half)
        own.append(copy(GATHER_SEMS * i, ins[t].at[lay], outs[t].at[lay, q], sibling))
        for j, (px, py) in enumerate(chips):
            sent.append(copy(GATHER_SEMS * i + 1 + j, ins[t].at[lay, rows], outs[t].at[lay, q, rows], (px, py, c)))
            landed = outs[t].at[lay, 2 * px + py, rows]
            passed.append(copy(GATHER_SEMS * i + 4 + j, landed, landed, sibling))
    return own, sent, passed


def _gather_start(items, ins, outs, send, recv):
    own, sent, _ = _gather_copies(items, ins, outs, send, recv)
    for cp in own + sent:
        cp.start()


def _gather_finish(items, ins, outs, send, recv):
    own, sent, passed = _gather_copies(items, ins, outs, send, recv)
    for arrived, forward in zip(sent, passed):
        arrived.wait_recv()
        forward.start()
    for cp in sent:
        cp.wait_send()
    for cp in own + passed:
        cp.wait()


def _gather_call(items, shards):
    n = len(shards)
    nsem = GATHER_SEMS * len(items)

    def body(*refs):
        ins, outs = refs[:n], refs[n:2 * n]
        _gather_start(items, ins, outs, *refs[2 * n:])
        _gather_finish(items, ins, outs, *refs[2 * n:])

    return pl.pallas_call(
        body, name="weight_all_gather", in_specs=[ANY] * n, out_specs=[ANY] * n,
        out_shape=[SDS((s.shape[0], N_CHIPS) + s.shape[1:], s.dtype) for s in shards],
        scratch_shapes=[pltpu.SemaphoreType.DMA((nsem,)), pltpu.SemaphoreType.DMA((nsem,))])(*shards)


def _call(body, operands, *, name, grid, in_specs, out_specs, out_shape, sem, scratch=(), gather=None):
    if gather is None:
        return pl.pallas_call(body, grid=grid, in_specs=in_specs, out_specs=out_specs, out_shape=out_shape,
                              scratch_shapes=list(scratch), name=name, compiler_params=_cparams(sem))(*operands), None
    start, finish, c_ins, c_io, c_new, nsem = gather
    n_in, n_out, n_scr = len(operands), len(out_shape), len(scratch)
    ni, nio, nco = len(c_ins), len(c_io), len(c_io) + len(c_new)

    def full(*refs):
        ins, sh = refs[:n_in], refs[n_in:n_in + ni]
        outs = refs[n_in + ni + nio:n_in + ni + nio + n_out]
        co = refs[n_in + ni + nio + n_out:n_in + ni + nio + n_out + nco]
        scr = refs[n_in + ni + nio + n_out + nco:]
        ids = [pl.program_id(a) for a in range(len(grid))]
        first = functools.reduce(jnp.logical_and, [i == 0 for i in ids])
        last = functools.reduce(jnp.logical_and, [i == g - 1 for i, g in zip(ids, grid)])

        @pl.when(first)
        def _():
            start(sh, co, scr[n_scr], scr[n_scr + 1])

        body(*ins, *outs, *scr[:n_scr])

        @pl.when(last)
        def _():
            finish(sh, co, scr[n_scr], scr[n_scr + 1])

    res = pl.pallas_call(
        full, grid=grid, in_specs=list(in_specs) + [ANY] * (ni + nio), out_specs=list(out_specs) + [ANY] * nco,
        out_shape=list(out_shape) + [SDS(g.shape, g.dtype) for g in list(c_io) + list(c_new)],
        scratch_shapes=list(scratch) + [pltpu.SemaphoreType.DMA((nsem,)), pltpu.SemaphoreType.DMA((nsem,))],
        input_output_aliases={n_in + ni + t: n_out + t for t in range(nio)}, name=name,
        compiler_params=_cparams(("arbitrary",) * len(grid)))(*operands, *c_ins, *c_io)
    return res[:n_out], res[n_out:]


def _pair_exchange(gs):
    n = len(gs)

    def body(*refs):
        _pair_copies(refs[:n], refs[n:2 * n], *refs[2 * n:], start=True)
        _pair_copies(refs[:n], refs[n:2 * n], *refs[2 * n:], start=False)

    return pl.pallas_call(
        body, name="grad_pair_exchange", in_specs=[ANY] * n, out_specs=[ANY] * n,
        out_shape=_pair_shapes(gs),
        scratch_shapes=[pltpu.SemaphoreType.DMA((n,)), pltpu.SemaphoreType.DMA((n,))])(*gs)


def _pair_shapes(gs):
    return [SDS(g.shape[:2] + (g.shape[2] // 2, g.shape[3]), f32) for g in gs]


def _pair_copies(ins, outs, send, recv, start):
    x, y, c, _ = _mesh_pos()
    for t in range(len(ins)):
        half = ins[t].shape[2] // 2
        src = ins[t].at[:, :, pl.ds(pl.multiple_of((1 - c) * half, SUBLANES), half)]
        cp = pltpu.make_async_remote_copy(src_ref=src, dst_ref=outs[t], send_sem=send.at[t], recv_sem=recv.at[t],
                                          device_id=(x, y, 1 - c), device_id_type=MESH)
        cp.start() if start else cp.wait()


def _pair_carry(gs):
    return (functools.partial(_pair_copies, start=True), functools.partial(_pair_copies, start=False),
            gs, [], _pair_shapes(gs), len(gs))


def _pair_sum(name, g, got, c):
    l, s, r, cols = g.shape

    def body(c_ref, a_ref, b_ref, o_ref):
        o_ref[...] = (a_ref[...] + b_ref[...]).astype(bf16)

    blk = (None, None, r // 2, cols)
    return pl.pallas_call(
        body, name=name, out_shape=SDS(got.shape, bf16),
        grid_spec=pltpu.PrefetchScalarGridSpec(
            num_scalar_prefetch=1, grid=(l, s),
            in_specs=[pl.BlockSpec(blk, lambda i, q, c_ref: (i, q, c_ref[0], 0)),
                      pl.BlockSpec(blk, lambda i, q, c_ref: (i, q, 0, 0))],
            out_specs=pl.BlockSpec(blk, lambda i, q, c_ref: (i, q, 0, 0))),
        compiler_params=_cparams(("parallel", "parallel")))(c, g, got)


def _chip_exchange(hs):
    n = len(hs)

    def body(*refs):
        _chip_copies(refs[:n], refs[n:2 * n], *refs[2 * n:], start=True)
        _chip_copies(refs[:n], refs[n:2 * n], *refs[2 * n:], start=False)

    return pl.pallas_call(
        body, name="grad_chip_exchange", in_specs=[ANY] * n, out_specs=[ANY] * n,
        out_shape=[SDS(h.shape, h.dtype) for h in hs],
        scratch_shapes=[pltpu.SemaphoreType.DMA((3 * n,)), pltpu.SemaphoreType.DMA((3 * n,))])(*hs)


def _chip_copies(ins, outs, send, recv, start):
    x, y, c, chips = _mesh_pos()
    q = 2 * x + y
    for t in range(len(ins)):
        for j, (px, py) in enumerate(chips):
            cp = pltpu.make_async_remote_copy(
                src_ref=ins[t].at[:, 2 * px + py], dst_ref=outs[t].at[:, q], send_sem=send.at[3 * t + j],
                recv_sem=recv.at[3 * t + j], device_id=(px, py, c), device_id_type=MESH)
            cp.start() if start else cp.wait()


def _chip_carry(hs):
    return (functools.partial(_chip_copies, start=True), functools.partial(_chip_copies, start=False),
            hs, [], [SDS(h.shape, h.dtype) for h in hs], 3 * len(hs))


def _chip_sum(name, s, h, pos, l0, layers, into):
    l, _, r, cols = s.shape

    def body(pos_ref, s0, s1, s2, s3, own_ref, *rest):
        vals = [jnp.where(pos_ref[0] == p, own_ref[...], ref[...]).astype(f32) for p, ref in enumerate((s0, s1, s2, s3))]
        rest[-1][...] = ((vals[0] + vals[1]) + vals[2]) + vals[3]

    blk = (None, None, r, cols)
    slot = lambda p: pl.BlockSpec(blk, lambda i, pos_ref: (i, jnp.where(pos_ref[0] == p, (p + 1) % N_CHIPS, p), 0, 0))
    extra, alias = ([], {}) if into is None else ([into], {6: 0})
    return pl.pallas_call(
        body, name=name, out_shape=SDS((layers, 2 * r, cols), f32), input_output_aliases=alias,
        grid_spec=pltpu.PrefetchScalarGridSpec(
            num_scalar_prefetch=1, grid=(l,),
            in_specs=[slot(p) for p in range(N_CHIPS)] + [pl.BlockSpec(blk, lambda i, pos_ref: (i, pos_ref[0], 0, 0))]
            + [ANY] * len(extra),
            out_specs=pl.BlockSpec((None, r, cols), lambda i, pos_ref: (l0 + i, pos_ref[1], 0))),
        compiler_params=_cparams(("parallel",)))(pos, s, s, s, s, h, *extra)


def _pair_gather(fulls):
    n = len(fulls)

    def body(*refs):
        ins, outs = refs[:n], refs[n:2 * n]
        send, recv = refs[2 * n:]
        x, y, c, _ = _mesh_pos()
        copies = []
        for t in range(n):
            half = outs[t].shape[1] // 2
            rows = outs[t].at[:, pl.ds(pl.multiple_of(c * half, SUBLANES), half)]
            copies.append(pltpu.make_async_remote_copy(
                src_ref=rows, dst_ref=rows, send_sem=send.at[t], recv_sem=recv.at[t],
                device_id=(x, y, 1 - c), device_id_type=MESH))
        _run_copies(copies)

    return pl.pallas_call(
        body, name="grad_pair_gather", in_specs=[ANY] * n, out_specs=[ANY] * n,
        out_shape=[SDS(f.shape, f32) for f in fulls], input_output_aliases={t: t for t in range(n)},
        scratch_shapes=[pltpu.SemaphoreType.DMA((n,)), pltpu.SemaphoreType.DMA((n,))])(*fulls)


COL_SHARDED = ("attn_w_in", "rg_w_in", "ffn_w_gate", "ffn_w_up")
ROW_SHARDED = ("attn_w_out", "rg_w_out")
GATES = ("rg_w_a", "rg_w_i")
VECTORS = ("rg_conv_w", "rg_conv_b", "rg_b_a", "rg_b_i", "rg_lambda")
REPLICATED = ("norm_mix_pre", "norm_mix_post", "norm_ffn_pre", "norm_ffn_post", "attn_rel_bias")
BIG_GRADS = COL_SHARDED + ROW_SHARDED + ("ffn_w_down",)
SMALL_GRADS = GATES + VECTORS + REPLICATED
WEIGHTS =("attn_w_in", "attn_rel_bias", "attn_w_out", "rg_w_in", "rg_conv_w", "rg_conv_b", "rg_w_a", "rg_b_a",
           "rg_w_i", "rg_b_i", "rg_lambda", "rg_w_out", "norm_mix_pre", "norm_mix_post", "norm_ffn_pre",
           "norm_ffn_post", "ffn_w_gate", "ffn_w_up", "ffn_w_down")
SMALL = VECTORS + REPLICATED


GATHER_PARTS = {
    "first": (("attn_w_in", 0, 1), ("attn_w_out", 0, 1), ("rg_w_a", 0, 8), ("rg_w_i", 0, 8), ("vec", 0, 1)),
    "chunk_attn_fwd": (("ffn_w_gate", 0, 1), ("ffn_w_up", 0, 1), ("ffn_w_down", 0, 1), ("rg_w_in", 0, 1),
                       ("rg_w_out", 0, 1)),
    "sb_attn_fwd": (("ffn_w_gate", 1, 3), ("ffn_w_up", 1, 3), ("ffn_w_down", 1, 3), ("attn_w_in", 1, 1),
                    ("attn_w_out", 1, 1)),
    "ffn_up": (("rg_w_in", 1, 1), ("rg_w_out", 1, 1)),
}


TRANSPOSED = ("ffn_w_gate", "ffn_w_up")


def _natural(name, a):
    return jnp.swapaxes(a, 1, 2) if name in TRANSPOSED else a


class _WeightGather:
    def __init__(self, w):
        self.w = w
        self.names = list(COL_SHARDED + ROW_SHARDED + GATES + ("ffn_w_down", "vec"))
        self.shards = {}
        for k in self.names[:-1]:
            a = _natural(k, w[k]).astype(bf16)
            self.shards[k] = a.reshape((-1,) + a.shape[-2:])
        self.shards["vec"] = jnp.concatenate([w[k].reshape(-1) for k in VECTORS]).reshape(1, -1, LANES)
        got = _gather_call(self._items("first", self.names), [self.shards[k] for k in self.names])
        self.raw = dict(zip(self.names, got))

    @staticmethod
    def _items(part, names):
        return [(names.index(k), l0, nl) for k, l0, nl in GATHER_PARTS[part]]

    def part(self, part):
        names = list(dict.fromkeys(k for k, _, _ in GATHER_PARTS[part]))
        items = self._items(part, names)
        return (functools.partial(_gather_start, items), functools.partial(_gather_finish, items),
                [self.shards[k] for k in names], [self.raw[k] for k in names], [], GATHER_SEMS * len(items)), names

    def views(self):
        got, w = self.raw, self.w
        out = {k: w[k] for k in REPLICATED}
        for k in COL_SHARDED + ("ffn_w_down",):
            out[k] = got[k]
        for k in ROW_SHARDED:
            l, s, ks, n = got[k].shape
            out[k] = got[k].reshape(l, 1, s * ks, n)
        for k in GATES:
            out[k] = got[k].reshape(2, LRU_BLOCKS, LRU_BW, LRU_BW)
        vec = got["vec"].reshape(N_CHIPS, -1)
        off = 0
        for k in VECTORS:
            shp = w[k].shape
            n = int(np.prod(shp))
            piece = vec[:, off:off + n].reshape((N_CHIPS,) + shp)
            off += n
            if k == "rg_conv_w":
                out[k] = piece.reshape(N_CHIPS, 2, 4, 256).transpose(1, 2, 0, 3).reshape(2, 4, D_MODEL)
            elif k in ("rg_b_a", "rg_b_i"):
                out[k] = piece.transpose(1, 2, 0, 3).reshape(2, 1, D_MODEL)
            else:
                out[k] = piece.transpose(1, 0, 2).reshape(2, 1, D_MODEL)
        return out


def _carried(plan, part, wts, fn, *args):
    if plan is None:
        return fn(*args, None)[0]
    gather, names = plan.part(part)
    out, new = fn(*args, gather)
    plan.raw.update(zip(names, new))
    wts.update(plan.views())
    return out


def _grad_blocks(name, g):
    st = jnp.stack([g[i] for i in sorted(g)])
    if name in GATES:
        st = st.reshape(2, LRU_BLOCKS, N_CHIPS, LRU_BW // N_CHIPS, LRU_BW).transpose(2, 0, 1, 3, 4)
    elif name == "rg_conv_w":
        st = st.reshape(2, 4, N_CHIPS, -1).transpose(2, 0, 1, 3)
    elif name in ("rg_b_a", "rg_b_i"):
        st = st.reshape(2, LRU_BLOCKS, N_CHIPS, -1).transpose(2, 0, 1, 3)
    elif name in VECTORS:
        st = st.reshape(2, N_CHIPS, -1).transpose(1, 0, 2)
    else:
        st = jnp.broadcast_to(st.reshape(1, -1), (N_CHIPS, st.size))
    return st.reshape(N_CHIPS, -1)


class _GradExchange:
    def __init__(self):
        self.c = lax.axis_index("c").astype(jnp.int32).reshape(1)
        self.pos = jnp.stack([2 * lax.axis_index("x") + lax.axis_index("y"), lax.axis_index("c")]).astype(jnp.int32)
        self.up = self.got_up = self.parts_up = self.slots_up = None

    @staticmethod
    def _blocked(g):
        if g.ndim == 3:
            g = g.reshape(g.shape[0], N_CHIPS, g.shape[1] // N_CHIPS, g.shape[2])
        return g

    def _sums(self, tag, names, gs, got):
        return [_pair_sum("grad_pair_sum_" + tag + k, g, r, self.c) for k, g, r in zip(names, gs, got)]

    def upper_carry(self, grads):
        self.up = [self._blocked(grads[k][1]) for k in BIG_GRADS]
        return _pair_carry(self.up)

    def upper_got(self, got):
        self.got_up = got

    def carry(self):
        self.parts_up = self._sums("up_", BIG_GRADS, self.up, self.got_up)
        return _chip_carry(self.parts_up)

    def carried(self, slots):
        self.slots_up = slots

    def finish(self, grads, shard_shapes):
        if self.got_up is None:
            self.upper_carry(grads)
            self.got_up = _pair_exchange(self.up)
        if self.slots_up is None:
            self.carry()
            self.slots_up = _chip_exchange(self.parts_up)
        blocks = [_grad_blocks(k, grads[k]) for k in SMALL_GRADS]
        used = sum(b.shape[1] for b in blocks)
        small = jnp.concatenate(blocks + [jnp.zeros((N_CHIPS, SMALL_ROWS * PACK_COLS - used), f32)], axis=1)
        names = tuple(k for k in BIG_GRADS if LOWER_LAYERS[k]) + ("small",)
        gs = [self._blocked(grads[k][0]) for k in names[:-1]] + [small.reshape(1, N_CHIPS, SMALL_ROWS, PACK_COLS)]
        parts = dict(zip(names, self._sums("lo_", names, gs, _pair_exchange(gs))))
        slots = dict(zip(names, _chip_exchange([parts[k] for k in names])))
        fulls = []
        for i, k in enumerate(BIG_GRADS):
            nlo, nup = LOWER_LAYERS[k], self.parts_up[i].shape[0]
            full = _chip_sum("grad_chip_sum_up_" + k, self.slots_up[i], self.parts_up[i], self.pos, nlo, nlo + nup, None)
            if nlo:
                full = _chip_sum("grad_chip_sum_lo_" + k, slots[k], parts[k], self.pos, 0, nlo + nup, full)
            fulls.append(full)
        fulls.append(_chip_sum("grad_chip_sum_lo_small", slots["small"], parts["small"], self.pos, 0, 1, None))
        full = _pair_gather(fulls)
        out = {k: f.reshape(shard_shapes[k]) for k, f in zip(BIG_GRADS, full)}
        flat, off = full[-1].reshape(-1), 0
        for k in SMALL_GRADS:
            n = int(np.prod(shard_shapes[k]))
            out[k] = flat[off:off + n].reshape(shard_shapes[k])
            off += n
        return out


def _adamw_fn(w, g, m, v):
    m = ADAM_B1 * m + (1.0 - ADAM_B1) * g
    v = ADAM_B2 * v + (1.0 - ADAM_B2) * (g * g)
    m_hat = m / (1.0 - ADAM_B1 ** ADAM_STEP)
    v_hat = v / (1.0 - ADAM_B2 ** ADAM_STEP)
    return -ADAM_LR * (m_hat / (jnp.sqrt(v_hat) + ADAM_EPS) + ADAM_WD * w), m, v


def _adamw(name, w, g, m, v):
    shp = w.shape
    if w.size >= 1 << 16:
        width = shp[-1]
        ops = [a.reshape(-1, width) for a in (w, g, m, v)]
        res = _rows(name, _adamw_fn, ops, [], [(width, f32)] * 3)
        return [r.reshape(shp) for r in res]
    n = w.size
    rows = -(-n // (SUBLANES * LANES)) * SUBLANES
    ops = [jnp.pad(a.reshape(-1), (0, rows * LANES - n)).reshape(rows, LANES) for a in (w, g, m, v)]
    res = _rows(name, _adamw_fn, ops, [], [(LANES, f32)] * 3, tr=rows)
    return [r.reshape(-1)[:n].reshape(shp) for r in res]


def kernel(x, attn_w_in, attn_rel_bias, attn_w_out, rg_w_in, rg_conv_w, rg_conv_b, rg_w_a, rg_b_a, rg_w_i, rg_b_i, rg_lambda, rg_w_out, norm_mix_pre, norm_mix_post, norm_ffn_pre, norm_ffn_post, ffn_w_gate, ffn_w_up, ffn_w_down, loss_target, m_attn_w_in, m_attn_rel_bias, m_attn_w_out, m_rg_w_in, m_rg_conv_w, m_rg_conv_b, m_rg_w_a, m_rg_b_a, m_rg_w_i, m_rg_b_i, m_rg_lambda, m_rg_w_out, m_norm_mix_pre, m_norm_mix_post, m_norm_ffn_pre, m_norm_ffn_post, m_ffn_w_gate, m_ffn_w_up, m_ffn_w_down, v_attn_w_in, v_attn_rel_bias, v_attn_w_out, v_rg_w_in, v_rg_conv_w, v_rg_conv_b, v_rg_w_a, v_rg_b_a, v_rg_w_i, v_rg_b_i, v_rg_lambda, v_rg_w_out, v_norm_mix_pre, v_norm_mix_post, v_norm_ffn_pre, v_norm_ffn_post, v_ffn_w_gate, v_ffn_w_up, v_ffn_w_down):
    w = dict(zip(WEIGHTS, (attn_w_in, attn_rel_bias, attn_w_out, rg_w_in, rg_conv_w, rg_conv_b, rg_w_a, rg_b_a, rg_w_i,
                           rg_b_i, rg_lambda, rg_w_out, norm_mix_pre, norm_mix_post, norm_ffn_pre, norm_ffn_post,
                           ffn_w_gate, ffn_w_up, ffn_w_down)))
    m = dict(zip(WEIGHTS, (m_attn_w_in, m_attn_rel_bias, m_attn_w_out, m_rg_w_in, m_rg_conv_w, m_rg_conv_b, m_rg_w_a,
                           m_rg_b_a, m_rg_w_i, m_rg_b_i, m_rg_lambda, m_rg_w_out, m_norm_mix_pre, m_norm_mix_post,
                           m_norm_ffn_pre, m_norm_ffn_post, m_ffn_w_gate, m_ffn_w_up, m_ffn_w_down)))
    v = dict(zip(WEIGHTS, (v_attn_w_in, v_attn_rel_bias, v_attn_w_out, v_rg_w_in, v_rg_conv_w, v_rg_conv_b, v_rg_w_a,
                           v_rg_b_a, v_rg_w_i, v_rg_b_i, v_rg_lambda, v_rg_w_out, v_norm_mix_pre, v_norm_mix_post,
                           v_norm_ffn_pre, v_norm_ffn_post, v_ffn_w_gate, v_ffn_w_up, v_ffn_w_down)))
    plan = _WeightGather(w)
    exch = _GradExchange()
    loss, dx, grads = _local_step(x[0], loss_target[0], plan.views(), plan, exch)
    loss = lax.psum(loss, ("x", "y", "c"))
    g = exch.finish(grads, {k: _natural(k, w[k]).shape for k in WEIGHTS})

    big = [k for k in WEIGHTS if k not in SMALL]
    upd = {}
    for k in big:
        res = _adamw("adamw_" + k, _natural(k, w[k]), g[k], _natural(k, m[k]), _natural(k, v[k]))
        upd[k] = [_natural(k, r) for r in res]
        g[k] = _natural(k, g[k])
    cat = lambda d: jnp.concatenate([d[k].reshape(-1) for k in SMALL])
    small = _adamw("adamw_small", cat(w), cat(g), cat(m), cat(v))
    off = 0
    for k in SMALL:
        n = w[k].size
        upd[k] = [r[off:off + n].reshape(w[k].shape) for r in small]
        off += n
    return (loss, dx[None], *[g[k] for k in WEIGHTS], *[upd[k][0] for k in WEIGHTS],
            *[upd[k][1] for k in WEIGHTS], *[upd[k][2] for k in WEIGHTS])
```

```python
import functools

import numpy as np
import jax
import jax.numpy as jnp
from jax import lax
from jax.experimental import pallas as pl
from jax.experimental.pallas import tpu as pltpu

f32 = jnp.float32
bf16 = jnp.bfloat16
SDS = jax.ShapeDtypeStruct
MESH = pl.DeviceIdType.MESH

D_MODEL = 1024
N_CHIPS = 4
DEPTH = 4
HEAD_DIM = 64
CHUNK = 64
N_LEFT = 8
REL_CLIP = 256
A_W = 512
LRU_BLOCKS = 4
LRU_BW = 256
LRU_C = 8.0
D_FF = 2816
RMS_EPS = 1e-6
LANES = 128
SUBLANES = 8
VMEM_LIMIT = 56 * 1024 * 1024

QB_A = 2 * CHUNK
QSUB_A = 4
KW_A = QB_A + N_LEFT * CHUNK
PAD_A = N_LEFT * CHUNK
EXT_A = 768
SB_BLK = 256
QSUB_B = 2
SB_DEAD = -110.0

ADAM_LR, ADAM_B1, ADAM_B2, ADAM_EPS, ADAM_WD, ADAM_STEP = 0.001, 0.9, 0.999, 1e-08, 0.01, 10


def _cparams(sem):
    return pltpu.CompilerParams(dimension_semantics=sem, vmem_limit_bytes=VMEM_LIMIT)


def _gemm(name, operands, in_specs, o_spec, out_shape, grid, dims, acc_shape, into=None):
    nred = grid[2]
    npair = len(operands) // 2
    nin = 2 * npair + (into is not None)

    def body(*refs):
        o_ref = refs[nin]
        p = None
        for t in range(npair):
            d = lax.dot_general(refs[2 * t][...], refs[2 * t + 1][...], (dims, ((), ())),
                                preferred_element_type=f32)
            p = d if p is None else p + d
        if nred == 1:
            o_ref[...] = p.astype(o_ref.dtype)
        else:
            acc = refs[nin + 1]
            r = pl.program_id(2)

            @pl.when(r == 0)
            def _():
                acc[...] = p

            @pl.when(r > 0)
            def _():
                acc[...] += p

            @pl.when(r == nred - 1)
            def _():
                o_ref[...] = acc[...].astype(o_ref.dtype)

    scratch = [] if nred == 1 else [pltpu.VMEM(acc_shape, f32)]
    extra, alias = ([], {}) if into is None else ([into], {2 * npair: 0})
    return pl.pallas_call(
        body, grid=grid, in_specs=list(in_specs) + [pl.BlockSpec(memory_space=pl.ANY)] * len(extra),
        out_specs=o_spec, out_shape=out_shape, scratch_shapes=scratch, name=name, input_output_aliases=alias,
        compiler_params=_cparams(("parallel", "parallel", "arbitrary")))(*operands, *extra)


LOWER_LAYERS = {"attn_w_in": 1, "attn_w_out": 1, "rg_w_in": 0, "rg_w_out": 0,
                "ffn_w_gate": 0, "ffn_w_up": 0, "ffn_w_down": 0}


def _grad_slot(name, l):
    n = LOWER_LAYERS[name]
    return (0, l) if l < n else (1, l - n)


class _Fresh:
    def __init__(self, shape):
        self.shape = tuple(shape)


def _into(buf):
    return None if isinstance(buf, _Fresh) else buf


NN = ((1,), (0,))
NT = ((1,), (1,))
TN = ((0,), (0,))


def _tile(t, want=1024):
    return min(want, t)


def _mm_cols(name, a, w, l, out_dtype):
    t, k = a.shape
    _, s, _, ns = w.shape
    tm = _tile(t)
    return _gemm(
        name, [a, w],
        [pl.BlockSpec((tm, k), lambda i, j, r: (i, 0)),
         pl.BlockSpec((None, None, k, ns), lambda i, j, r: (l, j, 0, 0))],
        pl.BlockSpec((tm, ns), lambda i, j, r: (i, j)),
        SDS((t, s * ns), out_dtype), (t // tm, s, 1), NN, None)


def _mm_cols_t(name, dy, w, l, out_dtype):
    t = dy.shape[0]
    _, s, k, ns = w.shape
    tm = _tile(t)
    return _gemm(
        name, [dy, w],
        [pl.BlockSpec((tm, ns), lambda i, j, r: (i, r)),
         pl.BlockSpec((None, None, k, ns), lambda i, j, r: (l, r, 0, 0))],
        pl.BlockSpec((tm, k), lambda i, j, r: (i, 0)),
        SDS((t, k), out_dtype), (t // tm, 1, s), NT, (tm, k))


def _mm_wgrad_cols(name, a, dy, buf, l):
    t, k = a.shape
    _, s, _, ns = buf.shape
    tt = _tile(t)
    return _gemm(
        name, [a, dy],
        [pl.BlockSpec((tt, k), lambda i, j, r: (r, 0)),
         pl.BlockSpec((tt, ns), lambda i, j, r: (r, i))],
        pl.BlockSpec((None, None, k, ns), lambda i, j, r: (l, i, 0, 0)),
        SDS(buf.shape, f32), (s, 1, t // tt), TN, (k, ns), into=_into(buf))


def _mm_rows(name, parts, w, l, out_dtype):
    t = parts[0].shape[0]
    n = w.shape[3]
    tm = _tile(t)
    ops, specs = [], []
    for p_i, a in enumerate(parts):
        kp = a.shape[1]
        ops += [a, w]
        specs += [pl.BlockSpec((tm, kp), lambda i, j, r: (i, 0)),
                  pl.BlockSpec((None, None, kp, n), lambda i, j, r, p_i=p_i: (l, 0, p_i, 0))]
    return _gemm(name, ops, specs, pl.BlockSpec((tm, n), lambda i, j, r: (i, 0)),
                 SDS((t, n), out_dtype), (t // tm, 1, 1), NN, None)


def _mm_rows_t(name, dy, w, l, out_dtype):
    t, n = dy.shape
    k = w.shape[2]
    tm = _tile(t)
    return _gemm(
        name, [dy, w],
        [pl.BlockSpec((tm, n), lambda i, j, r: (i, 0)),
         pl.BlockSpec((None, None, k, n), lambda i, j, r: (l, 0, 0, 0))],
        pl.BlockSpec((tm, k), lambda i, j, r: (i, 0)),
        SDS((t, k), out_dtype), (t // tm, 1, 1), NT, None)


def _mm_wgrad(name, a, dy, buf, l, part=0):
    t, k = a.shape
    n = dy.shape[1]
    tt = _tile(t)
    return _gemm(
        name, [a, dy],
        [pl.BlockSpec((tt, k), lambda i, j, r: (r, 0)),
         pl.BlockSpec((tt, n), lambda i, j, r: (r, 0))],
        pl.BlockSpec((None, k, n), lambda i, j, r: (l, part, 0)),
        SDS(buf.shape, f32), (1, 1, t // tt), TN, (k, n), into=_into(buf))


def _ffn_up(h, wg, wu, l, gather):
    t, k = h.shape
    s, fs = wg.shape[1], wg.shape[2]
    tm = _tile(t)

    def body(h_ref, wg_ref, wu_ref, g_ref, u_ref, hid_ref):
        hv = h_ref[...]
        g = lax.dot_general(hv, wg_ref[...], (NT, ((), ())), preferred_element_type=f32)
        u = lax.dot_general(hv, wu_ref[...], (NT, ((), ())), preferred_element_type=f32)
        g_ref[...] = g.astype(bf16)
        u_ref[...] = u.astype(bf16)
        hid_ref[...] = (g * jax.nn.sigmoid(g) * u).astype(bf16)

    wspec = pl.BlockSpec((None, None, fs, k), lambda j, i: (l, j, 0, 0))
    ospec = pl.BlockSpec((None, tm, fs), lambda j, i: (j, i, 0))
    return _call(
        body, [h, wg, wu], grid=(s, t // tm), name="ffn_up",
        in_specs=[pl.BlockSpec((tm, k), lambda j, i: (i, 0)), wspec, wspec],
        out_specs=[ospec, ospec, ospec], out_shape=[SDS((s, t, fs), bf16)] * 3,
        sem=("parallel", "parallel"), gather=gather)


def _ffn_down(hid, wd, l):
    s, t, fs = hid.shape
    n = wd.shape[3]
    tm = _tile(t, 512)
    ops, specs = [], []
    for r in range(s):
        ops += [hid, wd]
        specs += [pl.BlockSpec((None, tm, fs), lambda i, j, k, r=r: (r, i, 0)),
                  pl.BlockSpec((None, None, fs, n), lambda i, j, k, r=r: (l, r, 0, 0))]
    return _gemm("ffn_down", ops, specs, pl.BlockSpec((tm, n), lambda i, j, k: (i, 0)),
                 SDS((t, n), f32), (t // tm, 1, 1), NN, None)


def _ffn_down_bwd(df, wd, l, g, u, gather):
    t, n = df.shape
    s, fs = wd.shape[1], wd.shape[2]
    tm = _tile(t)

    def body(df_ref, wd_ref, g_ref, u_ref, dg_ref, du_ref):
        dh = lax.dot_general(df_ref[...], wd_ref[...], (NT, ((), ())), preferred_element_type=f32)
        gv = g_ref[...].astype(f32)
        uv = u_ref[...].astype(f32)
        sg = jax.nn.sigmoid(gv)
        du_ref[...] = (dh * gv * sg).astype(bf16)
        dg_ref[...] = (dh * uv * (sg * (1.0 + gv * (1.0 - sg)))).astype(bf16)

    bspec = pl.BlockSpec((None, tm, fs), lambda j, i: (j, i, 0))
    return _call(
        body, [df, wd, g, u], grid=(s, t // tm), name="ffn_down_bwd",
        in_specs=[pl.BlockSpec((tm, n), lambda j, i: (i, 0)),
                  pl.BlockSpec((None, None, fs, n), lambda j, i: (l, j, 0, 0)), bspec, bspec],
        out_specs=[bspec, bspec], out_shape=[SDS((s, t, fs), bf16)] * 2,
        sem=("parallel", "parallel"), gather=gather)


def _ffn_up_bwd(dg, du, wg, wu, l):
    s, t, fs = dg.shape
    k = wg.shape[3]
    tm = _tile(t, 512)
    ops, specs = [], []
    for r in range(s):
        aspec = pl.BlockSpec((None, tm, fs), lambda i, j, kk, r=r: (r, i, 0))
        wspec = pl.BlockSpec((None, None, fs, k), lambda i, j, kk, r=r: (l, r, 0, 0))
        ops += [dg, wg, du, wu]
        specs += [aspec, wspec, aspec, wspec]
    return _gemm("ffn_up_bwd", ops, specs, pl.BlockSpec((tm, k), lambda i, j, kk: (i, 0)),
                 SDS((t, k), f32), (t // tm, 1, 1), NN, None)


def _ffn_wgrad_up(h, dg, du, buf_g, buf_u, l):
    t, k = h.shape
    s, _, fs = dg.shape
    tt = _tile(t)
    nred = t // tt

    fresh = isinstance(buf_g, _Fresh)

    def body(*refs):
        h_ref, dg_ref, du_ref = refs[:3]
        og_ref, ou_ref, acc_g, acc_u = refs[-4:]
        r = pl.program_id(1)
        hv = h_ref[...]
        pg = lax.dot_general(dg_ref[...], hv, (TN, ((), ())), preferred_element_type=f32)
        pu = lax.dot_general(du_ref[...], hv, (TN, ((), ())), preferred_element_type=f32)

        @pl.when(r == 0)
        def _():
            acc_g[...] = pg
            acc_u[...] = pu

        @pl.when(r > 0)
        def _():
            acc_g[...] += pg
            acc_u[...] += pu

        @pl.when(r == nred - 1)
        def _():
            og_ref[...] = acc_g[...]
            ou_ref[...] = acc_u[...]

    dspec = pl.BlockSpec((None, tt, fs), lambda i, r: (i, r, 0))
    ospec = pl.BlockSpec((None, None, fs, k), lambda i, r: (l, i, 0, 0))
    extra, alias = ([], {}) if fresh else ([buf_g, buf_u], {3: 0, 4: 1})
    return pl.pallas_call(
        body, grid=(s, nred), name="ffn_wgrad_up",
        in_specs=[pl.BlockSpec((tt, k), lambda i, r: (r, 0)), dspec, dspec] + [ANY] * len(extra),
        out_specs=[ospec, ospec], out_shape=[SDS(buf_g.shape, f32), SDS(buf_u.shape, f32)],
        scratch_shapes=[pltpu.VMEM((fs, k), f32)] * 2, input_output_aliases=alias,
        compiler_params=_cparams(("parallel", "arbitrary")))(h, dg, du, *extra)


def _ffn_wgrad_down(hid, df, buf, l):
    s, t, fs = hid.shape
    n = df.shape[1]
    tt = _tile(t)
    return _gemm(
        "ffn_wgrad_down", [hid, df],
        [pl.BlockSpec((None, tt, fs), lambda i, j, r: (i, r, 0)),
         pl.BlockSpec((tt, n), lambda i, j, r: (r, 0))],
        pl.BlockSpec((None, None, fs, n), lambda i, j, r: (l, i, 0, 0)),
        SDS(buf.shape, f32), (s, 1, t // tt), TN, (fs, n), into=_into(buf))


def _rows(name, fn, rows, consts, row_outs, acc_outs=(), tr=512):
    rows = [r if isinstance(r, tuple) else (r, r.shape[1], 0) for r in rows]
    t = rows[0][0].shape[0]
    tr = max(d for d in range(SUBLANES, min(tr, t) + 1, SUBLANES) if t % d == 0)
    nin = len(rows) + len(consts)
    no, na = len(row_outs), len(acc_outs)

    def body(*refs):
        vals = fn(*[r[...] for r in refs[:nin]])
        if not isinstance(vals, (tuple, list)):
            vals = (vals,)
        for k in range(no):
            refs[nin + k][...] = vals[k].astype(refs[nin + k].dtype)
        first = pl.program_id(0) == 0
        for k in range(na):
            ref, val = refs[nin + no + k], vals[no + k]

            @pl.when(first)
            def _(ref=ref, val=val):
                ref[...] = val

            @pl.when(jnp.logical_not(first))
            def _(ref=ref, val=val):
                ref[...] += val

    in_specs = [pl.BlockSpec((tr, w), lambda i, cb=cb: (i, cb)) for (_, w, cb) in rows]
    in_specs += [pl.BlockSpec(c.shape, lambda i, nd=c.ndim: (0,) * nd) for c in consts]
    out_specs = [pl.BlockSpec((tr, w), lambda i: (i, 0)) for (w, _) in row_outs]
    out_specs += [pl.BlockSpec(s, lambda i, nd=len(s): (0,) * nd) for (s, _) in acc_outs]
    out_shape = [SDS((t, w), dt) for (w, dt) in row_outs] + [SDS(s, dt) for (s, dt) in acc_outs]
    res = pl.pallas_call(
        body, grid=(t // tr,), in_specs=in_specs, out_specs=out_specs, out_shape=out_shape,
        name=name, compiler_params=_cparams(("arbitrary",)))(*[r[0] for r in rows], *consts)
    return res


def _rstd(x):
    return lax.rsqrt(jnp.mean(x * x, axis=-1, keepdims=True) + RMS_EPS)


def _norm_fwd(x, g):
    return x * _rstd(x) * g


def _norm_bwd(u, dy, g):
    r = _rstd(u)
    n = u * r
    dn = dy * g
    du = r * (dn - n * jnp.mean(dn * n, axis=-1, keepdims=True))
    return du, jnp.sum(dy * n, axis=0, keepdims=True)


def _gelu(x):
    c = 0.7978845608028654
    return 0.5 * x * (1.0 + jnp.tanh(c * (x + 0.044715 * x * x * x)))


def _gelu_grad(x):
    c = 0.7978845608028654
    th = jnp.tanh(c * (x + 0.044715 * x * x * x))
    return 0.5 * (1.0 + th) + 0.5 * x * (1.0 - th * th) * c * (1.0 + 3.0 * 0.044715 * x * x)


def _mask_heads(x):
    lane = lax.broadcasted_iota(jnp.int32, x.shape, 1)
    return [jnp.where((lane >= h * HEAD_DIM) & (lane < (h + 1) * HEAD_DIM), x, jnp.zeros_like(x))
            for h in range(LANES // HEAD_DIM)]


def _chunk_valid(start):
    qi = lax.broadcasted_iota(jnp.int32, (QB_A, KW_A), 0)
    kj = lax.broadcasted_iota(jnp.int32, (QB_A, KW_A), 1)
    qc = qi // CHUNK
    kc = kj // CHUNK
    return (kc >= qc) & (kc <= qc + N_LEFT) & (kj + start >= PAD_A)


def _chunk_probs(q, k, bias, valid):
    s = lax.dot_general(q, k, (NT, ((), ())), preferred_element_type=f32) * (HEAD_DIM ** -0.5) + bias
    s = jnp.where(valid, s, -1e30)
    p = jnp.exp(s - jnp.max(s, axis=-1, keepdims=True))
    return p / jnp.sum(p, axis=-1, keepdims=True)


def _chunk_attn_fwd(proj, kpad, vpad, bias, gather):
    t = proj.shape[0]
    tp = kpad.shape[0]
    step = QSUB_A * QB_A

    def body(q_ref, k_ref, v_ref, b_ref, o_ref):
        for sb in range(QSUB_A):
            start = pl.multiple_of((pl.program_id(1) * QSUB_A + sb) * QB_A, QB_A)
            rows = pl.ds(sb * QB_A, QB_A)
            valid = _chunk_valid(start)
            kw = k_ref[pl.ds(start, KW_A), :]
            qm = _mask_heads(q_ref[rows, :])
            vm = _mask_heads(v_ref[pl.ds(start, KW_A), :])
            o = None
            for h in range(len(qm)):
                p = _chunk_probs(qm[h], kw, b_ref[h], valid)
                d = jnp.dot(p.astype(bf16), vm[h], preferred_element_type=f32)
                o = d if o is None else o + d
            o_ref[rows, :] = o.astype(bf16)

    kv_spec = pl.BlockSpec((tp, LANES), lambda hp, qb: (0, hp))
    outs, new = _call(
        body, [proj, kpad, vpad, bias], grid=(A_W // LANES, t // step), name="chunk_attn_fwd",
        in_specs=[pl.BlockSpec((step, LANES), lambda hp, qb: (qb, hp)), kv_spec, kv_spec,
                  pl.BlockSpec((2, QB_A, KW_A), lambda hp, qb: (hp, 0, 0))],
        out_specs=[pl.BlockSpec((step, LANES), lambda hp, qb: (qb, hp))],
        out_shape=[SDS((t, A_W), bf16)], sem=("parallel", "arbitrary"), gather=gather)
    return outs[0], new


def _chunk_attn_bwd(proj, kpad, vpad, bias, dout, gather):
    t = proj.shape[0]
    tp = kpad.shape[0]
    step = QSUB_A * QB_A

    def body(q_ref, k_ref, v_ref, b_ref, do_ref, dq_ref, dk_ref, dv_ref, db_ref):
        qb = pl.program_id(1)

        @pl.when(qb == 0)
        def _():
            dk_ref[...] = jnp.zeros_like(dk_ref)
            dv_ref[...] = jnp.zeros_like(dv_ref)
            db_ref[...] = jnp.zeros_like(db_ref)

        for sb in range(QSUB_A):
            start = pl.multiple_of((qb * QSUB_A + sb) * QB_A, QB_A)
            rows = pl.ds(sb * QB_A, QB_A)
            win = pl.ds(start, KW_A)
            valid = _chunk_valid(start)
            kw = k_ref[win, :]
            vw = v_ref[win, :]
            qm = _mask_heads(q_ref[rows, :])
            dom = _mask_heads(do_ref[rows, :])
            km = _mask_heads(kw)
            dq = dk = dv = None
            for h in range(len(qm)):
                p = _chunk_probs(qm[h], kw, b_ref[h], valid)
                dp = lax.dot_general(dom[h], vw, (NT, ((), ())), preferred_element_type=f32)
                ds = p * (dp - jnp.sum(dp * p, axis=-1, keepdims=True))
                db_ref[h] += ds
                dsb = (ds * (HEAD_DIM ** -0.5)).astype(bf16)
                terms = (jnp.dot(dsb, km[h], preferred_element_type=f32),
                         lax.dot_general(dsb, qm[h], (TN, ((), ())), preferred_element_type=f32),
                         lax.dot_general(p.astype(bf16), dom[h], (TN, ((), ())), preferred_element_type=f32))
                dq, dk, dv = terms if dq is None else (dq + terms[0], dk + terms[1], dv + terms[2])
            dq_ref[rows, :] = dq.astype(bf16)
            dk_ref[win, :] += dk
            dv_ref[win, :] += dv

    kv_spec = pl.BlockSpec((tp, LANES), lambda hp, qb: (0, hp))
    q_spec = pl.BlockSpec((step, LANES), lambda hp, qb: (qb, hp))
    b_spec = pl.BlockSpec((2, QB_A, KW_A), lambda hp, qb: (hp, 0, 0))
    return _call(
        body, [proj, kpad, vpad, bias, dout], grid=(A_W // LANES, t // step), name="chunk_attn_bwd",
        in_specs=[q_spec, kv_spec, kv_spec, b_spec, q_spec],
        out_specs=[q_spec, kv_spec, kv_spec, b_spec],
        out_shape=[SDS((t, A_W), bf16), SDS((tp, A_W), f32), SDS((tp, A_W), f32),
                   SDS((2 * A_W // LANES, QB_A, KW_A), f32)],
        sem=("parallel", "arbitrary"), gather=gather)


def _bias_ext(table):
    flat = PAD_A + QB_A - 1 - REL_CLIP
    top = jnp.broadcast_to(table[:, 2 * REL_CLIP:], (table.shape[0], flat))
    lo = 2 * REL_CLIP - (EXT_A - 1 - flat)
    return jnp.concatenate([top, jnp.flip(table[:, lo:], axis=1)], axis=1)


def _bias_window(table):
    nh = table.shape[0]
    e = jnp.broadcast_to(_bias_ext(table)[:, None, :], (nh, QB_A, EXT_A)).reshape(nh, QB_A * EXT_A)
    m = e[:, :QB_A * (EXT_A - 1)].reshape(nh, QB_A, EXT_A - 1)
    return m[:, :, QB_A - 1:]


def _bias_window_grad(dbias):
    nh = dbias.shape[0]
    m = jnp.pad(dbias, ((0, 0), (0, 0), (QB_A - 1, 0))).reshape(nh, QB_A * (EXT_A - 1))
    dext = jnp.sum(jnp.pad(m, ((0, 0), (0, QB_A))).reshape(nh, QB_A, EXT_A), axis=1)
    flat = PAD_A + QB_A - 1 - REL_CLIP
    lo = 2 * REL_CLIP - (EXT_A - 1 - flat)
    tail = jnp.flip(dext[:, flat:], axis=1)
    tail = tail.at[:, -1].add(jnp.sum(dext[:, :flat], axis=1))
    return jnp.pad(tail, ((0, 0), (lo, 0)))


def _tri_suffix(x, tri):
    hi = x.astype(bf16)
    lo = (x - hi.astype(f32)).astype(bf16)
    return jnp.dot(hi, tri, preferred_element_type=f32) + jnp.dot(lo, tri, preferred_element_type=f32)


def _sb_block(q, k, run, tri, causal):
    z = lax.dot_general(q, k, (NT, ((), ())), preferred_element_type=f32) * (HEAD_DIM ** -0.5)
    e = jnp.exp(-jnp.abs(z))
    l1p = jnp.log(1.0 + e)
    lb = jnp.minimum(z, 0.0) - l1p
    lmb = lb - z
    if causal is not None:
        lmb = jnp.where(causal, lmb, 0.0)
    cs = _tri_suffix(lmb, tri)
    w = jnp.exp(lb + (run + cs - lmb))
    if causal is not None:
        w = jnp.where(causal, w, 0.0)
    return z, e, w, run + cs[:, 0:1]


def _sb_tri():
    r = lax.broadcasted_iota(jnp.int32, (SB_BLK, SB_BLK), 0)
    c = lax.broadcasted_iota(jnp.int32, (SB_BLK, SB_BLK), 1)
    return (r >= c).astype(bf16), c < r


def _sb_live(runs):
    m = runs[0]
    for r in runs[1:]:
        m = jnp.maximum(m, r)
    return jnp.max(m) > SB_DEAD


def _sb_fwd(proj, gather):
    t = proj.shape[0]
    cb = A_W // LANES
    nh = LANES // HEAD_DIM

    step_rows = QSUB_B * SB_BLK

    def body(q_ref, k_ref, v_ref, o_ref, of_ref):
        tri, diag = _sb_tri()
        for sb in range(QSUB_B):
            _sb_fwd_block(pl.program_id(1) * QSUB_B + sb, pl.ds(sb * SB_BLK, SB_BLK), tri, diag,
                          q_ref, k_ref, v_ref, o_ref, of_ref)

    def _sb_fwd_block(qb, qrows, tri, diag, q_ref, k_ref, v_ref, o_ref, of_ref):
        qm = _mask_heads(q_ref[qrows, :])

        def pair(kb, carry, causal):
            rows = pl.ds(pl.multiple_of(kb * SB_BLK, SB_BLK), SB_BLK)
            k = k_ref[rows, :]
            vm = _mask_heads(v_ref[rows, :])
            runs, acc = [], carry[nh]
            for h in range(nh):
                _, _, w, run = _sb_block(qm[h], k, carry[h], tri, causal)
                acc = acc + jnp.dot(w.astype(bf16), vm[h], preferred_element_type=f32)
                runs.append(run)
            return (*runs, acc)

        zero = jnp.zeros((SB_BLK, 1), f32)
        carry = pair(qb, (zero,) * nh + (jnp.zeros((SB_BLK, LANES), f32),), diag)

        def cond(st):
            return (st[0] < qb) & _sb_live(st[1][:nh])

        def step(st):
            return st[0] + 1, pair(qb - 1 - st[0], st[1], None)

        _, carry = lax.while_loop(cond, step, (jnp.int32(0), carry))
        o_ref[qrows, :] = carry[nh].astype(bf16)
        of_ref[qrows, :] = carry[nh]

    ospec = pl.BlockSpec((step_rows, LANES), lambda hp, qb: (qb, hp))
    return _call(
        body, [proj, proj, proj], grid=(cb, t // step_rows), name="sb_attn_fwd",
        in_specs=[pl.BlockSpec((step_rows, LANES), lambda hp, qb: (qb, 3 * cb + hp)),
                  pl.BlockSpec((t, LANES), lambda hp, qb: (0, 4 * cb + hp)),
                  pl.BlockSpec((t, LANES), lambda hp, qb: (0, 5 * cb + hp))],
        out_specs=[ospec, ospec], out_shape=[SDS((t, A_W), bf16), SDS((t, A_W), f32)],
        sem=("parallel", "arbitrary"), gather=gather)


def _sb_bwd(proj, out_b, dout, gather):
    t = proj.shape[0]
    cb = A_W // LANES
    nh = LANES // HEAD_DIM

    step_rows = QSUB_B * SB_BLK

    def body(q_ref, k_ref, v_ref, o_ref, do_ref, dq_ref, dk_ref, dv_ref):
        tri, diag = _sb_tri()

        @pl.when(pl.program_id(1) == 0)
        def _():
            dk_ref[...] = jnp.zeros_like(dk_ref)
            dv_ref[...] = jnp.zeros_like(dv_ref)

        for sb in range(QSUB_B):
            _sb_bwd_block(pl.program_id(1) * QSUB_B + sb, pl.ds(sb * SB_BLK, SB_BLK), tri, diag,
                          q_ref, k_ref, v_ref, o_ref, do_ref, dq_ref, dk_ref, dv_ref)

    def _sb_bwd_block(qb, qrows, tri, diag, q_ref, k_ref, v_ref, o_ref, do_ref, dq_ref, dk_ref, dv_ref):
        qm = _mask_heads(q_ref[qrows, :])
        do = do_ref[qrows, :]
        dom = _mask_heads(do)
        dsums = [jnp.sum(t_, axis=-1, keepdims=True) for t_ in _mask_heads(do.astype(f32) * o_ref[qrows, :])]

        def pair(kb, carry, causal):
            rows = pl.ds(pl.multiple_of(kb * SB_BLK, SB_BLK), SB_BLK)
            k = k_ref[rows, :]
            v = v_ref[rows, :]
            km = _mask_heads(k)
            new, dq, dk, dv = [], carry[2 * nh], None, None
            for h in range(nh):
                z, e, w, run = _sb_block(qm[h], k, carry[2 * h], tri, causal)
                inv = 1.0 / (1.0 + e)
                beta = jnp.where(z >= 0.0, inv, e * inv)
                wb = w.astype(bf16)
                g = lax.dot_general(dom[h], v, (NT, ((), ())), preferred_element_type=f32) * wb.astype(f32)
                sg = _tri_suffix(g, tri)
                dz = g * (1.0 - beta) - (dsums[h] - carry[2 * h + 1] - sg) * beta
                if causal is not None:
                    dz = jnp.where(causal, dz, 0.0)
                dzb = (dz * (HEAD_DIM ** -0.5)).astype(bf16)
                dq = dq + jnp.dot(dzb, km[h], preferred_element_type=f32)
                tk = lax.dot_general(dzb, qm[h], (TN, ((), ())), preferred_element_type=f32)
                tv = lax.dot_general(wb, dom[h], (TN, ((), ())), preferred_element_type=f32)
                dk, dv = (tk, tv) if dk is None else (dk + tk, dv + tv)
                new += [run, carry[2 * h + 1] + sg[:, 0:1]]
            dk_ref[rows, :] += dk
            dv_ref[rows, :] += dv
            return (*new, dq)

        zero = jnp.zeros((SB_BLK, 1), f32)
        carry = pair(qb, (zero,) * (2 * nh) + (jnp.zeros((SB_BLK, LANES), f32),), diag)

        def cond(st):
            return (st[0] < qb) & _sb_live(st[1][0:2 * nh:2])

        def step(st):
            return st[0] + 1, pair(qb - 1 - st[0], st[1], None)

        _, carry = lax.while_loop(cond, step, (jnp.int32(0), carry))
        dq_ref[qrows, :] = carry[2 * nh].astype(bf16)

    kv_in = lambda seg: pl.BlockSpec((t, LANES), lambda hp, qb: (0, seg * cb + hp))
    q_spec = pl.BlockSpec((step_rows, LANES), lambda hp, qb: (qb, hp))
    kv_out = pl.BlockSpec((t, LANES), lambda hp, qb: (0, hp))
    return _call(
        body, [proj, proj, proj, out_b, dout], grid=(cb, t // step_rows), name="sb_attn_bwd",
        in_specs=[pl.BlockSpec((step_rows, LANES), lambda hp, qb: (qb, 3 * cb + hp)), kv_in(4), kv_in(5),
                  q_spec, pl.BlockSpec((step_rows, LANES), lambda hp, qb: (qb, cb + hp))],
        out_specs=[q_spec, kv_out, kv_out],
        out_shape=[SDS((t, A_W), bf16), SDS((t, A_W), f32), SDS((t, A_W), f32)],
        sem=("parallel", "arbitrary"), gather=gather)


def _halo_specs(tr, w, col, nblk):
    per = tr // SUBLANES
    cur = pl.BlockSpec((tr, w), lambda i: (i, col))
    prev = pl.BlockSpec((SUBLANES, w), lambda i: (jnp.maximum(i * per - 1, 0), col))
    nxt = pl.BlockSpec((SUBLANES, w), lambda i: (jnp.minimum((i + 1) * per, nblk * per - 1), col))
    return cur, prev, nxt


def _taps_before(cur, prev8, first):
    prev8 = jnp.where(first, 0.0, prev8)
    ext = jnp.concatenate([prev8, cur], axis=0)
    return [pltpu.roll(ext, s, 0)[SUBLANES:] for s in (3, 2, 1)]


def _taps_after(cur, next8, last):
    n = cur.shape[0]
    next8 = jnp.where(last, 0.0, next8)
    ext = jnp.concatenate([cur, next8], axis=0)
    return [pltpu.roll(ext, n + SUBLANES - s, 0)[:n] for s in (1, 2, 3)]


def _block_diag(x, w_ref, dims):
    outs = [lax.dot_general(x[:, n * LRU_BW:(n + 1) * LRU_BW], w_ref[n], (dims, ((), ())),
                            preferred_element_type=f32) for n in range(LRU_BLOCKS)]
    return jnp.concatenate(outs, axis=1)


def _lru_gates(xc, wa_ref, wi_ref, ba, bi, lam):
    xb = xc.astype(bf16)
    r = jax.nn.sigmoid(_block_diag(xb, wa_ref, NN) + ba)
    ig = jax.nn.sigmoid(_block_diag(xb, wi_ref, NN) + bi)
    sp = jnp.maximum(-lam, 0.0) + jnp.log(1.0 + jnp.exp(-jnp.abs(lam)))
    log_a = -LRU_C * r * sp
    a = jnp.exp(log_a)
    x2 = 2.0 * log_a
    one_minus = jnp.where(x2 > -1e-2, -x2 * (1.0 + x2 * (0.5 + x2 * (1.0 / 6.0))), 1.0 - a * a)
    mult = jnp.sqrt(one_minus)
    return xb, r, ig, sp, a, mult


def _rg_gates_fwd(proj, conv_w, conv_b, wa, wi, ba, bi, lam, tr=512):
    t = proj.shape[0]
    w = D_MODEL
    tr = min(tr, t)
    nblk = t // tr
    cur, prev, _ = _halo_specs(tr, w, 1, nblk)

    def body(x_ref, xp_ref, cw_ref, cb_ref, wa_ref, wi_ref, ba_ref, bi_ref, lam_ref, xc_ref, a_ref, u_ref):
        x = x_ref[...]
        taps = _taps_before(x, xp_ref[...], pl.program_id(0) == 0) + [x]
        xc = cb_ref[...]
        for k in range(4):
            xc = xc + cw_ref[k:k + 1, :] * taps[k]
        _, _, ig, _, a, mult = _lru_gates(xc, wa_ref, wi_ref, ba_ref[...], bi_ref[...], lam_ref[...])
        xc_ref[...] = xc
        a_ref[...] = a
        u_ref[...] = mult * (ig * xc)

    full = lambda a_: pl.BlockSpec(a_.shape, lambda i, nd=a_.ndim: (0,) * nd)
    ospec = pl.BlockSpec((tr, w), lambda i: (i, 0))
    return pl.pallas_call(
        body, grid=(nblk,), name="rg_gates_fwd",
        in_specs=[cur, prev] + [full(a_) for a_ in (conv_w, conv_b, wa, wi, ba, bi, lam)],
        out_specs=[ospec] * 3, out_shape=[SDS((t, w), f32)] * 3,
        compiler_params=_cparams(("parallel",)))(proj, proj, conv_w, conv_b, wa, wi, ba, bi, lam)


def _lru_scan(name, a, b, reverse, tt=512):
    t, w = a.shape
    tt = min(tt, t)
    nt = t // tt
    ng = tt // SUBLANES

    def body(a_ref, b_ref, h_ref, carry_ref):
        @pl.when(pl.program_id(0) == 0)
        def _():
            carry_ref[...] = jnp.zeros_like(carry_ref)

        row = lax.broadcasted_iota(jnp.int32, (SUBLANES, w), 0)

        def group(gi, carry):
            g = (ng - 1 - gi) if reverse else gi
            rows = pl.ds(pl.multiple_of(g * SUBLANES, SUBLANES), SUBLANES)
            av = a_ref[rows, :]
            bv = b_ref[rows, :]
            for s in (1, 2, 4):
                sh = (SUBLANES - s) if reverse else s
                ok = (row < SUBLANES - s) if reverse else (row >= s)
                a_s = pltpu.roll(av, sh, 0)
                b_s = pltpu.roll(bv, sh, 0)
                bv = jnp.where(ok, av * b_s + bv, bv)
                av = jnp.where(ok, av * a_s, av)
            h = av * carry + bv
            h_ref[rows, :] = h
            edge = h[0:1, :] if reverse else h[SUBLANES - 1:SUBLANES, :]
            return jnp.broadcast_to(edge, (SUBLANES, w))

        carry_ref[...] = lax.fori_loop(0, ng, group, carry_ref[...], unroll=4)

    tmap = (lambda i: (nt - 1 - i, 0)) if reverse else (lambda i: (i, 0))
    spec = pl.BlockSpec((tt, w), tmap)
    return pl.pallas_call(
        body, grid=(nt,), name=name, in_specs=[spec, spec], out_specs=spec,
        out_shape=SDS((t, w), f32), scratch_shapes=[pltpu.VMEM((SUBLANES, w), f32)],
        compiler_params=_cparams(("arbitrary",)))(a, b)


def _rg_gates_bwd(dhs, c, hs, xc, wa, wi, ba, bi, lam, tr=256):
    t, w = xc.shape
    tr = min(tr, t)
    nblk = t // tr
    cur, prev, nxt = _halo_specs(tr, w, 0, nblk)

    def body(dhs_ref, c_ref, cn_ref, hs_ref, hp_ref, xc_ref, wa_ref, wi_ref, ba_ref, bi_ref, lam_ref,
             dxc_ref, dwa_ref, dwi_ref, dba_ref, dbi_ref, dlam_ref):
        i = pl.program_id(0)
        c_next = _taps_after(c_ref[...], cn_ref[...], i == nblk - 1)[0]
        h_prev = _taps_before(hs_ref[...], hp_ref[...], i == 0)[2]
        xc = xc_ref[...]
        lam = lam_ref[...]
        xb, r, ig, sp, a, mult = _lru_gates(xc, wa_ref, wi_ref, ba_ref[...], bi_ref[...], lam)
        dh = dhs_ref[...] + c_next
        dlog_a = dh * h_prev * a - (dh * ig * xc) * (a * a / mult)
        dpre_a = (dlog_a * (-LRU_C * sp) * r * (1.0 - r)).astype(bf16)
        dpre_i = (dh * mult * xc * ig * (1.0 - ig)).astype(bf16)
        dxc_ref[...] = (dh * mult * ig + _block_diag(dpre_a, wa_ref, NT) + _block_diag(dpre_i, wi_ref, NT))
        dsig = 1.0 / (1.0 + jnp.exp(lam))
        sums = [jnp.sum(dpre_a.astype(f32), axis=0, keepdims=True),
                jnp.sum(dpre_i.astype(f32), axis=0, keepdims=True),
                jnp.sum(dlog_a * (-LRU_C * r), axis=0, keepdims=True) * (-dsig)]

        @pl.when(i == 0)
        def _():
            dwa_ref[...] = jnp.zeros_like(dwa_ref)
            dwi_ref[...] = jnp.zeros_like(dwi_ref)
            dba_ref[...] = jnp.zeros_like(dba_ref)
            dbi_ref[...] = jnp.zeros_like(dbi_ref)
            dlam_ref[...] = jnp.zeros_like(dlam_ref)

        for n in range(LRU_BLOCKS):
            sl = slice(n * LRU_BW, (n + 1) * LRU_BW)
            dwa_ref[n] += lax.dot_general(xb[:, sl], dpre_a[:, sl], (TN, ((), ())), preferred_element_type=f32)
            dwi_ref[n] += lax.dot_general(xb[:, sl], dpre_i[:, sl], (TN, ((), ())), preferred_element_type=f32)
        dba_ref[...] += sums[0]
        dbi_ref[...] += sums[1]
        dlam_ref[...] += sums[2]

    full = lambda a_: pl.BlockSpec(a_.shape, lambda i, nd=a_.ndim: (0,) * nd)
    vec = pl.BlockSpec((1, w), lambda i: (0, 0))
    mat = pl.BlockSpec((LRU_BLOCKS, LRU_BW, LRU_BW), lambda i: (0, 0, 0))
    return pl.pallas_call(
        body, grid=(nblk,), name="rg_gates_bwd",
        in_specs=[cur, cur, nxt, cur, prev, cur] + [full(a_) for a_ in (wa, wi, ba, bi, lam)],
        out_specs=[cur, mat, mat, vec, vec, vec],
        out_shape=[SDS((t, w), f32), SDS((LRU_BLOCKS, LRU_BW, LRU_BW), f32), SDS((LRU_BLOCKS, LRU_BW, LRU_BW), f32),
                   SDS((1, w), f32), SDS((1, w), f32), SDS((1, w), f32)],
        compiler_params=_cparams(("arbitrary",)))(dhs, c, c, hs, hs, xc, wa, wi, ba, bi, lam)


def _rg_conv_bwd(dxc, proj, conv_w, tr=512):
    t, w = dxc.shape
    tr = min(tr, t)
    nblk = t // tr
    cur, _, nxt = _halo_specs(tr, w, 0, nblk)
    xcur, xprev, _ = _halo_specs(tr, w, 1, nblk)

    def body(d_ref, dn_ref, x_ref, xp_ref, cw_ref, dx_ref, dcw_ref, dcb_ref):
        i = pl.program_id(0)
        d = d_ref[...]
        x = x_ref[...]
        after = _taps_after(d, dn_ref[...], i == nblk - 1)
        before = _taps_before(x, xp_ref[...], i == 0) + [x]
        dx = cw_ref[3:4, :] * d
        for s in (1, 2, 3):
            dx = dx + cw_ref[3 - s:4 - s, :] * after[s - 1]
        dx_ref[...] = dx.astype(bf16)
        dcw = jnp.concatenate([jnp.sum(d * before[k], axis=0, keepdims=True) for k in range(4)], axis=0)
        dcb = jnp.sum(d, axis=0, keepdims=True)

        @pl.when(i == 0)
        def _():
            dcw_ref[...] = dcw
            dcb_ref[...] = dcb

        @pl.when(i > 0)
        def _():
            dcw_ref[...] += dcw
            dcb_ref[...] += dcb

    return pl.pallas_call(
        body, grid=(nblk,), name="rg_conv_bwd",
        in_specs=[cur, nxt, xcur, xprev, pl.BlockSpec((4, w), lambda i: (0, 0))],
        out_specs=[cur, pl.BlockSpec((4, w), lambda i: (0, 0)), pl.BlockSpec((1, w), lambda i: (0, 0))],
        out_shape=[SDS((t, w), bf16), SDS((4, w), f32), SDS((1, w), f32)],
        compiler_params=_cparams(("arbitrary",)))(dxc, dxc, proj, proj, conv_w)


def _attn_fwd(h, wts, j, plan):
    proj = _mm_cols("attn_in", h, wts["attn_w_in"], j, bf16)
    kpad = jnp.pad(proj[:, A_W:2 * A_W], ((PAD_A, 0), (0, 0)))
    vpad = jnp.pad(proj[:, 2 * A_W:3 * A_W], ((PAD_A, 0), (0, 0)))
    bias = _bias_window(wts["attn_rel_bias"][j])
    plan = plan if j == 0 else None
    out_a = _carried(plan, "chunk_attn_fwd", wts, _chunk_attn_fwd, proj, kpad, vpad, bias)
    out_b, out_b32 = _carried(plan, "sb_attn_fwd", wts, _sb_fwd, proj)
    m = _mm_rows("attn_out", [out_a, out_b], wts["attn_w_out"], j, f32)
    return m, (proj, kpad, vpad, bias, out_a, out_b, out_b32)


def _attn_bwd(dm, h, saved, wts, j, grads, exch):
    proj, kpad, vpad, bias, out_a, out_b, out_b32 = saved
    dout = _mm_rows_t("attn_out_t", dm, wts["attn_w_out"], j, bf16)
    gi, ll = _grad_slot("attn_w_out", j)
    grads["attn_w_out"][gi] = _mm_wgrad("attn_out_wgrad_a", out_a, dm, grads["attn_w_out"][gi], ll, 0)
    grads["attn_w_out"][gi] = _mm_wgrad("attn_out_wgrad_b", out_b, dm, grads["attn_w_out"][gi], ll, 1)
    if exch is not None and j == 0:
        (dqa, dka, dva, dbias), got = _chunk_attn_bwd(proj, kpad, vpad, bias, dout, exch.upper_carry(grads))
        exch.upper_got(got)
        (dqs, dks, dvs), slots = _sb_bwd(proj, out_b32, dout, exch.carry())
        exch.carried(slots)
    else:
        dqa, dka, dva, dbias = _chunk_attn_bwd(proj, kpad, vpad, bias, dout, None)[0]
        dqs, dks, dvs = _sb_bwd(proj, out_b32, dout, None)[0]
    grads["attn_rel_bias"][j] = _bias_window_grad(dbias)
    dproj = jnp.concatenate([dqa, dka[PAD_A:].astype(bf16), dva[PAD_A:].astype(bf16),
                             dqs, dks.astype(bf16), dvs.astype(bf16)], axis=1)
    grads["attn_w_in"][gi] = _mm_wgrad_cols("attn_in_wgrad", h, dproj, grads["attn_w_in"][gi], ll)
    return _mm_cols_t("attn_in_t", dproj, wts["attn_w_in"], j, f32)


def _rg_fwd(h, wts, j, plan):
    proj =_mm_cols("rg_in", h, wts["rg_w_in"], j, f32)
    small = [wts[k][j] for k in ("rg_conv_w", "rg_conv_b", "rg_w_a", "rg_w_i", "rg_b_a", "rg_b_i", "rg_lambda")]
    xc, a, u = _rg_gates_fwd(proj, *small)
    hs = _lru_scan("lru_scan_fwd", a, u, False)
    yp = _rows("rg_gate_out", lambda hv, gv: hv * _gelu(gv), [hs, (proj, D_MODEL, 0)], [], [(D_MODEL, bf16)])[0]
    m = _mm_rows("rg_out", [yp], wts["rg_w_out"], j, f32)
    return m, (proj, xc, a, hs, yp)


def _rg_bwd(dm, h, saved, wts, j, grads, exch):
    proj, xc, a, hs, yp = saved
    dyp = _mm_rows_t("rg_out_t", dm, wts["rg_w_out"], j, f32)
    gi, ll = _grad_slot("rg_w_out", j)
    grads["rg_w_out"][gi] = _mm_wgrad("rg_out_wgrad", yp, dm, grads["rg_w_out"][gi], ll)

    def gate_bwd(dy, hv, gv, av):
        dhs = dy * _gelu(gv)
        return dhs, av * dhs, dy * hv * _gelu_grad(gv)

    dhs, ab, dgate = _rows("rg_gate_out_bwd", gate_bwd, [dyp, hs, (proj, D_MODEL, 0), a], [],
                           [(D_MODEL, f32), (D_MODEL, f32), (D_MODEL, bf16)])
    c = _lru_scan("lru_scan_bwd", a, ab, True)
    wa, wi, ba, bi, lam = [wts[k][j] for k in ("rg_w_a", "rg_w_i", "rg_b_a", "rg_b_i", "rg_lambda")]
    dxc, dwa, dwi, dba, dbi, dlam = _rg_gates_bwd(dhs, c, hs, xc, wa, wi, ba, bi, lam)
    dxr, dcw, dcb = _rg_conv_bwd(dxc, proj, wts["rg_conv_w"][j])
    for k, v in (("rg_w_a", dwa), ("rg_w_i", dwi), ("rg_b_a", dba), ("rg_b_i", dbi), ("rg_lambda", dlam),
                 ("rg_conv_w", dcw), ("rg_conv_b", dcb)):
        grads[k][j] = v
    dproj = jnp.concatenate([dgate, dxr], axis=1)
    grads["rg_w_in"][gi] = _mm_wgrad_cols("rg_in_wgrad", h, dproj, grads["rg_w_in"][gi], ll)
    return _mm_cols_t("rg_in_t", dproj, wts["rg_w_in"], j, f32)


def _local_step(x, target, wts, plan=None, exch=None):
    t = x.shape[0]
    d = D_MODEL
    gains = {k: wts[k] for k in ("norm_mix_pre", "norm_mix_post", "norm_ffn_pre", "norm_ffn_post")}
    gain = lambda k, l: gains[k][l:l + 1]

    saved = []
    h = _rows("norm_in", _norm_fwd, [x], [gain("norm_mix_pre", 0)], [(d, bf16)])[0]
    loss_cols = None
    for l in range(DEPTH):
        j = l // 2
        m, mix_saved = (_attn_fwd if l % 2 == 0 else _rg_fwd)(h, wts, j, plan)

        def resid_next(xv, mv, g_post, g_next):
            x1 = xv + _norm_fwd(mv, g_post)
            return x1, _norm_fwd(x1, g_next)

        x1, h2 = _rows("resid_mix", resid_next, [x, m], [gain("norm_mix_post", l), gain("norm_ffn_pre", l)],
                       [(d, f32), (d, bf16)])
        g, u, hid = _carried(plan if l == 0 else None, "ffn_up", wts, _ffn_up, h2, wts["ffn_w_gate"],
                             wts["ffn_w_up"], l)
        f = _ffn_down(hid, wts["ffn_w_down"], l)
        saved.append((x, h, m, mix_saved, x1, h2, g, u, hid, f))
        if l + 1 < DEPTH:
            x, h = _rows("resid_ffn", resid_next, [x1, f], [gain("norm_ffn_post", l), gain("norm_mix_pre", l + 1)],
                         [(d, f32), (d, bf16)])
        else:
            def resid_loss(xv, fv, tv, g_post):
                err = xv + _norm_fwd(fv, g_post) - tv
                return err * (1.0 / d), jnp.sum(err * err, axis=0, keepdims=True)

            dx, loss_cols = _rows("resid_loss", resid_loss, [x1, f, target], [gain("norm_ffn_post", l)],
                                  [(d, f32)], [((1, d), f32)])
    loss = 0.5 * jnp.sum(loss_cols) / d

    grads = {k: {} for k in SMALL_GRADS}
    for k in BIG_GRADS:
        shp = wts[k].shape
        rest = shp[2:] if shp[1] == 1 else shp[1:]
        grads[k] = [_Fresh((LOWER_LAYERS[k],) + rest), _Fresh((shp[0] - LOWER_LAYERS[k],) + rest)]

    def norm_bwd_cast(uv, dyv, gv):
        du, dg = _norm_bwd(uv, dyv, gv)
        return du, dg

    def norm_bwd_resid(uv, dhv, dxv, gv):
        du, dg = _norm_bwd(uv, dhv, gv)
        return dxv + du, dg

    for l in reversed(range(DEPTH)):
        j = l // 2
        x_in, h, m, mix_saved, x1, h2, g, u, hid, f = saved[l]
        df, grads["norm_ffn_post"][l] = _rows("norm_ffn_post_bwd", norm_bwd_cast, [f, dx], [gain("norm_ffn_post", l)],
                                              [(d, bf16)], [((1, d), f32)])
        dg, du = _ffn_down_bwd(df, wts["ffn_w_down"], l, g, u, None)[0]
        gi, ll = _grad_slot("ffn_w_down", l)
        grads["ffn_w_down"][gi] = _ffn_wgrad_down(hid, df, grads["ffn_w_down"][gi], ll)
        dh2 = _ffn_up_bwd(dg, du, wts["ffn_w_gate"], wts["ffn_w_up"], l)
        grads["ffn_w_gate"][gi], grads["ffn_w_up"][gi] = _ffn_wgrad_up(
            h2, dg, du, grads["ffn_w_gate"][gi], grads["ffn_w_up"][gi], ll)
        dx1, grads["norm_ffn_pre"][l] = _rows("norm_ffn_pre_bwd", norm_bwd_resid, [x1, dh2, dx],
                                              [gain("norm_ffn_pre", l)], [(d, f32)], [((1, d), f32)])
        dm, grads["norm_mix_post"][l] = _rows("norm_mix_post_bwd", norm_bwd_cast, [m, dx1], [gain("norm_mix_post", l)],
                                              [(d, bf16)], [((1, d), f32)])
        dh = (_attn_bwd if l % 2 == 0 else _rg_bwd)(dm, h, mix_saved, wts, j, grads, exch)
        dx, grads["norm_mix_pre"][l] = _rows("norm_mix_pre_bwd", norm_bwd_resid, [x_in, dh, dx1],
                                             [gain("norm_mix_pre", l)], [(d, f32)], [((1, d), f32)])
    return loss, dx, grads


ANY = pl.BlockSpec(memory_space=pl.ANY)
PACK_COLS = 1024
SMALL_ROWS = 288


def _mesh_pos():
    x, y, c = lax.axis_index("x"), lax.axis_index("y"), lax.axis_index("c")
    return x, y, c, [(1 - x, y), (x, 1 - y), (1 - x, 1 - y)]


def _run_copies(copies):
    for cp in copies:
        cp.start()
    for cp in copies:
        cp.wait()


GATHER_SEMS = 7


def _gather_copies(items, ins, outs, send, recv):
    x, y, c, chips = _mesh_pos()
    q = 2 * x + y
    sibling = (x, y, 1 - c)

    def copy(k, src, dst, to):
        return pltpu.make_async_remote_copy(src_ref=src, dst_ref=dst, send_sem=send.at[k], recv_sem=recv.at[k],
                                            device_id=to, device_id_type=MESH)

    own, sent, passed = [], [], []
    for i, (t, l0, nl) in enumerate(items):
        lay = pl.ds(l0, nl)
        half = ins[t].shape[1] // 2
        rows = pl.ds(pl.multiple_of(c * half, half), half)
        own.append(copy(GATHER_SEMS * i, ins[t].at[lay], outs[t].at[lay, q], sibling))
        for j, (px, py) in enumerate(chips):
            sent.append(copy(GATHER_SEMS * i + 1 + j, ins[t].at[lay, rows], outs[t].at[lay, q, rows], (px, py, c)))
            landed = outs[t].at[lay, 2 * px + py, rows]
            passed.append(copy(GATHER_SEMS * i + 4 + j, landed, landed, sibling))
    return own, sent, passed


def _gather_start(items, ins, outs, send, recv):
    own, sent, _ = _gather_copies(items, ins, outs, send, recv)
    for cp in own + sent:
        cp.start()


def _gather_finish(items, ins, outs, send, recv):
    own, sent, passed = _gather_copies(items, ins, outs, send, recv)
    for arrived, forward in zip(sent, passed):
        arrived.wait_recv()
        forward.start()
    for cp in sent:
        cp.wait_send()
    for cp in own + passed:
        cp.wait()


def _gather_call(items, shards):
    n = len(shards)
    nsem = GATHER_SEMS * len(items)

    def body(*refs):
        ins, outs = refs[:n], refs[n:2 * n]
        _gather_start(items, ins, outs, *refs[2 * n:])
        _gather_finish(items, ins, outs, *refs[2 * n:])

    return pl.pallas_call(
        body, name="weight_all_gather", in_specs=[ANY] * n, out_specs=[ANY] * n,
        out_shape=[SDS((s.shape[0], N_CHIPS) + s.shape[1:], s.dtype) for s in shards],
        scratch_shapes=[pltpu.SemaphoreType.DMA((nsem,)), pltpu.SemaphoreType.DMA((nsem,))])(*shards)


def _call(body, operands, *, name, grid, in_specs, out_specs, out_shape, sem, scratch=(), gather=None):
    if gather is None:
        return pl.pallas_call(body, grid=grid, in_specs=in_specs, out_specs=out_specs, out_shape=out_shape,
                              scratch_shapes=list(scratch), name=name, compiler_params=_cparams(sem))(*operands), None
    start, finish, c_ins, c_io, c_new, nsem = gather
    n_in, n_out, n_scr = len(operands), len(out_shape), len(scratch)
    ni, nio, nco = len(c_ins), len(c_io), len(c_io) + len(c_new)

    def full(*refs):
        ins, sh = refs[:n_in], refs[n_in:n_in + ni]
        outs = refs[n_in + ni + nio:n_in + ni + nio + n_out]
        co = refs[n_in + ni + nio + n_out:n_in + ni + nio + n_out + nco]
        scr = refs[n_in + ni + nio + n_out + nco:]
        ids = [pl.program_id(a) for a in range(len(grid))]
        first = functools.reduce(jnp.logical_and, [i == 0 for i in ids])
        last = functools.reduce(jnp.logical_and, [i == g - 1 for i, g in zip(ids, grid)])

        @pl.when(first)
        def _():
            start(sh, co, scr[n_scr], scr[n_scr + 1])

        body(*ins, *outs, *scr[:n_scr])

        @pl.when(last)
        def _():
            finish(sh, co, scr[n_scr], scr[n_scr + 1])

    res = pl.pallas_call(
        full, grid=grid, in_specs=list(in_specs) + [ANY] * (ni + nio), out_specs=list(out_specs) + [ANY] * nco,
        out_shape=list(out_shape) + [SDS(g.shape, g.dtype) for g in list(c_io) + list(c_new)],
        scratch_shapes=list(scratch) + [pltpu.SemaphoreType.DMA((nsem,)), pltpu.SemaphoreType.DMA((nsem,))],
        input_output_aliases={n_in + ni + t: n_out + t for t in range(nio)}, name=name,
        compiler_params=_cparams(("arbitrary",) * len(grid)))(*operands, *c_ins, *c_io)
    return res[:n_out], res[n_out:]


def _pair_exchange(gs):
    n = len(gs)

    def body(*refs):
        _pair_copies(refs[:n], refs[n:2 * n], *refs[2 * n:], start=True)
        _pair_copies(refs[:n], refs[n:2 * n], *refs[2 * n:], start=False)

    return pl.pallas_call(
        body, name="grad_pair_exchange", in_specs=[ANY] * n, out_specs=[ANY] * n,
        out_shape=_pair_shapes(gs),
        scratch_shapes=[pltpu.SemaphoreType.DMA((n,)), pltpu.SemaphoreType.DMA((n,))])(*gs)


def _pair_shapes(gs):
    return [SDS(g.shape[:2] + (g.shape[2] // 2, g.shape[3]), f32) for g in gs]


def _pair_copies(ins, outs, send, recv, start):
    x, y, c, _ = _mesh_pos()
    for t in range(len(ins)):
        half = ins[t].shape[2] // 2
        src = ins[t].at[:, :, pl.ds(pl.multiple_of((1 - c) * half, SUBLANES), half)]
        cp = pltpu.make_async_remote_copy(src_ref=src, dst_ref=outs[t], send_sem=send.at[t], recv_sem=recv.at[t],
                                          device_id=(x, y, 1 - c), device_id_type=MESH)
        cp.start() if start else cp.wait()


def _pair_carry(gs):
    return (functools.partial(_pair_copies, start=True), functools.partial(_pair_copies, start=False),
            gs, [], _pair_shapes(gs), len(gs))


def _pair_sum(name, g, got, c):
    l, s, r, cols = g.shape

    def body(c_ref, a_ref, b_ref, o_ref):
        o_ref[...] = (a_ref[...] + b_ref[...]).astype(bf16)

    blk = (None, None, r // 2, cols)
    return pl.pallas_call(
        body, name=name, out_shape=SDS(got.shape, bf16),
        grid_spec=pltpu.PrefetchScalarGridSpec(
            num_scalar_prefetch=1, grid=(l, s),
            in_specs=[pl.BlockSpec(blk, lambda i, q, c_ref: (i, q, c_ref[0], 0)),
                      pl.BlockSpec(blk, lambda i, q, c_ref: (i, q, 0, 0))],
            out_specs=pl.BlockSpec(blk, lambda i, q, c_ref: (i, q, 0, 0))),
        compiler_params=_cparams(("parallel", "parallel")))(c, g, got)


def _chip_exchange(hs):
    n = len(hs)

    def body(*refs):
        _chip_copies(refs[:n], refs[n:2 * n], *refs[2 * n:], start=True)
        _chip_copies(refs[:n], refs[n:2 * n], *refs[2 * n:], start=False)

    return pl.pallas_call(
        body, name="grad_chip_exchange", in_specs=[ANY] * n, out_specs=[ANY] * n,
        out_shape=[SDS(h.shape, h.dtype) for h in hs],
        scratch_shapes=[pltpu.SemaphoreType.DMA((3 * n,)), pltpu.SemaphoreType.DMA((3 * n,))])(*hs)


def _chip_copies(ins, outs, send, recv, start):
    x, y, c, chips = _mesh_pos()
    q = 2 * x + y
    for t in range(len(ins)):
        for j, (px, py) in enumerate(chips):
            cp = pltpu.make_async_remote_copy(
                src_ref=ins[t].at[:, 2 * px + py], dst_ref=outs[t].at[:, q], send_sem=send.at[3 * t + j],
                recv_sem=recv.at[3 * t + j], device_id=(px, py, c), device_id_type=MESH)
            cp.start() if start else cp.wait()


def _chip_carry(hs):
    return (functools.partial(_chip_copies, start=True), functools.partial(_chip_copies, start=False),
            hs, [], [SDS(h.shape, h.dtype) for h in hs], 3 * len(hs))


def _chip_sum(name, s, h, pos, l0, layers, into):
    l, _, r, cols = s.shape

    def body(pos_ref, s0, s1, s2, s3, own_ref, *rest):
        vals = [jnp.where(pos_ref[0] == p, own_ref[...], ref[...]).astype(f32) for p, ref in enumerate((s0, s1, s2, s3))]
        rest[-1][...] = ((vals[0] + vals[1]) + vals[2]) + vals[3]

    blk = (None, None, r, cols)
    slot = lambda p: pl.BlockSpec(blk, lambda i, pos_ref: (i, jnp.where(pos_ref[0] == p, (p + 1) % N_CHIPS, p), 0, 0))
    extra, alias = ([], {}) if into is None else ([into], {6: 0})
    return pl.pallas_call(
        body, name=name, out_shape=SDS((layers, 2 * r, cols), f32), input_output_aliases=alias,
        grid_spec=pltpu.PrefetchScalarGridSpec(
            num_scalar_prefetch=1, grid=(l,),
            in_specs=[slot(p) for p in range(N_CHIPS)] + [pl.BlockSpec(blk, lambda i, pos_ref: (i, pos_ref[0], 0, 0))]
            + [ANY] * len(extra),
            out_specs=pl.BlockSpec((None, r, cols), lambda i, pos_ref: (l0 + i, pos_ref[1], 0))),
        compiler_params=_cparams(("parallel",)))(pos, s, s, s, s, h, *extra)


def _pair_gather(fulls):
    n = len(fulls)

    def body(*refs):
        ins, outs = refs[:n], refs[n:2 * n]
        send, recv = refs[2 * n:]
        x, y, c, _ = _mesh_pos()
        copies = []
        for t in range(n):
            half = outs[t].shape[1] // 2
            rows = outs[t].at[:, pl.ds(pl.multiple_of(c * half, SUBLANES), half)]
            copies.append(pltpu.make_async_remote_copy(
                src_ref=rows, dst_ref=rows, send_sem=send.at[t], recv_sem=recv.at[t],
                device_id=(x, y, 1 - c), device_id_type=MESH))
        _run_copies(copies)

    return pl.pallas_call(
        body, name="grad_pair_gather", in_specs=[ANY] * n, out_specs=[ANY] * n,
        out_shape=[SDS(f.shape, f32) for f in fulls], input_output_aliases={t: t for t in range(n)},
        scratch_shapes=[pltpu.SemaphoreType.DMA((n,)), pltpu.SemaphoreType.DMA((n,))])(*fulls)


COL_SHARDED = ("attn_w_in", "rg_w_in", "ffn_w_gate", "ffn_w_up")
ROW_SHARDED = ("attn_w_out", "rg_w_out")
GATES = ("rg_w_a", "rg_w_i")
VECTORS = ("rg_conv_w", "rg_conv_b", "rg_b_a", "rg_b_i", "rg_lambda")
REPLICATED = ("norm_mix_pre", "norm_mix_post", "norm_ffn_pre", "norm_ffn_post", "attn_rel_bias")
BIG_GRADS = COL_SHARDED + ROW_SHARDED + ("ffn_w_down",)
SMALL_GRADS = GATES + VECTORS + REPLICATED
WEIGHTS =("attn_w_in", "attn_rel_bias", "attn_w_out", "rg_w_in", "rg_conv_w", "rg_conv_b", "rg_w_a", "rg_b_a",
           "rg_w_i", "rg_b_i", "rg_lambda", "rg_w_out", "norm_mix_pre", "norm_mix_post", "norm_ffn_pre",
           "norm_ffn_post", "ffn_w_gate", "ffn_w_up", "ffn_w_down")
SMALL = VECTORS + REPLICATED


GATHER_PARTS = {
    "first": (("attn_w_in", 0, 1), ("attn_w_out", 0, 1), ("rg_w_a", 0, 8), ("rg_w_i", 0, 8), ("vec", 0, 1)),
    "chunk_attn_fwd": (("ffn_w_gate", 0, 1), ("ffn_w_up", 0, 1), ("ffn_w_down", 0, 1), ("rg_w_in", 0, 1),
                       ("rg_w_out", 0, 1)),
    "sb_attn_fwd": (("ffn_w_gate", 1, 3), ("ffn_w_up", 1, 3), ("ffn_w_down", 1, 3)),
    "ffn_up": (("rg_w_in", 1, 1), ("rg_w_out", 1, 1), ("attn_w_in", 1, 1), ("attn_w_out", 1, 1)),
}


TRANSPOSED = ("ffn_w_gate", "ffn_w_up")


def _natural(name, a):
    return jnp.swapaxes(a, 1, 2) if name in TRANSPOSED else a


class _WeightGather:
    def __init__(self, w):
        self.w = w
        self.names = list(COL_SHARDED + ROW_SHARDED + GATES + ("ffn_w_down", "vec"))
        self.shards = {}
        for k in self.names[:-1]:
            a = _natural(k, w[k]).astype(bf16)
            self.shards[k] = a.reshape((-1,) + a.shape[-2:])
        self.shards["vec"] = jnp.concatenate([w[k].reshape(-1) for k in VECTORS]).reshape(1, -1, LANES)
        got = _gather_call(self._items("first", self.names), [self.shards[k] for k in self.names])
        self.raw = dict(zip(self.names, got))

    @staticmethod
    def _items(part, names):
        return [(names.index(k), l0, nl) for k, l0, nl in GATHER_PARTS[part]]

    def part(self, part):
        names = list(dict.fromkeys(k for k, _, _ in GATHER_PARTS[part]))
        items = self._items(part, names)
        return (functools.partial(_gather_start, items), functools.partial(_gather_finish, items),
                [self.shards[k] for k in names], [self.raw[k] for k in names], [], GATHER_SEMS * len(items)), names

    def views(self):
        got, w = self.raw, self.w
        out = {k: w[k] for k in REPLICATED}
        for k in COL_SHARDED + ("ffn_w_down",):
            out[k] = got[k]
        for k in ROW_SHARDED:
            l, s, ks, n = got[k].shape
            out[k] = got[k].reshape(l, 1, s * ks, n)
        for k in GATES:
            out[k] = got[k].reshape(2, LRU_BLOCKS, LRU_BW, LRU_BW)
        vec = got["vec"].reshape(N_CHIPS, -1)
        off = 0
        for k in VECTORS:
            shp = w[k].shape
            n = int(np.prod(shp))
            piece = vec[:, off:off + n].reshape((N_CHIPS,) + shp)
            off += n
            if k == "rg_conv_w":
                out[k] = piece.reshape(N_CHIPS, 2, 4, 256).transpose(1, 2, 0, 3).reshape(2, 4, D_MODEL)
            elif k in ("rg_b_a", "rg_b_i"):
                out[k] = piece.transpose(1, 2, 0, 3).reshape(2, 1, D_MODEL)
            else:
                out[k] = piece.transpose(1, 0, 2).reshape(2, 1, D_MODEL)
        return out


def _carried(plan, part, wts, fn, *args):
    if plan is None:
        return fn(*args, None)[0]
    gather, names = plan.part(part)
    out, new = fn(*args, gather)
    plan.raw.update(zip(names, new))
    wts.update(plan.views())
    return out


def _grad_blocks(name, g):
    st = jnp.stack([g[i] for i in sorted(g)])
    if name in GATES:
        st = st.reshape(2, LRU_BLOCKS, N_CHIPS, LRU_BW // N_CHIPS, LRU_BW).transpose(2, 0, 1, 3, 4)
    elif name == "rg_conv_w":
        st = st.reshape(2, 4, N_CHIPS, -1).transpose(2, 0, 1, 3)
    elif name in ("rg_b_a", "rg_b_i"):
        st = st.reshape(2, LRU_BLOCKS, N_CHIPS, -1).transpose(2, 0, 1, 3)
    elif name in VECTORS:
        st = st.reshape(2, N_CHIPS, -1).transpose(1, 0, 2)
    else:
        st = jnp.broadcast_to(st.reshape(1, -1), (N_CHIPS, st.size))
    return st.reshape(N_CHIPS, -1)


class _GradExchange:
    def __init__(self):
        self.c = lax.axis_index("c").astype(jnp.int32).reshape(1)
        self.pos = jnp.stack([2 * lax.axis_index("x") + lax.axis_index("y"), lax.axis_index("c")]).astype(jnp.int32)
        self.up = self.got_up = self.parts_up = self.slots_up = None

    @staticmethod
    def _blocked(g):
        if g.ndim == 3:
            g = g.reshape(g.shape[0], N_CHIPS, g.shape[1] // N_CHIPS, g.shape[2])
        return g

    def _sums(self, tag, names, gs, got):
        return [_pair_sum("grad_pair_sum_" + tag + k, g, r, self.c) for k, g, r in zip(names, gs, got)]

    def upper_carry(self, grads):
        self.up = [self._blocked(grads[k][1]) for k in BIG_GRADS]
        return _pair_carry(self.up)

    def upper_got(self, got):
        self.got_up = got

    def carry(self):
        self.parts_up = self._sums("up_", BIG_GRADS, self.up, self.got_up)
        return _chip_carry(self.parts_up)

    def carried(self, slots):
        self.slots_up = slots

    def finish(self, grads, shard_shapes):
        if self.got_up is None:
            self.upper_carry(grads)
            self.got_up = _pair_exchange(self.up)
        if self.slots_up is None:
            self.carry()
            self.slots_up = _chip_exchange(self.parts_up)
        blocks = [_grad_blocks(k, grads[k]) for k in SMALL_GRADS]
        used = sum(b.shape[1] for b in blocks)
        small = jnp.concatenate(blocks + [jnp.zeros((N_CHIPS, SMALL_ROWS * PACK_COLS - used), f32)], axis=1)
        names = tuple(k for k in BIG_GRADS if LOWER_LAYERS[k]) + ("small",)
        gs = [self._blocked(grads[k][0]) for k in names[:-1]] + [small.reshape(1, N_CHIPS, SMALL_ROWS, PACK_COLS)]
        parts = dict(zip(names, self._sums("lo_", names, gs, _pair_exchange(gs))))
        slots = dict(zip(names, _chip_exchange([parts[k] for k in names])))
        fulls = []
        for i, k in enumerate(BIG_GRADS):
            nlo, nup = LOWER_LAYERS[k], self.parts_up[i].shape[0]
            full = _chip_sum("grad_chip_sum_up_" + k, self.slots_up[i], self.parts_up[i], self.pos, nlo, nlo + nup, None)
            if nlo:
                full = _chip_sum("grad_chip_sum_lo_" + k, slots[k], parts[k], self.pos, 0, nlo + nup, full)
            fulls.append(full)
        fulls.append(_chip_sum("grad_chip_sum_lo_small", slots["small"], parts["small"], self.pos, 0, 1, None))
        full = _pair_gather(fulls)
        out = {k: f.reshape(shard_shapes[k]) for k, f in zip(BIG_GRADS, full)}
        flat, off = full[-1].reshape(-1), 0
        for k in SMALL_GRADS:
            n = int(np.prod(shard_shapes[k]))
            out[k] = flat[off:off + n].reshape(shard_shapes[k])
            off += n
        return out


def _adamw_fn(w, g, m, v):
    m = ADAM_B1 * m + (1.0 - ADAM_B1) * g
    v = ADAM_B2 * v + (1.0 - ADAM_B2) * (g * g)
    m_hat = m / (1.0 - ADAM_B1 ** ADAM_STEP)
    v_hat = v / (1.0 - ADAM_B2 ** ADAM_STEP)
    return -ADAM_LR * (m_hat / (jnp.sqrt(v_hat) + ADAM_EPS) + ADAM_WD * w), m, v


def _adamw(name, w, g, m, v):
    shp = w.shape
    if w.size >= 1 << 16:
        width = shp[-1]
        ops = [a.reshape(-1, width) for a in (w, g, m, v)]
        res = _rows(name, _adamw_fn, ops, [], [(width, f32)] * 3)
        return [r.reshape(shp) for r in res]
    n = w.size
    rows = -(-n // (SUBLANES * LANES)) * SUBLANES
    ops = [jnp.pad(a.reshape(-1), (0, rows * LANES - n)).reshape(rows, LANES) for a in (w, g, m, v)]
    res = _rows(name, _adamw_fn, ops, [], [(LANES, f32)] * 3, tr=rows)
    return [r.reshape(-1)[:n].reshape(shp) for r in res]


def kernel(x, attn_w_in, attn_rel_bias, attn_w_out, rg_w_in, rg_conv_w, rg_conv_b, rg_w_a, rg_b_a, rg_w_i, rg_b_i, rg_lambda, rg_w_out, norm_mix_pre, norm_mix_post, norm_ffn_pre, norm_ffn_post, ffn_w_gate, ffn_w_up, ffn_w_down, loss_target, m_attn_w_in, m_attn_rel_bias, m_attn_w_out, m_rg_w_in, m_rg_conv_w, m_rg_conv_b, m_rg_w_a, m_rg_b_a, m_rg_w_i, m_rg_b_i, m_rg_lambda, m_rg_w_out, m_norm_mix_pre, m_norm_mix_post, m_norm_ffn_pre, m_norm_ffn_post, m_ffn_w_gate, m_ffn_w_up, m_ffn_w_down, v_attn_w_in, v_attn_rel_bias, v_attn_w_out, v_rg_w_in, v_rg_conv_w, v_rg_conv_b, v_rg_w_a, v_rg_b_a, v_rg_w_i, v_rg_b_i, v_rg_lambda, v_rg_w_out, v_norm_mix_pre, v_norm_mix_post, v_norm_ffn_pre, v_norm_ffn_post, v_ffn_w_gate, v_ffn_w_up, v_ffn_w_down):
    w = dict(zip(WEIGHTS, (attn_w_in, attn_rel_bias, attn_w_out, rg_w_in, rg_conv_w, rg_conv_b, rg_w_a, rg_b_a, rg_w_i,
                           rg_b_i, rg_lambda, rg_w_out, norm_mix_pre, norm_mix_post, norm_ffn_pre, norm_ffn_post,
                           ffn_w_gate, ffn_w_up, ffn_w_down)))
    m = dict(zip(WEIGHTS, (m_attn_w_in, m_attn_rel_bias, m_attn_w_out, m_rg_w_in, m_rg_conv_w, m_rg_conv_b, m_rg_w_a,
                           m_rg_b_a, m_rg_w_i, m_rg_b_i, m_rg_lambda, m_rg_w_out, m_norm_mix_pre, m_norm_mix_post,
                           m_norm_ffn_pre, m_norm_ffn_post, m_ffn_w_gate, m_ffn_w_up, m_ffn_w_down)))
    v = dict(zip(WEIGHTS, (v_attn_w_in, v_attn_rel_bias, v_attn_w_out, v_rg_w_in, v_rg_conv_w, v_rg_conv_b, v_rg_w_a,
                           v_rg_b_a, v_rg_w_i, v_rg_b_i, v_rg_lambda, v_rg_w_out, v_norm_mix_pre, v_norm_mix_post,
                           v_norm_ffn_pre, v_norm_ffn_post, v_ffn_w_gate, v_ffn_w_up, v_ffn_w_down)))
    plan = _WeightGather(w)
    exch = _GradExchange()
    loss, dx, grads = _local_step(x[0], loss_target[0], plan.views(), plan, exch)
    loss = lax.psum(loss, ("x", "y", "c"))
    g = exch.finish(grads, {k: _natural(k, w[k]).shape for k in WEIGHTS})

    big = [k for k in WEIGHTS if k not in SMALL]
    upd = {}
    for k in big:
        res = _adamw("adamw_" + k, _natural(k, w[k]), g[k], _natural(k, m[k]), _natural(k, v[k]))
        upd[k] = [_natural(k, r) for r in res]
        g[k] = _natural(k, g[k])
    cat = lambda d: jnp.concatenate([d[k].reshape(-1) for k in SMALL])
    small = _adamw("adamw_small", cat(w), cat(g), cat(m), cat(v))
    off = 0
    for k in SMALL:
        n = w[k].size
        upd[k] = [r[off:off + n].reshape(w[k].shape) for r in small]
        off += n
    return (loss, dx[None], *[g[k] for k in WEIGHTS], *[upd[k][0] for k in WEIGHTS],
            *[upd[k][1] for k in WEIGHTS], *[upd[k][2] for k in WEIGHTS])
```

```python
import functools

import numpy as np
import jax
import jax.numpy as jnp
from jax import lax
from jax.experimental import pallas as pl
from jax.experimental.pallas import tpu as pltpu

f32 = jnp.float32
bf16 = jnp.bfloat16
SDS = jax.ShapeDtypeStruct
MESH = pl.DeviceIdType.MESH

D_MODEL = 1024
N_CHIPS = 4
DEPTH = 4
HEAD_DIM = 64
CHUNK = 64
N_LEFT = 8
REL_CLIP = 256
A_W = 512
LRU_BLOCKS = 4
LRU_BW = 256
LRU_C = 8.0
D_FF = 2816
RMS_EPS = 1e-6
LANES = 128
SUBLANES = 8
VMEM_LIMIT = 56 * 1024 * 1024

QB_A = 2 * CHUNK
QSUB_A = 4
KW_A = QB_A + N_LEFT * CHUNK
PAD_A = N_LEFT * CHUNK
EXT_A = 768
SB_BLK = 256
QSUB_B = 2
SB_DEAD = -110.0

ADAM_LR, ADAM_B1, ADAM_B2, ADAM_EPS, ADAM_WD, ADAM_STEP = 0.001, 0.9, 0.999, 1e-08, 0.01, 10


def _cparams(sem):
    return pltpu.CompilerParams(dimension_semantics=sem, vmem_limit_bytes=VMEM_LIMIT)


def _gemm(name, operands, in_specs, o_spec, out_shape, grid, dims, acc_shape, into=None):
    nred = grid[2]
    npair = len(operands) // 2
    nin = 2 * npair + (into is not None)

    def body(*refs):
        o_ref = refs[nin]
        p = None
        for t in range(npair):
            d = lax.dot_general(refs[2 * t][...], refs[2 * t + 1][...], (dims, ((), ())),
                                preferred_element_type=f32)
            p = d if p is None else p + d
        if nred == 1:
            o_ref[...] = p.astype(o_ref.dtype)
        else:
            acc = refs[nin + 1]
            r = pl.program_id(2)

            @pl.when(r == 0)
            def _():
                acc[...] = p

            @pl.when(r > 0)
            def _():
                acc[...] += p

            @pl.when(r == nred - 1)
            def _():
                o_ref[...] = acc[...].astype(o_ref.dtype)

    scratch = [] if nred == 1 else [pltpu.VMEM(acc_shape, f32)]
    extra, alias = ([], {}) if into is None else ([into], {2 * npair: 0})
    return pl.pallas_call(
        body, grid=grid, in_specs=list(in_specs) + [pl.BlockSpec(memory_space=pl.ANY)] * len(extra),
        out_specs=o_spec, out_shape=out_shape, scratch_shapes=scratch, name=name, input_output_aliases=alias,
        compiler_params=_cparams(("parallel", "parallel", "arbitrary")))(*operands, *extra)


LOWER_LAYERS = {"attn_w_in": 1, "attn_w_out": 1, "rg_w_in": 0, "rg_w_out": 0,
                "ffn_w_gate": 0, "ffn_w_up": 0, "ffn_w_down": 0}


def _grad_slot(name, l):
    n = LOWER_LAYERS[name]
    return (0, l) if l < n else (1, l - n)


class _Fresh:
    def __init__(self, shape):
        self.shape = tuple(shape)


def _into(buf):
    return None if isinstance(buf, _Fresh) else buf


NN = ((1,), (0,))
NT = ((1,), (1,))
TN = ((0,), (0,))


def _tile(t, want=1024):
    return min(want, t)


def _mm_cols(name, a, w, l, out_dtype):
    t, k = a.shape
    _, s, _, ns = w.shape
    tm = _tile(t)
    return _gemm(
        name, [a, w],
        [pl.BlockSpec((tm, k), lambda i, j, r: (i, 0)),
         pl.BlockSpec((None, None, k, ns), lambda i, j, r: (l, j, 0, 0))],
        pl.BlockSpec((tm, ns), lambda i, j, r: (i, j)),
        SDS((t, s * ns), out_dtype), (t // tm, s, 1), NN, None)


def _mm_cols_t(name, dy, w, l, out_dtype):
    t = dy.shape[0]
    _, s, k, ns = w.shape
    tm = _tile(t)
    return _gemm(
        name, [dy, w],
        [pl.BlockSpec((tm, ns), lambda i, j, r: (i, r)),
         pl.BlockSpec((None, None, k, ns), lambda i, j, r: (l, r, 0, 0))],
        pl.BlockSpec((tm, k), lambda i, j, r: (i, 0)),
        SDS((t, k), out_dtype), (t // tm, 1, s), NT, (tm, k))


def _mm_wgrad_cols(name, a, dy, buf, l):
    t, k = a.shape
    _, s, _, ns = buf.shape
    tt = _tile(t)
    return _gemm(
        name, [a, dy],
        [pl.BlockSpec((tt, k), lambda i, j, r: (r, 0)),
         pl.BlockSpec((tt, ns), lambda i, j, r: (r, i))],
        pl.BlockSpec((None, None, k, ns), lambda i, j, r: (l, i, 0, 0)),
        SDS(buf.shape, f32), (s, 1, t // tt), TN, (k, ns), into=_into(buf))


def _mm_rows(name, parts, w, l, out_dtype):
    t = parts[0].shape[0]
    n = w.shape[3]
    tm = _tile(t)
    ops, specs = [], []
    for p_i, a in enumerate(parts):
        kp = a.shape[1]
        ops += [a, w]
        specs += [pl.BlockSpec((tm, kp), lambda i, j, r: (i, 0)),
                  pl.BlockSpec((None, None, kp, n), lambda i, j, r, p_i=p_i: (l, 0, p_i, 0))]
    return _gemm(name, ops, specs, pl.BlockSpec((tm, n), lambda i, j, r: (i, 0)),
                 SDS((t, n), out_dtype), (t // tm, 1, 1), NN, None)


def _mm_rows_t(name, dy, w, l, out_dtype):
    t, n = dy.shape
    k = w.shape[2]
    tm = _tile(t)
    return _gemm(
        name, [dy, w],
        [pl.BlockSpec((tm, n), lambda i, j, r: (i, 0)),
         pl.BlockSpec((None, None, k, n), lambda i, j, r: (l, 0, 0, 0))],
        pl.BlockSpec((tm, k), lambda i, j, r: (i, 0)),
        SDS((t, k), out_dtype), (t // tm, 1, 1), NT, None)


def _mm_wgrad(name, a, dy, buf, l, part=0):
    t, k = a.shape
    n = dy.shape[1]
    tt = _tile(t)
    return _gemm(
        name, [a, dy],
        [pl.BlockSpec((tt, k), lambda i, j, r: (r, 0)),
         pl.BlockSpec((tt, n), lambda i, j, r: (r, 0))],
        pl.BlockSpec((None, k, n), lambda i, j, r: (l, part, 0)),
        SDS(buf.shape, f32), (1, 1, t // tt), TN, (k, n), into=_into(buf))


def _ffn_up(h, wg, wu, l, gather):
    t, k = h.shape
    s, fs = wg.shape[1], wg.shape[2]
    tm = _tile(t)

    def body(h_ref, wg_ref, wu_ref, g_ref, u_ref, hid_ref):
        hv = h_ref[...]
        g = lax.dot_general(hv, wg_ref[...], (NT, ((), ())), preferred_element_type=f32)
        u = lax.dot_general(hv, wu_ref[...], (NT, ((), ())), preferred_element_type=f32)
        g_ref[...] = g.astype(bf16)
        u_ref[...] = u.astype(bf16)
        hid_ref[...] = (g * jax.nn.sigmoid(g) * u).astype(bf16)

    wspec = pl.BlockSpec((None, None, fs, k), lambda j, i: (l, j, 0, 0))
    ospec = pl.BlockSpec((None, tm, fs), lambda j, i: (j, i, 0))
    return _call(
        body, [h, wg, wu], grid=(s, t // tm), name="ffn_up",
        in_specs=[pl.BlockSpec((tm, k), lambda j, i: (i, 0)), wspec, wspec],
        out_specs=[ospec, ospec, ospec], out_shape=[SDS((s, t, fs), bf16)] * 3,
        sem=("parallel", "parallel"), gather=gather)


def _ffn_down(hid, wd, l):
    s, t, fs = hid.shape
    n = wd.shape[3]
    tm = _tile(t, 512)
    ops, specs = [], []
    for r in range(s):
        ops += [hid, wd]
        specs += [pl.BlockSpec((None, tm, fs), lambda i, j, k, r=r: (r, i, 0)),
                  pl.BlockSpec((None, None, fs, n), lambda i, j, k, r=r: (l, r, 0, 0))]
    return _gemm("ffn_down", ops, specs, pl.BlockSpec((tm, n), lambda i, j, k: (i, 0)),
                 SDS((t, n), f32), (t // tm, 1, 1), NN, None)


def _ffn_down_bwd(df, wd, l, g, u, gather):
    t, n = df.shape
    s, fs = wd.shape[1], wd.shape[2]
    tm = _tile(t)

    def body(df_ref, wd_ref, g_ref, u_ref, dg_ref, du_ref):
        dh = lax.dot_general(df_ref[...], wd_ref[...], (NT, ((), ())), preferred_element_type=f32)
        gv = g_ref[...].astype(f32)
        uv = u_ref[...].astype(f32)
        sg = jax.nn.sigmoid(gv)
        du_ref[...] = (dh * gv * sg).astype(bf16)
        dg_ref[...] = (dh * uv * (sg * (1.0 + gv * (1.0 - sg)))).astype(bf16)

    bspec = pl.BlockSpec((None, tm, fs), lambda j, i: (j, i, 0))
    return _call(
        body, [df, wd, g, u], grid=(s, t // tm), name="ffn_down_bwd",
        in_specs=[pl.BlockSpec((tm, n), lambda j, i: (i, 0)),
                  pl.BlockSpec((None, None, fs, n), lambda j, i: (l, j, 0, 0)), bspec, bspec],
        out_specs=[bspec, bspec], out_shape=[SDS((s, t, fs), bf16)] * 2,
        sem=("parallel", "parallel"), gather=gather)


def _ffn_up_bwd(dg, du, wg, wu, l):
    s, t, fs = dg.shape
    k = wg.shape[3]
    tm = _tile(t, 512)
    ops, specs = [], []
    for r in range(s):
        aspec = pl.BlockSpec((None, tm, fs), lambda i, j, kk, r=r: (r, i, 0))
        wspec = pl.BlockSpec((None, None, fs, k), lambda i, j, kk, r=r: (l, r, 0, 0))
        ops += [dg, wg, du, wu]
        specs += [aspec, wspec, aspec, wspec]
    return _gemm("ffn_up_bwd", ops, specs, pl.BlockSpec((tm, k), lambda i, j, kk: (i, 0)),
                 SDS((t, k), f32), (t // tm, 1, 1), NN, None)


def _ffn_wgrad_up(h, dg, du, buf_g, buf_u, l):
    t, k = h.shape
    s, _, fs = dg.shape
    tt = _tile(t)
    nred = t // tt

    fresh = isinstance(buf_g, _Fresh)

    def body(*refs):
        h_ref, dg_ref, du_ref = refs[:3]
        og_ref, ou_ref, acc_g, acc_u = refs[-4:]
        r = pl.program_id(1)
        hv = h_ref[...]
        pg = lax.dot_general(dg_ref[...], hv, (TN, ((), ())), preferred_element_type=f32)
        pu = lax.dot_general(du_ref[...], hv, (TN, ((), ())), preferred_element_type=f32)

        @pl.when(r == 0)
        def _():
            acc_g[...] = pg
            acc_u[...] = pu

        @pl.when(r > 0)
        def _():
            acc_g[...] += pg
            acc_u[...] += pu

        @pl.when(r == nred - 1)
        def _():
            og_ref[...] = acc_g[...]
            ou_ref[...] = acc_u[...]

    dspec = pl.BlockSpec((None, tt, fs), lambda i, r: (i, r, 0))
    ospec = pl.BlockSpec((None, None, fs, k), lambda i, r: (l, i, 0, 0))
    extra, alias = ([], {}) if fresh else ([buf_g, buf_u], {3: 0, 4: 1})
    return pl.pallas_call(
        body, grid=(s, nred), name="ffn_wgrad_up",
        in_specs=[pl.BlockSpec((tt, k), lambda i, r: (r, 0)), dspec, dspec] + [ANY] * len(extra),
        out_specs=[ospec, ospec], out_shape=[SDS(buf_g.shape, f32), SDS(buf_u.shape, f32)],
        scratch_shapes=[pltpu.VMEM((fs, k), f32)] * 2, input_output_aliases=alias,
        compiler_params=_cparams(("parallel", "arbitrary")))(h, dg, du, *extra)


def _ffn_wgrad_down(hid, df, buf, l):
    s, t, fs = hid.shape
    n = df.shape[1]
    tt = _tile(t)
    return _gemm(
        "ffn_wgrad_down", [hid, df],
        [pl.BlockSpec((None, tt, fs), lambda i, j, r: (i, r, 0)),
         pl.BlockSpec((tt, n), lambda i, j, r: (r, 0))],
        pl.BlockSpec((None, None, fs, n), lambda i, j, r: (l, i, 0, 0)),
        SDS(buf.shape, f32), (s, 1, t // tt), TN, (fs, n), into=_into(buf))


def _rows(name, fn, rows, consts, row_outs, acc_outs=(), tr=512):
    rows = [r if isinstance(r, tuple) else (r, r.shape[1], 0) for r in rows]
    t = rows[0][0].shape[0]
    tr = max(d for d in range(SUBLANES, min(tr, t) + 1, SUBLANES) if t % d == 0)
    nin = len(rows) + len(consts)
    no, na = len(row_outs), len(acc_outs)

    def body(*refs):
        vals = fn(*[r[...] for r in refs[:nin]])
        if not isinstance(vals, (tuple, list)):
            vals = (vals,)
        for k in range(no):
            refs[nin + k][...] = vals[k].astype(refs[nin + k].dtype)
        first = pl.program_id(0) == 0
        for k in range(na):
            ref, val = refs[nin + no + k], vals[no + k]

            @pl.when(first)
            def _(ref=ref, val=val):
                ref[...] = val

            @pl.when(jnp.logical_not(first))
            def _(ref=ref, val=val):
                ref[...] += val

    in_specs = [pl.BlockSpec((tr, w), lambda i, cb=cb: (i, cb)) for (_, w, cb) in rows]
    in_specs += [pl.BlockSpec(c.shape, lambda i, nd=c.ndim: (0,) * nd) for c in consts]
    out_specs = [pl.BlockSpec((tr, w), lambda i: (i, 0)) for (w, _) in row_outs]
    out_specs += [pl.BlockSpec(s, lambda i, nd=len(s): (0,) * nd) for (s, _) in acc_outs]
    out_shape = [SDS((t, w), dt) for (w, dt) in row_outs] + [SDS(s, dt) for (s, dt) in acc_outs]
    res = pl.pallas_call(
        body, grid=(t // tr,), in_specs=in_specs, out_specs=out_specs, out_shape=out_shape,
        name=name, compiler_params=_cparams(("arbitrary",)))(*[r[0] for r in rows], *consts)
    return res


def _rstd(x):
    return lax.rsqrt(jnp.mean(x * x, axis=-1, keepdims=True) + RMS_EPS)


def _norm_fwd(x, g):
    return x * _rstd(x) * g


def _norm_bwd(u, dy, g):
    r = _rstd(u)
    n = u * r
    dn = dy * g
    du = r * (dn - n * jnp.mean(dn * n, axis=-1, keepdims=True))
    return du, jnp.sum(dy * n, axis=0, keepdims=True)


def _gelu(x):
    c = 0.7978845608028654
    return 0.5 * x * (1.0 + jnp.tanh(c * (x + 0.044715 * x * x * x)))


def _gelu_grad(x):
    c = 0.7978845608028654
    th = jnp.tanh(c * (x + 0.044715 * x * x * x))
    return 0.5 * (1.0 + th) + 0.5 * x * (1.0 - th * th) * c * (1.0 + 3.0 * 0.044715 * x * x)


def _mask_heads(x):
    lane = lax.broadcasted_iota(jnp.int32, x.shape, 1)
    return [jnp.where((lane >= h * HEAD_DIM) & (lane < (h + 1) * HEAD_DIM), x, jnp.zeros_like(x))
            for h in range(LANES // HEAD_DIM)]


def _chunk_valid(start):
    qi = lax.broadcasted_iota(jnp.int32, (QB_A, KW_A), 0)
    kj = lax.broadcasted_iota(jnp.int32, (QB_A, KW_A), 1)
    qc = qi // CHUNK
    kc = kj // CHUNK
    return (kc >= qc) & (kc <= qc + N_LEFT) & (kj + start >= PAD_A)


def _chunk_probs(q, k, bias, valid):
    s = lax.dot_general(q, k, (NT, ((), ())), preferred_element_type=f32) * (HEAD_DIM ** -0.5) + bias
    s = jnp.where(valid, s, -1e30)
    p = jnp.exp(s - jnp.max(s, axis=-1, keepdims=True))
    return p / jnp.sum(p, axis=-1, keepdims=True)


def _chunk_attn_fwd(proj, kpad, vpad, bias, gather):
    t = proj.shape[0]
    tp = kpad.shape[0]
    step = QSUB_A * QB_A

    def body(q_ref, k_ref, v_ref, b_ref, o_ref):
        for sb in range(QSUB_A):
            start = pl.multiple_of((pl.program_id(1) * QSUB_A + sb) * QB_A, QB_A)
            rows = pl.ds(sb * QB_A, QB_A)
            valid = _chunk_valid(start)
            kw = k_ref[pl.ds(start, KW_A), :]
            qm = _mask_heads(q_ref[rows, :])
            vm = _mask_heads(v_ref[pl.ds(start, KW_A), :])
            o = None
            for h in range(len(qm)):
                p = _chunk_probs(qm[h], kw, b_ref[h], valid)
                d = jnp.dot(p.astype(bf16), vm[h], preferred_element_type=f32)
                o = d if o is None else o + d
            o_ref[rows, :] = o.astype(bf16)

    kv_spec = pl.BlockSpec((tp, LANES), lambda hp, qb: (0, hp))
    outs, new = _call(
        body, [proj, kpad, vpad, bias], grid=(A_W // LANES, t // step), name="chunk_attn_fwd",
        in_specs=[pl.BlockSpec((step, LANES), lambda hp, qb: (qb, hp)), kv_spec, kv_spec,
                  pl.BlockSpec((2, QB_A, KW_A), lambda hp, qb: (hp, 0, 0))],
        out_specs=[pl.BlockSpec((step, LANES), lambda hp, qb: (qb, hp))],
        out_shape=[SDS((t, A_W), bf16)], sem=("parallel", "arbitrary"), gather=gather)
    return outs[0], new


def _chunk_attn_bwd(proj, kpad, vpad, bias, dout, gather):
    t = proj.shape[0]
    tp = kpad.shape[0]
    step = QSUB_A * QB_A

    def body(q_ref, k_ref, v_ref, b_ref, do_ref, dq_ref, dk_ref, dv_ref, db_ref):
        qb = pl.program_id(1)

        @pl.when(qb == 0)
        def _():
            dk_ref[...] = jnp.zeros_like(dk_ref)
            dv_ref[...] = jnp.zeros_like(dv_ref)
            db_ref[...] = jnp.zeros_like(db_ref)

        for sb in range(QSUB_A):
            start = pl.multiple_of((qb * QSUB_A + sb) * QB_A, QB_A)
            rows = pl.ds(sb * QB_A, QB_A)
            win = pl.ds(start, KW_A)
            valid = _chunk_valid(start)
            kw = k_ref[win, :]
            vw = v_ref[win, :]
            qm = _mask_heads(q_ref[rows, :])
            dom = _mask_heads(do_ref[rows, :])
            km = _mask_heads(kw)
            dq = dk = dv = None
            for h in range(len(qm)):
                p = _chunk_probs(qm[h], kw, b_ref[h], valid)
                dp = lax.dot_general(dom[h], vw, (NT, ((), ())), preferred_element_type=f32)
                ds = p * (dp - jnp.sum(dp * p, axis=-1, keepdims=True))
                db_ref[h] += ds
                dsb = (ds * (HEAD_DIM ** -0.5)).astype(bf16)
                terms = (jnp.dot(dsb, km[h], preferred_element_type=f32),
                         lax.dot_general(dsb, qm[h], (TN, ((), ())), preferred_element_type=f32),
                         lax.dot_general(p.astype(bf16), dom[h], (TN, ((), ())), preferred_element_type=f32))
                dq, dk, dv = terms if dq is None else (dq + terms[0], dk + terms[1], dv + terms[2])
            dq_ref[rows, :] = dq.astype(bf16)
            dk_ref[win, :] += dk
            dv_ref[win, :] += dv

    kv_spec = pl.BlockSpec((tp, LANES), lambda hp, qb: (0, hp))
    q_spec = pl.BlockSpec((step, LANES), lambda hp, qb: (qb, hp))
    b_spec = pl.BlockSpec((2, QB_A, KW_A), lambda hp, qb: (hp, 0, 0))
    return _call(
        body, [proj, kpad, vpad, bias, dout], grid=(A_W // LANES, t // step), name="chunk_attn_bwd",
        in_specs=[q_spec, kv_spec, kv_spec, b_spec, q_spec],
        out_specs=[q_spec, kv_spec, kv_spec, b_spec],
        out_shape=[SDS((t, A_W), bf16), SDS((tp, A_W), f32), SDS((tp, A_W), f32),
                   SDS((2 * A_W // LANES, QB_A, KW_A), f32)],
        sem=("parallel", "arbitrary"), gather=gather)


def _bias_ext(table):
    flat = PAD_A + QB_A - 1 - REL_CLIP
    top = jnp.broadcast_to(table[:, 2 * REL_CLIP:], (table.shape[0], flat))
    lo = 2 * REL_CLIP - (EXT_A - 1 - flat)
    return jnp.concatenate([top, jnp.flip(table[:, lo:], axis=1)], axis=1)


def _bias_window(table):
    nh = table.shape[0]
    e = jnp.broadcast_to(_bias_ext(table)[:, None, :], (nh, QB_A, EXT_A)).reshape(nh, QB_A * EXT_A)
    m = e[:, :QB_A * (EXT_A - 1)].reshape(nh, QB_A, EXT_A - 1)
    return m[:, :, QB_A - 1:]


def _bias_window_grad(dbias):
    nh = dbias.shape[0]
    m = jnp.pad(dbias, ((0, 0), (0, 0), (QB_A - 1, 0))).reshape(nh, QB_A * (EXT_A - 1))
    dext = jnp.sum(jnp.pad(m, ((0, 0), (0, QB_A))).reshape(nh, QB_A, EXT_A), axis=1)
    flat = PAD_A + QB_A - 1 - REL_CLIP
    lo = 2 * REL_CLIP - (EXT_A - 1 - flat)
    tail = jnp.flip(dext[:, flat:], axis=1)
    tail = tail.at[:, -1].add(jnp.sum(dext[:, :flat], axis=1))
    return jnp.pad(tail, ((0, 0), (lo, 0)))


def _tri_suffix(x, tri):
    hi = x.astype(bf16)
    lo = (x - hi.astype(f32)).astype(bf16)
    return jnp.dot(hi, tri, preferred_element_type=f32) + jnp.dot(lo, tri, preferred_element_type=f32)


def _sb_block(q, k, run, tri, causal):
    z = lax.dot_general(q, k, (NT, ((), ())), preferred_element_type=f32) * (HEAD_DIM ** -0.5)
    e = jnp.exp(-jnp.abs(z))
    l1p = jnp.log(1.0 + e)
    lb = jnp.minimum(z, 0.0) - l1p
    lmb = lb - z
    if causal is not None:
        lmb = jnp.where(causal, lmb, 0.0)
    cs = _tri_suffix(lmb, tri)
    w = jnp.exp(lb + (run + cs - lmb))
    if causal is not None:
        w = jnp.where(causal, w, 0.0)
    return z, e, w, run + cs[:, 0:1]


def _sb_tri():
    r = lax.broadcasted_iota(jnp.int32, (SB_BLK, SB_BLK), 0)
    c = lax.broadcasted_iota(jnp.int32, (SB_BLK, SB_BLK), 1)
    return (r >= c).astype(bf16), c < r


def _sb_live(runs):
    m = runs[0]
    for r in runs[1:]:
        m = jnp.maximum(m, r)
    return jnp.max(m) > SB_DEAD


def _sb_fwd(proj, gather):
    t = proj.shape[0]
    cb = A_W // LANES
    nh = LANES // HEAD_DIM

    step_rows = QSUB_B * SB_BLK

    def body(q_ref, k_ref, v_ref, o_ref, of_ref):
        tri, diag = _sb_tri()
        for sb in range(QSUB_B):
            _sb_fwd_block(pl.program_id(1) * QSUB_B + sb, pl.ds(sb * SB_BLK, SB_BLK), tri, diag,
                          q_ref, k_ref, v_ref, o_ref, of_ref)

    def _sb_fwd_block(qb, qrows, tri, diag, q_ref, k_ref, v_ref, o_ref, of_ref):
        qm = _mask_heads(q_ref[qrows, :])

        def pair(kb, carry, causal):
            rows = pl.ds(pl.multiple_of(kb * SB_BLK, SB_BLK), SB_BLK)
            k = k_ref[rows, :]
            vm = _mask_heads(v_ref[rows, :])
            runs, acc = [], carry[nh]
            for h in range(nh):
                _, _, w, run = _sb_block(qm[h], k, carry[h], tri, causal)
                acc = acc + jnp.dot(w.astype(bf16), vm[h], preferred_element_type=f32)
                runs.append(run)
            return (*runs, acc)

        zero = jnp.zeros((SB_BLK, 1), f32)
        carry = pair(qb, (zero,) * nh + (jnp.zeros((SB_BLK, LANES), f32),), diag)

        def cond(st):
            return (st[0] < qb) & _sb_live(st[1][:nh])

        def step(st):
            return st[0] + 1, pair(qb - 1 - st[0], st[1], None)

        _, carry = lax.while_loop(cond, step, (jnp.int32(0), carry))
        o_ref[qrows, :] = carry[nh].astype(bf16)
        of_ref[qrows, :] = carry[nh]

    ospec = pl.BlockSpec((step_rows, LANES), lambda hp, qb: (qb, hp))
    return _call(
        body, [proj, proj, proj], grid=(cb, t // step_rows), name="sb_attn_fwd",
        in_specs=[pl.BlockSpec((step_rows, LANES), lambda hp, qb: (qb, 3 * cb + hp)),
                  pl.BlockSpec((t, LANES), lambda hp, qb: (0, 4 * cb + hp)),
                  pl.BlockSpec((t, LANES), lambda hp, qb: (0, 5 * cb + hp))],
        out_specs=[ospec, ospec], out_shape=[SDS((t, A_W), bf16), SDS((t, A_W), f32)],
        sem=("parallel", "arbitrary"), gather=gather)


def _sb_bwd(proj, out_b, dout, gather):
    t = proj.shape[0]
    cb = A_W // LANES
    nh = LANES // HEAD_DIM

    step_rows = QSUB_B * SB_BLK

    def body(q_ref, k_ref, v_ref, o_ref, do_ref, dq_ref, dk_ref, dv_ref):
        tri, diag = _sb_tri()

        @pl.when(pl.program_id(1) == 0)
        def _():
            dk_ref[...] = jnp.zeros_like(dk_ref)
            dv_ref[...] = jnp.zeros_like(dv_ref)

        for sb in range(QSUB_B):
            _sb_bwd_block(pl.program_id(1) * QSUB_B + sb, pl.ds(sb * SB_BLK, SB_BLK), tri, diag,
                          q_ref, k_ref, v_ref, o_ref, do_ref, dq_ref, dk_ref, dv_ref)

    def _sb_bwd_block(qb, qrows, tri, diag, q_ref, k_ref, v_ref, o_ref, do_ref, dq_ref, dk_ref, dv_ref):
        qm = _mask_heads(q_ref[qrows, :])
        do = do_ref[qrows, :]
        dom = _mask_heads(do)
        dsums = [jnp.sum(t_, axis=-1, keepdims=True) for t_ in _mask_heads(do.astype(f32) * o_ref[qrows, :])]

        def pair(kb, carry, causal):
            rows = pl.ds(pl.multiple_of(kb * SB_BLK, SB_BLK), SB_BLK)
            k = k_ref[rows, :]
            v = v_ref[rows, :]
            km = _mask_heads(k)
            new, dq, dk, dv = [], carry[2 * nh], None, None
            for h in range(nh):
                z, e, w, run = _sb_block(qm[h], k, carry[2 * h], tri, causal)
                inv = 1.0 / (1.0 + e)
                beta = jnp.where(z >= 0.0, inv, e * inv)
                wb = w.astype(bf16)
                g = lax.dot_general(dom[h], v, (NT, ((), ())), preferred_element_type=f32) * wb.astype(f32)
                sg = _tri_suffix(g, tri)
                dz = g * (1.0 - beta) - (dsums[h] - carry[2 * h + 1] - sg) * beta
                if causal is not None:
                    dz = jnp.where(causal, dz, 0.0)
                dzb = (dz * (HEAD_DIM ** -0.5)).astype(bf16)
                dq = dq + jnp.dot(dzb, km[h], preferred_element_type=f32)
                tk = lax.dot_general(dzb, qm[h], (TN, ((), ())), preferred_element_type=f32)
                tv = lax.dot_general(wb, dom[h], (TN, ((), ())), preferred_element_type=f32)
                dk, dv = (tk, tv) if dk is None else (dk + tk, dv + tv)
                new += [run, carry[2 * h + 1] + sg[:, 0:1]]
            dk_ref[rows, :] += dk
            dv_ref[rows, :] += dv
            return (*new, dq)

        zero = jnp.zeros((SB_BLK, 1), f32)
        carry = pair(qb, (zero,) * (2 * nh) + (jnp.zeros((SB_BLK, LANES), f32),), diag)

        def cond(st):
            return (st[0] < qb) & _sb_live(st[1][0:2 * nh:2])

        def step(st):
            return st[0] + 1, pair(qb - 1 - st[0], st[1], None)

        _, carry = lax.while_loop(cond, step, (jnp.int32(0), carry))
        dq_ref[qrows, :] = carry[2 * nh].astype(bf16)

    kv_in = lambda seg: pl.BlockSpec((t, LANES), lambda hp, qb: (0, seg * cb + hp))
    q_spec = pl.BlockSpec((step_rows, LANES), lambda hp, qb: (qb, hp))
    kv_out = pl.BlockSpec((t, LANES), lambda hp, qb: (0, hp))
    return _call(
        body, [proj, proj, proj, out_b, dout], grid=(cb, t // step_rows), name="sb_attn_bwd",
        in_specs=[pl.BlockSpec((step_rows, LANES), lambda hp, qb: (qb, 3 * cb + hp)), kv_in(4), kv_in(5),
                  q_spec, pl.BlockSpec((step_rows, LANES), lambda hp, qb: (qb, cb + hp))],
        out_specs=[q_spec, kv_out, kv_out],
        out_shape=[SDS((t, A_W), bf16), SDS((t, A_W), f32), SDS((t, A_W), f32)],
        sem=("parallel", "arbitrary"), gather=gather)


def _halo_specs(tr, w, col, nblk):
    per = tr // SUBLANES
    cur = pl.BlockSpec((tr, w), lambda i: (i, col))
    prev = pl.BlockSpec((SUBLANES, w), lambda i: (jnp.maximum(i * per - 1, 0), col))
    nxt = pl.BlockSpec((SUBLANES, w), lambda i: (jnp.minimum((i + 1) * per, nblk * per - 1), col))
    return cur, prev, nxt


def _taps_before(cur, prev8, first):
    prev8 = jnp.where(first, 0.0, prev8)
    ext = jnp.concatenate([prev8, cur], axis=0)
    return [pltpu.roll(ext, s, 0)[SUBLANES:] for s in (3, 2, 1)]


def _taps_after(cur, next8, last):
    n = cur.shape[0]
    next8 = jnp.where(last, 0.0, next8)
    ext = jnp.concatenate([cur, next8], axis=0)
    return [pltpu.roll(ext, n + SUBLANES - s, 0)[:n] for s in (1, 2, 3)]


def _block_diag(x, w_ref, dims):
    outs = [lax.dot_general(x[:, n * LRU_BW:(n + 1) * LRU_BW], w_ref[n], (dims, ((), ())),
                            preferred_element_type=f32) for n in range(LRU_BLOCKS)]
    return jnp.concatenate(outs, axis=1)


def _lru_gates(xc, wa_ref, wi_ref, ba, bi, lam):
    xb = xc.astype(bf16)
    r = jax.nn.sigmoid(_block_diag(xb, wa_ref, NN) + ba)
    ig = jax.nn.sigmoid(_block_diag(xb, wi_ref, NN) + bi)
    sp = jnp.maximum(-lam, 0.0) + jnp.log(1.0 + jnp.exp(-jnp.abs(lam)))
    log_a = -LRU_C * r * sp
    a = jnp.exp(log_a)
    x2 = 2.0 * log_a
    one_minus = jnp.where(x2 > -1e-2, -x2 * (1.0 + x2 * (0.5 + x2 * (1.0 / 6.0))), 1.0 - a * a)
    mult = jnp.sqrt(one_minus)
    return xb, r, ig, sp, a, mult


def _rg_gates_fwd(proj, conv_w, conv_b, wa, wi, ba, bi, lam, tr=512):
    t = proj.shape[0]
    w = D_MODEL
    tr = min(tr, t)
    nblk = t // tr
    cur, prev, _ = _halo_specs(tr, w, 1, nblk)

    def body(x_ref, xp_ref, cw_ref, cb_ref, wa_ref, wi_ref, ba_ref, bi_ref, lam_ref, xc_ref, a_ref, u_ref):
        x = x_ref[...]
        taps = _taps_before(x, xp_ref[...], pl.program_id(0) == 0) + [x]
        xc = cb_ref[...]
        for k in range(4):
            xc = xc + cw_ref[k:k + 1, :] * taps[k]
        _, _, ig, _, a, mult = _lru_gates(xc, wa_ref, wi_ref, ba_ref[...], bi_ref[...], lam_ref[...])
        xc_ref[...] = xc
        a_ref[...] = a
        u_ref[...] = mult * (ig * xc)

    full = lambda a_: pl.BlockSpec(a_.shape, lambda i, nd=a_.ndim: (0,) * nd)
    ospec = pl.BlockSpec((tr, w), lambda i: (i, 0))
    return pl.pallas_call(
        body, grid=(nblk,), name="rg_gates_fwd",
        in_specs=[cur, prev] + [full(a_) for a_ in (conv_w, conv_b, wa, wi, ba, bi, lam)],
        out_specs=[ospec] * 3, out_shape=[SDS((t, w), f32)] * 3,
        compiler_params=_cparams(("parallel",)))(proj, proj, conv_w, conv_b, wa, wi, ba, bi, lam)


def _lru_scan(name, a, b, reverse, tt=512):
    t, w = a.shape
    tt = min(tt, t)
    nt = t // tt
    ng = tt // SUBLANES

    def body(a_ref, b_ref, h_ref, carry_ref):
        @pl.when(pl.program_id(0) == 0)
        def _():
            carry_ref[...] = jnp.zeros_like(carry_ref)

        row = lax.broadcasted_iota(jnp.int32, (SUBLANES, w), 0)

        def group(gi, carry):
            g = (ng - 1 - gi) if reverse else gi
            rows = pl.ds(pl.multiple_of(g * SUBLANES, SUBLANES), SUBLANES)
            av = a_ref[rows, :]
            bv = b_ref[rows, :]
            for s in (1, 2, 4):
                sh = (SUBLANES - s) if reverse else s
                ok = (row < SUBLANES - s) if reverse else (row >= s)
                a_s = pltpu.roll(av, sh, 0)
                b_s = pltpu.roll(bv, sh, 0)
                bv = jnp.where(ok, av * b_s + bv, bv)
                av = jnp.where(ok, av * a_s, av)
            h = av * carry + bv
            h_ref[rows, :] = h
            edge = h[0:1, :] if reverse else h[SUBLANES - 1:SUBLANES, :]
            return jnp.broadcast_to(edge, (SUBLANES, w))

        carry_ref[...] = lax.fori_loop(0, ng, group, carry_ref[...], unroll=4)

    tmap = (lambda i: (nt - 1 - i, 0)) if reverse else (lambda i: (i, 0))
    spec = pl.BlockSpec((tt, w), tmap)
    return pl.pallas_call(
        body, grid=(nt,), name=name, in_specs=[spec, spec], out_specs=spec,
        out_shape=SDS((t, w), f32), scratch_shapes=[pltpu.VMEM((SUBLANES, w), f32)],
        compiler_params=_cparams(("arbitrary",)))(a, b)


def _rg_gates_bwd(dhs, c, hs, xc, wa, wi, ba, bi, lam, tr=512):
    t, w = xc.shape
    tr = min(tr, t)
    nblk = t // tr
    cur, prev, nxt = _halo_specs(tr, w, 0, nblk)

    def body(dhs_ref, c_ref, cn_ref, hs_ref, hp_ref, xc_ref, wa_ref, wi_ref, ba_ref, bi_ref, lam_ref,
             dxc_ref, dwa_ref, dwi_ref, dba_ref, dbi_ref, dlam_ref):
        i = pl.program_id(0)
        c_next = _taps_after(c_ref[...], cn_ref[...], i == nblk - 1)[0]
        h_prev = _taps_before(hs_ref[...], hp_ref[...], i == 0)[2]
        xc = xc_ref[...]
        lam = lam_ref[...]
        xb, r, ig, sp, a, mult = _lru_gates(xc, wa_ref, wi_ref, ba_ref[...], bi_ref[...], lam)
        dh = dhs_ref[...] + c_next
        dlog_a = dh * h_prev * a - (dh * ig * xc) * (a * a / mult)
        dpre_a = (dlog_a * (-LRU_C * sp) * r * (1.0 - r)).astype(bf16)
        dpre_i = (dh * mult * xc * ig * (1.0 - ig)).astype(bf16)
        dxc_ref[...] = (dh * mult * ig + _block_diag(dpre_a, wa_ref, NT) + _block_diag(dpre_i, wi_ref, NT))
        dsig = 1.0 / (1.0 + jnp.exp(lam))
        sums = [jnp.sum(dpre_a.astype(f32), axis=0, keepdims=True),
                jnp.sum(dpre_i.astype(f32), axis=0, keepdims=True),
                jnp.sum(dlog_a * (-LRU_C * r), axis=0, keepdims=True) * (-dsig)]

        @pl.when(i == 0)
        def _():
            dwa_ref[...] = jnp.zeros_like(dwa_ref)
            dwi_ref[...] = jnp.zeros_like(dwi_ref)
            dba_ref[...] = jnp.zeros_like(dba_ref)
            dbi_ref[...] = jnp.zeros_like(dbi_ref)
            dlam_ref[...] = jnp.zeros_like(dlam_ref)

        for n in range(LRU_BLOCKS):
            sl = slice(n * LRU_BW, (n + 1) * LRU_BW)
            dwa_ref[n] += lax.dot_general(xb[:, sl], dpre_a[:, sl], (TN, ((), ())), preferred_element_type=f32)
            dwi_ref[n] += lax.dot_general(xb[:, sl], dpre_i[:, sl], (TN, ((), ())), preferred_element_type=f32)
        dba_ref[...] += sums[0]
        dbi_ref[...] += sums[1]
        dlam_ref[...] += sums[2]

    full = lambda a_: pl.BlockSpec(a_.shape, lambda i, nd=a_.ndim: (0,) * nd)
    vec = pl.BlockSpec((1, w), lambda i: (0, 0))
    mat = pl.BlockSpec((LRU_BLOCKS, LRU_BW, LRU_BW), lambda i: (0, 0, 0))
    return pl.pallas_call(
        body, grid=(nblk,), name="rg_gates_bwd",
        in_specs=[cur, cur, nxt, cur, prev, cur] + [full(a_) for a_ in (wa, wi, ba, bi, lam)],
        out_specs=[cur, mat, mat, vec, vec, vec],
        out_shape=[SDS((t, w), f32), SDS((LRU_BLOCKS, LRU_BW, LRU_BW), f32), SDS((LRU_BLOCKS, LRU_BW, LRU_BW), f32),
                   SDS((1, w), f32), SDS((1, w), f32), SDS((1, w), f32)],
        compiler_params=_cparams(("arbitrary",)))(dhs, c, c, hs, hs, xc, wa, wi, ba, bi, lam)


def _rg_conv_bwd(dxc, proj, conv_w, tr=512):
    t, w = dxc.shape
    tr = min(tr, t)
    nblk = t // tr
    cur, _, nxt = _halo_specs(tr, w, 0, nblk)
    xcur, xprev, _ = _halo_specs(tr, w, 1, nblk)

    def body(d_ref, dn_ref, x_ref, xp_ref, cw_ref, dx_ref, dcw_ref, dcb_ref):
        i = pl.program_id(0)
        d = d_ref[...]
        x = x_ref[...]
        after = _taps_after(d, dn_ref[...], i == nblk - 1)
        before = _taps_before(x, xp_ref[...], i == 0) + [x]
        dx = cw_ref[3:4, :] * d
        for s in (1, 2, 3):
            dx = dx + cw_ref[3 - s:4 - s, :] * after[s - 1]
        dx_ref[...] = dx.astype(bf16)
        dcw = jnp.concatenate([jnp.sum(d * before[k], axis=0, keepdims=True) for k in range(4)], axis=0)
        dcb = jnp.sum(d, axis=0, keepdims=True)

        @pl.when(i == 0)
        def _():
            dcw_ref[...] = dcw
            dcb_ref[...] = dcb

        @pl.when(i > 0)
        def _():
            dcw_ref[...] += dcw
            dcb_ref[...] += dcb

    return pl.pallas_call(
        body, grid=(nblk,), name="rg_conv_bwd",
        in_specs=[cur, nxt, xcur, xprev, pl.BlockSpec((4, w), lambda i: (0, 0))],
        out_specs=[cur, pl.BlockSpec((4, w), lambda i: (0, 0)), pl.BlockSpec((1, w), lambda i: (0, 0))],
        out_shape=[SDS((t, w), bf16), SDS((4, w), f32), SDS((1, w), f32)],
        compiler_params=_cparams(("arbitrary",)))(dxc, dxc, proj, proj, conv_w)


def _attn_fwd(h, wts, j, plan):
    proj = _mm_cols("attn_in", h, wts["attn_w_in"], j, bf16)
    kpad = jnp.pad(proj[:, A_W:2 * A_W], ((PAD_A, 0), (0, 0)))
    vpad = jnp.pad(proj[:, 2 * A_W:3 * A_W], ((PAD_A, 0), (0, 0)))
    bias = _bias_window(wts["attn_rel_bias"][j])
    plan = plan if j == 0 else None
    out_a = _carried(plan, "chunk_attn_fwd", wts, _chunk_attn_fwd, proj, kpad, vpad, bias)
    out_b, out_b32 = _carried(plan, "sb_attn_fwd", wts, _sb_fwd, proj)
    m = _mm_rows("attn_out", [out_a, out_b], wts["attn_w_out"], j, f32)
    return m, (proj, kpad, vpad, bias, out_a, out_b, out_b32)


def _attn_bwd(dm, h, saved, wts, j, grads, exch):
    proj, kpad, vpad, bias, out_a, out_b, out_b32 = saved
    dout = _mm_rows_t("attn_out_t", dm, wts["attn_w_out"], j, bf16)
    gi, ll = _grad_slot("attn_w_out", j)
    grads["attn_w_out"][gi] = _mm_wgrad("attn_out_wgrad_a", out_a, dm, grads["attn_w_out"][gi], ll, 0)
    grads["attn_w_out"][gi] = _mm_wgrad("attn_out_wgrad_b", out_b, dm, grads["attn_w_out"][gi], ll, 1)
    if exch is not None and j == 0:
        (dqa, dka, dva, dbias), got = _chunk_attn_bwd(proj, kpad, vpad, bias, dout, exch.upper_carry(grads))
        exch.upper_got(got)
        (dqs, dks, dvs), slots = _sb_bwd(proj, out_b32, dout, exch.carry())
        exch.carried(slots)
    else:
        dqa, dka, dva, dbias = _chunk_attn_bwd(proj, kpad, vpad, bias, dout, None)[0]
        dqs, dks, dvs = _sb_bwd(proj, out_b32, dout, None)[0]
    grads["attn_rel_bias"][j] = _bias_window_grad(dbias)
    dproj = jnp.concatenate([dqa, dka[PAD_A:].astype(bf16), dva[PAD_A:].astype(bf16),
                             dqs, dks.astype(bf16), dvs.astype(bf16)], axis=1)
    grads["attn_w_in"][gi] = _mm_wgrad_cols("attn_in_wgrad", h, dproj, grads["attn_w_in"][gi], ll)
    return _mm_cols_t("attn_in_t", dproj, wts["attn_w_in"], j, f32)


def _rg_fwd(h, wts, j, plan):
    proj =_mm_cols("rg_in", h, wts["rg_w_in"], j, f32)
    small = [wts[k][j] for k in ("rg_conv_w", "rg_conv_b", "rg_w_a", "rg_w_i", "rg_b_a", "rg_b_i", "rg_lambda")]
    xc, a, u = _rg_gates_fwd(proj, *small)
    hs = _lru_scan("lru_scan_fwd", a, u, False)
    yp = _rows("rg_gate_out", lambda hv, gv: hv * _gelu(gv), [hs, (proj, D_MODEL, 0)], [], [(D_MODEL, bf16)])[0]
    m = _mm_rows("rg_out", [yp], wts["rg_w_out"], j, f32)
    return m, (proj, xc, a, hs, yp)


def _rg_bwd(dm, h, saved, wts, j, grads, exch):
    proj, xc, a, hs, yp = saved
    dyp = _mm_rows_t("rg_out_t", dm, wts["rg_w_out"], j, f32)
    gi, ll = _grad_slot("rg_w_out", j)
    grads["rg_w_out"][gi] = _mm_wgrad("rg_out_wgrad", yp, dm, grads["rg_w_out"][gi], ll)

    def gate_bwd(dy, hv, gv, av):
        dhs = dy * _gelu(gv)
        return dhs, av * dhs, dy * hv * _gelu_grad(gv)

    dhs, ab, dgate = _rows("rg_gate_out_bwd", gate_bwd, [dyp, hs, (proj, D_MODEL, 0), a], [],
                           [(D_MODEL, f32), (D_MODEL, f32), (D_MODEL, bf16)])
    c = _lru_scan("lru_scan_bwd", a, ab, True)
    wa, wi, ba, bi, lam = [wts[k][j] for k in ("rg_w_a", "rg_w_i", "rg_b_a", "rg_b_i", "rg_lambda")]
    dxc, dwa, dwi, dba, dbi, dlam = _rg_gates_bwd(dhs, c, hs, xc, wa, wi, ba, bi, lam)
    dxr, dcw, dcb = _rg_conv_bwd(dxc, proj, wts["rg_conv_w"][j])
    for k, v in (("rg_w_a", dwa), ("rg_w_i", dwi), ("rg_b_a", dba), ("rg_b_i", dbi), ("rg_lambda", dlam),
                 ("rg_conv_w", dcw), ("rg_conv_b", dcb)):
        grads[k][j] = v
    dproj = jnp.concatenate([dgate, dxr], axis=1)
    grads["rg_w_in"][gi] = _mm_wgrad_cols("rg_in_wgrad", h, dproj, grads["rg_w_in"][gi], ll)
    return _mm_cols_t("rg_in_t", dproj, wts["rg_w_in"], j, f32)


def _local_step(x, target, wts, plan=None, exch=None):
    t = x.shape[0]
    d = D_MODEL
    gains = {k: wts[k] for k in ("norm_mix_pre", "norm_mix_post", "norm_ffn_pre", "norm_ffn_post")}
    gain = lambda k, l: gains[k][l:l + 1]

    saved = []
    h = _rows("norm_in", _norm_fwd, [x], [gain("norm_mix_pre", 0)], [(d, bf16)])[0]
    loss_cols = None
    for l in range(DEPTH):
        j = l // 2
        m, mix_saved = (_attn_fwd if l % 2 == 0 else _rg_fwd)(h, wts, j, plan)

        def resid_next(xv, mv, g_post, g_next):
            x1 = xv + _norm_fwd(mv, g_post)
            return x1, _norm_fwd(x1, g_next)

        x1, h2 = _rows("resid_mix", resid_next, [x, m], [gain("norm_mix_post", l), gain("norm_ffn_pre", l)],
                       [(d, f32), (d, bf16)])
        g, u, hid = _carried(plan if l == 0 else None, "ffn_up", wts, _ffn_up, h2, wts["ffn_w_gate"],
                             wts["ffn_w_up"], l)
        f = _ffn_down(hid, wts["ffn_w_down"], l)
        saved.append((x, h, m, mix_saved, x1, h2, g, u, hid, f))
        if l + 1 < DEPTH:
            x, h = _rows("resid_ffn", resid_next, [x1, f], [gain("norm_ffn_post", l), gain("norm_mix_pre", l + 1)],
                         [(d, f32), (d, bf16)])
        else:
            def resid_loss(xv, fv, tv, g_post):
                err = xv + _norm_fwd(fv, g_post) - tv
                return err * (1.0 / d), jnp.sum(err * err, axis=0, keepdims=True)

            dx, loss_cols = _rows("resid_loss", resid_loss, [x1, f, target], [gain("norm_ffn_post", l)],
                                  [(d, f32)], [((1, d), f32)])
    loss = 0.5 * jnp.sum(loss_cols) / d

    grads = {k: {} for k in SMALL_GRADS}
    for k in BIG_GRADS:
        shp = wts[k].shape
        rest = shp[2:] if shp[1] == 1 else shp[1:]
        grads[k] = [_Fresh((LOWER_LAYERS[k],) + rest), _Fresh((shp[0] - LOWER_LAYERS[k],) + rest)]

    def norm_bwd_cast(uv, dyv, gv):
        du, dg = _norm_bwd(uv, dyv, gv)
        return du, dg

    def norm_bwd_resid(uv, dhv, dxv, gv):
        du, dg = _norm_bwd(uv, dhv, gv)
        return dxv + du, dg

    def norm_bwd_pair(uv, dhv, dxv, nv, g_pre, g_post):
        dx_, dg_pre = norm_bwd_resid(uv, dhv, dxv, g_pre)
        dn, dg_post = _norm_bwd(nv, dx_, g_post)
        return dx_, dn, dg_pre, dg_post

    df = None
    for l in reversed(range(DEPTH)):
        j = l // 2
        x_in, h, m, mix_saved, x1, h2, g, u, hid, f = saved[l]
        if df is None:
            df, grads["norm_ffn_post"][l] = _rows("norm_ffn_post_bwd", norm_bwd_cast, [f, dx],
                                                  [gain("norm_ffn_post", l)], [(d, bf16)], [((1, d), f32)])
        dg, du = _ffn_down_bwd(df, wts["ffn_w_down"], l, g, u, None)[0]
        gi, ll = _grad_slot("ffn_w_down", l)
        grads["ffn_w_down"][gi] = _ffn_wgrad_down(hid, df, grads["ffn_w_down"][gi], ll)
        dh2 = _ffn_up_bwd(dg, du, wts["ffn_w_gate"], wts["ffn_w_up"], l)
        grads["ffn_w_gate"][gi], grads["ffn_w_up"][gi] = _ffn_wgrad_up(
            h2, dg, du, grads["ffn_w_gate"][gi], grads["ffn_w_up"][gi], ll)
        dx1, dm, grads["norm_ffn_pre"][l], grads["norm_mix_post"][l] = _rows(
            "norm_ffn_mix_bwd", norm_bwd_pair, [x1, dh2, dx, m], [gain("norm_ffn_pre", l), gain("norm_mix_post", l)],
            [(d, f32), (d, bf16)], [((1, d), f32), ((1, d), f32)])
        dh = (_attn_bwd if l % 2 == 0 else _rg_bwd)(dm, h, mix_saved, wts, j, grads, exch)
        if l > 0:
            dx, df, grads["norm_mix_pre"][l], grads["norm_ffn_post"][l - 1] = _rows(
                "norm_mix_ffn_bwd", norm_bwd_pair, [x_in, dh, dx1, saved[l - 1][9]],
                [gain("norm_mix_pre", l), gain("norm_ffn_post", l - 1)],
                [(d, f32), (d, bf16)], [((1, d), f32), ((1, d), f32)])
        else:
            dx, grads["norm_mix_pre"][l] = _rows("norm_mix_pre_bwd", norm_bwd_resid, [x_in, dh, dx1],
                                                 [gain("norm_mix_pre", l)], [(d, f32)], [((1, d), f32)])
    return loss, dx, grads


ANY = pl.BlockSpec(memory_space=pl.ANY)
PACK_COLS = 1024
SMALL_ROWS = 288


def _mesh_pos():
    x, y, c = lax.axis_index("x"), lax.axis_index("y"), lax.axis_index("c")
    return x, y, c, [(1 - x, y), (x, 1 - y), (1 - x, 1 - y)]


def _run_copies(copies):
    for cp in copies:
        cp.start()
    for cp in copies:
        cp.wait()


GATHER_SEMS = 7


def _gather_copies(items, ins, outs, send, recv):
    x, y, c, chips = _mesh_pos()
    q = 2 * x + y
    sibling = (x, y, 1 - c)

    def copy(k, src, dst, to):
        return pltpu.make_async_remote_copy(src_ref=src, dst_ref=dst, send_sem=send.at[k], recv_sem=recv.at[k],
                                            device_id=to, device_id_type=MESH)

    own, sent, passed = [], [], []
    for i, (t, l0, nl) in enumerate(items):
        lay = pl.ds(l0, nl)
        half = ins[t].shape[1] // 2
        rows = pl.ds(pl.multiple_of(c * half, half), half)
        own.append(copy(GATHER_SEMS * i, ins[t].at[lay], outs[t].at[lay, q], sibling))
        for j, (px, py) in enumerate(chips):
            sent.append(copy(GATHER_SEMS * i + 1 + j, ins[t].at[lay, rows], outs[t].at[lay, q, rows], (px, py, c)))
            landed = outs[t].at[lay, 2 * px + py, rows]
            passed.append(copy(GATHER_SEMS * i + 4 + j, landed, landed, sibling))
    return own, sent, passed


def _gather_start(items, ins, outs, send, recv):
    own, sent, _ = _gather_copies(items, ins, outs, send, recv)
    for cp in own + sent:
        cp.start()


def _gather_finish(items, ins, outs, send, recv):
    own, sent, passed = _gather_copies(items, ins, outs, send, recv)
    for arrived, forward in zip(sent, passed):
        arrived.wait_recv()
        forward.start()
    for cp in sent:
        cp.wait_send()
    for cp in own + passed:
        cp.wait()


def _gather_call(items, shards):
    n = len(shards)
    nsem = GATHER_SEMS * len(items)

    def body(*refs):
        ins, outs = refs[:n], refs[n:2 * n]
        _gather_start(items, ins, outs, *refs[2 * n:])
        _gather_finish(items, ins, outs, *refs[2 * n:])

    return pl.pallas_call(
        body, name="weight_all_gather", in_specs=[ANY] * n, out_specs=[ANY] * n,
        out_shape=[SDS((s.shape[0], N_CHIPS) + s.shape[1:], s.dtype) for s in shards],
        scratch_shapes=[pltpu.SemaphoreType.DMA((nsem,)), pltpu.SemaphoreType.DMA((nsem,))])(*shards)


def _call(body, operands, *, name, grid, in_specs, out_specs, out_shape, sem, scratch=(), gather=None):
    if gather is None:
        return pl.pallas_call(body, grid=grid, in_specs=in_specs, out_specs=out_specs, out_shape=out_shape,
                              scratch_shapes=list(scratch), name=name, compiler_params=_cparams(sem))(*operands), None
    start, finish, c_ins, c_io, c_new, nsem = gather
    n_in, n_out, n_scr = len(operands), len(out_shape), len(scratch)
    ni, nio, nco = len(c_ins), len(c_io), len(c_io) + len(c_new)

    def full(*refs):
        ins, sh = refs[:n_in], refs[n_in:n_in + ni]
        outs = refs[n_in + ni + nio:n_in + ni + nio + n_out]
        co = refs[n_in + ni + nio + n_out:n_in + ni + nio + n_out + nco]
        scr = refs[n_in + ni + nio + n_out + nco:]
        ids = [pl.program_id(a) for a in range(len(grid))]
        first = functools.reduce(jnp.logical_and, [i == 0 for i in ids])
        last = functools.reduce(jnp.logical_and, [i == g - 1 for i, g in zip(ids, grid)])

        @pl.when(first)
        def _():
            start(sh, co, scr[n_scr], scr[n_scr + 1])

        body(*ins, *outs, *scr[:n_scr])

        @pl.when(last)
        def _():
            finish(sh, co, scr[n_scr], scr[n_scr + 1])

    res = pl.pallas_call(
        full, grid=grid, in_specs=list(in_specs) + [ANY] * (ni + nio), out_specs=list(out_specs) + [ANY] * nco,
        out_shape=list(out_shape) + [SDS(g.shape, g.dtype) for g in list(c_io) + list(c_new)],
        scratch_shapes=list(scratch) + [pltpu.SemaphoreType.DMA((nsem,)), pltpu.SemaphoreType.DMA((nsem,))],
        input_output_aliases={n_in + ni + t: n_out + t for t in range(nio)}, name=name,
        compiler_params=_cparams(("arbitrary",) * len(grid)))(*operands, *c_ins, *c_io)
    return res[:n_out], res[n_out:]


def _pair_exchange(gs):
    n = len(gs)

    def body(*refs):
        _pair_copies(refs[:n], refs[n:2 * n], *refs[2 * n:], start=True)
        _pair_copies(refs[:n], refs[n:2 * n], *refs[2 * n:], start=False)

    return pl.pallas_call(
        body, name="grad_pair_exchange", in_specs=[ANY] * n, out_specs=[ANY] * n,
        out_shape=_pair_shapes(gs),
        scratch_shapes=[pltpu.SemaphoreType.DMA((n,)), pltpu.SemaphoreType.DMA((n,))])(*gs)


def _pair_shapes(gs):
    return [SDS(g.shape[:2] + (g.shape[2] // 2, g.shape[3]), f32) for g in gs]


def _pair_copies(ins, outs, send, recv, start):
    x, y, c, _ = _mesh_pos()
    for t in range(len(ins)):
        half = ins[t].shape[2] // 2
        src = ins[t].at[:, :, pl.ds(pl.multiple_of((1 - c) * half, SUBLANES), half)]
        cp = pltpu.make_async_remote_copy(src_ref=src, dst_ref=outs[t], send_sem=send.at[t], recv_sem=recv.at[t],
                                          device_id=(x, y, 1 - c), device_id_type=MESH)
        cp.start() if start else cp.wait()


def _pair_carry(gs):
    return (functools.partial(_pair_copies, start=True), functools.partial(_pair_copies, start=False),
            gs, [], _pair_shapes(gs), len(gs))


def _pair_sum(name, g, got, c):
    l, s, r, cols = g.shape

    def body(c_ref, a_ref, b_ref, o_ref):
        o_ref[...] = (a_ref[...] + b_ref[...]).astype(bf16)

    blk = (None, None, r // 2, cols)
    return pl.pallas_call(
        body, name=name, out_shape=SDS(got.shape, bf16),
        grid_spec=pltpu.PrefetchScalarGridSpec(
            num_scalar_prefetch=1, grid=(l, s),
            in_specs=[pl.BlockSpec(blk, lambda i, q, c_ref: (i, q, c_ref[0], 0)),
                      pl.BlockSpec(blk, lambda i, q, c_ref: (i, q, 0, 0))],
            out_specs=pl.BlockSpec(blk, lambda i, q, c_ref: (i, q, 0, 0))),
        compiler_params=_cparams(("parallel", "parallel")))(c, g, got)


def _chip_exchange(hs):
    n = len(hs)

    def body(*refs):
        _chip_copies(refs[:n], refs[n:2 * n], *refs[2 * n:], start=True)
        _chip_copies(refs[:n], refs[n:2 * n], *refs[2 * n:], start=False)

    return pl.pallas_call(
        body, name="grad_chip_exchange", in_specs=[ANY] * n, out_specs=[ANY] * n,
        out_shape=[SDS(h.shape, h.dtype) for h in hs],
        scratch_shapes=[pltpu.SemaphoreType.DMA((3 * n,)), pltpu.SemaphoreType.DMA((3 * n,))])(*hs)


def _chip_copies(ins, outs, send, recv, start):
    x, y, c, chips = _mesh_pos()
    q = 2 * x + y
    for t in range(len(ins)):
        for j, (px, py) in enumerate(chips):
            cp = pltpu.make_async_remote_copy(
                src_ref=ins[t].at[:, 2 * px + py], dst_ref=outs[t].at[:, q], send_sem=send.at[3 * t + j],
                recv_sem=recv.at[3 * t + j], device_id=(px, py, c), device_id_type=MESH)
            cp.start() if start else cp.wait()


def _chip_carry(hs):
    return (functools.partial(_chip_copies, start=True), functools.partial(_chip_copies, start=False),
            hs, [], [SDS(h.shape, h.dtype) for h in hs], 3 * len(hs))


def _chip_sum(name, s, h, pos, l0, layers, into):
    l, _, r, cols = s.shape

    def body(pos_ref, s0, s1, s2, s3, own_ref, *rest):
        vals = [jnp.where(pos_ref[0] == p, own_ref[...], ref[...]).astype(f32) for p, ref in enumerate((s0, s1, s2, s3))]
        rest[-1][...] = ((vals[0] + vals[1]) + vals[2]) + vals[3]

    blk = (None, None, r, cols)
    slot = lambda p: pl.BlockSpec(blk, lambda i, pos_ref: (i, jnp.where(pos_ref[0] == p, (p + 1) % N_CHIPS, p), 0, 0))
    extra, alias = ([], {}) if into is None else ([into], {6: 0})
    return pl.pallas_call(
        body, name=name, out_shape=SDS((layers, 2 * r, cols), f32), input_output_aliases=alias,
        grid_spec=pltpu.PrefetchScalarGridSpec(
            num_scalar_prefetch=1, grid=(l,),
            in_specs=[slot(p) for p in range(N_CHIPS)] + [pl.BlockSpec(blk, lambda i, pos_ref: (i, pos_ref[0], 0, 0))]
            + [ANY] * len(extra),
            out_specs=pl.BlockSpec((None, r, cols), lambda i, pos_ref: (l0 + i, pos_ref[1], 0))),
        compiler_params=_cparams(("parallel",)))(pos, s, s, s, s, h, *extra)


def _pair_gather(fulls):
    n = len(fulls)

    def body(*refs):
        ins, outs = refs[:n], refs[n:2 * n]
        send, recv = refs[2 * n:]
        x, y, c, _ = _mesh_pos()
        copies = []
        for t in range(n):
            half = outs[t].shape[1] // 2
            rows = outs[t].at[:, pl.ds(pl.multiple_of(c * half, SUBLANES), half)]
            copies.append(pltpu.make_async_remote_copy(
                src_ref=rows, dst_ref=rows, send_sem=send.at[t], recv_sem=recv.at[t],
                device_id=(x, y, 1 - c), device_id_type=MESH))
        _run_copies(copies)

    return pl.pallas_call(
        body, name="grad_pair_gather", in_specs=[ANY] * n, out_specs=[ANY] * n,
        out_shape=[SDS(f.shape, f32) for f in fulls], input_output_aliases={t: t for t in range(n)},
        scratch_shapes=[pltpu.SemaphoreType.DMA((n,)), pltpu.SemaphoreType.DMA((n,))])(*fulls)


COL_SHARDED = ("attn_w_in", "rg_w_in", "ffn_w_gate", "ffn_w_up")
ROW_SHARDED = ("attn_w_out", "rg_w_out")
GATES = ("rg_w_a", "rg_w_i")
VECTORS = ("rg_conv_w", "rg_conv_b", "rg_b_a", "rg_b_i", "rg_lambda")
REPLICATED = ("norm_mix_pre", "norm_mix_post", "norm_ffn_pre", "norm_ffn_post", "attn_rel_bias")
BIG_GRADS = COL_SHARDED + ROW_SHARDED + ("ffn_w_down",)
SMALL_GRADS = GATES + VECTORS + REPLICATED
WEIGHTS =("attn_w_in", "attn_rel_bias", "attn_w_out", "rg_w_in", "rg_conv_w", "rg_conv_b", "rg_w_a", "rg_b_a",
           "rg_w_i", "rg_b_i", "rg_lambda", "rg_w_out", "norm_mix_pre", "norm_mix_post", "norm_ffn_pre",
           "norm_ffn_post", "ffn_w_gate", "ffn_w_up", "ffn_w_down")
SMALL = VECTORS + REPLICATED


GATHER_PARTS = {
    "first": (("attn_w_in", 0, 1), ("attn_w_out", 0, 1), ("rg_w_a", 0, 8), ("rg_w_i", 0, 8), ("vec", 0, 1)),
    "chunk_attn_fwd": (("ffn_w_gate", 0, 1), ("ffn_w_up", 0, 1), ("ffn_w_down", 0, 1), ("rg_w_in", 0, 1),
                       ("rg_w_out", 0, 1)),
    "sb_attn_fwd": (("ffn_w_gate", 1, 3), ("ffn_w_up", 1, 3), ("ffn_w_down", 1, 3)),
    "ffn_up": (("rg_w_in", 1, 1), ("rg_w_out", 1, 1), ("attn_w_in", 1, 1), ("attn_w_out", 1, 1)),
}


TRANSPOSED = ("ffn_w_gate", "ffn_w_up")


def _natural(name, a):
    return jnp.swapaxes(a, 1, 2) if name in TRANSPOSED else a


class _WeightGather:
    def __init__(self, w):
        self.w = w
        self.names = list(COL_SHARDED + ROW_SHARDED + GATES + ("ffn_w_down", "vec"))
        self.shards = {}
        for k in self.names[:-1]:
            a = _natural(k, w[k]).astype(bf16)
            self.shards[k] = a.reshape((-1,) + a.shape[-2:])
        self.shards["vec"] = jnp.concatenate([w[k].reshape(-1) for k in VECTORS]).reshape(1, -1, LANES)
        got = _gather_call(self._items("first", self.names), [self.shards[k] for k in self.names])
        self.raw = dict(zip(self.names, got))

    @staticmethod
    def _items(part, names):
        return [(names.index(k), l0, nl) for k, l0, nl in GATHER_PARTS[part]]

    def part(self, part):
        names = list(dict.fromkeys(k for k, _, _ in GATHER_PARTS[part]))
        items = self._items(part, names)
        return (functools.partial(_gather_start, items), functools.partial(_gather_finish, items),
                [self.shards[k] for k in names], [self.raw[k] for k in names], [], GATHER_SEMS * len(items)), names

    def views(self):
        got, w = self.raw, self.w
        out = {k: w[k] for k in REPLICATED}
        for k in COL_SHARDED + ("ffn_w_down",):
            out[k] = got[k]
        for k in ROW_SHARDED:
            l, s, ks, n = got[k].shape
            out[k] = got[k].reshape(l, 1, s * ks, n)
        for k in GATES:
            out[k] = got[k].reshape(2, LRU_BLOCKS, LRU_BW, LRU_BW)
        vec = got["vec"].reshape(N_CHIPS, -1)
        off = 0
        for k in VECTORS:
            shp = w[k].shape
            n = int(np.prod(shp))
            piece = vec[:, off:off + n].reshape((N_CHIPS,) + shp)
            off += n
            if k == "rg_conv_w":
                out[k] = piece.reshape(N_CHIPS, 2, 4, 256).transpose(1, 2, 0, 3).reshape(2, 4, D_MODEL)
            elif k in ("rg_b_a", "rg_b_i"):
                out[k] = piece.transpose(1, 2, 0, 3).reshape(2, 1, D_MODEL)
            else:
                out[k] = piece.transpose(1, 0, 2).reshape(2, 1, D_MODEL)
        return out


def _carried(plan, part, wts, fn, *args):
    if plan is None:
        return fn(*args, None)[0]
    gather, names = plan.part(part)
    out, new = fn(*args, gather)
    plan.raw.update(zip(names, new))
    wts.update(plan.views())
    return out


def _grad_blocks(name, g):
    st = jnp.stack([g[i] for i in sorted(g)])
    if name in GATES:
        st = st.reshape(2, LRU_BLOCKS, N_CHIPS, LRU_BW // N_CHIPS, LRU_BW).transpose(2, 0, 1, 3, 4)
    elif name == "rg_conv_w":
        st = st.reshape(2, 4, N_CHIPS, -1).transpose(2, 0, 1, 3)
    elif name in ("rg_b_a", "rg_b_i"):
        st = st.reshape(2, LRU_BLOCKS, N_CHIPS, -1).transpose(2, 0, 1, 3)
    elif name in VECTORS:
        st = st.reshape(2, N_CHIPS, -1).transpose(1, 0, 2)
    else:
        st = jnp.broadcast_to(st.reshape(1, -1), (N_CHIPS, st.size))
    return st.reshape(N_CHIPS, -1)


class _GradExchange:
    def __init__(self):
        self.c = lax.axis_index("c").astype(jnp.int32).reshape(1)
        self.pos = jnp.stack([2 * lax.axis_index("x") + lax.axis_index("y"), lax.axis_index("c")]).astype(jnp.int32)
        self.up = self.got_up = self.parts_up = self.slots_up = None

    @staticmethod
    def _blocked(g):
        if g.ndim == 3:
            g = g.reshape(g.shape[0], N_CHIPS, g.shape[1] // N_CHIPS, g.shape[2])
        return g

    def _sums(self, tag, names, gs, got):
        return [_pair_sum("grad_pair_sum_" + tag + k, g, r, self.c) for k, g, r in zip(names, gs, got)]

    def upper_carry(self, grads):
        self.up = [self._blocked(grads[k][1]) for k in BIG_GRADS]
        return _pair_carry(self.up)

    def upper_got(self, got):
        self.got_up = got

    def carry(self):
        self.parts_up = self._sums("up_", BIG_GRADS, self.up, self.got_up)
        return _chip_carry(self.parts_up)

    def carried(self, slots):
        self.slots_up = slots

    def finish(self, grads, shard_shapes):
        if self.got_up is None:
            self.upper_carry(grads)
            self.got_up = _pair_exchange(self.up)
        if self.slots_up is None:
            self.carry()
            self.slots_up = _chip_exchange(self.parts_up)
        blocks = [_grad_blocks(k, grads[k]) for k in SMALL_GRADS]
        used = sum(b.shape[1] for b in blocks)
        small = jnp.concatenate(blocks + [jnp.zeros((N_CHIPS, SMALL_ROWS * PACK_COLS - used), f32)], axis=1)
        names = tuple(k for k in BIG_GRADS if LOWER_LAYERS[k]) + ("small",)
        gs = [self._blocked(grads[k][0]) for k in names[:-1]] + [small.reshape(1, N_CHIPS, SMALL_ROWS, PACK_COLS)]
        parts = dict(zip(names, self._sums("lo_", names, gs, _pair_exchange(gs))))
        slots = dict(zip(names, _chip_exchange([parts[k] for k in names])))
        fulls = []
        for i, k in enumerate(BIG_GRADS):
            nlo, nup = LOWER_LAYERS[k], self.parts_up[i].shape[0]
            full = _chip_sum("grad_chip_sum_up_" + k, self.slots_up[i], self.parts_up[i], self.pos, nlo, nlo + nup, None)
            if nlo:
                full = _chip_sum("grad_chip_sum_lo_" + k, slots[k], parts[k], self.pos, 0, nlo + nup, full)
            fulls.append(full)
        fulls.append(_chip_sum("grad_chip_sum_lo_small", slots["small"], parts["small"], self.pos, 0, 1, None))
        full = _pair_gather(fulls)
        out = {k: f.reshape(shard_shapes[k]) for k, f in zip(BIG_GRADS, full)}
        flat, off = full[-1].reshape(-1), 0
        for k in SMALL_GRADS:
            n = int(np.prod(shard_shapes[k]))
            out[k] = flat[off:off + n].reshape(shard_shapes[k])
            off += n
        return out


def _adamw_fn(w, g, m, v):
    m = ADAM_B1 * m + (1.0 - ADAM_B1) * g
    v = ADAM_B2 * v + (1.0 - ADAM_B2) * (g * g)
    m_hat = m / (1.0 - ADAM_B1 ** ADAM_STEP)
    v_hat = v / (1.0 - ADAM_B2 ** ADAM_STEP)
    return -ADAM_LR * (m_hat / (jnp.sqrt(v_hat) + ADAM_EPS) + ADAM_WD * w), m, v


def _adamw(name, w, g, m, v):
    shp = w.shape
    if w.size >= 1 << 16:
        width = shp[-1]
        ops = [a.reshape(-1, width) for a in (w, g, m, v)]
        res = _rows(name, _adamw_fn, ops, [], [(width, f32)] * 3)
        return [r.reshape(shp) for r in res]
    n = w.size
    rows = -(-n // (SUBLANES * LANES)) * SUBLANES
    ops = [jnp.pad(a.reshape(-1), (0, rows * LANES - n)).reshape(rows, LANES) for a in (w, g, m, v)]
    res = _rows(name, _adamw_fn, ops, [], [(LANES, f32)] * 3, tr=rows)
    return [r.reshape(-1)[:n].reshape(shp) for r in res]


def kernel(x, attn_w_in, attn_rel_bias, attn_w_out, rg_w_in, rg_conv_w, rg_conv_b, rg_w_a, rg_b_a, rg_w_i, rg_b_i, rg_lambda, rg_w_out, norm_mix_pre, norm_mix_post, norm_ffn_pre, norm_ffn_post, ffn_w_gate, ffn_w_up, ffn_w_down, loss_target, m_attn_w_in, m_attn_rel_bias, m_attn_w_out, m_rg_w_in, m_rg_conv_w, m_rg_conv_b, m_rg_w_a, m_rg_b_a, m_rg_w_i, m_rg_b_i, m_rg_lambda, m_rg_w_out, m_norm_mix_pre, m_norm_mix_post, m_norm_ffn_pre, m_norm_ffn_post, m_ffn_w_gate, m_ffn_w_up, m_ffn_w_down, v_attn_w_in, v_attn_rel_bias, v_attn_w_out, v_rg_w_in, v_rg_conv_w, v_rg_conv_b, v_rg_w_a, v_rg_b_a, v_rg_w_i, v_rg_b_i, v_rg_lambda, v_rg_w_out, v_norm_mix_pre, v_norm_mix_post, v_norm_ffn_pre, v_norm_ffn_post, v_ffn_w_gate, v_ffn_w_up, v_ffn_w_down):
    w = dict(zip(WEIGHTS, (attn_w_in, attn_rel_bias, attn_w_out, rg_w_in, rg_conv_w, rg_conv_b, rg_w_a, rg_b_a, rg_w_i,
                           rg_b_i, rg_lambda, rg_w_out, norm_mix_pre, norm_mix_post, norm_ffn_pre, norm_ffn_post,
                           ffn_w_gate, ffn_w_up, ffn_w_down)))
    m = dict(zip(WEIGHTS, (m_attn_w_in, m_attn_rel_bias, m_attn_w_out, m_rg_w_in, m_rg_conv_w, m_rg_conv_b, m_rg_w_a,
                           m_rg_b_a, m_rg_w_i, m_rg_b_i, m_rg_lambda, m_rg_w_out, m_norm_mix_pre, m_norm_mix_post,
                           m_norm_ffn_pre, m_norm_ffn_post, m_ffn_w_gate, m_ffn_w_up, m_ffn_w_down)))
    v = dict(zip(WEIGHTS, (v_attn_w_in, v_attn_rel_bias, v_attn_w_out, v_rg_w_in, v_rg_conv_w, v_rg_conv_b, v_rg_w_a,
                           v_rg_b_a, v_rg_w_i, v_rg_b_i, v_rg_lambda, v_rg_w_out, v_norm_mix_pre, v_norm_mix_post,
                           v_norm_ffn_pre, v_norm_ffn_post, v_ffn_w_gate, v_ffn_w_up, v_ffn_w_down)))
    plan = _WeightGather(w)
    exch = _GradExchange()
    loss, dx, grads = _local_step(x[0], loss_target[0], plan.views(), plan, exch)
    loss = lax.psum(loss, ("x", "y", "c"))
    g = exch.finish(grads, {k: _natural(k, w[k]).shape for k in WEIGHTS})

    big = [k for k in WEIGHTS if k not in SMALL]
    upd = {}
    for k in big:
        res = _adamw("adamw_" + k, _natural(k, w[k]), g[k], _natural(k, m[k]), _natural(k, v[k]))
        upd[k] = [_natural(k, r) for r in res]
        g[k] = _natural(k, g[k])
    cat = lambda d: jnp.concatenate([d[k].reshape(-1) for k in SMALL])
    small = _adamw("adamw_small", cat(w), cat(g), cat(m), cat(v))
    off = 0
    for k in SMALL:
        n = w[k].size
        upd[k] = [r[off:off + n].reshape(w[k].shape) for r in small]
        off += n
    return (loss, dx[None], *[g[k] for k in WEIGHTS], *[upd[k][0] for k in WEIGHTS],
            *[upd[k][1] for k in WEIGHTS], *[upd[k][2] for k in WEIGHTS])
```

```python
import functools

import numpy as np
import jax
import jax.numpy as jnp
from jax import lax
from jax.experimental import pallas as pl
from jax.experimental.pallas import tpu as pltpu

f32 = jnp.float32
bf16 = jnp.bfloat16
SDS = jax.ShapeDtypeStruct
MESH = pl.DeviceIdType.MESH

D_MODEL = 1024
N_CHIPS = 4
DEPTH = 4
HEAD_DIM = 64
CHUNK = 64
N_LEFT = 8
REL_CLIP = 256
A_W = 512
LRU_BLOCKS = 4
LRU_BW = 256
LRU_C = 8.0
D_FF = 2816
RMS_EPS = 1e-6
LANES = 128
SUBLANES = 8
VMEM_LIMIT = 56 * 1024 * 1024

QB_A = 2 * CHUNK
QSUB_A = 4
KW_A = QB_A + N_LEFT * CHUNK
PAD_A = N_LEFT * CHUNK
EXT_A = 768
SB_BLK = 256
QSUB_B = 4
SB_DEAD = -110.0

ADAM_LR, ADAM_B1, ADAM_B2, ADAM_EPS, ADAM_WD, ADAM_STEP = 0.001, 0.9, 0.999, 1e-08, 0.01, 10


def _cparams(sem):
    return pltpu.CompilerParams(dimension_semantics=sem, vmem_limit_bytes=VMEM_LIMIT)


def _gemm(name, operands, in_specs, o_spec, out_shape, grid, dims, acc_shape, into=None):
    nred = grid[2]
    npair = len(operands) // 2
    nin = 2 * npair + (into is not None)

    def body(*refs):
        o_ref = refs[nin]
        p = None
        for t in range(npair):
            d = lax.dot_general(refs[2 * t][...], refs[2 * t + 1][...], (dims, ((), ())),
                                preferred_element_type=f32)
            p = d if p is None else p + d
        if nred == 1:
            o_ref[...] = p.astype(o_ref.dtype)
        else:
            acc = refs[nin + 1]
            r = pl.program_id(2)

            @pl.when(r == 0)
            def _():
                acc[...] = p

            @pl.when(r > 0)
            def _():
                acc[...] += p

            @pl.when(r == nred - 1)
            def _():
                o_ref[...] = acc[...].astype(o_ref.dtype)

    scratch = [] if nred == 1 else [pltpu.VMEM(acc_shape, f32)]
    extra, alias = ([], {}) if into is None else ([into], {2 * npair: 0})
    return pl.pallas_call(
        body, grid=grid, in_specs=list(in_specs) + [pl.BlockSpec(memory_space=pl.ANY)] * len(extra),
        out_specs=o_spec, out_shape=out_shape, scratch_shapes=scratch, name=name, input_output_aliases=alias,
        compiler_params=_cparams(("parallel", "parallel", "arbitrary")))(*operands, *extra)


LOWER_LAYERS = {"attn_w_in": 1, "attn_w_out": 1, "rg_w_in": 0, "rg_w_out": 0,
                "ffn_w_gate": 0, "ffn_w_up": 0, "ffn_w_down": 0}


def _grad_slot(name, l):
    n = LOWER_LAYERS[name]
    return (0, l) if l < n else (1, l - n)


class _Fresh:
    def __init__(self, shape):
        self.shape = tuple(shape)


def _into(buf):
    return None if isinstance(buf, _Fresh) else buf


NN = ((1,), (0,))
NT = ((1,), (1,))
TN = ((0,), (0,))


WGRAD_TOKENS = 2048


def _tile(t, want=1024):
    return min(want, t)


def _mm_cols(name, a, w, l, out_dtype):
    t, k = a.shape
    _, s, _, ns = w.shape
    tm = _tile(t)
    return _gemm(
        name, [a, w],
        [pl.BlockSpec((tm, k), lambda i, j, r: (i, 0)),
         pl.BlockSpec((None, None, k, ns), lambda i, j, r: (l, j, 0, 0))],
        pl.BlockSpec((tm, ns), lambda i, j, r: (i, j)),
        SDS((t, s * ns), out_dtype), (t // tm, s, 1), NN, None)


def _mm_cols_t(name, dy, w, l, out_dtype):
    t = dy.shape[0]
    _, s, k, ns = w.shape
    tm = _tile(t)
    return _gemm(
        name, [dy, w],
        [pl.BlockSpec((tm, ns), lambda i, j, r: (i, r)),
         pl.BlockSpec((None, None, k, ns), lambda i, j, r: (l, r, 0, 0))],
        pl.BlockSpec((tm, k), lambda i, j, r: (i, 0)),
        SDS((t, k), out_dtype), (t // tm, 1, s), NT, (tm, k))


def _mm_wgrad_cols(name, a, dy, buf, l):
    t, k = a.shape
    _, s, _, ns = buf.shape
    tt = _tile(t, WGRAD_TOKENS)
    return _gemm(
        name, [a, dy],
        [pl.BlockSpec((tt, k), lambda i, j, r: (r, 0)),
         pl.BlockSpec((tt, ns), lambda i, j, r: (r, i))],
        pl.BlockSpec((None, None, k, ns), lambda i, j, r: (l, i, 0, 0)),
        SDS(buf.shape, f32), (s, 1, t // tt), TN, (k, ns), into=_into(buf))


def _mm_rows(name, parts, w, l, out_dtype):
    t = parts[0].shape[0]
    n = w.shape[3]
    tm = _tile(t)
    ops, specs = [], []
    for p_i, a in enumerate(parts):
        kp = a.shape[1]
        ops += [a, w]
        specs += [pl.BlockSpec((tm, kp), lambda i, j, r: (i, 0)),
                  pl.BlockSpec((None, None, kp, n), lambda i, j, r, p_i=p_i: (l, 0, p_i, 0))]
    return _gemm(name, ops, specs, pl.BlockSpec((tm, n), lambda i, j, r: (i, 0)),
                 SDS((t, n), out_dtype), (t // tm, 1, 1), NN, None)


def _mm_rows_t(name, dy, w, l, out_dtype):
    t, n = dy.shape
    k = w.shape[2]
    tm = _tile(t)
    return _gemm(
        name, [dy, w],
        [pl.BlockSpec((tm, n), lambda i, j, r: (i, 0)),
         pl.BlockSpec((None, None, k, n), lambda i, j, r: (l, 0, 0, 0))],
        pl.BlockSpec((tm, k), lambda i, j, r: (i, 0)),
        SDS((t, k), out_dtype), (t // tm, 1, 1), NT, None)


def _mm_wgrad(name, a, dy, buf, l, part=0):
    t, k = a.shape
    n = dy.shape[1]
    tt = _tile(t)
    return _gemm(
        name, [a, dy],
        [pl.BlockSpec((tt, k), lambda i, j, r: (r, 0)),
         pl.BlockSpec((tt, n), lambda i, j, r: (r, 0))],
        pl.BlockSpec((None, k, n), lambda i, j, r: (l, part, 0)),
        SDS(buf.shape, f32), (1, 1, t // tt), TN, (k, n), into=_into(buf))


def _ffn_up(h, wg, wu, l, gather):
    t, k = h.shape
    s, fs = wg.shape[1], wg.shape[2]
    tm = _tile(t)

    def body(h_ref, wg_ref, wu_ref, g_ref, u_ref, hid_ref):
        hv = h_ref[...]
        g = lax.dot_general(hv, wg_ref[...], (NT, ((), ())), preferred_element_type=f32)
        u = lax.dot_general(hv, wu_ref[...], (NT, ((), ())), preferred_element_type=f32)
        g_ref[...] = g.astype(bf16)
        u_ref[...] = u.astype(bf16)
        hid_ref[...] = (g * jax.nn.sigmoid(g) * u).astype(bf16)

    wspec = pl.BlockSpec((None, None, fs, k), lambda j, i: (l, j, 0, 0))
    ospec = pl.BlockSpec((None, tm, fs), lambda j, i: (j, i, 0))
    return _call(
        body, [h, wg, wu], grid=(s, t // tm), name="ffn_up",
        in_specs=[pl.BlockSpec((tm, k), lambda j, i: (i, 0)), wspec, wspec],
        out_specs=[ospec, ospec, ospec], out_shape=[SDS((s, t, fs), bf16)] * 3,
        sem=("parallel", "parallel"), gather=gather)


def _ffn_down(hid, wd, l):
    s, t, fs = hid.shape
    n = wd.shape[3]
    tm = _tile(t, 512)
    ops, specs = [], []
    for r in range(s):
        ops += [hid, wd]
        specs += [pl.BlockSpec((None, tm, fs), lambda i, j, k, r=r: (r, i, 0)),
                  pl.BlockSpec((None, None, fs, n), lambda i, j, k, r=r: (l, r, 0, 0))]
    return _gemm("ffn_down", ops, specs, pl.BlockSpec((tm, n), lambda i, j, k: (i, 0)),
                 SDS((t, n), f32), (t // tm, 1, 1), NN, None)


def _ffn_down_bwd(df, wd, l, g, u, gather):
    t, n = df.shape
    s, fs = wd.shape[1], wd.shape[2]
    tm = _tile(t, 512)

    def body(df_ref, wd_ref, g_ref, u_ref, dg_ref, du_ref):
        dh = lax.dot_general(df_ref[...], wd_ref[...], (NT, ((), ())), preferred_element_type=f32)
        gv = g_ref[...].astype(f32)
        uv = u_ref[...].astype(f32)
        sg = jax.nn.sigmoid(gv)
        du_ref[...] = (dh * gv * sg).astype(bf16)
        dg_ref[...] = (dh * uv * (sg * (1.0 + gv * (1.0 - sg)))).astype(bf16)

    bspec = pl.BlockSpec((None, tm, fs), lambda j, i: (j, i, 0))
    return _call(
        body, [df, wd, g, u], grid=(s, t // tm), name="ffn_down_bwd",
        in_specs=[pl.BlockSpec((tm, n), lambda j, i: (i, 0)),
                  pl.BlockSpec((None, None, fs, n), lambda j, i: (l, j, 0, 0)), bspec, bspec],
        out_specs=[bspec, bspec], out_shape=[SDS((s, t, fs), bf16)] * 2,
        sem=("parallel", "parallel"), gather=gather)


def _ffn_up_bwd(dg, du, wg, wu, l):
    s, t, fs = dg.shape
    k = wg.shape[3]
    tm = _tile(t, 512)
    ops, specs = [], []
    for r in range(s):
        aspec = pl.BlockSpec((None, tm, fs), lambda i, j, kk, r=r: (r, i, 0))
        wspec = pl.BlockSpec((None, None, fs, k), lambda i, j, kk, r=r: (l, r, 0, 0))
        ops += [dg, wg, du, wu]
        specs += [aspec, wspec, aspec, wspec]
    return _gemm("ffn_up_bwd", ops, specs, pl.BlockSpec((tm, k), lambda i, j, kk: (i, 0)),
                 SDS((t, k), f32), (t // tm, 1, 1), NN, None)


def _ffn_wgrad_up(h, dg, du, buf_g, buf_u, l):
    t, k = h.shape
    s, _, fs = dg.shape
    tt = _tile(t, WGRAD_TOKENS)
    nred = t // tt

    fresh = isinstance(buf_g, _Fresh)

    def body(*refs):
        h_ref, dg_ref, du_ref = refs[:3]
        og_ref, ou_ref, acc_g, acc_u = refs[-4:]
        r = pl.program_id(1)
        hv = h_ref[...]
        pg = lax.dot_general(dg_ref[...], hv, (TN, ((), ())), preferred_element_type=f32)
        pu = lax.dot_general(du_ref[...], hv, (TN, ((), ())), preferred_element_type=f32)

        @pl.when(r == 0)
        def _():
            acc_g[...] = pg
            acc_u[...] = pu

        @pl.when(r > 0)
        def _():
            acc_g[...] += pg
            acc_u[...] += pu

        @pl.when(r == nred - 1)
        def _():
            og_ref[...] = acc_g[...]
            ou_ref[...] = acc_u[...]

    dspec = pl.BlockSpec((None, tt, fs), lambda i, r: (i, r, 0))
    ospec = pl.BlockSpec((None, None, fs, k), lambda i, r: (l, i, 0, 0))
    extra, alias = ([], {}) if fresh else ([buf_g, buf_u], {3: 0, 4: 1})
    return pl.pallas_call(
        body, grid=(s, nred), name="ffn_wgrad_up",
        in_specs=[pl.BlockSpec((tt, k), lambda i, r: (r, 0)), dspec, dspec] + [ANY] * len(extra),
        out_specs=[ospec, ospec], out_shape=[SDS(buf_g.shape, f32), SDS(buf_u.shape, f32)],
        scratch_shapes=[pltpu.VMEM((fs, k), f32)] * 2, input_output_aliases=alias,
        compiler_params=_cparams(("parallel", "arbitrary")))(h, dg, du, *extra)


def _ffn_wgrad_down(hid, df, buf, l):
    s, t, fs = hid.shape
    n = df.shape[1]
    tt = _tile(t, WGRAD_TOKENS)
    return _gemm(
        "ffn_wgrad_down", [hid, df],
        [pl.BlockSpec((None, tt, fs), lambda i, j, r: (i, r, 0)),
         pl.BlockSpec((tt, n), lambda i, j, r: (r, 0))],
        pl.BlockSpec((None, None, fs, n), lambda i, j, r: (l, i, 0, 0)),
        SDS(buf.shape, f32), (s, 1, t // tt), TN, (fs, n), into=_into(buf))


def _rows(name, fn, rows, consts, row_outs, acc_outs=(), tr=512):
    rows = [r if isinstance(r, tuple) else (r, r.shape[1], 0) for r in rows]
    t = rows[0][0].shape[0]
    tr = max(d for d in range(SUBLANES, min(tr, t) + 1, SUBLANES) if t % d == 0)
    nin = len(rows) + len(consts)
    no, na = len(row_outs), len(acc_outs)

    def body(*refs):
        vals = fn(*[r[...] for r in refs[:nin]])
        if not isinstance(vals, (tuple, list)):
            vals = (vals,)
        for k in range(no):
            refs[nin + k][...] = vals[k].astype(refs[nin + k].dtype)
        first = pl.program_id(0) == 0
        for k in range(na):
            ref, val = refs[nin + no + k], vals[no + k]

            @pl.when(first)
            def _(ref=ref, val=val):
                ref[...] = val

            @pl.when(jnp.logical_not(first))
            def _(ref=ref, val=val):
                ref[...] += val

    in_specs = [pl.BlockSpec((tr, w), lambda i, cb=cb: (i, cb)) for (_, w, cb) in rows]
    in_specs += [pl.BlockSpec(c.shape, lambda i, nd=c.ndim: (0,) * nd) for c in consts]
    out_specs = [pl.BlockSpec((tr, w), lambda i: (i, 0)) for (w, _) in row_outs]
    out_specs += [pl.BlockSpec(s, lambda i, nd=len(s): (0,) * nd) for (s, _) in acc_outs]
    out_shape = [SDS((t, w), dt) for (w, dt) in row_outs] + [SDS(s, dt) for (s, dt) in acc_outs]
    res = pl.pallas_call(
        body, grid=(t // tr,), in_specs=in_specs, out_specs=out_specs, out_shape=out_shape,
        name=name, compiler_params=_cparams(("arbitrary",)))(*[r[0] for r in rows], *consts)
    return res


def _rstd(x):
    return lax.rsqrt(jnp.mean(x * x, axis=-1, keepdims=True) + RMS_EPS)


def _norm_fwd(x, g):
    return x * _rstd(x) * g


def _norm_bwd(u, dy, g):
    r = _rstd(u)
    n = u * r
    dn = dy * g
    du = r * (dn - n * jnp.mean(dn * n, axis=-1, keepdims=True))
    return du, jnp.sum(dy * n, axis=0, keepdims=True)


def _gelu(x):
    c = 0.7978845608028654
    return 0.5 * x * (1.0 + jnp.tanh(c * (x + 0.044715 * x * x * x)))


def _gelu_grad(x):
    c = 0.7978845608028654
    th = jnp.tanh(c * (x + 0.044715 * x * x * x))
    return 0.5 * (1.0 + th) + 0.5 * x * (1.0 - th * th) * c * (1.0 + 3.0 * 0.044715 * x * x)


def _mask_heads(x):
    lane = lax.broadcasted_iota(jnp.int32, x.shape, 1)
    return [jnp.where((lane >= h * HEAD_DIM) & (lane < (h + 1) * HEAD_DIM), x, jnp.zeros_like(x))
            for h in range(LANES // HEAD_DIM)]


def _chunk_valid(start):
    qi = lax.broadcasted_iota(jnp.int32, (QB_A, KW_A), 0)
    kj = lax.broadcasted_iota(jnp.int32, (QB_A, KW_A), 1)
    qc = qi // CHUNK
    kc = kj // CHUNK
    return (kc >= qc) & (kc <= qc + N_LEFT) & (kj + start >= PAD_A)


def _chunk_probs(q, k, bias, valid):
    s = lax.dot_general(q, k, (NT, ((), ())), preferred_element_type=f32) * (HEAD_DIM ** -0.5) + bias
    s = jnp.where(valid, s, -1e30)
    p = jnp.exp(s - jnp.max(s, axis=-1, keepdims=True))
    return p / jnp.sum(p, axis=-1, keepdims=True)


def _chunk_attn_fwd(proj, kpad, vpad, bias, gather):
    t = proj.shape[0]
    tp = kpad.shape[0]
    step = QSUB_A * QB_A

    def body(q_ref, k_ref, v_ref, b_ref, o_ref):
        for sb in range(QSUB_A):
            start = pl.multiple_of((pl.program_id(1) * QSUB_A + sb) * QB_A, QB_A)
            rows = pl.ds(sb * QB_A, QB_A)
            valid = _chunk_valid(start)
            kw = k_ref[pl.ds(start, KW_A), :]
            qm = _mask_heads(q_ref[rows, :])
            vm = _mask_heads(v_ref[pl.ds(start, KW_A), :])
            o = None
            for h in range(len(qm)):
                p = _chunk_probs(qm[h], kw, b_ref[h], valid)
                d = jnp.dot(p.astype(bf16), vm[h], preferred_element_type=f32)
                o = d if o is None else o + d
            o_ref[rows, :] = o.astype(bf16)

    kv_spec = pl.BlockSpec((tp, LANES), lambda hp, qb: (0, hp))
    outs, new = _call(
        body, [proj, kpad, vpad, bias], grid=(A_W // LANES, t // step), name="chunk_attn_fwd",
        in_specs=[pl.BlockSpec((step, LANES), lambda hp, qb: (qb, hp)), kv_spec, kv_spec,
                  pl.BlockSpec((2, QB_A, KW_A), lambda hp, qb: (hp, 0, 0))],
        out_specs=[pl.BlockSpec((step, LANES), lambda hp, qb: (qb, hp))],
        out_shape=[SDS((t, A_W), bf16)], sem=("parallel", "arbitrary"), gather=gather)
    return outs[0], new


def _chunk_attn_bwd(proj, kpad, vpad, bias, dout, gather):
    t = proj.shape[0]
    tp = kpad.shape[0]
    step = QSUB_A * QB_A

    def body(q_ref, k_ref, v_ref, b_ref, do_ref, dq_ref, dk_ref, dv_ref, db_ref):
        qb = pl.program_id(1)

        @pl.when(qb == 0)
        def _():
            dk_ref[...] = jnp.zeros_like(dk_ref)
            dv_ref[...] = jnp.zeros_like(dv_ref)
            db_ref[...] = jnp.zeros_like(db_ref)

        for sb in range(QSUB_A):
            start = pl.multiple_of((qb * QSUB_A + sb) * QB_A, QB_A)
            rows = pl.ds(sb * QB_A, QB_A)
            win = pl.ds(start, KW_A)
            valid = _chunk_valid(start)
            kw = k_ref[win, :]
            vw = v_ref[win, :]
            qm = _mask_heads(q_ref[rows, :])
            dom = _mask_heads(do_ref[rows, :])
            km = _mask_heads(kw)
            dq = dk = dv = None
            for h in range(len(qm)):
                p = _chunk_probs(qm[h], kw, b_ref[h], valid)
                dp = lax.dot_general(dom[h], vw, (NT, ((), ())), preferred_element_type=f32)
                ds = p * (dp - jnp.sum(dp * p, axis=-1, keepdims=True))
                db_ref[h] += ds
                dsb = (ds * (HEAD_DIM ** -0.5)).astype(bf16)
                terms = (jnp.dot(dsb, km[h], preferred_element_type=f32),
                         lax.dot_general(dsb, qm[h], (TN, ((), ())), preferred_element_type=f32),
                         lax.dot_general(p.astype(bf16), dom[h], (TN, ((), ())), preferred_element_type=f32))
                dq, dk, dv = terms if dq is None else (dq + terms[0], dk + terms[1], dv + terms[2])
            dq_ref[rows, :] = dq.astype(bf16)
            dk_ref[win, :] += dk
            dv_ref[win, :] += dv

    kv_spec = pl.BlockSpec((tp, LANES), lambda hp, qb: (0, hp))
    q_spec = pl.BlockSpec((step, LANES), lambda hp, qb: (qb, hp))
    b_spec = pl.BlockSpec((2, QB_A, KW_A), lambda hp, qb: (hp, 0, 0))
    return _call(
        body, [proj, kpad, vpad, bias, dout], grid=(A_W // LANES, t // step), name="chunk_attn_bwd",
        in_specs=[q_spec, kv_spec, kv_spec, b_spec, q_spec],
        out_specs=[q_spec, kv_spec, kv_spec, b_spec],
        out_shape=[SDS((t, A_W), bf16), SDS((tp, A_W), f32), SDS((tp, A_W), f32),
                   SDS((2 * A_W // LANES, QB_A, KW_A), f32)],
        sem=("parallel", "arbitrary"), gather=gather)


def _bias_ext(table):
    flat = PAD_A + QB_A - 1 - REL_CLIP
    top = jnp.broadcast_to(table[:, 2 * REL_CLIP:], (table.shape[0], flat))
    lo = 2 * REL_CLIP - (EXT_A - 1 - flat)
    return jnp.concatenate([top, jnp.flip(table[:, lo:], axis=1)], axis=1)


def _bias_window(table):
    nh = table.shape[0]
    e = jnp.broadcast_to(_bias_ext(table)[:, None, :], (nh, QB_A, EXT_A)).reshape(nh, QB_A * EXT_A)
    m = e[:, :QB_A * (EXT_A - 1)].reshape(nh, QB_A, EXT_A - 1)
    return m[:, :, QB_A - 1:]


def _bias_window_grad(dbias):
    nh = dbias.shape[0]
    m = jnp.pad(dbias, ((0, 0), (0, 0), (QB_A - 1, 0))).reshape(nh, QB_A * (EXT_A - 1))
    dext = jnp.sum(jnp.pad(m, ((0, 0), (0, QB_A))).reshape(nh, QB_A, EXT_A), axis=1)
    flat = PAD_A + QB_A - 1 - REL_CLIP
    lo = 2 * REL_CLIP - (EXT_A - 1 - flat)
    tail = jnp.flip(dext[:, flat:], axis=1)
    tail = tail.at[:, -1].add(jnp.sum(dext[:, :flat], axis=1))
    return jnp.pad(tail, ((0, 0), (lo, 0)))


def _tri_suffix(x, tri):
    hi = x.astype(bf16)
    lo = (x - hi.astype(f32)).astype(bf16)
    return jnp.dot(hi, tri, preferred_element_type=f32) + jnp.dot(lo, tri, preferred_element_type=f32)


def _sb_block(q, k, run, tri, causal):
    z = lax.dot_general(q, k, (NT, ((), ())), preferred_element_type=f32) * (HEAD_DIM ** -0.5)
    e = jnp.exp(-jnp.abs(z))
    l1p = jnp.log(1.0 + e)
    lb = jnp.minimum(z, 0.0) - l1p
    lmb = lb - z
    if causal is not None:
        lmb = jnp.where(causal, lmb, 0.0)
    cs = _tri_suffix(lmb, tri)
    w = jnp.exp(lb + (run + cs - lmb))
    if causal is not None:
        w = jnp.where(causal, w, 0.0)
    return z, e, w, run + cs[:, 0:1]


def _sb_tri():
    r = lax.broadcasted_iota(jnp.int32, (SB_BLK, SB_BLK), 0)
    c = lax.broadcasted_iota(jnp.int32, (SB_BLK, SB_BLK), 1)
    return (r >= c).astype(bf16), c < r


def _sb_live(runs):
    m = runs[0]
    for r in runs[1:]:
        m = jnp.maximum(m, r)
    return jnp.max(m) > SB_DEAD


def _sb_fwd(proj, gather):
    t = proj.shape[0]
    cb = A_W // LANES
    nh = LANES // HEAD_DIM

    step_rows = QSUB_B * SB_BLK

    def body(q_ref, k_ref, v_ref, o_ref, of_ref):
        tri, diag = _sb_tri()
        for sb in range(QSUB_B):
            _sb_fwd_block(pl.program_id(1) * QSUB_B + sb, pl.ds(sb * SB_BLK, SB_BLK), tri, diag,
                          q_ref, k_ref, v_ref, o_ref, of_ref)

    def _sb_fwd_block(qb, qrows, tri, diag, q_ref, k_ref, v_ref, o_ref, of_ref):
        qm = _mask_heads(q_ref[qrows, :])

        def pair(kb, carry, causal):
            rows = pl.ds(pl.multiple_of(kb * SB_BLK, SB_BLK), SB_BLK)
            k = k_ref[rows, :]
            vm = _mask_heads(v_ref[rows, :])
            runs, acc = [], carry[nh]
            for h in range(nh):
                _, _, w, run = _sb_block(qm[h], k, carry[h], tri, causal)
                acc = acc + jnp.dot(w.astype(bf16), vm[h], preferred_element_type=f32)
                runs.append(run)
            return (*runs, acc)

        zero = jnp.zeros((SB_BLK, 1), f32)
        carry = pair(qb, (zero,) * nh + (jnp.zeros((SB_BLK, LANES), f32),), diag)

        def cond(st):
            return (st[0] < qb) & _sb_live(st[1][:nh])

        def step(st):
            return st[0] + 1, pair(qb - 1 - st[0], st[1], None)

        _, carry = lax.while_loop(cond, step, (jnp.int32(0), carry))
        o_ref[qrows, :] = carry[nh].astype(bf16)
        of_ref[qrows, :] = carry[nh]

    ospec = pl.BlockSpec((step_rows, LANES), lambda hp, qb: (qb, hp))
    return _call(
        body, [proj, proj, proj], grid=(cb, t // step_rows), name="sb_attn_fwd",
        in_specs=[pl.BlockSpec((step_rows, LANES), lambda hp, qb: (qb, 3 * cb + hp)),
                  pl.BlockSpec((t, LANES), lambda hp, qb: (0, 4 * cb + hp)),
                  pl.BlockSpec((t, LANES), lambda hp, qb: (0, 5 * cb + hp))],
        out_specs=[ospec, ospec], out_shape=[SDS((t, A_W), bf16), SDS((t, A_W), f32)],
        sem=("parallel", "arbitrary"), gather=gather)


def _sb_bwd(proj, out_b, dout, gather):
    t = proj.shape[0]
    cb = A_W // LANES
    nh = LANES // HEAD_DIM

    step_rows = QSUB_B * SB_BLK

    def body(q_ref, k_ref, v_ref, o_ref, do_ref, dq_ref, dk_ref, dv_ref):
        tri, diag = _sb_tri()

        @pl.when(pl.program_id(1) == 0)
        def _():
            dk_ref[...] = jnp.zeros_like(dk_ref)
            dv_ref[...] = jnp.zeros_like(dv_ref)

        for sb in range(QSUB_B):
            _sb_bwd_block(pl.program_id(1) * QSUB_B + sb, pl.ds(sb * SB_BLK, SB_BLK), tri, diag,
                          q_ref, k_ref, v_ref, o_ref, do_ref, dq_ref, dk_ref, dv_ref)

    def _sb_bwd_block(qb, qrows, tri, diag, q_ref, k_ref, v_ref, o_ref, do_ref, dq_ref, dk_ref, dv_ref):
        qm = _mask_heads(q_ref[qrows, :])
        do = do_ref[qrows, :]
        dom = _mask_heads(do)
        dsums = [jnp.sum(t_, axis=-1, keepdims=True) for t_ in _mask_heads(do.astype(f32) * o_ref[qrows, :])]

        def pair(kb, carry, causal):
            rows = pl.ds(pl.multiple_of(kb * SB_BLK, SB_BLK), SB_BLK)
            k = k_ref[rows, :]
            v = v_ref[rows, :]
            km = _mask_heads(k)
            new, dq, dk, dv = [], carry[2 * nh], None, None
            for h in range(nh):
                z, e, w, run = _sb_block(qm[h], k, carry[2 * h], tri, causal)
                inv = 1.0 / (1.0 + e)
                beta = jnp.where(z >= 0.0, inv, e * inv)
                wb = w.astype(bf16)
                g = lax.dot_general(dom[h], v, (NT, ((), ())), preferred_element_type=f32) * wb.astype(f32)
                sg = _tri_suffix(g, tri)
                dz = g * (1.0 - beta) - (dsums[h] - carry[2 * h + 1] - sg) * beta
                if causal is not None:
                    dz = jnp.where(causal, dz, 0.0)
                dzb = (dz * (HEAD_DIM ** -0.5)).astype(bf16)
                dq = dq + jnp.dot(dzb, km[h], preferred_element_type=f32)
                tk = lax.dot_general(dzb, qm[h], (TN, ((), ())), preferred_element_type=f32)
                tv = lax.dot_general(wb, dom[h], (TN, ((), ())), preferred_element_type=f32)
                dk, dv = (tk, tv) if dk is None else (dk + tk, dv + tv)
                new += [run, carry[2 * h + 1] + sg[:, 0:1]]
            dk_ref[rows, :] += dk
            dv_ref[rows, :] += dv
            return (*new, dq)

        zero = jnp.zeros((SB_BLK, 1), f32)
        carry = pair(qb, (zero,) * (2 * nh) + (jnp.zeros((SB_BLK, LANES), f32),), diag)

        def cond(st):
            return (st[0] < qb) & _sb_live(st[1][0:2 * nh:2])

        def step(st):
            return st[0] + 1, pair(qb - 1 - st[0], st[1], None)

        _, carry = lax.while_loop(cond, step, (jnp.int32(0), carry))
        dq_ref[qrows, :] = carry[2 * nh].astype(bf16)

    kv_in = lambda seg: pl.BlockSpec((t, LANES), lambda hp, qb: (0, seg * cb + hp))
    q_spec = pl.BlockSpec((step_rows, LANES), lambda hp, qb: (qb, hp))
    kv_out = pl.BlockSpec((t, LANES), lambda hp, qb: (0, hp))
    return _call(
        body, [proj, proj, proj, out_b, dout], grid=(cb, t // step_rows), name="sb_attn_bwd",
        in_specs=[pl.BlockSpec((step_rows, LANES), lambda hp, qb: (qb, 3 * cb + hp)), kv_in(4), kv_in(5),
                  q_spec, pl.BlockSpec((step_rows, LANES), lambda hp, qb: (qb, cb + hp))],
        out_specs=[q_spec, kv_out, kv_out],
        out_shape=[SDS((t, A_W), bf16), SDS((t, A_W), f32), SDS((t, A_W), f32)],
        sem=("parallel", "arbitrary"), gather=gather)


def _halo_specs(tr, w, col, nblk):
    per = tr // SUBLANES
    cur = pl.BlockSpec((tr, w), lambda i: (i, col))
    prev = pl.BlockSpec((SUBLANES, w), lambda i: (jnp.maximum(i * per - 1, 0), col))
    nxt = pl.BlockSpec((SUBLANES, w), lambda i: (jnp.minimum((i + 1) * per, nblk * per - 1), col))
    return cur, prev, nxt


def _taps_before(cur, prev8, first):
    prev8 = jnp.where(first, 0.0, prev8)
    ext = jnp.concatenate([prev8, cur], axis=0)
    return [pltpu.roll(ext, s, 0)[SUBLANES:] for s in (3, 2, 1)]


def _taps_after(cur, next8, last):
    n = cur.shape[0]
    next8 = jnp.where(last, 0.0, next8)
    ext = jnp.concatenate([cur, next8], axis=0)
    return [pltpu.roll(ext, n + SUBLANES - s, 0)[:n] for s in (1, 2, 3)]


def _block_diag(x, w_ref, dims):
    outs = [lax.dot_general(x[:, n * LRU_BW:(n + 1) * LRU_BW], w_ref[n], (dims, ((), ())),
                            preferred_element_type=f32) for n in range(LRU_BLOCKS)]
    return jnp.concatenate(outs, axis=1)


def _lru_gates(xc, wa_ref, wi_ref, ba, bi, lam):
    xb = xc.astype(bf16)
    r = jax.nn.sigmoid(_block_diag(xb, wa_ref, NN) + ba)
    ig = jax.nn.sigmoid(_block_diag(xb, wi_ref, NN) + bi)
    sp = jnp.maximum(-lam, 0.0) + jnp.log(1.0 + jnp.exp(-jnp.abs(lam)))
    log_a = -LRU_C * r * sp
    a = jnp.exp(log_a)
    x2 = 2.0 * log_a
    one_minus = jnp.where(x2 > -1e-2, -x2 * (1.0 + x2 * (0.5 + x2 * (1.0 / 6.0))), 1.0 - a * a)
    mult = jnp.sqrt(one_minus)
    return xb, r, ig, sp, a, mult


def _rg_gates_fwd(proj, conv_w, conv_b, wa, wi, ba, bi, lam, tr=512):
    t = proj.shape[0]
    w = D_MODEL
    tr = min(tr, t)
    nblk = t // tr
    cur, prev, _ = _halo_specs(tr, w, 1, nblk)

    def body(x_ref, xp_ref, cw_ref, cb_ref, wa_ref, wi_ref, ba_ref, bi_ref, lam_ref, xc_ref, a_ref, u_ref):
        x = x_ref[...]
        taps = _taps_before(x, xp_ref[...], pl.program_id(0) == 0) + [x]
        xc = cb_ref[...]
        for k in range(4):
            xc = xc + cw_ref[k:k + 1, :] * taps[k]
        _, _, ig, _, a, mult = _lru_gates(xc, wa_ref, wi_ref, ba_ref[...], bi_ref[...], lam_ref[...])
        xc_ref[...] = xc
        a_ref[...] = a
        u_ref[...] = mult * (ig * xc)

    full = lambda a_: pl.BlockSpec(a_.shape, lambda i, nd=a_.ndim: (0,) * nd)
    ospec = pl.BlockSpec((tr, w), lambda i: (i, 0))
    return pl.pallas_call(
        body, grid=(nblk,), name="rg_gates_fwd",
        in_specs=[cur, prev] + [full(a_) for a_ in (conv_w, conv_b, wa, wi, ba, bi, lam)],
        out_specs=[ospec] * 3, out_shape=[SDS((t, w), f32)] * 3,
        compiler_params=_cparams(("parallel",)))(proj, proj, conv_w, conv_b, wa, wi, ba, bi, lam)


def _lru_scan(name, a, b, reverse, tt=512):
    t, w = a.shape
    tt = min(tt, t)
    nt = t // tt
    ng = tt // SUBLANES

    def body(a_ref, b_ref, h_ref, carry_ref):
        @pl.when(pl.program_id(0) == 0)
        def _():
            carry_ref[...] = jnp.zeros_like(carry_ref)

        row = lax.broadcasted_iota(jnp.int32, (SUBLANES, w), 0)

        def group(gi, carry):
            g = (ng - 1 - gi) if reverse else gi
            rows = pl.ds(pl.multiple_of(g * SUBLANES, SUBLANES), SUBLANES)
            av = a_ref[rows, :]
            bv = b_ref[rows, :]
            for s in (1, 2, 4):
                sh = (SUBLANES - s) if reverse else s
                ok = (row < SUBLANES - s) if reverse else (row >= s)
                a_s = pltpu.roll(av, sh, 0)
                b_s = pltpu.roll(bv, sh, 0)
                bv = jnp.where(ok, av * b_s + bv, bv)
                av = jnp.where(ok, av * a_s, av)
            h = av * carry + bv
            h_ref[rows, :] = h
            edge = h[0:1, :] if reverse else h[SUBLANES - 1:SUBLANES, :]
            return jnp.broadcast_to(edge, (SUBLANES, w))

        carry_ref[...] = lax.fori_loop(0, ng, group, carry_ref[...], unroll=4)

    tmap = (lambda i: (nt - 1 - i, 0)) if reverse else (lambda i: (i, 0))
    spec = pl.BlockSpec((tt, w), tmap)
    return pl.pallas_call(
        body, grid=(nt,), name=name, in_specs=[spec, spec], out_specs=spec,
        out_shape=SDS((t, w), f32), scratch_shapes=[pltpu.VMEM((SUBLANES, w), f32)],
        compiler_params=_cparams(("arbitrary",)))(a, b)


def _rg_gates_bwd(dhs, c, hs, xc, wa, wi, ba, bi, lam, tr=512):
    t, w = xc.shape
    tr = min(tr, t)
    nblk = t // tr
    cur, prev, nxt = _halo_specs(tr, w, 0, nblk)

    def body(dhs_ref, c_ref, cn_ref, hs_ref, hp_ref, xc_ref, wa_ref, wi_ref, ba_ref, bi_ref, lam_ref,
             dxc_ref, dwa_ref, dwi_ref, dba_ref, dbi_ref, dlam_ref):
        i = pl.program_id(0)
        c_next = _taps_after(c_ref[...], cn_ref[...], i == nblk - 1)[0]
        h_prev = _taps_before(hs_ref[...], hp_ref[...], i == 0)[2]
        xc = xc_ref[...]
        lam = lam_ref[...]
        xb, r, ig, sp, a, mult = _lru_gates(xc, wa_ref, wi_ref, ba_ref[...], bi_ref[...], lam)
        dh = dhs_ref[...] + c_next
        dlog_a = dh * h_prev * a - (dh * ig * xc) * (a * a / mult)
        dpre_a = (dlog_a * (-LRU_C * sp) * r * (1.0 - r)).astype(bf16)
        dpre_i = (dh * mult * xc * ig * (1.0 - ig)).astype(bf16)
        dxc_ref[...] = (dh * mult * ig + _block_diag(dpre_a, wa_ref, NT) + _block_diag(dpre_i, wi_ref, NT))
        dsig = 1.0 / (1.0 + jnp.exp(lam))
        sums = [jnp.sum(dpre_a.astype(f32), axis=0, keepdims=True),
                jnp.sum(dpre_i.astype(f32), axis=0, keepdims=True),
                jnp.sum(dlog_a * (-LRU_C * r), axis=0, keepdims=True) * (-dsig)]

        @pl.when(i == 0)
        def _():
            dwa_ref[...] = jnp.zeros_like(dwa_ref)
            dwi_ref[...] = jnp.zeros_like(dwi_ref)
            dba_ref[...] = jnp.zeros_like(dba_ref)
            dbi_ref[...] = jnp.zeros_like(dbi_ref)
            dlam_ref[...] = jnp.zeros_like(dlam_ref)

        for n in range(LRU_BLOCKS):
            sl = slice(n * LRU_BW, (n + 1) * LRU_BW)
            dwa_ref[n] += lax.dot_general(xb[:, sl], dpre_a[:, sl], (TN, ((), ())), preferred_element_type=f32)
            dwi_ref[n] += lax.dot_general(xb[:, sl], dpre_i[:, sl], (TN, ((), ())), preferred_element_type=f32)
        dba_ref[...] += sums[0]
        dbi_ref[...] += sums[1]
        dlam_ref[...] += sums[2]

    full = lambda a_: pl.BlockSpec(a_.shape, lambda i, nd=a_.ndim: (0,) * nd)
    vec = pl.BlockSpec((1, w), lambda i: (0, 0))
    mat = pl.BlockSpec((LRU_BLOCKS, LRU_BW, LRU_BW), lambda i: (0, 0, 0))
    return pl.pallas_call(
        body, grid=(nblk,), name="rg_gates_bwd",
        in_specs=[cur, cur, nxt, cur, prev, cur] + [full(a_) for a_ in (wa, wi, ba, bi, lam)],
        out_specs=[cur, mat, mat, vec, vec, vec],
        out_shape=[SDS((t, w), f32), SDS((LRU_BLOCKS, LRU_BW, LRU_BW), f32), SDS((LRU_BLOCKS, LRU_BW, LRU_BW), f32),
                   SDS((1, w), f32), SDS((1, w), f32), SDS((1, w), f32)],
        compiler_params=_cparams(("arbitrary",)))(dhs, c, c, hs, hs, xc, wa, wi, ba, bi, lam)


def _rg_conv_bwd(dxc, proj, conv_w, tr=512):
    t, w = dxc.shape
    tr = min(tr, t)
    nblk = t // tr
    cur, _, nxt = _halo_specs(tr, w, 0, nblk)
    xcur, xprev, _ = _halo_specs(tr, w, 1, nblk)

    def body(d_ref, dn_ref, x_ref, xp_ref, cw_ref, dx_ref, dcw_ref, dcb_ref):
        i = pl.program_id(0)
        d = d_ref[...]
        x = x_ref[...]
        after = _taps_after(d, dn_ref[...], i == nblk - 1)
        before = _taps_before(x, xp_ref[...], i == 0) + [x]
        dx = cw_ref[3:4, :] * d
        for s in (1, 2, 3):
            dx = dx + cw_ref[3 - s:4 - s, :] * after[s - 1]
        dx_ref[...] = dx.astype(bf16)
        dcw = jnp.concatenate([jnp.sum(d * before[k], axis=0, keepdims=True) for k in range(4)], axis=0)
        dcb = jnp.sum(d, axis=0, keepdims=True)

        @pl.when(i == 0)
        def _():
            dcw_ref[...] = dcw
            dcb_ref[...] = dcb

        @pl.when(i > 0)
        def _():
            dcw_ref[...] += dcw
            dcb_ref[...] += dcb

    return pl.pallas_call(
        body, grid=(nblk,), name="rg_conv_bwd",
        in_specs=[cur, nxt, xcur, xprev, pl.BlockSpec((4, w), lambda i: (0, 0))],
        out_specs=[cur, pl.BlockSpec((4, w), lambda i: (0, 0)), pl.BlockSpec((1, w), lambda i: (0, 0))],
        out_shape=[SDS((t, w), bf16), SDS((4, w), f32), SDS((1, w), f32)],
        compiler_params=_cparams(("arbitrary",)))(dxc, dxc, proj, proj, conv_w)


def _attn_fwd(h, wts, j, plan):
    proj = _mm_cols("attn_in", h, wts["attn_w_in"], j, bf16)
    kpad = jnp.pad(proj[:, A_W:2 * A_W], ((PAD_A, 0), (0, 0)))
    vpad = jnp.pad(proj[:, 2 * A_W:3 * A_W], ((PAD_A, 0), (0, 0)))
    bias = _bias_window(wts["attn_rel_bias"][j])
    plan = plan if j == 0 else None
    out_a = _carried(plan, "chunk_attn_fwd", wts, _chunk_attn_fwd, proj, kpad, vpad, bias)
    out_b, out_b32 = _carried(plan, "sb_attn_fwd", wts, _sb_fwd, proj)
    m = _mm_rows("attn_out", [out_a, out_b], wts["attn_w_out"], j, f32)
    return m, (proj, kpad, vpad, bias, out_a, out_b, out_b32)


def _attn_bwd(dm, h, saved, wts, j, grads, exch):
    proj, kpad, vpad, bias, out_a, out_b, out_b32 = saved
    dout = _mm_rows_t("attn_out_t", dm, wts["attn_w_out"], j, bf16)
    gi, ll = _grad_slot("attn_w_out", j)
    grads["attn_w_out"][gi] = _mm_wgrad("attn_out_wgrad_a", out_a, dm, grads["attn_w_out"][gi], ll, 0)
    grads["attn_w_out"][gi] = _mm_wgrad("attn_out_wgrad_b", out_b, dm, grads["attn_w_out"][gi], ll, 1)
    if exch is not None and j == 0:
        (dqa, dka, dva, dbias), got = _chunk_attn_bwd(proj, kpad, vpad, bias, dout, exch.upper_carry(grads))
        exch.upper_got(got)
        (dqs, dks, dvs), slots = _sb_bwd(proj, out_b32, dout, exch.carry())
        exch.carried(slots)
    else:
        dqa, dka, dva, dbias = _chunk_attn_bwd(proj, kpad, vpad, bias, dout, None)[0]
        dqs, dks, dvs = _sb_bwd(proj, out_b32, dout, None)[0]
    grads["attn_rel_bias"][j] = _bias_window_grad(dbias)
    dproj = jnp.concatenate([dqa, dka[PAD_A:].astype(bf16), dva[PAD_A:].astype(bf16),
                             dqs, dks.astype(bf16), dvs.astype(bf16)], axis=1)
    grads["attn_w_in"][gi] = _mm_wgrad_cols("attn_in_wgrad", h, dproj, grads["attn_w_in"][gi], ll)
    return _mm_cols_t("attn_in_t", dproj, wts["attn_w_in"], j, f32)


def _rg_fwd(h, wts, j, plan):
    proj =_mm_cols("rg_in", h, wts["rg_w_in"], j, f32)
    small = [wts[k][j] for k in ("rg_conv_w", "rg_conv_b", "rg_w_a", "rg_w_i", "rg_b_a", "rg_b_i", "rg_lambda")]
    xc, a, u = _rg_gates_fwd(proj, *small)
    hs = _lru_scan("lru_scan_fwd", a, u, False)
    yp = _rows("rg_gate_out", lambda hv, gv: hv * _gelu(gv), [hs, (proj, D_MODEL, 0)], [], [(D_MODEL, bf16)])[0]
    m = _mm_rows("rg_out", [yp], wts["rg_w_out"], j, f32)
    return m, (proj, xc, a, hs, yp)


def _rg_bwd(dm, h, saved, wts, j, grads, exch):
    proj, xc, a, hs, yp = saved
    dyp = _mm_rows_t("rg_out_t", dm, wts["rg_w_out"], j, f32)
    gi, ll = _grad_slot("rg_w_out", j)
    grads["rg_w_out"][gi] = _mm_wgrad("rg_out_wgrad", yp, dm, grads["rg_w_out"][gi], ll)

    def gate_bwd(dy, hv, gv, av):
        dhs = dy * _gelu(gv)
        return dhs, av * dhs, dy * hv * _gelu_grad(gv)

    dhs, ab, dgate = _rows("rg_gate_out_bwd", gate_bwd, [dyp, hs, (proj, D_MODEL, 0), a], [],
                           [(D_MODEL, f32), (D_MODEL, f32), (D_MODEL, bf16)])
    c = _lru_scan("lru_scan_bwd", a, ab, True)
    wa, wi, ba, bi, lam = [wts[k][j] for k in ("rg_w_a", "rg_w_i", "rg_b_a", "rg_b_i", "rg_lambda")]
    dxc, dwa, dwi, dba, dbi, dlam = _rg_gates_bwd(dhs, c, hs, xc, wa, wi, ba, bi, lam)
    dxr, dcw, dcb = _rg_conv_bwd(dxc, proj, wts["rg_conv_w"][j])
    for k, v in (("rg_w_a", dwa), ("rg_w_i", dwi), ("rg_b_a", dba), ("rg_b_i", dbi), ("rg_lambda", dlam),
                 ("rg_conv_w", dcw), ("rg_conv_b", dcb)):
        grads[k][j] = v
    dproj = jnp.concatenate([dgate, dxr], axis=1)
    grads["rg_w_in"][gi] = _mm_wgrad_cols("rg_in_wgrad", h, dproj, grads["rg_w_in"][gi], ll)
    return _mm_cols_t("rg_in_t", dproj, wts["rg_w_in"], j, f32)


def _local_step(x, target, wts, plan=None, exch=None):
    t = x.shape[0]
    d = D_MODEL
    gains = {k: wts[k] for k in ("norm_mix_pre", "norm_mix_post", "norm_ffn_pre", "norm_ffn_post")}
    gain = lambda k, l: gains[k][l:l + 1]

    saved = []
    h = _rows("norm_in", _norm_fwd, [x], [gain("norm_mix_pre", 0)], [(d, bf16)])[0]
    loss_cols = None
    for l in range(DEPTH):
        j = l // 2
        m, mix_saved = (_attn_fwd if l % 2 == 0 else _rg_fwd)(h, wts, j, plan)

        def resid_next(xv, mv, g_post, g_next):
            x1 = xv + _norm_fwd(mv, g_post)
            return x1, _norm_fwd(x1, g_next)

        x1, h2 = _rows("resid_mix", resid_next, [x, m], [gain("norm_mix_post", l), gain("norm_ffn_pre", l)],
                       [(d, f32), (d, bf16)])
        g, u, hid = _carried(plan if l == 0 else None, "ffn_up", wts, _ffn_up, h2, wts["ffn_w_gate"],
                             wts["ffn_w_up"], l)
        f = _ffn_down(hid, wts["ffn_w_down"], l)
        saved.append((x, h, m, mix_saved, x1, h2, g, u, hid, f))
        if l + 1 < DEPTH:
            x, h = _rows("resid_ffn", resid_next, [x1, f], [gain("norm_ffn_post", l), gain("norm_mix_pre", l + 1)],
                         [(d, f32), (d, bf16)])
        else:
            def resid_loss(xv, fv, tv, g_post):
                err = xv + _norm_fwd(fv, g_post) - tv
                return err * (1.0 / d), jnp.sum(err * err, axis=0, keepdims=True)

            dx, loss_cols = _rows("resid_loss", resid_loss, [x1, f, target], [gain("norm_ffn_post", l)],
                                  [(d, f32)], [((1, d), f32)])
    loss = 0.5 * jnp.sum(loss_cols) / d

    grads = {k: {} for k in SMALL_GRADS}
    for k in BIG_GRADS:
        shp = wts[k].shape
        rest = shp[2:] if shp[1] == 1 else shp[1:]
        grads[k] = [_Fresh((LOWER_LAYERS[k],) + rest), _Fresh((shp[0] - LOWER_LAYERS[k],) + rest)]

    def norm_bwd_cast(uv, dyv, gv):
        du, dg = _norm_bwd(uv, dyv, gv)
        return du, dg

    def norm_bwd_resid(uv, dhv, dxv, gv):
        du, dg = _norm_bwd(uv, dhv, gv)
        return dxv + du, dg

    def norm_bwd_pair(uv, dhv, dxv, nv, g_pre, g_post):
        dx_, dg_pre = norm_bwd_resid(uv, dhv, dxv, g_pre)
        dn, dg_post = _norm_bwd(nv, dx_, g_post)
        return dx_, dn, dg_pre, dg_post

    df = None
    for l in reversed(range(DEPTH)):
        j = l // 2
        x_in, h, m, mix_saved, x1, h2, g, u, hid, f = saved[l]
        if df is None:
            df, grads["norm_ffn_post"][l] = _rows("norm_ffn_post_bwd", norm_bwd_cast, [f, dx],
                                                  [gain("norm_ffn_post", l)], [(d, bf16)], [((1, d), f32)])
        dg, du = _ffn_down_bwd(df, wts["ffn_w_down"], l, g, u, None)[0]
        gi, ll = _grad_slot("ffn_w_down", l)
        grads["ffn_w_down"][gi] = _ffn_wgrad_down(hid, df, grads["ffn_w_down"][gi], ll)
        dh2 = _ffn_up_bwd(dg, du, wts["ffn_w_gate"], wts["ffn_w_up"], l)
        grads["ffn_w_gate"][gi], grads["ffn_w_up"][gi] = _ffn_wgrad_up(
            h2, dg, du, grads["ffn_w_gate"][gi], grads["ffn_w_up"][gi], ll)
        dx1, dm, grads["norm_ffn_pre"][l], grads["norm_mix_post"][l] = _rows(
            "norm_ffn_mix_bwd", norm_bwd_pair, [x1, dh2, dx, m], [gain("norm_ffn_pre", l), gain("norm_mix_post", l)],
            [(d, f32), (d, bf16)], [((1, d), f32), ((1, d), f32)])
        dh = (_attn_bwd if l % 2 == 0 else _rg_bwd)(dm, h, mix_saved, wts, j, grads, exch)
        if l > 0:
            dx, df, grads["norm_mix_pre"][l], grads["norm_ffn_post"][l - 1] = _rows(
                "norm_mix_ffn_bwd", norm_bwd_pair, [x_in, dh, dx1, saved[l - 1][9]],
                [gain("norm_mix_pre", l), gain("norm_ffn_post", l - 1)],
                [(d, f32), (d, bf16)], [((1, d), f32), ((1, d), f32)])
        else:
            dx, grads["norm_mix_pre"][l] = _rows("norm_mix_pre_bwd", norm_bwd_resid, [x_in, dh, dx1],
                                                 [gain("norm_mix_pre", l)], [(d, f32)], [((1, d), f32)])
    return loss, dx, grads


ANY = pl.BlockSpec(memory_space=pl.ANY)
PACK_COLS = 1024
SMALL_ROWS = 288


def _mesh_pos():
    x, y, c = lax.axis_index("x"), lax.axis_index("y"), lax.axis_index("c")
    return x, y, c, [(1 - x, y), (x, 1 - y), (1 - x, 1 - y)]


def _run_copies(copies):
    for cp in copies:
        cp.start()
    for cp in copies:
        cp.wait()


GATHER_SEMS = 7


def _gather_copies(items, ins, outs, send, recv):
    x, y, c, chips = _mesh_pos()
    q = 2 * x + y
    sibling = (x, y, 1 - c)

    def copy(k, src, dst, to):
        return pltpu.make_async_remote_copy(src_ref=src, dst_ref=dst, send_sem=send.at[k], recv_sem=recv.at[k],
                                            device_id=to, device_id_type=MESH)

    own, sent, passed = [], [], []
    for i, (t, l0, nl) in enumerate(items):
        lay = pl.ds(l0, nl)
        half = ins[t].shape[1] // 2
        rows = pl.ds(pl.multiple_of(c * half, half), half)
        own.append(copy(GATHER_SEMS * i, ins[t].at[lay], outs[t].at[lay, q], sibling))
        for j, (px, py) in enumerate(chips):
            sent.append(copy(GATHER_SEMS * i + 1 + j, ins[t].at[lay, rows], outs[t].at[lay, q, rows], (px, py, c)))
            landed = outs[t].at[lay, 2 * px + py, rows]
            passed.append(copy(GATHER_SEMS * i + 4 + j, landed, landed, sibling))
    return own, sent, passed


def _gather_start(items, ins, outs, send, recv):
    own, sent, _ = _gather_copies(items, ins, outs, send, recv)
    for cp in own + sent:
        cp.start()


def _gather_finish(items, ins, outs, send, recv):
    own, sent, passed = _gather_copies(items, ins, outs, send, recv)
    for arrived, forward in zip(sent, passed):
        arrived.wait_recv()
        forward.start()
    for cp in sent:
        cp.wait_send()
    for cp in own + passed:
        cp.wait()


def _gather_call(items, shards):
    n = len(shards)
    nsem = GATHER_SEMS * len(items)

    def body(*refs):
        ins, outs = refs[:n], refs[n:2 * n]
        _gather_start(items, ins, outs, *refs[2 * n:])
        _gather_finish(items, ins, outs, *refs[2 * n:])

    return pl.pallas_call(
        body, name="weight_all_gather", in_specs=[ANY] * n, out_specs=[ANY] * n,
        out_shape=[SDS((s.shape[0], N_CHIPS) + s.shape[1:], s.dtype) for s in shards],
        scratch_shapes=[pltpu.SemaphoreType.DMA((nsem,)), pltpu.SemaphoreType.DMA((nsem,))])(*shards)


def _call(body, operands, *, name, grid, in_specs, out_specs, out_shape, sem, scratch=(), gather=None):
    if gather is None:
        return pl.pallas_call(body, grid=grid, in_specs=in_specs, out_specs=out_specs, out_shape=out_shape,
                              scratch_shapes=list(scratch), name=name, compiler_params=_cparams(sem))(*operands), None
    start, finish, c_ins, c_io, c_new, nsem = gather
    n_in, n_out, n_scr = len(operands), len(out_shape), len(scratch)
    ni, nio, nco = len(c_ins), len(c_io), len(c_io) + len(c_new)

    def full(*refs):
        ins, sh = refs[:n_in], refs[n_in:n_in + ni]
        outs = refs[n_in + ni + nio:n_in + ni + nio + n_out]
        co = refs[n_in + ni + nio + n_out:n_in + ni + nio + n_out + nco]
        scr = refs[n_in + ni + nio + n_out + nco:]
        ids = [pl.program_id(a) for a in range(len(grid))]
        first = functools.reduce(jnp.logical_and, [i == 0 for i in ids])
        last = functools.reduce(jnp.logical_and, [i == g - 1 for i, g in zip(ids, grid)])

        @pl.when(first)
        def _():
            start(sh, co, scr[n_scr], scr[n_scr + 1])

        body(*ins, *outs, *scr[:n_scr])

        @pl.when(last)
        def _():
            finish(sh, co, scr[n_scr], scr[n_scr + 1])

    res = pl.pallas_call(
        full, grid=grid, in_specs=list(in_specs) + [ANY] * (ni + nio), out_specs=list(out_specs) + [ANY] * nco,
        out_shape=list(out_shape) + [SDS(g.shape, g.dtype) for g in list(c_io) + list(c_new)],
        scratch_shapes=list(scratch) + [pltpu.SemaphoreType.DMA((nsem,)), pltpu.SemaphoreType.DMA((nsem,))],
        input_output_aliases={n_in + ni + t: n_out + t for t in range(nio)}, name=name,
        compiler_params=_cparams(("arbitrary",) * len(grid)))(*operands, *c_ins, *c_io)
    return res[:n_out], res[n_out:]


def _pair_exchange(gs):
    n = len(gs)

    def body(*refs):
        _pair_copies(refs[:n], refs[n:2 * n], *refs[2 * n:], start=True)
        _pair_copies(refs[:n], refs[n:2 * n], *refs[2 * n:], start=False)

    return pl.pallas_call(
        body, name="grad_pair_exchange", in_specs=[ANY] * n, out_specs=[ANY] * n,
        out_shape=_pair_shapes(gs),
        scratch_shapes=[pltpu.SemaphoreType.DMA((n,)), pltpu.SemaphoreType.DMA((n,))])(*gs)


def _pair_shapes(gs):
    return [SDS(g.shape[:2] + (g.shape[2] // 2, g.shape[3]), f32) for g in gs]


def _pair_copies(ins, outs, send, recv, start):
    x, y, c, _ = _mesh_pos()
    for t in range(len(ins)):
        half = ins[t].shape[2] // 2
        src = ins[t].at[:, :, pl.ds(pl.multiple_of((1 - c) * half, SUBLANES), half)]
        cp = pltpu.make_async_remote_copy(src_ref=src, dst_ref=outs[t], send_sem=send.at[t], recv_sem=recv.at[t],
                                          device_id=(x, y, 1 - c), device_id_type=MESH)
        cp.start() if start else cp.wait()


def _pair_carry(gs):
    return (functools.partial(_pair_copies, start=True), functools.partial(_pair_copies, start=False),
            gs, [], _pair_shapes(gs), len(gs))


def _pair_sum(name, g, got, c):
    l, s, r, cols = g.shape

    def body(c_ref, a_ref, b_ref, o_ref):
        o_ref[...] = (a_ref[...] + b_ref[...]).astype(bf16)

    blk = (None, None, r // 2, cols)
    return pl.pallas_call(
        body, name=name, out_shape=SDS(got.shape, bf16),
        grid_spec=pltpu.PrefetchScalarGridSpec(
            num_scalar_prefetch=1, grid=(l, s),
            in_specs=[pl.BlockSpec(blk, lambda i, q, c_ref: (i, q, c_ref[0], 0)),
                      pl.BlockSpec(blk, lambda i, q, c_ref: (i, q, 0, 0))],
            out_specs=pl.BlockSpec(blk, lambda i, q, c_ref: (i, q, 0, 0))),
        compiler_params=_cparams(("parallel", "parallel")))(c, g, got)


def _chip_exchange(hs):
    n = len(hs)

    def body(*refs):
        _chip_copies(refs[:n], refs[n:2 * n], *refs[2 * n:], start=True)
        _chip_copies(refs[:n], refs[n:2 * n], *refs[2 * n:], start=False)

    return pl.pallas_call(
        body, name="grad_chip_exchange", in_specs=[ANY] * n, out_specs=[ANY] * n,
        out_shape=[SDS(h.shape, h.dtype) for h in hs],
        scratch_shapes=[pltpu.SemaphoreType.DMA((3 * n,)), pltpu.SemaphoreType.DMA((3 * n,))])(*hs)


def _chip_copies(ins, outs, send, recv, start):
    x, y, c, chips = _mesh_pos()
    q = 2 * x + y
    for t in range(len(ins)):
        for j, (px, py) in enumerate(chips):
            cp = pltpu.make_async_remote_copy(
                src_ref=ins[t].at[:, 2 * px + py], dst_ref=outs[t].at[:, q], send_sem=send.at[3 * t + j],
                recv_sem=recv.at[3 * t + j], device_id=(px, py, c), device_id_type=MESH)
            cp.start() if start else cp.wait()


def _chip_carry(hs):
    return (functools.partial(_chip_copies, start=True), functools.partial(_chip_copies, start=False),
            hs, [], [SDS(h.shape, h.dtype) for h in hs], 3 * len(hs))


def _chip_sum(name, s, h, pos, l0, layers, into):
    l, _, r, cols = s.shape

    def body(pos_ref, s0, s1, s2, s3, own_ref, *rest):
        vals = [jnp.where(pos_ref[0] == p, own_ref[...], ref[...]).astype(f32) for p, ref in enumerate((s0, s1, s2, s3))]
        rest[-1][...] = ((vals[0] + vals[1]) + vals[2]) + vals[3]

    blk = (None, None, r, cols)
    slot = lambda p: pl.BlockSpec(blk, lambda i, pos_ref: (i, jnp.where(pos_ref[0] == p, (p + 1) % N_CHIPS, p), 0, 0))
    extra, alias = ([], {}) if into is None else ([into], {6: 0})
    return pl.pallas_call(
        body, name=name, out_shape=SDS((layers, 2 * r, cols), f32), input_output_aliases=alias,
        grid_spec=pltpu.PrefetchScalarGridSpec(
            num_scalar_prefetch=1, grid=(l,),
            in_specs=[slot(p) for p in range(N_CHIPS)] + [pl.BlockSpec(blk, lambda i, pos_ref: (i, pos_ref[0], 0, 0))]
            + [ANY] * len(extra),
            out_specs=pl.BlockSpec((None, r, cols), lambda i, pos_ref: (l0 + i, pos_ref[1], 0))),
        compiler_params=_cparams(("parallel",)))(pos, s, s, s, s, h, *extra)


def _pair_gather(fulls):
    n = len(fulls)

    def body(*refs):
        ins, outs = refs[:n], refs[n:2 * n]
        send, recv = refs[2 * n:]
        x, y, c, _ = _mesh_pos()
        copies = []
        for t in range(n):
            half = outs[t].shape[1] // 2
            rows = outs[t].at[:, pl.ds(pl.multiple_of(c * half, SUBLANES), half)]
            copies.append(pltpu.make_async_remote_copy(
                src_ref=rows, dst_ref=rows, send_sem=send.at[t], recv_sem=recv.at[t],
                device_id=(x, y, 1 - c), device_id_type=MESH))
        _run_copies(copies)

    return pl.pallas_call(
        body, name="grad_pair_gather", in_specs=[ANY] * n, out_specs=[ANY] * n,
        out_shape=[SDS(f.shape, f32) for f in fulls], input_output_aliases={t: t for t in range(n)},
        scratch_shapes=[pltpu.SemaphoreType.DMA((n,)), pltpu.SemaphoreType.DMA((n,))])(*fulls)


COL_SHARDED = ("attn_w_in", "rg_w_in", "ffn_w_gate", "ffn_w_up")
ROW_SHARDED = ("attn_w_out", "rg_w_out")
GATES = ("rg_w_a", "rg_w_i")
VECTORS = ("rg_conv_w", "rg_conv_b", "rg_b_a", "rg_b_i", "rg_lambda")
REPLICATED = ("norm_mix_pre", "norm_mix_post", "norm_ffn_pre", "norm_ffn_post", "attn_rel_bias")
BIG_GRADS = COL_SHARDED + ROW_SHARDED + ("ffn_w_down",)
SMALL_GRADS = GATES + VECTORS + REPLICATED
WEIGHTS =("attn_w_in", "attn_rel_bias", "attn_w_out", "rg_w_in", "rg_conv_w", "rg_conv_b", "rg_w_a", "rg_b_a",
           "rg_w_i", "rg_b_i", "rg_lambda", "rg_w_out", "norm_mix_pre", "norm_mix_post", "norm_ffn_pre",
           "norm_ffn_post", "ffn_w_gate", "ffn_w_up", "ffn_w_down")
SMALL = VECTORS + REPLICATED


GATHER_PARTS = {
    "first": (("attn_w_in", 0, 1), ("attn_w_out", 0, 1), ("rg_w_a", 0, 8), ("rg_w_i", 0, 8), ("vec", 0, 1)),
    "chunk_attn_fwd": (("ffn_w_gate", 0, 1), ("ffn_w_up", 0, 1), ("ffn_w_down", 0, 1), ("rg_w_in", 0, 1),
                       ("rg_w_out", 0, 1)),
    "sb_attn_fwd": (("ffn_w_gate", 1, 3), ("ffn_w_up", 1, 3), ("ffn_w_down", 1, 3)),
    "ffn_up": (("rg_w_in", 1, 1), ("rg_w_out", 1, 1), ("attn_w_in", 1, 1), ("attn_w_out", 1, 1)),
}


TRANSPOSED = ("ffn_w_gate", "ffn_w_up")


def _natural(name, a):
    return jnp.swapaxes(a, 1, 2) if name in TRANSPOSED else a


class _WeightGather:
    def __init__(self, w):
        self.w = w
        self.names = list(COL_SHARDED + ROW_SHARDED + GATES + ("ffn_w_down", "vec"))
        self.shards = {}
        for k in self.names[:-1]:
            a = _natural(k, w[k]).astype(bf16)
            self.shards[k] = a.reshape((-1,) + a.shape[-2:])
        self.shards["vec"] = jnp.concatenate([w[k].reshape(-1) for k in VECTORS]).reshape(1, -1, LANES)
        got = _gather_call(self._items("first", self.names), [self.shards[k] for k in self.names])
        self.raw = dict(zip(self.names, got))

    @staticmethod
    def _items(part, names):
        return [(names.index(k), l0, nl) for k, l0, nl in GATHER_PARTS[part]]

    def part(self, part):
        names = list(dict.fromkeys(k for k, _, _ in GATHER_PARTS[part]))
        items = self._items(part, names)
        return (functools.partial(_gather_start, items), functools.partial(_gather_finish, items),
                [self.shards[k] for k in names], [self.raw[k] for k in names], [], GATHER_SEMS * len(items)), names

    def views(self):
        got, w = self.raw, self.w
        out = {k: w[k] for k in REPLICATED}
        for k in COL_SHARDED + ("ffn_w_down",):
            out[k] = got[k]
        for k in ROW_SHARDED:
            l, s, ks, n = got[k].shape
            out[k] = got[k].reshape(l, 1, s * ks, n)
        for k in GATES:
            out[k] = got[k].reshape(2, LRU_BLOCKS, LRU_BW, LRU_BW)
        vec = got["vec"].reshape(N_CHIPS, -1)
        off = 0
        for k in VECTORS:
            shp = w[k].shape
            n = int(np.prod(shp))
            piece = vec[:, off:off + n].reshape((N_CHIPS,) + shp)
            off += n
            if k == "rg_conv_w":
                out[k] = piece.reshape(N_CHIPS, 2, 4, 256).transpose(1, 2, 0, 3).reshape(2, 4, D_MODEL)
            elif k in ("rg_b_a", "rg_b_i"):
                out[k] = piece.transpose(1, 2, 0, 3).reshape(2, 1, D_MODEL)
            else:
                out[k] = piece.transpose(1, 0, 2).reshape(2, 1, D_MODEL)
        return out


def _carried(plan, part, wts, fn, *args):
    if plan is None:
        return fn(*args, None)[0]
    gather, names = plan.part(part)
    out, new = fn(*args, gather)
    plan.raw.update(zip(names, new))
    wts.update(plan.views())
    return out


def _grad_blocks(name, g):
    st = jnp.stack([g[i] for i in sorted(g)])
    if name in GATES:
        st = st.reshape(2, LRU_BLOCKS, N_CHIPS, LRU_BW // N_CHIPS, LRU_BW).transpose(2, 0, 1, 3, 4)
    elif name == "rg_conv_w":
        st = st.reshape(2, 4, N_CHIPS, -1).transpose(2, 0, 1, 3)
    elif name in ("rg_b_a", "rg_b_i"):
        st = st.reshape(2, LRU_BLOCKS, N_CHIPS, -1).transpose(2, 0, 1, 3)
    elif name in VECTORS:
        st = st.reshape(2, N_CHIPS, -1).transpose(1, 0, 2)
    else:
        st = jnp.broadcast_to(st.reshape(1, -1), (N_CHIPS, st.size))
    return st.reshape(N_CHIPS, -1)


class _GradExchange:
    def __init__(self):
        self.c = lax.axis_index("c").astype(jnp.int32).reshape(1)
        self.pos = jnp.stack([2 * lax.axis_index("x") + lax.axis_index("y"), lax.axis_index("c")]).astype(jnp.int32)
        self.up = self.got_up = self.parts_up = self.slots_up = None

    @staticmethod
    def _blocked(g):
        if g.ndim == 3:
            g = g.reshape(g.shape[0], N_CHIPS, g.shape[1] // N_CHIPS, g.shape[2])
        return g

    def _sums(self, tag, names, gs, got):
        return [_pair_sum("grad_pair_sum_" + tag + k, g, r, self.c) for k, g, r in zip(names, gs, got)]

    def upper_carry(self, grads):
        self.up = [self._blocked(grads[k][1]) for k in BIG_GRADS]
        return _pair_carry(self.up)

    def upper_got(self, got):
        self.got_up = got

    def carry(self):
        self.parts_up = self._sums("up_", BIG_GRADS, self.up, self.got_up)
        return _chip_carry(self.parts_up)

    def carried(self, slots):
        self.slots_up = slots

    def finish(self, grads, shard_shapes):
        if self.got_up is None:
            self.upper_carry(grads)
            self.got_up = _pair_exchange(self.up)
        if self.slots_up is None:
            self.carry()
            self.slots_up = _chip_exchange(self.parts_up)
        blocks = [_grad_blocks(k, grads[k]) for k in SMALL_GRADS]
        used = sum(b.shape[1] for b in blocks)
        small = jnp.concatenate(blocks + [jnp.zeros((N_CHIPS, SMALL_ROWS * PACK_COLS - used), f32)], axis=1)
        names = tuple(k for k in BIG_GRADS if LOWER_LAYERS[k]) + ("small",)
        gs = [self._blocked(grads[k][0]) for k in names[:-1]] + [small.reshape(1, N_CHIPS, SMALL_ROWS, PACK_COLS)]
        parts = dict(zip(names, self._sums("lo_", names, gs, _pair_exchange(gs))))
        slots = dict(zip(names, _chip_exchange([parts[k] for k in names])))
        fulls = []
        for i, k in enumerate(BIG_GRADS):
            nlo, nup = LOWER_LAYERS[k], self.parts_up[i].shape[0]
            full = _chip_sum("grad_chip_sum_up_" + k, self.slots_up[i], self.parts_up[i], self.pos, nlo, nlo + nup, None)
            if nlo:
                full = _chip_sum("grad_chip_sum_lo_" + k, slots[k], parts[k], self.pos, 0, nlo + nup, full)
            fulls.append(full)
        fulls.append(_chip_sum("grad_chip_sum_lo_small", slots["small"], parts["small"], self.pos, 0, 1, None))
        full = _pair_gather(fulls)
        out = {k: f.reshape(shard_shapes[k]) for k, f in zip(BIG_GRADS, full)}
        flat, off = full[-1].reshape(-1), 0
        for k in SMALL_GRADS:
            n = int(np.prod(shard_shapes[k]))
            out[k] = flat[off:off + n].reshape(shard_shapes[k])
            off += n
        return out


def _adamw_fn(w, g, m, v):
    m = ADAM_B1 * m + (1.0 - ADAM_B1) * g
    v = ADAM_B2 * v + (1.0 - ADAM_B2) * (g * g)
    m_hat = m / (1.0 - ADAM_B1 ** ADAM_STEP)
    v_hat = v / (1.0 - ADAM_B2 ** ADAM_STEP)
    return -ADAM_LR * (m_hat / (jnp.sqrt(v_hat) + ADAM_EPS) + ADAM_WD * w), m, v


def _adamw(name, w, g, m, v):
    shp = w.shape
    if w.size >= 1 << 16:
        width = shp[-1]
        ops = [a.reshape(-1, width) for a in (w, g, m, v)]
        res = _rows(name, _adamw_fn, ops, [], [(width, f32)] * 3)
        return [r.reshape(shp) for r in res]
    n = w.size
    rows = -(-n // (SUBLANES * LANES)) * SUBLANES
    ops = [jnp.pad(a.reshape(-1), (0, rows * LANES - n)).reshape(rows, LANES) for a in (w, g, m, v)]
    res = _rows(name, _adamw_fn, ops, [], [(LANES, f32)] * 3, tr=rows)
    return [r.reshape(-1)[:n].reshape(shp) for r in res]


def kernel(x, attn_w_in, attn_rel_bias, attn_w_out, rg_w_in, rg_conv_w, rg_conv_b, rg_w_a, rg_b_a, rg_w_i, rg_b_i, rg_lambda, rg_w_out, norm_mix_pre, norm_mix_post, norm_ffn_pre, norm_ffn_post, ffn_w_gate, ffn_w_up, ffn_w_down, loss_target, m_attn_w_in, m_attn_rel_bias, m_attn_w_out, m_rg_w_in, m_rg_conv_w, m_rg_conv_b, m_rg_w_a, m_rg_b_a, m_rg_w_i, m_rg_b_i, m_rg_lambda, m_rg_w_out, m_norm_mix_pre, m_norm_mix_post, m_norm_ffn_pre, m_norm_ffn_post, m_ffn_w_gate, m_ffn_w_up, m_ffn_w_down, v_attn_w_in, v_attn_rel_bias, v_attn_w_out, v_rg_w_in, v_rg_conv_w, v_rg_conv_b, v_rg_w_a, v_rg_b_a, v_rg_w_i, v_rg_b_i, v_rg_lambda, v_rg_w_out, v_norm_mix_pre, v_norm_mix_post, v_norm_ffn_pre, v_norm_ffn_post, v_ffn_w_gate, v_ffn_w_up, v_ffn_w_down):
    w = dict(zip(WEIGHTS, (attn_w_in, attn_rel_bias, attn_w_out, rg_w_in, rg_conv_w, rg_conv_b, rg_w_a, rg_b_a, rg_w_i,
                           rg_b_i, rg_lambda, rg_w_out, norm_mix_pre, norm_mix_post, norm_ffn_pre, norm_ffn_post,
                           ffn_w_gate, ffn_w_up, ffn_w_down)))
    m = dict(zip(WEIGHTS, (m_attn_w_in, m_attn_rel_bias, m_attn_w_out, m_rg_w_in, m_rg_conv_w, m_rg_conv_b, m_rg_w_a,
                           m_rg_b_a, m_rg_w_i, m_rg_b_i, m_rg_lambda, m_rg_w_out, m_norm_mix_pre, m_norm_mix_post,
                           m_norm_ffn_pre, m_norm_ffn_post, m_ffn_w_gate, m_ffn_w_up, m_ffn_w_down)))
    v = dict(zip(WEIGHTS, (v_attn_w_in, v_attn_rel_bias, v_attn_w_out, v_rg_w_in, v_rg_conv_w, v_rg_conv_b, v_rg_w_a,
                           v_rg_b_a, v_rg_w_i, v_rg_b_i, v_rg_lambda, v_rg_w_out, v_norm_mix_pre, v_norm_mix_post,
                           v_norm_ffn_pre, v_norm_ffn_post, v_ffn_w_gate, v_ffn_w_up, v_ffn_w_down)))
    plan = _WeightGather(w)
    exch = _GradExchange()
    loss, dx, grads = _local_step(x[0], loss_target[0], plan.views(), plan, exch)
    loss = lax.psum(loss, ("x", "y", "c"))
    g = exch.finish(grads, {k: _natural(k, w[k]).shape for k in WEIGHTS})

    big = [k for k in WEIGHTS if k not in SMALL]
    upd = {}
    for k in big:
        res = _adamw("adamw_" + k, _natural(k, w[k]), g[k], _natural(k, m[k]), _natural(k, v[k]))
        upd[k] = [_natural(k, r) for r in res]
        g[k] = _natural(k, g[k])
    cat = lambda d: jnp.concatenate([d[k].reshape(-1) for k in SMALL])
    small = _adamw("adamw_small", cat(w), cat(g), cat(m), cat(v))
    off = 0
    for k in SMALL:
        n = w[k].size
        upd[k] = [r[off:off + n].reshape(w[k].shape) for r in small]
        off += n
    return (loss, dx[None], *[g[k] for k in WEIGHTS], *[upd[k][0] for k in WEIGHTS],
            *[upd[k][1] for k in WEIGHTS], *[upd[k][2] for k in WEIGHTS])
```

```python
import functools

import numpy as np
import jax
import jax.numpy as jnp
from jax import lax
from jax.experimental import pallas as pl
from jax.experimental.pallas import tpu as pltpu

f32 = jnp.float32
bf16 = jnp.bfloat16
SDS = jax.ShapeDtypeStruct
MESH = pl.DeviceIdType.MESH

D_MODEL = 1024
N_CHIPS = 4
DEPTH = 4
HEAD_DIM = 64
CHUNK = 64
N_LEFT = 8
REL_CLIP = 256
A_W = 512
LRU_BLOCKS = 4
LRU_BW = 256
LRU_C = 8.0
D_FF = 2816
RMS_EPS = 1e-6
LANES = 128
SUBLANES = 8
VMEM_LIMIT = 56 * 1024 * 1024

QB_A = 2 * CHUNK
QSUB_A = 4
KW_A = QB_A + N_LEFT * CHUNK
PAD_A = N_LEFT * CHUNK
EXT_A = 768
SB_BLK = 256
QSUB_B = 4
SB_DEAD = -110.0

ADAM_LR, ADAM_B1, ADAM_B2, ADAM_EPS, ADAM_WD, ADAM_STEP = 0.001, 0.9, 0.999, 1e-08, 0.01, 10


def _cparams(sem):
    return pltpu.CompilerParams(dimension_semantics=sem, vmem_limit_bytes=VMEM_LIMIT)


def _gemm(name, operands, in_specs, o_spec, out_shape, grid, dims, acc_shape, into=None):
    nred = grid[2]
    npair = len(operands) // 2
    nin = 2 * npair + (into is not None)

    def body(*refs):
        o_ref = refs[nin]
        p = None
        for t in range(npair):
            d = lax.dot_general(refs[2 * t][...], refs[2 * t + 1][...], (dims, ((), ())),
                                preferred_element_type=f32)
            p = d if p is None else p + d
        if nred == 1:
            o_ref[...] = p.astype(o_ref.dtype)
        else:
            acc = refs[nin + 1]
            r = pl.program_id(2)

            @pl.when(r == 0)
            def _():
                acc[...] = p

            @pl.when(r > 0)
            def _():
                acc[...] += p

            @pl.when(r == nred - 1)
            def _():
                o_ref[...] = acc[...].astype(o_ref.dtype)

    scratch = [] if nred == 1 else [pltpu.VMEM(acc_shape, f32)]
    extra, alias = ([], {}) if into is None else ([into], {2 * npair: 0})
    return pl.pallas_call(
        body, grid=grid, in_specs=list(in_specs) + [pl.BlockSpec(memory_space=pl.ANY)] * len(extra),
        out_specs=o_spec, out_shape=out_shape, scratch_shapes=scratch, name=name, input_output_aliases=alias,
        compiler_params=_cparams(("parallel", "parallel", "arbitrary")))(*operands, *extra)


LOWER_LAYERS = {"attn_w_in": 1, "attn_w_out": 1, "rg_w_in": 0, "rg_w_out": 0,
                "ffn_w_gate": 0, "ffn_w_up": 0, "ffn_w_down": 0}


def _grad_slot(name, l):
    n = LOWER_LAYERS[name]
    return (0, l) if l < n else (1, l - n)


class _Fresh:
    def __init__(self, shape):
        self.shape = tuple(shape)


def _into(buf):
    return None if isinstance(buf, _Fresh) else buf


NN = ((1,), (0,))
NT = ((1,), (1,))
TN = ((0,), (0,))


WGRAD_TOKENS = 2048


def _tile(t, want=1024):
    return min(want, t)


def _mm_cols(name, a, w, l, out_dtype):
    t, k = a.shape
    _, s, _, ns = w.shape
    tm = _tile(t, 2048)
    return _gemm(
        name, [a, w],
        [pl.BlockSpec((tm, k), lambda i, j, r: (i, 0)),
         pl.BlockSpec((None, None, k, ns), lambda i, j, r: (l, j, 0, 0))],
        pl.BlockSpec((tm, ns), lambda i, j, r: (i, j)),
        SDS((t, s * ns), out_dtype), (t // tm, s, 1), NN, None)


def _mm_cols_t(name, dy, w, l, out_dtype):
    t = dy.shape[0]
    _, s, k, ns = w.shape
    tm = _tile(t, 2048)
    return _gemm(
        name, [dy, w],
        [pl.BlockSpec((tm, ns), lambda i, j, r: (i, r)),
         pl.BlockSpec((None, None, k, ns), lambda i, j, r: (l, r, 0, 0))],
        pl.BlockSpec((tm, k), lambda i, j, r: (i, 0)),
        SDS((t, k), out_dtype), (t // tm, 1, s), NT, (tm, k))


def _mm_wgrad_cols(name, a, dy, buf, l):
    t, k = a.shape
    _, s, _, ns = buf.shape
    tt = _tile(t, 2 * WGRAD_TOKENS)
    return _gemm(
        name, [a, dy],
        [pl.BlockSpec((tt, k), lambda i, j, r: (r, 0)),
         pl.BlockSpec((tt, ns), lambda i, j, r: (r, i))],
        pl.BlockSpec((None, None, k, ns), lambda i, j, r: (l, i, 0, 0)),
        SDS(buf.shape, f32), (s, 1, t // tt), TN, (k, ns), into=_into(buf))


def _mm_rows(name, parts, w, l, out_dtype):
    t = parts[0].shape[0]
    n = w.shape[3]
    tm = _tile(t)
    ops, specs = [], []
    for p_i, a in enumerate(parts):
        kp = a.shape[1]
        ops += [a, w]
        specs += [pl.BlockSpec((tm, kp), lambda i, j, r: (i, 0)),
                  pl.BlockSpec((None, None, kp, n), lambda i, j, r, p_i=p_i: (l, 0, p_i, 0))]
    return _gemm(name, ops, specs, pl.BlockSpec((tm, n), lambda i, j, r: (i, 0)),
                 SDS((t, n), out_dtype), (t // tm, 1, 1), NN, None)


def _mm_rows_t(name, dy, w, l, out_dtype):
    t, n = dy.shape
    k = w.shape[2]
    tm = _tile(t)
    return _gemm(
        name, [dy, w],
        [pl.BlockSpec((tm, n), lambda i, j, r: (i, 0)),
         pl.BlockSpec((None, None, k, n), lambda i, j, r: (l, 0, 0, 0))],
        pl.BlockSpec((tm, k), lambda i, j, r: (i, 0)),
        SDS((t, k), out_dtype), (t // tm, 1, 1), NT, None)


def _mm_wgrad(name, a, dy, buf, l, part=0):
    t, k = a.shape
    n = dy.shape[1]
    tt = _tile(t, 2 * WGRAD_TOKENS)
    return _gemm(
        name, [a, dy],
        [pl.BlockSpec((tt, k), lambda i, j, r: (r, 0)),
         pl.BlockSpec((tt, n), lambda i, j, r: (r, 0))],
        pl.BlockSpec((None, k, n), lambda i, j, r: (l, part, 0)),
        SDS(buf.shape, f32), (1, 1, t // tt), TN, (k, n), into=_into(buf))


def _ffn_up(h, wg, wu, l, gather):
    t, k = h.shape
    s, fs = wg.shape[1], wg.shape[2]
    tm = _tile(t)

    def body(h_ref, wg_ref, wu_ref, g_ref, u_ref, hid_ref):
        hv = h_ref[...]
        g = lax.dot_general(hv, wg_ref[...], (NT, ((), ())), preferred_element_type=f32)
        u = lax.dot_general(hv, wu_ref[...], (NT, ((), ())), preferred_element_type=f32)
        g_ref[...] = g.astype(bf16)
        u_ref[...] = u.astype(bf16)
        hid_ref[...] = (g * jax.nn.sigmoid(g) * u).astype(bf16)

    wspec = pl.BlockSpec((None, None, fs, k), lambda j, i: (l, j, 0, 0))
    ospec = pl.BlockSpec((None, tm, fs), lambda j, i: (j, i, 0))
    return _call(
        body, [h, wg, wu], grid=(s, t // tm), name="ffn_up",
        in_specs=[pl.BlockSpec((tm, k), lambda j, i: (i, 0)), wspec, wspec],
        out_specs=[ospec, ospec, ospec], out_shape=[SDS((s, t, fs), bf16)] * 3,
        sem=("parallel", "parallel"), gather=gather)


def _ffn_down(hid, wd, l):
    s, t, fs = hid.shape
    n = wd.shape[3]
    tm = _tile(t, 512)
    ops, specs = [], []
    for r in range(s):
        ops += [hid, wd]
        specs += [pl.BlockSpec((None, tm, fs), lambda i, j, k, r=r: (r, i, 0)),
                  pl.BlockSpec((None, None, fs, n), lambda i, j, k, r=r: (l, r, 0, 0))]
    return _gemm("ffn_down", ops, specs, pl.BlockSpec((tm, n), lambda i, j, k: (i, 0)),
                 SDS((t, n), f32), (t // tm, 1, 1), NN, None)


def _ffn_down_bwd(df, wd, l, g, u, gather):
    t, n = df.shape
    s, fs = wd.shape[1], wd.shape[2]
    tm = _tile(t)

    def body(df_ref, wd_ref, g_ref, u_ref, dg_ref, du_ref):
        dh = lax.dot_general(df_ref[...], wd_ref[...], (NT, ((), ())), preferred_element_type=f32)
        gv = g_ref[...].astype(f32)
        uv = u_ref[...].astype(f32)
        sg = jax.nn.sigmoid(gv)
        du_ref[...] = (dh * gv * sg).astype(bf16)
        dg_ref[...] = (dh * uv * (sg * (1.0 + gv * (1.0 - sg)))).astype(bf16)

    bspec = pl.BlockSpec((None, tm, fs), lambda j, i: (j, i, 0))
    return _call(
        body, [df, wd, g, u], grid=(s, t // tm), name="ffn_down_bwd",
        in_specs=[pl.BlockSpec((tm, n), lambda j, i: (i, 0)),
                  pl.BlockSpec((None, None, fs, n), lambda j, i: (l, j, 0, 0)), bspec, bspec],
        out_specs=[bspec, bspec], out_shape=[SDS((s, t, fs), bf16)] * 2,
        sem=("parallel", "parallel"), gather=gather)


def _ffn_up_bwd(dg, du, wg, wu, l):
    s, t, fs = dg.shape
    k = wg.shape[3]
    tm = _tile(t, 512)
    ops, specs = [], []
    for r in range(s):
        aspec = pl.BlockSpec((None, tm, fs), lambda i, j, kk, r=r: (r, i, 0))
        wspec = pl.BlockSpec((None, None, fs, k), lambda i, j, kk, r=r: (l, r, 0, 0))
        ops += [dg, wg, du, wu]
        specs += [aspec, wspec, aspec, wspec]
    return _gemm("ffn_up_bwd", ops, specs, pl.BlockSpec((tm, k), lambda i, j, kk: (i, 0)),
                 SDS((t, k), f32), (t // tm, 1, 1), NN, None)


def _ffn_wgrad_up(h, dg, du, buf_g, buf_u, l):
    t, k = h.shape
    s, _, fs = dg.shape
    tt = _tile(t, WGRAD_TOKENS)
    nred = t // tt

    fresh = isinstance(buf_g, _Fresh)

    def body(*refs):
        h_ref, dg_ref, du_ref = refs[:3]
        og_ref, ou_ref, acc_g, acc_u = refs[-4:]
        r = pl.program_id(1)
        hv = h_ref[...]
        pg = lax.dot_general(dg_ref[...], hv, (TN, ((), ())), preferred_element_type=f32)
        pu = lax.dot_general(du_ref[...], hv, (TN, ((), ())), preferred_element_type=f32)

        @pl.when(r == 0)
        def _():
            acc_g[...] = pg
            acc_u[...] = pu

        @pl.when(r > 0)
        def _():
            acc_g[...] += pg
            acc_u[...] += pu

        @pl.when(r == nred - 1)
        def _():
            og_ref[...] = acc_g[...]
            ou_ref[...] = acc_u[...]

    dspec = pl.BlockSpec((None, tt, fs), lambda i, r: (i, r, 0))
    ospec = pl.BlockSpec((None, None, fs, k), lambda i, r: (l, i, 0, 0))
    extra, alias = ([], {}) if fresh else ([buf_g, buf_u], {3: 0, 4: 1})
    return pl.pallas_call(
        body, grid=(s, nred), name="ffn_wgrad_up",
        in_specs=[pl.BlockSpec((tt, k), lambda i, r: (r, 0)), dspec, dspec] + [ANY] * len(extra),
        out_specs=[ospec, ospec], out_shape=[SDS(buf_g.shape, f32), SDS(buf_u.shape, f32)],
        scratch_shapes=[pltpu.VMEM((fs, k), f32)] * 2, input_output_aliases=alias,
        compiler_params=_cparams(("parallel", "arbitrary")))(h, dg, du, *extra)


def _ffn_wgrad_down(hid, df, buf, l):
    s, t, fs = hid.shape
    n = df.shape[1]
    tt = _tile(t, 2 * WGRAD_TOKENS)
    return _gemm(
        "ffn_wgrad_down", [hid, df],
        [pl.BlockSpec((None, tt, fs), lambda i, j, r: (i, r, 0)),
         pl.BlockSpec((tt, n), lambda i, j, r: (r, 0))],
        pl.BlockSpec((None, None, fs, n), lambda i, j, r: (l, i, 0, 0)),
        SDS(buf.shape, f32), (s, 1, t // tt), TN, (fs, n), into=_into(buf))


def _rows(name, fn, rows, consts, row_outs, acc_outs=(), tr=512):
    rows = [r if isinstance(r, tuple) else (r, r.shape[1], 0) for r in rows]
    t = rows[0][0].shape[0]
    tr = max(d for d in range(SUBLANES, min(tr, t) + 1, SUBLANES) if t % d == 0)
    nin = len(rows) + len(consts)
    no, na = len(row_outs), len(acc_outs)

    def body(*refs):
        vals = fn(*[r[...] for r in refs[:nin]])
        if not isinstance(vals, (tuple, list)):
            vals = (vals,)
        for k in range(no):
            refs[nin + k][...] = vals[k].astype(refs[nin + k].dtype)
        first = pl.program_id(0) == 0
        for k in range(na):
            ref, val = refs[nin + no + k], vals[no + k]

            @pl.when(first)
            def _(ref=ref, val=val):
                ref[...] = val

            @pl.when(jnp.logical_not(first))
            def _(ref=ref, val=val):
                ref[...] += val

    in_specs = [pl.BlockSpec((tr, w), lambda i, cb=cb: (i, cb)) for (_, w, cb) in rows]
    in_specs += [pl.BlockSpec(c.shape, lambda i, nd=c.ndim: (0,) * nd) for c in consts]
    out_specs = [pl.BlockSpec((tr, w), lambda i: (i, 0)) for (w, _) in row_outs]
    out_specs += [pl.BlockSpec(s, lambda i, nd=len(s): (0,) * nd) for (s, _) in acc_outs]
    out_shape = [SDS((t, w), dt) for (w, dt) in row_outs] + [SDS(s, dt) for (s, dt) in acc_outs]
    res = pl.pallas_call(
        body, grid=(t // tr,), in_specs=in_specs, out_specs=out_specs, out_shape=out_shape,
        name=name, compiler_params=_cparams(("arbitrary",)))(*[r[0] for r in rows], *consts)
    return res


def _rstd(x):
    return lax.rsqrt(jnp.mean(x * x, axis=-1, keepdims=True) + RMS_EPS)


def _norm_fwd(x, g):
    return x * _rstd(x) * g


def _norm_bwd(u, dy, g):
    r = _rstd(u)
    n = u * r
    dn = dy * g
    du = r * (dn - n * jnp.mean(dn * n, axis=-1, keepdims=True))
    return du, jnp.sum(dy * n, axis=0, keepdims=True)


def _gelu(x):
    c = 0.7978845608028654
    return 0.5 * x * (1.0 + jnp.tanh(c * (x + 0.044715 * x * x * x)))


def _gelu_grad(x):
    c = 0.7978845608028654
    th = jnp.tanh(c * (x + 0.044715 * x * x * x))
    return 0.5 * (1.0 + th) + 0.5 * x * (1.0 - th * th) * c * (1.0 + 3.0 * 0.044715 * x * x)


def _mask_heads(x):
    lane = lax.broadcasted_iota(jnp.int32, x.shape, 1)
    return [jnp.where((lane >= h * HEAD_DIM) & (lane < (h + 1) * HEAD_DIM), x, jnp.zeros_like(x))
            for h in range(LANES // HEAD_DIM)]


def _chunk_valid(start):
    qi = lax.broadcasted_iota(jnp.int32, (QB_A, KW_A), 0)
    kj = lax.broadcasted_iota(jnp.int32, (QB_A, KW_A), 1)
    qc = qi // CHUNK
    kc = kj // CHUNK
    return (kc >= qc) & (kc <= qc + N_LEFT) & (kj + start >= PAD_A)


def _chunk_probs(q, k, bias, valid):
    s = lax.dot_general(q, k, (NT, ((), ())), preferred_element_type=f32) * (HEAD_DIM ** -0.5) + bias
    s = jnp.where(valid, s, -1e30)
    p = jnp.exp(s - jnp.max(s, axis=-1, keepdims=True))
    return p / jnp.sum(p, axis=-1, keepdims=True)


def _chunk_attn_fwd(proj, kpad, vpad, bias, gather):
    t = proj.shape[0]
    tp = kpad.shape[0]
    step = QSUB_A * QB_A

    def body(q_ref, k_ref, v_ref, b_ref, o_ref):
        for sb in range(QSUB_A):
            start = pl.multiple_of((pl.program_id(1) * QSUB_A + sb) * QB_A, QB_A)
            rows = pl.ds(sb * QB_A, QB_A)
            valid = _chunk_valid(start)
            kw = k_ref[pl.ds(start, KW_A), :]
            qm = _mask_heads(q_ref[rows, :])
            vm = _mask_heads(v_ref[pl.ds(start, KW_A), :])
            o = None
            for h in range(len(qm)):
                p = _chunk_probs(qm[h], kw, b_ref[h], valid)
                d = jnp.dot(p.astype(bf16), vm[h], preferred_element_type=f32)
                o = d if o is None else o + d
            o_ref[rows, :] = o.astype(bf16)

    kv_spec = pl.BlockSpec((tp, LANES), lambda hp, qb: (0, hp))
    outs, new = _call(
        body, [proj, kpad, vpad, bias], grid=(A_W // LANES, t // step), name="chunk_attn_fwd",
        in_specs=[pl.BlockSpec((step, LANES), lambda hp, qb: (qb, hp)), kv_spec, kv_spec,
                  pl.BlockSpec((2, QB_A, KW_A), lambda hp, qb: (hp, 0, 0))],
        out_specs=[pl.BlockSpec((step, LANES), lambda hp, qb: (qb, hp))],
        out_shape=[SDS((t, A_W), bf16)], sem=("parallel", "arbitrary"), gather=gather)
    return outs[0], new


def _chunk_attn_bwd(proj, kpad, vpad, bias, dout, gather):
    t = proj.shape[0]
    tp = kpad.shape[0]
    step = QSUB_A * QB_A

    def body(q_ref, k_ref, v_ref, b_ref, do_ref, dq_ref, dk_ref, dv_ref, db_ref):
        qb = pl.program_id(1)

        @pl.when(qb == 0)
        def _():
            dk_ref[...] = jnp.zeros_like(dk_ref)
            dv_ref[...] = jnp.zeros_like(dv_ref)
            db_ref[...] = jnp.zeros_like(db_ref)

        for sb in range(QSUB_A):
            start = pl.multiple_of((qb * QSUB_A + sb) * QB_A, QB_A)
            rows = pl.ds(sb * QB_A, QB_A)
            win = pl.ds(start, KW_A)
            valid = _chunk_valid(start)
            kw = k_ref[win, :]
            vw = v_ref[win, :]
            qm = _mask_heads(q_ref[rows, :])
            dom = _mask_heads(do_ref[rows, :])
            km = _mask_heads(kw)
            dq = dk = dv = None
            for h in range(len(qm)):
                p = _chunk_probs(qm[h], kw, b_ref[h], valid)
                dp = lax.dot_general(dom[h], vw, (NT, ((), ())), preferred_element_type=f32)
                ds = p * (dp - jnp.sum(dp * p, axis=-1, keepdims=True))
                db_ref[h] += ds
                dsb = (ds * (HEAD_DIM ** -0.5)).astype(bf16)
                terms = (jnp.dot(dsb, km[h], preferred_element_type=f32),
                         lax.dot_general(dsb, qm[h], (TN, ((), ())), preferred_element_type=f32),
                         lax.dot_general(p.astype(bf16), dom[h], (TN, ((), ())), preferred_element_type=f32))
                dq, dk, dv = terms if dq is None else (dq + terms[0], dk + terms[1], dv + terms[2])
            dq_ref[rows, :] = dq.astype(bf16)
            dk_ref[win, :] += dk
            dv_ref[win, :] += dv

    kv_spec = pl.BlockSpec((tp, LANES), lambda hp, qb: (0, hp))
    q_spec = pl.BlockSpec((step, LANES), lambda hp, qb: (qb, hp))
    b_spec = pl.BlockSpec((2, QB_A, KW_A), lambda hp, qb: (hp, 0, 0))
    return _call(
        body, [proj, kpad, vpad, bias, dout], grid=(A_W // LANES, t // step), name="chunk_attn_bwd",
        in_specs=[q_spec, kv_spec, kv_spec, b_spec, q_spec],
        out_specs=[q_spec, kv_spec, kv_spec, b_spec],
        out_shape=[SDS((t, A_W), bf16), SDS((tp, A_W), f32), SDS((tp, A_W), f32),
                   SDS((2 * A_W // LANES, QB_A, KW_A), f32)],
        sem=("parallel", "arbitrary"), gather=gather)


def _bias_ext(table):
    flat = PAD_A + QB_A - 1 - REL_CLIP
    top = jnp.broadcast_to(table[:, 2 * REL_CLIP:], (table.shape[0], flat))
    lo = 2 * REL_CLIP - (EXT_A - 1 - flat)
    return jnp.concatenate([top, jnp.flip(table[:, lo:], axis=1)], axis=1)


def _bias_window(table):
    nh = table.shape[0]
    e = jnp.broadcast_to(_bias_ext(table)[:, None, :], (nh, QB_A, EXT_A)).reshape(nh, QB_A * EXT_A)
    m = e[:, :QB_A * (EXT_A - 1)].reshape(nh, QB_A, EXT_A - 1)
    return m[:, :, QB_A - 1:]


def _bias_window_grad(dbias):
    nh = dbias.shape[0]
    m = jnp.pad(dbias, ((0, 0), (0, 0), (QB_A - 1, 0))).reshape(nh, QB_A * (EXT_A - 1))
    dext = jnp.sum(jnp.pad(m, ((0, 0), (0, QB_A))).reshape(nh, QB_A, EXT_A), axis=1)
    flat = PAD_A + QB_A - 1 - REL_CLIP
    lo = 2 * REL_CLIP - (EXT_A - 1 - flat)
    tail = jnp.flip(dext[:, flat:], axis=1)
    tail = tail.at[:, -1].add(jnp.sum(dext[:, :flat], axis=1))
    return jnp.pad(tail, ((0, 0), (lo, 0)))


def _tri_suffix(x, tri):
    hi = x.astype(bf16)
    lo = (x - hi.astype(f32)).astype(bf16)
    return jnp.dot(hi, tri, preferred_element_type=f32) + jnp.dot(lo, tri, preferred_element_type=f32)


def _sb_block(q, k, run, tri, causal):
    z = lax.dot_general(q, k, (NT, ((), ())), preferred_element_type=f32) * (HEAD_DIM ** -0.5)
    e = jnp.exp(-jnp.abs(z))
    l1p = jnp.log(1.0 + e)
    lb = jnp.minimum(z, 0.0) - l1p
    lmb = lb - z
    if causal is not None:
        lmb = jnp.where(causal, lmb, 0.0)
    cs = _tri_suffix(lmb, tri)
    w = jnp.exp(lb + (run + cs - lmb))
    if causal is not None:
        w = jnp.where(causal, w, 0.0)
    return z, e, w, run + cs[:, 0:1]


def _sb_tri():
    r = lax.broadcasted_iota(jnp.int32, (SB_BLK, SB_BLK), 0)
    c = lax.broadcasted_iota(jnp.int32, (SB_BLK, SB_BLK), 1)
    return (r >= c).astype(bf16), c < r


def _sb_live(runs):
    m = runs[0]
    for r in runs[1:]:
        m = jnp.maximum(m, r)
    return jnp.max(m) > SB_DEAD


def _sb_fwd(proj, gather):
    t = proj.shape[0]
    cb = A_W // LANES
    nh = LANES // HEAD_DIM

    step_rows = QSUB_B * SB_BLK

    def body(q_ref, k_ref, v_ref, o_ref, of_ref):
        tri, diag = _sb_tri()
        for sb in range(QSUB_B):
            _sb_fwd_block(pl.program_id(1) * QSUB_B + sb, pl.ds(sb * SB_BLK, SB_BLK), tri, diag,
                          q_ref, k_ref, v_ref, o_ref, of_ref)

    def _sb_fwd_block(qb, qrows, tri, diag, q_ref, k_ref, v_ref, o_ref, of_ref):
        qm = _mask_heads(q_ref[qrows, :])

        def pair(kb, carry, causal):
            rows = pl.ds(pl.multiple_of(kb * SB_BLK, SB_BLK), SB_BLK)
            k = k_ref[rows, :]
            vm = _mask_heads(v_ref[rows, :])
            runs, acc = [], carry[nh]
            for h in range(nh):
                _, _, w, run = _sb_block(qm[h], k, carry[h], tri, causal)
                acc = acc + jnp.dot(w.astype(bf16), vm[h], preferred_element_type=f32)
                runs.append(run)
            return (*runs, acc)

        zero = jnp.zeros((SB_BLK, 1), f32)
        carry = pair(qb, (zero,) * nh + (jnp.zeros((SB_BLK, LANES), f32),), diag)

        def cond(st):
            return (st[0] < qb) & _sb_live(st[1][:nh])

        def step(st):
            return st[0] + 1, pair(qb - 1 - st[0], st[1], None)

        _, carry = lax.while_loop(cond, step, (jnp.int32(0), carry))
        o_ref[qrows, :] = carry[nh].astype(bf16)
        of_ref[qrows, :] = carry[nh]

    ospec = pl.BlockSpec((step_rows, LANES), lambda hp, qb: (qb, hp))
    return _call(
        body, [proj, proj, proj], grid=(cb, t // step_rows), name="sb_attn_fwd",
        in_specs=[pl.BlockSpec((step_rows, LANES), lambda hp, qb: (qb, 3 * cb + hp)),
                  pl.BlockSpec((t, LANES), lambda hp, qb: (0, 4 * cb + hp)),
                  pl.BlockSpec((t, LANES), lambda hp, qb: (0, 5 * cb + hp))],
        out_specs=[ospec, ospec], out_shape=[SDS((t, A_W), bf16), SDS((t, A_W), f32)],
        sem=("parallel", "arbitrary"), gather=gather)


def _sb_bwd(proj, out_b, dout, gather):
    t = proj.shape[0]
    cb = A_W // LANES
    nh = LANES // HEAD_DIM

    step_rows = QSUB_B * SB_BLK

    def body(q_ref, k_ref, v_ref, o_ref, do_ref, dq_ref, dk_ref, dv_ref):
        tri, diag = _sb_tri()

        @pl.when(pl.program_id(1) == 0)
        def _():
            dk_ref[...] = jnp.zeros_like(dk_ref)
            dv_ref[...] = jnp.zeros_like(dv_ref)

        for sb in range(QSUB_B):
            _sb_bwd_block(pl.program_id(1) * QSUB_B + sb, pl.ds(sb * SB_BLK, SB_BLK), tri, diag,
                          q_ref, k_ref, v_ref, o_ref, do_ref, dq_ref, dk_ref, dv_ref)

    def _sb_bwd_block(qb, qrows, tri, diag, q_ref, k_ref, v_ref, o_ref, do_ref, dq_ref, dk_ref, dv_ref):
        qm = _mask_heads(q_ref[qrows, :])
        do = do_ref[qrows, :]
        dom = _mask_heads(do)
        dsums = [jnp.sum(t_, axis=-1, keepdims=True) for t_ in _mask_heads(do.astype(f32) * o_ref[qrows, :])]

        def pair(kb, carry, causal):
            rows = pl.ds(pl.multiple_of(kb * SB_BLK, SB_BLK), SB_BLK)
            k = k_ref[rows, :]
            v = v_ref[rows, :]
            km = _mask_heads(k)
            new, dq, dk, dv = [], carry[2 * nh], None, None
            for h in range(nh):
                z, e, w, run = _sb_block(qm[h], k, carry[2 * h], tri, causal)
                inv = 1.0 / (1.0 + e)
                beta = jnp.where(z >= 0.0, inv, e * inv)
                wb = w.astype(bf16)
                g = lax.dot_general(dom[h], v, (NT, ((), ())), preferred_element_type=f32) * wb.astype(f32)
                sg = _tri_suffix(g, tri)
                dz = g * (1.0 - beta) - (dsums[h] - carry[2 * h + 1] - sg) * beta
                if causal is not None:
                    dz = jnp.where(causal, dz, 0.0)
                dzb = (dz * (HEAD_DIM ** -0.5)).astype(bf16)
                dq = dq + jnp.dot(dzb, km[h], preferred_element_type=f32)
                tk = lax.dot_general(dzb, qm[h], (TN, ((), ())), preferred_element_type=f32)
                tv = lax.dot_general(wb, dom[h], (TN, ((), ())), preferred_element_type=f32)
                dk, dv = (tk, tv) if dk is None else (dk + tk, dv + tv)
                new += [run, carry[2 * h + 1] + sg[:, 0:1]]
            dk_ref[rows, :] += dk
            dv_ref[rows, :] += dv
            return (*new, dq)

        zero = jnp.zeros((SB_BLK, 1), f32)
        carry = pair(qb, (zero,) * (2 * nh) + (jnp.zeros((SB_BLK, LANES), f32),), diag)

        def cond(st):
            return (st[0] < qb) & _sb_live(st[1][0:2 * nh:2])

        def step(st):
            return st[0] + 1, pair(qb - 1 - st[0], st[1], None)

        _, carry = lax.while_loop(cond, step, (jnp.int32(0), carry))
        dq_ref[qrows, :] = carry[2 * nh].astype(bf16)

    kv_in = lambda seg: pl.BlockSpec((t, LANES), lambda hp, qb: (0, seg * cb + hp))
    q_spec = pl.BlockSpec((step_rows, LANES), lambda hp, qb: (qb, hp))
    kv_out = pl.BlockSpec((t, LANES), lambda hp, qb: (0, hp))
    return _call(
        body, [proj, proj, proj, out_b, dout], grid=(cb, t // step_rows), name="sb_attn_bwd",
        in_specs=[pl.BlockSpec((step_rows, LANES), lambda hp, qb: (qb, 3 * cb + hp)), kv_in(4), kv_in(5),
                  q_spec, pl.BlockSpec((step_rows, LANES), lambda hp, qb: (qb, cb + hp))],
        out_specs=[q_spec, kv_out, kv_out],
        out_shape=[SDS((t, A_W), bf16), SDS((t, A_W), f32), SDS((t, A_W), f32)],
        sem=("parallel", "arbitrary"), gather=gather)


def _halo_specs(tr, w, col, nblk):
    per = tr // SUBLANES
    cur = pl.BlockSpec((tr, w), lambda i: (i, col))
    prev = pl.BlockSpec((SUBLANES, w), lambda i: (jnp.maximum(i * per - 1, 0), col))
    nxt = pl.BlockSpec((SUBLANES, w), lambda i: (jnp.minimum((i + 1) * per, nblk * per - 1), col))
    return cur, prev, nxt


def _taps_before(cur, prev8, first):
    prev8 = jnp.where(first, 0.0, prev8)
    ext = jnp.concatenate([prev8, cur], axis=0)
    return [pltpu.roll(ext, s, 0)[SUBLANES:] for s in (3, 2, 1)]


def _taps_after(cur, next8, last):
    n = cur.shape[0]
    next8 = jnp.where(last, 0.0, next8)
    ext = jnp.concatenate([cur, next8], axis=0)
    return [pltpu.roll(ext, n + SUBLANES - s, 0)[:n] for s in (1, 2, 3)]


def _block_diag(x, w_ref, dims):
    outs = [lax.dot_general(x[:, n * LRU_BW:(n + 1) * LRU_BW], w_ref[n], (dims, ((), ())),
                            preferred_element_type=f32) for n in range(LRU_BLOCKS)]
    return jnp.concatenate(outs, axis=1)


def _lru_gates(xc, wa_ref, wi_ref, ba, bi, lam):
    xb = xc.astype(bf16)
    r = jax.nn.sigmoid(_block_diag(xb, wa_ref, NN) + ba)
    ig = jax.nn.sigmoid(_block_diag(xb, wi_ref, NN) + bi)
    sp = jnp.maximum(-lam, 0.0) + jnp.log(1.0 + jnp.exp(-jnp.abs(lam)))
    log_a = -LRU_C * r * sp
    a = jnp.exp(log_a)
    x2 = 2.0 * log_a
    one_minus = jnp.where(x2 > -1e-2, -x2 * (1.0 + x2 * (0.5 + x2 * (1.0 / 6.0))), 1.0 - a * a)
    mult = jnp.sqrt(one_minus)
    return xb, r, ig, sp, a, mult


def _rg_gates_fwd(proj, conv_w, conv_b, wa, wi, ba, bi, lam, tr=512):
    t = proj.shape[0]
    w = D_MODEL
    tr = min(tr, t)
    nblk = t // tr
    cur, prev, _ = _halo_specs(tr, w, 1, nblk)

    def body(x_ref, xp_ref, cw_ref, cb_ref, wa_ref, wi_ref, ba_ref, bi_ref, lam_ref, xc_ref, a_ref, u_ref):
        x = x_ref[...]
        taps = _taps_before(x, xp_ref[...], pl.program_id(0) == 0) + [x]
        xc = cb_ref[...]
        for k in range(4):
            xc = xc + cw_ref[k:k + 1, :] * taps[k]
        _, _, ig, _, a, mult = _lru_gates(xc, wa_ref, wi_ref, ba_ref[...], bi_ref[...], lam_ref[...])
        xc_ref[...] = xc
        a_ref[...] = a
        u_ref[...] = mult * (ig * xc)

    full = lambda a_: pl.BlockSpec(a_.shape, lambda i, nd=a_.ndim: (0,) * nd)
    ospec = pl.BlockSpec((tr, w), lambda i: (i, 0))
    return pl.pallas_call(
        body, grid=(nblk,), name="rg_gates_fwd",
        in_specs=[cur, prev] + [full(a_) for a_ in (conv_w, conv_b, wa, wi, ba, bi, lam)],
        out_specs=[ospec] * 3, out_shape=[SDS((t, w), f32)] * 3,
        compiler_params=_cparams(("parallel",)))(proj, proj, conv_w, conv_b, wa, wi, ba, bi, lam)


def _lru_scan(name, a, b, reverse, tt=512):
    t, w = a.shape
    tt = min(tt, t)
    nt = t // tt
    ng = tt // SUBLANES

    def body(a_ref, b_ref, h_ref, carry_ref):
        @pl.when(pl.program_id(0) == 0)
        def _():
            carry_ref[...] = jnp.zeros_like(carry_ref)

        row = lax.broadcasted_iota(jnp.int32, (SUBLANES, w), 0)

        def group(gi, carry):
            g = (ng - 1 - gi) if reverse else gi
            rows = pl.ds(pl.multiple_of(g * SUBLANES, SUBLANES), SUBLANES)
            av = a_ref[rows, :]
            bv = b_ref[rows, :]
            for s in (1, 2, 4):
                sh = (SUBLANES - s) if reverse else s
                ok = (row < SUBLANES - s) if reverse else (row >= s)
                a_s = pltpu.roll(av, sh, 0)
                b_s = pltpu.roll(bv, sh, 0)
                bv = jnp.where(ok, av * b_s + bv, bv)
                av = jnp.where(ok, av * a_s, av)
            h = av * carry + bv
            h_ref[rows, :] = h
            edge = h[0:1, :] if reverse else h[SUBLANES - 1:SUBLANES, :]
            return jnp.broadcast_to(edge, (SUBLANES, w))

        carry_ref[...] = lax.fori_loop(0, ng, group, carry_ref[...], unroll=4)

    tmap = (lambda i: (nt - 1 - i, 0)) if reverse else (lambda i: (i, 0))
    spec = pl.BlockSpec((tt, w), tmap)
    return pl.pallas_call(
        body, grid=(nt,), name=name, in_specs=[spec, spec], out_specs=spec,
        out_shape=SDS((t, w), f32), scratch_shapes=[pltpu.VMEM((SUBLANES, w), f32)],
        compiler_params=_cparams(("arbitrary",)))(a, b)


def _rg_gates_bwd(dhs, c, hs, xc, wa, wi, ba, bi, lam, tr=512):
    t, w = xc.shape
    tr = min(tr, t)
    nblk = t // tr
    cur, prev, nxt = _halo_specs(tr, w, 0, nblk)

    def body(dhs_ref, c_ref, cn_ref, hs_ref, hp_ref, xc_ref, wa_ref, wi_ref, ba_ref, bi_ref, lam_ref,
             dxc_ref, dwa_ref, dwi_ref, dba_ref, dbi_ref, dlam_ref):
        i = pl.program_id(0)
        c_next = _taps_after(c_ref[...], cn_ref[...], i == nblk - 1)[0]
        h_prev = _taps_before(hs_ref[...], hp_ref[...], i == 0)[2]
        xc = xc_ref[...]
        lam = lam_ref[...]
        xb, r, ig, sp, a, mult = _lru_gates(xc, wa_ref, wi_ref, ba_ref[...], bi_ref[...], lam)
        dh = dhs_ref[...] + c_next
        dlog_a = dh * h_prev * a - (dh * ig * xc) * (a * a / mult)
        dpre_a = (dlog_a * (-LRU_C * sp) * r * (1.0 - r)).astype(bf16)
        dpre_i = (dh * mult * xc * ig * (1.0 - ig)).astype(bf16)
        dxc_ref[...] = (dh * mult * ig + _block_diag(dpre_a, wa_ref, NT) + _block_diag(dpre_i, wi_ref, NT))
        dsig = 1.0 / (1.0 + jnp.exp(lam))
        sums = [jnp.sum(dpre_a.astype(f32), axis=0, keepdims=True),
                jnp.sum(dpre_i.astype(f32), axis=0, keepdims=True),
                jnp.sum(dlog_a * (-LRU_C * r), axis=0, keepdims=True) * (-dsig)]

        @pl.when(i == 0)
        def _():
            dwa_ref[...] = jnp.zeros_like(dwa_ref)
            dwi_ref[...] = jnp.zeros_like(dwi_ref)
            dba_ref[...] = jnp.zeros_like(dba_ref)
            dbi_ref[...] = jnp.zeros_like(dbi_ref)
            dlam_ref[...] = jnp.zeros_like(dlam_ref)

        for n in range(LRU_BLOCKS):
            sl = slice(n * LRU_BW, (n + 1) * LRU_BW)
            dwa_ref[n] += lax.dot_general(xb[:, sl], dpre_a[:, sl], (TN, ((), ())), preferred_element_type=f32)
            dwi_ref[n] += lax.dot_general(xb[:, sl], dpre_i[:, sl], (TN, ((), ())), preferred_element_type=f32)
        dba_ref[...] += sums[0]
        dbi_ref[...] += sums[1]
        dlam_ref[...] += sums[2]

    full = lambda a_: pl.BlockSpec(a_.shape, lambda i, nd=a_.ndim: (0,) * nd)
    vec = pl.BlockSpec((1, w), lambda i: (0, 0))
    mat = pl.BlockSpec((LRU_BLOCKS, LRU_BW, LRU_BW), lambda i: (0, 0, 0))
    return pl.pallas_call(
        body, grid=(nblk,), name="rg_gates_bwd",
        in_specs=[cur, cur, nxt, cur, prev, cur] + [full(a_) for a_ in (wa, wi, ba, bi, lam)],
        out_specs=[cur, mat, mat, vec, vec, vec],
        out_shape=[SDS((t, w), f32), SDS((LRU_BLOCKS, LRU_BW, LRU_BW), f32), SDS((LRU_BLOCKS, LRU_BW, LRU_BW), f32),
                   SDS((1, w), f32), SDS((1, w), f32), SDS((1, w), f32)],
        compiler_params=_cparams(("arbitrary",)))(dhs, c, c, hs, hs, xc, wa, wi, ba, bi, lam)


def _rg_conv_bwd(dxc, proj, conv_w, tr=512):
    t, w = dxc.shape
    tr = min(tr, t)
    nblk = t // tr
    cur, _, nxt = _halo_specs(tr, w, 0, nblk)
    xcur, xprev, _ = _halo_specs(tr, w, 1, nblk)

    def body(d_ref, dn_ref, x_ref, xp_ref, cw_ref, dx_ref, dcw_ref, dcb_ref):
        i = pl.program_id(0)
        d = d_ref[...]
        x = x_ref[...]
        after = _taps_after(d, dn_ref[...], i == nblk - 1)
        before = _taps_before(x, xp_ref[...], i == 0) + [x]
        dx = cw_ref[3:4, :] * d
        for s in (1, 2, 3):
            dx = dx + cw_ref[3 - s:4 - s, :] * after[s - 1]
        dx_ref[...] = dx.astype(bf16)
        dcw = jnp.concatenate([jnp.sum(d * before[k], axis=0, keepdims=True) for k in range(4)], axis=0)
        dcb = jnp.sum(d, axis=0, keepdims=True)

        @pl.when(i == 0)
        def _():
            dcw_ref[...] = dcw
            dcb_ref[...] = dcb

        @pl.when(i > 0)
        def _():
            dcw_ref[...] += dcw
            dcb_ref[...] += dcb

    return pl.pallas_call(
        body, grid=(nblk,), name="rg_conv_bwd",
        in_specs=[cur, nxt, xcur, xprev, pl.BlockSpec((4, w), lambda i: (0, 0))],
        out_specs=[cur, pl.BlockSpec((4, w), lambda i: (0, 0)), pl.BlockSpec((1, w), lambda i: (0, 0))],
        out_shape=[SDS((t, w), bf16), SDS((4, w), f32), SDS((1, w), f32)],
        compiler_params=_cparams(("arbitrary",)))(dxc, dxc, proj, proj, conv_w)


def _attn_fwd(h, wts, j, plan):
    proj = _mm_cols("attn_in", h, wts["attn_w_in"], j, bf16)
    kpad = jnp.pad(proj[:, A_W:2 * A_W], ((PAD_A, 0), (0, 0)))
    vpad = jnp.pad(proj[:, 2 * A_W:3 * A_W], ((PAD_A, 0), (0, 0)))
    bias = _bias_window(wts["attn_rel_bias"][j])
    plan = plan if j == 0 else None
    out_a = _carried(plan, "chunk_attn_fwd", wts, _chunk_attn_fwd, proj, kpad, vpad, bias)
    out_b, out_b32 = _carried(plan, "sb_attn_fwd", wts, _sb_fwd, proj)
    m = _mm_rows("attn_out", [out_a, out_b], wts["attn_w_out"], j, f32)
    return m, (proj, kpad, vpad, bias, out_a, out_b, out_b32)


def _attn_bwd(dm, h, saved, wts, j, grads, exch):
    proj, kpad, vpad, bias, out_a, out_b, out_b32 = saved
    dout = _mm_rows_t("attn_out_t", dm, wts["attn_w_out"], j, bf16)
    gi, ll = _grad_slot("attn_w_out", j)
    grads["attn_w_out"][gi] = _mm_wgrad("attn_out_wgrad_a", out_a, dm, grads["attn_w_out"][gi], ll, 0)
    grads["attn_w_out"][gi] = _mm_wgrad("attn_out_wgrad_b", out_b, dm, grads["attn_w_out"][gi], ll, 1)
    if exch is not None and j == 0:
        (dqa, dka, dva, dbias), got = _chunk_attn_bwd(proj, kpad, vpad, bias, dout, exch.upper_carry(grads))
        exch.upper_got(got)
        (dqs, dks, dvs), slots = _sb_bwd(proj, out_b32, dout, exch.carry())
        exch.carried(slots)
    else:
        dqa, dka, dva, dbias = _chunk_attn_bwd(proj, kpad, vpad, bias, dout, None)[0]
        dqs, dks, dvs = _sb_bwd(proj, out_b32, dout, None)[0]
    grads["attn_rel_bias"][j] = _bias_window_grad(dbias)
    dproj = jnp.concatenate([dqa, dka[PAD_A:].astype(bf16), dva[PAD_A:].astype(bf16),
                             dqs, dks.astype(bf16), dvs.astype(bf16)], axis=1)
    grads["attn_w_in"][gi] = _mm_wgrad_cols("attn_in_wgrad", h, dproj, grads["attn_w_in"][gi], ll)
    return _mm_cols_t("attn_in_t", dproj, wts["attn_w_in"], j, f32)


def _rg_fwd(h, wts, j, plan):
    proj =_mm_cols("rg_in", h, wts["rg_w_in"], j, f32)
    small = [wts[k][j] for k in ("rg_conv_w", "rg_conv_b", "rg_w_a", "rg_w_i", "rg_b_a", "rg_b_i", "rg_lambda")]
    xc, a, u = _rg_gates_fwd(proj, *small)
    hs = _lru_scan("lru_scan_fwd", a, u, False)
    yp = _rows("rg_gate_out", lambda hv, gv: hv * _gelu(gv), [hs, (proj, D_MODEL, 0)], [], [(D_MODEL, bf16)])[0]
    m = _mm_rows("rg_out", [yp], wts["rg_w_out"], j, f32)
    return m, (proj, xc, a, hs, yp)


def _rg_bwd(dm, h, saved, wts, j, grads, exch):
    proj, xc, a, hs, yp = saved
    dyp = _mm_rows_t("rg_out_t", dm, wts["rg_w_out"], j, f32)
    gi, ll = _grad_slot("rg_w_out", j)
    grads["rg_w_out"][gi] = _mm_wgrad("rg_out_wgrad", yp, dm, grads["rg_w_out"][gi], ll)

    def gate_bwd(dy, hv, gv, av):
        dhs = dy * _gelu(gv)
        return dhs, av * dhs, dy * hv * _gelu_grad(gv)

    dhs, ab, dgate = _rows("rg_gate_out_bwd", gate_bwd, [dyp, hs, (proj, D_MODEL, 0), a], [],
                           [(D_MODEL, f32), (D_MODEL, f32), (D_MODEL, bf16)])
    c = _lru_scan("lru_scan_bwd", a, ab, True)
    wa, wi, ba, bi, lam = [wts[k][j] for k in ("rg_w_a", "rg_w_i", "rg_b_a", "rg_b_i", "rg_lambda")]
    dxc, dwa, dwi, dba, dbi, dlam = _rg_gates_bwd(dhs, c, hs, xc, wa, wi, ba, bi, lam)
    dxr, dcw, dcb = _rg_conv_bwd(dxc, proj, wts["rg_conv_w"][j])
    for k, v in (("rg_w_a", dwa), ("rg_w_i", dwi), ("rg_b_a", dba), ("rg_b_i", dbi), ("rg_lambda", dlam),
                 ("rg_conv_w", dcw), ("rg_conv_b", dcb)):
        grads[k][j] = v
    dproj = jnp.concatenate([dgate, dxr], axis=1)
    grads["rg_w_in"][gi] = _mm_wgrad_cols("rg_in_wgrad", h, dproj, grads["rg_w_in"][gi], ll)
    return _mm_cols_t("rg_in_t", dproj, wts["rg_w_in"], j, f32)


def _local_step(x, target, wts, plan=None, exch=None):
    t = x.shape[0]
    d = D_MODEL
    gains = {k: wts[k] for k in ("norm_mix_pre", "norm_mix_post", "norm_ffn_pre", "norm_ffn_post")}
    gain = lambda k, l: gains[k][l:l + 1]

    saved = []
    h = _rows("norm_in", _norm_fwd, [x], [gain("norm_mix_pre", 0)], [(d, bf16)])[0]
    loss_cols = None
    for l in range(DEPTH):
        j = l // 2
        m, mix_saved = (_attn_fwd if l % 2 == 0 else _rg_fwd)(h, wts, j, plan)

        def resid_next(xv, mv, g_post, g_next):
            x1 = xv + _norm_fwd(mv, g_post)
            return x1, _norm_fwd(x1, g_next)

        x1, h2 = _rows("resid_mix", resid_next, [x, m], [gain("norm_mix_post", l), gain("norm_ffn_pre", l)],
                       [(d, f32), (d, bf16)])
        g, u, hid = _carried(plan if l == 0 else None, "ffn_up", wts, _ffn_up, h2, wts["ffn_w_gate"],
                             wts["ffn_w_up"], l)
        f = _ffn_down(hid, wts["ffn_w_down"], l)
        saved.append((x, h, m, mix_saved, x1, h2, g, u, hid, f))
        if l + 1 < DEPTH:
            x, h = _rows("resid_ffn", resid_next, [x1, f], [gain("norm_ffn_post", l), gain("norm_mix_pre", l + 1)],
                         [(d, f32), (d, bf16)])
        else:
            def resid_loss(xv, fv, tv, g_post):
                err = xv + _norm_fwd(fv, g_post) - tv
                return err * (1.0 / d), jnp.sum(err * err, axis=0, keepdims=True)

            dx, loss_cols = _rows("resid_loss", resid_loss, [x1, f, target], [gain("norm_ffn_post", l)],
                                  [(d, f32)], [((1, d), f32)])
    loss = 0.5 * jnp.sum(loss_cols) / d

    grads = {k: {} for k in SMALL_GRADS}
    for k in BIG_GRADS:
        shp = wts[k].shape
        rest = shp[2:] if shp[1] == 1 else shp[1:]
        grads[k] = [_Fresh((LOWER_LAYERS[k],) + rest), _Fresh((shp[0] - LOWER_LAYERS[k],) + rest)]

    def norm_bwd_cast(uv, dyv, gv):
        du, dg = _norm_bwd(uv, dyv, gv)
        return du, dg

    def norm_bwd_resid(uv, dhv, dxv, gv):
        du, dg = _norm_bwd(uv, dhv, gv)
        return dxv + du, dg

    def norm_bwd_pair(uv, dhv, dxv, nv, g_pre, g_post):
        dx_, dg_pre = norm_bwd_resid(uv, dhv, dxv, g_pre)
        dn, dg_post = _norm_bwd(nv, dx_, g_post)
        return dx_, dn, dg_pre, dg_post

    df = None
    for l in reversed(range(DEPTH)):
        j = l // 2
        x_in, h, m, mix_saved, x1, h2, g, u, hid, f = saved[l]
        if df is None:
            df, grads["norm_ffn_post"][l] = _rows("norm_ffn_post_bwd", norm_bwd_cast, [f, dx],
                                                  [gain("norm_ffn_post", l)], [(d, bf16)], [((1, d), f32)])
        dg, du = _ffn_down_bwd(df, wts["ffn_w_down"], l, g, u, None)[0]
        gi, ll = _grad_slot("ffn_w_down", l)
        grads["ffn_w_down"][gi] = _ffn_wgrad_down(hid, df, grads["ffn_w_down"][gi], ll)
        dh2 = _ffn_up_bwd(dg, du, wts["ffn_w_gate"], wts["ffn_w_up"], l)
        grads["ffn_w_gate"][gi], grads["ffn_w_up"][gi] = _ffn_wgrad_up(
            h2, dg, du, grads["ffn_w_gate"][gi], grads["ffn_w_up"][gi], ll)
        dx1, dm, grads["norm_ffn_pre"][l], grads["norm_mix_post"][l] = _rows(
            "norm_ffn_mix_bwd", norm_bwd_pair, [x1, dh2, dx, m], [gain("norm_ffn_pre", l), gain("norm_mix_post", l)],
            [(d, f32), (d, bf16)], [((1, d), f32), ((1, d), f32)])
        dh = (_attn_bwd if l % 2 == 0 else _rg_bwd)(dm, h, mix_saved, wts, j, grads, exch)
        if l > 0:
            dx, df, grads["norm_mix_pre"][l], grads["norm_ffn_post"][l - 1] = _rows(
                "norm_mix_ffn_bwd", norm_bwd_pair, [x_in, dh, dx1, saved[l - 1][9]],
                [gain("norm_mix_pre", l), gain("norm_ffn_post", l - 1)],
                [(d, f32), (d, bf16)], [((1, d), f32), ((1, d), f32)])
        else:
            dx, grads["norm_mix_pre"][l] = _rows("norm_mix_pre_bwd", norm_bwd_resid, [x_in, dh, dx1],
                                                 [gain("norm_mix_pre", l)], [(d, f32)], [((1, d), f32)])
    return loss, dx, grads


ANY = pl.BlockSpec(memory_space=pl.ANY)
PACK_COLS = 1024
SMALL_ROWS = 288


def _mesh_pos():
    x, y, c = lax.axis_index("x"), lax.axis_index("y"), lax.axis_index("c")
    return x, y, c, [(1 - x, y), (x, 1 - y), (1 - x, 1 - y)]


def _run_copies(copies):
    for cp in copies:
        cp.start()
    for cp in copies:
        cp.wait()


GATHER_SEMS = 7


def _gather_copies(items, ins, outs, send, recv):
    x, y, c, chips = _mesh_pos()
    q = 2 * x + y
    sibling = (x, y, 1 - c)

    def copy(k, src, dst, to):
        return pltpu.make_async_remote_copy(src_ref=src, dst_ref=dst, send_sem=send.at[k], recv_sem=recv.at[k],
                                            device_id=to, device_id_type=MESH)

    own, sent, passed = [], [], []
    for i, (t, l0, nl) in enumerate(items):
        lay = pl.ds(l0, nl)
        half = ins[t].shape[1] // 2
        rows = pl.ds(pl.multiple_of(c * half, half), half)
        own.append(copy(GATHER_SEMS * i, ins[t].at[lay], outs[t].at[lay, q], sibling))
        for j, (px, py) in enumerate(chips):
            sent.append(copy(GATHER_SEMS * i + 1 + j, ins[t].at[lay, rows], outs[t].at[lay, q, rows], (px, py, c)))
            landed = outs[t].at[lay, 2 * px + py, rows]
            passed.append(copy(GATHER_SEMS * i + 4 + j, landed, landed, sibling))
    return own, sent, passed


def _gather_start(items, ins, outs, send, recv):
    own, sent, _ = _gather_copies(items, ins, outs, send, recv)
    for cp in own + sent:
        cp.start()


def _gather_finish(items, ins, outs, send, recv):
    own, sent, passed = _gather_copies(items, ins, outs, send, recv)
    for arrived, forward in zip(sent, passed):
        arrived.wait_recv()
        forward.start()
    for cp in sent:
        cp.wait_send()
    for cp in own + passed:
        cp.wait()


def _gather_call(items, shards):
    n = len(shards)
    nsem = GATHER_SEMS * len(items)

    def body(*refs):
        ins, outs = refs[:n], refs[n:2 * n]
        _gather_start(items, ins, outs, *refs[2 * n:])
        _gather_finish(items, ins, outs, *refs[2 * n:])

    return pl.pallas_call(
        body, name="weight_all_gather", in_specs=[ANY] * n, out_specs=[ANY] * n,
        out_shape=[SDS((s.shape[0], N_CHIPS) + s.shape[1:], s.dtype) for s in shards],
        scratch_shapes=[pltpu.SemaphoreType.DMA((nsem,)), pltpu.SemaphoreType.DMA((nsem,))])(*shards)


def _call(body, operands, *, name, grid, in_specs, out_specs, out_shape, sem, scratch=(), gather=None):
    if gather is None:
        return pl.pallas_call(body, grid=grid, in_specs=in_specs, out_specs=out_specs, out_shape=out_shape,
                              scratch_shapes=list(scratch), name=name, compiler_params=_cparams(sem))(*operands), None
    start, finish, c_ins, c_io, c_new, nsem = gather
    n_in, n_out, n_scr = len(operands), len(out_shape), len(scratch)
    ni, nio, nco = len(c_ins), len(c_io), len(c_io) + len(c_new)

    def full(*refs):
        ins, sh = refs[:n_in], refs[n_in:n_in + ni]
        outs = refs[n_in + ni + nio:n_in + ni + nio + n_out]
        co = refs[n_in + ni + nio + n_out:n_in + ni + nio + n_out + nco]
        scr = refs[n_in + ni + nio + n_out + nco:]
        ids = [pl.program_id(a) for a in range(len(grid))]
        first = functools.reduce(jnp.logical_and, [i == 0 for i in ids])
        last = functools.reduce(jnp.logical_and, [i == g - 1 for i, g in zip(ids, grid)])

        @pl.when(first)
        def _():
            start(sh, co, scr[n_scr], scr[n_scr + 1])

        body(*ins, *outs, *scr[:n_scr])

        @pl.when(last)
        def _():
            finish(sh, co, scr[n_scr], scr[n_scr + 1])

    res = pl.pallas_call(
        full, grid=grid, in_specs=list(in_specs) + [ANY] * (ni + nio), out_specs=list(out_specs) + [ANY] * nco,
        out_shape=list(out_shape) + [SDS(g.shape, g.dtype) for g in list(c_io) + list(c_new)],
        scratch_shapes=list(scratch) + [pltpu.SemaphoreType.DMA((nsem,)), pltpu.SemaphoreType.DMA((nsem,))],
        input_output_aliases={n_in + ni + t: n_out + t for t in range(nio)}, name=name,
        compiler_params=_cparams(("arbitrary",) * len(grid)))(*operands, *c_ins, *c_io)
    return res[:n_out], res[n_out:]


def _pair_exchange(gs):
    n = len(gs)

    def body(*refs):
        _pair_copies(refs[:n], refs[n:2 * n], *refs[2 * n:], start=True)
        _pair_copies(refs[:n], refs[n:2 * n], *refs[2 * n:], start=False)

    return pl.pallas_call(
        body, name="grad_pair_exchange", in_specs=[ANY] * n, out_specs=[ANY] * n,
        out_shape=_pair_shapes(gs),
        scratch_shapes=[pltpu.SemaphoreType.DMA((n,)), pltpu.SemaphoreType.DMA((n,))])(*gs)


def _pair_shapes(gs):
    return [SDS(g.shape[:2] + (g.shape[2] // 2, g.shape[3]), f32) for g in gs]


def _pair_copies(ins, outs, send, recv, start):
    x, y, c, _ = _mesh_pos()
    for t in range(len(ins)):
        half = ins[t].shape[2] // 2
        src = ins[t].at[:, :, pl.ds(pl.multiple_of((1 - c) * half, SUBLANES), half)]
        cp = pltpu.make_async_remote_copy(src_ref=src, dst_ref=outs[t], send_sem=send.at[t], recv_sem=recv.at[t],
                                          device_id=(x, y, 1 - c), device_id_type=MESH)
        cp.start() if start else cp.wait()


def _pair_carry(gs):
    return (functools.partial(_pair_copies, start=True), functools.partial(_pair_copies, start=False),
            gs, [], _pair_shapes(gs), len(gs))


def _pair_sum(name, g, got, c):
    l, s, r, cols = g.shape

    def body(c_ref, a_ref, b_ref, o_ref):
        o_ref[...] = (a_ref[...] + b_ref[...]).astype(bf16)

    blk = (None, None, r // 2, cols)
    return pl.pallas_call(
        body, name=name, out_shape=SDS(got.shape, bf16),
        grid_spec=pltpu.PrefetchScalarGridSpec(
            num_scalar_prefetch=1, grid=(l, s),
            in_specs=[pl.BlockSpec(blk, lambda i, q, c_ref: (i, q, c_ref[0], 0)),
                      pl.BlockSpec(blk, lambda i, q, c_ref: (i, q, 0, 0))],
            out_specs=pl.BlockSpec(blk, lambda i, q, c_ref: (i, q, 0, 0))),
        compiler_params=_cparams(("parallel", "parallel")))(c, g, got)


def _chip_exchange(hs):
    n = len(hs)

    def body(*refs):
        _chip_copies(refs[:n], refs[n:2 * n], *refs[2 * n:], start=True)
        _chip_copies(refs[:n], refs[n:2 * n], *refs[2 * n:], start=False)

    return pl.pallas_call(
        body, name="grad_chip_exchange", in_specs=[ANY] * n, out_specs=[ANY] * n,
        out_shape=[SDS(h.shape, h.dtype) for h in hs],
        scratch_shapes=[pltpu.SemaphoreType.DMA((3 * n,)), pltpu.SemaphoreType.DMA((3 * n,))])(*hs)


def _chip_copies(ins, outs, send, recv, start):
    x, y, c, chips = _mesh_pos()
    q = 2 * x + y
    for t in range(len(ins)):
        for j, (px, py) in enumerate(chips):
            cp = pltpu.make_async_remote_copy(
                src_ref=ins[t].at[:, 2 * px + py], dst_ref=outs[t].at[:, q], send_sem=send.at[3 * t + j],
                recv_sem=recv.at[3 * t + j], device_id=(px, py, c), device_id_type=MESH)
            cp.start() if start else cp.wait()


def _chip_carry(hs):
    return (functools.partial(_chip_copies, start=True), functools.partial(_chip_copies, start=False),
            hs, [], [SDS(h.shape, h.dtype) for h in hs], 3 * len(hs))


def _chip_sum(name, s, h, pos, l0, layers, into):
    l, _, r, cols = s.shape

    def body(pos_ref, s0, s1, s2, s3, own_ref, *rest):
        vals = [jnp.where(pos_ref[0] == p, own_ref[...], ref[...]).astype(f32) for p, ref in enumerate((s0, s1, s2, s3))]
        rest[-1][...] = ((vals[0] + vals[1]) + vals[2]) + vals[3]

    blk = (None, None, r, cols)
    slot = lambda p: pl.BlockSpec(blk, lambda i, pos_ref: (i, jnp.where(pos_ref[0] == p, (p + 1) % N_CHIPS, p), 0, 0))
    extra, alias = ([], {}) if into is None else ([into], {6: 0})
    return pl.pallas_call(
        body, name=name, out_shape=SDS((layers, 2 * r, cols), f32), input_output_aliases=alias,
        grid_spec=pltpu.PrefetchScalarGridSpec(
            num_scalar_prefetch=1, grid=(l,),
            in_specs=[slot(p) for p in range(N_CHIPS)] + [pl.BlockSpec(blk, lambda i, pos_ref: (i, pos_ref[0], 0, 0))]
            + [ANY] * len(extra),
            out_specs=pl.BlockSpec((None, r, cols), lambda i, pos_ref: (l0 + i, pos_ref[1], 0))),
        compiler_params=_cparams(("parallel",)))(pos, s, s, s, s, h, *extra)


def _pair_gather(fulls):
    n = len(fulls)

    def body(*refs):
        ins, outs = refs[:n], refs[n:2 * n]
        send, recv = refs[2 * n:]
        x, y, c, _ = _mesh_pos()
        copies = []
        for t in range(n):
            half = outs[t].shape[1] // 2
            rows = outs[t].at[:, pl.ds(pl.multiple_of(c * half, SUBLANES), half)]
            copies.append(pltpu.make_async_remote_copy(
                src_ref=rows, dst_ref=rows, send_sem=send.at[t], recv_sem=recv.at[t],
                device_id=(x, y, 1 - c), device_id_type=MESH))
        _run_copies(copies)

    return pl.pallas_call(
        body, name="grad_pair_gather", in_specs=[ANY] * n, out_specs=[ANY] * n,
        out_shape=[SDS(f.shape, f32) for f in fulls], input_output_aliases={t: t for t in range(n)},
        scratch_shapes=[pltpu.SemaphoreType.DMA((n,)), pltpu.SemaphoreType.DMA((n,))])(*fulls)


COL_SHARDED = ("attn_w_in", "rg_w_in", "ffn_w_gate", "ffn_w_up")
ROW_SHARDED = ("attn_w_out", "rg_w_out")
GATES = ("rg_w_a", "rg_w_i")
VECTORS = ("rg_conv_w", "rg_conv_b", "rg_b_a", "rg_b_i", "rg_lambda")
REPLICATED = ("norm_mix_pre", "norm_mix_post", "norm_ffn_pre", "norm_ffn_post", "attn_rel_bias")
BIG_GRADS = COL_SHARDED + ROW_SHARDED + ("ffn_w_down",)
SMALL_GRADS = GATES + VECTORS + REPLICATED
WEIGHTS =("attn_w_in", "attn_rel_bias", "attn_w_out", "rg_w_in", "rg_conv_w", "rg_conv_b", "rg_w_a", "rg_b_a",
           "rg_w_i", "rg_b_i", "rg_lambda", "rg_w_out", "norm_mix_pre", "norm_mix_post", "norm_ffn_pre",
           "norm_ffn_post", "ffn_w_gate", "ffn_w_up", "ffn_w_down")
SMALL = VECTORS + REPLICATED


GATHER_PARTS = {
    "first": (("attn_w_in", 0, 1), ("attn_w_out", 0, 1), ("rg_w_a", 0, 8), ("rg_w_i", 0, 8), ("vec", 0, 1)),
    "chunk_attn_fwd": (("ffn_w_gate", 0, 1), ("ffn_w_up", 0, 1), ("ffn_w_down", 0, 1), ("rg_w_in", 0, 1),
                       ("rg_w_out", 0, 1)),
    "sb_attn_fwd": (("ffn_w_gate", 1, 3), ("ffn_w_up", 1, 3), ("ffn_w_down", 1, 3)),
    "ffn_up": (("rg_w_in", 1, 1), ("rg_w_out", 1, 1), ("attn_w_in", 1, 1), ("attn_w_out", 1, 1)),
}


TRANSPOSED = ("ffn_w_gate", "ffn_w_up")


def _natural(name, a):
    return jnp.swapaxes(a, 1, 2) if name in TRANSPOSED else a


class _WeightGather:
    def __init__(self, w):
        self.w = w
        self.names = list(COL_SHARDED + ROW_SHARDED + GATES + ("ffn_w_down", "vec"))
        self.shards = {}
        for k in self.names[:-1]:
            a = _natural(k, w[k]).astype(bf16)
            self.shards[k] = a.reshape((-1,) + a.shape[-2:])
        self.shards["vec"] = jnp.concatenate([w[k].reshape(-1) for k in VECTORS]).reshape(1, -1, LANES)
        got = _gather_call(self._items("first", self.names), [self.shards[k] for k in self.names])
        self.raw = dict(zip(self.names, got))

    @staticmethod
    def _items(part, names):
        return [(names.index(k), l0, nl) for k, l0, nl in GATHER_PARTS[part]]

    def part(self, part):
        names = list(dict.fromkeys(k for k, _, _ in GATHER_PARTS[part]))
        items = self._items(part, names)
        return (functools.partial(_gather_start, items), functools.partial(_gather_finish, items),
                [self.shards[k] for k in names], [self.raw[k] for k in names], [], GATHER_SEMS * len(items)), names

    def views(self):
        got, w = self.raw, self.w
        out = {k: w[k] for k in REPLICATED}
        for k in COL_SHARDED + ("ffn_w_down",):
            out[k] = got[k]
        for k in ROW_SHARDED:
            l, s, ks, n = got[k].shape
            out[k] = got[k].reshape(l, 1, s * ks, n)
        for k in GATES:
            out[k] = got[k].reshape(2, LRU_BLOCKS, LRU_BW, LRU_BW)
        vec = got["vec"].reshape(N_CHIPS, -1)
        off = 0
        for k in VECTORS:
            shp = w[k].shape
            n = int(np.prod(shp))
            piece = vec[:, off:off + n].reshape((N_CHIPS,) + shp)
            off += n
            if k == "rg_conv_w":
                out[k] = piece.reshape(N_CHIPS, 2, 4, 256).transpose(1, 2, 0, 3).reshape(2, 4, D_MODEL)
            elif k in ("rg_b_a", "rg_b_i"):
                out[k] = piece.transpose(1, 2, 0, 3).reshape(2, 1, D_MODEL)
            else:
                out[k] = piece.transpose(1, 0, 2).reshape(2, 1, D_MODEL)
        return out


def _carried(plan, part, wts, fn, *args):
    if plan is None:
        return fn(*args, None)[0]
    gather, names = plan.part(part)
    out, new = fn(*args, gather)
    plan.raw.update(zip(names, new))
    wts.update(plan.views())
    return out


def _grad_blocks(name, g):
    st = jnp.stack([g[i] for i in sorted(g)])
    if name in GATES:
        st = st.reshape(2, LRU_BLOCKS, N_CHIPS, LRU_BW // N_CHIPS, LRU_BW).transpose(2, 0, 1, 3, 4)
    elif name == "rg_conv_w":
        st = st.reshape(2, 4, N_CHIPS, -1).transpose(2, 0, 1, 3)
    elif name in ("rg_b_a", "rg_b_i"):
        st = st.reshape(2, LRU_BLOCKS, N_CHIPS, -1).transpose(2, 0, 1, 3)
    elif name in VECTORS:
        st = st.reshape(2, N_CHIPS, -1).transpose(1, 0, 2)
    else:
        st = jnp.broadcast_to(st.reshape(1, -1), (N_CHIPS, st.size))
    return st.reshape(N_CHIPS, -1)


class _GradExchange:
    def __init__(self):
        self.c = lax.axis_index("c").astype(jnp.int32).reshape(1)
        self.pos = jnp.stack([2 * lax.axis_index("x") + lax.axis_index("y"), lax.axis_index("c")]).astype(jnp.int32)
        self.up = self.got_up = self.parts_up = self.slots_up = None

    @staticmethod
    def _blocked(g):
        if g.ndim == 3:
            g = g.reshape(g.shape[0], N_CHIPS, g.shape[1] // N_CHIPS, g.shape[2])
        return g

    def _sums(self, tag, names, gs, got):
        return [_pair_sum("grad_pair_sum_" + tag + k, g, r, self.c) for k, g, r in zip(names, gs, got)]

    def upper_carry(self, grads):
        self.up = [self._blocked(grads[k][1]) for k in BIG_GRADS]
        return _pair_carry(self.up)

    def upper_got(self, got):
        self.got_up = got

    def carry(self):
        self.parts_up = self._sums("up_", BIG_GRADS, self.up, self.got_up)
        return _chip_carry(self.parts_up)

    def carried(self, slots):
        self.slots_up = slots

    def finish(self, grads, shard_shapes):
        if self.got_up is None:
            self.upper_carry(grads)
            self.got_up = _pair_exchange(self.up)
        if self.slots_up is None:
            self.carry()
            self.slots_up = _chip_exchange(self.parts_up)
        blocks = [_grad_blocks(k, grads[k]) for k in SMALL_GRADS]
        used = sum(b.shape[1] for b in blocks)
        small = jnp.concatenate(blocks + [jnp.zeros((N_CHIPS, SMALL_ROWS * PACK_COLS - used), f32)], axis=1)
        names = tuple(k for k in BIG_GRADS if LOWER_LAYERS[k]) + ("small",)
        gs = [self._blocked(grads[k][0]) for k in names[:-1]] + [small.reshape(1, N_CHIPS, SMALL_ROWS, PACK_COLS)]
        parts = dict(zip(names, self._sums("lo_", names, gs, _pair_exchange(gs))))
        slots = dict(zip(names, _chip_exchange([parts[k] for k in names])))
        fulls = []
        for i, k in enumerate(BIG_GRADS):
            nlo, nup = LOWER_LAYERS[k], self.parts_up[i].shape[0]
            full = _chip_sum("grad_chip_sum_up_" + k, self.slots_up[i], self.parts_up[i], self.pos, nlo, nlo + nup, None)
            if nlo:
                full = _chip_sum("grad_chip_sum_lo_" + k, slots[k], parts[k], self.pos, 0, nlo + nup, full)
            fulls.append(full)
        fulls.append(_chip_sum("grad_chip_sum_lo_small", slots["small"], parts["small"], self.pos, 0, 1, None))
        full = _pair_gather(fulls)
        out = {k: f.reshape(shard_shapes[k]) for k, f in zip(BIG_GRADS, full)}
        flat, off = full[-1].reshape(-1), 0
        for k in SMALL_GRADS:
            n = int(np.prod(shard_shapes[k]))
            out[k] = flat[off:off + n].reshape(shard_shapes[k])
            off += n
        return out


def _adamw_fn(w, g, m, v):
    m = ADAM_B1 * m + (1.0 - ADAM_B1) * g
    v = ADAM_B2 * v + (1.0 - ADAM_B2) * (g * g)
    m_hat = m / (1.0 - ADAM_B1 ** ADAM_STEP)
    v_hat = v / (1.0 - ADAM_B2 ** ADAM_STEP)
    return -ADAM_LR * (m_hat / (jnp.sqrt(v_hat) + ADAM_EPS) + ADAM_WD * w), m, v


def _adamw(name, w, g, m, v):
    shp = w.shape
    if w.size >= 1 << 16:
        width = shp[-1]
        ops = [a.reshape(-1, width) for a in (w, g, m, v)]
        res = _rows(name, _adamw_fn, ops, [], [(width, f32)] * 3)
        return [r.reshape(shp) for r in res]
    n = w.size
    rows = -(-n // (SUBLANES * LANES)) * SUBLANES
    ops = [jnp.pad(a.reshape(-1), (0, rows * LANES - n)).reshape(rows, LANES) for a in (w, g, m, v)]
    res = _rows(name, _adamw_fn, ops, [], [(LANES, f32)] * 3, tr=rows)
    return [r.reshape(-1)[:n].reshape(shp) for r in res]


def kernel(x, attn_w_in, attn_rel_bias, attn_w_out, rg_w_in, rg_conv_w, rg_conv_b, rg_w_a, rg_b_a, rg_w_i, rg_b_i, rg_lambda, rg_w_out, norm_mix_pre, norm_mix_post, norm_ffn_pre, norm_ffn_post, ffn_w_gate, ffn_w_up, ffn_w_down, loss_target, m_attn_w_in, m_attn_rel_bias, m_attn_w_out, m_rg_w_in, m_rg_conv_w, m_rg_conv_b, m_rg_w_a, m_rg_b_a, m_rg_w_i, m_rg_b_i, m_rg_lambda, m_rg_w_out, m_norm_mix_pre, m_norm_mix_post, m_norm_ffn_pre, m_norm_ffn_post, m_ffn_w_gate, m_ffn_w_up, m_ffn_w_down, v_attn_w_in, v_attn_rel_bias, v_attn_w_out, v_rg_w_in, v_rg_conv_w, v_rg_conv_b, v_rg_w_a, v_rg_b_a, v_rg_w_i, v_rg_b_i, v_rg_lambda, v_rg_w_out, v_norm_mix_pre, v_norm_mix_post, v_norm_ffn_pre, v_norm_ffn_post, v_ffn_w_gate, v_ffn_w_up, v_ffn_w_down):
    w = dict(zip(WEIGHTS, (attn_w_in, attn_rel_bias, attn_w_out, rg_w_in, rg_conv_w, rg_conv_b, rg_w_a, rg_b_a, rg_w_i,
                           rg_b_i, rg_lambda, rg_w_out, norm_mix_pre, norm_mix_post, norm_ffn_pre, norm_ffn_post,
                           ffn_w_gate, ffn_w_up, ffn_w_down)))
    m = dict(zip(WEIGHTS, (m_attn_w_in, m_attn_rel_bias, m_attn_w_out, m_rg_w_in, m_rg_conv_w, m_rg_conv_b, m_rg_w_a,
                           m_rg_b_a, m_rg_w_i, m_rg_b_i, m_rg_lambda, m_rg_w_out, m_norm_mix_pre, m_norm_mix_post,
                           m_norm_ffn_pre, m_norm_ffn_post, m_ffn_w_gate, m_ffn_w_up, m_ffn_w_down)))
    v = dict(zip(WEIGHTS, (v_attn_w_in, v_attn_rel_bias, v_attn_w_out, v_rg_w_in, v_rg_conv_w, v_rg_conv_b, v_rg_w_a,
                           v_rg_b_a, v_rg_w_i, v_rg_b_i, v_rg_lambda, v_rg_w_out, v_norm_mix_pre, v_norm_mix_post,
                           v_norm_ffn_pre, v_norm_ffn_post, v_ffn_w_gate, v_ffn_w_up, v_ffn_w_down)))
    plan = _WeightGather(w)
    exch = _GradExchange()
    loss, dx, grads = _local_step(x[0], loss_target[0], plan.views(), plan, exch)
    loss = lax.psum(loss, ("x", "y", "c"))
    g = exch.finish(grads, {k: _natural(k, w[k]).shape for k in WEIGHTS})

    big = [k for k in WEIGHTS if k not in SMALL]
    upd = {}
    for k in big:
        res = _adamw("adamw_" + k, _natural(k, w[k]), g[k], _natural(k, m[k]), _natural(k, v[k]))
        upd[k] = [_natural(k, r) for r in res]
        g[k] = _natural(k, g[k])
    cat = lambda d: jnp.concatenate([d[k].reshape(-1) for k in SMALL])
    small = _adamw("adamw_small", cat(w), cat(g), cat(m), cat(v))
    off = 0
    for k in SMALL:
        n = w[k].size
        upd[k] = [r[off:off + n].reshape(w[k].shape) for r in small]
        off += n
    return (loss, dx[None], *[g[k] for k in WEIGHTS], *[upd[k][0] for k in WEIGHTS],
            *[upd[k][1] for k in WEIGHTS], *[upd[k][2] for k in WEIGHTS])
```

```python
import functools

import numpy as np
import jax
import jax.numpy as jnp
from jax import lax
from jax.experimental import pallas as pl
from jax.experimental.pallas import tpu as pltpu

f32 = jnp.float32
bf16 = jnp.bfloat16
SDS = jax.ShapeDtypeStruct
MESH = pl.DeviceIdType.MESH

D_MODEL = 1024
N_CHIPS = 4
DEPTH = 4
HEAD_DIM = 64
CHUNK = 64
N_LEFT = 8
REL_CLIP = 256
A_W = 512
LRU_BLOCKS = 4
LRU_BW = 256
LRU_C = 8.0
D_FF = 2816
RMS_EPS = 1e-6
LANES = 128
SUBLANES = 8
VMEM_LIMIT = 56 * 1024 * 1024

QB_A = 2 * CHUNK
QSUB_A = 8
KW_A = QB_A + N_LEFT * CHUNK
PAD_A = N_LEFT * CHUNK
EXT_A = 768
SB_BLK = 256
QSUB_B = 4
SB_DEAD = -110.0

ADAM_LR, ADAM_B1, ADAM_B2, ADAM_EPS, ADAM_WD, ADAM_STEP = 0.001, 0.9, 0.999, 1e-08, 0.01, 10


def _cparams(sem):
    return pltpu.CompilerParams(dimension_semantics=sem, vmem_limit_bytes=VMEM_LIMIT)


def _gemm(name, operands, in_specs, o_spec, out_shape, grid, dims, acc_shape, into=None):
    nred = grid[2]
    npair = len(operands) // 2
    nin = 2 * npair + (into is not None)

    def body(*refs):
        o_ref = refs[nin]
        p = None
        for t in range(npair):
            d = lax.dot_general(refs[2 * t][...], refs[2 * t + 1][...], (dims, ((), ())),
                                preferred_element_type=f32)
            p = d if p is None else p + d
        if nred == 1:
            o_ref[...] = p.astype(o_ref.dtype)
        else:
            acc = refs[nin + 1]
            r = pl.program_id(2)

            @pl.when(r == 0)
            def _():
                acc[...] = p

            @pl.when(r > 0)
            def _():
                acc[...] += p

            @pl.when(r == nred - 1)
            def _():
                o_ref[...] = acc[...].astype(o_ref.dtype)

    scratch = [] if nred == 1 else [pltpu.VMEM(acc_shape, f32)]
    extra, alias = ([], {}) if into is None else ([into], {2 * npair: 0})
    return pl.pallas_call(
        body, grid=grid, in_specs=list(in_specs) + [pl.BlockSpec(memory_space=pl.ANY)] * len(extra),
        out_specs=o_spec, out_shape=out_shape, scratch_shapes=scratch, name=name, input_output_aliases=alias,
        compiler_params=_cparams(("parallel", "parallel", "arbitrary")))(*operands, *extra)


LOWER_LAYERS = {"attn_w_in": 1, "attn_w_out": 1, "rg_w_in": 0, "rg_w_out": 0,
                "ffn_w_gate": 0, "ffn_w_up": 0, "ffn_w_down": 0}


def _grad_slot(name, l):
    n = LOWER_LAYERS[name]
    return (0, l) if l < n else (1, l - n)


class _Fresh:
    def __init__(self, shape):
        self.shape = tuple(shape)


def _into(buf):
    return None if isinstance(buf, _Fresh) else buf


NN = ((1,), (0,))
NT = ((1,), (1,))
TN = ((0,), (0,))


WGRAD_TOKENS = 2048


def _tile(t, want=1024):
    return min(want, t)


def _mm_cols(name, a, w, l, out_dtype):
    t, k = a.shape
    _, s, _, ns = w.shape
    tm = _tile(t, 2048)
    return _gemm(
        name, [a, w],
        [pl.BlockSpec((tm, k), lambda i, j, r: (i, 0)),
         pl.BlockSpec((None, None, k, ns), lambda i, j, r: (l, j, 0, 0))],
        pl.BlockSpec((tm, ns), lambda i, j, r: (i, j)),
        SDS((t, s * ns), out_dtype), (t // tm, s, 1), NN, None)


def _mm_cols_t(name, dy, w, l, out_dtype):
    t = dy.shape[0]
    _, s, k, ns = w.shape
    tm = _tile(t, 2048)
    return _gemm(
        name, [dy, w],
        [pl.BlockSpec((tm, ns), lambda i, j, r: (i, r)),
         pl.BlockSpec((None, None, k, ns), lambda i, j, r: (l, r, 0, 0))],
        pl.BlockSpec((tm, k), lambda i, j, r: (i, 0)),
        SDS((t, k), out_dtype), (t // tm, 1, s), NT, (tm, k))


def _mm_wgrad_cols(name, a, dy, buf, l):
    t, k = a.shape
    _, s, _, ns = buf.shape
    tt = _tile(t, 2 * WGRAD_TOKENS)
    return _gemm(
        name, [a, dy],
        [pl.BlockSpec((tt, k), lambda i, j, r: (r, 0)),
         pl.BlockSpec((tt, ns), lambda i, j, r: (r, i))],
        pl.BlockSpec((None, None, k, ns), lambda i, j, r: (l, i, 0, 0)),
        SDS(buf.shape, f32), (s, 1, t // tt), TN, (k, ns), into=_into(buf))


def _mm_rows(name, parts, w, l, out_dtype):
    t = parts[0].shape[0]
    n = w.shape[3]
    tm = _tile(t, 2048)
    ops, specs = [], []
    for p_i, a in enumerate(parts):
        kp = a.shape[1]
        ops += [a, w]
        specs += [pl.BlockSpec((tm, kp), lambda i, j, r: (i, 0)),
                  pl.BlockSpec((None, None, kp, n), lambda i, j, r, p_i=p_i: (l, 0, p_i, 0))]
    return _gemm(name, ops, specs, pl.BlockSpec((tm, n), lambda i, j, r: (i, 0)),
                 SDS((t, n), out_dtype), (t // tm, 1, 1), NN, None)


def _mm_rows_t(name, dy, w, l, out_dtype):
    t, n = dy.shape
    k = w.shape[2]
    tm = _tile(t, 2048)
    return _gemm(
        name, [dy, w],
        [pl.BlockSpec((tm, n), lambda i, j, r: (i, 0)),
         pl.BlockSpec((None, None, k, n), lambda i, j, r: (l, 0, 0, 0))],
        pl.BlockSpec((tm, k), lambda i, j, r: (i, 0)),
        SDS((t, k), out_dtype), (t // tm, 1, 1), NT, None)


def _mm_wgrad(name, a, dy, buf, l, part=0):
    t, k = a.shape
    n = dy.shape[1]
    tt = _tile(t, 2 * WGRAD_TOKENS)
    return _gemm(
        name, [a, dy],
        [pl.BlockSpec((tt, k), lambda i, j, r: (r, 0)),
         pl.BlockSpec((tt, n), lambda i, j, r: (r, 0))],
        pl.BlockSpec((None, k, n), lambda i, j, r: (l, part, 0)),
        SDS(buf.shape, f32), (1, 1, t // tt), TN, (k, n), into=_into(buf))


def _ffn_up(h, wg, wu, l, gather):
    t, k = h.shape
    s, fs = wg.shape[1], wg.shape[2]
    tm = _tile(t, 2048)

    def body(h_ref, wg_ref, wu_ref, g_ref, u_ref, hid_ref):
        hv = h_ref[...]
        g = lax.dot_general(hv, wg_ref[...], (NT, ((), ())), preferred_element_type=f32)
        u = lax.dot_general(hv, wu_ref[...], (NT, ((), ())), preferred_element_type=f32)
        g_ref[...] = g.astype(bf16)
        u_ref[...] = u.astype(bf16)
        hid_ref[...] = (g * jax.nn.sigmoid(g) * u).astype(bf16)

    wspec = pl.BlockSpec((None, None, fs, k), lambda j, i: (l, j, 0, 0))
    ospec = pl.BlockSpec((None, tm, fs), lambda j, i: (j, i, 0))
    return _call(
        body, [h, wg, wu], grid=(s, t // tm), name="ffn_up",
        in_specs=[pl.BlockSpec((tm, k), lambda j, i: (i, 0)), wspec, wspec],
        out_specs=[ospec, ospec, ospec], out_shape=[SDS((s, t, fs), bf16)] * 3,
        sem=("parallel", "parallel"), gather=gather)


def _ffn_down(hid, wd, l):
    s, t, fs = hid.shape
    n = wd.shape[3]
    tm = _tile(t)
    ops, specs = [], []
    for r in range(s):
        ops += [hid, wd]
        specs += [pl.BlockSpec((None, tm, fs), lambda i, j, k, r=r: (r, i, 0)),
                  pl.BlockSpec((None, None, fs, n), lambda i, j, k, r=r: (l, r, 0, 0))]
    return _gemm("ffn_down", ops, specs, pl.BlockSpec((tm, n), lambda i, j, k: (i, 0)),
                 SDS((t, n), f32), (t // tm, 1, 1), NN, None)


def _ffn_down_bwd(df, wd, l, g, u, gather):
    t, n = df.shape
    s, fs = wd.shape[1], wd.shape[2]
    tm = _tile(t)

    def body(df_ref, wd_ref, g_ref, u_ref, dg_ref, du_ref):
        dh = lax.dot_general(df_ref[...], wd_ref[...], (NT, ((), ())), preferred_element_type=f32)
        gv = g_ref[...].astype(f32)
        uv = u_ref[...].astype(f32)
        sg = jax.nn.sigmoid(gv)
        du_ref[...] = (dh * gv * sg).astype(bf16)
        dg_ref[...] = (dh * uv * (sg * (1.0 + gv * (1.0 - sg)))).astype(bf16)

    bspec = pl.BlockSpec((None, tm, fs), lambda j, i: (j, i, 0))
    return _call(
        body, [df, wd, g, u], grid=(s, t // tm), name="ffn_down_bwd",
        in_specs=[pl.BlockSpec((tm, n), lambda j, i: (i, 0)),
                  pl.BlockSpec((None, None, fs, n), lambda j, i: (l, j, 0, 0)), bspec, bspec],
        out_specs=[bspec, bspec], out_shape=[SDS((s, t, fs), bf16)] * 2,
        sem=("parallel", "parallel"), gather=gather)


def _ffn_up_bwd(dg, du, wg, wu, l):
    s, t, fs = dg.shape
    k = wg.shape[3]
    tm = _tile(t, 512)
    ops, specs = [], []
    for r in range(s):
        aspec = pl.BlockSpec((None, tm, fs), lambda i, j, kk, r=r: (r, i, 0))
        wspec = pl.BlockSpec((None, None, fs, k), lambda i, j, kk, r=r: (l, r, 0, 0))
        ops += [dg, wg, du, wu]
        specs += [aspec, wspec, aspec, wspec]
    return _gemm("ffn_up_bwd", ops, specs, pl.BlockSpec((tm, k), lambda i, j, kk: (i, 0)),
                 SDS((t, k), f32), (t // tm, 1, 1), NN, None)


def _ffn_wgrad_up(h, dg, du, buf_g, buf_u, l):
    t, k = h.shape
    s, _, fs = dg.shape
    tt = _tile(t, WGRAD_TOKENS)
    nred = t // tt

    fresh = isinstance(buf_g, _Fresh)

    def body(*refs):
        h_ref, dg_ref, du_ref = refs[:3]
        og_ref, ou_ref, acc_g, acc_u = refs[-4:]
        r = pl.program_id(1)
        hv = h_ref[...]
        pg = lax.dot_general(dg_ref[...], hv, (TN, ((), ())), preferred_element_type=f32)
        pu = lax.dot_general(du_ref[...], hv, (TN, ((), ())), preferred_element_type=f32)

        @pl.when(r == 0)
        def _():
            acc_g[...] = pg
            acc_u[...] = pu

        @pl.when(r > 0)
        def _():
            acc_g[...] += pg
            acc_u[...] += pu

        @pl.when(r == nred - 1)
        def _():
            og_ref[...] = acc_g[...]
            ou_ref[...] = acc_u[...]

    dspec = pl.BlockSpec((None, tt, fs), lambda i, r: (i, r, 0))
    ospec = pl.BlockSpec((None, None, fs, k), lambda i, r: (l, i, 0, 0))
    extra, alias = ([], {}) if fresh else ([buf_g, buf_u], {3: 0, 4: 1})
    return pl.pallas_call(
        body, grid=(s, nred), name="ffn_wgrad_up",
        in_specs=[pl.BlockSpec((tt, k), lambda i, r: (r, 0)), dspec, dspec] + [ANY] * len(extra),
        out_specs=[ospec, ospec], out_shape=[SDS(buf_g.shape, f32), SDS(buf_u.shape, f32)],
        scratch_shapes=[pltpu.VMEM((fs, k), f32)] * 2, input_output_aliases=alias,
        compiler_params=_cparams(("parallel", "arbitrary")))(h, dg, du, *extra)


def _ffn_wgrad_down(hid, df, buf, l):
    s, t, fs = hid.shape
    n = df.shape[1]
    tt = _tile(t, 2 * WGRAD_TOKENS)
    return _gemm(
        "ffn_wgrad_down", [hid, df],
        [pl.BlockSpec((None, tt, fs), lambda i, j, r: (i, r, 0)),
         pl.BlockSpec((tt, n), lambda i, j, r: (r, 0))],
        pl.BlockSpec((None, None, fs, n), lambda i, j, r: (l, i, 0, 0)),
        SDS(buf.shape, f32), (s, 1, t // tt), TN, (fs, n), into=_into(buf))


def _rows(name, fn, rows, consts, row_outs, acc_outs=(), tr=512):
    rows = [r if isinstance(r, tuple) else (r, r.shape[1], 0) for r in rows]
    t = rows[0][0].shape[0]
    tr = max(d for d in range(SUBLANES, min(tr, t) + 1, SUBLANES) if t % d == 0)
    nin = len(rows) + len(consts)
    no, na = len(row_outs), len(acc_outs)

    def body(*refs):
        vals = fn(*[r[...] for r in refs[:nin]])
        if not isinstance(vals, (tuple, list)):
            vals = (vals,)
        for k in range(no):
            refs[nin + k][...] = vals[k].astype(refs[nin + k].dtype)
        first = pl.program_id(0) == 0
        for k in range(na):
            ref, val = refs[nin + no + k], vals[no + k]

            @pl.when(first)
            def _(ref=ref, val=val):
                ref[...] = val

            @pl.when(jnp.logical_not(first))
            def _(ref=ref, val=val):
                ref[...] += val

    in_specs = [pl.BlockSpec((tr, w), lambda i, cb=cb: (i, cb)) for (_, w, cb) in rows]
    in_specs += [pl.BlockSpec(c.shape, lambda i, nd=c.ndim: (0,) * nd) for c in consts]
    out_specs = [pl.BlockSpec((tr, w), lambda i: (i, 0)) for (w, _) in row_outs]
    out_specs += [pl.BlockSpec(s, lambda i, nd=len(s): (0,) * nd) for (s, _) in acc_outs]
    out_shape = [SDS((t, w), dt) for (w, dt) in row_outs] + [SDS(s, dt) for (s, dt) in acc_outs]
    res = pl.pallas_call(
        body, grid=(t // tr,), in_specs=in_specs, out_specs=out_specs, out_shape=out_shape,
        name=name, compiler_params=_cparams(("arbitrary",)))(*[r[0] for r in rows], *consts)
    return res


def _rstd(x):
    return lax.rsqrt(jnp.mean(x * x, axis=-1, keepdims=True) + RMS_EPS)


def _norm_fwd(x, g):
    return x * _rstd(x) * g


def _norm_bwd(u, dy, g):
    r = _rstd(u)
    n = u * r
    dn = dy * g
    du = r * (dn - n * jnp.mean(dn * n, axis=-1, keepdims=True))
    return du, jnp.sum(dy * n, axis=0, keepdims=True)


def _gelu(x):
    c = 0.7978845608028654
    return 0.5 * x * (1.0 + jnp.tanh(c * (x + 0.044715 * x * x * x)))


def _gelu_grad(x):
    c = 0.7978845608028654
    th = jnp.tanh(c * (x + 0.044715 * x * x * x))
    return 0.5 * (1.0 + th) + 0.5 * x * (1.0 - th * th) * c * (1.0 + 3.0 * 0.044715 * x * x)


def _mask_heads(x):
    lane = lax.broadcasted_iota(jnp.int32, x.shape, 1)
    return [jnp.where((lane >= h * HEAD_DIM) & (lane < (h + 1) * HEAD_DIM), x, jnp.zeros_like(x))
            for h in range(LANES // HEAD_DIM)]


def _chunk_valid(start):
    qi = lax.broadcasted_iota(jnp.int32, (QB_A, KW_A), 0)
    kj = lax.broadcasted_iota(jnp.int32, (QB_A, KW_A), 1)
    qc = qi // CHUNK
    kc = kj // CHUNK
    return (kc >= qc) & (kc <= qc + N_LEFT) & (kj + start >= PAD_A)


def _chunk_probs(q, k, bias, valid):
    s = lax.dot_general(q, k, (NT, ((), ())), preferred_element_type=f32) * (HEAD_DIM ** -0.5) + bias
    s = jnp.where(valid, s, -1e30)
    p = jnp.exp(s - jnp.max(s, axis=-1, keepdims=True))
    return p / jnp.sum(p, axis=-1, keepdims=True)


def _chunk_attn_fwd(proj, kpad, vpad, bias, gather):
    t = proj.shape[0]
    tp = kpad.shape[0]
    step = QSUB_A * QB_A

    def body(q_ref, k_ref, v_ref, b_ref, o_ref):
        for sb in range(QSUB_A):
            start = pl.multiple_of((pl.program_id(1) * QSUB_A + sb) * QB_A, QB_A)
            rows = pl.ds(sb * QB_A, QB_A)
            valid = _chunk_valid(start)
            kw = k_ref[pl.ds(start, KW_A), :]
            qm = _mask_heads(q_ref[rows, :])
            vm = _mask_heads(v_ref[pl.ds(start, KW_A), :])
            o = None
            for h in range(len(qm)):
                p = _chunk_probs(qm[h], kw, b_ref[h], valid)
                d = jnp.dot(p.astype(bf16), vm[h], preferred_element_type=f32)
                o = d if o is None else o + d
            o_ref[rows, :] = o.astype(bf16)

    kv_spec = pl.BlockSpec((tp, LANES), lambda hp, qb: (0, hp))
    outs, new = _call(
        body, [proj, kpad, vpad, bias], grid=(A_W // LANES, t // step), name="chunk_attn_fwd",
        in_specs=[pl.BlockSpec((step, LANES), lambda hp, qb: (qb, hp)), kv_spec, kv_spec,
                  pl.BlockSpec((2, QB_A, KW_A), lambda hp, qb: (hp, 0, 0))],
        out_specs=[pl.BlockSpec((step, LANES), lambda hp, qb: (qb, hp))],
        out_shape=[SDS((t, A_W), bf16)], sem=("parallel", "arbitrary"), gather=gather)
    return outs[0], new


def _chunk_attn_bwd(proj, kpad, vpad, bias, dout, gather):
    t = proj.shape[0]
    tp = kpad.shape[0]
    step = QSUB_A * QB_A

    def body(q_ref, k_ref, v_ref, b_ref, do_ref, dq_ref, dk_ref, dv_ref, db_ref):
        qb = pl.program_id(1)

        @pl.when(qb == 0)
        def _():
            dk_ref[...] = jnp.zeros_like(dk_ref)
            dv_ref[...] = jnp.zeros_like(dv_ref)
            db_ref[...] = jnp.zeros_like(db_ref)

        for sb in range(QSUB_A):
            start = pl.multiple_of((qb * QSUB_A + sb) * QB_A, QB_A)
            rows = pl.ds(sb * QB_A, QB_A)
            win = pl.ds(start, KW_A)
            valid = _chunk_valid(start)
            kw = k_ref[win, :]
            vw = v_ref[win, :]
            qm = _mask_heads(q_ref[rows, :])
            dom = _mask_heads(do_ref[rows, :])
            km = _mask_heads(kw)
            dq = dk = dv = None
            for h in range(len(qm)):
                p = _chunk_probs(qm[h], kw, b_ref[h], valid)
                dp = lax.dot_general(dom[h], vw, (NT, ((), ())), preferred_element_type=f32)
                ds = p * (dp - jnp.sum(dp * p, axis=-1, keepdims=True))
                db_ref[h] += ds
                dsb = (ds * (HEAD_DIM ** -0.5)).astype(bf16)
                terms = (jnp.dot(dsb, km[h], preferred_element_type=f32),
                         lax.dot_general(dsb, qm[h], (TN, ((), ())), preferred_element_type=f32),
                         lax.dot_general(p.astype(bf16), dom[h], (TN, ((), ())), preferred_element_type=f32))
                dq, dk, dv = terms if dq is None else (dq + terms[0], dk + terms[1], dv + terms[2])
            dq_ref[rows, :] = dq.astype(bf16)
            dk_ref[win, :] += dk
            dv_ref[win, :] += dv

    kv_spec = pl.BlockSpec((tp, LANES), lambda hp, qb: (0, hp))
    q_spec = pl.BlockSpec((step, LANES), lambda hp, qb: (qb, hp))
    b_spec = pl.BlockSpec((2, QB_A, KW_A), lambda hp, qb: (hp, 0, 0))
    return _call(
        body, [proj, kpad, vpad, bias, dout], grid=(A_W // LANES, t // step), name="chunk_attn_bwd",
        in_specs=[q_spec, kv_spec, kv_spec, b_spec, q_spec],
        out_specs=[q_spec, kv_spec, kv_spec, b_spec],
        out_shape=[SDS((t, A_W), bf16), SDS((tp, A_W), f32), SDS((tp, A_W), f32),
                   SDS((2 * A_W // LANES, QB_A, KW_A), f32)],
        sem=("parallel", "arbitrary"), gather=gather)


def _bias_ext(table):
    flat = PAD_A + QB_A - 1 - REL_CLIP
    top = jnp.broadcast_to(table[:, 2 * REL_CLIP:], (table.shape[0], flat))
    lo = 2 * REL_CLIP - (EXT_A - 1 - flat)
    return jnp.concatenate([top, jnp.flip(table[:, lo:], axis=1)], axis=1)


def _bias_window(table):
    nh = table.shape[0]
    e = jnp.broadcast_to(_bias_ext(table)[:, None, :], (nh, QB_A, EXT_A)).reshape(nh, QB_A * EXT_A)
    m = e[:, :QB_A * (EXT_A - 1)].reshape(nh, QB_A, EXT_A - 1)
    return m[:, :, QB_A - 1:]


def _bias_window_grad(dbias):
    nh = dbias.shape[0]
    m = jnp.pad(dbias, ((0, 0), (0, 0), (QB_A - 1, 0))).reshape(nh, QB_A * (EXT_A - 1))
    dext = jnp.sum(jnp.pad(m, ((0, 0), (0, QB_A))).reshape(nh, QB_A, EXT_A), axis=1)
    flat = PAD_A + QB_A - 1 - REL_CLIP
    lo = 2 * REL_CLIP - (EXT_A - 1 - flat)
    tail = jnp.flip(dext[:, flat:], axis=1)
    tail = tail.at[:, -1].add(jnp.sum(dext[:, :flat], axis=1))
    return jnp.pad(tail, ((0, 0), (lo, 0)))


def _tri_suffix(x, tri):
    hi = x.astype(bf16)
    lo = (x - hi.astype(f32)).astype(bf16)
    return jnp.dot(hi, tri, preferred_element_type=f32) + jnp.dot(lo, tri, preferred_element_type=f32)


def _sb_block(q, k, run, tri, causal):
    z = lax.dot_general(q, k, (NT, ((), ())), preferred_element_type=f32) * (HEAD_DIM ** -0.5)
    e = jnp.exp(-jnp.abs(z))
    l1p = jnp.log(1.0 + e)
    lb = jnp.minimum(z, 0.0) - l1p
    lmb = lb - z
    if causal is not None:
        lmb = jnp.where(causal, lmb, 0.0)
    cs = _tri_suffix(lmb, tri)
    w = jnp.exp(lb + (run + cs - lmb))
    if causal is not None:
        w = jnp.where(causal, w, 0.0)
    return z, e, w, run + cs[:, 0:1]


def _sb_tri():
    r = lax.broadcasted_iota(jnp.int32, (SB_BLK, SB_BLK), 0)
    c = lax.broadcasted_iota(jnp.int32, (SB_BLK, SB_BLK), 1)
    return (r >= c).astype(bf16), c < r


def _sb_live(runs):
    m = runs[0]
    for r in runs[1:]:
        m = jnp.maximum(m, r)
    return jnp.max(m) > SB_DEAD


def _sb_fwd(proj, gather):
    t = proj.shape[0]
    cb = A_W // LANES
    nh = LANES // HEAD_DIM

    step_rows = QSUB_B * SB_BLK

    def body(q_ref, k_ref, v_ref, o_ref, of_ref):
        tri, diag = _sb_tri()
        for sb in range(QSUB_B):
            _sb_fwd_block(pl.program_id(1) * QSUB_B + sb, pl.ds(sb * SB_BLK, SB_BLK), tri, diag,
                          q_ref, k_ref, v_ref, o_ref, of_ref)

    def _sb_fwd_block(qb, qrows, tri, diag, q_ref, k_ref, v_ref, o_ref, of_ref):
        qm = _mask_heads(q_ref[qrows, :])

        def pair(kb, carry, causal):
            rows = pl.ds(pl.multiple_of(kb * SB_BLK, SB_BLK), SB_BLK)
            k = k_ref[rows, :]
            vm = _mask_heads(v_ref[rows, :])
            runs, acc = [], carry[nh]
            for h in range(nh):
                _, _, w, run = _sb_block(qm[h], k, carry[h], tri, causal)
                acc = acc + jnp.dot(w.astype(bf16), vm[h], preferred_element_type=f32)
                runs.append(run)
            return (*runs, acc)

        zero = jnp.zeros((SB_BLK, 1), f32)
        carry = pair(qb, (zero,) * nh + (jnp.zeros((SB_BLK, LANES), f32),), diag)

        def cond(st):
            return (st[0] < qb) & _sb_live(st[1][:nh])

        def step(st):
            return st[0] + 1, pair(qb - 1 - st[0], st[1], None)

        _, carry = lax.while_loop(cond, step, (jnp.int32(0), carry))
        o_ref[qrows, :] = carry[nh].astype(bf16)
        of_ref[qrows, :] = carry[nh]

    ospec = pl.BlockSpec((step_rows, LANES), lambda hp, qb: (qb, hp))
    return _call(
        body, [proj, proj, proj], grid=(cb, t // step_rows), name="sb_attn_fwd",
        in_specs=[pl.BlockSpec((step_rows, LANES), lambda hp, qb: (qb, 3 * cb + hp)),
                  pl.BlockSpec((t, LANES), lambda hp, qb: (0, 4 * cb + hp)),
                  pl.BlockSpec((t, LANES), lambda hp, qb: (0, 5 * cb + hp))],
        out_specs=[ospec, ospec], out_shape=[SDS((t, A_W), bf16), SDS((t, A_W), f32)],
        sem=("parallel", "arbitrary"), gather=gather)


def _sb_bwd(proj, out_b, dout, gather):
    t = proj.shape[0]
    cb = A_W // LANES
    nh = LANES // HEAD_DIM

    step_rows = QSUB_B * SB_BLK

    def body(q_ref, k_ref, v_ref, o_ref, do_ref, dq_ref, dk_ref, dv_ref):
        tri, diag = _sb_tri()

        @pl.when(pl.program_id(1) == 0)
        def _():
            dk_ref[...] = jnp.zeros_like(dk_ref)
            dv_ref[...] = jnp.zeros_like(dv_ref)

        for sb in range(QSUB_B):
            _sb_bwd_block(pl.program_id(1) * QSUB_B + sb, pl.ds(sb * SB_BLK, SB_BLK), tri, diag,
                          q_ref, k_ref, v_ref, o_ref, do_ref, dq_ref, dk_ref, dv_ref)

    def _sb_bwd_block(qb, qrows, tri, diag, q_ref, k_ref, v_ref, o_ref, do_ref, dq_ref, dk_ref, dv_ref):
        qm = _mask_heads(q_ref[qrows, :])
        do = do_ref[qrows, :]
        dom = _mask_heads(do)
        dsums = [jnp.sum(t_, axis=-1, keepdims=True) for t_ in _mask_heads(do.astype(f32) * o_ref[qrows, :])]

        def pair(kb, carry, causal):
            rows = pl.ds(pl.multiple_of(kb * SB_BLK, SB_BLK), SB_BLK)
            k = k_ref[rows, :]
            v = v_ref[rows, :]
            km = _mask_heads(k)
            new, dq, dk, dv = [], carry[2 * nh], None, None
            for h in range(nh):
                z, e, w, run = _sb_block(qm[h], k, carry[2 * h], tri, causal)
                inv = 1.0 / (1.0 + e)
                beta = jnp.where(z >= 0.0, inv, e * inv)
                wb = w.astype(bf16)
                g = lax.dot_general(dom[h], v, (NT, ((), ())), preferred_element_type=f32) * wb.astype(f32)
                sg = _tri_suffix(g, tri)
                dz = g * (1.0 - beta) - (dsums[h] - carry[2 * h + 1] - sg) * beta
                if causal is not None:
                    dz = jnp.where(causal, dz, 0.0)
                dzb = (dz * (HEAD_DIM ** -0.5)).astype(bf16)
                dq = dq + jnp.dot(dzb, km[h], preferred_element_type=f32)
                tk = lax.dot_general(dzb, qm[h], (TN, ((), ())), preferred_element_type=f32)
                tv = lax.dot_general(wb, dom[h], (TN, ((), ())), preferred_element_type=f32)
                dk, dv = (tk, tv) if dk is None else (dk + tk, dv + tv)
                new += [run, carry[2 * h + 1] + sg[:, 0:1]]
            dk_ref[rows, :] += dk
            dv_ref[rows, :] += dv
            return (*new, dq)

        zero = jnp.zeros((SB_BLK, 1), f32)
        carry = pair(qb, (zero,) * (2 * nh) + (jnp.zeros((SB_BLK, LANES), f32),), diag)

        def cond(st):
            return (st[0] < qb) & _sb_live(st[1][0:2 * nh:2])

        def step(st):
            return st[0] + 1, pair(qb - 1 - st[0], st[1], None)

        _, carry = lax.while_loop(cond, step, (jnp.int32(0), carry))
        dq_ref[qrows, :] = carry[2 * nh].astype(bf16)

    kv_in = lambda seg: pl.BlockSpec((t, LANES), lambda hp, qb: (0, seg * cb + hp))
    q_spec = pl.BlockSpec((step_rows, LANES), lambda hp, qb: (qb, hp))
    kv_out = pl.BlockSpec((t, LANES), lambda hp, qb: (0, hp))
    return _call(
        body, [proj, proj, proj, out_b, dout], grid=(cb, t // step_rows), name="sb_attn_bwd",
        in_specs=[pl.BlockSpec((step_rows, LANES), lambda hp, qb: (qb, 3 * cb + hp)), kv_in(4), kv_in(5),
                  q_spec, pl.BlockSpec((step_rows, LANES), lambda hp, qb: (qb, cb + hp))],
        out_specs=[q_spec, kv_out, kv_out],
        out_shape=[SDS((t, A_W), bf16), SDS((t, A_W), f32), SDS((t, A_W), f32)],
        sem=("parallel", "arbitrary"), gather=gather)


def _halo_specs(tr, w, col, nblk):
    per = tr // SUBLANES
    cur = pl.BlockSpec((tr, w), lambda i: (i, col))
    prev = pl.BlockSpec((SUBLANES, w), lambda i: (jnp.maximum(i * per - 1, 0), col))
    nxt = pl.BlockSpec((SUBLANES, w), lambda i: (jnp.minimum((i + 1) * per, nblk * per - 1), col))
    return cur, prev, nxt


def _taps_before(cur, prev8, first):
    prev8 = jnp.where(first, 0.0, prev8)
    ext = jnp.concatenate([prev8, cur], axis=0)
    return [pltpu.roll(ext, s, 0)[SUBLANES:] for s in (3, 2, 1)]


def _taps_after(cur, next8, last):
    n = cur.shape[0]
    next8 = jnp.where(last, 0.0, next8)
    ext = jnp.concatenate([cur, next8], axis=0)
    return [pltpu.roll(ext, n + SUBLANES - s, 0)[:n] for s in (1, 2, 3)]


def _block_diag(x, w_ref, dims):
    outs = [lax.dot_general(x[:, n * LRU_BW:(n + 1) * LRU_BW], w_ref[n], (dims, ((), ())),
                            preferred_element_type=f32) for n in range(LRU_BLOCKS)]
    return jnp.concatenate(outs, axis=1)


def _lru_gates(xc, wa_ref, wi_ref, ba, bi, lam):
    xb = xc.astype(bf16)
    r = jax.nn.sigmoid(_block_diag(xb, wa_ref, NN) + ba)
    ig = jax.nn.sigmoid(_block_diag(xb, wi_ref, NN) + bi)
    sp = jnp.maximum(-lam, 0.0) + jnp.log(1.0 + jnp.exp(-jnp.abs(lam)))
    log_a = -LRU_C * r * sp
    a = jnp.exp(log_a)
    x2 = 2.0 * log_a
    one_minus = jnp.where(x2 > -1e-2, -x2 * (1.0 + x2 * (0.5 + x2 * (1.0 / 6.0))), 1.0 - a * a)
    mult = jnp.sqrt(one_minus)
    return xb, r, ig, sp, a, mult


def _rg_gates_fwd(proj, conv_w, conv_b, wa, wi, ba, bi, lam, tr=512):
    t = proj.shape[0]
    w = D_MODEL
    tr = min(tr, t)
    nblk = t // tr
    cur, prev, _ = _halo_specs(tr, w, 1, nblk)

    def body(x_ref, xp_ref, cw_ref, cb_ref, wa_ref, wi_ref, ba_ref, bi_ref, lam_ref, xc_ref, a_ref, u_ref):
        x = x_ref[...]
        taps = _taps_before(x, xp_ref[...], pl.program_id(0) == 0) + [x]
        xc = cb_ref[...]
        for k in range(4):
            xc = xc + cw_ref[k:k + 1, :] * taps[k]
        _, _, ig, _, a, mult = _lru_gates(xc, wa_ref, wi_ref, ba_ref[...], bi_ref[...], lam_ref[...])
        xc_ref[...] = xc
        a_ref[...] = a
        u_ref[...] = mult * (ig * xc)

    full = lambda a_: pl.BlockSpec(a_.shape, lambda i, nd=a_.ndim: (0,) * nd)
    ospec = pl.BlockSpec((tr, w), lambda i: (i, 0))
    return pl.pallas_call(
        body, grid=(nblk,), name="rg_gates_fwd",
        in_specs=[cur, prev] + [full(a_) for a_ in (conv_w, conv_b, wa, wi, ba, bi, lam)],
        out_specs=[ospec] * 3, out_shape=[SDS((t, w), f32)] * 3,
        compiler_params=_cparams(("parallel",)))(proj, proj, conv_w, conv_b, wa, wi, ba, bi, lam)


def _lru_scan(name, a, b, reverse, tt=1024):
    t, w = a.shape
    tt = min(tt, t)
    nt = t // tt
    ng = tt // SUBLANES

    def body(a_ref, b_ref, h_ref, carry_ref):
        @pl.when(pl.program_id(0) == 0)
        def _():
            carry_ref[...] = jnp.zeros_like(carry_ref)

        row = lax.broadcasted_iota(jnp.int32, (SUBLANES, w), 0)

        def group(gi, carry):
            g = (ng - 1 - gi) if reverse else gi
            rows = pl.ds(pl.multiple_of(g * SUBLANES, SUBLANES), SUBLANES)
            av = a_ref[rows, :]
            bv = b_ref[rows, :]
            for s in (1, 2, 4):
                sh = (SUBLANES - s) if reverse else s
                ok = (row < SUBLANES - s) if reverse else (row >= s)
                a_s = pltpu.roll(av, sh, 0)
                b_s = pltpu.roll(bv, sh, 0)
                bv = jnp.where(ok, av * b_s + bv, bv)
                av = jnp.where(ok, av * a_s, av)
            h = av * carry + bv
            h_ref[rows, :] = h
            edge = h[0:1, :] if reverse else h[SUBLANES - 1:SUBLANES, :]
            return jnp.broadcast_to(edge, (SUBLANES, w))

        carry_ref[...] = lax.fori_loop(0, ng, group, carry_ref[...], unroll=4)

    tmap = (lambda i: (nt - 1 - i, 0)) if reverse else (lambda i: (i, 0))
    spec = pl.BlockSpec((tt, w), tmap)
    return pl.pallas_call(
        body, grid=(nt,), name=name, in_specs=[spec, spec], out_specs=spec,
        out_shape=SDS((t, w), f32), scratch_shapes=[pltpu.VMEM((SUBLANES, w), f32)],
        compiler_params=_cparams(("arbitrary",)))(a, b)


def _rg_gates_bwd(dhs, c, hs, xc, wa, wi, ba, bi, lam, tr=512):
    t, w = xc.shape
    tr = min(tr, t)
    nblk = t // tr
    cur, prev, nxt = _halo_specs(tr, w, 0, nblk)

    def body(dhs_ref, c_ref, cn_ref, hs_ref, hp_ref, xc_ref, wa_ref, wi_ref, ba_ref, bi_ref, lam_ref,
             dxc_ref, dwa_ref, dwi_ref, dba_ref, dbi_ref, dlam_ref):
        i = pl.program_id(0)
        c_next = _taps_after(c_ref[...], cn_ref[...], i == nblk - 1)[0]
        h_prev = _taps_before(hs_ref[...], hp_ref[...], i == 0)[2]
        xc = xc_ref[...]
        lam = lam_ref[...]
        xb, r, ig, sp, a, mult = _lru_gates(xc, wa_ref, wi_ref, ba_ref[...], bi_ref[...], lam)
        dh = dhs_ref[...] + c_next
        dlog_a = dh * h_prev * a - (dh * ig * xc) * (a * a / mult)
        dpre_a = (dlog_a * (-LRU_C * sp) * r * (1.0 - r)).astype(bf16)
        dpre_i = (dh * mult * xc * ig * (1.0 - ig)).astype(bf16)
        dxc_ref[...] = (dh * mult * ig + _block_diag(dpre_a, wa_ref, NT) + _block_diag(dpre_i, wi_ref, NT))
        dsig = 1.0 / (1.0 + jnp.exp(lam))
        sums = [jnp.sum(dpre_a.astype(f32), axis=0, keepdims=True),
                jnp.sum(dpre_i.astype(f32), axis=0, keepdims=True),
                jnp.sum(dlog_a * (-LRU_C * r), axis=0, keepdims=True) * (-dsig)]

        @pl.when(i == 0)
        def _():
            dwa_ref[...] = jnp.zeros_like(dwa_ref)
            dwi_ref[...] = jnp.zeros_like(dwi_ref)
            dba_ref[...] = jnp.zeros_like(dba_ref)
            dbi_ref[...] = jnp.zeros_like(dbi_ref)
            dlam_ref[...] = jnp.zeros_like(dlam_ref)

        for n in range(LRU_BLOCKS):
            sl = slice(n * LRU_BW, (n + 1) * LRU_BW)
            dwa_ref[n] += lax.dot_general(xb[:, sl], dpre_a[:, sl], (TN, ((), ())), preferred_element_type=f32)
            dwi_ref[n] += lax.dot_general(xb[:, sl], dpre_i[:, sl], (TN, ((), ())), preferred_element_type=f32)
        dba_ref[...] += sums[0]
        dbi_ref[...] += sums[1]
        dlam_ref[...] += sums[2]

    full = lambda a_: pl.BlockSpec(a_.shape, lambda i, nd=a_.ndim: (0,) * nd)
    vec = pl.BlockSpec((1, w), lambda i: (0, 0))
    mat = pl.BlockSpec((LRU_BLOCKS, LRU_BW, LRU_BW), lambda i: (0, 0, 0))
    return pl.pallas_call(
        body, grid=(nblk,), name="rg_gates_bwd",
        in_specs=[cur, cur, nxt, cur, prev, cur] + [full(a_) for a_ in (wa, wi, ba, bi, lam)],
        out_specs=[cur, mat, mat, vec, vec, vec],
        out_shape=[SDS((t, w), f32), SDS((LRU_BLOCKS, LRU_BW, LRU_BW), f32), SDS((LRU_BLOCKS, LRU_BW, LRU_BW), f32),
                   SDS((1, w), f32), SDS((1, w), f32), SDS((1, w), f32)],
        compiler_params=_cparams(("arbitrary",)))(dhs, c, c, hs, hs, xc, wa, wi, ba, bi, lam)


def _rg_conv_bwd(dxc, proj, conv_w, tr=512):
    t, w = dxc.shape
    tr = min(tr, t)
    nblk = t // tr
    cur, _, nxt = _halo_specs(tr, w, 0, nblk)
    xcur, xprev, _ = _halo_specs(tr, w, 1, nblk)

    def body(d_ref, dn_ref, x_ref, xp_ref, cw_ref, dx_ref, dcw_ref, dcb_ref):
        i = pl.program_id(0)
        d = d_ref[...]
        x = x_ref[...]
        after = _taps_after(d, dn_ref[...], i == nblk - 1)
        before = _taps_before(x, xp_ref[...], i == 0) + [x]
        dx = cw_ref[3:4, :] * d
        for s in (1, 2, 3):
            dx = dx + cw_ref[3 - s:4 - s, :] * after[s - 1]
        dx_ref[...] = dx.astype(bf16)
        dcw = jnp.concatenate([jnp.sum(d * before[k], axis=0, keepdims=True) for k in range(4)], axis=0)
        dcb = jnp.sum(d, axis=0, keepdims=True)

        @pl.when(i == 0)
        def _():
            dcw_ref[...] = dcw
            dcb_ref[...] = dcb

        @pl.when(i > 0)
        def _():
            dcw_ref[...] += dcw
            dcb_ref[...] += dcb

    return pl.pallas_call(
        body, grid=(nblk,), name="rg_conv_bwd",
        in_specs=[cur, nxt, xcur, xprev, pl.BlockSpec((4, w), lambda i: (0, 0))],
        out_specs=[cur, pl.BlockSpec((4, w), lambda i: (0, 0)), pl.BlockSpec((1, w), lambda i: (0, 0))],
        out_shape=[SDS((t, w), bf16), SDS((4, w), f32), SDS((1, w), f32)],
        compiler_params=_cparams(("arbitrary",)))(dxc, dxc, proj, proj, conv_w)


def _attn_fwd(h, wts, j, plan):
    proj = _mm_cols("attn_in", h, wts["attn_w_in"], j, bf16)
    kpad = jnp.pad(proj[:, A_W:2 * A_W], ((PAD_A, 0), (0, 0)))
    vpad = jnp.pad(proj[:, 2 * A_W:3 * A_W], ((PAD_A, 0), (0, 0)))
    bias = _bias_window(wts["attn_rel_bias"][j])
    plan = plan if j == 0 else None
    out_a = _carried(plan, "chunk_attn_fwd", wts, _chunk_attn_fwd, proj, kpad, vpad, bias)
    out_b, out_b32 = _carried(plan, "sb_attn_fwd", wts, _sb_fwd, proj)
    m = _mm_rows("attn_out", [out_a, out_b], wts["attn_w_out"], j, f32)
    return m, (proj, kpad, vpad, bias, out_a, out_b, out_b32)


def _attn_bwd(dm, h, saved, wts, j, grads, exch):
    proj, kpad, vpad, bias, out_a, out_b, out_b32 = saved
    dout = _mm_rows_t("attn_out_t", dm, wts["attn_w_out"], j, bf16)
    gi, ll = _grad_slot("attn_w_out", j)
    grads["attn_w_out"][gi] = _mm_wgrad("attn_out_wgrad_a", out_a, dm, grads["attn_w_out"][gi], ll, 0)
    grads["attn_w_out"][gi] = _mm_wgrad("attn_out_wgrad_b", out_b, dm, grads["attn_w_out"][gi], ll, 1)
    if exch is not None and j == 0:
        (dqa, dka, dva, dbias), got = _chunk_attn_bwd(proj, kpad, vpad, bias, dout, exch.upper_carry(grads))
        exch.upper_got(got)
        (dqs, dks, dvs), slots = _sb_bwd(proj, out_b32, dout, exch.carry())
        exch.carried(slots)
    else:
        dqa, dka, dva, dbias = _chunk_attn_bwd(proj, kpad, vpad, bias, dout, None)[0]
        dqs, dks, dvs = _sb_bwd(proj, out_b32, dout, None)[0]
    grads["attn_rel_bias"][j] = _bias_window_grad(dbias)
    dproj = jnp.concatenate([dqa, dka[PAD_A:].astype(bf16), dva[PAD_A:].astype(bf16),
                             dqs, dks.astype(bf16), dvs.astype(bf16)], axis=1)
    grads["attn_w_in"][gi] = _mm_wgrad_cols("attn_in_wgrad", h, dproj, grads["attn_w_in"][gi], ll)
    return _mm_cols_t("attn_in_t", dproj, wts["attn_w_in"], j, f32)


def _rg_fwd(h, wts, j, plan):
    proj =_mm_cols("rg_in", h, wts["rg_w_in"], j, f32)
    small = [wts[k][j] for k in ("rg_conv_w", "rg_conv_b", "rg_w_a", "rg_w_i", "rg_b_a", "rg_b_i", "rg_lambda")]
    xc, a, u = _rg_gates_fwd(proj, *small)
    hs = _lru_scan("lru_scan_fwd", a, u, False)
    yp = _rows("rg_gate_out", lambda hv, gv: hv * _gelu(gv), [hs, (proj, D_MODEL, 0)], [], [(D_MODEL, bf16)])[0]
    m = _mm_rows("rg_out", [yp], wts["rg_w_out"], j, f32)
    return m, (proj, xc, a, hs, yp)


def _rg_bwd(dm, h, saved, wts, j, grads, exch):
    proj, xc, a, hs, yp = saved
    dyp = _mm_rows_t("rg_out_t", dm, wts["rg_w_out"], j, f32)
    gi, ll = _grad_slot("rg_w_out", j)
    grads["rg_w_out"][gi] = _mm_wgrad("rg_out_wgrad", yp, dm, grads["rg_w_out"][gi], ll)

    def gate_bwd(dy, hv, gv, av):
        dhs = dy * _gelu(gv)
        return dhs, av * dhs, dy * hv * _gelu_grad(gv)

    dhs, ab, dgate = _rows("rg_gate_out_bwd", gate_bwd, [dyp, hs, (proj, D_MODEL, 0), a], [],
                           [(D_MODEL, f32), (D_MODEL, f32), (D_MODEL, bf16)])
    c = _lru_scan("lru_scan_bwd", a, ab, True)
    wa, wi, ba, bi, lam = [wts[k][j] for k in ("rg_w_a", "rg_w_i", "rg_b_a", "rg_b_i", "rg_lambda")]
    dxc, dwa, dwi, dba, dbi, dlam = _rg_gates_bwd(dhs, c, hs, xc, wa, wi, ba, bi, lam)
    dxr, dcw, dcb = _rg_conv_bwd(dxc, proj, wts["rg_conv_w"][j])
    for k, v in (("rg_w_a", dwa), ("rg_w_i", dwi), ("rg_b_a", dba), ("rg_b_i", dbi), ("rg_lambda", dlam),
                 ("rg_conv_w", dcw), ("rg_conv_b", dcb)):
        grads[k][j] = v
    dproj = jnp.concatenate([dgate, dxr], axis=1)
    grads["rg_w_in"][gi] = _mm_wgrad_cols("rg_in_wgrad", h, dproj, grads["rg_w_in"][gi], ll)
    return _mm_cols_t("rg_in_t", dproj, wts["rg_w_in"], j, f32)


def _local_step(x, target, wts, plan=None, exch=None):
    t = x.shape[0]
    d = D_MODEL
    gains = {k: wts[k] for k in ("norm_mix_pre", "norm_mix_post", "norm_ffn_pre", "norm_ffn_post")}
    gain = lambda k, l: gains[k][l:l + 1]

    saved = []
    h = _rows("norm_in", _norm_fwd, [x], [gain("norm_mix_pre", 0)], [(d, bf16)])[0]
    loss_cols = None
    for l in range(DEPTH):
        j = l // 2
        m, mix_saved = (_attn_fwd if l % 2 == 0 else _rg_fwd)(h, wts, j, plan)

        def resid_next(xv, mv, g_post, g_next):
            x1 = xv + _norm_fwd(mv, g_post)
            return x1, _norm_fwd(x1, g_next)

        x1, h2 = _rows("resid_mix", resid_next, [x, m], [gain("norm_mix_post", l), gain("norm_ffn_pre", l)],
                       [(d, f32), (d, bf16)])
        g, u, hid = _carried(plan if l == 0 else None, "ffn_up", wts, _ffn_up, h2, wts["ffn_w_gate"],
                             wts["ffn_w_up"], l)
        f = _ffn_down(hid, wts["ffn_w_down"], l)
        saved.append((x, h, m, mix_saved, x1, h2, g, u, hid, f))
        if l + 1 < DEPTH:
            x, h = _rows("resid_ffn", resid_next, [x1, f], [gain("norm_ffn_post", l), gain("norm_mix_pre", l + 1)],
                         [(d, f32), (d, bf16)])
        else:
            def resid_loss(xv, fv, tv, g_post):
                err = xv + _norm_fwd(fv, g_post) - tv
                return err * (1.0 / d), jnp.sum(err * err, axis=0, keepdims=True)

            dx, loss_cols = _rows("resid_loss", resid_loss, [x1, f, target], [gain("norm_ffn_post", l)],
                                  [(d, f32)], [((1, d), f32)])
    loss = 0.5 * jnp.sum(loss_cols) / d

    grads = {k: {} for k in SMALL_GRADS}
    for k in BIG_GRADS:
        shp = wts[k].shape
        rest = shp[2:] if shp[1] == 1 else shp[1:]
        grads[k] = [_Fresh((LOWER_LAYERS[k],) + rest), _Fresh((shp[0] - LOWER_LAYERS[k],) + rest)]

    def norm_bwd_cast(uv, dyv, gv):
        du, dg = _norm_bwd(uv, dyv, gv)
        return du, dg

    def norm_bwd_resid(uv, dhv, dxv, gv):
        du, dg = _norm_bwd(uv, dhv, gv)
        return dxv + du, dg

    def norm_bwd_pair(uv, dhv, dxv, nv, g_pre, g_post):
        dx_, dg_pre = norm_bwd_resid(uv, dhv, dxv, g_pre)
        dn, dg_post = _norm_bwd(nv, dx_, g_post)
        return dx_, dn, dg_pre, dg_post

    df = None
    for l in reversed(range(DEPTH)):
        j = l // 2
        x_in, h, m, mix_saved, x1, h2, g, u, hid, f = saved[l]
        if df is None:
            df, grads["norm_ffn_post"][l] = _rows("norm_ffn_post_bwd", norm_bwd_cast, [f, dx],
                                                  [gain("norm_ffn_post", l)], [(d, bf16)], [((1, d), f32)])
        dg, du = _ffn_down_bwd(df, wts["ffn_w_down"], l, g, u, None)[0]
        gi, ll = _grad_slot("ffn_w_down", l)
        grads["ffn_w_down"][gi] = _ffn_wgrad_down(hid, df, grads["ffn_w_down"][gi], ll)
        dh2 = _ffn_up_bwd(dg, du, wts["ffn_w_gate"], wts["ffn_w_up"], l)
        grads["ffn_w_gate"][gi], grads["ffn_w_up"][gi] = _ffn_wgrad_up(
            h2, dg, du, grads["ffn_w_gate"][gi], grads["ffn_w_up"][gi], ll)
        dx1, dm, grads["norm_ffn_pre"][l], grads["norm_mix_post"][l] = _rows(
            "norm_ffn_mix_bwd", norm_bwd_pair, [x1, dh2, dx, m], [gain("norm_ffn_pre", l), gain("norm_mix_post", l)],
            [(d, f32), (d, bf16)], [((1, d), f32), ((1, d), f32)])
        dh = (_attn_bwd if l % 2 == 0 else _rg_bwd)(dm, h, mix_saved, wts, j, grads, exch)
        if l > 0:
            dx, df, grads["norm_mix_pre"][l], grads["norm_ffn_post"][l - 1] = _rows(
                "norm_mix_ffn_bwd", norm_bwd_pair, [x_in, dh, dx1, saved[l - 1][9]],
                [gain("norm_mix_pre", l), gain("norm_ffn_post", l - 1)],
                [(d, f32), (d, bf16)], [((1, d), f32), ((1, d), f32)])
        else:
            dx, grads["norm_mix_pre"][l] = _rows("norm_mix_pre_bwd", norm_bwd_resid, [x_in, dh, dx1],
                                                 [gain("norm_mix_pre", l)], [(d, f32)], [((1, d), f32)])
    return loss, dx, grads


ANY = pl.BlockSpec(memory_space=pl.ANY)
PACK_COLS = 1024
SMALL_ROWS = 288


def _mesh_pos():
    x, y, c = lax.axis_index("x"), lax.axis_index("y"), lax.axis_index("c")
    return x, y, c, [(1 - x, y), (x, 1 - y), (1 - x, 1 - y)]


def _run_copies(copies):
    for cp in copies:
        cp.start()
    for cp in copies:
        cp.wait()


GATHER_SEMS = 7


def _gather_copies(items, ins, outs, send, recv):
    x, y, c, chips = _mesh_pos()
    q = 2 * x + y
    sibling = (x, y, 1 - c)

    def copy(k, src, dst, to):
        return pltpu.make_async_remote_copy(src_ref=src, dst_ref=dst, send_sem=send.at[k], recv_sem=recv.at[k],
                                            device_id=to, device_id_type=MESH)

    own, sent, passed = [], [], []
    for i, (t, l0, nl) in enumerate(items):
        lay = pl.ds(l0, nl)
        half = ins[t].shape[1] // 2
        rows = pl.ds(pl.multiple_of(c * half, half), half)
        own.append(copy(GATHER_SEMS * i, ins[t].at[lay], outs[t].at[lay, q], sibling))
        for j, (px, py) in enumerate(chips):
            sent.append(copy(GATHER_SEMS * i + 1 + j, ins[t].at[lay, rows], outs[t].at[lay, q, rows], (px, py, c)))
            landed = outs[t].at[lay, 2 * px + py, rows]
            passed.append(copy(GATHER_SEMS * i + 4 + j, landed, landed, sibling))
    return own, sent, passed


def _gather_start(items, ins, outs, send, recv):
    own, sent, _ = _gather_copies(items, ins, outs, send, recv)
    for cp in own + sent:
        cp.start()


def _gather_finish(items, ins, outs, send, recv):
    own, sent, passed = _gather_copies(items, ins, outs, send, recv)
    for arrived, forward in zip(sent, passed):
        arrived.wait_recv()
        forward.start()
    for cp in sent:
        cp.wait_send()
    for cp in own + passed:
        cp.wait()


def _gather_call(items, shards):
    n = len(shards)
    nsem = GATHER_SEMS * len(items)

    def body(*refs):
        ins, outs = refs[:n], refs[n:2 * n]
        _gather_start(items, ins, outs, *refs[2 * n:])
        _gather_finish(items, ins, outs, *refs[2 * n:])

    return pl.pallas_call(
        body, name="weight_all_gather", in_specs=[ANY] * n, out_specs=[ANY] * n,
        out_shape=[SDS((s.shape[0], N_CHIPS) + s.shape[1:], s.dtype) for s in shards],
        scratch_shapes=[pltpu.SemaphoreType.DMA((nsem,)), pltpu.SemaphoreType.DMA((nsem,))])(*shards)


def _call(body, operands, *, name, grid, in_specs, out_specs, out_shape, sem, scratch=(), gather=None):
    if gather is None:
        return pl.pallas_call(body, grid=grid, in_specs=in_specs, out_specs=out_specs, out_shape=out_shape,
                              scratch_shapes=list(scratch), name=name, compiler_params=_cparams(sem))(*operands), None
    start, finish, c_ins, c_io, c_new, nsem = gather
    n_in, n_out, n_scr = len(operands), len(out_shape), len(scratch)
    ni, nio, nco = len(c_ins), len(c_io), len(c_io) + len(c_new)

    def full(*refs):
        ins, sh = refs[:n_in], refs[n_in:n_in + ni]
        outs = refs[n_in + ni + nio:n_in + ni + nio + n_out]
        co = refs[n_in + ni + nio + n_out:n_in + ni + nio + n_out + nco]
        scr = refs[n_in + ni + nio + n_out + nco:]
        ids = [pl.program_id(a) for a in range(len(grid))]
        first = functools.reduce(jnp.logical_and, [i == 0 for i in ids])
        last = functools.reduce(jnp.logical_and, [i == g - 1 for i, g in zip(ids, grid)])

        @pl.when(first)
        def _():
            start(sh, co, scr[n_scr], scr[n_scr + 1])

        body(*ins, *outs, *scr[:n_scr])

        @pl.when(last)
        def _():
            finish(sh, co, scr[n_scr], scr[n_scr + 1])

    res = pl.pallas_call(
        full, grid=grid, in_specs=list(in_specs) + [ANY] * (ni + nio), out_specs=list(out_specs) + [ANY] * nco,
        out_shape=list(out_shape) + [SDS(g.shape, g.dtype) for g in list(c_io) + list(c_new)],
        scratch_shapes=list(scratch) + [pltpu.SemaphoreType.DMA((nsem,)), pltpu.SemaphoreType.DMA((nsem,))],
        input_output_aliases={n_in + ni + t: n_out + t for t in range(nio)}, name=name,
        compiler_params=_cparams(("arbitrary",) * len(grid)))(*operands, *c_ins, *c_io)
    return res[:n_out], res[n_out:]


def _pair_exchange(gs):
    n = len(gs)

    def body(*refs):
        _pair_copies(refs[:n], refs[n:2 * n], *refs[2 * n:], start=True)
        _pair_copies(refs[:n], refs[n:2 * n], *refs[2 * n:], start=False)

    return pl.pallas_call(
        body, name="grad_pair_exchange", in_specs=[ANY] * n, out_specs=[ANY] * n,
        out_shape=_pair_shapes(gs),
        scratch_shapes=[pltpu.SemaphoreType.DMA((n,)), pltpu.SemaphoreType.DMA((n,))])(*gs)


def _pair_shapes(gs):
    return [SDS(g.shape[:2] + (g.shape[2] // 2, g.shape[3]), f32) for g in gs]


def _pair_copies(ins, outs, send, recv, start):
    x, y, c, _ = _mesh_pos()
    for t in range(len(ins)):
        half = ins[t].shape[2] // 2
        src = ins[t].at[:, :, pl.ds(pl.multiple_of((1 - c) * half, SUBLANES), half)]
        cp = pltpu.make_async_remote_copy(src_ref=src, dst_ref=outs[t], send_sem=send.at[t], recv_sem=recv.at[t],
                                          device_id=(x, y, 1 - c), device_id_type=MESH)
        cp.start() if start else cp.wait()


def _pair_carry(gs):
    return (functools.partial(_pair_copies, start=True), functools.partial(_pair_copies, start=False),
            gs, [], _pair_shapes(gs), len(gs))


def _pair_sum(name, g, got, c):
    l, s, r, cols = g.shape

    def body(c_ref, a_ref, b_ref, o_ref):
        o_ref[...] = (a_ref[...] + b_ref[...]).astype(bf16)

    blk = (None, None, r // 2, cols)
    return pl.pallas_call(
        body, name=name, out_shape=SDS(got.shape, bf16),
        grid_spec=pltpu.PrefetchScalarGridSpec(
            num_scalar_prefetch=1, grid=(l, s),
            in_specs=[pl.BlockSpec(blk, lambda i, q, c_ref: (i, q, c_ref[0], 0)),
                      pl.BlockSpec(blk, lambda i, q, c_ref: (i, q, 0, 0))],
            out_specs=pl.BlockSpec(blk, lambda i, q, c_ref: (i, q, 0, 0))),
        compiler_params=_cparams(("parallel", "parallel")))(c, g, got)


def _chip_exchange(hs):
    n = len(hs)

    def body(*refs):
        _chip_copies(refs[:n], refs[n:2 * n], *refs[2 * n:], start=True)
        _chip_copies(refs[:n], refs[n:2 * n], *refs[2 * n:], start=False)

    return pl.pallas_call(
        body, name="grad_chip_exchange", in_specs=[ANY] * n, out_specs=[ANY] * n,
        out_shape=[SDS(h.shape, h.dtype) for h in hs],
        scratch_shapes=[pltpu.SemaphoreType.DMA((3 * n,)), pltpu.SemaphoreType.DMA((3 * n,))])(*hs)


def _chip_copies(ins, outs, send, recv, start):
    x, y, c, chips = _mesh_pos()
    q = 2 * x + y
    for t in range(len(ins)):
        for j, (px, py) in enumerate(chips):
            cp = pltpu.make_async_remote_copy(
                src_ref=ins[t].at[:, 2 * px + py], dst_ref=outs[t].at[:, q], send_sem=send.at[3 * t + j],
                recv_sem=recv.at[3 * t + j], device_id=(px, py, c), device_id_type=MESH)
            cp.start() if start else cp.wait()


def _chip_carry(hs):
    return (functools.partial(_chip_copies, start=True), functools.partial(_chip_copies, start=False),
            hs, [], [SDS(h.shape, h.dtype) for h in hs], 3 * len(hs))


def _chip_sum(name, s, h, pos, l0, layers, into):
    l, _, r, cols = s.shape

    def body(pos_ref, s0, s1, s2, s3, own_ref, *rest):
        vals = [jnp.where(pos_ref[0] == p, own_ref[...], ref[...]).astype(f32) for p, ref in enumerate((s0, s1, s2, s3))]
        rest[-1][...] = ((vals[0] + vals[1]) + vals[2]) + vals[3]

    blk = (None, None, r, cols)
    slot = lambda p: pl.BlockSpec(blk, lambda i, pos_ref: (i, jnp.where(pos_ref[0] == p, (p + 1) % N_CHIPS, p), 0, 0))
    extra, alias = ([], {}) if into is None else ([into], {6: 0})
    return pl.pallas_call(
        body, name=name, out_shape=SDS((layers, 2 * r, cols), f32), input_output_aliases=alias,
        grid_spec=pltpu.PrefetchScalarGridSpec(
            num_scalar_prefetch=1, grid=(l,),
            in_specs=[slot(p) for p in range(N_CHIPS)] + [pl.BlockSpec(blk, lambda i, pos_ref: (i, pos_ref[0], 0, 0))]
            + [ANY] * len(extra),
            out_specs=pl.BlockSpec((None, r, cols), lambda i, pos_ref: (l0 + i, pos_ref[1], 0))),
        compiler_params=_cparams(("parallel",)))(pos, s, s, s, s, h, *extra)


def _pair_gather(fulls):
    n = len(fulls)

    def body(*refs):
        ins, outs = refs[:n], refs[n:2 * n]
        send, recv = refs[2 * n:]
        x, y, c, _ = _mesh_pos()
        copies = []
        for t in range(n):
            half = outs[t].shape[1] // 2
            rows = outs[t].at[:, pl.ds(pl.multiple_of(c * half, SUBLANES), half)]
            copies.append(pltpu.make_async_remote_copy(
                src_ref=rows, dst_ref=rows, send_sem=send.at[t], recv_sem=recv.at[t],
                device_id=(x, y, 1 - c), device_id_type=MESH))
        _run_copies(copies)

    return pl.pallas_call(
        body, name="grad_pair_gather", in_specs=[ANY] * n, out_specs=[ANY] * n,
        out_shape=[SDS(f.shape, f32) for f in fulls], input_output_aliases={t: t for t in range(n)},
        scratch_shapes=[pltpu.SemaphoreType.DMA((n,)), pltpu.SemaphoreType.DMA((n,))])(*fulls)


COL_SHARDED = ("attn_w_in", "rg_w_in", "ffn_w_gate", "ffn_w_up")
ROW_SHARDED = ("attn_w_out", "rg_w_out")
GATES = ("rg_w_a", "rg_w_i")
VECTORS = ("rg_conv_w", "rg_conv_b", "rg_b_a", "rg_b_i", "rg_lambda")
REPLICATED = ("norm_mix_pre", "norm_mix_post", "norm_ffn_pre", "norm_ffn_post", "attn_rel_bias")
BIG_GRADS = COL_SHARDED + ROW_SHARDED + ("ffn_w_down",)
SMALL_GRADS = GATES + VECTORS + REPLICATED
WEIGHTS =("attn_w_in", "attn_rel_bias", "attn_w_out", "rg_w_in", "rg_conv_w", "rg_conv_b", "rg_w_a", "rg_b_a",
           "rg_w_i", "rg_b_i", "rg_lambda", "rg_w_out", "norm_mix_pre", "norm_mix_post", "norm_ffn_pre",
           "norm_ffn_post", "ffn_w_gate", "ffn_w_up", "ffn_w_down")
SMALL = VECTORS + REPLICATED


GATHER_PARTS = {
    "first": (("attn_w_in", 0, 1), ("attn_w_out", 0, 1), ("rg_w_a", 0, 8), ("rg_w_i", 0, 8), ("vec", 0, 1)),
    "chunk_attn_fwd": (("ffn_w_gate", 0, 1), ("ffn_w_up", 0, 1), ("ffn_w_down", 0, 1), ("rg_w_in", 0, 1),
                       ("rg_w_out", 0, 1)),
    "sb_attn_fwd": (("ffn_w_gate", 1, 3), ("ffn_w_up", 1, 3), ("ffn_w_down", 1, 3)),
    "ffn_up": (("rg_w_in", 1, 1), ("rg_w_out", 1, 1), ("attn_w_in", 1, 1), ("attn_w_out", 1, 1)),
}


TRANSPOSED = ("ffn_w_gate", "ffn_w_up")


def _natural(name, a):
    return jnp.swapaxes(a, 1, 2) if name in TRANSPOSED else a


class _WeightGather:
    def __init__(self, w):
        self.w = w
        self.names = list(COL_SHARDED + ROW_SHARDED + GATES + ("ffn_w_down", "vec"))
        self.shards = {}
        for k in self.names[:-1]:
            a = _natural(k, w[k]).astype(bf16)
            self.shards[k] = a.reshape((-1,) + a.shape[-2:])
        self.shards["vec"] = jnp.concatenate([w[k].reshape(-1) for k in VECTORS]).reshape(1, -1, LANES)
        got = _gather_call(self._items("first", self.names), [self.shards[k] for k in self.names])
        self.raw = dict(zip(self.names, got))

    @staticmethod
    def _items(part, names):
        return [(names.index(k), l0, nl) for k, l0, nl in GATHER_PARTS[part]]

    def part(self, part):
        names = list(dict.fromkeys(k for k, _, _ in GATHER_PARTS[part]))
        items = self._items(part, names)
        return (functools.partial(_gather_start, items), functools.partial(_gather_finish, items),
                [self.shards[k] for k in names], [self.raw[k] for k in names], [], GATHER_SEMS * len(items)), names

    def views(self):
        got, w = self.raw, self.w
        out = {k: w[k] for k in REPLICATED}
        for k in COL_SHARDED + ("ffn_w_down",):
            out[k] = got[k]
        for k in ROW_SHARDED:
            l, s, ks, n = got[k].shape
            out[k] = got[k].reshape(l, 1, s * ks, n)
        for k in GATES:
            out[k] = got[k].reshape(2, LRU_BLOCKS, LRU_BW, LRU_BW)
        vec = got["vec"].reshape(N_CHIPS, -1)
        off = 0
        for k in VECTORS:
            shp = w[k].shape
            n = int(np.prod(shp))
            piece = vec[:, off:off + n].reshape((N_CHIPS,) + shp)
            off += n
            if k == "rg_conv_w":
                out[k] = piece.reshape(N_CHIPS, 2, 4, 256).transpose(1, 2, 0, 3).reshape(2, 4, D_MODEL)
            elif k in ("rg_b_a", "rg_b_i"):
                out[k] = piece.transpose(1, 2, 0, 3).reshape(2, 1, D_MODEL)
            else:
                out[k] = piece.transpose(1, 0, 2).reshape(2, 1, D_MODEL)
        return out


def _carried(plan, part, wts, fn, *args):
    if plan is None:
        return fn(*args, None)[0]
    gather, names = plan.part(part)
    out, new = fn(*args, gather)
    plan.raw.update(zip(names, new))
    wts.update(plan.views())
    return out


def _grad_blocks(name, g):
    st = jnp.stack([g[i] for i in sorted(g)])
    if name in GATES:
        st = st.reshape(2, LRU_BLOCKS, N_CHIPS, LRU_BW // N_CHIPS, LRU_BW).transpose(2, 0, 1, 3, 4)
    elif name == "rg_conv_w":
        st = st.reshape(2, 4, N_CHIPS, -1).transpose(2, 0, 1, 3)
    elif name in ("rg_b_a", "rg_b_i"):
        st = st.reshape(2, LRU_BLOCKS, N_CHIPS, -1).transpose(2, 0, 1, 3)
    elif name in VECTORS:
        st = st.reshape(2, N_CHIPS, -1).transpose(1, 0, 2)
    else:
        st = jnp.broadcast_to(st.reshape(1, -1), (N_CHIPS, st.size))
    return st.reshape(N_CHIPS, -1)


class _GradExchange:
    def __init__(self):
        self.c = lax.axis_index("c").astype(jnp.int32).reshape(1)
        self.pos = jnp.stack([2 * lax.axis_index("x") + lax.axis_index("y"), lax.axis_index("c")]).astype(jnp.int32)
        self.up = self.got_up = self.parts_up = self.slots_up = None

    @staticmethod
    def _blocked(g):
        if g.ndim == 3:
            g = g.reshape(g.shape[0], N_CHIPS, g.shape[1] // N_CHIPS, g.shape[2])
        return g

    def _sums(self, tag, names, gs, got):
        return [_pair_sum("grad_pair_sum_" + tag + k, g, r, self.c) for k, g, r in zip(names, gs, got)]

    def upper_carry(self, grads):
        self.up = [self._blocked(grads[k][1]) for k in BIG_GRADS]
        return _pair_carry(self.up)

    def upper_got(self, got):
        self.got_up = got

    def carry(self):
        self.parts_up = self._sums("up_", BIG_GRADS, self.up, self.got_up)
        return _chip_carry(self.parts_up)

    def carried(self, slots):
        self.slots_up = slots

    def finish(self, grads, shard_shapes):
        if self.got_up is None:
            self.upper_carry(grads)
            self.got_up = _pair_exchange(self.up)
        if self.slots_up is None:
            self.carry()
            self.slots_up = _chip_exchange(self.parts_up)
        blocks = [_grad_blocks(k, grads[k]) for k in SMALL_GRADS]
        used = sum(b.shape[1] for b in blocks)
        small = jnp.concatenate(blocks + [jnp.zeros((N_CHIPS, SMALL_ROWS * PACK_COLS - used), f32)], axis=1)
        names = tuple(k for k in BIG_GRADS if LOWER_LAYERS[k]) + ("small",)
        gs = [self._blocked(grads[k][0]) for k in names[:-1]] + [small.reshape(1, N_CHIPS, SMALL_ROWS, PACK_COLS)]
        parts = dict(zip(names, self._sums("lo_", names, gs, _pair_exchange(gs))))
        slots = dict(zip(names, _chip_exchange([parts[k] for k in names])))
        fulls = []
        for i, k in enumerate(BIG_GRADS):
            nlo, nup = LOWER_LAYERS[k], self.parts_up[i].shape[0]
            full = _chip_sum("grad_chip_sum_up_" + k, self.slots_up[i], self.parts_up[i], self.pos, nlo, nlo + nup, None)
            if nlo:
                full = _chip_sum("grad_chip_sum_lo_" + k, slots[k], parts[k], self.pos, 0, nlo + nup, full)
            fulls.append(full)
        fulls.append(_chip_sum("grad_chip_sum_lo_small", slots["small"], parts["small"], self.pos, 0, 1, None))
        full = _pair_gather(fulls)
        out = {k: f.reshape(shard_shapes[k]) for k, f in zip(BIG_GRADS, full)}
        flat, off = full[-1].reshape(-1), 0
        for k in SMALL_GRADS:
            n = int(np.prod(shard_shapes[k]))
            out[k] = flat[off:off + n].reshape(shard_shapes[k])
            off += n
        return out


def _adamw_fn(w, g, m, v):
    m = ADAM_B1 * m + (1.0 - ADAM_B1) * g
    v = ADAM_B2 * v + (1.0 - ADAM_B2) * (g * g)
    m_hat = m / (1.0 - ADAM_B1 ** ADAM_STEP)
    v_hat = v / (1.0 - ADAM_B2 ** ADAM_STEP)
    return -ADAM_LR * (m_hat / (jnp.sqrt(v_hat) + ADAM_EPS) + ADAM_WD * w), m, v


def _adamw(name, w, g, m, v):
    shp = w.shape
    if w.size >= 1 << 16:
        width = shp[-1]
        ops = [a.reshape(-1, width) for a in (w, g, m, v)]
        res = _rows(name, _adamw_fn, ops, [], [(width, f32)] * 3)
        return [r.reshape(shp) for r in res]
    n = w.size
    rows = -(-n // (SUBLANES * LANES)) * SUBLANES
    ops = [jnp.pad(a.reshape(-1), (0, rows * LANES - n)).reshape(rows, LANES) for a in (w, g, m, v)]
    res = _rows(name, _adamw_fn, ops, [], [(LANES, f32)] * 3, tr=rows)
    return [r.reshape(-1)[:n].reshape(shp) for r in res]


def kernel(x, attn_w_in, attn_rel_bias, attn_w_out, rg_w_in, rg_conv_w, rg_conv_b, rg_w_a, rg_b_a, rg_w_i, rg_b_i, rg_lambda, rg_w_out, norm_mix_pre, norm_mix_post, norm_ffn_pre, norm_ffn_post, ffn_w_gate, ffn_w_up, ffn_w_down, loss_target, m_attn_w_in, m_attn_rel_bias, m_attn_w_out, m_rg_w_in, m_rg_conv_w, m_rg_conv_b, m_rg_w_a, m_rg_b_a, m_rg_w_i, m_rg_b_i, m_rg_lambda, m_rg_w_out, m_norm_mix_pre, m_norm_mix_post, m_norm_ffn_pre, m_norm_ffn_post, m_ffn_w_gate, m_ffn_w_up, m_ffn_w_down, v_attn_w_in, v_attn_rel_bias, v_attn_w_out, v_rg_w_in, v_rg_conv_w, v_rg_conv_b, v_rg_w_a, v_rg_b_a, v_rg_w_i, v_rg_b_i, v_rg_lambda, v_rg_w_out, v_norm_mix_pre, v_norm_mix_post, v_norm_ffn_pre, v_norm_ffn_post, v_ffn_w_gate, v_ffn_w_up, v_ffn_w_down):
    w = dict(zip(WEIGHTS, (attn_w_in, attn_rel_bias, attn_w_out, rg_w_in, rg_conv_w, rg_conv_b, rg_w_a, rg_b_a, rg_w_i,
                           rg_b_i, rg_lambda, rg_w_out, norm_mix_pre, norm_mix_post, norm_ffn_pre, norm_ffn_post,
                           ffn_w_gate, ffn_w_up, ffn_w_down)))
    m = dict(zip(WEIGHTS, (m_attn_w_in, m_attn_rel_bias, m_attn_w_out, m_rg_w_in, m_rg_conv_w, m_rg_conv_b, m_rg_w_a,
                           m_rg_b_a, m_rg_w_i, m_rg_b_i, m_rg_lambda, m_rg_w_out, m_norm_mix_pre, m_norm_mix_post,
                           m_norm_ffn_pre, m_norm_ffn_post, m_ffn_w_gate, m_ffn_w_up, m_ffn_w_down)))
    v = dict(zip(WEIGHTS, (v_attn_w_in, v_attn_rel_bias, v_attn_w_out, v_rg_w_in, v_rg_conv_w, v_rg_conv_b, v_rg_w_a,
                           v_rg_b_a, v_rg_w_i, v_rg_b_i, v_rg_lambda, v_rg_w_out, v_norm_mix_pre, v_norm_mix_post,
                           v_norm_ffn_pre, v_norm_ffn_post, v_ffn_w_gate, v_ffn_w_up, v_ffn_w_down)))
    plan = _WeightGather(w)
    exch = _GradExchange()
    loss, dx, grads = _local_step(x[0], loss_target[0], plan.views(), plan, exch)
    loss = lax.psum(loss, ("x", "y", "c"))
    g = exch.finish(grads, {k: _natural(k, w[k]).shape for k in WEIGHTS})

    big = [k for k in WEIGHTS if k not in SMALL]
    upd = {}
    for k in big:
        res = _adamw("adamw_" + k, _natural(k, w[k]), g[k], _natural(k, m[k]), _natural(k, v[k]))
        upd[k] = [_natural(k, r) for r in res]
        g[k] = _natural(k, g[k])
    cat = lambda d: jnp.concatenate([d[k].reshape(-1) for k in SMALL])
    small = _adamw("adamw_small", cat(w), cat(g), cat(m), cat(v))
    off = 0
    for k in SMALL:
        n = w[k].size
        upd[k] = [r[off:off + n].reshape(w[k].shape) for r in small]
        off += n
    return (loss, dx[None], *[g[k] for k in WEIGHTS], *[upd[k][0] for k in WEIGHTS],
            *[upd[k][1] for k in WEIGHTS], *[upd[k][2] for k in WEIGHTS])
```

```python
import functools

import numpy as np
import jax
import jax.numpy as jnp
from jax import lax
from jax.experimental import pallas as pl
from jax.experimental.pallas import tpu as pltpu

f32 = jnp.float32
bf16 = jnp.bfloat16
SDS = jax.ShapeDtypeStruct
MESH = pl.DeviceIdType.MESH

D_MODEL = 1024
N_CHIPS = 4
DEPTH = 4
HEAD_DIM = 64
CHUNK = 64
N_LEFT = 8
REL_CLIP = 256
A_W = 512
LRU_BLOCKS = 4
LRU_BW = 256
LRU_C = 8.0
D_FF = 2816
RMS_EPS = 1e-6
LANES = 128
SUBLANES = 8
VMEM_LIMIT = 56 * 1024 * 1024

QB_A = 2 * CHUNK
QSUB_A = 8
KW_A = QB_A + N_LEFT * CHUNK
PAD_A = N_LEFT * CHUNK
EXT_A = 768
SB_BLK = 256
QSUB_B = 4
SB_DEAD = -110.0

ADAM_LR, ADAM_B1, ADAM_B2, ADAM_EPS, ADAM_WD, ADAM_STEP = 0.001, 0.9, 0.999, 1e-08, 0.01, 10


def _cparams(sem):
    return pltpu.CompilerParams(dimension_semantics=sem, vmem_limit_bytes=VMEM_LIMIT)


def _gemm(name, operands, in_specs, o_spec, out_shape, grid, dims, acc_shape, into=None):
    nred = grid[2]
    npair = len(operands) // 2
    nin = 2 * npair + (into is not None)

    def body(*refs):
        o_ref = refs[nin]
        p = None
        for t in range(npair):
            d = lax.dot_general(refs[2 * t][...], refs[2 * t + 1][...], (dims, ((), ())),
                                preferred_element_type=f32)
            p = d if p is None else p + d
        if nred == 1:
            o_ref[...] = p.astype(o_ref.dtype)
        else:
            acc = refs[nin + 1]
            r = pl.program_id(2)

            @pl.when(r == 0)
            def _():
                acc[...] = p

            @pl.when(r > 0)
            def _():
                acc[...] += p

            @pl.when(r == nred - 1)
            def _():
                o_ref[...] = acc[...].astype(o_ref.dtype)

    scratch = [] if nred == 1 else [pltpu.VMEM(acc_shape, f32)]
    extra, alias = ([], {}) if into is None else ([into], {2 * npair: 0})
    return pl.pallas_call(
        body, grid=grid, in_specs=list(in_specs) + [pl.BlockSpec(memory_space=pl.ANY)] * len(extra),
        out_specs=o_spec, out_shape=out_shape, scratch_shapes=scratch, name=name, input_output_aliases=alias,
        compiler_params=_cparams(("parallel", "parallel", "arbitrary")))(*operands, *extra)


LOWER_LAYERS = {"attn_w_in": 1, "attn_w_out": 1, "rg_w_in": 0, "rg_w_out": 0,
                "ffn_w_gate": 0, "ffn_w_up": 0, "ffn_w_down": 0}


def _grad_slot(name, l):
    n = LOWER_LAYERS[name]
    return (0, l) if l < n else (1, l - n)


class _Fresh:
    def __init__(self, shape):
        self.shape = tuple(shape)


def _into(buf):
    return None if isinstance(buf, _Fresh) else buf


NN = ((1,), (0,))
NT = ((1,), (1,))
TN = ((0,), (0,))


WGRAD_TOKENS = 2048


def _tile(t, want=1024):
    return min(want, t)


def _mm_cols(name, a, w, l, out_dtype):
    t, k = a.shape
    _, s, _, ns = w.shape
    tm = _tile(t, 2048)
    return _gemm(
        name, [a, w],
        [pl.BlockSpec((tm, k), lambda i, j, r: (i, 0)),
         pl.BlockSpec((None, None, k, ns), lambda i, j, r: (l, j, 0, 0))],
        pl.BlockSpec((tm, ns), lambda i, j, r: (i, j)),
        SDS((t, s * ns), out_dtype), (t // tm, s, 1), NN, None)


def _mm_cols_t(name, dy, w, l, out_dtype):
    t = dy.shape[0]
    _, s, k, ns = w.shape
    tm = _tile(t, 2048)
    return _gemm(
        name, [dy, w],
        [pl.BlockSpec((tm, ns), lambda i, j, r: (i, r)),
         pl.BlockSpec((None, None, k, ns), lambda i, j, r: (l, r, 0, 0))],
        pl.BlockSpec((tm, k), lambda i, j, r: (i, 0)),
        SDS((t, k), out_dtype), (t // tm, 1, s), NT, (tm, k))


def _mm_wgrad_cols(name, a, dy, buf, l):
    t, k = a.shape
    _, s, _, ns = buf.shape
    tt = _tile(t, 2 * WGRAD_TOKENS)
    return _gemm(
        name, [a, dy],
        [pl.BlockSpec((tt, k), lambda i, j, r: (r, 0)),
         pl.BlockSpec((tt, ns), lambda i, j, r: (r, i))],
        pl.BlockSpec((None, None, k, ns), lambda i, j, r: (l, i, 0, 0)),
        SDS(buf.shape, f32), (s, 1, t // tt), TN, (k, ns), into=_into(buf))


def _mm_rows(name, parts, w, l, out_dtype):
    t = parts[0].shape[0]
    n = w.shape[3]
    tm = _tile(t, 2048)
    ops, specs = [], []
    for p_i, a in enumerate(parts):
        kp = a.shape[1]
        ops += [a, w]
        specs += [pl.BlockSpec((tm, kp), lambda i, j, r: (i, 0)),
                  pl.BlockSpec((None, None, kp, n), lambda i, j, r, p_i=p_i: (l, 0, p_i, 0))]
    return _gemm(name, ops, specs, pl.BlockSpec((tm, n), lambda i, j, r: (i, 0)),
                 SDS((t, n), out_dtype), (t // tm, 1, 1), NN, None)


def _mm_rows_t(name, dy, w, l, out_dtype):
    t, n = dy.shape
    k = w.shape[2]
    tm = _tile(t, 2048)
    return _gemm(
        name, [dy, w],
        [pl.BlockSpec((tm, n), lambda i, j, r: (i, 0)),
         pl.BlockSpec((None, None, k, n), lambda i, j, r: (l, 0, 0, 0))],
        pl.BlockSpec((tm, k), lambda i, j, r: (i, 0)),
        SDS((t, k), out_dtype), (t // tm, 1, 1), NT, None)


def _mm_wgrad(name, a, dy, buf, l, part=0):
    t, k = a.shape
    n = dy.shape[1]
    tt = _tile(t, 2 * WGRAD_TOKENS)
    return _gemm(
        name, [a, dy],
        [pl.BlockSpec((tt, k), lambda i, j, r: (r, 0)),
         pl.BlockSpec((tt, n), lambda i, j, r: (r, 0))],
        pl.BlockSpec((None, k, n), lambda i, j, r: (l, part, 0)),
        SDS(buf.shape, f32), (1, 1, t // tt), TN, (k, n), into=_into(buf))


def _ffn_up(h, wg, wu, l, gather):
    t, k = h.shape
    s, fs = wg.shape[1], wg.shape[2]
    tm = _tile(t)

    def body(h_ref, wg_ref, wu_ref, g_ref, u_ref, hid_ref):
        hv = h_ref[...]
        g = lax.dot_general(hv, wg_ref[...], (NT, ((), ())), preferred_element_type=f32)
        u = lax.dot_general(hv, wu_ref[...], (NT, ((), ())), preferred_element_type=f32)
        g_ref[...] = g.astype(bf16)
        u_ref[...] = u.astype(bf16)
        hid_ref[...] = (g * jax.nn.sigmoid(g) * u).astype(bf16)

    wspec = pl.BlockSpec((None, None, fs, k), lambda j, i: (l, j, 0, 0))
    ospec = pl.BlockSpec((None, tm, fs), lambda j, i: (j, i, 0))
    return _call(
        body, [h, wg, wu], grid=(s, t // tm), name="ffn_up",
        in_specs=[pl.BlockSpec((tm, k), lambda j, i: (i, 0)), wspec, wspec],
        out_specs=[ospec, ospec, ospec], out_shape=[SDS((s, t, fs), bf16)] * 3,
        sem=("parallel", "parallel"), gather=gather)


def _ffn_down(hid, wd, l):
    s, t, fs = hid.shape
    n = wd.shape[3]
    tm = _tile(t)
    ops, specs = [], []
    for r in range(s):
        ops += [hid, wd]
        specs += [pl.BlockSpec((None, tm, fs), lambda i, j, k, r=r: (r, i, 0)),
                  pl.BlockSpec((None, None, fs, n), lambda i, j, k, r=r: (l, r, 0, 0))]
    return _gemm("ffn_down", ops, specs, pl.BlockSpec((tm, n), lambda i, j, k: (i, 0)),
                 SDS((t, n), f32), (t // tm, 1, 1), NN, None)


def _ffn_down_bwd(df, wd, l, g, u, gather):
    t, n = df.shape
    s, fs = wd.shape[1], wd.shape[2]
    tm = _tile(t)

    def body(df_ref, wd_ref, g_ref, u_ref, dg_ref, du_ref):
        dh = lax.dot_general(df_ref[...], wd_ref[...], (NT, ((), ())), preferred_element_type=f32)
        gv = g_ref[...].astype(f32)
        uv = u_ref[...].astype(f32)
        sg = jax.nn.sigmoid(gv)
        du_ref[...] = (dh * gv * sg).astype(bf16)
        dg_ref[...] = (dh * uv * (sg * (1.0 + gv * (1.0 - sg)))).astype(bf16)

    bspec = pl.BlockSpec((None, tm, fs), lambda j, i: (j, i, 0))
    return _call(
        body, [df, wd, g, u], grid=(s, t // tm), name="ffn_down_bwd",
        in_specs=[pl.BlockSpec((tm, n), lambda j, i: (i, 0)),
                  pl.BlockSpec((None, None, fs, n), lambda j, i: (l, j, 0, 0)), bspec, bspec],
        out_specs=[bspec, bspec], out_shape=[SDS((s, t, fs), bf16)] * 2,
        sem=("parallel", "parallel"), gather=gather)


def _ffn_up_bwd(dg, du, wg, wu, l):
    s, t, fs = dg.shape
    k = wg.shape[3]
    tm = _tile(t, 512)
    ops, specs = [], []
    for r in range(s):
        aspec = pl.BlockSpec((None, tm, fs), lambda i, j, kk, r=r: (r, i, 0))
        wspec = pl.BlockSpec((None, None, fs, k), lambda i, j, kk, r=r: (l, r, 0, 0))
        ops += [dg, wg, du, wu]
        specs += [aspec, wspec, aspec, wspec]
    return _gemm("ffn_up_bwd", ops, specs, pl.BlockSpec((tm, k), lambda i, j, kk: (i, 0)),
                 SDS((t, k), f32), (t // tm, 1, 1), NN, None)


def _ffn_wgrad_up(h, dg, du, buf_g, buf_u, l):
    t, k = h.shape
    s, _, fs = dg.shape
    tt = _tile(t, WGRAD_TOKENS)
    nred = t // tt

    fresh = isinstance(buf_g, _Fresh)

    def body(*refs):
        h_ref, dg_ref, du_ref = refs[:3]
        og_ref, ou_ref, acc_g, acc_u = refs[-4:]
        r = pl.program_id(1)
        hv = h_ref[...]
        pg = lax.dot_general(dg_ref[...], hv, (TN, ((), ())), preferred_element_type=f32)
        pu = lax.dot_general(du_ref[...], hv, (TN, ((), ())), preferred_element_type=f32)

        @pl.when(r == 0)
        def _():
            acc_g[...] = pg
            acc_u[...] = pu

        @pl.when(r > 0)
        def _():
            acc_g[...] += pg
            acc_u[...] += pu

        @pl.when(r == nred - 1)
        def _():
            og_ref[...] = acc_g[...]
            ou_ref[...] = acc_u[...]

    dspec = pl.BlockSpec((None, tt, fs), lambda i, r: (i, r, 0))
    ospec = pl.BlockSpec((None, None, fs, k), lambda i, r: (l, i, 0, 0))
    extra, alias = ([], {}) if fresh else ([buf_g, buf_u], {3: 0, 4: 1})
    return pl.pallas_call(
        body, grid=(s, nred), name="ffn_wgrad_up",
        in_specs=[pl.BlockSpec((tt, k), lambda i, r: (r, 0)), dspec, dspec] + [ANY] * len(extra),
        out_specs=[ospec, ospec], out_shape=[SDS(buf_g.shape, f32), SDS(buf_u.shape, f32)],
        scratch_shapes=[pltpu.VMEM((fs, k), f32)] * 2, input_output_aliases=alias,
        compiler_params=_cparams(("parallel", "arbitrary")))(h, dg, du, *extra)


def _ffn_wgrad_down(hid, df, buf, l):
    s, t, fs = hid.shape
    n = df.shape[1]
    tt = _tile(t, 2 * WGRAD_TOKENS)
    return _gemm(
        "ffn_wgrad_down", [hid, df],
        [pl.BlockSpec((None, tt, fs), lambda i, j, r: (i, r, 0)),
         pl.BlockSpec((tt, n), lambda i, j, r: (r, 0))],
        pl.BlockSpec((None, None, fs, n), lambda i, j, r: (l, i, 0, 0)),
        SDS(buf.shape, f32), (s, 1, t // tt), TN, (fs, n), into=_into(buf))


def _rows(name, fn, rows, consts, row_outs, acc_outs=(), tr=512):
    rows = [r if isinstance(r, tuple) else (r, r.shape[1], 0) for r in rows]
    t = rows[0][0].shape[0]
    tr = max(d for d in range(SUBLANES, min(tr, t) + 1, SUBLANES) if t % d == 0)
    nin = len(rows) + len(consts)
    no, na = len(row_outs), len(acc_outs)

    def body(*refs):
        vals = fn(*[r[...] for r in refs[:nin]])
        if not isinstance(vals, (tuple, list)):
            vals = (vals,)
        for k in range(no):
            refs[nin + k][...] = vals[k].astype(refs[nin + k].dtype)
        first = pl.program_id(0) == 0
        for k in range(na):
            ref, val = refs[nin + no + k], vals[no + k]

            @pl.when(first)
            def _(ref=ref, val=val):
                ref[...] = val

            @pl.when(jnp.logical_not(first))
            def _(ref=ref, val=val):
                ref[...] += val

    in_specs = [pl.BlockSpec((tr, w), lambda i, cb=cb: (i, cb)) for (_, w, cb) in rows]
    in_specs += [pl.BlockSpec(c.shape, lambda i, nd=c.ndim: (0,) * nd) for c in consts]
    out_specs = [pl.BlockSpec((tr, w), lambda i: (i, 0)) for (w, _) in row_outs]
    out_specs += [pl.BlockSpec(s, lambda i, nd=len(s): (0,) * nd) for (s, _) in acc_outs]
    out_shape = [SDS((t, w), dt) for (w, dt) in row_outs] + [SDS(s, dt) for (s, dt) in acc_outs]
    res = pl.pallas_call(
        body, grid=(t // tr,), in_specs=in_specs, out_specs=out_specs, out_shape=out_shape,
        name=name, compiler_params=_cparams(("arbitrary",)))(*[r[0] for r in rows], *consts)
    return res


def _rstd(x):
    return lax.rsqrt(jnp.mean(x * x, axis=-1, keepdims=True) + RMS_EPS)


def _norm_fwd(x, g):
    return x * _rstd(x) * g


def _norm_bwd(u, dy, g):
    r = _rstd(u)
    n = u * r
    dn = dy * g
    du = r * (dn - n * jnp.mean(dn * n, axis=-1, keepdims=True))
    return du, jnp.sum(dy * n, axis=0, keepdims=True)


def _gelu(x):
    c = 0.7978845608028654
    return 0.5 * x * (1.0 + jnp.tanh(c * (x + 0.044715 * x * x * x)))


def _gelu_grad(x):
    c = 0.7978845608028654
    th = jnp.tanh(c * (x + 0.044715 * x * x * x))
    return 0.5 * (1.0 + th) + 0.5 * x * (1.0 - th * th) * c * (1.0 + 3.0 * 0.044715 * x * x)


def _mask_heads(x):
    lane = lax.broadcasted_iota(jnp.int32, x.shape, 1)
    return [jnp.where((lane >= h * HEAD_DIM) & (lane < (h + 1) * HEAD_DIM), x, jnp.zeros_like(x))
            for h in range(LANES // HEAD_DIM)]


def _chunk_valid(start):
    qi = lax.broadcasted_iota(jnp.int32, (QB_A, KW_A), 0)
    kj = lax.broadcasted_iota(jnp.int32, (QB_A, KW_A), 1)
    qc = qi // CHUNK
    kc = kj // CHUNK
    return (kc >= qc) & (kc <= qc + N_LEFT) & (kj + start >= PAD_A)


def _scaled(q):
    return q * (HEAD_DIM ** -0.5)


def _chunk_probs(q, k, bias, valid):
    s = lax.dot_general(q, k, (NT, ((), ())), preferred_element_type=f32) + bias
    s = jnp.where(valid, s, -1e30)
    p = jnp.exp(s - jnp.max(s, axis=-1, keepdims=True))
    return p / jnp.sum(p, axis=-1, keepdims=True)


def _chunk_attn_fwd(proj, kpad, vpad, bias, gather):
    t = proj.shape[0]
    tp = kpad.shape[0]
    step = QSUB_A * QB_A

    def body(q_ref, k_ref, v_ref, b_ref, o_ref):
        for sb in range(QSUB_A):
            start = pl.multiple_of((pl.program_id(1) * QSUB_A + sb) * QB_A, QB_A)
            rows = pl.ds(sb * QB_A, QB_A)
            valid = _chunk_valid(start)
            kw = k_ref[pl.ds(start, KW_A), :]
            qm = _mask_heads(_scaled(q_ref[rows, :]))
            vm = _mask_heads(v_ref[pl.ds(start, KW_A), :])
            o = None
            for h in range(len(qm)):
                p = _chunk_probs(qm[h], kw, b_ref[h], valid)
                d = jnp.dot(p.astype(bf16), vm[h], preferred_element_type=f32)
                o = d if o is None else o + d
            o_ref[rows, :] = o.astype(bf16)

    kv_spec = pl.BlockSpec((tp, LANES), lambda hp, qb: (0, hp))
    outs, new = _call(
        body, [proj, kpad, vpad, bias], grid=(A_W // LANES, t // step), name="chunk_attn_fwd",
        in_specs=[pl.BlockSpec((step, LANES), lambda hp, qb: (qb, hp)), kv_spec, kv_spec,
                  pl.BlockSpec((2, QB_A, KW_A), lambda hp, qb: (hp, 0, 0))],
        out_specs=[pl.BlockSpec((step, LANES), lambda hp, qb: (qb, hp))],
        out_shape=[SDS((t, A_W), bf16)], sem=("parallel", "arbitrary"), gather=gather)
    return outs[0], new


def _chunk_attn_bwd(proj, kpad, vpad, bias, dout, gather):
    t = proj.shape[0]
    tp = kpad.shape[0]
    step = QSUB_A * QB_A

    def body(q_ref, k_ref, v_ref, b_ref, do_ref, dq_ref, dk_ref, dv_ref, db_ref):
        qb = pl.program_id(1)

        @pl.when(qb == 0)
        def _():
            dk_ref[...] = jnp.zeros_like(dk_ref)
            dv_ref[...] = jnp.zeros_like(dv_ref)
            db_ref[...] = jnp.zeros_like(db_ref)

        for sb in range(QSUB_A):
            start = pl.multiple_of((qb * QSUB_A + sb) * QB_A, QB_A)
            rows = pl.ds(sb * QB_A, QB_A)
            win = pl.ds(start, KW_A)
            valid = _chunk_valid(start)
            kw = k_ref[win, :]
            vw = v_ref[win, :]
            qm = _mask_heads(_scaled(q_ref[rows, :]))
            dom = _mask_heads(do_ref[rows, :])
            km = _mask_heads(kw)
            dq = dk = dv = None
            for h in range(len(qm)):
                p = _chunk_probs(qm[h], kw, b_ref[h], valid)
                dp = lax.dot_general(dom[h], vw, (NT, ((), ())), preferred_element_type=f32)
                ds = p * (dp - jnp.sum(dp * p, axis=-1, keepdims=True))
                db_ref[h] += ds
                dsb = ds.astype(bf16)
                terms = (jnp.dot(dsb, km[h], preferred_element_type=f32),
                         lax.dot_general(dsb, qm[h], (TN, ((), ())), preferred_element_type=f32),
                         lax.dot_general(p.astype(bf16), dom[h], (TN, ((), ())), preferred_element_type=f32))
                dq, dk, dv = terms if dq is None else (dq + terms[0], dk + terms[1], dv + terms[2])
            dq_ref[rows, :] = _scaled(dq).astype(bf16)
            dk_ref[win, :] += dk
            dv_ref[win, :] += dv

    kv_spec = pl.BlockSpec((tp, LANES), lambda hp, qb: (0, hp))
    q_spec = pl.BlockSpec((step, LANES), lambda hp, qb: (qb, hp))
    b_spec = pl.BlockSpec((2, QB_A, KW_A), lambda hp, qb: (hp, 0, 0))
    return _call(
        body, [proj, kpad, vpad, bias, dout], grid=(A_W // LANES, t // step), name="chunk_attn_bwd",
        in_specs=[q_spec, kv_spec, kv_spec, b_spec, q_spec],
        out_specs=[q_spec, kv_spec, kv_spec, b_spec],
        out_shape=[SDS((t, A_W), bf16), SDS((tp, A_W), f32), SDS((tp, A_W), f32),
                   SDS((2 * A_W // LANES, QB_A, KW_A), f32)],
        sem=("parallel", "arbitrary"), gather=gather)


def _bias_ext(table):
    flat = PAD_A + QB_A - 1 - REL_CLIP
    top = jnp.broadcast_to(table[:, 2 * REL_CLIP:], (table.shape[0], flat))
    lo = 2 * REL_CLIP - (EXT_A - 1 - flat)
    return jnp.concatenate([top, jnp.flip(table[:, lo:], axis=1)], axis=1)


def _bias_window(table):
    nh = table.shape[0]
    e = jnp.broadcast_to(_bias_ext(table)[:, None, :], (nh, QB_A, EXT_A)).reshape(nh, QB_A * EXT_A)
    m = e[:, :QB_A * (EXT_A - 1)].reshape(nh, QB_A, EXT_A - 1)
    return m[:, :, QB_A - 1:]


def _bias_window_grad(dbias):
    nh = dbias.shape[0]
    m = jnp.pad(dbias, ((0, 0), (0, 0), (QB_A - 1, 0))).reshape(nh, QB_A * (EXT_A - 1))
    dext = jnp.sum(jnp.pad(m, ((0, 0), (0, QB_A))).reshape(nh, QB_A, EXT_A), axis=1)
    flat = PAD_A + QB_A - 1 - REL_CLIP
    lo = 2 * REL_CLIP - (EXT_A - 1 - flat)
    tail = jnp.flip(dext[:, flat:], axis=1)
    tail = tail.at[:, -1].add(jnp.sum(dext[:, :flat], axis=1))
    return jnp.pad(tail, ((0, 0), (lo, 0)))


def _tri_suffix(x, tri):
    hi = x.astype(bf16)
    lo = (x - hi.astype(f32)).astype(bf16)
    return jnp.dot(hi, tri, preferred_element_type=f32) + jnp.dot(lo, tri, preferred_element_type=f32)


def _sb_block(q, k, run, tri, causal):
    z = lax.dot_general(q, k, (NT, ((), ())), preferred_element_type=f32)
    e = jnp.exp(-jnp.abs(z))
    l1p = jnp.log(1.0 + e)
    lb = jnp.minimum(z, 0.0) - l1p
    lmb = lb - z
    if causal is not None:
        lmb = jnp.where(causal, lmb, 0.0)
    cs = _tri_suffix(lmb, tri)
    w = jnp.exp(lb + (run + cs - lmb))
    if causal is not None:
        w = jnp.where(causal, w, 0.0)
    return z, e, w, run + cs[:, 0:1]


def _sb_tri():
    r = lax.broadcasted_iota(jnp.int32, (SB_BLK, SB_BLK), 0)
    c = lax.broadcasted_iota(jnp.int32, (SB_BLK, SB_BLK), 1)
    return (r >= c).astype(bf16), c < r


def _sb_live(runs):
    m = runs[0]
    for r in runs[1:]:
        m = jnp.maximum(m, r)
    return jnp.max(m) > SB_DEAD


def _sb_fwd(proj, gather):
    t = proj.shape[0]
    cb = A_W // LANES
    nh = LANES // HEAD_DIM

    step_rows = QSUB_B * SB_BLK

    def body(q_ref, k_ref, v_ref, o_ref, of_ref):
        tri, diag = _sb_tri()
        for sb in range(QSUB_B):
            _sb_fwd_block(pl.program_id(1) * QSUB_B + sb, pl.ds(sb * SB_BLK, SB_BLK), tri, diag,
                          q_ref, k_ref, v_ref, o_ref, of_ref)

    def _sb_fwd_block(qb, qrows, tri, diag, q_ref, k_ref, v_ref, o_ref, of_ref):
        qm = _mask_heads(_scaled(q_ref[qrows, :]))

        def pair(kb, carry, causal):
            rows = pl.ds(pl.multiple_of(kb * SB_BLK, SB_BLK), SB_BLK)
            k = k_ref[rows, :]
            vm = _mask_heads(v_ref[rows, :])
            runs, acc = [], carry[nh]
            for h in range(nh):
                _, _, w, run = _sb_block(qm[h], k, carry[h], tri, causal)
                acc = acc + jnp.dot(w.astype(bf16), vm[h], preferred_element_type=f32)
                runs.append(run)
            return (*runs, acc)

        zero = jnp.zeros((SB_BLK, 1), f32)
        carry = pair(qb, (zero,) * nh + (jnp.zeros((SB_BLK, LANES), f32),), diag)

        def cond(st):
            return (st[0] < qb) & _sb_live(st[1][:nh])

        def step(st):
            return st[0] + 1, pair(qb - 1 - st[0], st[1], None)

        _, carry = lax.while_loop(cond, step, (jnp.int32(0), carry))
        o_ref[qrows, :] = carry[nh].astype(bf16)
        of_ref[qrows, :] = carry[nh]

    ospec = pl.BlockSpec((step_rows, LANES), lambda hp, qb: (qb, hp))
    return _call(
        body, [proj, proj, proj], grid=(cb, t // step_rows), name="sb_attn_fwd",
        in_specs=[pl.BlockSpec((step_rows, LANES), lambda hp, qb: (qb, 3 * cb + hp)),
                  pl.BlockSpec((t, LANES), lambda hp, qb: (0, 4 * cb + hp)),
                  pl.BlockSpec((t, LANES), lambda hp, qb: (0, 5 * cb + hp))],
        out_specs=[ospec, ospec], out_shape=[SDS((t, A_W), bf16), SDS((t, A_W), f32)],
        sem=("parallel", "arbitrary"), gather=gather)


def _sb_bwd(proj, out_b, dout, gather):
    t = proj.shape[0]
    cb = A_W // LANES
    nh = LANES // HEAD_DIM

    step_rows = QSUB_B * SB_BLK

    def body(q_ref, k_ref, v_ref, o_ref, do_ref, dq_ref, dk_ref, dv_ref):
        tri, diag = _sb_tri()

        @pl.when(pl.program_id(1) == 0)
        def _():
            dk_ref[...] = jnp.zeros_like(dk_ref)
            dv_ref[...] = jnp.zeros_like(dv_ref)

        for sb in range(QSUB_B):
            _sb_bwd_block(pl.program_id(1) * QSUB_B + sb, pl.ds(sb * SB_BLK, SB_BLK), tri, diag,
                          q_ref, k_ref, v_ref, o_ref, do_ref, dq_ref, dk_ref, dv_ref)

    def _sb_bwd_block(qb, qrows, tri, diag, q_ref, k_ref, v_ref, o_ref, do_ref, dq_ref, dk_ref, dv_ref):
        qm = _mask_heads(_scaled(q_ref[qrows, :]))
        do = do_ref[qrows, :]
        dom = _mask_heads(do)
        dsums = [jnp.sum(t_, axis=-1, keepdims=True) for t_ in _mask_heads(do.astype(f32) * o_ref[qrows, :])]

        def pair(kb, carry, causal):
            rows = pl.ds(pl.multiple_of(kb * SB_BLK, SB_BLK), SB_BLK)
            k = k_ref[rows, :]
            v = v_ref[rows, :]
            km = _mask_heads(k)
            new, dq, dk, dv = [], carry[2 * nh], None, None
            for h in range(nh):
                z, e, w, run = _sb_block(qm[h], k, carry[2 * h], tri, causal)
                inv = 1.0 / (1.0 + e)
                beta = jnp.where(z >= 0.0, inv, e * inv)
                wb = w.astype(bf16)
                g = lax.dot_general(dom[h], v, (NT, ((), ())), preferred_element_type=f32) * wb.astype(f32)
                sg = _tri_suffix(g, tri)
                dz = g - (g + (dsums[h] - carry[2 * h + 1] - sg)) * beta
                if causal is not None:
                    dz = jnp.where(causal, dz, 0.0)
                dzb = dz.astype(bf16)
                dq = dq + jnp.dot(dzb, km[h], preferred_element_type=f32)
                tk = lax.dot_general(dzb, qm[h], (TN, ((), ())), preferred_element_type=f32)
                tv = lax.dot_general(wb, dom[h], (TN, ((), ())), preferred_element_type=f32)
                dk, dv = (tk, tv) if dk is None else (dk + tk, dv + tv)
                new += [run, carry[2 * h + 1] + sg[:, 0:1]]
            dk_ref[rows, :] += dk
            dv_ref[rows, :] += dv
            return (*new, dq)

        zero = jnp.zeros((SB_BLK, 1), f32)
        carry = pair(qb, (zero,) * (2 * nh) + (jnp.zeros((SB_BLK, LANES), f32),), diag)

        def cond(st):
            return (st[0] < qb) & _sb_live(st[1][0:2 * nh:2])

        def step(st):
            return st[0] + 1, pair(qb - 1 - st[0], st[1], None)

        _, carry = lax.while_loop(cond, step, (jnp.int32(0), carry))
        dq_ref[qrows, :] = _scaled(carry[2 * nh]).astype(bf16)

    kv_in = lambda seg: pl.BlockSpec((t, LANES), lambda hp, qb: (0, seg * cb + hp))
    q_spec = pl.BlockSpec((step_rows, LANES), lambda hp, qb: (qb, hp))
    kv_out = pl.BlockSpec((t, LANES), lambda hp, qb: (0, hp))
    return _call(
        body, [proj, proj, proj, out_b, dout], grid=(cb, t // step_rows), name="sb_attn_bwd",
        in_specs=[pl.BlockSpec((step_rows, LANES), lambda hp, qb: (qb, 3 * cb + hp)), kv_in(4), kv_in(5),
                  q_spec, pl.BlockSpec((step_rows, LANES), lambda hp, qb: (qb, cb + hp))],
        out_specs=[q_spec, kv_out, kv_out],
        out_shape=[SDS((t, A_W), bf16), SDS((t, A_W), f32), SDS((t, A_W), f32)],
        sem=("parallel", "arbitrary"), gather=gather)


def _halo_specs(tr, w, col, nblk):
    per = tr // SUBLANES
    cur = pl.BlockSpec((tr, w), lambda i: (i, col))
    prev = pl.BlockSpec((SUBLANES, w), lambda i: (jnp.maximum(i * per - 1, 0), col))
    nxt = pl.BlockSpec((SUBLANES, w), lambda i: (jnp.minimum((i + 1) * per, nblk * per - 1), col))
    return cur, prev, nxt


def _taps_before(cur, prev8, first):
    prev8 = jnp.where(first, 0.0, prev8)
    ext = jnp.concatenate([prev8, cur], axis=0)
    return [pltpu.roll(ext, s, 0)[SUBLANES:] for s in (3, 2, 1)]


def _taps_after(cur, next8, last):
    n = cur.shape[0]
    next8 = jnp.where(last, 0.0, next8)
    ext = jnp.concatenate([cur, next8], axis=0)
    return [pltpu.roll(ext, n + SUBLANES - s, 0)[:n] for s in (1, 2, 3)]


def _block_diag(x, w_ref, dims):
    outs = [lax.dot_general(x[:, n * LRU_BW:(n + 1) * LRU_BW], w_ref[n], (dims, ((), ())),
                            preferred_element_type=f32) for n in range(LRU_BLOCKS)]
    return jnp.concatenate(outs, axis=1)


def _lru_gates(xc, wa_ref, wi_ref, ba, bi, lam):
    xb = xc.astype(bf16)
    r = jax.nn.sigmoid(_block_diag(xb, wa_ref, NN) + ba)
    ig = jax.nn.sigmoid(_block_diag(xb, wi_ref, NN) + bi)
    sp = jnp.maximum(-lam, 0.0) + jnp.log(1.0 + jnp.exp(-jnp.abs(lam)))
    log_a = -LRU_C * r * sp
    a = jnp.exp(log_a)
    x2 = 2.0 * log_a
    one_minus = jnp.where(x2 > -1e-2, -x2 * (1.0 + x2 * (0.5 + x2 * (1.0 / 6.0))), 1.0 - a * a)
    mult = jnp.sqrt(one_minus)
    return xb, r, ig, sp, a, mult


def _rg_gates_fwd(proj, conv_w, conv_b, wa, wi, ba, bi, lam, tr=512):
    t = proj.shape[0]
    w = D_MODEL
    tr = min(tr, t)
    nblk = t // tr
    cur, prev, _ = _halo_specs(tr, w, 1, nblk)

    def body(x_ref, xp_ref, cw_ref, cb_ref, wa_ref, wi_ref, ba_ref, bi_ref, lam_ref, xc_ref, a_ref, u_ref):
        x = x_ref[...]
        taps = _taps_before(x, xp_ref[...], pl.program_id(0) == 0) + [x]
        xc = cb_ref[...]
        for k in range(4):
            xc = xc + cw_ref[k:k + 1, :] * taps[k]
        _, _, ig, _, a, mult = _lru_gates(xc, wa_ref, wi_ref, ba_ref[...], bi_ref[...], lam_ref[...])
        xc_ref[...] = xc
        a_ref[...] = a
        u_ref[...] = mult * (ig * xc)

    full = lambda a_: pl.BlockSpec(a_.shape, lambda i, nd=a_.ndim: (0,) * nd)
    ospec = pl.BlockSpec((tr, w), lambda i: (i, 0))
    return pl.pallas_call(
        body, grid=(nblk,), name="rg_gates_fwd",
        in_specs=[cur, prev] + [full(a_) for a_ in (conv_w, conv_b, wa, wi, ba, bi, lam)],
        out_specs=[ospec] * 3, out_shape=[SDS((t, w), f32)] * 3,
        compiler_params=_cparams(("parallel",)))(proj, proj, conv_w, conv_b, wa, wi, ba, bi, lam)


def _lru_scan(name, a, b, reverse, tt=1024):
    t, w = a.shape
    tt = min(tt, t)
    nt = t // tt
    ng = tt // SUBLANES

    def body(a_ref, b_ref, h_ref, carry_ref):
        @pl.when(pl.program_id(0) == 0)
        def _():
            carry_ref[...] = jnp.zeros_like(carry_ref)

        row = lax.broadcasted_iota(jnp.int32, (SUBLANES, w), 0)

        def group(gi, carry):
            g = (ng - 1 - gi) if reverse else gi
            rows = pl.ds(pl.multiple_of(g * SUBLANES, SUBLANES), SUBLANES)
            av = a_ref[rows, :]
            bv = b_ref[rows, :]
            for s in (1, 2, 4):
                sh = (SUBLANES - s) if reverse else s
                ok = (row < SUBLANES - s) if reverse else (row >= s)
                a_s = pltpu.roll(av, sh, 0)
                b_s = pltpu.roll(bv, sh, 0)
                bv = jnp.where(ok, av * b_s + bv, bv)
                av = jnp.where(ok, av * a_s, av)
            h = av * carry + bv
            h_ref[rows, :] = h
            edge = h[0:1, :] if reverse else h[SUBLANES - 1:SUBLANES, :]
            return jnp.broadcast_to(edge, (SUBLANES, w))

        carry_ref[...] = lax.fori_loop(0, ng, group, carry_ref[...], unroll=4)

    tmap = (lambda i: (nt - 1 - i, 0)) if reverse else (lambda i: (i, 0))
    spec = pl.BlockSpec((tt, w), tmap)
    return pl.pallas_call(
        body, grid=(nt,), name=name, in_specs=[spec, spec], out_specs=spec,
        out_shape=SDS((t, w), f32), scratch_shapes=[pltpu.VMEM((SUBLANES, w), f32)],
        compiler_params=_cparams(("arbitrary",)))(a, b)


def _rg_gates_bwd(dhs, c, hs, xc, wa, wi, ba, bi, lam, tr=512):
    t, w = xc.shape
    tr = min(tr, t)
    nblk = t // tr
    cur, prev, nxt = _halo_specs(tr, w, 0, nblk)

    def body(dhs_ref, c_ref, cn_ref, hs_ref, hp_ref, xc_ref, wa_ref, wi_ref, ba_ref, bi_ref, lam_ref,
             dxc_ref, dwa_ref, dwi_ref, dba_ref, dbi_ref, dlam_ref):
        i = pl.program_id(0)
        c_next = _taps_after(c_ref[...], cn_ref[...], i == nblk - 1)[0]
        h_prev = _taps_before(hs_ref[...], hp_ref[...], i == 0)[2]
        xc = xc_ref[...]
        lam = lam_ref[...]
        xb, r, ig, sp, a, mult = _lru_gates(xc, wa_ref, wi_ref, ba_ref[...], bi_ref[...], lam)
        dh = dhs_ref[...] + c_next
        dlog_a = dh * h_prev * a - (dh * ig * xc) * (a * a / mult)
        dpre_a = (dlog_a * (-LRU_C * sp) * r * (1.0 - r)).astype(bf16)
        dpre_i = (dh * mult * xc * ig * (1.0 - ig)).astype(bf16)
        dxc_ref[...] = (dh * mult * ig + _block_diag(dpre_a, wa_ref, NT) + _block_diag(dpre_i, wi_ref, NT))
        dsig = 1.0 / (1.0 + jnp.exp(lam))
        sums = [jnp.sum(dpre_a.astype(f32), axis=0, keepdims=True),
                jnp.sum(dpre_i.astype(f32), axis=0, keepdims=True),
                jnp.sum(dlog_a * (-LRU_C * r), axis=0, keepdims=True) * (-dsig)]

        @pl.when(i == 0)
        def _():
            dwa_ref[...] = jnp.zeros_like(dwa_ref)
            dwi_ref[...] = jnp.zeros_like(dwi_ref)
            dba_ref[...] = jnp.zeros_like(dba_ref)
            dbi_ref[...] = jnp.zeros_like(dbi_ref)
            dlam_ref[...] = jnp.zeros_like(dlam_ref)

        for n in range(LRU_BLOCKS):
            sl = slice(n * LRU_BW, (n + 1) * LRU_BW)
            dwa_ref[n] += lax.dot_general(xb[:, sl], dpre_a[:, sl], (TN, ((), ())), preferred_element_type=f32)
            dwi_ref[n] += lax.dot_general(xb[:, sl], dpre_i[:, sl], (TN, ((), ())), preferred_element_type=f32)
        dba_ref[...] += sums[0]
        dbi_ref[...] += sums[1]
        dlam_ref[...] += sums[2]

    full = lambda a_: pl.BlockSpec(a_.shape, lambda i, nd=a_.ndim: (0,) * nd)
    vec = pl.BlockSpec((1, w), lambda i: (0, 0))
    mat = pl.BlockSpec((LRU_BLOCKS, LRU_BW, LRU_BW), lambda i: (0, 0, 0))
    return pl.pallas_call(
        body, grid=(nblk,), name="rg_gates_bwd",
        in_specs=[cur, cur, nxt, cur, prev, cur] + [full(a_) for a_ in (wa, wi, ba, bi, lam)],
        out_specs=[cur, mat, mat, vec, vec, vec],
        out_shape=[SDS((t, w), f32), SDS((LRU_BLOCKS, LRU_BW, LRU_BW), f32), SDS((LRU_BLOCKS, LRU_BW, LRU_BW), f32),
                   SDS((1, w), f32), SDS((1, w), f32), SDS((1, w), f32)],
        compiler_params=_cparams(("arbitrary",)))(dhs, c, c, hs, hs, xc, wa, wi, ba, bi, lam)


def _rg_conv_bwd(dxc, proj, conv_w, tr=512):
    t, w = dxc.shape
    tr = min(tr, t)
    nblk = t // tr
    cur, _, nxt = _halo_specs(tr, w, 0, nblk)
    xcur, xprev, _ = _halo_specs(tr, w, 1, nblk)

    def body(d_ref, dn_ref, x_ref, xp_ref, cw_ref, dx_ref, dcw_ref, dcb_ref):
        i = pl.program_id(0)
        d = d_ref[...]
        x = x_ref[...]
        after = _taps_after(d, dn_ref[...], i == nblk - 1)
        before = _taps_before(x, xp_ref[...], i == 0) + [x]
        dx = cw_ref[3:4, :] * d
        for s in (1, 2, 3):
            dx = dx + cw_ref[3 - s:4 - s, :] * after[s - 1]
        dx_ref[...] = dx.astype(bf16)
        dcw = jnp.concatenate([jnp.sum(d * before[k], axis=0, keepdims=True) for k in range(4)], axis=0)
        dcb = jnp.sum(d, axis=0, keepdims=True)

        @pl.when(i == 0)
        def _():
            dcw_ref[...] = dcw
            dcb_ref[...] = dcb

        @pl.when(i > 0)
        def _():
            dcw_ref[...] += dcw
            dcb_ref[...] += dcb

    return pl.pallas_call(
        body, grid=(nblk,), name="rg_conv_bwd",
        in_specs=[cur, nxt, xcur, xprev, pl.BlockSpec((4, w), lambda i: (0, 0))],
        out_specs=[cur, pl.BlockSpec((4, w), lambda i: (0, 0)), pl.BlockSpec((1, w), lambda i: (0, 0))],
        out_shape=[SDS((t, w), bf16), SDS((4, w), f32), SDS((1, w), f32)],
        compiler_params=_cparams(("arbitrary",)))(dxc, dxc, proj, proj, conv_w)


def _attn_fwd(h, wts, j, plan):
    proj = _mm_cols("attn_in", h, wts["attn_w_in"], j, bf16)
    kpad = jnp.pad(proj[:, A_W:2 * A_W], ((PAD_A, 0), (0, 0)))
    vpad = jnp.pad(proj[:, 2 * A_W:3 * A_W], ((PAD_A, 0), (0, 0)))
    bias = _bias_window(wts["attn_rel_bias"][j])
    plan = plan if j == 0 else None
    out_a = _carried(plan, "chunk_attn_fwd", wts, _chunk_attn_fwd, proj, kpad, vpad, bias)
    out_b, out_b32 = _carried(plan, "sb_attn_fwd", wts, _sb_fwd, proj)
    m = _mm_rows("attn_out", [out_a, out_b], wts["attn_w_out"], j, f32)
    return m, (proj, kpad, vpad, bias, out_a, out_b, out_b32)


def _attn_bwd(dm, h, saved, wts, j, grads, exch):
    proj, kpad, vpad, bias, out_a, out_b, out_b32 = saved
    dout = _mm_rows_t("attn_out_t", dm, wts["attn_w_out"], j, bf16)
    gi, ll = _grad_slot("attn_w_out", j)
    grads["attn_w_out"][gi] = _mm_wgrad("attn_out_wgrad_a", out_a, dm, grads["attn_w_out"][gi], ll, 0)
    grads["attn_w_out"][gi] = _mm_wgrad("attn_out_wgrad_b", out_b, dm, grads["attn_w_out"][gi], ll, 1)
    if exch is not None and j == 0:
        (dqa, dka, dva, dbias), got = _chunk_attn_bwd(proj, kpad, vpad, bias, dout, exch.upper_carry(grads))
        exch.upper_got(got)
        (dqs, dks, dvs), slots = _sb_bwd(proj, out_b32, dout, exch.carry())
        exch.carried(slots)
    else:
        dqa, dka, dva, dbias = _chunk_attn_bwd(proj, kpad, vpad, bias, dout, None)[0]
        dqs, dks, dvs = _sb_bwd(proj, out_b32, dout, None)[0]
    grads["attn_rel_bias"][j] = _bias_window_grad(dbias)
    dproj = jnp.concatenate([dqa, dka[PAD_A:].astype(bf16), dva[PAD_A:].astype(bf16),
                             dqs, dks.astype(bf16), dvs.astype(bf16)], axis=1)
    grads["attn_w_in"][gi] = _mm_wgrad_cols("attn_in_wgrad", h, dproj, grads["attn_w_in"][gi], ll)
    return _mm_cols_t("attn_in_t", dproj, wts["attn_w_in"], j, f32)


def _rg_fwd(h, wts, j, plan):
    proj =_mm_cols("rg_in", h, wts["rg_w_in"], j, f32)
    small = [wts[k][j] for k in ("rg_conv_w", "rg_conv_b", "rg_w_a", "rg_w_i", "rg_b_a", "rg_b_i", "rg_lambda")]
    xc, a, u = _rg_gates_fwd(proj, *small)
    hs = _lru_scan("lru_scan_fwd", a, u, False)
    yp = _rows("rg_gate_out", lambda hv, gv: hv * _gelu(gv), [hs, (proj, D_MODEL, 0)], [], [(D_MODEL, bf16)])[0]
    m = _mm_rows("rg_out", [yp], wts["rg_w_out"], j, f32)
    return m, (proj, xc, a, hs, yp)


def _rg_bwd(dm, h, saved, wts, j, grads, exch):
    proj, xc, a, hs, yp = saved
    dyp = _mm_rows_t("rg_out_t", dm, wts["rg_w_out"], j, f32)
    gi, ll = _grad_slot("rg_w_out", j)
    grads["rg_w_out"][gi] = _mm_wgrad("rg_out_wgrad", yp, dm, grads["rg_w_out"][gi], ll)

    def gate_bwd(dy, hv, gv, av):
        dhs = dy * _gelu(gv)
        return dhs, av * dhs, dy * hv * _gelu_grad(gv)

    dhs, ab, dgate = _rows("rg_gate_out_bwd", gate_bwd, [dyp, hs, (proj, D_MODEL, 0), a], [],
                           [(D_MODEL, f32), (D_MODEL, f32), (D_MODEL, bf16)])
    c = _lru_scan("lru_scan_bwd", a, ab, True)
    wa, wi, ba, bi, lam = [wts[k][j] for k in ("rg_w_a", "rg_w_i", "rg_b_a", "rg_b_i", "rg_lambda")]
    dxc, dwa, dwi, dba, dbi, dlam = _rg_gates_bwd(dhs, c, hs, xc, wa, wi, ba, bi, lam)
    dxr, dcw, dcb = _rg_conv_bwd(dxc, proj, wts["rg_conv_w"][j])
    for k, v in (("rg_w_a", dwa), ("rg_w_i", dwi), ("rg_b_a", dba), ("rg_b_i", dbi), ("rg_lambda", dlam),
                 ("rg_conv_w", dcw), ("rg_conv_b", dcb)):
        grads[k][j] = v
    dproj = jnp.concatenate([dgate, dxr], axis=1)
    grads["rg_w_in"][gi] = _mm_wgrad_cols("rg_in_wgrad", h, dproj, grads["rg_w_in"][gi], ll)
    return _mm_cols_t("rg_in_t", dproj, wts["rg_w_in"], j, f32)


def _local_step(x, target, wts, plan=None, exch=None):
    t = x.shape[0]
    d = D_MODEL
    gains = {k: wts[k] for k in ("norm_mix_pre", "norm_mix_post", "norm_ffn_pre", "norm_ffn_post")}
    gain = lambda k, l: gains[k][l:l + 1]

    saved = []
    h = _rows("norm_in", _norm_fwd, [x], [gain("norm_mix_pre", 0)], [(d, bf16)])[0]
    loss_cols = None
    for l in range(DEPTH):
        j = l // 2
        m, mix_saved = (_attn_fwd if l % 2 == 0 else _rg_fwd)(h, wts, j, plan)

        def resid_next(xv, mv, g_post, g_next):
            x1 = xv + _norm_fwd(mv, g_post)
            return x1, _norm_fwd(x1, g_next)

        x1, h2 = _rows("resid_mix", resid_next, [x, m], [gain("norm_mix_post", l), gain("norm_ffn_pre", l)],
                       [(d, f32), (d, bf16)])
        g, u, hid = _carried(plan if l == 0 else None, "ffn_up", wts, _ffn_up, h2, wts["ffn_w_gate"],
                             wts["ffn_w_up"], l)
        f = _ffn_down(hid, wts["ffn_w_down"], l)
        saved.append((x, h, m, mix_saved, x1, h2, g, u, hid, f))
        if l + 1 < DEPTH:
            x, h = _rows("resid_ffn", resid_next, [x1, f], [gain("norm_ffn_post", l), gain("norm_mix_pre", l + 1)],
                         [(d, f32), (d, bf16)])
        else:
            def resid_loss(xv, fv, tv, g_post):
                err = xv + _norm_fwd(fv, g_post) - tv
                return err * (1.0 / d), jnp.sum(err * err, axis=0, keepdims=True)

            dx, loss_cols = _rows("resid_loss", resid_loss, [x1, f, target], [gain("norm_ffn_post", l)],
                                  [(d, f32)], [((1, d), f32)])
    loss = 0.5 * jnp.sum(loss_cols) / d

    grads = {k: {} for k in SMALL_GRADS}
    for k in BIG_GRADS:
        shp = wts[k].shape
        rest = shp[2:] if shp[1] == 1 else shp[1:]
        grads[k] = [_Fresh((LOWER_LAYERS[k],) + rest), _Fresh((shp[0] - LOWER_LAYERS[k],) + rest)]

    def norm_bwd_cast(uv, dyv, gv):
        du, dg = _norm_bwd(uv, dyv, gv)
        return du, dg

    def norm_bwd_resid(uv, dhv, dxv, gv):
        du, dg = _norm_bwd(uv, dhv, gv)
        return dxv + du, dg

    def norm_bwd_pair(uv, dhv, dxv, nv, g_pre, g_post):
        dx_, dg_pre = norm_bwd_resid(uv, dhv, dxv, g_pre)
        dn, dg_post = _norm_bwd(nv, dx_, g_post)
        return dx_, dn, dg_pre, dg_post

    df = None
    for l in reversed(range(DEPTH)):
        j = l // 2
        x_in, h, m, mix_saved, x1, h2, g, u, hid, f = saved[l]
        if df is None:
            df, grads["norm_ffn_post"][l] = _rows("norm_ffn_post_bwd", norm_bwd_cast, [f, dx],
                                                  [gain("norm_ffn_post", l)], [(d, bf16)], [((1, d), f32)])
        dg, du = _ffn_down_bwd(df, wts["ffn_w_down"], l, g, u, None)[0]
        gi, ll = _grad_slot("ffn_w_down", l)
        grads["ffn_w_down"][gi] = _ffn_wgrad_down(hid, df, grads["ffn_w_down"][gi], ll)
        dh2 = _ffn_up_bwd(dg, du, wts["ffn_w_gate"], wts["ffn_w_up"], l)
        grads["ffn_w_gate"][gi], grads["ffn_w_up"][gi] = _ffn_wgrad_up(
            h2, dg, du, grads["ffn_w_gate"][gi], grads["ffn_w_up"][gi], ll)
        dx1, dm, grads["norm_ffn_pre"][l], grads["norm_mix_post"][l] = _rows(
            "norm_ffn_mix_bwd", norm_bwd_pair, [x1, dh2, dx, m], [gain("norm_ffn_pre", l), gain("norm_mix_post", l)],
            [(d, f32), (d, bf16)], [((1, d), f32), ((1, d), f32)])
        dh = (_attn_bwd if l % 2 == 0 else _rg_bwd)(dm, h, mix_saved, wts, j, grads, exch)
        if l > 0:
            dx, df, grads["norm_mix_pre"][l], grads["norm_ffn_post"][l - 1] = _rows(
                "norm_mix_ffn_bwd", norm_bwd_pair, [x_in, dh, dx1, saved[l - 1][9]],
                [gain("norm_mix_pre", l), gain("norm_ffn_post", l - 1)],
                [(d, f32), (d, bf16)], [((1, d), f32), ((1, d), f32)])
        else:
            dx, grads["norm_mix_pre"][l] = _rows("norm_mix_pre_bwd", norm_bwd_resid, [x_in, dh, dx1],
                                                 [gain("norm_mix_pre", l)], [(d, f32)], [((1, d), f32)])
    return loss, dx, grads


ANY = pl.BlockSpec(memory_space=pl.ANY)
PACK_COLS = 1024
SMALL_ROWS = 288


def _mesh_pos():
    x, y, c = lax.axis_index("x"), lax.axis_index("y"), lax.axis_index("c")
    return x, y, c, [(1 - x, y), (x, 1 - y), (1 - x, 1 - y)]


def _run_copies(copies):
    for cp in copies:
        cp.start()
    for cp in copies:
        cp.wait()


GATHER_SEMS = 7


def _gather_copies(items, ins, outs, send, recv):
    x, y, c, chips = _mesh_pos()
    q = 2 * x + y
    sibling = (x, y, 1 - c)

    def copy(k, src, dst, to):
        return pltpu.make_async_remote_copy(src_ref=src, dst_ref=dst, send_sem=send.at[k], recv_sem=recv.at[k],
                                            device_id=to, device_id_type=MESH)

    own, sent, passed = [], [], []
    for i, (t, l0, nl) in enumerate(items):
        lay = pl.ds(l0, nl)
        half = ins[t].shape[1] // 2
        rows = pl.ds(pl.multiple_of(c * half, half), half)
        own.append(copy(GATHER_SEMS * i, ins[t].at[lay], outs[t].at[lay, q], sibling))
        for j, (px, py) in enumerate(chips):
            sent.append(copy(GATHER_SEMS * i + 1 + j, ins[t].at[lay, rows], outs[t].at[lay, q, rows], (px, py, c)))
            landed = outs[t].at[lay, 2 * px + py, rows]
            passed.append(copy(GATHER_SEMS * i + 4 + j, landed, landed, sibling))
    return own, sent, passed


def _gather_start(items, ins, outs, send, recv):
    own, sent, _ = _gather_copies(items, ins, outs, send, recv)
    for cp in own + sent:
        cp.start()


def _gather_finish(items, ins, outs, send, recv):
    own, sent, passed = _gather_copies(items, ins, outs, send, recv)
    for arrived, forward in zip(sent, passed):
        arrived.wait_recv()
        forward.start()
    for cp in sent:
        cp.wait_send()
    for cp in own + passed:
        cp.wait()


def _gather_call(items, shards):
    n = len(shards)
    nsem = GATHER_SEMS * len(items)

    def body(*refs):
        ins, outs = refs[:n], refs[n:2 * n]
        _gather_start(items, ins, outs, *refs[2 * n:])
        _gather_finish(items, ins, outs, *refs[2 * n:])

    return pl.pallas_call(
        body, name="weight_all_gather", in_specs=[ANY] * n, out_specs=[ANY] * n,
        out_shape=[SDS((s.shape[0], N_CHIPS) + s.shape[1:], s.dtype) for s in shards],
        scratch_shapes=[pltpu.SemaphoreType.DMA((nsem,)), pltpu.SemaphoreType.DMA((nsem,))])(*shards)


def _call(body, operands, *, name, grid, in_specs, out_specs, out_shape, sem, scratch=(), gather=None):
    if gather is None:
        return pl.pallas_call(body, grid=grid, in_specs=in_specs, out_specs=out_specs, out_shape=out_shape,
                              scratch_shapes=list(scratch), name=name, compiler_params=_cparams(sem))(*operands), None
    start, finish, c_ins, c_io, c_new, nsem = gather
    n_in, n_out, n_scr = len(operands), len(out_shape), len(scratch)
    ni, nio, nco = len(c_ins), len(c_io), len(c_io) + len(c_new)

    def full(*refs):
        ins, sh = refs[:n_in], refs[n_in:n_in + ni]
        outs = refs[n_in + ni + nio:n_in + ni + nio + n_out]
        co = refs[n_in + ni + nio + n_out:n_in + ni + nio + n_out + nco]
        scr = refs[n_in + ni + nio + n_out + nco:]
        ids = [pl.program_id(a) for a in range(len(grid))]
        first = functools.reduce(jnp.logical_and, [i == 0 for i in ids])
        last = functools.reduce(jnp.logical_and, [i == g - 1 for i, g in zip(ids, grid)])

        @pl.when(first)
        def _():
            start(sh, co, scr[n_scr], scr[n_scr + 1])

        body(*ins, *outs, *scr[:n_scr])

        @pl.when(last)
        def _():
            finish(sh, co, scr[n_scr], scr[n_scr + 1])

    res = pl.pallas_call(
        full, grid=grid, in_specs=list(in_specs) + [ANY] * (ni + nio), out_specs=list(out_specs) + [ANY] * nco,
        out_shape=list(out_shape) + [SDS(g.shape, g.dtype) for g in list(c_io) + list(c_new)],
        scratch_shapes=list(scratch) + [pltpu.SemaphoreType.DMA((nsem,)), pltpu.SemaphoreType.DMA((nsem,))],
        input_output_aliases={n_in + ni + t: n_out + t for t in range(nio)}, name=name,
        compiler_params=_cparams(("arbitrary",) * len(grid)))(*operands, *c_ins, *c_io)
    return res[:n_out], res[n_out:]


def _pair_exchange(gs):
    n = len(gs)

    def body(*refs):
        _pair_copies(refs[:n], refs[n:2 * n], *refs[2 * n:], start=True)
        _pair_copies(refs[:n], refs[n:2 * n], *refs[2 * n:], start=False)

    return pl.pallas_call(
        body, name="grad_pair_exchange", in_specs=[ANY] * n, out_specs=[ANY] * n,
        out_shape=_pair_shapes(gs),
        scratch_shapes=[pltpu.SemaphoreType.DMA((n,)), pltpu.SemaphoreType.DMA((n,))])(*gs)


def _pair_shapes(gs):
    return [SDS(g.shape[:2] + (g.shape[2] // 2, g.shape[3]), f32) for g in gs]


def _pair_copies(ins, outs, send, recv, start):
    x, y, c, _ = _mesh_pos()
    for t in range(len(ins)):
        half = ins[t].shape[2] // 2
        src = ins[t].at[:, :, pl.ds(pl.multiple_of((1 - c) * half, SUBLANES), half)]
        cp = pltpu.make_async_remote_copy(src_ref=src, dst_ref=outs[t], send_sem=send.at[t], recv_sem=recv.at[t],
                                          device_id=(x, y, 1 - c), device_id_type=MESH)
        cp.start() if start else cp.wait()


def _pair_carry(gs):
    return (functools.partial(_pair_copies, start=True), functools.partial(_pair_copies, start=False),
            gs, [], _pair_shapes(gs), len(gs))


def _pair_sum(name, g, got, c):
    l, s, r, cols = g.shape

    def body(c_ref, a_ref, b_ref, o_ref):
        o_ref[...] = (a_ref[...] + b_ref[...]).astype(bf16)

    blk = (None, None, r // 2, cols)
    return pl.pallas_call(
        body, name=name, out_shape=SDS(got.shape, bf16),
        grid_spec=pltpu.PrefetchScalarGridSpec(
            num_scalar_prefetch=1, grid=(l, s),
            in_specs=[pl.BlockSpec(blk, lambda i, q, c_ref: (i, q, c_ref[0], 0)),
                      pl.BlockSpec(blk, lambda i, q, c_ref: (i, q, 0, 0))],
            out_specs=pl.BlockSpec(blk, lambda i, q, c_ref: (i, q, 0, 0))),
        compiler_params=_cparams(("parallel", "parallel")))(c, g, got)


def _chip_exchange(hs):
    n = len(hs)

    def body(*refs):
        _chip_copies(refs[:n], refs[n:2 * n], *refs[2 * n:], start=True)
        _chip_copies(refs[:n], refs[n:2 * n], *refs[2 * n:], start=False)

    return pl.pallas_call(
        body, name="grad_chip_exchange", in_specs=[ANY] * n, out_specs=[ANY] * n,
        out_shape=[SDS(h.shape, h.dtype) for h in hs],
        scratch_shapes=[pltpu.SemaphoreType.DMA((3 * n,)), pltpu.SemaphoreType.DMA((3 * n,))])(*hs)


def _chip_copies(ins, outs, send, recv, start):
    x, y, c, chips = _mesh_pos()
    q = 2 * x + y
    for t in range(len(ins)):
        for j, (px, py) in enumerate(chips):
            cp = pltpu.make_async_remote_copy(
                src_ref=ins[t].at[:, 2 * px + py], dst_ref=outs[t].at[:, q], send_sem=send.at[3 * t + j],
                recv_sem=recv.at[3 * t + j], device_id=(px, py, c), device_id_type=MESH)
            cp.start() if start else cp.wait()


def _chip_carry(hs):
    return (functools.partial(_chip_copies, start=True), functools.partial(_chip_copies, start=False),
            hs, [], [SDS(h.shape, h.dtype) for h in hs], 3 * len(hs))


def _chip_sum(name, s, h, pos, l0, layers, into):
    l, _, r, cols = s.shape

    def body(pos_ref, s0, s1, s2, s3, own_ref, *rest):
        vals = [jnp.where(pos_ref[0] == p, own_ref[...], ref[...]).astype(f32) for p, ref in enumerate((s0, s1, s2, s3))]
        rest[-1][...] = ((vals[0] + vals[1]) + vals[2]) + vals[3]

    blk = (None, None, r, cols)
    slot = lambda p: pl.BlockSpec(blk, lambda i, pos_ref: (i, jnp.where(pos_ref[0] == p, (p + 1) % N_CHIPS, p), 0, 0))
    extra, alias = ([], {}) if into is None else ([into], {6: 0})
    return pl.pallas_call(
        body, name=name, out_shape=SDS((layers, 2 * r, cols), f32), input_output_aliases=alias,
        grid_spec=pltpu.PrefetchScalarGridSpec(
            num_scalar_prefetch=1, grid=(l,),
            in_specs=[slot(p) for p in range(N_CHIPS)] + [pl.BlockSpec(blk, lambda i, pos_ref: (i, pos_ref[0], 0, 0))]
            + [ANY] * len(extra),
            out_specs=pl.BlockSpec((None, r, cols), lambda i, pos_ref: (l0 + i, pos_ref[1], 0))),
        compiler_params=_cparams(("parallel",)))(pos, s, s, s, s, h, *extra)


def _pair_gather(fulls):
    n = len(fulls)

    def body(*refs):
        ins, outs = refs[:n], refs[n:2 * n]
        send, recv = refs[2 * n:]
        x, y, c, _ = _mesh_pos()
        copies = []
        for t in range(n):
            half = outs[t].shape[1] // 2
            rows = outs[t].at[:, pl.ds(pl.multiple_of(c * half, SUBLANES), half)]
            copies.append(pltpu.make_async_remote_copy(
                src_ref=rows, dst_ref=rows, send_sem=send.at[t], recv_sem=recv.at[t],
                device_id=(x, y, 1 - c), device_id_type=MESH))
        _run_copies(copies)

    return pl.pallas_call(
        body, name="grad_pair_gather", in_specs=[ANY] * n, out_specs=[ANY] * n,
        out_shape=[SDS(f.shape, f32) for f in fulls], input_output_aliases={t: t for t in range(n)},
        scratch_shapes=[pltpu.SemaphoreType.DMA((n,)), pltpu.SemaphoreType.DMA((n,))])(*fulls)


COL_SHARDED = ("attn_w_in", "rg_w_in", "ffn_w_gate", "ffn_w_up")
ROW_SHARDED = ("attn_w_out", "rg_w_out")
GATES = ("rg_w_a", "rg_w_i")
VECTORS = ("rg_conv_w", "rg_conv_b", "rg_b_a", "rg_b_i", "rg_lambda")
REPLICATED = ("norm_mix_pre", "norm_mix_post", "norm_ffn_pre", "norm_ffn_post", "attn_rel_bias")
BIG_GRADS = COL_SHARDED + ROW_SHARDED + ("ffn_w_down",)
SMALL_GRADS = GATES + VECTORS + REPLICATED
WEIGHTS =("attn_w_in", "attn_rel_bias", "attn_w_out", "rg_w_in", "rg_conv_w", "rg_conv_b", "rg_w_a", "rg_b_a",
           "rg_w_i", "rg_b_i", "rg_lambda", "rg_w_out", "norm_mix_pre", "norm_mix_post", "norm_ffn_pre",
           "norm_ffn_post", "ffn_w_gate", "ffn_w_up", "ffn_w_down")
SMALL = VECTORS + REPLICATED


GATHER_PARTS = {
    "first": (("attn_w_in", 0, 1), ("attn_w_out", 0, 1), ("rg_w_a", 0, 8), ("rg_w_i", 0, 8), ("vec", 0, 1)),
    "chunk_attn_fwd": (("ffn_w_gate", 0, 1), ("ffn_w_up", 0, 1), ("ffn_w_down", 0, 1), ("rg_w_in", 0, 1),
                       ("rg_w_out", 0, 1)),
    "sb_attn_fwd": (("ffn_w_gate", 1, 3), ("ffn_w_up", 1, 3), ("ffn_w_down", 1, 3)),
    "ffn_up": (("rg_w_in", 1, 1), ("rg_w_out", 1, 1), ("attn_w_in", 1, 1), ("attn_w_out", 1, 1)),
}


TRANSPOSED = ("ffn_w_gate", "ffn_w_up")


def _natural(name, a):
    return jnp.swapaxes(a, 1, 2) if name in TRANSPOSED else a


class _WeightGather:
    def __init__(self, w):
        self.w = w
        self.names = list(COL_SHARDED + ROW_SHARDED + GATES + ("ffn_w_down", "vec"))
        self.shards = {}
        for k in self.names[:-1]:
            a = _natural(k, w[k]).astype(bf16)
            self.shards[k] = a.reshape((-1,) + a.shape[-2:])
        self.shards["vec"] = jnp.concatenate([w[k].reshape(-1) for k in VECTORS]).reshape(1, -1, LANES)
        got = _gather_call(self._items("first", self.names), [self.shards[k] for k in self.names])
        self.raw = dict(zip(self.names, got))

    @staticmethod
    def _items(part, names):
        return [(names.index(k), l0, nl) for k, l0, nl in GATHER_PARTS[part]]

    def part(self, part):
        names = list(dict.fromkeys(k for k, _, _ in GATHER_PARTS[part]))
        items = self._items(part, names)
        return (functools.partial(_gather_start, items), functools.partial(_gather_finish, items),
                [self.shards[k] for k in names], [self.raw[k] for k in names], [], GATHER_SEMS * len(items)), names

    def views(self):
        got, w = self.raw, self.w
        out = {k: w[k] for k in REPLICATED}
        for k in COL_SHARDED + ("ffn_w_down",):
            out[k] = got[k]
        for k in ROW_SHARDED:
            l, s, ks, n = got[k].shape
            out[k] = got[k].reshape(l, 1, s * ks, n)
        for k in GATES:
            out[k] = got[k].reshape(2, LRU_BLOCKS, LRU_BW, LRU_BW)
        vec = got["vec"].reshape(N_CHIPS, -1)
        off = 0
        for k in VECTORS:
            shp = w[k].shape
            n = int(np.prod(shp))
            piece = vec[:, off:off + n].reshape((N_CHIPS,) + shp)
            off += n
            if k == "rg_conv_w":
                out[k] = piece.reshape(N_CHIPS, 2, 4, 256).transpose(1, 2, 0, 3).reshape(2, 4, D_MODEL)
            elif k in ("rg_b_a", "rg_b_i"):
                out[k] = piece.transpose(1, 2, 0, 3).reshape(2, 1, D_MODEL)
            else:
                out[k] = piece.transpose(1, 0, 2).reshape(2, 1, D_MODEL)
        return out


def _carried(plan, part, wts, fn, *args):
    if plan is None:
        return fn(*args, None)[0]
    gather, names = plan.part(part)
    out, new = fn(*args, gather)
    plan.raw.update(zip(names, new))
    wts.update(plan.views())
    return out


def _grad_blocks(name, g):
    st = jnp.stack([g[i] for i in sorted(g)])
    if name in GATES:
        st = st.reshape(2, LRU_BLOCKS, N_CHIPS, LRU_BW // N_CHIPS, LRU_BW).transpose(2, 0, 1, 3, 4)
    elif name == "rg_conv_w":
        st = st.reshape(2, 4, N_CHIPS, -1).transpose(2, 0, 1, 3)
    elif name in ("rg_b_a", "rg_b_i"):
        st = st.reshape(2, LRU_BLOCKS, N_CHIPS, -1).transpose(2, 0, 1, 3)
    elif name in VECTORS:
        st = st.reshape(2, N_CHIPS, -1).transpose(1, 0, 2)
    else:
        st = jnp.broadcast_to(st.reshape(1, -1), (N_CHIPS, st.size))
    return st.reshape(N_CHIPS, -1)


class _GradExchange:
    def __init__(self):
        self.c = lax.axis_index("c").astype(jnp.int32).reshape(1)
        self.pos = jnp.stack([2 * lax.axis_index("x") + lax.axis_index("y"), lax.axis_index("c")]).astype(jnp.int32)
        self.up = self.got_up = self.parts_up = self.slots_up = None

    @staticmethod
    def _blocked(g):
        if g.ndim == 3:
            g = g.reshape(g.shape[0], N_CHIPS, g.shape[1] // N_CHIPS, g.shape[2])
        return g

    def _sums(self, tag, names, gs, got):
        return [_pair_sum("grad_pair_sum_" + tag + k, g, r, self.c) for k, g, r in zip(names, gs, got)]

    def upper_carry(self, grads):
        self.up = [self._blocked(grads[k][1]) for k in BIG_GRADS]
        return _pair_carry(self.up)

    def upper_got(self, got):
        self.got_up = got

    def carry(self):
        self.parts_up = self._sums("up_", BIG_GRADS, self.up, self.got_up)
        return _chip_carry(self.parts_up)

    def carried(self, slots):
        self.slots_up = slots

    def finish(self, grads, shard_shapes):
        if self.got_up is None:
            self.upper_carry(grads)
            self.got_up = _pair_exchange(self.up)
        if self.slots_up is None:
            self.carry()
            self.slots_up = _chip_exchange(self.parts_up)
        blocks = [_grad_blocks(k, grads[k]) for k in SMALL_GRADS]
        used = sum(b.shape[1] for b in blocks)
        small = jnp.concatenate(blocks + [jnp.zeros((N_CHIPS, SMALL_ROWS * PACK_COLS - used), f32)], axis=1)
        names = tuple(k for k in BIG_GRADS if LOWER_LAYERS[k]) + ("small",)
        gs = [self._blocked(grads[k][0]) for k in names[:-1]] + [small.reshape(1, N_CHIPS, SMALL_ROWS, PACK_COLS)]
        parts = dict(zip(names, self._sums("lo_", names, gs, _pair_exchange(gs))))
        slots = dict(zip(names, _chip_exchange([parts[k] for k in names])))
        fulls = []
        for i, k in enumerate(BIG_GRADS):
            nlo, nup = LOWER_LAYERS[k], self.parts_up[i].shape[0]
            full = _chip_sum("grad_chip_sum_up_" + k, self.slots_up[i], self.parts_up[i], self.pos, nlo, nlo + nup, None)
            if nlo:
                full = _chip_sum("grad_chip_sum_lo_" + k, slots[k], parts[k], self.pos, 0, nlo + nup, full)
            fulls.append(full)
        fulls.append(_chip_sum("grad_chip_sum_lo_small", slots["small"], parts["small"], self.pos, 0, 1, None))
        full = _pair_gather(fulls)
        out = {k: f.reshape(shard_shapes[k]) for k, f in zip(BIG_GRADS, full)}
        flat, off = full[-1].reshape(-1), 0
        for k in SMALL_GRADS:
            n = int(np.prod(shard_shapes[k]))
            out[k] = flat[off:off + n].reshape(shard_shapes[k])
            off += n
        return out


def _adamw_fn(w, g, m, v):
    m = ADAM_B1 * m + (1.0 - ADAM_B1) * g
    v = ADAM_B2 * v + (1.0 - ADAM_B2) * (g * g)
    m_hat = m / (1.0 - ADAM_B1 ** ADAM_STEP)
    v_hat = v / (1.0 - ADAM_B2 ** ADAM_STEP)
    return -ADAM_LR * (m_hat / (jnp.sqrt(v_hat) + ADAM_EPS) + ADAM_WD * w), m, v


def _adamw(name, w, g, m, v):
    shp = w.shape
    if w.size >= 1 << 16:
        width = shp[-1]
        ops = [a.reshape(-1, width) for a in (w, g, m, v)]
        res = _rows(name, _adamw_fn, ops, [], [(width, f32)] * 3)
        return [r.reshape(shp) for r in res]
    n = w.size
    rows = -(-n // (SUBLANES * LANES)) * SUBLANES
    ops = [jnp.pad(a.reshape(-1), (0, rows * LANES - n)).reshape(rows, LANES) for a in (w, g, m, v)]
    res = _rows(name, _adamw_fn, ops, [], [(LANES, f32)] * 3, tr=rows)
    return [r.reshape(-1)[:n].reshape(shp) for r in res]


def kernel(x, attn_w_in, attn_rel_bias, attn_w_out, rg_w_in, rg_conv_w, rg_conv_b, rg_w_a, rg_b_a, rg_w_i, rg_b_i, rg_lambda, rg_w_out, norm_mix_pre, norm_mix_post, norm_ffn_pre, norm_ffn_post, ffn_w_gate, ffn_w_up, ffn_w_down, loss_target, m_attn_w_in, m_attn_rel_bias, m_attn_w_out, m_rg_w_in, m_rg_conv_w, m_rg_conv_b, m_rg_w_a, m_rg_b_a, m_rg_w_i, m_rg_b_i, m_rg_lambda, m_rg_w_out, m_norm_mix_pre, m_norm_mix_post, m_norm_ffn_pre, m_norm_ffn_post, m_ffn_w_gate, m_ffn_w_up, m_ffn_w_down, v_attn_w_in, v_attn_rel_bias, v_attn_w_out, v_rg_w_in, v_rg_conv_w, v_rg_conv_b, v_rg_w_a, v_rg_b_a, v_rg_w_i, v_rg_b_i, v_rg_lambda, v_rg_w_out, v_norm_mix_pre, v_norm_mix_post, v_norm_ffn_pre, v_norm_ffn_post, v_ffn_w_gate, v_ffn_w_up, v_ffn_w_down):
    w = dict(zip(WEIGHTS, (attn_w_in, attn_rel_bias, attn_w_out, rg_w_in, rg_conv_w, rg_conv_b, rg_w_a, rg_b_a, rg_w_i,
                           rg_b_i, rg_lambda, rg_w_out, norm_mix_pre, norm_mix_post, norm_ffn_pre, norm_ffn_post,
                           ffn_w_gate, ffn_w_up, ffn_w_down)))
    m = dict(zip(WEIGHTS, (m_attn_w_in, m_attn_rel_bias, m_attn_w_out, m_rg_w_in, m_rg_conv_w, m_rg_conv_b, m_rg_w_a,
                           m_rg_b_a, m_rg_w_i, m_rg_b_i, m_rg_lambda, m_rg_w_out, m_norm_mix_pre, m_norm_mix_post,
                           m_norm_ffn_pre, m_norm_ffn_post, m_ffn_w_gate, m_ffn_w_up, m_ffn_w_down)))
    v = dict(zip(WEIGHTS, (v_attn_w_in, v_attn_rel_bias, v_attn_w_out, v_rg_w_in, v_rg_conv_w, v_rg_conv_b, v_rg_w_a,
                           v_rg_b_a, v_rg_w_i, v_rg_b_i, v_rg_lambda, v_rg_w_out, v_norm_mix_pre, v_norm_mix_post,
                           v_norm_ffn_pre, v_norm_ffn_post, v_ffn_w_gate, v_ffn_w_up, v_ffn_w_down)))
    plan = _WeightGather(w)
    exch = _GradExchange()
    loss, dx, grads = _local_step(x[0], loss_target[0], plan.views(), plan, exch)
    loss = lax.psum(loss, ("x", "y", "c"))
    g = exch.finish(grads, {k: _natural(k, w[k]).shape for k in WEIGHTS})

    big = [k for k in WEIGHTS if k not in SMALL]
    upd = {}
    for k in big:
        res = _adamw("adamw_" + k, _natural(k, w[k]), g[k], _natural(k, m[k]), _natural(k, v[k]))
        upd[k] = [_natural(k, r) for r in res]
        g[k] = _natural(k, g[k])
    cat = lambda d: jnp.concatenate([d[k].reshape(-1) for k in SMALL])
    small = _adamw("adamw_small", cat(w), cat(g), cat(m), cat(v))
    off = 0
    for k in SMALL:
        n = w[k].size
        upd[k] = [r[off:off + n].reshape(w[k].shape) for r in small]
        off += n
    return (loss, dx[None], *[g[k] for k in WEIGHTS], *[upd[k][0] for k in WEIGHTS],
            *[upd[k][1] for k in WEIGHTS], *[upd[k][2] for k in WEIGHTS])
```

```python
import functools

import numpy as np
import jax
import jax.numpy as jnp
from jax import lax
from jax.experimental import pallas as pl
from jax.experimental.pallas import tpu as pltpu

f32 = jnp.float32
bf16 = jnp.bfloat16
SDS = jax.ShapeDtypeStruct
MESH = pl.DeviceIdType.MESH

D_MODEL = 1024
N_CHIPS = 4
DEPTH = 4
HEAD_DIM = 64
CHUNK = 64
N_LEFT = 8
REL_CLIP = 256
A_W = 512
LRU_BLOCKS = 4
LRU_BW = 256
LRU_C = 8.0
D_FF = 2816
RMS_EPS = 1e-6
LANES = 128
SUBLANES = 8
VMEM_LIMIT = 56 * 1024 * 1024

QB_A = 2 * CHUNK
QSUB_A = 8
KW_A = QB_A + N_LEFT * CHUNK
PAD_A = N_LEFT * CHUNK
EXT_A = 768
SB_BLK = 256
QSUB_B = 4
LOG2_E = 1.4426950408889634
SB_DEAD = -160.0

ADAM_LR, ADAM_B1, ADAM_B2, ADAM_EPS, ADAM_WD, ADAM_STEP = 0.001, 0.9, 0.999, 1e-08, 0.01, 10


def _cparams(sem):
    return pltpu.CompilerParams(dimension_semantics=sem, vmem_limit_bytes=VMEM_LIMIT)


def _gemm(name, operands, in_specs, o_spec, out_shape, grid, dims, acc_shape, into=None):
    nred = grid[2]
    npair = len(operands) // 2
    nin = 2 * npair + (into is not None)

    def body(*refs):
        o_ref = refs[nin]
        p = None
        for t in range(npair):
            d = lax.dot_general(refs[2 * t][...], refs[2 * t + 1][...], (dims, ((), ())),
                                preferred_element_type=f32)
            p = d if p is None else p + d
        if nred == 1:
            o_ref[...] = p.astype(o_ref.dtype)
        else:
            acc = refs[nin + 1]
            r = pl.program_id(2)

            @pl.when(r == 0)
            def _():
                acc[...] = p

            @pl.when(r > 0)
            def _():
                acc[...] += p

            @pl.when(r == nred - 1)
            def _():
                o_ref[...] = acc[...].astype(o_ref.dtype)

    scratch = [] if nred == 1 else [pltpu.VMEM(acc_shape, f32)]
    extra, alias = ([], {}) if into is None else ([into], {2 * npair: 0})
    return pl.pallas_call(
        body, grid=grid, in_specs=list(in_specs) + [pl.BlockSpec(memory_space=pl.ANY)] * len(extra),
        out_specs=o_spec, out_shape=out_shape, scratch_shapes=scratch, name=name, input_output_aliases=alias,
        compiler_params=_cparams(("parallel", "parallel", "arbitrary")))(*operands, *extra)


LOWER_LAYERS = {"attn_w_in": 1, "attn_w_out": 0, "rg_w_in": 0, "rg_w_out": 0,
                "ffn_w_gate": 0, "ffn_w_up": 0, "ffn_w_down": 0}


def _grad_slot(name, l):
    n = LOWER_LAYERS[name]
    return (0, l) if l < n else (1, l - n)


class _Fresh:
    def __init__(self, shape):
        self.shape = tuple(shape)


def _into(buf):
    return None if isinstance(buf, _Fresh) else buf


NN = ((1,), (0,))
NT = ((1,), (1,))
TN = ((0,), (0,))


WGRAD_TOKENS = 2048


def _tile(t, want=1024):
    return min(want, t)


def _mm_cols(name, a, w, l, out_dtype):
    t, k = a.shape
    _, s, _, ns = w.shape
    tm = _tile(t, 2048)
    return _gemm(
        name, [a, w],
        [pl.BlockSpec((tm, k), lambda i, j, r: (i, 0)),
         pl.BlockSpec((None, None, k, ns), lambda i, j, r: (l, j, 0, 0))],
        pl.BlockSpec((tm, ns), lambda i, j, r: (i, j)),
        SDS((t, s * ns), out_dtype), (t // tm, s, 1), NN, None)


def _mm_cols_t(name, dy, w, l, out_dtype):
    t = dy.shape[0]
    _, s, k, ns = w.shape
    tm = _tile(t, 2048)
    return _gemm(
        name, [dy, w],
        [pl.BlockSpec((tm, ns), lambda i, j, r: (i, r)),
         pl.BlockSpec((None, None, k, ns), lambda i, j, r: (l, r, 0, 0))],
        pl.BlockSpec((tm, k), lambda i, j, r: (i, 0)),
        SDS((t, k), out_dtype), (t // tm, 1, s), NT, (tm, k))


def _mm_wgrad_cols(name, a, dy, buf, l):
    t, k = a.shape
    _, s, _, ns = buf.shape
    tt = _tile(t, 2 * WGRAD_TOKENS)
    return _gemm(
        name, [a, dy],
        [pl.BlockSpec((tt, k), lambda i, j, r: (r, 0)),
         pl.BlockSpec((tt, ns), lambda i, j, r: (r, i))],
        pl.BlockSpec((None, None, k, ns), lambda i, j, r: (l, i, 0, 0)),
        SDS(buf.shape, f32), (s, 1, t // tt), TN, (k, ns), into=_into(buf))


def _mm_rows(name, parts, w, l, out_dtype):
    t = parts[0].shape[0]
    n = w.shape[3]
    tm = _tile(t, 2048)
    ops, specs = [], []
    for p_i, a in enumerate(parts):
        kp = a.shape[1]
        ops += [a, w]
        specs += [pl.BlockSpec((tm, kp), lambda i, j, r: (i, 0)),
                  pl.BlockSpec((None, None, kp, n), lambda i, j, r, p_i=p_i: (l, 0, p_i, 0))]
    return _gemm(name, ops, specs, pl.BlockSpec((tm, n), lambda i, j, r: (i, 0)),
                 SDS((t, n), out_dtype), (t // tm, 1, 1), NN, None)


def _mm_rows_t(name, dy, w, l, out_dtype):
    t, n = dy.shape
    k = w.shape[2]
    tm = _tile(t, 2048)
    return _gemm(
        name, [dy, w],
        [pl.BlockSpec((tm, n), lambda i, j, r: (i, 0)),
         pl.BlockSpec((None, None, k, n), lambda i, j, r: (l, 0, 0, 0))],
        pl.BlockSpec((tm, k), lambda i, j, r: (i, 0)),
        SDS((t, k), out_dtype), (t // tm, 1, 1), NT, None)


def _mm_wgrad(name, a, dy, buf, l, part=0):
    t, k = a.shape
    n = dy.shape[1]
    tt = _tile(t, 2 * WGRAD_TOKENS)
    return _gemm(
        name, [a, dy],
        [pl.BlockSpec((tt, k), lambda i, j, r: (r, 0)),
         pl.BlockSpec((tt, n), lambda i, j, r: (r, 0))],
        pl.BlockSpec((None, k, n), lambda i, j, r: (l, part, 0)),
        SDS(buf.shape, f32), (1, 1, t // tt), TN, (k, n), into=_into(buf))


def _ffn_up(h, wg, wu, l, gather):
    t, k = h.shape
    s, fs = wg.shape[1], wg.shape[2]
    tm = _tile(t)

    def body(h_ref, wg_ref, wu_ref, g_ref, u_ref, hid_ref):
        hv = h_ref[...]
        g = lax.dot_general(hv, wg_ref[...], (NT, ((), ())), preferred_element_type=f32)
        u = lax.dot_general(hv, wu_ref[...], (NT, ((), ())), preferred_element_type=f32)
        g_ref[...] = g.astype(bf16)
        u_ref[...] = u.astype(bf16)
        hid_ref[...] = (g * jax.nn.sigmoid(g) * u).astype(bf16)

    wspec = pl.BlockSpec((None, None, fs, k), lambda j, i: (l, j, 0, 0))
    ospec = pl.BlockSpec((None, tm, fs), lambda j, i: (j, i, 0))
    return _call(
        body, [h, wg, wu], grid=(s, t // tm), name="ffn_up",
        in_specs=[pl.BlockSpec((tm, k), lambda j, i: (i, 0)), wspec, wspec],
        out_specs=[ospec, ospec, ospec], out_shape=[SDS((s, t, fs), bf16)] * 3,
        sem=("parallel", "parallel"), gather=gather)


def _ffn_down(hid, wd, l):
    s, t, fs = hid.shape
    n = wd.shape[3]
    tm = _tile(t)
    ops, specs = [], []
    for r in range(s):
        ops += [hid, wd]
        specs += [pl.BlockSpec((None, tm, fs), lambda i, j, k, r=r: (r, i, 0)),
                  pl.BlockSpec((None, None, fs, n), lambda i, j, k, r=r: (l, r, 0, 0))]
    return _gemm("ffn_down", ops, specs, pl.BlockSpec((tm, n), lambda i, j, k: (i, 0)),
                 SDS((t, n), f32), (t // tm, 1, 1), NN, None)


def _ffn_down_bwd(df, wd, l, g, u, gather):
    t, n = df.shape
    s, fs = wd.shape[1], wd.shape[2]
    tm = _tile(t)

    def body(df_ref, wd_ref, g_ref, u_ref, dg_ref, du_ref):
        dh = lax.dot_general(df_ref[...], wd_ref[...], (NT, ((), ())), preferred_element_type=f32)
        gv = g_ref[...].astype(f32)
        uv = u_ref[...].astype(f32)
        sg = jax.nn.sigmoid(gv)
        du_ref[...] = (dh * gv * sg).astype(bf16)
        dg_ref[...] = (dh * uv * (sg * (1.0 + gv * (1.0 - sg)))).astype(bf16)

    bspec = pl.BlockSpec((None, tm, fs), lambda j, i: (j, i, 0))
    return _call(
        body, [df, wd, g, u], grid=(s, t // tm), name="ffn_down_bwd",
        in_specs=[pl.BlockSpec((tm, n), lambda j, i: (i, 0)),
                  pl.BlockSpec((None, None, fs, n), lambda j, i: (l, j, 0, 0)), bspec, bspec],
        out_specs=[bspec, bspec], out_shape=[SDS((s, t, fs), bf16)] * 2,
        sem=("parallel", "parallel"), gather=gather)


def _ffn_up_bwd(dg, du, wg, wu, l):
    s, t, fs = dg.shape
    k = wg.shape[3]
    tm = _tile(t, 512)
    ops, specs = [], []
    for r in range(s):
        aspec = pl.BlockSpec((None, tm, fs), lambda i, j, kk, r=r: (r, i, 0))
        wspec = pl.BlockSpec((None, None, fs, k), lambda i, j, kk, r=r: (l, r, 0, 0))
        ops += [dg, wg, du, wu]
        specs += [aspec, wspec, aspec, wspec]
    return _gemm("ffn_up_bwd", ops, specs, pl.BlockSpec((tm, k), lambda i, j, kk: (i, 0)),
                 SDS((t, k), f32), (t // tm, 1, 1), NN, None)


def _ffn_wgrad_up(h, dg, du, buf_g, buf_u, l):
    t, k = h.shape
    s, _, fs = dg.shape
    tt = _tile(t, WGRAD_TOKENS)
    nred = t // tt

    fresh = isinstance(buf_g, _Fresh)

    def body(*refs):
        h_ref, dg_ref, du_ref = refs[:3]
        og_ref, ou_ref, acc_g, acc_u = refs[-4:]
        r = pl.program_id(1)
        hv = h_ref[...]
        pg = lax.dot_general(dg_ref[...], hv, (TN, ((), ())), preferred_element_type=f32)
        pu = lax.dot_general(du_ref[...], hv, (TN, ((), ())), preferred_element_type=f32)

        @pl.when(r == 0)
        def _():
            acc_g[...] = pg
            acc_u[...] = pu

        @pl.when(r > 0)
        def _():
            acc_g[...] += pg
            acc_u[...] += pu

        @pl.when(r == nred - 1)
        def _():
            og_ref[...] = acc_g[...]
            ou_ref[...] = acc_u[...]

    dspec = pl.BlockSpec((None, tt, fs), lambda i, r: (i, r, 0))
    ospec = pl.BlockSpec((None, None, fs, k), lambda i, r: (l, i, 0, 0))
    extra, alias = ([], {}) if fresh else ([buf_g, buf_u], {3: 0, 4: 1})
    return pl.pallas_call(
        body, grid=(s, nred), name="ffn_wgrad_up",
        in_specs=[pl.BlockSpec((tt, k), lambda i, r: (r, 0)), dspec, dspec] + [ANY] * len(extra),
        out_specs=[ospec, ospec], out_shape=[SDS(buf_g.shape, f32), SDS(buf_u.shape, f32)],
        scratch_shapes=[pltpu.VMEM((fs, k), f32)] * 2, input_output_aliases=alias,
        compiler_params=_cparams(("parallel", "arbitrary")))(h, dg, du, *extra)


def _ffn_wgrad_down(hid, df, buf, l):
    s, t, fs = hid.shape
    n = df.shape[1]
    tt = _tile(t, 2 * WGRAD_TOKENS)
    return _gemm(
        "ffn_wgrad_down", [hid, df],
        [pl.BlockSpec((None, tt, fs), lambda i, j, r: (i, r, 0)),
         pl.BlockSpec((tt, n), lambda i, j, r: (r, 0))],
        pl.BlockSpec((None, None, fs, n), lambda i, j, r: (l, i, 0, 0)),
        SDS(buf.shape, f32), (s, 1, t // tt), TN, (fs, n), into=_into(buf))


def _rows(name, fn, rows, consts, row_outs, acc_outs=(), tr=512):
    rows = [r if isinstance(r, tuple) else (r, r.shape[1], 0) for r in rows]
    t = rows[0][0].shape[0]
    tr = max(d for d in range(SUBLANES, min(tr, t) + 1, SUBLANES) if t % d == 0)
    nin = len(rows) + len(consts)
    no, na = len(row_outs), len(acc_outs)

    def body(*refs):
        vals = fn(*[r[...] for r in refs[:nin]])
        if not isinstance(vals, (tuple, list)):
            vals = (vals,)
        for k in range(no):
            refs[nin + k][...] = vals[k].astype(refs[nin + k].dtype)
        first = pl.program_id(0) == 0
        for k in range(na):
            ref, val = refs[nin + no + k], vals[no + k]

            @pl.when(first)
            def _(ref=ref, val=val):
                ref[...] = val

            @pl.when(jnp.logical_not(first))
            def _(ref=ref, val=val):
                ref[...] += val

    in_specs = [pl.BlockSpec((tr, w), lambda i, cb=cb: (i, cb)) for (_, w, cb) in rows]
    in_specs += [pl.BlockSpec(c.shape, lambda i, nd=c.ndim: (0,) * nd) for c in consts]
    out_specs = [pl.BlockSpec((tr, w), lambda i: (i, 0)) for (w, _) in row_outs]
    out_specs += [pl.BlockSpec(s, lambda i, nd=len(s): (0,) * nd) for (s, _) in acc_outs]
    out_shape = [SDS((t, w), dt) for (w, dt) in row_outs] + [SDS(s, dt) for (s, dt) in acc_outs]
    res = pl.pallas_call(
        body, grid=(t // tr,), in_specs=in_specs, out_specs=out_specs, out_shape=out_shape,
        name=name, compiler_params=_cparams(("arbitrary",)))(*[r[0] for r in rows], *consts)
    return res


def _rstd(x):
    return lax.rsqrt(jnp.mean(x * x, axis=-1, keepdims=True) + RMS_EPS)


def _norm_fwd(x, g):
    return x * _rstd(x) * g


def _norm_bwd(u, dy, g):
    r = _rstd(u)
    n = u * r
    dn = dy * g
    du = r * (dn - n * jnp.mean(dn * n, axis=-1, keepdims=True))
    return du, jnp.sum(dy * n, axis=0, keepdims=True)


def _gelu(x):
    c = 0.7978845608028654
    return 0.5 * x * (1.0 + jnp.tanh(c * (x + 0.044715 * x * x * x)))


def _gelu_grad(x):
    c = 0.7978845608028654
    th = jnp.tanh(c * (x + 0.044715 * x * x * x))
    return 0.5 * (1.0 + th) + 0.5 * x * (1.0 - th * th) * c * (1.0 + 3.0 * 0.044715 * x * x)


def _mask_heads(x):
    lane = lax.broadcasted_iota(jnp.int32, x.shape, 1)
    return [jnp.where((lane >= h * HEAD_DIM) & (lane < (h + 1) * HEAD_DIM), x, jnp.zeros_like(x))
            for h in range(LANES // HEAD_DIM)]


def _chunk_valid(start):
    qi = lax.broadcasted_iota(jnp.int32, (QB_A, KW_A), 0)
    kj = lax.broadcasted_iota(jnp.int32, (QB_A, KW_A), 1)
    qc = qi // CHUNK
    kc = kj // CHUNK
    return (kc >= qc) & (kc <= qc + N_LEFT) & (kj + start >= PAD_A)


def _scaled(q):
    return q * (HEAD_DIM ** -0.5)


def _chunk_probs(q, k, bias, valid):
    s = lax.dot_general(q, k, (NT, ((), ())), preferred_element_type=f32) + bias
    s = jnp.where(valid, s, -1e30)
    p = jnp.exp(s - jnp.max(s, axis=-1, keepdims=True))
    return p / jnp.sum(p, axis=-1, keepdims=True)


def _chunk_attn_fwd(proj, kpad, vpad, bias, gather):
    t = proj.shape[0]
    tp = kpad.shape[0]
    step = QSUB_A * QB_A

    def body(q_ref, k_ref, v_ref, b_ref, o_ref):
        for sb in range(QSUB_A):
            start = pl.multiple_of((pl.program_id(1) * QSUB_A + sb) * QB_A, QB_A)
            rows = pl.ds(sb * QB_A, QB_A)
            valid = _chunk_valid(start)
            kw = k_ref[pl.ds(start, KW_A), :]
            qm = _mask_heads(_scaled(q_ref[rows, :]))
            vm = _mask_heads(v_ref[pl.ds(start, KW_A), :])
            o = None
            for h in range(len(qm)):
                p = _chunk_probs(qm[h], kw, b_ref[h], valid)
                d = jnp.dot(p.astype(bf16), vm[h], preferred_element_type=f32)
                o = d if o is None else o + d
            o_ref[rows, :] = o.astype(bf16)

    kv_spec = pl.BlockSpec((tp, LANES), lambda hp, qb: (0, hp))
    outs, new = _call(
        body, [proj, kpad, vpad, bias], grid=(A_W // LANES, t // step), name="chunk_attn_fwd",
        in_specs=[pl.BlockSpec((step, LANES), lambda hp, qb: (qb, hp)), kv_spec, kv_spec,
                  pl.BlockSpec((2, QB_A, KW_A), lambda hp, qb: (hp, 0, 0))],
        out_specs=[pl.BlockSpec((step, LANES), lambda hp, qb: (qb, hp))],
        out_shape=[SDS((t, A_W), bf16)], sem=("parallel", "arbitrary"), gather=gather)
    return outs[0], new


def _chunk_attn_bwd(proj, kpad, vpad, bias, dout, gather):
    t = proj.shape[0]
    tp = kpad.shape[0]
    step = QSUB_A * QB_A

    def body(q_ref, k_ref, v_ref, b_ref, do_ref, dq_ref, dk_ref, dv_ref, db_ref):
        qb = pl.program_id(1)

        @pl.when(qb == 0)
        def _():
            dk_ref[...] = jnp.zeros_like(dk_ref)
            dv_ref[...] = jnp.zeros_like(dv_ref)
            db_ref[...] = jnp.zeros_like(db_ref)

        for sb in range(QSUB_A):
            start = pl.multiple_of((qb * QSUB_A + sb) * QB_A, QB_A)
            rows = pl.ds(sb * QB_A, QB_A)
            win = pl.ds(start, KW_A)
            valid = _chunk_valid(start)
            kw = k_ref[win, :]
            vw = v_ref[win, :]
            qm = _mask_heads(_scaled(q_ref[rows, :]))
            dom = _mask_heads(do_ref[rows, :])
            km = _mask_heads(kw)
            dq = dk = dv = None
            for h in range(len(qm)):
                p = _chunk_probs(qm[h], kw, b_ref[h], valid)
                dp = lax.dot_general(dom[h], vw, (NT, ((), ())), preferred_element_type=f32)
                ds = p * (dp - jnp.sum(dp * p, axis=-1, keepdims=True))
                db_ref[h] += ds
                dsb = ds.astype(bf16)
                terms = (jnp.dot(dsb, km[h], preferred_element_type=f32),
                         lax.dot_general(dsb, qm[h], (TN, ((), ())), preferred_element_type=f32),
                         lax.dot_general(p.astype(bf16), dom[h], (TN, ((), ())), preferred_element_type=f32))
                dq, dk, dv = terms if dq is None else (dq + terms[0], dk + terms[1], dv + terms[2])
            dq_ref[rows, :] = _scaled(dq).astype(bf16)
            dk_ref[win, :] += dk
            dv_ref[win, :] += dv

    kv_spec = pl.BlockSpec((tp, LANES), lambda hp, qb: (0, hp))
    q_spec = pl.BlockSpec((step, LANES), lambda hp, qb: (qb, hp))
    b_spec = pl.BlockSpec((2, QB_A, KW_A), lambda hp, qb: (hp, 0, 0))
    return _call(
        body, [proj, kpad, vpad, bias, dout], grid=(A_W // LANES, t // step), name="chunk_attn_bwd",
        in_specs=[q_spec, kv_spec, kv_spec, b_spec, q_spec],
        out_specs=[q_spec, kv_spec, kv_spec, b_spec],
        out_shape=[SDS((t, A_W), bf16), SDS((tp, A_W), f32), SDS((tp, A_W), f32),
                   SDS((2 * A_W // LANES, QB_A, KW_A), f32)],
        sem=("parallel", "arbitrary"), gather=gather)


def _bias_ext(table):
    flat = PAD_A + QB_A - 1 - REL_CLIP
    top = jnp.broadcast_to(table[:, 2 * REL_CLIP:], (table.shape[0], flat))
    lo = 2 * REL_CLIP - (EXT_A - 1 - flat)
    return jnp.concatenate([top, jnp.flip(table[:, lo:], axis=1)], axis=1)


def _bias_window(table):
    nh = table.shape[0]
    e = jnp.broadcast_to(_bias_ext(table)[:, None, :], (nh, QB_A, EXT_A)).reshape(nh, QB_A * EXT_A)
    m = e[:, :QB_A * (EXT_A - 1)].reshape(nh, QB_A, EXT_A - 1)
    return m[:, :, QB_A - 1:]


def _bias_window_grad(dbias):
    nh = dbias.shape[0]
    m = jnp.pad(dbias, ((0, 0), (0, 0), (QB_A - 1, 0))).reshape(nh, QB_A * (EXT_A - 1))
    dext = jnp.sum(jnp.pad(m, ((0, 0), (0, QB_A))).reshape(nh, QB_A, EXT_A), axis=1)
    flat = PAD_A + QB_A - 1 - REL_CLIP
    lo = 2 * REL_CLIP - (EXT_A - 1 - flat)
    tail = jnp.flip(dext[:, flat:], axis=1)
    tail = tail.at[:, -1].add(jnp.sum(dext[:, :flat], axis=1))
    return jnp.pad(tail, ((0, 0), (lo, 0)))


def _tri_suffix(x, tri):
    hi = x.astype(bf16)
    lo = (x - hi.astype(f32)).astype(bf16)
    return jnp.dot(hi, tri, preferred_element_type=f32) + jnp.dot(lo, tri, preferred_element_type=f32)


def _sb_block(q, k, run, tri, causal):
    z = lax.dot_general(q, k, (NT, ((), ())), preferred_element_type=f32) * LOG2_E
    e = jnp.exp2(-jnp.abs(z))
    lb = jnp.minimum(z, 0.0) - jnp.log2(1.0 + e)
    lmb = lb - z
    if causal is not None:
        lmb = jnp.where(causal, lmb, 0.0)
    cs = _tri_suffix(lmb, tri)
    w = jnp.exp2(lb + (run + cs - lmb))
    if causal is not None:
        w = jnp.where(causal, w, 0.0)
    return z, e, w, run + cs[:, 0:1]


def _sb_tri():
    r = lax.broadcasted_iota(jnp.int32, (SB_BLK, SB_BLK), 0)
    c = lax.broadcasted_iota(jnp.int32, (SB_BLK, SB_BLK), 1)
    return (r >= c).astype(bf16), c < r


def _sb_live(runs):
    m = runs[0]
    for r in runs[1:]:
        m = jnp.maximum(m, r)
    return jnp.max(m) > SB_DEAD


def _sb_fwd(proj, gather):
    t = proj.shape[0]
    cb = A_W // LANES
    nh = LANES // HEAD_DIM

    step_rows = QSUB_B * SB_BLK

    def body(q_ref, k_ref, v_ref, o_ref, of_ref):
        tri, diag = _sb_tri()
        for sb in range(QSUB_B):
            _sb_fwd_block(pl.program_id(1) * QSUB_B + sb, pl.ds(sb * SB_BLK, SB_BLK), tri, diag,
                          q_ref, k_ref, v_ref, o_ref, of_ref)

    def _sb_fwd_block(qb, qrows, tri, diag, q_ref, k_ref, v_ref, o_ref, of_ref):
        qm = _mask_heads(_scaled(q_ref[qrows, :]))

        def pair(kb, carry, causal):
            rows = pl.ds(pl.multiple_of(kb * SB_BLK, SB_BLK), SB_BLK)
            k = k_ref[rows, :]
            vm = _mask_heads(v_ref[rows, :])
            runs, acc = [], carry[nh]
            for h in range(nh):
                _, _, w, run = _sb_block(qm[h], k, carry[h], tri, causal)
                acc = acc + jnp.dot(w.astype(bf16), vm[h], preferred_element_type=f32)
                runs.append(run)
            return (*runs, acc)

        zero = jnp.zeros((SB_BLK, 1), f32)
        carry = pair(qb, (zero,) * nh + (jnp.zeros((SB_BLK, LANES), f32),), diag)

        def cond(st):
            return (st[0] < qb) & _sb_live(st[1][:nh])

        def step(st):
            return st[0] + 1, pair(qb - 1 - st[0], st[1], None)

        _, carry = lax.while_loop(cond, step, (jnp.int32(0), carry))
        o_ref[qrows, :] = carry[nh].astype(bf16)
        of_ref[qrows, :] = carry[nh]

    ospec = pl.BlockSpec((step_rows, LANES), lambda hp, qb: (qb, hp))
    return _call(
        body, [proj, proj, proj], grid=(cb, t // step_rows), name="sb_attn_fwd",
        in_specs=[pl.BlockSpec((step_rows, LANES), lambda hp, qb: (qb, 3 * cb + hp)),
                  pl.BlockSpec((t, LANES), lambda hp, qb: (0, 4 * cb + hp)),
                  pl.BlockSpec((t, LANES), lambda hp, qb: (0, 5 * cb + hp))],
        out_specs=[ospec, ospec], out_shape=[SDS((t, A_W), bf16), SDS((t, A_W), f32)],
        sem=("parallel", "arbitrary"), gather=gather)


def _sb_bwd(proj, out_b, dout, gather):
    t = proj.shape[0]
    cb = A_W // LANES
    nh = LANES // HEAD_DIM

    step_rows = QSUB_B * SB_BLK

    def body(q_ref, k_ref, v_ref, o_ref, do_ref, dq_ref, dk_ref, dv_ref):
        tri, diag = _sb_tri()

        @pl.when(pl.program_id(1) == 0)
        def _():
            dk_ref[...] = jnp.zeros_like(dk_ref)
            dv_ref[...] = jnp.zeros_like(dv_ref)

        for sb in range(QSUB_B):
            _sb_bwd_block(pl.program_id(1) * QSUB_B + sb, pl.ds(sb * SB_BLK, SB_BLK), tri, diag,
                          q_ref, k_ref, v_ref, o_ref, do_ref, dq_ref, dk_ref, dv_ref)

    def _sb_bwd_block(qb, qrows, tri, diag, q_ref, k_ref, v_ref, o_ref, do_ref, dq_ref, dk_ref, dv_ref):
        qm = _mask_heads(_scaled(q_ref[qrows, :]))
        do = do_ref[qrows, :]
        dom = _mask_heads(do)
        dsums = [jnp.sum(t_, axis=-1, keepdims=True) for t_ in _mask_heads(do.astype(f32) * o_ref[qrows, :])]

        def pair(kb, carry, causal):
            rows = pl.ds(pl.multiple_of(kb * SB_BLK, SB_BLK), SB_BLK)
            k = k_ref[rows, :]
            v = v_ref[rows, :]
            km = _mask_heads(k)
            new, dq, dk, dv = [], carry[2 * nh], None, None
            for h in range(nh):
                z, e, w, run = _sb_block(qm[h], k, carry[2 * h], tri, causal)
                inv = 1.0 / (1.0 + e)
                beta = jnp.where(z >= 0.0, inv, e * inv)
                wb = w.astype(bf16)
                g = lax.dot_general(dom[h], v, (NT, ((), ())), preferred_element_type=f32) * wb.astype(f32)
                sg = _tri_suffix(g, tri)
                dz = g - (g + (dsums[h] - carry[2 * h + 1] - sg)) * beta
                if causal is not None:
                    dz = jnp.where(causal, dz, 0.0)
                dzb = dz.astype(bf16)
                dq = dq + jnp.dot(dzb, km[h], preferred_element_type=f32)
                tk = lax.dot_general(dzb, qm[h], (TN, ((), ())), preferred_element_type=f32)
                tv = lax.dot_general(wb, dom[h], (TN, ((), ())), preferred_element_type=f32)
                dk, dv = (tk, tv) if dk is None else (dk + tk, dv + tv)
                new += [run, carry[2 * h + 1] + sg[:, 0:1]]
            dk_ref[rows, :] += dk
            dv_ref[rows, :] += dv
            return (*new, dq)

        zero = jnp.zeros((SB_BLK, 1), f32)
        carry = pair(qb, (zero,) * (2 * nh) + (jnp.zeros((SB_BLK, LANES), f32),), diag)

        def cond(st):
            return (st[0] < qb) & _sb_live(st[1][0:2 * nh:2])

        def step(st):
            return st[0] + 1, pair(qb - 1 - st[0], st[1], None)

        _, carry = lax.while_loop(cond, step, (jnp.int32(0), carry))
        dq_ref[qrows, :] = _scaled(carry[2 * nh]).astype(bf16)

    kv_in = lambda seg: pl.BlockSpec((t, LANES), lambda hp, qb: (0, seg * cb + hp))
    q_spec = pl.BlockSpec((step_rows, LANES), lambda hp, qb: (qb, hp))
    kv_out = pl.BlockSpec((t, LANES), lambda hp, qb: (0, hp))
    return _call(
        body, [proj, proj, proj, out_b, dout], grid=(cb, t // step_rows), name="sb_attn_bwd",
        in_specs=[pl.BlockSpec((step_rows, LANES), lambda hp, qb: (qb, 3 * cb + hp)), kv_in(4), kv_in(5),
                  q_spec, pl.BlockSpec((step_rows, LANES), lambda hp, qb: (qb, cb + hp))],
        out_specs=[q_spec, kv_out, kv_out],
        out_shape=[SDS((t, A_W), bf16), SDS((t, A_W), f32), SDS((t, A_W), f32)],
        sem=("parallel", "arbitrary"), gather=gather)


def _halo_specs(tr, w, col, nblk):
    per = tr // SUBLANES
    cur = pl.BlockSpec((tr, w), lambda i: (i, col))
    prev = pl.BlockSpec((SUBLANES, w), lambda i: (jnp.maximum(i * per - 1, 0), col))
    nxt = pl.BlockSpec((SUBLANES, w), lambda i: (jnp.minimum((i + 1) * per, nblk * per - 1), col))
    return cur, prev, nxt


def _taps_before(cur, prev8, first):
    prev8 = jnp.where(first, 0.0, prev8)
    ext = jnp.concatenate([prev8, cur], axis=0)
    return [pltpu.roll(ext, s, 0)[SUBLANES:] for s in (3, 2, 1)]


def _taps_after(cur, next8, last):
    n = cur.shape[0]
    next8 = jnp.where(last, 0.0, next8)
    ext = jnp.concatenate([cur, next8], axis=0)
    return [pltpu.roll(ext, n + SUBLANES - s, 0)[:n] for s in (1, 2, 3)]


def _block_diag(x, w_ref, dims):
    outs = [lax.dot_general(x[:, n * LRU_BW:(n + 1) * LRU_BW], w_ref[n], (dims, ((), ())),
                            preferred_element_type=f32) for n in range(LRU_BLOCKS)]
    return jnp.concatenate(outs, axis=1)


def _lru_gates(xc, wa_ref, wi_ref, ba, bi, lam):
    xb = xc.astype(bf16)
    r = jax.nn.sigmoid(_block_diag(xb, wa_ref, NN) + ba)
    ig = jax.nn.sigmoid(_block_diag(xb, wi_ref, NN) + bi)
    sp = jnp.maximum(-lam, 0.0) + jnp.log(1.0 + jnp.exp(-jnp.abs(lam)))
    log_a = -LRU_C * r * sp
    a = jnp.exp(log_a)
    x2 = 2.0 * log_a
    one_minus = jnp.where(x2 > -1e-2, -x2 * (1.0 + x2 * (0.5 + x2 * (1.0 / 6.0))), 1.0 - a * a)
    mult = jnp.sqrt(one_minus)
    return xb, r, ig, sp, a, mult


def _rg_gates_fwd(proj, conv_w, conv_b, wa, wi, ba, bi, lam, tr=512):
    t = proj.shape[0]
    w = D_MODEL
    tr = min(tr, t)
    nblk = t // tr
    cur, prev, _ = _halo_specs(tr, w, 1, nblk)

    def body(x_ref, xp_ref, cw_ref, cb_ref, wa_ref, wi_ref, ba_ref, bi_ref, lam_ref, xc_ref, a_ref, u_ref):
        x = x_ref[...]
        taps = _taps_before(x, xp_ref[...], pl.program_id(0) == 0) + [x]
        xc = cb_ref[...]
        for k in range(4):
            xc = xc + cw_ref[k:k + 1, :] * taps[k]
        _, _, ig, _, a, mult = _lru_gates(xc, wa_ref, wi_ref, ba_ref[...], bi_ref[...], lam_ref[...])
        xc_ref[...] = xc
        a_ref[...] = a
        u_ref[...] = mult * (ig * xc)

    full = lambda a_: pl.BlockSpec(a_.shape, lambda i, nd=a_.ndim: (0,) * nd)
    ospec = pl.BlockSpec((tr, w), lambda i: (i, 0))
    return pl.pallas_call(
        body, grid=(nblk,), name="rg_gates_fwd",
        in_specs=[cur, prev] + [full(a_) for a_ in (conv_w, conv_b, wa, wi, ba, bi, lam)],
        out_specs=[ospec] * 3, out_shape=[SDS((t, w), f32)] * 3,
        compiler_params=_cparams(("parallel",)))(proj, proj, conv_w, conv_b, wa, wi, ba, bi, lam)


def _lru_scan(name, a, b, reverse, tt=1024):
    t, w = a.shape
    tt = min(tt, t)
    nt = t // tt
    ng = tt // SUBLANES

    def body(a_ref, b_ref, h_ref, carry_ref):
        @pl.when(pl.program_id(0) == 0)
        def _():
            carry_ref[...] = jnp.zeros_like(carry_ref)

        row = lax.broadcasted_iota(jnp.int32, (SUBLANES, w), 0)

        def group(gi, carry):
            g = (ng - 1 - gi) if reverse else gi
            rows = pl.ds(pl.multiple_of(g * SUBLANES, SUBLANES), SUBLANES)
            av = a_ref[rows, :]
            bv = b_ref[rows, :]
            for s in (1, 2, 4):
                sh = (SUBLANES - s) if reverse else s
                ok = (row < SUBLANES - s) if reverse else (row >= s)
                a_s = pltpu.roll(av, sh, 0)
                b_s = pltpu.roll(bv, sh, 0)
                bv = jnp.where(ok, av * b_s + bv, bv)
                av = jnp.where(ok, av * a_s, av)
            h = av * carry + bv
            h_ref[rows, :] = h
            edge = h[0:1, :] if reverse else h[SUBLANES - 1:SUBLANES, :]
            return jnp.broadcast_to(edge, (SUBLANES, w))

        carry_ref[...] = lax.fori_loop(0, ng, group, carry_ref[...], unroll=4)

    tmap = (lambda i: (nt - 1 - i, 0)) if reverse else (lambda i: (i, 0))
    spec = pl.BlockSpec((tt, w), tmap)
    return pl.pallas_call(
        body, grid=(nt,), name=name, in_specs=[spec, spec], out_specs=spec,
        out_shape=SDS((t, w), f32), scratch_shapes=[pltpu.VMEM((SUBLANES, w), f32)],
        compiler_params=_cparams(("arbitrary",)))(a, b)


def _rg_gates_bwd(dhs, c, hs, xc, wa, wi, ba, bi, lam, tr=512):
    t, w = xc.shape
    tr = min(tr, t)
    nblk = t // tr
    cur, prev, nxt = _halo_specs(tr, w, 0, nblk)

    def body(dhs_ref, c_ref, cn_ref, hs_ref, hp_ref, xc_ref, wa_ref, wi_ref, ba_ref, bi_ref, lam_ref,
             dxc_ref, dwa_ref, dwi_ref, dba_ref, dbi_ref, dlam_ref):
        i = pl.program_id(0)
        c_next = _taps_after(c_ref[...], cn_ref[...], i == nblk - 1)[0]
        h_prev = _taps_before(hs_ref[...], hp_ref[...], i == 0)[2]
        xc = xc_ref[...]
        lam = lam_ref[...]
        xb, r, ig, sp, a, mult = _lru_gates(xc, wa_ref, wi_ref, ba_ref[...], bi_ref[...], lam)
        dh = dhs_ref[...] + c_next
        dlog_a = dh * h_prev * a - (dh * ig * xc) * (a * a / mult)
        dpre_a = (dlog_a * (-LRU_C * sp) * r * (1.0 - r)).astype(bf16)
        dpre_i = (dh * mult * xc * ig * (1.0 - ig)).astype(bf16)
        dxc_ref[...] = (dh * mult * ig + _block_diag(dpre_a, wa_ref, NT) + _block_diag(dpre_i, wi_ref, NT))
        dsig = 1.0 / (1.0 + jnp.exp(lam))
        sums = [jnp.sum(dpre_a.astype(f32), axis=0, keepdims=True),
                jnp.sum(dpre_i.astype(f32), axis=0, keepdims=True),
                jnp.sum(dlog_a * (-LRU_C * r), axis=0, keepdims=True) * (-dsig)]

        @pl.when(i == 0)
        def _():
            dwa_ref[...] = jnp.zeros_like(dwa_ref)
            dwi_ref[...] = jnp.zeros_like(dwi_ref)
            dba_ref[...] = jnp.zeros_like(dba_ref)
            dbi_ref[...] = jnp.zeros_like(dbi_ref)
            dlam_ref[...] = jnp.zeros_like(dlam_ref)

        for n in range(LRU_BLOCKS):
            sl = slice(n * LRU_BW, (n + 1) * LRU_BW)
            dwa_ref[n] += lax.dot_general(xb[:, sl], dpre_a[:, sl], (TN, ((), ())), preferred_element_type=f32)
            dwi_ref[n] += lax.dot_general(xb[:, sl], dpre_i[:, sl], (TN, ((), ())), preferred_element_type=f32)
        dba_ref[...] += sums[0]
        dbi_ref[...] += sums[1]
        dlam_ref[...] += sums[2]

    full = lambda a_: pl.BlockSpec(a_.shape, lambda i, nd=a_.ndim: (0,) * nd)
    vec = pl.BlockSpec((1, w), lambda i: (0, 0))
    mat = pl.BlockSpec((LRU_BLOCKS, LRU_BW, LRU_BW), lambda i: (0, 0, 0))
    return pl.pallas_call(
        body, grid=(nblk,), name="rg_gates_bwd",
        in_specs=[cur, cur, nxt, cur, prev, cur] + [full(a_) for a_ in (wa, wi, ba, bi, lam)],
        out_specs=[cur, mat, mat, vec, vec, vec],
        out_shape=[SDS((t, w), f32), SDS((LRU_BLOCKS, LRU_BW, LRU_BW), f32), SDS((LRU_BLOCKS, LRU_BW, LRU_BW), f32),
                   SDS((1, w), f32), SDS((1, w), f32), SDS((1, w), f32)],
        compiler_params=_cparams(("arbitrary",)))(dhs, c, c, hs, hs, xc, wa, wi, ba, bi, lam)


def _rg_conv_bwd(dxc, proj, conv_w, tr=512):
    t, w = dxc.shape
    tr = min(tr, t)
    nblk = t // tr
    cur, _, nxt = _halo_specs(tr, w, 0, nblk)
    xcur, xprev, _ = _halo_specs(tr, w, 1, nblk)

    def body(d_ref, dn_ref, x_ref, xp_ref, cw_ref, dx_ref, dcw_ref, dcb_ref):
        i = pl.program_id(0)
        d = d_ref[...]
        x = x_ref[...]
        after = _taps_after(d, dn_ref[...], i == nblk - 1)
        before = _taps_before(x, xp_ref[...], i == 0) + [x]
        dx = cw_ref[3:4, :] * d
        for s in (1, 2, 3):
            dx = dx + cw_ref[3 - s:4 - s, :] * after[s - 1]
        dx_ref[...] = dx.astype(bf16)
        dcw = jnp.concatenate([jnp.sum(d * before[k], axis=0, keepdims=True) for k in range(4)], axis=0)
        dcb = jnp.sum(d, axis=0, keepdims=True)

        @pl.when(i == 0)
        def _():
            dcw_ref[...] = dcw
            dcb_ref[...] = dcb

        @pl.when(i > 0)
        def _():
            dcw_ref[...] += dcw
            dcb_ref[...] += dcb

    return pl.pallas_call(
        body, grid=(nblk,), name="rg_conv_bwd",
        in_specs=[cur, nxt, xcur, xprev, pl.BlockSpec((4, w), lambda i: (0, 0))],
        out_specs=[cur, pl.BlockSpec((4, w), lambda i: (0, 0)), pl.BlockSpec((1, w), lambda i: (0, 0))],
        out_shape=[SDS((t, w), bf16), SDS((4, w), f32), SDS((1, w), f32)],
        compiler_params=_cparams(("arbitrary",)))(dxc, dxc, proj, proj, conv_w)


def _attn_fwd(h, wts, j, plan):
    proj = _mm_cols("attn_in", h, wts["attn_w_in"], j, bf16)
    kpad = jnp.pad(proj[:, A_W:2 * A_W], ((PAD_A, 0), (0, 0)))
    vpad = jnp.pad(proj[:, 2 * A_W:3 * A_W], ((PAD_A, 0), (0, 0)))
    bias = _bias_window(wts["attn_rel_bias"][j])
    plan = plan if j == 0 else None
    out_a = _carried(plan, "chunk_attn_fwd", wts, _chunk_attn_fwd, proj, kpad, vpad, bias)
    out_b, out_b32 = _carried(plan, "sb_attn_fwd", wts, _sb_fwd, proj)
    m = _mm_rows("attn_out", [out_a, out_b], wts["attn_w_out"], j, f32)
    return m, (proj, kpad, vpad, bias, out_a, out_b, out_b32)


def _attn_bwd(dm, h, saved, wts, j, grads, exch):
    proj, kpad, vpad, bias, out_a, out_b, out_b32 = saved
    dout = _mm_rows_t("attn_out_t", dm, wts["attn_w_out"], j, bf16)
    gi, ll = _grad_slot("attn_w_out", j)
    grads["attn_w_out"][gi] = _mm_wgrad("attn_out_wgrad_a", out_a, dm, grads["attn_w_out"][gi], ll, 0)
    grads["attn_w_out"][gi] = _mm_wgrad("attn_out_wgrad_b", out_b, dm, grads["attn_w_out"][gi], ll, 1)
    if exch is not None and j == 0:
        (dqa, dka, dva, dbias), got = _chunk_attn_bwd(proj, kpad, vpad, bias, dout, exch.upper_carry(grads))
        exch.upper_got(got)
        (dqs, dks, dvs), slots = _sb_bwd(proj, out_b32, dout, exch.carry())
        exch.carried(slots)
    else:
        dqa, dka, dva, dbias = _chunk_attn_bwd(proj, kpad, vpad, bias, dout, None)[0]
        dqs, dks, dvs = _sb_bwd(proj, out_b32, dout, None)[0]
    grads["attn_rel_bias"][j] = _bias_window_grad(dbias)
    dproj = jnp.concatenate([dqa, dka[PAD_A:].astype(bf16), dva[PAD_A:].astype(bf16),
                             dqs, dks.astype(bf16), dvs.astype(bf16)], axis=1)
    gi, ll = _grad_slot("attn_w_in", j)
    grads["attn_w_in"][gi] = _mm_wgrad_cols("attn_in_wgrad", h, dproj, grads["attn_w_in"][gi], ll)
    return _mm_cols_t("attn_in_t", dproj, wts["attn_w_in"], j, f32)


def _rg_fwd(h, wts, j, plan):
    proj =_mm_cols("rg_in", h, wts["rg_w_in"], j, f32)
    small = [wts[k][j] for k in ("rg_conv_w", "rg_conv_b", "rg_w_a", "rg_w_i", "rg_b_a", "rg_b_i", "rg_lambda")]
    xc, a, u = _rg_gates_fwd(proj, *small)
    hs = _lru_scan("lru_scan_fwd", a, u, False)
    yp = _rows("rg_gate_out", lambda hv, gv: hv * _gelu(gv), [hs, (proj, D_MODEL, 0)], [], [(D_MODEL, bf16)])[0]
    m = _mm_rows("rg_out", [yp], wts["rg_w_out"], j, f32)
    return m, (proj, xc, a, hs, yp)


def _rg_bwd(dm, h, saved, wts, j, grads, exch):
    proj, xc, a, hs, yp = saved
    dyp = _mm_rows_t("rg_out_t", dm, wts["rg_w_out"], j, f32)
    gi, ll = _grad_slot("rg_w_out", j)
    grads["rg_w_out"][gi] = _mm_wgrad("rg_out_wgrad", yp, dm, grads["rg_w_out"][gi], ll)

    def gate_bwd(dy, hv, gv, av):
        dhs = dy * _gelu(gv)
        return dhs, av * dhs, dy * hv * _gelu_grad(gv)

    dhs, ab, dgate = _rows("rg_gate_out_bwd", gate_bwd, [dyp, hs, (proj, D_MODEL, 0), a], [],
                           [(D_MODEL, f32), (D_MODEL, f32), (D_MODEL, bf16)])
    c = _lru_scan("lru_scan_bwd", a, ab, True)
    wa, wi, ba, bi, lam = [wts[k][j] for k in ("rg_w_a", "rg_w_i", "rg_b_a", "rg_b_i", "rg_lambda")]
    dxc, dwa, dwi, dba, dbi, dlam = _rg_gates_bwd(dhs, c, hs, xc, wa, wi, ba, bi, lam)
    dxr, dcw, dcb = _rg_conv_bwd(dxc, proj, wts["rg_conv_w"][j])
    for k, v in (("rg_w_a", dwa), ("rg_w_i", dwi), ("rg_b_a", dba), ("rg_b_i", dbi), ("rg_lambda", dlam),
                 ("rg_conv_w", dcw), ("rg_conv_b", dcb)):
        grads[k][j] = v
    dproj = jnp.concatenate([dgate, dxr], axis=1)
    grads["rg_w_in"][gi] = _mm_wgrad_cols("rg_in_wgrad", h, dproj, grads["rg_w_in"][gi], ll)
    return _mm_cols_t("rg_in_t", dproj, wts["rg_w_in"], j, f32)


def _local_step(x, target, wts, plan=None, exch=None):
    t = x.shape[0]
    d = D_MODEL
    gains = {k: wts[k] for k in ("norm_mix_pre", "norm_mix_post", "norm_ffn_pre", "norm_ffn_post")}
    gain = lambda k, l: gains[k][l:l + 1]

    saved = []
    h = _rows("norm_in", _norm_fwd, [x], [gain("norm_mix_pre", 0)], [(d, bf16)])[0]
    loss_cols = None
    for l in range(DEPTH):
        j = l // 2
        m, mix_saved = (_attn_fwd if l % 2 == 0 else _rg_fwd)(h, wts, j, plan)

        def resid_next(xv, mv, g_post, g_next):
            x1 = xv + _norm_fwd(mv, g_post)
            return x1, _norm_fwd(x1, g_next)

        x1, h2 = _rows("resid_mix", resid_next, [x, m], [gain("norm_mix_post", l), gain("norm_ffn_pre", l)],
                       [(d, f32), (d, bf16)])
        g, u, hid = _carried(plan if l == 0 else None, "ffn_up", wts, _ffn_up, h2, wts["ffn_w_gate"],
                             wts["ffn_w_up"], l)
        f = _ffn_down(hid, wts["ffn_w_down"], l)
        saved.append((x, h, m, mix_saved, x1, h2, g, u, hid, f))
        if l + 1 < DEPTH:
            x, h = _rows("resid_ffn", resid_next, [x1, f], [gain("norm_ffn_post", l), gain("norm_mix_pre", l + 1)],
                         [(d, f32), (d, bf16)])
        else:
            def resid_loss(xv, fv, tv, g_post):
                err = xv + _norm_fwd(fv, g_post) - tv
                return err * (1.0 / d), jnp.sum(err * err, axis=0, keepdims=True)

            dx, loss_cols = _rows("resid_loss", resid_loss, [x1, f, target], [gain("norm_ffn_post", l)],
                                  [(d, f32)], [((1, d), f32)])
    loss = 0.5 * jnp.sum(loss_cols) / d

    grads = {k: {} for k in SMALL_GRADS}
    for k in BIG_GRADS:
        shp = wts[k].shape
        rest = shp[2:] if shp[1] == 1 else shp[1:]
        grads[k] = [_Fresh((LOWER_LAYERS[k],) + rest), _Fresh((shp[0] - LOWER_LAYERS[k],) + rest)]

    def norm_bwd_cast(uv, dyv, gv):
        du, dg = _norm_bwd(uv, dyv, gv)
        return du, dg

    def norm_bwd_resid(uv, dhv, dxv, gv):
        du, dg = _norm_bwd(uv, dhv, gv)
        return dxv + du, dg

    def norm_bwd_pair(uv, dhv, dxv, nv, g_pre, g_post):
        dx_, dg_pre = norm_bwd_resid(uv, dhv, dxv, g_pre)
        dn, dg_post = _norm_bwd(nv, dx_, g_post)
        return dx_, dn, dg_pre, dg_post

    df = None
    for l in reversed(range(DEPTH)):
        j = l // 2
        x_in, h, m, mix_saved, x1, h2, g, u, hid, f = saved[l]
        if df is None:
            df, grads["norm_ffn_post"][l] = _rows("norm_ffn_post_bwd", norm_bwd_cast, [f, dx],
                                                  [gain("norm_ffn_post", l)], [(d, bf16)], [((1, d), f32)])
        dg, du = _ffn_down_bwd(df, wts["ffn_w_down"], l, g, u, None)[0]
        gi, ll = _grad_slot("ffn_w_down", l)
        grads["ffn_w_down"][gi] = _ffn_wgrad_down(hid, df, grads["ffn_w_down"][gi], ll)
        dh2 = _ffn_up_bwd(dg, du, wts["ffn_w_gate"], wts["ffn_w_up"], l)
        grads["ffn_w_gate"][gi], grads["ffn_w_up"][gi] = _ffn_wgrad_up(
            h2, dg, du, grads["ffn_w_gate"][gi], grads["ffn_w_up"][gi], ll)
        dx1, dm, grads["norm_ffn_pre"][l], grads["norm_mix_post"][l] = _rows(
            "norm_ffn_mix_bwd", norm_bwd_pair, [x1, dh2, dx, m], [gain("norm_ffn_pre", l), gain("norm_mix_post", l)],
            [(d, f32), (d, bf16)], [((1, d), f32), ((1, d), f32)])
        dh = (_attn_bwd if l % 2 == 0 else _rg_bwd)(dm, h, mix_saved, wts, j, grads, exch)
        if l > 0:
            dx, df, grads["norm_mix_pre"][l], grads["norm_ffn_post"][l - 1] = _rows(
                "norm_mix_ffn_bwd", norm_bwd_pair, [x_in, dh, dx1, saved[l - 1][9]],
                [gain("norm_mix_pre", l), gain("norm_ffn_post", l - 1)],
                [(d, f32), (d, bf16)], [((1, d), f32), ((1, d), f32)])
        else:
            dx, grads["norm_mix_pre"][l] = _rows("norm_mix_pre_bwd", norm_bwd_resid, [x_in, dh, dx1],
                                                 [gain("norm_mix_pre", l)], [(d, f32)], [((1, d), f32)])
    return loss, dx, grads


ANY = pl.BlockSpec(memory_space=pl.ANY)
PACK_COLS = 1024
SMALL_ROWS = 288


def _mesh_pos():
    x, y, c = lax.axis_index("x"), lax.axis_index("y"), lax.axis_index("c")
    return x, y, c, [(1 - x, y), (x, 1 - y), (1 - x, 1 - y)]


def _run_copies(copies):
    for cp in copies:
        cp.start()
    for cp in copies:
        cp.wait()


GATHER_SEMS = 7


def _gather_copies(items, ins, outs, send, recv):
    x, y, c, chips = _mesh_pos()
    q = 2 * x + y
    sibling = (x, y, 1 - c)

    def copy(k, src, dst, to):
        return pltpu.make_async_remote_copy(src_ref=src, dst_ref=dst, send_sem=send.at[k], recv_sem=recv.at[k],
                                            device_id=to, device_id_type=MESH)

    own, sent, passed = [], [], []
    for i, (t, l0, nl) in enumerate(items):
        lay = pl.ds(l0, nl)
        half = ins[t].shape[1] // 2
        rows = pl.ds(pl.multiple_of(c * half, half), half)
        own.append(copy(GATHER_SEMS * i, ins[t].at[lay], outs[t].at[lay, q], sibling))
        for j, (px, py) in enumerate(chips):
            sent.append(copy(GATHER_SEMS * i + 1 + j, ins[t].at[lay, rows], outs[t].at[lay, q, rows], (px, py, c)))
            landed = outs[t].at[lay, 2 * px + py, rows]
            passed.append(copy(GATHER_SEMS * i + 4 + j, landed, landed, sibling))
    return own, sent, passed


def _gather_start(items, ins, outs, send, recv):
    own, sent, _ = _gather_copies(items, ins, outs, send, recv)
    for cp in own + sent:
        cp.start()


def _gather_finish(items, ins, outs, send, recv):
    own, sent, passed = _gather_copies(items, ins, outs, send, recv)
    for arrived, forward in zip(sent, passed):
        arrived.wait_recv()
        forward.start()
    for cp in sent:
        cp.wait_send()
    for cp in own + passed:
        cp.wait()


def _gather_call(items, shards):
    n = len(shards)
    nsem = GATHER_SEMS * len(items)

    def body(*refs):
        ins, outs = refs[:n], refs[n:2 * n]
        _gather_start(items, ins, outs, *refs[2 * n:])
        _gather_finish(items, ins, outs, *refs[2 * n:])

    return pl.pallas_call(
        body, name="weight_all_gather", in_specs=[ANY] * n, out_specs=[ANY] * n,
        out_shape=[SDS((s.shape[0], N_CHIPS) + s.shape[1:], s.dtype) for s in shards],
        scratch_shapes=[pltpu.SemaphoreType.DMA((nsem,)), pltpu.SemaphoreType.DMA((nsem,))])(*shards)


def _call(body, operands, *, name, grid, in_specs, out_specs, out_shape, sem, scratch=(), gather=None):
    if gather is None:
        return pl.pallas_call(body, grid=grid, in_specs=in_specs, out_specs=out_specs, out_shape=out_shape,
                              scratch_shapes=list(scratch), name=name, compiler_params=_cparams(sem))(*operands), None
    start, finish, c_ins, c_io, c_new, nsem = gather
    n_in, n_out, n_scr = len(operands), len(out_shape), len(scratch)
    ni, nio, nco = len(c_ins), len(c_io), len(c_io) + len(c_new)

    def full(*refs):
        ins, sh = refs[:n_in], refs[n_in:n_in + ni]
        outs = refs[n_in + ni + nio:n_in + ni + nio + n_out]
        co = refs[n_in + ni + nio + n_out:n_in + ni + nio + n_out + nco]
        scr = refs[n_in + ni + nio + n_out + nco:]
        ids = [pl.program_id(a) for a in range(len(grid))]
        first = functools.reduce(jnp.logical_and, [i == 0 for i in ids])
        last = functools.reduce(jnp.logical_and, [i == g - 1 for i, g in zip(ids, grid)])

        @pl.when(first)
        def _():
            start(sh, co, scr[n_scr], scr[n_scr + 1])

        body(*ins, *outs, *scr[:n_scr])

        @pl.when(last)
        def _():
            finish(sh, co, scr[n_scr], scr[n_scr + 1])

    res = pl.pallas_call(
        full, grid=grid, in_specs=list(in_specs) + [ANY] * (ni + nio), out_specs=list(out_specs) + [ANY] * nco,
        out_shape=list(out_shape) + [SDS(g.shape, g.dtype) for g in list(c_io) + list(c_new)],
        scratch_shapes=list(scratch) + [pltpu.SemaphoreType.DMA((nsem,)), pltpu.SemaphoreType.DMA((nsem,))],
        input_output_aliases={n_in + ni + t: n_out + t for t in range(nio)}, name=name,
        compiler_params=_cparams(("arbitrary",) * len(grid)))(*operands, *c_ins, *c_io)
    return res[:n_out], res[n_out:]


def _pair_exchange(gs):
    n = len(gs)

    def body(*refs):
        _pair_copies(refs[:n], refs[n:2 * n], *refs[2 * n:], start=True)
        _pair_copies(refs[:n], refs[n:2 * n], *refs[2 * n:], start=False)

    return pl.pallas_call(
        body, name="grad_pair_exchange", in_specs=[ANY] * n, out_specs=[ANY] * n,
        out_shape=_pair_shapes(gs),
        scratch_shapes=[pltpu.SemaphoreType.DMA((n,)), pltpu.SemaphoreType.DMA((n,))])(*gs)


def _pair_shapes(gs):
    return [SDS(g.shape[:2] + (g.shape[2] // 2, g.shape[3]), f32) for g in gs]


def _pair_copies(ins, outs, send, recv, start):
    x, y, c, _ = _mesh_pos()
    for t in range(len(ins)):
        half = ins[t].shape[2] // 2
        src = ins[t].at[:, :, pl.ds(pl.multiple_of((1 - c) * half, SUBLANES), half)]
        cp = pltpu.make_async_remote_copy(src_ref=src, dst_ref=outs[t], send_sem=send.at[t], recv_sem=recv.at[t],
                                          device_id=(x, y, 1 - c), device_id_type=MESH)
        cp.start() if start else cp.wait()


def _pair_carry(gs):
    return (functools.partial(_pair_copies, start=True), functools.partial(_pair_copies, start=False),
            gs, [], _pair_shapes(gs), len(gs))


def _pair_sum(name, g, got, c):
    l, s, r, cols = g.shape

    def body(c_ref, a_ref, b_ref, o_ref):
        o_ref[...] = (a_ref[...] + b_ref[...]).astype(bf16)

    blk = (None, None, r // 2, cols)
    return pl.pallas_call(
        body, name=name, out_shape=SDS(got.shape, bf16),
        grid_spec=pltpu.PrefetchScalarGridSpec(
            num_scalar_prefetch=1, grid=(l, s),
            in_specs=[pl.BlockSpec(blk, lambda i, q, c_ref: (i, q, c_ref[0], 0)),
                      pl.BlockSpec(blk, lambda i, q, c_ref: (i, q, 0, 0))],
            out_specs=pl.BlockSpec(blk, lambda i, q, c_ref: (i, q, 0, 0))),
        compiler_params=_cparams(("parallel", "parallel")))(c, g, got)


def _chip_exchange(hs):
    n = len(hs)

    def body(*refs):
        _chip_copies(refs[:n], refs[n:2 * n], *refs[2 * n:], start=True)
        _chip_copies(refs[:n], refs[n:2 * n], *refs[2 * n:], start=False)

    return pl.pallas_call(
        body, name="grad_chip_exchange", in_specs=[ANY] * n, out_specs=[ANY] * n,
        out_shape=[SDS(h.shape, h.dtype) for h in hs],
        scratch_shapes=[pltpu.SemaphoreType.DMA((3 * n,)), pltpu.SemaphoreType.DMA((3 * n,))])(*hs)


def _chip_copies(ins, outs, send, recv, start):
    x, y, c, chips = _mesh_pos()
    q = 2 * x + y
    for t in range(len(ins)):
        for j, (px, py) in enumerate(chips):
            cp = pltpu.make_async_remote_copy(
                src_ref=ins[t].at[:, 2 * px + py], dst_ref=outs[t].at[:, q], send_sem=send.at[3 * t + j],
                recv_sem=recv.at[3 * t + j], device_id=(px, py, c), device_id_type=MESH)
            cp.start() if start else cp.wait()


def _chip_carry(hs):
    return (functools.partial(_chip_copies, start=True), functools.partial(_chip_copies, start=False),
            hs, [], [SDS(h.shape, h.dtype) for h in hs], 3 * len(hs))


def _chip_sum(name, s, h, pos, l0, layers, into):
    l, _, r, cols = s.shape

    def body(pos_ref, s0, s1, s2, s3, own_ref, *rest):
        vals = [jnp.where(pos_ref[0] == p, own_ref[...], ref[...]).astype(f32) for p, ref in enumerate((s0, s1, s2, s3))]
        rest[-1][...] = ((vals[0] + vals[1]) + vals[2]) + vals[3]

    blk = (None, None, r, cols)
    slot = lambda p: pl.BlockSpec(blk, lambda i, pos_ref: (i, jnp.where(pos_ref[0] == p, (p + 1) % N_CHIPS, p), 0, 0))
    extra, alias = ([], {}) if into is None else ([into], {6: 0})
    return pl.pallas_call(
        body, name=name, out_shape=SDS((layers, 2 * r, cols), f32), input_output_aliases=alias,
        grid_spec=pltpu.PrefetchScalarGridSpec(
            num_scalar_prefetch=1, grid=(l,),
            in_specs=[slot(p) for p in range(N_CHIPS)] + [pl.BlockSpec(blk, lambda i, pos_ref: (i, pos_ref[0], 0, 0))]
            + [ANY] * len(extra),
            out_specs=pl.BlockSpec((None, r, cols), lambda i, pos_ref: (l0 + i, pos_ref[1], 0))),
        compiler_params=_cparams(("parallel",)))(pos, s, s, s, s, h, *extra)


def _pair_gather(fulls):
    n = len(fulls)

    def body(*refs):
        ins, outs = refs[:n], refs[n:2 * n]
        send, recv = refs[2 * n:]
        x, y, c, _ = _mesh_pos()
        copies = []
        for t in range(n):
            half = outs[t].shape[1] // 2
            rows = outs[t].at[:, pl.ds(pl.multiple_of(c * half, SUBLANES), half)]
            copies.append(pltpu.make_async_remote_copy(
                src_ref=rows, dst_ref=rows, send_sem=send.at[t], recv_sem=recv.at[t],
                device_id=(x, y, 1 - c), device_id_type=MESH))
        _run_copies(copies)

    return pl.pallas_call(
        body, name="grad_pair_gather", in_specs=[ANY] * n, out_specs=[ANY] * n,
        out_shape=[SDS(f.shape, f32) for f in fulls], input_output_aliases={t: t for t in range(n)},
        scratch_shapes=[pltpu.SemaphoreType.DMA((n,)), pltpu.SemaphoreType.DMA((n,))])(*fulls)


COL_SHARDED = ("attn_w_in", "rg_w_in", "ffn_w_gate", "ffn_w_up")
ROW_SHARDED = ("attn_w_out", "rg_w_out")
GATES = ("rg_w_a", "rg_w_i")
VECTORS = ("rg_conv_w", "rg_conv_b", "rg_b_a", "rg_b_i", "rg_lambda")
REPLICATED = ("norm_mix_pre", "norm_mix_post", "norm_ffn_pre", "norm_ffn_post", "attn_rel_bias")
BIG_GRADS = COL_SHARDED + ROW_SHARDED + ("ffn_w_down",)
SMALL_GRADS = GATES + VECTORS + REPLICATED
WEIGHTS =("attn_w_in", "attn_rel_bias", "attn_w_out", "rg_w_in", "rg_conv_w", "rg_conv_b", "rg_w_a", "rg_b_a",
           "rg_w_i", "rg_b_i", "rg_lambda", "rg_w_out", "norm_mix_pre", "norm_mix_post", "norm_ffn_pre",
           "norm_ffn_post", "ffn_w_gate", "ffn_w_up", "ffn_w_down")
SMALL = VECTORS + REPLICATED


GATHER_PARTS = {
    "first": (("attn_w_in", 0, 1), ("attn_w_out", 0, 1), ("rg_w_a", 0, 8), ("rg_w_i", 0, 8), ("vec", 0, 1)),
    "chunk_attn_fwd": (("ffn_w_gate", 0, 1), ("ffn_w_up", 0, 1), ("ffn_w_down", 0, 1), ("rg_w_in", 0, 1),
                       ("rg_w_out", 0, 1)),
    "sb_attn_fwd": (("ffn_w_gate", 1, 3), ("ffn_w_up", 1, 3), ("ffn_w_down", 1, 3)),
    "ffn_up": (("rg_w_in", 1, 1), ("rg_w_out", 1, 1), ("attn_w_in", 1, 1), ("attn_w_out", 1, 1)),
}


TRANSPOSED = ("ffn_w_gate", "ffn_w_up")


def _natural(name, a):
    return jnp.swapaxes(a, 1, 2) if name in TRANSPOSED else a


class _WeightGather:
    def __init__(self, w):
        self.w = w
        self.names = list(COL_SHARDED + ROW_SHARDED + GATES + ("ffn_w_down", "vec"))
        self.shards = {}
        for k in self.names[:-1]:
            a = _natural(k, w[k]).astype(bf16)
            self.shards[k] = a.reshape((-1,) + a.shape[-2:])
        self.shards["vec"] = jnp.concatenate([w[k].reshape(-1) for k in VECTORS]).reshape(1, -1, LANES)
        got = _gather_call(self._items("first", self.names), [self.shards[k] for k in self.names])
        self.raw = dict(zip(self.names, got))

    @staticmethod
    def _items(part, names):
        return [(names.index(k), l0, nl) for k, l0, nl in GATHER_PARTS[part]]

    def part(self, part):
        names = list(dict.fromkeys(k for k, _, _ in GATHER_PARTS[part]))
        items = self._items(part, names)
        return (functools.partial(_gather_start, items), functools.partial(_gather_finish, items),
                [self.shards[k] for k in names], [self.raw[k] for k in names], [], GATHER_SEMS * len(items)), names

    def views(self):
        got, w = self.raw, self.w
        out = {k: w[k] for k in REPLICATED}
        for k in COL_SHARDED + ("ffn_w_down",):
            out[k] = got[k]
        for k in ROW_SHARDED:
            l, s, ks, n = got[k].shape
            out[k] = got[k].reshape(l, 1, s * ks, n)
        for k in GATES:
            out[k] = got[k].reshape(2, LRU_BLOCKS, LRU_BW, LRU_BW)
        vec = got["vec"].reshape(N_CHIPS, -1)
        off = 0
        for k in VECTORS:
            shp = w[k].shape
            n = int(np.prod(shp))
            piece = vec[:, off:off + n].reshape((N_CHIPS,) + shp)
            off += n
            if k == "rg_conv_w":
                out[k] = piece.reshape(N_CHIPS, 2, 4, 256).transpose(1, 2, 0, 3).reshape(2, 4, D_MODEL)
            elif k in ("rg_b_a", "rg_b_i"):
                out[k] = piece.transpose(1, 2, 0, 3).reshape(2, 1, D_MODEL)
            else:
                out[k] = piece.transpose(1, 0, 2).reshape(2, 1, D_MODEL)
        return out


def _carried(plan, part, wts, fn, *args):
    if plan is None:
        return fn(*args, None)[0]
    gather, names = plan.part(part)
    out, new = fn(*args, gather)
    plan.raw.update(zip(names, new))
    wts.update(plan.views())
    return out


def _grad_blocks(name, g):
    st = jnp.stack([g[i] for i in sorted(g)])
    if name in GATES:
        st = st.reshape(2, LRU_BLOCKS, N_CHIPS, LRU_BW // N_CHIPS, LRU_BW).transpose(2, 0, 1, 3, 4)
    elif name == "rg_conv_w":
        st = st.reshape(2, 4, N_CHIPS, -1).transpose(2, 0, 1, 3)
    elif name in ("rg_b_a", "rg_b_i"):
        st = st.reshape(2, LRU_BLOCKS, N_CHIPS, -1).transpose(2, 0, 1, 3)
    elif name in VECTORS:
        st = st.reshape(2, N_CHIPS, -1).transpose(1, 0, 2)
    else:
        st = jnp.broadcast_to(st.reshape(1, -1), (N_CHIPS, st.size))
    return st.reshape(N_CHIPS, -1)


class _GradExchange:
    def __init__(self):
        self.c = lax.axis_index("c").astype(jnp.int32).reshape(1)
        self.pos = jnp.stack([2 * lax.axis_index("x") + lax.axis_index("y"), lax.axis_index("c")]).astype(jnp.int32)
        self.up = self.got_up = self.parts_up = self.slots_up = None

    @staticmethod
    def _blocked(g):
        if g.ndim == 3:
            g = g.reshape(g.shape[0], N_CHIPS, g.shape[1] // N_CHIPS, g.shape[2])
        return g

    def _sums(self, tag, names, gs, got):
        return [_pair_sum("grad_pair_sum_" + tag + k, g, r, self.c) for k, g, r in zip(names, gs, got)]

    def upper_carry(self, grads):
        self.up = [self._blocked(grads[k][1]) for k in BIG_GRADS]
        return _pair_carry(self.up)

    def upper_got(self, got):
        self.got_up = got

    def carry(self):
        self.parts_up = self._sums("up_", BIG_GRADS, self.up, self.got_up)
        return _chip_carry(self.parts_up)

    def carried(self, slots):
        self.slots_up = slots

    def finish(self, grads, shard_shapes):
        if self.got_up is None:
            self.upper_carry(grads)
            self.got_up = _pair_exchange(self.up)
        if self.slots_up is None:
            self.carry()
            self.slots_up = _chip_exchange(self.parts_up)
        blocks = [_grad_blocks(k, grads[k]) for k in SMALL_GRADS]
        used = sum(b.shape[1] for b in blocks)
        small = jnp.concatenate(blocks + [jnp.zeros((N_CHIPS, SMALL_ROWS * PACK_COLS - used), f32)], axis=1)
        names = tuple(k for k in BIG_GRADS if LOWER_LAYERS[k]) + ("small",)
        gs = [self._blocked(grads[k][0]) for k in names[:-1]] + [small.reshape(1, N_CHIPS, SMALL_ROWS, PACK_COLS)]
        parts = dict(zip(names, self._sums("lo_", names, gs, _pair_exchange(gs))))
        slots = dict(zip(names, _chip_exchange([parts[k] for k in names])))
        fulls = []
        for i, k in enumerate(BIG_GRADS):
            nlo, nup = LOWER_LAYERS[k], self.parts_up[i].shape[0]
            full = _chip_sum("grad_chip_sum_up_" + k, self.slots_up[i], self.parts_up[i], self.pos, nlo, nlo + nup, None)
            if nlo:
                full = _chip_sum("grad_chip_sum_lo_" + k, slots[k], parts[k], self.pos, 0, nlo + nup, full)
            fulls.append(full)
        fulls.append(_chip_sum("grad_chip_sum_lo_small", slots["small"], parts["small"], self.pos, 0, 1, None))
        full = _pair_gather(fulls)
        out = {k: f.reshape(shard_shapes[k]) for k, f in zip(BIG_GRADS, full)}
        flat, off = full[-1].reshape(-1), 0
        for k in SMALL_GRADS:
            n = int(np.prod(shard_shapes[k]))
            out[k] = flat[off:off + n].reshape(shard_shapes[k])
            off += n
        return out


def _adamw_fn(w, g, m, v):
    m = ADAM_B1 * m + (1.0 - ADAM_B1) * g
    v = ADAM_B2 * v + (1.0 - ADAM_B2) * (g * g)
    m_hat = m / (1.0 - ADAM_B1 ** ADAM_STEP)
    v_hat = v / (1.0 - ADAM_B2 ** ADAM_STEP)
    return -ADAM_LR * (m_hat / (jnp.sqrt(v_hat) + ADAM_EPS) + ADAM_WD * w), m, v


def _adamw(name, w, g, m, v):
    shp = w.shape
    if w.size >= 1 << 16:
        width = shp[-1]
        ops = [a.reshape(-1, width) for a in (w, g, m, v)]
        res = _rows(name, _adamw_fn, ops, [], [(width, f32)] * 3)
        return [r.reshape(shp) for r in res]
    n = w.size
    rows = -(-n // (SUBLANES * LANES)) * SUBLANES
    ops = [jnp.pad(a.reshape(-1), (0, rows * LANES - n)).reshape(rows, LANES) for a in (w, g, m, v)]
    res = _rows(name, _adamw_fn, ops, [], [(LANES, f32)] * 3, tr=rows)
    return [r.reshape(-1)[:n].reshape(shp) for r in res]


def kernel(x, attn_w_in, attn_rel_bias, attn_w_out, rg_w_in, rg_conv_w, rg_conv_b, rg_w_a, rg_b_a, rg_w_i, rg_b_i, rg_lambda, rg_w_out, norm_mix_pre, norm_mix_post, norm_ffn_pre, norm_ffn_post, ffn_w_gate, ffn_w_up, ffn_w_down, loss_target, m_attn_w_in, m_attn_rel_bias, m_attn_w_out, m_rg_w_in, m_rg_conv_w, m_rg_conv_b, m_rg_w_a, m_rg_b_a, m_rg_w_i, m_rg_b_i, m_rg_lambda, m_rg_w_out, m_norm_mix_pre, m_norm_mix_post, m_norm_ffn_pre, m_norm_ffn_post, m_ffn_w_gate, m_ffn_w_up, m_ffn_w_down, v_attn_w_in, v_attn_rel_bias, v_attn_w_out, v_rg_w_in, v_rg_conv_w, v_rg_conv_b, v_rg_w_a, v_rg_b_a, v_rg_w_i, v_rg_b_i, v_rg_lambda, v_rg_w_out, v_norm_mix_pre, v_norm_mix_post, v_norm_ffn_pre, v_norm_ffn_post, v_ffn_w_gate, v_ffn_w_up, v_ffn_w_down):
    w = dict(zip(WEIGHTS, (attn_w_in, attn_rel_bias, attn_w_out, rg_w_in, rg_conv_w, rg_conv_b, rg_w_a, rg_b_a, rg_w_i,
                           rg_b_i, rg_lambda, rg_w_out, norm_mix_pre, norm_mix_post, norm_ffn_pre, norm_ffn_post,
                           ffn_w_gate, ffn_w_up, ffn_w_down)))
    m = dict(zip(WEIGHTS, (m_attn_w_in, m_attn_rel_bias, m_attn_w_out, m_rg_w_in, m_rg_conv_w, m_rg_conv_b, m_rg_w_a,
                           m_rg_b_a, m_rg_w_i, m_rg_b_i, m_rg_lambda, m_rg_w_out, m_norm_mix_pre, m_norm_mix_post,
                           m_norm_ffn_pre, m_norm_ffn_post, m_ffn_w_gate, m_ffn_w_up, m_ffn_w_down)))
    v = dict(zip(WEIGHTS, (v_attn_w_in, v_attn_rel_bias, v_attn_w_out, v_rg_w_in, v_rg_conv_w, v_rg_conv_b, v_rg_w_a,
                           v_rg_b_a, v_rg_w_i, v_rg_b_i, v_rg_lambda, v_rg_w_out, v_norm_mix_pre, v_norm_mix_post,
                           v_norm_ffn_pre, v_norm_ffn_post, v_ffn_w_gate, v_ffn_w_up, v_ffn_w_down)))
    plan = _WeightGather(w)
    exch = _GradExchange()
    loss, dx, grads = _local_step(x[0], loss_target[0], plan.views(), plan, exch)
    loss = lax.psum(loss, ("x", "y", "c"))
    g = exch.finish(grads, {k: _natural(k, w[k]).shape for k in WEIGHTS})

    big = [k for k in WEIGHTS if k not in SMALL]
    upd = {}
    for k in big:
        res = _adamw("adamw_" + k, _natural(k, w[k]), g[k], _natural(k, m[k]), _natural(k, v[k]))
        upd[k] = [_natural(k, r) for r in res]
        g[k] = _natural(k, g[k])
    cat = lambda d: jnp.concatenate([d[k].reshape(-1) for k in SMALL])
    small = _adamw("adamw_small", cat(w), cat(g), cat(m), cat(v))
    off = 0
    for k in SMALL:
        n = w[k].size
        upd[k] = [r[off:off + n].reshape(w[k].shape) for r in small]
        off += n
    return (loss, dx[None], *[g[k] for k in WEIGHTS], *[upd[k][0] for k in WEIGHTS],
            *[upd[k][1] for k in WEIGHTS], *[upd[k][2] for k in WEIGHTS])
```

```python
import functools

import numpy as np
import jax
import jax.numpy as jnp
from jax import lax
from jax.experimental import pallas as pl
from jax.experimental.pallas import tpu as pltpu

f32 = jnp.float32
bf16 = jnp.bfloat16
SDS = jax.ShapeDtypeStruct
MESH = pl.DeviceIdType.MESH

D_MODEL = 1024
N_CHIPS = 4
DEPTH = 4
HEAD_DIM = 64
CHUNK = 64
N_LEFT = 8
REL_CLIP = 256
A_W = 512
LRU_BLOCKS = 4
LRU_BW = 256
LRU_C = 8.0
D_FF = 2816
RMS_EPS = 1e-6
LANES = 128
SUBLANES = 8
VMEM_LIMIT = 56 * 1024 * 1024

QB_A = 2 * CHUNK
QSUB_A = 8
KW_A = QB_A + N_LEFT * CHUNK
PAD_A = N_LEFT * CHUNK
EXT_A = 768
SB_BLK = 256
QSUB_B = 4
SB_DEAD = -110.0

ADAM_LR, ADAM_B1, ADAM_B2, ADAM_EPS, ADAM_WD, ADAM_STEP = 0.001, 0.9, 0.999, 1e-08, 0.01, 10


def _cparams(sem):
    return pltpu.CompilerParams(dimension_semantics=sem, vmem_limit_bytes=VMEM_LIMIT)


def _gemm(name, operands, in_specs, o_spec, out_shape, grid, dims, acc_shape, into=None):
    nred = grid[2]
    npair = len(operands) // 2
    nin = 2 * npair + (into is not None)

    def body(*refs):
        o_ref = refs[nin]
        p = None
        for t in range(npair):
            d = lax.dot_general(refs[2 * t][...], refs[2 * t + 1][...], (dims, ((), ())),
                                preferred_element_type=f32)
            p = d if p is None else p + d
        if nred == 1:
            o_ref[...] = p.astype(o_ref.dtype)
        else:
            acc = refs[nin + 1]
            r = pl.program_id(2)

            @pl.when(r == 0)
            def _():
                acc[...] = p

            @pl.when(r > 0)
            def _():
                acc[...] += p

            @pl.when(r == nred - 1)
            def _():
                o_ref[...] = acc[...].astype(o_ref.dtype)

    scratch = [] if nred == 1 else [pltpu.VMEM(acc_shape, f32)]
    extra, alias = ([], {}) if into is None else ([into], {2 * npair: 0})
    return pl.pallas_call(
        body, grid=grid, in_specs=list(in_specs) + [pl.BlockSpec(memory_space=pl.ANY)] * len(extra),
        out_specs=o_spec, out_shape=out_shape, scratch_shapes=scratch, name=name, input_output_aliases=alias,
        compiler_params=_cparams(("parallel", "parallel", "arbitrary")))(*operands, *extra)


LOWER_LAYERS = {"attn_w_in": 1, "attn_w_out": 0, "rg_w_in": 0, "rg_w_out": 0,
                "ffn_w_gate": 0, "ffn_w_up": 0, "ffn_w_down": 0}


def _grad_slot(name, l):
    n = LOWER_LAYERS[name]
    return (0, l) if l < n else (1, l - n)


class _Fresh:
    def __init__(self, shape):
        self.shape = tuple(shape)


def _into(buf):
    return None if isinstance(buf, _Fresh) else buf


NN = ((1,), (0,))
NT = ((1,), (1,))
TN = ((0,), (0,))


WGRAD_TOKENS = 2048


def _tile(t, want=1024):
    return min(want, t)


def _mm_cols(name, a, w, l, out_dtype):
    t, k = a.shape
    _, s, _, ns = w.shape
    tm = _tile(t, 2048)
    return _gemm(
        name, [a, w],
        [pl.BlockSpec((tm, k), lambda i, j, r: (i, 0)),
         pl.BlockSpec((None, None, k, ns), lambda i, j, r: (l, j, 0, 0))],
        pl.BlockSpec((tm, ns), lambda i, j, r: (i, j)),
        SDS((t, s * ns), out_dtype), (t // tm, s, 1), NN, None)


def _mm_cols_t(name, dy, w, l, out_dtype):
    t = dy.shape[0]
    _, s, k, ns = w.shape
    tm = _tile(t, 2048)
    return _gemm(
        name, [dy, w],
        [pl.BlockSpec((tm, ns), lambda i, j, r: (i, r)),
         pl.BlockSpec((None, None, k, ns), lambda i, j, r: (l, r, 0, 0))],
        pl.BlockSpec((tm, k), lambda i, j, r: (i, 0)),
        SDS((t, k), out_dtype), (t // tm, 1, s), NT, (tm, k))


def _mm_wgrad_cols(name, a, dy, buf, l):
    t, k = a.shape
    _, s, _, ns = buf.shape
    tt = _tile(t, 2 * WGRAD_TOKENS)
    return _gemm(
        name, [a, dy],
        [pl.BlockSpec((tt, k), lambda i, j, r: (r, 0)),
         pl.BlockSpec((tt, ns), lambda i, j, r: (r, i))],
        pl.BlockSpec((None, None, k, ns), lambda i, j, r: (l, i, 0, 0)),
        SDS(buf.shape, f32), (s, 1, t // tt), TN, (k, ns), into=_into(buf))


def _mm_rows(name, parts, w, l, out_dtype):
    t = parts[0].shape[0]
    n = w.shape[3]
    tm = _tile(t, 2048)
    ops, specs = [], []
    for p_i, a in enumerate(parts):
        kp = a.shape[1]
        ops += [a, w]
        specs += [pl.BlockSpec((tm, kp), lambda i, j, r: (i, 0)),
                  pl.BlockSpec((None, None, kp, n), lambda i, j, r, p_i=p_i: (l, 0, p_i, 0))]
    return _gemm(name, ops, specs, pl.BlockSpec((tm, n), lambda i, j, r: (i, 0)),
                 SDS((t, n), out_dtype), (t // tm, 1, 1), NN, None)


def _mm_rows_t(name, dy, w, l, out_dtype):
    t, n = dy.shape
    k = w.shape[2]
    tm = _tile(t, 2048)
    return _gemm(
        name, [dy, w],
        [pl.BlockSpec((tm, n), lambda i, j, r: (i, 0)),
         pl.BlockSpec((None, None, k, n), lambda i, j, r: (l, 0, 0, 0))],
        pl.BlockSpec((tm, k), lambda i, j, r: (i, 0)),
        SDS((t, k), out_dtype), (t // tm, 1, 1), NT, None)


def _mm_wgrad(name, a, dy, buf, l, part=0):
    t, k = a.shape
    n = dy.shape[1]
    tt = _tile(t, 2 * WGRAD_TOKENS)
    return _gemm(
        name, [a, dy],
        [pl.BlockSpec((tt, k), lambda i, j, r: (r, 0)),
         pl.BlockSpec((tt, n), lambda i, j, r: (r, 0))],
        pl.BlockSpec((None, k, n), lambda i, j, r: (l, part, 0)),
        SDS(buf.shape, f32), (1, 1, t // tt), TN, (k, n), into=_into(buf))


def _ffn_up(h, wg, wu, l, gather):
    t, k = h.shape
    s, fs = wg.shape[1], wg.shape[2]
    tm = _tile(t)

    def body(h_ref, wg_ref, wu_ref, g_ref, u_ref, hid_ref):
        hv = h_ref[...]
        g = lax.dot_general(hv, wg_ref[...], (NT, ((), ())), preferred_element_type=f32)
        u = lax.dot_general(hv, wu_ref[...], (NT, ((), ())), preferred_element_type=f32)
        g_ref[...] = g.astype(bf16)
        u_ref[...] = u.astype(bf16)
        hid_ref[...] = (g * jax.nn.sigmoid(g) * u).astype(bf16)

    wspec = pl.BlockSpec((None, None, fs, k), lambda j, i: (l, j, 0, 0))
    ospec = pl.BlockSpec((None, tm, fs), lambda j, i: (j, i, 0))
    return _call(
        body, [h, wg, wu], grid=(s, t // tm), name="ffn_up",
        in_specs=[pl.BlockSpec((tm, k), lambda j, i: (i, 0)), wspec, wspec],
        out_specs=[ospec, ospec, ospec], out_shape=[SDS((s, t, fs), bf16)] * 3,
        sem=("parallel", "parallel"), gather=gather)


def _ffn_down(hid, wd, l):
    s, t, fs = hid.shape
    n = wd.shape[3]
    tm = _tile(t)
    ops, specs = [], []
    for r in range(s):
        ops += [hid, wd]
        specs += [pl.BlockSpec((None, tm, fs), lambda i, j, k, r=r: (r, i, 0)),
                  pl.BlockSpec((None, None, fs, n), lambda i, j, k, r=r: (l, r, 0, 0))]
    return _gemm("ffn_down", ops, specs, pl.BlockSpec((tm, n), lambda i, j, k: (i, 0)),
                 SDS((t, n), f32), (t // tm, 1, 1), NN, None)


def _ffn_down_bwd(df, wd, l, g, u, gather):
    t, n = df.shape
    s, fs = wd.shape[1], wd.shape[2]
    tm = _tile(t)

    def body(df_ref, wd_ref, g_ref, u_ref, dg_ref, du_ref):
        dh = lax.dot_general(df_ref[...], wd_ref[...], (NT, ((), ())), preferred_element_type=f32)
        gv = g_ref[...].astype(f32)
        uv = u_ref[...].astype(f32)
        sg = jax.nn.sigmoid(gv)
        du_ref[...] = (dh * gv * sg).astype(bf16)
        dg_ref[...] = (dh * uv * (sg * (1.0 + gv * (1.0 - sg)))).astype(bf16)

    bspec = pl.BlockSpec((None, tm, fs), lambda j, i: (j, i, 0))
    return _call(
        body, [df, wd, g, u], grid=(s, t // tm), name="ffn_down_bwd",
        in_specs=[pl.BlockSpec((tm, n), lambda j, i: (i, 0)),
                  pl.BlockSpec((None, None, fs, n), lambda j, i: (l, j, 0, 0)), bspec, bspec],
        out_specs=[bspec, bspec], out_shape=[SDS((s, t, fs), bf16)] * 2,
        sem=("parallel", "parallel"), gather=gather)


def _ffn_up_bwd(dg, du, wg, wu, l):
    s, t, fs = dg.shape
    k = wg.shape[3]
    tm = _tile(t, 512)
    ops, specs = [], []
    for r in range(s):
        aspec = pl.BlockSpec((None, tm, fs), lambda i, j, kk, r=r: (r, i, 0))
        wspec = pl.BlockSpec((None, None, fs, k), lambda i, j, kk, r=r: (l, r, 0, 0))
        ops += [dg, wg, du, wu]
        specs += [aspec, wspec, aspec, wspec]
    return _gemm("ffn_up_bwd", ops, specs, pl.BlockSpec((tm, k), lambda i, j, kk: (i, 0)),
                 SDS((t, k), f32), (t // tm, 1, 1), NN, None)


def _ffn_wgrad_up(h, dg, du, buf_g, buf_u, l):
    t, k = h.shape
    s, _, fs = dg.shape
    tt = _tile(t, WGRAD_TOKENS)
    nred = t // tt

    fresh = isinstance(buf_g, _Fresh)

    def body(*refs):
        h_ref, dg_ref, du_ref = refs[:3]
        og_ref, ou_ref, acc_g, acc_u = refs[-4:]
        r = pl.program_id(1)
        hv = h_ref[...]
        pg = lax.dot_general(dg_ref[...], hv, (TN, ((), ())), preferred_element_type=f32)
        pu = lax.dot_general(du_ref[...], hv, (TN, ((), ())), preferred_element_type=f32)

        @pl.when(r == 0)
        def _():
            acc_g[...] = pg
            acc_u[...] = pu

        @pl.when(r > 0)
        def _():
            acc_g[...] += pg
            acc_u[...] += pu

        @pl.when(r == nred - 1)
        def _():
            og_ref[...] = acc_g[...]
            ou_ref[...] = acc_u[...]

    dspec = pl.BlockSpec((None, tt, fs), lambda i, r: (i, r, 0))
    ospec = pl.BlockSpec((None, None, fs, k), lambda i, r: (l, i, 0, 0))
    extra, alias = ([], {}) if fresh else ([buf_g, buf_u], {3: 0, 4: 1})
    return pl.pallas_call(
        body, grid=(s, nred), name="ffn_wgrad_up",
        in_specs=[pl.BlockSpec((tt, k), lambda i, r: (r, 0)), dspec, dspec] + [ANY] * len(extra),
        out_specs=[ospec, ospec], out_shape=[SDS(buf_g.shape, f32), SDS(buf_u.shape, f32)],
        scratch_shapes=[pltpu.VMEM((fs, k), f32)] * 2, input_output_aliases=alias,
        compiler_params=_cparams(("parallel", "arbitrary")))(h, dg, du, *extra)


def _ffn_wgrad_down(hid, df, buf, l):
    s, t, fs = hid.shape
    n = df.shape[1]
    tt = _tile(t, 2 * WGRAD_TOKENS)
    return _gemm(
        "ffn_wgrad_down", [hid, df],
        [pl.BlockSpec((None, tt, fs), lambda i, j, r: (i, r, 0)),
         pl.BlockSpec((tt, n), lambda i, j, r: (r, 0))],
        pl.BlockSpec((None, None, fs, n), lambda i, j, r: (l, i, 0, 0)),
        SDS(buf.shape, f32), (s, 1, t // tt), TN, (fs, n), into=_into(buf))


def _rows(name, fn, rows, consts, row_outs, acc_outs=(), tr=512):
    rows = [r if isinstance(r, tuple) else (r, r.shape[1], 0) for r in rows]
    t = rows[0][0].shape[0]
    tr = max(d for d in range(SUBLANES, min(tr, t) + 1, SUBLANES) if t % d == 0)
    nin = len(rows) + len(consts)
    no, na = len(row_outs), len(acc_outs)

    def body(*refs):
        vals = fn(*[r[...] for r in refs[:nin]])
        if not isinstance(vals, (tuple, list)):
            vals = (vals,)
        for k in range(no):
            refs[nin + k][...] = vals[k].astype(refs[nin + k].dtype)
        first = pl.program_id(0) == 0
        for k in range(na):
            ref, val = refs[nin + no + k], vals[no + k]

            @pl.when(first)
            def _(ref=ref, val=val):
                ref[...] = val

            @pl.when(jnp.logical_not(first))
            def _(ref=ref, val=val):
                ref[...] += val

    in_specs = [pl.BlockSpec((tr, w), lambda i, cb=cb: (i, cb)) for (_, w, cb) in rows]
    in_specs += [pl.BlockSpec(c.shape, lambda i, nd=c.ndim: (0,) * nd) for c in consts]
    out_specs = [pl.BlockSpec((tr, w), lambda i: (i, 0)) for (w, _) in row_outs]
    out_specs += [pl.BlockSpec(s, lambda i, nd=len(s): (0,) * nd) for (s, _) in acc_outs]
    out_shape = [SDS((t, w), dt) for (w, dt) in row_outs] + [SDS(s, dt) for (s, dt) in acc_outs]
    res = pl.pallas_call(
        body, grid=(t // tr,), in_specs=in_specs, out_specs=out_specs, out_shape=out_shape,
        name=name, compiler_params=_cparams(("arbitrary",)))(*[r[0] for r in rows], *consts)
    return res


def _rstd(x):
    return lax.rsqrt(jnp.mean(x * x, axis=-1, keepdims=True) + RMS_EPS)


def _norm_fwd(x, g):
    return x * _rstd(x) * g


def _norm_bwd(u, dy, g):
    r = _rstd(u)
    n = u * r
    dn = dy * g
    du = r * (dn - n * jnp.mean(dn * n, axis=-1, keepdims=True))
    return du, jnp.sum(dy * n, axis=0, keepdims=True)


def _gelu(x):
    c = 0.7978845608028654
    return 0.5 * x * (1.0 + jnp.tanh(c * (x + 0.044715 * x * x * x)))


def _gelu_grad(x):
    c = 0.7978845608028654
    th = jnp.tanh(c * (x + 0.044715 * x * x * x))
    return 0.5 * (1.0 + th) + 0.5 * x * (1.0 - th * th) * c * (1.0 + 3.0 * 0.044715 * x * x)


def _mask_heads(x):
    lane = lax.broadcasted_iota(jnp.int32, x.shape, 1)
    return [jnp.where((lane >= h * HEAD_DIM) & (lane < (h + 1) * HEAD_DIM), x, jnp.zeros_like(x))
            for h in range(LANES // HEAD_DIM)]


def _chunk_valid(start):
    qi = lax.broadcasted_iota(jnp.int32, (QB_A, KW_A), 0)
    kj = lax.broadcasted_iota(jnp.int32, (QB_A, KW_A), 1)
    qc = qi // CHUNK
    kc = kj // CHUNK
    return (kc >= qc) & (kc <= qc + N_LEFT) & (kj + start >= PAD_A)


def _scaled(q):
    return q * (HEAD_DIM ** -0.5)


def _chunk_probs(q, k, bias, valid):
    s = lax.dot_general(q, k, (NT, ((), ())), preferred_element_type=f32) + bias
    s = jnp.where(valid, s, -1e30)
    p = jnp.exp(s - jnp.max(s, axis=-1, keepdims=True))
    return p / jnp.sum(p, axis=-1, keepdims=True)


def _chunk_attn_fwd(proj, kpad, vpad, bias, gather):
    t = proj.shape[0]
    tp = kpad.shape[0]
    step = QSUB_A * QB_A

    def body(q_ref, k_ref, v_ref, b_ref, o_ref):
        for sb in range(QSUB_A):
            start = pl.multiple_of((pl.program_id(1) * QSUB_A + sb) * QB_A, QB_A)
            rows = pl.ds(sb * QB_A, QB_A)
            valid = _chunk_valid(start)
            kw = k_ref[pl.ds(start, KW_A), :]
            qm = _mask_heads(_scaled(q_ref[rows, :]))
            vm = _mask_heads(v_ref[pl.ds(start, KW_A), :])
            o = None
            for h in range(len(qm)):
                p = _chunk_probs(qm[h], kw, b_ref[h], valid)
                d = jnp.dot(p.astype(bf16), vm[h], preferred_element_type=f32)
                o = d if o is None else o + d
            o_ref[rows, :] = o.astype(bf16)

    kv_spec = pl.BlockSpec((tp, LANES), lambda hp, qb: (0, hp))
    outs, new = _call(
        body, [proj, kpad, vpad, bias], grid=(A_W // LANES, t // step), name="chunk_attn_fwd",
        in_specs=[pl.BlockSpec((step, LANES), lambda hp, qb: (qb, hp)), kv_spec, kv_spec,
                  pl.BlockSpec((2, QB_A, KW_A), lambda hp, qb: (hp, 0, 0))],
        out_specs=[pl.BlockSpec((step, LANES), lambda hp, qb: (qb, hp))],
        out_shape=[SDS((t, A_W), bf16)], sem=("parallel", "arbitrary"), gather=gather)
    return outs[0], new


def _chunk_attn_bwd(proj, kpad, vpad, bias, dout, gather):
    t = proj.shape[0]
    tp = kpad.shape[0]
    step = QSUB_A * QB_A

    def body(q_ref, k_ref, v_ref, b_ref, do_ref, dq_ref, dk_ref, dv_ref, db_ref):
        qb = pl.program_id(1)

        @pl.when(qb == 0)
        def _():
            dk_ref[...] = jnp.zeros_like(dk_ref)
            dv_ref[...] = jnp.zeros_like(dv_ref)
            db_ref[...] = jnp.zeros_like(db_ref)

        for sb in range(QSUB_A):
            start = pl.multiple_of((qb * QSUB_A + sb) * QB_A, QB_A)
            rows = pl.ds(sb * QB_A, QB_A)
            win = pl.ds(start, KW_A)
            valid = _chunk_valid(start)
            kw = k_ref[win, :]
            vw = v_ref[win, :]
            qm = _mask_heads(_scaled(q_ref[rows, :]))
            dom = _mask_heads(do_ref[rows, :])
            km = _mask_heads(kw)
            dq = dk = dv = None
            for h in range(len(qm)):
                p = _chunk_probs(qm[h], kw, b_ref[h], valid)
                dp = lax.dot_general(dom[h], vw, (NT, ((), ())), preferred_element_type=f32)
                ds = p * (dp - jnp.sum(dp * p, axis=-1, keepdims=True))
                db_ref[h] += ds
                dsb = ds.astype(bf16)
                terms = (jnp.dot(dsb, km[h], preferred_element_type=f32),
                         lax.dot_general(dsb, qm[h], (TN, ((), ())), preferred_element_type=f32),
                         lax.dot_general(p.astype(bf16), dom[h], (TN, ((), ())), preferred_element_type=f32))
                dq, dk, dv = terms if dq is None else (dq + terms[0], dk + terms[1], dv + terms[2])
            dq_ref[rows, :] = _scaled(dq).astype(bf16)
            dk_ref[win, :] += dk
            dv_ref[win, :] += dv

    kv_spec = pl.BlockSpec((tp, LANES), lambda hp, qb: (0, hp))
    q_spec = pl.BlockSpec((step, LANES), lambda hp, qb: (qb, hp))
    b_spec = pl.BlockSpec((2, QB_A, KW_A), lambda hp, qb: (hp, 0, 0))
    return _call(
        body, [proj, kpad, vpad, bias, dout], grid=(A_W // LANES, t // step), name="chunk_attn_bwd",
        in_specs=[q_spec, kv_spec, kv_spec, b_spec, q_spec],
        out_specs=[q_spec, kv_spec, kv_spec, b_spec],
        out_shape=[SDS((t, A_W), bf16), SDS((tp, A_W), f32), SDS((tp, A_W), f32),
                   SDS((2 * A_W // LANES, QB_A, KW_A), f32)],
        sem=("parallel", "arbitrary"), gather=gather)


def _bias_ext(table):
    flat = PAD_A + QB_A - 1 - REL_CLIP
    top = jnp.broadcast_to(table[:, 2 * REL_CLIP:], (table.shape[0], flat))
    lo = 2 * REL_CLIP - (EXT_A - 1 - flat)
    return jnp.concatenate([top, jnp.flip(table[:, lo:], axis=1)], axis=1)


def _bias_window(table):
    nh = table.shape[0]
    e = jnp.broadcast_to(_bias_ext(table)[:, None, :], (nh, QB_A, EXT_A)).reshape(nh, QB_A * EXT_A)
    m = e[:, :QB_A * (EXT_A - 1)].reshape(nh, QB_A, EXT_A - 1)
    return m[:, :, QB_A - 1:]


def _bias_window_grad(dbias):
    nh = dbias.shape[0]
    m = jnp.pad(dbias, ((0, 0), (0, 0), (QB_A - 1, 0))).reshape(nh, QB_A * (EXT_A - 1))
    dext = jnp.sum(jnp.pad(m, ((0, 0), (0, QB_A))).reshape(nh, QB_A, EXT_A), axis=1)
    flat = PAD_A + QB_A - 1 - REL_CLIP
    lo = 2 * REL_CLIP - (EXT_A - 1 - flat)
    tail = jnp.flip(dext[:, flat:], axis=1)
    tail = tail.at[:, -1].add(jnp.sum(dext[:, :flat], axis=1))
    return jnp.pad(tail, ((0, 0), (lo, 0)))


def _tri_suffix(x, tri):
    hi = x.astype(bf16)
    lo = (x - hi.astype(f32)).astype(bf16)
    return jnp.dot(hi, tri, preferred_element_type=f32) + jnp.dot(lo, tri, preferred_element_type=f32)


def _sb_block(q, k, run, tri, causal):
    z = lax.dot_general(q, k, (NT, ((), ())), preferred_element_type=f32)
    e = jnp.exp(-jnp.abs(z))
    l1p = jnp.log(1.0 + e)
    lb = jnp.minimum(z, 0.0) - l1p
    lmb = lb - z
    if causal is not None:
        lmb = jnp.where(causal, lmb, 0.0)
    cs = _tri_suffix(lmb, tri)
    w = jnp.exp(lb + (run + cs - lmb))
    if causal is not None:
        w = jnp.where(causal, w, 0.0)
    return z, e, w, run + cs[:, 0:1]


def _sb_tri():
    r = lax.broadcasted_iota(jnp.int32, (SB_BLK, SB_BLK), 0)
    c = lax.broadcasted_iota(jnp.int32, (SB_BLK, SB_BLK), 1)
    return (r >= c).astype(bf16), c < r


def _sb_live(runs):
    m = runs[0]
    for r in runs[1:]:
        m = jnp.maximum(m, r)
    return jnp.max(m) > SB_DEAD


def _sb_fwd(proj, gather):
    t = proj.shape[0]
    cb = A_W // LANES
    nh = LANES // HEAD_DIM

    step_rows = QSUB_B * SB_BLK

    def body(q_ref, k_ref, v_ref, o_ref, of_ref):
        tri, diag = _sb_tri()
        for sb in range(QSUB_B):
            _sb_fwd_block(pl.program_id(1) * QSUB_B + sb, pl.ds(sb * SB_BLK, SB_BLK), tri, diag,
                          q_ref, k_ref, v_ref, o_ref, of_ref)

    def _sb_fwd_block(qb, qrows, tri, diag, q_ref, k_ref, v_ref, o_ref, of_ref):
        qm = _mask_heads(_scaled(q_ref[qrows, :]))

        def pair(kb, carry, causal):
            rows = pl.ds(pl.multiple_of(kb * SB_BLK, SB_BLK), SB_BLK)
            k = k_ref[rows, :]
            vm = _mask_heads(v_ref[rows, :])
            runs, acc = [], carry[nh]
            for h in range(nh):
                _, _, w, run = _sb_block(qm[h], k, carry[h], tri, causal)
                acc = acc + jnp.dot(w.astype(bf16), vm[h], preferred_element_type=f32)
                runs.append(run)
            return (*runs, acc)

        zero = jnp.zeros((SB_BLK, 1), f32)
        carry = pair(qb, (zero,) * nh + (jnp.zeros((SB_BLK, LANES), f32),), diag)

        def cond(st):
            return (st[0] < qb) & _sb_live(st[1][:nh])

        def step(st):
            return st[0] + 1, pair(qb - 1 - st[0], st[1], None)

        _, carry = lax.while_loop(cond, step, (jnp.int32(0), carry))
        o_ref[qrows, :] = carry[nh].astype(bf16)
        of_ref[qrows, :] = carry[nh]

    ospec = pl.BlockSpec((step_rows, LANES), lambda hp, qb: (qb, hp))
    return _call(
        body, [proj, proj, proj], grid=(cb, t // step_rows), name="sb_attn_fwd",
        in_specs=[pl.BlockSpec((step_rows, LANES), lambda hp, qb: (qb, 3 * cb + hp)),
                  pl.BlockSpec((t, LANES), lambda hp, qb: (0, 4 * cb + hp)),
                  pl.BlockSpec((t, LANES), lambda hp, qb: (0, 5 * cb + hp))],
        out_specs=[ospec, ospec], out_shape=[SDS((t, A_W), bf16), SDS((t, A_W), f32)],
        sem=("parallel", "arbitrary"), gather=gather)


def _sb_bwd(proj, out_b, dout, gather):
    t = proj.shape[0]
    cb = A_W // LANES
    nh = LANES // HEAD_DIM

    step_rows = QSUB_B * SB_BLK

    def body(q_ref, k_ref, v_ref, o_ref, do_ref, dq_ref, dk_ref, dv_ref):
        tri, diag = _sb_tri()

        @pl.when(pl.program_id(1) == 0)
        def _():
            dk_ref[...] = jnp.zeros_like(dk_ref)
            dv_ref[...] = jnp.zeros_like(dv_ref)

        for sb in range(QSUB_B):
            _sb_bwd_block(pl.program_id(1) * QSUB_B + sb, pl.ds(sb * SB_BLK, SB_BLK), tri, diag,
                          q_ref, k_ref, v_ref, o_ref, do_ref, dq_ref, dk_ref, dv_ref)

    def _sb_bwd_block(qb, qrows, tri, diag, q_ref, k_ref, v_ref, o_ref, do_ref, dq_ref, dk_ref, dv_ref):
        qm = _mask_heads(_scaled(q_ref[qrows, :]))
        do = do_ref[qrows, :]
        dom = _mask_heads(do)
        dsums = [jnp.sum(t_, axis=-1, keepdims=True) for t_ in _mask_heads(do.astype(f32) * o_ref[qrows, :])]

        def pair(kb, carry, causal):
            rows = pl.ds(pl.multiple_of(kb * SB_BLK, SB_BLK), SB_BLK)
            k = k_ref[rows, :]
            v = v_ref[rows, :]
            km = _mask_heads(k)
            new, dq, dk, dv = [], carry[2 * nh], None, None
            for h in range(nh):
                z, e, w, run = _sb_block(qm[h], k, carry[2 * h], tri, causal)
                inv = 1.0 / (1.0 + e)
                beta = jnp.where(z >= 0.0, inv, e * inv)
                wb = w.astype(bf16)
                g = lax.dot_general(dom[h], v, (NT, ((), ())), preferred_element_type=f32) * wb.astype(f32)
                sg = _tri_suffix(g, tri)
                dz = g - (g + (dsums[h] - carry[2 * h + 1] - sg)) * beta
                if causal is not None:
                    dz = jnp.where(causal, dz, 0.0)
                dzb = dz.astype(bf16)
                dq = dq + jnp.dot(dzb, km[h], preferred_element_type=f32)
                tk = lax.dot_general(dzb, qm[h], (TN, ((), ())), preferred_element_type=f32)
                tv = lax.dot_general(wb, dom[h], (TN, ((), ())), preferred_element_type=f32)
                dk, dv = (tk, tv) if dk is None else (dk + tk, dv + tv)
                new += [run, carry[2 * h + 1] + sg[:, 0:1]]
            dk_ref[rows, :] += dk
            dv_ref[rows, :] += dv
            return (*new, dq)

        zero = jnp.zeros((SB_BLK, 1), f32)
        carry = pair(qb, (zero,) * (2 * nh) + (jnp.zeros((SB_BLK, LANES), f32),), diag)

        def cond(st):
            return (st[0] < qb) & _sb_live(st[1][0:2 * nh:2])

        def step(st):
            return st[0] + 1, pair(qb - 1 - st[0], st[1], None)

        _, carry = lax.while_loop(cond, step, (jnp.int32(0), carry))
        dq_ref[qrows, :] = _scaled(carry[2 * nh]).astype(bf16)

    kv_in = lambda seg: pl.BlockSpec((t, LANES), lambda hp, qb: (0, seg * cb + hp))
    q_spec = pl.BlockSpec((step_rows, LANES), lambda hp, qb: (qb, hp))
    kv_out = pl.BlockSpec((t, LANES), lambda hp, qb: (0, hp))
    return _call(
        body, [proj, proj, proj, out_b, dout], grid=(cb, t // step_rows), name="sb_attn_bwd",
        in_specs=[pl.BlockSpec((step_rows, LANES), lambda hp, qb: (qb, 3 * cb + hp)), kv_in(4), kv_in(5),
                  q_spec, pl.BlockSpec((step_rows, LANES), lambda hp, qb: (qb, cb + hp))],
        out_specs=[q_spec, kv_out, kv_out],
        out_shape=[SDS((t, A_W), bf16), SDS((t, A_W), f32), SDS((t, A_W), f32)],
        sem=("parallel", "arbitrary"), gather=gather)


def _halo_specs(tr, w, col, nblk):
    per = tr // SUBLANES
    cur = pl.BlockSpec((tr, w), lambda i: (i, col))
    prev = pl.BlockSpec((SUBLANES, w), lambda i: (jnp.maximum(i * per - 1, 0), col))
    nxt = pl.BlockSpec((SUBLANES, w), lambda i: (jnp.minimum((i + 1) * per, nblk * per - 1), col))
    return cur, prev, nxt


def _taps_before(cur, prev8, first):
    prev8 = jnp.where(first, 0.0, prev8)
    ext = jnp.concatenate([prev8, cur], axis=0)
    return [pltpu.roll(ext, s, 0)[SUBLANES:] for s in (3, 2, 1)]


def _taps_after(cur, next8, last):
    n = cur.shape[0]
    next8 = jnp.where(last, 0.0, next8)
    ext = jnp.concatenate([cur, next8], axis=0)
    return [pltpu.roll(ext, n + SUBLANES - s, 0)[:n] for s in (1, 2, 3)]


def _block_diag(x, w_ref, dims):
    outs = [lax.dot_general(x[:, n * LRU_BW:(n + 1) * LRU_BW], w_ref[n], (dims, ((), ())),
                            preferred_element_type=f32) for n in range(LRU_BLOCKS)]
    return jnp.concatenate(outs, axis=1)


def _lru_gates(xc, wa_ref, wi_ref, ba, bi, lam):
    xb = xc.astype(bf16)
    r = jax.nn.sigmoid(_block_diag(xb, wa_ref, NN) + ba)
    ig = jax.nn.sigmoid(_block_diag(xb, wi_ref, NN) + bi)
    sp = jnp.maximum(-lam, 0.0) + jnp.log(1.0 + jnp.exp(-jnp.abs(lam)))
    log_a = -LRU_C * r * sp
    a = jnp.exp(log_a)
    x2 = 2.0 * log_a
    one_minus = jnp.where(x2 > -1e-2, -x2 * (1.0 + x2 * (0.5 + x2 * (1.0 / 6.0))), 1.0 - a * a)
    mult = jnp.sqrt(one_minus)
    return xb, r, ig, sp, a, mult


def _rg_gates_fwd(proj, conv_w, conv_b, wa, wi, ba, bi, lam, tr=512):
    t = proj.shape[0]
    w = D_MODEL
    tr = min(tr, t)
    nblk = t // tr
    cur, prev, _ = _halo_specs(tr, w, 1, nblk)

    def body(x_ref, xp_ref, cw_ref, cb_ref, wa_ref, wi_ref, ba_ref, bi_ref, lam_ref, xc_ref, a_ref, u_ref):
        x = x_ref[...]
        taps = _taps_before(x, xp_ref[...], pl.program_id(0) == 0) + [x]
        xc = cb_ref[...]
        for k in range(4):
            xc = xc + cw_ref[k:k + 1, :] * taps[k]
        _, _, ig, _, a, mult = _lru_gates(xc, wa_ref, wi_ref, ba_ref[...], bi_ref[...], lam_ref[...])
        xc_ref[...] = xc
        a_ref[...] = a
        u_ref[...] = mult * (ig * xc)

    full = lambda a_: pl.BlockSpec(a_.shape, lambda i, nd=a_.ndim: (0,) * nd)
    ospec = pl.BlockSpec((tr, w), lambda i: (i, 0))
    return pl.pallas_call(
        body, grid=(nblk,), name="rg_gates_fwd",
        in_specs=[cur, prev] + [full(a_) for a_ in (conv_w, conv_b, wa, wi, ba, bi, lam)],
        out_specs=[ospec] * 3, out_shape=[SDS((t, w), f32)] * 3,
        compiler_params=_cparams(("parallel",)))(proj, proj, conv_w, conv_b, wa, wi, ba, bi, lam)


def _lru_scan(name, a, b, reverse, tt=1024):
    t, w = a.shape
    tt = min(tt, t)
    nt = t // tt
    ng = tt // SUBLANES

    def body(a_ref, b_ref, h_ref, carry_ref):
        @pl.when(pl.program_id(0) == 0)
        def _():
            carry_ref[...] = jnp.zeros_like(carry_ref)

        row = lax.broadcasted_iota(jnp.int32, (SUBLANES, w), 0)

        def group(gi, carry):
            g = (ng - 1 - gi) if reverse else gi
            rows = pl.ds(pl.multiple_of(g * SUBLANES, SUBLANES), SUBLANES)
            av = a_ref[rows, :]
            bv = b_ref[rows, :]
            for s in (1, 2, 4):
                sh = (SUBLANES - s) if reverse else s
                ok = (row < SUBLANES - s) if reverse else (row >= s)
                a_s = pltpu.roll(av, sh, 0)
                b_s = pltpu.roll(bv, sh, 0)
                bv = jnp.where(ok, av * b_s + bv, bv)
                av = jnp.where(ok, av * a_s, av)
            h = av * carry + bv
            h_ref[rows, :] = h
            edge = h[0:1, :] if reverse else h[SUBLANES - 1:SUBLANES, :]
            return jnp.broadcast_to(edge, (SUBLANES, w))

        carry_ref[...] = lax.fori_loop(0, ng, group, carry_ref[...], unroll=4)

    tmap = (lambda i: (nt - 1 - i, 0)) if reverse else (lambda i: (i, 0))
    spec = pl.BlockSpec((tt, w), tmap)
    return pl.pallas_call(
        body, grid=(nt,), name=name, in_specs=[spec, spec], out_specs=spec,
        out_shape=SDS((t, w), f32), scratch_shapes=[pltpu.VMEM((SUBLANES, w), f32)],
        compiler_params=_cparams(("arbitrary",)))(a, b)


def _rg_gates_bwd(dhs, c, hs, xc, wa, wi, ba, bi, lam, tr=512):
    t, w = xc.shape
    tr = min(tr, t)
    nblk = t // tr
    cur, prev, nxt = _halo_specs(tr, w, 0, nblk)

    def body(dhs_ref, c_ref, cn_ref, hs_ref, hp_ref, xc_ref, wa_ref, wi_ref, ba_ref, bi_ref, lam_ref,
             dxc_ref, dwa_ref, dwi_ref, dba_ref, dbi_ref, dlam_ref):
        i = pl.program_id(0)
        c_next = _taps_after(c_ref[...], cn_ref[...], i == nblk - 1)[0]
        h_prev = _taps_before(hs_ref[...], hp_ref[...], i == 0)[2]
        xc = xc_ref[...]
        lam = lam_ref[...]
        xb, r, ig, sp, a, mult = _lru_gates(xc, wa_ref, wi_ref, ba_ref[...], bi_ref[...], lam)
        dh = dhs_ref[...] + c_next
        dlog_a = dh * h_prev * a - (dh * ig * xc) * (a * a / mult)
        dpre_a = (dlog_a * (-LRU_C * sp) * r * (1.0 - r)).astype(bf16)
        dpre_i = (dh * mult * xc * ig * (1.0 - ig)).astype(bf16)
        dxc_ref[...] = (dh * mult * ig + _block_diag(dpre_a, wa_ref, NT) + _block_diag(dpre_i, wi_ref, NT))
        dsig = 1.0 / (1.0 + jnp.exp(lam))
        sums = [jnp.sum(dpre_a.astype(f32), axis=0, keepdims=True),
                jnp.sum(dpre_i.astype(f32), axis=0, keepdims=True),
                jnp.sum(dlog_a * (-LRU_C * r), axis=0, keepdims=True) * (-dsig)]

        @pl.when(i == 0)
        def _():
            dwa_ref[...] = jnp.zeros_like(dwa_ref)
            dwi_ref[...] = jnp.zeros_like(dwi_ref)
            dba_ref[...] = jnp.zeros_like(dba_ref)
            dbi_ref[...] = jnp.zeros_like(dbi_ref)
            dlam_ref[...] = jnp.zeros_like(dlam_ref)

        for n in range(LRU_BLOCKS):
            sl = slice(n * LRU_BW, (n + 1) * LRU_BW)
            dwa_ref[n] += lax.dot_general(xb[:, sl], dpre_a[:, sl], (TN, ((), ())), preferred_element_type=f32)
            dwi_ref[n] += lax.dot_general(xb[:, sl], dpre_i[:, sl], (TN, ((), ())), preferred_element_type=f32)
        dba_ref[...] += sums[0]
        dbi_ref[...] += sums[1]
        dlam_ref[...] += sums[2]

    full = lambda a_: pl.BlockSpec(a_.shape, lambda i, nd=a_.ndim: (0,) * nd)
    vec = pl.BlockSpec((1, w), lambda i: (0, 0))
    mat = pl.BlockSpec((LRU_BLOCKS, LRU_BW, LRU_BW), lambda i: (0, 0, 0))
    return pl.pallas_call(
        body, grid=(nblk,), name="rg_gates_bwd",
        in_specs=[cur, cur, nxt, cur, prev, cur] + [full(a_) for a_ in (wa, wi, ba, bi, lam)],
        out_specs=[cur, mat, mat, vec, vec, vec],
        out_shape=[SDS((t, w), f32), SDS((LRU_BLOCKS, LRU_BW, LRU_BW), f32), SDS((LRU_BLOCKS, LRU_BW, LRU_BW), f32),
                   SDS((1, w), f32), SDS((1, w), f32), SDS((1, w), f32)],
        compiler_params=_cparams(("arbitrary",)))(dhs, c, c, hs, hs, xc, wa, wi, ba, bi, lam)


def _rg_conv_bwd(dxc, proj, conv_w, tr=512):
    t, w = dxc.shape
    tr = min(tr, t)
    nblk = t // tr
    cur, _, nxt = _halo_specs(tr, w, 0, nblk)
    xcur, xprev, _ = _halo_specs(tr, w, 1, nblk)

    def body(d_ref, dn_ref, x_ref, xp_ref, cw_ref, dx_ref, dcw_ref, dcb_ref):
        i = pl.program_id(0)
        d = d_ref[...]
        x = x_ref[...]
        after = _taps_after(d, dn_ref[...], i == nblk - 1)
        before = _taps_before(x, xp_ref[...], i == 0) + [x]
        dx = cw_ref[3:4, :] * d
        for s in (1, 2, 3):
            dx = dx + cw_ref[3 - s:4 - s, :] * after[s - 1]
        dx_ref[...] = dx.astype(bf16)
        dcw = jnp.concatenate([jnp.sum(d * before[k], axis=0, keepdims=True) for k in range(4)], axis=0)
        dcb = jnp.sum(d, axis=0, keepdims=True)

        @pl.when(i == 0)
        def _():
            dcw_ref[...] = dcw
            dcb_ref[...] = dcb

        @pl.when(i > 0)
        def _():
            dcw_ref[...] += dcw
            dcb_ref[...] += dcb

    return pl.pallas_call(
        body, grid=(nblk,), name="rg_conv_bwd",
        in_specs=[cur, nxt, xcur, xprev, pl.BlockSpec((4, w), lambda i: (0, 0))],
        out_specs=[cur, pl.BlockSpec((4, w), lambda i: (0, 0)), pl.BlockSpec((1, w), lambda i: (0, 0))],
        out_shape=[SDS((t, w), bf16), SDS((4, w), f32), SDS((1, w), f32)],
        compiler_params=_cparams(("arbitrary",)))(dxc, dxc, proj, proj, conv_w)


def _attn_fwd(h, wts, j, plan):
    proj = _mm_cols("attn_in", h, wts["attn_w_in"], j, bf16)
    kpad = jnp.pad(proj[:, A_W:2 * A_W], ((PAD_A, 0), (0, 0)))
    vpad = jnp.pad(proj[:, 2 * A_W:3 * A_W], ((PAD_A, 0), (0, 0)))
    bias = _bias_window(wts["attn_rel_bias"][j])
    plan = plan if j == 0 else None
    out_a = _carried(plan, "chunk_attn_fwd", wts, _chunk_attn_fwd, proj, kpad, vpad, bias)
    out_b, out_b32 = _carried(plan, "sb_attn_fwd", wts, _sb_fwd, proj)
    m = _mm_rows("attn_out", [out_a, out_b], wts["attn_w_out"], j, f32)
    return m, (proj, kpad, vpad, bias, out_a, out_b, out_b32)


def _attn_bwd(dm, h, saved, wts, j, grads, exch):
    proj, kpad, vpad, bias, out_a, out_b, out_b32 = saved
    dout = _mm_rows_t("attn_out_t", dm, wts["attn_w_out"], j, bf16)
    gi, ll = _grad_slot("attn_w_out", j)
    grads["attn_w_out"][gi] = _mm_wgrad("attn_out_wgrad_a", out_a, dm, grads["attn_w_out"][gi], ll, 0)
    grads["attn_w_out"][gi] = _mm_wgrad("attn_out_wgrad_b", out_b, dm, grads["attn_w_out"][gi], ll, 1)
    if exch is not None and j == 0:
        (dqa, dka, dva, dbias), got = _chunk_attn_bwd(proj, kpad, vpad, bias, dout, exch.upper_carry(grads))
        exch.upper_got(got)
        (dqs, dks, dvs), slots = _sb_bwd(proj, out_b32, dout, exch.carry())
        exch.carried(slots)
    else:
        dqa, dka, dva, dbias = _chunk_attn_bwd(proj, kpad, vpad, bias, dout, None)[0]
        dqs, dks, dvs = _sb_bwd(proj, out_b32, dout, None)[0]
    grads["attn_rel_bias"][j] = _bias_window_grad(dbias)
    dproj = jnp.concatenate([dqa, dka[PAD_A:].astype(bf16), dva[PAD_A:].astype(bf16),
                             dqs, dks.astype(bf16), dvs.astype(bf16)], axis=1)
    gi, ll = _grad_slot("attn_w_in", j)
    grads["attn_w_in"][gi] = _mm_wgrad_cols("attn_in_wgrad", h, dproj, grads["attn_w_in"][gi], ll)
    return _mm_cols_t("attn_in_t", dproj, wts["attn_w_in"], j, f32)


def _rg_fwd(h, wts, j, plan):
    proj =_mm_cols("rg_in", h, wts["rg_w_in"], j, f32)
    small = [wts[k][j] for k in ("rg_conv_w", "rg_conv_b", "rg_w_a", "rg_w_i", "rg_b_a", "rg_b_i", "rg_lambda")]
    xc, a, u = _rg_gates_fwd(proj, *small)
    hs = _lru_scan("lru_scan_fwd", a, u, False)
    yp = _rows("rg_gate_out", lambda hv, gv: hv * _gelu(gv), [hs, (proj, D_MODEL, 0)], [], [(D_MODEL, bf16)])[0]
    m = _mm_rows("rg_out", [yp], wts["rg_w_out"], j, f32)
    return m, (proj, xc, a, hs, yp)


def _rg_bwd(dm, h, saved, wts, j, grads, exch):
    proj, xc, a, hs, yp = saved
    dyp = _mm_rows_t("rg_out_t", dm, wts["rg_w_out"], j, f32)
    gi, ll = _grad_slot("rg_w_out", j)
    grads["rg_w_out"][gi] = _mm_wgrad("rg_out_wgrad", yp, dm, grads["rg_w_out"][gi], ll)

    def gate_bwd(dy, hv, gv, av):
        dhs = dy * _gelu(gv)
        return dhs, av * dhs, dy * hv * _gelu_grad(gv)

    dhs, ab, dgate = _rows("rg_gate_out_bwd", gate_bwd, [dyp, hs, (proj, D_MODEL, 0), a], [],
                           [(D_MODEL, f32), (D_MODEL, f32), (D_MODEL, bf16)])
    c = _lru_scan("lru_scan_bwd", a, ab, True)
    wa, wi, ba, bi, lam = [wts[k][j] for k in ("rg_w_a", "rg_w_i", "rg_b_a", "rg_b_i", "rg_lambda")]
    dxc, dwa, dwi, dba, dbi, dlam = _rg_gates_bwd(dhs, c, hs, xc, wa, wi, ba, bi, lam)
    dxr, dcw, dcb = _rg_conv_bwd(dxc, proj, wts["rg_conv_w"][j])
    for k, v in (("rg_w_a", dwa), ("rg_w_i", dwi), ("rg_b_a", dba), ("rg_b_i", dbi), ("rg_lambda", dlam),
                 ("rg_conv_w", dcw), ("rg_conv_b", dcb)):
        grads[k][j] = v
    dproj = jnp.concatenate([dgate, dxr], axis=1)
    grads["rg_w_in"][gi] = _mm_wgrad_cols("rg_in_wgrad", h, dproj, grads["rg_w_in"][gi], ll)
    return _mm_cols_t("rg_in_t", dproj, wts["rg_w_in"], j, f32)


def _local_step(x, target, wts, plan=None, exch=None):
    t = x.shape[0]
    d = D_MODEL
    gains = {k: wts[k] for k in ("norm_mix_pre", "norm_mix_post", "norm_ffn_pre", "norm_ffn_post")}
    gain = lambda k, l: gains[k][l:l + 1]

    saved = []
    h = _rows("norm_in", _norm_fwd, [x], [gain("norm_mix_pre", 0)], [(d, bf16)])[0]
    loss_cols = None
    for l in range(DEPTH):
        j = l // 2
        m, mix_saved = (_attn_fwd if l % 2 == 0 else _rg_fwd)(h, wts, j, plan)

        def resid_next(xv, mv, g_post, g_next):
            x1 = xv + _norm_fwd(mv, g_post)
            return x1, _norm_fwd(x1, g_next)

        x1, h2 = _rows("resid_mix", resid_next, [x, m], [gain("norm_mix_post", l), gain("norm_ffn_pre", l)],
                       [(d, f32), (d, bf16)], tr=1024)
        g, u, hid = _carried(plan if l == 0 else None, "ffn_up", wts, _ffn_up, h2, wts["ffn_w_gate"],
                             wts["ffn_w_up"], l)
        f = _ffn_down(hid, wts["ffn_w_down"], l)
        saved.append((x, h, m, mix_saved, x1, h2, g, u, hid, f))
        if l + 1 < DEPTH:
            x, h = _rows("resid_ffn", resid_next, [x1, f], [gain("norm_ffn_post", l), gain("norm_mix_pre", l + 1)],
                         [(d, f32), (d, bf16)], tr=1024)
        else:
            def resid_loss(xv, fv, tv, g_post):
                err = xv + _norm_fwd(fv, g_post) - tv
                return err * (1.0 / d), jnp.sum(err * err, axis=0, keepdims=True)

            dx, loss_cols = _rows("resid_loss", resid_loss, [x1, f, target], [gain("norm_ffn_post", l)],
                                  [(d, f32)], [((1, d), f32)])
    loss = 0.5 * jnp.sum(loss_cols) / d

    grads = {k: {} for k in SMALL_GRADS}
    for k in BIG_GRADS:
        shp = wts[k].shape
        rest = shp[2:] if shp[1] == 1 else shp[1:]
        grads[k] = [_Fresh((LOWER_LAYERS[k],) + rest), _Fresh((shp[0] - LOWER_LAYERS[k],) + rest)]

    def norm_bwd_cast(uv, dyv, gv):
        du, dg = _norm_bwd(uv, dyv, gv)
        return du, dg

    def norm_bwd_resid(uv, dhv, dxv, gv):
        du, dg = _norm_bwd(uv, dhv, gv)
        return dxv + du, dg

    def norm_bwd_pair(uv, dhv, dxv, nv, g_pre, g_post):
        dx_, dg_pre = norm_bwd_resid(uv, dhv, dxv, g_pre)
        dn, dg_post = _norm_bwd(nv, dx_, g_post)
        return dx_, dn, dg_pre, dg_post

    df = None
    for l in reversed(range(DEPTH)):
        j = l // 2
        x_in, h, m, mix_saved, x1, h2, g, u, hid, f = saved[l]
        if df is None:
            df, grads["norm_ffn_post"][l] = _rows("norm_ffn_post_bwd", norm_bwd_cast, [f, dx],
                                                  [gain("norm_ffn_post", l)], [(d, bf16)], [((1, d), f32)])
        dg, du = _ffn_down_bwd(df, wts["ffn_w_down"], l, g, u, None)[0]
        gi, ll = _grad_slot("ffn_w_down", l)
        grads["ffn_w_down"][gi] = _ffn_wgrad_down(hid, df, grads["ffn_w_down"][gi], ll)
        dh2 = _ffn_up_bwd(dg, du, wts["ffn_w_gate"], wts["ffn_w_up"], l)
        grads["ffn_w_gate"][gi], grads["ffn_w_up"][gi] = _ffn_wgrad_up(
            h2, dg, du, grads["ffn_w_gate"][gi], grads["ffn_w_up"][gi], ll)
        dx1, dm, grads["norm_ffn_pre"][l], grads["norm_mix_post"][l] = _rows(
            "norm_ffn_mix_bwd", norm_bwd_pair, [x1, dh2, dx, m], [gain("norm_ffn_pre", l), gain("norm_mix_post", l)],
            [(d, f32), (d, bf16)], [((1, d), f32), ((1, d), f32)])
        dh = (_attn_bwd if l % 2 == 0 else _rg_bwd)(dm, h, mix_saved, wts, j, grads, exch)
        if l > 0:
            dx, df, grads["norm_mix_pre"][l], grads["norm_ffn_post"][l - 1] = _rows(
                "norm_mix_ffn_bwd", norm_bwd_pair, [x_in, dh, dx1, saved[l - 1][9]],
                [gain("norm_mix_pre", l), gain("norm_ffn_post", l - 1)],
                [(d, f32), (d, bf16)], [((1, d), f32), ((1, d), f32)])
        else:
            dx, grads["norm_mix_pre"][l] = _rows("norm_mix_pre_bwd", norm_bwd_resid, [x_in, dh, dx1],
                                                 [gain("norm_mix_pre", l)], [(d, f32)], [((1, d), f32)])
    return loss, dx, grads


ANY = pl.BlockSpec(memory_space=pl.ANY)
PACK_COLS = 1024
SMALL_ROWS = 288


def _mesh_pos():
    x, y, c = lax.axis_index("x"), lax.axis_index("y"), lax.axis_index("c")
    return x, y, c, [(1 - x, y), (x, 1 - y), (1 - x, 1 - y)]


def _run_copies(copies):
    for cp in copies:
        cp.start()
    for cp in copies:
        cp.wait()


GATHER_SEMS = 7


def _gather_copies(items, ins, outs, send, recv):
    x, y, c, chips = _mesh_pos()
    q = 2 * x + y
    sibling = (x, y, 1 - c)

    def copy(k, src, dst, to):
        return pltpu.make_async_remote_copy(src_ref=src, dst_ref=dst, send_sem=send.at[k], recv_sem=recv.at[k],
                                            device_id=to, device_id_type=MESH)

    own, sent, passed = [], [], []
    for i, (t, l0, nl) in enumerate(items):
        lay = pl.ds(l0, nl)
        half = ins[t].shape[1] // 2
        rows = pl.ds(pl.multiple_of(c * half, half), half)
        own.append(copy(GATHER_SEMS * i, ins[t].at[lay], outs[t].at[lay, q], sibling))
        for j, (px, py) in enumerate(chips):
            sent.append(copy(GATHER_SEMS * i + 1 + j, ins[t].at[lay, rows], outs[t].at[lay, q, rows], (px, py, c)))
            landed = outs[t].at[lay, 2 * px + py, rows]
            passed.append(copy(GATHER_SEMS * i + 4 + j, landed, landed, sibling))
    return own, sent, passed


def _gather_start(items, ins, outs, send, recv):
    own, sent, _ = _gather_copies(items, ins, outs, send, recv)
    for cp in own + sent:
        cp.start()


def _gather_finish(items, ins, outs, send, recv):
    own, sent, passed = _gather_copies(items, ins, outs, send, recv)
    for arrived, forward in zip(sent, passed):
        arrived.wait_recv()
        forward.start()
    for cp in sent:
        cp.wait_send()
    for cp in own + passed:
        cp.wait()


def _gather_call(items, shards):
    n = len(shards)
    nsem = GATHER_SEMS * len(items)

    def body(*refs):
        ins, outs = refs[:n], refs[n:2 * n]
        _gather_start(items, ins, outs, *refs[2 * n:])
        _gather_finish(items, ins, outs, *refs[2 * n:])

    return pl.pallas_call(
        body, name="weight_all_gather", in_specs=[ANY] * n, out_specs=[ANY] * n,
        out_shape=[SDS((s.shape[0], N_CHIPS) + s.shape[1:], s.dtype) for s in shards],
        scratch_shapes=[pltpu.SemaphoreType.DMA((nsem,)), pltpu.SemaphoreType.DMA((nsem,))])(*shards)


def _call(body, operands, *, name, grid, in_specs, out_specs, out_shape, sem, scratch=(), gather=None):
    if gather is None:
        return pl.pallas_call(body, grid=grid, in_specs=in_specs, out_specs=out_specs, out_shape=out_shape,
                              scratch_shapes=list(scratch), name=name, compiler_params=_cparams(sem))(*operands), None
    start, finish, c_ins, c_io, c_new, nsem = gather
    n_in, n_out, n_scr = len(operands), len(out_shape), len(scratch)
    ni, nio, nco = len(c_ins), len(c_io), len(c_io) + len(c_new)

    def full(*refs):
        ins, sh = refs[:n_in], refs[n_in:n_in + ni]
        outs = refs[n_in + ni + nio:n_in + ni + nio + n_out]
        co = refs[n_in + ni + nio + n_out:n_in + ni + nio + n_out + nco]
        scr = refs[n_in + ni + nio + n_out + nco:]
        ids = [pl.program_id(a) for a in range(len(grid))]
        first = functools.reduce(jnp.logical_and, [i == 0 for i in ids])
        last = functools.reduce(jnp.logical_and, [i == g - 1 for i, g in zip(ids, grid)])

        @pl.when(first)
        def _():
            start(sh, co, scr[n_scr], scr[n_scr + 1])

        body(*ins, *outs, *scr[:n_scr])

        @pl.when(last)
        def _():
            finish(sh, co, scr[n_scr], scr[n_scr + 1])

    res = pl.pallas_call(
        full, grid=grid, in_specs=list(in_specs) + [ANY] * (ni + nio), out_specs=list(out_specs) + [ANY] * nco,
        out_shape=list(out_shape) + [SDS(g.shape, g.dtype) for g in list(c_io) + list(c_new)],
        scratch_shapes=list(scratch) + [pltpu.SemaphoreType.DMA((nsem,)), pltpu.SemaphoreType.DMA((nsem,))],
        input_output_aliases={n_in + ni + t: n_out + t for t in range(nio)}, name=name,
        compiler_params=_cparams(("arbitrary",) * len(grid)))(*operands, *c_ins, *c_io)
    return res[:n_out], res[n_out:]


def _pair_exchange(gs):
    n = len(gs)

    def body(*refs):
        _pair_copies(refs[:n], refs[n:2 * n], *refs[2 * n:], start=True)
        _pair_copies(refs[:n], refs[n:2 * n], *refs[2 * n:], start=False)

    return pl.pallas_call(
        body, name="grad_pair_exchange", in_specs=[ANY] * n, out_specs=[ANY] * n,
        out_shape=_pair_shapes(gs),
        scratch_shapes=[pltpu.SemaphoreType.DMA((n,)), pltpu.SemaphoreType.DMA((n,))])(*gs)


def _pair_shapes(gs):
    return [SDS(g.shape[:2] + (g.shape[2] // 2, g.shape[3]), f32) for g in gs]


def _pair_copies(ins, outs, send, recv, start):
    x, y, c, _ = _mesh_pos()
    for t in range(len(ins)):
        half = ins[t].shape[2] // 2
        src = ins[t].at[:, :, pl.ds(pl.multiple_of((1 - c) * half, SUBLANES), half)]
        cp = pltpu.make_async_remote_copy(src_ref=src, dst_ref=outs[t], send_sem=send.at[t], recv_sem=recv.at[t],
                                          device_id=(x, y, 1 - c), device_id_type=MESH)
        cp.start() if start else cp.wait()


def _pair_carry(gs):
    return (functools.partial(_pair_copies, start=True), functools.partial(_pair_copies, start=False),
            gs, [], _pair_shapes(gs), len(gs))


def _pair_sum(name, g, got, c):
    l, s, r, cols = g.shape

    def body(c_ref, a_ref, b_ref, o_ref):
        o_ref[...] = (a_ref[...] + b_ref[...]).astype(bf16)

    blk = (None, None, r // 2, cols)
    return pl.pallas_call(
        body, name=name, out_shape=SDS(got.shape, bf16),
        grid_spec=pltpu.PrefetchScalarGridSpec(
            num_scalar_prefetch=1, grid=(l, s),
            in_specs=[pl.BlockSpec(blk, lambda i, q, c_ref: (i, q, c_ref[0], 0)),
                      pl.BlockSpec(blk, lambda i, q, c_ref: (i, q, 0, 0))],
            out_specs=pl.BlockSpec(blk, lambda i, q, c_ref: (i, q, 0, 0))),
        compiler_params=_cparams(("parallel", "parallel")))(c, g, got)


def _chip_exchange(hs):
    n = len(hs)

    def body(*refs):
        _chip_copies(refs[:n], refs[n:2 * n], *refs[2 * n:], start=True)
        _chip_copies(refs[:n], refs[n:2 * n], *refs[2 * n:], start=False)

    return pl.pallas_call(
        body, name="grad_chip_exchange", in_specs=[ANY] * n, out_specs=[ANY] * n,
        out_shape=[SDS(h.shape, h.dtype) for h in hs],
        scratch_shapes=[pltpu.SemaphoreType.DMA((3 * n,)), pltpu.SemaphoreType.DMA((3 * n,))])(*hs)


def _chip_copies(ins, outs, send, recv, start):
    x, y, c, chips = _mesh_pos()
    q = 2 * x + y
    for t in range(len(ins)):
        for j, (px, py) in enumerate(chips):
            cp = pltpu.make_async_remote_copy(
                src_ref=ins[t].at[:, 2 * px + py], dst_ref=outs[t].at[:, q], send_sem=send.at[3 * t + j],
                recv_sem=recv.at[3 * t + j], device_id=(px, py, c), device_id_type=MESH)
            cp.start() if start else cp.wait()


def _chip_carry(hs):
    return (functools.partial(_chip_copies, start=True), functools.partial(_chip_copies, start=False),
            hs, [], [SDS(h.shape, h.dtype) for h in hs], 3 * len(hs))


def _chip_sum(name, s, h, pos, l0, layers, into):
    l, _, r, cols = s.shape

    def body(pos_ref, s0, s1, s2, s3, own_ref, *rest):
        vals = [jnp.where(pos_ref[0] == p, own_ref[...], ref[...]).astype(f32) for p, ref in enumerate((s0, s1, s2, s3))]
        rest[-1][...] = ((vals[0] + vals[1]) + vals[2]) + vals[3]

    blk = (None, None, r, cols)
    slot = lambda p: pl.BlockSpec(blk, lambda i, pos_ref: (i, jnp.where(pos_ref[0] == p, (p + 1) % N_CHIPS, p), 0, 0))
    extra, alias = ([], {}) if into is None else ([into], {6: 0})
    return pl.pallas_call(
        body, name=name, out_shape=SDS((layers, 2 * r, cols), f32), input_output_aliases=alias,
        grid_spec=pltpu.PrefetchScalarGridSpec(
            num_scalar_prefetch=1, grid=(l,),
            in_specs=[slot(p) for p in range(N_CHIPS)] + [pl.BlockSpec(blk, lambda i, pos_ref: (i, pos_ref[0], 0, 0))]
            + [ANY] * len(extra),
            out_specs=pl.BlockSpec((None, r, cols), lambda i, pos_ref: (l0 + i, pos_ref[1], 0))),
        compiler_params=_cparams(("parallel",)))(pos, s, s, s, s, h, *extra)


def _pair_gather(fulls):
    n = len(fulls)

    def body(*refs):
        ins, outs = refs[:n], refs[n:2 * n]
        send, recv = refs[2 * n:]
        x, y, c, _ = _mesh_pos()
        copies = []
        for t in range(n):
            half = outs[t].shape[1] // 2
            rows = outs[t].at[:, pl.ds(pl.multiple_of(c * half, SUBLANES), half)]
            copies.append(pltpu.make_async_remote_copy(
                src_ref=rows, dst_ref=rows, send_sem=send.at[t], recv_sem=recv.at[t],
                device_id=(x, y, 1 - c), device_id_type=MESH))
        _run_copies(copies)

    return pl.pallas_call(
        body, name="grad_pair_gather", in_specs=[ANY] * n, out_specs=[ANY] * n,
        out_shape=[SDS(f.shape, f32) for f in fulls], input_output_aliases={t: t for t in range(n)},
        scratch_shapes=[pltpu.SemaphoreType.DMA((n,)), pltpu.SemaphoreType.DMA((n,))])(*fulls)


COL_SHARDED = ("attn_w_in", "rg_w_in", "ffn_w_gate", "ffn_w_up")
ROW_SHARDED = ("attn_w_out", "rg_w_out")
GATES = ("rg_w_a", "rg_w_i")
VECTORS = ("rg_conv_w", "rg_conv_b", "rg_b_a", "rg_b_i", "rg_lambda")
REPLICATED = ("norm_mix_pre", "norm_mix_post", "norm_ffn_pre", "norm_ffn_post", "attn_rel_bias")
BIG_GRADS = COL_SHARDED + ROW_SHARDED + ("ffn_w_down",)
SMALL_GRADS = GATES + VECTORS + REPLICATED
WEIGHTS =("attn_w_in", "attn_rel_bias", "attn_w_out", "rg_w_in", "rg_conv_w", "rg_conv_b", "rg_w_a", "rg_b_a",
           "rg_w_i", "rg_b_i", "rg_lambda", "rg_w_out", "norm_mix_pre", "norm_mix_post", "norm_ffn_pre",
           "norm_ffn_post", "ffn_w_gate", "ffn_w_up", "ffn_w_down")
SMALL = VECTORS + REPLICATED


GATHER_PARTS = {
    "first": (("attn_w_in", 0, 1), ("attn_w_out", 0, 1), ("rg_w_a", 0, 8), ("rg_w_i", 0, 8), ("vec", 0, 1)),
    "chunk_attn_fwd": (("ffn_w_gate", 0, 1), ("ffn_w_up", 0, 1), ("ffn_w_down", 0, 1), ("rg_w_in", 0, 1),
                       ("rg_w_out", 0, 1)),
    "sb_attn_fwd": (("ffn_w_gate", 1, 3), ("ffn_w_up", 1, 3), ("ffn_w_down", 1, 3)),
    "ffn_up": (("rg_w_in", 1, 1), ("rg_w_out", 1, 1), ("attn_w_in", 1, 1), ("attn_w_out", 1, 1)),
}


TRANSPOSED = ("ffn_w_gate", "ffn_w_up")


def _natural(name, a):
    return jnp.swapaxes(a, 1, 2) if name in TRANSPOSED else a


class _WeightGather:
    def __init__(self, w):
        self.w = w
        self.names = list(COL_SHARDED + ROW_SHARDED + GATES + ("ffn_w_down", "vec"))
        self.shards = {}
        for k in self.names[:-1]:
            a = _natural(k, w[k]).astype(bf16)
            self.shards[k] = a.reshape((-1,) + a.shape[-2:])
        self.shards["vec"] = jnp.concatenate([w[k].reshape(-1) for k in VECTORS]).reshape(1, -1, LANES)
        got = _gather_call(self._items("first", self.names), [self.shards[k] for k in self.names])
        self.raw = dict(zip(self.names, got))

    @staticmethod
    def _items(part, names):
        return [(names.index(k), l0, nl) for k, l0, nl in GATHER_PARTS[part]]

    def part(self, part):
        names = list(dict.fromkeys(k for k, _, _ in GATHER_PARTS[part]))
        items = self._items(part, names)
        return (functools.partial(_gather_start, items), functools.partial(_gather_finish, items),
                [self.shards[k] for k in names], [self.raw[k] for k in names], [], GATHER_SEMS * len(items)), names

    def views(self):
        got, w = self.raw, self.w
        out = {k: w[k] for k in REPLICATED}
        for k in COL_SHARDED + ("ffn_w_down",):
            out[k] = got[k]
        for k in ROW_SHARDED:
            l, s, ks, n = got[k].shape
            out[k] = got[k].reshape(l, 1, s * ks, n)
        for k in GATES:
            out[k] = got[k].reshape(2, LRU_BLOCKS, LRU_BW, LRU_BW)
        vec = got["vec"].reshape(N_CHIPS, -1)
        off = 0
        for k in VECTORS:
            shp = w[k].shape
            n = int(np.prod(shp))
            piece = vec[:, off:off + n].reshape((N_CHIPS,) + shp)
            off += n
            if k == "rg_conv_w":
                out[k] = piece.reshape(N_CHIPS, 2, 4, 256).transpose(1, 2, 0, 3).reshape(2, 4, D_MODEL)
            elif k in ("rg_b_a", "rg_b_i"):
                out[k] = piece.transpose(1, 2, 0, 3).reshape(2, 1, D_MODEL)
            else:
                out[k] = piece.transpose(1, 0, 2).reshape(2, 1, D_MODEL)
        return out


def _carried(plan, part, wts, fn, *args):
    if plan is None:
        return fn(*args, None)[0]
    gather, names = plan.part(part)
    out, new = fn(*args, gather)
    plan.raw.update(zip(names, new))
    wts.update(plan.views())
    return out


def _grad_blocks(name, g):
    st = jnp.stack([g[i] for i in sorted(g)])
    if name in GATES:
        st = st.reshape(2, LRU_BLOCKS, N_CHIPS, LRU_BW // N_CHIPS, LRU_BW).transpose(2, 0, 1, 3, 4)
    elif name == "rg_conv_w":
        st = st.reshape(2, 4, N_CHIPS, -1).transpose(2, 0, 1, 3)
    elif name in ("rg_b_a", "rg_b_i"):
        st = st.reshape(2, LRU_BLOCKS, N_CHIPS, -1).transpose(2, 0, 1, 3)
    elif name in VECTORS:
        st = st.reshape(2, N_CHIPS, -1).transpose(1, 0, 2)
    else:
        st = jnp.broadcast_to(st.reshape(1, -1), (N_CHIPS, st.size))
    return st.reshape(N_CHIPS, -1)


class _GradExchange:
    def __init__(self):
        self.c = lax.axis_index("c").astype(jnp.int32).reshape(1)
        self.pos = jnp.stack([2 * lax.axis_index("x") + lax.axis_index("y"), lax.axis_index("c")]).astype(jnp.int32)
        self.up = self.got_up = self.parts_up = self.slots_up = None

    @staticmethod
    def _blocked(g):
        if g.ndim == 3:
            g = g.reshape(g.shape[0], N_CHIPS, g.shape[1] // N_CHIPS, g.shape[2])
        return g

    def _sums(self, tag, names, gs, got):
        return [_pair_sum("grad_pair_sum_" + tag + k, g, r, self.c) for k, g, r in zip(names, gs, got)]

    def upper_carry(self, grads):
        self.up = [self._blocked(grads[k][1]) for k in BIG_GRADS]
        return _pair_carry(self.up)

    def upper_got(self, got):
        self.got_up = got

    def carry(self):
        self.parts_up = self._sums("up_", BIG_GRADS, self.up, self.got_up)
        return _chip_carry(self.parts_up)

    def carried(self, slots):
        self.slots_up = slots

    def finish(self, grads, shard_shapes):
        if self.got_up is None:
            self.upper_carry(grads)
            self.got_up = _pair_exchange(self.up)
        if self.slots_up is None:
            self.carry()
            self.slots_up = _chip_exchange(self.parts_up)
        blocks = [_grad_blocks(k, grads[k]) for k in SMALL_GRADS]
        used = sum(b.shape[1] for b in blocks)
        small = jnp.concatenate(blocks + [jnp.zeros((N_CHIPS, SMALL_ROWS * PACK_COLS - used), f32)], axis=1)
        names = tuple(k for k in BIG_GRADS if LOWER_LAYERS[k]) + ("small",)
        gs = [self._blocked(grads[k][0]) for k in names[:-1]] + [small.reshape(1, N_CHIPS, SMALL_ROWS, PACK_COLS)]
        parts = dict(zip(names, self._sums("lo_", names, gs, _pair_exchange(gs))))
        slots = dict(zip(names, _chip_exchange([parts[k] for k in names])))
        fulls = []
        for i, k in enumerate(BIG_GRADS):
            nlo, nup = LOWER_LAYERS[k], self.parts_up[i].shape[0]
            full = _chip_sum("grad_chip_sum_up_" + k, self.slots_up[i], self.parts_up[i], self.pos, nlo, nlo + nup, None)
            if nlo:
                full = _chip_sum("grad_chip_sum_lo_" + k, slots[k], parts[k], self.pos, 0, nlo + nup, full)
            fulls.append(full)
        fulls.append(_chip_sum("grad_chip_sum_lo_small", slots["small"], parts["small"], self.pos, 0, 1, None))
        full = _pair_gather(fulls)
        out = {k: f.reshape(shard_shapes[k]) for k, f in zip(BIG_GRADS, full)}
        flat, off = full[-1].reshape(-1), 0
        for k in SMALL_GRADS:
            n = int(np.prod(shard_shapes[k]))
            out[k] = flat[off:off + n].reshape(shard_shapes[k])
            off += n
        return out


def _adamw_fn(w, g, m, v):
    m = ADAM_B1 * m + (1.0 - ADAM_B1) * g
    v = ADAM_B2 * v + (1.0 - ADAM_B2) * (g * g)
    m_hat = m / (1.0 - ADAM_B1 ** ADAM_STEP)
    v_hat = v / (1.0 - ADAM_B2 ** ADAM_STEP)
    return -ADAM_LR * (m_hat / (jnp.sqrt(v_hat) + ADAM_EPS) + ADAM_WD * w), m, v


def _adamw(name, w, g, m, v):
    shp = w.shape
    if w.size >= 1 << 16:
        width = shp[-1]
        ops = [a.reshape(-1, width) for a in (w, g, m, v)]
        res = _rows(name, _adamw_fn, ops, [], [(width, f32)] * 3)
        return [r.reshape(shp) for r in res]
    n = w.size
    rows = -(-n // (SUBLANES * LANES)) * SUBLANES
    ops = [jnp.pad(a.reshape(-1), (0, rows * LANES - n)).reshape(rows, LANES) for a in (w, g, m, v)]
    res = _rows(name, _adamw_fn, ops, [], [(LANES, f32)] * 3, tr=rows)
    return [r.reshape(-1)[:n].reshape(shp) for r in res]


def kernel(x, attn_w_in, attn_rel_bias, attn_w_out, rg_w_in, rg_conv_w, rg_conv_b, rg_w_a, rg_b_a, rg_w_i, rg_b_i, rg_lambda, rg_w_out, norm_mix_pre, norm_mix_post, norm_ffn_pre, norm_ffn_post, ffn_w_gate, ffn_w_up, ffn_w_down, loss_target, m_attn_w_in, m_attn_rel_bias, m_attn_w_out, m_rg_w_in, m_rg_conv_w, m_rg_conv_b, m_rg_w_a, m_rg_b_a, m_rg_w_i, m_rg_b_i, m_rg_lambda, m_rg_w_out, m_norm_mix_pre, m_norm_mix_post, m_norm_ffn_pre, m_norm_ffn_post, m_ffn_w_gate, m_ffn_w_up, m_ffn_w_down, v_attn_w_in, v_attn_rel_bias, v_attn_w_out, v_rg_w_in, v_rg_conv_w, v_rg_conv_b, v_rg_w_a, v_rg_b_a, v_rg_w_i, v_rg_b_i, v_rg_lambda, v_rg_w_out, v_norm_mix_pre, v_norm_mix_post, v_norm_ffn_pre, v_norm_ffn_post, v_ffn_w_gate, v_ffn_w_up, v_ffn_w_down):
    w = dict(zip(WEIGHTS, (attn_w_in, attn_rel_bias, attn_w_out, rg_w_in, rg_conv_w, rg_conv_b, rg_w_a, rg_b_a, rg_w_i,
                           rg_b_i, rg_lambda, rg_w_out, norm_mix_pre, norm_mix_post, norm_ffn_pre, norm_ffn_post,
                           ffn_w_gate, ffn_w_up, ffn_w_down)))
    m = dict(zip(WEIGHTS, (m_attn_w_in, m_attn_rel_bias, m_attn_w_out, m_rg_w_in, m_rg_conv_w, m_rg_conv_b, m_rg_w_a,
                           m_rg_b_a, m_rg_w_i, m_rg_b_i, m_rg_lambda, m_rg_w_out, m_norm_mix_pre, m_norm_mix_post,
                           m_norm_ffn_pre, m_norm_ffn_post, m_ffn_w_gate, m_ffn_w_up, m_ffn_w_down)))
    v = dict(zip(WEIGHTS, (v_attn_w_in, v_attn_rel_bias, v_attn_w_out, v_rg_w_in, v_rg_conv_w, v_rg_conv_b, v_rg_w_a,
                           v_rg_b_a, v_rg_w_i, v_rg_b_i, v_rg_lambda, v_rg_w_out, v_norm_mix_pre, v_norm_mix_post,
                           v_norm_ffn_pre, v_norm_ffn_post, v_ffn_w_gate, v_ffn_w_up, v_ffn_w_down)))
    plan = _WeightGather(w)
    exch = _GradExchange()
    loss, dx, grads = _local_step(x[0], loss_target[0], plan.views(), plan, exch)
    loss = lax.psum(loss, ("x", "y", "c"))
    g = exch.finish(grads, {k: _natural(k, w[k]).shape for k in WEIGHTS})

    big = [k for k in WEIGHTS if k not in SMALL]
    upd = {}
    for k in big:
        res = _adamw("adamw_" + k, _natural(k, w[k]), g[k], _natural(k, m[k]), _natural(k, v[k]))
        upd[k] = [_natural(k, r) for r in res]
        g[k] = _natural(k, g[k])
    cat = lambda d: jnp.concatenate([d[k].reshape(-1) for k in SMALL])
    small = _adamw("adamw_small", cat(w), cat(g), cat(m), cat(v))
    off = 0
    for k in SMALL:
        n = w[k].size
        upd[k] = [r[off:off + n].reshape(w[k].shape) for r in small]
        off += n
    return (loss, dx[None], *[g[k] for k in WEIGHTS], *[upd[k][0] for k in WEIGHTS],
            *[upd[k][1] for k in WEIGHTS], *[upd[k][2] for k in WEIGHTS])
```

```python
import functools

import numpy as np
import jax
import jax.numpy as jnp
from jax import lax
from jax.experimental import pallas as pl
from jax.experimental.pallas import tpu as pltpu

f32 = jnp.float32
bf16 = jnp.bfloat16
SDS = jax.ShapeDtypeStruct
MESH = pl.DeviceIdType.MESH

D_MODEL = 1024
N_CHIPS = 4
DEPTH = 4
HEAD_DIM = 64
CHUNK = 64
N_LEFT = 8
REL_CLIP = 256
A_W = 512
LRU_BLOCKS = 4
LRU_BW = 256
LRU_C = 8.0
D_FF = 2816
RMS_EPS = 1e-6
LANES = 128
SUBLANES = 8
VMEM_LIMIT = 56 * 1024 * 1024

QB_A = 2 * CHUNK
QSUB_A = 8
KW_A = QB_A + N_LEFT * CHUNK
PAD_A = N_LEFT * CHUNK
EXT_A = 768
SB_BLK = 256
QSUB_B = 4
SB_DEAD = -110.0

ADAM_LR, ADAM_B1, ADAM_B2, ADAM_EPS, ADAM_WD, ADAM_STEP = 0.001, 0.9, 0.999, 1e-08, 0.01, 10


def _cparams(sem):
    return pltpu.CompilerParams(dimension_semantics=sem, vmem_limit_bytes=VMEM_LIMIT)


def _gemm(name, operands, in_specs, o_spec, out_shape, grid, dims, acc_shape, into=None):
    nred = grid[2]
    npair = len(operands) // 2
    nin = 2 * npair + (into is not None)

    def body(*refs):
        o_ref = refs[nin]
        p = None
        for t in range(npair):
            d = lax.dot_general(refs[2 * t][...], refs[2 * t + 1][...], (dims, ((), ())),
                                preferred_element_type=f32)
            p = d if p is None else p + d
        if nred == 1:
            o_ref[...] = p.astype(o_ref.dtype)
        else:
            acc = refs[nin + 1]
            r = pl.program_id(2)

            @pl.when(r == 0)
            def _():
                acc[...] = p

            @pl.when(r > 0)
            def _():
                acc[...] += p

            @pl.when(r == nred - 1)
            def _():
                o_ref[...] = acc[...].astype(o_ref.dtype)

    scratch = [] if nred == 1 else [pltpu.VMEM(acc_shape, f32)]
    extra, alias = ([], {}) if into is None else ([into], {2 * npair: 0})
    return pl.pallas_call(
        body, grid=grid, in_specs=list(in_specs) + [pl.BlockSpec(memory_space=pl.ANY)] * len(extra),
        out_specs=o_spec, out_shape=out_shape, scratch_shapes=scratch, name=name, input_output_aliases=alias,
        compiler_params=_cparams(("parallel", "parallel", "arbitrary")))(*operands, *extra)


LOWER_LAYERS = {"attn_w_in": 1, "attn_w_out": 1, "rg_w_in": 0, "rg_w_out": 0,
                "ffn_w_gate": 0, "ffn_w_up": 0, "ffn_w_down": 0}


def _grad_slot(name, l):
    n = LOWER_LAYERS[name]
    return (0, l) if l < n else (1, l - n)


class _Fresh:
    def __init__(self, shape):
        self.shape = tuple(shape)


def _into(buf):
    return None if isinstance(buf, _Fresh) else buf


NN = ((1,), (0,))
NT = ((1,), (1,))
TN = ((0,), (0,))


WGRAD_TOKENS = 2048


def _tile(t, want=1024):
    return min(want, t)


def _mm_cols(name, a, w, l, out_dtype):
    t, k = a.shape
    _, s, _, ns = w.shape
    tm = _tile(t, 2048)
    return _gemm(
        name, [a, w],
        [pl.BlockSpec((tm, k), lambda i, j, r: (i, 0)),
         pl.BlockSpec((None, None, k, ns), lambda i, j, r: (l, j, 0, 0))],
        pl.BlockSpec((tm, ns), lambda i, j, r: (i, j)),
        SDS((t, s * ns), out_dtype), (t // tm, s, 1), NN, None)


def _mm_cols_t(name, dy, w, l, out_dtype):
    t = dy.shape[0]
    _, s, k, ns = w.shape
    tm = _tile(t)
    ops, specs = [], []
    for r in range(s):
        ops += [dy, w]
        specs += [pl.BlockSpec((tm, ns), lambda i, j, kk, r=r: (i, r)),
                  pl.BlockSpec((None, None, k, ns), lambda i, j, kk, r=r: (l, r, 0, 0))]
    return _gemm(name, ops, specs, pl.BlockSpec((tm, k), lambda i, j, kk: (i, 0)),
                 SDS((t, k), out_dtype), (t // tm, 1, 1), NT, None)


def _mm_wgrad_cols(name, a, dy, buf, l):
    t, k = a.shape
    _, s, _, ns = buf.shape
    tt = _tile(t, 2 * WGRAD_TOKENS)
    return _gemm(
        name, [a, dy],
        [pl.BlockSpec((tt, k), lambda i, j, r: (r, 0)),
         pl.BlockSpec((tt, ns), lambda i, j, r: (r, i))],
        pl.BlockSpec((None, None, k, ns), lambda i, j, r: (l, i, 0, 0)),
        SDS(buf.shape, f32), (s, 1, t // tt), TN, (k, ns), into=_into(buf))


def _mm_rows(name, parts, w, l, out_dtype):
    t = parts[0].shape[0]
    n = w.shape[3]
    tm = _tile(t, 2048)
    ops, specs = [], []
    for p_i, a in enumerate(parts):
        kp = a.shape[1]
        ops += [a, w]
        specs += [pl.BlockSpec((tm, kp), lambda i, j, r: (i, 0)),
                  pl.BlockSpec((None, None, kp, n), lambda i, j, r, p_i=p_i: (l, 0, p_i, 0))]
    return _gemm(name, ops, specs, pl.BlockSpec((tm, n), lambda i, j, r: (i, 0)),
                 SDS((t, n), out_dtype), (t // tm, 1, 1), NN, None)


def _mm_rows_t(name, dy, w, l, out_dtype):
    t, n = dy.shape
    k = w.shape[2]
    tm = _tile(t, 2048)
    return _gemm(
        name, [dy, w],
        [pl.BlockSpec((tm, n), lambda i, j, r: (i, 0)),
         pl.BlockSpec((None, None, k, n), lambda i, j, r: (l, 0, 0, 0))],
        pl.BlockSpec((tm, k), lambda i, j, r: (i, 0)),
        SDS((t, k), out_dtype), (t // tm, 1, 1), NT, None)


def _mm_wgrad(name, a, dy, buf, l, part=0):
    t, k = a.shape
    n = dy.shape[1]
    tt = _tile(t, 2 * WGRAD_TOKENS)
    return _gemm(
        name, [a, dy],
        [pl.BlockSpec((tt, k), lambda i, j, r: (r, 0)),
         pl.BlockSpec((tt, n), lambda i, j, r: (r, 0))],
        pl.BlockSpec((None, k, n), lambda i, j, r: (l, part, 0)),
        SDS(buf.shape, f32), (1, 1, t // tt), TN, (k, n), into=_into(buf))


def _ffn_up(h, wg, wu, l, gather):
    t, k = h.shape
    s, fs = wg.shape[1], wg.shape[2]
    tm = _tile(t)

    def body(h_ref, wg_ref, wu_ref, g_ref, u_ref, hid_ref):
        hv = h_ref[...]
        g = lax.dot_general(hv, wg_ref[...], (NT, ((), ())), preferred_element_type=f32)
        u = lax.dot_general(hv, wu_ref[...], (NT, ((), ())), preferred_element_type=f32)
        g_ref[...] = g.astype(bf16)
        u_ref[...] = u.astype(bf16)
        hid_ref[...] = (g * jax.nn.sigmoid(g) * u).astype(bf16)

    wspec = pl.BlockSpec((None, None, fs, k), lambda j, i: (l, j, 0, 0))
    ospec = pl.BlockSpec((None, tm, fs), lambda j, i: (j, i, 0))
    return _call(
        body, [h, wg, wu], grid=(s, t // tm), name="ffn_up",
        in_specs=[pl.BlockSpec((tm, k), lambda j, i: (i, 0)), wspec, wspec],
        out_specs=[ospec, ospec, ospec], out_shape=[SDS((s, t, fs), bf16)] * 3,
        sem=("parallel", "parallel"), gather=gather)


def _ffn_down(hid, wd, l):
    s, t, fs = hid.shape
    n = wd.shape[3]
    tm = _tile(t)
    ops, specs = [], []
    for r in range(s):
        ops += [hid, wd]
        specs += [pl.BlockSpec((None, tm, fs), lambda i, j, k, r=r: (r, i, 0)),
                  pl.BlockSpec((None, None, fs, n), lambda i, j, k, r=r: (l, r, 0, 0))]
    return _gemm("ffn_down", ops, specs, pl.BlockSpec((tm, n), lambda i, j, k: (i, 0)),
                 SDS((t, n), f32), (t // tm, 1, 1), NN, None)


def _ffn_down_bwd(df, wd, l, g, u):
    t, n = df.shape
    s, fs = wd.shape[1], wd.shape[2]
    tm = _tile(t)

    def body(df_ref, wd_ref, g_ref, u_ref, dg_ref, du_ref):
        dh = lax.dot_general(df_ref[...], wd_ref[...], (NT, ((), ())), preferred_element_type=f32)
        gv = g_ref[...].astype(f32)
        uv = u_ref[...].astype(f32)
        sg = jax.nn.sigmoid(gv)
        du_ref[...] = (dh * gv * sg).astype(bf16)
        dg_ref[...] = (dh * uv * (sg * (1.0 + gv * (1.0 - sg)))).astype(bf16)

    bspec = pl.BlockSpec((None, tm, fs), lambda j, i: (j, i, 0))
    return pl.pallas_call(
        body, grid=(s, t // tm), name="ffn_down_bwd",
        in_specs=[pl.BlockSpec((tm, n), lambda j, i: (i, 0)),
                  pl.BlockSpec((None, None, fs, n), lambda j, i: (l, j, 0, 0)), bspec, bspec],
        out_specs=[bspec, bspec], out_shape=[SDS((s, t, fs), bf16)] * 2,
        compiler_params=_cparams(("parallel", "parallel")))(df, wd, g, u)


def _ffn_up_bwd(dg, du, wg, wu, l):
    s, t, fs = dg.shape
    k = wg.shape[3]
    tm = _tile(t, 512)
    ops, specs = [], []
    for r in range(s):
        aspec = pl.BlockSpec((None, tm, fs), lambda i, j, kk, r=r: (r, i, 0))
        wspec = pl.BlockSpec((None, None, fs, k), lambda i, j, kk, r=r: (l, r, 0, 0))
        ops += [dg, wg, du, wu]
        specs += [aspec, wspec, aspec, wspec]
    return _gemm("ffn_up_bwd", ops, specs, pl.BlockSpec((tm, k), lambda i, j, kk: (i, 0)),
                 SDS((t, k), f32), (t // tm, 1, 1), NN, None)


def _ffn_wgrad_up(h, dg, du, buf_g, buf_u, l):
    t, k = h.shape
    s, _, fs = dg.shape
    tt = _tile(t, WGRAD_TOKENS)
    nred = t // tt

    fresh = isinstance(buf_g, _Fresh)

    def body(*refs):
        h_ref, dg_ref, du_ref = refs[:3]
        og_ref, ou_ref, acc_g, acc_u = refs[-4:]
        r = pl.program_id(1)
        hv = h_ref[...]
        pg = lax.dot_general(dg_ref[...], hv, (TN, ((), ())), preferred_element_type=f32)
        pu = lax.dot_general(du_ref[...], hv, (TN, ((), ())), preferred_element_type=f32)

        @pl.when(r == 0)
        def _():
            acc_g[...] = pg
            acc_u[...] = pu

        @pl.when(r > 0)
        def _():
            acc_g[...] += pg
            acc_u[...] += pu

        @pl.when(r == nred - 1)
        def _():
            og_ref[...] = acc_g[...]
            ou_ref[...] = acc_u[...]

    dspec = pl.BlockSpec((None, tt, fs), lambda i, r: (i, r, 0))
    ospec = pl.BlockSpec((None, None, fs, k), lambda i, r: (l, i, 0, 0))
    extra, alias = ([], {}) if fresh else ([buf_g, buf_u], {3: 0, 4: 1})
    return pl.pallas_call(
        body, grid=(s, nred), name="ffn_wgrad_up",
        in_specs=[pl.BlockSpec((tt, k), lambda i, r: (r, 0)), dspec, dspec] + [ANY] * len(extra),
        out_specs=[ospec, ospec], out_shape=[SDS(buf_g.shape, f32), SDS(buf_u.shape, f32)],
        scratch_shapes=[pltpu.VMEM((fs, k), f32)] * 2, input_output_aliases=alias,
        compiler_params=_cparams(("parallel", "arbitrary")))(h, dg, du, *extra)


def _ffn_wgrad_down(hid, df, buf, l):
    s, t, fs = hid.shape
    n = df.shape[1]
    tt = _tile(t, 2 * WGRAD_TOKENS)
    return _gemm(
        "ffn_wgrad_down", [hid, df],
        [pl.BlockSpec((None, tt, fs), lambda i, j, r: (i, r, 0)),
         pl.BlockSpec((tt, n), lambda i, j, r: (r, 0))],
        pl.BlockSpec((None, None, fs, n), lambda i, j, r: (l, i, 0, 0)),
        SDS(buf.shape, f32), (s, 1, t // tt), TN, (fs, n), into=_into(buf))


def _rows(name, fn, rows, consts, row_outs, acc_outs=(), tr=512):
    rows = [r if isinstance(r, tuple) else (r, r.shape[1], 0) for r in rows]
    t = rows[0][0].shape[0]
    tr = max(d for d in range(SUBLANES, min(tr, t) + 1, SUBLANES) if t % d == 0)
    nin = len(rows) + len(consts)
    no, na = len(row_outs), len(acc_outs)

    def body(*refs):
        vals = fn(*[r[...] for r in refs[:nin]])
        if not isinstance(vals, (tuple, list)):
            vals = (vals,)
        for k in range(no):
            refs[nin + k][...] = vals[k].astype(refs[nin + k].dtype)
        first = pl.program_id(0) == 0
        for k in range(na):
            ref, val = refs[nin + no + k], vals[no + k]

            @pl.when(first)
            def _(ref=ref, val=val):
                ref[...] = val

            @pl.when(jnp.logical_not(first))
            def _(ref=ref, val=val):
                ref[...] += val

    in_specs = [pl.BlockSpec((tr, w), lambda i, cb=cb: (i, cb)) for (_, w, cb) in rows]
    in_specs += [pl.BlockSpec(c.shape, lambda i, nd=c.ndim: (0,) * nd) for c in consts]
    out_specs = [pl.BlockSpec((tr, w), lambda i: (i, 0)) for (w, _) in row_outs]
    out_specs += [pl.BlockSpec(s, lambda i, nd=len(s): (0,) * nd) for (s, _) in acc_outs]
    out_shape = [SDS((t, w), dt) for (w, dt) in row_outs] + [SDS(s, dt) for (s, dt) in acc_outs]
    res = pl.pallas_call(
        body, grid=(t // tr,), in_specs=in_specs, out_specs=out_specs, out_shape=out_shape,
        name=name, compiler_params=_cparams(("arbitrary",)))(*[r[0] for r in rows], *consts)
    return res


def _rstd(x):
    return lax.rsqrt(jnp.mean(x * x, axis=-1, keepdims=True) + RMS_EPS)


def _norm_fwd(x, g):
    return x * _rstd(x) * g


def _norm_bwd(u, dy, g):
    r = _rstd(u)
    n = u * r
    dn = dy * g
    du = r * (dn - n * jnp.mean(dn * n, axis=-1, keepdims=True))
    return du, jnp.sum(dy * n, axis=0, keepdims=True)


def _gelu(x):
    c = 0.7978845608028654
    return 0.5 * x * (1.0 + jnp.tanh(c * (x + 0.044715 * x * x * x)))


def _gelu_grad(x):
    c = 0.7978845608028654
    th = jnp.tanh(c * (x + 0.044715 * x * x * x))
    return 0.5 * (1.0 + th) + 0.5 * x * (1.0 - th * th) * c * (1.0 + 3.0 * 0.044715 * x * x)


def _mask_heads(x):
    lane = lax.broadcasted_iota(jnp.int32, x.shape, 1)
    return [jnp.where((lane >= h * HEAD_DIM) & (lane < (h + 1) * HEAD_DIM), x, jnp.zeros_like(x))
            for h in range(LANES // HEAD_DIM)]


def _chunk_valid(start):
    qi = lax.broadcasted_iota(jnp.int32, (QB_A, KW_A), 0)
    kj = lax.broadcasted_iota(jnp.int32, (QB_A, KW_A), 1)
    qc = qi // CHUNK
    kc = kj // CHUNK
    return (kc >= qc) & (kc <= qc + N_LEFT) & (kj + start >= PAD_A)


def _scaled(q):
    return q * (HEAD_DIM ** -0.5)


def _chunk_probs(q, k, bias, valid):
    s = lax.dot_general(q, k, (NT, ((), ())), preferred_element_type=f32) + bias
    s = jnp.where(valid, s, -1e30)
    p = jnp.exp(s - jnp.max(s, axis=-1, keepdims=True))
    return p / jnp.sum(p, axis=-1, keepdims=True)


def _chunk_attn_fwd(proj, kpad, vpad, bias, gather):
    t = proj.shape[0]
    tp = kpad.shape[0]
    step = QSUB_A * QB_A

    def body(q_ref, k_ref, v_ref, b_ref, o_ref):
        for sb in range(QSUB_A):
            start = pl.multiple_of((pl.program_id(1) * QSUB_A + sb) * QB_A, QB_A)
            rows = pl.ds(sb * QB_A, QB_A)
            valid = _chunk_valid(start)
            kw = k_ref[pl.ds(start, KW_A), :]
            qm = _mask_heads(_scaled(q_ref[rows, :]))
            vm = _mask_heads(v_ref[pl.ds(start, KW_A), :])
            o = None
            for h in range(len(qm)):
                p = _chunk_probs(qm[h], kw, b_ref[h], valid)
                d = jnp.dot(p.astype(bf16), vm[h], preferred_element_type=f32)
                o = d if o is None else o + d
            o_ref[rows, :] = o.astype(bf16)

    kv_spec = pl.BlockSpec((tp, LANES), lambda hp, qb: (0, hp))
    outs, new = _call(
        body, [proj, kpad, vpad, bias], grid=(A_W // LANES, t // step), name="chunk_attn_fwd",
        in_specs=[pl.BlockSpec((step, LANES), lambda hp, qb: (qb, hp)), kv_spec, kv_spec,
                  pl.BlockSpec((2, QB_A, KW_A), lambda hp, qb: (hp, 0, 0))],
        out_specs=[pl.BlockSpec((step, LANES), lambda hp, qb: (qb, hp))],
        out_shape=[SDS((t, A_W), bf16)], sem=("parallel", "arbitrary"), gather=gather)
    return outs[0], new


def _chunk_attn_bwd(proj, kpad, vpad, bias, dout, gather):
    t = proj.shape[0]
    tp = kpad.shape[0]
    step = QSUB_A * QB_A

    def body(q_ref, k_ref, v_ref, b_ref, do_ref, dq_ref, dk_ref, dv_ref, db_ref):
        qb = pl.program_id(1)

        @pl.when(qb == 0)
        def _():
            dk_ref[...] = jnp.zeros_like(dk_ref)
            dv_ref[...] = jnp.zeros_like(dv_ref)
            db_ref[...] = jnp.zeros_like(db_ref)

        for sb in range(QSUB_A):
            start = pl.multiple_of((qb * QSUB_A + sb) * QB_A, QB_A)
            rows = pl.ds(sb * QB_A, QB_A)
            win = pl.ds(start, KW_A)
            valid = _chunk_valid(start)
            kw = k_ref[win, :]
            vw = v_ref[win, :]
            qm = _mask_heads(_scaled(q_ref[rows, :]))
            dom = _mask_heads(do_ref[rows, :])
            km = _mask_heads(kw)
            dq = dk = dv = None
            for h in range(len(qm)):
                p = _chunk_probs(qm[h], kw, b_ref[h], valid)
                dp = lax.dot_general(dom[h], vw, (NT, ((), ())), preferred_element_type=f32)
                ds = p * (dp - jnp.sum(dp * p, axis=-1, keepdims=True))
                db_ref[h] += ds
                dsb = ds.astype(bf16)
                terms = (jnp.dot(dsb, km[h], preferred_element_type=f32),
                         lax.dot_general(dsb, qm[h], (TN, ((), ())), preferred_element_type=f32),
                         lax.dot_general(p.astype(bf16), dom[h], (TN, ((), ())), preferred_element_type=f32))
                dq, dk, dv = terms if dq is None else (dq + terms[0], dk + terms[1], dv + terms[2])
            dq_ref[rows, :] = _scaled(dq).astype(bf16)
            dk_ref[win, :] += dk
            dv_ref[win, :] += dv

    kv_spec = pl.BlockSpec((tp, LANES), lambda hp, qb: (0, hp))
    q_spec = pl.BlockSpec((step, LANES), lambda hp, qb: (qb, hp))
    b_spec = pl.BlockSpec((2, QB_A, KW_A), lambda hp, qb: (hp, 0, 0))
    return _call(
        body, [proj, kpad, vpad, bias, dout], grid=(A_W // LANES, t // step), name="chunk_attn_bwd",
        in_specs=[q_spec, kv_spec, kv_spec, b_spec, q_spec],
        out_specs=[q_spec, kv_spec, kv_spec, b_spec],
        out_shape=[SDS((t, A_W), bf16), SDS((tp, A_W), f32), SDS((tp, A_W), f32),
                   SDS((2 * A_W // LANES, QB_A, KW_A), f32)],
        sem=("parallel", "arbitrary"), gather=gather)


def _bias_ext(table):
    flat = PAD_A + QB_A - 1 - REL_CLIP
    top = jnp.broadcast_to(table[:, 2 * REL_CLIP:], (table.shape[0], flat))
    lo = 2 * REL_CLIP - (EXT_A - 1 - flat)
    return jnp.concatenate([top, jnp.flip(table[:, lo:], axis=1)], axis=1)


def _bias_window(table):
    nh = table.shape[0]
    e = jnp.broadcast_to(_bias_ext(table)[:, None, :], (nh, QB_A, EXT_A)).reshape(nh, QB_A * EXT_A)
    m = e[:, :QB_A * (EXT_A - 1)].reshape(nh, QB_A, EXT_A - 1)
    return m[:, :, QB_A - 1:]


def _bias_window_grad(dbias):
    nh = dbias.shape[0]
    m = jnp.pad(dbias, ((0, 0), (0, 0), (QB_A - 1, 0))).reshape(nh, QB_A * (EXT_A - 1))
    dext = jnp.sum(jnp.pad(m, ((0, 0), (0, QB_A))).reshape(nh, QB_A, EXT_A), axis=1)
    flat = PAD_A + QB_A - 1 - REL_CLIP
    lo = 2 * REL_CLIP - (EXT_A - 1 - flat)
    tail = jnp.flip(dext[:, flat:], axis=1)
    tail = tail.at[:, -1].add(jnp.sum(dext[:, :flat], axis=1))
    return jnp.pad(tail, ((0, 0), (lo, 0)))


def _tri_suffix(x, tri):
    hi = x.astype(bf16)
    lo = (x - hi.astype(f32)).astype(bf16)
    return jnp.dot(hi, tri, preferred_element_type=f32) + jnp.dot(lo, tri, preferred_element_type=f32)


def _sb_block(q, k, run, tri, causal):
    z = lax.dot_general(q, k, (NT, ((), ())), preferred_element_type=f32)
    e = jnp.exp(-jnp.abs(z))
    l1p = jnp.log(1.0 + e)
    lb = jnp.minimum(z, 0.0) - l1p
    lmb = lb - z
    if causal is not None:
        lmb = jnp.where(causal, lmb, 0.0)
    cs = _tri_suffix(lmb, tri)
    w = jnp.exp(lb + (run + cs - lmb))
    if causal is not None:
        w = jnp.where(causal, w, 0.0)
    return z, e, w, run + cs[:, 0:1]


def _sb_tri():
    r = lax.broadcasted_iota(jnp.int32, (SB_BLK, SB_BLK), 0)
    c = lax.broadcasted_iota(jnp.int32, (SB_BLK, SB_BLK), 1)
    return (r >= c).astype(bf16), c < r


def _sb_live(runs):
    m = runs[0]
    for r in runs[1:]:
        m = jnp.maximum(m, r)
    return jnp.max(m) > SB_DEAD


def _sb_fwd(proj, gather):
    t = proj.shape[0]
    cb = A_W // LANES
    nh = LANES // HEAD_DIM

    step_rows = QSUB_B * SB_BLK

    def body(q_ref, k_ref, v_ref, o_ref, of_ref):
        tri, diag = _sb_tri()
        for sb in range(QSUB_B):
            _sb_fwd_block(pl.program_id(1) * QSUB_B + sb, pl.ds(sb * SB_BLK, SB_BLK), tri, diag,
                          q_ref, k_ref, v_ref, o_ref, of_ref)

    def _sb_fwd_block(qb, qrows, tri, diag, q_ref, k_ref, v_ref, o_ref, of_ref):
        qm = _mask_heads(_scaled(q_ref[qrows, :]))

        def pair(kb, carry, causal):
            rows = pl.ds(pl.multiple_of(kb * SB_BLK, SB_BLK), SB_BLK)
            k = k_ref[rows, :]
            vm = _mask_heads(v_ref[rows, :])
            runs, acc = [], carry[nh]
            for h in range(nh):
                _, _, w, run = _sb_block(qm[h], k, carry[h], tri, causal)
                acc = acc + jnp.dot(w.astype(bf16), vm[h], preferred_element_type=f32)
                runs.append(run)
            return (*runs, acc)

        zero = jnp.zeros((SB_BLK, 1), f32)
        carry = pair(qb, (zero,) * nh + (jnp.zeros((SB_BLK, LANES), f32),), diag)

        def cond(st):
            return (st[0] < qb) & _sb_live(st[1][:nh])

        def step(st):
            return st[0] + 1, pair(qb - 1 - st[0], st[1], None)

        _, carry = lax.while_loop(cond, step, (jnp.int32(0), carry))
        o_ref[qrows, :] = carry[nh].astype(bf16)
        of_ref[qrows, :] = carry[nh]

    ospec = pl.BlockSpec((step_rows, LANES), lambda hp, qb: (qb, hp))
    return _call(
        body, [proj, proj, proj], grid=(cb, t // step_rows), name="sb_attn_fwd",
        in_specs=[pl.BlockSpec((step_rows, LANES), lambda hp, qb: (qb, 3 * cb + hp)),
                  pl.BlockSpec((t, LANES), lambda hp, qb: (0, 4 * cb + hp)),
                  pl.BlockSpec((t, LANES), lambda hp, qb: (0, 5 * cb + hp))],
        out_specs=[ospec, ospec], out_shape=[SDS((t, A_W), bf16), SDS((t, A_W), f32)],
        sem=("parallel", "arbitrary"), gather=gather)


def _sb_bwd(proj, out_b, dout, gather):
    t = proj.shape[0]
    cb = A_W // LANES
    nh = LANES // HEAD_DIM

    step_rows = QSUB_B * SB_BLK

    def body(q_ref, k_ref, v_ref, o_ref, do_ref, dq_ref, dk_ref, dv_ref):
        tri, diag = _sb_tri()

        @pl.when(pl.program_id(1) == 0)
        def _():
            dk_ref[...] = jnp.zeros_like(dk_ref)
            dv_ref[...] = jnp.zeros_like(dv_ref)

        for sb in range(QSUB_B):
            _sb_bwd_block(pl.program_id(1) * QSUB_B + sb, pl.ds(sb * SB_BLK, SB_BLK), tri, diag,
                          q_ref, k_ref, v_ref, o_ref, do_ref, dq_ref, dk_ref, dv_ref)

    def _sb_bwd_block(qb, qrows, tri, diag, q_ref, k_ref, v_ref, o_ref, do_ref, dq_ref, dk_ref, dv_ref):
        qm = _mask_heads(_scaled(q_ref[qrows, :]))
        do = do_ref[qrows, :]
        dom = _mask_heads(do)
        dsums = [jnp.sum(t_, axis=-1, keepdims=True) for t_ in _mask_heads(do.astype(f32) * o_ref[qrows, :])]

        def pair(kb, carry, causal):
            rows = pl.ds(pl.multiple_of(kb * SB_BLK, SB_BLK), SB_BLK)
            k = k_ref[rows, :]
            v = v_ref[rows, :]
            km = _mask_heads(k)
            new, dq, dk, dv = [], carry[2 * nh], None, None
            for h in range(nh):
                z, e, w, run = _sb_block(qm[h], k, carry[2 * h], tri, causal)
                inv = 1.0 / (1.0 + e)
                beta = jnp.where(z >= 0.0, inv, e * inv)
                wb = w.astype(bf16)
                g = lax.dot_general(dom[h], v, (NT, ((), ())), preferred_element_type=f32) * wb.astype(f32)
                sg = _tri_suffix(g, tri)
                dz = g - (g + (dsums[h] - carry[2 * h + 1] - sg)) * beta
                if causal is not None:
                    dz = jnp.where(causal, dz, 0.0)
                dzb = dz.astype(bf16)
                dq = dq + jnp.dot(dzb, km[h], preferred_element_type=f32)
                tk = lax.dot_general(dzb, qm[h], (TN, ((), ())), preferred_element_type=f32)
                tv = lax.dot_general(wb, dom[h], (TN, ((), ())), preferred_element_type=f32)
                dk, dv = (tk, tv) if dk is None else (dk + tk, dv + tv)
                new += [run, carry[2 * h + 1] + sg[:, 0:1]]
            dk_ref[rows, :] += dk
            dv_ref[rows, :] += dv
            return (*new, dq)

        zero = jnp.zeros((SB_BLK, 1), f32)
        carry = pair(qb, (zero,) * (2 * nh) + (jnp.zeros((SB_BLK, LANES), f32),), diag)

        def cond(st):
            return (st[0] < qb) & _sb_live(st[1][0:2 * nh:2])

        def step(st):
            return st[0] + 1, pair(qb - 1 - st[0], st[1], None)

        _, carry = lax.while_loop(cond, step, (jnp.int32(0), carry))
        dq_ref[qrows, :] = _scaled(carry[2 * nh]).astype(bf16)

    kv_in = lambda seg: pl.BlockSpec((t, LANES), lambda hp, qb: (0, seg * cb + hp))
    q_spec = pl.BlockSpec((step_rows, LANES), lambda hp, qb: (qb, hp))
    kv_out = pl.BlockSpec((t, LANES), lambda hp, qb: (0, hp))
    return _call(
        body, [proj, proj, proj, out_b, dout], grid=(cb, t // step_rows), name="sb_attn_bwd",
        in_specs=[pl.BlockSpec((step_rows, LANES), lambda hp, qb: (qb, 3 * cb + hp)), kv_in(4), kv_in(5),
                  q_spec, pl.BlockSpec((step_rows, LANES), lambda hp, qb: (qb, cb + hp))],
        out_specs=[q_spec, kv_out, kv_out],
        out_shape=[SDS((t, A_W), bf16), SDS((t, A_W), f32), SDS((t, A_W), f32)],
        sem=("parallel", "arbitrary"), gather=gather)


def _halo_specs(tr, w, col, nblk):
    per = tr // SUBLANES
    cur = pl.BlockSpec((tr, w), lambda i: (i, col))
    prev = pl.BlockSpec((SUBLANES, w), lambda i: (jnp.maximum(i * per - 1, 0), col))
    nxt = pl.BlockSpec((SUBLANES, w), lambda i: (jnp.minimum((i + 1) * per, nblk * per - 1), col))
    return cur, prev, nxt


def _taps_before(cur, prev8, first):
    prev8 = jnp.where(first, 0.0, prev8)
    ext = jnp.concatenate([prev8, cur], axis=0)
    return [pltpu.roll(ext, s, 0)[SUBLANES:] for s in (3, 2, 1)]


def _taps_after(cur, next8, last):
    n = cur.shape[0]
    next8 = jnp.where(last, 0.0, next8)
    ext = jnp.concatenate([cur, next8], axis=0)
    return [pltpu.roll(ext, n + SUBLANES - s, 0)[:n] for s in (1, 2, 3)]


def _block_diag(x, w_ref, dims):
    outs = [lax.dot_general(x[:, n * LRU_BW:(n + 1) * LRU_BW], w_ref[n], (dims, ((), ())),
                            preferred_element_type=f32) for n in range(LRU_BLOCKS)]
    return jnp.concatenate(outs, axis=1)


def _lru_gates(xc, wa_ref, wi_ref, ba, bi, lam):
    xb = xc.astype(bf16)
    r = jax.nn.sigmoid(_block_diag(xb, wa_ref, NN) + ba)
    ig = jax.nn.sigmoid(_block_diag(xb, wi_ref, NN) + bi)
    sp = jnp.maximum(-lam, 0.0) + jnp.log(1.0 + jnp.exp(-jnp.abs(lam)))
    log_a = -LRU_C * r * sp
    a = jnp.exp(log_a)
    x2 = 2.0 * log_a
    one_minus = jnp.where(x2 > -1e-2, -x2 * (1.0 + x2 * (0.5 + x2 * (1.0 / 6.0))), 1.0 - a * a)
    mult = jnp.sqrt(one_minus)
    return xb, r, ig, sp, a, mult


def _rg_gates_fwd(proj, conv_w, conv_b, wa, wi, ba, bi, lam, tr=512):
    t = proj.shape[0]
    w = D_MODEL
    tr = min(tr, t)
    nblk = t // tr
    cur, prev, _ = _halo_specs(tr, w, 1, nblk)

    def body(x_ref, xp_ref, cw_ref, cb_ref, wa_ref, wi_ref, ba_ref, bi_ref, lam_ref, xc_ref, a_ref, u_ref):
        x = x_ref[...]
        taps = _taps_before(x, xp_ref[...], pl.program_id(0) == 0) + [x]
        xc = cb_ref[...]
        for k in range(4):
            xc = xc + cw_ref[k:k + 1, :] * taps[k]
        _, _, ig, _, a, mult = _lru_gates(xc, wa_ref, wi_ref, ba_ref[...], bi_ref[...], lam_ref[...])
        xc_ref[...] = xc
        a_ref[...] = a
        u_ref[...] = mult * (ig * xc)

    full = lambda a_: pl.BlockSpec(a_.shape, lambda i, nd=a_.ndim: (0,) * nd)
    ospec = pl.BlockSpec((tr, w), lambda i: (i, 0))
    return pl.pallas_call(
        body, grid=(nblk,), name="rg_gates_fwd",
        in_specs=[cur, prev] + [full(a_) for a_ in (conv_w, conv_b, wa, wi, ba, bi, lam)],
        out_specs=[ospec] * 3, out_shape=[SDS((t, w), f32)] * 3,
        compiler_params=_cparams(("parallel",)))(proj, proj, conv_w, conv_b, wa, wi, ba, bi, lam)


def _lru_scan(name, a, b, reverse, tt=1024):
    t, w = a.shape
    tt = min(tt, t)
    nt = t // tt
    ng = tt // SUBLANES

    def body(a_ref, b_ref, h_ref, carry_ref):
        @pl.when(pl.program_id(0) == 0)
        def _():
            carry_ref[...] = jnp.zeros_like(carry_ref)

        row = lax.broadcasted_iota(jnp.int32, (SUBLANES, w), 0)

        def group(gi, carry):
            g = (ng - 1 - gi) if reverse else gi
            rows = pl.ds(pl.multiple_of(g * SUBLANES, SUBLANES), SUBLANES)
            av = a_ref[rows, :]
            bv = b_ref[rows, :]
            for s in (1, 2, 4):
                sh = (SUBLANES - s) if reverse else s
                ok = (row < SUBLANES - s) if reverse else (row >= s)
                a_s = pltpu.roll(av, sh, 0)
                b_s = pltpu.roll(bv, sh, 0)
                bv = jnp.where(ok, av * b_s + bv, bv)
                av = jnp.where(ok, av * a_s, av)
            h = av * carry + bv
            h_ref[rows, :] = h
            edge = h[0:1, :] if reverse else h[SUBLANES - 1:SUBLANES, :]
            return jnp.broadcast_to(edge, (SUBLANES, w))

        carry_ref[...] = lax.fori_loop(0, ng, group, carry_ref[...], unroll=4)

    tmap = (lambda i: (nt - 1 - i, 0)) if reverse else (lambda i: (i, 0))
    spec = pl.BlockSpec((tt, w), tmap)
    return pl.pallas_call(
        body, grid=(nt,), name=name, in_specs=[spec, spec], out_specs=spec,
        out_shape=SDS((t, w), f32), scratch_shapes=[pltpu.VMEM((SUBLANES, w), f32)],
        compiler_params=_cparams(("arbitrary",)))(a, b)


def _rg_gates_bwd(dhs, c, hs, xc, wa, wi, ba, bi, lam, tr=512):
    t, w = xc.shape
    tr = min(tr, t)
    nblk = t // tr
    cur, prev, nxt = _halo_specs(tr, w, 0, nblk)

    def body(dhs_ref, c_ref, cn_ref, hs_ref, hp_ref, xc_ref, wa_ref, wi_ref, ba_ref, bi_ref, lam_ref,
             dxc_ref, dwa_ref, dwi_ref, dba_ref, dbi_ref, dlam_ref):
        i = pl.program_id(0)
        c_next = _taps_after(c_ref[...], cn_ref[...], i == nblk - 1)[0]
        h_prev = _taps_before(hs_ref[...], hp_ref[...], i == 0)[2]
        xc = xc_ref[...]
        lam = lam_ref[...]
        xb, r, ig, sp, a, mult = _lru_gates(xc, wa_ref, wi_ref, ba_ref[...], bi_ref[...], lam)
        dh = dhs_ref[...] + c_next
        dlog_a = dh * h_prev * a - (dh * ig * xc) * (a * a / mult)
        dpre_a = (dlog_a * (-LRU_C * sp) * r * (1.0 - r)).astype(bf16)
        dpre_i = (dh * mult * xc * ig * (1.0 - ig)).astype(bf16)
        dxc_ref[...] = (dh * mult * ig + _block_diag(dpre_a, wa_ref, NT) + _block_diag(dpre_i, wi_ref, NT))
        dsig = 1.0 / (1.0 + jnp.exp(lam))
        sums = [jnp.sum(dpre_a.astype(f32), axis=0, keepdims=True),
                jnp.sum(dpre_i.astype(f32), axis=0, keepdims=True),
                jnp.sum(dlog_a * (-LRU_C * r), axis=0, keepdims=True) * (-dsig)]

        @pl.when(i == 0)
        def _():
            dwa_ref[...] = jnp.zeros_like(dwa_ref)
            dwi_ref[...] = jnp.zeros_like(dwi_ref)
            dba_ref[...] = jnp.zeros_like(dba_ref)
            dbi_ref[...] = jnp.zeros_like(dbi_ref)
            dlam_ref[...] = jnp.zeros_like(dlam_ref)

        for n in range(LRU_BLOCKS):
            sl = slice(n * LRU_BW, (n + 1) * LRU_BW)
            dwa_ref[n] += lax.dot_general(xb[:, sl], dpre_a[:, sl], (TN, ((), ())), preferred_element_type=f32)
            dwi_ref[n] += lax.dot_general(xb[:, sl], dpre_i[:, sl], (TN, ((), ())), preferred_element_type=f32)
        dba_ref[...] += sums[0]
        dbi_ref[...] += sums[1]
        dlam_ref[...] += sums[2]

    full = lambda a_: pl.BlockSpec(a_.shape, lambda i, nd=a_.ndim: (0,) * nd)
    vec = pl.BlockSpec((1, w), lambda i: (0, 0))
    mat = pl.BlockSpec((LRU_BLOCKS, LRU_BW, LRU_BW), lambda i: (0, 0, 0))
    return pl.pallas_call(
        body, grid=(nblk,), name="rg_gates_bwd",
        in_specs=[cur, cur, nxt, cur, prev, cur] + [full(a_) for a_ in (wa, wi, ba, bi, lam)],
        out_specs=[cur, mat, mat, vec, vec, vec],
        out_shape=[SDS((t, w), f32), SDS((LRU_BLOCKS, LRU_BW, LRU_BW), f32), SDS((LRU_BLOCKS, LRU_BW, LRU_BW), f32),
                   SDS((1, w), f32), SDS((1, w), f32), SDS((1, w), f32)],
        compiler_params=_cparams(("arbitrary",)))(dhs, c, c, hs, hs, xc, wa, wi, ba, bi, lam)


def _rg_conv_bwd(dxc, proj, conv_w, tr=512):
    t, w = dxc.shape
    tr = min(tr, t)
    nblk = t // tr
    cur, _, nxt = _halo_specs(tr, w, 0, nblk)
    xcur, xprev, _ = _halo_specs(tr, w, 1, nblk)

    def body(d_ref, dn_ref, x_ref, xp_ref, cw_ref, dx_ref, dcw_ref, dcb_ref):
        i = pl.program_id(0)
        d = d_ref[...]
        x = x_ref[...]
        after = _taps_after(d, dn_ref[...], i == nblk - 1)
        before = _taps_before(x, xp_ref[...], i == 0) + [x]
        dx = cw_ref[3:4, :] * d
        for s in (1, 2, 3):
            dx = dx + cw_ref[3 - s:4 - s, :] * after[s - 1]
        dx_ref[...] = dx.astype(bf16)
        dcw = jnp.concatenate([jnp.sum(d * before[k], axis=0, keepdims=True) for k in range(4)], axis=0)
        dcb = jnp.sum(d, axis=0, keepdims=True)

        @pl.when(i == 0)
        def _():
            dcw_ref[...] = dcw
            dcb_ref[...] = dcb

        @pl.when(i > 0)
        def _():
            dcw_ref[...] += dcw
            dcb_ref[...] += dcb

    return pl.pallas_call(
        body, grid=(nblk,), name="rg_conv_bwd",
        in_specs=[cur, nxt, xcur, xprev, pl.BlockSpec((4, w), lambda i: (0, 0))],
        out_specs=[cur, pl.BlockSpec((4, w), lambda i: (0, 0)), pl.BlockSpec((1, w), lambda i: (0, 0))],
        out_shape=[SDS((t, w), bf16), SDS((4, w), f32), SDS((1, w), f32)],
        compiler_params=_cparams(("arbitrary",)))(dxc, dxc, proj, proj, conv_w)


def _attn_fwd(h, wts, j, plan):
    proj = _mm_cols("attn_in", h, wts["attn_w_in"], j, bf16)
    kpad = jnp.pad(proj[:, A_W:2 * A_W], ((PAD_A, 0), (0, 0)))
    vpad = jnp.pad(proj[:, 2 * A_W:3 * A_W], ((PAD_A, 0), (0, 0)))
    bias = _bias_window(wts["attn_rel_bias"][j])
    plan = plan if j == 0 else None
    out_a = _carried(plan, "chunk_attn_fwd", wts, _chunk_attn_fwd, proj, kpad, vpad, bias)
    out_b, out_b32 = _carried(plan, "sb_attn_fwd", wts, _sb_fwd, proj)
    m = _mm_rows("attn_out", [out_a, out_b], wts["attn_w_out"], j, f32)
    return m, (proj, kpad, vpad, bias, out_a, out_b, out_b32)


def _attn_bwd(dm, h, saved, wts, j, grads, exch):
    proj, kpad, vpad, bias, out_a, out_b, out_b32 = saved
    dout = _mm_rows_t("attn_out_t", dm, wts["attn_w_out"], j, bf16)
    gi, ll = _grad_slot("attn_w_out", j)
    grads["attn_w_out"][gi] = _mm_wgrad("attn_out_wgrad_a", out_a, dm, grads["attn_w_out"][gi], ll, 0)
    grads["attn_w_out"][gi] = _mm_wgrad("attn_out_wgrad_b", out_b, dm, grads["attn_w_out"][gi], ll, 1)
    if exch is not None and j == 0:
        (dqa, dka, dva, dbias), got = _chunk_attn_bwd(proj, kpad, vpad, bias, dout, exch.upper_carry(grads))
        exch.upper_got(got)
        (dqs, dks, dvs), slots = _sb_bwd(proj, out_b32, dout, exch.carry())
        exch.carried(slots)
    else:
        dqa, dka, dva, dbias = _chunk_attn_bwd(proj, kpad, vpad, bias, dout, None)[0]
        dqs, dks, dvs = _sb_bwd(proj, out_b32, dout, None)[0]
    grads["attn_rel_bias"][j] = _bias_window_grad(dbias)
    dproj = jnp.concatenate([dqa, dka[PAD_A:].astype(bf16), dva[PAD_A:].astype(bf16),
                             dqs, dks.astype(bf16), dvs.astype(bf16)], axis=1)
    gi, ll = _grad_slot("attn_w_in", j)
    grads["attn_w_in"][gi] = _mm_wgrad_cols("attn_in_wgrad", h, dproj, grads["attn_w_in"][gi], ll)
    return _mm_cols_t("attn_in_t", dproj, wts["attn_w_in"], j, f32)


def _rg_fwd(h, wts, j, plan):
    proj =_mm_cols("rg_in", h, wts["rg_w_in"], j, f32)
    small = [wts[k][j] for k in ("rg_conv_w", "rg_conv_b", "rg_w_a", "rg_w_i", "rg_b_a", "rg_b_i", "rg_lambda")]
    xc, a, u = _rg_gates_fwd(proj, *small)
    hs = _lru_scan("lru_scan_fwd", a, u, False)
    yp = _rows("rg_gate_out", lambda hv, gv: hv * _gelu(gv), [hs, (proj, D_MODEL, 0)], [], [(D_MODEL, bf16)])[0]
    m = _mm_rows("rg_out", [yp], wts["rg_w_out"], j, f32)
    return m, (proj, xc, a, hs, yp)


def _rg_bwd(dm, h, saved, wts, j, grads, exch):
    proj, xc, a, hs, yp = saved
    dyp = _mm_rows_t("rg_out_t", dm, wts["rg_w_out"], j, f32)
    gi, ll = _grad_slot("rg_w_out", j)
    grads["rg_w_out"][gi] = _mm_wgrad("rg_out_wgrad", yp, dm, grads["rg_w_out"][gi], ll)

    def gate_bwd(dy, hv, gv, av):
        dhs = dy * _gelu(gv)
        return dhs, av * dhs, dy * hv * _gelu_grad(gv)

    dhs, ab, dgate = _rows("rg_gate_out_bwd", gate_bwd, [dyp, hs, (proj, D_MODEL, 0), a], [],
                           [(D_MODEL, f32), (D_MODEL, f32), (D_MODEL, bf16)])
    c = _lru_scan("lru_scan_bwd", a, ab, True)
    wa, wi, ba, bi, lam = [wts[k][j] for k in ("rg_w_a", "rg_w_i", "rg_b_a", "rg_b_i", "rg_lambda")]
    dxc, dwa, dwi, dba, dbi, dlam = _rg_gates_bwd(dhs, c, hs, xc, wa, wi, ba, bi, lam)
    dxr, dcw, dcb = _rg_conv_bwd(dxc, proj, wts["rg_conv_w"][j])
    for k, v in (("rg_w_a", dwa), ("rg_w_i", dwi), ("rg_b_a", dba), ("rg_b_i", dbi), ("rg_lambda", dlam),
                 ("rg_conv_w", dcw), ("rg_conv_b", dcb)):
        grads[k][j] = v
    dproj = jnp.concatenate([dgate, dxr], axis=1)
    grads["rg_w_in"][gi] = _mm_wgrad_cols("rg_in_wgrad", h, dproj, grads["rg_w_in"][gi], ll)
    return _mm_cols_t("rg_in_t", dproj, wts["rg_w_in"], j, f32)


def _local_step(x, target, wts, plan=None, exch=None):
    t = x.shape[0]
    d = D_MODEL
    gains = {k: wts[k] for k in ("norm_mix_pre", "norm_mix_post", "norm_ffn_pre", "norm_ffn_post")}
    gain = lambda k, l: gains[k][l:l + 1]

    saved = []
    h = _rows("norm_in", _norm_fwd, [x], [gain("norm_mix_pre", 0)], [(d, bf16)])[0]
    loss_cols = None
    for l in range(DEPTH):
        j = l // 2
        m, mix_saved = (_attn_fwd if l % 2 == 0 else _rg_fwd)(h, wts, j, plan)

        def resid_next(xv, mv, g_post, g_next):
            x1 = xv + _norm_fwd(mv, g_post)
            return x1, _norm_fwd(x1, g_next)

        x1, h2 = _rows("resid_mix", resid_next, [x, m], [gain("norm_mix_post", l), gain("norm_ffn_pre", l)],
                       [(d, f32), (d, bf16)])
        g, u, hid = _carried(plan if l == 0 else None, "ffn_up", wts, _ffn_up, h2, wts["ffn_w_gate"],
                             wts["ffn_w_up"], l)
        f = _ffn_down(hid, wts["ffn_w_down"], l)
        saved.append((x, h, m, mix_saved, x1, h2, g, u, hid, f))
        if l + 1 < DEPTH:
            x, h = _rows("resid_ffn", resid_next, [x1, f], [gain("norm_ffn_post", l), gain("norm_mix_pre", l + 1)],
                         [(d, f32), (d, bf16)])
        else:
            def resid_loss(xv, fv, tv, g_post):
                err = xv + _norm_fwd(fv, g_post) - tv
                return err * (1.0 / d), jnp.sum(err * err, axis=0, keepdims=True)

            dx, loss_cols = _rows("resid_loss", resid_loss, [x1, f, target], [gain("norm_ffn_post", l)],
                                  [(d, f32)], [((1, d), f32)])
    loss = 0.5 * jnp.sum(loss_cols) / d

    grads = {k: {} for k in SMALL_GRADS}
    for k in BIG_GRADS:
        shp = wts[k].shape
        rest = shp[2:] if shp[1] == 1 else shp[1:]
        grads[k] = [_Fresh((LOWER_LAYERS[k],) + rest), _Fresh((shp[0] - LOWER_LAYERS[k],) + rest)]

    def norm_bwd_cast(uv, dyv, gv):
        du, dg = _norm_bwd(uv, dyv, gv)
        return du, dg

    def norm_bwd_resid(uv, dhv, dxv, gv):
        du, dg = _norm_bwd(uv, dhv, gv)
        return dxv + du, dg

    def norm_bwd_pair(uv, dhv, dxv, nv, g_pre, g_post):
        dx_, dg_pre = norm_bwd_resid(uv, dhv, dxv, g_pre)
        dn, dg_post = _norm_bwd(nv, dx_, g_post)
        return dx_, dn, dg_pre, dg_post

    df = None
    for l in reversed(range(DEPTH)):
        j = l // 2
        x_in, h, m, mix_saved, x1, h2, g, u, hid, f = saved[l]
        if df is None:
            df, grads["norm_ffn_post"][l] = _rows("norm_ffn_post_bwd", norm_bwd_cast, [f, dx],
                                                  [gain("norm_ffn_post", l)], [(d, bf16)], [((1, d), f32)])
        dg, du = _ffn_down_bwd(df, wts["ffn_w_down"], l, g, u)
        gi, ll = _grad_slot("ffn_w_down", l)
        grads["ffn_w_down"][gi] = _ffn_wgrad_down(hid, df, grads["ffn_w_down"][gi], ll)
        dh2 = _ffn_up_bwd(dg, du, wts["ffn_w_gate"], wts["ffn_w_up"], l)
        grads["ffn_w_gate"][gi], grads["ffn_w_up"][gi] = _ffn_wgrad_up(
            h2, dg, du, grads["ffn_w_gate"][gi], grads["ffn_w_up"][gi], ll)
        dx1, dm, grads["norm_ffn_pre"][l], grads["norm_mix_post"][l] = _rows(
            "norm_ffn_mix_bwd", norm_bwd_pair, [x1, dh2, dx, m], [gain("norm_ffn_pre", l), gain("norm_mix_post", l)],
            [(d, f32), (d, bf16)], [((1, d), f32), ((1, d), f32)])
        dh = (_attn_bwd if l % 2 == 0 else _rg_bwd)(dm, h, mix_saved, wts, j, grads, exch)
        if l > 0:
            dx, df, grads["norm_mix_pre"][l], grads["norm_ffn_post"][l - 1] = _rows(
                "norm_mix_ffn_bwd", norm_bwd_pair, [x_in, dh, dx1, saved[l - 1][9]],
                [gain("norm_mix_pre", l), gain("norm_ffn_post", l - 1)],
                [(d, f32), (d, bf16)], [((1, d), f32), ((1, d), f32)])
        else:
            dx, grads["norm_mix_pre"][l] = _rows("norm_mix_pre_bwd", norm_bwd_resid, [x_in, dh, dx1],
                                                 [gain("norm_mix_pre", l)], [(d, f32)], [((1, d), f32)])
    return loss, dx, grads


ANY = pl.BlockSpec(memory_space=pl.ANY)
PACK_COLS = 1024
SMALL_ROWS = 288


def _mesh_pos():
    x, y, c = lax.axis_index("x"), lax.axis_index("y"), lax.axis_index("c")
    return x, y, c, [(1 - x, y), (x, 1 - y), (1 - x, 1 - y)]


def _run_copies(copies):
    for cp in copies:
        cp.start()
    for cp in copies:
        cp.wait()


GATHER_SEMS = 7


def _gather_copies(items, ins, outs, send, recv):
    x, y, c, chips = _mesh_pos()
    q = 2 * x + y
    sibling = (x, y, 1 - c)

    def copy(k, src, dst, to):
        return pltpu.make_async_remote_copy(src_ref=src, dst_ref=dst, send_sem=send.at[k], recv_sem=recv.at[k],
                                            device_id=to, device_id_type=MESH)

    own, sent, passed = [], [], []
    for i, (t, l0, nl) in enumerate(items):
        lay = pl.ds(l0, nl)
        half = ins[t].shape[1] // 2
        rows = pl.ds(pl.multiple_of(c * half, half), half)
        own.append(copy(GATHER_SEMS * i, ins[t].at[lay], outs[t].at[lay, q], sibling))
        for j, (px, py) in enumerate(chips):
            sent.append(copy(GATHER_SEMS * i + 1 + j, ins[t].at[lay, rows], outs[t].at[lay, q, rows], (px, py, c)))
            landed = outs[t].at[lay, 2 * px + py, rows]
            passed.append(copy(GATHER_SEMS * i + 4 + j, landed, landed, sibling))
    return own, sent, passed


def _gather_start(items, ins, outs, send, recv):
    own, sent, _ = _gather_copies(items, ins, outs, send, recv)
    for cp in own + sent:
        cp.start()


def _gather_finish(items, ins, outs, send, recv):
    own, sent, passed = _gather_copies(items, ins, outs, send, recv)
    for arrived, forward in zip(sent, passed):
        arrived.wait_recv()
        forward.start()
    for cp in sent:
        cp.wait_send()
    for cp in own + passed:
        cp.wait()


def _gather_call(items, shards):
    n = len(shards)
    nsem = GATHER_SEMS * len(items)

    def body(*refs):
        ins, outs = refs[:n], refs[n:2 * n]
        _gather_start(items, ins, outs, *refs[2 * n:])
        _gather_finish(items, ins, outs, *refs[2 * n:])

    return pl.pallas_call(
        body, name="weight_all_gather", in_specs=[ANY] * n, out_specs=[ANY] * n,
        out_shape=[SDS((s.shape[0], N_CHIPS) + s.shape[1:], s.dtype) for s in shards],
        scratch_shapes=[pltpu.SemaphoreType.DMA((nsem,)), pltpu.SemaphoreType.DMA((nsem,))])(*shards)


def _call(body, operands, *, name, grid, in_specs, out_specs, out_shape, sem, scratch=(), gather=None):
    if gather is None:
        return pl.pallas_call(body, grid=grid, in_specs=in_specs, out_specs=out_specs, out_shape=out_shape,
                              scratch_shapes=list(scratch), name=name, compiler_params=_cparams(sem))(*operands), None
    start, finish, c_ins, c_io, c_new, nsem = gather
    n_in, n_out, n_scr = len(operands), len(out_shape), len(scratch)
    ni, nio, nco = len(c_ins), len(c_io), len(c_io) + len(c_new)

    def full(*refs):
        ins, sh = refs[:n_in], refs[n_in:n_in + ni]
        outs = refs[n_in + ni + nio:n_in + ni + nio + n_out]
        co = refs[n_in + ni + nio + n_out:n_in + ni + nio + n_out + nco]
        scr = refs[n_in + ni + nio + n_out + nco:]
        ids = [pl.program_id(a) for a in range(len(grid))]
        first = functools.reduce(jnp.logical_and, [i == 0 for i in ids])
        last = functools.reduce(jnp.logical_and, [i == g - 1 for i, g in zip(ids, grid)])

        @pl.when(first)
        def _():
            start(sh, co, scr[n_scr], scr[n_scr + 1])

        body(*ins, *outs, *scr[:n_scr])

        @pl.when(last)
        def _():
            finish(sh, co, scr[n_scr], scr[n_scr + 1])

    res = pl.pallas_call(
        full, grid=grid, in_specs=list(in_specs) + [ANY] * (ni + nio), out_specs=list(out_specs) + [ANY] * nco,
        out_shape=list(out_shape) + [SDS(g.shape, g.dtype) for g in list(c_io) + list(c_new)],
        scratch_shapes=list(scratch) + [pltpu.SemaphoreType.DMA((nsem,)), pltpu.SemaphoreType.DMA((nsem,))],
        input_output_aliases={n_in + ni + t: n_out + t for t in range(nio)}, name=name,
        compiler_params=_cparams(("arbitrary",) * len(grid)))(*operands, *c_ins, *c_io)
    return res[:n_out], res[n_out:]


def _pair_exchange(gs):
    n = len(gs)

    def body(*refs):
        _pair_copies(refs[:n], refs[n:2 * n], *refs[2 * n:], start=True)
        _pair_copies(refs[:n], refs[n:2 * n], *refs[2 * n:], start=False)

    return pl.pallas_call(
        body, name="grad_pair_exchange", in_specs=[ANY] * n, out_specs=[ANY] * n,
        out_shape=_pair_shapes(gs),
        scratch_shapes=[pltpu.SemaphoreType.DMA((n,)), pltpu.SemaphoreType.DMA((n,))])(*gs)


def _pair_shapes(gs):
    return [SDS(g.shape[:2] + (g.shape[2] // 2, g.shape[3]), f32) for g in gs]


def _pair_copies(ins, outs, send, recv, start):
    x, y, c, _ = _mesh_pos()
    for t in range(len(ins)):
        half = ins[t].shape[2] // 2
        src = ins[t].at[:, :, pl.ds(pl.multiple_of((1 - c) * half, SUBLANES), half)]
        cp = pltpu.make_async_remote_copy(src_ref=src, dst_ref=outs[t], send_sem=send.at[t], recv_sem=recv.at[t],
                                          device_id=(x, y, 1 - c), device_id_type=MESH)
        cp.start() if start else cp.wait()


def _pair_carry(gs):
    return (functools.partial(_pair_copies, start=True), functools.partial(_pair_copies, start=False),
            gs, [], _pair_shapes(gs), len(gs))


def _pair_sum(name, g, got, c):
    l, s, r, cols = g.shape

    def body(c_ref, a_ref, b_ref, o_ref):
        o_ref[...] = (a_ref[...] + b_ref[...]).astype(bf16)

    blk = (None, None, r // 2, cols)
    return pl.pallas_call(
        body, name=name, out_shape=SDS(got.shape, bf16),
        grid_spec=pltpu.PrefetchScalarGridSpec(
            num_scalar_prefetch=1, grid=(l, s),
            in_specs=[pl.BlockSpec(blk, lambda i, q, c_ref: (i, q, c_ref[0], 0)),
                      pl.BlockSpec(blk, lambda i, q, c_ref: (i, q, 0, 0))],
            out_specs=pl.BlockSpec(blk, lambda i, q, c_ref: (i, q, 0, 0))),
        compiler_params=_cparams(("parallel", "parallel")))(c, g, got)


def _chip_exchange(hs):
    n = len(hs)

    def body(*refs):
        _chip_copies(refs[:n], refs[n:2 * n], *refs[2 * n:], start=True)
        _chip_copies(refs[:n], refs[n:2 * n], *refs[2 * n:], start=False)

    return pl.pallas_call(
        body, name="grad_chip_exchange", in_specs=[ANY] * n, out_specs=[ANY] * n,
        out_shape=[SDS(h.shape, h.dtype) for h in hs],
        scratch_shapes=[pltpu.SemaphoreType.DMA((3 * n,)), pltpu.SemaphoreType.DMA((3 * n,))])(*hs)


def _chip_copies(ins, outs, send, recv, start):
    x, y, c, chips = _mesh_pos()
    q = 2 * x + y
    for t in range(len(ins)):
        for j, (px, py) in enumerate(chips):
            cp = pltpu.make_async_remote_copy(
                src_ref=ins[t].at[:, 2 * px + py], dst_ref=outs[t].at[:, q], send_sem=send.at[3 * t + j],
                recv_sem=recv.at[3 * t + j], device_id=(px, py, c), device_id_type=MESH)
            cp.start() if start else cp.wait()


def _chip_carry(hs):
    return (functools.partial(_chip_copies, start=True), functools.partial(_chip_copies, start=False),
            hs, [], [SDS(h.shape, h.dtype) for h in hs], 3 * len(hs))


def _chip_sum(name, s, h, pos, l0, layers, into):
    l, _, r, cols = s.shape

    def body(pos_ref, s0, s1, s2, s3, own_ref, *rest):
        vals = [jnp.where(pos_ref[0] == p, own_ref[...], ref[...]).astype(f32) for p, ref in enumerate((s0, s1, s2, s3))]
        rest[-1][...] = ((vals[0] + vals[1]) + vals[2]) + vals[3]

    blk = (None, None, r, cols)
    slot = lambda p: pl.BlockSpec(blk, lambda i, pos_ref: (i, jnp.where(pos_ref[0] == p, (p + 1) % N_CHIPS, p), 0, 0))
    extra, alias = ([], {}) if into is None else ([into], {6: 0})
    return pl.pallas_call(
        body, name=name, out_shape=SDS((layers, 2 * r, cols), f32), input_output_aliases=alias,
        grid_spec=pltpu.PrefetchScalarGridSpec(
            num_scalar_prefetch=1, grid=(l,),
            in_specs=[slot(p) for p in range(N_CHIPS)] + [pl.BlockSpec(blk, lambda i, pos_ref: (i, pos_ref[0], 0, 0))]
            + [ANY] * len(extra),
            out_specs=pl.BlockSpec((None, r, cols), lambda i, pos_ref: (l0 + i, pos_ref[1], 0))),
        compiler_params=_cparams(("parallel",)))(pos, s, s, s, s, h, *extra)


def _pair_gather(fulls):
    n = len(fulls)

    def body(*refs):
        ins, outs = refs[:n], refs[n:2 * n]
        send, recv = refs[2 * n:]
        x, y, c, _ = _mesh_pos()
        copies = []
        for t in range(n):
            half = outs[t].shape[1] // 2
            rows = outs[t].at[:, pl.ds(pl.multiple_of(c * half, SUBLANES), half)]
            copies.append(pltpu.make_async_remote_copy(
                src_ref=rows, dst_ref=rows, send_sem=send.at[t], recv_sem=recv.at[t],
                device_id=(x, y, 1 - c), device_id_type=MESH))
        _run_copies(copies)

    return pl.pallas_call(
        body, name="grad_pair_gather", in_specs=[ANY] * n, out_specs=[ANY] * n,
        out_shape=[SDS(f.shape, f32) for f in fulls], input_output_aliases={t: t for t in range(n)},
        scratch_shapes=[pltpu.SemaphoreType.DMA((n,)), pltpu.SemaphoreType.DMA((n,))])(*fulls)


COL_SHARDED = ("attn_w_in", "rg_w_in", "ffn_w_gate", "ffn_w_up")
ROW_SHARDED = ("attn_w_out", "rg_w_out")
GATES = ("rg_w_a", "rg_w_i")
VECTORS = ("rg_conv_w", "rg_conv_b", "rg_b_a", "rg_b_i", "rg_lambda")
REPLICATED = ("norm_mix_pre", "norm_mix_post", "norm_ffn_pre", "norm_ffn_post", "attn_rel_bias")
BIG_GRADS = COL_SHARDED + ROW_SHARDED + ("ffn_w_down",)
SMALL_GRADS = GATES + VECTORS + REPLICATED
WEIGHTS =("attn_w_in", "attn_rel_bias", "attn_w_out", "rg_w_in", "rg_conv_w", "rg_conv_b", "rg_w_a", "rg_b_a",
           "rg_w_i", "rg_b_i", "rg_lambda", "rg_w_out", "norm_mix_pre", "norm_mix_post", "norm_ffn_pre",
           "norm_ffn_post", "ffn_w_gate", "ffn_w_up", "ffn_w_down")
SMALL = VECTORS + REPLICATED


GATHER_PARTS = {
    "first": (("attn_w_in", 0, 1), ("attn_w_out", 0, 1), ("rg_w_a", 0, 8), ("rg_w_i", 0, 8), ("vec", 0, 1)),
    "chunk_attn_fwd": (("ffn_w_gate", 0, 1), ("ffn_w_up", 0, 1), ("ffn_w_down", 0, 1), ("rg_w_in", 0, 1),
                       ("rg_w_out", 0, 1)),
    "sb_attn_fwd": (("ffn_w_gate", 1, 3), ("ffn_w_up", 1, 3), ("ffn_w_down", 1, 3)),
    "ffn_up": (("rg_w_in", 1, 1), ("rg_w_out", 1, 1), ("attn_w_in", 1, 1), ("attn_w_out", 1, 1)),
}


TRANSPOSED = ("ffn_w_gate", "ffn_w_up")


def _natural(name, a):
    return jnp.swapaxes(a, 1, 2) if name in TRANSPOSED else a


class _WeightGather:
    def __init__(self, w):
        self.w = w
        self.names = list(COL_SHARDED + ROW_SHARDED + GATES + ("ffn_w_down", "vec"))
        self.shards = {}
        for k in self.names[:-1]:
            a = _natural(k, w[k]).astype(bf16)
            self.shards[k] = a.reshape((-1,) + a.shape[-2:])
        self.shards["vec"] = jnp.concatenate([w[k].reshape(-1) for k in VECTORS]).reshape(1, -1, LANES)
        got = _gather_call(self._items("first", self.names), [self.shards[k] for k in self.names])
        self.raw = dict(zip(self.names, got))

    @staticmethod
    def _items(part, names):
        return [(names.index(k), l0, nl) for k, l0, nl in GATHER_PARTS[part]]

    def part(self, part):
        names = list(dict.fromkeys(k for k, _, _ in GATHER_PARTS[part]))
        items = self._items(part, names)
        return (functools.partial(_gather_start, items), functools.partial(_gather_finish, items),
                [self.shards[k] for k in names], [self.raw[k] for k in names], [], GATHER_SEMS * len(items)), names

    def views(self):
        got, w = self.raw, self.w
        out = {k: w[k] for k in REPLICATED}
        for k in COL_SHARDED + ("ffn_w_down",):
            out[k] = got[k]
        for k in ROW_SHARDED:
            l, s, ks, n = got[k].shape
            out[k] = got[k].reshape(l, 1, s * ks, n)
        for k in GATES:
            out[k] = got[k].reshape(2, LRU_BLOCKS, LRU_BW, LRU_BW)
        vec = got["vec"].reshape(N_CHIPS, -1)
        off = 0
        for k in VECTORS:
            shp = w[k].shape
            n = int(np.prod(shp))
            piece = vec[:, off:off + n].reshape((N_CHIPS,) + shp)
            off += n
            if k == "rg_conv_w":
                out[k] = piece.reshape(N_CHIPS, 2, 4, 256).transpose(1, 2, 0, 3).reshape(2, 4, D_MODEL)
            elif k in ("rg_b_a", "rg_b_i"):
                out[k] = piece.transpose(1, 2, 0, 3).reshape(2, 1, D_MODEL)
            else:
                out[k] = piece.transpose(1, 0, 2).reshape(2, 1, D_MODEL)
        return out


def _carried(plan, part, wts, fn, *args):
    if plan is None:
        return fn(*args, None)[0]
    gather, names = plan.part(part)
    out, new = fn(*args, gather)
    plan.raw.update(zip(names, new))
    wts.update(plan.views())
    return out


def _grad_blocks(name, g):
    st = jnp.stack([g[i] for i in sorted(g)])
    if name in GATES:
        st = st.reshape(2, LRU_BLOCKS, N_CHIPS, LRU_BW // N_CHIPS, LRU_BW).transpose(2, 0, 1, 3, 4)
    elif name == "rg_conv_w":
        st = st.reshape(2, 4, N_CHIPS, -1).transpose(2, 0, 1, 3)
    elif name in ("rg_b_a", "rg_b_i"):
        st = st.reshape(2, LRU_BLOCKS, N_CHIPS, -1).transpose(2, 0, 1, 3)
    elif name in VECTORS:
        st = st.reshape(2, N_CHIPS, -1).transpose(1, 0, 2)
    else:
        st = jnp.broadcast_to(st.reshape(1, -1), (N_CHIPS, st.size))
    return st.reshape(N_CHIPS, -1)


class _GradExchange:
    def __init__(self):
        self.c = lax.axis_index("c").astype(jnp.int32).reshape(1)
        self.pos = jnp.stack([2 * lax.axis_index("x") + lax.axis_index("y"), lax.axis_index("c")]).astype(jnp.int32)
        self.up = self.got_up = self.parts_up = self.slots_up = None

    @staticmethod
    def _blocked(g):
        if g.ndim == 3:
            g = g.reshape(g.shape[0], N_CHIPS, g.shape[1] // N_CHIPS, g.shape[2])
        return g

    def _sums(self, tag, names, gs, got):
        return [_pair_sum("grad_pair_sum_" + tag + k, g, r, self.c) for k, g, r in zip(names, gs, got)]

    def upper_carry(self, grads):
        self.up = [self._blocked(grads[k][1]) for k in BIG_GRADS]
        return _pair_carry(self.up)

    def upper_got(self, got):
        self.got_up = got

    def carry(self):
        self.parts_up = self._sums("up_", BIG_GRADS, self.up, self.got_up)
        return _chip_carry(self.parts_up)

    def carried(self, slots):
        self.slots_up = slots

    def finish(self, grads, shard_shapes):
        if self.got_up is None:
            self.upper_carry(grads)
            self.got_up = _pair_exchange(self.up)
        if self.slots_up is None:
            self.carry()
            self.slots_up = _chip_exchange(self.parts_up)
        blocks = [_grad_blocks(k, grads[k]) for k in SMALL_GRADS]
        used = sum(b.shape[1] for b in blocks)
        small = jnp.concatenate(blocks + [jnp.zeros((N_CHIPS, SMALL_ROWS * PACK_COLS - used), f32)], axis=1)
        names = tuple(k for k in BIG_GRADS if LOWER_LAYERS[k]) + ("small",)
        gs = [self._blocked(grads[k][0]) for k in names[:-1]] + [small.reshape(1, N_CHIPS, SMALL_ROWS, PACK_COLS)]
        parts = dict(zip(names, self._sums("lo_", names, gs, _pair_exchange(gs))))
        slots = dict(zip(names, _chip_exchange([parts[k] for k in names])))
        fulls = []
        for i, k in enumerate(BIG_GRADS):
            nlo, nup = LOWER_LAYERS[k], self.parts_up[i].shape[0]
            full = _chip_sum("grad_chip_sum_up_" + k, self.slots_up[i], self.parts_up[i], self.pos, nlo, nlo + nup, None)
            if nlo:
                full = _chip_sum("grad_chip_sum_lo_" + k, slots[k], parts[k], self.pos, 0, nlo + nup, full)
            fulls.append(full)
        fulls.append(_chip_sum("grad_chip_sum_lo_small", slots["small"], parts["small"], self.pos, 0, 1, None))
        full = _pair_gather(fulls)
        out = {k: f.reshape(shard_shapes[k]) for k, f in zip(BIG_GRADS, full)}
        flat, off = full[-1].reshape(-1), 0
        for k in SMALL_GRADS:
            n = int(np.prod(shard_shapes[k]))
            out[k] = flat[off:off + n].reshape(shard_shapes[k])
            off += n
        return out


def _adamw_fn(w, g, m, v):
    m = ADAM_B1 * m + (1.0 - ADAM_B1) * g
    v = ADAM_B2 * v + (1.0 - ADAM_B2) * (g * g)
    m_hat = m / (1.0 - ADAM_B1 ** ADAM_STEP)
    v_hat = v / (1.0 - ADAM_B2 ** ADAM_STEP)
    return -ADAM_LR * (m_hat / (jnp.sqrt(v_hat) + ADAM_EPS) + ADAM_WD * w), m, v


def _adamw(name, w, g, m, v):
    shp = w.shape
    if w.size >= 1 << 16:
        width = shp[-1]
        ops = [a.reshape(-1, width) for a in (w, g, m, v)]
        res = _rows(name, _adamw_fn, ops, [], [(width, f32)] * 3)
        return [r.reshape(shp) for r in res]
    n = w.size
    rows = -(-n // (SUBLANES * LANES)) * SUBLANES
    ops = [jnp.pad(a.reshape(-1), (0, rows * LANES - n)).reshape(rows, LANES) for a in (w, g, m, v)]
    res = _rows(name, _adamw_fn, ops, [], [(LANES, f32)] * 3, tr=rows)
    return [r.reshape(-1)[:n].reshape(shp) for r in res]


def kernel(x, attn_w_in, attn_rel_bias, attn_w_out, rg_w_in, rg_conv_w, rg_conv_b, rg_w_a, rg_b_a, rg_w_i, rg_b_i, rg_lambda, rg_w_out, norm_mix_pre, norm_mix_post, norm_ffn_pre, norm_ffn_post, ffn_w_gate, ffn_w_up, ffn_w_down, loss_target, m_attn_w_in, m_attn_rel_bias, m_attn_w_out, m_rg_w_in, m_rg_conv_w, m_rg_conv_b, m_rg_w_a, m_rg_b_a, m_rg_w_i, m_rg_b_i, m_rg_lambda, m_rg_w_out, m_norm_mix_pre, m_norm_mix_post, m_norm_ffn_pre, m_norm_ffn_post, m_ffn_w_gate, m_ffn_w_up, m_ffn_w_down, v_attn_w_in, v_attn_rel_bias, v_attn_w_out, v_rg_w_in, v_rg_conv_w, v_rg_conv_b, v_rg_w_a, v_rg_b_a, v_rg_w_i, v_rg_b_i, v_rg_lambda, v_rg_w_out, v_norm_mix_pre, v_norm_mix_post, v_norm_ffn_pre, v_norm_ffn_post, v_ffn_w_gate, v_ffn_w_up, v_ffn_w_down):
    w = dict(zip(WEIGHTS, (attn_w_in, attn_rel_bias, attn_w_out, rg_w_in, rg_conv_w, rg_conv_b, rg_w_a, rg_b_a, rg_w_i,
                           rg_b_i, rg_lambda, rg_w_out, norm_mix_pre, norm_mix_post, norm_ffn_pre, norm_ffn_post,
                           ffn_w_gate, ffn_w_up, ffn_w_down)))
    m = dict(zip(WEIGHTS, (m_attn_w_in, m_attn_rel_bias, m_attn_w_out, m_rg_w_in, m_rg_conv_w, m_rg_conv_b, m_rg_w_a,
                           m_rg_b_a, m_rg_w_i, m_rg_b_i, m_rg_lambda, m_rg_w_out, m_norm_mix_pre, m_norm_mix_post,
                           m_norm_ffn_pre, m_norm_ffn_post, m_ffn_w_gate, m_ffn_w_up, m_ffn_w_down)))
    v = dict(zip(WEIGHTS, (v_attn_w_in, v_attn_rel_bias, v_attn_w_out, v_rg_w_in, v_rg_conv_w, v_rg_conv_b, v_rg_w_a,
                           v_rg_b_a, v_rg_w_i, v_rg_b_i, v_rg_lambda, v_rg_w_out, v_norm_mix_pre, v_norm_mix_post,
                           v_norm_ffn_pre, v_norm_ffn_post, v_ffn_w_gate, v_ffn_w_up, v_ffn_w_down)))
    plan = _WeightGather(w)
    exch = _GradExchange()
    loss, dx, grads = _local_step(x[0], loss_target[0], plan.views(), plan, exch)
    loss = lax.psum(loss, ("x", "y", "c"))
    g = exch.finish(grads, {k: _natural(k, w[k]).shape for k in WEIGHTS})

    big = [k for k in WEIGHTS if k not in SMALL]
    upd = {}
    for k in big:
        res = _adamw("adamw_" + k, _natural(k, w[k]), g[k], _natural(k, m[k]), _natural(k, v[k]))
        upd[k] = [_natural(k, r) for r in res]
        g[k] = _natural(k, g[k])
    cat = lambda d: jnp.concatenate([d[k].reshape(-1) for k in SMALL])
    small = _adamw("adamw_small", cat(w), cat(g), cat(m), cat(v))
    off = 0
    for k in SMALL:
        n = w[k].size
        upd[k] = [r[off:off + n].reshape(w[k].shape) for r in small]
        off += n
    return (loss, dx[None], *[g[k] for k in WEIGHTS], *[upd[k][0] for k in WEIGHTS],
            *[upd[k][1] for k in WEIGHTS], *[upd[k][2] for k in WEIGHTS])
```

```python
import functools

import numpy as np
import jax
import jax.numpy as jnp
from jax import lax
from jax.experimental import pallas as pl
from jax.experimental.pallas import tpu as pltpu

f32 = jnp.float32
bf16 = jnp.bfloat16
SDS = jax.ShapeDtypeStruct
MESH = pl.DeviceIdType.MESH

D_MODEL = 1024
N_CHIPS = 4
DEPTH = 4
HEAD_DIM = 64
CHUNK = 64
N_LEFT = 8
REL_CLIP = 256
A_W = 512
LRU_BLOCKS = 4
LRU_BW = 256
LRU_C = 8.0
D_FF = 2816
RMS_EPS = 1e-6
LANES = 128
SUBLANES = 8
VMEM_LIMIT = 56 * 1024 * 1024

QB_A = 2 * CHUNK
QSUB_A = 16
KW_A = QB_A + N_LEFT * CHUNK
PAD_A = N_LEFT * CHUNK
EXT_A = 768
SB_BLK = 256
QSUB_B = 4
SB_DEAD = -110.0

ADAM_LR, ADAM_B1, ADAM_B2, ADAM_EPS, ADAM_WD, ADAM_STEP = 0.001, 0.9, 0.999, 1e-08, 0.01, 10


def _cparams(sem):
    return pltpu.CompilerParams(dimension_semantics=sem, vmem_limit_bytes=VMEM_LIMIT)


def _gemm(name, operands, in_specs, o_spec, out_shape, grid, dims, acc_shape, into=None):
    nred = grid[2]
    npair = len(operands) // 2
    nin = 2 * npair + (into is not None)

    def body(*refs):
        o_ref = refs[nin]
        p = None
        for t in range(npair):
            d = lax.dot_general(refs[2 * t][...], refs[2 * t + 1][...], (dims, ((), ())),
                                preferred_element_type=f32)
            p = d if p is None else p + d
        if nred == 1:
            o_ref[...] = p.astype(o_ref.dtype)
        else:
            acc = refs[nin + 1]
            r = pl.program_id(2)

            @pl.when(r == 0)
            def _():
                acc[...] = p

            @pl.when(r > 0)
            def _():
                acc[...] += p

            @pl.when(r == nred - 1)
            def _():
                o_ref[...] = acc[...].astype(o_ref.dtype)

    scratch = [] if nred == 1 else [pltpu.VMEM(acc_shape, f32)]
    extra, alias = ([], {}) if into is None else ([into], {2 * npair: 0})
    return pl.pallas_call(
        body, grid=grid, in_specs=list(in_specs) + [pl.BlockSpec(memory_space=pl.ANY)] * len(extra),
        out_specs=o_spec, out_shape=out_shape, scratch_shapes=scratch, name=name, input_output_aliases=alias,
        compiler_params=_cparams(("parallel", "parallel", "arbitrary")))(*operands, *extra)


LOWER_LAYERS = {"attn_w_in": 1, "attn_w_out": 1, "rg_w_in": 0, "rg_w_out": 0,
                "ffn_w_gate": 0, "ffn_w_up": 0, "ffn_w_down": 0}


def _grad_slot(name, l):
    n = LOWER_LAYERS[name]
    return (0, l) if l < n else (1, l - n)


class _Fresh:
    def __init__(self, shape):
        self.shape = tuple(shape)


def _into(buf):
    return None if isinstance(buf, _Fresh) else buf


NN = ((1,), (0,))
NT = ((1,), (1,))
TN = ((0,), (0,))


WGRAD_TOKENS = 2048


def _tile(t, want=1024):
    return min(want, t)


def _mm_cols(name, a, w, l, out_dtype):
    t, k = a.shape
    _, s, _, ns = w.shape
    tm = _tile(t, 2048)
    return _gemm(
        name, [a, w],
        [pl.BlockSpec((tm, k), lambda i, j, r: (i, 0)),
         pl.BlockSpec((None, None, k, ns), lambda i, j, r: (l, j, 0, 0))],
        pl.BlockSpec((tm, ns), lambda i, j, r: (i, j)),
        SDS((t, s * ns), out_dtype), (t // tm, s, 1), NN, None)


def _mm_cols_t(name, dy, w, l, out_dtype):
    t = dy.shape[0]
    _, s, k, ns = w.shape
    tm = _tile(t)
    ops, specs = [], []
    for r in range(s):
        ops += [dy, w]
        specs += [pl.BlockSpec((tm, ns), lambda i, j, kk, r=r: (i, r)),
                  pl.BlockSpec((None, None, k, ns), lambda i, j, kk, r=r: (l, r, 0, 0))]
    return _gemm(name, ops, specs, pl.BlockSpec((tm, k), lambda i, j, kk: (i, 0)),
                 SDS((t, k), out_dtype), (t // tm, 1, 1), NT, None)


def _mm_wgrad_cols(name, a, dy, buf, l):
    t, k = a.shape
    _, s, _, ns = buf.shape
    tt = _tile(t, 2 * WGRAD_TOKENS)
    return _gemm(
        name, [a, dy],
        [pl.BlockSpec((tt, k), lambda i, j, r: (r, 0)),
         pl.BlockSpec((tt, ns), lambda i, j, r: (r, i))],
        pl.BlockSpec((None, None, k, ns), lambda i, j, r: (l, i, 0, 0)),
        SDS(buf.shape, f32), (s, 1, t // tt), TN, (k, ns), into=_into(buf))


def _mm_rows(name, parts, w, l, out_dtype):
    t = parts[0].shape[0]
    n = w.shape[3]
    tm = _tile(t, 2048)
    ops, specs = [], []
    for p_i, a in enumerate(parts):
        kp = a.shape[1]
        ops += [a, w]
        specs += [pl.BlockSpec((tm, kp), lambda i, j, r: (i, 0)),
                  pl.BlockSpec((None, None, kp, n), lambda i, j, r, p_i=p_i: (l, 0, p_i, 0))]
    return _gemm(name, ops, specs, pl.BlockSpec((tm, n), lambda i, j, r: (i, 0)),
                 SDS((t, n), out_dtype), (t // tm, 1, 1), NN, None)


def _mm_rows_t(name, dy, w, l, out_dtype):
    t, n = dy.shape
    k = w.shape[2]
    tm = _tile(t, 2048)
    return _gemm(
        name, [dy, w],
        [pl.BlockSpec((tm, n), lambda i, j, r: (i, 0)),
         pl.BlockSpec((None, None, k, n), lambda i, j, r: (l, 0, 0, 0))],
        pl.BlockSpec((tm, k), lambda i, j, r: (i, 0)),
        SDS((t, k), out_dtype), (t // tm, 1, 1), NT, None)


def _mm_wgrad(name, a, dy, buf, l, part=0):
    t, k = a.shape
    n = dy.shape[1]
    tt = _tile(t, 2 * WGRAD_TOKENS)
    return _gemm(
        name, [a, dy],
        [pl.BlockSpec((tt, k), lambda i, j, r: (r, 0)),
         pl.BlockSpec((tt, n), lambda i, j, r: (r, 0))],
        pl.BlockSpec((None, k, n), lambda i, j, r: (l, part, 0)),
        SDS(buf.shape, f32), (1, 1, t // tt), TN, (k, n), into=_into(buf))


def _ffn_up(h, wg, wu, l, gather):
    t, k = h.shape
    s, fs = wg.shape[1], wg.shape[2]
    tm = _tile(t)

    def body(h_ref, wg_ref, wu_ref, g_ref, u_ref, hid_ref):
        hv = h_ref[...]
        g = lax.dot_general(hv, wg_ref[...], (NT, ((), ())), preferred_element_type=f32)
        u = lax.dot_general(hv, wu_ref[...], (NT, ((), ())), preferred_element_type=f32)
        g_ref[...] = g.astype(bf16)
        u_ref[...] = u.astype(bf16)
        hid_ref[...] = (g * jax.nn.sigmoid(g) * u).astype(bf16)

    wspec = pl.BlockSpec((None, None, fs, k), lambda j, i: (l, j, 0, 0))
    ospec = pl.BlockSpec((None, tm, fs), lambda j, i: (j, i, 0))
    return _call(
        body, [h, wg, wu], grid=(s, t // tm), name="ffn_up",
        in_specs=[pl.BlockSpec((tm, k), lambda j, i: (i, 0)), wspec, wspec],
        out_specs=[ospec, ospec, ospec], out_shape=[SDS((s, t, fs), bf16)] * 3,
        sem=("parallel", "parallel"), gather=gather)


def _ffn_down(hid, wd, l):
    s, t, fs = hid.shape
    n = wd.shape[3]
    tm = _tile(t)
    ops, specs = [], []
    for r in range(s):
        ops += [hid, wd]
        specs += [pl.BlockSpec((None, tm, fs), lambda i, j, k, r=r: (r, i, 0)),
                  pl.BlockSpec((None, None, fs, n), lambda i, j, k, r=r: (l, r, 0, 0))]
    return _gemm("ffn_down", ops, specs, pl.BlockSpec((tm, n), lambda i, j, k: (i, 0)),
                 SDS((t, n), f32), (t // tm, 1, 1), NN, None)


def _ffn_down_bwd(df, wd, l, g, u):
    t, n = df.shape
    s, fs = wd.shape[1], wd.shape[2]
    tm = _tile(t)

    def body(df_ref, wd_ref, g_ref, u_ref, dg_ref, du_ref):
        dh = lax.dot_general(df_ref[...], wd_ref[...], (NT, ((), ())), preferred_element_type=f32)
        gv = g_ref[...].astype(f32)
        uv = u_ref[...].astype(f32)
        sg = jax.nn.sigmoid(gv)
        du_ref[...] = (dh * gv * sg).astype(bf16)
        dg_ref[...] = (dh * uv * (sg * (1.0 + gv * (1.0 - sg)))).astype(bf16)

    bspec = pl.BlockSpec((None, tm, fs), lambda j, i: (j, i, 0))
    return pl.pallas_call(
        body, grid=(s, t // tm), name="ffn_down_bwd",
        in_specs=[pl.BlockSpec((tm, n), lambda j, i: (i, 0)),
                  pl.BlockSpec((None, None, fs, n), lambda j, i: (l, j, 0, 0)), bspec, bspec],
        out_specs=[bspec, bspec], out_shape=[SDS((s, t, fs), bf16)] * 2,
        compiler_params=_cparams(("parallel", "parallel")))(df, wd, g, u)


def _ffn_up_bwd(dg, du, wg, wu, l):
    s, t, fs = dg.shape
    k = wg.shape[3]
    tm = _tile(t, 512)
    ops, specs = [], []
    for r in range(s):
        aspec = pl.BlockSpec((None, tm, fs), lambda i, j, kk, r=r: (r, i, 0))
        wspec = pl.BlockSpec((None, None, fs, k), lambda i, j, kk, r=r: (l, r, 0, 0))
        ops += [dg, wg, du, wu]
        specs += [aspec, wspec, aspec, wspec]
    return _gemm("ffn_up_bwd", ops, specs, pl.BlockSpec((tm, k), lambda i, j, kk: (i, 0)),
                 SDS((t, k), f32), (t // tm, 1, 1), NN, None)


def _ffn_wgrad_up(h, dg, du, buf_g, buf_u, l):
    t, k = h.shape
    s, _, fs = dg.shape
    tt = _tile(t, WGRAD_TOKENS)
    nred = t // tt

    fresh = isinstance(buf_g, _Fresh)

    def body(*refs):
        h_ref, dg_ref, du_ref = refs[:3]
        og_ref, ou_ref, acc_g, acc_u = refs[-4:]
        r = pl.program_id(1)
        hv = h_ref[...]
        pg = lax.dot_general(dg_ref[...], hv, (TN, ((), ())), preferred_element_type=f32)
        pu = lax.dot_general(du_ref[...], hv, (TN, ((), ())), preferred_element_type=f32)

        @pl.when(r == 0)
        def _():
            acc_g[...] = pg
            acc_u[...] = pu

        @pl.when(r > 0)
        def _():
            acc_g[...] += pg
            acc_u[...] += pu

        @pl.when(r == nred - 1)
        def _():
            og_ref[...] = acc_g[...]
            ou_ref[...] = acc_u[...]

    dspec = pl.BlockSpec((None, tt, fs), lambda i, r: (i, r, 0))
    ospec = pl.BlockSpec((None, None, fs, k), lambda i, r: (l, i, 0, 0))
    extra, alias = ([], {}) if fresh else ([buf_g, buf_u], {3: 0, 4: 1})
    return pl.pallas_call(
        body, grid=(s, nred), name="ffn_wgrad_up",
        in_specs=[pl.BlockSpec((tt, k), lambda i, r: (r, 0)), dspec, dspec] + [ANY] * len(extra),
        out_specs=[ospec, ospec], out_shape=[SDS(buf_g.shape, f32), SDS(buf_u.shape, f32)],
        scratch_shapes=[pltpu.VMEM((fs, k), f32)] * 2, input_output_aliases=alias,
        compiler_params=_cparams(("parallel", "arbitrary")))(h, dg, du, *extra)


def _ffn_wgrad_down(hid, df, buf, l):
    s, t, fs = hid.shape
    n = df.shape[1]
    tt = _tile(t, 2 * WGRAD_TOKENS)
    return _gemm(
        "ffn_wgrad_down", [hid, df],
        [pl.BlockSpec((None, tt, fs), lambda i, j, r: (i, r, 0)),
         pl.BlockSpec((tt, n), lambda i, j, r: (r, 0))],
        pl.BlockSpec((None, None, fs, n), lambda i, j, r: (l, i, 0, 0)),
        SDS(buf.shape, f32), (s, 1, t // tt), TN, (fs, n), into=_into(buf))


def _rows(name, fn, rows, consts, row_outs, acc_outs=(), tr=512):
    rows = [r if isinstance(r, tuple) else (r, r.shape[1], 0) for r in rows]
    t = rows[0][0].shape[0]
    tr = max(d for d in range(SUBLANES, min(tr, t) + 1, SUBLANES) if t % d == 0)
    nin = len(rows) + len(consts)
    no, na = len(row_outs), len(acc_outs)

    def body(*refs):
        vals = fn(*[r[...] for r in refs[:nin]])
        if not isinstance(vals, (tuple, list)):
            vals = (vals,)
        for k in range(no):
            refs[nin + k][...] = vals[k].astype(refs[nin + k].dtype)
        first = pl.program_id(0) == 0
        for k in range(na):
            ref, val = refs[nin + no + k], vals[no + k]

            @pl.when(first)
            def _(ref=ref, val=val):
                ref[...] = val

            @pl.when(jnp.logical_not(first))
            def _(ref=ref, val=val):
                ref[...] += val

    in_specs = [pl.BlockSpec((tr, w), lambda i, cb=cb: (i, cb)) for (_, w, cb) in rows]
    in_specs += [pl.BlockSpec(c.shape, lambda i, nd=c.ndim: (0,) * nd) for c in consts]
    out_specs = [pl.BlockSpec((tr, w), lambda i: (i, 0)) for (w, _) in row_outs]
    out_specs += [pl.BlockSpec(s, lambda i, nd=len(s): (0,) * nd) for (s, _) in acc_outs]
    out_shape = [SDS((t, w), dt) for (w, dt) in row_outs] + [SDS(s, dt) for (s, dt) in acc_outs]
    res = pl.pallas_call(
        body, grid=(t // tr,), in_specs=in_specs, out_specs=out_specs, out_shape=out_shape,
        name=name, compiler_params=_cparams(("arbitrary",)))(*[r[0] for r in rows], *consts)
    return res


def _rstd(x):
    return lax.rsqrt(jnp.mean(x * x, axis=-1, keepdims=True) + RMS_EPS)


def _norm_fwd(x, g):
    return x * _rstd(x) * g


def _norm_bwd(u, dy, g):
    r = _rstd(u)
    n = u * r
    dn = dy * g
    du = r * (dn - n * jnp.mean(dn * n, axis=-1, keepdims=True))
    return du, jnp.sum(dy * n, axis=0, keepdims=True)


def _gelu(x):
    c = 0.7978845608028654
    return 0.5 * x * (1.0 + jnp.tanh(c * (x + 0.044715 * x * x * x)))


def _gelu_grad(x):
    c = 0.7978845608028654
    th = jnp.tanh(c * (x + 0.044715 * x * x * x))
    return 0.5 * (1.0 + th) + 0.5 * x * (1.0 - th * th) * c * (1.0 + 3.0 * 0.044715 * x * x)


def _mask_heads(x):
    lane = lax.broadcasted_iota(jnp.int32, x.shape, 1)
    return [jnp.where((lane >= h * HEAD_DIM) & (lane < (h + 1) * HEAD_DIM), x, jnp.zeros_like(x))
            for h in range(LANES // HEAD_DIM)]


def _chunk_valid(start):
    qi = lax.broadcasted_iota(jnp.int32, (QB_A, KW_A), 0)
    kj = lax.broadcasted_iota(jnp.int32, (QB_A, KW_A), 1)
    qc = qi // CHUNK
    kc = kj // CHUNK
    return (kc >= qc) & (kc <= qc + N_LEFT) & (kj + start >= PAD_A)


def _scaled(q):
    return q * (HEAD_DIM ** -0.5)


def _chunk_probs(q, k, bias, valid):
    s = lax.dot_general(q, k, (NT, ((), ())), preferred_element_type=f32) + bias
    s = jnp.where(valid, s, -1e30)
    p = jnp.exp(s - jnp.max(s, axis=-1, keepdims=True))
    return p / jnp.sum(p, axis=-1, keepdims=True)


def _chunk_attn_fwd(proj, kpad, vpad, bias, gather):
    t = proj.shape[0]
    tp = kpad.shape[0]
    step = QSUB_A * QB_A

    def body(q_ref, k_ref, v_ref, b_ref, o_ref):
        for sb in range(QSUB_A):
            start = pl.multiple_of((pl.program_id(1) * QSUB_A + sb) * QB_A, QB_A)
            rows = pl.ds(sb * QB_A, QB_A)
            valid = _chunk_valid(start)
            kw = k_ref[pl.ds(start, KW_A), :]
            qm = _mask_heads(_scaled(q_ref[rows, :]))
            vm = _mask_heads(v_ref[pl.ds(start, KW_A), :])
            o = None
            for h in range(len(qm)):
                p = _chunk_probs(qm[h], kw, b_ref[h], valid)
                d = jnp.dot(p.astype(bf16), vm[h], preferred_element_type=f32)
                o = d if o is None else o + d
            o_ref[rows, :] = o.astype(bf16)

    kv_spec = pl.BlockSpec((tp, LANES), lambda hp, qb: (0, hp))
    outs, new = _call(
        body, [proj, kpad, vpad, bias], grid=(A_W // LANES, t // step), name="chunk_attn_fwd",
        in_specs=[pl.BlockSpec((step, LANES), lambda hp, qb: (qb, hp)), kv_spec, kv_spec,
                  pl.BlockSpec((2, QB_A, KW_A), lambda hp, qb: (hp, 0, 0))],
        out_specs=[pl.BlockSpec((step, LANES), lambda hp, qb: (qb, hp))],
        out_shape=[SDS((t, A_W), bf16)], sem=("parallel", "arbitrary"), gather=gather)
    return outs[0], new


def _chunk_attn_bwd(proj, kpad, vpad, bias, dout, gather):
    t = proj.shape[0]
    tp = kpad.shape[0]
    step = QSUB_A * QB_A

    def body(q_ref, k_ref, v_ref, b_ref, do_ref, dq_ref, dk_ref, dv_ref, db_ref):
        qb = pl.program_id(1)

        @pl.when(qb == 0)
        def _():
            dk_ref[...] = jnp.zeros_like(dk_ref)
            dv_ref[...] = jnp.zeros_like(dv_ref)
            db_ref[...] = jnp.zeros_like(db_ref)

        for sb in range(QSUB_A):
            start = pl.multiple_of((qb * QSUB_A + sb) * QB_A, QB_A)
            rows = pl.ds(sb * QB_A, QB_A)
            win = pl.ds(start, KW_A)
            valid = _chunk_valid(start)
            kw = k_ref[win, :]
            vw = v_ref[win, :]
            qm = _mask_heads(_scaled(q_ref[rows, :]))
            dom = _mask_heads(do_ref[rows, :])
            km = _mask_heads(kw)
            dq = dk = dv = None
            for h in range(len(qm)):
                p = _chunk_probs(qm[h], kw, b_ref[h], valid)
                dp = lax.dot_general(dom[h], vw, (NT, ((), ())), preferred_element_type=f32)
                ds = p * (dp - jnp.sum(dp * p, axis=-1, keepdims=True))
                db_ref[h] += ds
                dsb = ds.astype(bf16)
                terms = (jnp.dot(dsb, km[h], preferred_element_type=f32),
                         lax.dot_general(dsb, qm[h], (TN, ((), ())), preferred_element_type=f32),
                         lax.dot_general(p.astype(bf16), dom[h], (TN, ((), ())), preferred_element_type=f32))
                dq, dk, dv = terms if dq is None else (dq + terms[0], dk + terms[1], dv + terms[2])
            dq_ref[rows, :] = _scaled(dq).astype(bf16)
            dk_ref[win, :] += dk
            dv_ref[win, :] += dv

    kv_spec = pl.BlockSpec((tp, LANES), lambda hp, qb: (0, hp))
    q_spec = pl.BlockSpec((step, LANES), lambda hp, qb: (qb, hp))
    b_spec = pl.BlockSpec((2, QB_A, KW_A), lambda hp, qb: (hp, 0, 0))
    return _call(
        body, [proj, kpad, vpad, bias, dout], grid=(A_W // LANES, t // step), name="chunk_attn_bwd",
        in_specs=[q_spec, kv_spec, kv_spec, b_spec, q_spec],
        out_specs=[q_spec, kv_spec, kv_spec, b_spec],
        out_shape=[SDS((t, A_W), bf16), SDS((tp, A_W), f32), SDS((tp, A_W), f32),
                   SDS((2 * A_W // LANES, QB_A, KW_A), f32)],
        sem=("parallel", "arbitrary"), gather=gather)


def _bias_ext(table):
    flat = PAD_A + QB_A - 1 - REL_CLIP
    top = jnp.broadcast_to(table[:, 2 * REL_CLIP:], (table.shape[0], flat))
    lo = 2 * REL_CLIP - (EXT_A - 1 - flat)
    return jnp.concatenate([top, jnp.flip(table[:, lo:], axis=1)], axis=1)


def _bias_window(table):
    nh = table.shape[0]
    e = jnp.broadcast_to(_bias_ext(table)[:, None, :], (nh, QB_A, EXT_A)).reshape(nh, QB_A * EXT_A)
    m = e[:, :QB_A * (EXT_A - 1)].reshape(nh, QB_A, EXT_A - 1)
    return m[:, :, QB_A - 1:]


def _bias_window_grad(dbias):
    nh = dbias.shape[0]
    m = jnp.pad(dbias, ((0, 0), (0, 0), (QB_A - 1, 0))).reshape(nh, QB_A * (EXT_A - 1))
    dext = jnp.sum(jnp.pad(m, ((0, 0), (0, QB_A))).reshape(nh, QB_A, EXT_A), axis=1)
    flat = PAD_A + QB_A - 1 - REL_CLIP
    lo = 2 * REL_CLIP - (EXT_A - 1 - flat)
    tail = jnp.flip(dext[:, flat:], axis=1)
    tail = tail.at[:, -1].add(jnp.sum(dext[:, :flat], axis=1))
    return jnp.pad(tail, ((0, 0), (lo, 0)))


def _tri_suffix(x, tri):
    hi = x.astype(bf16)
    lo = (x - hi.astype(f32)).astype(bf16)
    return jnp.dot(hi, tri, preferred_element_type=f32) + jnp.dot(lo, tri, preferred_element_type=f32)


def _sb_block(q, k, run, tri, causal):
    z = lax.dot_general(q, k, (NT, ((), ())), preferred_element_type=f32)
    e = jnp.exp(-jnp.abs(z))
    l1p = jnp.log(1.0 + e)
    lb = jnp.minimum(z, 0.0) - l1p
    lmb = lb - z
    if causal is not None:
        lmb = jnp.where(causal, lmb, 0.0)
    cs = _tri_suffix(lmb, tri)
    w = jnp.exp(lb + (run + cs - lmb))
    if causal is not None:
        w = jnp.where(causal, w, 0.0)
    return z, e, w, run + cs[:, 0:1]


def _sb_tri():
    r = lax.broadcasted_iota(jnp.int32, (SB_BLK, SB_BLK), 0)
    c = lax.broadcasted_iota(jnp.int32, (SB_BLK, SB_BLK), 1)
    return (r >= c).astype(bf16), c < r


def _sb_live(runs):
    m = runs[0]
    for r in runs[1:]:
        m = jnp.maximum(m, r)
    return jnp.max(m) > SB_DEAD


def _sb_fwd(proj, gather):
    t = proj.shape[0]
    cb = A_W // LANES
    nh = LANES // HEAD_DIM

    step_rows = QSUB_B * SB_BLK

    def body(q_ref, k_ref, v_ref, o_ref, of_ref):
        tri, diag = _sb_tri()
        for sb in range(QSUB_B):
            _sb_fwd_block(pl.program_id(1) * QSUB_B + sb, pl.ds(sb * SB_BLK, SB_BLK), tri, diag,
                          q_ref, k_ref, v_ref, o_ref, of_ref)

    def _sb_fwd_block(qb, qrows, tri, diag, q_ref, k_ref, v_ref, o_ref, of_ref):
        qm = _mask_heads(_scaled(q_ref[qrows, :]))

        def pair(kb, carry, causal):
            rows = pl.ds(pl.multiple_of(kb * SB_BLK, SB_BLK), SB_BLK)
            k = k_ref[rows, :]
            vm = _mask_heads(v_ref[rows, :])
            runs, acc = [], carry[nh]
            for h in range(nh):
                _, _, w, run = _sb_block(qm[h], k, carry[h], tri, causal)
                acc = acc + jnp.dot(w.astype(bf16), vm[h], preferred_element_type=f32)
                runs.append(run)
            return (*runs, acc)

        zero = jnp.zeros((SB_BLK, 1), f32)
        carry = pair(qb, (zero,) * nh + (jnp.zeros((SB_BLK, LANES), f32),), diag)

        def cond(st):
            return (st[0] < qb) & _sb_live(st[1][:nh])

        def step(st):
            return st[0] + 1, pair(qb - 1 - st[0], st[1], None)

        _, carry = lax.while_loop(cond, step, (jnp.int32(0), carry))
        o_ref[qrows, :] = carry[nh].astype(bf16)
        of_ref[qrows, :] = carry[nh]

    ospec = pl.BlockSpec((step_rows, LANES), lambda hp, qb: (qb, hp))
    return _call(
        body, [proj, proj, proj], grid=(cb, t // step_rows), name="sb_attn_fwd",
        in_specs=[pl.BlockSpec((step_rows, LANES), lambda hp, qb: (qb, 3 * cb + hp)),
                  pl.BlockSpec((t, LANES), lambda hp, qb: (0, 4 * cb + hp)),
                  pl.BlockSpec((t, LANES), lambda hp, qb: (0, 5 * cb + hp))],
        out_specs=[ospec, ospec], out_shape=[SDS((t, A_W), bf16), SDS((t, A_W), f32)],
        sem=("parallel", "arbitrary"), gather=gather)


def _sb_bwd(proj, out_b, dout, gather):
    t = proj.shape[0]
    cb = A_W // LANES
    nh = LANES // HEAD_DIM

    step_rows = QSUB_B * SB_BLK

    def body(q_ref, k_ref, v_ref, o_ref, do_ref, dq_ref, dk_ref, dv_ref):
        tri, diag = _sb_tri()

        @pl.when(pl.program_id(1) == 0)
        def _():
            dk_ref[...] = jnp.zeros_like(dk_ref)
            dv_ref[...] = jnp.zeros_like(dv_ref)

        for sb in range(QSUB_B):
            _sb_bwd_block(pl.program_id(1) * QSUB_B + sb, pl.ds(sb * SB_BLK, SB_BLK), tri, diag,
                          q_ref, k_ref, v_ref, o_ref, do_ref, dq_ref, dk_ref, dv_ref)

    def _sb_bwd_block(qb, qrows, tri, diag, q_ref, k_ref, v_ref, o_ref, do_ref, dq_ref, dk_ref, dv_ref):
        qm = _mask_heads(_scaled(q_ref[qrows, :]))
        do = do_ref[qrows, :]
        dom = _mask_heads(do)
        dsums = [jnp.sum(t_, axis=-1, keepdims=True) for t_ in _mask_heads(do.astype(f32) * o_ref[qrows, :])]

        def pair(kb, carry, causal):
            rows = pl.ds(pl.multiple_of(kb * SB_BLK, SB_BLK), SB_BLK)
            k = k_ref[rows, :]
            v = v_ref[rows, :]
            km = _mask_heads(k)
            new, dq, dk, dv = [], carry[2 * nh], None, None
            for h in range(nh):
                z, e, w, run = _sb_block(qm[h], k, carry[2 * h], tri, causal)
                inv = 1.0 / (1.0 + e)
                beta = jnp.where(z >= 0.0, inv, e * inv)
                wb = w.astype(bf16)
                g = lax.dot_general(dom[h], v, (NT, ((), ())), preferred_element_type=f32) * wb.astype(f32)
                sg = _tri_suffix(g, tri)
                dz = g - (g + (dsums[h] - carry[2 * h + 1] - sg)) * beta
                if causal is not None:
                    dz = jnp.where(causal, dz, 0.0)
                dzb = dz.astype(bf16)
                dq = dq + jnp.dot(dzb, km[h], preferred_element_type=f32)
                tk = lax.dot_general(dzb, qm[h], (TN, ((), ())), preferred_element_type=f32)
                tv = lax.dot_general(wb, dom[h], (TN, ((), ())), preferred_element_type=f32)
                dk, dv = (tk, tv) if dk is None else (dk + tk, dv + tv)
                new += [run, carry[2 * h + 1] + sg[:, 0:1]]
            dk_ref[rows, :] += dk
            dv_ref[rows, :] += dv
            return (*new, dq)

        zero = jnp.zeros((SB_BLK, 1), f32)
        carry = pair(qb, (zero,) * (2 * nh) + (jnp.zeros((SB_BLK, LANES), f32),), diag)

        def cond(st):
            return (st[0] < qb) & _sb_live(st[1][0:2 * nh:2])

        def step(st):
            return st[0] + 1, pair(qb - 1 - st[0], st[1], None)

        _, carry = lax.while_loop(cond, step, (jnp.int32(0), carry))
        dq_ref[qrows, :] = _scaled(carry[2 * nh]).astype(bf16)

    kv_in = lambda seg: pl.BlockSpec((t, LANES), lambda hp, qb: (0, seg * cb + hp))
    q_spec = pl.BlockSpec((step_rows, LANES), lambda hp, qb: (qb, hp))
    kv_out = pl.BlockSpec((t, LANES), lambda hp, qb: (0, hp))
    return _call(
        body, [proj, proj, proj, out_b, dout], grid=(cb, t // step_rows), name="sb_attn_bwd",
        in_specs=[pl.BlockSpec((step_rows, LANES), lambda hp, qb: (qb, 3 * cb + hp)), kv_in(4), kv_in(5),
                  q_spec, pl.BlockSpec((step_rows, LANES), lambda hp, qb: (qb, cb + hp))],
        out_specs=[q_spec, kv_out, kv_out],
        out_shape=[SDS((t, A_W), bf16), SDS((t, A_W), f32), SDS((t, A_W), f32)],
        sem=("parallel", "arbitrary"), gather=gather)


def _halo_specs(tr, w, col, nblk):
    per = tr // SUBLANES
    cur = pl.BlockSpec((tr, w), lambda i: (i, col))
    prev = pl.BlockSpec((SUBLANES, w), lambda i: (jnp.maximum(i * per - 1, 0), col))
    nxt = pl.BlockSpec((SUBLANES, w), lambda i: (jnp.minimum((i + 1) * per, nblk * per - 1), col))
    return cur, prev, nxt


def _taps_before(cur, prev8, first):
    prev8 = jnp.where(first, 0.0, prev8)
    ext = jnp.concatenate([prev8, cur], axis=0)
    return [pltpu.roll(ext, s, 0)[SUBLANES:] for s in (3, 2, 1)]


def _taps_after(cur, next8, last):
    n = cur.shape[0]
    next8 = jnp.where(last, 0.0, next8)
    ext = jnp.concatenate([cur, next8], axis=0)
    return [pltpu.roll(ext, n + SUBLANES - s, 0)[:n] for s in (1, 2, 3)]


def _block_diag(x, w_ref, dims):
    outs = [lax.dot_general(x[:, n * LRU_BW:(n + 1) * LRU_BW], w_ref[n], (dims, ((), ())),
                            preferred_element_type=f32) for n in range(LRU_BLOCKS)]
    return jnp.concatenate(outs, axis=1)


def _lru_gates(xc, wa_ref, wi_ref, ba, bi, lam):
    xb = xc.astype(bf16)
    r = jax.nn.sigmoid(_block_diag(xb, wa_ref, NN) + ba)
    ig = jax.nn.sigmoid(_block_diag(xb, wi_ref, NN) + bi)
    sp = jnp.maximum(-lam, 0.0) + jnp.log(1.0 + jnp.exp(-jnp.abs(lam)))
    log_a = -LRU_C * r * sp
    a = jnp.exp(log_a)
    x2 = 2.0 * log_a
    one_minus = jnp.where(x2 > -1e-2, -x2 * (1.0 + x2 * (0.5 + x2 * (1.0 / 6.0))), 1.0 - a * a)
    mult = jnp.sqrt(one_minus)
    return xb, r, ig, sp, a, mult


def _rg_gates_fwd(proj, conv_w, conv_b, wa, wi, ba, bi, lam, tr=512):
    t = proj.shape[0]
    w = D_MODEL
    tr = min(tr, t)
    nblk = t // tr
    cur, prev, _ = _halo_specs(tr, w, 1, nblk)

    def body(x_ref, xp_ref, cw_ref, cb_ref, wa_ref, wi_ref, ba_ref, bi_ref, lam_ref, xc_ref, a_ref, u_ref):
        x = x_ref[...]
        taps = _taps_before(x, xp_ref[...], pl.program_id(0) == 0) + [x]
        xc = cb_ref[...]
        for k in range(4):
            xc = xc + cw_ref[k:k + 1, :] * taps[k]
        _, _, ig, _, a, mult = _lru_gates(xc, wa_ref, wi_ref, ba_ref[...], bi_ref[...], lam_ref[...])
        xc_ref[...] = xc
        a_ref[...] = a
        u_ref[...] = mult * (ig * xc)

    full = lambda a_: pl.BlockSpec(a_.shape, lambda i, nd=a_.ndim: (0,) * nd)
    ospec = pl.BlockSpec((tr, w), lambda i: (i, 0))
    return pl.pallas_call(
        body, grid=(nblk,), name="rg_gates_fwd",
        in_specs=[cur, prev] + [full(a_) for a_ in (conv_w, conv_b, wa, wi, ba, bi, lam)],
        out_specs=[ospec] * 3, out_shape=[SDS((t, w), f32)] * 3,
        compiler_params=_cparams(("parallel",)))(proj, proj, conv_w, conv_b, wa, wi, ba, bi, lam)


def _lru_scan(name, a, b, reverse, tt=1024):
    t, w = a.shape
    tt = min(tt, t)
    nt = t // tt
    ng = tt // SUBLANES

    def body(a_ref, b_ref, h_ref, carry_ref):
        @pl.when(pl.program_id(0) == 0)
        def _():
            carry_ref[...] = jnp.zeros_like(carry_ref)

        row = lax.broadcasted_iota(jnp.int32, (SUBLANES, w), 0)

        def group(gi, carry):
            g = (ng - 1 - gi) if reverse else gi
            rows = pl.ds(pl.multiple_of(g * SUBLANES, SUBLANES), SUBLANES)
            av = a_ref[rows, :]
            bv = b_ref[rows, :]
            for s in (1, 2, 4):
                sh = (SUBLANES - s) if reverse else s
                ok = (row < SUBLANES - s) if reverse else (row >= s)
                a_s = pltpu.roll(av, sh, 0)
                b_s = pltpu.roll(bv, sh, 0)
                bv = jnp.where(ok, av * b_s + bv, bv)
                av = jnp.where(ok, av * a_s, av)
            h = av * carry + bv
            h_ref[rows, :] = h
            edge = h[0:1, :] if reverse else h[SUBLANES - 1:SUBLANES, :]
            return jnp.broadcast_to(edge, (SUBLANES, w))

        carry_ref[...] = lax.fori_loop(0, ng, group, carry_ref[...], unroll=4)

    tmap = (lambda i: (nt - 1 - i, 0)) if reverse else (lambda i: (i, 0))
    spec = pl.BlockSpec((tt, w), tmap)
    return pl.pallas_call(
        body, grid=(nt,), name=name, in_specs=[spec, spec], out_specs=spec,
        out_shape=SDS((t, w), f32), scratch_shapes=[pltpu.VMEM((SUBLANES, w), f32)],
        compiler_params=_cparams(("arbitrary",)))(a, b)


def _rg_gates_bwd(dhs, c, hs, xc, wa, wi, ba, bi, lam, tr=512):
    t, w = xc.shape
    tr = min(tr, t)
    nblk = t // tr
    cur, prev, nxt = _halo_specs(tr, w, 0, nblk)

    def body(dhs_ref, c_ref, cn_ref, hs_ref, hp_ref, xc_ref, wa_ref, wi_ref, ba_ref, bi_ref, lam_ref,
             dxc_ref, dwa_ref, dwi_ref, dba_ref, dbi_ref, dlam_ref):
        i = pl.program_id(0)
        c_next = _taps_after(c_ref[...], cn_ref[...], i == nblk - 1)[0]
        h_prev = _taps_before(hs_ref[...], hp_ref[...], i == 0)[2]
        xc = xc_ref[...]
        lam = lam_ref[...]
        xb, r, ig, sp, a, mult = _lru_gates(xc, wa_ref, wi_ref, ba_ref[...], bi_ref[...], lam)
        dh = dhs_ref[...] + c_next
        dlog_a = dh * h_prev * a - (dh * ig * xc) * (a * a / mult)
        dpre_a = (dlog_a * (-LRU_C * sp) * r * (1.0 - r)).astype(bf16)
        dpre_i = (dh * mult * xc * ig * (1.0 - ig)).astype(bf16)
        dxc_ref[...] = (dh * mult * ig + _block_diag(dpre_a, wa_ref, NT) + _block_diag(dpre_i, wi_ref, NT))
        dsig = 1.0 / (1.0 + jnp.exp(lam))
        sums = [jnp.sum(dpre_a.astype(f32), axis=0, keepdims=True),
                jnp.sum(dpre_i.astype(f32), axis=0, keepdims=True),
                jnp.sum(dlog_a * (-LRU_C * r), axis=0, keepdims=True) * (-dsig)]

        @pl.when(i == 0)
        def _():
            dwa_ref[...] = jnp.zeros_like(dwa_ref)
            dwi_ref[...] = jnp.zeros_like(dwi_ref)
            dba_ref[...] = jnp.zeros_like(dba_ref)
            dbi_ref[...] = jnp.zeros_like(dbi_ref)
            dlam_ref[...] = jnp.zeros_like(dlam_ref)

        for n in range(LRU_BLOCKS):
            sl = slice(n * LRU_BW, (n + 1) * LRU_BW)
            dwa_ref[n] += lax.dot_general(xb[:, sl], dpre_a[:, sl], (TN, ((), ())), preferred_element_type=f32)
            dwi_ref[n] += lax.dot_general(xb[:, sl], dpre_i[:, sl], (TN, ((), ())), preferred_element_type=f32)
        dba_ref[...] += sums[0]
        dbi_ref[...] += sums[1]
        dlam_ref[...] += sums[2]

    full = lambda a_: pl.BlockSpec(a_.shape, lambda i, nd=a_.ndim: (0,) * nd)
    vec = pl.BlockSpec((1, w), lambda i: (0, 0))
    mat = pl.BlockSpec((LRU_BLOCKS, LRU_BW, LRU_BW), lambda i: (0, 0, 0))
    return pl.pallas_call(
        body, grid=(nblk,), name="rg_gates_bwd",
        in_specs=[cur, cur, nxt, cur, prev, cur] + [full(a_) for a_ in (wa, wi, ba, bi, lam)],
        out_specs=[cur, mat, mat, vec, vec, vec],
        out_shape=[SDS((t, w), f32), SDS((LRU_BLOCKS, LRU_BW, LRU_BW), f32), SDS((LRU_BLOCKS, LRU_BW, LRU_BW), f32),
                   SDS((1, w), f32), SDS((1, w), f32), SDS((1, w), f32)],
        compiler_params=_cparams(("arbitrary",)))(dhs, c, c, hs, hs, xc, wa, wi, ba, bi, lam)


def _rg_conv_bwd(dxc, proj, conv_w, tr=512):
    t, w = dxc.shape
    tr = min(tr, t)
    nblk = t // tr
    cur, _, nxt = _halo_specs(tr, w, 0, nblk)
    xcur, xprev, _ = _halo_specs(tr, w, 1, nblk)

    def body(d_ref, dn_ref, x_ref, xp_ref, cw_ref, dx_ref, dcw_ref, dcb_ref):
        i = pl.program_id(0)
        d = d_ref[...]
        x = x_ref[...]
        after = _taps_after(d, dn_ref[...], i == nblk - 1)
        before = _taps_before(x, xp_ref[...], i == 0) + [x]
        dx = cw_ref[3:4, :] * d
        for s in (1, 2, 3):
            dx = dx + cw_ref[3 - s:4 - s, :] * after[s - 1]
        dx_ref[...] = dx.astype(bf16)
        dcw = jnp.concatenate([jnp.sum(d * before[k], axis=0, keepdims=True) for k in range(4)], axis=0)
        dcb = jnp.sum(d, axis=0, keepdims=True)

        @pl.when(i == 0)
        def _():
            dcw_ref[...] = dcw
            dcb_ref[...] = dcb

        @pl.when(i > 0)
        def _():
            dcw_ref[...] += dcw
            dcb_ref[...] += dcb

    return pl.pallas_call(
        body, grid=(nblk,), name="rg_conv_bwd",
        in_specs=[cur, nxt, xcur, xprev, pl.BlockSpec((4, w), lambda i: (0, 0))],
        out_specs=[cur, pl.BlockSpec((4, w), lambda i: (0, 0)), pl.BlockSpec((1, w), lambda i: (0, 0))],
        out_shape=[SDS((t, w), bf16), SDS((4, w), f32), SDS((1, w), f32)],
        compiler_params=_cparams(("arbitrary",)))(dxc, dxc, proj, proj, conv_w)


def _attn_fwd(h, wts, j, plan):
    proj = _mm_cols("attn_in", h, wts["attn_w_in"], j, bf16)
    kpad = jnp.pad(proj[:, A_W:2 * A_W], ((PAD_A, 0), (0, 0)))
    vpad = jnp.pad(proj[:, 2 * A_W:3 * A_W], ((PAD_A, 0), (0, 0)))
    bias = _bias_window(wts["attn_rel_bias"][j])
    plan = plan if j == 0 else None
    out_a = _carried(plan, "chunk_attn_fwd", wts, _chunk_attn_fwd, proj, kpad, vpad, bias)
    out_b, out_b32 = _carried(plan, "sb_attn_fwd", wts, _sb_fwd, proj)
    m = _mm_rows("attn_out", [out_a, out_b], wts["attn_w_out"], j, f32)
    return m, (proj, kpad, vpad, bias, out_a, out_b, out_b32)


def _attn_bwd(dm, h, saved, wts, j, grads, exch):
    proj, kpad, vpad, bias, out_a, out_b, out_b32 = saved
    dout = _mm_rows_t("attn_out_t", dm, wts["attn_w_out"], j, bf16)
    gi, ll = _grad_slot("attn_w_out", j)
    grads["attn_w_out"][gi] = _mm_wgrad("attn_out_wgrad_a", out_a, dm, grads["attn_w_out"][gi], ll, 0)
    grads["attn_w_out"][gi] = _mm_wgrad("attn_out_wgrad_b", out_b, dm, grads["attn_w_out"][gi], ll, 1)
    if exch is not None and j == 0:
        (dqa, dka, dva, dbias), got = _chunk_attn_bwd(proj, kpad, vpad, bias, dout, exch.upper_carry(grads))
        exch.upper_got(got)
        (dqs, dks, dvs), slots = _sb_bwd(proj, out_b32, dout, exch.carry())
        exch.carried(slots)
    else:
        dqa, dka, dva, dbias = _chunk_attn_bwd(proj, kpad, vpad, bias, dout, None)[0]
        dqs, dks, dvs = _sb_bwd(proj, out_b32, dout, None)[0]
    grads["attn_rel_bias"][j] = _bias_window_grad(dbias)
    dproj = jnp.concatenate([dqa, dka[PAD_A:].astype(bf16), dva[PAD_A:].astype(bf16),
                             dqs, dks.astype(bf16), dvs.astype(bf16)], axis=1)
    gi, ll = _grad_slot("attn_w_in", j)
    grads["attn_w_in"][gi] = _mm_wgrad_cols("attn_in_wgrad", h, dproj, grads["attn_w_in"][gi], ll)
    return _mm_cols_t("attn_in_t", dproj, wts["attn_w_in"], j, f32)


def _rg_fwd(h, wts, j, plan):
    proj =_mm_cols("rg_in", h, wts["rg_w_in"], j, f32)
    small = [wts[k][j] for k in ("rg_conv_w", "rg_conv_b", "rg_w_a", "rg_w_i", "rg_b_a", "rg_b_i", "rg_lambda")]
    xc, a, u = _rg_gates_fwd(proj, *small)
    hs = _lru_scan("lru_scan_fwd", a, u, False)
    yp = _rows("rg_gate_out", lambda hv, gv: hv * _gelu(gv), [hs, (proj, D_MODEL, 0)], [], [(D_MODEL, bf16)])[0]
    m = _mm_rows("rg_out", [yp], wts["rg_w_out"], j, f32)
    return m, (proj, xc, a, hs, yp)


def _rg_bwd(dm, h, saved, wts, j, grads, exch):
    proj, xc, a, hs, yp = saved
    dyp = _mm_rows_t("rg_out_t", dm, wts["rg_w_out"], j, f32)
    gi, ll = _grad_slot("rg_w_out", j)
    grads["rg_w_out"][gi] = _mm_wgrad("rg_out_wgrad", yp, dm, grads["rg_w_out"][gi], ll)

    def gate_bwd(dy, hv, gv, av):
        dhs = dy * _gelu(gv)
        return dhs, av * dhs, dy * hv * _gelu_grad(gv)

    dhs, ab, dgate = _rows("rg_gate_out_bwd", gate_bwd, [dyp, hs, (proj, D_MODEL, 0), a], [],
                           [(D_MODEL, f32), (D_MODEL, f32), (D_MODEL, bf16)])
    c = _lru_scan("lru_scan_bwd", a, ab, True)
    wa, wi, ba, bi, lam = [wts[k][j] for k in ("rg_w_a", "rg_w_i", "rg_b_a", "rg_b_i", "rg_lambda")]
    dxc, dwa, dwi, dba, dbi, dlam = _rg_gates_bwd(dhs, c, hs, xc, wa, wi, ba, bi, lam)
    dxr, dcw, dcb = _rg_conv_bwd(dxc, proj, wts["rg_conv_w"][j])
    for k, v in (("rg_w_a", dwa), ("rg_w_i", dwi), ("rg_b_a", dba), ("rg_b_i", dbi), ("rg_lambda", dlam),
                 ("rg_conv_w", dcw), ("rg_conv_b", dcb)):
        grads[k][j] = v
    dproj = jnp.concatenate([dgate, dxr], axis=1)
    grads["rg_w_in"][gi] = _mm_wgrad_cols("rg_in_wgrad", h, dproj, grads["rg_w_in"][gi], ll)
    return _mm_cols_t("rg_in_t", dproj, wts["rg_w_in"], j, f32)


def _local_step(x, target, wts, plan=None, exch=None):
    t = x.shape[0]
    d = D_MODEL
    gains = {k: wts[k] for k in ("norm_mix_pre", "norm_mix_post", "norm_ffn_pre", "norm_ffn_post")}
    gain = lambda k, l: gains[k][l:l + 1]

    saved = []
    h = _rows("norm_in", _norm_fwd, [x], [gain("norm_mix_pre", 0)], [(d, bf16)])[0]
    loss_cols = None
    for l in range(DEPTH):
        j = l // 2
        m, mix_saved = (_attn_fwd if l % 2 == 0 else _rg_fwd)(h, wts, j, plan)

        def resid_next(xv, mv, g_post, g_next):
            x1 = xv + _norm_fwd(mv, g_post)
            return x1, _norm_fwd(x1, g_next)

        x1, h2 = _rows("resid_mix", resid_next, [x, m], [gain("norm_mix_post", l), gain("norm_ffn_pre", l)],
                       [(d, f32), (d, bf16)])
        g, u, hid = _carried(plan if l == 0 else None, "ffn_up", wts, _ffn_up, h2, wts["ffn_w_gate"],
                             wts["ffn_w_up"], l)
        f = _ffn_down(hid, wts["ffn_w_down"], l)
        saved.append((x, h, m, mix_saved, x1, h2, g, u, hid, f))
        if l + 1 < DEPTH:
            x, h = _rows("resid_ffn", resid_next, [x1, f], [gain("norm_ffn_post", l), gain("norm_mix_pre", l + 1)],
                         [(d, f32), (d, bf16)])
        else:
            def resid_loss(xv, fv, tv, g_post):
                err = xv + _norm_fwd(fv, g_post) - tv
                return err * (1.0 / d), jnp.sum(err * err, axis=0, keepdims=True)

            dx, loss_cols = _rows("resid_loss", resid_loss, [x1, f, target], [gain("norm_ffn_post", l)],
                                  [(d, f32)], [((1, d), f32)])
    loss = 0.5 * jnp.sum(loss_cols) / d

    grads = {k: {} for k in SMALL_GRADS}
    for k in BIG_GRADS:
        shp = wts[k].shape
        rest = shp[2:] if shp[1] == 1 else shp[1:]
        grads[k] = [_Fresh((LOWER_LAYERS[k],) + rest), _Fresh((shp[0] - LOWER_LAYERS[k],) + rest)]

    def norm_bwd_cast(uv, dyv, gv):
        du, dg = _norm_bwd(uv, dyv, gv)
        return du, dg

    def norm_bwd_resid(uv, dhv, dxv, gv):
        du, dg = _norm_bwd(uv, dhv, gv)
        return dxv + du, dg

    def norm_bwd_pair(uv, dhv, dxv, nv, g_pre, g_post):
        dx_, dg_pre = norm_bwd_resid(uv, dhv, dxv, g_pre)
        dn, dg_post = _norm_bwd(nv, dx_, g_post)
        return dx_, dn, dg_pre, dg_post

    df = None
    for l in reversed(range(DEPTH)):
        j = l // 2
        x_in, h, m, mix_saved, x1, h2, g, u, hid, f = saved[l]
        if df is None:
            df, grads["norm_ffn_post"][l] = _rows("norm_ffn_post_bwd", norm_bwd_cast, [f, dx],
                                                  [gain("norm_ffn_post", l)], [(d, bf16)], [((1, d), f32)])
        dg, du = _ffn_down_bwd(df, wts["ffn_w_down"], l, g, u)
        gi, ll = _grad_slot("ffn_w_down", l)
        grads["ffn_w_down"][gi] = _ffn_wgrad_down(hid, df, grads["ffn_w_down"][gi], ll)
        dh2 = _ffn_up_bwd(dg, du, wts["ffn_w_gate"], wts["ffn_w_up"], l)
        grads["ffn_w_gate"][gi], grads["ffn_w_up"][gi] = _ffn_wgrad_up(
            h2, dg, du, grads["ffn_w_gate"][gi], grads["ffn_w_up"][gi], ll)
        dx1, dm, grads["norm_ffn_pre"][l], grads["norm_mix_post"][l] = _rows(
            "norm_ffn_mix_bwd", norm_bwd_pair, [x1, dh2, dx, m], [gain("norm_ffn_pre", l), gain("norm_mix_post", l)],
            [(d, f32), (d, bf16)], [((1, d), f32), ((1, d), f32)])
        dh = (_attn_bwd if l % 2 == 0 else _rg_bwd)(dm, h, mix_saved, wts, j, grads, exch)
        if l > 0:
            dx, df, grads["norm_mix_pre"][l], grads["norm_ffn_post"][l - 1] = _rows(
                "norm_mix_ffn_bwd", norm_bwd_pair, [x_in, dh, dx1, saved[l - 1][9]],
                [gain("norm_mix_pre", l), gain("norm_ffn_post", l - 1)],
                [(d, f32), (d, bf16)], [((1, d), f32), ((1, d), f32)])
        else:
            dx, grads["norm_mix_pre"][l] = _rows("norm_mix_pre_bwd", norm_bwd_resid, [x_in, dh, dx1],
                                                 [gain("norm_mix_pre", l)], [(d, f32)], [((1, d), f32)])
    return loss, dx, grads


ANY = pl.BlockSpec(memory_space=pl.ANY)
PACK_COLS = 1024
SMALL_ROWS = 288


def _mesh_pos():
    x, y, c = lax.axis_index("x"), lax.axis_index("y"), lax.axis_index("c")
    return x, y, c, [(1 - x, y), (x, 1 - y), (1 - x, 1 - y)]


def _run_copies(copies):
    for cp in copies:
        cp.start()
    for cp in copies:
        cp.wait()


GATHER_SEMS = 7


def _gather_copies(items, ins, outs, send, recv):
    x, y, c, chips = _mesh_pos()
    q = 2 * x + y
    sibling = (x, y, 1 - c)

    def copy(k, src, dst, to):
        return pltpu.make_async_remote_copy(src_ref=src, dst_ref=dst, send_sem=send.at[k], recv_sem=recv.at[k],
                                            device_id=to, device_id_type=MESH)

    own, sent, passed = [], [], []
    for i, (t, l0, nl) in enumerate(items):
        lay = pl.ds(l0, nl)
        half = ins[t].shape[1] // 2
        rows = pl.ds(pl.multiple_of(c * half, half), half)
        own.append(copy(GATHER_SEMS * i, ins[t].at[lay], outs[t].at[lay, q], sibling))
        for j, (px, py) in enumerate(chips):
            sent.append(copy(GATHER_SEMS * i + 1 + j, ins[t].at[lay, rows], outs[t].at[lay, q, rows], (px, py, c)))
            landed = outs[t].at[lay, 2 * px + py, rows]
            passed.append(copy(GATHER_SEMS * i + 4 + j, landed, landed, sibling))
    return own, sent, passed


def _gather_start(items, ins, outs, send, recv):
    own, sent, _ = _gather_copies(items, ins, outs, send, recv)
    for cp in own + sent:
        cp.start()


def _gather_finish(items, ins, outs, send, recv):
    own, sent, passed = _gather_copies(items, ins, outs, send, recv)
    for arrived, forward in zip(sent, passed):
        arrived.wait_recv()
        forward.start()
    for cp in sent:
        cp.wait_send()
    for cp in own + passed:
        cp.wait()


def _gather_call(items, shards):
    n = len(shards)
    nsem = GATHER_SEMS * len(items)

    def body(*refs):
        ins, outs = refs[:n], refs[n:2 * n]
        _gather_start(items, ins, outs, *refs[2 * n:])
        _gather_finish(items, ins, outs, *refs[2 * n:])

    return pl.pallas_call(
        body, name="weight_all_gather", in_specs=[ANY] * n, out_specs=[ANY] * n,
        out_shape=[SDS((s.shape[0], N_CHIPS) + s.shape[1:], s.dtype) for s in shards],
        scratch_shapes=[pltpu.SemaphoreType.DMA((nsem,)), pltpu.SemaphoreType.DMA((nsem,))])(*shards)


def _call(body, operands, *, name, grid, in_specs, out_specs, out_shape, sem, scratch=(), gather=None):
    if gather is None:
        return pl.pallas_call(body, grid=grid, in_specs=in_specs, out_specs=out_specs, out_shape=out_shape,
                              scratch_shapes=list(scratch), name=name, compiler_params=_cparams(sem))(*operands), None
    start, finish, c_ins, c_io, c_new, nsem = gather
    n_in, n_out, n_scr = len(operands), len(out_shape), len(scratch)
    ni, nio, nco = len(c_ins), len(c_io), len(c_io) + len(c_new)

    def full(*refs):
        ins, sh = refs[:n_in], refs[n_in:n_in + ni]
        outs = refs[n_in + ni + nio:n_in + ni + nio + n_out]
        co = refs[n_in + ni + nio + n_out:n_in + ni + nio + n_out + nco]
        scr = refs[n_in + ni + nio + n_out + nco:]
        ids = [pl.program_id(a) for a in range(len(grid))]
        first = functools.reduce(jnp.logical_and, [i == 0 for i in ids])
        last = functools.reduce(jnp.logical_and, [i == g - 1 for i, g in zip(ids, grid)])

        @pl.when(first)
        def _():
            start(sh, co, scr[n_scr], scr[n_scr + 1])

        body(*ins, *outs, *scr[:n_scr])

        @pl.when(last)
        def _():
            finish(sh, co, scr[n_scr], scr[n_scr + 1])

    res = pl.pallas_call(
        full, grid=grid, in_specs=list(in_specs) + [ANY] * (ni + nio), out_specs=list(out_specs) + [ANY] * nco,
        out_shape=list(out_shape) + [SDS(g.shape, g.dtype) for g in list(c_io) + list(c_new)],
        scratch_shapes=list(scratch) + [pltpu.SemaphoreType.DMA((nsem,)), pltpu.SemaphoreType.DMA((nsem,))],
        input_output_aliases={n_in + ni + t: n_out + t for t in range(nio)}, name=name,
        compiler_params=_cparams(("arbitrary",) * len(grid)))(*operands, *c_ins, *c_io)
    return res[:n_out], res[n_out:]


def _pair_exchange(gs):
    n = len(gs)

    def body(*refs):
        _pair_copies(refs[:n], refs[n:2 * n], *refs[2 * n:], start=True)
        _pair_copies(refs[:n], refs[n:2 * n], *refs[2 * n:], start=False)

    return pl.pallas_call(
        body, name="grad_pair_exchange", in_specs=[ANY] * n, out_specs=[ANY] * n,
        out_shape=_pair_shapes(gs),
        scratch_shapes=[pltpu.SemaphoreType.DMA((n,)), pltpu.SemaphoreType.DMA((n,))])(*gs)


def _pair_shapes(gs):
    return [SDS(g.shape[:2] + (g.shape[2] // 2, g.shape[3]), f32) for g in gs]


def _pair_copies(ins, outs, send, recv, start):
    x, y, c, _ = _mesh_pos()
    for t in range(len(ins)):
        half = ins[t].shape[2] // 2
        src = ins[t].at[:, :, pl.ds(pl.multiple_of((1 - c) * half, SUBLANES), half)]
        cp = pltpu.make_async_remote_copy(src_ref=src, dst_ref=outs[t], send_sem=send.at[t], recv_sem=recv.at[t],
                                          device_id=(x, y, 1 - c), device_id_type=MESH)
        cp.start() if start else cp.wait()


def _pair_carry(gs):
    return (functools.partial(_pair_copies, start=True), functools.partial(_pair_copies, start=False),
            gs, [], _pair_shapes(gs), len(gs))


def _pair_sum(name, g, got, c):
    l, s, r, cols = g.shape

    def body(c_ref, a_ref, b_ref, o_ref):
        o_ref[...] = (a_ref[...] + b_ref[...]).astype(bf16)

    blk = (None, None, r // 2, cols)
    return pl.pallas_call(
        body, name=name, out_shape=SDS(got.shape, bf16),
        grid_spec=pltpu.PrefetchScalarGridSpec(
            num_scalar_prefetch=1, grid=(l, s),
            in_specs=[pl.BlockSpec(blk, lambda i, q, c_ref: (i, q, c_ref[0], 0)),
                      pl.BlockSpec(blk, lambda i, q, c_ref: (i, q, 0, 0))],
            out_specs=pl.BlockSpec(blk, lambda i, q, c_ref: (i, q, 0, 0))),
        compiler_params=_cparams(("parallel", "parallel")))(c, g, got)


def _chip_exchange(hs):
    n = len(hs)

    def body(*refs):
        _chip_copies(refs[:n], refs[n:2 * n], *refs[2 * n:], start=True)
        _chip_copies(refs[:n], refs[n:2 * n], *refs[2 * n:], start=False)

    return pl.pallas_call(
        body, name="grad_chip_exchange", in_specs=[ANY] * n, out_specs=[ANY] * n,
        out_shape=[SDS(h.shape, h.dtype) for h in hs],
        scratch_shapes=[pltpu.SemaphoreType.DMA((3 * n,)), pltpu.SemaphoreType.DMA((3 * n,))])(*hs)


def _chip_copies(ins, outs, send, recv, start):
    x, y, c, chips = _mesh_pos()
    q = 2 * x + y
    for t in range(len(ins)):
        for j, (px, py) in enumerate(chips):
            cp = pltpu.make_async_remote_copy(
                src_ref=ins[t].at[:, 2 * px + py], dst_ref=outs[t].at[:, q], send_sem=send.at[3 * t + j],
                recv_sem=recv.at[3 * t + j], device_id=(px, py, c), device_id_type=MESH)
            cp.start() if start else cp.wait()


def _chip_carry(hs):
    return (functools.partial(_chip_copies, start=True), functools.partial(_chip_copies, start=False),
            hs, [], [SDS(h.shape, h.dtype) for h in hs], 3 * len(hs))


def _chip_sum(name, s, h, pos, l0, layers, into):
    l, _, r, cols = s.shape

    def body(pos_ref, s0, s1, s2, s3, own_ref, *rest):
        vals = [jnp.where(pos_ref[0] == p, own_ref[...], ref[...]).astype(f32) for p, ref in enumerate((s0, s1, s2, s3))]
        rest[-1][...] = ((vals[0] + vals[1]) + vals[2]) + vals[3]

    blk = (None, None, r, cols)
    slot = lambda p: pl.BlockSpec(blk, lambda i, pos_ref: (i, jnp.where(pos_ref[0] == p, (p + 1) % N_CHIPS, p), 0, 0))
    extra, alias = ([], {}) if into is None else ([into], {6: 0})
    return pl.pallas_call(
        body, name=name, out_shape=SDS((layers, 2 * r, cols), f32), input_output_aliases=alias,
        grid_spec=pltpu.PrefetchScalarGridSpec(
            num_scalar_prefetch=1, grid=(l,),
            in_specs=[slot(p) for p in range(N_CHIPS)] + [pl.BlockSpec(blk, lambda i, pos_ref: (i, pos_ref[0], 0, 0))]
            + [ANY] * len(extra),
            out_specs=pl.BlockSpec((None, r, cols), lambda i, pos_ref: (l0 + i, pos_ref[1], 0))),
        compiler_params=_cparams(("parallel",)))(pos, s, s, s, s, h, *extra)


def _pair_gather(fulls):
    n = len(fulls)

    def body(*refs):
        ins, outs = refs[:n], refs[n:2 * n]
        send, recv = refs[2 * n:]
        x, y, c, _ = _mesh_pos()
        copies = []
        for t in range(n):
            half = outs[t].shape[1] // 2
            rows = outs[t].at[:, pl.ds(pl.multiple_of(c * half, SUBLANES), half)]
            copies.append(pltpu.make_async_remote_copy(
                src_ref=rows, dst_ref=rows, send_sem=send.at[t], recv_sem=recv.at[t],
                device_id=(x, y, 1 - c), device_id_type=MESH))
        _run_copies(copies)

    return pl.pallas_call(
        body, name="grad_pair_gather", in_specs=[ANY] * n, out_specs=[ANY] * n,
        out_shape=[SDS(f.shape, f32) for f in fulls], input_output_aliases={t: t for t in range(n)},
        scratch_shapes=[pltpu.SemaphoreType.DMA((n,)), pltpu.SemaphoreType.DMA((n,))])(*fulls)


COL_SHARDED = ("attn_w_in", "rg_w_in", "ffn_w_gate", "ffn_w_up")
ROW_SHARDED = ("attn_w_out", "rg_w_out")
GATES = ("rg_w_a", "rg_w_i")
VECTORS = ("rg_conv_w", "rg_conv_b", "rg_b_a", "rg_b_i", "rg_lambda")
REPLICATED = ("norm_mix_pre", "norm_mix_post", "norm_ffn_pre", "norm_ffn_post", "attn_rel_bias")
BIG_GRADS = COL_SHARDED + ROW_SHARDED + ("ffn_w_down",)
SMALL_GRADS = GATES + VECTORS + REPLICATED
WEIGHTS =("attn_w_in", "attn_rel_bias", "attn_w_out", "rg_w_in", "rg_conv_w", "rg_conv_b", "rg_w_a", "rg_b_a",
           "rg_w_i", "rg_b_i", "rg_lambda", "rg_w_out", "norm_mix_pre", "norm_mix_post", "norm_ffn_pre",
           "norm_ffn_post", "ffn_w_gate", "ffn_w_up", "ffn_w_down")
SMALL = VECTORS + REPLICATED


GATHER_PARTS = {
    "first": (("attn_w_in", 0, 1), ("attn_w_out", 0, 1), ("rg_w_a", 0, 8), ("rg_w_i", 0, 8), ("vec", 0, 1)),
    "chunk_attn_fwd": (("ffn_w_gate", 0, 1), ("ffn_w_up", 0, 1), ("ffn_w_down", 0, 1), ("rg_w_in", 0, 1),
                       ("rg_w_out", 0, 1)),
    "sb_attn_fwd": (("ffn_w_gate", 1, 3), ("ffn_w_up", 1, 3), ("ffn_w_down", 1, 3)),
    "ffn_up": (("rg_w_in", 1, 1), ("rg_w_out", 1, 1), ("attn_w_in", 1, 1), ("attn_w_out", 1, 1)),
}


TRANSPOSED = ("ffn_w_gate", "ffn_w_up")


def _natural(name, a):
    return jnp.swapaxes(a, 1, 2) if name in TRANSPOSED else a


class _WeightGather:
    def __init__(self, w):
        self.w = w
        self.names = list(COL_SHARDED + ROW_SHARDED + GATES + ("ffn_w_down", "vec"))
        self.shards = {}
        for k in self.names[:-1]:
            a = _natural(k, w[k]).astype(bf16)
            self.shards[k] = a.reshape((-1,) + a.shape[-2:])
        self.shards["vec"] = jnp.concatenate([w[k].reshape(-1) for k in VECTORS]).reshape(1, -1, LANES)
        got = _gather_call(self._items("first", self.names), [self.shards[k] for k in self.names])
        self.raw = dict(zip(self.names, got))

    @staticmethod
    def _items(part, names):
        return [(names.index(k), l0, nl) for k, l0, nl in GATHER_PARTS[part]]

    def part(self, part):
        names = list(dict.fromkeys(k for k, _, _ in GATHER_PARTS[part]))
        items = self._items(part, names)
        return (functools.partial(_gather_start, items), functools.partial(_gather_finish, items),
                [self.shards[k] for k in names], [self.raw[k] for k in names], [], GATHER_SEMS * len(items)), names

    def views(self):
        got, w = self.raw, self.w
        out = {k: w[k] for k in REPLICATED}
        for k in COL_SHARDED + ("ffn_w_down",):
            out[k] = got[k]
        for k in ROW_SHARDED:
            l, s, ks, n = got[k].shape
            out[k] = got[k].reshape(l, 1, s * ks, n)
        for k in GATES:
            out[k] = got[k].reshape(2, LRU_BLOCKS, LRU_BW, LRU_BW)
        vec = got["vec"].reshape(N_CHIPS, -1)
        off = 0
        for k in VECTORS:
            shp = w[k].shape
            n = int(np.prod(shp))
            piece = vec[:, off:off + n].reshape((N_CHIPS,) + shp)
            off += n
            if k == "rg_conv_w":
                out[k] = piece.reshape(N_CHIPS, 2, 4, 256).transpose(1, 2, 0, 3).reshape(2, 4, D_MODEL)
            elif k in ("rg_b_a", "rg_b_i"):
                out[k] = piece.transpose(1, 2, 0, 3).reshape(2, 1, D_MODEL)
            else:
                out[k] = piece.transpose(1, 0, 2).reshape(2, 1, D_MODEL)
        return out


def _carried(plan, part, wts, fn, *args):
    if plan is None:
        return fn(*args, None)[0]
    gather, names = plan.part(part)
    out, new = fn(*args, gather)
    plan.raw.update(zip(names, new))
    wts.update(plan.views())
    return out


def _grad_blocks(name, g):
    st = jnp.stack([g[i] for i in sorted(g)])
    if name in GATES:
        st = st.reshape(2, LRU_BLOCKS, N_CHIPS, LRU_BW // N_CHIPS, LRU_BW).transpose(2, 0, 1, 3, 4)
    elif name == "rg_conv_w":
        st = st.reshape(2, 4, N_CHIPS, -1).transpose(2, 0, 1, 3)
    elif name in ("rg_b_a", "rg_b_i"):
        st = st.reshape(2, LRU_BLOCKS, N_CHIPS, -1).transpose(2, 0, 1, 3)
    elif name in VECTORS:
        st = st.reshape(2, N_CHIPS, -1).transpose(1, 0, 2)
    else:
        st = jnp.broadcast_to(st.reshape(1, -1), (N_CHIPS, st.size))
    return st.reshape(N_CHIPS, -1)


class _GradExchange:
    def __init__(self):
        self.c = lax.axis_index("c").astype(jnp.int32).reshape(1)
        self.pos = jnp.stack([2 * lax.axis_index("x") + lax.axis_index("y"), lax.axis_index("c")]).astype(jnp.int32)
        self.up = self.got_up = self.parts_up = self.slots_up = None

    @staticmethod
    def _blocked(g):
        if g.ndim == 3:
            g = g.reshape(g.shape[0], N_CHIPS, g.shape[1] // N_CHIPS, g.shape[2])
        return g

    def _sums(self, tag, names, gs, got):
        return [_pair_sum("grad_pair_sum_" + tag + k, g, r, self.c) for k, g, r in zip(names, gs, got)]

    def upper_carry(self, grads):
        self.up = [self._blocked(grads[k][1]) for k in BIG_GRADS]
        return _pair_carry(self.up)

    def upper_got(self, got):
        self.got_up = got

    def carry(self):
        self.parts_up = self._sums("up_", BIG_GRADS, self.up, self.got_up)
        return _chip_carry(self.parts_up)

    def carried(self, slots):
        self.slots_up = slots

    def finish(self, grads, shard_shapes):
        if self.got_up is None:
            self.upper_carry(grads)
            self.got_up = _pair_exchange(self.up)
        if self.slots_up is None:
            self.carry()
            self.slots_up = _chip_exchange(self.parts_up)
        blocks = [_grad_blocks(k, grads[k]) for k in SMALL_GRADS]
        used = sum(b.shape[1] for b in blocks)
        small = jnp.concatenate(blocks + [jnp.zeros((N_CHIPS, SMALL_ROWS * PACK_COLS - used), f32)], axis=1)
        names = tuple(k for k in BIG_GRADS if LOWER_LAYERS[k]) + ("small",)
        gs = [self._blocked(grads[k][0]) for k in names[:-1]] + [small.reshape(1, N_CHIPS, SMALL_ROWS, PACK_COLS)]
        parts = dict(zip(names, self._sums("lo_", names, gs, _pair_exchange(gs))))
        slots = dict(zip(names, _chip_exchange([parts[k] for k in names])))
        fulls = []
        for i, k in enumerate(BIG_GRADS):
            nlo, nup = LOWER_LAYERS[k], self.parts_up[i].shape[0]
            full = _chip_sum("grad_chip_sum_up_" + k, self.slots_up[i], self.parts_up[i], self.pos, nlo, nlo + nup, None)
            if nlo:
                full = _chip_sum("grad_chip_sum_lo_" + k, slots[k], parts[k], self.pos, 0, nlo + nup, full)
            fulls.append(full)
        fulls.append(_chip_sum("grad_chip_sum_lo_small", slots["small"], parts["small"], self.pos, 0, 1, None))
        full = _pair_gather(fulls)
        out = {k: f.reshape(shard_shapes[k]) for k, f in zip(BIG_GRADS, full)}
        flat, off = full[-1].reshape(-1), 0
        for k in SMALL_GRADS:
            n = int(np.prod(shard_shapes[k]))
            out[k] = flat[off:off + n].reshape(shard_shapes[k])
            off += n
        return out


def _adamw_fn(w, g, m, v):
    m = ADAM_B1 * m + (1.0 - ADAM_B1) * g
    v = ADAM_B2 * v + (1.0 - ADAM_B2) * (g * g)
    m_hat = m / (1.0 - ADAM_B1 ** ADAM_STEP)
    v_hat = v / (1.0 - ADAM_B2 ** ADAM_STEP)
    return -ADAM_LR * (m_hat / (jnp.sqrt(v_hat) + ADAM_EPS) + ADAM_WD * w), m, v


def _adamw(name, w, g, m, v):
    shp = w.shape
    if w.size >= 1 << 16:
        width = shp[-1]
        ops = [a.reshape(-1, width) for a in (w, g, m, v)]
        res = _rows(name, _adamw_fn, ops, [], [(width, f32)] * 3)
        return [r.reshape(shp) for r in res]
    n = w.size
    rows = -(-n // (SUBLANES * LANES)) * SUBLANES
    ops = [jnp.pad(a.reshape(-1), (0, rows * LANES - n)).reshape(rows, LANES) for a in (w, g, m, v)]
    res = _rows(name, _adamw_fn, ops, [], [(LANES, f32)] * 3, tr=rows)
    return [r.reshape(-1)[:n].reshape(shp) for r in res]


def kernel(x, attn_w_in, attn_rel_bias, attn_w_out, rg_w_in, rg_conv_w, rg_conv_b, rg_w_a, rg_b_a, rg_w_i, rg_b_i, rg_lambda, rg_w_out, norm_mix_pre, norm_mix_post, norm_ffn_pre, norm_ffn_post, ffn_w_gate, ffn_w_up, ffn_w_down, loss_target, m_attn_w_in, m_attn_rel_bias, m_attn_w_out, m_rg_w_in, m_rg_conv_w, m_rg_conv_b, m_rg_w_a, m_rg_b_a, m_rg_w_i, m_rg_b_i, m_rg_lambda, m_rg_w_out, m_norm_mix_pre, m_norm_mix_post, m_norm_ffn_pre, m_norm_ffn_post, m_ffn_w_gate, m_ffn_w_up, m_ffn_w_down, v_attn_w_in, v_attn_rel_bias, v_attn_w_out, v_rg_w_in, v_rg_conv_w, v_rg_conv_b, v_rg_w_a, v_rg_b_a, v_rg_w_i, v_rg_b_i, v_rg_lambda, v_rg_w_out, v_norm_mix_pre, v_norm_mix_post, v_norm_ffn_pre, v_norm_ffn_post, v_ffn_w_gate, v_ffn_w_up, v_ffn_w_down):
    w = dict(zip(WEIGHTS, (attn_w_in, attn_rel_bias, attn_w_out, rg_w_in, rg_conv_w, rg_conv_b, rg_w_a, rg_b_a, rg_w_i,
                           rg_b_i, rg_lambda, rg_w_out, norm_mix_pre, norm_mix_post, norm_ffn_pre, norm_ffn_post,
                           ffn_w_gate, ffn_w_up, ffn_w_down)))
    m = dict(zip(WEIGHTS, (m_attn_w_in, m_attn_rel_bias, m_attn_w_out, m_rg_w_in, m_rg_conv_w, m_rg_conv_b, m_rg_w_a,
                           m_rg_b_a, m_rg_w_i, m_rg_b_i, m_rg_lambda, m_rg_w_out, m_norm_mix_pre, m_norm_mix_post,
                           m_norm_ffn_pre, m_norm_ffn_post, m_ffn_w_gate, m_ffn_w_up, m_ffn_w_down)))
    v = dict(zip(WEIGHTS, (v_attn_w_in, v_attn_rel_bias, v_attn_w_out, v_rg_w_in, v_rg_conv_w, v_rg_conv_b, v_rg_w_a,
                           v_rg_b_a, v_rg_w_i, v_rg_b_i, v_rg_lambda, v_rg_w_out, v_norm_mix_pre, v_norm_mix_post,
                           v_norm_ffn_pre, v_norm_ffn_post, v_ffn_w_gate, v_ffn_w_up, v_ffn_w_down)))
    plan = _WeightGather(w)
    exch = _GradExchange()
    loss, dx, grads = _local_step(x[0], loss_target[0], plan.views(), plan, exch)
    loss = lax.psum(loss, ("x", "y", "c"))
    g = exch.finish(grads, {k: _natural(k, w[k]).shape for k in WEIGHTS})

    big = [k for k in WEIGHTS if k not in SMALL]
    upd = {}
    for k in big:
        res = _adamw("adamw_" + k, _natural(k, w[k]), g[k], _natural(k, m[k]), _natural(k, v[k]))
        upd[k] = [_natural(k, r) for r in res]
        g[k] = _natural(k, g[k])
    cat = lambda d: jnp.concatenate([d[k].reshape(-1) for k in SMALL])
    small = _adamw("adamw_small", cat(w), cat(g), cat(m), cat(v))
    off = 0
    for k in SMALL:
        n = w[k].size
        upd[k] = [r[off:off + n].reshape(w[k].shape) for r in small]
        off += n
    return (loss, dx[None], *[g[k] for k in WEIGHTS], *[upd[k][0] for k in WEIGHTS],
            *[upd[k][1] for k in WEIGHTS], *[upd[k][2] for k in WEIGHTS])
```

```python
import functools

import numpy as np
import jax
import jax.numpy as jnp
from jax import lax
from jax.experimental import pallas as pl
from jax.experimental.pallas import tpu as pltpu

f32 = jnp.float32
bf16 = jnp.bfloat16
SDS = jax.ShapeDtypeStruct
MESH = pl.DeviceIdType.MESH

D_MODEL = 1024
N_CHIPS = 4
DEPTH = 4
HEAD_DIM = 64
CHUNK = 64
N_LEFT = 8
REL_CLIP = 256
A_W = 512
LRU_BLOCKS = 4
LRU_BW = 256
LRU_C = 8.0
D_FF = 2816
RMS_EPS = 1e-6
LANES = 128
SUBLANES = 8
VMEM_LIMIT = 56 * 1024 * 1024

QB_A = 2 * CHUNK
QSUB_A = 16
KW_A = QB_A + N_LEFT * CHUNK
PAD_A = N_LEFT * CHUNK
EXT_A = 768
SB_BLK = 128
QSUB_B = 8
SB_DEAD = -110.0

ADAM_LR, ADAM_B1, ADAM_B2, ADAM_EPS, ADAM_WD, ADAM_STEP = 0.001, 0.9, 0.999, 1e-08, 0.01, 10


def _cparams(sem):
    return pltpu.CompilerParams(dimension_semantics=sem, vmem_limit_bytes=VMEM_LIMIT)


def _gemm(name, operands, in_specs, o_spec, out_shape, grid, dims, acc_shape, into=None):
    nred = grid[2]
    npair = len(operands) // 2
    nin = 2 * npair + (into is not None)

    def body(*refs):
        o_ref = refs[nin]
        p = None
        for t in range(npair):
            d = lax.dot_general(refs[2 * t][...], refs[2 * t + 1][...], (dims, ((), ())),
                                preferred_element_type=f32)
            p = d if p is None else p + d
        if nred == 1:
            o_ref[...] = p.astype(o_ref.dtype)
        else:
            acc = refs[nin + 1]
            r = pl.program_id(2)

            @pl.when(r == 0)
            def _():
                acc[...] = p

            @pl.when(r > 0)
            def _():
                acc[...] += p

            @pl.when(r == nred - 1)
            def _():
                o_ref[...] = acc[...].astype(o_ref.dtype)

    scratch = [] if nred == 1 else [pltpu.VMEM(acc_shape, f32)]
    extra, alias = ([], {}) if into is None else ([into], {2 * npair: 0})
    return pl.pallas_call(
        body, grid=grid, in_specs=list(in_specs) + [pl.BlockSpec(memory_space=pl.ANY)] * len(extra),
        out_specs=o_spec, out_shape=out_shape, scratch_shapes=scratch, name=name, input_output_aliases=alias,
        compiler_params=_cparams(("parallel", "parallel", "arbitrary")))(*operands, *extra)


LOWER_LAYERS = {"attn_w_in": 1, "attn_w_out": 1, "rg_w_in": 0, "rg_w_out": 0,
                "ffn_w_gate": 0, "ffn_w_up": 0, "ffn_w_down": 0}


def _grad_slot(name, l):
    n = LOWER_LAYERS[name]
    return (0, l) if l < n else (1, l - n)


class _Fresh:
    def __init__(self, shape):
        self.shape = tuple(shape)


def _into(buf):
    return None if isinstance(buf, _Fresh) else buf


NN = ((1,), (0,))
NT = ((1,), (1,))
TN = ((0,), (0,))


WGRAD_TOKENS = 2048


def _tile(t, want=1024):
    return min(want, t)


def _mm_cols(name, a, w, l, out_dtype):
    t, k = a.shape
    _, s, _, ns = w.shape
    tm = _tile(t, 2048)
    return _gemm(
        name, [a, w],
        [pl.BlockSpec((tm, k), lambda i, j, r: (i, 0)),
         pl.BlockSpec((None, None, k, ns), lambda i, j, r: (l, j, 0, 0))],
        pl.BlockSpec((tm, ns), lambda i, j, r: (i, j)),
        SDS((t, s * ns), out_dtype), (t // tm, s, 1), NN, None)


def _mm_cols_t(name, dy, w, l, out_dtype):
    t = dy.shape[0]
    _, s, k, ns = w.shape
    tm = _tile(t)
    ops, specs = [], []
    for r in range(s):
        ops += [dy, w]
        specs += [pl.BlockSpec((tm, ns), lambda i, j, kk, r=r: (i, r)),
                  pl.BlockSpec((None, None, k, ns), lambda i, j, kk, r=r: (l, r, 0, 0))]
    return _gemm(name, ops, specs, pl.BlockSpec((tm, k), lambda i, j, kk: (i, 0)),
                 SDS((t, k), out_dtype), (t // tm, 1, 1), NT, None)


def _mm_wgrad_cols(name, a, dy, buf, l):
    t, k = a.shape
    _, s, _, ns = buf.shape
    tt = _tile(t, 2 * WGRAD_TOKENS)
    return _gemm(
        name, [a, dy],
        [pl.BlockSpec((tt, k), lambda i, j, r: (r, 0)),
         pl.BlockSpec((tt, ns), lambda i, j, r: (r, i))],
        pl.BlockSpec((None, None, k, ns), lambda i, j, r: (l, i, 0, 0)),
        SDS(buf.shape, f32), (s, 1, t // tt), TN, (k, ns), into=_into(buf))


def _mm_rows(name, parts, w, l, out_dtype):
    t = parts[0].shape[0]
    n = w.shape[3]
    tm = _tile(t, 2048)
    ops, specs = [], []
    for p_i, a in enumerate(parts):
        kp = a.shape[1]
        ops += [a, w]
        specs += [pl.BlockSpec((tm, kp), lambda i, j, r: (i, 0)),
                  pl.BlockSpec((None, None, kp, n), lambda i, j, r, p_i=p_i: (l, 0, p_i, 0))]
    return _gemm(name, ops, specs, pl.BlockSpec((tm, n), lambda i, j, r: (i, 0)),
                 SDS((t, n), out_dtype), (t // tm, 1, 1), NN, None)


def _mm_rows_t(name, dy, w, l, out_dtype):
    t, n = dy.shape
    k = w.shape[2]
    tm = _tile(t, 2048)
    return _gemm(
        name, [dy, w],
        [pl.BlockSpec((tm, n), lambda i, j, r: (i, 0)),
         pl.BlockSpec((None, None, k, n), lambda i, j, r: (l, 0, 0, 0))],
        pl.BlockSpec((tm, k), lambda i, j, r: (i, 0)),
        SDS((t, k), out_dtype), (t // tm, 1, 1), NT, None)


def _mm_wgrad(name, a, dy, buf, l, part=0):
    t, k = a.shape
    n = dy.shape[1]
    tt = _tile(t, 2 * WGRAD_TOKENS)
    return _gemm(
        name, [a, dy],
        [pl.BlockSpec((tt, k), lambda i, j, r: (r, 0)),
         pl.BlockSpec((tt, n), lambda i, j, r: (r, 0))],
        pl.BlockSpec((None, k, n), lambda i, j, r: (l, part, 0)),
        SDS(buf.shape, f32), (1, 1, t // tt), TN, (k, n), into=_into(buf))


def _ffn_up(h, wg, wu, l, gather):
    t, k = h.shape
    s, fs = wg.shape[1], wg.shape[2]
    tm = _tile(t)

    def body(h_ref, wg_ref, wu_ref, g_ref, u_ref, hid_ref):
        hv = h_ref[...]
        g = lax.dot_general(hv, wg_ref[...], (NT, ((), ())), preferred_element_type=f32)
        u = lax.dot_general(hv, wu_ref[...], (NT, ((), ())), preferred_element_type=f32)
        g_ref[...] = g.astype(bf16)
        u_ref[...] = u.astype(bf16)
        hid_ref[...] = (g * jax.nn.sigmoid(g) * u).astype(bf16)

    wspec = pl.BlockSpec((None, None, fs, k), lambda j, i: (l, j, 0, 0))
    ospec = pl.BlockSpec((None, tm, fs), lambda j, i: (j, i, 0))
    return _call(
        body, [h, wg, wu], grid=(s, t // tm), name="ffn_up",
        in_specs=[pl.BlockSpec((tm, k), lambda j, i: (i, 0)), wspec, wspec],
        out_specs=[ospec, ospec, ospec], out_shape=[SDS((s, t, fs), bf16)] * 3,
        sem=("parallel", "parallel"), gather=gather)


def _ffn_down(hid, wd, l):
    s, t, fs = hid.shape
    n = wd.shape[3]
    tm = _tile(t)
    ops, specs = [], []
    for r in range(s):
        ops += [hid, wd]
        specs += [pl.BlockSpec((None, tm, fs), lambda i, j, k, r=r: (r, i, 0)),
                  pl.BlockSpec((None, None, fs, n), lambda i, j, k, r=r: (l, r, 0, 0))]
    return _gemm("ffn_down", ops, specs, pl.BlockSpec((tm, n), lambda i, j, k: (i, 0)),
                 SDS((t, n), f32), (t // tm, 1, 1), NN, None)


def _ffn_down_bwd(df, wd, l, g, u):
    t, n = df.shape
    s, fs = wd.shape[1], wd.shape[2]
    tm = _tile(t)

    def body(df_ref, wd_ref, g_ref, u_ref, dg_ref, du_ref):
        dh = lax.dot_general(df_ref[...], wd_ref[...], (NT, ((), ())), preferred_element_type=f32)
        gv = g_ref[...].astype(f32)
        uv = u_ref[...].astype(f32)
        sg = jax.nn.sigmoid(gv)
        du_ref[...] = (dh * gv * sg).astype(bf16)
        dg_ref[...] = (dh * uv * (sg * (1.0 + gv * (1.0 - sg)))).astype(bf16)

    bspec = pl.BlockSpec((None, tm, fs), lambda j, i: (j, i, 0))
    return pl.pallas_call(
        body, grid=(s, t // tm), name="ffn_down_bwd",
        in_specs=[pl.BlockSpec((tm, n), lambda j, i: (i, 0)),
                  pl.BlockSpec((None, None, fs, n), lambda j, i: (l, j, 0, 0)), bspec, bspec],
        out_specs=[bspec, bspec], out_shape=[SDS((s, t, fs), bf16)] * 2,
        compiler_params=_cparams(("parallel", "parallel")))(df, wd, g, u)


def _ffn_up_bwd(dg, du, wg, wu, l):
    s, t, fs = dg.shape
    k = wg.shape[3]
    tm = _tile(t, 512)
    ops, specs = [], []
    for r in range(s):
        aspec = pl.BlockSpec((None, tm, fs), lambda i, j, kk, r=r: (r, i, 0))
        wspec = pl.BlockSpec((None, None, fs, k), lambda i, j, kk, r=r: (l, r, 0, 0))
        ops += [dg, wg, du, wu]
        specs += [aspec, wspec, aspec, wspec]
    return _gemm("ffn_up_bwd", ops, specs, pl.BlockSpec((tm, k), lambda i, j, kk: (i, 0)),
                 SDS((t, k), f32), (t // tm, 1, 1), NN, None)


def _ffn_wgrad_up(h, dg, du, buf_g, buf_u, l):
    t, k = h.shape
    s, _, fs = dg.shape
    tt = _tile(t, WGRAD_TOKENS)
    nred = t // tt

    fresh = isinstance(buf_g, _Fresh)

    def body(*refs):
        h_ref, dg_ref, du_ref = refs[:3]
        og_ref, ou_ref, acc_g, acc_u = refs[-4:]
        r = pl.program_id(1)
        hv = h_ref[...]
        pg = lax.dot_general(dg_ref[...], hv, (TN, ((), ())), preferred_element_type=f32)
        pu = lax.dot_general(du_ref[...], hv, (TN, ((), ())), preferred_element_type=f32)

        @pl.when(r == 0)
        def _():
            acc_g[...] = pg
            acc_u[...] = pu

        @pl.when(r > 0)
        def _():
            acc_g[...] += pg
            acc_u[...] += pu

        @pl.when(r == nred - 1)
        def _():
            og_ref[...] = acc_g[...]
            ou_ref[...] = acc_u[...]

    dspec = pl.BlockSpec((None, tt, fs), lambda i, r: (i, r, 0))
    ospec = pl.BlockSpec((None, None, fs, k), lambda i, r: (l, i, 0, 0))
    extra, alias = ([], {}) if fresh else ([buf_g, buf_u], {3: 0, 4: 1})
    return pl.pallas_call(
        body, grid=(s, nred), name="ffn_wgrad_up",
        in_specs=[pl.BlockSpec((tt, k), lambda i, r: (r, 0)), dspec, dspec] + [ANY] * len(extra),
        out_specs=[ospec, ospec], out_shape=[SDS(buf_g.shape, f32), SDS(buf_u.shape, f32)],
        scratch_shapes=[pltpu.VMEM((fs, k), f32)] * 2, input_output_aliases=alias,
        compiler_params=_cparams(("parallel", "arbitrary")))(h, dg, du, *extra)


def _ffn_wgrad_down(hid, df, buf, l):
    s, t, fs = hid.shape
    n = df.shape[1]
    tt = _tile(t, 2 * WGRAD_TOKENS)
    return _gemm(
        "ffn_wgrad_down", [hid, df],
        [pl.BlockSpec((None, tt, fs), lambda i, j, r: (i, r, 0)),
         pl.BlockSpec((tt, n), lambda i, j, r: (r, 0))],
        pl.BlockSpec((None, None, fs, n), lambda i, j, r: (l, i, 0, 0)),
        SDS(buf.shape, f32), (s, 1, t // tt), TN, (fs, n), into=_into(buf))


def _rows(name, fn, rows, consts, row_outs, acc_outs=(), tr=512):
    rows = [r if isinstance(r, tuple) else (r, r.shape[1], 0) for r in rows]
    t = rows[0][0].shape[0]
    tr = max(d for d in range(SUBLANES, min(tr, t) + 1, SUBLANES) if t % d == 0)
    nin = len(rows) + len(consts)
    no, na = len(row_outs), len(acc_outs)

    def body(*refs):
        vals = fn(*[r[...] for r in refs[:nin]])
        if not isinstance(vals, (tuple, list)):
            vals = (vals,)
        for k in range(no):
            refs[nin + k][...] = vals[k].astype(refs[nin + k].dtype)
        first = pl.program_id(0) == 0
        for k in range(na):
            ref, val = refs[nin + no + k], vals[no + k]

            @pl.when(first)
            def _(ref=ref, val=val):
                ref[...] = val

            @pl.when(jnp.logical_not(first))
            def _(ref=ref, val=val):
                ref[...] += val

    in_specs = [pl.BlockSpec((tr, w), lambda i, cb=cb: (i, cb)) for (_, w, cb) in rows]
    in_specs += [pl.BlockSpec(c.shape, lambda i, nd=c.ndim: (0,) * nd) for c in consts]
    out_specs = [pl.BlockSpec((tr, w), lambda i: (i, 0)) for (w, _) in row_outs]
    out_specs += [pl.BlockSpec(s, lambda i, nd=len(s): (0,) * nd) for (s, _) in acc_outs]
    out_shape = [SDS((t, w), dt) for (w, dt) in row_outs] + [SDS(s, dt) for (s, dt) in acc_outs]
    res = pl.pallas_call(
        body, grid=(t // tr,), in_specs=in_specs, out_specs=out_specs, out_shape=out_shape,
        name=name, compiler_params=_cparams(("arbitrary",)))(*[r[0] for r in rows], *consts)
    return res


def _rstd(x):
    return lax.rsqrt(jnp.mean(x * x, axis=-1, keepdims=True) + RMS_EPS)


def _norm_fwd(x, g):
    return x * _rstd(x) * g


def _norm_bwd(u, dy, g):
    r = _rstd(u)
    n = u * r
    dn = dy * g
    du = r * (dn - n * jnp.mean(dn * n, axis=-1, keepdims=True))
    return du, jnp.sum(dy * n, axis=0, keepdims=True)


def _gelu(x):
    c = 0.7978845608028654
    return 0.5 * x * (1.0 + jnp.tanh(c * (x + 0.044715 * x * x * x)))


def _gelu_grad(x):
    c = 0.7978845608028654
    th = jnp.tanh(c * (x + 0.044715 * x * x * x))
    return 0.5 * (1.0 + th) + 0.5 * x * (1.0 - th * th) * c * (1.0 + 3.0 * 0.044715 * x * x)


def _mask_heads(x):
    lane = lax.broadcasted_iota(jnp.int32, x.shape, 1)
    return [jnp.where((lane >= h * HEAD_DIM) & (lane < (h + 1) * HEAD_DIM), x, jnp.zeros_like(x))
            for h in range(LANES // HEAD_DIM)]


def _chunk_valid(start):
    qi = lax.broadcasted_iota(jnp.int32, (QB_A, KW_A), 0)
    kj = lax.broadcasted_iota(jnp.int32, (QB_A, KW_A), 1)
    qc = qi // CHUNK
    kc = kj // CHUNK
    return (kc >= qc) & (kc <= qc + N_LEFT) & (kj + start >= PAD_A)


def _scaled(q):
    return q * (HEAD_DIM ** -0.5)


def _chunk_probs(q, k, bias, valid):
    s = lax.dot_general(q, k, (NT, ((), ())), preferred_element_type=f32) + bias
    s = jnp.where(valid, s, -1e30)
    p = jnp.exp(s - jnp.max(s, axis=-1, keepdims=True))
    return p / jnp.sum(p, axis=-1, keepdims=True)


def _chunk_attn_fwd(proj, kpad, vpad, bias, gather):
    t = proj.shape[0]
    tp = kpad.shape[0]
    step = QSUB_A * QB_A

    def body(q_ref, k_ref, v_ref, b_ref, o_ref):
        for sb in range(QSUB_A):
            start = pl.multiple_of((pl.program_id(1) * QSUB_A + sb) * QB_A, QB_A)
            rows = pl.ds(sb * QB_A, QB_A)
            valid = _chunk_valid(start)
            kw = k_ref[pl.ds(start, KW_A), :]
            qm = _mask_heads(_scaled(q_ref[rows, :]))
            vm = _mask_heads(v_ref[pl.ds(start, KW_A), :])
            o = None
            for h in range(len(qm)):
                p = _chunk_probs(qm[h], kw, b_ref[h], valid)
                d = jnp.dot(p.astype(bf16), vm[h], preferred_element_type=f32)
                o = d if o is None else o + d
            o_ref[rows, :] = o.astype(bf16)

    kv_spec = pl.BlockSpec((tp, LANES), lambda hp, qb: (0, hp))
    outs, new = _call(
        body, [proj, kpad, vpad, bias], grid=(A_W // LANES, t // step), name="chunk_attn_fwd",
        in_specs=[pl.BlockSpec((step, LANES), lambda hp, qb: (qb, hp)), kv_spec, kv_spec,
                  pl.BlockSpec((2, QB_A, KW_A), lambda hp, qb: (hp, 0, 0))],
        out_specs=[pl.BlockSpec((step, LANES), lambda hp, qb: (qb, hp))],
        out_shape=[SDS((t, A_W), bf16)], sem=("parallel", "arbitrary"), gather=gather)
    return outs[0], new


def _chunk_attn_bwd(proj, kpad, vpad, bias, dout, gather):
    t = proj.shape[0]
    tp = kpad.shape[0]
    step = QSUB_A * QB_A

    def body(q_ref, k_ref, v_ref, b_ref, do_ref, dq_ref, dk_ref, dv_ref, db_ref):
        qb = pl.program_id(1)

        @pl.when(qb == 0)
        def _():
            dk_ref[...] = jnp.zeros_like(dk_ref)
            dv_ref[...] = jnp.zeros_like(dv_ref)
            db_ref[...] = jnp.zeros_like(db_ref)

        for sb in range(QSUB_A):
            start = pl.multiple_of((qb * QSUB_A + sb) * QB_A, QB_A)
            rows = pl.ds(sb * QB_A, QB_A)
            win = pl.ds(start, KW_A)
            valid = _chunk_valid(start)
            kw = k_ref[win, :]
            vw = v_ref[win, :]
            qm = _mask_heads(_scaled(q_ref[rows, :]))
            dom = _mask_heads(do_ref[rows, :])
            km = _mask_heads(kw)
            dq = dk = dv = None
            for h in range(len(qm)):
                p = _chunk_probs(qm[h], kw, b_ref[h], valid)
                dp = lax.dot_general(dom[h], vw, (NT, ((), ())), preferred_element_type=f32)
                ds = p * (dp - jnp.sum(dp * p, axis=-1, keepdims=True))
                db_ref[h] += ds
                dsb = ds.astype(bf16)
                terms = (jnp.dot(dsb, km[h], preferred_element_type=f32),
                         lax.dot_general(dsb, qm[h], (TN, ((), ())), preferred_element_type=f32),
                         lax.dot_general(p.astype(bf16), dom[h], (TN, ((), ())), preferred_element_type=f32))
                dq, dk, dv = terms if dq is None else (dq + terms[0], dk + terms[1], dv + terms[2])
            dq_ref[rows, :] = _scaled(dq).astype(bf16)
            dk_ref[win, :] += dk
            dv_ref[win, :] += dv

    kv_spec = pl.BlockSpec((tp, LANES), lambda hp, qb: (0, hp))
    q_spec = pl.BlockSpec((step, LANES), lambda hp, qb: (qb, hp))
    b_spec = pl.BlockSpec((2, QB_A, KW_A), lambda hp, qb: (hp, 0, 0))
    return _call(
        body, [proj, kpad, vpad, bias, dout], grid=(A_W // LANES, t // step), name="chunk_attn_bwd",
        in_specs=[q_spec, kv_spec, kv_spec, b_spec, q_spec],
        out_specs=[q_spec, kv_spec, kv_spec, b_spec],
        out_shape=[SDS((t, A_W), bf16), SDS((tp, A_W), f32), SDS((tp, A_W), f32),
                   SDS((2 * A_W // LANES, QB_A, KW_A), f32)],
        sem=("parallel", "arbitrary"), gather=gather)


def _bias_ext(table):
    flat = PAD_A + QB_A - 1 - REL_CLIP
    top = jnp.broadcast_to(table[:, 2 * REL_CLIP:], (table.shape[0], flat))
    lo = 2 * REL_CLIP - (EXT_A - 1 - flat)
    return jnp.concatenate([top, jnp.flip(table[:, lo:], axis=1)], axis=1)


def _bias_window(table):
    nh = table.shape[0]
    e = jnp.broadcast_to(_bias_ext(table)[:, None, :], (nh, QB_A, EXT_A)).reshape(nh, QB_A * EXT_A)
    m = e[:, :QB_A * (EXT_A - 1)].reshape(nh, QB_A, EXT_A - 1)
    return m[:, :, QB_A - 1:]


def _bias_window_grad(dbias):
    nh = dbias.shape[0]
    m = jnp.pad(dbias, ((0, 0), (0, 0), (QB_A - 1, 0))).reshape(nh, QB_A * (EXT_A - 1))
    dext = jnp.sum(jnp.pad(m, ((0, 0), (0, QB_A))).reshape(nh, QB_A, EXT_A), axis=1)
    flat = PAD_A + QB_A - 1 - REL_CLIP
    lo = 2 * REL_CLIP - (EXT_A - 1 - flat)
    tail = jnp.flip(dext[:, flat:], axis=1)
    tail = tail.at[:, -1].add(jnp.sum(dext[:, :flat], axis=1))
    return jnp.pad(tail, ((0, 0), (lo, 0)))


def _tri_suffix(x, tri):
    hi = x.astype(bf16)
    lo = (x - hi.astype(f32)).astype(bf16)
    return jnp.dot(hi, tri, preferred_element_type=f32) + jnp.dot(lo, tri, preferred_element_type=f32)


def _sb_block(q, k, run, tri, causal):
    z = lax.dot_general(q, k, (NT, ((), ())), preferred_element_type=f32)
    e = jnp.exp(-jnp.abs(z))
    l1p = jnp.log(1.0 + e)
    lb = jnp.minimum(z, 0.0) - l1p
    lmb = lb - z
    if causal is not None:
        lmb = jnp.where(causal, lmb, 0.0)
    cs = _tri_suffix(lmb, tri)
    w = jnp.exp(lb + (run + cs - lmb))
    if causal is not None:
        w = jnp.where(causal, w, 0.0)
    return z, e, w, run + cs[:, 0:1]


def _sb_tri():
    r = lax.broadcasted_iota(jnp.int32, (SB_BLK, SB_BLK), 0)
    c = lax.broadcasted_iota(jnp.int32, (SB_BLK, SB_BLK), 1)
    return (r >= c).astype(bf16), c < r


def _sb_live(runs):
    m = runs[0]
    for r in runs[1:]:
        m = jnp.maximum(m, r)
    return jnp.max(m) > SB_DEAD


def _sb_fwd(proj, gather):
    t = proj.shape[0]
    cb = A_W // LANES
    nh = LANES // HEAD_DIM

    step_rows = QSUB_B * SB_BLK

    def body(q_ref, k_ref, v_ref, o_ref, of_ref):
        tri, diag = _sb_tri()
        for sb in range(QSUB_B):
            _sb_fwd_block(pl.program_id(1) * QSUB_B + sb, pl.ds(sb * SB_BLK, SB_BLK), tri, diag,
                          q_ref, k_ref, v_ref, o_ref, of_ref)

    def _sb_fwd_block(qb, qrows, tri, diag, q_ref, k_ref, v_ref, o_ref, of_ref):
        qm = _mask_heads(_scaled(q_ref[qrows, :]))

        def pair(kb, carry, causal):
            rows = pl.ds(pl.multiple_of(kb * SB_BLK, SB_BLK), SB_BLK)
            k = k_ref[rows, :]
            vm = _mask_heads(v_ref[rows, :])
            runs, acc = [], carry[nh]
            for h in range(nh):
                _, _, w, run = _sb_block(qm[h], k, carry[h], tri, causal)
                acc = acc + jnp.dot(w.astype(bf16), vm[h], preferred_element_type=f32)
                runs.append(run)
            return (*runs, acc)

        zero = jnp.zeros((SB_BLK, 1), f32)
        carry = pair(qb, (zero,) * nh + (jnp.zeros((SB_BLK, LANES), f32),), diag)

        def cond(st):
            return (st[0] < qb) & _sb_live(st[1][:nh])

        def step(st):
            return st[0] + 1, pair(qb - 1 - st[0], st[1], None)

        _, carry = lax.while_loop(cond, step, (jnp.int32(0), carry))
        o_ref[qrows, :] = carry[nh].astype(bf16)
        of_ref[qrows, :] = carry[nh]

    ospec = pl.BlockSpec((step_rows, LANES), lambda hp, qb: (qb, hp))
    return _call(
        body, [proj, proj, proj], grid=(cb, t // step_rows), name="sb_attn_fwd",
        in_specs=[pl.BlockSpec((step_rows, LANES), lambda hp, qb: (qb, 3 * cb + hp)),
                  pl.BlockSpec((t, LANES), lambda hp, qb: (0, 4 * cb + hp)),
                  pl.BlockSpec((t, LANES), lambda hp, qb: (0, 5 * cb + hp))],
        out_specs=[ospec, ospec], out_shape=[SDS((t, A_W), bf16), SDS((t, A_W), f32)],
        sem=("parallel", "arbitrary"), gather=gather)


def _sb_bwd(proj, out_b, dout, gather):
    t = proj.shape[0]
    cb = A_W // LANES
    nh = LANES // HEAD_DIM

    step_rows = QSUB_B * SB_BLK

    def body(q_ref, k_ref, v_ref, o_ref, do_ref, dq_ref, dk_ref, dv_ref):
        tri, diag = _sb_tri()

        @pl.when(pl.program_id(1) == 0)
        def _():
            dk_ref[...] = jnp.zeros_like(dk_ref)
            dv_ref[...] = jnp.zeros_like(dv_ref)

        for sb in range(QSUB_B):
            _sb_bwd_block(pl.program_id(1) * QSUB_B + sb, pl.ds(sb * SB_BLK, SB_BLK), tri, diag,
                          q_ref, k_ref, v_ref, o_ref, do_ref, dq_ref, dk_ref, dv_ref)

    def _sb_bwd_block(qb, qrows, tri, diag, q_ref, k_ref, v_ref, o_ref, do_ref, dq_ref, dk_ref, dv_ref):
        qm = _mask_heads(_scaled(q_ref[qrows, :]))
        do = do_ref[qrows, :]
        dom = _mask_heads(do)
        dsums = [jnp.sum(t_, axis=-1, keepdims=True) for t_ in _mask_heads(do.astype(f32) * o_ref[qrows, :])]

        def pair(kb, carry, causal):
            rows = pl.ds(pl.multiple_of(kb * SB_BLK, SB_BLK), SB_BLK)
            k = k_ref[rows, :]
            v = v_ref[rows, :]
            km = _mask_heads(k)
            new, dq, dk, dv = [], carry[2 * nh], None, None
            for h in range(nh):
                z, e, w, run = _sb_block(qm[h], k, carry[2 * h], tri, causal)
                inv = 1.0 / (1.0 + e)
                beta = jnp.where(z >= 0.0, inv, e * inv)
                wb = w.astype(bf16)
                g = lax.dot_general(dom[h], v, (NT, ((), ())), preferred_element_type=f32) * wb.astype(f32)
                sg = _tri_suffix(g, tri)
                dz = g - (g + (dsums[h] - carry[2 * h + 1] - sg)) * beta
                if causal is not None:
                    dz = jnp.where(causal, dz, 0.0)
                dzb = dz.astype(bf16)
                dq = dq + jnp.dot(dzb, km[h], preferred_element_type=f32)
                tk = lax.dot_general(dzb, qm[h], (TN, ((), ())), preferred_element_type=f32)
                tv = lax.dot_general(wb, dom[h], (TN, ((), ())), preferred_element_type=f32)
                dk, dv = (tk, tv) if dk is None else (dk + tk, dv + tv)
                new += [run, carry[2 * h + 1] + sg[:, 0:1]]
            dk_ref[rows, :] += dk
            dv_ref[rows, :] += dv
            return (*new, dq)

        zero = jnp.zeros((SB_BLK, 1), f32)
        carry = pair(qb, (zero,) * (2 * nh) + (jnp.zeros((SB_BLK, LANES), f32),), diag)

        def cond(st):
            return (st[0] < qb) & _sb_live(st[1][0:2 * nh:2])

        def step(st):
            return st[0] + 1, pair(qb - 1 - st[0], st[1], None)

        _, carry = lax.while_loop(cond, step, (jnp.int32(0), carry))
        dq_ref[qrows, :] = _scaled(carry[2 * nh]).astype(bf16)

    kv_in = lambda seg: pl.BlockSpec((t, LANES), lambda hp, qb: (0, seg * cb + hp))
    q_spec = pl.BlockSpec((step_rows, LANES), lambda hp, qb: (qb, hp))
    kv_out = pl.BlockSpec((t, LANES), lambda hp, qb: (0, hp))
    return _call(
        body, [proj, proj, proj, out_b, dout], grid=(cb, t // step_rows), name="sb_attn_bwd",
        in_specs=[pl.BlockSpec((step_rows, LANES), lambda hp, qb: (qb, 3 * cb + hp)), kv_in(4), kv_in(5),
                  q_spec, pl.BlockSpec((step_rows, LANES), lambda hp, qb: (qb, cb + hp))],
        out_specs=[q_spec, kv_out, kv_out],
        out_shape=[SDS((t, A_W), bf16), SDS((t, A_W), f32), SDS((t, A_W), f32)],
        sem=("parallel", "arbitrary"), gather=gather)


def _halo_specs(tr, w, col, nblk):
    per = tr // SUBLANES
    cur = pl.BlockSpec((tr, w), lambda i: (i, col))
    prev = pl.BlockSpec((SUBLANES, w), lambda i: (jnp.maximum(i * per - 1, 0), col))
    nxt = pl.BlockSpec((SUBLANES, w), lambda i: (jnp.minimum((i + 1) * per, nblk * per - 1), col))
    return cur, prev, nxt


def _taps_before(cur, prev8, first):
    prev8 = jnp.where(first, 0.0, prev8)
    ext = jnp.concatenate([prev8, cur], axis=0)
    return [pltpu.roll(ext, s, 0)[SUBLANES:] for s in (3, 2, 1)]


def _taps_after(cur, next8, last):
    n = cur.shape[0]
    next8 = jnp.where(last, 0.0, next8)
    ext = jnp.concatenate([cur, next8], axis=0)
    return [pltpu.roll(ext, n + SUBLANES - s, 0)[:n] for s in (1, 2, 3)]


def _block_diag(x, w_ref, dims):
    outs = [lax.dot_general(x[:, n * LRU_BW:(n + 1) * LRU_BW], w_ref[n], (dims, ((), ())),
                            preferred_element_type=f32) for n in range(LRU_BLOCKS)]
    return jnp.concatenate(outs, axis=1)


def _lru_gates(xc, wa_ref, wi_ref, ba, bi, lam):
    xb = xc.astype(bf16)
    r = jax.nn.sigmoid(_block_diag(xb, wa_ref, NN) + ba)
    ig = jax.nn.sigmoid(_block_diag(xb, wi_ref, NN) + bi)
    sp = jnp.maximum(-lam, 0.0) + jnp.log(1.0 + jnp.exp(-jnp.abs(lam)))
    log_a = -LRU_C * r * sp
    a = jnp.exp(log_a)
    x2 = 2.0 * log_a
    one_minus = jnp.where(x2 > -1e-2, -x2 * (1.0 + x2 * (0.5 + x2 * (1.0 / 6.0))), 1.0 - a * a)
    mult = jnp.sqrt(one_minus)
    return xb, r, ig, sp, a, mult


def _rg_gates_fwd(proj, conv_w, conv_b, wa, wi, ba, bi, lam, tr=512):
    t = proj.shape[0]
    w = D_MODEL
    tr = min(tr, t)
    nblk = t // tr
    cur, prev, _ = _halo_specs(tr, w, 1, nblk)

    def body(x_ref, xp_ref, cw_ref, cb_ref, wa_ref, wi_ref, ba_ref, bi_ref, lam_ref, xc_ref, a_ref, u_ref):
        x = x_ref[...]
        taps = _taps_before(x, xp_ref[...], pl.program_id(0) == 0) + [x]
        xc = cb_ref[...]
        for k in range(4):
            xc = xc + cw_ref[k:k + 1, :] * taps[k]
        _, _, ig, _, a, mult = _lru_gates(xc, wa_ref, wi_ref, ba_ref[...], bi_ref[...], lam_ref[...])
        xc_ref[...] = xc
        a_ref[...] = a
        u_ref[...] = mult * (ig * xc)

    full = lambda a_: pl.BlockSpec(a_.shape, lambda i, nd=a_.ndim: (0,) * nd)
    ospec = pl.BlockSpec((tr, w), lambda i: (i, 0))
    return pl.pallas_call(
        body, grid=(nblk,), name="rg_gates_fwd",
        in_specs=[cur, prev] + [full(a_) for a_ in (conv_w, conv_b, wa, wi, ba, bi, lam)],
        out_specs=[ospec] * 3, out_shape=[SDS((t, w), f32)] * 3,
        compiler_params=_cparams(("parallel",)))(proj, proj, conv_w, conv_b, wa, wi, ba, bi, lam)


def _lru_scan(name, a, b, reverse, tt=1024):
    t, w = a.shape
    tt = min(tt, t)
    nt = t // tt
    ng = tt // SUBLANES

    def body(a_ref, b_ref, h_ref, carry_ref):
        @pl.when(pl.program_id(0) == 0)
        def _():
            carry_ref[...] = jnp.zeros_like(carry_ref)

        row = lax.broadcasted_iota(jnp.int32, (SUBLANES, w), 0)

        def group(gi, carry):
            g = (ng - 1 - gi) if reverse else gi
            rows = pl.ds(pl.multiple_of(g * SUBLANES, SUBLANES), SUBLANES)
            av = a_ref[rows, :]
            bv = b_ref[rows, :]
            for s in (1, 2, 4):
                sh = (SUBLANES - s) if reverse else s
                ok = (row < SUBLANES - s) if reverse else (row >= s)
                a_s = pltpu.roll(av, sh, 0)
                b_s = pltpu.roll(bv, sh, 0)
                bv = jnp.where(ok, av * b_s + bv, bv)
                av = jnp.where(ok, av * a_s, av)
            h = av * carry + bv
            h_ref[rows, :] = h
            edge = h[0:1, :] if reverse else h[SUBLANES - 1:SUBLANES, :]
            return jnp.broadcast_to(edge, (SUBLANES, w))

        carry_ref[...] = lax.fori_loop(0, ng, group, carry_ref[...], unroll=4)

    tmap = (lambda i: (nt - 1 - i, 0)) if reverse else (lambda i: (i, 0))
    spec = pl.BlockSpec((tt, w), tmap)
    return pl.pallas_call(
        body, grid=(nt,), name=name, in_specs=[spec, spec], out_specs=spec,
        out_shape=SDS((t, w), f32), scratch_shapes=[pltpu.VMEM((SUBLANES, w), f32)],
        compiler_params=_cparams(("arbitrary",)))(a, b)


def _rg_gates_bwd(dhs, c, hs, xc, wa, wi, ba, bi, lam, tr=512):
    t, w = xc.shape
    tr = min(tr, t)
    nblk = t // tr
    cur, prev, nxt = _halo_specs(tr, w, 0, nblk)

    def body(dhs_ref, c_ref, cn_ref, hs_ref, hp_ref, xc_ref, wa_ref, wi_ref, ba_ref, bi_ref, lam_ref,
             dxc_ref, dwa_ref, dwi_ref, dba_ref, dbi_ref, dlam_ref):
        i = pl.program_id(0)
        c_next = _taps_after(c_ref[...], cn_ref[...], i == nblk - 1)[0]
        h_prev = _taps_before(hs_ref[...], hp_ref[...], i == 0)[2]
        xc = xc_ref[...]
        lam = lam_ref[...]
        xb, r, ig, sp, a, mult = _lru_gates(xc, wa_ref, wi_ref, ba_ref[...], bi_ref[...], lam)
        dh = dhs_ref[...] + c_next
        dlog_a = dh * h_prev * a - (dh * ig * xc) * (a * a / mult)
        dpre_a = (dlog_a * (-LRU_C * sp) * r * (1.0 - r)).astype(bf16)
        dpre_i = (dh * mult * xc * ig * (1.0 - ig)).astype(bf16)
        dxc_ref[...] = (dh * mult * ig + _block_diag(dpre_a, wa_ref, NT) + _block_diag(dpre_i, wi_ref, NT))
        dsig = 1.0 / (1.0 + jnp.exp(lam))
        sums = [jnp.sum(dpre_a.astype(f32), axis=0, keepdims=True),
                jnp.sum(dpre_i.astype(f32), axis=0, keepdims=True),
                jnp.sum(dlog_a * (-LRU_C * r), axis=0, keepdims=True) * (-dsig)]

        @pl.when(i == 0)
        def _():
            dwa_ref[...] = jnp.zeros_like(dwa_ref)
            dwi_ref[...] = jnp.zeros_like(dwi_ref)
            dba_ref[...] = jnp.zeros_like(dba_ref)
            dbi_ref[...] = jnp.zeros_like(dbi_ref)
            dlam_ref[...] = jnp.zeros_like(dlam_ref)

        for n in range(LRU_BLOCKS):
            sl = slice(n * LRU_BW, (n + 1) * LRU_BW)
            dwa_ref[n] += lax.dot_general(xb[:, sl], dpre_a[:, sl], (TN, ((), ())), preferred_element_type=f32)
            dwi_ref[n] += lax.dot_general(xb[:, sl], dpre_i[:, sl], (TN, ((), ())), preferred_element_type=f32)
        dba_ref[...] += sums[0]
        dbi_ref[...] += sums[1]
        dlam_ref[...] += sums[2]

    full = lambda a_: pl.BlockSpec(a_.shape, lambda i, nd=a_.ndim: (0,) * nd)
    vec = pl.BlockSpec((1, w), lambda i: (0, 0))
    mat = pl.BlockSpec((LRU_BLOCKS, LRU_BW, LRU_BW), lambda i: (0, 0, 0))
    return pl.pallas_call(
        body, grid=(nblk,), name="rg_gates_bwd",
        in_specs=[cur, cur, nxt, cur, prev, cur] + [full(a_) for a_ in (wa, wi, ba, bi, lam)],
        out_specs=[cur, mat, mat, vec, vec, vec],
        out_shape=[SDS((t, w), f32), SDS((LRU_BLOCKS, LRU_BW, LRU_BW), f32), SDS((LRU_BLOCKS, LRU_BW, LRU_BW), f32),
                   SDS((1, w), f32), SDS((1, w), f32), SDS((1, w), f32)],
        compiler_params=_cparams(("arbitrary",)))(dhs, c, c, hs, hs, xc, wa, wi, ba, bi, lam)


def _rg_conv_bwd(dxc, proj, conv_w, tr=512):
    t, w = dxc.shape
    tr = min(tr, t)
    nblk = t // tr
    cur, _, nxt = _halo_specs(tr, w, 0, nblk)
    xcur, xprev, _ = _halo_specs(tr, w, 1, nblk)

    def body(d_ref, dn_ref, x_ref, xp_ref, cw_ref, dx_ref, dcw_ref, dcb_ref):
        i = pl.program_id(0)
        d = d_ref[...]
        x = x_ref[...]
        after = _taps_after(d, dn_ref[...], i == nblk - 1)
        before = _taps_before(x, xp_ref[...], i == 0) + [x]
        dx = cw_ref[3:4, :] * d
        for s in (1, 2, 3):
            dx = dx + cw_ref[3 - s:4 - s, :] * after[s - 1]
        dx_ref[...] = dx.astype(bf16)
        dcw = jnp.concatenate([jnp.sum(d * before[k], axis=0, keepdims=True) for k in range(4)], axis=0)
        dcb = jnp.sum(d, axis=0, keepdims=True)

        @pl.when(i == 0)
        def _():
            dcw_ref[...] = dcw
            dcb_ref[...] = dcb

        @pl.when(i > 0)
        def _():
            dcw_ref[...] += dcw
            dcb_ref[...] += dcb

    return pl.pallas_call(
        body, grid=(nblk,), name="rg_conv_bwd",
        in_specs=[cur, nxt, xcur, xprev, pl.BlockSpec((4, w), lambda i: (0, 0))],
        out_specs=[cur, pl.BlockSpec((4, w), lambda i: (0, 0)), pl.BlockSpec((1, w), lambda i: (0, 0))],
        out_shape=[SDS((t, w), bf16), SDS((4, w), f32), SDS((1, w), f32)],
        compiler_params=_cparams(("arbitrary",)))(dxc, dxc, proj, proj, conv_w)


def _attn_fwd(h, wts, j, plan):
    proj = _mm_cols("attn_in", h, wts["attn_w_in"], j, bf16)
    kpad = jnp.pad(proj[:, A_W:2 * A_W], ((PAD_A, 0), (0, 0)))
    vpad = jnp.pad(proj[:, 2 * A_W:3 * A_W], ((PAD_A, 0), (0, 0)))
    bias = _bias_window(wts["attn_rel_bias"][j])
    plan = plan if j == 0 else None
    out_a = _carried(plan, "chunk_attn_fwd", wts, _chunk_attn_fwd, proj, kpad, vpad, bias)
    out_b, out_b32 = _carried(plan, "sb_attn_fwd", wts, _sb_fwd, proj)
    m = _mm_rows("attn_out", [out_a, out_b], wts["attn_w_out"], j, f32)
    return m, (proj, kpad, vpad, bias, out_a, out_b, out_b32)


def _attn_bwd(dm, h, saved, wts, j, grads, exch):
    proj, kpad, vpad, bias, out_a, out_b, out_b32 = saved
    dout = _mm_rows_t("attn_out_t", dm, wts["attn_w_out"], j, bf16)
    gi, ll = _grad_slot("attn_w_out", j)
    grads["attn_w_out"][gi] = _mm_wgrad("attn_out_wgrad_a", out_a, dm, grads["attn_w_out"][gi], ll, 0)
    grads["attn_w_out"][gi] = _mm_wgrad("attn_out_wgrad_b", out_b, dm, grads["attn_w_out"][gi], ll, 1)
    if exch is not None and j == 0:
        (dqa, dka, dva, dbias), got = _chunk_attn_bwd(proj, kpad, vpad, bias, dout, exch.upper_carry(grads))
        exch.upper_got(got)
        (dqs, dks, dvs), slots = _sb_bwd(proj, out_b32, dout, exch.carry())
        exch.carried(slots)
    else:
        dqa, dka, dva, dbias = _chunk_attn_bwd(proj, kpad, vpad, bias, dout, None)[0]
        dqs, dks, dvs = _sb_bwd(proj, out_b32, dout, None)[0]
    grads["attn_rel_bias"][j] = _bias_window_grad(dbias)
    dproj = jnp.concatenate([dqa, dka[PAD_A:].astype(bf16), dva[PAD_A:].astype(bf16),
                             dqs, dks.astype(bf16), dvs.astype(bf16)], axis=1)
    gi, ll = _grad_slot("attn_w_in", j)
    grads["attn_w_in"][gi] = _mm_wgrad_cols("attn_in_wgrad", h, dproj, grads["attn_w_in"][gi], ll)
    return _mm_cols_t("attn_in_t", dproj, wts["attn_w_in"], j, f32)


def _rg_fwd(h, wts, j, plan):
    proj =_mm_cols("rg_in", h, wts["rg_w_in"], j, f32)
    small = [wts[k][j] for k in ("rg_conv_w", "rg_conv_b", "rg_w_a", "rg_w_i", "rg_b_a", "rg_b_i", "rg_lambda")]
    xc, a, u = _rg_gates_fwd(proj, *small)
    hs = _lru_scan("lru_scan_fwd", a, u, False)
    yp = _rows("rg_gate_out", lambda hv, gv: hv * _gelu(gv), [hs, (proj, D_MODEL, 0)], [], [(D_MODEL, bf16)])[0]
    m = _mm_rows("rg_out", [yp], wts["rg_w_out"], j, f32)
    return m, (proj, xc, a, hs, yp)


def _rg_bwd(dm, h, saved, wts, j, grads, exch):
    proj, xc, a, hs, yp = saved
    dyp = _mm_rows_t("rg_out_t", dm, wts["rg_w_out"], j, f32)
    gi, ll = _grad_slot("rg_w_out", j)
    grads["rg_w_out"][gi] = _mm_wgrad("rg_out_wgrad", yp, dm, grads["rg_w_out"][gi], ll)

    def gate_bwd(dy, hv, gv, av):
        dhs = dy * _gelu(gv)
        return dhs, av * dhs, dy * hv * _gelu_grad(gv)

    dhs, ab, dgate = _rows("rg_gate_out_bwd", gate_bwd, [dyp, hs, (proj, D_MODEL, 0), a], [],
                           [(D_MODEL, f32), (D_MODEL, f32), (D_MODEL, bf16)])
    c = _lru_scan("lru_scan_bwd", a, ab, True)
    wa, wi, ba, bi, lam = [wts[k][j] for k in ("rg_w_a", "rg_w_i", "rg_b_a", "rg_b_i", "rg_lambda")]
    dxc, dwa, dwi, dba, dbi, dlam = _rg_gates_bwd(dhs, c, hs, xc, wa, wi, ba, bi, lam)
    dxr, dcw, dcb = _rg_conv_bwd(dxc, proj, wts["rg_conv_w"][j])
    for k, v in (("rg_w_a", dwa), ("rg_w_i", dwi), ("rg_b_a", dba), ("rg_b_i", dbi), ("rg_lambda", dlam),
                 ("rg_conv_w", dcw), ("rg_conv_b", dcb)):
        grads[k][j] = v
    dproj = jnp.concatenate([dgate, dxr], axis=1)
    grads["rg_w_in"][gi] = _mm_wgrad_cols("rg_in_wgrad", h, dproj, grads["rg_w_in"][gi], ll)
    return _mm_cols_t("rg_in_t", dproj, wts["rg_w_in"], j, f32)


def _local_step(x, target, wts, plan=None, exch=None):
    t = x.shape[0]
    d = D_MODEL
    gains = {k: wts[k] for k in ("norm_mix_pre", "norm_mix_post", "norm_ffn_pre", "norm_ffn_post")}
    gain = lambda k, l: gains[k][l:l + 1]

    saved = []
    h = _rows("norm_in", _norm_fwd, [x], [gain("norm_mix_pre", 0)], [(d, bf16)])[0]
    loss_cols = None
    for l in range(DEPTH):
        j = l // 2
        m, mix_saved = (_attn_fwd if l % 2 == 0 else _rg_fwd)(h, wts, j, plan)

        def resid_next(xv, mv, g_post, g_next):
            x1 = xv + _norm_fwd(mv, g_post)
            return x1, _norm_fwd(x1, g_next)

        x1, h2 = _rows("resid_mix", resid_next, [x, m], [gain("norm_mix_post", l), gain("norm_ffn_pre", l)],
                       [(d, f32), (d, bf16)])
        g, u, hid = _carried(plan if l == 0 else None, "ffn_up", wts, _ffn_up, h2, wts["ffn_w_gate"],
                             wts["ffn_w_up"], l)
        f = _ffn_down(hid, wts["ffn_w_down"], l)
        saved.append((x, h, m, mix_saved, x1, h2, g, u, hid, f))
        if l + 1 < DEPTH:
            x, h = _rows("resid_ffn", resid_next, [x1, f], [gain("norm_ffn_post", l), gain("norm_mix_pre", l + 1)],
                         [(d, f32), (d, bf16)])
        else:
            def resid_loss(xv, fv, tv, g_post):
                err = xv + _norm_fwd(fv, g_post) - tv
                return err * (1.0 / d), jnp.sum(err * err, axis=0, keepdims=True)

            dx, loss_cols = _rows("resid_loss", resid_loss, [x1, f, target], [gain("norm_ffn_post", l)],
                                  [(d, f32)], [((1, d), f32)])
    loss = 0.5 * jnp.sum(loss_cols) / d

    grads = {k: {} for k in SMALL_GRADS}
    for k in BIG_GRADS:
        shp = wts[k].shape
        rest = shp[2:] if shp[1] == 1 else shp[1:]
        grads[k] = [_Fresh((LOWER_LAYERS[k],) + rest), _Fresh((shp[0] - LOWER_LAYERS[k],) + rest)]

    def norm_bwd_cast(uv, dyv, gv):
        du, dg = _norm_bwd(uv, dyv, gv)
        return du, dg

    def norm_bwd_resid(uv, dhv, dxv, gv):
        du, dg = _norm_bwd(uv, dhv, gv)
        return dxv + du, dg

    def norm_bwd_pair(uv, dhv, dxv, nv, g_pre, g_post):
        dx_, dg_pre = norm_bwd_resid(uv, dhv, dxv, g_pre)
        dn, dg_post = _norm_bwd(nv, dx_, g_post)
        return dx_, dn, dg_pre, dg_post

    df = None
    for l in reversed(range(DEPTH)):
        j = l // 2
        x_in, h, m, mix_saved, x1, h2, g, u, hid, f = saved[l]
        if df is None:
            df, grads["norm_ffn_post"][l] = _rows("norm_ffn_post_bwd", norm_bwd_cast, [f, dx],
                                                  [gain("norm_ffn_post", l)], [(d, bf16)], [((1, d), f32)])
        dg, du = _ffn_down_bwd(df, wts["ffn_w_down"], l, g, u)
        gi, ll = _grad_slot("ffn_w_down", l)
        grads["ffn_w_down"][gi] = _ffn_wgrad_down(hid, df, grads["ffn_w_down"][gi], ll)
        dh2 = _ffn_up_bwd(dg, du, wts["ffn_w_gate"], wts["ffn_w_up"], l)
        grads["ffn_w_gate"][gi], grads["ffn_w_up"][gi] = _ffn_wgrad_up(
            h2, dg, du, grads["ffn_w_gate"][gi], grads["ffn_w_up"][gi], ll)
        dx1, dm, grads["norm_ffn_pre"][l], grads["norm_mix_post"][l] = _rows(
            "norm_ffn_mix_bwd", norm_bwd_pair, [x1, dh2, dx, m], [gain("norm_ffn_pre", l), gain("norm_mix_post", l)],
            [(d, f32), (d, bf16)], [((1, d), f32), ((1, d), f32)])
        dh = (_attn_bwd if l % 2 == 0 else _rg_bwd)(dm, h, mix_saved, wts, j, grads, exch)
        if l > 0:
            dx, df, grads["norm_mix_pre"][l], grads["norm_ffn_post"][l - 1] = _rows(
                "norm_mix_ffn_bwd", norm_bwd_pair, [x_in, dh, dx1, saved[l - 1][9]],
                [gain("norm_mix_pre", l), gain("norm_ffn_post", l - 1)],
                [(d, f32), (d, bf16)], [((1, d), f32), ((1, d), f32)])
        else:
            dx, grads["norm_mix_pre"][l] = _rows("norm_mix_pre_bwd", norm_bwd_resid, [x_in, dh, dx1],
                                                 [gain("norm_mix_pre", l)], [(d, f32)], [((1, d), f32)])
    return loss, dx, grads


ANY = pl.BlockSpec(memory_space=pl.ANY)
PACK_COLS = 1024
SMALL_ROWS = 288


def _mesh_pos():
    x, y, c = lax.axis_index("x"), lax.axis_index("y"), lax.axis_index("c")
    return x, y, c, [(1 - x, y), (x, 1 - y), (1 - x, 1 - y)]


def _run_copies(copies):
    for cp in copies:
        cp.start()
    for cp in copies:
        cp.wait()


GATHER_SEMS = 7


def _gather_copies(items, ins, outs, send, recv):
    x, y, c, chips = _mesh_pos()
    q = 2 * x + y
    sibling = (x, y, 1 - c)

    def copy(k, src, dst, to):
        return pltpu.make_async_remote_copy(src_ref=src, dst_ref=dst, send_sem=send.at[k], recv_sem=recv.at[k],
                                            device_id=to, device_id_type=MESH)

    own, sent, passed = [], [], []
    for i, (t, l0, nl) in enumerate(items):
        lay = pl.ds(l0, nl)
        half = ins[t].shape[1] // 2
        rows = pl.ds(pl.multiple_of(c * half, half), half)
        own.append(copy(GATHER_SEMS * i, ins[t].at[lay], outs[t].at[lay, q], sibling))
        for j, (px, py) in enumerate(chips):
            sent.append(copy(GATHER_SEMS * i + 1 + j, ins[t].at[lay, rows], outs[t].at[lay, q, rows], (px, py, c)))
            landed = outs[t].at[lay, 2 * px + py, rows]
            passed.append(copy(GATHER_SEMS * i + 4 + j, landed, landed, sibling))
    return own, sent, passed


def _gather_start(items, ins, outs, send, recv):
    own, sent, _ = _gather_copies(items, ins, outs, send, recv)
    for cp in own + sent:
        cp.start()


def _gather_finish(items, ins, outs, send, recv):
    own, sent, passed = _gather_copies(items, ins, outs, send, recv)
    for arrived, forward in zip(sent, passed):
        arrived.wait_recv()
        forward.start()
    for cp in sent:
        cp.wait_send()
    for cp in own + passed:
        cp.wait()


def _gather_call(items, shards):
    n = len(shards)
    nsem = GATHER_SEMS * len(items)

    def body(*refs):
        ins, outs = refs[:n], refs[n:2 * n]
        _gather_start(items, ins, outs, *refs[2 * n:])
        _gather_finish(items, ins, outs, *refs[2 * n:])

    return pl.pallas_call(
        body, name="weight_all_gather", in_specs=[ANY] * n, out_specs=[ANY] * n,
        out_shape=[SDS((s.shape[0], N_CHIPS) + s.shape[1:], s.dtype) for s in shards],
        scratch_shapes=[pltpu.SemaphoreType.DMA((nsem,)), pltpu.SemaphoreType.DMA((nsem,))])(*shards)


def _call(body, operands, *, name, grid, in_specs, out_specs, out_shape, sem, scratch=(), gather=None):
    if gather is None:
        return pl.pallas_call(body, grid=grid, in_specs=in_specs, out_specs=out_specs, out_shape=out_shape,
                              scratch_shapes=list(scratch), name=name, compiler_params=_cparams(sem))(*operands), None
    start, finish, c_ins, c_io, c_new, nsem = gather
    n_in, n_out, n_scr = len(operands), len(out_shape), len(scratch)
    ni, nio, nco = len(c_ins), len(c_io), len(c_io) + len(c_new)

    def full(*refs):
        ins, sh = refs[:n_in], refs[n_in:n_in + ni]
        outs = refs[n_in + ni + nio:n_in + ni + nio + n_out]
        co = refs[n_in + ni + nio + n_out:n_in + ni + nio + n_out + nco]
        scr = refs[n_in + ni + nio + n_out + nco:]
        ids = [pl.program_id(a) for a in range(len(grid))]
        first = functools.reduce(jnp.logical_and, [i == 0 for i in ids])
        last = functools.reduce(jnp.logical_and, [i == g - 1 for i, g in zip(ids, grid)])

        @pl.when(first)
        def _():
            start(sh, co, scr[n_scr], scr[n_scr + 1])

        body(*ins, *outs, *scr[:n_scr])

        @pl.when(last)
        def _():
            finish(sh, co, scr[n_scr], scr[n_scr + 1])

    res = pl.pallas_call(
        full, grid=grid, in_specs=list(in_specs) + [ANY] * (ni + nio), out_specs=list(out_specs) + [ANY] * nco,
        out_shape=list(out_shape) + [SDS(g.shape, g.dtype) for g in list(c_io) + list(c_new)],
        scratch_shapes=list(scratch) + [pltpu.SemaphoreType.DMA((nsem,)), pltpu.SemaphoreType.DMA((nsem,))],
        input_output_aliases={n_in + ni + t: n_out + t for t in range(nio)}, name=name,
        compiler_params=_cparams(("arbitrary",) * len(grid)))(*operands, *c_ins, *c_io)
    return res[:n_out], res[n_out:]


def _pair_exchange(gs):
    n = len(gs)

    def body(*refs):
        _pair_copies(refs[:n], refs[n:2 * n], *refs[2 * n:], start=True)
        _pair_copies(refs[:n], refs[n:2 * n], *refs[2 * n:], start=False)

    return pl.pallas_call(
        body, name="grad_pair_exchange", in_specs=[ANY] * n, out_specs=[ANY] * n,
        out_shape=_pair_shapes(gs),
        scratch_shapes=[pltpu.SemaphoreType.DMA((n,)), pltpu.SemaphoreType.DMA((n,))])(*gs)


def _pair_shapes(gs):
    return [SDS(g.shape[:2] + (g.shape[2] // 2, g.shape[3]), f32) for g in gs]


def _pair_copies(ins, outs, send, recv, start):
    x, y, c, _ = _mesh_pos()
    for t in range(len(ins)):
        half = ins[t].shape[2] // 2
        src = ins[t].at[:, :, pl.ds(pl.multiple_of((1 - c) * half, SUBLANES), half)]
        cp = pltpu.make_async_remote_copy(src_ref=src, dst_ref=outs[t], send_sem=send.at[t], recv_sem=recv.at[t],
                                          device_id=(x, y, 1 - c), device_id_type=MESH)
        cp.start() if start else cp.wait()


def _pair_carry(gs):
    return (functools.partial(_pair_copies, start=True), functools.partial(_pair_copies, start=False),
            gs, [], _pair_shapes(gs), len(gs))


def _pair_sum(name, g, got, c):
    l, s, r, cols = g.shape

    def body(c_ref, a_ref, b_ref, o_ref):
        o_ref[...] = (a_ref[...] + b_ref[...]).astype(bf16)

    blk = (None, None, r // 2, cols)
    return pl.pallas_call(
        body, name=name, out_shape=SDS(got.shape, bf16),
        grid_spec=pltpu.PrefetchScalarGridSpec(
            num_scalar_prefetch=1, grid=(l, s),
            in_specs=[pl.BlockSpec(blk, lambda i, q, c_ref: (i, q, c_ref[0], 0)),
                      pl.BlockSpec(blk, lambda i, q, c_ref: (i, q, 0, 0))],
            out_specs=pl.BlockSpec(blk, lambda i, q, c_ref: (i, q, 0, 0))),
        compiler_params=_cparams(("parallel", "parallel")))(c, g, got)


def _chip_exchange(hs):
    n = len(hs)

    def body(*refs):
        _chip_copies(refs[:n], refs[n:2 * n], *refs[2 * n:], start=True)
        _chip_copies(refs[:n], refs[n:2 * n], *refs[2 * n:], start=False)

    return pl.pallas_call(
        body, name="grad_chip_exchange", in_specs=[ANY] * n, out_specs=[ANY] * n,
        out_shape=[SDS(h.shape, h.dtype) for h in hs],
        scratch_shapes=[pltpu.SemaphoreType.DMA((3 * n,)), pltpu.SemaphoreType.DMA((3 * n,))])(*hs)


def _chip_copies(ins, outs, send, recv, start):
    x, y, c, chips = _mesh_pos()
    q = 2 * x + y
    for t in range(len(ins)):
        for j, (px, py) in enumerate(chips):
            cp = pltpu.make_async_remote_copy(
                src_ref=ins[t].at[:, 2 * px + py], dst_ref=outs[t].at[:, q], send_sem=send.at[3 * t + j],
                recv_sem=recv.at[3 * t + j], device_id=(px, py, c), device_id_type=MESH)
            cp.start() if start else cp.wait()


def _chip_carry(hs):
    return (functools.partial(_chip_copies, start=True), functools.partial(_chip_copies, start=False),
            hs, [], [SDS(h.shape, h.dtype) for h in hs], 3 * len(hs))


def _chip_sum(name, s, h, pos, l0, layers, into):
    l, _, r, cols = s.shape

    def body(pos_ref, s0, s1, s2, s3, own_ref, *rest):
        vals = [jnp.where(pos_ref[0] == p, own_ref[...], ref[...]).astype(f32) for p, ref in enumerate((s0, s1, s2, s3))]
        rest[-1][...] = ((vals[0] + vals[1]) + vals[2]) + vals[3]

    blk = (None, None, r, cols)
    slot = lambda p: pl.BlockSpec(blk, lambda i, pos_ref: (i, jnp.where(pos_ref[0] == p, (p + 1) % N_CHIPS, p), 0, 0))
    extra, alias = ([], {}) if into is None else ([into], {6: 0})
    return pl.pallas_call(
        body, name=name, out_shape=SDS((layers, 2 * r, cols), f32), input_output_aliases=alias,
        grid_spec=pltpu.PrefetchScalarGridSpec(
            num_scalar_prefetch=1, grid=(l,),
            in_specs=[slot(p) for p in range(N_CHIPS)] + [pl.BlockSpec(blk, lambda i, pos_ref: (i, pos_ref[0], 0, 0))]
            + [ANY] * len(extra),
            out_specs=pl.BlockSpec((None, r, cols), lambda i, pos_ref: (l0 + i, pos_ref[1], 0))),
        compiler_params=_cparams(("parallel",)))(pos, s, s, s, s, h, *extra)


def _pair_gather(fulls):
    n = len(fulls)

    def body(*refs):
        ins, outs = refs[:n], refs[n:2 * n]
        send, recv = refs[2 * n:]
        x, y, c, _ = _mesh_pos()
        copies = []
        for t in range(n):
            half = outs[t].shape[1] // 2
            rows = outs[t].at[:, pl.ds(pl.multiple_of(c * half, SUBLANES), half)]
            copies.append(pltpu.make_async_remote_copy(
                src_ref=rows, dst_ref=rows, send_sem=send.at[t], recv_sem=recv.at[t],
                device_id=(x, y, 1 - c), device_id_type=MESH))
        _run_copies(copies)

    return pl.pallas_call(
        body, name="grad_pair_gather", in_specs=[ANY] * n, out_specs=[ANY] * n,
        out_shape=[SDS(f.shape, f32) for f in fulls], input_output_aliases={t: t for t in range(n)},
        scratch_shapes=[pltpu.SemaphoreType.DMA((n,)), pltpu.SemaphoreType.DMA((n,))])(*fulls)


COL_SHARDED = ("attn_w_in", "rg_w_in", "ffn_w_gate", "ffn_w_up")
ROW_SHARDED = ("attn_w_out", "rg_w_out")
GATES = ("rg_w_a", "rg_w_i")
VECTORS = ("rg_conv_w", "rg_conv_b", "rg_b_a", "rg_b_i", "rg_lambda")
REPLICATED = ("norm_mix_pre", "norm_mix_post", "norm_ffn_pre", "norm_ffn_post", "attn_rel_bias")
BIG_GRADS = COL_SHARDED + ROW_SHARDED + ("ffn_w_down",)
SMALL_GRADS = GATES + VECTORS + REPLICATED
WEIGHTS =("attn_w_in", "attn_rel_bias", "attn_w_out", "rg_w_in", "rg_conv_w", "rg_conv_b", "rg_w_a", "rg_b_a",
           "rg_w_i", "rg_b_i", "rg_lambda", "rg_w_out", "norm_mix_pre", "norm_mix_post", "norm_ffn_pre",
           "norm_ffn_post", "ffn_w_gate", "ffn_w_up", "ffn_w_down")
SMALL = VECTORS + REPLICATED


GATHER_PARTS = {
    "first": (("attn_w_in", 0, 1), ("attn_w_out", 0, 1), ("rg_w_a", 0, 8), ("rg_w_i", 0, 8), ("vec", 0, 1)),
    "chunk_attn_fwd": (("ffn_w_gate", 0, 1), ("ffn_w_up", 0, 1), ("ffn_w_down", 0, 1), ("rg_w_in", 0, 1),
                       ("rg_w_out", 0, 1)),
    "sb_attn_fwd": (("ffn_w_gate", 1, 3), ("ffn_w_up", 1, 3), ("ffn_w_down", 1, 3)),
    "ffn_up": (("rg_w_in", 1, 1), ("rg_w_out", 1, 1), ("attn_w_in", 1, 1), ("attn_w_out", 1, 1)),
}


TRANSPOSED = ("ffn_w_gate", "ffn_w_up")


def _natural(name, a):
    return jnp.swapaxes(a, 1, 2) if name in TRANSPOSED else a


class _WeightGather:
    def __init__(self, w):
        self.w = w
        self.names = list(COL_SHARDED + ROW_SHARDED + GATES + ("ffn_w_down", "vec"))
        self.shards = {}
        for k in self.names[:-1]:
            a = _natural(k, w[k]).astype(bf16)
            self.shards[k] = a.reshape((-1,) + a.shape[-2:])
        self.shards["vec"] = jnp.concatenate([w[k].reshape(-1) for k in VECTORS]).reshape(1, -1, LANES)
        got = _gather_call(self._items("first", self.names), [self.shards[k] for k in self.names])
        self.raw = dict(zip(self.names, got))

    @staticmethod
    def _items(part, names):
        return [(names.index(k), l0, nl) for k, l0, nl in GATHER_PARTS[part]]

    def part(self, part):
        names = list(dict.fromkeys(k for k, _, _ in GATHER_PARTS[part]))
        items = self._items(part, names)
        return (functools.partial(_gather_start, items), functools.partial(_gather_finish, items),
                [self.shards[k] for k in names], [self.raw[k] for k in names], [], GATHER_SEMS * len(items)), names

    def views(self):
        got, w = self.raw, self.w
        out = {k: w[k] for k in REPLICATED}
        for k in COL_SHARDED + ("ffn_w_down",):
            out[k] = got[k]
        for k in ROW_SHARDED:
            l, s, ks, n = got[k].shape
            out[k] = got[k].reshape(l, 1, s * ks, n)
        for k in GATES:
            out[k] = got[k].reshape(2, LRU_BLOCKS, LRU_BW, LRU_BW)
        vec = got["vec"].reshape(N_CHIPS, -1)
        off = 0
        for k in VECTORS:
            shp = w[k].shape
            n = int(np.prod(shp))
            piece = vec[:, off:off + n].reshape((N_CHIPS,) + shp)
            off += n
            if k == "rg_conv_w":
                out[k] = piece.reshape(N_CHIPS, 2, 4, 256).transpose(1, 2, 0, 3).reshape(2, 4, D_MODEL)
            elif k in ("rg_b_a", "rg_b_i"):
                out[k] = piece.transpose(1, 2, 0, 3).reshape(2, 1, D_MODEL)
            else:
                out[k] = piece.transpose(1, 0, 2).reshape(2, 1, D_MODEL)
        return out


def _carried(plan, part, wts, fn, *args):
    if plan is None:
        return fn(*args, None)[0]
    gather, names = plan.part(part)
    out, new = fn(*args, gather)
    plan.raw.update(zip(names, new))
    wts.update(plan.views())
    return out


def _grad_blocks(name, g):
    st = jnp.stack([g[i] for i in sorted(g)])
    if name in GATES:
        st = st.reshape(2, LRU_BLOCKS, N_CHIPS, LRU_BW // N_CHIPS, LRU_BW).transpose(2, 0, 1, 3, 4)
    elif name == "rg_conv_w":
        st = st.reshape(2, 4, N_CHIPS, -1).transpose(2, 0, 1, 3)
    elif name in ("rg_b_a", "rg_b_i"):
        st = st.reshape(2, LRU_BLOCKS, N_CHIPS, -1).transpose(2, 0, 1, 3)
    elif name in VECTORS:
        st = st.reshape(2, N_CHIPS, -1).transpose(1, 0, 2)
    else:
        st = jnp.broadcast_to(st.reshape(1, -1), (N_CHIPS, st.size))
    return st.reshape(N_CHIPS, -1)


class _GradExchange:
    def __init__(self):
        self.c = lax.axis_index("c").astype(jnp.int32).reshape(1)
        self.pos = jnp.stack([2 * lax.axis_index("x") + lax.axis_index("y"), lax.axis_index("c")]).astype(jnp.int32)
        self.up = self.got_up = self.parts_up = self.slots_up = None

    @staticmethod
    def _blocked(g):
        if g.ndim == 3:
            g = g.reshape(g.shape[0], N_CHIPS, g.shape[1] // N_CHIPS, g.shape[2])
        return g

    def _sums(self, tag, names, gs, got):
        return [_pair_sum("grad_pair_sum_" + tag + k, g, r, self.c) for k, g, r in zip(names, gs, got)]

    def upper_carry(self, grads):
        self.up = [self._blocked(grads[k][1]) for k in BIG_GRADS]
        return _pair_carry(self.up)

    def upper_got(self, got):
        self.got_up = got

    def carry(self):
        self.parts_up = self._sums("up_", BIG_GRADS, self.up, self.got_up)
        return _chip_carry(self.parts_up)

    def carried(self, slots):
        self.slots_up = slots

    def finish(self, grads, shard_shapes):
        if self.got_up is None:
            self.upper_carry(grads)
            self.got_up = _pair_exchange(self.up)
        if self.slots_up is None:
            self.carry()
            self.slots_up = _chip_exchange(self.parts_up)
        blocks = [_grad_blocks(k, grads[k]) for k in SMALL_GRADS]
        used = sum(b.shape[1] for b in blocks)
        small = jnp.concatenate(blocks + [jnp.zeros((N_CHIPS, SMALL_ROWS * PACK_COLS - used), f32)], axis=1)
        names = tuple(k for k in BIG_GRADS if LOWER_LAYERS[k]) + ("small",)
        gs = [self._blocked(grads[k][0]) for k in names[:-1]] + [small.reshape(1, N_CHIPS, SMALL_ROWS, PACK_COLS)]
        parts = dict(zip(names, self._sums("lo_", names, gs, _pair_exchange(gs))))
        slots = dict(zip(names, _chip_exchange([parts[k] for k in names])))
        fulls = []
        for i, k in enumerate(BIG_GRADS):
            nlo, nup = LOWER_LAYERS[k], self.parts_up[i].shape[0]
            full = _chip_sum("grad_chip_sum_up_" + k, self.slots_up[i], self.parts_up[i], self.pos, nlo, nlo + nup, None)
            if nlo:
                full = _chip_sum("grad_chip_sum_lo_" + k, slots[k], parts[k], self.pos, 0, nlo + nup, full)
            fulls.append(full)
        fulls.append(_chip_sum("grad_chip_sum_lo_small", slots["small"], parts["small"], self.pos, 0, 1, None))
        full = _pair_gather(fulls)
        out = {k: f.reshape(shard_shapes[k]) for k, f in zip(BIG_GRADS, full)}
        flat, off = full[-1].reshape(-1), 0
        for k in SMALL_GRADS:
            n = int(np.prod(shard_shapes[k]))
            out[k] = flat[off:off + n].reshape(shard_shapes[k])
            off += n
        return out


def _adamw_fn(w, g, m, v):
    m = ADAM_B1 * m + (1.0 - ADAM_B1) * g
    v = ADAM_B2 * v + (1.0 - ADAM_B2) * (g * g)
    m_hat = m / (1.0 - ADAM_B1 ** ADAM_STEP)
    v_hat = v / (1.0 - ADAM_B2 ** ADAM_STEP)
    return -ADAM_LR * (m_hat / (jnp.sqrt(v_hat) + ADAM_EPS) + ADAM_WD * w), m, v


def _adamw(name, w, g, m, v):
    shp = w.shape
    if w.size >= 1 << 16:
        width = shp[-1]
        ops = [a.reshape(-1, width) for a in (w, g, m, v)]
        res = _rows(name, _adamw_fn, ops, [], [(width, f32)] * 3)
        return [r.reshape(shp) for r in res]
    n = w.size
    rows = -(-n // (SUBLANES * LANES)) * SUBLANES
    ops = [jnp.pad(a.reshape(-1), (0, rows * LANES - n)).reshape(rows, LANES) for a in (w, g, m, v)]
    res = _rows(name, _adamw_fn, ops, [], [(LANES, f32)] * 3, tr=rows)
    return [r.reshape(-1)[:n].reshape(shp) for r in res]


def kernel(x, attn_w_in, attn_rel_bias, attn_w_out, rg_w_in, rg_conv_w, rg_conv_b, rg_w_a, rg_b_a, rg_w_i, rg_b_i, rg_lambda, rg_w_out, norm_mix_pre, norm_mix_post, norm_ffn_pre, norm_ffn_post, ffn_w_gate, ffn_w_up, ffn_w_down, loss_target, m_attn_w_in, m_attn_rel_bias, m_attn_w_out, m_rg_w_in, m_rg_conv_w, m_rg_conv_b, m_rg_w_a, m_rg_b_a, m_rg_w_i, m_rg_b_i, m_rg_lambda, m_rg_w_out, m_norm_mix_pre, m_norm_mix_post, m_norm_ffn_pre, m_norm_ffn_post, m_ffn_w_gate, m_ffn_w_up, m_ffn_w_down, v_attn_w_in, v_attn_rel_bias, v_attn_w_out, v_rg_w_in, v_rg_conv_w, v_rg_conv_b, v_rg_w_a, v_rg_b_a, v_rg_w_i, v_rg_b_i, v_rg_lambda, v_rg_w_out, v_norm_mix_pre, v_norm_mix_post, v_norm_ffn_pre, v_norm_ffn_post, v_ffn_w_gate, v_ffn_w_up, v_ffn_w_down):
    w = dict(zip(WEIGHTS, (attn_w_in, attn_rel_bias, attn_w_out, rg_w_in, rg_conv_w, rg_conv_b, rg_w_a, rg_b_a, rg_w_i,
                           rg_b_i, rg_lambda, rg_w_out, norm_mix_pre, norm_mix_post, norm_ffn_pre, norm_ffn_post,
                           ffn_w_gate, ffn_w_up, ffn_w_down)))
    m = dict(zip(WEIGHTS, (m_attn_w_in, m_attn_rel_bias, m_attn_w_out, m_rg_w_in, m_rg_conv_w, m_rg_conv_b, m_rg_w_a,
                           m_rg_b_a, m_rg_w_i, m_rg_b_i, m_rg_lambda, m_rg_w_out, m_norm_mix_pre, m_norm_mix_post,
                           m_norm_ffn_pre, m_norm_ffn_post, m_ffn_w_gate, m_ffn_w_up, m_ffn_w_down)))
    v = dict(zip(WEIGHTS, (v_attn_w_in, v_attn_rel_bias, v_attn_w_out, v_rg_w_in, v_rg_conv_w, v_rg_conv_b, v_rg_w_a,
                           v_rg_b_a, v_rg_w_i, v_rg_b_i, v_rg_lambda, v_rg_w_out, v_norm_mix_pre, v_norm_mix_post,
                           v_norm_ffn_pre, v_norm_ffn_post, v_ffn_w_gate, v_ffn_w_up, v_ffn_w_down)))
    plan = _WeightGather(w)
    exch = _GradExchange()
    loss, dx, grads = _local_step(x[0], loss_target[0], plan.views(), plan, exch)
    loss = lax.psum(loss, ("x", "y", "c"))
    g = exch.finish(grads, {k: _natural(k, w[k]).shape for k in WEIGHTS})

    big = [k for k in WEIGHTS if k not in SMALL]
    upd = {}
    for k in big:
        res = _adamw("adamw_" + k, _natural(k, w[k]), g[k], _natural(k, m[k]), _natural(k, v[k]))
        upd[k] = [_natural(k, r) for r in res]
        g[k] = _natural(k, g[k])
    cat = lambda d: jnp.concatenate([d[k].reshape(-1) for k in SMALL])
    small = _adamw("adamw_small", cat(w), cat(g), cat(m), cat(v))
    off = 0
    for k in SMALL:
        n = w[k].size
        upd[k] = [r[off:off + n].reshape(w[k].shape) for r in small]
        off += n
    return (loss, dx[None], *[g[k] for k in WEIGHTS], *[upd[k][0] for k in WEIGHTS],
            *[upd[k][1] for k in WEIGHTS], *[upd[k][2] for k in WEIGHTS])
```
